```python
import jax, jax.numpy as jnp
from jax import lax
import numpy as np

D_MODEL = 2048
BATCH = 8
SEQ = 2048
DEPTH = 1

N_META = 16
D_RNN = D_MODEL
N_RNN_BLOCKS = 8
RNN_BLOCK = D_RNN // N_RNN_BLOCKS
CONV_WIDTH = 4
LRU_C = 8.0
LRU_MIN_RAD = 0.9
LRU_MAX_RAD = 0.999
HEAD_DIM = 64
N_Q_HEADS = D_MODEL // HEAD_DIM
N_KV_HEADS = N_Q_HEADS // 8
GROUP = N_Q_HEADS // N_KV_HEADS
D_ATTN = N_Q_HEADS * HEAD_DIM
D_KV = N_KV_HEADS * HEAD_DIM
WINDOW = 128
BLOCK = 128
ROPE_THETA = 10000.0
NEG_INF = -1e30
N_BRANCHES = 2
LN_EPS = 1e-5
DEEPNORM_ALPHA = (2.0 * DEPTH) ** 0.25
DEEPNORM_BETA = (8.0 * DEPTH) ** -0.25
OFF_GR = D_RNN
OFF_Q = 2 * D_RNN
OFF_K = OFF_Q + D_ATTN
OFF_V = OFF_K + D_KV
OFF_GA = OFF_V + D_KV
OFF_G = OFF_GA + D_ATTN
D_IN = OFF_G + N_BRANCHES * D_MODEL

kernel_name = "hybrid_rglru_swa_sink_gated_merge"


def layer_norm(x, g, b):
    xf = x.astype(jnp.float32)
    mu = xf.mean(-1, keepdims=True)
    var = jnp.square(xf - mu).mean(-1, keepdims=True)
    y = (xf - mu) * lax.rsqrt(var + LN_EPS)
    return (y * g.astype(jnp.float32) + b.astype(jnp.float32)).astype(x.dtype)


def rope(x, pos):
    half = HEAD_DIM // 2
    inv = ROPE_THETA ** (-jnp.arange(half, dtype=jnp.float32) / half)
    ang = pos.astype(jnp.float32)[:, None] * inv[None, :]
    cos = jnp.cos(ang)[None, :, None, :]
    sin = jnp.sin(ang)[None, :, None, :]
    xf = x.astype(jnp.float32)
    x1, x2 = xf[..., :half], xf[..., half:]
    return jnp.concatenate([x1 * cos - x2 * sin, x2 * cos + x1 * sin], axis=-1).astype(x.dtype)


def causal_depthwise_conv(x, w, b):
    T = x.shape[1]
    xp = jnp.pad(x, ((0, 0), (CONV_WIDTH - 1, 0), (0, 0)))
    y = b
    for k in range(CONV_WIDTH):
        s = CONV_WIDTH - 1 - k
        y = y + w[k] * xp[:, s:s + T]
    return y


def rg_lru(x, w_ra, b_ra, w_ri, b_ri, lam):
    B, T, _ = x.shape
    xb = x.reshape(B, T, N_RNN_BLOCKS, RNN_BLOCK)
    gate_r = jax.nn.sigmoid((jnp.einsum('btnc,ncd->btnd', xb, w_ra).reshape(B, T, D_RNN) + b_ra).astype(jnp.float32))
    gate_i = jax.nn.sigmoid((jnp.einsum('btnc,ncd->btnd', xb, w_ri).reshape(B, T, D_RNN) + b_ri).astype(jnp.float32))
    log_a = LRU_C * gate_r * jax.nn.log_sigmoid(lam.astype(jnp.float32))
    a = jnp.exp(log_a)
    mult = jnp.sqrt(-jnp.expm1(2.0 * log_a))
    mult = jnp.where((jnp.arange(T) == 0)[None, :, None], 1.0, mult)
    u = mult * gate_i * x.astype(jnp.float32)

    def combine(left, right):
        a1, b1 = left
        a2, b2 = right
        return a1 * a2, a2 * b1 + b2

    _, h = lax.associative_scan(combine, (a, u), axis=1)
    return h.astype(x.dtype)


def sliding_window_sink_attention(q, k, v, sinks):
    B, T = q.shape[:2]
    pad = BLOCK - N_META
    Lp = T + pad
    NB = Lp // BLOCK
    padf = lambda t: jnp.pad(t, ((0, 0), (pad, 0), (0, 0), (0, 0)))
    qb = padf(q).reshape(B, NB, BLOCK, N_KV_HEADS, GROUP, HEAD_DIM)
    kb = padf(k).reshape(B, NB, BLOCK, N_KV_HEADS, HEAD_DIM)
    vb = padf(v).reshape(B, NB, BLOCK, N_KV_HEADS, HEAD_DIM)

    def banded(t):
        prev = jnp.pad(t, ((0, 0), (1, 0), (0, 0), (0, 0), (0, 0)))[:, :NB]
        meta = jnp.broadcast_to(t[:, :1, pad:], (B, NB, N_META, N_KV_HEADS, HEAD_DIM))
        return jnp.concatenate([meta, prev, t], axis=2)

    kk, vv = banded(kb), banded(vb)

    qi = jnp.arange(NB)[:, None] * BLOCK + jnp.arange(BLOCK)[None, :]
    jb = (jnp.arange(NB)[:, None] - 1) * BLOCK + jnp.arange(2 * BLOCK)[None, :]
    jm = pad + jnp.arange(N_META)
    band_ok = ((jb[:, None, :] >= BLOCK) & (jb[:, None, :] <= qi[:, :, None])
               & (qi[:, :, None] - jb[:, None, :] < WINDOW))
    meta_ok = jnp.broadcast_to(jm[None, None, :] <= qi[:, :, None], (NB, BLOCK, N_META))
    mask = jnp.concatenate([meta_ok, band_ok], axis=-1)

    s = jnp.einsum('bnqkgd,bnskd->bnkgqs', qb, kk).astype(jnp.float32) * (HEAD_DIM ** -0.5)
    s = jnp.where(mask[None, :, None, None], s, NEG_INF)
    sink = sinks.astype(jnp.float32).reshape(N_KV_HEADS, GROUP)[None, None, :, :, None, None]
    m = jnp.maximum(s.max(-1, keepdims=True), sink)
    p = jnp.exp(s - m)
    denom = p.sum(-1, keepdims=True) + jnp.exp(sink - m)
    o = jnp.einsum('bnkgqs,bnskd->bnqkgd', (p / denom).astype(v.dtype), vv)
    return o.reshape(B, Lp, D_ATTN)[:, pad:]


def hybrid_layer(h, pos, w_in, b_in, conv_w, conv_b, w_ra, b_ra, w_ri, b_ri, lam, sinks,
                 w_rnn_out, w_attn_out, w_o, b_o, ln_g, ln_b):
    B, T, _ = h.shape
    z = h @ w_in + b_in
    xr, gr, q, k, v, ga, mg = jnp.split(z, [OFF_GR, OFF_Q, OFF_K, OFF_V, OFF_GA, OFF_G], axis=-1)

    hr = rg_lru(causal_depthwise_conv(xr, conv_w, conv_b), w_ra, b_ra, w_ri, b_ri, lam)
    y_a = (hr * jax.nn.silu(gr)) @ w_rnn_out

    q = rope(q.reshape(B, T, N_Q_HEADS, HEAD_DIM), pos)
    k = rope(k.reshape(B, T, N_KV_HEADS, HEAD_DIM), pos)
    v = v.reshape(B, T, N_KV_HEADS, HEAD_DIM)
    o = sliding_window_sink_attention(q, k, v, sinks)
    y_b = (o * jax.nn.silu(ga)) @ w_attn_out

    g = jax.nn.sigmoid(mg.astype(jnp.float32)).astype(h.dtype)
    mixed = g[..., :D_MODEL] * y_a + g[..., D_MODEL:] * y_b
    out = mixed @ w_o + b_o
    return layer_norm(DEEPNORM_ALPHA * h + out, ln_g, ln_b)


def _fwd_setup_inputs(seed: int = 0) -> dict:
    key = jax.random.key(seed)
    ks = jax.random.split(key, 24)
    f32 = jnp.float32
    nrm = lambda k, shape, scale: jax.random.normal(k, shape, f32) * scale
    u = jax.random.uniform(ks[12], (DEPTH, D_RNN), f32, LRU_MIN_RAD, LRU_MAX_RAD)
    s_rad = u ** (1.0 / LRU_C)
    lru_lambda = jnp.log(s_rad) - jnp.log1p(-s_rad)
    return {
        "x": jax.random.normal(ks[0], (BATCH, SEQ, D_MODEL), f32),
        "meta_tokens": nrm(ks[1], (N_META, D_MODEL), 1.0),
        "ln_emb_g": 1.0 + nrm(ks[2], (D_MODEL,), 0.01),
        "ln_emb_b": nrm(ks[3], (D_MODEL,), 0.01),
        "w_in": nrm(ks[4], (DEPTH, D_MODEL, D_IN), D_MODEL ** -0.5),
        "b_in": nrm(ks[5], (DEPTH, D_IN), 0.01),
        "conv_w": nrm(ks[6], (DEPTH, CONV_WIDTH, D_RNN), CONV_WIDTH ** -0.5),
        "conv_b": nrm(ks[7], (DEPTH, D_RNN), 0.01),
        "w_ra": nrm(ks[8], (DEPTH, N_RNN_BLOCKS, RNN_BLOCK, RNN_BLOCK), RNN_BLOCK ** -0.5),
        "b_ra": nrm(ks[9], (DEPTH, D_RNN), 0.01),
        "w_ri": nrm(ks[10], (DEPTH, N_RNN_BLOCKS, RNN_BLOCK, RNN_BLOCK), RNN_BLOCK ** -0.5),
        "b_ri": nrm(ks[11], (DEPTH, D_RNN), 0.01),
        "lru_lambda": lru_lambda,
        "sinks": nrm(ks[13], (DEPTH, N_Q_HEADS), 0.5),
        "w_rnn_out": nrm(ks[14], (DEPTH, D_RNN, D_MODEL), D_RNN ** -0.5 * DEEPNORM_BETA),
        "w_attn_out": nrm(ks[15], (DEPTH, D_ATTN, D_MODEL), D_ATTN ** -0.5 * DEEPNORM_BETA),
        "w_o": nrm(ks[16], (DEPTH, D_MODEL, D_MODEL), D_MODEL ** -0.5 * DEEPNORM_BETA),
        "b_o": nrm(ks[17], (DEPTH, D_MODEL), 0.01),
        "ln_g": 1.0 + nrm(ks[18], (DEPTH, D_MODEL), 0.01),
        "ln_b": nrm(ks[19], (DEPTH, D_MODEL), 0.01),
    }


def _fwd_reference(x, meta_tokens, ln_emb_g, ln_emb_b, w_in, b_in, conv_w, conv_b, w_ra, b_ra, w_ri, b_ri,
              lru_lambda, sinks, w_rnn_out, w_attn_out, w_o, b_o, ln_g, ln_b):
    B = x.shape[0]
    meta = jnp.broadcast_to(meta_tokens.astype(x.dtype)[None], (B, N_META, D_MODEL))
    h = jnp.concatenate([meta, x], axis=1)
    h = layer_norm(h, ln_emb_g, ln_emb_b)
    pos = jnp.arange(h.shape[1])
    for l in range(DEPTH):
        h = hybrid_layer(h, pos, w_in[l], b_in[l], conv_w[l], conv_b[l], w_ra[l], b_ra[l], w_ri[l], b_ri[l],
                         lru_lambda[l], sinks[l], w_rnn_out[l], w_attn_out[l], w_o[l], b_o[l], ln_g[l], ln_b[l])
    return h[:, N_META:]


import jax as _jax
import jax.numpy as _jnp

TWIN_FORMAT = 'train_step'
FWD_PARAMS = ['x', 'meta_tokens', 'ln_emb_g', 'ln_emb_b', 'w_in', 'b_in', 'conv_w', 'conv_b', 'w_ra', 'b_ra', 'w_ri', 'b_ri', 'lru_lambda', 'sinks', 'w_rnn_out', 'w_attn_out', 'w_o', 'b_o', 'ln_g', 'ln_b']
TWIN_WEIGHTS = ['meta_tokens', 'ln_emb_g', 'ln_emb_b', 'w_in', 'b_in', 'conv_w', 'conv_b', 'w_ra', 'b_ra', 'w_ri', 'b_ri', 'lru_lambda', 'sinks', 'w_rnn_out', 'w_attn_out', 'w_o', 'b_o', 'ln_g', 'ln_b']
TWIN_DIFF_INPUT = 'x'
TWIN_INPUTS = ['x', 'meta_tokens', 'ln_emb_g', 'ln_emb_b', 'w_in', 'b_in', 'conv_w', 'conv_b', 'w_ra', 'b_ra', 'w_ri', 'b_ri', 'lru_lambda', 'sinks', 'w_rnn_out', 'w_attn_out', 'w_o', 'b_o', 'ln_g', 'ln_b', 'loss_target', 'm_meta_tokens', 'm_ln_emb_g', 'm_ln_emb_b', 'm_w_in', 'm_b_in', 'm_conv_w', 'm_conv_b', 'm_w_ra', 'm_b_ra', 'm_w_ri', 'm_b_ri', 'm_lru_lambda', 'm_sinks', 'm_w_rnn_out', 'm_w_attn_out', 'm_w_o', 'm_b_o', 'm_ln_g', 'm_ln_b', 'v_meta_tokens', 'v_ln_emb_g', 'v_ln_emb_b', 'v_w_in', 'v_b_in', 'v_conv_w', 'v_conv_b', 'v_w_ra', 'v_b_ra', 'v_w_ri', 'v_b_ri', 'v_lru_lambda', 'v_sinks', 'v_w_rnn_out', 'v_w_attn_out', 'v_w_o', 'v_b_o', 'v_ln_g', 'v_ln_b']
TWIN_OUTPUTS = ['loss', 'grad_x', 'grad_meta_tokens', 'grad_ln_emb_g', 'grad_ln_emb_b', 'grad_w_in', 'grad_b_in', 'grad_conv_w', 'grad_conv_b', 'grad_w_ra', 'grad_b_ra', 'grad_w_ri', 'grad_b_ri', 'grad_lru_lambda', 'grad_sinks', 'grad_w_rnn_out', 'grad_w_attn_out', 'grad_w_o', 'grad_b_o', 'grad_ln_g', 'grad_ln_b', 'delta_meta_tokens', 'delta_ln_emb_g', 'delta_ln_emb_b', 'delta_w_in', 'delta_b_in', 'delta_conv_w', 'delta_conv_b', 'delta_w_ra', 'delta_b_ra', 'delta_w_ri', 'delta_b_ri', 'delta_lru_lambda', 'delta_sinks', 'delta_w_rnn_out', 'delta_w_attn_out', 'delta_w_o', 'delta_b_o', 'delta_ln_g', 'delta_ln_b', 'new_m_meta_tokens', 'new_m_ln_emb_g', 'new_m_ln_emb_b', 'new_m_w_in', 'new_m_b_in', 'new_m_conv_w', 'new_m_conv_b', 'new_m_w_ra', 'new_m_b_ra', 'new_m_w_ri', 'new_m_b_ri', 'new_m_lru_lambda', 'new_m_sinks', 'new_m_w_rnn_out', 'new_m_w_attn_out', 'new_m_w_o', 'new_m_b_o', 'new_m_ln_g', 'new_m_ln_b', 'new_v_meta_tokens', 'new_v_ln_emb_g', 'new_v_ln_emb_b', 'new_v_w_in', 'new_v_b_in', 'new_v_conv_w', 'new_v_conv_b', 'new_v_w_ra', 'new_v_b_ra', 'new_v_w_ri', 'new_v_b_ri', 'new_v_lru_lambda', 'new_v_sinks', 'new_v_w_rnn_out', 'new_v_w_attn_out', 'new_v_w_o', 'new_v_b_o', 'new_v_ln_g', 'new_v_ln_b']
TWIN_LEAF_KINDS = {'loss': 'loss', 'grad_x': 'grad_x', 'grad_meta_tokens': 'grad_w', 'grad_ln_emb_g': 'grad_w', 'grad_ln_emb_b': 'grad_w', 'grad_w_in': 'grad_w', 'grad_b_in': 'grad_w', 'grad_conv_w': 'grad_w', 'grad_conv_b': 'grad_w', 'grad_w_ra': 'grad_w', 'grad_b_ra': 'grad_w', 'grad_w_ri': 'grad_w', 'grad_b_ri': 'grad_w', 'grad_lru_lambda': 'grad_w', 'grad_sinks': 'grad_w', 'grad_w_rnn_out': 'grad_w', 'grad_w_attn_out': 'grad_w', 'grad_w_o': 'grad_w', 'grad_b_o': 'grad_w', 'grad_ln_g': 'grad_w', 'grad_ln_b': 'grad_w', 'delta_meta_tokens': 'delta_w', 'delta_ln_emb_g': 'delta_w', 'delta_ln_emb_b': 'delta_w', 'delta_w_in': 'delta_w', 'delta_b_in': 'delta_w', 'delta_conv_w': 'delta_w', 'delta_conv_b': 'delta_w', 'delta_w_ra': 'delta_w', 'delta_b_ra': 'delta_w', 'delta_w_ri': 'delta_w', 'delta_b_ri': 'delta_w', 'delta_lru_lambda': 'delta_w', 'delta_sinks': 'delta_w', 'delta_w_rnn_out': 'delta_w', 'delta_w_attn_out': 'delta_w', 'delta_w_o': 'delta_w', 'delta_b_o': 'delta_w', 'delta_ln_g': 'delta_w', 'delta_ln_b': 'delta_w', 'new_m_meta_tokens': 'new_m', 'new_m_ln_emb_g': 'new_m', 'new_m_ln_emb_b': 'new_m', 'new_m_w_in': 'new_m', 'new_m_b_in': 'new_m', 'new_m_conv_w': 'new_m', 'new_m_conv_b': 'new_m', 'new_m_w_ra': 'new_m', 'new_m_b_ra': 'new_m', 'new_m_w_ri': 'new_m', 'new_m_b_ri': 'new_m', 'new_m_lru_lambda': 'new_m', 'new_m_sinks': 'new_m', 'new_m_w_rnn_out': 'new_m', 'new_m_w_attn_out': 'new_m', 'new_m_w_o': 'new_m', 'new_m_b_o': 'new_m', 'new_m_ln_g': 'new_m', 'new_m_ln_b': 'new_m', 'new_v_meta_tokens': 'new_v', 'new_v_ln_emb_g': 'new_v', 'new_v_ln_emb_b': 'new_v', 'new_v_w_in': 'new_v', 'new_v_b_in': 'new_v', 'new_v_conv_w': 'new_v', 'new_v_conv_b': 'new_v', 'new_v_w_ra': 'new_v', 'new_v_b_ra': 'new_v', 'new_v_w_ri': 'new_v', 'new_v_b_ri': 'new_v', 'new_v_lru_lambda': 'new_v', 'new_v_sinks': 'new_v', 'new_v_w_rnn_out': 'new_v', 'new_v_w_attn_out': 'new_v', 'new_v_w_o': 'new_v', 'new_v_b_o': 'new_v', 'new_v_ln_g': 'new_v', 'new_v_ln_b': 'new_v'}


def _forward(args):
    return _fwd_reference(*[args[k] for k in FWD_PARAMS])


def _output_shape():
    out = _jax.eval_shape(lambda: _forward(_fwd_setup_inputs(0)))
    return out.shape, out.dtype

N_MICROBATCH = 1
ADAM_LR = 0.001
ADAM_B1 = 0.9
ADAM_B2 = 0.999
ADAM_EPS = 1e-08
ADAM_WD = 0.01
ADAM_STEP = 10
PER_EXAMPLE_BATCH_AXIS = {'x': 0, 'loss_target': 0}
SHARED_INPUTS = []
_WEIGHT_DTYPES = {'meta_tokens': _jnp.float32, 'ln_emb_g': _jnp.float32, 'ln_emb_b': _jnp.float32, 'w_in': _jnp.float32, 'b_in': _jnp.float32, 'conv_w': _jnp.float32, 'conv_b': _jnp.float32, 'w_ra': _jnp.float32, 'b_ra': _jnp.float32, 'w_ri': _jnp.float32, 'b_ri': _jnp.float32, 'lru_lambda': _jnp.float32, 'sinks': _jnp.float32, 'w_rnn_out': _jnp.float32, 'w_attn_out': _jnp.float32, 'w_o': _jnp.float32, 'b_o': _jnp.float32, 'ln_g': _jnp.float32, 'ln_b': _jnp.float32}
MOMENT_SCALE = {'meta_tokens': 3.008150e-04, 'ln_emb_g': 1.434481e-01, 'ln_emb_b': 9.386288e-02, 'w_in': 2.247195e-03, 'b_in': 1.853074e-02, 'conv_w': 3.859671e-03, 'conv_b': 4.360165e-02, 'w_ra': 9.987538e-04, 'b_ra': 8.457696e-04, 'w_ri': 1.764056e-03, 'b_ri': 1.340093e-03, 'lru_lambda': 1.703726e-03, 'sinks': 1.573049e-04, 'w_rnn_out': 6.145867e-03, 'w_attn_out': 1.740981e-03, 'w_o': 6.254181e-03, 'b_o': 7.199596e-02, 'ln_g': 7.999321e+00, 'ln_b': 1.417043e-01}


def _to_microbatches(a, axis):
    t = _jnp.moveaxis(a, axis, 0)
    t = t.reshape((N_MICROBATCH, t.shape[0] // N_MICROBATCH) + t.shape[1:])
    return _jnp.moveaxis(t, 1, axis + 1)


def setup_inputs(seed: int = 0) -> dict:
    inp = _fwd_setup_inputs(seed)
    key = _jax.random.fold_in(_jax.random.key(seed), 7919)
    shape, _ = _output_shape()
    out = dict(inp)
    out["loss_target"] = _jax.random.normal(_jax.random.fold_in(key, 0), shape, _jnp.float32)
    for i, name in enumerate(TWIN_WEIGHTS):
        w = inp[name].astype(_jnp.float32)
        if MOMENT_SCALE is None:
            s = _jnp.sqrt(_jnp.mean(_jnp.square(w)) + 1e-30)
        else:
            s = MOMENT_SCALE[name]
        km, kv = _jax.random.split(_jax.random.fold_in(key, i + 1))
        out[name] = w
        out["m_" + name] = s * _jax.random.normal(km, w.shape, _jnp.float32)
        out["v_" + name] = (s * s) * _jax.random.uniform(kv, w.shape, _jnp.float32, 0.5, 1.5)
    if N_MICROBATCH > 1:
        for name, axis in PER_EXAMPLE_BATCH_AXIS.items():
            out[name] = _to_microbatches(out[name], axis)
    return {'x': out['x'], 'meta_tokens': out['meta_tokens'], 'ln_emb_g': out['ln_emb_g'], 'ln_emb_b': out['ln_emb_b'], 'w_in': out['w_in'], 'b_in': out['b_in'], 'conv_w': out['conv_w'], 'conv_b': out['conv_b'], 'w_ra': out['w_ra'], 'b_ra': out['b_ra'], 'w_ri': out['w_ri'], 'b_ri': out['b_ri'], 'lru_lambda': out['lru_lambda'], 'sinks': out['sinks'], 'w_rnn_out': out['w_rnn_out'], 'w_attn_out': out['w_attn_out'], 'w_o': out['w_o'], 'b_o': out['b_o'], 'ln_g': out['ln_g'], 'ln_b': out['ln_b'], 'loss_target': out['loss_target'], 'm_meta_tokens': out['m_meta_tokens'], 'm_ln_emb_g': out['m_ln_emb_g'], 'm_ln_emb_b': out['m_ln_emb_b'], 'm_w_in': out['m_w_in'], 'm_b_in': out['m_b_in'], 'm_conv_w': out['m_conv_w'], 'm_conv_b': out['m_conv_b'], 'm_w_ra': out['m_w_ra'], 'm_b_ra': out['m_b_ra'], 'm_w_ri': out['m_w_ri'], 'm_b_ri': out['m_b_ri'], 'm_lru_lambda': out['m_lru_lambda'], 'm_sinks': out['m_sinks'], 'm_w_rnn_out': out['m_w_rnn_out'], 'm_w_attn_out': out['m_w_attn_out'], 'm_w_o': out['m_w_o'], 'm_b_o': out['m_b_o'], 'm_ln_g': out['m_ln_g'], 'm_ln_b': out['m_ln_b'], 'v_meta_tokens': out['v_meta_tokens'], 'v_ln_emb_g': out['v_ln_emb_g'], 'v_ln_emb_b': out['v_ln_emb_b'], 'v_w_in': out['v_w_in'], 'v_b_in': out['v_b_in'], 'v_conv_w': out['v_conv_w'], 'v_conv_b': out['v_conv_b'], 'v_w_ra': out['v_w_ra'], 'v_b_ra': out['v_b_ra'], 'v_w_ri': out['v_w_ri'], 'v_b_ri': out['v_b_ri'], 'v_lru_lambda': out['v_lru_lambda'], 'v_sinks': out['v_sinks'], 'v_w_rnn_out': out['v_w_rnn_out'], 'v_w_attn_out': out['v_w_attn_out'], 'v_w_o': out['v_w_o'], 'v_b_o': out['v_b_o'], 'v_ln_g': out['v_ln_g'], 'v_ln_b': out['v_ln_b']}


def _loss(weights, diff, rest, loss_target):
    with _jax.named_scope("forward"):
        args = {**rest, TWIN_DIFF_INPUT: diff, **{k: w.astype(_WEIGHT_DTYPES[k]) for k, w in weights.items()}}
        y = _forward(args)
    with _jax.named_scope("loss_head"):
        err = _jnp.square(y.astype(_jnp.float32) - loss_target)
        return 0.5 * _jnp.sum(_jnp.mean(err, axis=-1)) if err.ndim else 0.5 * err


def _adamw(w, g, m, v):
    m = ADAM_B1 * m + (1.0 - ADAM_B1) * g
    v = ADAM_B2 * v + (1.0 - ADAM_B2) * _jnp.square(g)
    m_hat = m / (1.0 - ADAM_B1 ** ADAM_STEP)
    v_hat = v / (1.0 - ADAM_B2 ** ADAM_STEP)
    delta = -ADAM_LR * (m_hat / (_jnp.sqrt(v_hat) + ADAM_EPS) + ADAM_WD * w)
    return delta, m, v


def reference(x, meta_tokens, ln_emb_g, ln_emb_b, w_in, b_in, conv_w, conv_b, w_ra, b_ra, w_ri, b_ri, lru_lambda, sinks, w_rnn_out, w_attn_out, w_o, b_o, ln_g, ln_b, loss_target, m_meta_tokens, m_ln_emb_g, m_ln_emb_b, m_w_in, m_b_in, m_conv_w, m_conv_b, m_w_ra, m_b_ra, m_w_ri, m_b_ri, m_lru_lambda, m_sinks, m_w_rnn_out, m_w_attn_out, m_w_o, m_b_o, m_ln_g, m_ln_b, v_meta_tokens, v_ln_emb_g, v_ln_emb_b, v_w_in, v_b_in, v_conv_w, v_conv_b, v_w_ra, v_b_ra, v_w_ri, v_b_ri, v_lru_lambda, v_sinks, v_w_rnn_out, v_w_attn_out, v_w_o, v_b_o, v_ln_g, v_ln_b):
    given = dict(x=x, meta_tokens=meta_tokens, ln_emb_g=ln_emb_g, ln_emb_b=ln_emb_b, w_in=w_in, b_in=b_in, conv_w=conv_w, conv_b=conv_b, w_ra=w_ra, b_ra=b_ra, w_ri=w_ri, b_ri=b_ri, lru_lambda=lru_lambda, sinks=sinks, w_rnn_out=w_rnn_out, w_attn_out=w_attn_out, w_o=w_o, b_o=b_o, ln_g=ln_g, ln_b=ln_b, loss_target=loss_target, m_meta_tokens=m_meta_tokens, m_ln_emb_g=m_ln_emb_g, m_ln_emb_b=m_ln_emb_b, m_w_in=m_w_in, m_b_in=m_b_in, m_conv_w=m_conv_w, m_conv_b=m_conv_b, m_w_ra=m_w_ra, m_b_ra=m_b_ra, m_w_ri=m_w_ri, m_b_ri=m_b_ri, m_lru_lambda=m_lru_lambda, m_sinks=m_sinks, m_w_rnn_out=m_w_rnn_out, m_w_attn_out=m_w_attn_out, m_w_o=m_w_o, m_b_o=m_b_o, m_ln_g=m_ln_g, m_ln_b=m_ln_b, v_meta_tokens=v_meta_tokens, v_ln_emb_g=v_ln_emb_g, v_ln_emb_b=v_ln_emb_b, v_w_in=v_w_in, v_b_in=v_b_in, v_conv_w=v_conv_w, v_conv_b=v_conv_b, v_w_ra=v_w_ra, v_b_ra=v_b_ra, v_w_ri=v_w_ri, v_b_ri=v_b_ri, v_lru_lambda=v_lru_lambda, v_sinks=v_sinks, v_w_rnn_out=v_w_rnn_out, v_w_attn_out=v_w_attn_out, v_w_o=v_w_o, v_b_o=v_b_o, v_ln_g=v_ln_g, v_ln_b=v_ln_b)
    weights = {n: given[n] for n in TWIN_WEIGHTS}
    shared = {n: given[n] for n in SHARED_INPUTS}
    per_example = {n: given[n] for n in ['x']}
    grad_fn = _jax.value_and_grad(_loss, argnums=(0, 1))

    def one_microbatch(ex, loss_target):
        ex = dict(ex)
        diff = ex.pop(TWIN_DIFF_INPUT)
        return grad_fn(weights, diff, {**shared, **ex}, loss_target)

    if N_MICROBATCH == 1:
        loss, (grad_w, grad_x) = one_microbatch(per_example, given["loss_target"])
    else:
        def body(carry, xs):
            loss_sum, grad_sum = carry
            l_k, (gw_k, gx_k) = one_microbatch(xs[0], xs[1])
            with _jax.named_scope("update"):
                return (loss_sum + l_k, _jax.tree.map(_jnp.add, grad_sum, gw_k)), gx_k

        init = (_jnp.zeros((), _jnp.float32), _jax.tree.map(_jnp.zeros_like, weights))
        (loss, grad_w), grad_x = _jax.lax.scan(body, init, (per_example, given["loss_target"]))
    with _jax.named_scope("update"):
        delta_w, new_m, new_v = {}, {}, {}
        for n in TWIN_WEIGHTS:
            delta_w[n], new_m[n], new_v[n] = _adamw(weights[n], grad_w[n], given["m_" + n], given["v_" + n])
    return (loss, grad_x, *[grad_w[n] for n in TWIN_WEIGHTS], *[delta_w[n] for n in TWIN_WEIGHTS],
            *[new_m[n] for n in TWIN_WEIGHTS], *[new_v[n] for n in TWIN_WEIGHTS])
```

```python
import functools

import jax
import jax.numpy as jnp
from jax import lax
from jax.experimental import pallas as pl
from jax.experimental.pallas import tpu as pltpu

f32, bf16 = jnp.float32, jnp.bfloat16
SDS = jax.ShapeDtypeStruct

N_DEV = 8
D = 2048
N_META = 16
BLK = 128
ROW0 = BLK - N_META
N_RNN_BLOCKS = 8
RNN_BLOCK = D // N_RNN_BLOCKS
CONV_WIDTH = 4
LRU_C = 8.0
HEAD_DIM = 64
N_KV = 4
GROUP = 8
HALF = HEAD_DIM // 2
ROPE_THETA = 10000.0
NEG_INF = -1e30
LN_EPS = 1e-5
ALPHA = 2.0 ** 0.25
D_IN = 12800
SHARD_IN = D_IN // N_DEV
OFF_GR, OFF_Q, OFF_K, OFF_V, OFF_GA, OFF_G = 2048, 4096, 6144, 6400, 6656, 8704
ADAM_LR, ADAM_B1, ADAM_B2, ADAM_EPS, ADAM_WD, ADAM_STEP = 1e-3, 0.9, 0.999, 1e-8, 0.01, 10
VMEM_LIMIT_MB = 56
MESH = pl.DeviceIdType.MESH


def _cp(sem=None, vmem_mb=40):
    return pltpu.CompilerParams(dimension_semantics=sem, vmem_limit_bytes=vmem_mb * 2 ** 20)


def _row_chunk(m):
    best = 16
    for c in range(16, 641, 16):
        if m % c == 0:
            best = c
    return best


def _sigmoid(x):
    return 1.0 / (1.0 + jnp.exp(-x))


def _silu_and_grad(x):
    s = _sigmoid(x)
    return x * s, s * (1.0 + x * (1.0 - s))


def _log_sigmoid(x):
    return jnp.minimum(x, 0.0) - jnp.log1p(jnp.exp(-jnp.abs(x)))


def _ln_rows(v, g, b):
    mu = jnp.mean(v, axis=-1, keepdims=True)
    c = v - mu
    var = jnp.mean(c * c, axis=-1, keepdims=True)
    rstd = lax.rsqrt(var + LN_EPS)
    xhat = c * rstd
    return xhat * g + b, xhat, rstd


def _ln_rows_bwd(dy, g, xhat, rstd):
    dxh = dy * g
    m1 = jnp.mean(dxh, axis=-1, keepdims=True)
    m2 = jnp.mean(dxh * xhat, axis=-1, keepdims=True)
    return rstd * (dxh - m1 - xhat * m2)


def _colsum(v):
    return jnp.sum(v, axis=0, keepdims=True)


def _dot(a, b):
    return jnp.dot(a, b, preferred_element_type=f32)


def _dot_nt(a, b):
    return lax.dot_general(a, b, (((1,), (1,)), ((), ())), preferred_element_type=f32)


def _dot_tn(a, b):
    return lax.dot_general(a, b, (((0,), (0,)), ((), ())), preferred_element_type=f32)


def _meta_full(sw_ref):
    return jnp.concatenate([sw_ref[s, 0:N_META, :] for s in range(N_DEV)], axis=1)


def _ln_emb(x, smallw, g_e, b_e):
    seq = x.shape[1]
    rows = seq + BLK
    nb = rows // BLK

    def body(x_ref, sw_ref, g_ref, b_ref, h32_ref, hb_ref, ht_ref):
        i = pl.program_id(0)
        g, b = g_ref[...], b_ref[...]

        def emit(blk):
            h32_ref[...] = blk
            hb_ref[...] = blk.astype(bf16)
            ht_ref[...] = blk.T.astype(bf16)

        @pl.when(i == 0)
        def _():
            hm = _ln_rows(_meta_full(sw_ref), g, b)[0]
            emit(jnp.concatenate([jnp.zeros((ROW0, D), f32), hm], axis=0))

        @pl.when(i > 0)
        def _():
            emit(_ln_rows(x_ref[0], g, b)[0])

    return pl.pallas_call(
        body, grid=(nb,),
        in_specs=[pl.BlockSpec((1, BLK, D), lambda i: (0, jnp.maximum(i - 1, 0), 0)),
                  pl.BlockSpec((N_DEV, 24, 256), lambda i: (0, 0, 0)),
                  pl.BlockSpec((1, D), lambda i: (0, 0)),
                  pl.BlockSpec((1, D), lambda i: (0, 0))],
        out_specs=[pl.BlockSpec((BLK, D), lambda i: (i, 0)),
                   pl.BlockSpec((BLK, D), lambda i: (i, 0)),
                   pl.BlockSpec((D, BLK), lambda i: (0, i))],
        out_shape=[SDS((rows, D), f32), SDS((rows, D), bf16), SDS((D, rows), bf16)],
        name="ln_emb", compiler_params=_cp(("arbitrary",)),
    )(x, smallw, g_e, b_e)


def _ln_emb_bwd(dh, du32, x, smallw, g_e):
    seq = x.shape[1]
    rows = seq + BLK
    nb = rows // BLK

    def body(dh_ref, du_ref, x_ref, sw_ref, g_ref, gx_ref, dmeta_ref, st_ref):
        i = pl.program_id(0)
        g = g_ref[...]
        dht = dh_ref[...] + ALPHA * du_ref[...]

        @pl.when(i == 0)
        def _():
            v = jnp.concatenate([jnp.zeros((ROW0, D), f32), _meta_full(sw_ref)], axis=0)
            valid = lax.broadcasted_iota(jnp.int32, (BLK, 1), 0) >= ROW0
            d = jnp.where(valid, dht, 0.0)
            _, xhat, rstd = _ln_rows(v, g, 0.0)
            dv = _ln_rows_bwd(d, g, xhat, rstd)
            dmeta_ref[...] = dv[ROW0:, :]
            st_ref[...] = jnp.concatenate([_colsum(d * xhat), _colsum(d), jnp.zeros((6, D), f32)], axis=0)

        @pl.when(i > 0)
        def _():
            _, xhat, rstd = _ln_rows(x_ref[0], g, 0.0)
            gx_ref[0] = _ln_rows_bwd(dht, g, xhat, rstd)
            st_ref[0:1, :] += _colsum(dht * xhat)
            st_ref[1:2, :] += _colsum(dht)

    return pl.pallas_call(
        body, grid=(nb,),
        in_specs=[pl.BlockSpec((BLK, D), lambda i: (i, 0)),
                  pl.BlockSpec((BLK, D), lambda i: (i, 0)),
                  pl.BlockSpec((1, BLK, D), lambda i: (0, jnp.maximum(i - 1, 0), 0)),
                  pl.BlockSpec((N_DEV, 24, 256), lambda i: (0, 0, 0)),
                  pl.BlockSpec((1, D), lambda i: (0, 0))],
        out_specs=[pl.BlockSpec((1, BLK, D), lambda i: (0, jnp.maximum(i - 1, 0), 0)),
                   pl.BlockSpec((N_META, D), lambda i: (0, 0)),
                   pl.BlockSpec((8, D), lambda i: (0, 0))],
        out_shape=[SDS((1, seq, D), f32), SDS((N_META, D), f32), SDS((8, D), f32)],
        name="ln_emb_bwd", compiler_params=_cp(("arbitrary",)),
    )(dh, du32, x, smallw, g_e)


def _mm(a, b, *, name, nt=False, sel=None, bias=None, out_dtype=f32, tn=512):
    m, k = a.shape
    cm = _row_chunk(m)
    stacked = sel is not None
    if stacked:
        n = D
        if nt:
            b_spec = pl.BlockSpec((tn // 256, None, 256, D), lambda j: (j, sel, 0, 0))
        else:
            b_spec = pl.BlockSpec((N_DEV, None, 256, tn), lambda j: (0, sel, 0, j))
    elif nt:
        n = b.shape[0]
        b_spec = pl.BlockSpec((tn, k), lambda j: (j, 0))
    else:
        n = b.shape[1]
        b_spec = pl.BlockSpec((k, tn), lambda j: (0, j))
    in_specs = [pl.BlockSpec((m, k), lambda j: (0, 0)), b_spec]
    args = [a, b]
    if bias is not None:
        in_specs.append(pl.BlockSpec((1, tn), lambda j: (0, j)))
        args.append(bias)

    def body(*refs):
        a_ref, b_ref, o_ref = refs[0], refs[1], refs[-1]
        bm = b_ref[...]
        if stacked:
            bm = bm.reshape((tn, D) if nt else (D, tn))
        for c in range(m // cm):
            acc = (_dot_nt if nt else _dot)(a_ref[c * cm:(c + 1) * cm, :], bm)
            if bias is not None:
                acc = acc + refs[2][...]
            o_ref[c * cm:(c + 1) * cm, :] = acc.astype(out_dtype)

    return pl.pallas_call(
        body, grid=(n // tn,), in_specs=in_specs,
        out_specs=pl.BlockSpec((m, tn), lambda j: (0, j)),
        out_shape=SDS((m, n), out_dtype), name=name, compiler_params=_cp(("arbitrary",), 48),
    )(*args)


def _mm_dh(dz, w_full):
    rows = dz.shape[0]
    tk, tn = 512, 1024
    cm = _row_chunk(rows)

    def body(a_ref, w_ref, o_ref):
        kk = pl.program_id(1)
        for c in range(rows // cm):
            acc = _dot_nt(a_ref[c * cm:(c + 1) * cm, :], w_ref[...])

            @pl.when(kk == 0)
            def _():
                o_ref[c * cm:(c + 1) * cm, :] = acc

            @pl.when(kk > 0)
            def _():
                o_ref[c * cm:(c + 1) * cm, :] += acc

    return pl.pallas_call(
        body, grid=(D // tn, D_IN // tk),
        in_specs=[pl.BlockSpec((rows, tk), lambda j, kk: (0, kk)),
                  pl.BlockSpec((tn, tk), lambda j, kk: (j, kk))],
        out_specs=pl.BlockSpec((rows, tn), lambda j, kk: (0, j)),
        out_shape=SDS((rows, D), f32), name="mm_dh", compiler_params=_cp(("arbitrary", "arbitrary"), 48),
    )(dz, w_full)


def _mm_dwin(h_t, dz):
    rows = dz.shape[0]
    pair = 2 * SHARD_IN
    tm = 512

    def body(a_ref, dz_ref, o_ref, db_ref):
        acc = _dot(a_ref[...], dz_ref[...])
        o_ref[0] = acc[:, :SHARD_IN].astype(bf16)
        o_ref[1] = acc[:, SHARD_IN:].astype(bf16)

        @pl.when(pl.program_id(1) == 0)
        def _():
            def step(i, s):
                blk = dz_ref[pl.ds(pl.multiple_of(i * BLK, BLK), BLK), :].astype(f32)
                return s + blk.reshape(BLK // 8, 8, pair).sum(axis=0)
            s = lax.fori_loop(0, rows // BLK, step, jnp.zeros((8, pair), f32))
            db_ref[...] = jnp.broadcast_to(_colsum(s), (8, pair))

    return pl.pallas_call(
        body, grid=(N_DEV // 2, D // tm),
        in_specs=[pl.BlockSpec((tm, rows), lambda p, i: (i, 0)),
                  pl.BlockSpec((rows, pair), lambda p, i: (0, p))],
        out_specs=[pl.BlockSpec((2, tm, SHARD_IN), lambda p, i: (p, i, 0)),
                   pl.BlockSpec((8, pair), lambda p, i: (0, p))],
        out_shape=[SDS((N_DEV, D, SHARD_IN), bf16), SDS((8, D_IN), f32)],
        name="mm_dwin", compiler_params=_cp(("arbitrary", "arbitrary"), VMEM_LIMIT_MB),
    )(h_t, dz)


def _transpose(x, name):
    rows, cols = x.shape

    def body(x_ref, o_ref):
        o_ref[...] = x_ref[...].astype(f32).T.astype(bf16)

    return pl.pallas_call(
        body, grid=(rows // BLK,),
        in_specs=[pl.BlockSpec((BLK, cols), lambda i: (i, 0))],
        out_specs=pl.BlockSpec((cols, BLK), lambda i: (0, i)),
        out_shape=SDS((cols, rows), bf16), name=name, compiler_params=_cp(("arbitrary",)),
    )(x)


def _relayout_w_in(wg):
    tm = 256

    def body(i_ref, o_ref):
        for d in range(N_DEV):
            o_ref[:, d * SHARD_IN:(d + 1) * SHARD_IN] = i_ref[d]

    return pl.pallas_call(
        body, grid=(D // tm,),
        in_specs=[pl.BlockSpec((N_DEV, tm, SHARD_IN), lambda i: (0, i, 0))],
        out_specs=pl.BlockSpec((tm, D_IN), lambda i: (i, 0)),
        out_shape=SDS((D, D_IN), bf16), name="relayout_w_in", compiler_params=_cp(("arbitrary",)),
    )(wg)


def _scan8(a, b, reverse):
    idx = lax.broadcasted_iota(jnp.int32, a.shape, 0)
    for s in (1, 2, 4):
        sh = 8 - s if reverse else s
        a_sh, b_sh = pltpu.roll(a, sh, 0), pltpu.roll(b, sh, 0)
        m = (idx < 8 - s) if reverse else (idx >= s)
        b = jnp.where(m, a * b_sh + b, b)
        a = jnp.where(m, a * a_sh, a)
    return a, b


def _shift_rows(prev8, cur, k):
    ext = jnp.concatenate([prev8, cur], axis=0)
    return pltpu.roll(ext, k, 0)[8:, :]


def _gates(xc, w_ra, b_ra, w_ri, b_ri, ls):
    xb = xc.astype(bf16)
    r = _sigmoid(_dot(xb, w_ra) + b_ra)
    ig = _sigmoid(_dot(xb, w_ri) + b_ri)
    la = LRU_C * r * ls
    a = jnp.exp(la)
    mult = jnp.sqrt(jnp.tanh(-la) * (1.0 + a * a))
    return xb, r, ig, a, mult


_RNN_IN_SPECS = lambda rows: [
    pl.BlockSpec((1, 24, 256), lambda n: (n, 0, 0)),
    pl.BlockSpec((1, RNN_BLOCK), lambda n: (0, n)),
    pl.BlockSpec((N_DEV, 2, None, 32, RNN_BLOCK), lambda n: (0, 0, n, 0, 0)),
    pl.BlockSpec((1, RNN_BLOCK), lambda n: (0, n)),
    pl.BlockSpec((1, RNN_BLOCK), lambda n: (0, n)),
    pl.BlockSpec((1, RNN_BLOCK), lambda n: (0, n)),
]


def _rnn_fwd(z, smallw, conv_b, wrg, b_ra, b_ri, lam):
    rows = z.shape[0]
    nb = rows // BLK
    col = lambda off: pl.BlockSpec((rows, RNN_BLOCK), lambda n: (0, off // RNN_BLOCK + n))

    def body(xr_ref, gr_ref, sw_ref, cb_ref, w_ref, bra_ref, bri_ref, lam_ref, xc_ref, hr_ref, ya_ref, a_s):
        cw = sw_ref[0, N_META:24, :]
        cb = cb_ref[...]
        w_ra = w_ref[:, 0].reshape(RNN_BLOCK, RNN_BLOCK)
        w_ri = w_ref[:, 1].reshape(RNN_BLOCK, RNN_BLOCK)
        b_ra_v, b_ri_v = bra_ref[...], bri_ref[...]
        ls = _log_sigmoid(lam_ref[...])
        rid = lax.broadcasted_iota(jnp.int32, (BLK, 1), 0)

        def blk_step(i, carry):
            r0 = pl.multiple_of(i * BLK, BLK)
            grow = rid + r0
            valid = grow >= ROW0
            cur = jnp.where(valid, xr_ref[pl.ds(r0, BLK), :], 0.0)
            prev8 = xr_ref[pl.ds(pl.multiple_of(jnp.maximum(r0 - 8, 0), 8), 8), :] * (i > 0).astype(f32)
            xc = cb + cw[0:1] * cur
            for k in range(1, CONV_WIDTH):
                xc = xc + cw[k:k + 1] * _shift_rows(prev8, cur, k)
            xc_ref[pl.ds(r0, BLK), :] = xc
            _, _, ig, a, mult = _gates(xc, w_ra, b_ra_v, w_ri, b_ri_v, ls)
            mult = jnp.where(grow == ROW0, 1.0, mult)
            a_s[pl.ds(r0, BLK), :] = a
            hr_ref[pl.ds(r0, BLK), :] = jnp.where(valid, mult * ig * xc, 0.0)
            return carry

        lax.fori_loop(0, nb, blk_step, 0)

        def scan_step(j, carry):
            r0 = pl.multiple_of(j * 8, 8)
            a, b = _scan8(a_s[pl.ds(r0, 8), :], hr_ref[pl.ds(r0, 8), :], False)
            h = b + a * carry
            hr_ref[pl.ds(r0, 8), :] = h
            return jnp.broadcast_to(h[7:8, :], (8, RNN_BLOCK))

        lax.fori_loop(0, rows // 8, scan_step, jnp.zeros((8, RNN_BLOCK), f32))

        def gate_step(i, carry):
            r0 = pl.multiple_of(i * BLK, BLK)
            ya_ref[pl.ds(r0, BLK), :] = (hr_ref[pl.ds(r0, BLK), :]
                                         * _silu_and_grad(gr_ref[pl.ds(r0, BLK), :])[0]).astype(bf16)
            return carry

        lax.fori_loop(0, nb, gate_step, 0)

    return pl.pallas_call(
        body, grid=(N_RNN_BLOCKS,),
        in_specs=[col(0), col(OFF_GR)] + _RNN_IN_SPECS(rows),
        out_specs=[pl.BlockSpec((rows, RNN_BLOCK), lambda n: (0, n))] * 3,
        out_shape=[SDS((rows, D), f32), SDS((rows, D), f32), SDS((rows, D), bf16)],
        scratch_shapes=[pltpu.VMEM((rows, RNN_BLOCK), f32)],
        name="rnn_fwd", compiler_params=_cp(("arbitrary",)),
    )(z, z, smallw, conv_b, wrg, b_ra, b_ri, lam)


def _rnn_bwd(dya, hr, xc, z, smallw, conv_b, wrg, b_ra, b_ri, lam):
    rows = z.shape[0]
    nb = rows // BLK
    col = lambda off: pl.BlockSpec((rows, RNN_BLOCK), lambda n: (0, off // RNN_BLOCK + n))
    blk = pl.BlockSpec((rows, RNN_BLOCK), lambda n: (0, n))

    def body(dya_ref, hr_ref, xc_ref, xr_ref, gr_ref, sw_ref, cb_ref, w_ref, bra_ref, bri_ref, lam_ref,
             dxr_ref, dgr_ref, dw_ref, vec_ref, a_s, lam_s, dxc_s, dw_s):
        cw = sw_ref[0, N_META:24, :]
        w_ra = w_ref[:, 0].reshape(RNN_BLOCK, RNN_BLOCK)
        w_ri = w_ref[:, 1].reshape(RNN_BLOCK, RNN_BLOCK)
        b_ra_v, b_ri_v = bra_ref[...], bri_ref[...]
        lam_v = lam_ref[...]
        ls = _log_sigmoid(lam_v)
        rid = lax.broadcasted_iota(jnp.int32, (BLK, 1), 0)
        zrow = jnp.zeros((1, RNN_BLOCK), f32)

        def p1(i, carry):
            r0 = pl.multiple_of(i * BLK, BLK)
            sl = pl.ds(r0, BLK)
            a = _gates(xc_ref[sl, :], w_ra, b_ra_v, w_ri, b_ri_v, ls)[3]
            a_s[sl, :] = a
            sg, dsg = _silu_and_grad(gr_ref[sl, :])
            d = dya_ref[sl, :]
            lam_s[sl, :] = d * sg
            dgr_ref[sl, :] = (d * hr_ref[sl, :] * dsg).astype(bf16)
            return carry

        lax.fori_loop(0, nb, p1, 0)

        def p2(jj, carry):
            j = rows // 8 - 1 - jj
            sl = pl.ds(pl.multiple_of(j * 8, 8), 8)
            a = a_s[sl, :]
            g = lam_s[sl, :]
            ca, cb_ = _scan8(a, a * g, True)
            mu = cb_ + ca * carry
            idx = lax.broadcasted_iota(jnp.int32, a.shape, 0)
            lam_s[sl, :] = g + jnp.where(idx < 7, pltpu.roll(mu, 7, 0), carry)
            return jnp.broadcast_to(mu[0:1, :], (8, RNN_BLOCK))

        lax.fori_loop(0, rows // 8, p2, jnp.zeros((8, RNN_BLOCK), f32))

        dw_s[...] = jnp.zeros_like(dw_s)

        def p3(i, carry):
            d_bra, d_bri, d_ls = carry
            r0 = pl.multiple_of(i * BLK, BLK)
            sl = pl.ds(r0, BLK)
            grow = rid + r0
            valid = grow >= ROW0
            first = grow == ROW0
            xcv = xc_ref[sl, :]
            xb, r, ig, a, mult = _gates(xcv, w_ra, b_ra_v, w_ri, b_ri_v, ls)
            mult = jnp.where(first, 1.0, mult)
            lam_t = lam_s[sl, :]
            du = jnp.where(valid, lam_t, 0.0)
            hprev = _shift_rows(hr_ref[pl.ds(pl.multiple_of(jnp.maximum(r0 - 8, 0), 8), 8), :] * (i > 0).astype(f32), hr_ref[sl, :], 1)
            da = lam_t * hprev
            dmult = jnp.where(first, 0.0, du * ig * xcv)
            di = du * mult * xcv
            dxc = du * mult * ig
            ratio = jnp.where(valid & jnp.logical_not(first), a * a / mult, 0.0)
            dla = da * a - dmult * ratio
            dpr = (dla * (LRU_C * ls)) * r * (1.0 - r)
            dpi = di * ig * (1.0 - ig)
            dprb, dpib = dpr.astype(bf16), dpi.astype(bf16)
            dw_s[0] += _dot_tn(xb, dprb)
            dw_s[1] += _dot_tn(xb, dpib)
            dxc_s[sl, :] = dxc + _dot_nt(dprb, w_ra) + _dot_nt(dpib, w_ri)
            return d_bra + _colsum(dpr), d_bri + _colsum(dpi), d_ls + _colsum(dla * (LRU_C * r))

        d_bra, d_bri, d_ls = lax.fori_loop(0, nb, p3, (zrow, zrow, zrow))

        def p4(i, carry):
            d_cb, d_w0, d_w1, d_w2, d_w3 = carry
            r0 = pl.multiple_of(i * BLK, BLK)
            sl = pl.ds(r0, BLK)
            grow = rid + r0
            valid = grow >= ROW0
            dxc = dxc_s[sl, :]
            nxt = dxc_s[pl.ds(pl.multiple_of(jnp.minimum(r0 + BLK, rows - 8), 8), 8), :] * (i < nb - 1).astype(f32)
            ext = jnp.concatenate([dxc, nxt], axis=0)
            dxr = cw[0:1] * dxc
            for k in range(1, CONV_WIDTH):
                dxr = dxr + cw[k:k + 1] * pltpu.roll(ext, BLK + 8 - k, 0)[:BLK, :]
            dxr_ref[sl, :] = jnp.where(valid, dxr, 0.0).astype(bf16)
            cur = jnp.where(valid, xr_ref[sl, :], 0.0)
            prev8 = xr_ref[pl.ds(pl.multiple_of(jnp.maximum(r0 - 8, 0), 8), 8), :] * (i > 0).astype(f32)
            dws = [d_w0 + _colsum(dxc * cur)]
            for k, acc in ((1, d_w1), (2, d_w2), (3, d_w3)):
                dws.append(acc + _colsum(dxc * _shift_rows(prev8, cur, k)))
            return (d_cb + _colsum(dxc), *dws)

        d_cb, d_w0, d_w1, d_w2, d_w3 = lax.fori_loop(0, nb, p4, (zrow,) * 5)

        d_lam = d_ls * _sigmoid(-lam_v)
        vec_ref[...] = jnp.concatenate([d_bra, d_bri, d_lam, d_cb, d_w0, d_w1, d_w2, d_w3], axis=0)
        dw_ref[:, 0] = dw_s[0].astype(bf16).reshape(N_DEV, 32, RNN_BLOCK)
        dw_ref[:, 1] = dw_s[1].astype(bf16).reshape(N_DEV, 32, RNN_BLOCK)

    return pl.pallas_call(
        body, grid=(N_RNN_BLOCKS,),
        in_specs=[blk, blk, blk, col(0), col(OFF_GR)] + _RNN_IN_SPECS(rows),
        out_specs=[blk, blk,
                   pl.BlockSpec((N_DEV, 2, None, 32, RNN_BLOCK), lambda n: (0, 0, n, 0, 0)),
                   pl.BlockSpec((8, RNN_BLOCK), lambda n: (0, n))],
        out_shape=[SDS((rows, D), bf16), SDS((rows, D), bf16),
                   SDS((N_DEV, 2, N_RNN_BLOCKS, 32, RNN_BLOCK), bf16), SDS((8, D), f32)],
        scratch_shapes=[pltpu.VMEM((rows, RNN_BLOCK), f32), pltpu.VMEM((rows, RNN_BLOCK), f32),
                        pltpu.VMEM((rows, RNN_BLOCK), f32), pltpu.VMEM((2, RNN_BLOCK, RNN_BLOCK), f32)],
        name="rnn_bwd", compiler_params=_cp(("arbitrary",), 48),
    )(dya, hr, xc, z, z, smallw, conv_b, wrg, b_ra, b_ri, lam)


def _rope_tables(rows):
    half = jnp.arange(HALF, dtype=f32)
    inv = ROPE_THETA ** (-half / HALF)
    pos = (jnp.arange(rows) - ROW0).astype(f32)
    ang = pos[:, None] * inv[None, :]
    cos, sin = jnp.cos(ang), jnp.sin(ang)
    cos128 = jnp.concatenate([cos, cos, cos, cos], axis=1)
    sin128 = jnp.concatenate([-sin, sin, -sin, sin], axis=1)
    return cos128, sin128


def _rope128(x, cos128, sin128):
    lane = lax.broadcasted_iota(jnp.int32, x.shape, 1)
    swapped = jnp.where(lane % HEAD_DIM < HALF, pltpu.roll(x, 128 - HALF, 1), pltpu.roll(x, HALF, 1))
    return x * cos128 + swapped * sin128


def _qkv_prep(z, cos128, sin128):
    rows = z.shape[0]

    def body(q_ref, kv_ref, c_ref, s_ref, qo_ref, ko_ref, vo_ref):
        c, s = c_ref[...], s_ref[...]
        for g in range(D // 128):
            qo_ref[:, g * 128:(g + 1) * 128] = (_rope128(q_ref[:, g * 128:(g + 1) * 128], c, s)
                                                * (HEAD_DIM ** -0.5)).astype(bf16)
        for g in range(2):
            kr = _rope128(kv_ref[:, g * 128:(g + 1) * 128], c, s)
            for j in range(2):
                ko_ref[2 * g + j] = kr[:, j * HEAD_DIM:(j + 1) * HEAD_DIM].astype(bf16)
        for h in range(N_KV):
            vo_ref[h] = kv_ref[:, 256 + h * HEAD_DIM:256 + (h + 1) * HEAD_DIM].astype(bf16)

    return pl.pallas_call(
        body, grid=(rows // BLK,),
        in_specs=[pl.BlockSpec((BLK, D), lambda i: (i, OFF_Q // D)),
                  pl.BlockSpec((BLK, 512), lambda i: (i, OFF_K // 512)),
                  pl.BlockSpec((BLK, 128), lambda i: (i, 0)),
                  pl.BlockSpec((BLK, 128), lambda i: (i, 0))],
        out_specs=[pl.BlockSpec((BLK, D), lambda i: (i, 0)),
                   pl.BlockSpec((N_KV, BLK, HEAD_DIM), lambda i: (0, i, 0)),
                   pl.BlockSpec((N_KV, BLK, HEAD_DIM), lambda i: (0, i, 0))],
        out_shape=[SDS((rows, D), bf16), SDS((N_KV, rows, HEAD_DIM), bf16), SDS((N_KV, rows, HEAD_DIM), bf16)],
        name="qkv_prep", compiler_params=_cp(("arbitrary",)),
    )(z, z, cos128, sin128)


def _attn_mask(n):
    qi = n * BLK + lax.broadcasted_iota(jnp.int32, (BLK, 2 * BLK + N_META), 0)
    c = lax.broadcasted_iota(jnp.int32, (BLK, 2 * BLK + N_META), 1)
    jb = (n - 1) * BLK + c
    band = (jb >= BLK) & (jb <= qi) & (qi - jb < BLK)
    meta = (ROW0 + c - 2 * BLK) <= qi
    return ((c < 2 * BLK) & band) | ((c >= 2 * BLK) & meta)


def _kv_specs(last):
    cl = lambda n: jnp.minimum(n, last)
    return [pl.BlockSpec((None, N_META, HEAD_DIM), lambda h, n: (h, ROW0 // N_META, 0)),
            pl.BlockSpec((None, BLK, HEAD_DIM), lambda h, n: (h, jnp.maximum(cl(n) - 1, 0), 0)),
            pl.BlockSpec((None, BLK, HEAD_DIM), lambda h, n: (h, cl(n), 0))]


def _attn_fwd(q_r, k_r, v_b, z, sinks):
    rows = q_r.shape[0]
    nb = rows // BLK

    def body(sink_ref, q_ref, km_ref, kp_ref, kc_ref, vm_ref, vp_ref, vc_ref, ga_ref, o_ref, yb_ref, lse_ref):
        h, n = pl.program_id(0), pl.program_id(1)
        kk = jnp.concatenate([kp_ref[...], kc_ref[...], km_ref[...]], axis=0)
        vv = jnp.concatenate([vp_ref[...], vc_ref[...], vm_ref[...]], axis=0)
        mask = _attn_mask(n)
        for g in range(GROUP):
            qg = q_ref[:, g * HEAD_DIM:(g + 1) * HEAD_DIM]
            s = jnp.where(mask, _dot_nt(qg, kk), NEG_INF)
            sink = sink_ref[h * GROUP + g]
            m = jnp.maximum(jnp.max(s, axis=-1, keepdims=True), sink)
            p = jnp.exp(s - m)
            den = jnp.sum(p, axis=-1, keepdims=True) + jnp.exp(sink - m)
            o_ref[:, g * HEAD_DIM:(g + 1) * HEAD_DIM] = _dot((p / den).astype(bf16), vv)
            lse_ref[:, g:g + 1] = m + jnp.log(den)
        yb_ref[...] = (o_ref[...] * _silu_and_grad(ga_ref[...])[0]).astype(bf16)

    tile = pl.BlockSpec((BLK, 512), lambda h, n: (n, h))
    return pl.pallas_call(
        body, grid=(N_KV, nb),
        in_specs=[pl.BlockSpec(memory_space=pltpu.SMEM), tile] + _kv_specs(nb - 1) + _kv_specs(nb - 1)
                 + [pl.BlockSpec((BLK, 512), lambda h, n: (n, OFF_GA // 512 + h))],
        out_specs=[tile, tile, pl.BlockSpec((None, BLK, GROUP), lambda h, n: (h, n, 0))],
        out_shape=[SDS((rows, D), f32), SDS((rows, D), bf16), SDS((N_KV, rows, GROUP), f32)],
        name="attn_fwd", compiler_params=_cp(("arbitrary", "arbitrary")),
    )(sinks, q_r, k_r, k_r, k_r, v_b, v_b, v_b, z)


def _attn_bwd(dyb, o32, lse, q_r, k_r, v_b, z, sinks):
    rows = q_r.shape[0]
    nb = rows // BLK
    cl = lambda n: jnp.minimum(n, nb - 1)

    def body(sink_ref, dyb_ref, o_ref, lse_ref, q_ref, km_ref, kp_ref, kc_ref, vm_ref, vp_ref, vc_ref, ga_ref,
             dq_ref, dga_ref, dk_ref, dv_ref, dkm_ref, dvm_ref, dsr_ref, ck_s, cv_s):
        h, n = pl.program_id(0), pl.program_id(1)

        @pl.when(n == 0)
        def _():
            dkm_ref[...] = jnp.zeros_like(dkm_ref)
            dvm_ref[...] = jnp.zeros_like(dvm_ref)
            ck_s[...] = jnp.zeros_like(ck_s)
            cv_s[...] = jnp.zeros_like(cv_s)

        @pl.when(n < nb)
        def _():
            kk = jnp.concatenate([kp_ref[...], kc_ref[...], km_ref[...]], axis=0)
            vv = jnp.concatenate([vp_ref[...], vc_ref[...], vm_ref[...]], axis=0)
            mask = _attn_mask(n)
            sg, dsg = _silu_and_grad(ga_ref[...])
            dyb_v = dyb_ref[...]
            o_v = o_ref[...]
            do = dyb_v * sg
            dga_ref[...] = (dyb_v * o_v * dsg).astype(bf16)
            dkk = jnp.zeros((2 * BLK + N_META, HEAD_DIM), f32)
            dvv = jnp.zeros((2 * BLK + N_META, HEAD_DIM), f32)
            for g in range(GROUP):
                cs = slice(g * HEAD_DIM, (g + 1) * HEAD_DIM)
                qg = q_ref[:, cs]
                lg = lse_ref[:, g:g + 1]
                s = jnp.where(mask, _dot_nt(qg, kk), NEG_INF)
                p = jnp.exp(s - lg)
                dog = do[:, cs]
                delta = jnp.sum(dog * o_v[:, cs], axis=-1, keepdims=True)
                dogb = dog.astype(bf16)
                ds = (p * (_dot_nt(dogb, vv) - delta)).astype(bf16)
                dsr_ref[:, g:g + 1] = -jnp.exp(sink_ref[h * GROUP + g] - lg) * delta
                dq_ref[:, cs] = _dot(ds, kk)
                dkk = dkk + _dot_tn(ds, qg)
                dvv = dvv + _dot_tn(p.astype(bf16), dogb)
            dk_ref[...] = ck_s[...] + dkk[:BLK]
            dv_ref[...] = cv_s[...] + dvv[:BLK]
            ck_s[...] = dkk[BLK:2 * BLK]
            cv_s[...] = dvv[BLK:2 * BLK]
            dkm_ref[...] += dkk[2 * BLK:]
            dvm_ref[...] += dvv[2 * BLK:]

        @pl.when(n == nb)
        def _():
            dk_ref[...] = ck_s[...]
            dv_ref[...] = cv_s[...]

    tile = pl.BlockSpec((BLK, 512), lambda h, n: (cl(n), h))
    kvout = pl.BlockSpec((None, BLK, HEAD_DIM), lambda h, n: (h, jnp.maximum(n - 1, 0), 0))
    mout = pl.BlockSpec((None, N_META, HEAD_DIM), lambda h, n: (h, 0, 0))
    stat = pl.BlockSpec((None, BLK, GROUP), lambda h, n: (h, cl(n), 0))
    return pl.pallas_call(
        body, grid=(N_KV, nb + 1),
        in_specs=[pl.BlockSpec(memory_space=pltpu.SMEM), tile, tile, stat, tile] + _kv_specs(nb - 1)
                 + _kv_specs(nb - 1) + [pl.BlockSpec((BLK, 512), lambda h, n: (cl(n), OFF_GA // 512 + h))],
        out_specs=[tile, tile, kvout, kvout, mout, mout, stat],
        out_shape=[SDS((rows, D), f32), SDS((rows, D), bf16),
                   SDS((N_KV, rows, HEAD_DIM), f32), SDS((N_KV, rows, HEAD_DIM), f32),
                   SDS((N_KV, N_META, HEAD_DIM), f32), SDS((N_KV, N_META, HEAD_DIM), f32),
                   SDS((N_KV, rows, GROUP), f32)],
        scratch_shapes=[pltpu.VMEM((BLK, HEAD_DIM), f32), pltpu.VMEM((BLK, HEAD_DIM), f32)],
        name="attn_bwd", compiler_params=_cp(("arbitrary", "arbitrary")),
    )(sinks, dyb, o32, lse, q_r, k_r, k_r, k_r, v_b, v_b, v_b, z)


def _qkv_finish(dq, dk, dv, dkm, dvm, cos128, sin128):
    rows = dq.shape[0]

    def body(dq_ref, dk_ref, dv_ref, dkm_ref, dvm_ref, c_ref, s_ref, oq_ref, okv_ref):
        first = (pl.program_id(0) == 0).astype(f32)
        c, s = c_ref[...], -s_ref[...]
        for g in range(D // 128):
            oq_ref[:, g * 128:(g + 1) * 128] = (_rope128(dq_ref[:, g * 128:(g + 1) * 128], c, s)
                                                * (HEAD_DIM ** -0.5)).astype(bf16)
        pad = jnp.zeros((ROW0, HEAD_DIM), f32)
        ks = [dk_ref[h] + first * jnp.concatenate([pad, dkm_ref[h]], axis=0) for h in range(N_KV)]
        vs = [dv_ref[h] + first * jnp.concatenate([pad, dvm_ref[h]], axis=0) for h in range(N_KV)]
        for g in range(2):
            kp = jnp.concatenate([ks[2 * g], ks[2 * g + 1]], axis=1)
            okv_ref[:, g * 128:(g + 1) * 128] = _rope128(kp, c, s).astype(bf16)
            okv_ref[:, 256 + g * 128:256 + (g + 1) * 128] = jnp.concatenate([vs[2 * g], vs[2 * g + 1]], axis=1).astype(bf16)

    kv = pl.BlockSpec((N_KV, BLK, HEAD_DIM), lambda i: (0, i, 0))
    mt = pl.BlockSpec((N_KV, N_META, HEAD_DIM), lambda i: (0, 0, 0))
    return pl.pallas_call(
        body, grid=(rows // BLK,),
        in_specs=[pl.BlockSpec((BLK, D), lambda i: (i, 0)), kv, kv, mt, mt,
                  pl.BlockSpec((BLK, 128), lambda i: (i, 0)), pl.BlockSpec((BLK, 128), lambda i: (i, 0))],
        out_specs=[pl.BlockSpec((BLK, D), lambda i: (i, 0)), pl.BlockSpec((BLK, 512), lambda i: (i, 0))],
        out_shape=[SDS((rows, D), bf16), SDS((rows, 512), bf16)],
        name="qkv_finish", compiler_params=_cp(("arbitrary",)),
    )(dq, dk, dv, dkm, dvm, cos128, sin128)


_TW = 512
_mix_tile = pl.BlockSpec((BLK, _TW), lambda i, j: (i, j))
_mix_ga = pl.BlockSpec((BLK, _TW), lambda i, j: (i, OFF_G // _TW + j))
_mix_gb = pl.BlockSpec((BLK, _TW), lambda i, j: (i, (OFF_G + D) // _TW + j))


def _mix_fwd(y_a, y_b, z):
    rows = y_a.shape[0]

    def body(ya_ref, yb_ref, ga_ref, gb_ref, o_ref):
        o_ref[...] = (_sigmoid(ga_ref[...]) * ya_ref[...] + _sigmoid(gb_ref[...]) * yb_ref[...]).astype(bf16)

    return pl.pallas_call(
        body, grid=(rows // BLK, D // _TW), in_specs=[_mix_tile, _mix_tile, _mix_ga, _mix_gb],
        out_specs=_mix_tile, out_shape=SDS((rows, D), bf16),
        name="mix_fwd", compiler_params=_cp(("arbitrary", "arbitrary")),
    )(y_a, y_b, z, z)


def _mix_bwd(dmixed, y_a, y_b, z):
    rows = y_a.shape[0]

    def body(dm_ref, ya_ref, yb_ref, ga_ref, gb_ref, dya_ref, dyb_ref, dga_ref, dgb_ref):
        dm = dm_ref[...]
        sa, sb = _sigmoid(ga_ref[...]), _sigmoid(gb_ref[...])
        dya_ref[...] = (dm * sa).astype(bf16)
        dyb_ref[...] = (dm * sb).astype(bf16)
        dga_ref[...] = (dm * ya_ref[...] * sa * (1.0 - sa)).astype(bf16)
        dgb_ref[...] = (dm * yb_ref[...] * sb * (1.0 - sb)).astype(bf16)

    return pl.pallas_call(
        body, grid=(rows // BLK, D // _TW), in_specs=[_mix_tile, _mix_tile, _mix_tile, _mix_ga, _mix_gb],
        out_specs=[_mix_tile] * 4, out_shape=[SDS((rows, D), bf16)] * 4,
        name="mix_bwd", compiler_params=_cp(("arbitrary", "arbitrary")),
    )(dmixed, y_a, y_b, z, z)


def _final_ln(out32, h32, tgt, ln_g, ln_b):
    rows = out32.shape[0]

    def body(o_ref, h_ref, t_ref, g_ref, b_ref, du_ref, dub_ref, st_ref):
        i = pl.program_id(0)
        g = g_ref[...]
        y, xhat, rstd = _ln_rows(ALPHA * h_ref[...] + o_ref[...], g, b_ref[...])
        e = jnp.where(i > 0, y - t_ref[0], 0.0)
        dy = e * (1.0 / D)
        du = _ln_rows_bwd(dy, g, xhat, rstd)
        du_ref[...] = du
        dub_ref[...] = du.astype(bf16)
        st = jnp.concatenate([_colsum(dy * xhat), _colsum(dy), _colsum(du), _colsum(e * e) * (0.5 / D),
                              jnp.zeros((4, D), f32)], axis=0)

        @pl.when(i == 0)
        def _():
            st_ref[...] = st

        @pl.when(i > 0)
        def _():
            st_ref[...] += st

    row = pl.BlockSpec((BLK, D), lambda i: (i, 0))
    vec = pl.BlockSpec((1, D), lambda i: (0, 0))
    return pl.pallas_call(
        body, grid=(rows // BLK,),
        in_specs=[row, row, pl.BlockSpec((1, BLK, D), lambda i: (0, jnp.maximum(i - 1, 0), 0)), vec, vec],
        out_specs=[row, row, pl.BlockSpec((8, D), lambda i: (0, 0))],
        out_shape=[SDS((rows, D), f32), SDS((rows, D), bf16), SDS((8, D), f32)],
        name="final_ln", compiler_params=_cp(("arbitrary",)),
    )(out32, h32, tgt, ln_g, ln_b)


def _assemble_dz(dxr, dgr, dq, dkv, dga, dma, dmb):
    rows = dxr.shape[0]
    parts = [(dxr, D), (dgr, D), (dq, D), (dkv, 512), (dga, D), (dma, D), (dmb, D)]

    def body(*refs):
        o_ref = refs[-1]
        off = 0
        for r, (_, w) in zip(refs[:-1], parts):
            o_ref[:, off:off + w] = r[...]
            off += w

    return pl.pallas_call(
        body, grid=(rows // BLK,),
        in_specs=[pl.BlockSpec((BLK, w), lambda i: (i, 0)) for _, w in parts],
        out_specs=pl.BlockSpec((BLK, D_IN), lambda i: (i, 0)),
        out_shape=SDS((rows, D_IN), bf16), name="assemble_dz", compiler_params=_cp(("arbitrary",)),
    )(*[p for p, _ in parts])


def _local_step(x, tgt, w_full, w3, wrg, smallw, p):
    rows = x.shape[1] + BLK
    cos128, sin128 = _rope_tables(rows)
    sinks = p["sinks"].reshape(N_KV * GROUP)
    h32, hb, h_t = _ln_emb(x, smallw, p["ln_emb_g"], p["ln_emb_b"])
    z = _mm(hb, w_full, bias=p["b_in"], name="mm_z")
    xc, hr, ya = _rnn_fwd(z, smallw, p["conv_b"], wrg, p["b_ra"], p["b_ri"], p["lru_lambda"])
    q_r, k_r, v_b = _qkv_prep(z, cos128, sin128)
    o32, yb, lse = _attn_fwd(q_r, k_r, v_b, z, sinks)
    y_a = _mm(ya, w3, sel=0, name="mm_ya")
    y_b = _mm(yb, w3, sel=1, name="mm_yb")
    mixed = _mix_fwd(y_a, y_b, z)
    out32 = _mm(mixed, w3, sel=2, bias=p["b_o"], name="mm_out")
    du32, dub, st_out = _final_ln(out32, h32, tgt, p["ln_g"], p["ln_b"])

    g_wo = _mm(_transpose(mixed, "t_mixed"), dub, out_dtype=bf16, name="mm_dwo")
    dmixed = _mm(dub, w3, sel=2, nt=True, name="mm_dmixed")
    dya_b, dyb_b, dma, dmb = _mix_bwd(dmixed, y_a, y_b, z)
    g_wrnn = _mm(_transpose(ya, "t_ya"), dya_b, out_dtype=bf16, name="mm_dwrnn")
    g_wattn = _mm(_transpose(yb, "t_yb"), dyb_b, out_dtype=bf16, name="mm_dwattn")
    dya = _mm(dya_b, w3, sel=0, nt=True, name="mm_dya")
    dyb = _mm(dyb_b, w3, sel=1, nt=True, name="mm_dyb")
    dxr, dgr, g_wrg, vec_rnn = _rnn_bwd(dya, hr, xc, z, smallw, p["conv_b"], wrg, p["b_ra"], p["b_ri"],
                                        p["lru_lambda"])
    dq_r, dga, dk, dv, dkm, dvm, dsr = _attn_bwd(dyb, o32, lse, q_r, k_r, v_b, z, sinks)
    dq, dkv = _qkv_finish(dq_r, dk, dv, dkm, dvm, cos128, sin128)
    dz = _assemble_dz(dxr, dgr, dq, dkv, dga, dma, dmb)
    g_win, db_in = _mm_dwin(h_t, dz)
    dh = _mm_dh(dz, w_full)
    grad_x, dmeta, st_emb = _ln_emb_bwd(dh, du32, x, smallw, p["ln_emb_g"])
    return dict(st_out=st_out, st_emb=st_emb, vec_rnn=vec_rnn, dsr=dsr, db_in=db_in, dmeta=dmeta,
                grad_x=grad_x, g_win=g_win, g_wo=g_wo, g_wrnn=g_wrnn, g_wattn=g_wattn, g_wrg=g_wrg)


_ANY = pl.BlockSpec(memory_space=pl.ANY)
_VMEM = pl.BlockSpec(memory_space=pltpu.VMEM)


def _place():
    x, y, c = lax.axis_index("x"), lax.axis_index("y"), lax.axis_index("c")
    return x, y, c


def _dev(px, py, pc):
    return 4 * px + 2 * py + pc


def _cast_w_in(w_in):
    tm = 256

    def body(i_ref, o_ref):
        o_ref[...] = i_ref[0].astype(bf16)

    return pl.pallas_call(
        body, grid=(D // tm,),
        in_specs=[pl.BlockSpec((1, tm, SHARD_IN), lambda i: (0, i, 0))],
        out_specs=pl.BlockSpec((tm, SHARD_IN), lambda i: (i, 0)),
        out_shape=SDS((D, SHARD_IN), bf16), name="cast_w_in", compiler_params=_cp(("arbitrary",)),
    )(w_in)


def _cast_small(w_rnn_out, w_attn_out, w_o, w_ra, w_ri, meta, conv_w):
    def body(a_ref, b_ref, c_ref, ra_ref, ri_ref, m_ref, cw_ref, w3_ref, wrg_ref, sw_ref):
        w3_ref[0] = a_ref[0].astype(bf16)
        w3_ref[1] = b_ref[0].astype(bf16)
        w3_ref[2] = c_ref[0].astype(bf16)
        wrg_ref[0] = ra_ref[0].astype(bf16)
        wrg_ref[1] = ri_ref[0].astype(bf16)
        sw_ref[...] = jnp.concatenate([m_ref[...], cw_ref[0], jnp.zeros((4, 256), f32)], axis=0)

    return pl.pallas_call(
        body,
        out_shape=[SDS((3, 256, D), bf16), SDS((2, N_RNN_BLOCKS, 32, RNN_BLOCK), bf16), SDS((24, 256), f32)],
        name="cast_small", compiler_params=_cp(None),
    )(w_rnn_out, w_attn_out, w_o, w_ra, w_ri, meta, conv_w)


def _all_gather(shards):
    n = len(shards)

    def body(*refs):
        ins, outs = refs[:n], refs[n:2 * n]
        send_sems, recv_sems, local_sems = refs[2 * n:]
        x, y, c = _place()
        me, sibling = (x, y, c), (x, y, 1 - c)
        chips = [(1 - x, y), (x, 1 - y), (1 - x, 1 - y)]

        def copy(a, k, block, to, src=None):
            dst = outs[a].at[_dev(*block)]
            return pltpu.make_async_remote_copy(
                src_ref=dst if src is None else src, dst_ref=dst,
                send_sem=send_sems.at[a * 7 + k], recv_sem=recv_sems.at[a * 7 + k],
                device_id=to, device_id_type=MESH)

        mine = [pltpu.make_async_copy(ins[a], outs[a].at[_dev(*me)], local_sems.at[a]) for a in range(n)]
        for cp in mine:
            cp.start()
        first = []
        for a in range(n):
            first.append(copy(a, 0, me, sibling, src=ins[a]))
            first += [copy(a, 1 + j, me, (*chip, c), src=ins[a]) for j, chip in enumerate(chips)]
        for cp in first:
            cp.start()
        passed = []
        for a in range(n):
            for j, chip in enumerate(chips):
                copy(a, 1 + j, (*chip, c), me).wait_recv()
                cp = copy(a, 4 + j, (*chip, c), sibling)
                cp.start()
                passed.append(cp)
        for a in range(n):
            copy(a, 0, sibling, me).wait_recv()
            for j, chip in enumerate(chips):
                copy(a, 4 + j, (*chip, 1 - c), me).wait_recv()
        for cp in first + passed:
            cp.wait_send()
        for cp in mine:
            cp.wait()

    return pl.pallas_call(
        body, in_specs=[_ANY] * n, out_specs=[_ANY] * n,
        out_shape=[SDS((N_DEV, *s.shape), s.dtype) for s in shards],
        scratch_shapes=[pltpu.SemaphoreType.DMA((7 * n,)), pltpu.SemaphoreType.DMA((7 * n,)),
                        pltpu.SemaphoreType.DMA((n,))],
        name="all_gather_weights",
    )(*shards)


def _exchange_siblings(grads):
    n = len(grads)

    def body(*refs):
        ins, outs = refs[:n], refs[n:2 * n]
        send_sems, recv_sems = refs[2 * n:]
        x, y, c = _place()
        copies = []
        for a in range(n):
            for q in range(4):
                copies.append(pltpu.make_async_remote_copy(
                    src_ref=ins[a].at[2 * q + (1 - c)], dst_ref=outs[a].at[q],
                    send_sem=send_sems.at[4 * a + q], recv_sem=recv_sems.at[4 * a + q],
                    device_id=(x, y, 1 - c), device_id_type=MESH))
        for cp in copies:
            cp.start()
        for cp in copies:
            cp.wait()

    return pl.pallas_call(
        body, in_specs=[_ANY] * n, out_specs=[_ANY] * n,
        out_shape=[SDS((4, *g.shape[1:]), g.dtype) for g in grads],
        scratch_shapes=[pltpu.SemaphoreType.DMA((4 * n,)), pltpu.SemaphoreType.DMA((4 * n,))],
        name="exchange_siblings",
    )(*grads)


def _exchange_chips(parts):
    n = len(parts)

    def body(*refs):
        ins, outs = refs[:n], refs[n:2 * n]
        send_sems, recv_sems = refs[2 * n:]
        x, y, c = _place()
        chips = [(1 - x, y), (x, 1 - y), (1 - x, 1 - y)]
        copies = []
        for a in range(n):
            for j, (qx, qy) in enumerate(chips):
                copies.append(pltpu.make_async_remote_copy(
                    src_ref=ins[a].at[2 * qx + qy], dst_ref=outs[a].at[j],
                    send_sem=send_sems.at[3 * a + j], recv_sem=recv_sems.at[3 * a + j],
                    device_id=(qx, qy, c), device_id_type=MESH))
        for cp in copies:
            cp.start()
        for cp in copies:
            cp.wait()

    return pl.pallas_call(
        body, in_specs=[_ANY] * n, out_specs=[_ANY] * n,
        out_shape=[SDS((3, *p.shape[1:]), p.dtype) for p in parts],
        scratch_shapes=[pltpu.SemaphoreType.DMA((3 * n,)), pltpu.SemaphoreType.DMA((3 * n,))],
        name="exchange_chips",
    )(*parts)


def _pair_sum(g, r1, c_idx, name):
    _, r, w = g.shape
    tr = min(r, 256)

    def body(c_ref, g_ref, r_ref, o_ref):
        o_ref[...] = (g_ref[...].astype(f32) + r_ref[...].astype(f32)).astype(bf16)

    return pl.pallas_call(
        body,
        grid_spec=pltpu.PrefetchScalarGridSpec(
            num_scalar_prefetch=1, grid=(4, r // tr),
            in_specs=[pl.BlockSpec((None, tr, w), lambda q, i, c_ref: (2 * q + c_ref[0], i, 0)),
                      pl.BlockSpec((None, tr, w), lambda q, i, c_ref: (q, i, 0))],
            out_specs=pl.BlockSpec((None, tr, w), lambda q, i, c_ref: (q, i, 0))),
        out_shape=SDS((4, r, w), bf16), name=name, compiler_params=_cp(("arbitrary", "arbitrary")),
    )(c_idx, g, r1)


def _adamw(w, g, m, v):
    m = ADAM_B1 * m + (1.0 - ADAM_B1) * g
    v = ADAM_B2 * v + (1.0 - ADAM_B2) * (g * g)
    m_hat = m / (1.0 - ADAM_B1 ** ADAM_STEP)
    v_hat = v / (1.0 - ADAM_B2 ** ADAM_STEP)
    delta = -ADAM_LR * (m_hat / (jnp.sqrt(v_hat) + ADAM_EPS) + ADAM_WD * w)
    return delta, m, v


def _adamw_big(part, r2, q_idx, w, m, v, name, row_off=0):
    r, wd = w.shape
    tr = min(r, 256)

    def body(q_ref, p_ref, r_ref, w_ref, m_ref, v_ref, g_out, d_out, m_out, v_out):
        g = p_ref[...].astype(f32)
        for j in range(3):
            g = g + r_ref[j].astype(f32)
        d, mn, vn = _adamw(w_ref[...], g, m_ref[...], v_ref[...])
        g_out[...] = g
        d_out[...] = d
        m_out[...] = mn
        v_out[...] = vn

    tile = pl.BlockSpec((tr, wd), lambda i, q_ref: (i, 0))
    return pl.pallas_call(
        body,
        grid_spec=pltpu.PrefetchScalarGridSpec(
            num_scalar_prefetch=1, grid=(r // tr,),
            in_specs=[pl.BlockSpec((None, tr, wd), lambda i, q_ref: (q_ref[0], row_off + i, 0)),
                      pl.BlockSpec((3, tr, wd), lambda i, q_ref: (0, row_off + i, 0)), tile, tile, tile],
            out_specs=[tile] * 4),
        out_shape=[SDS((r, wd), f32)] * 4, name=name, compiler_params=_cp(("arbitrary",), 48),
    )(q_idx, part, r2, w, m, v)


_SMALL_ROWS = 24


def _pack_small(st_emb, vec_rnn, st_out, dsr, db_in, dmeta):
    def body(se_ref, vr_ref, so_ref, dsr_ref, db_ref, dm_ref, sm_ref, sm2_ref):
        sm_ref[...] = jnp.zeros_like(sm_ref)
        sm2_ref[...] = jnp.zeros_like(sm2_ref)
        sm_ref[0:2, :] = se_ref[0:2, :]
        sm_ref[2:3, :] = vr_ref[3:4, :]
        sm_ref[3:6, :] = vr_ref[0:3, :]
        sm_ref[6:7, :] = so_ref[2:3, :]
        sm_ref[7:9, :] = so_ref[0:2, :]
        for h in range(N_KV):
            sm_ref[9:10, h * GROUP:(h + 1) * GROUP] = _colsum(dsr_ref[h])
        for j in range(6):
            sm_ref[16 + j:17 + j, :] = db_ref[0:1, j * D:(j + 1) * D]
        sm_ref[22:23, 0:D_IN - 6 * D] = db_ref[0:1, 6 * D:D_IN]
        for s in range(N_DEV):
            sm2_ref[s, 0:N_META, :] = dm_ref[:, s * 256:(s + 1) * 256]
            sm2_ref[s, N_META:N_META + CONV_WIDTH, :] = vr_ref[4:8, s * 256:(s + 1) * 256]

    return pl.pallas_call(
        body, out_shape=[SDS((_SMALL_ROWS, D), f32), SDS((N_DEV, 24, 256), f32)],
        name="pack_small", compiler_params=_cp(None),
    )(st_emb, vec_rnn, st_out, dsr, db_in, dmeta)


def _small_allreduce(sm, sm2):
    def body(sm_ref, sm2_ref, o_ref, o2_ref, buf, buf2, send_sems, recv_sems):
        x, y, c = _place()
        me = _dev(x, y, c)
        copies = []
        for f in range(1, N_DEV):
            fx, fy, fc = f // 4, (f // 2) % 2, f % 2
            peer = ((x + fx) % 2, (y + fy) % 2, (c + fc) % 2)
            for t, (src, dst) in enumerate(((sm_ref, buf), (sm2_ref, buf2))):
                k = 2 * (f - 1) + t
                copies.append(pltpu.make_async_remote_copy(
                    src_ref=src, dst_ref=dst.at[me], send_sem=send_sems.at[k], recv_sem=recv_sems.at[k],
                    device_id=peer, device_id_type=MESH))
        for cp in copies:
            cp.start()
        buf[me] = sm_ref[...]
        buf2[me] = sm2_ref[...]
        for cp in copies:
            cp.wait()
        acc, acc2 = buf[0], buf2[0]
        for e in range(1, N_DEV):
            acc, acc2 = acc + buf[e], acc2 + buf2[e]
        o_ref[...] = acc
        o2_ref[...] = acc2

    return pl.pallas_call(
        body, in_specs=[_VMEM, _VMEM], out_specs=[_VMEM, _VMEM],
        out_shape=[SDS(sm.shape, f32), SDS(sm2.shape, f32)],
        scratch_shapes=[pltpu.VMEM((N_DEV, *sm.shape), f32), pltpu.VMEM((N_DEV, *sm2.shape), f32),
                        pltpu.SemaphoreType.DMA((14,)), pltpu.SemaphoreType.DMA((14,))],
        name="small_allreduce",
    )(sm, sm2)


_SMALL_ROW_OF = {"ln_emb_g": 0, "ln_emb_b": 1, "conv_b": 2, "b_ra": 3, "b_ri": 4, "lru_lambda": 5, "b_o": 6,
                 "ln_g": 7, "ln_b": 8}
_SMALL_NAMES = ["ln_emb_g", "ln_emb_b", "conv_b", "b_ra", "b_ri", "lru_lambda", "b_o", "ln_g", "ln_b",
                "sinks", "b_in", "meta_tokens", "conv_w"]


def _small_update(sm, sm2_mine, wmv):
    def grad_of(name, sm_ref, s2_ref):
        if name in _SMALL_ROW_OF:
            r = _SMALL_ROW_OF[name]
            return sm_ref[r:r + 1, :]
        if name == "sinks":
            return sm_ref[9:10, 0:N_KV * GROUP]
        if name == "b_in":
            return jnp.concatenate([sm_ref[16 + j:17 + j, :] for j in range(7)], axis=1)[:, :D_IN]
        if name == "meta_tokens":
            return s2_ref[0:N_META, :]
        return s2_ref[N_META:N_META + CONV_WIDTH, :]

    def body(*refs):
        sm_ref, s2_ref = refs[0], refs[1]
        ins = refs[2:2 + 3 * len(_SMALL_NAMES)]
        outs = refs[2 + 3 * len(_SMALL_NAMES):]
        for i, name in enumerate(_SMALL_NAMES):
            w_ref, m_ref, v_ref = ins[3 * i:3 * i + 3]
            g = grad_of(name, sm_ref, s2_ref)
            d, mn, vn = _adamw(w_ref[...], g, m_ref[...], v_ref[...])
            outs[4 * i][...] = g
            outs[4 * i + 1][...] = d
            outs[4 * i + 2][...] = mn
            outs[4 * i + 3][...] = vn

    args, out_shape = [sm, sm2_mine], []
    for name in _SMALL_NAMES:
        args += list(wmv[name])
        out_shape += [SDS(wmv[name][0].shape, f32)] * 4
    res = pl.pallas_call(body, out_shape=out_shape, name="small_update", compiler_params=_cp(None))(*args)
    return {name: tuple(res[4 * i:4 * i + 4]) for i, name in enumerate(_SMALL_NAMES)}


_WEIGHTS = ["meta_tokens", "ln_emb_g", "ln_emb_b", "w_in", "b_in", "conv_w", "conv_b", "w_ra", "b_ra", "w_ri",
            "b_ri", "lru_lambda", "sinks", "w_rnn_out", "w_attn_out", "w_o", "b_o", "ln_g", "ln_b"]
_SMALL_2D = {"meta_tokens": (N_META, 256), "conv_w": (CONV_WIDTH, 256), "b_in": (1, D_IN), "sinks": (1, N_KV * GROUP)}


def kernel(x, meta_tokens, ln_emb_g, ln_emb_b, w_in, b_in, conv_w, conv_b, w_ra, b_ra, w_ri, b_ri, lru_lambda, sinks, w_rnn_out, w_attn_out, w_o, b_o, ln_g, ln_b, loss_target, m_meta_tokens, m_ln_emb_g, m_ln_emb_b, m_w_in, m_b_in, m_conv_w, m_conv_b, m_w_ra, m_b_ra, m_w_ri, m_b_ri, m_lru_lambda, m_sinks, m_w_rnn_out, m_w_attn_out, m_w_o, m_b_o, m_ln_g, m_ln_b, v_meta_tokens, v_ln_emb_g, v_ln_emb_b, v_w_in, v_b_in, v_conv_w, v_conv_b, v_w_ra, v_b_ra, v_w_ri, v_b_ri, v_lru_lambda, v_sinks, v_w_rnn_out, v_w_attn_out, v_w_o, v_b_o, v_ln_g, v_ln_b):
    w = dict(meta_tokens=meta_tokens, ln_emb_g=ln_emb_g, ln_emb_b=ln_emb_b, w_in=w_in, b_in=b_in, conv_w=conv_w,
             conv_b=conv_b, w_ra=w_ra, b_ra=b_ra, w_ri=w_ri, b_ri=b_ri, lru_lambda=lru_lambda, sinks=sinks,
             w_rnn_out=w_rnn_out, w_attn_out=w_attn_out, w_o=w_o, b_o=b_o, ln_g=ln_g, ln_b=ln_b)
    m = dict(meta_tokens=m_meta_tokens, ln_emb_g=m_ln_emb_g, ln_emb_b=m_ln_emb_b, w_in=m_w_in, b_in=m_b_in,
             conv_w=m_conv_w, conv_b=m_conv_b, w_ra=m_w_ra, b_ra=m_b_ra, w_ri=m_w_ri, b_ri=m_b_ri,
             lru_lambda=m_lru_lambda, sinks=m_sinks, w_rnn_out=m_w_rnn_out, w_attn_out=m_w_attn_out, w_o=m_w_o,
             b_o=m_b_o, ln_g=m_ln_g, ln_b=m_ln_b)
    v = dict(meta_tokens=v_meta_tokens, ln_emb_g=v_ln_emb_g, ln_emb_b=v_ln_emb_b, w_in=v_w_in, b_in=v_b_in,
             conv_w=v_conv_w, conv_b=v_conv_b, w_ra=v_w_ra, b_ra=v_b_ra, w_ri=v_w_ri, b_ri=v_b_ri,
             lru_lambda=v_lru_lambda, sinks=v_sinks, w_rnn_out=v_w_rnn_out, w_attn_out=v_w_attn_out, w_o=v_w_o,
             b_o=v_b_o, ln_g=v_ln_g, ln_b=v_ln_b)
    px, py, pc = _place()
    c_idx = jnp.reshape(pc, (1,)).astype(jnp.int32)
    q_idx = jnp.reshape(2 * px + py, (1,)).astype(jnp.int32)

    w3_s, wrg_s, small_s = _cast_small(w_rnn_out, w_attn_out, w_o, w_ra, w_ri, meta_tokens, conv_w)
    wg, w3, wrg, smallw = _all_gather([_cast_w_in(w_in), w3_s, wrg_s, small_s])
    w_full = _relayout_w_in(wg)

    vec = lambda name: w[name].reshape(1, -1)
    p = {k: vec(k) for k in ("ln_emb_g", "ln_emb_b", "b_in", "conv_b", "b_ra", "b_ri", "lru_lambda", "sinks",
                             "b_o", "ln_g", "ln_b")}
    loc = _local_step(x, loss_target, w_full, w3, wrg, smallw, p)
    loss = lax.psum(jnp.sum(loc["st_out"][3]), ("x", "y", "c"))

    grads = [loc["g_win"], loc["g_wo"].reshape(N_DEV, 256, D), loc["g_wrnn"].reshape(N_DEV, 256, D),
             loc["g_wattn"].reshape(N_DEV, 256, D), loc["g_wrg"].reshape(N_DEV, 2 * RNN_BLOCK, RNN_BLOCK)]
    r1 = _exchange_siblings(grads)
    parts = [_pair_sum(g, r, c_idx, "pair_sum_%d" % i) for i, (g, r) in enumerate(zip(grads, r1))]
    r2 = _exchange_chips(parts)
    big = {}
    for i, name in enumerate(("w_in", "w_o", "w_rnn_out", "w_attn_out")):
        shp = w[name].shape
        two_d = (shp[-2], shp[-1])
        res = _adamw_big(parts[i], r2[i], q_idx, w[name].reshape(two_d), m[name].reshape(two_d),
                         v[name].reshape(two_d), "adamw_" + name)
        big[name] = tuple(t.reshape(shp) for t in res)
    for t, name in enumerate(("w_ra", "w_ri")):
        shp = w[name].shape
        two_d = (RNN_BLOCK, RNN_BLOCK)
        res = _adamw_big(parts[4], r2[4], q_idx, w[name].reshape(two_d), m[name].reshape(two_d),
                         v[name].reshape(two_d), "adamw_" + name, row_off=t)
        big[name] = tuple(t_.reshape(shp) for t_ in res)

    sm, sm2 = _pack_small(loc["st_emb"], loc["vec_rnn"], loc["st_out"], loc["dsr"], loc["db_in"], loc["dmeta"])
    sm, sm2 = _small_allreduce(sm, sm2)
    sm2_mine = lax.dynamic_index_in_dim(sm2, _dev(px, py, pc), 0, keepdims=False)
    two = lambda name, t: t.reshape(_SMALL_2D.get(name, (1, D)))
    small = _small_update(sm, sm2_mine, {k: (two(k, w[k]), two(k, m[k]), two(k, v[k])) for k in _SMALL_NAMES})
    res = dict(big)
    for k in _SMALL_NAMES:
        res[k] = tuple(t.reshape(w[k].shape) for t in small[k])

    outs = [loss, loc["grad_x"]]
    for j in range(4):
        outs += [res[k][j] for k in _WEIGHTS]
    return tuple(outs)
```

```python
import functools

import jax
import jax.numpy as jnp
from jax import lax
from jax.experimental import pallas as pl
from jax.experimental.pallas import tpu as pltpu

f32, bf16 = jnp.float32, jnp.bfloat16
SDS = jax.ShapeDtypeStruct

N_DEV = 8
D = 2048
N_META = 16
BLK = 128
ROW0 = BLK - N_META
N_RNN_BLOCKS = 8
RNN_BLOCK = D // N_RNN_BLOCKS
CONV_WIDTH = 4
LRU_C = 8.0
HEAD_DIM = 64
N_KV = 4
GROUP = 8
HALF = HEAD_DIM // 2
ROPE_THETA = 10000.0
NEG_INF = -1e30
LN_EPS = 1e-5
ALPHA = 2.0 ** 0.25
D_IN = 12800
SHARD_IN = D_IN // N_DEV
OFF_GR, OFF_Q, OFF_K, OFF_V, OFF_GA, OFF_G = 2048, 4096, 6144, 6400, 6656, 8704
ADAM_LR, ADAM_B1, ADAM_B2, ADAM_EPS, ADAM_WD, ADAM_STEP = 1e-3, 0.9, 0.999, 1e-8, 0.01, 10
VMEM_LIMIT_MB = 56
MESH = pl.DeviceIdType.MESH


def _cp(sem=None, vmem_mb=40):
    return pltpu.CompilerParams(dimension_semantics=sem, vmem_limit_bytes=vmem_mb * 2 ** 20)


def _row_chunk(m):
    best = 16
    for c in range(16, 641, 16):
        if m % c == 0:
            best = c
    return best


def _sigmoid(x):
    return 1.0 / (1.0 + jnp.exp(-x))


def _silu_and_grad(x):
    s = _sigmoid(x)
    return x * s, s * (1.0 + x * (1.0 - s))


def _log_sigmoid(x):
    return jnp.minimum(x, 0.0) - jnp.log1p(jnp.exp(-jnp.abs(x)))


def _ln_rows(v, g, b):
    mu = jnp.mean(v, axis=-1, keepdims=True)
    c = v - mu
    var = jnp.mean(c * c, axis=-1, keepdims=True)
    rstd = lax.rsqrt(var + LN_EPS)
    xhat = c * rstd
    return xhat * g + b, xhat, rstd


def _ln_rows_bwd(dy, g, xhat, rstd):
    dxh = dy * g
    m1 = jnp.mean(dxh, axis=-1, keepdims=True)
    m2 = jnp.mean(dxh * xhat, axis=-1, keepdims=True)
    return rstd * (dxh - m1 - xhat * m2)


def _colsum(v):
    return jnp.sum(v, axis=0, keepdims=True)


def _dot(a, b):
    return jnp.dot(a, b, preferred_element_type=f32)


def _dot_nt(a, b):
    return lax.dot_general(a, b, (((1,), (1,)), ((), ())), preferred_element_type=f32)


def _dot_tn(a, b):
    return lax.dot_general(a, b, (((0,), (0,)), ((), ())), preferred_element_type=f32)


def _meta_full(sw_ref):
    return jnp.concatenate([sw_ref[s, 0:N_META, :] for s in range(N_DEV)], axis=1)


def _ln_emb(x, smallw, g_e, b_e):
    seq = x.shape[1]
    rows = seq + BLK
    nb = rows // BLK

    def body(x_ref, sw_ref, g_ref, b_ref, h32_ref, hb_ref, ht_ref):
        i = pl.program_id(0)
        g, b = g_ref[...], b_ref[...]

        def emit(blk):
            h32_ref[...] = blk
            hb_ref[...] = blk.astype(bf16)
            ht_ref[...] = blk.T.astype(bf16)

        @pl.when(i == 0)
        def _():
            hm = _ln_rows(_meta_full(sw_ref), g, b)[0]
            emit(jnp.concatenate([jnp.zeros((ROW0, D), f32), hm], axis=0))

        @pl.when(i > 0)
        def _():
            emit(_ln_rows(x_ref[0], g, b)[0])

    return pl.pallas_call(
        body, grid=(nb,),
        in_specs=[pl.BlockSpec((1, BLK, D), lambda i: (0, jnp.maximum(i - 1, 0), 0)),
                  pl.BlockSpec((N_DEV, 24, 256), lambda i: (0, 0, 0)),
                  pl.BlockSpec((1, D), lambda i: (0, 0)),
                  pl.BlockSpec((1, D), lambda i: (0, 0))],
        out_specs=[pl.BlockSpec((BLK, D), lambda i: (i, 0)),
                   pl.BlockSpec((BLK, D), lambda i: (i, 0)),
                   pl.BlockSpec((D, BLK), lambda i: (0, i))],
        out_shape=[SDS((rows, D), f32), SDS((rows, D), bf16), SDS((D, rows), bf16)],
        name="ln_emb", compiler_params=_cp(("arbitrary",)),
    )(x, smallw, g_e, b_e)


def _ln_emb_bwd(dh, du32, x, smallw, g_e):
    seq = x.shape[1]
    rows = seq + BLK
    nb = rows // BLK

    def body(dh_ref, du_ref, x_ref, sw_ref, g_ref, gx_ref, dmeta_ref, st_ref):
        i = pl.program_id(0)
        g = g_ref[...]
        dht = dh_ref[...] + ALPHA * du_ref[...]

        @pl.when(i == 0)
        def _():
            v = jnp.concatenate([jnp.zeros((ROW0, D), f32), _meta_full(sw_ref)], axis=0)
            valid = lax.broadcasted_iota(jnp.int32, (BLK, 1), 0) >= ROW0
            d = jnp.where(valid, dht, 0.0)
            _, xhat, rstd = _ln_rows(v, g, 0.0)
            dv = _ln_rows_bwd(d, g, xhat, rstd)
            dmeta_ref[...] = dv[ROW0:, :]
            st_ref[...] = jnp.concatenate([_colsum(d * xhat), _colsum(d), jnp.zeros((6, D), f32)], axis=0)

        @pl.when(i > 0)
        def _():
            _, xhat, rstd = _ln_rows(x_ref[0], g, 0.0)
            gx_ref[0] = _ln_rows_bwd(dht, g, xhat, rstd)
            st_ref[0:1, :] += _colsum(dht * xhat)
            st_ref[1:2, :] += _colsum(dht)

    return pl.pallas_call(
        body, grid=(nb,),
        in_specs=[pl.BlockSpec((BLK, D), lambda i: (i, 0)),
                  pl.BlockSpec((BLK, D), lambda i: (i, 0)),
                  pl.BlockSpec((1, BLK, D), lambda i: (0, jnp.maximum(i - 1, 0), 0)),
                  pl.BlockSpec((N_DEV, 24, 256), lambda i: (0, 0, 0)),
                  pl.BlockSpec((1, D), lambda i: (0, 0))],
        out_specs=[pl.BlockSpec((1, BLK, D), lambda i: (0, jnp.maximum(i - 1, 0), 0)),
                   pl.BlockSpec((N_META, D), lambda i: (0, 0)),
                   pl.BlockSpec((8, D), lambda i: (0, 0))],
        out_shape=[SDS((1, seq, D), f32), SDS((N_META, D), f32), SDS((8, D), f32)],
        name="ln_emb_bwd", compiler_params=_cp(("arbitrary",)),
    )(dh, du32, x, smallw, g_e)


def _mm(a, b, *, name, nt=False, sel=None, bias=None, out_dtype=f32, tn=512):
    m, k = a.shape
    cm = _row_chunk(m)
    stacked = sel is not None
    if stacked:
        n = D
        if nt:
            b_spec = pl.BlockSpec((tn // 256, None, 256, D), lambda j: (j, sel, 0, 0))
        else:
            b_spec = pl.BlockSpec((N_DEV, None, 256, tn), lambda j: (0, sel, 0, j))
    elif nt:
        n = b.shape[0]
        b_spec = pl.BlockSpec((tn, k), lambda j: (j, 0))
    else:
        n = b.shape[1]
        b_spec = pl.BlockSpec((k, tn), lambda j: (0, j))
    in_specs = [pl.BlockSpec((m, k), lambda j: (0, 0)), b_spec]
    args = [a, b]
    if bias is not None:
        in_specs.append(pl.BlockSpec((1, tn), lambda j: (0, j)))
        args.append(bias)

    def body(*refs):
        a_ref, b_ref, o_ref = refs[0], refs[1], refs[-1]
        bm = b_ref[...]
        if stacked:
            bm = bm.reshape((tn, D) if nt else (D, tn))
        for c in range(m // cm):
            acc = (_dot_nt if nt else _dot)(a_ref[c * cm:(c + 1) * cm, :], bm)
            if bias is not None:
                acc = acc + refs[2][...]
            o_ref[c * cm:(c + 1) * cm, :] = acc.astype(out_dtype)

    return pl.pallas_call(
        body, grid=(n // tn,), in_specs=in_specs,
        out_specs=pl.BlockSpec((m, tn), lambda j: (0, j)),
        out_shape=SDS((m, n), out_dtype), name=name, compiler_params=_cp(("arbitrary",), 48),
    )(*args)


def _mm_dh(dz, w_full):
    rows = dz.shape[0]
    tk, tn = 1280, 1024
    cm = _row_chunk(rows)

    def body(a_ref, w_ref, o_ref):
        kk = pl.program_id(1)
        for c in range(rows // cm):
            acc = _dot_nt(a_ref[c * cm:(c + 1) * cm, :], w_ref[...])

            @pl.when(kk == 0)
            def _():
                o_ref[c * cm:(c + 1) * cm, :] = acc

            @pl.when(kk > 0)
            def _():
                o_ref[c * cm:(c + 1) * cm, :] += acc

    return pl.pallas_call(
        body, grid=(D // tn, D_IN // tk),
        in_specs=[pl.BlockSpec((rows, tk), lambda j, kk: (0, kk)),
                  pl.BlockSpec((tn, tk), lambda j, kk: (j, kk))],
        out_specs=pl.BlockSpec((rows, tn), lambda j, kk: (0, j)),
        out_shape=SDS((rows, D), f32), name="mm_dh", compiler_params=_cp(("arbitrary", "arbitrary"), 48),
    )(dz, w_full)


def _mm_dwin(h_t, dz):
    rows = dz.shape[0]
    pair = 2 * SHARD_IN
    tm = 512

    def body(a_ref, dz_ref, o_ref, db_ref):
        acc = _dot(a_ref[...], dz_ref[...])
        o_ref[0] = acc[:, :SHARD_IN].astype(bf16)
        o_ref[1] = acc[:, SHARD_IN:].astype(bf16)

        @pl.when(pl.program_id(1) == 0)
        def _():
            def step(i, s):
                blk = dz_ref[pl.ds(pl.multiple_of(i * BLK, BLK), BLK), :].astype(f32)
                return s + blk.reshape(BLK // 8, 8, pair).sum(axis=0)
            s = lax.fori_loop(0, rows // BLK, step, jnp.zeros((8, pair), f32))
            db_ref[...] = jnp.broadcast_to(_colsum(s), (8, pair))

    return pl.pallas_call(
        body, grid=(N_DEV // 2, D // tm),
        in_specs=[pl.BlockSpec((tm, rows), lambda p, i: (i, 0)),
                  pl.BlockSpec((rows, pair), lambda p, i: (0, p))],
        out_specs=[pl.BlockSpec((2, tm, SHARD_IN), lambda p, i: (p, i, 0)),
                   pl.BlockSpec((8, pair), lambda p, i: (0, p))],
        out_shape=[SDS((N_DEV, D, SHARD_IN), bf16), SDS((8, D_IN), f32)],
        name="mm_dwin", compiler_params=_cp(("arbitrary", "arbitrary"), VMEM_LIMIT_MB),
    )(h_t, dz)


def _transpose(x, name):
    rows, cols = x.shape

    def body(x_ref, o_ref):
        o_ref[...] = x_ref[...].astype(f32).T.astype(bf16)

    return pl.pallas_call(
        body, grid=(rows // BLK,),
        in_specs=[pl.BlockSpec((BLK, cols), lambda i: (i, 0))],
        out_specs=pl.BlockSpec((cols, BLK), lambda i: (0, i)),
        out_shape=SDS((cols, rows), bf16), name=name, compiler_params=_cp(("arbitrary",)),
    )(x)


def _relayout_w_in(wg):
    tm = 256

    def body(i_ref, o_ref):
        for d in range(N_DEV):
            o_ref[:, d * SHARD_IN:(d + 1) * SHARD_IN] = i_ref[d]

    return pl.pallas_call(
        body, grid=(D // tm,),
        in_specs=[pl.BlockSpec((N_DEV, tm, SHARD_IN), lambda i: (0, i, 0))],
        out_specs=pl.BlockSpec((tm, D_IN), lambda i: (i, 0)),
        out_shape=SDS((D, D_IN), bf16), name="relayout_w_in", compiler_params=_cp(("arbitrary",)),
    )(wg)


def _scan8(a, b, reverse):
    idx = lax.broadcasted_iota(jnp.int32, a.shape, 0)
    for s in (1, 2, 4):
        sh = 8 - s if reverse else s
        a_sh, b_sh = pltpu.roll(a, sh, 0), pltpu.roll(b, sh, 0)
        m = (idx < 8 - s) if reverse else (idx >= s)
        b = jnp.where(m, a * b_sh + b, b)
        a = jnp.where(m, a * a_sh, a)
    return a, b


def _shift_rows(prev8, cur, k):
    ext = jnp.concatenate([prev8, cur], axis=0)
    return pltpu.roll(ext, k, 0)[8:, :]


def _gates(xc, w_ra, b_ra, w_ri, b_ri, ls):
    xb = xc.astype(bf16)
    r = _sigmoid(_dot(xb, w_ra) + b_ra)
    ig = _sigmoid(_dot(xb, w_ri) + b_ri)
    la = LRU_C * r * ls
    a = jnp.exp(la)
    mult = jnp.sqrt(jnp.tanh(-la) * (1.0 + a * a))
    return xb, r, ig, a, mult


_RNN_IN_SPECS = lambda rows: [
    pl.BlockSpec((1, 24, 256), lambda n: (n, 0, 0)),
    pl.BlockSpec((1, RNN_BLOCK), lambda n: (0, n)),
    pl.BlockSpec((N_DEV, 2, None, 32, RNN_BLOCK), lambda n: (0, 0, n, 0, 0)),
    pl.BlockSpec((1, RNN_BLOCK), lambda n: (0, n)),
    pl.BlockSpec((1, RNN_BLOCK), lambda n: (0, n)),
    pl.BlockSpec((1, RNN_BLOCK), lambda n: (0, n)),
]


def _rnn_fwd(z, smallw, conv_b, wrg, b_ra, b_ri, lam):
    rows = z.shape[0]
    nb = rows // BLK
    col = lambda off: pl.BlockSpec((rows, RNN_BLOCK), lambda n: (0, off // RNN_BLOCK + n))

    def body(xr_ref, gr_ref, sw_ref, cb_ref, w_ref, bra_ref, bri_ref, lam_ref, xc_ref, hr_ref, ya_ref, a_s):
        cw = sw_ref[0, N_META:24, :]
        cb = cb_ref[...]
        w_ra = w_ref[:, 0].reshape(RNN_BLOCK, RNN_BLOCK)
        w_ri = w_ref[:, 1].reshape(RNN_BLOCK, RNN_BLOCK)
        b_ra_v, b_ri_v = bra_ref[...], bri_ref[...]
        ls = _log_sigmoid(lam_ref[...])
        rid = lax.broadcasted_iota(jnp.int32, (BLK, 1), 0)

        def blk_step(i, carry):
            r0 = pl.multiple_of(i * BLK, BLK)
            grow = rid + r0
            valid = grow >= ROW0
            cur = jnp.where(valid, xr_ref[pl.ds(r0, BLK), :], 0.0)
            prev8 = xr_ref[pl.ds(pl.multiple_of(jnp.maximum(r0 - 8, 0), 8), 8), :] * (i > 0).astype(f32)
            xc = cb + cw[0:1] * cur
            for k in range(1, CONV_WIDTH):
                xc = xc + cw[k:k + 1] * _shift_rows(prev8, cur, k)
            xc_ref[pl.ds(r0, BLK), :] = xc
            _, _, ig, a, mult = _gates(xc, w_ra, b_ra_v, w_ri, b_ri_v, ls)
            mult = jnp.where(grow == ROW0, 1.0, mult)
            a_s[pl.ds(r0, BLK), :] = a
            hr_ref[pl.ds(r0, BLK), :] = jnp.where(valid, mult * ig * xc, 0.0)
            return carry

        lax.fori_loop(0, nb, blk_step, 0)

        def scan_step(j, carry):
            r0 = pl.multiple_of(j * 8, 8)
            a, b = _scan8(a_s[pl.ds(r0, 8), :], hr_ref[pl.ds(r0, 8), :], False)
            h = b + a * carry
            hr_ref[pl.ds(r0, 8), :] = h
            return jnp.broadcast_to(h[7:8, :], (8, RNN_BLOCK))

        lax.fori_loop(0, rows // 8, scan_step, jnp.zeros((8, RNN_BLOCK), f32))

        def gate_step(i, carry):
            r0 = pl.multiple_of(i * BLK, BLK)
            ya_ref[pl.ds(r0, BLK), :] = (hr_ref[pl.ds(r0, BLK), :]
                                         * _silu_and_grad(gr_ref[pl.ds(r0, BLK), :])[0]).astype(bf16)
            return carry

        lax.fori_loop(0, nb, gate_step, 0)

    return pl.pallas_call(
        body, grid=(N_RNN_BLOCKS,),
        in_specs=[col(0), col(OFF_GR)] + _RNN_IN_SPECS(rows),
        out_specs=[pl.BlockSpec((rows, RNN_BLOCK), lambda n: (0, n))] * 3,
        out_shape=[SDS((rows, D), f32), SDS((rows, D), f32), SDS((rows, D), bf16)],
        scratch_shapes=[pltpu.VMEM((rows, RNN_BLOCK), f32)],
        name="rnn_fwd", compiler_params=_cp(("arbitrary",)),
    )(z, z, smallw, conv_b, wrg, b_ra, b_ri, lam)


def _rnn_bwd(dya, hr, xc, z, smallw, conv_b, wrg, b_ra, b_ri, lam):
    rows = z.shape[0]
    nb = rows // BLK
    col = lambda off: pl.BlockSpec((rows, RNN_BLOCK), lambda n: (0, off // RNN_BLOCK + n))
    blk = pl.BlockSpec((rows, RNN_BLOCK), lambda n: (0, n))

    def body(dya_ref, hr_ref, xc_ref, xr_ref, gr_ref, sw_ref, cb_ref, w_ref, bra_ref, bri_ref, lam_ref,
             dxr_ref, dgr_ref, dw_ref, vec_ref, a_s, lam_s, dxc_s, dw_s):
        cw = sw_ref[0, N_META:24, :]
        w_ra = w_ref[:, 0].reshape(RNN_BLOCK, RNN_BLOCK)
        w_ri = w_ref[:, 1].reshape(RNN_BLOCK, RNN_BLOCK)
        b_ra_v, b_ri_v = bra_ref[...], bri_ref[...]
        lam_v = lam_ref[...]
        ls = _log_sigmoid(lam_v)
        rid = lax.broadcasted_iota(jnp.int32, (BLK, 1), 0)
        zrow = jnp.zeros((1, RNN_BLOCK), f32)

        def p1(i, carry):
            r0 = pl.multiple_of(i * BLK, BLK)
            sl = pl.ds(r0, BLK)
            a = _gates(xc_ref[sl, :], w_ra, b_ra_v, w_ri, b_ri_v, ls)[3]
            a_s[sl, :] = a
            sg, dsg = _silu_and_grad(gr_ref[sl, :])
            d = dya_ref[sl, :]
            lam_s[sl, :] = d * sg
            dgr_ref[sl, :] = (d * hr_ref[sl, :] * dsg).astype(bf16)
            return carry

        lax.fori_loop(0, nb, p1, 0)

        def p2(jj, carry):
            j = rows // 8 - 1 - jj
            sl = pl.ds(pl.multiple_of(j * 8, 8), 8)
            a = a_s[sl, :]
            g = lam_s[sl, :]
            ca, cb_ = _scan8(a, a * g, True)
            mu = cb_ + ca * carry
            idx = lax.broadcasted_iota(jnp.int32, a.shape, 0)
            lam_s[sl, :] = g + jnp.where(idx < 7, pltpu.roll(mu, 7, 0), carry)
            return jnp.broadcast_to(mu[0:1, :], (8, RNN_BLOCK))

        lax.fori_loop(0, rows // 8, p2, jnp.zeros((8, RNN_BLOCK), f32))

        dw_s[...] = jnp.zeros_like(dw_s)

        def p3(i, carry):
            d_bra, d_bri, d_ls = carry
            r0 = pl.multiple_of(i * BLK, BLK)
            sl = pl.ds(r0, BLK)
            grow = rid + r0
            valid = grow >= ROW0
            first = grow == ROW0
            xcv = xc_ref[sl, :]
            xb, r, ig, a, mult = _gates(xcv, w_ra, b_ra_v, w_ri, b_ri_v, ls)
            mult = jnp.where(first, 1.0, mult)
            lam_t = lam_s[sl, :]
            du = jnp.where(valid, lam_t, 0.0)
            hprev = _shift_rows(hr_ref[pl.ds(pl.multiple_of(jnp.maximum(r0 - 8, 0), 8), 8), :] * (i > 0).astype(f32), hr_ref[sl, :], 1)
            da = lam_t * hprev
            dmult = jnp.where(first, 0.0, du * ig * xcv)
            di = du * mult * xcv
            dxc = du * mult * ig
            ratio = jnp.where(valid & jnp.logical_not(first), a * a / mult, 0.0)
            dla = da * a - dmult * ratio
            dpr = (dla * (LRU_C * ls)) * r * (1.0 - r)
            dpi = di * ig * (1.0 - ig)
            dprb, dpib = dpr.astype(bf16), dpi.astype(bf16)
            dw_s[0] += _dot_tn(xb, dprb)
            dw_s[1] += _dot_tn(xb, dpib)
            dxc_s[sl, :] = dxc + _dot_nt(dprb, w_ra) + _dot_nt(dpib, w_ri)
            return d_bra + _colsum(dpr), d_bri + _colsum(dpi), d_ls + _colsum(dla * (LRU_C * r))

        d_bra, d_bri, d_ls = lax.fori_loop(0, nb, p3, (zrow, zrow, zrow))

        def p4(i, carry):
            d_cb, d_w0, d_w1, d_w2, d_w3 = carry
            r0 = pl.multiple_of(i * BLK, BLK)
            sl = pl.ds(r0, BLK)
            grow = rid + r0
            valid = grow >= ROW0
            dxc = dxc_s[sl, :]
            nxt = dxc_s[pl.ds(pl.multiple_of(jnp.minimum(r0 + BLK, rows - 8), 8), 8), :] * (i < nb - 1).astype(f32)
            ext = jnp.concatenate([dxc, nxt], axis=0)
            dxr = cw[0:1] * dxc
            for k in range(1, CONV_WIDTH):
                dxr = dxr + cw[k:k + 1] * pltpu.roll(ext, BLK + 8 - k, 0)[:BLK, :]
            dxr_ref[sl, :] = jnp.where(valid, dxr, 0.0).astype(bf16)
            cur = jnp.where(valid, xr_ref[sl, :], 0.0)
            prev8 = xr_ref[pl.ds(pl.multiple_of(jnp.maximum(r0 - 8, 0), 8), 8), :] * (i > 0).astype(f32)
            dws = [d_w0 + _colsum(dxc * cur)]
            for k, acc in ((1, d_w1), (2, d_w2), (3, d_w3)):
                dws.append(acc + _colsum(dxc * _shift_rows(prev8, cur, k)))
            return (d_cb + _colsum(dxc), *dws)

        d_cb, d_w0, d_w1, d_w2, d_w3 = lax.fori_loop(0, nb, p4, (zrow,) * 5)

        d_lam = d_ls * _sigmoid(-lam_v)
        vec_ref[...] = jnp.concatenate([d_bra, d_bri, d_lam, d_cb, d_w0, d_w1, d_w2, d_w3], axis=0)
        dw_ref[:, 0] = dw_s[0].astype(bf16).reshape(N_DEV, 32, RNN_BLOCK)
        dw_ref[:, 1] = dw_s[1].astype(bf16).reshape(N_DEV, 32, RNN_BLOCK)

    return pl.pallas_call(
        body, grid=(N_RNN_BLOCKS,),
        in_specs=[blk, blk, blk, col(0), col(OFF_GR)] + _RNN_IN_SPECS(rows),
        out_specs=[blk, blk,
                   pl.BlockSpec((N_DEV, 2, None, 32, RNN_BLOCK), lambda n: (0, 0, n, 0, 0)),
                   pl.BlockSpec((8, RNN_BLOCK), lambda n: (0, n))],
        out_shape=[SDS((rows, D), bf16), SDS((rows, D), bf16),
                   SDS((N_DEV, 2, N_RNN_BLOCKS, 32, RNN_BLOCK), bf16), SDS((8, D), f32)],
        scratch_shapes=[pltpu.VMEM((rows, RNN_BLOCK), f32), pltpu.VMEM((rows, RNN_BLOCK), f32),
                        pltpu.VMEM((rows, RNN_BLOCK), f32), pltpu.VMEM((2, RNN_BLOCK, RNN_BLOCK), f32)],
        name="rnn_bwd", compiler_params=_cp(("arbitrary",), 48),
    )(dya, hr, xc, z, z, smallw, conv_b, wrg, b_ra, b_ri, lam)


def _rope_tables(rows):
    half = jnp.arange(HALF, dtype=f32)
    inv = ROPE_THETA ** (-half / HALF)
    pos = (jnp.arange(rows) - ROW0).astype(f32)
    ang = pos[:, None] * inv[None, :]
    cos, sin = jnp.cos(ang), jnp.sin(ang)
    cos128 = jnp.concatenate([cos, cos, cos, cos], axis=1)
    sin128 = jnp.concatenate([-sin, sin, -sin, sin], axis=1)
    return cos128, sin128


def _rope128(x, cos128, sin128):
    lane = lax.broadcasted_iota(jnp.int32, x.shape, 1)
    swapped = jnp.where(lane % HEAD_DIM < HALF, pltpu.roll(x, 128 - HALF, 1), pltpu.roll(x, HALF, 1))
    return x * cos128 + swapped * sin128


def _qkv_prep(z, cos128, sin128):
    rows = z.shape[0]

    def body(q_ref, kv_ref, c_ref, s_ref, qo_ref, ko_ref, vo_ref):
        c, s = c_ref[...], s_ref[...]
        for g in range(D // 128):
            qo_ref[:, g * 128:(g + 1) * 128] = (_rope128(q_ref[:, g * 128:(g + 1) * 128], c, s)
                                                * (HEAD_DIM ** -0.5)).astype(bf16)
        for g in range(2):
            kr = _rope128(kv_ref[:, g * 128:(g + 1) * 128], c, s)
            for j in range(2):
                ko_ref[2 * g + j] = kr[:, j * HEAD_DIM:(j + 1) * HEAD_DIM].astype(bf16)
        for h in range(N_KV):
            vo_ref[h] = kv_ref[:, 256 + h * HEAD_DIM:256 + (h + 1) * HEAD_DIM].astype(bf16)

    return pl.pallas_call(
        body, grid=(rows // BLK,),
        in_specs=[pl.BlockSpec((BLK, D), lambda i: (i, OFF_Q // D)),
                  pl.BlockSpec((BLK, 512), lambda i: (i, OFF_K // 512)),
                  pl.BlockSpec((BLK, 128), lambda i: (i, 0)),
                  pl.BlockSpec((BLK, 128), lambda i: (i, 0))],
        out_specs=[pl.BlockSpec((BLK, D), lambda i: (i, 0)),
                   pl.BlockSpec((N_KV, BLK, HEAD_DIM), lambda i: (0, i, 0)),
                   pl.BlockSpec((N_KV, BLK, HEAD_DIM), lambda i: (0, i, 0))],
        out_shape=[SDS((rows, D), bf16), SDS((N_KV, rows, HEAD_DIM), bf16), SDS((N_KV, rows, HEAD_DIM), bf16)],
        name="qkv_prep", compiler_params=_cp(("arbitrary",)),
    )(z, z, cos128, sin128)


def _attn_mask(n):
    qi = n * BLK + lax.broadcasted_iota(jnp.int32, (BLK, 2 * BLK + N_META), 0)
    c = lax.broadcasted_iota(jnp.int32, (BLK, 2 * BLK + N_META), 1)
    jb = (n - 1) * BLK + c
    band = (jb >= BLK) & (jb <= qi) & (qi - jb < BLK)
    meta = (ROW0 + c - 2 * BLK) <= qi
    return ((c < 2 * BLK) & band) | ((c >= 2 * BLK) & meta)


N_KEYS = 2 * BLK + N_META


def _stack_heads(t):
    return jnp.concatenate([t[:, g * HEAD_DIM:(g + 1) * HEAD_DIM] for g in range(GROUP)], axis=0)


def _sink_column(sink_ref, h):
    g = lax.broadcasted_iota(jnp.int32, (GROUP, 1, 1), 0)
    col = jnp.zeros((GROUP, 1, 1), f32)
    for j in range(GROUP):
        col = jnp.where(g == j, sink_ref[h * GROUP + j], col)
    return col


def _kv_specs(last):
    cl = lambda n: jnp.minimum(n, last)
    return [pl.BlockSpec((None, N_META, HEAD_DIM), lambda h, n: (h, ROW0 // N_META, 0)),
            pl.BlockSpec((None, BLK, HEAD_DIM), lambda h, n: (h, jnp.maximum(cl(n) - 1, 0), 0)),
            pl.BlockSpec((None, BLK, HEAD_DIM), lambda h, n: (h, cl(n), 0))]


def _attn_fwd(q_r, k_r, v_b, z, sinks):
    rows = q_r.shape[0]
    nb = rows // BLK

    def body(sink_ref, q_ref, km_ref, kp_ref, kc_ref, vm_ref, vp_ref, vc_ref, ga_ref, o_ref, yb_ref, lse_ref):
        h, n = pl.program_id(0), pl.program_id(1)
        kk = jnp.concatenate([kp_ref[...], kc_ref[...], km_ref[...]], axis=0)
        vv = jnp.concatenate([vp_ref[...], vc_ref[...], vm_ref[...]], axis=0)
        q2 = _stack_heads(q_ref[...])
        s = jnp.where(_attn_mask(n)[None], _dot_nt(q2, kk).reshape(GROUP, BLK, N_KEYS), NEG_INF)
        sink = _sink_column(sink_ref, h)
        m = jnp.maximum(jnp.max(s, axis=-1, keepdims=True), sink)
        p = jnp.exp(s - m)
        den = jnp.sum(p, axis=-1, keepdims=True) + jnp.exp(sink - m)
        o2 = _dot((p / den).astype(bf16).reshape(GROUP * BLK, N_KEYS), vv)
        lse = m + jnp.log(den)
        for g in range(GROUP):
            o_ref[:, g * HEAD_DIM:(g + 1) * HEAD_DIM] = o2[g * BLK:(g + 1) * BLK]
            lse_ref[:, g:g + 1] = lse[g]
        yb_ref[...] = (o_ref[...] * _silu_and_grad(ga_ref[...])[0]).astype(bf16)

    tile = pl.BlockSpec((BLK, 512), lambda h, n: (n, h))
    return pl.pallas_call(
        body, grid=(N_KV, nb),
        in_specs=[pl.BlockSpec(memory_space=pltpu.SMEM), tile] + _kv_specs(nb - 1) + _kv_specs(nb - 1)
                 + [pl.BlockSpec((BLK, 512), lambda h, n: (n, OFF_GA // 512 + h))],
        out_specs=[tile, tile, pl.BlockSpec((None, BLK, GROUP), lambda h, n: (h, n, 0))],
        out_shape=[SDS((rows, D), f32), SDS((rows, D), bf16), SDS((N_KV, rows, GROUP), f32)],
        name="attn_fwd", compiler_params=_cp(("arbitrary", "arbitrary")),
    )(sinks, q_r, k_r, k_r, k_r, v_b, v_b, v_b, z)


def _attn_bwd(dyb, o32, lse, q_r, k_r, v_b, z, sinks):
    rows = q_r.shape[0]
    nb = rows // BLK
    cl = lambda n: jnp.minimum(n, nb - 1)

    def body(sink_ref, dyb_ref, o_ref, lse_ref, q_ref, km_ref, kp_ref, kc_ref, vm_ref, vp_ref, vc_ref, ga_ref,
             dq_ref, dga_ref, dk_ref, dv_ref, dkm_ref, dvm_ref, dsr_ref, ck_s, cv_s):
        h, n = pl.program_id(0), pl.program_id(1)

        @pl.when(n == 0)
        def _():
            dkm_ref[...] = jnp.zeros_like(dkm_ref)
            dvm_ref[...] = jnp.zeros_like(dvm_ref)
            ck_s[...] = jnp.zeros_like(ck_s)
            cv_s[...] = jnp.zeros_like(cv_s)

        @pl.when(n < nb)
        def _():
            kk = jnp.concatenate([kp_ref[...], kc_ref[...], km_ref[...]], axis=0)
            vv = jnp.concatenate([vp_ref[...], vc_ref[...], vm_ref[...]], axis=0)
            sg, dsg = _silu_and_grad(ga_ref[...])
            dyb_v = dyb_ref[...]
            o_v = o_ref[...]
            dga_ref[...] = (dyb_v * o_v * dsg).astype(bf16)
            q2 = _stack_heads(q_ref[...])
            do2 = _stack_heads(dyb_v * sg)
            lse_v = lse_ref[...]
            lse = jnp.concatenate([lse_v[:, g:g + 1] for g in range(GROUP)], axis=0).reshape(GROUP, BLK, 1)
            delta = jnp.sum(do2 * _stack_heads(o_v), axis=-1, keepdims=True).reshape(GROUP, BLK, 1)
            s = jnp.where(_attn_mask(n)[None], _dot_nt(q2, kk).reshape(GROUP, BLK, N_KEYS), NEG_INF)
            p = jnp.exp(s - lse)
            do2b = do2.astype(bf16)
            ds = (p * (_dot_nt(do2b, vv).reshape(GROUP, BLK, N_KEYS) - delta)).astype(bf16)
            ds = ds.reshape(GROUP * BLK, N_KEYS)
            dsr = -jnp.exp(_sink_column(sink_ref, h) - lse) * delta
            dq2 = _dot(ds, kk)
            for g in range(GROUP):
                dq_ref[:, g * HEAD_DIM:(g + 1) * HEAD_DIM] = dq2[g * BLK:(g + 1) * BLK]
                dsr_ref[:, g:g + 1] = dsr[g]
            dkk = _dot_tn(ds, q2)
            dvv = _dot_tn(p.astype(bf16).reshape(GROUP * BLK, N_KEYS), do2b)
            dk_ref[...] = ck_s[...] + dkk[:BLK]
            dv_ref[...] = cv_s[...] + dvv[:BLK]
            ck_s[...] = dkk[BLK:2 * BLK]
            cv_s[...] = dvv[BLK:2 * BLK]
            dkm_ref[...] += dkk[2 * BLK:]
            dvm_ref[...] += dvv[2 * BLK:]

        @pl.when(n == nb)
        def _():
            dk_ref[...] = ck_s[...]
            dv_ref[...] = cv_s[...]

    tile = pl.BlockSpec((BLK, 512), lambda h, n: (cl(n), h))
    kvout = pl.BlockSpec((None, BLK, HEAD_DIM), lambda h, n: (h, jnp.maximum(n - 1, 0), 0))
    mout = pl.BlockSpec((None, N_META, HEAD_DIM), lambda h, n: (h, 0, 0))
    stat = pl.BlockSpec((None, BLK, GROUP), lambda h, n: (h, cl(n), 0))
    return pl.pallas_call(
        body, grid=(N_KV, nb + 1),
        in_specs=[pl.BlockSpec(memory_space=pltpu.SMEM), tile, tile, stat, tile] + _kv_specs(nb - 1)
                 + _kv_specs(nb - 1) + [pl.BlockSpec((BLK, 512), lambda h, n: (cl(n), OFF_GA // 512 + h))],
        out_specs=[tile, tile, kvout, kvout, mout, mout, stat],
        out_shape=[SDS((rows, D), f32), SDS((rows, D), bf16),
                   SDS((N_KV, rows, HEAD_DIM), f32), SDS((N_KV, rows, HEAD_DIM), f32),
                   SDS((N_KV, N_META, HEAD_DIM), f32), SDS((N_KV, N_META, HEAD_DIM), f32),
                   SDS((N_KV, rows, GROUP), f32)],
        scratch_shapes=[pltpu.VMEM((BLK, HEAD_DIM), f32), pltpu.VMEM((BLK, HEAD_DIM), f32)],
        name="attn_bwd", compiler_params=_cp(("arbitrary", "arbitrary")),
    )(sinks, dyb, o32, lse, q_r, k_r, k_r, k_r, v_b, v_b, v_b, z)


def _qkv_finish(dq, dk, dv, dkm, dvm, cos128, sin128):
    rows = dq.shape[0]

    def body(dq_ref, dk_ref, dv_ref, dkm_ref, dvm_ref, c_ref, s_ref, oq_ref, okv_ref):
        first = (pl.program_id(0) == 0).astype(f32)
        c, s = c_ref[...], -s_ref[...]
        for g in range(D // 128):
            oq_ref[:, g * 128:(g + 1) * 128] = (_rope128(dq_ref[:, g * 128:(g + 1) * 128], c, s)
                                                * (HEAD_DIM ** -0.5)).astype(bf16)
        pad = jnp.zeros((ROW0, HEAD_DIM), f32)
        ks = [dk_ref[h] + first * jnp.concatenate([pad, dkm_ref[h]], axis=0) for h in range(N_KV)]
        vs = [dv_ref[h] + first * jnp.concatenate([pad, dvm_ref[h]], axis=0) for h in range(N_KV)]
        for g in range(2):
            kp = jnp.concatenate([ks[2 * g], ks[2 * g + 1]], axis=1)
            okv_ref[:, g * 128:(g + 1) * 128] = _rope128(kp, c, s).astype(bf16)
            okv_ref[:, 256 + g * 128:256 + (g + 1) * 128] = jnp.concatenate([vs[2 * g], vs[2 * g + 1]], axis=1).astype(bf16)

    kv = pl.BlockSpec((N_KV, BLK, HEAD_DIM), lambda i: (0, i, 0))
    mt = pl.BlockSpec((N_KV, N_META, HEAD_DIM), lambda i: (0, 0, 0))
    return pl.pallas_call(
        body, grid=(rows // BLK,),
        in_specs=[pl.BlockSpec((BLK, D), lambda i: (i, 0)), kv, kv, mt, mt,
                  pl.BlockSpec((BLK, 128), lambda i: (i, 0)), pl.BlockSpec((BLK, 128), lambda i: (i, 0))],
        out_specs=[pl.BlockSpec((BLK, D), lambda i: (i, 0)), pl.BlockSpec((BLK, 512), lambda i: (i, 0))],
        out_shape=[SDS((rows, D), bf16), SDS((rows, 512), bf16)],
        name="qkv_finish", compiler_params=_cp(("arbitrary",)),
    )(dq, dk, dv, dkm, dvm, cos128, sin128)


_TW = 512


def _mix_specs(rows):
    tr = _row_chunk(rows)
    tile = pl.BlockSpec((tr, _TW), lambda i, j: (i, j))
    ga = pl.BlockSpec((tr, _TW), lambda i, j: (i, OFF_G // _TW + j))
    gb = pl.BlockSpec((tr, _TW), lambda i, j: (i, (OFF_G + D) // _TW + j))
    return (rows // tr, D // _TW), tile, ga, gb


def _mix_fwd(y_a, y_b, z):
    rows = y_a.shape[0]
    grid, _mix_tile, _mix_ga, _mix_gb = _mix_specs(rows)

    def body(ya_ref, yb_ref, ga_ref, gb_ref, o_ref):
        o_ref[...] = (_sigmoid(ga_ref[...]) * ya_ref[...] + _sigmoid(gb_ref[...]) * yb_ref[...]).astype(bf16)

    return pl.pallas_call(
        body, grid=grid, in_specs=[_mix_tile, _mix_tile, _mix_ga, _mix_gb],
        out_specs=_mix_tile, out_shape=SDS((rows, D), bf16),
        name="mix_fwd", compiler_params=_cp(("arbitrary", "arbitrary")),
    )(y_a, y_b, z, z)


def _mix_bwd(dmixed, y_a, y_b, z):
    rows = y_a.shape[0]
    grid, _mix_tile, _mix_ga, _mix_gb = _mix_specs(rows)

    def body(dm_ref, ya_ref, yb_ref, ga_ref, gb_ref, dya_ref, dyb_ref, dga_ref, dgb_ref):
        dm = dm_ref[...]
        sa, sb = _sigmoid(ga_ref[...]), _sigmoid(gb_ref[...])
        dya_ref[...] = (dm * sa).astype(bf16)
        dyb_ref[...] = (dm * sb).astype(bf16)
        dga_ref[...] = (dm * ya_ref[...] * sa * (1.0 - sa)).astype(bf16)
        dgb_ref[...] = (dm * yb_ref[...] * sb * (1.0 - sb)).astype(bf16)

    return pl.pallas_call(
        body, grid=grid, in_specs=[_mix_tile, _mix_tile, _mix_tile, _mix_ga, _mix_gb],
        out_specs=[_mix_tile] * 4, out_shape=[SDS((rows, D), bf16)] * 4,
        name="mix_bwd", compiler_params=_cp(("arbitrary", "arbitrary")),
    )(dmixed, y_a, y_b, z, z)


def _final_ln(out32, h32, tgt, ln_g, ln_b):
    rows = out32.shape[0]

    def body(o_ref, h_ref, t_ref, g_ref, b_ref, du_ref, dub_ref, st_ref):
        i = pl.program_id(0)
        g = g_ref[...]
        y, xhat, rstd = _ln_rows(ALPHA * h_ref[...] + o_ref[...], g, b_ref[...])
        e = jnp.where(i > 0, y - t_ref[0], 0.0)
        dy = e * (1.0 / D)
        du = _ln_rows_bwd(dy, g, xhat, rstd)
        du_ref[...] = du
        dub_ref[...] = du.astype(bf16)
        st = jnp.concatenate([_colsum(dy * xhat), _colsum(dy), _colsum(du), _colsum(e * e) * (0.5 / D),
                              jnp.zeros((4, D), f32)], axis=0)

        @pl.when(i == 0)
        def _():
            st_ref[...] = st

        @pl.when(i > 0)
        def _():
            st_ref[...] += st

    row = pl.BlockSpec((BLK, D), lambda i: (i, 0))
    vec = pl.BlockSpec((1, D), lambda i: (0, 0))
    return pl.pallas_call(
        body, grid=(rows // BLK,),
        in_specs=[row, row, pl.BlockSpec((1, BLK, D), lambda i: (0, jnp.maximum(i - 1, 0), 0)), vec, vec],
        out_specs=[row, row, pl.BlockSpec((8, D), lambda i: (0, 0))],
        out_shape=[SDS((rows, D), f32), SDS((rows, D), bf16), SDS((8, D), f32)],
        name="final_ln", compiler_params=_cp(("arbitrary",)),
    )(out32, h32, tgt, ln_g, ln_b)


def _assemble_dz(dxr, dgr, dq, dkv, dga, dma, dmb):
    rows = dxr.shape[0]
    parts = [(dxr, D), (dgr, D), (dq, D), (dkv, 512), (dga, D), (dma, D), (dmb, D)]

    def body(*refs):
        o_ref = refs[-1]
        off = 0
        for r, (_, w) in zip(refs[:-1], parts):
            o_ref[:, off:off + w] = r[...]
            off += w

    return pl.pallas_call(
        body, grid=(rows // BLK,),
        in_specs=[pl.BlockSpec((BLK, w), lambda i: (i, 0)) for _, w in parts],
        out_specs=pl.BlockSpec((BLK, D_IN), lambda i: (i, 0)),
        out_shape=SDS((rows, D_IN), bf16), name="assemble_dz", compiler_params=_cp(("arbitrary",)),
    )(*[p for p, _ in parts])


def _local_step(x, tgt, w_full, w3, wrg, smallw, p):
    rows = x.shape[1] + BLK
    cos128, sin128 = _rope_tables(rows)
    sinks = p["sinks"].reshape(N_KV * GROUP)
    h32, hb, h_t = _ln_emb(x, smallw, p["ln_emb_g"], p["ln_emb_b"])
    z = _mm(hb, w_full, bias=p["b_in"], name="mm_z")
    xc, hr, ya = _rnn_fwd(z, smallw, p["conv_b"], wrg, p["b_ra"], p["b_ri"], p["lru_lambda"])
    q_r, k_r, v_b = _qkv_prep(z, cos128, sin128)
    o32, yb, lse = _attn_fwd(q_r, k_r, v_b, z, sinks)
    y_a = _mm(ya, w3, sel=0, name="mm_ya")
    y_b = _mm(yb, w3, sel=1, name="mm_yb")
    mixed = _mix_fwd(y_a, y_b, z)
    out32 = _mm(mixed, w3, sel=2, bias=p["b_o"], name="mm_out")
    du32, dub, st_out = _final_ln(out32, h32, tgt, p["ln_g"], p["ln_b"])

    g_wo = _mm(_transpose(mixed, "t_mixed"), dub, out_dtype=bf16, name="mm_dwo")
    dmixed = _mm(dub, w3, sel=2, nt=True, name="mm_dmixed")
    dya_b, dyb_b, dma, dmb = _mix_bwd(dmixed, y_a, y_b, z)
    g_wrnn = _mm(_transpose(ya, "t_ya"), dya_b, out_dtype=bf16, name="mm_dwrnn")
    g_wattn = _mm(_transpose(yb, "t_yb"), dyb_b, out_dtype=bf16, name="mm_dwattn")
    dya = _mm(dya_b, w3, sel=0, nt=True, name="mm_dya")
    dyb = _mm(dyb_b, w3, sel=1, nt=True, name="mm_dyb")
    dxr, dgr, g_wrg, vec_rnn = _rnn_bwd(dya, hr, xc, z, smallw, p["conv_b"], wrg, p["b_ra"], p["b_ri"],
                                        p["lru_lambda"])
    dq_r, dga, dk, dv, dkm, dvm, dsr = _attn_bwd(dyb, o32, lse, q_r, k_r, v_b, z, sinks)
    dq, dkv = _qkv_finish(dq_r, dk, dv, dkm, dvm, cos128, sin128)
    dz = _assemble_dz(dxr, dgr, dq, dkv, dga, dma, dmb)
    g_win, db_in = _mm_dwin(h_t, dz)
    dh = _mm_dh(dz, w_full)
    grad_x, dmeta, st_emb = _ln_emb_bwd(dh, du32, x, smallw, p["ln_emb_g"])
    return dict(st_out=st_out, st_emb=st_emb, vec_rnn=vec_rnn, dsr=dsr, db_in=db_in, dmeta=dmeta,
                grad_x=grad_x, g_win=g_win, g_wo=g_wo, g_wrnn=g_wrnn, g_wattn=g_wattn, g_wrg=g_wrg)


_ANY = pl.BlockSpec(memory_space=pl.ANY)
_VMEM = pl.BlockSpec(memory_space=pltpu.VMEM)


def _place():
    x, y, c = lax.axis_index("x"), lax.axis_index("y"), lax.axis_index("c")
    return x, y, c


def _dev(px, py, pc):
    return 4 * px + 2 * py + pc


def _cast_w_in(w_in):
    tm = 256

    def body(i_ref, o_ref):
        o_ref[...] = i_ref[0].astype(bf16)

    return pl.pallas_call(
        body, grid=(D // tm,),
        in_specs=[pl.BlockSpec((1, tm, SHARD_IN), lambda i: (0, i, 0))],
        out_specs=pl.BlockSpec((tm, SHARD_IN), lambda i: (i, 0)),
        out_shape=SDS((D, SHARD_IN), bf16), name="cast_w_in", compiler_params=_cp(("arbitrary",)),
    )(w_in)


def _cast_small(w_rnn_out, w_attn_out, w_o, w_ra, w_ri, meta, conv_w):
    def body(a_ref, b_ref, c_ref, ra_ref, ri_ref, m_ref, cw_ref, w3_ref, wrg_ref, sw_ref):
        w3_ref[0] = a_ref[0].astype(bf16)
        w3_ref[1] = b_ref[0].astype(bf16)
        w3_ref[2] = c_ref[0].astype(bf16)
        wrg_ref[0] = ra_ref[0].astype(bf16)
        wrg_ref[1] = ri_ref[0].astype(bf16)
        sw_ref[...] = jnp.concatenate([m_ref[...], cw_ref[0], jnp.zeros((4, 256), f32)], axis=0)

    return pl.pallas_call(
        body,
        out_shape=[SDS((3, 256, D), bf16), SDS((2, N_RNN_BLOCKS, 32, RNN_BLOCK), bf16), SDS((24, 256), f32)],
        name="cast_small", compiler_params=_cp(None),
    )(w_rnn_out, w_attn_out, w_o, w_ra, w_ri, meta, conv_w)


def _all_gather(shards):
    n = len(shards)

    def body(*refs):
        ins, outs = refs[:n], refs[n:2 * n]
        send_sems, recv_sems, local_sems = refs[2 * n:]
        x, y, c = _place()
        me, sibling = (x, y, c), (x, y, 1 - c)
        chips = [(1 - x, y), (x, 1 - y), (1 - x, 1 - y)]

        def copy(a, k, block, to, src=None):
            dst = outs[a].at[_dev(*block)]
            return pltpu.make_async_remote_copy(
                src_ref=dst if src is None else src, dst_ref=dst,
                send_sem=send_sems.at[a * 7 + k], recv_sem=recv_sems.at[a * 7 + k],
                device_id=to, device_id_type=MESH)

        mine = [pltpu.make_async_copy(ins[a], outs[a].at[_dev(*me)], local_sems.at[a]) for a in range(n)]
        for cp in mine:
            cp.start()
        first = []
        for a in range(n):
            first.append(copy(a, 0, me, sibling, src=ins[a]))
            first += [copy(a, 1 + j, me, (*chip, c), src=ins[a]) for j, chip in enumerate(chips)]
        for cp in first:
            cp.start()
        passed = []
        for a in range(n):
            for j, chip in enumerate(chips):
                copy(a, 1 + j, (*chip, c), me).wait_recv()
                cp = copy(a, 4 + j, (*chip, c), sibling)
                cp.start()
                passed.append(cp)
        for a in range(n):
            copy(a, 0, sibling, me).wait_recv()
            for j, chip in enumerate(chips):
                copy(a, 4 + j, (*chip, 1 - c), me).wait_recv()
        for cp in first + passed:
            cp.wait_send()
        for cp in mine:
            cp.wait()

    return pl.pallas_call(
        body, in_specs=[_ANY] * n, out_specs=[_ANY] * n,
        out_shape=[SDS((N_DEV, *s.shape), s.dtype) for s in shards],
        scratch_shapes=[pltpu.SemaphoreType.DMA((7 * n,)), pltpu.SemaphoreType.DMA((7 * n,)),
                        pltpu.SemaphoreType.DMA((n,))],
        name="all_gather_weights",
    )(*shards)


def _exchange_siblings(grads):
    n = len(grads)

    def body(*refs):
        ins, outs = refs[:n], refs[n:2 * n]
        send_sems, recv_sems = refs[2 * n:]
        x, y, c = _place()
        copies = []
        for a in range(n):
            for q in range(4):
                copies.append(pltpu.make_async_remote_copy(
                    src_ref=ins[a].at[2 * q + (1 - c)], dst_ref=outs[a].at[q],
                    send_sem=send_sems.at[4 * a + q], recv_sem=recv_sems.at[4 * a + q],
                    device_id=(x, y, 1 - c), device_id_type=MESH))
        for cp in copies:
            cp.start()
        for cp in copies:
            cp.wait()

    return pl.pallas_call(
        body, in_specs=[_ANY] * n, out_specs=[_ANY] * n,
        out_shape=[SDS((4, *g.shape[1:]), g.dtype) for g in grads],
        scratch_shapes=[pltpu.SemaphoreType.DMA((4 * n,)), pltpu.SemaphoreType.DMA((4 * n,))],
        name="exchange_siblings",
    )(*grads)


def _exchange_chips(parts):
    n = len(parts)

    def body(*refs):
        ins, outs = refs[:n], refs[n:2 * n]
        send_sems, recv_sems = refs[2 * n:]
        x, y, c = _place()
        chips = [(1 - x, y), (x, 1 - y), (1 - x, 1 - y)]
        copies = []
        for a in range(n):
            for j, (qx, qy) in enumerate(chips):
                copies.append(pltpu.make_async_remote_copy(
                    src_ref=ins[a].at[2 * qx + qy], dst_ref=outs[a].at[j],
                    send_sem=send_sems.at[3 * a + j], recv_sem=recv_sems.at[3 * a + j],
                    device_id=(qx, qy, c), device_id_type=MESH))
        for cp in copies:
            cp.start()
        for cp in copies:
            cp.wait()

    return pl.pallas_call(
        body, in_specs=[_ANY] * n, out_specs=[_ANY] * n,
        out_shape=[SDS((3, *p.shape[1:]), p.dtype) for p in parts],
        scratch_shapes=[pltpu.SemaphoreType.DMA((3 * n,)), pltpu.SemaphoreType.DMA((3 * n,))],
        name="exchange_chips",
    )(*parts)


def _pair_sum(g, r1, c_idx, name):
    _, r, w = g.shape
    tr = min(r, 256)

    def body(c_ref, g_ref, r_ref, o_ref):
        o_ref[...] = (g_ref[...].astype(f32) + r_ref[...].astype(f32)).astype(bf16)

    return pl.pallas_call(
        body,
        grid_spec=pltpu.PrefetchScalarGridSpec(
            num_scalar_prefetch=1, grid=(4, r // tr),
            in_specs=[pl.BlockSpec((None, tr, w), lambda q, i, c_ref: (2 * q + c_ref[0], i, 0)),
                      pl.BlockSpec((None, tr, w), lambda q, i, c_ref: (q, i, 0))],
            out_specs=pl.BlockSpec((None, tr, w), lambda q, i, c_ref: (q, i, 0))),
        out_shape=SDS((4, r, w), bf16), name=name, compiler_params=_cp(("arbitrary", "arbitrary")),
    )(c_idx, g, r1)


def _adamw(w, g, m, v):
    m = ADAM_B1 * m + (1.0 - ADAM_B1) * g
    v = ADAM_B2 * v + (1.0 - ADAM_B2) * (g * g)
    m_hat = m / (1.0 - ADAM_B1 ** ADAM_STEP)
    v_hat = v / (1.0 - ADAM_B2 ** ADAM_STEP)
    delta = -ADAM_LR * (m_hat / (jnp.sqrt(v_hat) + ADAM_EPS) + ADAM_WD * w)
    return delta, m, v


def _adamw_big(part, r2, q_idx, w, m, v, name, row_off=0):
    r, wd = w.shape
    tr = min(r, 256)

    def body(q_ref, p_ref, r_ref, w_ref, m_ref, v_ref, g_out, d_out, m_out, v_out):
        g = p_ref[...].astype(f32)
        for j in range(3):
            g = g + r_ref[j].astype(f32)
        d, mn, vn = _adamw(w_ref[...], g, m_ref[...], v_ref[...])
        g_out[...] = g
        d_out[...] = d
        m_out[...] = mn
        v_out[...] = vn

    tile = pl.BlockSpec((tr, wd), lambda i, q_ref: (i, 0))
    return pl.pallas_call(
        body,
        grid_spec=pltpu.PrefetchScalarGridSpec(
            num_scalar_prefetch=1, grid=(r // tr,),
            in_specs=[pl.BlockSpec((None, tr, wd), lambda i, q_ref: (q_ref[0], row_off + i, 0)),
                      pl.BlockSpec((3, tr, wd), lambda i, q_ref: (0, row_off + i, 0)), tile, tile, tile],
            out_specs=[tile] * 4),
        out_shape=[SDS((r, wd), f32)] * 4, name=name, compiler_params=_cp(("arbitrary",), 48),
    )(q_idx, part, r2, w, m, v)


_SMALL_ROWS = 24


def _pack_small(st_emb, vec_rnn, st_out, dsr, db_in, dmeta):
    def body(se_ref, vr_ref, so_ref, dsr_ref, db_ref, dm_ref, sm_ref, sm2_ref):
        sm_ref[...] = jnp.zeros_like(sm_ref)
        sm2_ref[...] = jnp.zeros_like(sm2_ref)
        sm_ref[0:2, :] = se_ref[0:2, :]
        sm_ref[2:3, :] = vr_ref[3:4, :]
        sm_ref[3:6, :] = vr_ref[0:3, :]
        sm_ref[6:7, :] = so_ref[2:3, :]
        sm_ref[7:9, :] = so_ref[0:2, :]
        for h in range(N_KV):
            sm_ref[9:10, h * GROUP:(h + 1) * GROUP] = _colsum(dsr_ref[h])
        for j in range(6):
            sm_ref[16 + j:17 + j, :] = db_ref[0:1, j * D:(j + 1) * D]
        sm_ref[22:23, 0:D_IN - 6 * D] = db_ref[0:1, 6 * D:D_IN]
        for s in range(N_DEV):
            sm2_ref[s, 0:N_META, :] = dm_ref[:, s * 256:(s + 1) * 256]
            sm2_ref[s, N_META:N_META + CONV_WIDTH, :] = vr_ref[4:8, s * 256:(s + 1) * 256]

    return pl.pallas_call(
        body, out_shape=[SDS((_SMALL_ROWS, D), f32), SDS((N_DEV, 24, 256), f32)],
        name="pack_small", compiler_params=_cp(None),
    )(st_emb, vec_rnn, st_out, dsr, db_in, dmeta)


def _small_allreduce(sm, sm2):
    def body(sm_ref, sm2_ref, o_ref, o2_ref, buf, buf2, send_sems, recv_sems):
        x, y, c = _place()
        me = _dev(x, y, c)
        copies = []
        for f in range(1, N_DEV):
            fx, fy, fc = f // 4, (f // 2) % 2, f % 2
            peer = ((x + fx) % 2, (y + fy) % 2, (c + fc) % 2)
            for t, (src, dst) in enumerate(((sm_ref, buf), (sm2_ref, buf2))):
                k = 2 * (f - 1) + t
                copies.append(pltpu.make_async_remote_copy(
                    src_ref=src, dst_ref=dst.at[me], send_sem=send_sems.at[k], recv_sem=recv_sems.at[k],
                    device_id=peer, device_id_type=MESH))
        for cp in copies:
            cp.start()
        buf[me] = sm_ref[...]
        buf2[me] = sm2_ref[...]
        for cp in copies:
            cp.wait()
        acc, acc2 = buf[0], buf2[0]
        for e in range(1, N_DEV):
            acc, acc2 = acc + buf[e], acc2 + buf2[e]
        o_ref[...] = acc
        o2_ref[...] = acc2

    return pl.pallas_call(
        body, in_specs=[_VMEM, _VMEM], out_specs=[_VMEM, _VMEM],
        out_shape=[SDS(sm.shape, f32), SDS(sm2.shape, f32)],
        scratch_shapes=[pltpu.VMEM((N_DEV, *sm.shape), f32), pltpu.VMEM((N_DEV, *sm2.shape), f32),
                        pltpu.SemaphoreType.DMA((14,)), pltpu.SemaphoreType.DMA((14,))],
        name="small_allreduce",
    )(sm, sm2)


_SMALL_ROW_OF = {"ln_emb_g": 0, "ln_emb_b": 1, "conv_b": 2, "b_ra": 3, "b_ri": 4, "lru_lambda": 5, "b_o": 6,
                 "ln_g": 7, "ln_b": 8}
_SMALL_NAMES = ["ln_emb_g", "ln_emb_b", "conv_b", "b_ra", "b_ri", "lru_lambda", "b_o", "ln_g", "ln_b",
                "sinks", "b_in", "meta_tokens", "conv_w"]


def _small_update(sm, sm2_mine, wmv):
    def grad_of(name, sm_ref, s2_ref):
        if name in _SMALL_ROW_OF:
            r = _SMALL_ROW_OF[name]
            return sm_ref[r:r + 1, :]
        if name == "sinks":
            return sm_ref[9:10, 0:N_KV * GROUP]
        if name == "b_in":
            return jnp.concatenate([sm_ref[16 + j:17 + j, :] for j in range(7)], axis=1)[:, :D_IN]
        if name == "meta_tokens":
            return s2_ref[0:N_META, :]
        return s2_ref[N_META:N_META + CONV_WIDTH, :]

    def body(*refs):
        sm_ref, s2_ref = refs[0], refs[1]
        ins = refs[2:2 + 3 * len(_SMALL_NAMES)]
        outs = refs[2 + 3 * len(_SMALL_NAMES):]
        for i, name in enumerate(_SMALL_NAMES):
            w_ref, m_ref, v_ref = ins[3 * i:3 * i + 3]
            g = grad_of(name, sm_ref, s2_ref)
            d, mn, vn = _adamw(w_ref[...], g, m_ref[...], v_ref[...])
            outs[4 * i][...] = g
            outs[4 * i + 1][...] = d
            outs[4 * i + 2][...] = mn
            outs[4 * i + 3][...] = vn

    args, out_shape = [sm, sm2_mine], []
    for name in _SMALL_NAMES:
        args += list(wmv[name])
        out_shape += [SDS(wmv[name][0].shape, f32)] * 4
    res = pl.pallas_call(body, out_shape=out_shape, name="small_update", compiler_params=_cp(None))(*args)
    return {name: tuple(res[4 * i:4 * i + 4]) for i, name in enumerate(_SMALL_NAMES)}


_WEIGHTS = ["meta_tokens", "ln_emb_g", "ln_emb_b", "w_in", "b_in", "conv_w", "conv_b", "w_ra", "b_ra", "w_ri",
            "b_ri", "lru_lambda", "sinks", "w_rnn_out", "w_attn_out", "w_o", "b_o", "ln_g", "ln_b"]
_SMALL_2D = {"meta_tokens": (N_META, 256), "conv_w": (CONV_WIDTH, 256), "b_in": (1, D_IN), "sinks": (1, N_KV * GROUP)}


def kernel(x, meta_tokens, ln_emb_g, ln_emb_b, w_in, b_in, conv_w, conv_b, w_ra, b_ra, w_ri, b_ri, lru_lambda, sinks, w_rnn_out, w_attn_out, w_o, b_o, ln_g, ln_b, loss_target, m_meta_tokens, m_ln_emb_g, m_ln_emb_b, m_w_in, m_b_in, m_conv_w, m_conv_b, m_w_ra, m_b_ra, m_w_ri, m_b_ri, m_lru_lambda, m_sinks, m_w_rnn_out, m_w_attn_out, m_w_o, m_b_o, m_ln_g, m_ln_b, v_meta_tokens, v_ln_emb_g, v_ln_emb_b, v_w_in, v_b_in, v_conv_w, v_conv_b, v_w_ra, v_b_ra, v_w_ri, v_b_ri, v_lru_lambda, v_sinks, v_w_rnn_out, v_w_attn_out, v_w_o, v_b_o, v_ln_g, v_ln_b):
    w = dict(meta_tokens=meta_tokens, ln_emb_g=ln_emb_g, ln_emb_b=ln_emb_b, w_in=w_in, b_in=b_in, conv_w=conv_w,
             conv_b=conv_b, w_ra=w_ra, b_ra=b_ra, w_ri=w_ri, b_ri=b_ri, lru_lambda=lru_lambda, sinks=sinks,
             w_rnn_out=w_rnn_out, w_attn_out=w_attn_out, w_o=w_o, b_o=b_o, ln_g=ln_g, ln_b=ln_b)
    m = dict(meta_tokens=m_meta_tokens, ln_emb_g=m_ln_emb_g, ln_emb_b=m_ln_emb_b, w_in=m_w_in, b_in=m_b_in,
             conv_w=m_conv_w, conv_b=m_conv_b, w_ra=m_w_ra, b_ra=m_b_ra, w_ri=m_w_ri, b_ri=m_b_ri,
             lru_lambda=m_lru_lambda, sinks=m_sinks, w_rnn_out=m_w_rnn_out, w_attn_out=m_w_attn_out, w_o=m_w_o,
             b_o=m_b_o, ln_g=m_ln_g, ln_b=m_ln_b)
    v = dict(meta_tokens=v_meta_tokens, ln_emb_g=v_ln_emb_g, ln_emb_b=v_ln_emb_b, w_in=v_w_in, b_in=v_b_in,
             conv_w=v_conv_w, conv_b=v_conv_b, w_ra=v_w_ra, b_ra=v_b_ra, w_ri=v_w_ri, b_ri=v_b_ri,
             lru_lambda=v_lru_lambda, sinks=v_sinks, w_rnn_out=v_w_rnn_out, w_attn_out=v_w_attn_out, w_o=v_w_o,
             b_o=v_b_o, ln_g=v_ln_g, ln_b=v_ln_b)
    px, py, pc = _place()
    c_idx = jnp.reshape(pc, (1,)).astype(jnp.int32)
    q_idx = jnp.reshape(2 * px + py, (1,)).astype(jnp.int32)

    w3_s, wrg_s, small_s = _cast_small(w_rnn_out, w_attn_out, w_o, w_ra, w_ri, meta_tokens, conv_w)
    wg, w3, wrg, smallw = _all_gather([_cast_w_in(w_in), w3_s, wrg_s, small_s])
    w_full = _relayout_w_in(wg)

    vec = lambda name: w[name].reshape(1, -1)
    p = {k: vec(k) for k in ("ln_emb_g", "ln_emb_b", "b_in", "conv_b", "b_ra", "b_ri", "lru_lambda", "sinks",
                             "b_o", "ln_g", "ln_b")}
    loc = _local_step(x, loss_target, w_full, w3, wrg, smallw, p)
    loss = lax.psum(jnp.sum(loc["st_out"][3]), ("x", "y", "c"))

    grads = [loc["g_win"], loc["g_wo"].reshape(N_DEV, 256, D), loc["g_wrnn"].reshape(N_DEV, 256, D),
             loc["g_wattn"].reshape(N_DEV, 256, D), loc["g_wrg"].reshape(N_DEV, 2 * RNN_BLOCK, RNN_BLOCK)]
    r1 = _exchange_siblings(grads)
    parts = [_pair_sum(g, r, c_idx, "pair_sum_%d" % i) for i, (g, r) in enumerate(zip(grads, r1))]
    r2 = _exchange_chips(parts)
    big = {}
    for i, name in enumerate(("w_in", "w_o", "w_rnn_out", "w_attn_out")):
        shp = w[name].shape
        two_d = (shp[-2], shp[-1])
        res = _adamw_big(parts[i], r2[i], q_idx, w[name].reshape(two_d), m[name].reshape(two_d),
                         v[name].reshape(two_d), "adamw_" + name)
        big[name] = tuple(t.reshape(shp) for t in res)
    for t, name in enumerate(("w_ra", "w_ri")):
        shp = w[name].shape
        two_d = (RNN_BLOCK, RNN_BLOCK)
        res = _adamw_big(parts[4], r2[4], q_idx, w[name].reshape(two_d), m[name].reshape(two_d),
                         v[name].reshape(two_d), "adamw_" + name, row_off=t)
        big[name] = tuple(t_.reshape(shp) for t_ in res)

    sm, sm2 = _pack_small(loc["st_emb"], loc["vec_rnn"], loc["st_out"], loc["dsr"], loc["db_in"], loc["dmeta"])
    sm, sm2 = _small_allreduce(sm, sm2)
    sm2_mine = lax.dynamic_index_in_dim(sm2, _dev(px, py, pc), 0, keepdims=False)
    two = lambda name, t: t.reshape(_SMALL_2D.get(name, (1, D)))
    small = _small_update(sm, sm2_mine, {k: (two(k, w[k]), two(k, m[k]), two(k, v[k])) for k in _SMALL_NAMES})
    res = dict(big)
    for k in _SMALL_NAMES:
        res[k] = tuple(t.reshape(w[k].shape) for t in small[k])

    outs = [loss, loc["grad_x"]]
    for j in range(4):
        outs += [res[k][j] for k in _WEIGHTS]
    return tuple(outs)
```

```python
import functools

import jax
import jax.numpy as jnp
from jax import lax
from jax.experimental import pallas as pl
from jax.experimental.pallas import tpu as pltpu

f32, bf16 = jnp.float32, jnp.bfloat16
SDS = jax.ShapeDtypeStruct

N_DEV = 8
D = 2048
N_META = 16
BLK = 128
ROW0 = BLK - N_META
N_RNN_BLOCKS = 8
RNN_BLOCK = D // N_RNN_BLOCKS
CONV_WIDTH = 4
LRU_C = 8.0
HEAD_DIM = 64
N_KV = 4
GROUP = 8
HALF = HEAD_DIM // 2
ROPE_THETA = 10000.0
NEG_INF = -1e30
LN_EPS = 1e-5
ALPHA = 2.0 ** 0.25
D_IN = 12800
SHARD_IN = D_IN // N_DEV
OFF_GR, OFF_Q, OFF_K, OFF_V, OFF_GA, OFF_G = 2048, 4096, 6144, 6400, 6656, 8704
ADAM_LR, ADAM_B1, ADAM_B2, ADAM_EPS, ADAM_WD, ADAM_STEP = 1e-3, 0.9, 0.999, 1e-8, 0.01, 10
VMEM_LIMIT_MB = 56
MESH = pl.DeviceIdType.MESH


def _cp(sem=None, vmem_mb=40):
    return pltpu.CompilerParams(dimension_semantics=sem, vmem_limit_bytes=vmem_mb * 2 ** 20)


def _row_chunk(m):
    best = 16
    for c in range(16, 641, 16):
        if m % c == 0:
            best = c
    return best


def _sigmoid(x):
    return 1.0 / (1.0 + jnp.exp(-x))


def _silu_and_grad(x):
    s = _sigmoid(x)
    return x * s, s * (1.0 + x * (1.0 - s))


def _log_sigmoid(x):
    return jnp.minimum(x, 0.0) - jnp.log1p(jnp.exp(-jnp.abs(x)))


def _ln_rows(v, g, b):
    mu = jnp.mean(v, axis=-1, keepdims=True)
    c = v - mu
    var = jnp.mean(c * c, axis=-1, keepdims=True)
    rstd = lax.rsqrt(var + LN_EPS)
    xhat = c * rstd
    return xhat * g + b, xhat, rstd


def _ln_rows_bwd(dy, g, xhat, rstd):
    dxh = dy * g
    m1 = jnp.mean(dxh, axis=-1, keepdims=True)
    m2 = jnp.mean(dxh * xhat, axis=-1, keepdims=True)
    return rstd * (dxh - m1 - xhat * m2)


def _colsum(v):
    return jnp.sum(v, axis=0, keepdims=True)


def _dot(a, b):
    return jnp.dot(a, b, preferred_element_type=f32)


def _dot_nt(a, b):
    return lax.dot_general(a, b, (((1,), (1,)), ((), ())), preferred_element_type=f32)


def _dot_tn(a, b):
    return lax.dot_general(a, b, (((0,), (0,)), ((), ())), preferred_element_type=f32)


def _meta_full(sw_ref):
    return jnp.concatenate([sw_ref[s, 0:N_META, :] for s in range(N_DEV)], axis=1)


def _ln_emb(x, smallw, g_e, b_e):
    seq = x.shape[1]
    rows = seq + BLK
    nb = rows // BLK

    def body(x_ref, sw_ref, g_ref, b_ref, h32_ref, hb_ref, ht_ref):
        i = pl.program_id(0)
        g, b = g_ref[...], b_ref[...]

        def emit(blk):
            h32_ref[...] = blk
            hb_ref[...] = blk.astype(bf16)
            ht_ref[...] = blk.T.astype(bf16)

        @pl.when(i == 0)
        def _():
            hm = _ln_rows(_meta_full(sw_ref), g, b)[0]
            emit(jnp.concatenate([jnp.zeros((ROW0, D), f32), hm], axis=0))

        @pl.when(i > 0)
        def _():
            emit(_ln_rows(x_ref[0], g, b)[0])

    return pl.pallas_call(
        body, grid=(nb,),
        in_specs=[pl.BlockSpec((1, BLK, D), lambda i: (0, jnp.maximum(i - 1, 0), 0)),
                  pl.BlockSpec((N_DEV, 24, 256), lambda i: (0, 0, 0)),
                  pl.BlockSpec((1, D), lambda i: (0, 0)),
                  pl.BlockSpec((1, D), lambda i: (0, 0))],
        out_specs=[pl.BlockSpec((BLK, D), lambda i: (i, 0)),
                   pl.BlockSpec((BLK, D), lambda i: (i, 0)),
                   pl.BlockSpec((D, BLK), lambda i: (0, i))],
        out_shape=[SDS((rows, D), f32), SDS((rows, D), bf16), SDS((D, rows), bf16)],
        name="ln_emb", compiler_params=_cp(("arbitrary",)),
    )(x, smallw, g_e, b_e)


def _ln_emb_bwd(dh, du32, x, smallw, g_e):
    seq = x.shape[1]
    rows = seq + BLK
    nb = rows // BLK

    def body(dh_ref, du_ref, x_ref, sw_ref, g_ref, gx_ref, dmeta_ref, st_ref):
        i = pl.program_id(0)
        g = g_ref[...]
        dht = dh_ref[...] + ALPHA * du_ref[...]

        @pl.when(i == 0)
        def _():
            v = jnp.concatenate([jnp.zeros((ROW0, D), f32), _meta_full(sw_ref)], axis=0)
            valid = lax.broadcasted_iota(jnp.int32, (BLK, 1), 0) >= ROW0
            d = jnp.where(valid, dht, 0.0)
            _, xhat, rstd = _ln_rows(v, g, 0.0)
            dv = _ln_rows_bwd(d, g, xhat, rstd)
            dmeta_ref[...] = dv[ROW0:, :]
            st_ref[...] = jnp.concatenate([_colsum(d * xhat), _colsum(d), jnp.zeros((6, D), f32)], axis=0)

        @pl.when(i > 0)
        def _():
            _, xhat, rstd = _ln_rows(x_ref[0], g, 0.0)
            gx_ref[0] = _ln_rows_bwd(dht, g, xhat, rstd)
            st_ref[0:1, :] += _colsum(dht * xhat)
            st_ref[1:2, :] += _colsum(dht)

    return pl.pallas_call(
        body, grid=(nb,),
        in_specs=[pl.BlockSpec((BLK, D), lambda i: (i, 0)),
                  pl.BlockSpec((BLK, D), lambda i: (i, 0)),
                  pl.BlockSpec((1, BLK, D), lambda i: (0, jnp.maximum(i - 1, 0), 0)),
                  pl.BlockSpec((N_DEV, 24, 256), lambda i: (0, 0, 0)),
                  pl.BlockSpec((1, D), lambda i: (0, 0))],
        out_specs=[pl.BlockSpec((1, BLK, D), lambda i: (0, jnp.maximum(i - 1, 0), 0)),
                   pl.BlockSpec((N_META, D), lambda i: (0, 0)),
                   pl.BlockSpec((8, D), lambda i: (0, 0))],
        out_shape=[SDS((1, seq, D), f32), SDS((N_META, D), f32), SDS((8, D), f32)],
        name="ln_emb_bwd", compiler_params=_cp(("arbitrary",)),
    )(dh, du32, x, smallw, g_e)


def _mm(a, b, *, name, nt=False, sel=None, bias=None, out_dtype=f32, tn=512):
    m, k = a.shape
    cm = _row_chunk(m)
    stacked = sel is not None
    if stacked:
        n = D
        if nt:
            b_spec = pl.BlockSpec((tn // 256, None, 256, D), lambda j: (j, sel, 0, 0))
        else:
            b_spec = pl.BlockSpec((N_DEV, None, 256, tn), lambda j: (0, sel, 0, j))
    elif nt:
        n = b.shape[0]
        b_spec = pl.BlockSpec((tn, k), lambda j: (j, 0))
    else:
        n = b.shape[1]
        b_spec = pl.BlockSpec((k, tn), lambda j: (0, j))
    in_specs = [pl.BlockSpec((m, k), lambda j: (0, 0)), b_spec]
    args = [a, b]
    if bias is not None:
        in_specs.append(pl.BlockSpec((1, tn), lambda j: (0, j)))
        args.append(bias)

    def body(*refs):
        a_ref, b_ref, o_ref = refs[0], refs[1], refs[-1]
        bm = b_ref[...]
        if stacked:
            bm = bm.reshape((tn, D) if nt else (D, tn))
        for c in range(m // cm):
            acc = (_dot_nt if nt else _dot)(a_ref[c * cm:(c + 1) * cm, :], bm)
            if bias is not None:
                acc = acc + refs[2][...]
            o_ref[c * cm:(c + 1) * cm, :] = acc.astype(out_dtype)

    return pl.pallas_call(
        body, grid=(n // tn,), in_specs=in_specs,
        out_specs=pl.BlockSpec((m, tn), lambda j: (0, j)),
        out_shape=SDS((m, n), out_dtype), name=name, compiler_params=_cp(("arbitrary",), 48),
    )(*args)


def _mm_dh(dz, w_full):
    rows = dz.shape[0]
    tk, tn = 1280, 1024
    cm = _row_chunk(rows)

    def body(a_ref, w_ref, o_ref):
        kk = pl.program_id(1)
        for c in range(rows // cm):
            acc = _dot_nt(a_ref[c * cm:(c + 1) * cm, :], w_ref[...])

            @pl.when(kk == 0)
            def _():
                o_ref[c * cm:(c + 1) * cm, :] = acc

            @pl.when(kk > 0)
            def _():
                o_ref[c * cm:(c + 1) * cm, :] += acc

    return pl.pallas_call(
        body, grid=(D // tn, D_IN // tk),
        in_specs=[pl.BlockSpec((rows, tk), lambda j, kk: (0, kk)),
                  pl.BlockSpec((tn, tk), lambda j, kk: (j, kk))],
        out_specs=pl.BlockSpec((rows, tn), lambda j, kk: (0, j)),
        out_shape=SDS((rows, D), f32), name="mm_dh", compiler_params=_cp(("arbitrary", "arbitrary"), 48),
    )(dz, w_full)


def _mm_dwin(h_t, dz):
    rows = dz.shape[0]
    pair = 2 * SHARD_IN
    tm = 512

    def body(a_ref, dz_ref, o_ref, db_ref):
        acc = _dot(a_ref[...], dz_ref[...])
        o_ref[0] = acc[:, :SHARD_IN].astype(bf16)
        o_ref[1] = acc[:, SHARD_IN:].astype(bf16)

        @pl.when(pl.program_id(1) == 0)
        def _():
            def step(i, s):
                blk = dz_ref[pl.ds(pl.multiple_of(i * BLK, BLK), BLK), :].astype(f32)
                return s + blk.reshape(BLK // 8, 8, pair).sum(axis=0)
            s = lax.fori_loop(0, rows // BLK, step, jnp.zeros((8, pair), f32))
            db_ref[...] = jnp.broadcast_to(_colsum(s), (8, pair))

    return pl.pallas_call(
        body, grid=(N_DEV // 2, D // tm),
        in_specs=[pl.BlockSpec((tm, rows), lambda p, i: (i, 0)),
                  pl.BlockSpec((rows, pair), lambda p, i: (0, p))],
        out_specs=[pl.BlockSpec((2, tm, SHARD_IN), lambda p, i: (p, i, 0)),
                   pl.BlockSpec((8, pair), lambda p, i: (0, p))],
        out_shape=[SDS((N_DEV, D, SHARD_IN), bf16), SDS((8, D_IN), f32)],
        name="mm_dwin", compiler_params=_cp(("arbitrary", "arbitrary"), VMEM_LIMIT_MB),
    )(h_t, dz)


def _transpose(x, name):
    rows, cols = x.shape

    def body(x_ref, o_ref):
        o_ref[...] = x_ref[...].astype(f32).T.astype(bf16)

    return pl.pallas_call(
        body, grid=(rows // BLK,),
        in_specs=[pl.BlockSpec((BLK, cols), lambda i: (i, 0))],
        out_specs=pl.BlockSpec((cols, BLK), lambda i: (0, i)),
        out_shape=SDS((cols, rows), bf16), name=name, compiler_params=_cp(("arbitrary",)),
    )(x)


def _relayout_w_in(wg):
    tm = 256

    def body(i_ref, o_ref):
        for d in range(N_DEV):
            o_ref[:, d * SHARD_IN:(d + 1) * SHARD_IN] = i_ref[d]

    return pl.pallas_call(
        body, grid=(D // tm,),
        in_specs=[pl.BlockSpec((N_DEV, tm, SHARD_IN), lambda i: (0, i, 0))],
        out_specs=pl.BlockSpec((tm, D_IN), lambda i: (i, 0)),
        out_shape=SDS((D, D_IN), bf16), name="relayout_w_in", compiler_params=_cp(("arbitrary",)),
    )(wg)


def _scan8(a, b, reverse):
    idx = lax.broadcasted_iota(jnp.int32, a.shape, 0)
    for s in (1, 2, 4):
        sh = 8 - s if reverse else s
        a_sh, b_sh = pltpu.roll(a, sh, 0), pltpu.roll(b, sh, 0)
        m = (idx < 8 - s) if reverse else (idx >= s)
        b = jnp.where(m, a * b_sh + b, b)
        a = jnp.where(m, a * a_sh, a)
    return a, b


def _shift_rows(prev8, cur, k):
    ext = jnp.concatenate([prev8, cur], axis=0)
    return pltpu.roll(ext, k, 0)[8:, :]


def _gates(xc, w_ra, b_ra, w_ri, b_ri, ls):
    xb = xc.astype(bf16)
    r = _sigmoid(_dot(xb, w_ra) + b_ra)
    ig = _sigmoid(_dot(xb, w_ri) + b_ri)
    la = LRU_C * r * ls
    a = jnp.exp(la)
    mult = jnp.sqrt(jnp.tanh(-la) * (1.0 + a * a))
    return xb, r, ig, a, mult


_RNN_IN_SPECS = lambda rows: [
    pl.BlockSpec((1, 24, 256), lambda n: (n, 0, 0)),
    pl.BlockSpec((1, RNN_BLOCK), lambda n: (0, n)),
    pl.BlockSpec((N_DEV, 2, None, 32, RNN_BLOCK), lambda n: (0, 0, n, 0, 0)),
    pl.BlockSpec((1, RNN_BLOCK), lambda n: (0, n)),
    pl.BlockSpec((1, RNN_BLOCK), lambda n: (0, n)),
    pl.BlockSpec((1, RNN_BLOCK), lambda n: (0, n)),
]


def _rnn_fwd(z, smallw, conv_b, wrg, b_ra, b_ri, lam):
    rows = z.shape[0]
    nb = rows // BLK
    col = lambda off: pl.BlockSpec((rows, RNN_BLOCK), lambda n: (0, off // RNN_BLOCK + n))

    def body(xr_ref, gr_ref, sw_ref, cb_ref, w_ref, bra_ref, bri_ref, lam_ref, xc_ref, hr_ref, ya_ref, a_s):
        cw = sw_ref[0, N_META:24, :]
        cb = cb_ref[...]
        w_ra = w_ref[:, 0].reshape(RNN_BLOCK, RNN_BLOCK)
        w_ri = w_ref[:, 1].reshape(RNN_BLOCK, RNN_BLOCK)
        b_ra_v, b_ri_v = bra_ref[...], bri_ref[...]
        ls = _log_sigmoid(lam_ref[...])
        rid = lax.broadcasted_iota(jnp.int32, (BLK, 1), 0)

        def blk_step(i, carry):
            r0 = pl.multiple_of(i * BLK, BLK)
            grow = rid + r0
            valid = grow >= ROW0
            cur = jnp.where(valid, xr_ref[pl.ds(r0, BLK), :], 0.0)
            prev8 = xr_ref[pl.ds(pl.multiple_of(jnp.maximum(r0 - 8, 0), 8), 8), :] * (i > 0).astype(f32)
            xc = cb + cw[0:1] * cur
            for k in range(1, CONV_WIDTH):
                xc = xc + cw[k:k + 1] * _shift_rows(prev8, cur, k)
            xc_ref[pl.ds(r0, BLK), :] = xc
            _, _, ig, a, mult = _gates(xc, w_ra, b_ra_v, w_ri, b_ri_v, ls)
            mult = jnp.where(grow == ROW0, 1.0, mult)
            a_s[pl.ds(r0, BLK), :] = a
            hr_ref[pl.ds(r0, BLK), :] = jnp.where(valid, mult * ig * xc, 0.0)
            return carry

        lax.fori_loop(0, nb, blk_step, 0)

        def scan_step(j, carry):
            r0 = pl.multiple_of(j * 8, 8)
            a, b = _scan8(a_s[pl.ds(r0, 8), :], hr_ref[pl.ds(r0, 8), :], False)
            h = b + a * carry
            hr_ref[pl.ds(r0, 8), :] = h
            return jnp.broadcast_to(h[7:8, :], (8, RNN_BLOCK))

        lax.fori_loop(0, rows // 8, scan_step, jnp.zeros((8, RNN_BLOCK), f32))

        def gate_step(i, carry):
            r0 = pl.multiple_of(i * BLK, BLK)
            ya_ref[pl.ds(r0, BLK), :] = (hr_ref[pl.ds(r0, BLK), :]
                                         * _silu_and_grad(gr_ref[pl.ds(r0, BLK), :])[0]).astype(bf16)
            return carry

        lax.fori_loop(0, nb, gate_step, 0)

    return pl.pallas_call(
        body, grid=(N_RNN_BLOCKS,),
        in_specs=[col(0), col(OFF_GR)] + _RNN_IN_SPECS(rows),
        out_specs=[pl.BlockSpec((rows, RNN_BLOCK), lambda n: (0, n))] * 3,
        out_shape=[SDS((rows, D), f32), SDS((rows, D), f32), SDS((rows, D), bf16)],
        scratch_shapes=[pltpu.VMEM((rows, RNN_BLOCK), f32)],
        name="rnn_fwd", compiler_params=_cp(("arbitrary",)),
    )(z, z, smallw, conv_b, wrg, b_ra, b_ri, lam)


def _rnn_bwd(dya, hr, xc, z, smallw, conv_b, wrg, b_ra, b_ri, lam):
    rows = z.shape[0]
    nb = rows // BLK
    col = lambda off: pl.BlockSpec((rows, RNN_BLOCK), lambda n: (0, off // RNN_BLOCK + n))
    blk = pl.BlockSpec((rows, RNN_BLOCK), lambda n: (0, n))

    def body(dya_ref, hr_ref, xc_ref, xr_ref, gr_ref, sw_ref, cb_ref, w_ref, bra_ref, bri_ref, lam_ref,
             dxr_ref, dgr_ref, dw_ref, vec_ref, a_s, lam_s, dxc_s, dw_s):
        cw = sw_ref[0, N_META:24, :]
        w_ra = w_ref[:, 0].reshape(RNN_BLOCK, RNN_BLOCK)
        w_ri = w_ref[:, 1].reshape(RNN_BLOCK, RNN_BLOCK)
        b_ra_v, b_ri_v = bra_ref[...], bri_ref[...]
        lam_v = lam_ref[...]
        ls = _log_sigmoid(lam_v)
        rid = lax.broadcasted_iota(jnp.int32, (BLK, 1), 0)
        zrow = jnp.zeros((1, RNN_BLOCK), f32)

        def p1(i, carry):
            r0 = pl.multiple_of(i * BLK, BLK)
            sl = pl.ds(r0, BLK)
            a = _gates(xc_ref[sl, :], w_ra, b_ra_v, w_ri, b_ri_v, ls)[3]
            a_s[sl, :] = a
            sg, dsg = _silu_and_grad(gr_ref[sl, :])
            d = dya_ref[sl, :]
            lam_s[sl, :] = d * sg
            dgr_ref[sl, :] = (d * hr_ref[sl, :] * dsg).astype(bf16)
            return carry

        lax.fori_loop(0, nb, p1, 0)

        def p2(jj, carry):
            j = rows // 8 - 1 - jj
            sl = pl.ds(pl.multiple_of(j * 8, 8), 8)
            a = a_s[sl, :]
            g = lam_s[sl, :]
            ca, cb_ = _scan8(a, a * g, True)
            mu = cb_ + ca * carry
            idx = lax.broadcasted_iota(jnp.int32, a.shape, 0)
            lam_s[sl, :] = g + jnp.where(idx < 7, pltpu.roll(mu, 7, 0), carry)
            return jnp.broadcast_to(mu[0:1, :], (8, RNN_BLOCK))

        lax.fori_loop(0, rows // 8, p2, jnp.zeros((8, RNN_BLOCK), f32))

        dw_s[...] = jnp.zeros_like(dw_s)

        def p3(i, carry):
            d_bra, d_bri, d_ls = carry
            r0 = pl.multiple_of(i * BLK, BLK)
            sl = pl.ds(r0, BLK)
            grow = rid + r0
            valid = grow >= ROW0
            first = grow == ROW0
            xcv = xc_ref[sl, :]
            xb, r, ig, a, mult = _gates(xcv, w_ra, b_ra_v, w_ri, b_ri_v, ls)
            mult = jnp.where(first, 1.0, mult)
            lam_t = lam_s[sl, :]
            du = jnp.where(valid, lam_t, 0.0)
            hprev = _shift_rows(hr_ref[pl.ds(pl.multiple_of(jnp.maximum(r0 - 8, 0), 8), 8), :] * (i > 0).astype(f32), hr_ref[sl, :], 1)
            da = lam_t * hprev
            dmult = jnp.where(first, 0.0, du * ig * xcv)
            di = du * mult * xcv
            dxc = du * mult * ig
            ratio = jnp.where(valid & jnp.logical_not(first), a * a / mult, 0.0)
            dla = da * a - dmult * ratio
            dpr = (dla * (LRU_C * ls)) * r * (1.0 - r)
            dpi = di * ig * (1.0 - ig)
            dprb, dpib = dpr.astype(bf16), dpi.astype(bf16)
            dw_s[0] += _dot_tn(xb, dprb)
            dw_s[1] += _dot_tn(xb, dpib)
            dxc_s[sl, :] = dxc + _dot_nt(dprb, w_ra) + _dot_nt(dpib, w_ri)
            return d_bra + _colsum(dpr), d_bri + _colsum(dpi), d_ls + _colsum(dla * (LRU_C * r))

        d_bra, d_bri, d_ls = lax.fori_loop(0, nb, p3, (zrow, zrow, zrow))

        def p4(i, carry):
            d_cb, d_w0, d_w1, d_w2, d_w3 = carry
            r0 = pl.multiple_of(i * BLK, BLK)
            sl = pl.ds(r0, BLK)
            grow = rid + r0
            valid = grow >= ROW0
            dxc = dxc_s[sl, :]
            nxt = dxc_s[pl.ds(pl.multiple_of(jnp.minimum(r0 + BLK, rows - 8), 8), 8), :] * (i < nb - 1).astype(f32)
            ext = jnp.concatenate([dxc, nxt], axis=0)
            dxr = cw[0:1] * dxc
            for k in range(1, CONV_WIDTH):
                dxr = dxr + cw[k:k + 1] * pltpu.roll(ext, BLK + 8 - k, 0)[:BLK, :]
            dxr_ref[sl, :] = jnp.where(valid, dxr, 0.0).astype(bf16)
            cur = jnp.where(valid, xr_ref[sl, :], 0.0)
            prev8 = xr_ref[pl.ds(pl.multiple_of(jnp.maximum(r0 - 8, 0), 8), 8), :] * (i > 0).astype(f32)
            dws = [d_w0 + _colsum(dxc * cur)]
            for k, acc in ((1, d_w1), (2, d_w2), (3, d_w3)):
                dws.append(acc + _colsum(dxc * _shift_rows(prev8, cur, k)))
            return (d_cb + _colsum(dxc), *dws)

        d_cb, d_w0, d_w1, d_w2, d_w3 = lax.fori_loop(0, nb, p4, (zrow,) * 5)

        d_lam = d_ls * _sigmoid(-lam_v)
        vec_ref[...] = jnp.concatenate([d_bra, d_bri, d_lam, d_cb, d_w0, d_w1, d_w2, d_w3], axis=0)
        dw_ref[:, 0] = dw_s[0].astype(bf16).reshape(N_DEV, 32, RNN_BLOCK)
        dw_ref[:, 1] = dw_s[1].astype(bf16).reshape(N_DEV, 32, RNN_BLOCK)

    return pl.pallas_call(
        body, grid=(N_RNN_BLOCKS,),
        in_specs=[blk, blk, blk, col(0), col(OFF_GR)] + _RNN_IN_SPECS(rows),
        out_specs=[blk, blk,
                   pl.BlockSpec((N_DEV, 2, None, 32, RNN_BLOCK), lambda n: (0, 0, n, 0, 0)),
                   pl.BlockSpec((8, RNN_BLOCK), lambda n: (0, n))],
        out_shape=[SDS((rows, D), bf16), SDS((rows, D), bf16),
                   SDS((N_DEV, 2, N_RNN_BLOCKS, 32, RNN_BLOCK), bf16), SDS((8, D), f32)],
        scratch_shapes=[pltpu.VMEM((rows, RNN_BLOCK), f32), pltpu.VMEM((rows, RNN_BLOCK), f32),
                        pltpu.VMEM((rows, RNN_BLOCK), f32), pltpu.VMEM((2, RNN_BLOCK, RNN_BLOCK), f32)],
        name="rnn_bwd", compiler_params=_cp(("arbitrary",), 48),
    )(dya, hr, xc, z, z, smallw, conv_b, wrg, b_ra, b_ri, lam)


def _rope_tables(rows):
    half = jnp.arange(HALF, dtype=f32)
    inv = ROPE_THETA ** (-half / HALF)
    pos = (jnp.arange(rows) - ROW0).astype(f32)
    ang = pos[:, None] * inv[None, :]
    cos, sin = jnp.cos(ang), jnp.sin(ang)
    cos128 = jnp.concatenate([cos, cos, cos, cos], axis=1)
    sin128 = jnp.concatenate([-sin, sin, -sin, sin], axis=1)
    return cos128, sin128


def _rope128(x, cos128, sin128):
    lane = lax.broadcasted_iota(jnp.int32, x.shape, 1)
    swapped = jnp.where(lane % HEAD_DIM < HALF, pltpu.roll(x, 128 - HALF, 1), pltpu.roll(x, HALF, 1))
    return x * cos128 + swapped * sin128


def _qkv_prep(z, cos128, sin128):
    rows = z.shape[0]

    def body(q_ref, kv_ref, c_ref, s_ref, qo_ref, ko_ref, vo_ref):
        c, s = c_ref[...], s_ref[...]
        for g in range(D // 128):
            qo_ref[:, g * 128:(g + 1) * 128] = (_rope128(q_ref[:, g * 128:(g + 1) * 128], c, s)
                                                * (HEAD_DIM ** -0.5)).astype(bf16)
        for g in range(2):
            kr = _rope128(kv_ref[:, g * 128:(g + 1) * 128], c, s)
            for j in range(2):
                ko_ref[2 * g + j] = kr[:, j * HEAD_DIM:(j + 1) * HEAD_DIM].astype(bf16)
        for h in range(N_KV):
            vo_ref[h] = kv_ref[:, 256 + h * HEAD_DIM:256 + (h + 1) * HEAD_DIM].astype(bf16)

    return pl.pallas_call(
        body, grid=(rows // BLK,),
        in_specs=[pl.BlockSpec((BLK, D), lambda i: (i, OFF_Q // D)),
                  pl.BlockSpec((BLK, 512), lambda i: (i, OFF_K // 512)),
                  pl.BlockSpec((BLK, 128), lambda i: (i, 0)),
                  pl.BlockSpec((BLK, 128), lambda i: (i, 0))],
        out_specs=[pl.BlockSpec((BLK, D), lambda i: (i, 0)),
                   pl.BlockSpec((N_KV, BLK, HEAD_DIM), lambda i: (0, i, 0)),
                   pl.BlockSpec((N_KV, BLK, HEAD_DIM), lambda i: (0, i, 0))],
        out_shape=[SDS((rows, D), bf16), SDS((N_KV, rows, HEAD_DIM), bf16), SDS((N_KV, rows, HEAD_DIM), bf16)],
        name="qkv_prep", compiler_params=_cp(("arbitrary",)),
    )(z, z, cos128, sin128)


def _attn_mask(n):
    qi = n * BLK + lax.broadcasted_iota(jnp.int32, (BLK, 2 * BLK + N_META), 0)
    c = lax.broadcasted_iota(jnp.int32, (BLK, 2 * BLK + N_META), 1)
    jb = (n - 1) * BLK + c
    band = (jb >= BLK) & (jb <= qi) & (qi - jb < BLK)
    meta = (ROW0 + c - 2 * BLK) <= qi
    return ((c < 2 * BLK) & band) | ((c >= 2 * BLK) & meta)


N_KEYS = 2 * BLK + N_META


def _stack_heads(t):
    return jnp.concatenate([t[:, g * HEAD_DIM:(g + 1) * HEAD_DIM] for g in range(GROUP)], axis=0)


def _sink_column(sink_ref, h):
    g = lax.broadcasted_iota(jnp.int32, (GROUP, 1, 1), 0)
    col = jnp.zeros((GROUP, 1, 1), f32)
    for j in range(GROUP):
        col = jnp.where(g == j, sink_ref[h * GROUP + j], col)
    return col


def _kv_specs(last):
    cl = lambda n: jnp.minimum(n, last)
    return [pl.BlockSpec((None, N_META, HEAD_DIM), lambda h, n: (h, ROW0 // N_META, 0)),
            pl.BlockSpec((None, BLK, HEAD_DIM), lambda h, n: (h, jnp.maximum(cl(n) - 1, 0), 0)),
            pl.BlockSpec((None, BLK, HEAD_DIM), lambda h, n: (h, cl(n), 0))]


def _attn_fwd(q_r, k_r, v_b, z, sinks):
    rows = q_r.shape[0]
    nb = rows // BLK

    def body(sink_ref, q_ref, km_ref, kp_ref, kc_ref, vm_ref, vp_ref, vc_ref, ga_ref, o_ref, yb_ref, lse_ref):
        h, n = pl.program_id(0), pl.program_id(1)
        kk = jnp.concatenate([kp_ref[...], kc_ref[...], km_ref[...]], axis=0)
        vv = jnp.concatenate([vp_ref[...], vc_ref[...], vm_ref[...]], axis=0)
        q2 = _stack_heads(q_ref[...])
        s = jnp.where(_attn_mask(n)[None], _dot_nt(q2, kk).reshape(GROUP, BLK, N_KEYS), NEG_INF)
        sink = _sink_column(sink_ref, h)
        m = jnp.maximum(jnp.max(s, axis=-1, keepdims=True), sink)
        p = jnp.exp(s - m)
        den = jnp.sum(p, axis=-1, keepdims=True) + jnp.exp(sink - m)
        o2 = _dot((p / den).astype(bf16).reshape(GROUP * BLK, N_KEYS), vv)
        lse = m + jnp.log(den)
        for g in range(GROUP):
            o_ref[:, g * HEAD_DIM:(g + 1) * HEAD_DIM] = o2[g * BLK:(g + 1) * BLK]
            lse_ref[:, g:g + 1] = lse[g]
        yb_ref[...] = (o_ref[...] * _silu_and_grad(ga_ref[...])[0]).astype(bf16)

    tile = pl.BlockSpec((BLK, 512), lambda h, n: (n, h))
    return pl.pallas_call(
        body, grid=(N_KV, nb),
        in_specs=[pl.BlockSpec(memory_space=pltpu.SMEM), tile] + _kv_specs(nb - 1) + _kv_specs(nb - 1)
                 + [pl.BlockSpec((BLK, 512), lambda h, n: (n, OFF_GA // 512 + h))],
        out_specs=[tile, tile, pl.BlockSpec((None, BLK, GROUP), lambda h, n: (h, n, 0))],
        out_shape=[SDS((rows, D), f32), SDS((rows, D), bf16), SDS((N_KV, rows, GROUP), f32)],
        name="attn_fwd", compiler_params=_cp(("arbitrary", "arbitrary")),
    )(sinks, q_r, k_r, k_r, k_r, v_b, v_b, v_b, z)


def _attn_bwd(dyb, o32, lse, q_r, k_r, v_b, z, sinks):
    rows = q_r.shape[0]
    nb = rows // BLK
    cl = lambda n: jnp.minimum(n, nb - 1)

    def body(sink_ref, dyb_ref, o_ref, lse_ref, q_ref, km_ref, kp_ref, kc_ref, vm_ref, vp_ref, vc_ref, ga_ref,
             dq_ref, dga_ref, dk_ref, dv_ref, dkm_ref, dvm_ref, dsr_ref, ck_s, cv_s):
        h, n = pl.program_id(0), pl.program_id(1)

        @pl.when(n == 0)
        def _():
            dkm_ref[...] = jnp.zeros_like(dkm_ref)
            dvm_ref[...] = jnp.zeros_like(dvm_ref)
            ck_s[...] = jnp.zeros_like(ck_s)
            cv_s[...] = jnp.zeros_like(cv_s)

        @pl.when(n < nb)
        def _():
            kk = jnp.concatenate([kp_ref[...], kc_ref[...], km_ref[...]], axis=0)
            vv = jnp.concatenate([vp_ref[...], vc_ref[...], vm_ref[...]], axis=0)
            sg, dsg = _silu_and_grad(ga_ref[...])
            dyb_v = dyb_ref[...]
            o_v = o_ref[...]
            dga_ref[...] = (dyb_v * o_v * dsg).astype(bf16)
            q2 = _stack_heads(q_ref[...])
            do2 = _stack_heads(dyb_v * sg)
            lse_v = lse_ref[...]
            lse = jnp.concatenate([lse_v[:, g:g + 1] for g in range(GROUP)], axis=0).reshape(GROUP, BLK, 1)
            delta = jnp.sum(do2 * _stack_heads(o_v), axis=-1, keepdims=True).reshape(GROUP, BLK, 1)
            s = jnp.where(_attn_mask(n)[None], _dot_nt(q2, kk).reshape(GROUP, BLK, N_KEYS), NEG_INF)
            p = jnp.exp(s - lse)
            do2b = do2.astype(bf16)
            ds = (p * (_dot_nt(do2b, vv).reshape(GROUP, BLK, N_KEYS) - delta)).astype(bf16)
            ds = ds.reshape(GROUP * BLK, N_KEYS)
            dsr = -jnp.exp(_sink_column(sink_ref, h) - lse) * delta
            dq2 = _dot(ds, kk)
            for g in range(GROUP):
                dq_ref[:, g * HEAD_DIM:(g + 1) * HEAD_DIM] = dq2[g * BLK:(g + 1) * BLK]
                dsr_ref[:, g:g + 1] = dsr[g]
            dkk = _dot_tn(ds, q2)
            dvv = _dot_tn(p.astype(bf16).reshape(GROUP * BLK, N_KEYS), do2b)
            dk_ref[...] = ck_s[...] + dkk[:BLK]
            dv_ref[...] = cv_s[...] + dvv[:BLK]
            ck_s[...] = dkk[BLK:2 * BLK]
            cv_s[...] = dvv[BLK:2 * BLK]
            dkm_ref[...] += dkk[2 * BLK:]
            dvm_ref[...] += dvv[2 * BLK:]

        @pl.when(n == nb)
        def _():
            dk_ref[...] = ck_s[...]
            dv_ref[...] = cv_s[...]

    tile = pl.BlockSpec((BLK, 512), lambda h, n: (cl(n), h))
    kvout = pl.BlockSpec((None, BLK, HEAD_DIM), lambda h, n: (h, jnp.maximum(n - 1, 0), 0))
    mout = pl.BlockSpec((None, N_META, HEAD_DIM), lambda h, n: (h, 0, 0))
    stat = pl.BlockSpec((None, BLK, GROUP), lambda h, n: (h, cl(n), 0))
    return pl.pallas_call(
        body, grid=(N_KV, nb + 1),
        in_specs=[pl.BlockSpec(memory_space=pltpu.SMEM), tile, tile, stat, tile] + _kv_specs(nb - 1)
                 + _kv_specs(nb - 1) + [pl.BlockSpec((BLK, 512), lambda h, n: (cl(n), OFF_GA // 512 + h))],
        out_specs=[tile, tile, kvout, kvout, mout, mout, stat],
        out_shape=[SDS((rows, D), f32), SDS((rows, D), bf16),
                   SDS((N_KV, rows, HEAD_DIM), f32), SDS((N_KV, rows, HEAD_DIM), f32),
                   SDS((N_KV, N_META, HEAD_DIM), f32), SDS((N_KV, N_META, HEAD_DIM), f32),
                   SDS((N_KV, rows, GROUP), f32)],
        scratch_shapes=[pltpu.VMEM((BLK, HEAD_DIM), f32), pltpu.VMEM((BLK, HEAD_DIM), f32)],
        name="attn_bwd", compiler_params=_cp(("arbitrary", "arbitrary")),
    )(sinks, dyb, o32, lse, q_r, k_r, k_r, k_r, v_b, v_b, v_b, z)


def _qkv_finish(dq, dk, dv, dkm, dvm, cos128, sin128):
    rows = dq.shape[0]

    def body(dq_ref, dk_ref, dv_ref, dkm_ref, dvm_ref, c_ref, s_ref, oq_ref, okv_ref):
        first = (pl.program_id(0) == 0).astype(f32)
        c, s = c_ref[...], -s_ref[...]
        for g in range(D // 128):
            oq_ref[:, g * 128:(g + 1) * 128] = (_rope128(dq_ref[:, g * 128:(g + 1) * 128], c, s)
                                                * (HEAD_DIM ** -0.5)).astype(bf16)
        pad = jnp.zeros((ROW0, HEAD_DIM), f32)
        ks = [dk_ref[h] + first * jnp.concatenate([pad, dkm_ref[h]], axis=0) for h in range(N_KV)]
        vs = [dv_ref[h] + first * jnp.concatenate([pad, dvm_ref[h]], axis=0) for h in range(N_KV)]
        for g in range(2):
            kp = jnp.concatenate([ks[2 * g], ks[2 * g + 1]], axis=1)
            okv_ref[:, g * 128:(g + 1) * 128] = _rope128(kp, c, s).astype(bf16)
            okv_ref[:, 256 + g * 128:256 + (g + 1) * 128] = jnp.concatenate([vs[2 * g], vs[2 * g + 1]], axis=1).astype(bf16)

    kv = pl.BlockSpec((N_KV, BLK, HEAD_DIM), lambda i: (0, i, 0))
    mt = pl.BlockSpec((N_KV, N_META, HEAD_DIM), lambda i: (0, 0, 0))
    return pl.pallas_call(
        body, grid=(rows // BLK,),
        in_specs=[pl.BlockSpec((BLK, D), lambda i: (i, 0)), kv, kv, mt, mt,
                  pl.BlockSpec((BLK, 128), lambda i: (i, 0)), pl.BlockSpec((BLK, 128), lambda i: (i, 0))],
        out_specs=[pl.BlockSpec((BLK, D), lambda i: (i, 0)), pl.BlockSpec((BLK, 512), lambda i: (i, 0))],
        out_shape=[SDS((rows, D), bf16), SDS((rows, 512), bf16)],
        name="qkv_finish", compiler_params=_cp(("arbitrary",)),
    )(dq, dk, dv, dkm, dvm, cos128, sin128)


_TW = 512


def _mix_specs(rows):
    tr = _row_chunk(rows)
    tile = pl.BlockSpec((tr, _TW), lambda i, j: (i, j))
    ga = pl.BlockSpec((tr, _TW), lambda i, j: (i, OFF_G // _TW + j))
    gb = pl.BlockSpec((tr, _TW), lambda i, j: (i, (OFF_G + D) // _TW + j))
    return (rows // tr, D // _TW), tile, ga, gb


def _mix_fwd(y_a, y_b, z):
    rows = y_a.shape[0]
    grid, _mix_tile, _mix_ga, _mix_gb = _mix_specs(rows)

    def body(ya_ref, yb_ref, ga_ref, gb_ref, o_ref):
        o_ref[...] = (_sigmoid(ga_ref[...]) * ya_ref[...] + _sigmoid(gb_ref[...]) * yb_ref[...]).astype(bf16)

    return pl.pallas_call(
        body, grid=grid, in_specs=[_mix_tile, _mix_tile, _mix_ga, _mix_gb],
        out_specs=_mix_tile, out_shape=SDS((rows, D), bf16),
        name="mix_fwd", compiler_params=_cp(("arbitrary", "arbitrary")),
    )(y_a, y_b, z, z)


def _mix_bwd(dmixed, y_a, y_b, z):
    rows = y_a.shape[0]
    grid, _mix_tile, _mix_ga, _mix_gb = _mix_specs(rows)

    def body(dm_ref, ya_ref, yb_ref, ga_ref, gb_ref, dya_ref, dyb_ref, dga_ref, dgb_ref):
        dm = dm_ref[...]
        sa, sb = _sigmoid(ga_ref[...]), _sigmoid(gb_ref[...])
        dya_ref[...] = (dm * sa).astype(bf16)
        dyb_ref[...] = (dm * sb).astype(bf16)
        dga_ref[...] = (dm * ya_ref[...] * sa * (1.0 - sa)).astype(bf16)
        dgb_ref[...] = (dm * yb_ref[...] * sb * (1.0 - sb)).astype(bf16)

    return pl.pallas_call(
        body, grid=grid, in_specs=[_mix_tile, _mix_tile, _mix_tile, _mix_ga, _mix_gb],
        out_specs=[_mix_tile] * 4, out_shape=[SDS((rows, D), bf16)] * 4,
        name="mix_bwd", compiler_params=_cp(("arbitrary", "arbitrary")),
    )(dmixed, y_a, y_b, z, z)


def _final_ln(out32, h32, tgt, ln_g, ln_b):
    rows = out32.shape[0]

    def body(o_ref, h_ref, t_ref, g_ref, b_ref, du_ref, dub_ref, st_ref):
        i = pl.program_id(0)
        g = g_ref[...]
        y, xhat, rstd = _ln_rows(ALPHA * h_ref[...] + o_ref[...], g, b_ref[...])
        e = jnp.where(i > 0, y - t_ref[0], 0.0)
        dy = e * (1.0 / D)
        du = _ln_rows_bwd(dy, g, xhat, rstd)
        du_ref[...] = du
        dub_ref[...] = du.astype(bf16)
        st = jnp.concatenate([_colsum(dy * xhat), _colsum(dy), _colsum(du), _colsum(e * e) * (0.5 / D),
                              jnp.zeros((4, D), f32)], axis=0)

        @pl.when(i == 0)
        def _():
            st_ref[...] = st

        @pl.when(i > 0)
        def _():
            st_ref[...] += st

    row = pl.BlockSpec((BLK, D), lambda i: (i, 0))
    vec = pl.BlockSpec((1, D), lambda i: (0, 0))
    return pl.pallas_call(
        body, grid=(rows // BLK,),
        in_specs=[row, row, pl.BlockSpec((1, BLK, D), lambda i: (0, jnp.maximum(i - 1, 0), 0)), vec, vec],
        out_specs=[row, row, pl.BlockSpec((8, D), lambda i: (0, 0))],
        out_shape=[SDS((rows, D), f32), SDS((rows, D), bf16), SDS((8, D), f32)],
        name="final_ln", compiler_params=_cp(("arbitrary",)),
    )(out32, h32, tgt, ln_g, ln_b)


def _assemble_dz(dxr, dgr, dq, dkv, dga, dma, dmb):
    rows = dxr.shape[0]
    parts = [(dxr, D), (dgr, D), (dq, D), (dkv, 512), (dga, D), (dma, D), (dmb, D)]

    def body(*refs):
        o_ref = refs[-1]
        off = 0
        for r, (_, w) in zip(refs[:-1], parts):
            o_ref[:, off:off + w] = r[...]
            off += w

    return pl.pallas_call(
        body, grid=(rows // BLK,),
        in_specs=[pl.BlockSpec((BLK, w), lambda i: (i, 0)) for _, w in parts],
        out_specs=pl.BlockSpec((BLK, D_IN), lambda i: (i, 0)),
        out_shape=SDS((rows, D_IN), bf16), name="assemble_dz", compiler_params=_cp(("arbitrary",)),
    )(*[p for p, _ in parts])


def _step_branches(x, w_full, wrg, smallw, p):
    rows = x.shape[1] + BLK
    cos128, sin128 = _rope_tables(rows)
    sinks = p["sinks"].reshape(N_KV * GROUP)
    h32, hb, h_t = _ln_emb(x, smallw, p["ln_emb_g"], p["ln_emb_b"])
    z = _mm(hb, w_full, bias=p["b_in"], name="mm_z")
    xc, hr, ya = _rnn_fwd(z, smallw, p["conv_b"], wrg, p["b_ra"], p["b_ri"], p["lru_lambda"])
    q_r, k_r, v_b = _qkv_prep(z, cos128, sin128)
    o32, yb, lse = _attn_fwd(q_r, k_r, v_b, z, sinks)
    return dict(cos128=cos128, sin128=sin128, sinks=sinks, h32=h32, h_t=h_t, z=z, xc=xc, hr=hr, ya=ya, q_r=q_r,
                k_r=k_r, v_b=v_b, o32=o32, yb=yb, lse=lse)


def _step_merge(s, tgt, w3, p):
    ya, yb, z = s["ya"], s["yb"], s["z"]
    y_a = _mm(ya, w3, sel=0, name="mm_ya")
    y_b = _mm(yb, w3, sel=1, name="mm_yb")
    mixed = _mix_fwd(y_a, y_b, z)
    out32 = _mm(mixed, w3, sel=2, bias=p["b_o"], name="mm_out")
    du32, dub, st_out = _final_ln(out32, s["h32"], tgt, p["ln_g"], p["ln_b"])

    g_wo = _mm(_transpose(mixed, "t_mixed"), dub, out_dtype=bf16, name="mm_dwo")
    dmixed = _mm(dub, w3, sel=2, nt=True, name="mm_dmixed")
    dya_b, dyb_b, dma, dmb = _mix_bwd(dmixed, y_a, y_b, z)
    g_wrnn = _mm(_transpose(ya, "t_ya"), dya_b, out_dtype=bf16, name="mm_dwrnn")
    g_wattn = _mm(_transpose(yb, "t_yb"), dyb_b, out_dtype=bf16, name="mm_dwattn")
    dya = _mm(dya_b, w3, sel=0, nt=True, name="mm_dya")
    dyb = _mm(dyb_b, w3, sel=1, nt=True, name="mm_dyb")
    return dict(du32=du32, st_out=st_out, dma=dma, dmb=dmb, dya=dya, dyb=dyb, g_wo=g_wo, g_wrnn=g_wrnn,
                g_wattn=g_wattn)


def _step_backward(s, t, x, w_full, wrg, smallw, p, conv_b):
    z = s["z"]
    dxr, dgr, g_wrg, vec_rnn = _rnn_bwd(t["dya"], s["hr"], s["xc"], z, smallw, conv_b, wrg, p["b_ra"], p["b_ri"],
                                        p["lru_lambda"])
    dq_r, dga, dk, dv, dkm, dvm, dsr = _attn_bwd(t["dyb"], s["o32"], s["lse"], s["q_r"], s["k_r"], s["v_b"], z,
                                                 s["sinks"])
    dq, dkv = _qkv_finish(dq_r, dk, dv, dkm, dvm, s["cos128"], s["sin128"])
    dz = _assemble_dz(dxr, dgr, dq, dkv, dga, t["dma"], t["dmb"])
    g_win, db_in = _mm_dwin(s["h_t"], dz)
    dh = _mm_dh(dz, w_full)
    grad_x, dmeta, st_emb = _ln_emb_bwd(dh, t["du32"], x, smallw, p["ln_emb_g"])
    return dict(st_emb=st_emb, vec_rnn=vec_rnn, dsr=dsr, db_in=db_in, dmeta=dmeta, grad_x=grad_x, g_win=g_win,
                g_wrg=g_wrg)


_ANY = pl.BlockSpec(memory_space=pl.ANY)
_VMEM = pl.BlockSpec(memory_space=pltpu.VMEM)


def _place():
    x, y, c = lax.axis_index("x"), lax.axis_index("y"), lax.axis_index("c")
    return x, y, c


def _dev(px, py, pc):
    return 4 * px + 2 * py + pc


def _cast_w_in(w_in):
    tm = 256

    def body(i_ref, o_ref):
        o_ref[...] = i_ref[0].astype(bf16)

    return pl.pallas_call(
        body, grid=(D // tm,),
        in_specs=[pl.BlockSpec((1, tm, SHARD_IN), lambda i: (0, i, 0))],
        out_specs=pl.BlockSpec((tm, SHARD_IN), lambda i: (i, 0)),
        out_shape=SDS((D, SHARD_IN), bf16), name="cast_w_in", compiler_params=_cp(("arbitrary",)),
    )(w_in)


def _cast_small(w_rnn_out, w_attn_out, w_o, w_ra, w_ri, meta, conv_w):
    def body(a_ref, b_ref, c_ref, ra_ref, ri_ref, m_ref, cw_ref, w3_ref, wrg_ref, sw_ref):
        w3_ref[0] = a_ref[0].astype(bf16)
        w3_ref[1] = b_ref[0].astype(bf16)
        w3_ref[2] = c_ref[0].astype(bf16)
        wrg_ref[0] = ra_ref[0].astype(bf16)
        wrg_ref[1] = ri_ref[0].astype(bf16)
        sw_ref[...] = jnp.concatenate([m_ref[...], cw_ref[0], jnp.zeros((4, 256), f32)], axis=0)

    return pl.pallas_call(
        body,
        out_shape=[SDS((3, 256, D), bf16), SDS((2, N_RNN_BLOCKS, 32, RNN_BLOCK), bf16), SDS((24, 256), f32)],
        name="cast_small", compiler_params=_cp(None),
    )(w_rnn_out, w_attn_out, w_o, w_ra, w_ri, meta, conv_w)


def _all_gather(shards, later):
    n = len(shards)
    nl = len(later)

    def body(*refs):
        ins, outs = refs[:n], refs[n + nl:2 * n + nl]
        send_sems, recv_sems, local_sems = refs[2 * (n + nl):]
        x, y, c = _place()
        me, sibling = (x, y, c), (x, y, 1 - c)
        chips = [(1 - x, y), (x, 1 - y), (1 - x, 1 - y)]

        def copy(a, k, block, to, src=None):
            dst = outs[a].at[_dev(*block)]
            return pltpu.make_async_remote_copy(
                src_ref=dst if src is None else src, dst_ref=dst,
                send_sem=send_sems.at[a * 7 + k], recv_sem=recv_sems.at[a * 7 + k],
                device_id=to, device_id_type=MESH)

        all_ins, all_outs = refs[:n + nl], refs[n + nl:2 * (n + nl)]
        mine = [pltpu.make_async_copy(all_ins[a], all_outs[a].at[_dev(*me)], local_sems.at[a]) for a in range(n + nl)]
        for cp in mine:
            cp.start()
        first = []
        for a in range(n):
            first.append(copy(a, 0, me, sibling, src=ins[a]))
            first += [copy(a, 1 + j, me, (*chip, c), src=ins[a]) for j, chip in enumerate(chips)]
        for cp in first:
            cp.start()
        passed = []
        for a in range(n):
            for j, chip in enumerate(chips):
                copy(a, 1 + j, (*chip, c), me).wait_recv()
                cp = copy(a, 4 + j, (*chip, c), sibling)
                cp.start()
                passed.append(cp)
        for a in range(n):
            copy(a, 0, sibling, me).wait_recv()
            for j, chip in enumerate(chips):
                copy(a, 4 + j, (*chip, 1 - c), me).wait_recv()
        for cp in first + passed:
            cp.wait_send()
        for cp in mine:
            cp.wait()

    return pl.pallas_call(
        body, in_specs=[_ANY] * (n + nl), out_specs=[_ANY] * (n + nl),
        out_shape=[SDS((N_DEV, *s.shape), s.dtype) for s in (*shards, *later)],
        scratch_shapes=[pltpu.SemaphoreType.DMA((7 * n,)), pltpu.SemaphoreType.DMA((7 * n,)),
                        pltpu.SemaphoreType.DMA((n + nl,))],
        name="all_gather_weights",
    )(*shards, *later)


def _exchange_siblings(grads):
    n = len(grads)

    def body(*refs):
        ins, outs = refs[:n], refs[n:2 * n]
        send_sems, recv_sems = refs[2 * n:]
        x, y, c = _place()
        copies = []
        for a in range(n):
            for q in range(4):
                copies.append(pltpu.make_async_remote_copy(
                    src_ref=ins[a].at[2 * q + (1 - c)], dst_ref=outs[a].at[q],
                    send_sem=send_sems.at[4 * a + q], recv_sem=recv_sems.at[4 * a + q],
                    device_id=(x, y, 1 - c), device_id_type=MESH))
        for cp in copies:
            cp.start()
        for cp in copies:
            cp.wait()

    return pl.pallas_call(
        body, in_specs=[_ANY] * n, out_specs=[_ANY] * n,
        out_shape=[SDS((4, *g.shape[1:]), g.dtype) for g in grads],
        scratch_shapes=[pltpu.SemaphoreType.DMA((4 * n,)), pltpu.SemaphoreType.DMA((4 * n,))],
        name="exchange_siblings",
    )(*grads)


def _exchange_chips(parts):
    n = len(parts)

    def body(*refs):
        ins, outs = refs[:n], refs[n:2 * n]
        send_sems, recv_sems = refs[2 * n:]
        x, y, c = _place()
        chips = [(1 - x, y), (x, 1 - y), (1 - x, 1 - y)]
        copies = []
        for a in range(n):
            for j, (qx, qy) in enumerate(chips):
                copies.append(pltpu.make_async_remote_copy(
                    src_ref=ins[a].at[2 * qx + qy], dst_ref=outs[a].at[j],
                    send_sem=send_sems.at[3 * a + j], recv_sem=recv_sems.at[3 * a + j],
                    device_id=(qx, qy, c), device_id_type=MESH))
        for cp in copies:
            cp.start()
        for cp in copies:
            cp.wait()

    return pl.pallas_call(
        body, in_specs=[_ANY] * n, out_specs=[_ANY] * n,
        out_shape=[SDS((3, *p.shape[1:]), p.dtype) for p in parts],
        scratch_shapes=[pltpu.SemaphoreType.DMA((3 * n,)), pltpu.SemaphoreType.DMA((3 * n,))],
        name="exchange_chips",
    )(*parts)


_HBM = pl.BlockSpec(memory_space=pltpu.HBM)
_SEM = pl.BlockSpec(memory_space=pltpu.SEMAPHORE)
_PEER_FLIPS = [(f // 4, (f // 2) % 2, f % 2) for f in range(1, N_DEV)]


def _peers(x, y, c):
    return [((x + fx) % 2, (y + fy) % 2, (c + fc) % 2) for fx, fy, fc in _PEER_FLIPS]


def _direct_copy(srcs, lands, send_sems, recv_sems, a, k, peer, me, same_src):
    src = srcs[a] if same_src else srcs[a].at[_dev(*peer)]
    return pltpu.make_async_remote_copy(
        src_ref=src, dst_ref=lands[a].at[me], send_sem=send_sems.at[7 * a + k], recv_sem=recv_sems.at[7 * a + k],
        device_id=peer, device_id_type=MESH)


def _direct_start(srcs, lands, dep, same_src, name):
    n = len(srcs)

    def body(*refs):
        src, land = refs[:n], refs[n:2 * n]
        send_sems, recv_sems = refs[2 * n + 1], refs[2 * n + 2]
        token = refs[-1]
        x, y, c = _place()
        me = _dev(x, y, c)
        for a in range(n):
            for k, peer in enumerate(_peers(x, y, c)):
                _direct_copy(src, land, send_sems, recv_sems, a, k, peer, me, same_src).start()
        token[...] = jnp.zeros_like(token)

    hbm = lambda t: pltpu.with_memory_space_constraint(t, pltpu.HBM)
    res = pl.pallas_call(
        body, name=name,
        out_shape=(pltpu.SemaphoreType.DMA((7 * n,)), pltpu.SemaphoreType.DMA((7 * n,)),
                   *[pltpu.HBM(t.shape, t.dtype) for t in (*srcs, *lands)], SDS((8, 128), f32)),
        in_specs=[_HBM] * (2 * n) + [_ANY], out_specs=(_SEM, _SEM, *([_HBM] * (2 * n)), _VMEM),
        input_output_aliases={i: 2 + i for i in range(2 * n)},
        compiler_params=pltpu.CompilerParams(has_side_effects=pltpu.SideEffectType.DATAFLOW_SIDE_EFFECTING),
    )(*[hbm(t) for t in (*srcs, *lands)], dep)
    return res[0], res[1], list(res[2:2 + n]), list(res[2 + n:2 + 2 * n]), res[-1]


def _direct_wait(send_sems, recv_sems, srcs, lands, after, same_src, name):
    n = len(srcs)

    def body(*refs):
        src, land = refs[:n], refs[n:2 * n]
        send_r, recv_r = refs[2 * n], refs[2 * n + 1]
        x, y, c = _place()
        for a in range(n):
            for k, peer in enumerate(_peers(x, y, c)):
                cp = _direct_copy(src, land, send_r, recv_r, a, k, peer, _dev(*peer), same_src)
                cp.wait_send()
                cp.wait_recv()

    res = pl.pallas_call(
        body, name=name,
        out_shape=tuple(pltpu.HBM(t.shape, t.dtype) for t in (*srcs, *lands)),
        in_specs=[_HBM] * (2 * n) + [_SEM, _SEM, _ANY], out_specs=tuple([_HBM] * (2 * n)),
        input_output_aliases={i: i for i in range(2 * n)},
        compiler_params=pltpu.CompilerParams(has_side_effects=pltpu.SideEffectType.DATAFLOW_SIDE_EFFECTING),
    )(*srcs, *lands, send_sems, recv_sems, after)
    return list(res[:n]), list(res[n:])


def _adamw_direct(g, land, me_idx, w, m, v, name):
    r, wd = w.shape
    tr = min(r, 256)

    def body(me_ref, *refs):
        g_ref, peers = refs[0], refs[1:N_DEV]
        w_ref, m_ref, v_ref, g_out, d_out, m_out, v_out = refs[N_DEV:]
        gs = g_ref[...].astype(f32)
        for p_ref in peers:
            gs = gs + p_ref[...].astype(f32)
        d, mn, vn = _adamw(w_ref[...], gs, m_ref[...], v_ref[...])
        g_out[...] = gs
        d_out[...] = d
        m_out[...] = mn
        v_out[...] = vn

    tile = pl.BlockSpec((tr, wd), lambda i, me_ref: (i, 0))
    slot = lambda k: pl.BlockSpec((None, tr, wd), lambda i, me_ref: ((me_ref[0] + k) % N_DEV, i, 0))
    return pl.pallas_call(
        body,
        grid_spec=pltpu.PrefetchScalarGridSpec(
            num_scalar_prefetch=1, grid=(r // tr,),
            in_specs=[slot(0)] + [slot(k) for k in range(1, N_DEV)] + [tile, tile, tile],
            out_specs=[tile] * 4),
        out_shape=[SDS((r, wd), f32)] * 4, name=name, compiler_params=_cp(("arbitrary",), 48),
    )(me_idx, g, *([land] * (N_DEV - 1)), w, m, v)


def _pair_sum(g, r1, c_idx, name):
    _, r, w = g.shape
    tr = min(r, 256)

    def body(c_ref, g_ref, r_ref, o_ref):
        o_ref[...] = (g_ref[...].astype(f32) + r_ref[...].astype(f32)).astype(bf16)

    return pl.pallas_call(
        body,
        grid_spec=pltpu.PrefetchScalarGridSpec(
            num_scalar_prefetch=1, grid=(4, r // tr),
            in_specs=[pl.BlockSpec((None, tr, w), lambda q, i, c_ref: (2 * q + c_ref[0], i, 0)),
                      pl.BlockSpec((None, tr, w), lambda q, i, c_ref: (q, i, 0))],
            out_specs=pl.BlockSpec((None, tr, w), lambda q, i, c_ref: (q, i, 0))),
        out_shape=SDS((4, r, w), bf16), name=name, compiler_params=_cp(("arbitrary", "arbitrary")),
    )(c_idx, g, r1)


def _adamw(w, g, m, v):
    m = ADAM_B1 * m + (1.0 - ADAM_B1) * g
    v = ADAM_B2 * v + (1.0 - ADAM_B2) * (g * g)
    m_hat = m / (1.0 - ADAM_B1 ** ADAM_STEP)
    v_hat = v / (1.0 - ADAM_B2 ** ADAM_STEP)
    delta = -ADAM_LR * (m_hat / (jnp.sqrt(v_hat) + ADAM_EPS) + ADAM_WD * w)
    return delta, m, v


def _adamw_big(part, r2, q_idx, w, m, v, name, row_off=0):
    r, wd = w.shape
    tr = min(r, 256)

    def body(q_ref, p_ref, r_ref, w_ref, m_ref, v_ref, g_out, d_out, m_out, v_out):
        g = p_ref[...].astype(f32)
        for j in range(3):
            g = g + r_ref[j].astype(f32)
        d, mn, vn = _adamw(w_ref[...], g, m_ref[...], v_ref[...])
        g_out[...] = g
        d_out[...] = d
        m_out[...] = mn
        v_out[...] = vn

    tile = pl.BlockSpec((tr, wd), lambda i, q_ref: (i, 0))
    return pl.pallas_call(
        body,
        grid_spec=pltpu.PrefetchScalarGridSpec(
            num_scalar_prefetch=1, grid=(r // tr,),
            in_specs=[pl.BlockSpec((None, tr, wd), lambda i, q_ref: (q_ref[0], row_off + i, 0)),
                      pl.BlockSpec((3, tr, wd), lambda i, q_ref: (0, row_off + i, 0)), tile, tile, tile],
            out_specs=[tile] * 4),
        out_shape=[SDS((r, wd), f32)] * 4, name=name, compiler_params=_cp(("arbitrary",), 48),
    )(q_idx, part, r2, w, m, v)


_SMALL_ROWS = 24


def _pack_small(st_emb, vec_rnn, st_out, dsr, db_in, dmeta):
    def body(se_ref, vr_ref, so_ref, dsr_ref, db_ref, dm_ref, sm_ref, sm2_ref):
        sm_ref[...] = jnp.zeros_like(sm_ref)
        sm2_ref[...] = jnp.zeros_like(sm2_ref)
        sm_ref[0:2, :] = se_ref[0:2, :]
        sm_ref[2:3, :] = vr_ref[3:4, :]
        sm_ref[3:6, :] = vr_ref[0:3, :]
        sm_ref[6:7, :] = so_ref[2:3, :]
        sm_ref[7:9, :] = so_ref[0:2, :]
        for h in range(N_KV):
            sm_ref[9:10, h * GROUP:(h + 1) * GROUP] = _colsum(dsr_ref[h])
        for j in range(6):
            sm_ref[16 + j:17 + j, :] = db_ref[0:1, j * D:(j + 1) * D]
        sm_ref[22:23, 0:D_IN - 6 * D] = db_ref[0:1, 6 * D:D_IN]
        for s in range(N_DEV):
            sm2_ref[s, 0:N_META, :] = dm_ref[:, s * 256:(s + 1) * 256]
            sm2_ref[s, N_META:N_META + CONV_WIDTH, :] = vr_ref[4:8, s * 256:(s + 1) * 256]

    return pl.pallas_call(
        body, out_shape=[SDS((_SMALL_ROWS, D), f32), SDS((N_DEV, 24, 256), f32)],
        name="pack_small", compiler_params=_cp(None),
    )(st_emb, vec_rnn, st_out, dsr, db_in, dmeta)


def _small_allreduce(sm, sm2):
    def body(sm_ref, sm2_ref, o_ref, o2_ref, buf, buf2, send_sems, recv_sems):
        x, y, c = _place()
        me = _dev(x, y, c)
        copies = []
        for f in range(1, N_DEV):
            fx, fy, fc = f // 4, (f // 2) % 2, f % 2
            peer = ((x + fx) % 2, (y + fy) % 2, (c + fc) % 2)
            for t, (src, dst) in enumerate(((sm_ref, buf), (sm2_ref, buf2))):
                k = 2 * (f - 1) + t
                copies.append(pltpu.make_async_remote_copy(
                    src_ref=src, dst_ref=dst.at[me], send_sem=send_sems.at[k], recv_sem=recv_sems.at[k],
                    device_id=peer, device_id_type=MESH))
        for cp in copies:
            cp.start()
        buf[me] = sm_ref[...]
        buf2[me] = sm2_ref[...]
        for cp in copies:
            cp.wait()
        acc, acc2 = buf[0], buf2[0]
        for e in range(1, N_DEV):
            acc, acc2 = acc + buf[e], acc2 + buf2[e]
        o_ref[...] = acc
        o2_ref[...] = acc2

    return pl.pallas_call(
        body, in_specs=[_VMEM, _VMEM], out_specs=[_VMEM, _VMEM],
        out_shape=[SDS(sm.shape, f32), SDS(sm2.shape, f32)],
        scratch_shapes=[pltpu.VMEM((N_DEV, *sm.shape), f32), pltpu.VMEM((N_DEV, *sm2.shape), f32),
                        pltpu.SemaphoreType.DMA((14,)), pltpu.SemaphoreType.DMA((14,))],
        name="small_allreduce",
    )(sm, sm2)


_SMALL_ROW_OF = {"ln_emb_g": 0, "ln_emb_b": 1, "conv_b": 2, "b_ra": 3, "b_ri": 4, "lru_lambda": 5, "b_o": 6,
                 "ln_g": 7, "ln_b": 8}
_SMALL_NAMES = ["ln_emb_g", "ln_emb_b", "conv_b", "b_ra", "b_ri", "lru_lambda", "b_o", "ln_g", "ln_b",
                "sinks", "b_in", "meta_tokens", "conv_w"]


def _small_update(sm, sm2_mine, wmv):
    def grad_of(name, sm_ref, s2_ref):
        if name in _SMALL_ROW_OF:
            r = _SMALL_ROW_OF[name]
            return sm_ref[r:r + 1, :]
        if name == "sinks":
            return sm_ref[9:10, 0:N_KV * GROUP]
        if name == "b_in":
            return jnp.concatenate([sm_ref[16 + j:17 + j, :] for j in range(7)], axis=1)[:, :D_IN]
        if name == "meta_tokens":
            return s2_ref[0:N_META, :]
        return s2_ref[N_META:N_META + CONV_WIDTH, :]

    def body(*refs):
        sm_ref, s2_ref = refs[0], refs[1]
        ins = refs[2:2 + 3 * len(_SMALL_NAMES)]
        outs = refs[2 + 3 * len(_SMALL_NAMES):]
        for i, name in enumerate(_SMALL_NAMES):
            w_ref, m_ref, v_ref = ins[3 * i:3 * i + 3]
            g = grad_of(name, sm_ref, s2_ref)
            d, mn, vn = _adamw(w_ref[...], g, m_ref[...], v_ref[...])
            outs[4 * i][...] = g
            outs[4 * i + 1][...] = d
            outs[4 * i + 2][...] = mn
            outs[4 * i + 3][...] = vn

    args, out_shape = [sm, sm2_mine], []
    for name in _SMALL_NAMES:
        args += list(wmv[name])
        out_shape += [SDS(wmv[name][0].shape, f32)] * 4
    res = pl.pallas_call(body, out_shape=out_shape, name="small_update", compiler_params=_cp(None))(*args)
    return {name: tuple(res[4 * i:4 * i + 4]) for i, name in enumerate(_SMALL_NAMES)}


_WEIGHTS = ["meta_tokens", "ln_emb_g", "ln_emb_b", "w_in", "b_in", "conv_w", "conv_b", "w_ra", "b_ra", "w_ri",
            "b_ri", "lru_lambda", "sinks", "w_rnn_out", "w_attn_out", "w_o", "b_o", "ln_g", "ln_b"]
_SMALL_2D = {"meta_tokens": (N_META, 256), "conv_w": (CONV_WIDTH, 256), "b_in": (1, D_IN), "sinks": (1, N_KV * GROUP)}


def kernel(x, meta_tokens, ln_emb_g, ln_emb_b, w_in, b_in, conv_w, conv_b, w_ra, b_ra, w_ri, b_ri, lru_lambda, sinks, w_rnn_out, w_attn_out, w_o, b_o, ln_g, ln_b, loss_target, m_meta_tokens, m_ln_emb_g, m_ln_emb_b, m_w_in, m_b_in, m_conv_w, m_conv_b, m_w_ra, m_b_ra, m_w_ri, m_b_ri, m_lru_lambda, m_sinks, m_w_rnn_out, m_w_attn_out, m_w_o, m_b_o, m_ln_g, m_ln_b, v_meta_tokens, v_ln_emb_g, v_ln_emb_b, v_w_in, v_b_in, v_conv_w, v_conv_b, v_w_ra, v_b_ra, v_w_ri, v_b_ri, v_lru_lambda, v_sinks, v_w_rnn_out, v_w_attn_out, v_w_o, v_b_o, v_ln_g, v_ln_b):
    w = dict(meta_tokens=meta_tokens, ln_emb_g=ln_emb_g, ln_emb_b=ln_emb_b, w_in=w_in, b_in=b_in, conv_w=conv_w,
             conv_b=conv_b, w_ra=w_ra, b_ra=b_ra, w_ri=w_ri, b_ri=b_ri, lru_lambda=lru_lambda, sinks=sinks,
             w_rnn_out=w_rnn_out, w_attn_out=w_attn_out, w_o=w_o, b_o=b_o, ln_g=ln_g, ln_b=ln_b)
    m = dict(meta_tokens=m_meta_tokens, ln_emb_g=m_ln_emb_g, ln_emb_b=m_ln_emb_b, w_in=m_w_in, b_in=m_b_in,
             conv_w=m_conv_w, conv_b=m_conv_b, w_ra=m_w_ra, b_ra=m_b_ra, w_ri=m_w_ri, b_ri=m_b_ri,
             lru_lambda=m_lru_lambda, sinks=m_sinks, w_rnn_out=m_w_rnn_out, w_attn_out=m_w_attn_out, w_o=m_w_o,
             b_o=m_b_o, ln_g=m_ln_g, ln_b=m_ln_b)
    v = dict(meta_tokens=v_meta_tokens, ln_emb_g=v_ln_emb_g, ln_emb_b=v_ln_emb_b, w_in=v_w_in, b_in=v_b_in,
             conv_w=v_conv_w, conv_b=v_conv_b, w_ra=v_w_ra, b_ra=v_b_ra, w_ri=v_w_ri, b_ri=v_b_ri,
             lru_lambda=v_lru_lambda, sinks=v_sinks, w_rnn_out=v_w_rnn_out, w_attn_out=v_w_attn_out, w_o=v_w_o,
             b_o=v_b_o, ln_g=v_ln_g, ln_b=v_ln_b)
    px, py, pc = _place()
    as_idx = lambda t: jnp.reshape(t, (1,)).astype(jnp.int32)
    c_idx, q_idx, me_idx = as_idx(pc), as_idx(2 * px + py), as_idx(_dev(px, py, pc))

    w3_s, wrg_s, small_s = _cast_small(w_rnn_out, w_attn_out, w_o, w_ra, w_ri, meta_tokens, conv_w)
    wg, wrg, smallw, w3_land = _all_gather([_cast_w_in(w_in), wrg_s, small_s], [w3_s])
    w3_pending = _direct_start([w3_s], [w3_land], smallw, True, "gather_w3_start")
    w_full = _relayout_w_in(wg)

    vec = lambda name: w[name].reshape(1, -1)
    p = {k: vec(k) for k in ("ln_emb_g", "ln_emb_b", "b_in", "conv_b", "b_ra", "b_ri", "lru_lambda", "sinks",
                             "b_o", "ln_g", "ln_b")}
    p["b_in"] = p["b_in"] + w3_pending[4][0:1, 0:1]
    s = _step_branches(x, w_full, wrg, smallw, p)
    w3 = _direct_wait(*w3_pending[:4], s["lse"], True, "gather_w3_wait")[1][0]
    t = _step_merge(s, loss_target, w3, p)
    loss = lax.psum(jnp.sum(t["st_out"][3]), ("x", "y", "c"))

    proj = ("w_o", "w_rnn_out", "w_attn_out")
    g_proj = [t[k].reshape(N_DEV, 256, D) for k in ("g_wo", "g_wrnn", "g_wattn")]
    g_pending = _direct_start(g_proj, [lax.empty((N_DEV, 256, D), bf16) for _ in proj], p["b_o"], False,
                              "reduce_proj_start")
    u = _step_backward(s, t, x, w_full, wrg, smallw, p, p["conv_b"] + g_pending[4][0:1, 0:1])
    g_proj, g_land = _direct_wait(*g_pending[:4], u["grad_x"], False, "reduce_proj_wait")

    grads = [u["g_win"], u["g_wrg"].reshape(N_DEV, 2 * RNN_BLOCK, RNN_BLOCK)]
    r1 = _exchange_siblings(grads)
    parts = [_pair_sum(g, r, c_idx, "pair_sum_%d" % i) for i, (g, r) in enumerate(zip(grads, r1))]
    r2 = _exchange_chips(parts)
    big = {}
    two_d = lambda name: (w[name].shape[-2], w[name].shape[-1])
    res = _adamw_big(parts[0], r2[0], q_idx, w["w_in"].reshape(two_d("w_in")), m["w_in"].reshape(two_d("w_in")),
                     v["w_in"].reshape(two_d("w_in")), "adamw_w_in")
    big["w_in"] = tuple(r.reshape(w["w_in"].shape) for r in res)
    for i, name in enumerate(proj):
        res = _adamw_direct(g_proj[i], g_land[i], me_idx, w[name].reshape(two_d(name)), m[name].reshape(two_d(name)),
                            v[name].reshape(two_d(name)), "adamw_" + name)
        big[name] = tuple(r.reshape(w[name].shape) for r in res)
    for i, name in enumerate(("w_ra", "w_ri")):
        sq = (RNN_BLOCK, RNN_BLOCK)
        res = _adamw_big(parts[1], r2[1], q_idx, w[name].reshape(sq), m[name].reshape(sq), v[name].reshape(sq),
                         "adamw_" + name, row_off=i)
        big[name] = tuple(r.reshape(w[name].shape) for r in res)

    loc = {**t, **u}
    sm, sm2 = _pack_small(loc["st_emb"], loc["vec_rnn"], loc["st_out"], loc["dsr"], loc["db_in"], loc["dmeta"])
    sm, sm2 = _small_allreduce(sm, sm2)
    sm2_mine = lax.dynamic_index_in_dim(sm2, _dev(px, py, pc), 0, keepdims=False)
    two = lambda name, t: t.reshape(_SMALL_2D.get(name, (1, D)))
    small = _small_update(sm, sm2_mine, {k: (two(k, w[k]), two(k, m[k]), two(k, v[k])) for k in _SMALL_NAMES})
    res = dict(big)
    for k in _SMALL_NAMES:
        res[k] = tuple(t.reshape(w[k].shape) for t in small[k])

    outs = [loss, loc["grad_x"]]
    for j in range(4):
        outs += [res[k][j] for k in _WEIGHTS]
    return tuple(outs)
```

```python
import functools

import jax
import jax.numpy as jnp
from jax import lax
from jax.experimental import pallas as pl
from jax.experimental.pallas import tpu as pltpu

f32, bf16 = jnp.float32, jnp.bfloat16
SDS = jax.ShapeDtypeStruct

N_DEV = 8
D = 2048
N_META = 16
BLK = 128
ROW0 = BLK - N_META
N_RNN_BLOCKS = 8
RNN_BLOCK = D // N_RNN_BLOCKS
CONV_WIDTH = 4
LRU_C = 8.0
HEAD_DIM = 64
N_KV = 4
GROUP = 8
HALF = HEAD_DIM // 2
ROPE_THETA = 10000.0
NEG_INF = -1e30
LN_EPS = 1e-5
ALPHA = 2.0 ** 0.25
D_IN = 12800
SHARD_IN = D_IN // N_DEV
OFF_GR, OFF_Q, OFF_K, OFF_V, OFF_GA, OFF_G = 2048, 4096, 6144, 6400, 6656, 8704
ADAM_LR, ADAM_B1, ADAM_B2, ADAM_EPS, ADAM_WD, ADAM_STEP = 1e-3, 0.9, 0.999, 1e-8, 0.01, 10
VMEM_LIMIT_MB = 56
MESH = pl.DeviceIdType.MESH


def _cp(sem=None, vmem_mb=40):
    return pltpu.CompilerParams(dimension_semantics=sem, vmem_limit_bytes=vmem_mb * 2 ** 20)


def _row_chunk(m):
    best = 16
    for c in range(16, 641, 16):
        if m % c == 0:
            best = c
    return best


def _sigmoid(x):
    return 1.0 / (1.0 + jnp.exp(-x))


def _silu_and_grad(x):
    s = _sigmoid(x)
    return x * s, s * (1.0 + x * (1.0 - s))


def _log_sigmoid(x):
    return jnp.minimum(x, 0.0) - jnp.log1p(jnp.exp(-jnp.abs(x)))


def _ln_rows(v, g, b):
    mu = jnp.mean(v, axis=-1, keepdims=True)
    c = v - mu
    var = jnp.mean(c * c, axis=-1, keepdims=True)
    rstd = lax.rsqrt(var + LN_EPS)
    xhat = c * rstd
    return xhat * g + b, xhat, rstd


def _ln_rows_bwd(dy, g, xhat, rstd):
    dxh = dy * g
    m1 = jnp.mean(dxh, axis=-1, keepdims=True)
    m2 = jnp.mean(dxh * xhat, axis=-1, keepdims=True)
    return rstd * (dxh - m1 - xhat * m2)


def _colsum(v):
    return jnp.sum(v, axis=0, keepdims=True)


def _dot(a, b):
    return jnp.dot(a, b, preferred_element_type=f32)


def _dot_nt(a, b):
    return lax.dot_general(a, b, (((1,), (1,)), ((), ())), preferred_element_type=f32)


def _dot_tn(a, b):
    return lax.dot_general(a, b, (((0,), (0,)), ((), ())), preferred_element_type=f32)


def _meta_full(sw_ref):
    return jnp.concatenate([sw_ref[s, 0:N_META, :] for s in range(N_DEV)], axis=1)


def _ln_emb(x, smallw, g_e, b_e):
    seq = x.shape[1]
    rows = seq + BLK
    nb = rows // BLK

    def body(x_ref, sw_ref, g_ref, b_ref, h32_ref, hb_ref, ht_ref):
        i = pl.program_id(0)
        g, b = g_ref[...], b_ref[...]

        def emit(blk):
            h32_ref[...] = blk
            hb_ref[...] = blk.astype(bf16)
            ht_ref[...] = blk.T.astype(bf16)

        @pl.when(i == 0)
        def _():
            hm = _ln_rows(_meta_full(sw_ref), g, b)[0]
            emit(jnp.concatenate([jnp.zeros((ROW0, D), f32), hm], axis=0))

        @pl.when(i > 0)
        def _():
            emit(_ln_rows(x_ref[0], g, b)[0])

    return pl.pallas_call(
        body, grid=(nb,),
        in_specs=[pl.BlockSpec((1, BLK, D), lambda i: (0, jnp.maximum(i - 1, 0), 0)),
                  pl.BlockSpec((N_DEV, 24, 256), lambda i: (0, 0, 0)),
                  pl.BlockSpec((1, D), lambda i: (0, 0)),
                  pl.BlockSpec((1, D), lambda i: (0, 0))],
        out_specs=[pl.BlockSpec((BLK, D), lambda i: (i, 0)),
                   pl.BlockSpec((BLK, D), lambda i: (i, 0)),
                   pl.BlockSpec((D, BLK), lambda i: (0, i))],
        out_shape=[SDS((rows, D), f32), SDS((rows, D), bf16), SDS((D, rows), bf16)],
        name="ln_emb", compiler_params=_cp(("arbitrary",)),
    )(x, smallw, g_e, b_e)


def _ln_emb_bwd(dh, du32, x, smallw, g_e):
    seq = x.shape[1]
    rows = seq + BLK
    nb = rows // BLK

    def body(dh_ref, du_ref, x_ref, sw_ref, g_ref, gx_ref, dmeta_ref, st_ref):
        i = pl.program_id(0)
        g = g_ref[...]
        dht = dh_ref[...] + ALPHA * du_ref[...]

        @pl.when(i == 0)
        def _():
            v = jnp.concatenate([jnp.zeros((ROW0, D), f32), _meta_full(sw_ref)], axis=0)
            valid = lax.broadcasted_iota(jnp.int32, (BLK, 1), 0) >= ROW0
            d = jnp.where(valid, dht, 0.0)
            _, xhat, rstd = _ln_rows(v, g, 0.0)
            dv = _ln_rows_bwd(d, g, xhat, rstd)
            dmeta_ref[...] = dv[ROW0:, :]
            st_ref[...] = jnp.concatenate([_colsum(d * xhat), _colsum(d), jnp.zeros((6, D), f32)], axis=0)

        @pl.when(i > 0)
        def _():
            _, xhat, rstd = _ln_rows(x_ref[0], g, 0.0)
            gx_ref[0] = _ln_rows_bwd(dht, g, xhat, rstd)
            st_ref[0:1, :] += _colsum(dht * xhat)
            st_ref[1:2, :] += _colsum(dht)

    return pl.pallas_call(
        body, grid=(nb,),
        in_specs=[pl.BlockSpec((BLK, D), lambda i: (i, 0)),
                  pl.BlockSpec((BLK, D), lambda i: (i, 0)),
                  pl.BlockSpec((1, BLK, D), lambda i: (0, jnp.maximum(i - 1, 0), 0)),
                  pl.BlockSpec((N_DEV, 24, 256), lambda i: (0, 0, 0)),
                  pl.BlockSpec((1, D), lambda i: (0, 0))],
        out_specs=[pl.BlockSpec((1, BLK, D), lambda i: (0, jnp.maximum(i - 1, 0), 0)),
                   pl.BlockSpec((N_META, D), lambda i: (0, 0)),
                   pl.BlockSpec((8, D), lambda i: (0, 0))],
        out_shape=[SDS((1, seq, D), f32), SDS((N_META, D), f32), SDS((8, D), f32)],
        name="ln_emb_bwd", compiler_params=_cp(("arbitrary",)),
    )(dh, du32, x, smallw, g_e)


def _mm(a, b, *, name, nt=False, sel=None, bias=None, out_dtype=f32, tn=512):
    m, k = a.shape
    cm = _row_chunk(m)
    stacked = sel is not None
    if stacked:
        n = D
        if nt:
            b_spec = pl.BlockSpec((tn // 256, None, 256, D), lambda j: (j, sel, 0, 0))
        else:
            b_spec = pl.BlockSpec((N_DEV, None, 256, tn), lambda j: (0, sel, 0, j))
    elif nt:
        n = b.shape[0]
        b_spec = pl.BlockSpec((tn, k), lambda j: (j, 0))
    else:
        n = b.shape[1]
        b_spec = pl.BlockSpec((k, tn), lambda j: (0, j))
    in_specs = [pl.BlockSpec((m, k), lambda j: (0, 0)), b_spec]
    args = [a, b]
    if bias is not None:
        in_specs.append(pl.BlockSpec((1, tn), lambda j: (0, j)))
        args.append(bias)

    def body(*refs):
        a_ref, b_ref, o_ref = refs[0], refs[1], refs[-1]
        bm = b_ref[...]
        if stacked:
            bm = bm.reshape((tn, D) if nt else (D, tn))
        for c in range(m // cm):
            acc = (_dot_nt if nt else _dot)(a_ref[c * cm:(c + 1) * cm, :], bm)
            if bias is not None:
                acc = acc + refs[2][...]
            o_ref[c * cm:(c + 1) * cm, :] = acc.astype(out_dtype)

    return pl.pallas_call(
        body, grid=(n // tn,), in_specs=in_specs,
        out_specs=pl.BlockSpec((m, tn), lambda j: (0, j)),
        out_shape=SDS((m, n), out_dtype), name=name, compiler_params=_cp(("arbitrary",), 48),
    )(*args)


def _mm_dh(dz, w_full, after):
    rows = dz.shape[0]
    tk, tn = 1280, 1024
    cm = _row_chunk(rows)

    def body(a_ref, w_ref, after_ref, o_ref):
        kk = pl.program_id(1)
        for c in range(rows // cm):
            acc = _dot_nt(a_ref[c * cm:(c + 1) * cm, :], w_ref[...])

            @pl.when(kk == 0)
            def _():
                o_ref[c * cm:(c + 1) * cm, :] = acc

            @pl.when(kk > 0)
            def _():
                o_ref[c * cm:(c + 1) * cm, :] += acc

    return pl.pallas_call(
        body, grid=(D // tn, D_IN // tk),
        in_specs=[pl.BlockSpec((rows, tk), lambda j, kk: (0, kk)),
                  pl.BlockSpec((tn, tk), lambda j, kk: (j, kk)),
                  pl.BlockSpec(memory_space=pl.ANY)],
        out_specs=pl.BlockSpec((rows, tn), lambda j, kk: (0, j)),
        out_shape=SDS((rows, D), f32), name="mm_dh", compiler_params=_cp(("arbitrary", "arbitrary"), 48),
    )(dz, w_full, after)


def _mm_dwin(h_t, dz):
    rows = dz.shape[0]
    pair = 2 * SHARD_IN
    tm = 512

    def body(a_ref, dz_ref, o_ref, db_ref):
        acc = _dot(a_ref[...], dz_ref[...])
        o_ref[0] = acc[:, :SHARD_IN].astype(bf16)
        o_ref[1] = acc[:, SHARD_IN:].astype(bf16)

        @pl.when(pl.program_id(1) == 0)
        def _():
            def step(i, s):
                blk = dz_ref[pl.ds(pl.multiple_of(i * BLK, BLK), BLK), :].astype(f32)
                return s + blk.reshape(BLK // 8, 8, pair).sum(axis=0)
            s = lax.fori_loop(0, rows // BLK, step, jnp.zeros((8, pair), f32))
            db_ref[...] = jnp.broadcast_to(_colsum(s), (8, pair))

    return pl.pallas_call(
        body, grid=(N_DEV // 2, D // tm),
        in_specs=[pl.BlockSpec((tm, rows), lambda p, i: (i, 0)),
                  pl.BlockSpec((rows, pair), lambda p, i: (0, p))],
        out_specs=[pl.BlockSpec((2, tm, SHARD_IN), lambda p, i: (p, i, 0)),
                   pl.BlockSpec((8, pair), lambda p, i: (0, p))],
        out_shape=[SDS((N_DEV, D, SHARD_IN), bf16), SDS((8, D_IN), f32)],
        name="mm_dwin", compiler_params=_cp(("arbitrary", "arbitrary"), VMEM_LIMIT_MB),
    )(h_t, dz)


def _transpose(x, name):
    rows, cols = x.shape

    def body(x_ref, o_ref):
        o_ref[...] = x_ref[...].astype(f32).T.astype(bf16)

    return pl.pallas_call(
        body, grid=(rows // BLK,),
        in_specs=[pl.BlockSpec((BLK, cols), lambda i: (i, 0))],
        out_specs=pl.BlockSpec((cols, BLK), lambda i: (0, i)),
        out_shape=SDS((cols, rows), bf16), name=name, compiler_params=_cp(("arbitrary",)),
    )(x)


def _relayout_w_in(wg):
    tm = 256

    def body(i_ref, o_ref):
        for d in range(N_DEV):
            o_ref[:, d * SHARD_IN:(d + 1) * SHARD_IN] = i_ref[d]

    return pl.pallas_call(
        body, grid=(D // tm,),
        in_specs=[pl.BlockSpec((N_DEV, tm, SHARD_IN), lambda i: (0, i, 0))],
        out_specs=pl.BlockSpec((tm, D_IN), lambda i: (i, 0)),
        out_shape=SDS((D, D_IN), bf16), name="relayout_w_in", compiler_params=_cp(("arbitrary",)),
    )(wg)


def _scan8(a, b, reverse):
    idx = lax.broadcasted_iota(jnp.int32, a.shape, 0)
    for s in (1, 2, 4):
        sh = 8 - s if reverse else s
        a_sh, b_sh = pltpu.roll(a, sh, 0), pltpu.roll(b, sh, 0)
        m = (idx < 8 - s) if reverse else (idx >= s)
        b = jnp.where(m, a * b_sh + b, b)
        a = jnp.where(m, a * a_sh, a)
    return a, b


def _shift_rows(prev8, cur, k):
    ext = jnp.concatenate([prev8, cur], axis=0)
    return pltpu.roll(ext, k, 0)[8:, :]


def _gates(xc, w_ra, b_ra, w_ri, b_ri, ls):
    xb = xc.astype(bf16)
    r = _sigmoid(_dot(xb, w_ra) + b_ra)
    ig = _sigmoid(_dot(xb, w_ri) + b_ri)
    la = LRU_C * r * ls
    a = jnp.exp(la)
    mult = jnp.sqrt(jnp.tanh(-la) * (1.0 + a * a))
    return xb, r, ig, a, mult


_RNN_IN_SPECS = lambda rows: [
    pl.BlockSpec((1, 24, 256), lambda n: (n, 0, 0)),
    pl.BlockSpec((1, RNN_BLOCK), lambda n: (0, n)),
    pl.BlockSpec((N_DEV, 2, None, 32, RNN_BLOCK), lambda n: (0, 0, n, 0, 0)),
    pl.BlockSpec((1, RNN_BLOCK), lambda n: (0, n)),
    pl.BlockSpec((1, RNN_BLOCK), lambda n: (0, n)),
    pl.BlockSpec((1, RNN_BLOCK), lambda n: (0, n)),
]


def _rnn_fwd(z, smallw, conv_b, wrg, b_ra, b_ri, lam):
    rows = z.shape[0]
    nb = rows // BLK
    col = lambda off: pl.BlockSpec((rows, RNN_BLOCK), lambda n: (0, off // RNN_BLOCK + n))

    def body(xr_ref, gr_ref, sw_ref, cb_ref, w_ref, bra_ref, bri_ref, lam_ref, xc_ref, hr_ref, ya_ref, a_s):
        cw = sw_ref[0, N_META:24, :]
        cb = cb_ref[...]
        w_ra = w_ref[:, 0].reshape(RNN_BLOCK, RNN_BLOCK)
        w_ri = w_ref[:, 1].reshape(RNN_BLOCK, RNN_BLOCK)
        b_ra_v, b_ri_v = bra_ref[...], bri_ref[...]
        ls = _log_sigmoid(lam_ref[...])
        rid = lax.broadcasted_iota(jnp.int32, (BLK, 1), 0)

        def blk_step(i, carry):
            r0 = pl.multiple_of(i * BLK, BLK)
            grow = rid + r0
            valid = grow >= ROW0
            cur = jnp.where(valid, xr_ref[pl.ds(r0, BLK), :], 0.0)
            prev8 = xr_ref[pl.ds(pl.multiple_of(jnp.maximum(r0 - 8, 0), 8), 8), :] * (i > 0).astype(f32)
            xc = cb + cw[0:1] * cur
            for k in range(1, CONV_WIDTH):
                xc = xc + cw[k:k + 1] * _shift_rows(prev8, cur, k)
            xc_ref[pl.ds(r0, BLK), :] = xc
            _, _, ig, a, mult = _gates(xc, w_ra, b_ra_v, w_ri, b_ri_v, ls)
            mult = jnp.where(grow == ROW0, 1.0, mult)
            a_s[pl.ds(r0, BLK), :] = a
            hr_ref[pl.ds(r0, BLK), :] = jnp.where(valid, mult * ig * xc, 0.0)
            return carry

        lax.fori_loop(0, nb, blk_step, 0)

        def scan_step(j, carry):
            r0 = pl.multiple_of(j * 8, 8)
            a, b = _scan8(a_s[pl.ds(r0, 8), :], hr_ref[pl.ds(r0, 8), :], False)
            h = b + a * carry
            hr_ref[pl.ds(r0, 8), :] = h
            return jnp.broadcast_to(h[7:8, :], (8, RNN_BLOCK))

        lax.fori_loop(0, rows // 8, scan_step, jnp.zeros((8, RNN_BLOCK), f32))

        def gate_step(i, carry):
            r0 = pl.multiple_of(i * BLK, BLK)
            ya_ref[pl.ds(r0, BLK), :] = (hr_ref[pl.ds(r0, BLK), :]
                                         * _silu_and_grad(gr_ref[pl.ds(r0, BLK), :])[0]).astype(bf16)
            return carry

        lax.fori_loop(0, nb, gate_step, 0)

    return pl.pallas_call(
        body, grid=(N_RNN_BLOCKS,),
        in_specs=[col(0), col(OFF_GR)] + _RNN_IN_SPECS(rows),
        out_specs=[pl.BlockSpec((rows, RNN_BLOCK), lambda n: (0, n))] * 3,
        out_shape=[SDS((rows, D), f32), SDS((rows, D), f32), SDS((rows, D), bf16)],
        scratch_shapes=[pltpu.VMEM((rows, RNN_BLOCK), f32)],
        name="rnn_fwd", compiler_params=_cp(("arbitrary",)),
    )(z, z, smallw, conv_b, wrg, b_ra, b_ri, lam)


def _rnn_bwd(dya, hr, xc, z, smallw, conv_b, wrg, b_ra, b_ri, lam):
    rows = z.shape[0]
    nb = rows // BLK
    col = lambda off: pl.BlockSpec((rows, RNN_BLOCK), lambda n: (0, off // RNN_BLOCK + n))
    blk = pl.BlockSpec((rows, RNN_BLOCK), lambda n: (0, n))

    def body(dya_ref, hr_ref, xc_ref, xr_ref, gr_ref, sw_ref, cb_ref, w_ref, bra_ref, bri_ref, lam_ref,
             dxr_ref, dgr_ref, dw_ref, vec_ref, a_s, lam_s, dxc_s, dw_s):
        cw = sw_ref[0, N_META:24, :]
        w_ra = w_ref[:, 0].reshape(RNN_BLOCK, RNN_BLOCK)
        w_ri = w_ref[:, 1].reshape(RNN_BLOCK, RNN_BLOCK)
        b_ra_v, b_ri_v = bra_ref[...], bri_ref[...]
        lam_v = lam_ref[...]
        ls = _log_sigmoid(lam_v)
        rid = lax.broadcasted_iota(jnp.int32, (BLK, 1), 0)
        zrow = jnp.zeros((1, RNN_BLOCK), f32)

        def p1(i, carry):
            r0 = pl.multiple_of(i * BLK, BLK)
            sl = pl.ds(r0, BLK)
            a = _gates(xc_ref[sl, :], w_ra, b_ra_v, w_ri, b_ri_v, ls)[3]
            a_s[sl, :] = a
            sg, dsg = _silu_and_grad(gr_ref[sl, :])
            d = dya_ref[sl, :]
            lam_s[sl, :] = d * sg
            dgr_ref[sl, :] = (d * hr_ref[sl, :] * dsg).astype(bf16)
            return carry

        lax.fori_loop(0, nb, p1, 0)

        def p2(jj, carry):
            j = rows // 8 - 1 - jj
            sl = pl.ds(pl.multiple_of(j * 8, 8), 8)
            a = a_s[sl, :]
            g = lam_s[sl, :]
            ca, cb_ = _scan8(a, a * g, True)
            mu = cb_ + ca * carry
            idx = lax.broadcasted_iota(jnp.int32, a.shape, 0)
            lam_s[sl, :] = g + jnp.where(idx < 7, pltpu.roll(mu, 7, 0), carry)
            return jnp.broadcast_to(mu[0:1, :], (8, RNN_BLOCK))

        lax.fori_loop(0, rows // 8, p2, jnp.zeros((8, RNN_BLOCK), f32))

        dw_s[...] = jnp.zeros_like(dw_s)

        def p3(i, carry):
            d_bra, d_bri, d_ls = carry
            r0 = pl.multiple_of(i * BLK, BLK)
            sl = pl.ds(r0, BLK)
            grow = rid + r0
            valid = grow >= ROW0
            first = grow == ROW0
            xcv = xc_ref[sl, :]
            xb, r, ig, a, mult = _gates(xcv, w_ra, b_ra_v, w_ri, b_ri_v, ls)
            mult = jnp.where(first, 1.0, mult)
            lam_t = lam_s[sl, :]
            du = jnp.where(valid, lam_t, 0.0)
            hprev = _shift_rows(hr_ref[pl.ds(pl.multiple_of(jnp.maximum(r0 - 8, 0), 8), 8), :] * (i > 0).astype(f32), hr_ref[sl, :], 1)
            da = lam_t * hprev
            dmult = jnp.where(first, 0.0, du * ig * xcv)
            di = du * mult * xcv
            dxc = du * mult * ig
            ratio = jnp.where(valid & jnp.logical_not(first), a * a / mult, 0.0)
            dla = da * a - dmult * ratio
            dpr = (dla * (LRU_C * ls)) * r * (1.0 - r)
            dpi = di * ig * (1.0 - ig)
            dprb, dpib = dpr.astype(bf16), dpi.astype(bf16)
            dw_s[0] += _dot_tn(xb, dprb)
            dw_s[1] += _dot_tn(xb, dpib)
            dxc_s[sl, :] = dxc + _dot_nt(dprb, w_ra) + _dot_nt(dpib, w_ri)
            return d_bra + _colsum(dpr), d_bri + _colsum(dpi), d_ls + _colsum(dla * (LRU_C * r))

        d_bra, d_bri, d_ls = lax.fori_loop(0, nb, p3, (zrow, zrow, zrow))

        def p4(i, carry):
            d_cb, d_w0, d_w1, d_w2, d_w3 = carry
            r0 = pl.multiple_of(i * BLK, BLK)
            sl = pl.ds(r0, BLK)
            grow = rid + r0
            valid = grow >= ROW0
            dxc = dxc_s[sl, :]
            nxt = dxc_s[pl.ds(pl.multiple_of(jnp.minimum(r0 + BLK, rows - 8), 8), 8), :] * (i < nb - 1).astype(f32)
            ext = jnp.concatenate([dxc, nxt], axis=0)
            dxr = cw[0:1] * dxc
            for k in range(1, CONV_WIDTH):
                dxr = dxr + cw[k:k + 1] * pltpu.roll(ext, BLK + 8 - k, 0)[:BLK, :]
            dxr_ref[sl, :] = jnp.where(valid, dxr, 0.0).astype(bf16)
            cur = jnp.where(valid, xr_ref[sl, :], 0.0)
            prev8 = xr_ref[pl.ds(pl.multiple_of(jnp.maximum(r0 - 8, 0), 8), 8), :] * (i > 0).astype(f32)
            dws = [d_w0 + _colsum(dxc * cur)]
            for k, acc in ((1, d_w1), (2, d_w2), (3, d_w3)):
                dws.append(acc + _colsum(dxc * _shift_rows(prev8, cur, k)))
            return (d_cb + _colsum(dxc), *dws)

        d_cb, d_w0, d_w1, d_w2, d_w3 = lax.fori_loop(0, nb, p4, (zrow,) * 5)

        d_lam = d_ls * _sigmoid(-lam_v)
        vec_ref[...] = jnp.concatenate([d_bra, d_bri, d_lam, d_cb, d_w0, d_w1, d_w2, d_w3], axis=0)
        dw_ref[:, 0] = dw_s[0].astype(bf16).reshape(N_DEV, 32, RNN_BLOCK)
        dw_ref[:, 1] = dw_s[1].astype(bf16).reshape(N_DEV, 32, RNN_BLOCK)

    return pl.pallas_call(
        body, grid=(N_RNN_BLOCKS,),
        in_specs=[blk, blk, blk, col(0), col(OFF_GR)] + _RNN_IN_SPECS(rows),
        out_specs=[blk, blk,
                   pl.BlockSpec((N_DEV, 2, None, 32, RNN_BLOCK), lambda n: (0, 0, n, 0, 0)),
                   pl.BlockSpec((8, RNN_BLOCK), lambda n: (0, n))],
        out_shape=[SDS((rows, D), bf16), SDS((rows, D), bf16),
                   SDS((N_DEV, 2, N_RNN_BLOCKS, 32, RNN_BLOCK), bf16), SDS((8, D), f32)],
        scratch_shapes=[pltpu.VMEM((rows, RNN_BLOCK), f32), pltpu.VMEM((rows, RNN_BLOCK), f32),
                        pltpu.VMEM((rows, RNN_BLOCK), f32), pltpu.VMEM((2, RNN_BLOCK, RNN_BLOCK), f32)],
        name="rnn_bwd", compiler_params=_cp(("arbitrary",), 48),
    )(dya, hr, xc, z, z, smallw, conv_b, wrg, b_ra, b_ri, lam)


def _rope_tables(rows):
    half = jnp.arange(HALF, dtype=f32)
    inv = ROPE_THETA ** (-half / HALF)
    pos = (jnp.arange(rows) - ROW0).astype(f32)
    ang = pos[:, None] * inv[None, :]
    cos, sin = jnp.cos(ang), jnp.sin(ang)
    cos128 = jnp.concatenate([cos, cos, cos, cos], axis=1)
    sin128 = jnp.concatenate([-sin, sin, -sin, sin], axis=1)
    return cos128, sin128


def _rope128(x, cos128, sin128):
    lane = lax.broadcasted_iota(jnp.int32, x.shape, 1)
    swapped = jnp.where(lane % HEAD_DIM < HALF, pltpu.roll(x, 128 - HALF, 1), pltpu.roll(x, HALF, 1))
    return x * cos128 + swapped * sin128


def _qkv_prep(z, cos128, sin128):
    rows = z.shape[0]

    def body(q_ref, kv_ref, c_ref, s_ref, qo_ref, ko_ref, vo_ref):
        c, s = c_ref[...], s_ref[...]
        for g in range(D // 128):
            qo_ref[:, g * 128:(g + 1) * 128] = (_rope128(q_ref[:, g * 128:(g + 1) * 128], c, s)
                                                * (HEAD_DIM ** -0.5)).astype(bf16)
        for g in range(2):
            kr = _rope128(kv_ref[:, g * 128:(g + 1) * 128], c, s)
            for j in range(2):
                ko_ref[2 * g + j] = kr[:, j * HEAD_DIM:(j + 1) * HEAD_DIM].astype(bf16)
        for h in range(N_KV):
            vo_ref[h] = kv_ref[:, 256 + h * HEAD_DIM:256 + (h + 1) * HEAD_DIM].astype(bf16)

    return pl.pallas_call(
        body, grid=(rows // BLK,),
        in_specs=[pl.BlockSpec((BLK, D), lambda i: (i, OFF_Q // D)),
                  pl.BlockSpec((BLK, 512), lambda i: (i, OFF_K // 512)),
                  pl.BlockSpec((BLK, 128), lambda i: (i, 0)),
                  pl.BlockSpec((BLK, 128), lambda i: (i, 0))],
        out_specs=[pl.BlockSpec((BLK, D), lambda i: (i, 0)),
                   pl.BlockSpec((N_KV, BLK, HEAD_DIM), lambda i: (0, i, 0)),
                   pl.BlockSpec((N_KV, BLK, HEAD_DIM), lambda i: (0, i, 0))],
        out_shape=[SDS((rows, D), bf16), SDS((N_KV, rows, HEAD_DIM), bf16), SDS((N_KV, rows, HEAD_DIM), bf16)],
        name="qkv_prep", compiler_params=_cp(("arbitrary",)),
    )(z, z, cos128, sin128)


def _attn_mask(n):
    qi = n * BLK + lax.broadcasted_iota(jnp.int32, (BLK, 2 * BLK + N_META), 0)
    c = lax.broadcasted_iota(jnp.int32, (BLK, 2 * BLK + N_META), 1)
    jb = (n - 1) * BLK + c
    band = (jb >= BLK) & (jb <= qi) & (qi - jb < BLK)
    meta = (ROW0 + c - 2 * BLK) <= qi
    return ((c < 2 * BLK) & band) | ((c >= 2 * BLK) & meta)


N_KEYS = 2 * BLK + N_META


def _stack_heads(t):
    return jnp.concatenate([t[:, g * HEAD_DIM:(g + 1) * HEAD_DIM] for g in range(GROUP)], axis=0)


def _sink_column(sink_ref, h):
    g = lax.broadcasted_iota(jnp.int32, (GROUP, 1, 1), 0)
    col = jnp.zeros((GROUP, 1, 1), f32)
    for j in range(GROUP):
        col = jnp.where(g == j, sink_ref[h * GROUP + j], col)
    return col


def _kv_specs(last):
    cl = lambda n: jnp.minimum(n, last)
    return [pl.BlockSpec((None, N_META, HEAD_DIM), lambda h, n: (h, ROW0 // N_META, 0)),
            pl.BlockSpec((None, BLK, HEAD_DIM), lambda h, n: (h, jnp.maximum(cl(n) - 1, 0), 0)),
            pl.BlockSpec((None, BLK, HEAD_DIM), lambda h, n: (h, cl(n), 0))]


def _attn_fwd(q_r, k_r, v_b, z, sinks):
    rows = q_r.shape[0]
    nb = rows // BLK

    def body(sink_ref, q_ref, km_ref, kp_ref, kc_ref, vm_ref, vp_ref, vc_ref, ga_ref, o_ref, yb_ref, lse_ref):
        h, n = pl.program_id(0), pl.program_id(1)
        kk = jnp.concatenate([kp_ref[...], kc_ref[...], km_ref[...]], axis=0)
        vv = jnp.concatenate([vp_ref[...], vc_ref[...], vm_ref[...]], axis=0)
        q2 = _stack_heads(q_ref[...])
        s = jnp.where(_attn_mask(n)[None], _dot_nt(q2, kk).reshape(GROUP, BLK, N_KEYS), NEG_INF)
        sink = _sink_column(sink_ref, h)
        m = jnp.maximum(jnp.max(s, axis=-1, keepdims=True), sink)
        p = jnp.exp(s - m)
        den = jnp.sum(p, axis=-1, keepdims=True) + jnp.exp(sink - m)
        o2 = _dot((p / den).astype(bf16).reshape(GROUP * BLK, N_KEYS), vv)
        lse = m + jnp.log(den)
        for g in range(GROUP):
            o_ref[:, g * HEAD_DIM:(g + 1) * HEAD_DIM] = o2[g * BLK:(g + 1) * BLK]
            lse_ref[:, g:g + 1] = lse[g]
        yb_ref[...] = (o_ref[...] * _silu_and_grad(ga_ref[...])[0]).astype(bf16)

    tile = pl.BlockSpec((BLK, 512), lambda h, n: (n, h))
    return pl.pallas_call(
        body, grid=(N_KV, nb),
        in_specs=[pl.BlockSpec(memory_space=pltpu.SMEM), tile] + _kv_specs(nb - 1) + _kv_specs(nb - 1)
                 + [pl.BlockSpec((BLK, 512), lambda h, n: (n, OFF_GA // 512 + h))],
        out_specs=[tile, tile, pl.BlockSpec((None, BLK, GROUP), lambda h, n: (h, n, 0))],
        out_shape=[SDS((rows, D), f32), SDS((rows, D), bf16), SDS((N_KV, rows, GROUP), f32)],
        name="attn_fwd", compiler_params=_cp(("arbitrary", "arbitrary")),
    )(sinks, q_r, k_r, k_r, k_r, v_b, v_b, v_b, z)


def _attn_bwd(dyb, o32, lse, q_r, k_r, v_b, z, sinks):
    rows = q_r.shape[0]
    nb = rows // BLK
    cl = lambda n: jnp.minimum(n, nb - 1)

    def body(sink_ref, dyb_ref, o_ref, lse_ref, q_ref, km_ref, kp_ref, kc_ref, vm_ref, vp_ref, vc_ref, ga_ref,
             dq_ref, dga_ref, dk_ref, dv_ref, dkm_ref, dvm_ref, dsr_ref, ck_s, cv_s):
        h, n = pl.program_id(0), pl.program_id(1)

        @pl.when(n == 0)
        def _():
            dkm_ref[...] = jnp.zeros_like(dkm_ref)
            dvm_ref[...] = jnp.zeros_like(dvm_ref)
            ck_s[...] = jnp.zeros_like(ck_s)
            cv_s[...] = jnp.zeros_like(cv_s)

        @pl.when(n < nb)
        def _():
            kk = jnp.concatenate([kp_ref[...], kc_ref[...], km_ref[...]], axis=0)
            vv = jnp.concatenate([vp_ref[...], vc_ref[...], vm_ref[...]], axis=0)
            sg, dsg = _silu_and_grad(ga_ref[...])
            dyb_v = dyb_ref[...]
            o_v = o_ref[...]
            dga_ref[...] = (dyb_v * o_v * dsg).astype(bf16)
            q2 = _stack_heads(q_ref[...])
            do2 = _stack_heads(dyb_v * sg)
            lse_v = lse_ref[...]
            lse = jnp.concatenate([lse_v[:, g:g + 1] for g in range(GROUP)], axis=0).reshape(GROUP, BLK, 1)
            delta = jnp.sum(do2 * _stack_heads(o_v), axis=-1, keepdims=True).reshape(GROUP, BLK, 1)
            s = jnp.where(_attn_mask(n)[None], _dot_nt(q2, kk).reshape(GROUP, BLK, N_KEYS), NEG_INF)
            p = jnp.exp(s - lse)
            do2b = do2.astype(bf16)
            ds = (p * (_dot_nt(do2b, vv).reshape(GROUP, BLK, N_KEYS) - delta)).astype(bf16)
            ds = ds.reshape(GROUP * BLK, N_KEYS)
            dsr = -jnp.exp(_sink_column(sink_ref, h) - lse) * delta
            dq2 = _dot(ds, kk)
            for g in range(GROUP):
                dq_ref[:, g * HEAD_DIM:(g + 1) * HEAD_DIM] = dq2[g * BLK:(g + 1) * BLK]
                dsr_ref[:, g:g + 1] = dsr[g]
            dkk = _dot_tn(ds, q2)
            dvv = _dot_tn(p.astype(bf16).reshape(GROUP * BLK, N_KEYS), do2b)
            dk_ref[...] = ck_s[...] + dkk[:BLK]
            dv_ref[...] = cv_s[...] + dvv[:BLK]
            ck_s[...] = dkk[BLK:2 * BLK]
            cv_s[...] = dvv[BLK:2 * BLK]
            dkm_ref[...] += dkk[2 * BLK:]
            dvm_ref[...] += dvv[2 * BLK:]

        @pl.when(n == nb)
        def _():
            dk_ref[...] = ck_s[...]
            dv_ref[...] = cv_s[...]

    tile = pl.BlockSpec((BLK, 512), lambda h, n: (cl(n), h))
    kvout = pl.BlockSpec((None, BLK, HEAD_DIM), lambda h, n: (h, jnp.maximum(n - 1, 0), 0))
    mout = pl.BlockSpec((None, N_META, HEAD_DIM), lambda h, n: (h, 0, 0))
    stat = pl.BlockSpec((None, BLK, GROUP), lambda h, n: (h, cl(n), 0))
    return pl.pallas_call(
        body, grid=(N_KV, nb + 1),
        in_specs=[pl.BlockSpec(memory_space=pltpu.SMEM), tile, tile, stat, tile] + _kv_specs(nb - 1)
                 + _kv_specs(nb - 1) + [pl.BlockSpec((BLK, 512), lambda h, n: (cl(n), OFF_GA // 512 + h))],
        out_specs=[tile, tile, kvout, kvout, mout, mout, stat],
        out_shape=[SDS((rows, D), f32), SDS((rows, D), bf16),
                   SDS((N_KV, rows, HEAD_DIM), f32), SDS((N_KV, rows, HEAD_DIM), f32),
                   SDS((N_KV, N_META, HEAD_DIM), f32), SDS((N_KV, N_META, HEAD_DIM), f32),
                   SDS((N_KV, rows, GROUP), f32)],
        scratch_shapes=[pltpu.VMEM((BLK, HEAD_DIM), f32), pltpu.VMEM((BLK, HEAD_DIM), f32)],
        name="attn_bwd", compiler_params=_cp(("arbitrary", "arbitrary")),
    )(sinks, dyb, o32, lse, q_r, k_r, k_r, k_r, v_b, v_b, v_b, z)


def _qkv_finish(dq, dk, dv, dkm, dvm, cos128, sin128):
    rows = dq.shape[0]

    def body(dq_ref, dk_ref, dv_ref, dkm_ref, dvm_ref, c_ref, s_ref, oq_ref, okv_ref):
        first = (pl.program_id(0) == 0).astype(f32)
        c, s = c_ref[...], -s_ref[...]
        for g in range(D // 128):
            oq_ref[:, g * 128:(g + 1) * 128] = (_rope128(dq_ref[:, g * 128:(g + 1) * 128], c, s)
                                                * (HEAD_DIM ** -0.5)).astype(bf16)
        pad = jnp.zeros((ROW0, HEAD_DIM), f32)
        ks = [dk_ref[h] + first * jnp.concatenate([pad, dkm_ref[h]], axis=0) for h in range(N_KV)]
        vs = [dv_ref[h] + first * jnp.concatenate([pad, dvm_ref[h]], axis=0) for h in range(N_KV)]
        for g in range(2):
            kp = jnp.concatenate([ks[2 * g], ks[2 * g + 1]], axis=1)
            okv_ref[:, g * 128:(g + 1) * 128] = _rope128(kp, c, s).astype(bf16)
            okv_ref[:, 256 + g * 128:256 + (g + 1) * 128] = jnp.concatenate([vs[2 * g], vs[2 * g + 1]], axis=1).astype(bf16)

    kv = pl.BlockSpec((N_KV, BLK, HEAD_DIM), lambda i: (0, i, 0))
    mt = pl.BlockSpec((N_KV, N_META, HEAD_DIM), lambda i: (0, 0, 0))
    return pl.pallas_call(
        body, grid=(rows // BLK,),
        in_specs=[pl.BlockSpec((BLK, D), lambda i: (i, 0)), kv, kv, mt, mt,
                  pl.BlockSpec((BLK, 128), lambda i: (i, 0)), pl.BlockSpec((BLK, 128), lambda i: (i, 0))],
        out_specs=[pl.BlockSpec((BLK, D), lambda i: (i, 0)), pl.BlockSpec((BLK, 512), lambda i: (i, 0))],
        out_shape=[SDS((rows, D), bf16), SDS((rows, 512), bf16)],
        name="qkv_finish", compiler_params=_cp(("arbitrary",)),
    )(dq, dk, dv, dkm, dvm, cos128, sin128)


_TW = 512


def _mix_specs(rows):
    tr = _row_chunk(rows)
    tile = pl.BlockSpec((tr, _TW), lambda i, j: (i, j))
    ga = pl.BlockSpec((tr, _TW), lambda i, j: (i, OFF_G // _TW + j))
    gb = pl.BlockSpec((tr, _TW), lambda i, j: (i, (OFF_G + D) // _TW + j))
    return (rows // tr, D // _TW), tile, ga, gb


def _mix_fwd(y_a, y_b, z):
    rows = y_a.shape[0]
    grid, _mix_tile, _mix_ga, _mix_gb = _mix_specs(rows)

    def body(ya_ref, yb_ref, ga_ref, gb_ref, o_ref):
        o_ref[...] = (_sigmoid(ga_ref[...]) * ya_ref[...] + _sigmoid(gb_ref[...]) * yb_ref[...]).astype(bf16)

    return pl.pallas_call(
        body, grid=grid, in_specs=[_mix_tile, _mix_tile, _mix_ga, _mix_gb],
        out_specs=_mix_tile, out_shape=SDS((rows, D), bf16),
        name="mix_fwd", compiler_params=_cp(("arbitrary", "arbitrary")),
    )(y_a, y_b, z, z)


def _mix_bwd(dmixed, y_a, y_b, z):
    rows = y_a.shape[0]
    grid, _mix_tile, _mix_ga, _mix_gb = _mix_specs(rows)

    def body(dm_ref, ya_ref, yb_ref, ga_ref, gb_ref, dya_ref, dyb_ref, dga_ref, dgb_ref):
        dm = dm_ref[...]
        sa, sb = _sigmoid(ga_ref[...]), _sigmoid(gb_ref[...])
        dya_ref[...] = (dm * sa).astype(bf16)
        dyb_ref[...] = (dm * sb).astype(bf16)
        dga_ref[...] = (dm * ya_ref[...] * sa * (1.0 - sa)).astype(bf16)
        dgb_ref[...] = (dm * yb_ref[...] * sb * (1.0 - sb)).astype(bf16)

    return pl.pallas_call(
        body, grid=grid, in_specs=[_mix_tile, _mix_tile, _mix_tile, _mix_ga, _mix_gb],
        out_specs=[_mix_tile] * 4, out_shape=[SDS((rows, D), bf16)] * 4,
        name="mix_bwd", compiler_params=_cp(("arbitrary", "arbitrary")),
    )(dmixed, y_a, y_b, z, z)


def _final_ln(out32, h32, tgt, ln_g, ln_b):
    rows = out32.shape[0]

    def body(o_ref, h_ref, t_ref, g_ref, b_ref, du_ref, dub_ref, st_ref):
        i = pl.program_id(0)
        g = g_ref[...]
        y, xhat, rstd = _ln_rows(ALPHA * h_ref[...] + o_ref[...], g, b_ref[...])
        e = jnp.where(i > 0, y - t_ref[0], 0.0)
        dy = e * (1.0 / D)
        du = _ln_rows_bwd(dy, g, xhat, rstd)
        du_ref[...] = du
        dub_ref[...] = du.astype(bf16)
        st = jnp.concatenate([_colsum(dy * xhat), _colsum(dy), _colsum(du), _colsum(e * e) * (0.5 / D),
                              jnp.zeros((4, D), f32)], axis=0)

        @pl.when(i == 0)
        def _():
            st_ref[...] = st

        @pl.when(i > 0)
        def _():
            st_ref[...] += st

    row = pl.BlockSpec((BLK, D), lambda i: (i, 0))
    vec = pl.BlockSpec((1, D), lambda i: (0, 0))
    return pl.pallas_call(
        body, grid=(rows // BLK,),
        in_specs=[row, row, pl.BlockSpec((1, BLK, D), lambda i: (0, jnp.maximum(i - 1, 0), 0)), vec, vec],
        out_specs=[row, row, pl.BlockSpec((8, D), lambda i: (0, 0))],
        out_shape=[SDS((rows, D), f32), SDS((rows, D), bf16), SDS((8, D), f32)],
        name="final_ln", compiler_params=_cp(("arbitrary",)),
    )(out32, h32, tgt, ln_g, ln_b)


def _assemble_dz(dxr, dgr, dq, dkv, dga, dma, dmb):
    rows = dxr.shape[0]
    parts = [(dxr, D), (dgr, D), (dq, D), (dkv, 512), (dga, D), (dma, D), (dmb, D)]

    def body(*refs):
        o_ref = refs[-1]
        off = 0
        for r, (_, w) in zip(refs[:-1], parts):
            o_ref[:, off:off + w] = r[...]
            off += w

    return pl.pallas_call(
        body, grid=(rows // BLK,),
        in_specs=[pl.BlockSpec((BLK, w), lambda i: (i, 0)) for _, w in parts],
        out_specs=pl.BlockSpec((BLK, D_IN), lambda i: (i, 0)),
        out_shape=SDS((rows, D_IN), bf16), name="assemble_dz", compiler_params=_cp(("arbitrary",)),
    )(*[p for p, _ in parts])


def _step_branches(x, w_full, wrg, smallw, p):
    rows = x.shape[1] + BLK
    cos128, sin128 = _rope_tables(rows)
    sinks = p["sinks"].reshape(N_KV * GROUP)
    h32, hb, h_t = _ln_emb(x, smallw, p["ln_emb_g"], p["ln_emb_b"])
    z = _mm(hb, w_full, bias=p["b_in"], name="mm_z")
    xc, hr, ya = _rnn_fwd(z, smallw, p["conv_b"], wrg, p["b_ra"], p["b_ri"], p["lru_lambda"])
    q_r, k_r, v_b = _qkv_prep(z, cos128, sin128)
    o32, yb, lse = _attn_fwd(q_r, k_r, v_b, z, sinks)
    return dict(cos128=cos128, sin128=sin128, sinks=sinks, h32=h32, h_t=h_t, z=z, xc=xc, hr=hr, ya=ya, q_r=q_r,
                k_r=k_r, v_b=v_b, o32=o32, yb=yb, lse=lse)


def _step_merge(s, tgt, w3, p):
    ya, yb, z = s["ya"], s["yb"], s["z"]
    y_a = _mm(ya, w3, sel=0, name="mm_ya")
    y_b = _mm(yb, w3, sel=1, name="mm_yb")
    mixed = _mix_fwd(y_a, y_b, z)
    out32 = _mm(mixed, w3, sel=2, bias=p["b_o"], name="mm_out")
    du32, dub, st_out = _final_ln(out32, s["h32"], tgt, p["ln_g"], p["ln_b"])

    g_wo = _mm(_transpose(mixed, "t_mixed"), dub, out_dtype=bf16, name="mm_dwo")
    dmixed = _mm(dub, w3, sel=2, nt=True, name="mm_dmixed")
    dya_b, dyb_b, dma, dmb = _mix_bwd(dmixed, y_a, y_b, z)
    g_wrnn = _mm(_transpose(ya, "t_ya"), dya_b, out_dtype=bf16, name="mm_dwrnn")
    g_wattn = _mm(_transpose(yb, "t_yb"), dyb_b, out_dtype=bf16, name="mm_dwattn")
    dya = _mm(dya_b, w3, sel=0, nt=True, name="mm_dya")
    dyb = _mm(dyb_b, w3, sel=1, nt=True, name="mm_dyb")
    return dict(du32=du32, st_out=st_out, dma=dma, dmb=dmb, dya=dya, dyb=dyb, g_wo=g_wo, g_wrnn=g_wrnn,
                g_wattn=g_wattn)


def _step_backward(s, t, wrg, smallw, p, conv_b):
    z = s["z"]
    dxr, dgr, g_wrg, vec_rnn = _rnn_bwd(t["dya"], s["hr"], s["xc"], z, smallw, conv_b, wrg, p["b_ra"], p["b_ri"],
                                        p["lru_lambda"])
    dq_r, dga, dk, dv, dkm, dvm, dsr = _attn_bwd(t["dyb"], s["o32"], s["lse"], s["q_r"], s["k_r"], s["v_b"], z,
                                                 s["sinks"])
    dq, dkv = _qkv_finish(dq_r, dk, dv, dkm, dvm, s["cos128"], s["sin128"])
    dz = _assemble_dz(dxr, dgr, dq, dkv, dga, t["dma"], t["dmb"])
    g_win, db_in = _mm_dwin(s["h_t"], dz)
    return dict(vec_rnn=vec_rnn, dsr=dsr, db_in=db_in, g_win=g_win, g_wrg=g_wrg, dz=dz)


def _step_input_grad(dz, w_full, after, du32, x, smallw, p):
    dh = _mm_dh(dz, w_full, after)
    grad_x, dmeta, st_emb = _ln_emb_bwd(dh, du32, x, smallw, p["ln_emb_g"])
    return dict(grad_x=grad_x, dmeta=dmeta, st_emb=st_emb)


_ANY = pl.BlockSpec(memory_space=pl.ANY)
_VMEM = pl.BlockSpec(memory_space=pltpu.VMEM)


def _place():
    x, y, c = lax.axis_index("x"), lax.axis_index("y"), lax.axis_index("c")
    return x, y, c


def _dev(px, py, pc):
    return 4 * px + 2 * py + pc


def _cast_w_in(w_in):
    tm = 256

    def body(i_ref, o_ref):
        o_ref[...] = i_ref[0].astype(bf16)

    return pl.pallas_call(
        body, grid=(D // tm,),
        in_specs=[pl.BlockSpec((1, tm, SHARD_IN), lambda i: (0, i, 0))],
        out_specs=pl.BlockSpec((tm, SHARD_IN), lambda i: (i, 0)),
        out_shape=SDS((D, SHARD_IN), bf16), name="cast_w_in", compiler_params=_cp(("arbitrary",)),
    )(w_in)


def _cast_small(w_rnn_out, w_attn_out, w_o, w_ra, w_ri, meta, conv_w):
    def body(a_ref, b_ref, c_ref, ra_ref, ri_ref, m_ref, cw_ref, w3_ref, wrg_ref, sw_ref):
        w3_ref[0] = a_ref[0].astype(bf16)
        w3_ref[1] = b_ref[0].astype(bf16)
        w3_ref[2] = c_ref[0].astype(bf16)
        wrg_ref[0] = ra_ref[0].astype(bf16)
        wrg_ref[1] = ri_ref[0].astype(bf16)
        sw_ref[...] = jnp.concatenate([m_ref[...], cw_ref[0], jnp.zeros((4, 256), f32)], axis=0)

    return pl.pallas_call(
        body,
        out_shape=[SDS((3, 256, D), bf16), SDS((2, N_RNN_BLOCKS, 32, RNN_BLOCK), bf16), SDS((24, 256), f32)],
        name="cast_small", compiler_params=_cp(None),
    )(w_rnn_out, w_attn_out, w_o, w_ra, w_ri, meta, conv_w)


def _all_gather(shards, later):
    n = len(shards)
    nl = len(later)

    def body(*refs):
        ins, outs = refs[:n], refs[n + nl:2 * n + nl]
        send_sems, recv_sems, local_sems = refs[2 * (n + nl):]
        x, y, c = _place()
        me, sibling = (x, y, c), (x, y, 1 - c)
        chips = [(1 - x, y), (x, 1 - y), (1 - x, 1 - y)]

        def copy(a, k, block, to, src=None):
            dst = outs[a].at[_dev(*block)]
            return pltpu.make_async_remote_copy(
                src_ref=dst if src is None else src, dst_ref=dst,
                send_sem=send_sems.at[a * 7 + k], recv_sem=recv_sems.at[a * 7 + k],
                device_id=to, device_id_type=MESH)

        all_ins, all_outs = refs[:n + nl], refs[n + nl:2 * (n + nl)]
        mine = [pltpu.make_async_copy(all_ins[a], all_outs[a].at[_dev(*me)], local_sems.at[a]) for a in range(n + nl)]
        for cp in mine:
            cp.start()
        first = []
        for a in range(n):
            first.append(copy(a, 0, me, sibling, src=ins[a]))
            first += [copy(a, 1 + j, me, (*chip, c), src=ins[a]) for j, chip in enumerate(chips)]
        for cp in first:
            cp.start()
        passed = []
        for a in range(n):
            for j, chip in enumerate(chips):
                copy(a, 1 + j, (*chip, c), me).wait_recv()
                cp = copy(a, 4 + j, (*chip, c), sibling)
                cp.start()
                passed.append(cp)
        for a in range(n):
            copy(a, 0, sibling, me).wait_recv()
            for j, chip in enumerate(chips):
                copy(a, 4 + j, (*chip, 1 - c), me).wait_recv()
        for cp in first + passed:
            cp.wait_send()
        for cp in mine:
            cp.wait()

    return pl.pallas_call(
        body, in_specs=[_ANY] * (n + nl), out_specs=[_ANY] * (n + nl),
        out_shape=[SDS((N_DEV, *s.shape), s.dtype) for s in (*shards, *later)],
        scratch_shapes=[pltpu.SemaphoreType.DMA((7 * n,)), pltpu.SemaphoreType.DMA((7 * n,)),
                        pltpu.SemaphoreType.DMA((n + nl,))],
        name="all_gather_weights",
    )(*shards, *later)


_HBM = pl.BlockSpec(memory_space=pltpu.HBM)
_SEM = pl.BlockSpec(memory_space=pltpu.SEMAPHORE)
_PEER_FLIPS = [(f // 4, (f // 2) % 2, f % 2) for f in range(1, N_DEV)]


def _remote(src, dst, send_sems, recv_sems, k, to):
    return pltpu.make_async_remote_copy(src_ref=src, dst_ref=dst, send_sem=send_sems.at[k], recv_sem=recv_sems.at[k],
                                        device_id=to, device_id_type=MESH)


def _copies_direct(same_src):
    def make(srcs, lands, send_sems, recv_sems):
        x, y, c = _place()
        me = _dev(x, y, c)
        out = []
        for a in range(len(srcs)):
            for k, (fx, fy, fc) in enumerate(_PEER_FLIPS):
                peer = ((x + fx) % 2, (y + fy) % 2, (c + fc) % 2)
                src = srcs[a] if same_src else srcs[a].at[_dev(*peer)]
                out.append(_remote(src, lands[a].at[me], send_sems, recv_sems, 7 * a + k, peer))
        return out
    return make


def _copies_siblings(srcs, lands, send_sems, recv_sems):
    x, y, c = _place()
    return [_remote(srcs[a].at[2 * q + (1 - c)], lands[a].at[q], send_sems, recv_sems, 4 * a + q, (x, y, 1 - c))
            for a in range(len(srcs)) for q in range(4)]


def _copies_chips(srcs, lands, send_sems, recv_sems):
    x, y, c = _place()
    chips = [(1 - x, y), (x, 1 - y), (1 - x, 1 - y)]
    return [_remote(srcs[a].at[2 * qx + qy], lands[a].at[j], send_sems, recv_sems, 3 * a + j, (qx, qy, c))
            for a in range(len(srcs)) for j, (qx, qy) in enumerate(chips)]


def _split_start(make, per_array, srcs, lands, dep, name):
    n = len(srcs)

    def body(*refs):
        send_sems, recv_sems, token = refs[2 * n + 1], refs[2 * n + 2], refs[-1]
        for cp in make(refs[:n], refs[n:2 * n], send_sems, recv_sems):
            cp.start()
        token[...] = jnp.zeros_like(token)

    hbm = lambda t: pltpu.with_memory_space_constraint(t, pltpu.HBM)
    res = pl.pallas_call(
        body, name=name,
        out_shape=(pltpu.SemaphoreType.DMA((per_array * n,)), pltpu.SemaphoreType.DMA((per_array * n,)),
                   *[pltpu.HBM(t.shape, t.dtype) for t in (*srcs, *lands)], SDS((8, 128), f32)),
        in_specs=[_HBM] * (2 * n) + [_ANY], out_specs=(_SEM, _SEM, *([_HBM] * (2 * n)), _VMEM),
        input_output_aliases={i: 2 + i for i in range(2 * n)},
        compiler_params=pltpu.CompilerParams(has_side_effects=pltpu.SideEffectType.DATAFLOW_SIDE_EFFECTING),
    )(*[hbm(t) for t in (*srcs, *lands)], dep)
    return res[0], res[1], list(res[2:2 + n]), list(res[2 + n:2 + 2 * n]), res[-1]


def _split_wait(make, send_sems, recv_sems, srcs, lands, after, name):
    n = len(srcs)

    def body(*refs):
        for cp in make(refs[:n], refs[n:2 * n], refs[2 * n], refs[2 * n + 1]):
            cp.wait_send()
            cp.wait_recv()

    res = pl.pallas_call(
        body, name=name,
        out_shape=tuple(pltpu.HBM(t.shape, t.dtype) for t in (*srcs, *lands)),
        in_specs=[_HBM] * (2 * n) + [_SEM, _SEM, _ANY], out_specs=tuple([_HBM] * (2 * n)),
        input_output_aliases={i: i for i in range(2 * n)},
        compiler_params=pltpu.CompilerParams(has_side_effects=pltpu.SideEffectType.DATAFLOW_SIDE_EFFECTING),
    )(*srcs, *lands, send_sems, recv_sems, after)
    return list(res[:n]), list(res[n:])


def _adamw_direct(g, land, me_idx, w, m, v, name):
    r, wd = w.shape
    tr = min(r, 256)

    def body(me_ref, *refs):
        g_ref, peers = refs[0], refs[1:N_DEV]
        w_ref, m_ref, v_ref, g_out, d_out, m_out, v_out = refs[N_DEV:]
        gs = g_ref[...].astype(f32)
        for p_ref in peers:
            gs = gs + p_ref[...].astype(f32)
        d, mn, vn = _adamw(w_ref[...], gs, m_ref[...], v_ref[...])
        g_out[...] = gs
        d_out[...] = d
        m_out[...] = mn
        v_out[...] = vn

    tile = pl.BlockSpec((tr, wd), lambda i, me_ref: (i, 0))
    slot = lambda k: pl.BlockSpec((None, tr, wd), lambda i, me_ref: ((me_ref[0] + k) % N_DEV, i, 0))
    return pl.pallas_call(
        body,
        grid_spec=pltpu.PrefetchScalarGridSpec(
            num_scalar_prefetch=1, grid=(r // tr,),
            in_specs=[slot(0)] + [slot(k) for k in range(1, N_DEV)] + [tile, tile, tile],
            out_specs=[tile] * 4),
        out_shape=[SDS((r, wd), f32)] * 4, name=name, compiler_params=_cp(("arbitrary",), 48),
    )(me_idx, g, *([land] * (N_DEV - 1)), w, m, v)


def _pair_sum(g, r1, c_idx, name):
    _, r, w = g.shape
    tr = min(r, 256)

    def body(c_ref, g_ref, r_ref, o_ref):
        o_ref[...] = (g_ref[...].astype(f32) + r_ref[...].astype(f32)).astype(bf16)

    return pl.pallas_call(
        body,
        grid_spec=pltpu.PrefetchScalarGridSpec(
            num_scalar_prefetch=1, grid=(4, r // tr),
            in_specs=[pl.BlockSpec((None, tr, w), lambda q, i, c_ref: (2 * q + c_ref[0], i, 0)),
                      pl.BlockSpec((None, tr, w), lambda q, i, c_ref: (q, i, 0))],
            out_specs=pl.BlockSpec((None, tr, w), lambda q, i, c_ref: (q, i, 0))),
        out_shape=SDS((4, r, w), bf16), name=name, compiler_params=_cp(("arbitrary", "arbitrary")),
    )(c_idx, g, r1)


def _adamw(w, g, m, v):
    m = ADAM_B1 * m + (1.0 - ADAM_B1) * g
    v = ADAM_B2 * v + (1.0 - ADAM_B2) * (g * g)
    m_hat = m / (1.0 - ADAM_B1 ** ADAM_STEP)
    v_hat = v / (1.0 - ADAM_B2 ** ADAM_STEP)
    delta = -ADAM_LR * (m_hat / (jnp.sqrt(v_hat) + ADAM_EPS) + ADAM_WD * w)
    return delta, m, v


def _adamw_big(part, r2, q_idx, w, m, v, name, row_off=0):
    r, wd = w.shape
    tr = min(r, 256)

    def body(q_ref, p_ref, r_ref, w_ref, m_ref, v_ref, g_out, d_out, m_out, v_out):
        g = p_ref[...].astype(f32)
        for j in range(3):
            g = g + r_ref[j].astype(f32)
        d, mn, vn = _adamw(w_ref[...], g, m_ref[...], v_ref[...])
        g_out[...] = g
        d_out[...] = d
        m_out[...] = mn
        v_out[...] = vn

    tile = pl.BlockSpec((tr, wd), lambda i, q_ref: (i, 0))
    return pl.pallas_call(
        body,
        grid_spec=pltpu.PrefetchScalarGridSpec(
            num_scalar_prefetch=1, grid=(r // tr,),
            in_specs=[pl.BlockSpec((None, tr, wd), lambda i, q_ref: (q_ref[0], row_off + i, 0)),
                      pl.BlockSpec((3, tr, wd), lambda i, q_ref: (0, row_off + i, 0)), tile, tile, tile],
            out_specs=[tile] * 4),
        out_shape=[SDS((r, wd), f32)] * 4, name=name, compiler_params=_cp(("arbitrary",), 48),
    )(q_idx, part, r2, w, m, v)


_SMALL_ROWS = 24


def _pack_small(st_emb, vec_rnn, st_out, dsr, db_in, dmeta):
    def body(se_ref, vr_ref, so_ref, dsr_ref, db_ref, dm_ref, sm_ref, sm2_ref):
        sm_ref[...] = jnp.zeros_like(sm_ref)
        sm2_ref[...] = jnp.zeros_like(sm2_ref)
        sm_ref[0:2, :] = se_ref[0:2, :]
        sm_ref[2:3, :] = vr_ref[3:4, :]
        sm_ref[3:6, :] = vr_ref[0:3, :]
        sm_ref[6:7, :] = so_ref[2:3, :]
        sm_ref[7:9, :] = so_ref[0:2, :]
        for h in range(N_KV):
            sm_ref[9:10, h * GROUP:(h + 1) * GROUP] = _colsum(dsr_ref[h])
        for j in range(6):
            sm_ref[16 + j:17 + j, :] = db_ref[0:1, j * D:(j + 1) * D]
        sm_ref[22:23, 0:D_IN - 6 * D] = db_ref[0:1, 6 * D:D_IN]
        for s in range(N_DEV):
            sm2_ref[s, 0:N_META, :] = dm_ref[:, s * 256:(s + 1) * 256]
            sm2_ref[s, N_META:N_META + CONV_WIDTH, :] = vr_ref[4:8, s * 256:(s + 1) * 256]

    return pl.pallas_call(
        body, out_shape=[SDS((_SMALL_ROWS, D), f32), SDS((N_DEV, 24, 256), f32)],
        name="pack_small", compiler_params=_cp(None),
    )(st_emb, vec_rnn, st_out, dsr, db_in, dmeta)


def _small_allreduce(sm, sm2):
    def body(sm_ref, sm2_ref, o_ref, o2_ref, buf, buf2, send_sems, recv_sems):
        x, y, c = _place()
        me = _dev(x, y, c)
        copies = []
        for f in range(1, N_DEV):
            fx, fy, fc = f // 4, (f // 2) % 2, f % 2
            peer = ((x + fx) % 2, (y + fy) % 2, (c + fc) % 2)
            for t, (src, dst) in enumerate(((sm_ref, buf), (sm2_ref, buf2))):
                k = 2 * (f - 1) + t
                copies.append(pltpu.make_async_remote_copy(
                    src_ref=src, dst_ref=dst.at[me], send_sem=send_sems.at[k], recv_sem=recv_sems.at[k],
                    device_id=peer, device_id_type=MESH))
        for cp in copies:
            cp.start()
        buf[me] = sm_ref[...]
        buf2[me] = sm2_ref[...]
        for cp in copies:
            cp.wait()
        acc, acc2 = buf[0], buf2[0]
        for e in range(1, N_DEV):
            acc, acc2 = acc + buf[e], acc2 + buf2[e]
        o_ref[...] = acc
        o2_ref[...] = acc2

    return pl.pallas_call(
        body, in_specs=[_VMEM, _VMEM], out_specs=[_VMEM, _VMEM],
        out_shape=[SDS(sm.shape, f32), SDS(sm2.shape, f32)],
        scratch_shapes=[pltpu.VMEM((N_DEV, *sm.shape), f32), pltpu.VMEM((N_DEV, *sm2.shape), f32),
                        pltpu.SemaphoreType.DMA((14,)), pltpu.SemaphoreType.DMA((14,))],
        name="small_allreduce",
    )(sm, sm2)


_SMALL_ROW_OF = {"ln_emb_g": 0, "ln_emb_b": 1, "conv_b": 2, "b_ra": 3, "b_ri": 4, "lru_lambda": 5, "b_o": 6,
                 "ln_g": 7, "ln_b": 8}
_SMALL_NAMES = ["ln_emb_g", "ln_emb_b", "conv_b", "b_ra", "b_ri", "lru_lambda", "b_o", "ln_g", "ln_b",
                "sinks", "b_in", "meta_tokens", "conv_w"]


def _small_update(sm, sm2_mine, wmv):
    def grad_of(name, sm_ref, s2_ref):
        if name in _SMALL_ROW_OF:
            r = _SMALL_ROW_OF[name]
            return sm_ref[r:r + 1, :]
        if name == "sinks":
            return sm_ref[9:10, 0:N_KV * GROUP]
        if name == "b_in":
            return jnp.concatenate([sm_ref[16 + j:17 + j, :] for j in range(7)], axis=1)[:, :D_IN]
        if name == "meta_tokens":
            return s2_ref[0:N_META, :]
        return s2_ref[N_META:N_META + CONV_WIDTH, :]

    def body(*refs):
        sm_ref, s2_ref = refs[0], refs[1]
        ins = refs[2:2 + 3 * len(_SMALL_NAMES)]
        outs = refs[2 + 3 * len(_SMALL_NAMES):]
        for i, name in enumerate(_SMALL_NAMES):
            w_ref, m_ref, v_ref = ins[3 * i:3 * i + 3]
            g = grad_of(name, sm_ref, s2_ref)
            d, mn, vn = _adamw(w_ref[...], g, m_ref[...], v_ref[...])
            outs[4 * i][...] = g
            outs[4 * i + 1][...] = d
            outs[4 * i + 2][...] = mn
            outs[4 * i + 3][...] = vn

    args, out_shape = [sm, sm2_mine], []
    for name in _SMALL_NAMES:
        args += list(wmv[name])
        out_shape += [SDS(wmv[name][0].shape, f32)] * 4
    res = pl.pallas_call(body, out_shape=out_shape, name="small_update", compiler_params=_cp(None))(*args)
    return {name: tuple(res[4 * i:4 * i + 4]) for i, name in enumerate(_SMALL_NAMES)}


_WEIGHTS = ["meta_tokens", "ln_emb_g", "ln_emb_b", "w_in", "b_in", "conv_w", "conv_b", "w_ra", "b_ra", "w_ri",
            "b_ri", "lru_lambda", "sinks", "w_rnn_out", "w_attn_out", "w_o", "b_o", "ln_g", "ln_b"]
_SMALL_2D = {"meta_tokens": (N_META, 256), "conv_w": (CONV_WIDTH, 256), "b_in": (1, D_IN), "sinks": (1, N_KV * GROUP)}


def kernel(x, meta_tokens, ln_emb_g, ln_emb_b, w_in, b_in, conv_w, conv_b, w_ra, b_ra, w_ri, b_ri, lru_lambda, sinks, w_rnn_out, w_attn_out, w_o, b_o, ln_g, ln_b, loss_target, m_meta_tokens, m_ln_emb_g, m_ln_emb_b, m_w_in, m_b_in, m_conv_w, m_conv_b, m_w_ra, m_b_ra, m_w_ri, m_b_ri, m_lru_lambda, m_sinks, m_w_rnn_out, m_w_attn_out, m_w_o, m_b_o, m_ln_g, m_ln_b, v_meta_tokens, v_ln_emb_g, v_ln_emb_b, v_w_in, v_b_in, v_conv_w, v_conv_b, v_w_ra, v_b_ra, v_w_ri, v_b_ri, v_lru_lambda, v_sinks, v_w_rnn_out, v_w_attn_out, v_w_o, v_b_o, v_ln_g, v_ln_b):
    w = dict(meta_tokens=meta_tokens, ln_emb_g=ln_emb_g, ln_emb_b=ln_emb_b, w_in=w_in, b_in=b_in, conv_w=conv_w,
             conv_b=conv_b, w_ra=w_ra, b_ra=b_ra, w_ri=w_ri, b_ri=b_ri, lru_lambda=lru_lambda, sinks=sinks,
             w_rnn_out=w_rnn_out, w_attn_out=w_attn_out, w_o=w_o, b_o=b_o, ln_g=ln_g, ln_b=ln_b)
    m = dict(meta_tokens=m_meta_tokens, ln_emb_g=m_ln_emb_g, ln_emb_b=m_ln_emb_b, w_in=m_w_in, b_in=m_b_in,
             conv_w=m_conv_w, conv_b=m_conv_b, w_ra=m_w_ra, b_ra=m_b_ra, w_ri=m_w_ri, b_ri=m_b_ri,
             lru_lambda=m_lru_lambda, sinks=m_sinks, w_rnn_out=m_w_rnn_out, w_attn_out=m_w_attn_out, w_o=m_w_o,
             b_o=m_b_o, ln_g=m_ln_g, ln_b=m_ln_b)
    v = dict(meta_tokens=v_meta_tokens, ln_emb_g=v_ln_emb_g, ln_emb_b=v_ln_emb_b, w_in=v_w_in, b_in=v_b_in,
             conv_w=v_conv_w, conv_b=v_conv_b, w_ra=v_w_ra, b_ra=v_b_ra, w_ri=v_w_ri, b_ri=v_b_ri,
             lru_lambda=v_lru_lambda, sinks=v_sinks, w_rnn_out=v_w_rnn_out, w_attn_out=v_w_attn_out, w_o=v_w_o,
             b_o=v_b_o, ln_g=v_ln_g, ln_b=v_ln_b)
    px, py, pc = _place()
    as_idx = lambda t: jnp.reshape(t, (1,)).astype(jnp.int32)
    c_idx, q_idx, me_idx = as_idx(pc), as_idx(2 * px + py), as_idx(_dev(px, py, pc))

    w3_s, wrg_s, small_s = _cast_small(w_rnn_out, w_attn_out, w_o, w_ra, w_ri, meta_tokens, conv_w)
    wg, wrg, smallw, w3_land = _all_gather([_cast_w_in(w_in), wrg_s, small_s], [w3_s])
    w3_pending = _split_start(_copies_direct(True), 7, [w3_s], [w3_land], smallw, "gather_w3_start")
    w_full = _relayout_w_in(wg)

    vec = lambda name: w[name].reshape(1, -1)
    p = {k: vec(k) for k in ("ln_emb_g", "ln_emb_b", "b_in", "conv_b", "b_ra", "b_ri", "lru_lambda", "sinks",
                             "b_o", "ln_g", "ln_b")}
    p["b_in"] = p["b_in"] + w3_pending[4][0:1, 0:1]
    s = _step_branches(x, w_full, wrg, smallw, p)
    w3 = _split_wait(_copies_direct(True), *w3_pending[:4], s["lse"], "gather_w3_wait")[1][0]
    t = _step_merge(s, loss_target, w3, p)
    loss = lax.psum(jnp.sum(t["st_out"][3]), ("x", "y", "c"))

    big = {}
    two_d = lambda name: (w[name].shape[-2], w[name].shape[-1])
    proj = ("w_o", "w_rnn_out", "w_attn_out")
    g_proj = [t[k].reshape(N_DEV, 256, D) for k in ("g_wo", "g_wrnn", "g_wattn")]
    g_pending = _split_start(_copies_direct(False), 7, g_proj, [lax.empty((N_DEV, 256, D), bf16) for _ in proj],
                             p["b_o"], "reduce_proj_start")
    u = _step_backward(s, t, wrg, smallw, p, p["conv_b"] + g_pending[4][0:1, 0:1])

    grads = [u["g_win"], u["g_wrg"].reshape(N_DEV, 2 * RNN_BLOCK, RNN_BLOCK)]
    sib = _split_start(_copies_siblings, 4, grads, [lax.empty((4, *g.shape[1:]), bf16) for g in grads],
                       u["db_in"], "reduce_siblings_start")
    g_proj, g_land = _split_wait(_copies_direct(False), *g_pending[:4], sib[4], "reduce_proj_wait")
    for i, name in enumerate(proj):
        res = _adamw_direct(g_proj[i], g_land[i], me_idx, w[name].reshape(two_d(name)), m[name].reshape(two_d(name)),
                            v[name].reshape(two_d(name)), "adamw_" + name)
        big[name] = tuple(r.reshape(w[name].shape) for r in res)
    grads, r1 = _split_wait(_copies_siblings, *sib[:4], big["w_attn_out"][3], "reduce_siblings_wait")
    parts = [_pair_sum(g, r, c_idx, "pair_sum_%d" % i) for i, (g, r) in enumerate(zip(grads, r1))]
    chp = _split_start(_copies_chips, 3, parts, [lax.empty((3, *q.shape[1:]), bf16) for q in parts], parts[1],
                       "reduce_chips_start")
    u.update(_step_input_grad(u["dz"], w_full, chp[4], t["du32"], x, smallw, p))

    loc = {**t, **u}
    sm, sm2 = _pack_small(loc["st_emb"], loc["vec_rnn"], loc["st_out"], loc["dsr"], loc["db_in"], loc["dmeta"])
    sm, sm2 = _small_allreduce(sm, sm2)
    sm2_mine = lax.dynamic_index_in_dim(sm2, _dev(px, py, pc), 0, keepdims=False)
    two = lambda name, t: t.reshape(_SMALL_2D.get(name, (1, D)))
    small = _small_update(sm, sm2_mine, {k: (two(k, w[k]), two(k, m[k]), two(k, v[k])) for k in _SMALL_NAMES})

    parts, r2 = _split_wait(_copies_chips, *chp[:4], small["b_in"][3], "reduce_chips_wait")
    res = _adamw_big(parts[0], r2[0], q_idx, w["w_in"].reshape(two_d("w_in")), m["w_in"].reshape(two_d("w_in")),
                     v["w_in"].reshape(two_d("w_in")), "adamw_w_in")
    big["w_in"] = tuple(r.reshape(w["w_in"].shape) for r in res)
    for i, name in enumerate(("w_ra", "w_ri")):
        sq = (RNN_BLOCK, RNN_BLOCK)
        res = _adamw_big(parts[1], r2[1], q_idx, w[name].reshape(sq), m[name].reshape(sq), v[name].reshape(sq),
                         "adamw_" + name, row_off=i)
        big[name] = tuple(r.reshape(w[name].shape) for r in res)
    res = dict(big)
    for k in _SMALL_NAMES:
        res[k] = tuple(t.reshape(w[k].shape) for t in small[k])

    outs = [loss, loc["grad_x"]]
    for j in range(4):
        outs += [res[k][j] for k in _WEIGHTS]
    return tuple(outs)
```

```python
import functools

import jax
import jax.numpy as jnp
from jax import lax
from jax.experimental import pallas as pl
from jax.experimental.pallas import tpu as pltpu

f32, bf16 = jnp.float32, jnp.bfloat16
SDS = jax.ShapeDtypeStruct

N_DEV = 8
D = 2048
N_META = 16
BLK = 128
ROW0 = BLK - N_META
N_RNN_BLOCKS = 8
RNN_BLOCK = D // N_RNN_BLOCKS
CONV_WIDTH = 4
LRU_C = 8.0
HEAD_DIM = 64
N_KV = 4
GROUP = 8
HALF = HEAD_DIM // 2
ROPE_THETA = 10000.0
NEG_INF = -1e30
LN_EPS = 1e-5
ALPHA = 2.0 ** 0.25
D_IN = 12800
SHARD_IN = D_IN // N_DEV
OFF_GR, OFF_Q, OFF_K, OFF_V, OFF_GA, OFF_G = 2048, 4096, 6144, 6400, 6656, 8704
ADAM_LR, ADAM_B1, ADAM_B2, ADAM_EPS, ADAM_WD, ADAM_STEP = 1e-3, 0.9, 0.999, 1e-8, 0.01, 10
VMEM_LIMIT_MB = 56
MESH = pl.DeviceIdType.MESH


def _cp(sem=None, vmem_mb=40):
    return pltpu.CompilerParams(dimension_semantics=sem, vmem_limit_bytes=vmem_mb * 2 ** 20)


def _row_chunk(m):
    best = 16
    for c in range(16, 641, 16):
        if m % c == 0:
            best = c
    return best


def _sigmoid(x):
    return 1.0 / (1.0 + jnp.exp(-x))


def _silu_and_grad(x):
    s = _sigmoid(x)
    return x * s, s * (1.0 + x * (1.0 - s))


def _log_sigmoid(x):
    return jnp.minimum(x, 0.0) - jnp.log1p(jnp.exp(-jnp.abs(x)))


def _ln_rows(v, g, b):
    mu = jnp.mean(v, axis=-1, keepdims=True)
    c = v - mu
    var = jnp.mean(c * c, axis=-1, keepdims=True)
    rstd = lax.rsqrt(var + LN_EPS)
    xhat = c * rstd
    return xhat * g + b, xhat, rstd


def _ln_rows_bwd(dy, g, xhat, rstd):
    dxh = dy * g
    m1 = jnp.mean(dxh, axis=-1, keepdims=True)
    m2 = jnp.mean(dxh * xhat, axis=-1, keepdims=True)
    return rstd * (dxh - m1 - xhat * m2)


def _colsum(v):
    return jnp.sum(v, axis=0, keepdims=True)


def _dot(a, b):
    return jnp.dot(a, b, preferred_element_type=f32)


def _dot_nt(a, b):
    return lax.dot_general(a, b, (((1,), (1,)), ((), ())), preferred_element_type=f32)


def _dot_tn(a, b):
    return lax.dot_general(a, b, (((0,), (0,)), ((), ())), preferred_element_type=f32)


def _meta_full(sw_ref):
    return jnp.concatenate([sw_ref[s, 0:N_META, :] for s in range(N_DEV)], axis=1)


def _ln_emb(x, smallw, g_e, b_e):
    seq = x.shape[1]
    rows = seq + BLK
    nb = rows // BLK

    def body(x_ref, sw_ref, g_ref, b_ref, h32_ref, hb_ref, ht_ref):
        i = pl.program_id(0)
        g, b = g_ref[...], b_ref[...]

        def emit(blk):
            h32_ref[...] = blk
            hb_ref[...] = blk.astype(bf16)
            ht_ref[...] = blk.T.astype(bf16)

        @pl.when(i == 0)
        def _():
            hm = _ln_rows(_meta_full(sw_ref), g, b)[0]
            emit(jnp.concatenate([jnp.zeros((ROW0, D), f32), hm], axis=0))

        @pl.when(i > 0)
        def _():
            emit(_ln_rows(x_ref[0], g, b)[0])

    return pl.pallas_call(
        body, grid=(nb,),
        in_specs=[pl.BlockSpec((1, BLK, D), lambda i: (0, jnp.maximum(i - 1, 0), 0)),
                  pl.BlockSpec((N_DEV, 24, 256), lambda i: (0, 0, 0)),
                  pl.BlockSpec((1, D), lambda i: (0, 0)),
                  pl.BlockSpec((1, D), lambda i: (0, 0))],
        out_specs=[pl.BlockSpec((BLK, D), lambda i: (i, 0)),
                   pl.BlockSpec((BLK, D), lambda i: (i, 0)),
                   pl.BlockSpec((D, BLK), lambda i: (0, i))],
        out_shape=[SDS((rows, D), f32), SDS((rows, D), bf16), SDS((D, rows), bf16)],
        name="ln_emb", compiler_params=_cp(("arbitrary",)),
    )(x, smallw, g_e, b_e)


def _ln_emb_bwd(dh, du32, x, smallw, g_e):
    seq = x.shape[1]
    rows = seq + BLK
    nb = rows // BLK

    def body(dh_ref, du_ref, x_ref, sw_ref, g_ref, gx_ref, dmeta_ref, st_ref):
        i = pl.program_id(0)
        g = g_ref[...]
        dht = dh_ref[...] + ALPHA * du_ref[...]

        @pl.when(i == 0)
        def _():
            v = jnp.concatenate([jnp.zeros((ROW0, D), f32), _meta_full(sw_ref)], axis=0)
            valid = lax.broadcasted_iota(jnp.int32, (BLK, 1), 0) >= ROW0
            d = jnp.where(valid, dht, 0.0)
            _, xhat, rstd = _ln_rows(v, g, 0.0)
            dv = _ln_rows_bwd(d, g, xhat, rstd)
            dmeta_ref[...] = dv[ROW0:, :]
            st_ref[...] = jnp.concatenate([_colsum(d * xhat), _colsum(d), jnp.zeros((6, D), f32)], axis=0)

        @pl.when(i > 0)
        def _():
            _, xhat, rstd = _ln_rows(x_ref[0], g, 0.0)
            gx_ref[0] = _ln_rows_bwd(dht, g, xhat, rstd)
            st_ref[0:1, :] += _colsum(dht * xhat)
            st_ref[1:2, :] += _colsum(dht)

    return pl.pallas_call(
        body, grid=(nb,),
        in_specs=[pl.BlockSpec((BLK, D), lambda i: (i, 0)),
                  pl.BlockSpec((BLK, D), lambda i: (i, 0)),
                  pl.BlockSpec((1, BLK, D), lambda i: (0, jnp.maximum(i - 1, 0), 0)),
                  pl.BlockSpec((N_DEV, 24, 256), lambda i: (0, 0, 0)),
                  pl.BlockSpec((1, D), lambda i: (0, 0))],
        out_specs=[pl.BlockSpec((1, BLK, D), lambda i: (0, jnp.maximum(i - 1, 0), 0)),
                   pl.BlockSpec((N_META, D), lambda i: (0, 0)),
                   pl.BlockSpec((8, D), lambda i: (0, 0))],
        out_shape=[SDS((1, seq, D), f32), SDS((N_META, D), f32), SDS((8, D), f32)],
        name="ln_emb_bwd", compiler_params=_cp(("arbitrary",)),
    )(dh, du32, x, smallw, g_e)


def _mm(a, b, *, name, nt=False, sel=None, bias=None, out_dtype=f32, tn=512):
    m, k = a.shape
    cm = _row_chunk(m)
    stacked = sel is not None
    if stacked:
        n = D
        if nt:
            b_spec = pl.BlockSpec((tn // 256, None, 256, D), lambda j: (j, sel, 0, 0))
        else:
            b_spec = pl.BlockSpec((N_DEV, None, 256, tn), lambda j: (0, sel, 0, j))
    elif nt:
        n = b.shape[0]
        b_spec = pl.BlockSpec((tn, k), lambda j: (j, 0))
    else:
        n = b.shape[1]
        b_spec = pl.BlockSpec((k, tn), lambda j: (0, j))
    in_specs = [pl.BlockSpec((m, k), lambda j: (0, 0)), b_spec]
    args = [a, b]
    if bias is not None:
        in_specs.append(pl.BlockSpec((1, tn), lambda j: (0, j)))
        args.append(bias)

    def body(*refs):
        a_ref, b_ref, o_ref = refs[0], refs[1], refs[-1]
        bm = b_ref[...]
        if stacked:
            bm = bm.reshape((tn, D) if nt else (D, tn))
        for c in range(m // cm):
            acc = (_dot_nt if nt else _dot)(a_ref[c * cm:(c + 1) * cm, :], bm)
            if bias is not None:
                acc = acc + refs[2][...]
            o_ref[c * cm:(c + 1) * cm, :] = acc.astype(out_dtype)

    return pl.pallas_call(
        body, grid=(n // tn,), in_specs=in_specs,
        out_specs=pl.BlockSpec((m, tn), lambda j: (0, j)),
        out_shape=SDS((m, n), out_dtype), name=name, compiler_params=_cp(("arbitrary",), 48),
    )(*args)


def _mm_dh(dz, w_full, after):
    rows = dz.shape[0]
    tk, tn = 1280, 1024
    cm = _row_chunk(rows)

    def body(a_ref, w_ref, after_ref, o_ref):
        kk = pl.program_id(1)
        for c in range(rows // cm):
            acc = _dot_nt(a_ref[c * cm:(c + 1) * cm, :], w_ref[...])

            @pl.when(kk == 0)
            def _():
                o_ref[c * cm:(c + 1) * cm, :] = acc

            @pl.when(kk > 0)
            def _():
                o_ref[c * cm:(c + 1) * cm, :] += acc

    return pl.pallas_call(
        body, grid=(D // tn, D_IN // tk),
        in_specs=[pl.BlockSpec((rows, tk), lambda j, kk: (0, kk)),
                  pl.BlockSpec((tn, tk), lambda j, kk: (j, kk)),
                  pl.BlockSpec(memory_space=pl.ANY)],
        out_specs=pl.BlockSpec((rows, tn), lambda j, kk: (0, j)),
        out_shape=SDS((rows, D), f32), name="mm_dh", compiler_params=_cp(("arbitrary", "arbitrary"), 48),
    )(dz, w_full, after)


def _mm_dwin(h_t, dz):
    rows = dz.shape[0]
    pair = 2 * SHARD_IN
    tm = 512

    def body(a_ref, dz_ref, o_ref, db_ref):
        acc = _dot(a_ref[...], dz_ref[...])
        o_ref[0] = acc[:, :SHARD_IN].astype(bf16)
        o_ref[1] = acc[:, SHARD_IN:].astype(bf16)

        @pl.when(pl.program_id(1) == 0)
        def _():
            def step(i, s):
                blk = dz_ref[pl.ds(pl.multiple_of(i * BLK, BLK), BLK), :].astype(f32)
                return s + blk.reshape(BLK // 8, 8, pair).sum(axis=0)
            s = lax.fori_loop(0, rows // BLK, step, jnp.zeros((8, pair), f32))
            db_ref[...] = jnp.broadcast_to(_colsum(s), (8, pair))

    return pl.pallas_call(
        body, grid=(N_DEV // 2, D // tm),
        in_specs=[pl.BlockSpec((tm, rows), lambda p, i: (i, 0)),
                  pl.BlockSpec((rows, pair), lambda p, i: (0, p))],
        out_specs=[pl.BlockSpec((2, tm, SHARD_IN), lambda p, i: (p, i, 0)),
                   pl.BlockSpec((8, pair), lambda p, i: (0, p))],
        out_shape=[SDS((N_DEV, D, SHARD_IN), bf16), SDS((8, D_IN), f32)],
        name="mm_dwin", compiler_params=_cp(("arbitrary", "arbitrary"), VMEM_LIMIT_MB),
    )(h_t, dz)


def _relayout_w_in(wg):
    tm = 256

    def body(i_ref, o_ref):
        for d in range(N_DEV):
            o_ref[:, d * SHARD_IN:(d + 1) * SHARD_IN] = i_ref[d]

    return pl.pallas_call(
        body, grid=(D // tm,),
        in_specs=[pl.BlockSpec((N_DEV, tm, SHARD_IN), lambda i: (0, i, 0))],
        out_specs=pl.BlockSpec((tm, D_IN), lambda i: (i, 0)),
        out_shape=SDS((D, D_IN), bf16), name="relayout_w_in", compiler_params=_cp(("arbitrary",)),
    )(wg)


SCAN_ROWS = 32


def _scan8(a, b, reverse):
    idx = lax.broadcasted_iota(jnp.int32, a.shape, 0)
    for s in (1, 2, 4):
        sh = 8 - s if reverse else s
        a_sh, b_sh = pltpu.roll(a, sh, 0), pltpu.roll(b, sh, 0)
        m = (idx < 8 - s) if reverse else (idx >= s)
        b = jnp.where(m, a * b_sh + b, b)
        a = jnp.where(m, a * a_sh, a)
    return a, b


def _shift_rows(prev8, cur, k):
    ext = jnp.concatenate([prev8, cur], axis=0)
    return pltpu.roll(ext, k, 0)[8:, :]


def _gates(xc, w_ra, b_ra, w_ri, b_ri, ls):
    xb = xc.astype(bf16)
    r = _sigmoid(_dot(xb, w_ra) + b_ra)
    ig = _sigmoid(_dot(xb, w_ri) + b_ri)
    la = LRU_C * r * ls
    a = jnp.exp(la)
    mult = jnp.sqrt(jnp.tanh(-la) * (1.0 + a * a))
    return xb, r, ig, a, mult


_RNN_IN_SPECS = lambda rows: [
    pl.BlockSpec((1, 24, 256), lambda n: (n, 0, 0)),
    pl.BlockSpec((1, RNN_BLOCK), lambda n: (0, n)),
    pl.BlockSpec((N_DEV, 2, None, 32, RNN_BLOCK), lambda n: (0, 0, n, 0, 0)),
    pl.BlockSpec((1, RNN_BLOCK), lambda n: (0, n)),
    pl.BlockSpec((1, RNN_BLOCK), lambda n: (0, n)),
    pl.BlockSpec((1, RNN_BLOCK), lambda n: (0, n)),
]


def _rnn_fwd(z, smallw, conv_b, wrg, b_ra, b_ri, lam):
    rows = z.shape[0]
    nb = rows // BLK
    col = lambda off: pl.BlockSpec((rows, RNN_BLOCK), lambda n: (0, off // RNN_BLOCK + n))

    def body(xr_ref, gr_ref, sw_ref, cb_ref, w_ref, bra_ref, bri_ref, lam_ref, xc_ref, hr_ref, ya_ref, yat_ref, a_s):
        cw = sw_ref[0, N_META:24, :]
        cb = cb_ref[...]
        w_ra = w_ref[:, 0].reshape(RNN_BLOCK, RNN_BLOCK)
        w_ri = w_ref[:, 1].reshape(RNN_BLOCK, RNN_BLOCK)
        b_ra_v, b_ri_v = bra_ref[...], bri_ref[...]
        ls = _log_sigmoid(lam_ref[...])
        rid = lax.broadcasted_iota(jnp.int32, (BLK, 1), 0)

        def blk_step(i, carry):
            r0 = pl.multiple_of(i * BLK, BLK)
            grow = rid + r0
            valid = grow >= ROW0
            cur = jnp.where(valid, xr_ref[pl.ds(r0, BLK), :], 0.0)
            prev8 = xr_ref[pl.ds(pl.multiple_of(jnp.maximum(r0 - 8, 0), 8), 8), :] * (i > 0).astype(f32)
            xc = cb + cw[0:1] * cur
            for k in range(1, CONV_WIDTH):
                xc = xc + cw[k:k + 1] * _shift_rows(prev8, cur, k)
            xc_ref[pl.ds(r0, BLK), :] = xc
            _, _, ig, a, mult = _gates(xc, w_ra, b_ra_v, w_ri, b_ri_v, ls)
            mult = jnp.where(grow == ROW0, 1.0, mult)
            a_s[pl.ds(r0, BLK), :] = a
            hr_ref[pl.ds(r0, BLK), :] = jnp.where(valid, mult * ig * xc, 0.0)
            return carry

        lax.fori_loop(0, nb, blk_step, 0)

        def scan_step(j, carry):
            r0 = pl.multiple_of(j * SCAN_ROWS, SCAN_ROWS)
            tiles = [_scan8(a_s[pl.ds(r0 + 8 * k, 8), :], hr_ref[pl.ds(r0 + 8 * k, 8), :], False)
                     for k in range(SCAN_ROWS // 8)]
            for k, (a, b) in enumerate(tiles):
                h = b + a * carry
                hr_ref[pl.ds(r0 + 8 * k, 8), :] = h
                carry = jnp.broadcast_to(h[7:8, :], (8, RNN_BLOCK))
            return carry

        lax.fori_loop(0, rows // SCAN_ROWS, scan_step, jnp.zeros((8, RNN_BLOCK), f32))

        def gate_step(i, carry):
            r0 = pl.multiple_of(i * BLK, BLK)
            ya_ref[pl.ds(r0, BLK), :] = (hr_ref[pl.ds(r0, BLK), :]
                                         * _silu_and_grad(gr_ref[pl.ds(r0, BLK), :])[0]).astype(bf16)
            return carry

        lax.fori_loop(0, nb, gate_step, 0)
        yat_ref[...] = ya_ref[...].astype(f32).T.astype(bf16)

    return pl.pallas_call(
        body, grid=(N_RNN_BLOCKS,),
        in_specs=[col(0), col(OFF_GR)] + _RNN_IN_SPECS(rows),
        out_specs=[pl.BlockSpec((rows, RNN_BLOCK), lambda n: (0, n))] * 3
                  + [pl.BlockSpec((RNN_BLOCK, rows), lambda n: (n, 0))],
        out_shape=[SDS((rows, D), f32), SDS((rows, D), f32), SDS((rows, D), bf16), SDS((D, rows), bf16)],
        scratch_shapes=[pltpu.VMEM((rows, RNN_BLOCK), f32)],
        name="rnn_fwd", compiler_params=_cp(("arbitrary",)),
    )(z, z, smallw, conv_b, wrg, b_ra, b_ri, lam)


def _rnn_bwd(dya, hr, xc, z, smallw, conv_b, wrg, b_ra, b_ri, lam):
    rows = z.shape[0]
    nb = rows // BLK
    col = lambda off: pl.BlockSpec((rows, RNN_BLOCK), lambda n: (0, off // RNN_BLOCK + n))
    blk = pl.BlockSpec((rows, RNN_BLOCK), lambda n: (0, n))

    def body(dya_ref, hr_ref, xc_ref, xr_ref, gr_ref, sw_ref, cb_ref, w_ref, bra_ref, bri_ref, lam_ref,
             dxr_ref, dgr_ref, dw_ref, vec_ref, a_s, lam_s, dxc_s, r_s, ig_s, mult_s, dw_s):
        cw = sw_ref[0, N_META:24, :]
        w_ra = w_ref[:, 0].reshape(RNN_BLOCK, RNN_BLOCK)
        w_ri = w_ref[:, 1].reshape(RNN_BLOCK, RNN_BLOCK)
        b_ra_v, b_ri_v = bra_ref[...], bri_ref[...]
        lam_v = lam_ref[...]
        ls = _log_sigmoid(lam_v)
        rid = lax.broadcasted_iota(jnp.int32, (BLK, 1), 0)
        zrow = jnp.zeros((1, RNN_BLOCK), f32)

        def p1(i, carry):
            r0 = pl.multiple_of(i * BLK, BLK)
            sl = pl.ds(r0, BLK)
            _, r, ig, a, mult = _gates(xc_ref[sl, :], w_ra, b_ra_v, w_ri, b_ri_v, ls)
            a_s[sl, :] = a
            r_s[sl, :] = r
            ig_s[sl, :] = ig
            mult_s[sl, :] = mult
            sg, dsg = _silu_and_grad(gr_ref[sl, :])
            d = dya_ref[sl, :]
            lam_s[sl, :] = d * sg
            dgr_ref[sl, :] = (d * hr_ref[sl, :] * dsg).astype(bf16)
            return carry

        lax.fori_loop(0, nb, p1, 0)

        def p2(jj, carry):
            r0 = pl.multiple_of((rows // SCAN_ROWS - 1 - jj) * SCAN_ROWS, SCAN_ROWS)
            idx = lax.broadcasted_iota(jnp.int32, (8, RNN_BLOCK), 0)
            tiles = []
            for k in range(SCAN_ROWS // 8):
                sl = pl.ds(r0 + 8 * k, 8)
                a, g = a_s[sl, :], lam_s[sl, :]
                tiles.append((g, *_scan8(a, a * g, True)))
            for k in reversed(range(SCAN_ROWS // 8)):
                g, ca, cb_ = tiles[k]
                mu = cb_ + ca * carry
                lam_s[pl.ds(r0 + 8 * k, 8), :] = g + jnp.where(idx < 7, pltpu.roll(mu, 7, 0), carry)
                carry = jnp.broadcast_to(mu[0:1, :], (8, RNN_BLOCK))
            return carry

        lax.fori_loop(0, rows // SCAN_ROWS, p2, jnp.zeros((8, RNN_BLOCK), f32))

        dw_s[...] = jnp.zeros_like(dw_s)

        def p3(i, carry):
            d_bra, d_bri, d_ls = carry
            r0 = pl.multiple_of(i * BLK, BLK)
            sl = pl.ds(r0, BLK)
            grow = rid + r0
            valid = grow >= ROW0
            first = grow == ROW0
            xcv = xc_ref[sl, :]
            xb = xcv.astype(bf16)
            r, ig, a = r_s[sl, :], ig_s[sl, :], a_s[sl, :]
            mult = jnp.where(first, 1.0, mult_s[sl, :])
            lam_t = lam_s[sl, :]
            du = jnp.where(valid, lam_t, 0.0)
            hprev = _shift_rows(hr_ref[pl.ds(pl.multiple_of(jnp.maximum(r0 - 8, 0), 8), 8), :] * (i > 0).astype(f32), hr_ref[sl, :], 1)
            da = lam_t * hprev
            dmult = jnp.where(first, 0.0, du * ig * xcv)
            di = du * mult * xcv
            dxc = du * mult * ig
            ratio = jnp.where(valid & jnp.logical_not(first), a * a / mult, 0.0)
            dla = da * a - dmult * ratio
            dpr = (dla * (LRU_C * ls)) * r * (1.0 - r)
            dpi = di * ig * (1.0 - ig)
            dprb, dpib = dpr.astype(bf16), dpi.astype(bf16)
            dw_s[0] += _dot_tn(xb, dprb)
            dw_s[1] += _dot_tn(xb, dpib)
            dxc_s[sl, :] = dxc + _dot_nt(dprb, w_ra) + _dot_nt(dpib, w_ri)
            return d_bra + _colsum(dpr), d_bri + _colsum(dpi), d_ls + _colsum(dla * (LRU_C * r))

        d_bra, d_bri, d_ls = lax.fori_loop(0, nb, p3, (zrow, zrow, zrow))

        def p4(i, carry):
            d_cb, d_w0, d_w1, d_w2, d_w3 = carry
            r0 = pl.multiple_of(i * BLK, BLK)
            sl = pl.ds(r0, BLK)
            grow = rid + r0
            valid = grow >= ROW0
            dxc = dxc_s[sl, :]
            nxt = dxc_s[pl.ds(pl.multiple_of(jnp.minimum(r0 + BLK, rows - 8), 8), 8), :] * (i < nb - 1).astype(f32)
            ext = jnp.concatenate([dxc, nxt], axis=0)
            dxr = cw[0:1] * dxc
            for k in range(1, CONV_WIDTH):
                dxr = dxr + cw[k:k + 1] * pltpu.roll(ext, BLK + 8 - k, 0)[:BLK, :]
            dxr_ref[sl, :] = jnp.where(valid, dxr, 0.0).astype(bf16)
            cur = jnp.where(valid, xr_ref[sl, :], 0.0)
            prev8 = xr_ref[pl.ds(pl.multiple_of(jnp.maximum(r0 - 8, 0), 8), 8), :] * (i > 0).astype(f32)
            dws = [d_w0 + _colsum(dxc * cur)]
            for k, acc in ((1, d_w1), (2, d_w2), (3, d_w3)):
                dws.append(acc + _colsum(dxc * _shift_rows(prev8, cur, k)))
            return (d_cb + _colsum(dxc), *dws)

        d_cb, d_w0, d_w1, d_w2, d_w3 = lax.fori_loop(0, nb, p4, (zrow,) * 5)

        d_lam = d_ls * _sigmoid(-lam_v)
        vec_ref[...] = jnp.concatenate([d_bra, d_bri, d_lam, d_cb, d_w0, d_w1, d_w2, d_w3], axis=0)
        dw_ref[:, 0] = dw_s[0].astype(bf16).reshape(N_DEV, 32, RNN_BLOCK)
        dw_ref[:, 1] = dw_s[1].astype(bf16).reshape(N_DEV, 32, RNN_BLOCK)

    return pl.pallas_call(
        body, grid=(N_RNN_BLOCKS,),
        in_specs=[blk, blk, blk, col(0), col(OFF_GR)] + _RNN_IN_SPECS(rows),
        out_specs=[blk, blk,
                   pl.BlockSpec((N_DEV, 2, None, 32, RNN_BLOCK), lambda n: (0, 0, n, 0, 0)),
                   pl.BlockSpec((8, RNN_BLOCK), lambda n: (0, n))],
        out_shape=[SDS((rows, D), bf16), SDS((rows, D), bf16),
                   SDS((N_DEV, 2, N_RNN_BLOCKS, 32, RNN_BLOCK), bf16), SDS((8, D), f32)],
        scratch_shapes=[pltpu.VMEM((rows, RNN_BLOCK), f32)] * 6 + [pltpu.VMEM((2, RNN_BLOCK, RNN_BLOCK), f32)],
        name="rnn_bwd", compiler_params=_cp(("arbitrary",), 48),
    )(dya, hr, xc, z, z, smallw, conv_b, wrg, b_ra, b_ri, lam)


def _rope_tables(rows):
    half = jnp.arange(HALF, dtype=f32)
    inv = ROPE_THETA ** (-half / HALF)
    pos = (jnp.arange(rows) - ROW0).astype(f32)
    ang = pos[:, None] * inv[None, :]
    cos, sin = jnp.cos(ang), jnp.sin(ang)
    cos128 = jnp.concatenate([cos, cos, cos, cos], axis=1)
    sin128 = jnp.concatenate([-sin, sin, -sin, sin], axis=1)
    return cos128, sin128


def _rope128(x, cos128, sin128):
    lane = lax.broadcasted_iota(jnp.int32, x.shape, 1)
    swapped = jnp.where(lane % HEAD_DIM < HALF, pltpu.roll(x, 128 - HALF, 1), pltpu.roll(x, HALF, 1))
    return x * cos128 + swapped * sin128


def _qkv_prep(z, cos128, sin128):
    rows = z.shape[0]

    def body(q_ref, kv_ref, c_ref, s_ref, qo_ref, ko_ref, vo_ref):
        c, s = c_ref[...], s_ref[...]
        for g in range(D // 128):
            qo_ref[:, g * 128:(g + 1) * 128] = (_rope128(q_ref[:, g * 128:(g + 1) * 128], c, s)
                                                * (HEAD_DIM ** -0.5)).astype(bf16)
        for g in range(2):
            kr = _rope128(kv_ref[:, g * 128:(g + 1) * 128], c, s)
            for j in range(2):
                ko_ref[2 * g + j] = kr[:, j * HEAD_DIM:(j + 1) * HEAD_DIM].astype(bf16)
        for h in range(N_KV):
            vo_ref[h] = kv_ref[:, 256 + h * HEAD_DIM:256 + (h + 1) * HEAD_DIM].astype(bf16)

    return pl.pallas_call(
        body, grid=(rows // BLK,),
        in_specs=[pl.BlockSpec((BLK, D), lambda i: (i, OFF_Q // D)),
                  pl.BlockSpec((BLK, 512), lambda i: (i, OFF_K // 512)),
                  pl.BlockSpec((BLK, 128), lambda i: (i, 0)),
                  pl.BlockSpec((BLK, 128), lambda i: (i, 0))],
        out_specs=[pl.BlockSpec((BLK, D), lambda i: (i, 0)),
                   pl.BlockSpec((N_KV, BLK, HEAD_DIM), lambda i: (0, i, 0)),
                   pl.BlockSpec((N_KV, BLK, HEAD_DIM), lambda i: (0, i, 0))],
        out_shape=[SDS((rows, D), bf16), SDS((N_KV, rows, HEAD_DIM), bf16), SDS((N_KV, rows, HEAD_DIM), bf16)],
        name="qkv_prep", compiler_params=_cp(("arbitrary",)),
    )(z, z, cos128, sin128)


def _attn_mask(n):
    qi = n * BLK + lax.broadcasted_iota(jnp.int32, (BLK, 2 * BLK + N_META), 0)
    c = lax.broadcasted_iota(jnp.int32, (BLK, 2 * BLK + N_META), 1)
    jb = (n - 1) * BLK + c
    band = (jb >= BLK) & (jb <= qi) & (qi - jb < BLK)
    meta = (ROW0 + c - 2 * BLK) <= qi
    return ((c < 2 * BLK) & band) | ((c >= 2 * BLK) & meta)


N_KEYS = 2 * BLK + N_META


def _stack_heads(t):
    return jnp.concatenate([t[:, g * HEAD_DIM:(g + 1) * HEAD_DIM] for g in range(GROUP)], axis=0)


def _sink_column(sink_ref, h):
    g = lax.broadcasted_iota(jnp.int32, (GROUP, 1, 1), 0)
    col = jnp.zeros((GROUP, 1, 1), f32)
    for j in range(GROUP):
        col = jnp.where(g == j, sink_ref[h * GROUP + j], col)
    return col


def _kv_specs(last):
    cl = lambda n: jnp.minimum(n, last)
    return [pl.BlockSpec((None, N_META, HEAD_DIM), lambda h, n: (h, ROW0 // N_META, 0)),
            pl.BlockSpec((None, BLK, HEAD_DIM), lambda h, n: (h, jnp.maximum(cl(n) - 1, 0), 0)),
            pl.BlockSpec((None, BLK, HEAD_DIM), lambda h, n: (h, cl(n), 0))]


def _attn_fwd(q_r, k_r, v_b, z, sinks):
    rows = q_r.shape[0]
    nb = rows // BLK

    def body(sink_ref, q_ref, km_ref, kp_ref, kc_ref, vm_ref, vp_ref, vc_ref, ga_ref, o_ref, yb_ref, ybt_ref, lse_ref):
        h, n = pl.program_id(0), pl.program_id(1)
        kk = jnp.concatenate([kp_ref[...], kc_ref[...], km_ref[...]], axis=0)
        vv = jnp.concatenate([vp_ref[...], vc_ref[...], vm_ref[...]], axis=0)
        q2 = _stack_heads(q_ref[...])
        s = jnp.where(_attn_mask(n)[None], _dot_nt(q2, kk).reshape(GROUP, BLK, N_KEYS), NEG_INF)
        sink = _sink_column(sink_ref, h)
        m = jnp.maximum(jnp.max(s, axis=-1, keepdims=True), sink)
        p = jnp.exp(s - m)
        den = jnp.sum(p, axis=-1, keepdims=True) + jnp.exp(sink - m)
        o2 = _dot((p / den).astype(bf16).reshape(GROUP * BLK, N_KEYS), vv)
        lse = m + jnp.log(den)
        for g in range(GROUP):
            o_ref[:, g * HEAD_DIM:(g + 1) * HEAD_DIM] = o2[g * BLK:(g + 1) * BLK]
            lse_ref[:, g:g + 1] = lse[g]
        yb = o_ref[...] * _silu_and_grad(ga_ref[...])[0]
        yb_ref[...] = yb.astype(bf16)
        ybt_ref[...] = yb.T.astype(bf16)

    tile = pl.BlockSpec((BLK, 512), lambda h, n: (n, h))
    return pl.pallas_call(
        body, grid=(N_KV, nb),
        in_specs=[pl.BlockSpec(memory_space=pltpu.SMEM), tile] + _kv_specs(nb - 1) + _kv_specs(nb - 1)
                 + [pl.BlockSpec((BLK, 512), lambda h, n: (n, OFF_GA // 512 + h))],
        out_specs=[tile, tile, pl.BlockSpec((512, BLK), lambda h, n: (h, n)),
                   pl.BlockSpec((None, BLK, GROUP), lambda h, n: (h, n, 0))],
        out_shape=[SDS((rows, D), f32), SDS((rows, D), bf16), SDS((D, rows), bf16),
                   SDS((N_KV, rows, GROUP), f32)],
        name="attn_fwd", compiler_params=_cp(("arbitrary", "arbitrary")),
    )(sinks, q_r, k_r, k_r, k_r, v_b, v_b, v_b, z)


def _attn_bwd(dyb, o32, lse, q_r, k_r, v_b, z, sinks):
    rows = q_r.shape[0]
    nb = rows // BLK
    cl = lambda n: jnp.minimum(n, nb - 1)

    def body(sink_ref, dyb_ref, o_ref, lse_ref, q_ref, km_ref, kp_ref, kc_ref, vm_ref, vp_ref, vc_ref, ga_ref,
             dq_ref, dga_ref, dk_ref, dv_ref, dkm_ref, dvm_ref, dsr_ref, ck_s, cv_s):
        h, n = pl.program_id(0), pl.program_id(1)

        @pl.when(n == 0)
        def _():
            dkm_ref[...] = jnp.zeros_like(dkm_ref)
            dvm_ref[...] = jnp.zeros_like(dvm_ref)
            ck_s[...] = jnp.zeros_like(ck_s)
            cv_s[...] = jnp.zeros_like(cv_s)

        @pl.when(n < nb)
        def _():
            kk = jnp.concatenate([kp_ref[...], kc_ref[...], km_ref[...]], axis=0)
            vv = jnp.concatenate([vp_ref[...], vc_ref[...], vm_ref[...]], axis=0)
            sg, dsg = _silu_and_grad(ga_ref[...])
            dyb_v = dyb_ref[...]
            o_v = o_ref[...]
            dga_ref[...] = (dyb_v * o_v * dsg).astype(bf16)
            q2 = _stack_heads(q_ref[...])
            do2 = _stack_heads(dyb_v * sg)
            lse_v = lse_ref[...]
            lse = jnp.concatenate([lse_v[:, g:g + 1] for g in range(GROUP)], axis=0).reshape(GROUP, BLK, 1)
            delta = jnp.sum(do2 * _stack_heads(o_v), axis=-1, keepdims=True).reshape(GROUP, BLK, 1)
            s = jnp.where(_attn_mask(n)[None], _dot_nt(q2, kk).reshape(GROUP, BLK, N_KEYS), NEG_INF)
            p = jnp.exp(s - lse)
            do2b = do2.astype(bf16)
            ds = (p * (_dot_nt(do2b, vv).reshape(GROUP, BLK, N_KEYS) - delta)).astype(bf16)
            ds = ds.reshape(GROUP * BLK, N_KEYS)
            dsr = -jnp.exp(_sink_column(sink_ref, h) - lse) * delta
            dq2 = _dot(ds, kk)
            for g in range(GROUP):
                dq_ref[:, g * HEAD_DIM:(g + 1) * HEAD_DIM] = dq2[g * BLK:(g + 1) * BLK]
                dsr_ref[:, g:g + 1] = dsr[g]
            dkk = _dot_tn(ds, q2)
            dvv = _dot_tn(p.astype(bf16).reshape(GROUP * BLK, N_KEYS), do2b)
            dk_ref[...] = ck_s[...] + dkk[:BLK]
            dv_ref[...] = cv_s[...] + dvv[:BLK]
            ck_s[...] = dkk[BLK:2 * BLK]
            cv_s[...] = dvv[BLK:2 * BLK]
            dkm_ref[...] += dkk[2 * BLK:]
            dvm_ref[...] += dvv[2 * BLK:]

        @pl.when(n == nb)
        def _():
            dk_ref[...] = ck_s[...]
            dv_ref[...] = cv_s[...]

    tile = pl.BlockSpec((BLK, 512), lambda h, n: (cl(n), h))
    kvout = pl.BlockSpec((None, BLK, HEAD_DIM), lambda h, n: (h, jnp.maximum(n - 1, 0), 0))
    mout = pl.BlockSpec((None, N_META, HEAD_DIM), lambda h, n: (h, 0, 0))
    stat = pl.BlockSpec((None, BLK, GROUP), lambda h, n: (h, cl(n), 0))
    return pl.pallas_call(
        body, grid=(N_KV, nb + 1),
        in_specs=[pl.BlockSpec(memory_space=pltpu.SMEM), tile, tile, stat, tile] + _kv_specs(nb - 1)
                 + _kv_specs(nb - 1) + [pl.BlockSpec((BLK, 512), lambda h, n: (cl(n), OFF_GA // 512 + h))],
        out_specs=[tile, tile, kvout, kvout, mout, mout, stat],
        out_shape=[SDS((rows, D), f32), SDS((rows, D), bf16),
                   SDS((N_KV, rows, HEAD_DIM), f32), SDS((N_KV, rows, HEAD_DIM), f32),
                   SDS((N_KV, N_META, HEAD_DIM), f32), SDS((N_KV, N_META, HEAD_DIM), f32),
                   SDS((N_KV, rows, GROUP), f32)],
        scratch_shapes=[pltpu.VMEM((BLK, HEAD_DIM), f32), pltpu.VMEM((BLK, HEAD_DIM), f32)],
        name="attn_bwd", compiler_params=_cp(("arbitrary", "arbitrary")),
    )(sinks, dyb, o32, lse, q_r, k_r, k_r, k_r, v_b, v_b, v_b, z)


def _qkv_finish(dq, dk, dv, dkm, dvm, cos128, sin128):
    rows = dq.shape[0]

    def body(dq_ref, dk_ref, dv_ref, dkm_ref, dvm_ref, c_ref, s_ref, oq_ref, okv_ref):
        first = (pl.program_id(0) == 0).astype(f32)
        c, s = c_ref[...], -s_ref[...]
        for g in range(D // 128):
            oq_ref[:, g * 128:(g + 1) * 128] = (_rope128(dq_ref[:, g * 128:(g + 1) * 128], c, s)
                                                * (HEAD_DIM ** -0.5)).astype(bf16)
        pad = jnp.zeros((ROW0, HEAD_DIM), f32)
        ks = [dk_ref[h] + first * jnp.concatenate([pad, dkm_ref[h]], axis=0) for h in range(N_KV)]
        vs = [dv_ref[h] + first * jnp.concatenate([pad, dvm_ref[h]], axis=0) for h in range(N_KV)]
        for g in range(2):
            kp = jnp.concatenate([ks[2 * g], ks[2 * g + 1]], axis=1)
            okv_ref[:, g * 128:(g + 1) * 128] = _rope128(kp, c, s).astype(bf16)
            okv_ref[:, 256 + g * 128:256 + (g + 1) * 128] = jnp.concatenate([vs[2 * g], vs[2 * g + 1]], axis=1).astype(bf16)

    kv = pl.BlockSpec((N_KV, BLK, HEAD_DIM), lambda i: (0, i, 0))
    mt = pl.BlockSpec((N_KV, N_META, HEAD_DIM), lambda i: (0, 0, 0))
    return pl.pallas_call(
        body, grid=(rows // BLK,),
        in_specs=[pl.BlockSpec((BLK, D), lambda i: (i, 0)), kv, kv, mt, mt,
                  pl.BlockSpec((BLK, 128), lambda i: (i, 0)), pl.BlockSpec((BLK, 128), lambda i: (i, 0))],
        out_specs=[pl.BlockSpec((BLK, D), lambda i: (i, 0)), pl.BlockSpec((BLK, 512), lambda i: (i, 0))],
        out_shape=[SDS((rows, D), bf16), SDS((rows, 512), bf16)],
        name="qkv_finish", compiler_params=_cp(("arbitrary",)),
    )(dq, dk, dv, dkm, dvm, cos128, sin128)


_TW = 512


def _mix_specs(rows):
    tr = _row_chunk(rows)
    tile = pl.BlockSpec((tr, _TW), lambda i, j: (i, j))
    ga = pl.BlockSpec((tr, _TW), lambda i, j: (i, OFF_G // _TW + j))
    gb = pl.BlockSpec((tr, _TW), lambda i, j: (i, (OFF_G + D) // _TW + j))
    return (rows // tr, D // _TW), tile, ga, gb


def _mix_fwd(y_a, y_b, z):
    rows = y_a.shape[0]
    tw = 256
    col = lambda off: pl.BlockSpec((rows, tw), lambda j: (0, off // tw + j))

    def body(ya_ref, yb_ref, ga_ref, gb_ref, o_ref, ot_ref):
        mixed = _sigmoid(ga_ref[...]) * ya_ref[...] + _sigmoid(gb_ref[...]) * yb_ref[...]
        o_ref[...] = mixed.astype(bf16)
        ot_ref[...] = mixed.T.astype(bf16)

    return pl.pallas_call(
        body, grid=(D // tw,), in_specs=[col(0), col(0), col(OFF_G), col(OFF_G + D)],
        out_specs=[col(0), pl.BlockSpec((tw, rows), lambda j: (j, 0))],
        out_shape=[SDS((rows, D), bf16), SDS((D, rows), bf16)],
        name="mix_fwd", compiler_params=_cp(("arbitrary",)),
    )(y_a, y_b, z, z)


def _mix_bwd(dmixed, y_a, y_b, z):
    rows = y_a.shape[0]
    grid, _mix_tile, _mix_ga, _mix_gb = _mix_specs(rows)

    def body(dm_ref, ya_ref, yb_ref, ga_ref, gb_ref, dya_ref, dyb_ref, dga_ref, dgb_ref):
        dm = dm_ref[...]
        sa, sb = _sigmoid(ga_ref[...]), _sigmoid(gb_ref[...])
        dya_ref[...] = (dm * sa).astype(bf16)
        dyb_ref[...] = (dm * sb).astype(bf16)
        dga_ref[...] = (dm * ya_ref[...] * sa * (1.0 - sa)).astype(bf16)
        dgb_ref[...] = (dm * yb_ref[...] * sb * (1.0 - sb)).astype(bf16)

    return pl.pallas_call(
        body, grid=grid, in_specs=[_mix_tile, _mix_tile, _mix_tile, _mix_ga, _mix_gb],
        out_specs=[_mix_tile] * 4, out_shape=[SDS((rows, D), bf16)] * 4,
        name="mix_bwd", compiler_params=_cp(("arbitrary", "arbitrary")),
    )(dmixed, y_a, y_b, z, z)


def _final_ln(out32, h32, tgt, ln_g, ln_b):
    rows = out32.shape[0]

    def body(o_ref, h_ref, t_ref, g_ref, b_ref, du_ref, dub_ref, st_ref):
        i = pl.program_id(0)
        g = g_ref[...]
        y, xhat, rstd = _ln_rows(ALPHA * h_ref[...] + o_ref[...], g, b_ref[...])
        e = jnp.where(i > 0, y - t_ref[0], 0.0)
        dy = e * (1.0 / D)
        du = _ln_rows_bwd(dy, g, xhat, rstd)
        du_ref[...] = du
        dub_ref[...] = du.astype(bf16)
        st = jnp.concatenate([_colsum(dy * xhat), _colsum(dy), _colsum(du), _colsum(e * e) * (0.5 / D),
                              jnp.zeros((4, D), f32)], axis=0)

        @pl.when(i == 0)
        def _():
            st_ref[...] = st

        @pl.when(i > 0)
        def _():
            st_ref[...] += st

    row = pl.BlockSpec((BLK, D), lambda i: (i, 0))
    vec = pl.BlockSpec((1, D), lambda i: (0, 0))
    return pl.pallas_call(
        body, grid=(rows // BLK,),
        in_specs=[row, row, pl.BlockSpec((1, BLK, D), lambda i: (0, jnp.maximum(i - 1, 0), 0)), vec, vec],
        out_specs=[row, row, pl.BlockSpec((8, D), lambda i: (0, 0))],
        out_shape=[SDS((rows, D), f32), SDS((rows, D), bf16), SDS((8, D), f32)],
        name="final_ln", compiler_params=_cp(("arbitrary",)),
    )(out32, h32, tgt, ln_g, ln_b)


def _assemble_dz(dxr, dgr, dq, dkv, dga, dma, dmb):
    rows = dxr.shape[0]
    parts = [(dxr, D), (dgr, D), (dq, D), (dkv, 512), (dga, D), (dma, D), (dmb, D)]

    def body(*refs):
        o_ref = refs[-1]
        off = 0
        for r, (_, w) in zip(refs[:-1], parts):
            o_ref[:, off:off + w] = r[...]
            off += w

    return pl.pallas_call(
        body, grid=(rows // BLK,),
        in_specs=[pl.BlockSpec((BLK, w), lambda i: (i, 0)) for _, w in parts],
        out_specs=pl.BlockSpec((BLK, D_IN), lambda i: (i, 0)),
        out_shape=SDS((rows, D_IN), bf16), name="assemble_dz", compiler_params=_cp(("arbitrary",)),
    )(*[p for p, _ in parts])


def _step_branches(x, w_full, wrg, smallw, p):
    rows = x.shape[1] + BLK
    cos128, sin128 = _rope_tables(rows)
    sinks = p["sinks"].reshape(N_KV * GROUP)
    h32, hb, h_t = _ln_emb(x, smallw, p["ln_emb_g"], p["ln_emb_b"])
    z = _mm(hb, w_full, bias=p["b_in"], name="mm_z")
    xc, hr, ya, ya_t = _rnn_fwd(z, smallw, p["conv_b"], wrg, p["b_ra"], p["b_ri"], p["lru_lambda"])
    q_r, k_r, v_b = _qkv_prep(z, cos128, sin128)
    o32, yb, yb_t, lse = _attn_fwd(q_r, k_r, v_b, z, sinks)
    return dict(cos128=cos128, sin128=sin128, sinks=sinks, h32=h32, h_t=h_t, z=z, xc=xc, hr=hr, ya=ya, ya_t=ya_t,
                q_r=q_r, k_r=k_r, v_b=v_b, o32=o32, yb=yb, yb_t=yb_t, lse=lse)


def _step_merge(s, tgt, w3, p):
    ya, yb, z = s["ya"], s["yb"], s["z"]
    y_a = _mm(ya, w3, sel=0, name="mm_ya")
    y_b = _mm(yb, w3, sel=1, name="mm_yb")
    mixed, mixed_t = _mix_fwd(y_a, y_b, z)
    out32 = _mm(mixed, w3, sel=2, bias=p["b_o"], name="mm_out")
    du32, dub, st_out = _final_ln(out32, s["h32"], tgt, p["ln_g"], p["ln_b"])

    g_wo = _mm(mixed_t, dub, out_dtype=bf16, name="mm_dwo")
    dmixed = _mm(dub, w3, sel=2, nt=True, name="mm_dmixed")
    dya_b, dyb_b, dma, dmb = _mix_bwd(dmixed, y_a, y_b, z)
    g_wrnn = _mm(s["ya_t"], dya_b, out_dtype=bf16, name="mm_dwrnn")
    g_wattn = _mm(s["yb_t"], dyb_b, out_dtype=bf16, name="mm_dwattn")
    dya = _mm(dya_b, w3, sel=0, nt=True, name="mm_dya")
    dyb = _mm(dyb_b, w3, sel=1, nt=True, name="mm_dyb")
    return dict(du32=du32, st_out=st_out, dma=dma, dmb=dmb, dya=dya, dyb=dyb, g_wo=g_wo, g_wrnn=g_wrnn,
                g_wattn=g_wattn)


def _step_backward(s, t, wrg, smallw, p, conv_b):
    z = s["z"]
    dxr, dgr, g_wrg, vec_rnn = _rnn_bwd(t["dya"], s["hr"], s["xc"], z, smallw, conv_b, wrg, p["b_ra"], p["b_ri"],
                                        p["lru_lambda"])
    dq_r, dga, dk, dv, dkm, dvm, dsr = _attn_bwd(t["dyb"], s["o32"], s["lse"], s["q_r"], s["k_r"], s["v_b"], z,
                                                 s["sinks"])
    dq, dkv = _qkv_finish(dq_r, dk, dv, dkm, dvm, s["cos128"], s["sin128"])
    dz = _assemble_dz(dxr, dgr, dq, dkv, dga, t["dma"], t["dmb"])
    g_win, db_in = _mm_dwin(s["h_t"], dz)
    return dict(vec_rnn=vec_rnn, dsr=dsr, db_in=db_in, g_win=g_win, g_wrg=g_wrg, dz=dz)


def _step_input_grad(dz, w_full, after, du32, x, smallw, p):
    dh = _mm_dh(dz, w_full, after)
    grad_x, dmeta, st_emb = _ln_emb_bwd(dh, du32, x, smallw, p["ln_emb_g"])
    return dict(grad_x=grad_x, dmeta=dmeta, st_emb=st_emb)


_ANY = pl.BlockSpec(memory_space=pl.ANY)
_VMEM = pl.BlockSpec(memory_space=pltpu.VMEM)


def _place():
    x, y, c = lax.axis_index("x"), lax.axis_index("y"), lax.axis_index("c")
    return x, y, c


def _dev(px, py, pc):
    return 4 * px + 2 * py + pc


def _cast_w_in(w_in):
    tm = 256

    def body(i_ref, o_ref):
        o_ref[...] = i_ref[0].astype(bf16)

    return pl.pallas_call(
        body, grid=(D // tm,),
        in_specs=[pl.BlockSpec((1, tm, SHARD_IN), lambda i: (0, i, 0))],
        out_specs=pl.BlockSpec((tm, SHARD_IN), lambda i: (i, 0)),
        out_shape=SDS((D, SHARD_IN), bf16), name="cast_w_in", compiler_params=_cp(("arbitrary",)),
    )(w_in)


def _cast_small(w_rnn_out, w_attn_out, w_o, w_ra, w_ri, meta, conv_w):
    def body(a_ref, b_ref, c_ref, ra_ref, ri_ref, m_ref, cw_ref, w3_ref, wrg_ref, sw_ref):
        w3_ref[0] = a_ref[0].astype(bf16)
        w3_ref[1] = b_ref[0].astype(bf16)
        w3_ref[2] = c_ref[0].astype(bf16)
        wrg_ref[0] = ra_ref[0].astype(bf16)
        wrg_ref[1] = ri_ref[0].astype(bf16)
        sw_ref[...] = jnp.concatenate([m_ref[...], cw_ref[0], jnp.zeros((4, 256), f32)], axis=0)

    return pl.pallas_call(
        body,
        out_shape=[SDS((3, 256, D), bf16), SDS((2, N_RNN_BLOCKS, 32, RNN_BLOCK), bf16), SDS((24, 256), f32)],
        name="cast_small", compiler_params=_cp(None),
    )(w_rnn_out, w_attn_out, w_o, w_ra, w_ri, meta, conv_w)


def _all_gather(shards, later):
    n = len(shards)
    nl = len(later)

    def body(*refs):
        ins, outs = refs[:n], refs[n + nl:2 * n + nl]
        send_sems, recv_sems, local_sems = refs[2 * (n + nl):]
        x, y, c = _place()
        me, sibling = (x, y, c), (x, y, 1 - c)
        chips = [(1 - x, y), (x, 1 - y), (1 - x, 1 - y)]

        def copy(a, k, block, to, src=None):
            dst = outs[a].at[_dev(*block)]
            return pltpu.make_async_remote_copy(
                src_ref=dst if src is None else src, dst_ref=dst,
                send_sem=send_sems.at[a * 7 + k], recv_sem=recv_sems.at[a * 7 + k],
                device_id=to, device_id_type=MESH)

        all_ins, all_outs = refs[:n + nl], refs[n + nl:2 * (n + nl)]
        mine = [pltpu.make_async_copy(all_ins[a], all_outs[a].at[_dev(*me)], local_sems.at[a]) for a in range(n + nl)]
        for cp in mine:
            cp.start()
        first = []
        for a in range(n):
            first.append(copy(a, 0, me, sibling, src=ins[a]))
            first += [copy(a, 1 + j, me, (*chip, c), src=ins[a]) for j, chip in enumerate(chips)]
        for cp in first:
            cp.start()
        passed = []
        for a in range(n):
            for j, chip in enumerate(chips):
                copy(a, 1 + j, (*chip, c), me).wait_recv()
                cp = copy(a, 4 + j, (*chip, c), sibling)
                cp.start()
                passed.append(cp)
        for a in range(n):
            copy(a, 0, sibling, me).wait_recv()
            for j, chip in enumerate(chips):
                copy(a, 4 + j, (*chip, 1 - c), me).wait_recv()
        for cp in first + passed:
            cp.wait_send()
        for cp in mine:
            cp.wait()

    return pl.pallas_call(
        body, in_specs=[_ANY] * (n + nl), out_specs=[_ANY] * (n + nl),
        out_shape=[SDS((N_DEV, *s.shape), s.dtype) for s in (*shards, *later)],
        scratch_shapes=[pltpu.SemaphoreType.DMA((7 * n,)), pltpu.SemaphoreType.DMA((7 * n,)),
                        pltpu.SemaphoreType.DMA((n + nl,))],
        name="all_gather_weights",
    )(*shards, *later)


_HBM = pl.BlockSpec(memory_space=pltpu.HBM)
_SEM = pl.BlockSpec(memory_space=pltpu.SEMAPHORE)
_PEER_FLIPS = [(f // 4, (f // 2) % 2, f % 2) for f in range(1, N_DEV)]


def _remote(src, dst, send_sems, recv_sems, k, to):
    return pltpu.make_async_remote_copy(src_ref=src, dst_ref=dst, send_sem=send_sems.at[k], recv_sem=recv_sems.at[k],
                                        device_id=to, device_id_type=MESH)


def _copies_direct(same_src):
    def make(srcs, lands, send_sems, recv_sems):
        x, y, c = _place()
        me = _dev(x, y, c)
        out = []
        for a in range(len(srcs)):
            for k, (fx, fy, fc) in enumerate(_PEER_FLIPS):
                peer = ((x + fx) % 2, (y + fy) % 2, (c + fc) % 2)
                src = srcs[a] if same_src else srcs[a].at[_dev(*peer)]
                out.append(_remote(src, lands[a].at[me], send_sems, recv_sems, 7 * a + k, peer))
        return out
    return make


def _copies_siblings(srcs, lands, send_sems, recv_sems):
    x, y, c = _place()
    return [_remote(srcs[a].at[2 * q + (1 - c)], lands[a].at[q], send_sems, recv_sems, 4 * a + q, (x, y, 1 - c))
            for a in range(len(srcs)) for q in range(4)]


def _copies_chips(srcs, lands, send_sems, recv_sems):
    x, y, c = _place()
    chips = [(1 - x, y), (x, 1 - y), (1 - x, 1 - y)]
    return [_remote(srcs[a].at[2 * qx + qy], lands[a].at[j], send_sems, recv_sems, 3 * a + j, (qx, qy, c))
            for a in range(len(srcs)) for j, (qx, qy) in enumerate(chips)]


def _split_start(make, per_array, srcs, lands, dep, name):
    n = len(srcs)

    def body(*refs):
        send_sems, recv_sems, token = refs[2 * n + 1], refs[2 * n + 2], refs[-1]
        for cp in make(refs[:n], refs[n:2 * n], send_sems, recv_sems):
            cp.start()
        token[...] = jnp.zeros_like(token)

    hbm = lambda t: pltpu.with_memory_space_constraint(t, pltpu.HBM)
    res = pl.pallas_call(
        body, name=name,
        out_shape=(pltpu.SemaphoreType.DMA((per_array * n,)), pltpu.SemaphoreType.DMA((per_array * n,)),
                   *[pltpu.HBM(t.shape, t.dtype) for t in (*srcs, *lands)], SDS((8, 128), f32)),
        in_specs=[_HBM] * (2 * n) + [_ANY], out_specs=(_SEM, _SEM, *([_HBM] * (2 * n)), _VMEM),
        input_output_aliases={i: 2 + i for i in range(2 * n)},
        compiler_params=pltpu.CompilerParams(has_side_effects=pltpu.SideEffectType.DATAFLOW_SIDE_EFFECTING),
    )(*[hbm(t) for t in (*srcs, *lands)], dep)
    return res[0], res[1], list(res[2:2 + n]), list(res[2 + n:2 + 2 * n]), res[-1]


def _split_wait(make, send_sems, recv_sems, srcs, lands, after, name):
    n = len(srcs)

    def body(*refs):
        for cp in make(refs[:n], refs[n:2 * n], refs[2 * n], refs[2 * n + 1]):
            cp.wait_send()
            cp.wait_recv()

    res = pl.pallas_call(
        body, name=name,
        out_shape=tuple(pltpu.HBM(t.shape, t.dtype) for t in (*srcs, *lands)),
        in_specs=[_HBM] * (2 * n) + [_SEM, _SEM, _ANY], out_specs=tuple([_HBM] * (2 * n)),
        input_output_aliases={i: i for i in range(2 * n)},
        compiler_params=pltpu.CompilerParams(has_side_effects=pltpu.SideEffectType.DATAFLOW_SIDE_EFFECTING),
    )(*srcs, *lands, send_sems, recv_sems, after)
    return list(res[:n]), list(res[n:])


def _adamw_direct(g, land, me_idx, w, m, v, name):
    r, wd = w.shape
    tr = min(r, 256)

    def body(me_ref, *refs):
        g_ref, peers = refs[0], refs[1:N_DEV]
        w_ref, m_ref, v_ref, g_out, d_out, m_out, v_out = refs[N_DEV:]
        gs = g_ref[...].astype(f32)
        for p_ref in peers:
            gs = gs + p_ref[...].astype(f32)
        d, mn, vn = _adamw(w_ref[...], gs, m_ref[...], v_ref[...])
        g_out[...] = gs
        d_out[...] = d
        m_out[...] = mn
        v_out[...] = vn

    tile = pl.BlockSpec((tr, wd), lambda i, me_ref: (i, 0))
    slot = lambda k: pl.BlockSpec((None, tr, wd), lambda i, me_ref: ((me_ref[0] + k) % N_DEV, i, 0))
    return pl.pallas_call(
        body,
        grid_spec=pltpu.PrefetchScalarGridSpec(
            num_scalar_prefetch=1, grid=(r // tr,),
            in_specs=[slot(0)] + [slot(k) for k in range(1, N_DEV)] + [tile, tile, tile],
            out_specs=[tile] * 4),
        out_shape=[SDS((r, wd), f32)] * 4, name=name, compiler_params=_cp(("arbitrary",), 48),
    )(me_idx, g, *([land] * (N_DEV - 1)), w, m, v)


def _pair_sum(g, r1, c_idx, name):
    _, r, w = g.shape
    tr = min(r, 256)

    def body(c_ref, g_ref, r_ref, o_ref):
        o_ref[...] = (g_ref[...].astype(f32) + r_ref[...].astype(f32)).astype(bf16)

    return pl.pallas_call(
        body,
        grid_spec=pltpu.PrefetchScalarGridSpec(
            num_scalar_prefetch=1, grid=(4, r // tr),
            in_specs=[pl.BlockSpec((None, tr, w), lambda q, i, c_ref: (2 * q + c_ref[0], i, 0)),
                      pl.BlockSpec((None, tr, w), lambda q, i, c_ref: (q, i, 0))],
            out_specs=pl.BlockSpec((None, tr, w), lambda q, i, c_ref: (q, i, 0))),
        out_shape=SDS((4, r, w), bf16), name=name, compiler_params=_cp(("arbitrary", "arbitrary")),
    )(c_idx, g, r1)


def _adamw(w, g, m, v):
    m = ADAM_B1 * m + (1.0 - ADAM_B1) * g
    v = ADAM_B2 * v + (1.0 - ADAM_B2) * (g * g)
    m_hat = m / (1.0 - ADAM_B1 ** ADAM_STEP)
    v_hat = v / (1.0 - ADAM_B2 ** ADAM_STEP)
    delta = -ADAM_LR * (m_hat / (jnp.sqrt(v_hat) + ADAM_EPS) + ADAM_WD * w)
    return delta, m, v


def _adamw_big(part, r2, q_idx, w, m, v, name, row_off=0):
    r, wd = w.shape
    tr = min(r, 256)

    def body(q_ref, p_ref, r_ref, w_ref, m_ref, v_ref, g_out, d_out, m_out, v_out):
        g = p_ref[...].astype(f32)
        for j in range(3):
            g = g + r_ref[j].astype(f32)
        d, mn, vn = _adamw(w_ref[...], g, m_ref[...], v_ref[...])
        g_out[...] = g
        d_out[...] = d
        m_out[...] = mn
        v_out[...] = vn

    tile = pl.BlockSpec((tr, wd), lambda i, q_ref: (i, 0))
    return pl.pallas_call(
        body,
        grid_spec=pltpu.PrefetchScalarGridSpec(
            num_scalar_prefetch=1, grid=(r // tr,),
            in_specs=[pl.BlockSpec((None, tr, wd), lambda i, q_ref: (q_ref[0], row_off + i, 0)),
                      pl.BlockSpec((3, tr, wd), lambda i, q_ref: (0, row_off + i, 0)), tile, tile, tile],
            out_specs=[tile] * 4),
        out_shape=[SDS((r, wd), f32)] * 4, name=name, compiler_params=_cp(("arbitrary",), 48),
    )(q_idx, part, r2, w, m, v)


_SMALL_ROWS = 24


def _pack_small(st_emb, vec_rnn, st_out, dsr, db_in, dmeta):
    def body(se_ref, vr_ref, so_ref, dsr_ref, db_ref, dm_ref, sm_ref, sm2_ref):
        sm_ref[...] = jnp.zeros_like(sm_ref)
        sm2_ref[...] = jnp.zeros_like(sm2_ref)
        sm_ref[0:2, :] = se_ref[0:2, :]
        sm_ref[2:3, :] = vr_ref[3:4, :]
        sm_ref[3:6, :] = vr_ref[0:3, :]
        sm_ref[6:7, :] = so_ref[2:3, :]
        sm_ref[7:9, :] = so_ref[0:2, :]
        for h in range(N_KV):
            sm_ref[9:10, h * GROUP:(h + 1) * GROUP] = _colsum(dsr_ref[h])
        for j in range(6):
            sm_ref[16 + j:17 + j, :] = db_ref[0:1, j * D:(j + 1) * D]
        sm_ref[22:23, 0:D_IN - 6 * D] = db_ref[0:1, 6 * D:D_IN]
        for s in range(N_DEV):
            sm2_ref[s, 0:N_META, :] = dm_ref[:, s * 256:(s + 1) * 256]
            sm2_ref[s, N_META:N_META + CONV_WIDTH, :] = vr_ref[4:8, s * 256:(s + 1) * 256]

    return pl.pallas_call(
        body, out_shape=[SDS((_SMALL_ROWS, D), f32), SDS((N_DEV, 24, 256), f32)],
        name="pack_small", compiler_params=_cp(None),
    )(st_emb, vec_rnn, st_out, dsr, db_in, dmeta)


def _small_allreduce(sm, sm2):
    def body(sm_ref, sm2_ref, o_ref, o2_ref, buf, buf2, send_sems, recv_sems):
        x, y, c = _place()
        me = _dev(x, y, c)
        copies = []
        for f in range(1, N_DEV):
            fx, fy, fc = f // 4, (f // 2) % 2, f % 2
            peer = ((x + fx) % 2, (y + fy) % 2, (c + fc) % 2)
            for t, (src, dst) in enumerate(((sm_ref, buf), (sm2_ref, buf2))):
                k = 2 * (f - 1) + t
                copies.append(pltpu.make_async_remote_copy(
                    src_ref=src, dst_ref=dst.at[me], send_sem=send_sems.at[k], recv_sem=recv_sems.at[k],
                    device_id=peer, device_id_type=MESH))
        for cp in copies:
            cp.start()
        buf[me] = sm_ref[...]
        buf2[me] = sm2_ref[...]
        for cp in copies:
            cp.wait()
        acc, acc2 = buf[0], buf2[0]
        for e in range(1, N_DEV):
            acc, acc2 = acc + buf[e], acc2 + buf2[e]
        o_ref[...] = acc
        o2_ref[...] = acc2

    return pl.pallas_call(
        body, in_specs=[_VMEM, _VMEM], out_specs=[_VMEM, _VMEM],
        out_shape=[SDS(sm.shape, f32), SDS(sm2.shape, f32)],
        scratch_shapes=[pltpu.VMEM((N_DEV, *sm.shape), f32), pltpu.VMEM((N_DEV, *sm2.shape), f32),
                        pltpu.SemaphoreType.DMA((14,)), pltpu.SemaphoreType.DMA((14,))],
        name="small_allreduce",
    )(sm, sm2)


_SMALL_ROW_OF = {"ln_emb_g": 0, "ln_emb_b": 1, "conv_b": 2, "b_ra": 3, "b_ri": 4, "lru_lambda": 5, "b_o": 6,
                 "ln_g": 7, "ln_b": 8}
_SMALL_NAMES = ["ln_emb_g", "ln_emb_b", "conv_b", "b_ra", "b_ri", "lru_lambda", "b_o", "ln_g", "ln_b",
                "sinks", "b_in", "meta_tokens", "conv_w"]


def _small_update(sm, sm2_mine, wmv):
    def grad_of(name, sm_ref, s2_ref):
        if name in _SMALL_ROW_OF:
            r = _SMALL_ROW_OF[name]
            return sm_ref[r:r + 1, :]
        if name == "sinks":
            return sm_ref[9:10, 0:N_KV * GROUP]
        if name == "b_in":
            return jnp.concatenate([sm_ref[16 + j:17 + j, :] for j in range(7)], axis=1)[:, :D_IN]
        if name == "meta_tokens":
            return s2_ref[0:N_META, :]
        return s2_ref[N_META:N_META + CONV_WIDTH, :]

    def body(*refs):
        sm_ref, s2_ref = refs[0], refs[1]
        ins = refs[2:2 + 3 * len(_SMALL_NAMES)]
        outs = refs[2 + 3 * len(_SMALL_NAMES):]
        for i, name in enumerate(_SMALL_NAMES):
            w_ref, m_ref, v_ref = ins[3 * i:3 * i + 3]
            g = grad_of(name, sm_ref, s2_ref)
            d, mn, vn = _adamw(w_ref[...], g, m_ref[...], v_ref[...])
            outs[4 * i][...] = g
            outs[4 * i + 1][...] = d
            outs[4 * i + 2][...] = mn
            outs[4 * i + 3][...] = vn

    args, out_shape = [sm, sm2_mine], []
    for name in _SMALL_NAMES:
        args += list(wmv[name])
        out_shape += [SDS(wmv[name][0].shape, f32)] * 4
    res = pl.pallas_call(body, out_shape=out_shape, name="small_update", compiler_params=_cp(None))(*args)
    return {name: tuple(res[4 * i:4 * i + 4]) for i, name in enumerate(_SMALL_NAMES)}


_WEIGHTS = ["meta_tokens", "ln_emb_g", "ln_emb_b", "w_in", "b_in", "conv_w", "conv_b", "w_ra", "b_ra", "w_ri",
            "b_ri", "lru_lambda", "sinks", "w_rnn_out", "w_attn_out", "w_o", "b_o", "ln_g", "ln_b"]
_SMALL_2D = {"meta_tokens": (N_META, 256), "conv_w": (CONV_WIDTH, 256), "b_in": (1, D_IN), "sinks": (1, N_KV * GROUP)}


def kernel(x, meta_tokens, ln_emb_g, ln_emb_b, w_in, b_in, conv_w, conv_b, w_ra, b_ra, w_ri, b_ri, lru_lambda, sinks, w_rnn_out, w_attn_out, w_o, b_o, ln_g, ln_b, loss_target, m_meta_tokens, m_ln_emb_g, m_ln_emb_b, m_w_in, m_b_in, m_conv_w, m_conv_b, m_w_ra, m_b_ra, m_w_ri, m_b_ri, m_lru_lambda, m_sinks, m_w_rnn_out, m_w_attn_out, m_w_o, m_b_o, m_ln_g, m_ln_b, v_meta_tokens, v_ln_emb_g, v_ln_emb_b, v_w_in, v_b_in, v_conv_w, v_conv_b, v_w_ra, v_b_ra, v_w_ri, v_b_ri, v_lru_lambda, v_sinks, v_w_rnn_out, v_w_attn_out, v_w_o, v_b_o, v_ln_g, v_ln_b):
    w = dict(meta_tokens=meta_tokens, ln_emb_g=ln_emb_g, ln_emb_b=ln_emb_b, w_in=w_in, b_in=b_in, conv_w=conv_w,
             conv_b=conv_b, w_ra=w_ra, b_ra=b_ra, w_ri=w_ri, b_ri=b_ri, lru_lambda=lru_lambda, sinks=sinks,
             w_rnn_out=w_rnn_out, w_attn_out=w_attn_out, w_o=w_o, b_o=b_o, ln_g=ln_g, ln_b=ln_b)
    m = dict(meta_tokens=m_meta_tokens, ln_emb_g=m_ln_emb_g, ln_emb_b=m_ln_emb_b, w_in=m_w_in, b_in=m_b_in,
             conv_w=m_conv_w, conv_b=m_conv_b, w_ra=m_w_ra, b_ra=m_b_ra, w_ri=m_w_ri, b_ri=m_b_ri,
             lru_lambda=m_lru_lambda, sinks=m_sinks, w_rnn_out=m_w_rnn_out, w_attn_out=m_w_attn_out, w_o=m_w_o,
             b_o=m_b_o, ln_g=m_ln_g, ln_b=m_ln_b)
    v = dict(meta_tokens=v_meta_tokens, ln_emb_g=v_ln_emb_g, ln_emb_b=v_ln_emb_b, w_in=v_w_in, b_in=v_b_in,
             conv_w=v_conv_w, conv_b=v_conv_b, w_ra=v_w_ra, b_ra=v_b_ra, w_ri=v_w_ri, b_ri=v_b_ri,
             lru_lambda=v_lru_lambda, sinks=v_sinks, w_rnn_out=v_w_rnn_out, w_attn_out=v_w_attn_out, w_o=v_w_o,
             b_o=v_b_o, ln_g=v_ln_g, ln_b=v_ln_b)
    px, py, pc = _place()
    as_idx = lambda t: jnp.reshape(t, (1,)).astype(jnp.int32)
    c_idx, q_idx, me_idx = as_idx(pc), as_idx(2 * px + py), as_idx(_dev(px, py, pc))

    w3_s, wrg_s, small_s = _cast_small(w_rnn_out, w_attn_out, w_o, w_ra, w_ri, meta_tokens, conv_w)
    wg, wrg, smallw, w3_land = _all_gather([_cast_w_in(w_in), wrg_s, small_s], [w3_s])
    w3_pending = _split_start(_copies_direct(True), 7, [w3_s], [w3_land], smallw, "gather_w3_start")
    w_full = _relayout_w_in(wg)

    vec = lambda name: w[name].reshape(1, -1)
    p = {k: vec(k) for k in ("ln_emb_g", "ln_emb_b", "b_in", "conv_b", "b_ra", "b_ri", "lru_lambda", "sinks",
                             "b_o", "ln_g", "ln_b")}
    p["b_in"] = p["b_in"] + w3_pending[4][0:1, 0:1]
    s = _step_branches(x, w_full, wrg, smallw, p)
    w3 = _split_wait(_copies_direct(True), *w3_pending[:4], s["lse"], "gather_w3_wait")[1][0]
    t = _step_merge(s, loss_target, w3, p)
    loss = lax.psum(jnp.sum(t["st_out"][3]), ("x", "y", "c"))

    big = {}
    two_d = lambda name: (w[name].shape[-2], w[name].shape[-1])
    proj = ("w_o", "w_rnn_out", "w_attn_out")
    g_proj = [t[k].reshape(N_DEV, 256, D) for k in ("g_wo", "g_wrnn", "g_wattn")]
    g_pending = _split_start(_copies_direct(False), 7, g_proj, [lax.empty((N_DEV, 256, D), bf16) for _ in proj],
                             p["b_o"], "reduce_proj_start")
    u = _step_backward(s, t, wrg, smallw, p, p["conv_b"] + g_pending[4][0:1, 0:1])

    grads = [u["g_win"], u["g_wrg"].reshape(N_DEV, 2 * RNN_BLOCK, RNN_BLOCK)]
    sib = _split_start(_copies_siblings, 4, grads, [lax.empty((4, *g.shape[1:]), bf16) for g in grads],
                       u["db_in"], "reduce_siblings_start")
    g_proj, g_land = _split_wait(_copies_direct(False), *g_pending[:4], sib[4], "reduce_proj_wait")
    for i, name in enumerate(proj):
        res = _adamw_direct(g_proj[i], g_land[i], me_idx, w[name].reshape(two_d(name)), m[name].reshape(two_d(name)),
                            v[name].reshape(two_d(name)), "adamw_" + name)
        big[name] = tuple(r.reshape(w[name].shape) for r in res)
    grads, r1 = _split_wait(_copies_siblings, *sib[:4], big["w_attn_out"][3], "reduce_siblings_wait")
    parts = [_pair_sum(g, r, c_idx, "pair_sum_%d" % i) for i, (g, r) in enumerate(zip(grads, r1))]
    chp = _split_start(_copies_chips, 3, parts, [lax.empty((3, *q.shape[1:]), bf16) for q in parts], parts[1],
                       "reduce_chips_start")
    u.update(_step_input_grad(u["dz"], w_full, chp[4], t["du32"], x, smallw, p))

    loc = {**t, **u}
    sm, sm2 = _pack_small(loc["st_emb"], loc["vec_rnn"], loc["st_out"], loc["dsr"], loc["db_in"], loc["dmeta"])
    sm, sm2 = _small_allreduce(sm, sm2)
    sm2_mine = lax.dynamic_index_in_dim(sm2, _dev(px, py, pc), 0, keepdims=False)
    two = lambda name, t: t.reshape(_SMALL_2D.get(name, (1, D)))
    small = _small_update(sm, sm2_mine, {k: (two(k, w[k]), two(k, m[k]), two(k, v[k])) for k in _SMALL_NAMES})

    parts, r2 = _split_wait(_copies_chips, *chp[:4], small["b_in"][3], "reduce_chips_wait")
    res = _adamw_big(parts[0], r2[0], q_idx, w["w_in"].reshape(two_d("w_in")), m["w_in"].reshape(two_d("w_in")),
                     v["w_in"].reshape(two_d("w_in")), "adamw_w_in")
    big["w_in"] = tuple(r.reshape(w["w_in"].shape) for r in res)
    for i, name in enumerate(("w_ra", "w_ri")):
        sq = (RNN_BLOCK, RNN_BLOCK)
        res = _adamw_big(parts[1], r2[1], q_idx, w[name].reshape(sq), m[name].reshape(sq), v[name].reshape(sq),
                         "adamw_" + name, row_off=i)
        big[name] = tuple(r.reshape(w[name].shape) for r in res)
    res = dict(big)
    for k in _SMALL_NAMES:
        res[k] = tuple(t.reshape(w[k].shape) for t in small[k])

    outs = [loss, loc["grad_x"]]
    for j in range(4):
        outs += [res[k][j] for k in _WEIGHTS]
    return tuple(outs)
```

```python
import functools

import jax
import jax.numpy as jnp
from jax import lax
from jax.experimental import pallas as pl
from jax.experimental.pallas import tpu as pltpu

f32, bf16 = jnp.float32, jnp.bfloat16
SDS = jax.ShapeDtypeStruct

N_DEV = 8
D = 2048
N_META = 16
BLK = 128
ROW0 = BLK - N_META
N_RNN_BLOCKS = 8
RNN_BLOCK = D // N_RNN_BLOCKS
CONV_WIDTH = 4
LRU_C = 8.0
HEAD_DIM = 64
N_KV = 4
GROUP = 8
HALF = HEAD_DIM // 2
ROPE_THETA = 10000.0
NEG_INF = -1e30
LN_EPS = 1e-5
ALPHA = 2.0 ** 0.25
D_IN = 12800
SHARD_IN = D_IN // N_DEV
OFF_GR, OFF_Q, OFF_K, OFF_V, OFF_GA, OFF_G = 2048, 4096, 6144, 6400, 6656, 8704
ADAM_LR, ADAM_B1, ADAM_B2, ADAM_EPS, ADAM_WD, ADAM_STEP = 1e-3, 0.9, 0.999, 1e-8, 0.01, 10
VMEM_LIMIT_MB = 56
MESH = pl.DeviceIdType.MESH


def _cp(sem=None, vmem_mb=40):
    return pltpu.CompilerParams(dimension_semantics=sem, vmem_limit_bytes=vmem_mb * 2 ** 20)


def _row_chunk(m):
    best = 16
    for c in range(16, 641, 16):
        if m % c == 0:
            best = c
    return best


def _sigmoid(x):
    return 1.0 / (1.0 + jnp.exp(-x))


def _silu_and_grad(x):
    s = _sigmoid(x)
    return x * s, s * (1.0 + x * (1.0 - s))


def _log_sigmoid(x):
    return jnp.minimum(x, 0.0) - jnp.log1p(jnp.exp(-jnp.abs(x)))


def _ln_rows(v, g, b):
    mu = jnp.mean(v, axis=-1, keepdims=True)
    c = v - mu
    var = jnp.mean(c * c, axis=-1, keepdims=True)
    rstd = lax.rsqrt(var + LN_EPS)
    xhat = c * rstd
    return xhat * g + b, xhat, rstd


def _ln_rows_bwd(dy, g, xhat, rstd):
    dxh = dy * g
    m1 = jnp.mean(dxh, axis=-1, keepdims=True)
    m2 = jnp.mean(dxh * xhat, axis=-1, keepdims=True)
    return rstd * (dxh - m1 - xhat * m2)


def _colsum(v):
    return jnp.sum(v, axis=0, keepdims=True)


def _dot(a, b):
    return jnp.dot(a, b, preferred_element_type=f32)


def _dot_nt(a, b):
    return lax.dot_general(a, b, (((1,), (1,)), ((), ())), preferred_element_type=f32)


def _dot_tn(a, b):
    return lax.dot_general(a, b, (((0,), (0,)), ((), ())), preferred_element_type=f32)


def _meta_full(sw_ref):
    return jnp.concatenate([sw_ref[s, 0:N_META, :] for s in range(N_DEV)], axis=1)


def _ln_emb(x, smallw, g_e, b_e):
    seq = x.shape[1]
    rows = seq + BLK
    nb = rows // BLK

    def body(x_ref, sw_ref, g_ref, b_ref, h32_ref, hb_ref, ht_ref):
        i = pl.program_id(0)
        g, b = g_ref[...], b_ref[...]

        def emit(blk):
            h32_ref[...] = blk
            hb_ref[...] = blk.astype(bf16)
            ht_ref[...] = blk.T.astype(bf16)

        @pl.when(i == 0)
        def _():
            hm = _ln_rows(_meta_full(sw_ref), g, b)[0]
            emit(jnp.concatenate([jnp.zeros((ROW0, D), f32), hm], axis=0))

        @pl.when(i > 0)
        def _():
            emit(_ln_rows(x_ref[0], g, b)[0])

    return pl.pallas_call(
        body, grid=(nb,),
        in_specs=[pl.BlockSpec((1, BLK, D), lambda i: (0, jnp.maximum(i - 1, 0), 0)),
                  pl.BlockSpec((N_DEV, 24, 256), lambda i: (0, 0, 0)),
                  pl.BlockSpec((1, D), lambda i: (0, 0)),
                  pl.BlockSpec((1, D), lambda i: (0, 0))],
        out_specs=[pl.BlockSpec((BLK, D), lambda i: (i, 0)),
                   pl.BlockSpec((BLK, D), lambda i: (i, 0)),
                   pl.BlockSpec((D, BLK), lambda i: (0, i))],
        out_shape=[SDS((rows, D), f32), SDS((rows, D), bf16), SDS((D, rows), bf16)],
        name="ln_emb", compiler_params=_cp(("arbitrary",)),
    )(x, smallw, g_e, b_e)


def _ln_emb_bwd(dh, du32, x, smallw, g_e):
    seq = x.shape[1]
    rows = seq + BLK
    nb = rows // BLK

    def body(dh_ref, du_ref, x_ref, sw_ref, g_ref, gx_ref, dmeta_ref, st_ref):
        i = pl.program_id(0)
        g = g_ref[...]
        dht = dh_ref[...] + ALPHA * du_ref[...]

        @pl.when(i == 0)
        def _():
            v = jnp.concatenate([jnp.zeros((ROW0, D), f32), _meta_full(sw_ref)], axis=0)
            valid = lax.broadcasted_iota(jnp.int32, (BLK, 1), 0) >= ROW0
            d = jnp.where(valid, dht, 0.0)
            _, xhat, rstd = _ln_rows(v, g, 0.0)
            dv = _ln_rows_bwd(d, g, xhat, rstd)
            dmeta_ref[...] = dv[ROW0:, :]
            st_ref[...] = jnp.concatenate([_colsum(d * xhat), _colsum(d), jnp.zeros((6, D), f32)], axis=0)

        @pl.when(i > 0)
        def _():
            _, xhat, rstd = _ln_rows(x_ref[0], g, 0.0)
            gx_ref[0] = _ln_rows_bwd(dht, g, xhat, rstd)
            st_ref[0:1, :] += _colsum(dht * xhat)
            st_ref[1:2, :] += _colsum(dht)

    return pl.pallas_call(
        body, grid=(nb,),
        in_specs=[pl.BlockSpec((BLK, D), lambda i: (i, 0)),
                  pl.BlockSpec((BLK, D), lambda i: (i, 0)),
                  pl.BlockSpec((1, BLK, D), lambda i: (0, jnp.maximum(i - 1, 0), 0)),
                  pl.BlockSpec((N_DEV, 24, 256), lambda i: (0, 0, 0)),
                  pl.BlockSpec((1, D), lambda i: (0, 0))],
        out_specs=[pl.BlockSpec((1, BLK, D), lambda i: (0, jnp.maximum(i - 1, 0), 0)),
                   pl.BlockSpec((N_META, D), lambda i: (0, 0)),
                   pl.BlockSpec((8, D), lambda i: (0, 0))],
        out_shape=[SDS((1, seq, D), f32), SDS((N_META, D), f32), SDS((8, D), f32)],
        name="ln_emb_bwd", compiler_params=_cp(("arbitrary",)),
    )(dh, du32, x, smallw, g_e)


def _mm(a, b, *, name, nt=False, sel=None, bias=None, out_dtype=f32, tn=512):
    m, k = a.shape
    cm = _row_chunk(m)
    stacked = sel is not None
    if stacked:
        n = D
        if nt:
            b_spec = pl.BlockSpec((tn // 256, None, 256, D), lambda j: (j, sel, 0, 0))
        else:
            b_spec = pl.BlockSpec((N_DEV, None, 256, tn), lambda j: (0, sel, 0, j))
    elif nt:
        n = b.shape[0]
        b_spec = pl.BlockSpec((tn, k), lambda j: (j, 0))
    else:
        n = b.shape[1]
        b_spec = pl.BlockSpec((k, tn), lambda j: (0, j))
    in_specs = [pl.BlockSpec((m, k), lambda j: (0, 0)), b_spec]
    args = [a, b]
    if bias is not None:
        in_specs.append(pl.BlockSpec((1, tn), lambda j: (0, j)))
        args.append(bias)

    def body(*refs):
        a_ref, b_ref, o_ref = refs[0], refs[1], refs[-1]
        bm = b_ref[...]
        if stacked:
            bm = bm.reshape((tn, D) if nt else (D, tn))
        for c in range(m // cm):
            acc = (_dot_nt if nt else _dot)(a_ref[c * cm:(c + 1) * cm, :], bm)
            if bias is not None:
                acc = acc + refs[2][...]
            o_ref[c * cm:(c + 1) * cm, :] = acc.astype(out_dtype)

    return pl.pallas_call(
        body, grid=(n // tn,), in_specs=in_specs,
        out_specs=pl.BlockSpec((m, tn), lambda j: (0, j)),
        out_shape=SDS((m, n), out_dtype), name=name, compiler_params=_cp(("arbitrary",), 48),
    )(*args)


def _mm_dh(dz, w_full, after):
    rows = dz.shape[0]
    tk, tn = 1280, 1024
    cm = _row_chunk(rows)

    def body(a_ref, w_ref, after_ref, o_ref):
        kk = pl.program_id(1)
        for c in range(rows // cm):
            acc = _dot_nt(a_ref[c * cm:(c + 1) * cm, :], w_ref[...])

            @pl.when(kk == 0)
            def _():
                o_ref[c * cm:(c + 1) * cm, :] = acc

            @pl.when(kk > 0)
            def _():
                o_ref[c * cm:(c + 1) * cm, :] += acc

    return pl.pallas_call(
        body, grid=(D // tn, D_IN // tk),
        in_specs=[pl.BlockSpec((rows, tk), lambda j, kk: (0, kk)),
                  pl.BlockSpec((tn, tk), lambda j, kk: (j, kk)),
                  pl.BlockSpec(memory_space=pl.ANY)],
        out_specs=pl.BlockSpec((rows, tn), lambda j, kk: (0, j)),
        out_shape=SDS((rows, D), f32), name="mm_dh", compiler_params=_cp(("arbitrary", "arbitrary"), 48),
    )(dz, w_full, after)


W_IN_HALF = D // 2


def _mm_dwin(h_t, dz, half, after):
    rows = dz.shape[0]
    pair = 2 * SHARD_IN
    tm = 512
    mt = W_IN_HALF // tm

    def body(a_ref, dz_ref, after_ref, o_ref, db_ref):
        acc = _dot(a_ref[...], dz_ref[...])
        o_ref[0] = acc[:, :SHARD_IN].astype(bf16)
        o_ref[1] = acc[:, SHARD_IN:].astype(bf16)

        @pl.when(pl.program_id(1) == 0)
        def _():
            def step(i, s):
                blk = dz_ref[pl.ds(pl.multiple_of(i * BLK, BLK), BLK), :].astype(f32)
                return s + blk.reshape(BLK // 8, 8, pair).sum(axis=0)
            s = lax.fori_loop(0, rows // BLK, step, jnp.zeros((8, pair), f32))
            db_ref[...] = jnp.broadcast_to(_colsum(s), (8, pair))

    return pl.pallas_call(
        body, grid=(N_DEV // 2, mt),
        in_specs=[pl.BlockSpec((tm, rows), lambda p, i: (half * mt + i, 0)),
                  pl.BlockSpec((rows, pair), lambda p, i: (0, p)),
                  pl.BlockSpec(memory_space=pl.ANY)],
        out_specs=[pl.BlockSpec((2, tm, SHARD_IN), lambda p, i: (p, i, 0)),
                   pl.BlockSpec((8, pair), lambda p, i: (0, p))],
        out_shape=[SDS((N_DEV, W_IN_HALF, SHARD_IN), bf16), SDS((8, D_IN), f32)],
        name="mm_dwin_%d" % half, compiler_params=_cp(("arbitrary", "arbitrary"), VMEM_LIMIT_MB),
    )(h_t, dz, after)


def _relayout_w_in(wg):
    tm = 256

    def body(i_ref, o_ref):
        for d in range(N_DEV):
            o_ref[:, d * SHARD_IN:(d + 1) * SHARD_IN] = i_ref[d]

    return pl.pallas_call(
        body, grid=(D // tm,),
        in_specs=[pl.BlockSpec((N_DEV, tm, SHARD_IN), lambda i: (0, i, 0))],
        out_specs=pl.BlockSpec((tm, D_IN), lambda i: (i, 0)),
        out_shape=SDS((D, D_IN), bf16), name="relayout_w_in", compiler_params=_cp(("arbitrary",)),
    )(wg)


SCAN_ROWS = 32


def _scan8(a, b, reverse):
    idx = lax.broadcasted_iota(jnp.int32, a.shape, 0)
    for s in (1, 2, 4):
        sh = 8 - s if reverse else s
        a_sh, b_sh = pltpu.roll(a, sh, 0), pltpu.roll(b, sh, 0)
        m = (idx < 8 - s) if reverse else (idx >= s)
        b = jnp.where(m, a * b_sh + b, b)
        a = jnp.where(m, a * a_sh, a)
    return a, b


def _shift_rows(prev8, cur, k):
    ext = jnp.concatenate([prev8, cur], axis=0)
    return pltpu.roll(ext, k, 0)[8:, :]


def _gates(xc, w_ra, b_ra, w_ri, b_ri, ls):
    xb = xc.astype(bf16)
    r = _sigmoid(_dot(xb, w_ra) + b_ra)
    ig = _sigmoid(_dot(xb, w_ri) + b_ri)
    la = LRU_C * r * ls
    a = jnp.exp(la)
    mult = jnp.sqrt(jnp.tanh(-la) * (1.0 + a * a))
    return xb, r, ig, a, mult


_RNN_IN_SPECS = lambda rows: [
    pl.BlockSpec((1, 24, 256), lambda n: (n, 0, 0)),
    pl.BlockSpec((1, RNN_BLOCK), lambda n: (0, n)),
    pl.BlockSpec((N_DEV, 2, None, 32, RNN_BLOCK), lambda n: (0, 0, n, 0, 0)),
    pl.BlockSpec((1, RNN_BLOCK), lambda n: (0, n)),
    pl.BlockSpec((1, RNN_BLOCK), lambda n: (0, n)),
    pl.BlockSpec((1, RNN_BLOCK), lambda n: (0, n)),
]


def _rnn_fwd(z, smallw, conv_b, wrg, b_ra, b_ri, lam):
    rows = z.shape[0]
    nb = rows // BLK
    col = lambda off: pl.BlockSpec((rows, RNN_BLOCK), lambda n: (0, off // RNN_BLOCK + n))

    def body(xr_ref, gr_ref, sw_ref, cb_ref, w_ref, bra_ref, bri_ref, lam_ref, xc_ref, hr_ref, ya_ref, yat_ref, a_s):
        cw = sw_ref[0, N_META:24, :]
        cb = cb_ref[...]
        w_ra = w_ref[:, 0].reshape(RNN_BLOCK, RNN_BLOCK)
        w_ri = w_ref[:, 1].reshape(RNN_BLOCK, RNN_BLOCK)
        b_ra_v, b_ri_v = bra_ref[...], bri_ref[...]
        ls = _log_sigmoid(lam_ref[...])
        rid = lax.broadcasted_iota(jnp.int32, (BLK, 1), 0)

        def blk_step(i, carry):
            r0 = pl.multiple_of(i * BLK, BLK)
            grow = rid + r0
            valid = grow >= ROW0
            cur = jnp.where(valid, xr_ref[pl.ds(r0, BLK), :], 0.0)
            prev8 = xr_ref[pl.ds(pl.multiple_of(jnp.maximum(r0 - 8, 0), 8), 8), :] * (i > 0).astype(f32)
            xc = cb + cw[0:1] * cur
            for k in range(1, CONV_WIDTH):
                xc = xc + cw[k:k + 1] * _shift_rows(prev8, cur, k)
            xc_ref[pl.ds(r0, BLK), :] = xc
            _, _, ig, a, mult = _gates(xc, w_ra, b_ra_v, w_ri, b_ri_v, ls)
            mult = jnp.where(grow == ROW0, 1.0, mult)
            a_s[pl.ds(r0, BLK), :] = a
            hr_ref[pl.ds(r0, BLK), :] = jnp.where(valid, mult * ig * xc, 0.0)
            return carry

        lax.fori_loop(0, nb, blk_step, 0)

        def scan_step(j, carry):
            r0 = pl.multiple_of(j * SCAN_ROWS, SCAN_ROWS)
            tiles = [_scan8(a_s[pl.ds(r0 + 8 * k, 8), :], hr_ref[pl.ds(r0 + 8 * k, 8), :], False)
                     for k in range(SCAN_ROWS // 8)]
            for k, (a, b) in enumerate(tiles):
                h = b + a * carry
                hr_ref[pl.ds(r0 + 8 * k, 8), :] = h
                carry = jnp.broadcast_to(h[7:8, :], (8, RNN_BLOCK))
            return carry

        lax.fori_loop(0, rows // SCAN_ROWS, scan_step, jnp.zeros((8, RNN_BLOCK), f32))

        def gate_step(i, carry):
            r0 = pl.multiple_of(i * BLK, BLK)
            ya_ref[pl.ds(r0, BLK), :] = (hr_ref[pl.ds(r0, BLK), :]
                                         * _silu_and_grad(gr_ref[pl.ds(r0, BLK), :])[0]).astype(bf16)
            return carry

        lax.fori_loop(0, nb, gate_step, 0)
        yat_ref[...] = ya_ref[...].astype(f32).T.astype(bf16)

    return pl.pallas_call(
        body, grid=(N_RNN_BLOCKS,),
        in_specs=[col(0), col(OFF_GR)] + _RNN_IN_SPECS(rows),
        out_specs=[pl.BlockSpec((rows, RNN_BLOCK), lambda n: (0, n))] * 3
                  + [pl.BlockSpec((RNN_BLOCK, rows), lambda n: (n, 0))],
        out_shape=[SDS((rows, D), f32), SDS((rows, D), f32), SDS((rows, D), bf16), SDS((D, rows), bf16)],
        scratch_shapes=[pltpu.VMEM((rows, RNN_BLOCK), f32)],
        name="rnn_fwd", compiler_params=_cp(("arbitrary",)),
    )(z, z, smallw, conv_b, wrg, b_ra, b_ri, lam)


def _rnn_bwd(dya, hr, xc, z, smallw, conv_b, wrg, b_ra, b_ri, lam):
    rows = z.shape[0]
    nb = rows // BLK
    col = lambda off: pl.BlockSpec((rows, RNN_BLOCK), lambda n: (0, off // RNN_BLOCK + n))
    blk = pl.BlockSpec((rows, RNN_BLOCK), lambda n: (0, n))

    def body(dya_ref, hr_ref, xc_ref, xr_ref, gr_ref, sw_ref, cb_ref, w_ref, bra_ref, bri_ref, lam_ref,
             dxr_ref, dgr_ref, dw_ref, vec_ref, a_s, lam_s, dxc_s, r_s, ig_s, mult_s, dw_s):
        cw = sw_ref[0, N_META:24, :]
        w_ra = w_ref[:, 0].reshape(RNN_BLOCK, RNN_BLOCK)
        w_ri = w_ref[:, 1].reshape(RNN_BLOCK, RNN_BLOCK)
        b_ra_v, b_ri_v = bra_ref[...], bri_ref[...]
        lam_v = lam_ref[...]
        ls = _log_sigmoid(lam_v)
        rid = lax.broadcasted_iota(jnp.int32, (BLK, 1), 0)
        zrow = jnp.zeros((1, RNN_BLOCK), f32)

        def p1(i, carry):
            r0 = pl.multiple_of(i * BLK, BLK)
            sl = pl.ds(r0, BLK)
            _, r, ig, a, mult = _gates(xc_ref[sl, :], w_ra, b_ra_v, w_ri, b_ri_v, ls)
            a_s[sl, :] = a
            r_s[sl, :] = r
            ig_s[sl, :] = ig
            mult_s[sl, :] = mult
            sg, dsg = _silu_and_grad(gr_ref[sl, :])
            d = dya_ref[sl, :]
            lam_s[sl, :] = d * sg
            dgr_ref[sl, :] = (d * hr_ref[sl, :] * dsg).astype(bf16)
            return carry

        lax.fori_loop(0, nb, p1, 0)

        def p2(jj, carry):
            r0 = pl.multiple_of((rows // SCAN_ROWS - 1 - jj) * SCAN_ROWS, SCAN_ROWS)
            idx = lax.broadcasted_iota(jnp.int32, (8, RNN_BLOCK), 0)
            tiles = []
            for k in range(SCAN_ROWS // 8):
                sl = pl.ds(r0 + 8 * k, 8)
                a, g = a_s[sl, :], lam_s[sl, :]
                tiles.append((g, *_scan8(a, a * g, True)))
            for k in reversed(range(SCAN_ROWS // 8)):
                g, ca, cb_ = tiles[k]
                mu = cb_ + ca * carry
                lam_s[pl.ds(r0 + 8 * k, 8), :] = g + jnp.where(idx < 7, pltpu.roll(mu, 7, 0), carry)
                carry = jnp.broadcast_to(mu[0:1, :], (8, RNN_BLOCK))
            return carry

        lax.fori_loop(0, rows // SCAN_ROWS, p2, jnp.zeros((8, RNN_BLOCK), f32))

        dw_s[...] = jnp.zeros_like(dw_s)

        def p3(i, carry):
            d_bra, d_bri, d_ls = carry
            r0 = pl.multiple_of(i * BLK, BLK)
            sl = pl.ds(r0, BLK)
            grow = rid + r0
            valid = grow >= ROW0
            first = grow == ROW0
            xcv = xc_ref[sl, :]
            xb = xcv.astype(bf16)
            r, ig, a = r_s[sl, :], ig_s[sl, :], a_s[sl, :]
            mult = jnp.where(first, 1.0, mult_s[sl, :])
            lam_t = lam_s[sl, :]
            du = jnp.where(valid, lam_t, 0.0)
            hprev = _shift_rows(hr_ref[pl.ds(pl.multiple_of(jnp.maximum(r0 - 8, 0), 8), 8), :] * (i > 0).astype(f32), hr_ref[sl, :], 1)
            da = lam_t * hprev
            dmult = jnp.where(first, 0.0, du * ig * xcv)
            di = du * mult * xcv
            dxc = du * mult * ig
            ratio = jnp.where(valid & jnp.logical_not(first), a * a / mult, 0.0)
            dla = da * a - dmult * ratio
            dpr = (dla * (LRU_C * ls)) * r * (1.0 - r)
            dpi = di * ig * (1.0 - ig)
            dprb, dpib = dpr.astype(bf16), dpi.astype(bf16)
            dw_s[0] += _dot_tn(xb, dprb)
            dw_s[1] += _dot_tn(xb, dpib)
            dxc_s[sl, :] = dxc + _dot_nt(dprb, w_ra) + _dot_nt(dpib, w_ri)
            return d_bra + _colsum(dpr), d_bri + _colsum(dpi), d_ls + _colsum(dla * (LRU_C * r))

        d_bra, d_bri, d_ls = lax.fori_loop(0, nb, p3, (zrow, zrow, zrow))

        def p4(i, carry):
            d_cb, d_w0, d_w1, d_w2, d_w3 = carry
            r0 = pl.multiple_of(i * BLK, BLK)
            sl = pl.ds(r0, BLK)
            grow = rid + r0
            valid = grow >= ROW0
            dxc = dxc_s[sl, :]
            nxt = dxc_s[pl.ds(pl.multiple_of(jnp.minimum(r0 + BLK, rows - 8), 8), 8), :] * (i < nb - 1).astype(f32)
            ext = jnp.concatenate([dxc, nxt], axis=0)
            dxr = cw[0:1] * dxc
            for k in range(1, CONV_WIDTH):
                dxr = dxr + cw[k:k + 1] * pltpu.roll(ext, BLK + 8 - k, 0)[:BLK, :]
            dxr_ref[sl, :] = jnp.where(valid, dxr, 0.0).astype(bf16)
            cur = jnp.where(valid, xr_ref[sl, :], 0.0)
            prev8 = xr_ref[pl.ds(pl.multiple_of(jnp.maximum(r0 - 8, 0), 8), 8), :] * (i > 0).astype(f32)
            dws = [d_w0 + _colsum(dxc * cur)]
            for k, acc in ((1, d_w1), (2, d_w2), (3, d_w3)):
                dws.append(acc + _colsum(dxc * _shift_rows(prev8, cur, k)))
            return (d_cb + _colsum(dxc), *dws)

        d_cb, d_w0, d_w1, d_w2, d_w3 = lax.fori_loop(0, nb, p4, (zrow,) * 5)

        d_lam = d_ls * _sigmoid(-lam_v)
        vec_ref[...] = jnp.concatenate([d_bra, d_bri, d_lam, d_cb, d_w0, d_w1, d_w2, d_w3], axis=0)
        dw_ref[:, 0] = dw_s[0].astype(bf16).reshape(N_DEV, 32, RNN_BLOCK)
        dw_ref[:, 1] = dw_s[1].astype(bf16).reshape(N_DEV, 32, RNN_BLOCK)

    return pl.pallas_call(
        body, grid=(N_RNN_BLOCKS,),
        in_specs=[blk, blk, blk, col(0), col(OFF_GR)] + _RNN_IN_SPECS(rows),
        out_specs=[blk, blk,
                   pl.BlockSpec((N_DEV, 2, None, 32, RNN_BLOCK), lambda n: (0, 0, n, 0, 0)),
                   pl.BlockSpec((8, RNN_BLOCK), lambda n: (0, n))],
        out_shape=[SDS((rows, D), bf16), SDS((rows, D), bf16),
                   SDS((N_DEV, 2, N_RNN_BLOCKS, 32, RNN_BLOCK), bf16), SDS((8, D), f32)],
        scratch_shapes=[pltpu.VMEM((rows, RNN_BLOCK), f32)] * 6 + [pltpu.VMEM((2, RNN_BLOCK, RNN_BLOCK), f32)],
        name="rnn_bwd", compiler_params=_cp(("arbitrary",), 48),
    )(dya, hr, xc, z, z, smallw, conv_b, wrg, b_ra, b_ri, lam)


def _rope_tables(rows):
    half = jnp.arange(HALF, dtype=f32)
    inv = ROPE_THETA ** (-half / HALF)
    pos = (jnp.arange(rows) - ROW0).astype(f32)
    ang = pos[:, None] * inv[None, :]
    cos, sin = jnp.cos(ang), jnp.sin(ang)
    cos128 = jnp.concatenate([cos, cos, cos, cos], axis=1)
    sin128 = jnp.concatenate([-sin, sin, -sin, sin], axis=1)
    return cos128, sin128


def _rope128(x, cos128, sin128):
    lane = lax.broadcasted_iota(jnp.int32, x.shape, 1)
    swapped = jnp.where(lane % HEAD_DIM < HALF, pltpu.roll(x, 128 - HALF, 1), pltpu.roll(x, HALF, 1))
    return x * cos128 + swapped * sin128


def _qkv_prep(z, cos128, sin128):
    rows = z.shape[0]

    def body(q_ref, kv_ref, c_ref, s_ref, qo_ref, ko_ref, vo_ref):
        c, s = c_ref[...], s_ref[...]
        for g in range(D // 128):
            qo_ref[:, g * 128:(g + 1) * 128] = (_rope128(q_ref[:, g * 128:(g + 1) * 128], c, s)
                                                * (HEAD_DIM ** -0.5)).astype(bf16)
        for g in range(2):
            kr = _rope128(kv_ref[:, g * 128:(g + 1) * 128], c, s)
            for j in range(2):
                ko_ref[2 * g + j] = kr[:, j * HEAD_DIM:(j + 1) * HEAD_DIM].astype(bf16)
        for h in range(N_KV):
            vo_ref[h] = kv_ref[:, 256 + h * HEAD_DIM:256 + (h + 1) * HEAD_DIM].astype(bf16)

    return pl.pallas_call(
        body, grid=(rows // BLK,),
        in_specs=[pl.BlockSpec((BLK, D), lambda i: (i, OFF_Q // D)),
                  pl.BlockSpec((BLK, 512), lambda i: (i, OFF_K // 512)),
                  pl.BlockSpec((BLK, 128), lambda i: (i, 0)),
                  pl.BlockSpec((BLK, 128), lambda i: (i, 0))],
        out_specs=[pl.BlockSpec((BLK, D), lambda i: (i, 0)),
                   pl.BlockSpec((N_KV, BLK, HEAD_DIM), lambda i: (0, i, 0)),
                   pl.BlockSpec((N_KV, BLK, HEAD_DIM), lambda i: (0, i, 0))],
        out_shape=[SDS((rows, D), bf16), SDS((N_KV, rows, HEAD_DIM), bf16), SDS((N_KV, rows, HEAD_DIM), bf16)],
        name="qkv_prep", compiler_params=_cp(("arbitrary",)),
    )(z, z, cos128, sin128)


def _attn_mask(n):
    qi = n * BLK + lax.broadcasted_iota(jnp.int32, (BLK, 2 * BLK + N_META), 0)
    c = lax.broadcasted_iota(jnp.int32, (BLK, 2 * BLK + N_META), 1)
    jb = (n - 1) * BLK + c
    band = (jb >= BLK) & (jb <= qi) & (qi - jb < BLK)
    meta = (ROW0 + c - 2 * BLK) <= qi
    return ((c < 2 * BLK) & band) | ((c >= 2 * BLK) & meta)


N_KEYS = 2 * BLK + N_META


def _stack_heads(t):
    return jnp.concatenate([t[:, g * HEAD_DIM:(g + 1) * HEAD_DIM] for g in range(GROUP)], axis=0)


def _sink_column(sink_ref, h):
    g = lax.broadcasted_iota(jnp.int32, (GROUP, 1, 1), 0)
    col = jnp.zeros((GROUP, 1, 1), f32)
    for j in range(GROUP):
        col = jnp.where(g == j, sink_ref[h * GROUP + j], col)
    return col


def _kv_specs(last):
    cl = lambda n: jnp.minimum(n, last)
    return [pl.BlockSpec((None, N_META, HEAD_DIM), lambda h, n: (h, ROW0 // N_META, 0)),
            pl.BlockSpec((None, BLK, HEAD_DIM), lambda h, n: (h, jnp.maximum(cl(n) - 1, 0), 0)),
            pl.BlockSpec((None, BLK, HEAD_DIM), lambda h, n: (h, cl(n), 0))]


def _attn_fwd(q_r, k_r, v_b, z, sinks):
    rows = q_r.shape[0]
    nb = rows // BLK

    def body(sink_ref, q_ref, km_ref, kp_ref, kc_ref, vm_ref, vp_ref, vc_ref, ga_ref, o_ref, yb_ref, ybt_ref, lse_ref):
        h, n = pl.program_id(0), pl.program_id(1)
        kk = jnp.concatenate([kp_ref[...], kc_ref[...], km_ref[...]], axis=0)
        vv = jnp.concatenate([vp_ref[...], vc_ref[...], vm_ref[...]], axis=0)
        q2 = _stack_heads(q_ref[...])
        s = jnp.where(_attn_mask(n)[None], _dot_nt(q2, kk).reshape(GROUP, BLK, N_KEYS), NEG_INF)
        sink = _sink_column(sink_ref, h)
        m = jnp.maximum(jnp.max(s, axis=-1, keepdims=True), sink)
        p = jnp.exp(s - m)
        den = jnp.sum(p, axis=-1, keepdims=True) + jnp.exp(sink - m)
        o2 = _dot((p / den).astype(bf16).reshape(GROUP * BLK, N_KEYS), vv)
        lse = m + jnp.log(den)
        for g in range(GROUP):
            o_ref[:, g * HEAD_DIM:(g + 1) * HEAD_DIM] = o2[g * BLK:(g + 1) * BLK]
            lse_ref[:, g:g + 1] = lse[g]
        yb = o_ref[...] * _silu_and_grad(ga_ref[...])[0]
        yb_ref[...] = yb.astype(bf16)
        ybt_ref[...] = yb.T.astype(bf16)

    tile = pl.BlockSpec((BLK, 512), lambda h, n: (n, h))
    return pl.pallas_call(
        body, grid=(N_KV, nb),
        in_specs=[pl.BlockSpec(memory_space=pltpu.SMEM), tile] + _kv_specs(nb - 1) + _kv_specs(nb - 1)
                 + [pl.BlockSpec((BLK, 512), lambda h, n: (n, OFF_GA // 512 + h))],
        out_specs=[tile, tile, pl.BlockSpec((512, BLK), lambda h, n: (h, n)),
                   pl.BlockSpec((None, BLK, GROUP), lambda h, n: (h, n, 0))],
        out_shape=[SDS((rows, D), f32), SDS((rows, D), bf16), SDS((D, rows), bf16),
                   SDS((N_KV, rows, GROUP), f32)],
        name="attn_fwd", compiler_params=_cp(("arbitrary", "arbitrary")),
    )(sinks, q_r, k_r, k_r, k_r, v_b, v_b, v_b, z)


def _attn_bwd(dyb, o32, lse, q_r, k_r, v_b, z, sinks):
    rows = q_r.shape[0]
    nb = rows // BLK
    cl = lambda n: jnp.minimum(n, nb - 1)

    def body(sink_ref, dyb_ref, o_ref, lse_ref, q_ref, km_ref, kp_ref, kc_ref, vm_ref, vp_ref, vc_ref, ga_ref,
             dq_ref, dga_ref, dk_ref, dv_ref, dkm_ref, dvm_ref, dsr_ref, ck_s, cv_s):
        h, n = pl.program_id(0), pl.program_id(1)

        @pl.when(n == 0)
        def _():
            dkm_ref[...] = jnp.zeros_like(dkm_ref)
            dvm_ref[...] = jnp.zeros_like(dvm_ref)
            ck_s[...] = jnp.zeros_like(ck_s)
            cv_s[...] = jnp.zeros_like(cv_s)

        @pl.when(n < nb)
        def _():
            kk = jnp.concatenate([kp_ref[...], kc_ref[...], km_ref[...]], axis=0)
            vv = jnp.concatenate([vp_ref[...], vc_ref[...], vm_ref[...]], axis=0)
            sg, dsg = _silu_and_grad(ga_ref[...])
            dyb_v = dyb_ref[...]
            o_v = o_ref[...]
            dga_ref[...] = (dyb_v * o_v * dsg).astype(bf16)
            q2 = _stack_heads(q_ref[...])
            do2 = _stack_heads(dyb_v * sg)
            lse_v = lse_ref[...]
            lse = jnp.concatenate([lse_v[:, g:g + 1] for g in range(GROUP)], axis=0).reshape(GROUP, BLK, 1)
            delta = jnp.sum(do2 * _stack_heads(o_v), axis=-1, keepdims=True).reshape(GROUP, BLK, 1)
            s = jnp.where(_attn_mask(n)[None], _dot_nt(q2, kk).reshape(GROUP, BLK, N_KEYS), NEG_INF)
            p = jnp.exp(s - lse)
            do2b = do2.astype(bf16)
            ds = (p * (_dot_nt(do2b, vv).reshape(GROUP, BLK, N_KEYS) - delta)).astype(bf16)
            ds = ds.reshape(GROUP * BLK, N_KEYS)
            dsr = -jnp.exp(_sink_column(sink_ref, h) - lse) * delta
            dq2 = _dot(ds, kk)
            for g in range(GROUP):
                dq_ref[:, g * HEAD_DIM:(g + 1) * HEAD_DIM] = dq2[g * BLK:(g + 1) * BLK]
                dsr_ref[:, g:g + 1] = dsr[g]
            dkk = _dot_tn(ds, q2)
            dvv = _dot_tn(p.astype(bf16).reshape(GROUP * BLK, N_KEYS), do2b)
            dk_ref[...] = ck_s[...] + dkk[:BLK]
            dv_ref[...] = cv_s[...] + dvv[:BLK]
            ck_s[...] = dkk[BLK:2 * BLK]
            cv_s[...] = dvv[BLK:2 * BLK]
            dkm_ref[...] += dkk[2 * BLK:]
            dvm_ref[...] += dvv[2 * BLK:]

        @pl.when(n == nb)
        def _():
            dk_ref[...] = ck_s[...]
            dv_ref[...] = cv_s[...]

    tile = pl.BlockSpec((BLK, 512), lambda h, n: (cl(n), h))
    kvout = pl.BlockSpec((None, BLK, HEAD_DIM), lambda h, n: (h, jnp.maximum(n - 1, 0), 0))
    mout = pl.BlockSpec((None, N_META, HEAD_DIM), lambda h, n: (h, 0, 0))
    stat = pl.BlockSpec((None, BLK, GROUP), lambda h, n: (h, cl(n), 0))
    return pl.pallas_call(
        body, grid=(N_KV, nb + 1),
        in_specs=[pl.BlockSpec(memory_space=pltpu.SMEM), tile, tile, stat, tile] + _kv_specs(nb - 1)
                 + _kv_specs(nb - 1) + [pl.BlockSpec((BLK, 512), lambda h, n: (cl(n), OFF_GA // 512 + h))],
        out_specs=[tile, tile, kvout, kvout, mout, mout, stat],
        out_shape=[SDS((rows, D), f32), SDS((rows, D), bf16),
                   SDS((N_KV, rows, HEAD_DIM), f32), SDS((N_KV, rows, HEAD_DIM), f32),
                   SDS((N_KV, N_META, HEAD_DIM), f32), SDS((N_KV, N_META, HEAD_DIM), f32),
                   SDS((N_KV, rows, GROUP), f32)],
        scratch_shapes=[pltpu.VMEM((BLK, HEAD_DIM), f32), pltpu.VMEM((BLK, HEAD_DIM), f32)],
        name="attn_bwd", compiler_params=_cp(("arbitrary", "arbitrary")),
    )(sinks, dyb, o32, lse, q_r, k_r, k_r, k_r, v_b, v_b, v_b, z)


def _qkv_finish(dq, dk, dv, dkm, dvm, cos128, sin128):
    rows = dq.shape[0]

    def body(dq_ref, dk_ref, dv_ref, dkm_ref, dvm_ref, c_ref, s_ref, oq_ref, okv_ref):
        first = (pl.program_id(0) == 0).astype(f32)
        c, s = c_ref[...], -s_ref[...]
        for g in range(D // 128):
            oq_ref[:, g * 128:(g + 1) * 128] = (_rope128(dq_ref[:, g * 128:(g + 1) * 128], c, s)
                                                * (HEAD_DIM ** -0.5)).astype(bf16)
        pad = jnp.zeros((ROW0, HEAD_DIM), f32)
        ks = [dk_ref[h] + first * jnp.concatenate([pad, dkm_ref[h]], axis=0) for h in range(N_KV)]
        vs = [dv_ref[h] + first * jnp.concatenate([pad, dvm_ref[h]], axis=0) for h in range(N_KV)]
        for g in range(2):
            kp = jnp.concatenate([ks[2 * g], ks[2 * g + 1]], axis=1)
            okv_ref[:, g * 128:(g + 1) * 128] = _rope128(kp, c, s).astype(bf16)
            okv_ref[:, 256 + g * 128:256 + (g + 1) * 128] = jnp.concatenate([vs[2 * g], vs[2 * g + 1]], axis=1).astype(bf16)

    kv = pl.BlockSpec((N_KV, BLK, HEAD_DIM), lambda i: (0, i, 0))
    mt = pl.BlockSpec((N_KV, N_META, HEAD_DIM), lambda i: (0, 0, 0))
    return pl.pallas_call(
        body, grid=(rows // BLK,),
        in_specs=[pl.BlockSpec((BLK, D), lambda i: (i, 0)), kv, kv, mt, mt,
                  pl.BlockSpec((BLK, 128), lambda i: (i, 0)), pl.BlockSpec((BLK, 128), lambda i: (i, 0))],
        out_specs=[pl.BlockSpec((BLK, D), lambda i: (i, 0)), pl.BlockSpec((BLK, 512), lambda i: (i, 0))],
        out_shape=[SDS((rows, D), bf16), SDS((rows, 512), bf16)],
        name="qkv_finish", compiler_params=_cp(("arbitrary",)),
    )(dq, dk, dv, dkm, dvm, cos128, sin128)


_TW = 512


def _mix_specs(rows):
    tr = _row_chunk(rows)
    tile = pl.BlockSpec((tr, _TW), lambda i, j: (i, j))
    ga = pl.BlockSpec((tr, _TW), lambda i, j: (i, OFF_G // _TW + j))
    gb = pl.BlockSpec((tr, _TW), lambda i, j: (i, (OFF_G + D) // _TW + j))
    return (rows // tr, D // _TW), tile, ga, gb


def _mix_fwd(y_a, y_b, z):
    rows = y_a.shape[0]
    tw = 256
    col = lambda off: pl.BlockSpec((rows, tw), lambda j: (0, off // tw + j))

    def body(ya_ref, yb_ref, ga_ref, gb_ref, o_ref, ot_ref):
        mixed = _sigmoid(ga_ref[...]) * ya_ref[...] + _sigmoid(gb_ref[...]) * yb_ref[...]
        o_ref[...] = mixed.astype(bf16)
        ot_ref[...] = mixed.T.astype(bf16)

    return pl.pallas_call(
        body, grid=(D // tw,), in_specs=[col(0), col(0), col(OFF_G), col(OFF_G + D)],
        out_specs=[col(0), pl.BlockSpec((tw, rows), lambda j: (j, 0))],
        out_shape=[SDS((rows, D), bf16), SDS((D, rows), bf16)],
        name="mix_fwd", compiler_params=_cp(("arbitrary",)),
    )(y_a, y_b, z, z)


def _mix_bwd(dmixed, y_a, y_b, z):
    rows = y_a.shape[0]
    grid, _mix_tile, _mix_ga, _mix_gb = _mix_specs(rows)

    def body(dm_ref, ya_ref, yb_ref, ga_ref, gb_ref, dya_ref, dyb_ref, dga_ref, dgb_ref):
        dm = dm_ref[...]
        sa, sb = _sigmoid(ga_ref[...]), _sigmoid(gb_ref[...])
        dya_ref[...] = (dm * sa).astype(bf16)
        dyb_ref[...] = (dm * sb).astype(bf16)
        dga_ref[...] = (dm * ya_ref[...] * sa * (1.0 - sa)).astype(bf16)
        dgb_ref[...] = (dm * yb_ref[...] * sb * (1.0 - sb)).astype(bf16)

    return pl.pallas_call(
        body, grid=grid, in_specs=[_mix_tile, _mix_tile, _mix_tile, _mix_ga, _mix_gb],
        out_specs=[_mix_tile] * 4, out_shape=[SDS((rows, D), bf16)] * 4,
        name="mix_bwd", compiler_params=_cp(("arbitrary", "arbitrary")),
    )(dmixed, y_a, y_b, z, z)


def _final_ln(out32, h32, tgt, ln_g, ln_b):
    rows = out32.shape[0]

    def body(o_ref, h_ref, t_ref, g_ref, b_ref, du_ref, dub_ref, st_ref):
        i = pl.program_id(0)
        g = g_ref[...]
        y, xhat, rstd = _ln_rows(ALPHA * h_ref[...] + o_ref[...], g, b_ref[...])
        e = jnp.where(i > 0, y - t_ref[0], 0.0)
        dy = e * (1.0 / D)
        du = _ln_rows_bwd(dy, g, xhat, rstd)
        du_ref[...] = du
        dub_ref[...] = du.astype(bf16)
        st = jnp.concatenate([_colsum(dy * xhat), _colsum(dy), _colsum(du), _colsum(e * e) * (0.5 / D),
                              jnp.zeros((4, D), f32)], axis=0)

        @pl.when(i == 0)
        def _():
            st_ref[...] = st

        @pl.when(i > 0)
        def _():
            st_ref[...] += st

    row = pl.BlockSpec((BLK, D), lambda i: (i, 0))
    vec = pl.BlockSpec((1, D), lambda i: (0, 0))
    return pl.pallas_call(
        body, grid=(rows // BLK,),
        in_specs=[row, row, pl.BlockSpec((1, BLK, D), lambda i: (0, jnp.maximum(i - 1, 0), 0)), vec, vec],
        out_specs=[row, row, pl.BlockSpec((8, D), lambda i: (0, 0))],
        out_shape=[SDS((rows, D), f32), SDS((rows, D), bf16), SDS((8, D), f32)],
        name="final_ln", compiler_params=_cp(("arbitrary",)),
    )(out32, h32, tgt, ln_g, ln_b)


def _assemble_dz(dxr, dgr, dq, dkv, dga, dma, dmb):
    rows = dxr.shape[0]
    parts = [(dxr, D), (dgr, D), (dq, D), (dkv, 512), (dga, D), (dma, D), (dmb, D)]

    def body(*refs):
        o_ref = refs[-1]
        off = 0
        for r, (_, w) in zip(refs[:-1], parts):
            o_ref[:, off:off + w] = r[...]
            off += w

    return pl.pallas_call(
        body, grid=(rows // BLK,),
        in_specs=[pl.BlockSpec((BLK, w), lambda i: (i, 0)) for _, w in parts],
        out_specs=pl.BlockSpec((BLK, D_IN), lambda i: (i, 0)),
        out_shape=SDS((rows, D_IN), bf16), name="assemble_dz", compiler_params=_cp(("arbitrary",)),
    )(*[p for p, _ in parts])


def _step_branches(x, w_full, wrg, smallw, p):
    rows = x.shape[1] + BLK
    cos128, sin128 = _rope_tables(rows)
    sinks = p["sinks"].reshape(N_KV * GROUP)
    h32, hb, h_t = _ln_emb(x, smallw, p["ln_emb_g"], p["ln_emb_b"])
    z = _mm(hb, w_full, bias=p["b_in"], name="mm_z")
    xc, hr, ya, ya_t = _rnn_fwd(z, smallw, p["conv_b"], wrg, p["b_ra"], p["b_ri"], p["lru_lambda"])
    q_r, k_r, v_b = _qkv_prep(z, cos128, sin128)
    o32, yb, yb_t, lse = _attn_fwd(q_r, k_r, v_b, z, sinks)
    return dict(cos128=cos128, sin128=sin128, sinks=sinks, h32=h32, h_t=h_t, z=z, xc=xc, hr=hr, ya=ya, ya_t=ya_t,
                q_r=q_r, k_r=k_r, v_b=v_b, o32=o32, yb=yb, yb_t=yb_t, lse=lse)


def _step_merge(s, tgt, w3, p):
    ya, yb, z = s["ya"], s["yb"], s["z"]
    y_a = _mm(ya, w3, sel=0, name="mm_ya")
    y_b = _mm(yb, w3, sel=1, name="mm_yb")
    mixed, mixed_t = _mix_fwd(y_a, y_b, z)
    out32 = _mm(mixed, w3, sel=2, bias=p["b_o"], name="mm_out")
    du32, dub, st_out = _final_ln(out32, s["h32"], tgt, p["ln_g"], p["ln_b"])

    g_wo = _mm(mixed_t, dub, out_dtype=bf16, name="mm_dwo")
    dmixed = _mm(dub, w3, sel=2, nt=True, name="mm_dmixed")
    dya_b, dyb_b, dma, dmb = _mix_bwd(dmixed, y_a, y_b, z)
    g_wrnn = _mm(s["ya_t"], dya_b, out_dtype=bf16, name="mm_dwrnn")
    g_wattn = _mm(s["yb_t"], dyb_b, out_dtype=bf16, name="mm_dwattn")
    dya = _mm(dya_b, w3, sel=0, nt=True, name="mm_dya")
    dyb = _mm(dyb_b, w3, sel=1, nt=True, name="mm_dyb")
    return dict(du32=du32, st_out=st_out, dma=dma, dmb=dmb, dya=dya, dyb=dyb, g_wo=g_wo, g_wrnn=g_wrnn,
                g_wattn=g_wattn)


def _step_backward(s, t, wrg, smallw, p, conv_b):
    z = s["z"]
    dxr, dgr, g_wrg, vec_rnn = _rnn_bwd(t["dya"], s["hr"], s["xc"], z, smallw, conv_b, wrg, p["b_ra"], p["b_ri"],
                                        p["lru_lambda"])
    dq_r, dga, dk, dv, dkm, dvm, dsr = _attn_bwd(t["dyb"], s["o32"], s["lse"], s["q_r"], s["k_r"], s["v_b"], z,
                                                 s["sinks"])
    dq, dkv = _qkv_finish(dq_r, dk, dv, dkm, dvm, s["cos128"], s["sin128"])
    dz = _assemble_dz(dxr, dgr, dq, dkv, dga, t["dma"], t["dmb"])
    return dict(vec_rnn=vec_rnn, dsr=dsr, g_wrg=g_wrg, dz=dz)


def _step_input_grad(dz, w_full, after, du32, x, smallw, p):
    dh = _mm_dh(dz, w_full, after)
    grad_x, dmeta, st_emb = _ln_emb_bwd(dh, du32, x, smallw, p["ln_emb_g"])
    return dict(grad_x=grad_x, dmeta=dmeta, st_emb=st_emb)


_ANY = pl.BlockSpec(memory_space=pl.ANY)
_VMEM = pl.BlockSpec(memory_space=pltpu.VMEM)


def _place():
    x, y, c = lax.axis_index("x"), lax.axis_index("y"), lax.axis_index("c")
    return x, y, c


def _dev(px, py, pc):
    return 4 * px + 2 * py + pc


def _cast_w_in(w_in):
    tm = 256

    def body(i_ref, o_ref):
        o_ref[...] = i_ref[0].astype(bf16)

    return pl.pallas_call(
        body, grid=(D // tm,),
        in_specs=[pl.BlockSpec((1, tm, SHARD_IN), lambda i: (0, i, 0))],
        out_specs=pl.BlockSpec((tm, SHARD_IN), lambda i: (i, 0)),
        out_shape=SDS((D, SHARD_IN), bf16), name="cast_w_in", compiler_params=_cp(("arbitrary",)),
    )(w_in)


def _cast_small(w_rnn_out, w_attn_out, w_o, w_ra, w_ri, meta, conv_w):
    def body(a_ref, b_ref, c_ref, ra_ref, ri_ref, m_ref, cw_ref, w3_ref, wrg_ref, sw_ref):
        w3_ref[0] = a_ref[0].astype(bf16)
        w3_ref[1] = b_ref[0].astype(bf16)
        w3_ref[2] = c_ref[0].astype(bf16)
        wrg_ref[0] = ra_ref[0].astype(bf16)
        wrg_ref[1] = ri_ref[0].astype(bf16)
        sw_ref[...] = jnp.concatenate([m_ref[...], cw_ref[0], jnp.zeros((4, 256), f32)], axis=0)

    return pl.pallas_call(
        body,
        out_shape=[SDS((3, 256, D), bf16), SDS((2, N_RNN_BLOCKS, 32, RNN_BLOCK), bf16), SDS((24, 256), f32)],
        name="cast_small", compiler_params=_cp(None),
    )(w_rnn_out, w_attn_out, w_o, w_ra, w_ri, meta, conv_w)


def _all_gather(shards, later):
    n = len(shards)
    nl = len(later)

    def body(*refs):
        ins, outs = refs[:n], refs[n + nl:2 * n + nl]
        send_sems, recv_sems, local_sems = refs[2 * (n + nl):]
        x, y, c = _place()
        me, sibling = (x, y, c), (x, y, 1 - c)
        chips = [(1 - x, y), (x, 1 - y), (1 - x, 1 - y)]

        def copy(a, k, block, to, src=None):
            dst = outs[a].at[_dev(*block)]
            return pltpu.make_async_remote_copy(
                src_ref=dst if src is None else src, dst_ref=dst,
                send_sem=send_sems.at[a * 7 + k], recv_sem=recv_sems.at[a * 7 + k],
                device_id=to, device_id_type=MESH)

        all_ins, all_outs = refs[:n + nl], refs[n + nl:2 * (n + nl)]
        mine = [pltpu.make_async_copy(all_ins[a], all_outs[a].at[_dev(*me)], local_sems.at[a]) for a in range(n + nl)]
        for cp in mine:
            cp.start()
        first = []
        for a in range(n):
            first.append(copy(a, 0, me, sibling, src=ins[a]))
            first += [copy(a, 1 + j, me, (*chip, c), src=ins[a]) for j, chip in enumerate(chips)]
        for cp in first:
            cp.start()
        passed = []
        for a in range(n):
            for j, chip in enumerate(chips):
                copy(a, 1 + j, (*chip, c), me).wait_recv()
                cp = copy(a, 4 + j, (*chip, c), sibling)
                cp.start()
                passed.append(cp)
        for a in range(n):
            copy(a, 0, sibling, me).wait_recv()
            for j, chip in enumerate(chips):
                copy(a, 4 + j, (*chip, 1 - c), me).wait_recv()
        for cp in first + passed:
            cp.wait_send()
        for cp in mine:
            cp.wait()

    return pl.pallas_call(
        body, in_specs=[_ANY] * (n + nl), out_specs=[_ANY] * (n + nl),
        out_shape=[SDS((N_DEV, *s.shape), s.dtype) for s in (*shards, *later)],
        scratch_shapes=[pltpu.SemaphoreType.DMA((7 * n,)), pltpu.SemaphoreType.DMA((7 * n,)),
                        pltpu.SemaphoreType.DMA((n + nl,))],
        name="all_gather_weights",
    )(*shards, *later)


_HBM = pl.BlockSpec(memory_space=pltpu.HBM)
_SEM = pl.BlockSpec(memory_space=pltpu.SEMAPHORE)
_PEER_FLIPS = [(f // 4, (f // 2) % 2, f % 2) for f in range(1, N_DEV)]


def _remote(src, dst, send_sems, recv_sems, k, to):
    return pltpu.make_async_remote_copy(src_ref=src, dst_ref=dst, send_sem=send_sems.at[k], recv_sem=recv_sems.at[k],
                                        device_id=to, device_id_type=MESH)


def _copies_direct(same_src):
    def make(srcs, lands, send_sems, recv_sems):
        x, y, c = _place()
        me = _dev(x, y, c)
        out = []
        for a in range(len(srcs)):
            for k, (fx, fy, fc) in enumerate(_PEER_FLIPS):
                peer = ((x + fx) % 2, (y + fy) % 2, (c + fc) % 2)
                src = srcs[a] if same_src else srcs[a].at[_dev(*peer)]
                out.append(_remote(src, lands[a].at[me], send_sems, recv_sems, 7 * a + k, peer))
        return out
    return make


def _copies_siblings(srcs, lands, send_sems, recv_sems):
    x, y, c = _place()
    return [_remote(srcs[a].at[2 * q + (1 - c)], lands[a].at[q], send_sems, recv_sems, 4 * a + q, (x, y, 1 - c))
            for a in range(len(srcs)) for q in range(4)]


def _copies_chips(srcs, lands, send_sems, recv_sems):
    x, y, c = _place()
    chips = [(1 - x, y), (x, 1 - y), (1 - x, 1 - y)]
    return [_remote(srcs[a].at[2 * qx + qy], lands[a].at[j], send_sems, recv_sems, 3 * a + j, (qx, qy, c))
            for a in range(len(srcs)) for j, (qx, qy) in enumerate(chips)]


def _split_start(make, per_array, srcs, lands, dep, name):
    n = len(srcs)

    def body(*refs):
        send_sems, recv_sems, token = refs[2 * n + 1], refs[2 * n + 2], refs[-1]
        for cp in make(refs[:n], refs[n:2 * n], send_sems, recv_sems):
            cp.start()
        token[...] = jnp.zeros_like(token)

    hbm = lambda t: pltpu.with_memory_space_constraint(t, pltpu.HBM)
    res = pl.pallas_call(
        body, name=name,
        out_shape=(pltpu.SemaphoreType.DMA((per_array * n,)), pltpu.SemaphoreType.DMA((per_array * n,)),
                   *[pltpu.HBM(t.shape, t.dtype) for t in (*srcs, *lands)], SDS((8, 128), f32)),
        in_specs=[_HBM] * (2 * n) + [_ANY], out_specs=(_SEM, _SEM, *([_HBM] * (2 * n)), _VMEM),
        input_output_aliases={i: 2 + i for i in range(2 * n)},
        compiler_params=pltpu.CompilerParams(has_side_effects=pltpu.SideEffectType.DATAFLOW_SIDE_EFFECTING),
    )(*[hbm(t) for t in (*srcs, *lands)], dep)
    return res[0], res[1], list(res[2:2 + n]), list(res[2 + n:2 + 2 * n]), res[-1]


def _split_wait(make, send_sems, recv_sems, srcs, lands, after, name):
    n = len(srcs)

    def body(*refs):
        for cp in make(refs[:n], refs[n:2 * n], refs[2 * n], refs[2 * n + 1]):
            cp.wait_send()
            cp.wait_recv()

    res = pl.pallas_call(
        body, name=name,
        out_shape=tuple(pltpu.HBM(t.shape, t.dtype) for t in (*srcs, *lands)),
        in_specs=[_HBM] * (2 * n) + [_SEM, _SEM, _ANY], out_specs=tuple([_HBM] * (2 * n)),
        input_output_aliases={i: i for i in range(2 * n)},
        compiler_params=pltpu.CompilerParams(has_side_effects=pltpu.SideEffectType.DATAFLOW_SIDE_EFFECTING),
    )(*srcs, *lands, send_sems, recv_sems, after)
    return list(res[:n]), list(res[n:])


def _adamw_direct(g, land, me_idx, w, m, v, name):
    r, wd = w.shape
    tr = min(r, 256)

    def body(me_ref, *refs):
        g_ref, peers = refs[0], refs[1:N_DEV]
        w_ref, m_ref, v_ref, g_out, d_out, m_out, v_out = refs[N_DEV:]
        gs = g_ref[...].astype(f32)
        for p_ref in peers:
            gs = gs + p_ref[...].astype(f32)
        d, mn, vn = _adamw(w_ref[...], gs, m_ref[...], v_ref[...])
        g_out[...] = gs
        d_out[...] = d
        m_out[...] = mn
        v_out[...] = vn

    tile = pl.BlockSpec((tr, wd), lambda i, me_ref: (i, 0))
    slot = lambda k: pl.BlockSpec((None, tr, wd), lambda i, me_ref: ((me_ref[0] + k) % N_DEV, i, 0))
    return pl.pallas_call(
        body,
        grid_spec=pltpu.PrefetchScalarGridSpec(
            num_scalar_prefetch=1, grid=(r // tr,),
            in_specs=[slot(0)] + [slot(k) for k in range(1, N_DEV)] + [tile, tile, tile],
            out_specs=[tile] * 4),
        out_shape=[SDS((r, wd), f32)] * 4, name=name, compiler_params=_cp(("arbitrary",), 48),
    )(me_idx, g, *([land] * (N_DEV - 1)), w, m, v)


def _pair_sum(g, r1, c_idx, name):
    _, r, w = g.shape
    tr = min(r, 256)

    def body(c_ref, g_ref, r_ref, o_ref):
        o_ref[...] = (g_ref[...].astype(f32) + r_ref[...].astype(f32)).astype(bf16)

    return pl.pallas_call(
        body,
        grid_spec=pltpu.PrefetchScalarGridSpec(
            num_scalar_prefetch=1, grid=(4, r // tr),
            in_specs=[pl.BlockSpec((None, tr, w), lambda q, i, c_ref: (2 * q + c_ref[0], i, 0)),
                      pl.BlockSpec((None, tr, w), lambda q, i, c_ref: (q, i, 0))],
            out_specs=pl.BlockSpec((None, tr, w), lambda q, i, c_ref: (q, i, 0))),
        out_shape=SDS((4, r, w), bf16), name=name, compiler_params=_cp(("arbitrary", "arbitrary")),
    )(c_idx, g, r1)


def _adamw(w, g, m, v):
    m = ADAM_B1 * m + (1.0 - ADAM_B1) * g
    v = ADAM_B2 * v + (1.0 - ADAM_B2) * (g * g)
    m_hat = m / (1.0 - ADAM_B1 ** ADAM_STEP)
    v_hat = v / (1.0 - ADAM_B2 ** ADAM_STEP)
    delta = -ADAM_LR * (m_hat / (jnp.sqrt(v_hat) + ADAM_EPS) + ADAM_WD * w)
    return delta, m, v


def _adamw_big(pieces, q_idx, w, m, v, name, row_off=0):
    r, wd = w.shape
    tr = min(r, 256)
    np_ = len(pieces)
    per = r // tr // np_

    def body(q_ref, *refs):
        w_ref, m_ref, v_ref, g_out, d_out, m_out, v_out = refs[2 * np_:]
        i = pl.program_id(0)
        for k in range(np_):
            @pl.when((i >= k * per) & (i < (k + 1) * per))
            def _():
                p_ref, r_ref = refs[2 * k], refs[2 * k + 1]
                g = p_ref[...].astype(f32)
                for j in range(3):
                    g = g + r_ref[j].astype(f32)
                d, mn, vn = _adamw(w_ref[...], g, m_ref[...], v_ref[...])
                g_out[...] = g
                d_out[...] = d
                m_out[...] = mn
                v_out[...] = vn

    tile = pl.BlockSpec((tr, wd), lambda i, q_ref: (i, 0))
    in_specs, args = [], []
    for k, (part, r2) in enumerate(pieces):
        loc = lambda i, k=k: row_off + jnp.clip(i - k * per, 0, per - 1)
        in_specs += [pl.BlockSpec((None, tr, wd), lambda i, q_ref, loc=loc: (q_ref[0], loc(i), 0)),
                     pl.BlockSpec((3, tr, wd), lambda i, q_ref, loc=loc: (0, loc(i), 0))]
        args += [part, r2]
    return pl.pallas_call(
        body,
        grid_spec=pltpu.PrefetchScalarGridSpec(
            num_scalar_prefetch=1, grid=(r // tr,), in_specs=in_specs + [tile, tile, tile], out_specs=[tile] * 4),
        out_shape=[SDS((r, wd), f32)] * 4, name=name, compiler_params=_cp(("arbitrary",), 48),
    )(q_idx, *args, w, m, v)


_SMALL_ROWS = 24


def _pack_small(st_emb, vec_rnn, st_out, dsr, db_in, dmeta):
    def body(se_ref, vr_ref, so_ref, dsr_ref, db_ref, dm_ref, sm_ref, sm2_ref):
        sm_ref[...] = jnp.zeros_like(sm_ref)
        sm2_ref[...] = jnp.zeros_like(sm2_ref)
        sm_ref[0:2, :] = se_ref[0:2, :]
        sm_ref[2:3, :] = vr_ref[3:4, :]
        sm_ref[3:6, :] = vr_ref[0:3, :]
        sm_ref[6:7, :] = so_ref[2:3, :]
        sm_ref[7:9, :] = so_ref[0:2, :]
        for h in range(N_KV):
            sm_ref[9:10, h * GROUP:(h + 1) * GROUP] = _colsum(dsr_ref[h])
        for j in range(6):
            sm_ref[16 + j:17 + j, :] = db_ref[0:1, j * D:(j + 1) * D]
        sm_ref[22:23, 0:D_IN - 6 * D] = db_ref[0:1, 6 * D:D_IN]
        for s in range(N_DEV):
            sm2_ref[s, 0:N_META, :] = dm_ref[:, s * 256:(s + 1) * 256]
            sm2_ref[s, N_META:N_META + CONV_WIDTH, :] = vr_ref[4:8, s * 256:(s + 1) * 256]

    return pl.pallas_call(
        body, out_shape=[SDS((_SMALL_ROWS, D), f32), SDS((N_DEV, 24, 256), f32)],
        name="pack_small", compiler_params=_cp(None),
    )(st_emb, vec_rnn, st_out, dsr, db_in, dmeta)


def _small_allreduce(sm, sm2):
    def body(sm_ref, sm2_ref, o_ref, o2_ref, buf, buf2, send_sems, recv_sems):
        x, y, c = _place()
        me = _dev(x, y, c)
        copies = []
        for f in range(1, N_DEV):
            fx, fy, fc = f // 4, (f // 2) % 2, f % 2
            peer = ((x + fx) % 2, (y + fy) % 2, (c + fc) % 2)
            for t, (src, dst) in enumerate(((sm_ref, buf), (sm2_ref, buf2))):
                k = 2 * (f - 1) + t
                copies.append(pltpu.make_async_remote_copy(
                    src_ref=src, dst_ref=dst.at[me], send_sem=send_sems.at[k], recv_sem=recv_sems.at[k],
                    device_id=peer, device_id_type=MESH))
        for cp in copies:
            cp.start()
        buf[me] = sm_ref[...]
        buf2[me] = sm2_ref[...]
        for cp in copies:
            cp.wait()
        acc, acc2 = buf[0], buf2[0]
        for e in range(1, N_DEV):
            acc, acc2 = acc + buf[e], acc2 + buf2[e]
        o_ref[...] = acc
        o2_ref[...] = acc2

    return pl.pallas_call(
        body, in_specs=[_VMEM, _VMEM], out_specs=[_VMEM, _VMEM],
        out_shape=[SDS(sm.shape, f32), SDS(sm2.shape, f32)],
        scratch_shapes=[pltpu.VMEM((N_DEV, *sm.shape), f32), pltpu.VMEM((N_DEV, *sm2.shape), f32),
                        pltpu.SemaphoreType.DMA((14,)), pltpu.SemaphoreType.DMA((14,))],
        name="small_allreduce",
    )(sm, sm2)


_SMALL_ROW_OF = {"ln_emb_g": 0, "ln_emb_b": 1, "conv_b": 2, "b_ra": 3, "b_ri": 4, "lru_lambda": 5, "b_o": 6,
                 "ln_g": 7, "ln_b": 8}
_SMALL_NAMES = ["ln_emb_g", "ln_emb_b", "conv_b", "b_ra", "b_ri", "lru_lambda", "b_o", "ln_g", "ln_b",
                "sinks", "b_in", "meta_tokens", "conv_w"]


def _small_update(sm, sm2_mine, wmv):
    def grad_of(name, sm_ref, s2_ref):
        if name in _SMALL_ROW_OF:
            r = _SMALL_ROW_OF[name]
            return sm_ref[r:r + 1, :]
        if name == "sinks":
            return sm_ref[9:10, 0:N_KV * GROUP]
        if name == "b_in":
            return jnp.concatenate([sm_ref[16 + j:17 + j, :] for j in range(7)], axis=1)[:, :D_IN]
        if name == "meta_tokens":
            return s2_ref[0:N_META, :]
        return s2_ref[N_META:N_META + CONV_WIDTH, :]

    def body(*refs):
        sm_ref, s2_ref = refs[0], refs[1]
        ins = refs[2:2 + 3 * len(_SMALL_NAMES)]
        outs = refs[2 + 3 * len(_SMALL_NAMES):]
        for i, name in enumerate(_SMALL_NAMES):
            w_ref, m_ref, v_ref = ins[3 * i:3 * i + 3]
            g = grad_of(name, sm_ref, s2_ref)
            d, mn, vn = _adamw(w_ref[...], g, m_ref[...], v_ref[...])
            outs[4 * i][...] = g
            outs[4 * i + 1][...] = d
            outs[4 * i + 2][...] = mn
            outs[4 * i + 3][...] = vn

    args, out_shape = [sm, sm2_mine], []
    for name in _SMALL_NAMES:
        args += list(wmv[name])
        out_shape += [SDS(wmv[name][0].shape, f32)] * 4
    res = pl.pallas_call(body, out_shape=out_shape, name="small_update", compiler_params=_cp(None))(*args)
    return {name: tuple(res[4 * i:4 * i + 4]) for i, name in enumerate(_SMALL_NAMES)}


_WEIGHTS = ["meta_tokens", "ln_emb_g", "ln_emb_b", "w_in", "b_in", "conv_w", "conv_b", "w_ra", "b_ra", "w_ri",
            "b_ri", "lru_lambda", "sinks", "w_rnn_out", "w_attn_out", "w_o", "b_o", "ln_g", "ln_b"]
_SMALL_2D = {"meta_tokens": (N_META, 256), "conv_w": (CONV_WIDTH, 256), "b_in": (1, D_IN), "sinks": (1, N_KV * GROUP)}


def kernel(x, meta_tokens, ln_emb_g, ln_emb_b, w_in, b_in, conv_w, conv_b, w_ra, b_ra, w_ri, b_ri, lru_lambda, sinks, w_rnn_out, w_attn_out, w_o, b_o, ln_g, ln_b, loss_target, m_meta_tokens, m_ln_emb_g, m_ln_emb_b, m_w_in, m_b_in, m_conv_w, m_conv_b, m_w_ra, m_b_ra, m_w_ri, m_b_ri, m_lru_lambda, m_sinks, m_w_rnn_out, m_w_attn_out, m_w_o, m_b_o, m_ln_g, m_ln_b, v_meta_tokens, v_ln_emb_g, v_ln_emb_b, v_w_in, v_b_in, v_conv_w, v_conv_b, v_w_ra, v_b_ra, v_w_ri, v_b_ri, v_lru_lambda, v_sinks, v_w_rnn_out, v_w_attn_out, v_w_o, v_b_o, v_ln_g, v_ln_b):
    w = dict(meta_tokens=meta_tokens, ln_emb_g=ln_emb_g, ln_emb_b=ln_emb_b, w_in=w_in, b_in=b_in, conv_w=conv_w,
             conv_b=conv_b, w_ra=w_ra, b_ra=b_ra, w_ri=w_ri, b_ri=b_ri, lru_lambda=lru_lambda, sinks=sinks,
             w_rnn_out=w_rnn_out, w_attn_out=w_attn_out, w_o=w_o, b_o=b_o, ln_g=ln_g, ln_b=ln_b)
    m = dict(meta_tokens=m_meta_tokens, ln_emb_g=m_ln_emb_g, ln_emb_b=m_ln_emb_b, w_in=m_w_in, b_in=m_b_in,
             conv_w=m_conv_w, conv_b=m_conv_b, w_ra=m_w_ra, b_ra=m_b_ra, w_ri=m_w_ri, b_ri=m_b_ri,
             lru_lambda=m_lru_lambda, sinks=m_sinks, w_rnn_out=m_w_rnn_out, w_attn_out=m_w_attn_out, w_o=m_w_o,
             b_o=m_b_o, ln_g=m_ln_g, ln_b=m_ln_b)
    v = dict(meta_tokens=v_meta_tokens, ln_emb_g=v_ln_emb_g, ln_emb_b=v_ln_emb_b, w_in=v_w_in, b_in=v_b_in,
             conv_w=v_conv_w, conv_b=v_conv_b, w_ra=v_w_ra, b_ra=v_b_ra, w_ri=v_w_ri, b_ri=v_b_ri,
             lru_lambda=v_lru_lambda, sinks=v_sinks, w_rnn_out=v_w_rnn_out, w_attn_out=v_w_attn_out, w_o=v_w_o,
             b_o=v_b_o, ln_g=v_ln_g, ln_b=v_ln_b)
    px, py, pc = _place()
    as_idx = lambda t: jnp.reshape(t, (1,)).astype(jnp.int32)
    c_idx, q_idx, me_idx = as_idx(pc), as_idx(2 * px + py), as_idx(_dev(px, py, pc))

    w3_s, wrg_s, small_s = _cast_small(w_rnn_out, w_attn_out, w_o, w_ra, w_ri, meta_tokens, conv_w)
    wg, wrg, smallw, w3_land = _all_gather([_cast_w_in(w_in), wrg_s, small_s], [w3_s])
    w3_pending = _split_start(_copies_direct(True), 7, [w3_s], [w3_land], smallw, "gather_w3_start")
    w_full = _relayout_w_in(wg)

    vec = lambda name: w[name].reshape(1, -1)
    p = {k: vec(k) for k in ("ln_emb_g", "ln_emb_b", "b_in", "conv_b", "b_ra", "b_ri", "lru_lambda", "sinks",
                             "b_o", "ln_g", "ln_b")}
    p["b_in"] = p["b_in"] + w3_pending[4][0:1, 0:1]
    s = _step_branches(x, w_full, wrg, smallw, p)
    w3 = _split_wait(_copies_direct(True), *w3_pending[:4], s["lse"], "gather_w3_wait")[1][0]
    t = _step_merge(s, loss_target, w3, p)
    loss = lax.psum(jnp.sum(t["st_out"][3]), ("x", "y", "c"))

    big = {}
    two_d = lambda name: (w[name].shape[-2], w[name].shape[-1])
    proj = ("w_o", "w_rnn_out", "w_attn_out")
    g_proj = [t[k].reshape(N_DEV, 256, D) for k in ("g_wo", "g_wrnn", "g_wattn")]
    g_pending = _split_start(_copies_direct(False), 7, g_proj, [lax.empty((N_DEV, 256, D), bf16) for _ in proj],
                             p["b_o"], "reduce_proj_start")
    u = _step_backward(s, t, wrg, smallw, p, p["conv_b"] + g_pending[4][0:1, 0:1])

    def siblings_start(gs, dep, tag):
        return _split_start(_copies_siblings, 4, gs, [lax.empty((4, *g.shape[1:]), bf16) for g in gs], dep,
                            "reduce_siblings_start_" + tag)

    def chips_start(gs, r1, dep, tag):
        parts = [_pair_sum(g, r, c_idx, "pair_sum_%s%d" % (tag, i)) for i, (g, r) in enumerate(zip(gs, r1))]
        return _split_start(_copies_chips, 3, parts, [lax.empty((3, *q.shape[1:]), bf16) for q in parts], dep,
                            "reduce_chips_start_" + tag)

    g_a, db_in = _mm_dwin(s["h_t"], u["dz"], 0, p["b_o"])
    sib_a = siblings_start([g_a, u["g_wrg"].reshape(N_DEV, 2 * RNN_BLOCK, RNN_BLOCK)], db_in, "a")
    g_b, _ = _mm_dwin(s["h_t"], u["dz"], 1, sib_a[4])
    sib_b = siblings_start([g_b], db_in, "b")
    chp_a = chips_start(*_split_wait(_copies_siblings, *sib_a[:4], sib_b[4], "reduce_siblings_wait_a"), db_in, "a")
    g_proj, g_land = _split_wait(_copies_direct(False), *g_pending[:4], chp_a[4], "reduce_proj_wait")
    for i, name in enumerate(proj):
        res = _adamw_direct(g_proj[i], g_land[i], me_idx, w[name].reshape(two_d(name)), m[name].reshape(two_d(name)),
                            v[name].reshape(two_d(name)), "adamw_" + name)
        big[name] = tuple(r.reshape(w[name].shape) for r in res)
    chp_b = chips_start(*_split_wait(_copies_siblings, *sib_b[:4], big["w_attn_out"][3], "reduce_siblings_wait_b"),
                        db_in, "b")
    u.update(_step_input_grad(u["dz"], w_full, chp_b[4], t["du32"], x, smallw, p))
    u["db_in"] = db_in

    loc = {**t, **u}
    sm, sm2 = _pack_small(loc["st_emb"], loc["vec_rnn"], loc["st_out"], loc["dsr"], loc["db_in"], loc["dmeta"])
    sm, sm2 = _small_allreduce(sm, sm2)
    sm2_mine = lax.dynamic_index_in_dim(sm2, _dev(px, py, pc), 0, keepdims=False)
    two = lambda name, t: t.reshape(_SMALL_2D.get(name, (1, D)))
    small = _small_update(sm, sm2_mine, {k: (two(k, w[k]), two(k, m[k]), two(k, v[k])) for k in _SMALL_NAMES})

    parts_a, r2_a = _split_wait(_copies_chips, *chp_a[:4], small["b_in"][3], "reduce_chips_wait_a")
    parts_b, r2_b = _split_wait(_copies_chips, *chp_b[:4], small["b_in"][2], "reduce_chips_wait_b")
    res = _adamw_big([(parts_a[0], r2_a[0]), (parts_b[0], r2_b[0])], q_idx, w["w_in"].reshape(two_d("w_in")),
                     m["w_in"].reshape(two_d("w_in")), v["w_in"].reshape(two_d("w_in")), "adamw_w_in")
    big["w_in"] = tuple(r.reshape(w["w_in"].shape) for r in res)
    for i, name in enumerate(("w_ra", "w_ri")):
        sq = (RNN_BLOCK, RNN_BLOCK)
        res = _adamw_big([(parts_a[1], r2_a[1])], q_idx, w[name].reshape(sq), m[name].reshape(sq), v[name].reshape(sq),
                         "adamw_" + name, row_off=i)
        big[name] = tuple(r.reshape(w[name].shape) for r in res)
    res = dict(big)
    for k in _SMALL_NAMES:
        res[k] = tuple(t.reshape(w[k].shape) for t in small[k])

    outs = [loss, loc["grad_x"]]
    for j in range(4):
        outs += [res[k][j] for k in _WEIGHTS]
    return tuple(outs)
```

```python
import functools

import jax
import jax.numpy as jnp
from jax import lax
from jax.experimental import pallas as pl
from jax.experimental.pallas import tpu as pltpu

f32, bf16 = jnp.float32, jnp.bfloat16
SDS = jax.ShapeDtypeStruct

N_DEV = 8
D = 2048
N_META = 16
BLK = 128
ROW0 = BLK - N_META
N_RNN_BLOCKS = 8
RNN_BLOCK = D // N_RNN_BLOCKS
CONV_WIDTH = 4
LRU_C = 8.0
HEAD_DIM = 64
N_KV = 4
GROUP = 8
HALF = HEAD_DIM // 2
ROPE_THETA = 10000.0
NEG_INF = -1e30
LN_EPS = 1e-5
ALPHA = 2.0 ** 0.25
D_IN = 12800
SHARD_IN = D_IN // N_DEV
OFF_GR, OFF_Q, OFF_K, OFF_V, OFF_GA, OFF_G = 2048, 4096, 6144, 6400, 6656, 8704
ADAM_LR, ADAM_B1, ADAM_B2, ADAM_EPS, ADAM_WD, ADAM_STEP = 1e-3, 0.9, 0.999, 1e-8, 0.01, 10
VMEM_LIMIT_MB = 56
MESH = pl.DeviceIdType.MESH


def _cp(sem=None, vmem_mb=40):
    return pltpu.CompilerParams(dimension_semantics=sem, vmem_limit_bytes=vmem_mb * 2 ** 20)


def _row_chunk(m):
    best = 16
    for c in range(16, 641, 16):
        if m % c == 0:
            best = c
    return best


def _sigmoid(x):
    return 1.0 / (1.0 + jnp.exp(-x))


def _silu_and_grad(x):
    s = _sigmoid(x)
    return x * s, s * (1.0 + x * (1.0 - s))


def _log_sigmoid(x):
    return jnp.minimum(x, 0.0) - jnp.log1p(jnp.exp(-jnp.abs(x)))


def _ln_rows(v, g, b):
    mu = jnp.mean(v, axis=-1, keepdims=True)
    c = v - mu
    var = jnp.mean(c * c, axis=-1, keepdims=True)
    rstd = lax.rsqrt(var + LN_EPS)
    xhat = c * rstd
    return xhat * g + b, xhat, rstd


def _ln_rows_bwd(dy, g, xhat, rstd):
    dxh = dy * g
    m1 = jnp.mean(dxh, axis=-1, keepdims=True)
    m2 = jnp.mean(dxh * xhat, axis=-1, keepdims=True)
    return rstd * (dxh - m1 - xhat * m2)


def _colsum(v):
    return jnp.sum(v, axis=0, keepdims=True)


def _dot(a, b):
    return jnp.dot(a, b, preferred_element_type=f32)


def _dot_nt(a, b):
    return lax.dot_general(a, b, (((1,), (1,)), ((), ())), preferred_element_type=f32)


def _dot_tn(a, b):
    return lax.dot_general(a, b, (((0,), (0,)), ((), ())), preferred_element_type=f32)


def _meta_full(sw_ref):
    return jnp.concatenate([sw_ref[s, 0:N_META, :] for s in range(N_DEV)], axis=1)


def _ln_emb(x, smallw, g_e, b_e):
    seq = x.shape[1]
    rows = seq + BLK
    nb = rows // BLK

    def body(x_ref, sw_ref, g_ref, b_ref, h32_ref, hb_ref, ht_ref):
        i = pl.program_id(0)
        g, b = g_ref[...], b_ref[...]

        def emit(blk):
            h32_ref[...] = blk
            hb_ref[...] = blk.astype(bf16)
            ht_ref[...] = blk.T.astype(bf16)

        @pl.when(i == 0)
        def _():
            hm = _ln_rows(_meta_full(sw_ref), g, b)[0]
            emit(jnp.concatenate([jnp.zeros((ROW0, D), f32), hm], axis=0))

        @pl.when(i > 0)
        def _():
            emit(_ln_rows(x_ref[0], g, b)[0])

    return pl.pallas_call(
        body, grid=(nb,),
        in_specs=[pl.BlockSpec((1, BLK, D), lambda i: (0, jnp.maximum(i - 1, 0), 0)),
                  pl.BlockSpec((N_DEV, 24, 256), lambda i: (0, 0, 0)),
                  pl.BlockSpec((1, D), lambda i: (0, 0)),
                  pl.BlockSpec((1, D), lambda i: (0, 0))],
        out_specs=[pl.BlockSpec((BLK, D), lambda i: (i, 0)),
                   pl.BlockSpec((BLK, D), lambda i: (i, 0)),
                   pl.BlockSpec((D, BLK), lambda i: (0, i))],
        out_shape=[SDS((rows, D), f32), SDS((rows, D), bf16), SDS((D, rows), bf16)],
        name="ln_emb", compiler_params=_cp(("arbitrary",)),
    )(x, smallw, g_e, b_e)


def _ln_emb_bwd(dh, du32, x, smallw, g_e):
    seq = x.shape[1]
    rows = seq + BLK
    nb = rows // BLK

    def body(dh_ref, du_ref, x_ref, sw_ref, g_ref, gx_ref, dmeta_ref, st_ref):
        i = pl.program_id(0)
        g = g_ref[...]
        dht = dh_ref[...] + ALPHA * du_ref[...]

        @pl.when(i == 0)
        def _():
            v = jnp.concatenate([jnp.zeros((ROW0, D), f32), _meta_full(sw_ref)], axis=0)
            valid = lax.broadcasted_iota(jnp.int32, (BLK, 1), 0) >= ROW0
            d = jnp.where(valid, dht, 0.0)
            _, xhat, rstd = _ln_rows(v, g, 0.0)
            dv = _ln_rows_bwd(d, g, xhat, rstd)
            dmeta_ref[...] = dv[ROW0:, :]
            st_ref[...] = jnp.concatenate([_colsum(d * xhat), _colsum(d), jnp.zeros((6, D), f32)], axis=0)

        @pl.when(i > 0)
        def _():
            _, xhat, rstd = _ln_rows(x_ref[0], g, 0.0)
            gx_ref[0] = _ln_rows_bwd(dht, g, xhat, rstd)
            st_ref[0:1, :] += _colsum(dht * xhat)
            st_ref[1:2, :] += _colsum(dht)

    return pl.pallas_call(
        body, grid=(nb,),
        in_specs=[pl.BlockSpec((BLK, D), lambda i: (i, 0)),
                  pl.BlockSpec((BLK, D), lambda i: (i, 0)),
                  pl.BlockSpec((1, BLK, D), lambda i: (0, jnp.maximum(i - 1, 0), 0)),
                  pl.BlockSpec((N_DEV, 24, 256), lambda i: (0, 0, 0)),
                  pl.BlockSpec((1, D), lambda i: (0, 0))],
        out_specs=[pl.BlockSpec((1, BLK, D), lambda i: (0, jnp.maximum(i - 1, 0), 0)),
                   pl.BlockSpec((N_META, D), lambda i: (0, 0)),
                   pl.BlockSpec((8, D), lambda i: (0, 0))],
        out_shape=[SDS((1, seq, D), f32), SDS((N_META, D), f32), SDS((8, D), f32)],
        name="ln_emb_bwd", compiler_params=_cp(("arbitrary",)),
    )(dh, du32, x, smallw, g_e)


def _mm(a, b, *, name, nt=False, sel=None, bias=None, out_dtype=f32, tn=512):
    m, k = a.shape
    cm = _row_chunk(m)
    stacked = sel is not None
    n = D if stacked else (b.shape[0] if nt else b.shape[1])
    row_grid = n // tn <= 8
    am = cm if row_grid else m
    if stacked and nt:
        b_spec = pl.BlockSpec((tn // 256, None, 256, D), lambda j, i: (j, sel, 0, 0))
    elif stacked:
        b_spec = pl.BlockSpec((N_DEV, None, 256, tn), lambda j, i: (0, sel, 0, j))
    elif nt:
        b_spec = pl.BlockSpec((tn, k), lambda j, i: (j, 0))
    else:
        b_spec = pl.BlockSpec((k, tn), lambda j, i: (0, j))
    in_specs = [pl.BlockSpec((am, k), lambda j, i: (i, 0)), b_spec]
    args = [a, b]
    if bias is not None:
        in_specs.append(pl.BlockSpec((1, tn), lambda j, i: (0, j)))
        args.append(bias)

    def body(*refs):
        a_ref, b_ref, o_ref = refs[0], refs[1], refs[-1]
        bm = b_ref[...]
        if stacked:
            bm = bm.reshape((tn, D) if nt else (D, tn))
        for c in range(am // cm):
            acc = (_dot_nt if nt else _dot)(a_ref[c * cm:(c + 1) * cm, :], bm)
            if bias is not None:
                acc = acc + refs[2][...]
            o_ref[c * cm:(c + 1) * cm, :] = acc.astype(out_dtype)

    return pl.pallas_call(
        body, grid=(n // tn, m // am), in_specs=in_specs,
        out_specs=pl.BlockSpec((am, tn), lambda j, i: (i, j)),
        out_shape=SDS((m, n), out_dtype), name=name, compiler_params=_cp(("arbitrary", "arbitrary"), 48),
    )(*args)


def _mm_dh(dz, w_full, after):
    rows = dz.shape[0]
    tk, tn = 2560, 512
    cm = _row_chunk(rows)

    def body(a_ref, w_ref, after_ref, o_ref):
        kk = pl.program_id(1)
        for c in range(rows // cm):
            acc = _dot_nt(a_ref[c * cm:(c + 1) * cm, :], w_ref[...])

            @pl.when(kk == 0)
            def _():
                o_ref[c * cm:(c + 1) * cm, :] = acc

            @pl.when(kk > 0)
            def _():
                o_ref[c * cm:(c + 1) * cm, :] += acc

    return pl.pallas_call(
        body, grid=(D // tn, D_IN // tk),
        in_specs=[pl.BlockSpec((rows, tk), lambda j, kk: (0, kk)),
                  pl.BlockSpec((tn, tk), lambda j, kk: (j, kk)),
                  pl.BlockSpec(memory_space=pl.ANY)],
        out_specs=pl.BlockSpec((rows, tn), lambda j, kk: (0, j)),
        out_shape=SDS((rows, D), f32), name="mm_dh", compiler_params=_cp(("arbitrary", "arbitrary"), 48),
    )(dz, w_full, after)


W_IN_HALF = D // 2


def _mm_dwin(h_t, dz, half, after):
    rows = dz.shape[0]
    pair = 2 * SHARD_IN
    tm = 512
    mt = W_IN_HALF // tm

    def body(a_ref, dz_ref, after_ref, o_ref, db_ref):
        acc = _dot(a_ref[...], dz_ref[...])
        o_ref[0] = acc[:, :SHARD_IN].astype(bf16)
        o_ref[1] = acc[:, SHARD_IN:].astype(bf16)

        @pl.when(pl.program_id(1) == 0)
        def _():
            def step(i, s):
                blk = dz_ref[pl.ds(pl.multiple_of(i * BLK, BLK), BLK), :].astype(f32)
                return s + blk.reshape(BLK // 8, 8, pair).sum(axis=0)
            s = lax.fori_loop(0, rows // BLK, step, jnp.zeros((8, pair), f32))
            db_ref[...] = jnp.broadcast_to(_colsum(s), (8, pair))

    return pl.pallas_call(
        body, grid=(N_DEV // 2, mt),
        in_specs=[pl.BlockSpec((tm, rows), lambda p, i: (half * mt + i, 0)),
                  pl.BlockSpec((rows, pair), lambda p, i: (0, p)),
                  pl.BlockSpec(memory_space=pl.ANY)],
        out_specs=[pl.BlockSpec((2, tm, SHARD_IN), lambda p, i: (p, i, 0)),
                   pl.BlockSpec((8, pair), lambda p, i: (0, p))],
        out_shape=[SDS((N_DEV, W_IN_HALF, SHARD_IN), bf16), SDS((8, D_IN), f32)],
        name="mm_dwin_%d" % half, compiler_params=_cp(("arbitrary", "arbitrary"), VMEM_LIMIT_MB),
    )(h_t, dz, after)


def _relayout_w_in(wg):
    tm = 256

    def body(i_ref, o_ref):
        for d in range(N_DEV):
            o_ref[:, d * SHARD_IN:(d + 1) * SHARD_IN] = i_ref[d]

    return pl.pallas_call(
        body, grid=(D // tm,),
        in_specs=[pl.BlockSpec((N_DEV, tm, SHARD_IN), lambda i: (0, i, 0))],
        out_specs=pl.BlockSpec((tm, D_IN), lambda i: (i, 0)),
        out_shape=SDS((D, D_IN), bf16), name="relayout_w_in", compiler_params=_cp(("arbitrary",)),
    )(wg)


SCAN_ROWS = 32


def _scan8(a, b, reverse):
    idx = lax.broadcasted_iota(jnp.int32, a.shape, 0)
    for s in (1, 2, 4):
        sh = 8 - s if reverse else s
        a_sh, b_sh = pltpu.roll(a, sh, 0), pltpu.roll(b, sh, 0)
        m = (idx < 8 - s) if reverse else (idx >= s)
        b = jnp.where(m, a * b_sh + b, b)
        a = jnp.where(m, a * a_sh, a)
    return a, b


def _shift_rows(prev8, cur, k):
    ext = jnp.concatenate([prev8, cur], axis=0)
    return pltpu.roll(ext, k, 0)[8:, :]


def _gates(xc, w_ra, b_ra, w_ri, b_ri, ls):
    xb = xc.astype(bf16)
    r = _sigmoid(_dot(xb, w_ra) + b_ra)
    ig = _sigmoid(_dot(xb, w_ri) + b_ri)
    la = LRU_C * r * ls
    a = jnp.exp(la)
    mult = jnp.sqrt(jnp.tanh(-la) * (1.0 + a * a))
    return xb, r, ig, a, mult


_RNN_IN_SPECS = lambda rows: [
    pl.BlockSpec((1, 24, 256), lambda n: (n, 0, 0)),
    pl.BlockSpec((1, RNN_BLOCK), lambda n: (0, n)),
    pl.BlockSpec((N_DEV, 2, None, 32, RNN_BLOCK), lambda n: (0, 0, n, 0, 0)),
    pl.BlockSpec((1, RNN_BLOCK), lambda n: (0, n)),
    pl.BlockSpec((1, RNN_BLOCK), lambda n: (0, n)),
    pl.BlockSpec((1, RNN_BLOCK), lambda n: (0, n)),
]


def _rnn_fwd(z, smallw, conv_b, wrg, b_ra, b_ri, lam):
    rows = z.shape[0]
    nb = rows // BLK
    col = lambda off: pl.BlockSpec((rows, RNN_BLOCK), lambda n: (0, off // RNN_BLOCK + n))

    def body(xr_ref, gr_ref, sw_ref, cb_ref, w_ref, bra_ref, bri_ref, lam_ref, xc_ref, hr_ref, ya_ref, yat_ref, a_s):
        cw = sw_ref[0, N_META:24, :]
        cb = cb_ref[...]
        w_ra = w_ref[:, 0].reshape(RNN_BLOCK, RNN_BLOCK)
        w_ri = w_ref[:, 1].reshape(RNN_BLOCK, RNN_BLOCK)
        b_ra_v, b_ri_v = bra_ref[...], bri_ref[...]
        ls = _log_sigmoid(lam_ref[...])
        rid = lax.broadcasted_iota(jnp.int32, (BLK, 1), 0)

        def blk_step(i, carry):
            r0 = pl.multiple_of(i * BLK, BLK)
            grow = rid + r0
            valid = grow >= ROW0
            cur = jnp.where(valid, xr_ref[pl.ds(r0, BLK), :], 0.0)
            prev8 = xr_ref[pl.ds(pl.multiple_of(jnp.maximum(r0 - 8, 0), 8), 8), :] * (i > 0).astype(f32)
            xc = cb + cw[0:1] * cur
            for k in range(1, CONV_WIDTH):
                xc = xc + cw[k:k + 1] * _shift_rows(prev8, cur, k)
            xc_ref[pl.ds(r0, BLK), :] = xc
            _, _, ig, a, mult = _gates(xc, w_ra, b_ra_v, w_ri, b_ri_v, ls)
            mult = jnp.where(grow == ROW0, 1.0, mult)
            a_s[pl.ds(r0, BLK), :] = a
            hr_ref[pl.ds(r0, BLK), :] = jnp.where(valid, mult * ig * xc, 0.0)
            return carry

        lax.fori_loop(0, nb, blk_step, 0)

        def scan_step(j, carry):
            r0 = pl.multiple_of(j * SCAN_ROWS, SCAN_ROWS)
            tiles = [_scan8(a_s[pl.ds(r0 + 8 * k, 8), :], hr_ref[pl.ds(r0 + 8 * k, 8), :], False)
                     for k in range(SCAN_ROWS // 8)]
            for k, (a, b) in enumerate(tiles):
                h = b + a * carry
                hr_ref[pl.ds(r0 + 8 * k, 8), :] = h
                carry = jnp.broadcast_to(h[7:8, :], (8, RNN_BLOCK))
            return carry

        lax.fori_loop(0, rows // SCAN_ROWS, scan_step, jnp.zeros((8, RNN_BLOCK), f32))

        def gate_step(i, carry):
            r0 = pl.multiple_of(i * BLK, BLK)
            ya_ref[pl.ds(r0, BLK), :] = (hr_ref[pl.ds(r0, BLK), :]
                                         * _silu_and_grad(gr_ref[pl.ds(r0, BLK), :])[0]).astype(bf16)
            return carry

        lax.fori_loop(0, nb, gate_step, 0)
        yat_ref[...] = ya_ref[...].astype(f32).T.astype(bf16)

    return pl.pallas_call(
        body, grid=(N_RNN_BLOCKS,),
        in_specs=[col(0), col(OFF_GR)] + _RNN_IN_SPECS(rows),
        out_specs=[pl.BlockSpec((rows, RNN_BLOCK), lambda n: (0, n))] * 3
                  + [pl.BlockSpec((RNN_BLOCK, rows), lambda n: (n, 0))],
        out_shape=[SDS((rows, D), f32), SDS((rows, D), f32), SDS((rows, D), bf16), SDS((D, rows), bf16)],
        scratch_shapes=[pltpu.VMEM((rows, RNN_BLOCK), f32)],
        name="rnn_fwd", compiler_params=_cp(("arbitrary",)),
    )(z, z, smallw, conv_b, wrg, b_ra, b_ri, lam)


def _rnn_bwd(dya, hr, xc, z, smallw, conv_b, wrg, b_ra, b_ri, lam):
    rows = z.shape[0]
    nb = rows // BLK
    col = lambda off: pl.BlockSpec((rows, RNN_BLOCK), lambda n: (0, off // RNN_BLOCK + n))
    blk = pl.BlockSpec((rows, RNN_BLOCK), lambda n: (0, n))

    def body(dya_ref, hr_ref, xc_ref, xr_ref, gr_ref, sw_ref, cb_ref, w_ref, bra_ref, bri_ref, lam_ref,
             dxr_ref, dgr_ref, dw_ref, vec_ref, a_s, lam_s, dxc_s, r_s, ig_s, mult_s, dw_s):
        cw = sw_ref[0, N_META:24, :]
        w_ra = w_ref[:, 0].reshape(RNN_BLOCK, RNN_BLOCK)
        w_ri = w_ref[:, 1].reshape(RNN_BLOCK, RNN_BLOCK)
        b_ra_v, b_ri_v = bra_ref[...], bri_ref[...]
        lam_v = lam_ref[...]
        ls = _log_sigmoid(lam_v)
        rid = lax.broadcasted_iota(jnp.int32, (BLK, 1), 0)
        zrow = jnp.zeros((1, RNN_BLOCK), f32)

        def p1(i, carry):
            r0 = pl.multiple_of(i * BLK, BLK)
            sl = pl.ds(r0, BLK)
            _, r, ig, a, mult = _gates(xc_ref[sl, :], w_ra, b_ra_v, w_ri, b_ri_v, ls)
            a_s[sl, :] = a
            r_s[sl, :] = r
            ig_s[sl, :] = ig
            mult_s[sl, :] = mult
            sg, dsg = _silu_and_grad(gr_ref[sl, :])
            d = dya_ref[sl, :]
            lam_s[sl, :] = d * sg
            dgr_ref[sl, :] = (d * hr_ref[sl, :] * dsg).astype(bf16)
            return carry

        lax.fori_loop(0, nb, p1, 0)

        def p2(jj, carry):
            r0 = pl.multiple_of((rows // SCAN_ROWS - 1 - jj) * SCAN_ROWS, SCAN_ROWS)
            idx = lax.broadcasted_iota(jnp.int32, (8, RNN_BLOCK), 0)
            tiles = []
            for k in range(SCAN_ROWS // 8):
                sl = pl.ds(r0 + 8 * k, 8)
                a, g = a_s[sl, :], lam_s[sl, :]
                tiles.append((g, *_scan8(a, a * g, True)))
            for k in reversed(range(SCAN_ROWS // 8)):
                g, ca, cb_ = tiles[k]
                mu = cb_ + ca * carry
                lam_s[pl.ds(r0 + 8 * k, 8), :] = g + jnp.where(idx < 7, pltpu.roll(mu, 7, 0), carry)
                carry = jnp.broadcast_to(mu[0:1, :], (8, RNN_BLOCK))
            return carry

        lax.fori_loop(0, rows // SCAN_ROWS, p2, jnp.zeros((8, RNN_BLOCK), f32))

        dw_s[...] = jnp.zeros_like(dw_s)

        def p3(i, carry):
            d_bra, d_bri, d_ls = carry
            r0 = pl.multiple_of(i * BLK, BLK)
            sl = pl.ds(r0, BLK)
            grow = rid + r0
            valid = grow >= ROW0
            first = grow == ROW0
            xcv = xc_ref[sl, :]
            xb = xcv.astype(bf16)
            r, ig, a = r_s[sl, :], ig_s[sl, :], a_s[sl, :]
            mult = jnp.where(first, 1.0, mult_s[sl, :])
            lam_t = lam_s[sl, :]
            du = jnp.where(valid, lam_t, 0.0)
            hprev = _shift_rows(hr_ref[pl.ds(pl.multiple_of(jnp.maximum(r0 - 8, 0), 8), 8), :] * (i > 0).astype(f32), hr_ref[sl, :], 1)
            da = lam_t * hprev
            dmult = jnp.where(first, 0.0, du * ig * xcv)
            di = du * mult * xcv
            dxc = du * mult * ig
            ratio = jnp.where(valid & jnp.logical_not(first), a * a / mult, 0.0)
            dla = da * a - dmult * ratio
            dpr = (dla * (LRU_C * ls)) * r * (1.0 - r)
            dpi = di * ig * (1.0 - ig)
            dprb, dpib = dpr.astype(bf16), dpi.astype(bf16)
            dw_s[0] += _dot_tn(xb, dprb)
            dw_s[1] += _dot_tn(xb, dpib)
            dxc_s[sl, :] = dxc + _dot_nt(dprb, w_ra) + _dot_nt(dpib, w_ri)
            return d_bra + _colsum(dpr), d_bri + _colsum(dpi), d_ls + _colsum(dla * (LRU_C * r))

        d_bra, d_bri, d_ls = lax.fori_loop(0, nb, p3, (zrow, zrow, zrow))

        def p4(i, carry):
            d_cb, d_w0, d_w1, d_w2, d_w3 = carry
            r0 = pl.multiple_of(i * BLK, BLK)
            sl = pl.ds(r0, BLK)
            grow = rid + r0
            valid = grow >= ROW0
            dxc = dxc_s[sl, :]
            nxt = dxc_s[pl.ds(pl.multiple_of(jnp.minimum(r0 + BLK, rows - 8), 8), 8), :] * (i < nb - 1).astype(f32)
            ext = jnp.concatenate([dxc, nxt], axis=0)
            dxr = cw[0:1] * dxc
            for k in range(1, CONV_WIDTH):
                dxr = dxr + cw[k:k + 1] * pltpu.roll(ext, BLK + 8 - k, 0)[:BLK, :]
            dxr_ref[sl, :] = jnp.where(valid, dxr, 0.0).astype(bf16)
            cur = jnp.where(valid, xr_ref[sl, :], 0.0)
            prev8 = xr_ref[pl.ds(pl.multiple_of(jnp.maximum(r0 - 8, 0), 8), 8), :] * (i > 0).astype(f32)
            dws = [d_w0 + _colsum(dxc * cur)]
            for k, acc in ((1, d_w1), (2, d_w2), (3, d_w3)):
                dws.append(acc + _colsum(dxc * _shift_rows(prev8, cur, k)))
            return (d_cb + _colsum(dxc), *dws)

        d_cb, d_w0, d_w1, d_w2, d_w3 = lax.fori_loop(0, nb, p4, (zrow,) * 5)

        d_lam = d_ls * _sigmoid(-lam_v)
        vec_ref[...] = jnp.concatenate([d_bra, d_bri, d_lam, d_cb, d_w0, d_w1, d_w2, d_w3], axis=0)
        dw_ref[:, 0] = dw_s[0].astype(bf16).reshape(N_DEV, 32, RNN_BLOCK)
        dw_ref[:, 1] = dw_s[1].astype(bf16).reshape(N_DEV, 32, RNN_BLOCK)

    return pl.pallas_call(
        body, grid=(N_RNN_BLOCKS,),
        in_specs=[blk, blk, blk, col(0), col(OFF_GR)] + _RNN_IN_SPECS(rows),
        out_specs=[blk, blk,
                   pl.BlockSpec((N_DEV, 2, None, 32, RNN_BLOCK), lambda n: (0, 0, n, 0, 0)),
                   pl.BlockSpec((8, RNN_BLOCK), lambda n: (0, n))],
        out_shape=[SDS((rows, D), bf16), SDS((rows, D), bf16),
                   SDS((N_DEV, 2, N_RNN_BLOCKS, 32, RNN_BLOCK), bf16), SDS((8, D), f32)],
        scratch_shapes=[pltpu.VMEM((rows, RNN_BLOCK), f32)] * 6 + [pltpu.VMEM((2, RNN_BLOCK, RNN_BLOCK), f32)],
        name="rnn_bwd", compiler_params=_cp(("arbitrary",), 48),
    )(dya, hr, xc, z, z, smallw, conv_b, wrg, b_ra, b_ri, lam)


def _rope_tables(rows):
    half = jnp.arange(HALF, dtype=f32)
    inv = ROPE_THETA ** (-half / HALF)
    pos = (jnp.arange(rows) - ROW0).astype(f32)
    ang = pos[:, None] * inv[None, :]
    cos, sin = jnp.cos(ang), jnp.sin(ang)
    cos128 = jnp.concatenate([cos, cos, cos, cos], axis=1)
    sin128 = jnp.concatenate([-sin, sin, -sin, sin], axis=1)
    return cos128, sin128


def _rope128(x, cos128, sin128):
    lane = lax.broadcasted_iota(jnp.int32, x.shape, 1)
    swapped = jnp.where(lane % HEAD_DIM < HALF, pltpu.roll(x, 128 - HALF, 1), pltpu.roll(x, HALF, 1))
    return x * cos128 + swapped * sin128


def _qkv_prep(z, cos128, sin128):
    rows = z.shape[0]

    def body(q_ref, kv_ref, c_ref, s_ref, qo_ref, ko_ref, vo_ref):
        c, s = c_ref[...], s_ref[...]
        for g in range(D // 128):
            qo_ref[:, g * 128:(g + 1) * 128] = (_rope128(q_ref[:, g * 128:(g + 1) * 128], c, s)
                                                * (HEAD_DIM ** -0.5)).astype(bf16)
        for g in range(2):
            kr = _rope128(kv_ref[:, g * 128:(g + 1) * 128], c, s)
            for j in range(2):
                ko_ref[2 * g + j] = kr[:, j * HEAD_DIM:(j + 1) * HEAD_DIM].astype(bf16)
        for h in range(N_KV):
            vo_ref[h] = kv_ref[:, 256 + h * HEAD_DIM:256 + (h + 1) * HEAD_DIM].astype(bf16)

    return pl.pallas_call(
        body, grid=(rows // BLK,),
        in_specs=[pl.BlockSpec((BLK, D), lambda i: (i, OFF_Q // D)),
                  pl.BlockSpec((BLK, 512), lambda i: (i, OFF_K // 512)),
                  pl.BlockSpec((BLK, 128), lambda i: (i, 0)),
                  pl.BlockSpec((BLK, 128), lambda i: (i, 0))],
        out_specs=[pl.BlockSpec((BLK, D), lambda i: (i, 0)),
                   pl.BlockSpec((N_KV, BLK, HEAD_DIM), lambda i: (0, i, 0)),
                   pl.BlockSpec((N_KV, BLK, HEAD_DIM), lambda i: (0, i, 0))],
        out_shape=[SDS((rows, D), bf16), SDS((N_KV, rows, HEAD_DIM), bf16), SDS((N_KV, rows, HEAD_DIM), bf16)],
        name="qkv_prep", compiler_params=_cp(("arbitrary",)),
    )(z, z, cos128, sin128)


def _attn_mask(n):
    qi = n * BLK + lax.broadcasted_iota(jnp.int32, (BLK, 2 * BLK + N_META), 0)
    c = lax.broadcasted_iota(jnp.int32, (BLK, 2 * BLK + N_META), 1)
    jb = (n - 1) * BLK + c
    band = (jb >= BLK) & (jb <= qi) & (qi - jb < BLK)
    meta = (ROW0 + c - 2 * BLK) <= qi
    return ((c < 2 * BLK) & band) | ((c >= 2 * BLK) & meta)


N_KEYS = 2 * BLK + N_META


def _stack_heads(t):
    return jnp.concatenate([t[:, g * HEAD_DIM:(g + 1) * HEAD_DIM] for g in range(GROUP)], axis=0)


def _sink_column(sink_ref, h):
    g = lax.broadcasted_iota(jnp.int32, (GROUP, 1, 1), 0)
    col = jnp.zeros((GROUP, 1, 1), f32)
    for j in range(GROUP):
        col = jnp.where(g == j, sink_ref[h * GROUP + j], col)
    return col


def _kv_specs(last):
    cl = lambda n: jnp.minimum(n, last)
    return [pl.BlockSpec((None, N_META, HEAD_DIM), lambda h, n: (h, ROW0 // N_META, 0)),
            pl.BlockSpec((None, BLK, HEAD_DIM), lambda h, n: (h, jnp.maximum(cl(n) - 1, 0), 0)),
            pl.BlockSpec((None, BLK, HEAD_DIM), lambda h, n: (h, cl(n), 0))]


def _attn_fwd(q_r, k_r, v_b, z, sinks):
    rows = q_r.shape[0]
    nb = rows // BLK

    def body(sink_ref, q_ref, km_ref, kp_ref, kc_ref, vm_ref, vp_ref, vc_ref, ga_ref, o_ref, yb_ref, ybt_ref, lse_ref):
        h, n = pl.program_id(0), pl.program_id(1)
        kk = jnp.concatenate([kp_ref[...], kc_ref[...], km_ref[...]], axis=0)
        vv = jnp.concatenate([vp_ref[...], vc_ref[...], vm_ref[...]], axis=0)
        q2 = _stack_heads(q_ref[...])
        s = jnp.where(_attn_mask(n)[None], _dot_nt(q2, kk).reshape(GROUP, BLK, N_KEYS), NEG_INF)
        sink = _sink_column(sink_ref, h)
        m = jnp.maximum(jnp.max(s, axis=-1, keepdims=True), sink)
        p = jnp.exp(s - m)
        den = jnp.sum(p, axis=-1, keepdims=True) + jnp.exp(sink - m)
        o2 = _dot((p / den).astype(bf16).reshape(GROUP * BLK, N_KEYS), vv)
        lse = m + jnp.log(den)
        for g in range(GROUP):
            o_ref[:, g * HEAD_DIM:(g + 1) * HEAD_DIM] = o2[g * BLK:(g + 1) * BLK]
            lse_ref[:, g:g + 1] = lse[g]
        yb = o_ref[...] * _silu_and_grad(ga_ref[...])[0]
        yb_ref[...] = yb.astype(bf16)
        ybt_ref[...] = yb.T.astype(bf16)

    tile = pl.BlockSpec((BLK, 512), lambda h, n: (n, h))
    return pl.pallas_call(
        body, grid=(N_KV, nb),
        in_specs=[pl.BlockSpec(memory_space=pltpu.SMEM), tile] + _kv_specs(nb - 1) + _kv_specs(nb - 1)
                 + [pl.BlockSpec((BLK, 512), lambda h, n: (n, OFF_GA // 512 + h))],
        out_specs=[tile, tile, pl.BlockSpec((512, BLK), lambda h, n: (h, n)),
                   pl.BlockSpec((None, BLK, GROUP), lambda h, n: (h, n, 0))],
        out_shape=[SDS((rows, D), f32), SDS((rows, D), bf16), SDS((D, rows), bf16),
                   SDS((N_KV, rows, GROUP), f32)],
        name="attn_fwd", compiler_params=_cp(("arbitrary", "arbitrary")),
    )(sinks, q_r, k_r, k_r, k_r, v_b, v_b, v_b, z)


def _attn_bwd(dyb, o32, lse, q_r, k_r, v_b, z, sinks):
    rows = q_r.shape[0]
    nb = rows // BLK
    cl = lambda n: jnp.minimum(n, nb - 1)

    def body(sink_ref, dyb_ref, o_ref, lse_ref, q_ref, km_ref, kp_ref, kc_ref, vm_ref, vp_ref, vc_ref, ga_ref,
             dq_ref, dga_ref, dk_ref, dv_ref, dkm_ref, dvm_ref, dsr_ref, ck_s, cv_s):
        h, n = pl.program_id(0), pl.program_id(1)

        @pl.when(n == 0)
        def _():
            dkm_ref[...] = jnp.zeros_like(dkm_ref)
            dvm_ref[...] = jnp.zeros_like(dvm_ref)
            ck_s[...] = jnp.zeros_like(ck_s)
            cv_s[...] = jnp.zeros_like(cv_s)

        @pl.when(n < nb)
        def _():
            kk = jnp.concatenate([kp_ref[...], kc_ref[...], km_ref[...]], axis=0)
            vv = jnp.concatenate([vp_ref[...], vc_ref[...], vm_ref[...]], axis=0)
            sg, dsg = _silu_and_grad(ga_ref[...])
            dyb_v = dyb_ref[...]
            o_v = o_ref[...]
            dga_ref[...] = (dyb_v * o_v * dsg).astype(bf16)
            q2 = _stack_heads(q_ref[...])
            do2 = _stack_heads(dyb_v * sg)
            lse_v = lse_ref[...]
            lse = jnp.concatenate([lse_v[:, g:g + 1] for g in range(GROUP)], axis=0).reshape(GROUP, BLK, 1)
            delta = jnp.sum(do2 * _stack_heads(o_v), axis=-1, keepdims=True).reshape(GROUP, BLK, 1)
            s = jnp.where(_attn_mask(n)[None], _dot_nt(q2, kk).reshape(GROUP, BLK, N_KEYS), NEG_INF)
            p = jnp.exp(s - lse)
            do2b = do2.astype(bf16)
            ds = (p * (_dot_nt(do2b, vv).reshape(GROUP, BLK, N_KEYS) - delta)).astype(bf16)
            ds = ds.reshape(GROUP * BLK, N_KEYS)
            dsr = -jnp.exp(_sink_column(sink_ref, h) - lse) * delta
            dq2 = _dot(ds, kk)
            for g in range(GROUP):
                dq_ref[:, g * HEAD_DIM:(g + 1) * HEAD_DIM] = dq2[g * BLK:(g + 1) * BLK]
                dsr_ref[:, g:g + 1] = dsr[g]
            dkk = _dot_tn(ds, q2)
            dvv = _dot_tn(p.astype(bf16).reshape(GROUP * BLK, N_KEYS), do2b)
            dk_ref[...] = ck_s[...] + dkk[:BLK]
            dv_ref[...] = cv_s[...] + dvv[:BLK]
            ck_s[...] = dkk[BLK:2 * BLK]
            cv_s[...] = dvv[BLK:2 * BLK]
            dkm_ref[...] += dkk[2 * BLK:]
            dvm_ref[...] += dvv[2 * BLK:]

        @pl.when(n == nb)
        def _():
            dk_ref[...] = ck_s[...]
            dv_ref[...] = cv_s[...]

    tile = pl.BlockSpec((BLK, 512), lambda h, n: (cl(n), h))
    kvout = pl.BlockSpec((None, BLK, HEAD_DIM), lambda h, n: (h, jnp.maximum(n - 1, 0), 0))
    mout = pl.BlockSpec((None, N_META, HEAD_DIM), lambda h, n: (h, 0, 0))
    stat = pl.BlockSpec((None, BLK, GROUP), lambda h, n: (h, cl(n), 0))
    return pl.pallas_call(
        body, grid=(N_KV, nb + 1),
        in_specs=[pl.BlockSpec(memory_space=pltpu.SMEM), tile, tile, stat, tile] + _kv_specs(nb - 1)
                 + _kv_specs(nb - 1) + [pl.BlockSpec((BLK, 512), lambda h, n: (cl(n), OFF_GA // 512 + h))],
        out_specs=[tile, tile, kvout, kvout, mout, mout, stat],
        out_shape=[SDS((rows, D), f32), SDS((rows, D), bf16),
                   SDS((N_KV, rows, HEAD_DIM), f32), SDS((N_KV, rows, HEAD_DIM), f32),
                   SDS((N_KV, N_META, HEAD_DIM), f32), SDS((N_KV, N_META, HEAD_DIM), f32),
                   SDS((N_KV, rows, GROUP), f32)],
        scratch_shapes=[pltpu.VMEM((BLK, HEAD_DIM), f32), pltpu.VMEM((BLK, HEAD_DIM), f32)],
        name="attn_bwd", compiler_params=_cp(("arbitrary", "arbitrary")),
    )(sinks, dyb, o32, lse, q_r, k_r, k_r, k_r, v_b, v_b, v_b, z)


def _qkv_finish(dq, dk, dv, dkm, dvm, cos128, sin128):
    rows = dq.shape[0]

    def body(dq_ref, dk_ref, dv_ref, dkm_ref, dvm_ref, c_ref, s_ref, oq_ref, okv_ref):
        first = (pl.program_id(0) == 0).astype(f32)
        c, s = c_ref[...], -s_ref[...]
        for g in range(D // 128):
            oq_ref[:, g * 128:(g + 1) * 128] = (_rope128(dq_ref[:, g * 128:(g + 1) * 128], c, s)
                                                * (HEAD_DIM ** -0.5)).astype(bf16)
        pad = jnp.zeros((ROW0, HEAD_DIM), f32)
        ks = [dk_ref[h] + first * jnp.concatenate([pad, dkm_ref[h]], axis=0) for h in range(N_KV)]
        vs = [dv_ref[h] + first * jnp.concatenate([pad, dvm_ref[h]], axis=0) for h in range(N_KV)]
        for g in range(2):
            kp = jnp.concatenate([ks[2 * g], ks[2 * g + 1]], axis=1)
            okv_ref[:, g * 128:(g + 1) * 128] = _rope128(kp, c, s).astype(bf16)
            okv_ref[:, 256 + g * 128:256 + (g + 1) * 128] = jnp.concatenate([vs[2 * g], vs[2 * g + 1]], axis=1).astype(bf16)

    kv = pl.BlockSpec((N_KV, BLK, HEAD_DIM), lambda i: (0, i, 0))
    mt = pl.BlockSpec((N_KV, N_META, HEAD_DIM), lambda i: (0, 0, 0))
    return pl.pallas_call(
        body, grid=(rows // BLK,),
        in_specs=[pl.BlockSpec((BLK, D), lambda i: (i, 0)), kv, kv, mt, mt,
                  pl.BlockSpec((BLK, 128), lambda i: (i, 0)), pl.BlockSpec((BLK, 128), lambda i: (i, 0))],
        out_specs=[pl.BlockSpec((BLK, D), lambda i: (i, 0)), pl.BlockSpec((BLK, 512), lambda i: (i, 0))],
        out_shape=[SDS((rows, D), bf16), SDS((rows, 512), bf16)],
        name="qkv_finish", compiler_params=_cp(("arbitrary",)),
    )(dq, dk, dv, dkm, dvm, cos128, sin128)


_TW = 512


def _mix_specs(rows):
    tr = _row_chunk(rows)
    tile = pl.BlockSpec((tr, _TW), lambda i, j: (i, j))
    ga = pl.BlockSpec((tr, _TW), lambda i, j: (i, OFF_G // _TW + j))
    gb = pl.BlockSpec((tr, _TW), lambda i, j: (i, (OFF_G + D) // _TW + j))
    return (rows // tr, D // _TW), tile, ga, gb


def _mix_fwd(y_a, y_b, z):
    rows = y_a.shape[0]
    tw = 256
    col = lambda off: pl.BlockSpec((rows, tw), lambda j: (0, off // tw + j))

    def body(ya_ref, yb_ref, ga_ref, gb_ref, o_ref, ot_ref):
        mixed = _sigmoid(ga_ref[...]) * ya_ref[...] + _sigmoid(gb_ref[...]) * yb_ref[...]
        o_ref[...] = mixed.astype(bf16)
        ot_ref[...] = mixed.T.astype(bf16)

    return pl.pallas_call(
        body, grid=(D // tw,), in_specs=[col(0), col(0), col(OFF_G), col(OFF_G + D)],
        out_specs=[col(0), pl.BlockSpec((tw, rows), lambda j: (j, 0))],
        out_shape=[SDS((rows, D), bf16), SDS((D, rows), bf16)],
        name="mix_fwd", compiler_params=_cp(("arbitrary",)),
    )(y_a, y_b, z, z)


def _mix_bwd(dmixed, y_a, y_b, z):
    rows = y_a.shape[0]
    grid, _mix_tile, _mix_ga, _mix_gb = _mix_specs(rows)

    def body(dm_ref, ya_ref, yb_ref, ga_ref, gb_ref, dya_ref, dyb_ref, dga_ref, dgb_ref):
        dm = dm_ref[...]
        sa, sb = _sigmoid(ga_ref[...]), _sigmoid(gb_ref[...])
        dya_ref[...] = (dm * sa).astype(bf16)
        dyb_ref[...] = (dm * sb).astype(bf16)
        dga_ref[...] = (dm * ya_ref[...] * sa * (1.0 - sa)).astype(bf16)
        dgb_ref[...] = (dm * yb_ref[...] * sb * (1.0 - sb)).astype(bf16)

    return pl.pallas_call(
        body, grid=grid, in_specs=[_mix_tile, _mix_tile, _mix_tile, _mix_ga, _mix_gb],
        out_specs=[_mix_tile] * 4, out_shape=[SDS((rows, D), bf16)] * 4,
        name="mix_bwd", compiler_params=_cp(("arbitrary", "arbitrary")),
    )(dmixed, y_a, y_b, z, z)


def _final_ln(out32, h32, tgt, ln_g, ln_b):
    rows = out32.shape[0]

    def body(o_ref, h_ref, t_ref, g_ref, b_ref, du_ref, dub_ref, st_ref):
        i = pl.program_id(0)
        g = g_ref[...]
        y, xhat, rstd = _ln_rows(ALPHA * h_ref[...] + o_ref[...], g, b_ref[...])
        e = jnp.where(i > 0, y - t_ref[0], 0.0)
        dy = e * (1.0 / D)
        du = _ln_rows_bwd(dy, g, xhat, rstd)
        du_ref[...] = du
        dub_ref[...] = du.astype(bf16)
        st = jnp.concatenate([_colsum(dy * xhat), _colsum(dy), _colsum(du), _colsum(e * e) * (0.5 / D),
                              jnp.zeros((4, D), f32)], axis=0)

        @pl.when(i == 0)
        def _():
            st_ref[...] = st

        @pl.when(i > 0)
        def _():
            st_ref[...] += st

    row = pl.BlockSpec((BLK, D), lambda i: (i, 0))
    vec = pl.BlockSpec((1, D), lambda i: (0, 0))
    return pl.pallas_call(
        body, grid=(rows // BLK,),
        in_specs=[row, row, pl.BlockSpec((1, BLK, D), lambda i: (0, jnp.maximum(i - 1, 0), 0)), vec, vec],
        out_specs=[row, row, pl.BlockSpec((8, D), lambda i: (0, 0))],
        out_shape=[SDS((rows, D), f32), SDS((rows, D), bf16), SDS((8, D), f32)],
        name="final_ln", compiler_params=_cp(("arbitrary",)),
    )(out32, h32, tgt, ln_g, ln_b)


def _assemble_dz(dxr, dgr, dq, dkv, dga, dma, dmb):
    rows = dxr.shape[0]
    parts = [(dxr, D), (dgr, D), (dq, D), (dkv, 512), (dga, D), (dma, D), (dmb, D)]

    def body(*refs):
        o_ref = refs[-1]
        off = 0
        for r, (_, w) in zip(refs[:-1], parts):
            o_ref[:, off:off + w] = r[...]
            off += w

    return pl.pallas_call(
        body, grid=(rows // BLK,),
        in_specs=[pl.BlockSpec((BLK, w), lambda i: (i, 0)) for _, w in parts],
        out_specs=pl.BlockSpec((BLK, D_IN), lambda i: (i, 0)),
        out_shape=SDS((rows, D_IN), bf16), name="assemble_dz", compiler_params=_cp(("arbitrary",)),
    )(*[p for p, _ in parts])


def _step_branches(x, w_full, wrg, smallw, p):
    rows = x.shape[1] + BLK
    cos128, sin128 = _rope_tables(rows)
    sinks = p["sinks"].reshape(N_KV * GROUP)
    h32, hb, h_t = _ln_emb(x, smallw, p["ln_emb_g"], p["ln_emb_b"])
    z = _mm(hb, w_full, bias=p["b_in"], name="mm_z")
    xc, hr, ya, ya_t = _rnn_fwd(z, smallw, p["conv_b"], wrg, p["b_ra"], p["b_ri"], p["lru_lambda"])
    q_r, k_r, v_b = _qkv_prep(z, cos128, sin128)
    o32, yb, yb_t, lse = _attn_fwd(q_r, k_r, v_b, z, sinks)
    return dict(cos128=cos128, sin128=sin128, sinks=sinks, h32=h32, h_t=h_t, z=z, xc=xc, hr=hr, ya=ya, ya_t=ya_t,
                q_r=q_r, k_r=k_r, v_b=v_b, o32=o32, yb=yb, yb_t=yb_t, lse=lse)


def _step_merge(s, tgt, w3, p):
    ya, yb, z = s["ya"], s["yb"], s["z"]
    y_a = _mm(ya, w3, sel=0, name="mm_ya")
    y_b = _mm(yb, w3, sel=1, name="mm_yb")
    mixed, mixed_t = _mix_fwd(y_a, y_b, z)
    out32 = _mm(mixed, w3, sel=2, bias=p["b_o"], name="mm_out")
    du32, dub, st_out = _final_ln(out32, s["h32"], tgt, p["ln_g"], p["ln_b"])

    g_wo = _mm(mixed_t, dub, out_dtype=bf16, name="mm_dwo")
    dmixed = _mm(dub, w3, sel=2, nt=True, name="mm_dmixed")
    dya_b, dyb_b, dma, dmb = _mix_bwd(dmixed, y_a, y_b, z)
    g_wrnn = _mm(s["ya_t"], dya_b, out_dtype=bf16, name="mm_dwrnn")
    g_wattn = _mm(s["yb_t"], dyb_b, out_dtype=bf16, name="mm_dwattn")
    dya = _mm(dya_b, w3, sel=0, nt=True, name="mm_dya")
    dyb = _mm(dyb_b, w3, sel=1, nt=True, name="mm_dyb")
    return dict(du32=du32, st_out=st_out, dma=dma, dmb=dmb, dya=dya, dyb=dyb, g_wo=g_wo, g_wrnn=g_wrnn,
                g_wattn=g_wattn)


def _step_backward(s, t, wrg, smallw, p, conv_b):
    z = s["z"]
    dxr, dgr, g_wrg, vec_rnn = _rnn_bwd(t["dya"], s["hr"], s["xc"], z, smallw, conv_b, wrg, p["b_ra"], p["b_ri"],
                                        p["lru_lambda"])
    dq_r, dga, dk, dv, dkm, dvm, dsr = _attn_bwd(t["dyb"], s["o32"], s["lse"], s["q_r"], s["k_r"], s["v_b"], z,
                                                 s["sinks"])
    dq, dkv = _qkv_finish(dq_r, dk, dv, dkm, dvm, s["cos128"], s["sin128"])
    dz = _assemble_dz(dxr, dgr, dq, dkv, dga, t["dma"], t["dmb"])
    return dict(vec_rnn=vec_rnn, dsr=dsr, g_wrg=g_wrg, dz=dz)


def _step_input_grad(dz, w_full, after, du32, x, smallw, p):
    dh = _mm_dh(dz, w_full, after)
    grad_x, dmeta, st_emb = _ln_emb_bwd(dh, du32, x, smallw, p["ln_emb_g"])
    return dict(grad_x=grad_x, dmeta=dmeta, st_emb=st_emb)


_ANY = pl.BlockSpec(memory_space=pl.ANY)
_VMEM = pl.BlockSpec(memory_space=pltpu.VMEM)


def _place():
    x, y, c = lax.axis_index("x"), lax.axis_index("y"), lax.axis_index("c")
    return x, y, c


def _dev(px, py, pc):
    return 4 * px + 2 * py + pc


def _cast_w_in(w_in):
    tm = 256

    def body(i_ref, o_ref):
        o_ref[...] = i_ref[0].astype(bf16)

    return pl.pallas_call(
        body, grid=(D // tm,),
        in_specs=[pl.BlockSpec((1, tm, SHARD_IN), lambda i: (0, i, 0))],
        out_specs=pl.BlockSpec((tm, SHARD_IN), lambda i: (i, 0)),
        out_shape=SDS((D, SHARD_IN), bf16), name="cast_w_in", compiler_params=_cp(("arbitrary",)),
    )(w_in)


def _cast_small(w_rnn_out, w_attn_out, w_o, w_ra, w_ri, meta, conv_w):
    def body(a_ref, b_ref, c_ref, ra_ref, ri_ref, m_ref, cw_ref, w3_ref, wrg_ref, sw_ref):
        w3_ref[0] = a_ref[0].astype(bf16)
        w3_ref[1] = b_ref[0].astype(bf16)
        w3_ref[2] = c_ref[0].astype(bf16)
        wrg_ref[0] = ra_ref[0].astype(bf16)
        wrg_ref[1] = ri_ref[0].astype(bf16)
        sw_ref[...] = jnp.concatenate([m_ref[...], cw_ref[0], jnp.zeros((4, 256), f32)], axis=0)

    return pl.pallas_call(
        body,
        out_shape=[SDS((3, 256, D), bf16), SDS((2, N_RNN_BLOCKS, 32, RNN_BLOCK), bf16), SDS((24, 256), f32)],
        name="cast_small", compiler_params=_cp(None),
    )(w_rnn_out, w_attn_out, w_o, w_ra, w_ri, meta, conv_w)


def _all_gather(big, shards, later):
    arrays = (big, *shards, *later)
    n, nl, na = len(shards), len(later), len(arrays)
    half = big.shape[0] // 2

    def body(*refs):
        ins, outs = refs[:na], refs[na:2 * na]
        send_sems, recv_sems, local_sems = refs[2 * na:]
        x, y, c = _place()
        me, sibling = (x, y, c), (x, y, 1 - c)
        xn, yn, dg = (1 - x, y), (x, 1 - y), (1 - x, 1 - y)
        chips = [xn, yn, dg]

        def remote(src, dst, k, to):
            return pltpu.make_async_remote_copy(src_ref=src, dst_ref=dst, send_sem=send_sems.at[k],
                                                recv_sem=recv_sems.at[k], device_id=to, device_id_type=MESH)

        mine = [pltpu.make_async_copy(ins[a], outs[a].at[_dev(*me)], local_sems.at[a]) for a in range(na)]
        for cp in mine:
            cp.start()

        def bslot(h, block):
            return outs[0].at[_dev(*block), pl.ds(h * half, half)]

        def bcopy(h, k, block, to, own=False):
            dst = bslot(h, block)
            return remote(ins[0].at[pl.ds(h * half, half)] if own else dst, dst, 7 * h + k, to)

        started = [bcopy(1, 1, me, (*xn, c), True), bcopy(0, 2, me, (*yn, c), True),
                   bcopy(0, 1, me, (*xn, c), True), bcopy(1, 2, me, (*yn, c), True),
                   bcopy(0, 0, me, sibling, True), bcopy(1, 0, me, sibling, True)]
        def scopy(a, k, block, to, own=False):
            dst = outs[a].at[_dev(*block)]
            return remote(ins[a] if own else dst, dst, 7 + 7 * a + k, to)

        for a in range(1, 1 + n):
            started.append(scopy(a, 0, me, sibling, True))
            started += [scopy(a, 1 + j, me, (*chip, c), True) for j, chip in enumerate(chips)]
        for cp in started:
            cp.start()

        def then(waited, *nexts):
            waited.wait_recv()
            for cp in nexts:
                cp.start()
                started.append(cp)

        then(bcopy(0, 2, (*yn, c), me), bcopy(0, 3, (*yn, c), (*xn, c)), bcopy(0, 5, (*yn, c), sibling))
        then(bcopy(1, 1, (*xn, c), me), bcopy(1, 3, (*xn, c), (*yn, c)), bcopy(1, 4, (*xn, c), sibling))
        then(bcopy(0, 1, (*xn, c), me), bcopy(0, 4, (*xn, c), sibling))
        then(bcopy(1, 2, (*yn, c), me), bcopy(1, 5, (*yn, c), sibling))
        for a in range(1, 1 + n):
            for j, chip in enumerate(chips):
                then(scopy(a, 1 + j, (*chip, c), me), scopy(a, 4 + j, (*chip, c), sibling))
        then(bcopy(0, 3, (*dg, c), me), bcopy(0, 6, (*dg, c), sibling))
        then(bcopy(1, 3, (*dg, c), me), bcopy(1, 6, (*dg, c), sibling))
        for h in range(2):
            bcopy(h, 0, sibling, me).wait_recv()
            for j, chip in enumerate(chips):
                bcopy(h, 4 + j, (*chip, 1 - c), me).wait_recv()
        for a in range(1, 1 + n):
            scopy(a, 0, sibling, me).wait_recv()
            for j, chip in enumerate(chips):
                scopy(a, 4 + j, (*chip, 1 - c), me).wait_recv()
        for cp in started:
            cp.wait_send()
        for cp in mine:
            cp.wait()

    nsem = 14 + 7 * n
    return pl.pallas_call(
        body, in_specs=[_ANY] * na, out_specs=[_ANY] * na,
        out_shape=[SDS((N_DEV, *s.shape), s.dtype) for s in arrays],
        scratch_shapes=[pltpu.SemaphoreType.DMA((nsem,)), pltpu.SemaphoreType.DMA((nsem,)),
                        pltpu.SemaphoreType.DMA((na,))],
        name="all_gather_weights",
    )(*arrays)


_HBM = pl.BlockSpec(memory_space=pltpu.HBM)
_SEM = pl.BlockSpec(memory_space=pltpu.SEMAPHORE)
_PEER_FLIPS = [(f // 4, (f // 2) % 2, f % 2) for f in range(1, N_DEV)]


def _remote(src, dst, send_sems, recv_sems, k, to):
    return pltpu.make_async_remote_copy(src_ref=src, dst_ref=dst, send_sem=send_sems.at[k], recv_sem=recv_sems.at[k],
                                        device_id=to, device_id_type=MESH)


def _copies_direct(same_src):
    def make(srcs, lands, send_sems, recv_sems):
        x, y, c = _place()
        me = _dev(x, y, c)
        out = []
        for a in range(len(srcs)):
            for k, (fx, fy, fc) in enumerate(_PEER_FLIPS):
                peer = ((x + fx) % 2, (y + fy) % 2, (c + fc) % 2)
                src = srcs[a] if same_src else srcs[a].at[_dev(*peer)]
                out.append(_remote(src, lands[a].at[me], send_sems, recv_sems, 7 * a + k, peer))
        return out
    return make


def _copies_siblings(srcs, lands, send_sems, recv_sems):
    x, y, c = _place()
    return [_remote(srcs[a].at[2 * q + (1 - c)], lands[a].at[q], send_sems, recv_sems, 4 * a + q, (x, y, 1 - c))
            for a in range(len(srcs)) for q in range(4)]


def _copies_chips(srcs, lands, send_sems, recv_sems):
    x, y, c = _place()
    chips = [(1 - x, y), (x, 1 - y), (1 - x, 1 - y)]
    return [_remote(srcs[a].at[2 * qx + qy], lands[a].at[j], send_sems, recv_sems, 3 * a + j, (qx, qy, c))
            for a in range(len(srcs)) for j, (qx, qy) in enumerate(chips)]


def _split_start(make, per_array, srcs, lands, dep, name):
    n = len(srcs)

    def body(*refs):
        send_sems, recv_sems, token = refs[2 * n + 1], refs[2 * n + 2], refs[-1]
        for cp in make(refs[:n], refs[n:2 * n], send_sems, recv_sems):
            cp.start()
        token[...] = jnp.zeros_like(token)

    hbm = lambda t: pltpu.with_memory_space_constraint(t, pltpu.HBM)
    res = pl.pallas_call(
        body, name=name,
        out_shape=(pltpu.SemaphoreType.DMA((per_array * n,)), pltpu.SemaphoreType.DMA((per_array * n,)),
                   *[pltpu.HBM(t.shape, t.dtype) for t in (*srcs, *lands)], SDS((8, 128), f32)),
        in_specs=[_HBM] * (2 * n) + [_ANY], out_specs=(_SEM, _SEM, *([_HBM] * (2 * n)), _VMEM),
        input_output_aliases={i: 2 + i for i in range(2 * n)},
        compiler_params=pltpu.CompilerParams(has_side_effects=pltpu.SideEffectType.DATAFLOW_SIDE_EFFECTING),
    )(*[hbm(t) for t in (*srcs, *lands)], dep)
    return res[0], res[1], list(res[2:2 + n]), list(res[2 + n:2 + 2 * n]), res[-1]


def _split_wait(make, send_sems, recv_sems, srcs, lands, after, name):
    n = len(srcs)

    def body(*refs):
        for cp in make(refs[:n], refs[n:2 * n], refs[2 * n], refs[2 * n + 1]):
            cp.wait_send()
            cp.wait_recv()

    res = pl.pallas_call(
        body, name=name,
        out_shape=tuple(pltpu.HBM(t.shape, t.dtype) for t in (*srcs, *lands)),
        in_specs=[_HBM] * (2 * n) + [_SEM, _SEM, _ANY], out_specs=tuple([_HBM] * (2 * n)),
        input_output_aliases={i: i for i in range(2 * n)},
        compiler_params=pltpu.CompilerParams(has_side_effects=pltpu.SideEffectType.DATAFLOW_SIDE_EFFECTING),
    )(*srcs, *lands, send_sems, recv_sems, after)
    return list(res[:n]), list(res[n:])


def _adamw_direct(g, land, me_idx, w, m, v, name):
    r, wd = w.shape
    tr = min(r, 256)

    def body(me_ref, *refs):
        g_ref, peers = refs[0], refs[1:N_DEV]
        w_ref, m_ref, v_ref, g_out, d_out, m_out, v_out = refs[N_DEV:]
        gs = g_ref[...].astype(f32)
        for p_ref in peers:
            gs = gs + p_ref[...].astype(f32)
        d, mn, vn = _adamw(w_ref[...], gs, m_ref[...], v_ref[...])
        g_out[...] = gs
        d_out[...] = d
        m_out[...] = mn
        v_out[...] = vn

    tile = pl.BlockSpec((tr, wd), lambda i, me_ref: (i, 0))
    slot = lambda k: pl.BlockSpec((None, tr, wd), lambda i, me_ref: ((me_ref[0] + k) % N_DEV, i, 0))
    return pl.pallas_call(
        body,
        grid_spec=pltpu.PrefetchScalarGridSpec(
            num_scalar_prefetch=1, grid=(r // tr,),
            in_specs=[slot(0)] + [slot(k) for k in range(1, N_DEV)] + [tile, tile, tile],
            out_specs=[tile] * 4),
        out_shape=[SDS((r, wd), f32)] * 4, name=name, compiler_params=_cp(("arbitrary",), 48),
    )(me_idx, g, *([land] * (N_DEV - 1)), w, m, v)


def _pair_sum(g, r1, c_idx, name):
    _, r, w = g.shape
    tr = min(r, 256)

    def body(c_ref, g_ref, r_ref, o_ref):
        o_ref[...] = (g_ref[...].astype(f32) + r_ref[...].astype(f32)).astype(bf16)

    return pl.pallas_call(
        body,
        grid_spec=pltpu.PrefetchScalarGridSpec(
            num_scalar_prefetch=1, grid=(4, r // tr),
            in_specs=[pl.BlockSpec((None, tr, w), lambda q, i, c_ref: (2 * q + c_ref[0], i, 0)),
                      pl.BlockSpec((None, tr, w), lambda q, i, c_ref: (q, i, 0))],
            out_specs=pl.BlockSpec((None, tr, w), lambda q, i, c_ref: (q, i, 0))),
        out_shape=SDS((4, r, w), bf16), name=name, compiler_params=_cp(("arbitrary", "arbitrary")),
    )(c_idx, g, r1)


def _adamw(w, g, m, v):
    m = ADAM_B1 * m + (1.0 - ADAM_B1) * g
    v = ADAM_B2 * v + (1.0 - ADAM_B2) * (g * g)
    m_hat = m / (1.0 - ADAM_B1 ** ADAM_STEP)
    v_hat = v / (1.0 - ADAM_B2 ** ADAM_STEP)
    delta = -ADAM_LR * (m_hat / (jnp.sqrt(v_hat) + ADAM_EPS) + ADAM_WD * w)
    return delta, m, v


def _adamw_big(pieces, q_idx, w, m, v, name, row_off=0):
    r, wd = w.shape
    tr = min(r, 256)
    np_ = len(pieces)
    per = r // tr // np_

    def body(q_ref, *refs):
        w_ref, m_ref, v_ref, g_out, d_out, m_out, v_out = refs[2 * np_:]
        i = pl.program_id(0)
        for k in range(np_):
            @pl.when((i >= k * per) & (i < (k + 1) * per))
            def _():
                p_ref, r_ref = refs[2 * k], refs[2 * k + 1]
                g = p_ref[...].astype(f32)
                for j in range(3):
                    g = g + r_ref[j].astype(f32)
                d, mn, vn = _adamw(w_ref[...], g, m_ref[...], v_ref[...])
                g_out[...] = g
                d_out[...] = d
                m_out[...] = mn
                v_out[...] = vn

    tile = pl.BlockSpec((tr, wd), lambda i, q_ref: (i, 0))
    in_specs, args = [], []
    for k, (part, r2) in enumerate(pieces):
        loc = lambda i, k=k: row_off + jnp.clip(i - k * per, 0, per - 1)
        in_specs += [pl.BlockSpec((None, tr, wd), lambda i, q_ref, loc=loc: (q_ref[0], loc(i), 0)),
                     pl.BlockSpec((3, tr, wd), lambda i, q_ref, loc=loc: (0, loc(i), 0))]
        args += [part, r2]
    return pl.pallas_call(
        body,
        grid_spec=pltpu.PrefetchScalarGridSpec(
            num_scalar_prefetch=1, grid=(r // tr,), in_specs=in_specs + [tile, tile, tile], out_specs=[tile] * 4),
        out_shape=[SDS((r, wd), f32)] * 4, name=name, compiler_params=_cp(("arbitrary",), 48),
    )(q_idx, *args, w, m, v)


_SMALL_ROWS = 24


def _pack_small(st_emb, vec_rnn, st_out, dsr, db_in, dmeta):
    def body(se_ref, vr_ref, so_ref, dsr_ref, db_ref, dm_ref, sm_ref, sm2_ref):
        sm_ref[...] = jnp.zeros_like(sm_ref)
        sm2_ref[...] = jnp.zeros_like(sm2_ref)
        sm_ref[0:2, :] = se_ref[0:2, :]
        sm_ref[2:3, :] = vr_ref[3:4, :]
        sm_ref[3:6, :] = vr_ref[0:3, :]
        sm_ref[6:7, :] = so_ref[2:3, :]
        sm_ref[7:9, :] = so_ref[0:2, :]
        for h in range(N_KV):
            sm_ref[9:10, h * GROUP:(h + 1) * GROUP] = _colsum(dsr_ref[h])
        for j in range(6):
            sm_ref[16 + j:17 + j, :] = db_ref[0:1, j * D:(j + 1) * D]
        sm_ref[22:23, 0:D_IN - 6 * D] = db_ref[0:1, 6 * D:D_IN]
        for s in range(N_DEV):
            sm2_ref[s, 0:N_META, :] = dm_ref[:, s * 256:(s + 1) * 256]
            sm2_ref[s, N_META:N_META + CONV_WIDTH, :] = vr_ref[4:8, s * 256:(s + 1) * 256]

    return pl.pallas_call(
        body, out_shape=[SDS((_SMALL_ROWS, D), f32), SDS((N_DEV, 24, 256), f32)],
        name="pack_small", compiler_params=_cp(None),
    )(st_emb, vec_rnn, st_out, dsr, db_in, dmeta)


def _small_allreduce(sm, sm2):
    def body(sm_ref, sm2_ref, o_ref, o2_ref, buf, buf2, send_sems, recv_sems):
        x, y, c = _place()
        me = _dev(x, y, c)
        copies = []
        for f in range(1, N_DEV):
            fx, fy, fc = f // 4, (f // 2) % 2, f % 2
            peer = ((x + fx) % 2, (y + fy) % 2, (c + fc) % 2)
            for t, (src, dst) in enumerate(((sm_ref, buf), (sm2_ref, buf2))):
                k = 2 * (f - 1) + t
                copies.append(pltpu.make_async_remote_copy(
                    src_ref=src, dst_ref=dst.at[me], send_sem=send_sems.at[k], recv_sem=recv_sems.at[k],
                    device_id=peer, device_id_type=MESH))
        for cp in copies:
            cp.start()
        buf[me] = sm_ref[...]
        buf2[me] = sm2_ref[...]
        for cp in copies:
            cp.wait()
        acc, acc2 = buf[0], buf2[0]
        for e in range(1, N_DEV):
            acc, acc2 = acc + buf[e], acc2 + buf2[e]
        o_ref[...] = acc
        o2_ref[...] = acc2

    return pl.pallas_call(
        body, in_specs=[_VMEM, _VMEM], out_specs=[_VMEM, _VMEM],
        out_shape=[SDS(sm.shape, f32), SDS(sm2.shape, f32)],
        scratch_shapes=[pltpu.VMEM((N_DEV, *sm.shape), f32), pltpu.VMEM((N_DEV, *sm2.shape), f32),
                        pltpu.SemaphoreType.DMA((14,)), pltpu.SemaphoreType.DMA((14,))],
        name="small_allreduce",
    )(sm, sm2)


_SMALL_ROW_OF = {"ln_emb_g": 0, "ln_emb_b": 1, "conv_b": 2, "b_ra": 3, "b_ri": 4, "lru_lambda": 5, "b_o": 6,
                 "ln_g": 7, "ln_b": 8}
_SMALL_NAMES = ["ln_emb_g", "ln_emb_b", "conv_b", "b_ra", "b_ri", "lru_lambda", "b_o", "ln_g", "ln_b",
                "sinks", "b_in", "meta_tokens", "conv_w"]


def _small_update(sm, sm2_mine, wmv):
    def grad_of(name, sm_ref, s2_ref):
        if name in _SMALL_ROW_OF:
            r = _SMALL_ROW_OF[name]
            return sm_ref[r:r + 1, :]
        if name == "sinks":
            return sm_ref[9:10, 0:N_KV * GROUP]
        if name == "b_in":
            return jnp.concatenate([sm_ref[16 + j:17 + j, :] for j in range(7)], axis=1)[:, :D_IN]
        if name == "meta_tokens":
            return s2_ref[0:N_META, :]
        return s2_ref[N_META:N_META + CONV_WIDTH, :]

    def body(*refs):
        sm_ref, s2_ref = refs[0], refs[1]
        ins = refs[2:2 + 3 * len(_SMALL_NAMES)]
        outs = refs[2 + 3 * len(_SMALL_NAMES):]
        for i, name in enumerate(_SMALL_NAMES):
            w_ref, m_ref, v_ref = ins[3 * i:3 * i + 3]
            g = grad_of(name, sm_ref, s2_ref)
            d, mn, vn = _adamw(w_ref[...], g, m_ref[...], v_ref[...])
            outs[4 * i][...] = g
            outs[4 * i + 1][...] = d
            outs[4 * i + 2][...] = mn
            outs[4 * i + 3][...] = vn

    args, out_shape = [sm, sm2_mine], []
    for name in _SMALL_NAMES:
        args += list(wmv[name])
        out_shape += [SDS(wmv[name][0].shape, f32)] * 4
    res = pl.pallas_call(body, out_shape=out_shape, name="small_update", compiler_params=_cp(None))(*args)
    return {name: tuple(res[4 * i:4 * i + 4]) for i, name in enumerate(_SMALL_NAMES)}


_WEIGHTS = ["meta_tokens", "ln_emb_g", "ln_emb_b", "w_in", "b_in", "conv_w", "conv_b", "w_ra", "b_ra", "w_ri",
            "b_ri", "lru_lambda", "sinks", "w_rnn_out", "w_attn_out", "w_o", "b_o", "ln_g", "ln_b"]
_SMALL_2D = {"meta_tokens": (N_META, 256), "conv_w": (CONV_WIDTH, 256), "b_in": (1, D_IN), "sinks": (1, N_KV * GROUP)}


def kernel(x, meta_tokens, ln_emb_g, ln_emb_b, w_in, b_in, conv_w, conv_b, w_ra, b_ra, w_ri, b_ri, lru_lambda, sinks, w_rnn_out, w_attn_out, w_o, b_o, ln_g, ln_b, loss_target, m_meta_tokens, m_ln_emb_g, m_ln_emb_b, m_w_in, m_b_in, m_conv_w, m_conv_b, m_w_ra, m_b_ra, m_w_ri, m_b_ri, m_lru_lambda, m_sinks, m_w_rnn_out, m_w_attn_out, m_w_o, m_b_o, m_ln_g, m_ln_b, v_meta_tokens, v_ln_emb_g, v_ln_emb_b, v_w_in, v_b_in, v_conv_w, v_conv_b, v_w_ra, v_b_ra, v_w_ri, v_b_ri, v_lru_lambda, v_sinks, v_w_rnn_out, v_w_attn_out, v_w_o, v_b_o, v_ln_g, v_ln_b):
    w = dict(meta_tokens=meta_tokens, ln_emb_g=ln_emb_g, ln_emb_b=ln_emb_b, w_in=w_in, b_in=b_in, conv_w=conv_w,
             conv_b=conv_b, w_ra=w_ra, b_ra=b_ra, w_ri=w_ri, b_ri=b_ri, lru_lambda=lru_lambda, sinks=sinks,
             w_rnn_out=w_rnn_out, w_attn_out=w_attn_out, w_o=w_o, b_o=b_o, ln_g=ln_g, ln_b=ln_b)
    m = dict(meta_tokens=m_meta_tokens, ln_emb_g=m_ln_emb_g, ln_emb_b=m_ln_emb_b, w_in=m_w_in, b_in=m_b_in,
             conv_w=m_conv_w, conv_b=m_conv_b, w_ra=m_w_ra, b_ra=m_b_ra, w_ri=m_w_ri, b_ri=m_b_ri,
             lru_lambda=m_lru_lambda, sinks=m_sinks, w_rnn_out=m_w_rnn_out, w_attn_out=m_w_attn_out, w_o=m_w_o,
             b_o=m_b_o, ln_g=m_ln_g, ln_b=m_ln_b)
    v = dict(meta_tokens=v_meta_tokens, ln_emb_g=v_ln_emb_g, ln_emb_b=v_ln_emb_b, w_in=v_w_in, b_in=v_b_in,
             conv_w=v_conv_w, conv_b=v_conv_b, w_ra=v_w_ra, b_ra=v_b_ra, w_ri=v_w_ri, b_ri=v_b_ri,
             lru_lambda=v_lru_lambda, sinks=v_sinks, w_rnn_out=v_w_rnn_out, w_attn_out=v_w_attn_out, w_o=v_w_o,
             b_o=v_b_o, ln_g=v_ln_g, ln_b=v_ln_b)
    px, py, pc = _place()
    as_idx = lambda t: jnp.reshape(t, (1,)).astype(jnp.int32)
    c_idx, q_idx, me_idx = as_idx(pc), as_idx(2 * px + py), as_idx(_dev(px, py, pc))

    w3_s, wrg_s, small_s = _cast_small(w_rnn_out, w_attn_out, w_o, w_ra, w_ri, meta_tokens, conv_w)
    wg, wrg, smallw, w3_land = _all_gather(_cast_w_in(w_in), [wrg_s, small_s], [w3_s])
    w3_pending = _split_start(_copies_direct(True), 7, [w3_s], [w3_land], smallw, "gather_w3_start")
    w_full = _relayout_w_in(wg)

    vec = lambda name: w[name].reshape(1, -1)
    p = {k: vec(k) for k in ("ln_emb_g", "ln_emb_b", "b_in", "conv_b", "b_ra", "b_ri", "lru_lambda", "sinks",
                             "b_o", "ln_g", "ln_b")}
    p["b_in"] = p["b_in"] + w3_pending[4][0:1, 0:1]
    s = _step_branches(x, w_full, wrg, smallw, p)
    w3 = _split_wait(_copies_direct(True), *w3_pending[:4], s["lse"], "gather_w3_wait")[1][0]
    t = _step_merge(s, loss_target, w3, p)
    loss = lax.psum(jnp.sum(t["st_out"][3]), ("x", "y", "c"))

    big = {}
    two_d = lambda name: (w[name].shape[-2], w[name].shape[-1])
    proj = ("w_o", "w_rnn_out", "w_attn_out")
    g_proj = [t[k].reshape(N_DEV, 256, D) for k in ("g_wo", "g_wrnn", "g_wattn")]
    g_pending = _split_start(_copies_direct(False), 7, g_proj, [lax.empty((N_DEV, 256, D), bf16) for _ in proj],
                             p["b_o"], "reduce_proj_start")
    u = _step_backward(s, t, wrg, smallw, p, p["conv_b"] + g_pending[4][0:1, 0:1])

    def siblings_start(gs, dep, tag):
        return _split_start(_copies_siblings, 4, gs, [lax.empty((4, *g.shape[1:]), bf16) for g in gs], dep,
                            "reduce_siblings_start_" + tag)

    def chips_start(gs, r1, dep, tag):
        parts = [_pair_sum(g, r, c_idx, "pair_sum_%s%d" % (tag, i)) for i, (g, r) in enumerate(zip(gs, r1))]
        return _split_start(_copies_chips, 3, parts, [lax.empty((3, *q.shape[1:]), bf16) for q in parts], dep,
                            "reduce_chips_start_" + tag)

    g_a, db_in = _mm_dwin(s["h_t"], u["dz"], 0, p["b_o"])
    sib_a = siblings_start([g_a, u["g_wrg"].reshape(N_DEV, 2 * RNN_BLOCK, RNN_BLOCK)], db_in, "a")
    g_b, _ = _mm_dwin(s["h_t"], u["dz"], 1, sib_a[4])
    sib_b = siblings_start([g_b], db_in, "b")
    chp_a = chips_start(*_split_wait(_copies_siblings, *sib_a[:4], sib_b[4], "reduce_siblings_wait_a"), db_in, "a")
    g_proj, g_land = _split_wait(_copies_direct(False), *g_pending[:4], chp_a[4], "reduce_proj_wait")
    for i, name in enumerate(proj):
        res = _adamw_direct(g_proj[i], g_land[i], me_idx, w[name].reshape(two_d(name)), m[name].reshape(two_d(name)),
                            v[name].reshape(two_d(name)), "adamw_" + name)
        big[name] = tuple(r.reshape(w[name].shape) for r in res)
    chp_b = chips_start(*_split_wait(_copies_siblings, *sib_b[:4], big["w_attn_out"][3], "reduce_siblings_wait_b"),
                        db_in, "b")
    u.update(_step_input_grad(u["dz"], w_full, chp_b[4], t["du32"], x, smallw, p))
    u["db_in"] = db_in

    loc = {**t, **u}
    sm, sm2 = _pack_small(loc["st_emb"], loc["vec_rnn"], loc["st_out"], loc["dsr"], loc["db_in"], loc["dmeta"])
    sm, sm2 = _small_allreduce(sm, sm2)
    sm2_mine = lax.dynamic_index_in_dim(sm2, _dev(px, py, pc), 0, keepdims=False)
    two = lambda name, t: t.reshape(_SMALL_2D.get(name, (1, D)))
    small = _small_update(sm, sm2_mine, {k: (two(k, w[k]), two(k, m[k]), two(k, v[k])) for k in _SMALL_NAMES})

    parts_a, r2_a = _split_wait(_copies_chips, *chp_a[:4], small["b_in"][3], "reduce_chips_wait_a")
    parts_b, r2_b = _split_wait(_copies_chips, *chp_b[:4], small["b_in"][2], "reduce_chips_wait_b")
    res = _adamw_big([(parts_a[0], r2_a[0]), (parts_b[0], r2_b[0])], q_idx, w["w_in"].reshape(two_d("w_in")),
                     m["w_in"].reshape(two_d("w_in")), v["w_in"].reshape(two_d("w_in")), "adamw_w_in")
    big["w_in"] = tuple(r.reshape(w["w_in"].shape) for r in res)
    for i, name in enumerate(("w_ra", "w_ri")):
        sq = (RNN_BLOCK, RNN_BLOCK)
        res = _adamw_big([(parts_a[1], r2_a[1])], q_idx, w[name].reshape(sq), m[name].reshape(sq), v[name].reshape(sq),
                         "adamw_" + name, row_off=i)
        big[name] = tuple(r.reshape(w[name].shape) for r in res)
    res = dict(big)
    for k in _SMALL_NAMES:
        res[k] = tuple(t.reshape(w[k].shape) for t in small[k])

    outs = [loss, loc["grad_x"]]
    for j in range(4):
        outs += [res[k][j] for k in _WEIGHTS]
    return tuple(outs)
```

```python
import functools

import jax
import jax.numpy as jnp
from jax import lax
from jax.experimental import pallas as pl
from jax.experimental.pallas import tpu as pltpu

f32, bf16 = jnp.float32, jnp.bfloat16
SDS = jax.ShapeDtypeStruct

N_DEV = 8
D = 2048
N_META = 16
BLK = 128
ROW0 = BLK - N_META
N_RNN_BLOCKS = 8
RNN_BLOCK = D // N_RNN_BLOCKS
CONV_WIDTH = 4
LRU_C = 8.0
HEAD_DIM = 64
N_KV = 4
GROUP = 8
HALF = HEAD_DIM // 2
ROPE_THETA = 10000.0
NEG_INF = -1e30
LN_EPS = 1e-5
ALPHA = 2.0 ** 0.25
D_IN = 12800
SHARD_IN = D_IN // N_DEV
OFF_GR, OFF_Q, OFF_K, OFF_V, OFF_GA, OFF_G = 2048, 4096, 6144, 6400, 6656, 8704
ADAM_LR, ADAM_B1, ADAM_B2, ADAM_EPS, ADAM_WD, ADAM_STEP = 1e-3, 0.9, 0.999, 1e-8, 0.01, 10
VMEM_LIMIT_MB = 56
MESH = pl.DeviceIdType.MESH


def _cp(sem=None, vmem_mb=40):
    return pltpu.CompilerParams(dimension_semantics=sem, vmem_limit_bytes=vmem_mb * 2 ** 20)


def _row_chunk(m):
    best = 16
    for c in range(16, 641, 16):
        if m % c == 0:
            best = c
    return best


def _sigmoid(x):
    return 1.0 / (1.0 + jnp.exp(-x))


def _silu_and_grad(x):
    s = _sigmoid(x)
    return x * s, s * (1.0 + x * (1.0 - s))


def _log_sigmoid(x):
    return jnp.minimum(x, 0.0) - jnp.log1p(jnp.exp(-jnp.abs(x)))


def _ln_rows(v, g, b):
    mu = jnp.mean(v, axis=-1, keepdims=True)
    c = v - mu
    var = jnp.mean(c * c, axis=-1, keepdims=True)
    rstd = lax.rsqrt(var + LN_EPS)
    xhat = c * rstd
    return xhat * g + b, xhat, rstd


def _ln_rows_bwd(dy, g, xhat, rstd):
    dxh = dy * g
    m1 = jnp.mean(dxh, axis=-1, keepdims=True)
    m2 = jnp.mean(dxh * xhat, axis=-1, keepdims=True)
    return rstd * (dxh - m1 - xhat * m2)


def _colsum(v):
    return jnp.sum(v, axis=0, keepdims=True)


def _dot(a, b):
    return jnp.dot(a, b, preferred_element_type=f32)


def _dot_nt(a, b):
    return lax.dot_general(a, b, (((1,), (1,)), ((), ())), preferred_element_type=f32)


def _dot_tn(a, b):
    return lax.dot_general(a, b, (((0,), (0,)), ((), ())), preferred_element_type=f32)


def _meta_full(sw_ref):
    return jnp.concatenate([sw_ref[s, 0:N_META, :] for s in range(N_DEV)], axis=1)


def _ln_emb(x, smallw, g_e, b_e):
    seq = x.shape[1]
    rows = seq + BLK
    nb = rows // BLK

    def body(x_ref, sw_ref, g_ref, b_ref, h32_ref, hb_ref):
        i = pl.program_id(0)
        g, b = g_ref[...], b_ref[...]

        def emit(blk):
            h32_ref[...] = blk
            hb_ref[...] = blk.astype(bf16)

        @pl.when(i == 0)
        def _():
            hm = _ln_rows(_meta_full(sw_ref), g, b)[0]
            emit(jnp.concatenate([jnp.zeros((ROW0, D), f32), hm], axis=0))

        @pl.when(i > 0)
        def _():
            emit(_ln_rows(x_ref[0], g, b)[0])

    return pl.pallas_call(
        body, grid=(nb,),
        in_specs=[pl.BlockSpec((1, BLK, D), lambda i: (0, jnp.maximum(i - 1, 0), 0)),
                  pl.BlockSpec((N_DEV, 24, 256), lambda i: (0, 0, 0)),
                  pl.BlockSpec((1, D), lambda i: (0, 0)),
                  pl.BlockSpec((1, D), lambda i: (0, 0))],
        out_specs=[pl.BlockSpec((BLK, D), lambda i: (i, 0)),
                   pl.BlockSpec((BLK, D), lambda i: (i, 0))],
        out_shape=[SDS((rows, D), f32), SDS((rows, D), bf16)],
        name="ln_emb", compiler_params=_cp(("arbitrary",)),
    )(x, smallw, g_e, b_e)


def _ln_emb_bwd(dh, du32, x, smallw, g_e):
    seq = x.shape[1]
    rows = seq + BLK
    nb = rows // BLK

    def body(dh_ref, du_ref, x_ref, sw_ref, g_ref, gx_ref, dmeta_ref, st_ref):
        i = pl.program_id(0)
        g = g_ref[...]
        dht = dh_ref[...] + ALPHA * du_ref[...]

        @pl.when(i == 0)
        def _():
            v = jnp.concatenate([jnp.zeros((ROW0, D), f32), _meta_full(sw_ref)], axis=0)
            valid = lax.broadcasted_iota(jnp.int32, (BLK, 1), 0) >= ROW0
            d = jnp.where(valid, dht, 0.0)
            _, xhat, rstd = _ln_rows(v, g, 0.0)
            dv = _ln_rows_bwd(d, g, xhat, rstd)
            dmeta_ref[...] = dv[ROW0:, :]
            st_ref[...] = jnp.concatenate([_colsum(d * xhat), _colsum(d), jnp.zeros((6, D), f32)], axis=0)

        @pl.when(i > 0)
        def _():
            _, xhat, rstd = _ln_rows(x_ref[0], g, 0.0)
            gx_ref[0] = _ln_rows_bwd(dht, g, xhat, rstd)
            st_ref[0:1, :] += _colsum(dht * xhat)
            st_ref[1:2, :] += _colsum(dht)

    return pl.pallas_call(
        body, grid=(nb,),
        in_specs=[pl.BlockSpec((BLK, D), lambda i: (i, 0)),
                  pl.BlockSpec((BLK, D), lambda i: (i, 0)),
                  pl.BlockSpec((1, BLK, D), lambda i: (0, jnp.maximum(i - 1, 0), 0)),
                  pl.BlockSpec((N_DEV, 24, 256), lambda i: (0, 0, 0)),
                  pl.BlockSpec((1, D), lambda i: (0, 0))],
        out_specs=[pl.BlockSpec((1, BLK, D), lambda i: (0, jnp.maximum(i - 1, 0), 0)),
                   pl.BlockSpec((N_META, D), lambda i: (0, 0)),
                   pl.BlockSpec((8, D), lambda i: (0, 0))],
        out_shape=[SDS((1, seq, D), f32), SDS((N_META, D), f32), SDS((8, D), f32)],
        name="ln_emb_bwd", compiler_params=_cp(("arbitrary",)),
    )(dh, du32, x, smallw, g_e)


def _mm(a, b, *, name, nt=False, sel=None, bias=None, out_dtype=f32, tn=512):
    m, k = a.shape
    cm = _row_chunk(m)
    stacked = sel is not None
    n = D if stacked else (b.shape[0] if nt else b.shape[1])
    am = m
    if stacked and nt:
        b_spec = pl.BlockSpec((tn // 256, None, 256, D), lambda j, i: (j, sel, 0, 0))
    elif stacked:
        b_spec = pl.BlockSpec((N_DEV, None, 256, tn), lambda j, i: (0, sel, 0, j))
    elif nt:
        b_spec = pl.BlockSpec((tn, k), lambda j, i: (j, 0))
    else:
        b_spec = pl.BlockSpec((k, tn), lambda j, i: (0, j))
    in_specs = [pl.BlockSpec((am, k), lambda j, i: (i, 0)), b_spec]
    args = [a, b]
    if bias is not None:
        in_specs.append(pl.BlockSpec((1, tn), lambda j, i: (0, j)))
        args.append(bias)

    def body(*refs):
        a_ref, b_ref, o_ref = refs[0], refs[1], refs[-1]
        bm = b_ref[...]
        if stacked:
            bm = bm.reshape((tn, D) if nt else (D, tn))
        for c in range(am // cm):
            acc = (_dot_nt if nt else _dot)(a_ref[c * cm:(c + 1) * cm, :], bm)
            if bias is not None:
                acc = acc + refs[2][...]
            o_ref[c * cm:(c + 1) * cm, :] = acc.astype(out_dtype)

    return pl.pallas_call(
        body, grid=(n // tn, m // am), in_specs=in_specs,
        out_specs=pl.BlockSpec((am, tn), lambda j, i: (i, j)),
        out_shape=SDS((m, n), out_dtype), name=name, compiler_params=_cp(("arbitrary", "arbitrary"), 48),
    )(*args)


def _mm_dh(dz, w_t, after):
    rows = dz.shape[0]
    tk, tn = 2560, 512
    cm = _row_chunk(rows)

    def body(a_ref, w_ref, after_ref, o_ref):
        kk = pl.program_id(1)
        for c in range(rows // cm):
            acc = _dot(a_ref[c * cm:(c + 1) * cm, :], w_ref[...])

            @pl.when(kk == 0)
            def _():
                o_ref[c * cm:(c + 1) * cm, :] = acc

            @pl.when(kk > 0)
            def _():
                o_ref[c * cm:(c + 1) * cm, :] += acc

    return pl.pallas_call(
        body, grid=(D // tn, D_IN // tk),
        in_specs=[pl.BlockSpec((rows, tk), lambda j, kk: (0, kk)),
                  pl.BlockSpec((tk, tn), lambda j, kk: (kk, j)),
                  pl.BlockSpec(memory_space=pl.ANY)],
        out_specs=pl.BlockSpec((rows, tn), lambda j, kk: (0, j)),
        out_shape=SDS((rows, D), f32), name="mm_dh", compiler_params=_cp(("arbitrary", "arbitrary"), 48),
    )(dz, w_t, after)


W_IN_HALF = D // 2


def _mm_dwin(hb, dz, half, after):
    rows = dz.shape[0]
    tc = 640
    with_db = half == 0

    def body(dz_ref, h_ref, after_ref, o_ref, *db_ref):
        o_ref[...] = _dot_tn(dz_ref[...], h_ref[...]).astype(bf16)
        if with_db:
            def step(i, s):
                blk = dz_ref[pl.ds(pl.multiple_of(i * BLK, BLK), BLK), :].astype(f32)
                return s + blk.reshape(BLK // 8, 8, tc).sum(axis=0)
            s = lax.fori_loop(0, rows // BLK, step, jnp.zeros((8, tc), f32))
            db_ref[0][...] = jnp.broadcast_to(_colsum(s), (8, tc))

    out_specs = [pl.BlockSpec((tc, W_IN_HALF), lambda j: (j, 0))]
    out_shape = [SDS((D_IN, W_IN_HALF), bf16)]
    if with_db:
        out_specs.append(pl.BlockSpec((8, tc), lambda j: (0, j)))
        out_shape.append(SDS((8, D_IN), f32))
    return pl.pallas_call(
        body, grid=(D_IN // tc,),
        in_specs=[pl.BlockSpec((rows, tc), lambda j: (0, j)),
                  pl.BlockSpec((rows, W_IN_HALF), lambda j: (0, half)),
                  pl.BlockSpec(memory_space=pl.ANY)],
        out_specs=out_specs, out_shape=out_shape,
        name="mm_dwin_%d" % half, compiler_params=_cp(("arbitrary",), 48),
    )(dz, hb, after)


SCAN_ROWS = 32


def _scan8(a, b, reverse):
    idx = lax.broadcasted_iota(jnp.int32, a.shape, 0)
    for s in (1, 2, 4):
        sh = 8 - s if reverse else s
        a_sh, b_sh = pltpu.roll(a, sh, 0), pltpu.roll(b, sh, 0)
        m = (idx < 8 - s) if reverse else (idx >= s)
        b = jnp.where(m, a * b_sh + b, b)
        a = jnp.where(m, a * a_sh, a)
    return a, b


def _shift_rows(prev8, cur, k):
    ext = jnp.concatenate([prev8, cur], axis=0)
    return pltpu.roll(ext, k, 0)[8:, :]


def _gates(xc, w_ra, b_ra, w_ri, b_ri, ls):
    xb = xc.astype(bf16)
    r = _sigmoid(_dot(xb, w_ra) + b_ra)
    ig = _sigmoid(_dot(xb, w_ri) + b_ri)
    la = LRU_C * r * ls
    a = jnp.exp(la)
    mult = jnp.sqrt(jnp.tanh(-la) * (1.0 + a * a))
    return xb, r, ig, a, mult


_RNN_IN_SPECS = lambda rows: [
    pl.BlockSpec((1, 24, 256), lambda n: (n, 0, 0)),
    pl.BlockSpec((1, RNN_BLOCK), lambda n: (0, n)),
    pl.BlockSpec((N_DEV, 2, None, 32, RNN_BLOCK), lambda n: (0, 0, n, 0, 0)),
    pl.BlockSpec((1, RNN_BLOCK), lambda n: (0, n)),
    pl.BlockSpec((1, RNN_BLOCK), lambda n: (0, n)),
    pl.BlockSpec((1, RNN_BLOCK), lambda n: (0, n)),
]


def _rnn_fwd(z, smallw, conv_b, wrg, b_ra, b_ri, lam):
    rows = z.shape[0]
    nb = rows // BLK
    col = lambda off: pl.BlockSpec((rows, RNN_BLOCK), lambda n: (0, off // RNN_BLOCK + n))

    def body(xr_ref, gr_ref, sw_ref, cb_ref, w_ref, bra_ref, bri_ref, lam_ref, xc_ref, hr_ref, ya_ref, yat_ref, a_s):
        cw = sw_ref[0, N_META:24, :]
        cb = cb_ref[...]
        w_ra = w_ref[:, 0].reshape(RNN_BLOCK, RNN_BLOCK)
        w_ri = w_ref[:, 1].reshape(RNN_BLOCK, RNN_BLOCK)
        b_ra_v, b_ri_v = bra_ref[...], bri_ref[...]
        ls = _log_sigmoid(lam_ref[...])
        rid = lax.broadcasted_iota(jnp.int32, (BLK, 1), 0)

        def blk_step(i, carry):
            r0 = pl.multiple_of(i * BLK, BLK)
            grow = rid + r0
            valid = grow >= ROW0
            cur = jnp.where(valid, xr_ref[pl.ds(r0, BLK), :], 0.0)
            prev8 = xr_ref[pl.ds(pl.multiple_of(jnp.maximum(r0 - 8, 0), 8), 8), :] * (i > 0).astype(f32)
            xc = cb + cw[0:1] * cur
            for k in range(1, CONV_WIDTH):
                xc = xc + cw[k:k + 1] * _shift_rows(prev8, cur, k)
            xc_ref[pl.ds(r0, BLK), :] = xc
            _, _, ig, a, mult = _gates(xc, w_ra, b_ra_v, w_ri, b_ri_v, ls)
            mult = jnp.where(grow == ROW0, 1.0, mult)
            a_s[pl.ds(r0, BLK), :] = a
            hr_ref[pl.ds(r0, BLK), :] = jnp.where(valid, mult * ig * xc, 0.0)
            return carry

        lax.fori_loop(0, nb, blk_step, 0)

        def scan_step(j, carry):
            r0 = pl.multiple_of(j * SCAN_ROWS, SCAN_ROWS)
            tiles = [_scan8(a_s[pl.ds(r0 + 8 * k, 8), :], hr_ref[pl.ds(r0 + 8 * k, 8), :], False)
                     for k in range(SCAN_ROWS // 8)]
            for k, (a, b) in enumerate(tiles):
                h = b + a * carry
                hr_ref[pl.ds(r0 + 8 * k, 8), :] = h
                carry = jnp.broadcast_to(h[7:8, :], (8, RNN_BLOCK))
            return carry

        lax.fori_loop(0, rows // SCAN_ROWS, scan_step, jnp.zeros((8, RNN_BLOCK), f32))

        def gate_step(i, carry):
            r0 = pl.multiple_of(i * BLK, BLK)
            ya_ref[pl.ds(r0, BLK), :] = (hr_ref[pl.ds(r0, BLK), :]
                                         * _silu_and_grad(gr_ref[pl.ds(r0, BLK), :])[0]).astype(bf16)
            return carry

        lax.fori_loop(0, nb, gate_step, 0)
        yat_ref[...] = ya_ref[...].astype(f32).T.astype(bf16)

    return pl.pallas_call(
        body, grid=(N_RNN_BLOCKS,),
        in_specs=[col(0), col(OFF_GR)] + _RNN_IN_SPECS(rows),
        out_specs=[pl.BlockSpec((rows, RNN_BLOCK), lambda n: (0, n))] * 3
                  + [pl.BlockSpec((RNN_BLOCK, rows), lambda n: (n, 0))],
        out_shape=[SDS((rows, D), f32), SDS((rows, D), f32), SDS((rows, D), bf16), SDS((D, rows), bf16)],
        scratch_shapes=[pltpu.VMEM((rows, RNN_BLOCK), f32)],
        name="rnn_fwd", compiler_params=_cp(("arbitrary",)),
    )(z, z, smallw, conv_b, wrg, b_ra, b_ri, lam)


def _rnn_bwd(dya, hr, xc, z, smallw, conv_b, wrg, b_ra, b_ri, lam):
    rows = z.shape[0]
    nb = rows // BLK
    col = lambda off: pl.BlockSpec((rows, RNN_BLOCK), lambda n: (0, off // RNN_BLOCK + n))
    blk = pl.BlockSpec((rows, RNN_BLOCK), lambda n: (0, n))

    def body(dya_ref, hr_ref, xc_ref, xr_ref, gr_ref, sw_ref, cb_ref, w_ref, bra_ref, bri_ref, lam_ref,
             dxr_ref, dgr_ref, dw_ref, vec_ref, a_s, lam_s, dxc_s, r_s, ig_s, mult_s, dw_s):
        cw = sw_ref[0, N_META:24, :]
        w_ra = w_ref[:, 0].reshape(RNN_BLOCK, RNN_BLOCK)
        w_ri = w_ref[:, 1].reshape(RNN_BLOCK, RNN_BLOCK)
        b_ra_v, b_ri_v = bra_ref[...], bri_ref[...]
        lam_v = lam_ref[...]
        ls = _log_sigmoid(lam_v)
        rid = lax.broadcasted_iota(jnp.int32, (BLK, 1), 0)
        zrow = jnp.zeros((1, RNN_BLOCK), f32)

        def p1(i, carry):
            r0 = pl.multiple_of(i * BLK, BLK)
            sl = pl.ds(r0, BLK)
            _, r, ig, a, mult = _gates(xc_ref[sl, :], w_ra, b_ra_v, w_ri, b_ri_v, ls)
            a_s[sl, :] = a
            r_s[sl, :] = r
            ig_s[sl, :] = ig
            mult_s[sl, :] = mult
            sg, dsg = _silu_and_grad(gr_ref[sl, :])
            d = dya_ref[sl, :]
            lam_s[sl, :] = d * sg
            dgr_ref[sl, :] = (d * hr_ref[sl, :] * dsg).astype(bf16)
            return carry

        lax.fori_loop(0, nb, p1, 0)

        def p2(jj, carry):
            r0 = pl.multiple_of((rows // SCAN_ROWS - 1 - jj) * SCAN_ROWS, SCAN_ROWS)
            idx = lax.broadcasted_iota(jnp.int32, (8, RNN_BLOCK), 0)
            tiles = []
            for k in range(SCAN_ROWS // 8):
                sl = pl.ds(r0 + 8 * k, 8)
                a, g = a_s[sl, :], lam_s[sl, :]
                tiles.append((g, *_scan8(a, a * g, True)))
            for k in reversed(range(SCAN_ROWS // 8)):
                g, ca, cb_ = tiles[k]
                mu = cb_ + ca * carry
                lam_s[pl.ds(r0 + 8 * k, 8), :] = g + jnp.where(idx < 7, pltpu.roll(mu, 7, 0), carry)
                carry = jnp.broadcast_to(mu[0:1, :], (8, RNN_BLOCK))
            return carry

        lax.fori_loop(0, rows // SCAN_ROWS, p2, jnp.zeros((8, RNN_BLOCK), f32))

        dw_s[...] = jnp.zeros_like(dw_s)

        def p3(i, carry):
            d_bra, d_bri, d_ls = carry
            r0 = pl.multiple_of(i * BLK, BLK)
            sl = pl.ds(r0, BLK)
            grow = rid + r0
            valid = grow >= ROW0
            first = grow == ROW0
            xcv = xc_ref[sl, :]
            xb = xcv.astype(bf16)
            r, ig, a = r_s[sl, :], ig_s[sl, :], a_s[sl, :]
            mult = jnp.where(first, 1.0, mult_s[sl, :])
            lam_t = lam_s[sl, :]
            du = jnp.where(valid, lam_t, 0.0)
            hprev = _shift_rows(hr_ref[pl.ds(pl.multiple_of(jnp.maximum(r0 - 8, 0), 8), 8), :] * (i > 0).astype(f32), hr_ref[sl, :], 1)
            da = lam_t * hprev
            dmult = jnp.where(first, 0.0, du * ig * xcv)
            di = du * mult * xcv
            dxc = du * mult * ig
            ratio = jnp.where(valid & jnp.logical_not(first), a * a / mult, 0.0)
            dla = da * a - dmult * ratio
            dpr = (dla * (LRU_C * ls)) * r * (1.0 - r)
            dpi = di * ig * (1.0 - ig)
            dprb, dpib = dpr.astype(bf16), dpi.astype(bf16)
            dw_s[0] += _dot_tn(xb, dprb)
            dw_s[1] += _dot_tn(xb, dpib)
            dxc_s[sl, :] = dxc + _dot_nt(dprb, w_ra) + _dot_nt(dpib, w_ri)
            return d_bra + _colsum(dpr), d_bri + _colsum(dpi), d_ls + _colsum(dla * (LRU_C * r))

        d_bra, d_bri, d_ls = lax.fori_loop(0, nb, p3, (zrow, zrow, zrow))

        def p4(i, carry):
            d_cb, d_w0, d_w1, d_w2, d_w3 = carry
            r0 = pl.multiple_of(i * BLK, BLK)
            sl = pl.ds(r0, BLK)
            grow = rid + r0
            valid = grow >= ROW0
            dxc = dxc_s[sl, :]
            nxt = dxc_s[pl.ds(pl.multiple_of(jnp.minimum(r0 + BLK, rows - 8), 8), 8), :] * (i < nb - 1).astype(f32)
            ext = jnp.concatenate([dxc, nxt], axis=0)
            dxr = cw[0:1] * dxc
            for k in range(1, CONV_WIDTH):
                dxr = dxr + cw[k:k + 1] * pltpu.roll(ext, BLK + 8 - k, 0)[:BLK, :]
            dxr_ref[sl, :] = jnp.where(valid, dxr, 0.0).astype(bf16)
            cur = jnp.where(valid, xr_ref[sl, :], 0.0)
            prev8 = xr_ref[pl.ds(pl.multiple_of(jnp.maximum(r0 - 8, 0), 8), 8), :] * (i > 0).astype(f32)
            dws = [d_w0 + _colsum(dxc * cur)]
            for k, acc in ((1, d_w1), (2, d_w2), (3, d_w3)):
                dws.append(acc + _colsum(dxc * _shift_rows(prev8, cur, k)))
            return (d_cb + _colsum(dxc), *dws)

        d_cb, d_w0, d_w1, d_w2, d_w3 = lax.fori_loop(0, nb, p4, (zrow,) * 5)

        d_lam = d_ls * _sigmoid(-lam_v)
        vec_ref[...] = jnp.concatenate([d_bra, d_bri, d_lam, d_cb, d_w0, d_w1, d_w2, d_w3], axis=0)
        dw_ref[:, 0] = dw_s[0].astype(bf16).reshape(N_DEV, 32, RNN_BLOCK)
        dw_ref[:, 1] = dw_s[1].astype(bf16).reshape(N_DEV, 32, RNN_BLOCK)

    return pl.pallas_call(
        body, grid=(N_RNN_BLOCKS,),
        in_specs=[blk, blk, blk, col(0), col(OFF_GR)] + _RNN_IN_SPECS(rows),
        out_specs=[blk, blk,
                   pl.BlockSpec((N_DEV, 2, None, 32, RNN_BLOCK), lambda n: (0, 0, n, 0, 0)),
                   pl.BlockSpec((8, RNN_BLOCK), lambda n: (0, n))],
        out_shape=[SDS((rows, D), bf16), SDS((rows, D), bf16),
                   SDS((N_DEV, 2, N_RNN_BLOCKS, 32, RNN_BLOCK), bf16), SDS((8, D), f32)],
        scratch_shapes=[pltpu.VMEM((rows, RNN_BLOCK), f32)] * 6 + [pltpu.VMEM((2, RNN_BLOCK, RNN_BLOCK), f32)],
        name="rnn_bwd", compiler_params=_cp(("arbitrary",), 48),
    )(dya, hr, xc, z, z, smallw, conv_b, wrg, b_ra, b_ri, lam)


def _rope_tables(rows):
    half = jnp.arange(HALF, dtype=f32)
    inv = ROPE_THETA ** (-half / HALF)
    pos = (jnp.arange(rows) - ROW0).astype(f32)
    ang = pos[:, None] * inv[None, :]
    cos, sin = jnp.cos(ang), jnp.sin(ang)
    cos128 = jnp.concatenate([cos, cos, cos, cos], axis=1)
    sin128 = jnp.concatenate([-sin, sin, -sin, sin], axis=1)
    return cos128, sin128


def _rope128(x, cos128, sin128):
    lane = lax.broadcasted_iota(jnp.int32, x.shape, 1)
    swapped = jnp.where(lane % HEAD_DIM < HALF, pltpu.roll(x, 128 - HALF, 1), pltpu.roll(x, HALF, 1))
    return x * cos128 + swapped * sin128


def _qkv_prep(z, cos128, sin128):
    rows = z.shape[0]

    def body(q_ref, kv_ref, c_ref, s_ref, qo_ref, ko_ref, vo_ref):
        c, s = c_ref[...], s_ref[...]
        for g in range(D // 128):
            qo_ref[:, g * 128:(g + 1) * 128] = (_rope128(q_ref[:, g * 128:(g + 1) * 128], c, s)
                                                * (HEAD_DIM ** -0.5)).astype(bf16)
        for g in range(2):
            kr = _rope128(kv_ref[:, g * 128:(g + 1) * 128], c, s)
            for j in range(2):
                ko_ref[2 * g + j] = kr[:, j * HEAD_DIM:(j + 1) * HEAD_DIM].astype(bf16)
        for h in range(N_KV):
            vo_ref[h] = kv_ref[:, 256 + h * HEAD_DIM:256 + (h + 1) * HEAD_DIM].astype(bf16)

    return pl.pallas_call(
        body, grid=(rows // BLK,),
        in_specs=[pl.BlockSpec((BLK, D), lambda i: (i, OFF_Q // D)),
                  pl.BlockSpec((BLK, 512), lambda i: (i, OFF_K // 512)),
                  pl.BlockSpec((BLK, 128), lambda i: (i, 0)),
                  pl.BlockSpec((BLK, 128), lambda i: (i, 0))],
        out_specs=[pl.BlockSpec((BLK, D), lambda i: (i, 0)),
                   pl.BlockSpec((N_KV, BLK, HEAD_DIM), lambda i: (0, i, 0)),
                   pl.BlockSpec((N_KV, BLK, HEAD_DIM), lambda i: (0, i, 0))],
        out_shape=[SDS((rows, D), bf16), SDS((N_KV, rows, HEAD_DIM), bf16), SDS((N_KV, rows, HEAD_DIM), bf16)],
        name="qkv_prep", compiler_params=_cp(("arbitrary",)),
    )(z, z, cos128, sin128)


def _attn_mask(n):
    qi = n * BLK + lax.broadcasted_iota(jnp.int32, (BLK, 2 * BLK + N_META), 0)
    c = lax.broadcasted_iota(jnp.int32, (BLK, 2 * BLK + N_META), 1)
    jb = (n - 1) * BLK + c
    band = (jb >= BLK) & (jb <= qi) & (qi - jb < BLK)
    meta = (ROW0 + c - 2 * BLK) <= qi
    return ((c < 2 * BLK) & band) | ((c >= 2 * BLK) & meta)


N_KEYS = 2 * BLK + N_META


def _stack_heads(t):
    return jnp.concatenate([t[:, g * HEAD_DIM:(g + 1) * HEAD_DIM] for g in range(GROUP)], axis=0)


def _sink_column(sink_ref, h):
    g = lax.broadcasted_iota(jnp.int32, (GROUP, 1, 1), 0)
    col = jnp.zeros((GROUP, 1, 1), f32)
    for j in range(GROUP):
        col = jnp.where(g == j, sink_ref[h * GROUP + j], col)
    return col


def _kv_specs(last):
    cl = lambda n: jnp.minimum(n, last)
    return [pl.BlockSpec((None, N_META, HEAD_DIM), lambda h, n: (h, ROW0 // N_META, 0)),
            pl.BlockSpec((None, BLK, HEAD_DIM), lambda h, n: (h, jnp.maximum(cl(n) - 1, 0), 0)),
            pl.BlockSpec((None, BLK, HEAD_DIM), lambda h, n: (h, cl(n), 0))]


def _attn_fwd(q_r, k_r, v_b, z, sinks):
    rows = q_r.shape[0]
    nb = rows // BLK

    def body(sink_ref, q_ref, km_ref, kp_ref, kc_ref, vm_ref, vp_ref, vc_ref, ga_ref, o_ref, yb_ref, ybt_ref, lse_ref):
        h, n = pl.program_id(0), pl.program_id(1)
        kk = jnp.concatenate([kp_ref[...], kc_ref[...], km_ref[...]], axis=0)
        vv = jnp.concatenate([vp_ref[...], vc_ref[...], vm_ref[...]], axis=0)
        q2 = _stack_heads(q_ref[...])
        s = jnp.where(_attn_mask(n)[None], _dot_nt(q2, kk).reshape(GROUP, BLK, N_KEYS), NEG_INF)
        sink = _sink_column(sink_ref, h)
        m = jnp.maximum(jnp.max(s, axis=-1, keepdims=True), sink)
        p = jnp.exp(s - m)
        den = jnp.sum(p, axis=-1, keepdims=True) + jnp.exp(sink - m)
        o2 = _dot((p / den).astype(bf16).reshape(GROUP * BLK, N_KEYS), vv)
        lse = m + jnp.log(den)
        for g in range(GROUP):
            o_ref[:, g * HEAD_DIM:(g + 1) * HEAD_DIM] = o2[g * BLK:(g + 1) * BLK]
            lse_ref[:, g:g + 1] = lse[g]
        yb = o_ref[...] * _silu_and_grad(ga_ref[...])[0]
        yb_ref[...] = yb.astype(bf16)
        ybt_ref[...] = yb.T.astype(bf16)

    tile = pl.BlockSpec((BLK, 512), lambda h, n: (n, h))
    return pl.pallas_call(
        body, grid=(N_KV, nb),
        in_specs=[pl.BlockSpec(memory_space=pltpu.SMEM), tile] + _kv_specs(nb - 1) + _kv_specs(nb - 1)
                 + [pl.BlockSpec((BLK, 512), lambda h, n: (n, OFF_GA // 512 + h))],
        out_specs=[tile, tile, pl.BlockSpec((512, BLK), lambda h, n: (h, n)),
                   pl.BlockSpec((None, BLK, GROUP), lambda h, n: (h, n, 0))],
        out_shape=[SDS((rows, D), f32), SDS((rows, D), bf16), SDS((D, rows), bf16),
                   SDS((N_KV, rows, GROUP), f32)],
        name="attn_fwd", compiler_params=_cp(("arbitrary", "arbitrary")),
    )(sinks, q_r, k_r, k_r, k_r, v_b, v_b, v_b, z)


def _attn_bwd(dyb, o32, lse, q_r, k_r, v_b, z, sinks):
    rows = q_r.shape[0]
    nb = rows // BLK
    cl = lambda n: jnp.minimum(n, nb - 1)

    def body(sink_ref, dyb_ref, o_ref, lse_ref, q_ref, km_ref, kp_ref, kc_ref, vm_ref, vp_ref, vc_ref, ga_ref,
             dq_ref, dga_ref, dk_ref, dv_ref, dkm_ref, dvm_ref, dsr_ref, ck_s, cv_s):
        h, n = pl.program_id(0), pl.program_id(1)

        @pl.when(n == 0)
        def _():
            dkm_ref[...] = jnp.zeros_like(dkm_ref)
            dvm_ref[...] = jnp.zeros_like(dvm_ref)
            ck_s[...] = jnp.zeros_like(ck_s)
            cv_s[...] = jnp.zeros_like(cv_s)

        @pl.when(n < nb)
        def _():
            kk = jnp.concatenate([kp_ref[...], kc_ref[...], km_ref[...]], axis=0)
            vv = jnp.concatenate([vp_ref[...], vc_ref[...], vm_ref[...]], axis=0)
            sg, dsg = _silu_and_grad(ga_ref[...])
            dyb_v = dyb_ref[...]
            o_v = o_ref[...]
            dga_ref[...] = (dyb_v * o_v * dsg).astype(bf16)
            q2 = _stack_heads(q_ref[...])
            do2 = _stack_heads(dyb_v * sg)
            lse_v = lse_ref[...]
            lse = jnp.concatenate([lse_v[:, g:g + 1] for g in range(GROUP)], axis=0).reshape(GROUP, BLK, 1)
            delta = jnp.sum(do2 * _stack_heads(o_v), axis=-1, keepdims=True).reshape(GROUP, BLK, 1)
            s = jnp.where(_attn_mask(n)[None], _dot_nt(q2, kk).reshape(GROUP, BLK, N_KEYS), NEG_INF)
            p = jnp.exp(s - lse)
            do2b = do2.astype(bf16)
            ds = (p * (_dot_nt(do2b, vv).reshape(GROUP, BLK, N_KEYS) - delta)).astype(bf16)
            ds = ds.reshape(GROUP * BLK, N_KEYS)
            dsr = -jnp.exp(_sink_column(sink_ref, h) - lse) * delta
            dq2 = _dot(ds, kk)
            for g in range(GROUP):
                dq_ref[:, g * HEAD_DIM:(g + 1) * HEAD_DIM] = dq2[g * BLK:(g + 1) * BLK]
                dsr_ref[:, g:g + 1] = dsr[g]
            dkk = _dot_tn(ds, q2)
            dvv = _dot_tn(p.astype(bf16).reshape(GROUP * BLK, N_KEYS), do2b)
            dk_ref[...] = ck_s[...] + dkk[:BLK]
            dv_ref[...] = cv_s[...] + dvv[:BLK]
            ck_s[...] = dkk[BLK:2 * BLK]
            cv_s[...] = dvv[BLK:2 * BLK]
            dkm_ref[...] += dkk[2 * BLK:]
            dvm_ref[...] += dvv[2 * BLK:]

        @pl.when(n == nb)
        def _():
            dk_ref[...] = ck_s[...]
            dv_ref[...] = cv_s[...]

    tile = pl.BlockSpec((BLK, 512), lambda h, n: (cl(n), h))
    kvout = pl.BlockSpec((None, BLK, HEAD_DIM), lambda h, n: (h, jnp.maximum(n - 1, 0), 0))
    mout = pl.BlockSpec((None, N_META, HEAD_DIM), lambda h, n: (h, 0, 0))
    stat = pl.BlockSpec((None, BLK, GROUP), lambda h, n: (h, cl(n), 0))
    return pl.pallas_call(
        body, grid=(N_KV, nb + 1),
        in_specs=[pl.BlockSpec(memory_space=pltpu.SMEM), tile, tile, stat, tile] + _kv_specs(nb - 1)
                 + _kv_specs(nb - 1) + [pl.BlockSpec((BLK, 512), lambda h, n: (cl(n), OFF_GA // 512 + h))],
        out_specs=[tile, tile, kvout, kvout, mout, mout, stat],
        out_shape=[SDS((rows, D), f32), SDS((rows, D), bf16),
                   SDS((N_KV, rows, HEAD_DIM), f32), SDS((N_KV, rows, HEAD_DIM), f32),
                   SDS((N_KV, N_META, HEAD_DIM), f32), SDS((N_KV, N_META, HEAD_DIM), f32),
                   SDS((N_KV, rows, GROUP), f32)],
        scratch_shapes=[pltpu.VMEM((BLK, HEAD_DIM), f32), pltpu.VMEM((BLK, HEAD_DIM), f32)],
        name="attn_bwd", compiler_params=_cp(("arbitrary", "arbitrary")),
    )(sinks, dyb, o32, lse, q_r, k_r, k_r, k_r, v_b, v_b, v_b, z)


def _qkv_finish(dq, dk, dv, dkm, dvm, cos128, sin128):
    rows = dq.shape[0]

    def body(dq_ref, dk_ref, dv_ref, dkm_ref, dvm_ref, c_ref, s_ref, oq_ref, okv_ref):
        first = (pl.program_id(0) == 0).astype(f32)
        c, s = c_ref[...], -s_ref[...]
        for g in range(D // 128):
            oq_ref[:, g * 128:(g + 1) * 128] = (_rope128(dq_ref[:, g * 128:(g + 1) * 128], c, s)
                                                * (HEAD_DIM ** -0.5)).astype(bf16)
        pad = jnp.zeros((ROW0, HEAD_DIM), f32)
        ks = [dk_ref[h] + first * jnp.concatenate([pad, dkm_ref[h]], axis=0) for h in range(N_KV)]
        vs = [dv_ref[h] + first * jnp.concatenate([pad, dvm_ref[h]], axis=0) for h in range(N_KV)]
        for g in range(2):
            kp = jnp.concatenate([ks[2 * g], ks[2 * g + 1]], axis=1)
            okv_ref[:, g * 128:(g + 1) * 128] = _rope128(kp, c, s).astype(bf16)
            okv_ref[:, 256 + g * 128:256 + (g + 1) * 128] = jnp.concatenate([vs[2 * g], vs[2 * g + 1]], axis=1).astype(bf16)

    kv = pl.BlockSpec((N_KV, BLK, HEAD_DIM), lambda i: (0, i, 0))
    mt = pl.BlockSpec((N_KV, N_META, HEAD_DIM), lambda i: (0, 0, 0))
    return pl.pallas_call(
        body, grid=(rows // BLK,),
        in_specs=[pl.BlockSpec((BLK, D), lambda i: (i, 0)), kv, kv, mt, mt,
                  pl.BlockSpec((BLK, 128), lambda i: (i, 0)), pl.BlockSpec((BLK, 128), lambda i: (i, 0))],
        out_specs=[pl.BlockSpec((BLK, D), lambda i: (i, 0)), pl.BlockSpec((BLK, 512), lambda i: (i, 0))],
        out_shape=[SDS((rows, D), bf16), SDS((rows, 512), bf16)],
        name="qkv_finish", compiler_params=_cp(("arbitrary",)),
    )(dq, dk, dv, dkm, dvm, cos128, sin128)


_TW = 512


def _mix_specs(rows):
    tr = _row_chunk(rows)
    tile = pl.BlockSpec((tr, _TW), lambda i, j: (i, j))
    ga = pl.BlockSpec((tr, _TW), lambda i, j: (i, OFF_G // _TW + j))
    gb = pl.BlockSpec((tr, _TW), lambda i, j: (i, (OFF_G + D) // _TW + j))
    return (rows // tr, D // _TW), tile, ga, gb


def _mix_fwd(y_a, y_b, z):
    rows = y_a.shape[0]
    tw = 256
    col = lambda off: pl.BlockSpec((rows, tw), lambda j: (0, off // tw + j))

    def body(ya_ref, yb_ref, ga_ref, gb_ref, o_ref, ot_ref):
        mixed = _sigmoid(ga_ref[...]) * ya_ref[...] + _sigmoid(gb_ref[...]) * yb_ref[...]
        o_ref[...] = mixed.astype(bf16)
        ot_ref[...] = mixed.T.astype(bf16)

    return pl.pallas_call(
        body, grid=(D // tw,), in_specs=[col(0), col(0), col(OFF_G), col(OFF_G + D)],
        out_specs=[col(0), pl.BlockSpec((tw, rows), lambda j: (j, 0))],
        out_shape=[SDS((rows, D), bf16), SDS((D, rows), bf16)],
        name="mix_fwd", compiler_params=_cp(("arbitrary",)),
    )(y_a, y_b, z, z)


def _mix_bwd(dmixed, y_a, y_b, z):
    rows = y_a.shape[0]
    grid, _mix_tile, _mix_ga, _mix_gb = _mix_specs(rows)

    def body(dm_ref, ya_ref, yb_ref, ga_ref, gb_ref, dya_ref, dyb_ref, dga_ref, dgb_ref):
        dm = dm_ref[...]
        sa, sb = _sigmoid(ga_ref[...]), _sigmoid(gb_ref[...])
        dya_ref[...] = (dm * sa).astype(bf16)
        dyb_ref[...] = (dm * sb).astype(bf16)
        dga_ref[...] = (dm * ya_ref[...] * sa * (1.0 - sa)).astype(bf16)
        dgb_ref[...] = (dm * yb_ref[...] * sb * (1.0 - sb)).astype(bf16)

    return pl.pallas_call(
        body, grid=grid, in_specs=[_mix_tile, _mix_tile, _mix_tile, _mix_ga, _mix_gb],
        out_specs=[_mix_tile] * 4, out_shape=[SDS((rows, D), bf16)] * 4,
        name="mix_bwd", compiler_params=_cp(("arbitrary", "arbitrary")),
    )(dmixed, y_a, y_b, z, z)


def _final_ln(out32, h32, tgt, ln_g, ln_b):
    rows = out32.shape[0]

    def body(o_ref, h_ref, t_ref, g_ref, b_ref, du_ref, dub_ref, st_ref):
        i = pl.program_id(0)
        g = g_ref[...]
        y, xhat, rstd = _ln_rows(ALPHA * h_ref[...] + o_ref[...], g, b_ref[...])
        e = jnp.where(i > 0, y - t_ref[0], 0.0)
        dy = e * (1.0 / D)
        du = _ln_rows_bwd(dy, g, xhat, rstd)
        du_ref[...] = du
        dub_ref[...] = du.astype(bf16)
        st = jnp.concatenate([_colsum(dy * xhat), _colsum(dy), _colsum(du), _colsum(e * e) * (0.5 / D),
                              jnp.zeros((4, D), f32)], axis=0)

        @pl.when(i == 0)
        def _():
            st_ref[...] = st

        @pl.when(i > 0)
        def _():
            st_ref[...] += st

    row = pl.BlockSpec((BLK, D), lambda i: (i, 0))
    vec = pl.BlockSpec((1, D), lambda i: (0, 0))
    return pl.pallas_call(
        body, grid=(rows // BLK,),
        in_specs=[row, row, pl.BlockSpec((1, BLK, D), lambda i: (0, jnp.maximum(i - 1, 0), 0)), vec, vec],
        out_specs=[row, row, pl.BlockSpec((8, D), lambda i: (0, 0))],
        out_shape=[SDS((rows, D), f32), SDS((rows, D), bf16), SDS((8, D), f32)],
        name="final_ln", compiler_params=_cp(("arbitrary",)),
    )(out32, h32, tgt, ln_g, ln_b)


def _assemble_dz(dxr, dgr, dq, dkv, dga, dma, dmb):
    rows = dxr.shape[0]
    parts = [(dxr, D), (dgr, D), (dq, D), (dkv, 512), (dga, D), (dma, D), (dmb, D)]

    def body(*refs):
        o_ref = refs[-1]
        off = 0
        for r, (_, w) in zip(refs[:-1], parts):
            o_ref[:, off:off + w] = r[...]
            off += w

    return pl.pallas_call(
        body, grid=(rows // BLK,),
        in_specs=[pl.BlockSpec((BLK, w), lambda i: (i, 0)) for _, w in parts],
        out_specs=pl.BlockSpec((BLK, D_IN), lambda i: (i, 0)),
        out_shape=SDS((rows, D_IN), bf16), name="assemble_dz", compiler_params=_cp(("arbitrary",)),
    )(*[p for p, _ in parts])


def _step_branches(x, w_t, wrg, smallw, p):
    rows = x.shape[1] + BLK
    cos128, sin128 = _rope_tables(rows)
    sinks = p["sinks"].reshape(N_KV * GROUP)
    h32, hb = _ln_emb(x, smallw, p["ln_emb_g"], p["ln_emb_b"])
    z = _mm(hb, w_t, nt=True, bias=p["b_in"], name="mm_z")
    xc, hr, ya, ya_t = _rnn_fwd(z, smallw, p["conv_b"], wrg, p["b_ra"], p["b_ri"], p["lru_lambda"])
    q_r, k_r, v_b = _qkv_prep(z, cos128, sin128)
    o32, yb, yb_t, lse = _attn_fwd(q_r, k_r, v_b, z, sinks)
    return dict(cos128=cos128, sin128=sin128, sinks=sinks, h32=h32, hb=hb, z=z, xc=xc, hr=hr, ya=ya, ya_t=ya_t,
                q_r=q_r, k_r=k_r, v_b=v_b, o32=o32, yb=yb, yb_t=yb_t, lse=lse)


def _step_merge(s, tgt, w3, p):
    ya, yb, z = s["ya"], s["yb"], s["z"]
    y_a = _mm(ya, w3, sel=0, name="mm_ya")
    y_b = _mm(yb, w3, sel=1, name="mm_yb")
    mixed, mixed_t = _mix_fwd(y_a, y_b, z)
    out32 = _mm(mixed, w3, sel=2, bias=p["b_o"], name="mm_out")
    du32, dub, st_out = _final_ln(out32, s["h32"], tgt, p["ln_g"], p["ln_b"])

    g_wo = _mm(mixed_t, dub, out_dtype=bf16, name="mm_dwo")
    dmixed = _mm(dub, w3, sel=2, nt=True, name="mm_dmixed")
    dya_b, dyb_b, dma, dmb = _mix_bwd(dmixed, y_a, y_b, z)
    g_wrnn = _mm(s["ya_t"], dya_b, out_dtype=bf16, name="mm_dwrnn")
    g_wattn = _mm(s["yb_t"], dyb_b, out_dtype=bf16, name="mm_dwattn")
    dya = _mm(dya_b, w3, sel=0, nt=True, name="mm_dya")
    dyb = _mm(dyb_b, w3, sel=1, nt=True, name="mm_dyb")
    return dict(du32=du32, st_out=st_out, dma=dma, dmb=dmb, dya=dya, dyb=dyb, g_wo=g_wo, g_wrnn=g_wrnn,
                g_wattn=g_wattn)


def _step_backward(s, t, wrg, smallw, p, conv_b):
    z = s["z"]
    dxr, dgr, g_wrg, vec_rnn = _rnn_bwd(t["dya"], s["hr"], s["xc"], z, smallw, conv_b, wrg, p["b_ra"], p["b_ri"],
                                        p["lru_lambda"])
    dq_r, dga, dk, dv, dkm, dvm, dsr = _attn_bwd(t["dyb"], s["o32"], s["lse"], s["q_r"], s["k_r"], s["v_b"], z,
                                                 s["sinks"])
    dq, dkv = _qkv_finish(dq_r, dk, dv, dkm, dvm, s["cos128"], s["sin128"])
    dz = _assemble_dz(dxr, dgr, dq, dkv, dga, t["dma"], t["dmb"])
    return dict(vec_rnn=vec_rnn, dsr=dsr, g_wrg=g_wrg, dz=dz)


def _step_input_grad(dz, w_t, after, du32, x, smallw, p):
    dh = _mm_dh(dz, w_t, after)
    grad_x, dmeta, st_emb = _ln_emb_bwd(dh, du32, x, smallw, p["ln_emb_g"])
    return dict(grad_x=grad_x, dmeta=dmeta, st_emb=st_emb)


_ANY = pl.BlockSpec(memory_space=pl.ANY)
_VMEM = pl.BlockSpec(memory_space=pltpu.VMEM)


def _place():
    x, y, c = lax.axis_index("x"), lax.axis_index("y"), lax.axis_index("c")
    return x, y, c


def _dev(px, py, pc):
    return 4 * px + 2 * py + pc


def _tile_rows(r):
    return max(t for t in range(16, 321, 16) if r % t == 0) if r > 320 else r


def _cast_w_in(w_in_t):
    tm = _tile_rows(SHARD_IN)

    def body(i_ref, o_ref):
        o_ref[...] = i_ref[...].astype(bf16)

    return pl.pallas_call(
        body, grid=(SHARD_IN // tm,),
        in_specs=[pl.BlockSpec((tm, D), lambda i: (i, 0))],
        out_specs=pl.BlockSpec((tm, D), lambda i: (i, 0)),
        out_shape=SDS((SHARD_IN, D), bf16), name="cast_w_in", compiler_params=_cp(("arbitrary",)),
    )(w_in_t)


def _cast_small(w_rnn_out, w_attn_out, w_o, w_ra, w_ri, meta, conv_w):
    def body(a_ref, b_ref, c_ref, ra_ref, ri_ref, m_ref, cw_ref, w3_ref, wrg_ref, sw_ref):
        w3_ref[0] = a_ref[0].astype(bf16)
        w3_ref[1] = b_ref[0].astype(bf16)
        w3_ref[2] = c_ref[0].astype(bf16)
        wrg_ref[0] = ra_ref[0].astype(bf16)
        wrg_ref[1] = ri_ref[0].astype(bf16)
        sw_ref[...] = jnp.concatenate([m_ref[...], cw_ref[0], jnp.zeros((4, 256), f32)], axis=0)

    return pl.pallas_call(
        body,
        out_shape=[SDS((3, 256, D), bf16), SDS((2, N_RNN_BLOCKS, 32, RNN_BLOCK), bf16), SDS((24, 256), f32)],
        name="cast_small", compiler_params=_cp(None),
    )(w_rnn_out, w_attn_out, w_o, w_ra, w_ri, meta, conv_w)


def _all_gather(shards, later):
    n = len(shards)
    nl = len(later)

    def body(*refs):
        ins, outs = refs[:n], refs[n + nl:2 * n + nl]
        send_sems, recv_sems, local_sems = refs[2 * (n + nl):]
        x, y, c = _place()
        me, sibling = (x, y, c), (x, y, 1 - c)
        chips = [(1 - x, y), (x, 1 - y), (1 - x, 1 - y)]

        def copy(a, k, block, to, src=None):
            dst = outs[a].at[_dev(*block)]
            return pltpu.make_async_remote_copy(
                src_ref=dst if src is None else src, dst_ref=dst,
                send_sem=send_sems.at[a * 7 + k], recv_sem=recv_sems.at[a * 7 + k],
                device_id=to, device_id_type=MESH)

        all_ins, all_outs = refs[:n + nl], refs[n + nl:2 * (n + nl)]
        mine = [pltpu.make_async_copy(all_ins[a], all_outs[a].at[_dev(*me)], local_sems.at[a]) for a in range(n + nl)]
        for cp in mine:
            cp.start()
        first = []
        for a in range(n):
            first.append(copy(a, 0, me, sibling, src=ins[a]))
            first += [copy(a, 1 + j, me, (*chip, c), src=ins[a]) for j, chip in enumerate(chips)]
        for cp in first:
            cp.start()
        passed = []
        for a in range(n):
            for j, chip in enumerate(chips):
                copy(a, 1 + j, (*chip, c), me).wait_recv()
                cp = copy(a, 4 + j, (*chip, c), sibling)
                cp.start()
                passed.append(cp)
        for a in range(n):
            copy(a, 0, sibling, me).wait_recv()
            for j, chip in enumerate(chips):
                copy(a, 4 + j, (*chip, 1 - c), me).wait_recv()
        for cp in first + passed:
            cp.wait_send()
        for cp in mine:
            cp.wait()

    return pl.pallas_call(
        body, in_specs=[_ANY] * (n + nl), out_specs=[_ANY] * (n + nl),
        out_shape=[SDS((N_DEV, *s.shape), s.dtype) for s in (*shards, *later)],
        scratch_shapes=[pltpu.SemaphoreType.DMA((7 * n,)), pltpu.SemaphoreType.DMA((7 * n,)),
                        pltpu.SemaphoreType.DMA((n + nl,))],
        name="all_gather_weights",
    )(*shards, *later)


_HBM = pl.BlockSpec(memory_space=pltpu.HBM)
_SEM = pl.BlockSpec(memory_space=pltpu.SEMAPHORE)
_PEER_FLIPS = [(f // 4, (f // 2) % 2, f % 2) for f in range(1, N_DEV)]


def _remote(src, dst, send_sems, recv_sems, k, to):
    return pltpu.make_async_remote_copy(src_ref=src, dst_ref=dst, send_sem=send_sems.at[k], recv_sem=recv_sems.at[k],
                                        device_id=to, device_id_type=MESH)


def _copies_direct(same_src):
    def make(srcs, lands, send_sems, recv_sems):
        x, y, c = _place()
        me = _dev(x, y, c)
        out = []
        for a in range(len(srcs)):
            for k, (fx, fy, fc) in enumerate(_PEER_FLIPS):
                peer = ((x + fx) % 2, (y + fy) % 2, (c + fc) % 2)
                src = srcs[a] if same_src else srcs[a].at[_dev(*peer)]
                out.append(_remote(src, lands[a].at[me], send_sems, recv_sems, 7 * a + k, peer))
        return out
    return make


def _copies_siblings(srcs, lands, send_sems, recv_sems):
    x, y, c = _place()
    return [_remote(srcs[a].at[2 * q + (1 - c)], lands[a].at[q], send_sems, recv_sems, 4 * a + q, (x, y, 1 - c))
            for a in range(len(srcs)) for q in range(4)]


def _copies_chips(srcs, lands, send_sems, recv_sems):
    x, y, c = _place()
    chips = [(1 - x, y), (x, 1 - y), (1 - x, 1 - y)]
    return [_remote(srcs[a].at[2 * qx + qy], lands[a].at[j], send_sems, recv_sems, 3 * a + j, (qx, qy, c))
            for a in range(len(srcs)) for j, (qx, qy) in enumerate(chips)]


def _split_start(make, per_array, srcs, lands, dep, name):
    n = len(srcs)

    def body(*refs):
        send_sems, recv_sems, token = refs[2 * n + 1], refs[2 * n + 2], refs[-1]
        for cp in make(refs[:n], refs[n:2 * n], send_sems, recv_sems):
            cp.start()
        token[...] = jnp.zeros_like(token)

    hbm = lambda t: pltpu.with_memory_space_constraint(t, pltpu.HBM)
    res = pl.pallas_call(
        body, name=name,
        out_shape=(pltpu.SemaphoreType.DMA((per_array * n,)), pltpu.SemaphoreType.DMA((per_array * n,)),
                   *[pltpu.HBM(t.shape, t.dtype) for t in (*srcs, *lands)], SDS((8, 128), f32)),
        in_specs=[_HBM] * (2 * n) + [_ANY], out_specs=(_SEM, _SEM, *([_HBM] * (2 * n)), _VMEM),
        input_output_aliases={i: 2 + i for i in range(2 * n)},
        compiler_params=pltpu.CompilerParams(has_side_effects=pltpu.SideEffectType.DATAFLOW_SIDE_EFFECTING),
    )(*[hbm(t) for t in (*srcs, *lands)], dep)
    return res[0], res[1], list(res[2:2 + n]), list(res[2 + n:2 + 2 * n]), res[-1]


def _split_wait(make, send_sems, recv_sems, srcs, lands, after, name):
    n = len(srcs)

    def body(*refs):
        for cp in make(refs[:n], refs[n:2 * n], refs[2 * n], refs[2 * n + 1]):
            cp.wait_send()
            cp.wait_recv()

    res = pl.pallas_call(
        body, name=name,
        out_shape=tuple(pltpu.HBM(t.shape, t.dtype) for t in (*srcs, *lands)),
        in_specs=[_HBM] * (2 * n) + [_SEM, _SEM, _ANY], out_specs=tuple([_HBM] * (2 * n)),
        input_output_aliases={i: i for i in range(2 * n)},
        compiler_params=pltpu.CompilerParams(has_side_effects=pltpu.SideEffectType.DATAFLOW_SIDE_EFFECTING),
    )(*srcs, *lands, send_sems, recv_sems, after)
    return list(res[:n]), list(res[n:])


def _adamw_direct(g, land, me_idx, w, m, v, name):
    r, wd = w.shape
    tr = min(r, 256)

    def body(me_ref, *refs):
        g_ref, peers = refs[0], refs[1:N_DEV]
        w_ref, m_ref, v_ref, g_out, d_out, m_out, v_out = refs[N_DEV:]
        gs = g_ref[...].astype(f32)
        for p_ref in peers:
            gs = gs + p_ref[...].astype(f32)
        d, mn, vn = _adamw(w_ref[...], gs, m_ref[...], v_ref[...])
        g_out[...] = gs
        d_out[...] = d
        m_out[...] = mn
        v_out[...] = vn

    tile = pl.BlockSpec((tr, wd), lambda i, me_ref: (i, 0))
    slot = lambda k: pl.BlockSpec((None, tr, wd), lambda i, me_ref: ((me_ref[0] + k) % N_DEV, i, 0))
    return pl.pallas_call(
        body,
        grid_spec=pltpu.PrefetchScalarGridSpec(
            num_scalar_prefetch=1, grid=(r // tr,),
            in_specs=[slot(0)] + [slot(k) for k in range(1, N_DEV)] + [tile, tile, tile],
            out_specs=[tile] * 4),
        out_shape=[SDS((r, wd), f32)] * 4, name=name, compiler_params=_cp(("arbitrary",), 48),
    )(me_idx, g, *([land] * (N_DEV - 1)), w, m, v)


def _pair_sum(g, r1, c_idx, name):
    _, r, w = g.shape
    tr = _tile_rows(r)

    def body(c_ref, g_ref, r_ref, o_ref):
        o_ref[...] = (g_ref[...].astype(f32) + r_ref[...].astype(f32)).astype(bf16)

    return pl.pallas_call(
        body,
        grid_spec=pltpu.PrefetchScalarGridSpec(
            num_scalar_prefetch=1, grid=(4, r // tr),
            in_specs=[pl.BlockSpec((None, tr, w), lambda q, i, c_ref: (2 * q + c_ref[0], i, 0)),
                      pl.BlockSpec((None, tr, w), lambda q, i, c_ref: (q, i, 0))],
            out_specs=pl.BlockSpec((None, tr, w), lambda q, i, c_ref: (q, i, 0))),
        out_shape=SDS((4, r, w), bf16), name=name, compiler_params=_cp(("arbitrary", "arbitrary")),
    )(c_idx, g, r1)


def _adamw(w, g, m, v):
    m = ADAM_B1 * m + (1.0 - ADAM_B1) * g
    v = ADAM_B2 * v + (1.0 - ADAM_B2) * (g * g)
    m_hat = m / (1.0 - ADAM_B1 ** ADAM_STEP)
    v_hat = v / (1.0 - ADAM_B2 ** ADAM_STEP)
    delta = -ADAM_LR * (m_hat / (jnp.sqrt(v_hat) + ADAM_EPS) + ADAM_WD * w)
    return delta, m, v


def _adamw_big(pieces, q_idx, w, m, v, name, row_off=0):
    r, wd = w.shape
    tr = _tile_rows(r)
    np_ = len(pieces)
    wp = wd // np_

    def body(q_ref, *refs):
        w_ref, m_ref, v_ref, g_out, d_out, m_out, v_out = refs[2 * np_:]
        for k in range(np_):
            @pl.when(pl.program_id(1) == k)
            def _():
                p_ref, r_ref = refs[2 * k], refs[2 * k + 1]
                g = p_ref[...].astype(f32)
                for j in range(3):
                    g = g + r_ref[j].astype(f32)
                d, mn, vn = _adamw(w_ref[...], g, m_ref[...], v_ref[...])
                g_out[...] = g
                d_out[...] = d
                m_out[...] = mn
                v_out[...] = vn

    tile = pl.BlockSpec((tr, wp), lambda i, k, q_ref: (i, k))
    in_specs, args = [], []
    for part, r2 in pieces:
        in_specs += [pl.BlockSpec((None, tr, wp), lambda i, k, q_ref: (q_ref[0], row_off + i, 0)),
                     pl.BlockSpec((3, tr, wp), lambda i, k, q_ref: (0, row_off + i, 0))]
        args += [part, r2]
    return pl.pallas_call(
        body,
        grid_spec=pltpu.PrefetchScalarGridSpec(
            num_scalar_prefetch=1, grid=(r // tr, np_), in_specs=in_specs + [tile, tile, tile],
            out_specs=[tile] * 4),
        out_shape=[SDS((r, wd), f32)] * 4, name=name, compiler_params=_cp(("arbitrary", "arbitrary"), 48),
    )(q_idx, *args, w, m, v)


_SMALL_ROWS = 24


def _pack_small(st_emb, vec_rnn, st_out, dsr, db_in, dmeta):
    def body(se_ref, vr_ref, so_ref, dsr_ref, db_ref, dm_ref, sm_ref, sm2_ref):
        sm_ref[...] = jnp.zeros_like(sm_ref)
        sm2_ref[...] = jnp.zeros_like(sm2_ref)
        sm_ref[0:2, :] = se_ref[0:2, :]
        sm_ref[2:3, :] = vr_ref[3:4, :]
        sm_ref[3:6, :] = vr_ref[0:3, :]
        sm_ref[6:7, :] = so_ref[2:3, :]
        sm_ref[7:9, :] = so_ref[0:2, :]
        for h in range(N_KV):
            sm_ref[9:10, h * GROUP:(h + 1) * GROUP] = _colsum(dsr_ref[h])
        for j in range(6):
            sm_ref[16 + j:17 + j, :] = db_ref[0:1, j * D:(j + 1) * D]
        sm_ref[22:23, 0:D_IN - 6 * D] = db_ref[0:1, 6 * D:D_IN]
        for s in range(N_DEV):
            sm2_ref[s, 0:N_META, :] = dm_ref[:, s * 256:(s + 1) * 256]
            sm2_ref[s, N_META:N_META + CONV_WIDTH, :] = vr_ref[4:8, s * 256:(s + 1) * 256]

    return pl.pallas_call(
        body, out_shape=[SDS((_SMALL_ROWS, D), f32), SDS((N_DEV, 24, 256), f32)],
        name="pack_small", compiler_params=_cp(None),
    )(st_emb, vec_rnn, st_out, dsr, db_in, dmeta)


def _small_allreduce(sm, sm2):
    def body(sm_ref, sm2_ref, o_ref, o2_ref, buf, buf2, send_sems, recv_sems):
        x, y, c = _place()
        me = _dev(x, y, c)
        copies = []
        for f in range(1, N_DEV):
            fx, fy, fc = f // 4, (f // 2) % 2, f % 2
            peer = ((x + fx) % 2, (y + fy) % 2, (c + fc) % 2)
            for t, (src, dst) in enumerate(((sm_ref, buf), (sm2_ref, buf2))):
                k = 2 * (f - 1) + t
                copies.append(pltpu.make_async_remote_copy(
                    src_ref=src, dst_ref=dst.at[me], send_sem=send_sems.at[k], recv_sem=recv_sems.at[k],
                    device_id=peer, device_id_type=MESH))
        for cp in copies:
            cp.start()
        buf[me] = sm_ref[...]
        buf2[me] = sm2_ref[...]
        for cp in copies:
            cp.wait()
        acc, acc2 = buf[0], buf2[0]
        for e in range(1, N_DEV):
            acc, acc2 = acc + buf[e], acc2 + buf2[e]
        o_ref[...] = acc
        o2_ref[...] = acc2

    return pl.pallas_call(
        body, in_specs=[_VMEM, _VMEM], out_specs=[_VMEM, _VMEM],
        out_shape=[SDS(sm.shape, f32), SDS(sm2.shape, f32)],
        scratch_shapes=[pltpu.VMEM((N_DEV, *sm.shape), f32), pltpu.VMEM((N_DEV, *sm2.shape), f32),
                        pltpu.SemaphoreType.DMA((14,)), pltpu.SemaphoreType.DMA((14,))],
        name="small_allreduce",
    )(sm, sm2)


_SMALL_ROW_OF = {"ln_emb_g": 0, "ln_emb_b": 1, "conv_b": 2, "b_ra": 3, "b_ri": 4, "lru_lambda": 5, "b_o": 6,
                 "ln_g": 7, "ln_b": 8}
_SMALL_NAMES = ["ln_emb_g", "ln_emb_b", "conv_b", "b_ra", "b_ri", "lru_lambda", "b_o", "ln_g", "ln_b",
                "sinks", "b_in", "meta_tokens", "conv_w"]


def _small_update(sm, sm2_mine, wmv):
    def grad_of(name, sm_ref, s2_ref):
        if name in _SMALL_ROW_OF:
            r = _SMALL_ROW_OF[name]
            return sm_ref[r:r + 1, :]
        if name == "sinks":
            return sm_ref[9:10, 0:N_KV * GROUP]
        if name == "b_in":
            return jnp.concatenate([sm_ref[16 + j:17 + j, :] for j in range(7)], axis=1)[:, :D_IN]
        if name == "meta_tokens":
            return s2_ref[0:N_META, :]
        return s2_ref[N_META:N_META + CONV_WIDTH, :]

    def body(*refs):
        sm_ref, s2_ref = refs[0], refs[1]
        ins = refs[2:2 + 3 * len(_SMALL_NAMES)]
        outs = refs[2 + 3 * len(_SMALL_NAMES):]
        for i, name in enumerate(_SMALL_NAMES):
            w_ref, m_ref, v_ref = ins[3 * i:3 * i + 3]
            g = grad_of(name, sm_ref, s2_ref)
            d, mn, vn = _adamw(w_ref[...], g, m_ref[...], v_ref[...])
            outs[4 * i][...] = g
            outs[4 * i + 1][...] = d
            outs[4 * i + 2][...] = mn
            outs[4 * i + 3][...] = vn

    args, out_shape = [sm, sm2_mine], []
    for name in _SMALL_NAMES:
        args += list(wmv[name])
        out_shape += [SDS(wmv[name][0].shape, f32)] * 4
    res = pl.pallas_call(body, out_shape=out_shape, name="small_update", compiler_params=_cp(None))(*args)
    return {name: tuple(res[4 * i:4 * i + 4]) for i, name in enumerate(_SMALL_NAMES)}


_WEIGHTS = ["meta_tokens", "ln_emb_g", "ln_emb_b", "w_in", "b_in", "conv_w", "conv_b", "w_ra", "b_ra", "w_ri",
            "b_ri", "lru_lambda", "sinks", "w_rnn_out", "w_attn_out", "w_o", "b_o", "ln_g", "ln_b"]
_SMALL_2D = {"meta_tokens": (N_META, 256), "conv_w": (CONV_WIDTH, 256), "b_in": (1, D_IN), "sinks": (1, N_KV * GROUP)}


def kernel(x, meta_tokens, ln_emb_g, ln_emb_b, w_in, b_in, conv_w, conv_b, w_ra, b_ra, w_ri, b_ri, lru_lambda, sinks, w_rnn_out, w_attn_out, w_o, b_o, ln_g, ln_b, loss_target, m_meta_tokens, m_ln_emb_g, m_ln_emb_b, m_w_in, m_b_in, m_conv_w, m_conv_b, m_w_ra, m_b_ra, m_w_ri, m_b_ri, m_lru_lambda, m_sinks, m_w_rnn_out, m_w_attn_out, m_w_o, m_b_o, m_ln_g, m_ln_b, v_meta_tokens, v_ln_emb_g, v_ln_emb_b, v_w_in, v_b_in, v_conv_w, v_conv_b, v_w_ra, v_b_ra, v_w_ri, v_b_ri, v_lru_lambda, v_sinks, v_w_rnn_out, v_w_attn_out, v_w_o, v_b_o, v_ln_g, v_ln_b):
    w = dict(meta_tokens=meta_tokens, ln_emb_g=ln_emb_g, ln_emb_b=ln_emb_b, w_in=w_in, b_in=b_in, conv_w=conv_w,
             conv_b=conv_b, w_ra=w_ra, b_ra=b_ra, w_ri=w_ri, b_ri=b_ri, lru_lambda=lru_lambda, sinks=sinks,
             w_rnn_out=w_rnn_out, w_attn_out=w_attn_out, w_o=w_o, b_o=b_o, ln_g=ln_g, ln_b=ln_b)
    m = dict(meta_tokens=m_meta_tokens, ln_emb_g=m_ln_emb_g, ln_emb_b=m_ln_emb_b, w_in=m_w_in, b_in=m_b_in,
             conv_w=m_conv_w, conv_b=m_conv_b, w_ra=m_w_ra, b_ra=m_b_ra, w_ri=m_w_ri, b_ri=m_b_ri,
             lru_lambda=m_lru_lambda, sinks=m_sinks, w_rnn_out=m_w_rnn_out, w_attn_out=m_w_attn_out, w_o=m_w_o,
             b_o=m_b_o, ln_g=m_ln_g, ln_b=m_ln_b)
    v = dict(meta_tokens=v_meta_tokens, ln_emb_g=v_ln_emb_g, ln_emb_b=v_ln_emb_b, w_in=v_w_in, b_in=v_b_in,
             conv_w=v_conv_w, conv_b=v_conv_b, w_ra=v_w_ra, b_ra=v_b_ra, w_ri=v_w_ri, b_ri=v_b_ri,
             lru_lambda=v_lru_lambda, sinks=v_sinks, w_rnn_out=v_w_rnn_out, w_attn_out=v_w_attn_out, w_o=v_w_o,
             b_o=v_b_o, ln_g=v_ln_g, ln_b=v_ln_b)
    px, py, pc = _place()
    as_idx = lambda t: jnp.reshape(t, (1,)).astype(jnp.int32)
    c_idx, q_idx, me_idx = as_idx(pc), as_idx(2 * px + py), as_idx(_dev(px, py, pc))

    w3_s, wrg_s, small_s = _cast_small(w_rnn_out, w_attn_out, w_o, w_ra, w_ri, meta_tokens, conv_w)
    w_in_t = lambda a: jnp.swapaxes(a, 1, 2).reshape(SHARD_IN, D)
    wg, wrg, smallw, w3_land = _all_gather([_cast_w_in(w_in_t(w_in)), wrg_s, small_s], [w3_s])
    w3_pending = _split_start(_copies_direct(True), 7, [w3_s], [w3_land], smallw, "gather_w3_start")
    w_full = wg.reshape(D_IN, D)

    vec = lambda name: w[name].reshape(1, -1)
    p = {k: vec(k) for k in ("ln_emb_g", "ln_emb_b", "b_in", "conv_b", "b_ra", "b_ri", "lru_lambda", "sinks",
                             "b_o", "ln_g", "ln_b")}
    p["b_in"] = p["b_in"] + w3_pending[4][0:1, 0:1]
    s = _step_branches(x, w_full, wrg, smallw, p)
    w3 = _split_wait(_copies_direct(True), *w3_pending[:4], s["lse"], "gather_w3_wait")[1][0]
    t = _step_merge(s, loss_target, w3, p)
    loss = lax.psum(jnp.sum(t["st_out"][3]), ("x", "y", "c"))

    big = {}
    two_d = lambda name: (w[name].shape[-2], w[name].shape[-1])
    proj = ("w_o", "w_rnn_out", "w_attn_out")
    g_proj = [t[k].reshape(N_DEV, 256, D) for k in ("g_wo", "g_wrnn", "g_wattn")]
    g_pending = _split_start(_copies_direct(False), 7, g_proj, [lax.empty((N_DEV, 256, D), bf16) for _ in proj],
                             p["b_o"], "reduce_proj_start")
    u = _step_backward(s, t, wrg, smallw, p, p["conv_b"] + g_pending[4][0:1, 0:1])

    def siblings_start(gs, dep, tag):
        return _split_start(_copies_siblings, 4, gs, [lax.empty((4, *g.shape[1:]), bf16) for g in gs], dep,
                            "reduce_siblings_start_" + tag)

    def chips_start(gs, r1, dep, tag):
        parts = [_pair_sum(g, r, c_idx, "pair_sum_%s%d" % (tag, i)) for i, (g, r) in enumerate(zip(gs, r1))]
        return _split_start(_copies_chips, 3, parts, [lax.empty((3, *q.shape[1:]), bf16) for q in parts], dep,
                            "reduce_chips_start_" + tag)

    g_a, db_in = _mm_dwin(s["hb"], u["dz"], 0, p["b_o"])
    shards = lambda g: g.reshape(N_DEV, SHARD_IN, W_IN_HALF)
    sib_a = siblings_start([shards(g_a), u["g_wrg"].reshape(N_DEV, 2 * RNN_BLOCK, RNN_BLOCK)], db_in, "a")
    g_b, = _mm_dwin(s["hb"], u["dz"], 1, sib_a[4])
    sib_b = siblings_start([shards(g_b)], db_in, "b")
    chp_a = chips_start(*_split_wait(_copies_siblings, *sib_a[:4], sib_b[4], "reduce_siblings_wait_a"), db_in, "a")
    g_proj, g_land = _split_wait(_copies_direct(False), *g_pending[:4], chp_a[4], "reduce_proj_wait")
    for i, name in enumerate(proj):
        res = _adamw_direct(g_proj[i], g_land[i], me_idx, w[name].reshape(two_d(name)), m[name].reshape(two_d(name)),
                            v[name].reshape(two_d(name)), "adamw_" + name)
        big[name] = tuple(r.reshape(w[name].shape) for r in res)
    chp_b = chips_start(*_split_wait(_copies_siblings, *sib_b[:4], big["w_attn_out"][3], "reduce_siblings_wait_b"),
                        db_in, "b")
    u.update(_step_input_grad(u["dz"], w_full, chp_b[4], t["du32"], x, smallw, p))
    u["db_in"] = db_in

    loc = {**t, **u}
    sm, sm2 = _pack_small(loc["st_emb"], loc["vec_rnn"], loc["st_out"], loc["dsr"], loc["db_in"], loc["dmeta"])
    sm, sm2 = _small_allreduce(sm, sm2)
    sm2_mine = lax.dynamic_index_in_dim(sm2, _dev(px, py, pc), 0, keepdims=False)
    two = lambda name, t: t.reshape(_SMALL_2D.get(name, (1, D)))
    small = _small_update(sm, sm2_mine, {k: (two(k, w[k]), two(k, m[k]), two(k, v[k])) for k in _SMALL_NAMES})

    parts_a, r2_a = _split_wait(_copies_chips, *chp_a[:4], small["b_in"][3], "reduce_chips_wait_a")
    parts_b, r2_b = _split_wait(_copies_chips, *chp_b[:4], small["b_in"][2], "reduce_chips_wait_b")
    res = _adamw_big([(parts_a[0], r2_a[0]), (parts_b[0], r2_b[0])], q_idx, w_in_t(w["w_in"]), w_in_t(m["w_in"]),
                     w_in_t(v["w_in"]), "adamw_w_in")
    big["w_in"] = tuple(jnp.swapaxes(r.reshape(1, SHARD_IN, D), 1, 2) for r in res)
    for i, name in enumerate(("w_ra", "w_ri")):
        sq = (RNN_BLOCK, RNN_BLOCK)
        res = _adamw_big([(parts_a[1], r2_a[1])], q_idx, w[name].reshape(sq), m[name].reshape(sq), v[name].reshape(sq),
                         "adamw_" + name, row_off=i)
        big[name] = tuple(r.reshape(w[name].shape) for r in res)
    res = dict(big)
    for k in _SMALL_NAMES:
        res[k] = tuple(t.reshape(w[k].shape) for t in small[k])

    outs = [loss, loc["grad_x"]]
    for j in range(4):
        outs += [res[k][j] for k in _WEIGHTS]
    return tuple(outs)
```

```python
import functools

import jax
import jax.numpy as jnp
from jax import lax
from jax.experimental import pallas as pl
from jax.experimental.pallas import tpu as pltpu

f32, bf16 = jnp.float32, jnp.bfloat16
SDS = jax.ShapeDtypeStruct

N_DEV = 8
D = 2048
N_META = 16
BLK = 128
ROW0 = BLK - N_META
N_RNN_BLOCKS = 8
RNN_BLOCK = D // N_RNN_BLOCKS
CONV_WIDTH = 4
LRU_C = 8.0
HEAD_DIM = 64
N_KV = 4
GROUP = 8
HALF = HEAD_DIM // 2
ROPE_THETA = 10000.0
NEG_INF = -1e30
LN_EPS = 1e-5
ALPHA = 2.0 ** 0.25
D_IN = 12800
SHARD_IN = D_IN // N_DEV
OFF_GR, OFF_Q, OFF_K, OFF_V, OFF_GA, OFF_G = 2048, 4096, 6144, 6400, 6656, 8704
ADAM_LR, ADAM_B1, ADAM_B2, ADAM_EPS, ADAM_WD, ADAM_STEP = 1e-3, 0.9, 0.999, 1e-8, 0.01, 10
VMEM_LIMIT_MB = 56
MESH = pl.DeviceIdType.MESH


def _cp(sem=None, vmem_mb=40):
    return pltpu.CompilerParams(dimension_semantics=sem, vmem_limit_bytes=vmem_mb * 2 ** 20)


def _row_chunk(m):
    best = 16
    for c in range(16, 641, 16):
        if m % c == 0:
            best = c
    return best


def _sigmoid(x):
    return 1.0 / (1.0 + jnp.exp(-x))


def _silu_and_grad(x):
    s = _sigmoid(x)
    return x * s, s * (1.0 + x * (1.0 - s))


def _log_sigmoid(x):
    return jnp.minimum(x, 0.0) - jnp.log1p(jnp.exp(-jnp.abs(x)))


def _ln_rows(v, g, b):
    mu = jnp.mean(v, axis=-1, keepdims=True)
    c = v - mu
    var = jnp.mean(c * c, axis=-1, keepdims=True)
    rstd = lax.rsqrt(var + LN_EPS)
    xhat = c * rstd
    return xhat * g + b, xhat, rstd


def _ln_rows_bwd(dy, g, xhat, rstd):
    dxh = dy * g
    m1 = jnp.mean(dxh, axis=-1, keepdims=True)
    m2 = jnp.mean(dxh * xhat, axis=-1, keepdims=True)
    return rstd * (dxh - m1 - xhat * m2)


def _colsum(v):
    return jnp.sum(v, axis=0, keepdims=True)


def _dot(a, b):
    return jnp.dot(a, b, preferred_element_type=f32)


def _dot_nt(a, b):
    return lax.dot_general(a, b, (((1,), (1,)), ((), ())), preferred_element_type=f32)


def _dot_tn(a, b):
    return lax.dot_general(a, b, (((0,), (0,)), ((), ())), preferred_element_type=f32)


def _meta_full(sw_ref):
    return jnp.concatenate([sw_ref[s, 0:N_META, :] for s in range(N_DEV)], axis=1)


def _ln_emb(x, smallw, g_e, b_e):
    seq = x.shape[1]
    rows = seq + BLK
    nb = rows // BLK

    def body(x_ref, sw_ref, g_ref, b_ref, h32_ref, hb_ref):
        i = pl.program_id(0)
        g, b = g_ref[...], b_ref[...]

        def emit(blk):
            h32_ref[...] = blk
            hb_ref[...] = blk.astype(bf16)

        @pl.when(i == 0)
        def _():
            hm = _ln_rows(_meta_full(sw_ref), g, b)[0]
            emit(jnp.concatenate([jnp.zeros((ROW0, D), f32), hm], axis=0))

        @pl.when(i > 0)
        def _():
            emit(_ln_rows(x_ref[0], g, b)[0])

    return pl.pallas_call(
        body, grid=(nb,),
        in_specs=[pl.BlockSpec((1, BLK, D), lambda i: (0, jnp.maximum(i - 1, 0), 0)),
                  pl.BlockSpec((N_DEV, 24, 256), lambda i: (0, 0, 0)),
                  pl.BlockSpec((1, D), lambda i: (0, 0)),
                  pl.BlockSpec((1, D), lambda i: (0, 0))],
        out_specs=[pl.BlockSpec((BLK, D), lambda i: (i, 0)),
                   pl.BlockSpec((BLK, D), lambda i: (i, 0))],
        out_shape=[SDS((rows, D), f32), SDS((rows, D), bf16)],
        name="ln_emb", compiler_params=_cp(("arbitrary",)),
    )(x, smallw, g_e, b_e)


def _ln_emb_bwd(dh, du32, x, smallw, g_e):
    seq = x.shape[1]
    rows = seq + BLK
    nb = rows // BLK

    def body(dh_ref, du_ref, x_ref, sw_ref, g_ref, gx_ref, dmeta_ref, st_ref):
        i = pl.program_id(0)
        g = g_ref[...]
        dht = dh_ref[...] + ALPHA * du_ref[...]

        @pl.when(i == 0)
        def _():
            v = jnp.concatenate([jnp.zeros((ROW0, D), f32), _meta_full(sw_ref)], axis=0)
            valid = lax.broadcasted_iota(jnp.int32, (BLK, 1), 0) >= ROW0
            d = jnp.where(valid, dht, 0.0)
            _, xhat, rstd = _ln_rows(v, g, 0.0)
            dv = _ln_rows_bwd(d, g, xhat, rstd)
            dmeta_ref[...] = dv[ROW0:, :]
            st_ref[...] = jnp.concatenate([_colsum(d * xhat), _colsum(d), jnp.zeros((6, D), f32)], axis=0)

        @pl.when(i > 0)
        def _():
            _, xhat, rstd = _ln_rows(x_ref[0], g, 0.0)
            gx_ref[0] = _ln_rows_bwd(dht, g, xhat, rstd)
            st_ref[0:1, :] += _colsum(dht * xhat)
            st_ref[1:2, :] += _colsum(dht)

    return pl.pallas_call(
        body, grid=(nb,),
        in_specs=[pl.BlockSpec((BLK, D), lambda i: (i, 0)),
                  pl.BlockSpec((BLK, D), lambda i: (i, 0)),
                  pl.BlockSpec((1, BLK, D), lambda i: (0, jnp.maximum(i - 1, 0), 0)),
                  pl.BlockSpec((N_DEV, 24, 256), lambda i: (0, 0, 0)),
                  pl.BlockSpec((1, D), lambda i: (0, 0))],
        out_specs=[pl.BlockSpec((1, BLK, D), lambda i: (0, jnp.maximum(i - 1, 0), 0)),
                   pl.BlockSpec((N_META, D), lambda i: (0, 0)),
                   pl.BlockSpec((8, D), lambda i: (0, 0))],
        out_shape=[SDS((1, seq, D), f32), SDS((N_META, D), f32), SDS((8, D), f32)],
        name="ln_emb_bwd", compiler_params=_cp(("arbitrary",)),
    )(dh, du32, x, smallw, g_e)


def _mm(a, b, *, name, nt=False, sel=None, bias=None, out_dtype=f32, tn=512):
    m, k = a.shape
    cm = _row_chunk(m)
    stacked = sel is not None
    n = D if stacked else (b.shape[0] if nt else b.shape[1])
    am = m
    if stacked and nt:
        b_spec = pl.BlockSpec((tn // 256, None, 256, D), lambda j, i: (j, sel, 0, 0))
    elif stacked:
        b_spec = pl.BlockSpec((N_DEV, None, 256, tn), lambda j, i: (0, sel, 0, j))
    elif nt:
        b_spec = pl.BlockSpec((tn, k), lambda j, i: (j, 0))
    else:
        b_spec = pl.BlockSpec((k, tn), lambda j, i: (0, j))
    in_specs = [pl.BlockSpec((am, k), lambda j, i: (i, 0)), b_spec]
    args = [a, b]
    if bias is not None:
        in_specs.append(pl.BlockSpec((1, tn), lambda j, i: (0, j)))
        args.append(bias)

    def body(*refs):
        a_ref, b_ref, o_ref = refs[0], refs[1], refs[-1]
        bm = b_ref[...]
        if stacked:
            bm = bm.reshape((tn, D) if nt else (D, tn))
        for c in range(am // cm):
            acc = (_dot_nt if nt else _dot)(a_ref[c * cm:(c + 1) * cm, :], bm)
            if bias is not None:
                acc = acc + refs[2][...]
            o_ref[c * cm:(c + 1) * cm, :] = acc.astype(out_dtype)

    return pl.pallas_call(
        body, grid=(n // tn, m // am), in_specs=in_specs,
        out_specs=pl.BlockSpec((am, tn), lambda j, i: (i, j)),
        out_shape=SDS((m, n), out_dtype), name=name, compiler_params=_cp(("arbitrary", "arbitrary"), 48),
    )(*args)


def _mm_dh(dz, w_t, after):
    rows = dz.shape[0]
    tk, tn = 2560, 512
    cm = _row_chunk(rows)

    def body(a_ref, w_ref, after_ref, o_ref):
        kk = pl.program_id(1)
        for c in range(rows // cm):
            acc = _dot(a_ref[c * cm:(c + 1) * cm, :], w_ref[...])

            @pl.when(kk == 0)
            def _():
                o_ref[c * cm:(c + 1) * cm, :] = acc

            @pl.when(kk > 0)
            def _():
                o_ref[c * cm:(c + 1) * cm, :] += acc

    return pl.pallas_call(
        body, grid=(D // tn, D_IN // tk),
        in_specs=[pl.BlockSpec((rows, tk), lambda j, kk: (0, kk)),
                  pl.BlockSpec((tk, tn), lambda j, kk: (kk, j)),
                  pl.BlockSpec(memory_space=pl.ANY)],
        out_specs=pl.BlockSpec((rows, tn), lambda j, kk: (0, j)),
        out_shape=SDS((rows, D), f32), name="mm_dh", compiler_params=_cp(("arbitrary", "arbitrary"), 48),
    )(dz, w_t, after)


W_IN_HALF = D // 2


def _mm_dwin(hb, dz, half, after):
    rows = dz.shape[0]
    tc = 640
    with_db = half == 0

    def body(dz_ref, h_ref, after_ref, o_ref, *db_ref):
        o_ref[...] = _dot_tn(dz_ref[...], h_ref[...]).astype(bf16)
        if with_db:
            def step(i, s):
                blk = dz_ref[pl.ds(pl.multiple_of(i * BLK, BLK), BLK), :].astype(f32)
                return s + blk.reshape(BLK // 8, 8, tc).sum(axis=0)
            s = lax.fori_loop(0, rows // BLK, step, jnp.zeros((8, tc), f32))
            db_ref[0][...] = jnp.broadcast_to(_colsum(s), (8, tc))

    out_specs = [pl.BlockSpec((tc, W_IN_HALF), lambda j: (j, 0))]
    out_shape = [SDS((D_IN, W_IN_HALF), bf16)]
    if with_db:
        out_specs.append(pl.BlockSpec((8, tc), lambda j: (0, j)))
        out_shape.append(SDS((8, D_IN), f32))
    return pl.pallas_call(
        body, grid=(D_IN // tc,),
        in_specs=[pl.BlockSpec((rows, tc), lambda j: (0, j)),
                  pl.BlockSpec((rows, W_IN_HALF), lambda j: (0, half)),
                  pl.BlockSpec(memory_space=pl.ANY)],
        out_specs=out_specs, out_shape=out_shape,
        name="mm_dwin_%d" % half, compiler_params=_cp(("arbitrary",), 48),
    )(dz, hb, after)


SCAN_ROWS = 32


def _scan8(a, b, reverse):
    idx = lax.broadcasted_iota(jnp.int32, a.shape, 0)
    for s in (1, 2, 4):
        sh = 8 - s if reverse else s
        a_sh, b_sh = pltpu.roll(a, sh, 0), pltpu.roll(b, sh, 0)
        m = (idx < 8 - s) if reverse else (idx >= s)
        b = jnp.where(m, a * b_sh + b, b)
        a = jnp.where(m, a * a_sh, a)
    return a, b


def _shift_rows(prev8, cur, k):
    ext = jnp.concatenate([prev8, cur], axis=0)
    return pltpu.roll(ext, k, 0)[8:, :]


def _gates(xc, w_ra, b_ra, w_ri, b_ri, ls):
    xb = xc.astype(bf16)
    r = _sigmoid(_dot(xb, w_ra) + b_ra)
    ig = _sigmoid(_dot(xb, w_ri) + b_ri)
    la = LRU_C * r * ls
    a = jnp.exp(la)
    mult = jnp.sqrt(jnp.tanh(-la) * (1.0 + a * a))
    return xb, r, ig, a, mult


_RNN_IN_SPECS = lambda rows: [
    pl.BlockSpec((1, 24, 256), lambda n: (n, 0, 0)),
    pl.BlockSpec((1, RNN_BLOCK), lambda n: (0, n)),
    pl.BlockSpec((N_DEV, 2, None, 32, RNN_BLOCK), lambda n: (0, 0, n, 0, 0)),
    pl.BlockSpec((1, RNN_BLOCK), lambda n: (0, n)),
    pl.BlockSpec((1, RNN_BLOCK), lambda n: (0, n)),
    pl.BlockSpec((1, RNN_BLOCK), lambda n: (0, n)),
]


def _rnn_fwd(z, smallw, conv_b, wrg, b_ra, b_ri, lam):
    rows = z.shape[0]
    nb = rows // BLK
    col = lambda off: pl.BlockSpec((rows, RNN_BLOCK), lambda n: (0, off // RNN_BLOCK + n))

    def body(xr_ref, gr_ref, sw_ref, cb_ref, w_ref, bra_ref, bri_ref, lam_ref, xc_ref, hr_ref, ya_ref, yat_ref, a_s):
        cw = sw_ref[0, N_META:24, :]
        cb = cb_ref[...]
        w_ra = w_ref[:, 0].reshape(RNN_BLOCK, RNN_BLOCK)
        w_ri = w_ref[:, 1].reshape(RNN_BLOCK, RNN_BLOCK)
        b_ra_v, b_ri_v = bra_ref[...], bri_ref[...]
        ls = _log_sigmoid(lam_ref[...])
        rid = lax.broadcasted_iota(jnp.int32, (BLK, 1), 0)

        def blk_step(i, carry):
            r0 = pl.multiple_of(i * BLK, BLK)
            grow = rid + r0
            valid = grow >= ROW0
            cur = jnp.where(valid, xr_ref[pl.ds(r0, BLK), :], 0.0)
            prev8 = xr_ref[pl.ds(pl.multiple_of(jnp.maximum(r0 - 8, 0), 8), 8), :] * (i > 0).astype(f32)
            xc = cb + cw[0:1] * cur
            for k in range(1, CONV_WIDTH):
                xc = xc + cw[k:k + 1] * _shift_rows(prev8, cur, k)
            xc_ref[pl.ds(r0, BLK), :] = xc
            _, _, ig, a, mult = _gates(xc, w_ra, b_ra_v, w_ri, b_ri_v, ls)
            mult = jnp.where(grow == ROW0, 1.0, mult)
            a_s[pl.ds(r0, BLK), :] = a
            hr_ref[pl.ds(r0, BLK), :] = jnp.where(valid, mult * ig * xc, 0.0)
            return carry

        lax.fori_loop(0, nb, blk_step, 0)

        def scan_step(j, carry):
            r0 = pl.multiple_of(j * SCAN_ROWS, SCAN_ROWS)
            tiles = [_scan8(a_s[pl.ds(r0 + 8 * k, 8), :], hr_ref[pl.ds(r0 + 8 * k, 8), :], False)
                     for k in range(SCAN_ROWS // 8)]
            for k, (a, b) in enumerate(tiles):
                h = b + a * carry
                hr_ref[pl.ds(r0 + 8 * k, 8), :] = h
                carry = jnp.broadcast_to(h[7:8, :], (8, RNN_BLOCK))
            return carry

        lax.fori_loop(0, rows // SCAN_ROWS, scan_step, jnp.zeros((8, RNN_BLOCK), f32))

        def gate_step(i, carry):
            r0 = pl.multiple_of(i * BLK, BLK)
            ya_ref[pl.ds(r0, BLK), :] = (hr_ref[pl.ds(r0, BLK), :]
                                         * _silu_and_grad(gr_ref[pl.ds(r0, BLK), :])[0]).astype(bf16)
            return carry

        lax.fori_loop(0, nb, gate_step, 0)
        yat_ref[...] = ya_ref[...].astype(f32).T.astype(bf16)

    return pl.pallas_call(
        body, grid=(N_RNN_BLOCKS,),
        in_specs=[col(0), col(OFF_GR)] + _RNN_IN_SPECS(rows),
        out_specs=[pl.BlockSpec((rows, RNN_BLOCK), lambda n: (0, n))] * 3
                  + [pl.BlockSpec((RNN_BLOCK, rows), lambda n: (n, 0))],
        out_shape=[SDS((rows, D), f32), SDS((rows, D), f32), SDS((rows, D), bf16), SDS((D, rows), bf16)],
        scratch_shapes=[pltpu.VMEM((rows, RNN_BLOCK), f32)],
        name="rnn_fwd", compiler_params=_cp(("arbitrary",)),
    )(z, z, smallw, conv_b, wrg, b_ra, b_ri, lam)


def _rnn_bwd(dya, hr, xc, z, smallw, conv_b, wrg, b_ra, b_ri, lam):
    rows = z.shape[0]
    nb = rows // BLK
    col = lambda off: pl.BlockSpec((rows, RNN_BLOCK), lambda n: (0, off // RNN_BLOCK + n))
    blk = pl.BlockSpec((rows, RNN_BLOCK), lambda n: (0, n))

    def body(dya_ref, hr_ref, xc_ref, xr_ref, gr_ref, sw_ref, cb_ref, w_ref, bra_ref, bri_ref, lam_ref,
             dxr_ref, dgr_ref, dw_ref, vec_ref, a_s, lam_s, dxc_s, r_s, ig_s, mult_s, dw_s):
        cw = sw_ref[0, N_META:24, :]
        w_ra = w_ref[:, 0].reshape(RNN_BLOCK, RNN_BLOCK)
        w_ri = w_ref[:, 1].reshape(RNN_BLOCK, RNN_BLOCK)
        b_ra_v, b_ri_v = bra_ref[...], bri_ref[...]
        lam_v = lam_ref[...]
        ls = _log_sigmoid(lam_v)
        rid = lax.broadcasted_iota(jnp.int32, (BLK, 1), 0)
        zrow = jnp.zeros((1, RNN_BLOCK), f32)

        def p1(i, carry):
            r0 = pl.multiple_of(i * BLK, BLK)
            sl = pl.ds(r0, BLK)
            _, r, ig, a, mult = _gates(xc_ref[sl, :], w_ra, b_ra_v, w_ri, b_ri_v, ls)
            a_s[sl, :] = a
            r_s[sl, :] = r
            ig_s[sl, :] = ig
            mult_s[sl, :] = mult
            sg, dsg = _silu_and_grad(gr_ref[sl, :])
            d = dya_ref[sl, :]
            lam_s[sl, :] = d * sg
            dgr_ref[sl, :] = (d * hr_ref[sl, :] * dsg).astype(bf16)
            return carry

        lax.fori_loop(0, nb, p1, 0)

        def p2(jj, carry):
            r0 = pl.multiple_of((rows // SCAN_ROWS - 1 - jj) * SCAN_ROWS, SCAN_ROWS)
            idx = lax.broadcasted_iota(jnp.int32, (8, RNN_BLOCK), 0)
            tiles = []
            for k in range(SCAN_ROWS // 8):
                sl = pl.ds(r0 + 8 * k, 8)
                a, g = a_s[sl, :], lam_s[sl, :]
                tiles.append((g, *_scan8(a, a * g, True)))
            for k in reversed(range(SCAN_ROWS // 8)):
                g, ca, cb_ = tiles[k]
                mu = cb_ + ca * carry
                lam_s[pl.ds(r0 + 8 * k, 8), :] = g + jnp.where(idx < 7, pltpu.roll(mu, 7, 0), carry)
                carry = jnp.broadcast_to(mu[0:1, :], (8, RNN_BLOCK))
            return carry

        lax.fori_loop(0, rows // SCAN_ROWS, p2, jnp.zeros((8, RNN_BLOCK), f32))

        dw_s[...] = jnp.zeros_like(dw_s)

        def p3(i, carry):
            d_bra, d_bri, d_ls = carry
            r0 = pl.multiple_of(i * BLK, BLK)
            sl = pl.ds(r0, BLK)
            grow = rid + r0
            valid = grow >= ROW0
            first = grow == ROW0
            xcv = xc_ref[sl, :]
            xb = xcv.astype(bf16)
            r, ig, a = r_s[sl, :], ig_s[sl, :], a_s[sl, :]
            mult = jnp.where(first, 1.0, mult_s[sl, :])
            lam_t = lam_s[sl, :]
            du = jnp.where(valid, lam_t, 0.0)
            hprev = _shift_rows(hr_ref[pl.ds(pl.multiple_of(jnp.maximum(r0 - 8, 0), 8), 8), :] * (i > 0).astype(f32), hr_ref[sl, :], 1)
            da = lam_t * hprev
            dmult = jnp.where(first, 0.0, du * ig * xcv)
            di = du * mult * xcv
            dxc = du * mult * ig
            ratio = jnp.where(valid & jnp.logical_not(first), a * a / mult, 0.0)
            dla = da * a - dmult * ratio
            dpr = (dla * (LRU_C * ls)) * r * (1.0 - r)
            dpi = di * ig * (1.0 - ig)
            dprb, dpib = dpr.astype(bf16), dpi.astype(bf16)
            dw_s[0] += _dot_tn(xb, dprb)
            dw_s[1] += _dot_tn(xb, dpib)
            dxc_s[sl, :] = dxc + _dot_nt(dprb, w_ra) + _dot_nt(dpib, w_ri)
            return d_bra + _colsum(dpr), d_bri + _colsum(dpi), d_ls + _colsum(dla * (LRU_C * r))

        d_bra, d_bri, d_ls = lax.fori_loop(0, nb, p3, (zrow, zrow, zrow))

        def p4(i, carry):
            d_cb, d_w0, d_w1, d_w2, d_w3 = carry
            r0 = pl.multiple_of(i * BLK, BLK)
            sl = pl.ds(r0, BLK)
            grow = rid + r0
            valid = grow >= ROW0
            dxc = dxc_s[sl, :]
            nxt = dxc_s[pl.ds(pl.multiple_of(jnp.minimum(r0 + BLK, rows - 8), 8), 8), :] * (i < nb - 1).astype(f32)
            ext = jnp.concatenate([dxc, nxt], axis=0)
            dxr = cw[0:1] * dxc
            for k in range(1, CONV_WIDTH):
                dxr = dxr + cw[k:k + 1] * pltpu.roll(ext, BLK + 8 - k, 0)[:BLK, :]
            dxr_ref[sl, :] = jnp.where(valid, dxr, 0.0).astype(bf16)
            cur = jnp.where(valid, xr_ref[sl, :], 0.0)
            prev8 = xr_ref[pl.ds(pl.multiple_of(jnp.maximum(r0 - 8, 0), 8), 8), :] * (i > 0).astype(f32)
            dws = [d_w0 + _colsum(dxc * cur)]
            for k, acc in ((1, d_w1), (2, d_w2), (3, d_w3)):
                dws.append(acc + _colsum(dxc * _shift_rows(prev8, cur, k)))
            return (d_cb + _colsum(dxc), *dws)

        d_cb, d_w0, d_w1, d_w2, d_w3 = lax.fori_loop(0, nb, p4, (zrow,) * 5)

        d_lam = d_ls * _sigmoid(-lam_v)
        vec_ref[...] = jnp.concatenate([d_bra, d_bri, d_lam, d_cb, d_w0, d_w1, d_w2, d_w3], axis=0)
        dw_ref[:, 0] = dw_s[0].astype(bf16).reshape(N_DEV, 32, RNN_BLOCK)
        dw_ref[:, 1] = dw_s[1].astype(bf16).reshape(N_DEV, 32, RNN_BLOCK)

    return pl.pallas_call(
        body, grid=(N_RNN_BLOCKS,),
        in_specs=[blk, blk, blk, col(0), col(OFF_GR)] + _RNN_IN_SPECS(rows),
        out_specs=[blk, blk,
                   pl.BlockSpec((N_DEV, 2, None, 32, RNN_BLOCK), lambda n: (0, 0, n, 0, 0)),
                   pl.BlockSpec((8, RNN_BLOCK), lambda n: (0, n))],
        out_shape=[SDS((rows, D), bf16), SDS((rows, D), bf16),
                   SDS((N_DEV, 2, N_RNN_BLOCKS, 32, RNN_BLOCK), bf16), SDS((8, D), f32)],
        scratch_shapes=[pltpu.VMEM((rows, RNN_BLOCK), f32)] * 6 + [pltpu.VMEM((2, RNN_BLOCK, RNN_BLOCK), f32)],
        name="rnn_bwd", compiler_params=_cp(("arbitrary",), 48),
    )(dya, hr, xc, z, z, smallw, conv_b, wrg, b_ra, b_ri, lam)


def _rope_tables(rows):
    half = jnp.arange(HALF, dtype=f32)
    inv = ROPE_THETA ** (-half / HALF)
    pos = (jnp.arange(rows) - ROW0).astype(f32)
    ang = pos[:, None] * inv[None, :]
    cos, sin = jnp.cos(ang), jnp.sin(ang)
    cos128 = jnp.concatenate([cos, cos, cos, cos], axis=1)
    sin128 = jnp.concatenate([-sin, sin, -sin, sin], axis=1)
    return cos128, sin128


def _rope128(x, cos128, sin128):
    lane = lax.broadcasted_iota(jnp.int32, x.shape, 1)
    swapped = jnp.where(lane % HEAD_DIM < HALF, pltpu.roll(x, 128 - HALF, 1), pltpu.roll(x, HALF, 1))
    return x * cos128 + swapped * sin128


def _qkv_prep(z, cos128, sin128):
    rows = z.shape[0]

    def body(q_ref, kv_ref, c_ref, s_ref, qo_ref, ko_ref, vo_ref):
        c, s = c_ref[...], s_ref[...]
        for g in range(D // 128):
            qo_ref[:, g * 128:(g + 1) * 128] = (_rope128(q_ref[:, g * 128:(g + 1) * 128], c, s)
                                                * (HEAD_DIM ** -0.5)).astype(bf16)
        for g in range(2):
            kr = _rope128(kv_ref[:, g * 128:(g + 1) * 128], c, s)
            for j in range(2):
                ko_ref[2 * g + j] = kr[:, j * HEAD_DIM:(j + 1) * HEAD_DIM].astype(bf16)
        for h in range(N_KV):
            vo_ref[h] = kv_ref[:, 256 + h * HEAD_DIM:256 + (h + 1) * HEAD_DIM].astype(bf16)

    return pl.pallas_call(
        body, grid=(rows // BLK,),
        in_specs=[pl.BlockSpec((BLK, D), lambda i: (i, OFF_Q // D)),
                  pl.BlockSpec((BLK, 512), lambda i: (i, OFF_K // 512)),
                  pl.BlockSpec((BLK, 128), lambda i: (i, 0)),
                  pl.BlockSpec((BLK, 128), lambda i: (i, 0))],
        out_specs=[pl.BlockSpec((BLK, D), lambda i: (i, 0)),
                   pl.BlockSpec((N_KV, BLK, HEAD_DIM), lambda i: (0, i, 0)),
                   pl.BlockSpec((N_KV, BLK, HEAD_DIM), lambda i: (0, i, 0))],
        out_shape=[SDS((rows, D), bf16), SDS((N_KV, rows, HEAD_DIM), bf16), SDS((N_KV, rows, HEAD_DIM), bf16)],
        name="qkv_prep", compiler_params=_cp(("arbitrary",)),
    )(z, z, cos128, sin128)


def _attn_mask(n):
    qi = n * BLK + lax.broadcasted_iota(jnp.int32, (BLK, 2 * BLK + N_META), 0)
    c = lax.broadcasted_iota(jnp.int32, (BLK, 2 * BLK + N_META), 1)
    jb = (n - 1) * BLK + c
    band = (jb >= BLK) & (jb <= qi) & (qi - jb < BLK)
    meta = (ROW0 + c - 2 * BLK) <= qi
    return ((c < 2 * BLK) & band) | ((c >= 2 * BLK) & meta)


N_KEYS = 2 * BLK + N_META


def _stack_heads(t):
    return jnp.concatenate([t[:, g * HEAD_DIM:(g + 1) * HEAD_DIM] for g in range(GROUP)], axis=0)


def _sink_column(sink_ref, h):
    g = lax.broadcasted_iota(jnp.int32, (GROUP, 1, 1), 0)
    col = jnp.zeros((GROUP, 1, 1), f32)
    for j in range(GROUP):
        col = jnp.where(g == j, sink_ref[h * GROUP + j], col)
    return col


def _kv_specs(last):
    cl = lambda n: jnp.minimum(n, last)
    return [pl.BlockSpec((None, N_META, HEAD_DIM), lambda h, n: (h, ROW0 // N_META, 0)),
            pl.BlockSpec((None, BLK, HEAD_DIM), lambda h, n: (h, jnp.maximum(cl(n) - 1, 0), 0)),
            pl.BlockSpec((None, BLK, HEAD_DIM), lambda h, n: (h, cl(n), 0))]


def _attn_fwd(q_r, k_r, v_b, z, sinks):
    rows = q_r.shape[0]
    nb = rows // BLK

    def body(sink_ref, q_ref, km_ref, kp_ref, kc_ref, vm_ref, vp_ref, vc_ref, ga_ref, o_ref, yb_ref, ybt_ref, lse_ref):
        h, n = pl.program_id(0), pl.program_id(1)
        kk = jnp.concatenate([kp_ref[...], kc_ref[...], km_ref[...]], axis=0)
        vv = jnp.concatenate([vp_ref[...], vc_ref[...], vm_ref[...]], axis=0)
        q2 = _stack_heads(q_ref[...])
        s = jnp.where(_attn_mask(n)[None], _dot_nt(q2, kk).reshape(GROUP, BLK, N_KEYS), NEG_INF)
        sink = _sink_column(sink_ref, h)
        m = jnp.maximum(jnp.max(s, axis=-1, keepdims=True), sink)
        p = jnp.exp(s - m)
        den = jnp.sum(p, axis=-1, keepdims=True) + jnp.exp(sink - m)
        o2 = _dot((p / den).astype(bf16).reshape(GROUP * BLK, N_KEYS), vv)
        lse = m + jnp.log(den)
        for g in range(GROUP):
            o_ref[:, g * HEAD_DIM:(g + 1) * HEAD_DIM] = o2[g * BLK:(g + 1) * BLK]
            lse_ref[:, g:g + 1] = lse[g]
        yb = o_ref[...] * _silu_and_grad(ga_ref[...])[0]
        yb_ref[...] = yb.astype(bf16)
        ybt_ref[...] = yb.T.astype(bf16)

    tile = pl.BlockSpec((BLK, 512), lambda h, n: (n, h))
    return pl.pallas_call(
        body, grid=(N_KV, nb),
        in_specs=[pl.BlockSpec(memory_space=pltpu.SMEM), tile] + _kv_specs(nb - 1) + _kv_specs(nb - 1)
                 + [pl.BlockSpec((BLK, 512), lambda h, n: (n, OFF_GA // 512 + h))],
        out_specs=[tile, tile, pl.BlockSpec((512, BLK), lambda h, n: (h, n)),
                   pl.BlockSpec((None, BLK, GROUP), lambda h, n: (h, n, 0))],
        out_shape=[SDS((rows, D), f32), SDS((rows, D), bf16), SDS((D, rows), bf16),
                   SDS((N_KV, rows, GROUP), f32)],
        name="attn_fwd", compiler_params=_cp(("arbitrary", "arbitrary")),
    )(sinks, q_r, k_r, k_r, k_r, v_b, v_b, v_b, z)


def _attn_bwd(dyb, o32, lse, q_r, k_r, v_b, z, sinks):
    rows = q_r.shape[0]
    nb = rows // BLK
    cl = lambda n: jnp.minimum(n, nb - 1)

    def body(sink_ref, dyb_ref, o_ref, lse_ref, q_ref, km_ref, kp_ref, kc_ref, vm_ref, vp_ref, vc_ref, ga_ref,
             dq_ref, dga_ref, dk_ref, dv_ref, dkm_ref, dvm_ref, dsr_ref, ck_s, cv_s):
        h, n = pl.program_id(0), pl.program_id(1)

        @pl.when(n == 0)
        def _():
            dkm_ref[...] = jnp.zeros_like(dkm_ref)
            dvm_ref[...] = jnp.zeros_like(dvm_ref)
            ck_s[...] = jnp.zeros_like(ck_s)
            cv_s[...] = jnp.zeros_like(cv_s)

        @pl.when(n < nb)
        def _():
            kk = jnp.concatenate([kp_ref[...], kc_ref[...], km_ref[...]], axis=0)
            vv = jnp.concatenate([vp_ref[...], vc_ref[...], vm_ref[...]], axis=0)
            sg, dsg = _silu_and_grad(ga_ref[...])
            dyb_v = dyb_ref[...]
            o_v = o_ref[...]
            dga_ref[...] = (dyb_v * o_v * dsg).astype(bf16)
            q2 = _stack_heads(q_ref[...])
            do2 = _stack_heads(dyb_v * sg)
            lse_v = lse_ref[...]
            lse = jnp.concatenate([lse_v[:, g:g + 1] for g in range(GROUP)], axis=0).reshape(GROUP, BLK, 1)
            delta = jnp.sum(do2 * _stack_heads(o_v), axis=-1, keepdims=True).reshape(GROUP, BLK, 1)
            s = jnp.where(_attn_mask(n)[None], _dot_nt(q2, kk).reshape(GROUP, BLK, N_KEYS), NEG_INF)
            p = jnp.exp(s - lse)
            do2b = do2.astype(bf16)
            ds = (p * (_dot_nt(do2b, vv).reshape(GROUP, BLK, N_KEYS) - delta)).astype(bf16)
            ds = ds.reshape(GROUP * BLK, N_KEYS)
            dsr = -jnp.exp(_sink_column(sink_ref, h) - lse) * delta
            dq2 = _dot(ds, kk)
            for g in range(GROUP):
                dq_ref[:, g * HEAD_DIM:(g + 1) * HEAD_DIM] = dq2[g * BLK:(g + 1) * BLK]
                dsr_ref[:, g:g + 1] = dsr[g]
            dkk = _dot_tn(ds, q2)
            dvv = _dot_tn(p.astype(bf16).reshape(GROUP * BLK, N_KEYS), do2b)
            dk_ref[...] = ck_s[...] + dkk[:BLK]
            dv_ref[...] = cv_s[...] + dvv[:BLK]
            ck_s[...] = dkk[BLK:2 * BLK]
            cv_s[...] = dvv[BLK:2 * BLK]
            dkm_ref[...] += dkk[2 * BLK:]
            dvm_ref[...] += dvv[2 * BLK:]

        @pl.when(n == nb)
        def _():
            dk_ref[...] = ck_s[...]
            dv_ref[...] = cv_s[...]

    tile = pl.BlockSpec((BLK, 512), lambda h, n: (cl(n), h))
    kvout = pl.BlockSpec((None, BLK, HEAD_DIM), lambda h, n: (h, jnp.maximum(n - 1, 0), 0))
    mout = pl.BlockSpec((None, N_META, HEAD_DIM), lambda h, n: (h, 0, 0))
    stat = pl.BlockSpec((None, BLK, GROUP), lambda h, n: (h, cl(n), 0))
    return pl.pallas_call(
        body, grid=(N_KV, nb + 1),
        in_specs=[pl.BlockSpec(memory_space=pltpu.SMEM), tile, tile, stat, tile] + _kv_specs(nb - 1)
                 + _kv_specs(nb - 1) + [pl.BlockSpec((BLK, 512), lambda h, n: (cl(n), OFF_GA // 512 + h))],
        out_specs=[tile, tile, kvout, kvout, mout, mout, stat],
        out_shape=[SDS((rows, D), f32), SDS((rows, D), bf16),
                   SDS((N_KV, rows, HEAD_DIM), f32), SDS((N_KV, rows, HEAD_DIM), f32),
                   SDS((N_KV, N_META, HEAD_DIM), f32), SDS((N_KV, N_META, HEAD_DIM), f32),
                   SDS((N_KV, rows, GROUP), f32)],
        scratch_shapes=[pltpu.VMEM((BLK, HEAD_DIM), f32), pltpu.VMEM((BLK, HEAD_DIM), f32)],
        name="attn_bwd", compiler_params=_cp(("arbitrary", "arbitrary")),
    )(sinks, dyb, o32, lse, q_r, k_r, k_r, k_r, v_b, v_b, v_b, z)


def _qkv_finish(dq, dk, dv, dkm, dvm, cos128, sin128):
    rows = dq.shape[0]

    def body(dq_ref, dk_ref, dv_ref, dkm_ref, dvm_ref, c_ref, s_ref, oq_ref, okv_ref):
        first = (pl.program_id(0) == 0).astype(f32)
        c, s = c_ref[...], -s_ref[...]
        for g in range(D // 128):
            oq_ref[:, g * 128:(g + 1) * 128] = (_rope128(dq_ref[:, g * 128:(g + 1) * 128], c, s)
                                                * (HEAD_DIM ** -0.5)).astype(bf16)
        pad = jnp.zeros((ROW0, HEAD_DIM), f32)
        ks = [dk_ref[h] + first * jnp.concatenate([pad, dkm_ref[h]], axis=0) for h in range(N_KV)]
        vs = [dv_ref[h] + first * jnp.concatenate([pad, dvm_ref[h]], axis=0) for h in range(N_KV)]
        for g in range(2):
            kp = jnp.concatenate([ks[2 * g], ks[2 * g + 1]], axis=1)
            okv_ref[:, g * 128:(g + 1) * 128] = _rope128(kp, c, s).astype(bf16)
            okv_ref[:, 256 + g * 128:256 + (g + 1) * 128] = jnp.concatenate([vs[2 * g], vs[2 * g + 1]], axis=1).astype(bf16)

    kv = pl.BlockSpec((N_KV, BLK, HEAD_DIM), lambda i: (0, i, 0))
    mt = pl.BlockSpec((N_KV, N_META, HEAD_DIM), lambda i: (0, 0, 0))
    return pl.pallas_call(
        body, grid=(rows // BLK,),
        in_specs=[pl.BlockSpec((BLK, D), lambda i: (i, 0)), kv, kv, mt, mt,
                  pl.BlockSpec((BLK, 128), lambda i: (i, 0)), pl.BlockSpec((BLK, 128), lambda i: (i, 0))],
        out_specs=[pl.BlockSpec((BLK, D), lambda i: (i, 0)), pl.BlockSpec((BLK, 512), lambda i: (i, 0))],
        out_shape=[SDS((rows, D), bf16), SDS((rows, 512), bf16)],
        name="qkv_finish", compiler_params=_cp(("arbitrary",)),
    )(dq, dk, dv, dkm, dvm, cos128, sin128)


_TW = 512


def _mix_specs(rows):
    tr = _row_chunk(rows)
    tile = pl.BlockSpec((tr, _TW), lambda i, j: (i, j))
    ga = pl.BlockSpec((tr, _TW), lambda i, j: (i, OFF_G // _TW + j))
    gb = pl.BlockSpec((tr, _TW), lambda i, j: (i, (OFF_G + D) // _TW + j))
    return (rows // tr, D // _TW), tile, ga, gb


def _mix_fwd(y_a, y_b, z):
    rows = y_a.shape[0]
    tw = 256
    col = lambda off: pl.BlockSpec((rows, tw), lambda j: (0, off // tw + j))

    def body(ya_ref, yb_ref, ga_ref, gb_ref, o_ref, ot_ref):
        mixed = _sigmoid(ga_ref[...]) * ya_ref[...] + _sigmoid(gb_ref[...]) * yb_ref[...]
        o_ref[...] = mixed.astype(bf16)
        ot_ref[...] = mixed.T.astype(bf16)

    return pl.pallas_call(
        body, grid=(D // tw,), in_specs=[col(0), col(0), col(OFF_G), col(OFF_G + D)],
        out_specs=[col(0), pl.BlockSpec((tw, rows), lambda j: (j, 0))],
        out_shape=[SDS((rows, D), bf16), SDS((D, rows), bf16)],
        name="mix_fwd", compiler_params=_cp(("arbitrary",)),
    )(y_a, y_b, z, z)


def _mix_bwd(dmixed, y_a, y_b, z):
    rows = y_a.shape[0]
    grid, _mix_tile, _mix_ga, _mix_gb = _mix_specs(rows)

    def body(dm_ref, ya_ref, yb_ref, ga_ref, gb_ref, dya_ref, dyb_ref, dga_ref, dgb_ref):
        dm = dm_ref[...]
        sa, sb = _sigmoid(ga_ref[...]), _sigmoid(gb_ref[...])
        dya_ref[...] = (dm * sa).astype(bf16)
        dyb_ref[...] = (dm * sb).astype(bf16)
        dga_ref[...] = (dm * ya_ref[...] * sa * (1.0 - sa)).astype(bf16)
        dgb_ref[...] = (dm * yb_ref[...] * sb * (1.0 - sb)).astype(bf16)

    return pl.pallas_call(
        body, grid=grid, in_specs=[_mix_tile, _mix_tile, _mix_tile, _mix_ga, _mix_gb],
        out_specs=[_mix_tile] * 4, out_shape=[SDS((rows, D), bf16)] * 4,
        name="mix_bwd", compiler_params=_cp(("arbitrary", "arbitrary")),
    )(dmixed, y_a, y_b, z, z)


def _final_ln(out32, h32, tgt, ln_g, ln_b):
    rows = out32.shape[0]

    def body(o_ref, h_ref, t_ref, g_ref, b_ref, du_ref, dub_ref, st_ref):
        i = pl.program_id(0)
        g = g_ref[...]
        y, xhat, rstd = _ln_rows(ALPHA * h_ref[...] + o_ref[...], g, b_ref[...])
        e = jnp.where(i > 0, y - t_ref[0], 0.0)
        dy = e * (1.0 / D)
        du = _ln_rows_bwd(dy, g, xhat, rstd)
        du_ref[...] = du
        dub_ref[...] = du.astype(bf16)
        st = jnp.concatenate([_colsum(dy * xhat), _colsum(dy), _colsum(du), _colsum(e * e) * (0.5 / D),
                              jnp.zeros((4, D), f32)], axis=0)

        @pl.when(i == 0)
        def _():
            st_ref[...] = st

        @pl.when(i > 0)
        def _():
            st_ref[...] += st

    row = pl.BlockSpec((BLK, D), lambda i: (i, 0))
    vec = pl.BlockSpec((1, D), lambda i: (0, 0))
    return pl.pallas_call(
        body, grid=(rows // BLK,),
        in_specs=[row, row, pl.BlockSpec((1, BLK, D), lambda i: (0, jnp.maximum(i - 1, 0), 0)), vec, vec],
        out_specs=[row, row, pl.BlockSpec((8, D), lambda i: (0, 0))],
        out_shape=[SDS((rows, D), f32), SDS((rows, D), bf16), SDS((8, D), f32)],
        name="final_ln", compiler_params=_cp(("arbitrary",)),
    )(out32, h32, tgt, ln_g, ln_b)


def _assemble_dz(dxr, dgr, dq, dkv, dga, dma, dmb):
    rows = dxr.shape[0]
    parts = [(dxr, D), (dgr, D), (dq, D), (dkv, 512), (dga, D), (dma, D), (dmb, D)]

    def body(*refs):
        o_ref = refs[-1]
        off = 0
        for r, (_, w) in zip(refs[:-1], parts):
            o_ref[:, off:off + w] = r[...]
            off += w

    return pl.pallas_call(
        body, grid=(rows // BLK,),
        in_specs=[pl.BlockSpec((BLK, w), lambda i: (i, 0)) for _, w in parts],
        out_specs=pl.BlockSpec((BLK, D_IN), lambda i: (i, 0)),
        out_shape=SDS((rows, D_IN), bf16), name="assemble_dz", compiler_params=_cp(("arbitrary",)),
    )(*[p for p, _ in parts])


def _step_branches(h32, hb, z, wrg, smallw, p):
    rows = z.shape[0]
    cos128, sin128 = _rope_tables(rows)
    sinks = p["sinks"].reshape(N_KV * GROUP)
    xc, hr, ya, ya_t = _rnn_fwd(z, smallw, p["conv_b"], wrg, p["b_ra"], p["b_ri"], p["lru_lambda"])
    q_r, k_r, v_b = _qkv_prep(z, cos128, sin128)
    o32, yb, yb_t, lse = _attn_fwd(q_r, k_r, v_b, z, sinks)
    return dict(cos128=cos128, sin128=sin128, sinks=sinks, h32=h32, hb=hb, z=z, xc=xc, hr=hr, ya=ya, ya_t=ya_t,
                q_r=q_r, k_r=k_r, v_b=v_b, o32=o32, yb=yb, yb_t=yb_t, lse=lse)


def _step_merge(s, tgt, w3, p):
    ya, yb, z = s["ya"], s["yb"], s["z"]
    y_a = _mm(ya, w3, sel=0, name="mm_ya")
    y_b = _mm(yb, w3, sel=1, name="mm_yb")
    mixed, mixed_t = _mix_fwd(y_a, y_b, z)
    out32 = _mm(mixed, w3, sel=2, bias=p["b_o"], name="mm_out")
    du32, dub, st_out = _final_ln(out32, s["h32"], tgt, p["ln_g"], p["ln_b"])

    g_wo = _mm(mixed_t, dub, out_dtype=bf16, name="mm_dwo")
    dmixed = _mm(dub, w3, sel=2, nt=True, name="mm_dmixed")
    dya_b, dyb_b, dma, dmb = _mix_bwd(dmixed, y_a, y_b, z)
    g_wrnn = _mm(s["ya_t"], dya_b, out_dtype=bf16, name="mm_dwrnn")
    g_wattn = _mm(s["yb_t"], dyb_b, out_dtype=bf16, name="mm_dwattn")
    dya = _mm(dya_b, w3, sel=0, nt=True, name="mm_dya")
    dyb = _mm(dyb_b, w3, sel=1, nt=True, name="mm_dyb")
    return dict(du32=du32, st_out=st_out, dma=dma, dmb=dmb, dya=dya, dyb=dyb, g_wo=g_wo, g_wrnn=g_wrnn,
                g_wattn=g_wattn)


def _step_backward(s, t, wrg, smallw, p, conv_b):
    z = s["z"]
    dxr, dgr, g_wrg, vec_rnn = _rnn_bwd(t["dya"], s["hr"], s["xc"], z, smallw, conv_b, wrg, p["b_ra"], p["b_ri"],
                                        p["lru_lambda"])
    dq_r, dga, dk, dv, dkm, dvm, dsr = _attn_bwd(t["dyb"], s["o32"], s["lse"], s["q_r"], s["k_r"], s["v_b"], z,
                                                 s["sinks"])
    dq, dkv = _qkv_finish(dq_r, dk, dv, dkm, dvm, s["cos128"], s["sin128"])
    dz = _assemble_dz(dxr, dgr, dq, dkv, dga, t["dma"], t["dmb"])
    return dict(vec_rnn=vec_rnn, dsr=dsr, g_wrg=g_wrg, dz=dz)


def _step_input_grad(dz, w_t, after, du32, x, smallw, p):
    dh = _mm_dh(dz, w_t, after)
    grad_x, dmeta, st_emb = _ln_emb_bwd(dh, du32, x, smallw, p["ln_emb_g"])
    return dict(grad_x=grad_x, dmeta=dmeta, st_emb=st_emb)


_ANY = pl.BlockSpec(memory_space=pl.ANY)
_VMEM = pl.BlockSpec(memory_space=pltpu.VMEM)


def _place():
    x, y, c = lax.axis_index("x"), lax.axis_index("y"), lax.axis_index("c")
    return x, y, c


def _dev(px, py, pc):
    return 4 * px + 2 * py + pc


def _tile_rows(r):
    return max(t for t in range(16, 321, 16) if r % t == 0) if r > 320 else r


def _cast_w_in(w_in_t):
    tm = _tile_rows(SHARD_IN)

    def body(i_ref, o_ref):
        o_ref[...] = i_ref[...].astype(bf16)

    return pl.pallas_call(
        body, grid=(SHARD_IN // tm,),
        in_specs=[pl.BlockSpec((tm, D), lambda i: (i, 0))],
        out_specs=pl.BlockSpec((tm, D), lambda i: (i, 0)),
        out_shape=SDS((SHARD_IN, D), bf16), name="cast_w_in", compiler_params=_cp(("arbitrary",)),
    )(w_in_t)


def _cast_small(w_rnn_out, w_attn_out, w_o, w_ra, w_ri, meta, conv_w):
    def body(a_ref, b_ref, c_ref, ra_ref, ri_ref, m_ref, cw_ref, w3_ref, wrg_ref, sw_ref):
        w3_ref[0] = a_ref[0].astype(bf16)
        w3_ref[1] = b_ref[0].astype(bf16)
        w3_ref[2] = c_ref[0].astype(bf16)
        wrg_ref[0] = ra_ref[0].astype(bf16)
        wrg_ref[1] = ri_ref[0].astype(bf16)
        sw_ref[...] = jnp.concatenate([m_ref[...], cw_ref[0], jnp.zeros((4, 256), f32)], axis=0)

    return pl.pallas_call(
        body,
        out_shape=[SDS((3, 256, D), bf16), SDS((2, N_RNN_BLOCKS, 32, RNN_BLOCK), bf16), SDS((24, 256), f32)],
        name="cast_small", compiler_params=_cp(None),
    )(w_rnn_out, w_attn_out, w_o, w_ra, w_ri, meta, conv_w)


def _gather_small(shard):
    def body(s_ref, o_ref, send_sems, recv_sems):
        x, y, c = _place()
        me = _dev(x, y, c)
        copies = []
        for k, (fx, fy, fc) in enumerate(_PEER_FLIPS):
            peer = ((x + fx) % 2, (y + fy) % 2, (c + fc) % 2)
            copies.append(_remote(s_ref, o_ref.at[me], send_sems, recv_sems, k, peer))
        for cp in copies:
            cp.start()
        o_ref[me] = s_ref[...]
        for cp in copies:
            cp.wait()

    return pl.pallas_call(
        body, in_specs=[_VMEM], out_specs=_VMEM, out_shape=SDS((N_DEV, *shard.shape), shard.dtype),
        scratch_shapes=[pltpu.SemaphoreType.DMA((7,)), pltpu.SemaphoreType.DMA((7,))],
        name="gather_small",
    )(shard)


def _gather_project(w_s, smalls, later, hb, b_in, order):
    arrays = (w_s, *smalls, *later)
    na, n = len(arrays), 1 + len(smalls)
    rows = hb.shape[0]
    cm = _row_chunk(rows)
    nm = rows // cm
    pair = 2 * SHARD_IN

    def body(order_ref, *refs):
        ins, hb_ref, b_ref = refs[:na], refs[na], refs[na + 1]
        outs, z_ref = refs[na + 2:2 * na + 2], refs[2 * na + 2]
        wbuf, send_sems, recv_sems, local_sems, load_sems = refs[2 * na + 3:]
        k, mi = pl.program_id(0), pl.program_id(1)
        x, y, c = _place()
        me, sibling = (x, y, c), (x, y, 1 - c)
        chips = [(1 - x, y), (x, 1 - y), (1 - x, 1 - y)]

        def copy(a, kk, block, to, src=None):
            dst = outs[a].at[_dev(*block)]
            return pltpu.make_async_remote_copy(
                src_ref=dst if src is None else src, dst_ref=dst,
                send_sem=send_sems.at[a * 7 + kk], recv_sem=recv_sems.at[a * 7 + kk],
                device_id=to, device_id_type=MESH)

        mine = [pltpu.make_async_copy(ins[a], outs[a].at[_dev(*me)], local_sems.at[a]) for a in range(na)]

        def first():
            cps = []
            for a in range(n):
                cps.append(copy(a, 0, me, sibling, src=ins[a]))
                cps += [copy(a, 1 + j, me, (*chip, c), src=ins[a]) for j, chip in enumerate(chips)]
            return cps

        def load_pair(chip):
            cps = [pltpu.make_async_copy(outs[0].at[_dev(*chip, cc)], wbuf.at[pl.ds(cc * SHARD_IN, SHARD_IN)],
                                         load_sems.at[cc]) for cc in range(2)]
            for cp in cps:
                cp.start()
            for cp in cps:
                cp.wait()

        @pl.when((k == 0) & (mi == 0))
        def _():
            for cp in mine + first():
                cp.start()
            mine[0].wait()
            copy(0, 0, sibling, me).wait_recv()
            load_pair((x, y))

        for j, chip in enumerate(chips):
            @pl.when((k == j + 1) & (mi == 0))
            def _():
                for a in range(n):
                    copy(a, 1 + j, (*chip, c), me).wait_recv()
                    copy(a, 4 + j, (*chip, c), sibling).start()
                copy(0, 4 + j, (*chip, 1 - c), me).wait_recv()
                load_pair(chip)

        z_ref[...] = _dot_nt(hb_ref[...], wbuf[...]) + b_ref[...]

        @pl.when((k == len(chips)) & (mi == nm - 1))
        def _():
            for a in range(1, n):
                copy(a, 0, sibling, me).wait_recv()
                for j, chip in enumerate(chips):
                    copy(a, 4 + j, (*chip, 1 - c), me).wait_recv()
            for cp in first():
                cp.wait_send()
            for a in range(n):
                for j, chip in enumerate(chips):
                    copy(a, 4 + j, (*chip, c), sibling).wait_send()
            for cp in mine[1:]:
                cp.wait()

    res = pl.pallas_call(
        body,
        grid_spec=pltpu.PrefetchScalarGridSpec(
            num_scalar_prefetch=1, grid=(N_DEV // 2, nm),
            in_specs=[_ANY] * na + [pl.BlockSpec((cm, D), lambda k, i, o: (i, 0)),
                                    pl.BlockSpec((1, pair), lambda k, i, o: (0, o[k]))],
            out_specs=[_ANY] * na + [pl.BlockSpec((cm, pair), lambda k, i, o: (i, o[k]))],
            scratch_shapes=[pltpu.VMEM((pair, D), bf16), pltpu.SemaphoreType.DMA((7 * n,)),
                            pltpu.SemaphoreType.DMA((7 * n,)), pltpu.SemaphoreType.DMA((na,)),
                            pltpu.SemaphoreType.DMA((2,))]),
        out_shape=[SDS((N_DEV, *s.shape), s.dtype) for s in arrays] + [SDS((rows, D_IN), f32)],
        name="gather_project", compiler_params=_cp(("arbitrary", "arbitrary"), 48),
    )(order, *arrays, hb, b_in)
    return res[:na], res[na]


_HBM = pl.BlockSpec(memory_space=pltpu.HBM)
_SEM = pl.BlockSpec(memory_space=pltpu.SEMAPHORE)
_PEER_FLIPS = [(f // 4, (f // 2) % 2, f % 2) for f in range(1, N_DEV)]


def _remote(src, dst, send_sems, recv_sems, k, to):
    return pltpu.make_async_remote_copy(src_ref=src, dst_ref=dst, send_sem=send_sems.at[k], recv_sem=recv_sems.at[k],
                                        device_id=to, device_id_type=MESH)


def _copies_direct(same_src):
    def make(srcs, lands, send_sems, recv_sems):
        x, y, c = _place()
        me = _dev(x, y, c)
        out = []
        for a in range(len(srcs)):
            for k, (fx, fy, fc) in enumerate(_PEER_FLIPS):
                peer = ((x + fx) % 2, (y + fy) % 2, (c + fc) % 2)
                src = srcs[a] if same_src else srcs[a].at[_dev(*peer)]
                out.append(_remote(src, lands[a].at[me], send_sems, recv_sems, 7 * a + k, peer))
        return out
    return make


def _copies_siblings(srcs, lands, send_sems, recv_sems):
    x, y, c = _place()
    return [_remote(srcs[a].at[2 * q + (1 - c)], lands[a].at[q], send_sems, recv_sems, 4 * a + q, (x, y, 1 - c))
            for a in range(len(srcs)) for q in range(4)]


def _copies_chips(srcs, lands, send_sems, recv_sems):
    x, y, c = _place()
    chips = [(1 - x, y), (x, 1 - y), (1 - x, 1 - y)]
    return [_remote(srcs[a].at[2 * qx + qy], lands[a].at[j], send_sems, recv_sems, 3 * a + j, (qx, qy, c))
            for a in range(len(srcs)) for j, (qx, qy) in enumerate(chips)]


def _split_start(make, per_array, srcs, lands, dep, name):
    n = len(srcs)

    def body(*refs):
        send_sems, recv_sems, token = refs[2 * n + 1], refs[2 * n + 2], refs[-1]
        for cp in make(refs[:n], refs[n:2 * n], send_sems, recv_sems):
            cp.start()
        token[...] = jnp.zeros_like(token)

    hbm = lambda t: pltpu.with_memory_space_constraint(t, pltpu.HBM)
    res = pl.pallas_call(
        body, name=name,
        out_shape=(pltpu.SemaphoreType.DMA((per_array * n,)), pltpu.SemaphoreType.DMA((per_array * n,)),
                   *[pltpu.HBM(t.shape, t.dtype) for t in (*srcs, *lands)], SDS((8, 128), f32)),
        in_specs=[_HBM] * (2 * n) + [_ANY], out_specs=(_SEM, _SEM, *([_HBM] * (2 * n)), _VMEM),
        input_output_aliases={i: 2 + i for i in range(2 * n)},
        compiler_params=pltpu.CompilerParams(has_side_effects=pltpu.SideEffectType.DATAFLOW_SIDE_EFFECTING),
    )(*[hbm(t) for t in (*srcs, *lands)], dep)
    return res[0], res[1], list(res[2:2 + n]), list(res[2 + n:2 + 2 * n]), res[-1]


def _split_wait(make, send_sems, recv_sems, srcs, lands, after, name):
    n = len(srcs)

    def body(*refs):
        for cp in make(refs[:n], refs[n:2 * n], refs[2 * n], refs[2 * n + 1]):
            cp.wait_send()
            cp.wait_recv()

    res = pl.pallas_call(
        body, name=name,
        out_shape=tuple(pltpu.HBM(t.shape, t.dtype) for t in (*srcs, *lands)),
        in_specs=[_HBM] * (2 * n) + [_SEM, _SEM, _ANY], out_specs=tuple([_HBM] * (2 * n)),
        input_output_aliases={i: i for i in range(2 * n)},
        compiler_params=pltpu.CompilerParams(has_side_effects=pltpu.SideEffectType.DATAFLOW_SIDE_EFFECTING),
    )(*srcs, *lands, send_sems, recv_sems, after)
    return list(res[:n]), list(res[n:])


def _adamw_direct(g, land, me_idx, w, m, v, name):
    r, wd = w.shape
    tr = min(r, 256)

    def body(me_ref, *refs):
        g_ref, peers = refs[0], refs[1:N_DEV]
        w_ref, m_ref, v_ref, g_out, d_out, m_out, v_out = refs[N_DEV:]
        gs = g_ref[...].astype(f32)
        for p_ref in peers:
            gs = gs + p_ref[...].astype(f32)
        d, mn, vn = _adamw(w_ref[...], gs, m_ref[...], v_ref[...])
        g_out[...] = gs
        d_out[...] = d
        m_out[...] = mn
        v_out[...] = vn

    tile = pl.BlockSpec((tr, wd), lambda i, me_ref: (i, 0))
    slot = lambda k: pl.BlockSpec((None, tr, wd), lambda i, me_ref: ((me_ref[0] + k) % N_DEV, i, 0))
    return pl.pallas_call(
        body,
        grid_spec=pltpu.PrefetchScalarGridSpec(
            num_scalar_prefetch=1, grid=(r // tr,),
            in_specs=[slot(0)] + [slot(k) for k in range(1, N_DEV)] + [tile, tile, tile],
            out_specs=[tile] * 4),
        out_shape=[SDS((r, wd), f32)] * 4, name=name, compiler_params=_cp(("arbitrary",), 48),
    )(me_idx, g, *([land] * (N_DEV - 1)), w, m, v)


def _pair_sum(g, r1, c_idx, name):
    _, r, w = g.shape
    tr = _tile_rows(r)

    def body(c_ref, g_ref, r_ref, o_ref):
        o_ref[...] = (g_ref[...].astype(f32) + r_ref[...].astype(f32)).astype(bf16)

    return pl.pallas_call(
        body,
        grid_spec=pltpu.PrefetchScalarGridSpec(
            num_scalar_prefetch=1, grid=(4, r // tr),
            in_specs=[pl.BlockSpec((None, tr, w), lambda q, i, c_ref: (2 * q + c_ref[0], i, 0)),
                      pl.BlockSpec((None, tr, w), lambda q, i, c_ref: (q, i, 0))],
            out_specs=pl.BlockSpec((None, tr, w), lambda q, i, c_ref: (q, i, 0))),
        out_shape=SDS((4, r, w), bf16), name=name, compiler_params=_cp(("arbitrary", "arbitrary")),
    )(c_idx, g, r1)


def _adamw(w, g, m, v):
    m = ADAM_B1 * m + (1.0 - ADAM_B1) * g
    v = ADAM_B2 * v + (1.0 - ADAM_B2) * (g * g)
    m_hat = m / (1.0 - ADAM_B1 ** ADAM_STEP)
    v_hat = v / (1.0 - ADAM_B2 ** ADAM_STEP)
    delta = -ADAM_LR * (m_hat / (jnp.sqrt(v_hat) + ADAM_EPS) + ADAM_WD * w)
    return delta, m, v


def _adamw_big(pieces, q_idx, w, m, v, name, row_off=0):
    r, wd = w.shape
    tr = _tile_rows(r)
    np_ = len(pieces)
    wp = wd // np_

    def body(q_ref, *refs):
        w_ref, m_ref, v_ref, g_out, d_out, m_out, v_out = refs[2 * np_:]
        for k in range(np_):
            @pl.when(pl.program_id(1) == k)
            def _():
                p_ref, r_ref = refs[2 * k], refs[2 * k + 1]
                g = p_ref[...].astype(f32)
                for j in range(3):
                    g = g + r_ref[j].astype(f32)
                d, mn, vn = _adamw(w_ref[...], g, m_ref[...], v_ref[...])
                g_out[...] = g
                d_out[...] = d
                m_out[...] = mn
                v_out[...] = vn

    tile = pl.BlockSpec((tr, wp), lambda i, k, q_ref: (i, k))
    in_specs, args = [], []
    for part, r2 in pieces:
        in_specs += [pl.BlockSpec((None, tr, wp), lambda i, k, q_ref: (q_ref[0], row_off + i, 0)),
                     pl.BlockSpec((3, tr, wp), lambda i, k, q_ref: (0, row_off + i, 0))]
        args += [part, r2]
    return pl.pallas_call(
        body,
        grid_spec=pltpu.PrefetchScalarGridSpec(
            num_scalar_prefetch=1, grid=(r // tr, np_), in_specs=in_specs + [tile, tile, tile],
            out_specs=[tile] * 4),
        out_shape=[SDS((r, wd), f32)] * 4, name=name, compiler_params=_cp(("arbitrary", "arbitrary"), 48),
    )(q_idx, *args, w, m, v)


_SMALL_ROWS = 24


def _pack_small(st_emb, vec_rnn, st_out, dsr, db_in, dmeta):
    def body(se_ref, vr_ref, so_ref, dsr_ref, db_ref, dm_ref, sm_ref, sm2_ref):
        sm_ref[...] = jnp.zeros_like(sm_ref)
        sm2_ref[...] = jnp.zeros_like(sm2_ref)
        sm_ref[0:2, :] = se_ref[0:2, :]
        sm_ref[2:3, :] = vr_ref[3:4, :]
        sm_ref[3:6, :] = vr_ref[0:3, :]
        sm_ref[6:7, :] = so_ref[2:3, :]
        sm_ref[7:9, :] = so_ref[0:2, :]
        for h in range(N_KV):
            sm_ref[9:10, h * GROUP:(h + 1) * GROUP] = _colsum(dsr_ref[h])
        for j in range(6):
            sm_ref[16 + j:17 + j, :] = db_ref[0:1, j * D:(j + 1) * D]
        sm_ref[22:23, 0:D_IN - 6 * D] = db_ref[0:1, 6 * D:D_IN]
        for s in range(N_DEV):
            sm2_ref[s, 0:N_META, :] = dm_ref[:, s * 256:(s + 1) * 256]
            sm2_ref[s, N_META:N_META + CONV_WIDTH, :] = vr_ref[4:8, s * 256:(s + 1) * 256]

    return pl.pallas_call(
        body, out_shape=[SDS((_SMALL_ROWS, D), f32), SDS((N_DEV, 24, 256), f32)],
        name="pack_small", compiler_params=_cp(None),
    )(st_emb, vec_rnn, st_out, dsr, db_in, dmeta)


def _small_allreduce(sm, sm2):
    def body(sm_ref, sm2_ref, o_ref, o2_ref, buf, buf2, send_sems, recv_sems):
        x, y, c = _place()
        me = _dev(x, y, c)
        copies = []
        for f in range(1, N_DEV):
            fx, fy, fc = f // 4, (f // 2) % 2, f % 2
            peer = ((x + fx) % 2, (y + fy) % 2, (c + fc) % 2)
            for t, (src, dst) in enumerate(((sm_ref, buf), (sm2_ref, buf2))):
                k = 2 * (f - 1) + t
                copies.append(pltpu.make_async_remote_copy(
                    src_ref=src, dst_ref=dst.at[me], send_sem=send_sems.at[k], recv_sem=recv_sems.at[k],
                    device_id=peer, device_id_type=MESH))
        for cp in copies:
            cp.start()
        buf[me] = sm_ref[...]
        buf2[me] = sm2_ref[...]
        for cp in copies:
            cp.wait()
        acc, acc2 = buf[0], buf2[0]
        for e in range(1, N_DEV):
            acc, acc2 = acc + buf[e], acc2 + buf2[e]
        o_ref[...] = acc
        o2_ref[...] = acc2

    return pl.pallas_call(
        body, in_specs=[_VMEM, _VMEM], out_specs=[_VMEM, _VMEM],
        out_shape=[SDS(sm.shape, f32), SDS(sm2.shape, f32)],
        scratch_shapes=[pltpu.VMEM((N_DEV, *sm.shape), f32), pltpu.VMEM((N_DEV, *sm2.shape), f32),
                        pltpu.SemaphoreType.DMA((14,)), pltpu.SemaphoreType.DMA((14,))],
        name="small_allreduce",
    )(sm, sm2)


_SMALL_ROW_OF = {"ln_emb_g": 0, "ln_emb_b": 1, "conv_b": 2, "b_ra": 3, "b_ri": 4, "lru_lambda": 5, "b_o": 6,
                 "ln_g": 7, "ln_b": 8}
_SMALL_NAMES = ["ln_emb_g", "ln_emb_b", "conv_b", "b_ra", "b_ri", "lru_lambda", "b_o", "ln_g", "ln_b",
                "sinks", "b_in", "meta_tokens", "conv_w"]


def _small_update(sm, sm2_mine, wmv):
    def grad_of(name, sm_ref, s2_ref):
        if name in _SMALL_ROW_OF:
            r = _SMALL_ROW_OF[name]
            return sm_ref[r:r + 1, :]
        if name == "sinks":
            return sm_ref[9:10, 0:N_KV * GROUP]
        if name == "b_in":
            return jnp.concatenate([sm_ref[16 + j:17 + j, :] for j in range(7)], axis=1)[:, :D_IN]
        if name == "meta_tokens":
            return s2_ref[0:N_META, :]
        return s2_ref[N_META:N_META + CONV_WIDTH, :]

    def body(*refs):
        sm_ref, s2_ref = refs[0], refs[1]
        ins = refs[2:2 + 3 * len(_SMALL_NAMES)]
        outs = refs[2 + 3 * len(_SMALL_NAMES):]
        for i, name in enumerate(_SMALL_NAMES):
            w_ref, m_ref, v_ref = ins[3 * i:3 * i + 3]
            g = grad_of(name, sm_ref, s2_ref)
            d, mn, vn = _adamw(w_ref[...], g, m_ref[...], v_ref[...])
            outs[4 * i][...] = g
            outs[4 * i + 1][...] = d
            outs[4 * i + 2][...] = mn
            outs[4 * i + 3][...] = vn

    args, out_shape = [sm, sm2_mine], []
    for name in _SMALL_NAMES:
        args += list(wmv[name])
        out_shape += [SDS(wmv[name][0].shape, f32)] * 4
    res = pl.pallas_call(body, out_shape=out_shape, name="small_update", compiler_params=_cp(None))(*args)
    return {name: tuple(res[4 * i:4 * i + 4]) for i, name in enumerate(_SMALL_NAMES)}


_WEIGHTS = ["meta_tokens", "ln_emb_g", "ln_emb_b", "w_in", "b_in", "conv_w", "conv_b", "w_ra", "b_ra", "w_ri",
            "b_ri", "lru_lambda", "sinks", "w_rnn_out", "w_attn_out", "w_o", "b_o", "ln_g", "ln_b"]
_SMALL_2D = {"meta_tokens": (N_META, 256), "conv_w": (CONV_WIDTH, 256), "b_in": (1, D_IN), "sinks": (1, N_KV * GROUP)}


def kernel(x, meta_tokens, ln_emb_g, ln_emb_b, w_in, b_in, conv_w, conv_b, w_ra, b_ra, w_ri, b_ri, lru_lambda, sinks, w_rnn_out, w_attn_out, w_o, b_o, ln_g, ln_b, loss_target, m_meta_tokens, m_ln_emb_g, m_ln_emb_b, m_w_in, m_b_in, m_conv_w, m_conv_b, m_w_ra, m_b_ra, m_w_ri, m_b_ri, m_lru_lambda, m_sinks, m_w_rnn_out, m_w_attn_out, m_w_o, m_b_o, m_ln_g, m_ln_b, v_meta_tokens, v_ln_emb_g, v_ln_emb_b, v_w_in, v_b_in, v_conv_w, v_conv_b, v_w_ra, v_b_ra, v_w_ri, v_b_ri, v_lru_lambda, v_sinks, v_w_rnn_out, v_w_attn_out, v_w_o, v_b_o, v_ln_g, v_ln_b):
    w = dict(meta_tokens=meta_tokens, ln_emb_g=ln_emb_g, ln_emb_b=ln_emb_b, w_in=w_in, b_in=b_in, conv_w=conv_w,
             conv_b=conv_b, w_ra=w_ra, b_ra=b_ra, w_ri=w_ri, b_ri=b_ri, lru_lambda=lru_lambda, sinks=sinks,
             w_rnn_out=w_rnn_out, w_attn_out=w_attn_out, w_o=w_o, b_o=b_o, ln_g=ln_g, ln_b=ln_b)
    m = dict(meta_tokens=m_meta_tokens, ln_emb_g=m_ln_emb_g, ln_emb_b=m_ln_emb_b, w_in=m_w_in, b_in=m_b_in,
             conv_w=m_conv_w, conv_b=m_conv_b, w_ra=m_w_ra, b_ra=m_b_ra, w_ri=m_w_ri, b_ri=m_b_ri,
             lru_lambda=m_lru_lambda, sinks=m_sinks, w_rnn_out=m_w_rnn_out, w_attn_out=m_w_attn_out, w_o=m_w_o,
             b_o=m_b_o, ln_g=m_ln_g, ln_b=m_ln_b)
    v = dict(meta_tokens=v_meta_tokens, ln_emb_g=v_ln_emb_g, ln_emb_b=v_ln_emb_b, w_in=v_w_in, b_in=v_b_in,
             conv_w=v_conv_w, conv_b=v_conv_b, w_ra=v_w_ra, b_ra=v_b_ra, w_ri=v_w_ri, b_ri=v_b_ri,
             lru_lambda=v_lru_lambda, sinks=v_sinks, w_rnn_out=v_w_rnn_out, w_attn_out=v_w_attn_out, w_o=v_w_o,
             b_o=v_b_o, ln_g=v_ln_g, ln_b=v_ln_b)
    px, py, pc = _place()
    as_idx = lambda t: jnp.reshape(t, (1,)).astype(jnp.int32)
    c_idx, q_idx, me_idx = as_idx(pc), as_idx(2 * px + py), as_idx(_dev(px, py, pc))

    w3_s, wrg_s, small_s = _cast_small(w_rnn_out, w_attn_out, w_o, w_ra, w_ri, meta_tokens, conv_w)
    vec = lambda name: w[name].reshape(1, -1)
    p = {k: vec(k) for k in ("ln_emb_g", "ln_emb_b", "b_in", "conv_b", "b_ra", "b_ri", "lru_lambda", "sinks",
                             "b_o", "ln_g", "ln_b")}
    w_in_t = lambda a: jnp.swapaxes(a, 1, 2).reshape(SHARD_IN, D)
    smallw = _gather_small(small_s)
    h32, hb = _ln_emb(x, smallw, p["ln_emb_g"], p["ln_emb_b"])
    order = jnp.stack([2 * px + py, 2 * (1 - px) + py, 2 * px + (1 - py), 2 * (1 - px) + (1 - py)]).astype(jnp.int32)
    (wg, wrg, w3_land), z = _gather_project(_cast_w_in(w_in_t(w_in)), [wrg_s], [w3_s], hb, p["b_in"], order)
    w3_pending = _split_start(_copies_direct(True), 7, [w3_s], [w3_land], wrg, "gather_w3_start")
    w_full = wg.reshape(D_IN, D)

    s = _step_branches(h32, hb, z, wrg, smallw, {**p, "conv_b": p["conv_b"] + w3_pending[4][0:1, 0:1]})
    w3 = _split_wait(_copies_direct(True), *w3_pending[:4], s["lse"], "gather_w3_wait")[1][0]
    t = _step_merge(s, loss_target, w3, p)
    loss = lax.psum(jnp.sum(t["st_out"][3]), ("x", "y", "c"))

    big = {}
    two_d = lambda name: (w[name].shape[-2], w[name].shape[-1])
    proj = ("w_o", "w_rnn_out", "w_attn_out")
    g_proj = [t[k].reshape(N_DEV, 256, D) for k in ("g_wo", "g_wrnn", "g_wattn")]
    g_pending = _split_start(_copies_direct(False), 7, g_proj, [lax.empty((N_DEV, 256, D), bf16) for _ in proj],
                             p["b_o"], "reduce_proj_start")
    u = _step_backward(s, t, wrg, smallw, p, p["conv_b"] + g_pending[4][0:1, 0:1])

    def siblings_start(gs, dep, tag):
        return _split_start(_copies_siblings, 4, gs, [lax.empty((4, *g.shape[1:]), bf16) for g in gs], dep,
                            "reduce_siblings_start_" + tag)

    def chips_start(gs, r1, dep, tag):
        parts = [_pair_sum(g, r, c_idx, "pair_sum_%s%d" % (tag, i)) for i, (g, r) in enumerate(zip(gs, r1))]
        return _split_start(_copies_chips, 3, parts, [lax.empty((3, *q.shape[1:]), bf16) for q in parts], dep,
                            "reduce_chips_start_" + tag)

    g_a, db_in = _mm_dwin(s["hb"], u["dz"], 0, p["b_o"])
    shards = lambda g: g.reshape(N_DEV, SHARD_IN, W_IN_HALF)
    sib_a = siblings_start([shards(g_a), u["g_wrg"].reshape(N_DEV, 2 * RNN_BLOCK, RNN_BLOCK)], db_in, "a")
    g_b, = _mm_dwin(s["hb"], u["dz"], 1, sib_a[4])
    sib_b = siblings_start([shards(g_b)], db_in, "b")
    chp_a = chips_start(*_split_wait(_copies_siblings, *sib_a[:4], sib_b[4], "reduce_siblings_wait_a"), db_in, "a")
    g_proj, g_land = _split_wait(_copies_direct(False), *g_pending[:4], chp_a[4], "reduce_proj_wait")
    for i, name in enumerate(proj):
        res = _adamw_direct(g_proj[i], g_land[i], me_idx, w[name].reshape(two_d(name)), m[name].reshape(two_d(name)),
                            v[name].reshape(two_d(name)), "adamw_" + name)
        big[name] = tuple(r.reshape(w[name].shape) for r in res)
    chp_b = chips_start(*_split_wait(_copies_siblings, *sib_b[:4], big["w_attn_out"][3], "reduce_siblings_wait_b"),
                        db_in, "b")
    u.update(_step_input_grad(u["dz"], w_full, chp_b[4], t["du32"], x, smallw, p))
    u["db_in"] = db_in

    loc = {**t, **u}
    sm, sm2 = _pack_small(loc["st_emb"], loc["vec_rnn"], loc["st_out"], loc["dsr"], loc["db_in"], loc["dmeta"])
    sm, sm2 = _small_allreduce(sm, sm2)
    sm2_mine = lax.dynamic_index_in_dim(sm2, _dev(px, py, pc), 0, keepdims=False)
    two = lambda name, t: t.reshape(_SMALL_2D.get(name, (1, D)))
    small = _small_update(sm, sm2_mine, {k: (two(k, w[k]), two(k, m[k]), two(k, v[k])) for k in _SMALL_NAMES})

    parts_a, r2_a = _split_wait(_copies_chips, *chp_a[:4], small["b_in"][3], "reduce_chips_wait_a")
    parts_b, r2_b = _split_wait(_copies_chips, *chp_b[:4], small["b_in"][2], "reduce_chips_wait_b")
    res = _adamw_big([(parts_a[0], r2_a[0]), (parts_b[0], r2_b[0])], q_idx, w_in_t(w["w_in"]), w_in_t(m["w_in"]),
                     w_in_t(v["w_in"]), "adamw_w_in")
    big["w_in"] = tuple(jnp.swapaxes(r.reshape(1, SHARD_IN, D), 1, 2) for r in res)
    for i, name in enumerate(("w_ra", "w_ri")):
        sq = (RNN_BLOCK, RNN_BLOCK)
        res = _adamw_big([(parts_a[1], r2_a[1])], q_idx, w[name].reshape(sq), m[name].reshape(sq), v[name].reshape(sq),
                         "adamw_" + name, row_off=i)
        big[name] = tuple(r.reshape(w[name].shape) for r in res)
    res = dict(big)
    for k in _SMALL_NAMES:
        res[k] = tuple(t.reshape(w[k].shape) for t in small[k])

    outs = [loss, loc["grad_x"]]
    for j in range(4):
        outs += [res[k][j] for k in _WEIGHTS]
    return tuple(outs)
```

```python
import functools

import jax
import jax.numpy as jnp
from jax import lax
from jax.experimental import pallas as pl
from jax.experimental.pallas import tpu as pltpu

f32, bf16 = jnp.float32, jnp.bfloat16
SDS = jax.ShapeDtypeStruct

N_DEV = 8
D = 2048
N_META = 16
BLK = 128
ROW0 = BLK - N_META
N_RNN_BLOCKS = 8
RNN_BLOCK = D // N_RNN_BLOCKS
CONV_WIDTH = 4
LRU_C = 8.0
HEAD_DIM = 64
N_KV = 4
GROUP = 8
HALF = HEAD_DIM // 2
ROPE_THETA = 10000.0
NEG_INF = -1e30
LN_EPS = 1e-5
ALPHA = 2.0 ** 0.25
D_IN = 12800
SHARD_IN = D_IN // N_DEV
OFF_GR, OFF_Q, OFF_K, OFF_V, OFF_GA, OFF_G = 2048, 4096, 6144, 6400, 6656, 8704
ADAM_LR, ADAM_B1, ADAM_B2, ADAM_EPS, ADAM_WD, ADAM_STEP = 1e-3, 0.9, 0.999, 1e-8, 0.01, 10
VMEM_LIMIT_MB = 56
MESH = pl.DeviceIdType.MESH


def _cp(sem=None, vmem_mb=40):
    return pltpu.CompilerParams(dimension_semantics=sem, vmem_limit_bytes=vmem_mb * 2 ** 20)


def _row_chunk(m):
    best = 16
    for c in range(16, 641, 16):
        if m % c == 0:
            best = c
    return best


def _sigmoid(x):
    return 1.0 / (1.0 + jnp.exp(-x))


def _silu_and_grad(x):
    s = _sigmoid(x)
    return x * s, s * (1.0 + x * (1.0 - s))


def _log_sigmoid(x):
    return jnp.minimum(x, 0.0) - jnp.log1p(jnp.exp(-jnp.abs(x)))


def _ln_rows(v, g, b):
    mu = jnp.mean(v, axis=-1, keepdims=True)
    c = v - mu
    var = jnp.mean(c * c, axis=-1, keepdims=True)
    rstd = lax.rsqrt(var + LN_EPS)
    xhat = c * rstd
    return xhat * g + b, xhat, rstd


def _ln_rows_bwd(dy, g, xhat, rstd):
    dxh = dy * g
    m1 = jnp.mean(dxh, axis=-1, keepdims=True)
    m2 = jnp.mean(dxh * xhat, axis=-1, keepdims=True)
    return rstd * (dxh - m1 - xhat * m2)


def _colsum(v):
    return jnp.sum(v, axis=0, keepdims=True)


def _dot(a, b):
    return jnp.dot(a, b, preferred_element_type=f32)


def _dot_nt(a, b):
    return lax.dot_general(a, b, (((1,), (1,)), ((), ())), preferred_element_type=f32)


def _dot_tn(a, b):
    return lax.dot_general(a, b, (((0,), (0,)), ((), ())), preferred_element_type=f32)


def _meta_full(sw_ref):
    return jnp.concatenate([sw_ref[s, 0:N_META, :] for s in range(N_DEV)], axis=1)


def _ln_emb(x, smallw, g_e, b_e):
    seq = x.shape[1]
    rows = seq + BLK
    nb = rows // BLK

    def body(x_ref, sw_ref, g_ref, b_ref, h32_ref, hb_ref):
        i = pl.program_id(0)
        g, b = g_ref[...], b_ref[...]

        def emit(blk):
            h32_ref[...] = blk
            hb_ref[...] = blk.astype(bf16)

        @pl.when(i == 0)
        def _():
            hm = _ln_rows(_meta_full(sw_ref), g, b)[0]
            emit(jnp.concatenate([jnp.zeros((ROW0, D), f32), hm], axis=0))

        @pl.when(i > 0)
        def _():
            emit(_ln_rows(x_ref[0], g, b)[0])

    return pl.pallas_call(
        body, grid=(nb,),
        in_specs=[pl.BlockSpec((1, BLK, D), lambda i: (0, jnp.maximum(i - 1, 0), 0)),
                  pl.BlockSpec((N_DEV, 24, 256), lambda i: (0, 0, 0)),
                  pl.BlockSpec((1, D), lambda i: (0, 0)),
                  pl.BlockSpec((1, D), lambda i: (0, 0))],
        out_specs=[pl.BlockSpec((BLK, D), lambda i: (i, 0)),
                   pl.BlockSpec((BLK, D), lambda i: (i, 0))],
        out_shape=[SDS((rows, D), f32), SDS((rows, D), bf16)],
        name="ln_emb", compiler_params=_cp(("arbitrary",)),
    )(x, smallw, g_e, b_e)


def _ln_emb_bwd(dh, du32, x, smallw, g_e):
    seq = x.shape[1]
    rows = seq + BLK
    nb = rows // BLK

    def body(dh_ref, du_ref, x_ref, sw_ref, g_ref, gx_ref, dmeta_ref, st_ref):
        i = pl.program_id(0)
        g = g_ref[...]
        dht = dh_ref[...] + ALPHA * du_ref[...]

        @pl.when(i == 0)
        def _():
            v = jnp.concatenate([jnp.zeros((ROW0, D), f32), _meta_full(sw_ref)], axis=0)
            valid = lax.broadcasted_iota(jnp.int32, (BLK, 1), 0) >= ROW0
            d = jnp.where(valid, dht, 0.0)
            _, xhat, rstd = _ln_rows(v, g, 0.0)
            dv = _ln_rows_bwd(d, g, xhat, rstd)
            dmeta_ref[...] = dv[ROW0:, :]
            st_ref[...] = jnp.concatenate([_colsum(d * xhat), _colsum(d), jnp.zeros((6, D), f32)], axis=0)

        @pl.when(i > 0)
        def _():
            _, xhat, rstd = _ln_rows(x_ref[0], g, 0.0)
            gx_ref[0] = _ln_rows_bwd(dht, g, xhat, rstd)
            st_ref[0:1, :] += _colsum(dht * xhat)
            st_ref[1:2, :] += _colsum(dht)

    return pl.pallas_call(
        body, grid=(nb,),
        in_specs=[pl.BlockSpec((BLK, D), lambda i: (i, 0)),
                  pl.BlockSpec((BLK, D), lambda i: (i, 0)),
                  pl.BlockSpec((1, BLK, D), lambda i: (0, jnp.maximum(i - 1, 0), 0)),
                  pl.BlockSpec((N_DEV, 24, 256), lambda i: (0, 0, 0)),
                  pl.BlockSpec((1, D), lambda i: (0, 0))],
        out_specs=[pl.BlockSpec((1, BLK, D), lambda i: (0, jnp.maximum(i - 1, 0), 0)),
                   pl.BlockSpec((N_META, D), lambda i: (0, 0)),
                   pl.BlockSpec((8, D), lambda i: (0, 0))],
        out_shape=[SDS((1, seq, D), f32), SDS((N_META, D), f32), SDS((8, D), f32)],
        name="ln_emb_bwd", compiler_params=_cp(("arbitrary",)),
    )(dh, du32, x, smallw, g_e)


def _mm(a, b, *, name, nt=False, sel=None, bias=None, out_dtype=f32, tn=512):
    m, k = a.shape
    cm = _row_chunk(m)
    stacked = sel is not None
    n = D if stacked else (b.shape[0] if nt else b.shape[1])
    am = m
    if stacked and nt:
        b_spec = pl.BlockSpec((tn // 256, None, 256, D), lambda j, i: (j, sel, 0, 0))
    elif stacked:
        b_spec = pl.BlockSpec((N_DEV, None, 256, tn), lambda j, i: (0, sel, 0, j))
    elif nt:
        b_spec = pl.BlockSpec((tn, k), lambda j, i: (j, 0))
    else:
        b_spec = pl.BlockSpec((k, tn), lambda j, i: (0, j))
    in_specs = [pl.BlockSpec((am, k), lambda j, i: (i, 0)), b_spec]
    args = [a, b]
    if bias is not None:
        in_specs.append(pl.BlockSpec((1, tn), lambda j, i: (0, j)))
        args.append(bias)

    def body(*refs):
        a_ref, b_ref, o_ref = refs[0], refs[1], refs[-1]
        bm = b_ref[...]
        if stacked:
            bm = bm.reshape((tn, D) if nt else (D, tn))
        for c in range(am // cm):
            acc = (_dot_nt if nt else _dot)(a_ref[c * cm:(c + 1) * cm, :], bm)
            if bias is not None:
                acc = acc + refs[2][...]
            o_ref[c * cm:(c + 1) * cm, :] = acc.astype(out_dtype)

    return pl.pallas_call(
        body, grid=(n // tn, m // am), in_specs=in_specs,
        out_specs=pl.BlockSpec((am, tn), lambda j, i: (i, j)),
        out_shape=SDS((m, n), out_dtype), name=name, compiler_params=_cp(("arbitrary", "arbitrary"), 48),
    )(*args)


def _mm_dh(dz, w_t, after):
    rows = dz.shape[0]
    tk, tn = 2560, 512
    cm = _row_chunk(rows)

    def body(a_ref, w_ref, after_ref, o_ref):
        kk = pl.program_id(1)
        for c in range(rows // cm):
            acc = _dot(a_ref[c * cm:(c + 1) * cm, :], w_ref[...])

            @pl.when(kk == 0)
            def _():
                o_ref[c * cm:(c + 1) * cm, :] = acc

            @pl.when(kk > 0)
            def _():
                o_ref[c * cm:(c + 1) * cm, :] += acc

    return pl.pallas_call(
        body, grid=(D // tn, D_IN // tk),
        in_specs=[pl.BlockSpec((rows, tk), lambda j, kk: (0, kk)),
                  pl.BlockSpec((tk, tn), lambda j, kk: (kk, j)),
                  pl.BlockSpec(memory_space=pl.ANY)],
        out_specs=pl.BlockSpec((rows, tn), lambda j, kk: (0, j)),
        out_shape=SDS((rows, D), f32), name="mm_dh", compiler_params=_cp(("arbitrary", "arbitrary"), 48),
    )(dz, w_t, after)


W_IN_HALF = D // 2


def _mm_dwin(hb, dz, half, after):
    rows = dz.shape[0]
    tc = 640
    with_db = half == 0

    def body(dz_ref, h_ref, after_ref, o_ref, *db_ref):
        o_ref[...] = _dot_tn(dz_ref[...], h_ref[...]).astype(bf16)
        if with_db:
            def step(i, s):
                blk = dz_ref[pl.ds(pl.multiple_of(i * BLK, BLK), BLK), :].astype(f32)
                return s + blk.reshape(BLK // 8, 8, tc).sum(axis=0)
            s = lax.fori_loop(0, rows // BLK, step, jnp.zeros((8, tc), f32))
            db_ref[0][...] = jnp.broadcast_to(_colsum(s), (8, tc))

    out_specs = [pl.BlockSpec((tc, W_IN_HALF), lambda j: (j, 0))]
    out_shape = [SDS((D_IN, W_IN_HALF), bf16)]
    if with_db:
        out_specs.append(pl.BlockSpec((8, tc), lambda j: (0, j)))
        out_shape.append(SDS((8, D_IN), f32))
    return pl.pallas_call(
        body, grid=(D_IN // tc,),
        in_specs=[pl.BlockSpec((rows, tc), lambda j: (0, j)),
                  pl.BlockSpec((rows, W_IN_HALF), lambda j: (0, half)),
                  pl.BlockSpec(memory_space=pl.ANY)],
        out_specs=out_specs, out_shape=out_shape,
        name="mm_dwin_%d" % half, compiler_params=_cp(("arbitrary",), 48),
    )(dz, hb, after)


SCAN_ROWS = 32


def _scan8(a, b, reverse):
    idx = lax.broadcasted_iota(jnp.int32, a.shape, 0)
    for s in (1, 2, 4):
        sh = 8 - s if reverse else s
        a_sh, b_sh = pltpu.roll(a, sh, 0), pltpu.roll(b, sh, 0)
        m = (idx < 8 - s) if reverse else (idx >= s)
        b = jnp.where(m, a * b_sh + b, b)
        a = jnp.where(m, a * a_sh, a)
    return a, b


def _shift_rows(prev8, cur, k):
    ext = jnp.concatenate([prev8, cur], axis=0)
    return pltpu.roll(ext, k, 0)[8:, :]


def _gates(xc, w_ra, b_ra, w_ri, b_ri, ls):
    xb = xc.astype(bf16)
    r = _sigmoid(_dot(xb, w_ra) + b_ra)
    ig = _sigmoid(_dot(xb, w_ri) + b_ri)
    la = LRU_C * r * ls
    a = jnp.exp(la)
    mult = jnp.sqrt(jnp.tanh(-la) * (1.0 + a * a))
    return xb, r, ig, a, mult


_RNN_IN_SPECS = lambda rows: [
    pl.BlockSpec((1, 24, 256), lambda n: (n, 0, 0)),
    pl.BlockSpec((1, RNN_BLOCK), lambda n: (0, n)),
    pl.BlockSpec((N_DEV, 2, None, 32, RNN_BLOCK), lambda n: (0, 0, n, 0, 0)),
    pl.BlockSpec((1, RNN_BLOCK), lambda n: (0, n)),
    pl.BlockSpec((1, RNN_BLOCK), lambda n: (0, n)),
    pl.BlockSpec((1, RNN_BLOCK), lambda n: (0, n)),
]


def _rnn_fwd(z, smallw, conv_b, wrg, b_ra, b_ri, lam):
    rows = z.shape[0]
    nb = rows // BLK
    col = lambda off: pl.BlockSpec((rows, RNN_BLOCK), lambda n: (0, off // RNN_BLOCK + n))

    def body(xr_ref, gr_ref, sw_ref, cb_ref, w_ref, bra_ref, bri_ref, lam_ref, xc_ref, hr_ref, ya_ref, yat_ref, a_s):
        cw = sw_ref[0, N_META:24, :]
        cb = cb_ref[...]
        w_ra = w_ref[:, 0].reshape(RNN_BLOCK, RNN_BLOCK)
        w_ri = w_ref[:, 1].reshape(RNN_BLOCK, RNN_BLOCK)
        b_ra_v, b_ri_v = bra_ref[...], bri_ref[...]
        ls = _log_sigmoid(lam_ref[...])
        rid = lax.broadcasted_iota(jnp.int32, (BLK, 1), 0)

        def blk_step(i, carry):
            r0 = pl.multiple_of(i * BLK, BLK)
            grow = rid + r0
            valid = grow >= ROW0
            cur = jnp.where(valid, xr_ref[pl.ds(r0, BLK), :], 0.0)
            prev8 = xr_ref[pl.ds(pl.multiple_of(jnp.maximum(r0 - 8, 0), 8), 8), :] * (i > 0).astype(f32)
            xc = cb + cw[0:1] * cur
            for k in range(1, CONV_WIDTH):
                xc = xc + cw[k:k + 1] * _shift_rows(prev8, cur, k)
            xc_ref[pl.ds(r0, BLK), :] = xc
            _, _, ig, a, mult = _gates(xc, w_ra, b_ra_v, w_ri, b_ri_v, ls)
            mult = jnp.where(grow == ROW0, 1.0, mult)
            a_s[pl.ds(r0, BLK), :] = a
            hr_ref[pl.ds(r0, BLK), :] = jnp.where(valid, mult * ig * xc, 0.0)
            return carry

        lax.fori_loop(0, nb, blk_step, 0)

        def scan_step(j, carry):
            r0 = pl.multiple_of(j * SCAN_ROWS, SCAN_ROWS)
            tiles = [_scan8(a_s[pl.ds(r0 + 8 * k, 8), :], hr_ref[pl.ds(r0 + 8 * k, 8), :], False)
                     for k in range(SCAN_ROWS // 8)]
            for k, (a, b) in enumerate(tiles):
                h = b + a * carry
                hr_ref[pl.ds(r0 + 8 * k, 8), :] = h
                carry = jnp.broadcast_to(h[7:8, :], (8, RNN_BLOCK))
            return carry

        lax.fori_loop(0, rows // SCAN_ROWS, scan_step, jnp.zeros((8, RNN_BLOCK), f32))

        def gate_step(i, carry):
            r0 = pl.multiple_of(i * BLK, BLK)
            ya_ref[pl.ds(r0, BLK), :] = (hr_ref[pl.ds(r0, BLK), :]
                                         * _silu_and_grad(gr_ref[pl.ds(r0, BLK), :])[0]).astype(bf16)
            return carry

        lax.fori_loop(0, nb, gate_step, 0)
        yat_ref[...] = ya_ref[...].astype(f32).T.astype(bf16)

    return pl.pallas_call(
        body, grid=(N_RNN_BLOCKS,),
        in_specs=[col(0), col(OFF_GR)] + _RNN_IN_SPECS(rows),
        out_specs=[pl.BlockSpec((rows, RNN_BLOCK), lambda n: (0, n))] * 3
                  + [pl.BlockSpec((RNN_BLOCK, rows), lambda n: (n, 0))],
        out_shape=[SDS((rows, D), f32), SDS((rows, D), f32), SDS((rows, D), bf16), SDS((D, rows), bf16)],
        scratch_shapes=[pltpu.VMEM((rows, RNN_BLOCK), f32)],
        name="rnn_fwd", compiler_params=_cp(("arbitrary",)),
    )(z, z, smallw, conv_b, wrg, b_ra, b_ri, lam)


def _rnn_bwd(dya, hr, xc, z, smallw, conv_b, wrg, b_ra, b_ri, lam):
    rows = z.shape[0]
    nb = rows // BLK
    col = lambda off: pl.BlockSpec((rows, RNN_BLOCK), lambda n: (0, off // RNN_BLOCK + n))
    blk = pl.BlockSpec((rows, RNN_BLOCK), lambda n: (0, n))

    def body(dya_ref, hr_ref, xc_ref, xr_ref, gr_ref, sw_ref, cb_ref, w_ref, bra_ref, bri_ref, lam_ref,
             dxr_ref, dgr_ref, dw_ref, vec_ref, a_s, lam_s, dxc_s, r_s, ig_s, mult_s, dw_s):
        cw = sw_ref[0, N_META:24, :]
        w_ra = w_ref[:, 0].reshape(RNN_BLOCK, RNN_BLOCK)
        w_ri = w_ref[:, 1].reshape(RNN_BLOCK, RNN_BLOCK)
        b_ra_v, b_ri_v = bra_ref[...], bri_ref[...]
        lam_v = lam_ref[...]
        ls = _log_sigmoid(lam_v)
        rid = lax.broadcasted_iota(jnp.int32, (BLK, 1), 0)
        zrow = jnp.zeros((1, RNN_BLOCK), f32)

        def p1(i, carry):
            r0 = pl.multiple_of(i * BLK, BLK)
            sl = pl.ds(r0, BLK)
            _, r, ig, a, mult = _gates(xc_ref[sl, :], w_ra, b_ra_v, w_ri, b_ri_v, ls)
            a_s[sl, :] = a
            r_s[sl, :] = r
            ig_s[sl, :] = ig
            mult_s[sl, :] = mult
            sg, dsg = _silu_and_grad(gr_ref[sl, :])
            d = dya_ref[sl, :]
            lam_s[sl, :] = d * sg
            dgr_ref[sl, :] = (d * hr_ref[sl, :] * dsg).astype(bf16)
            return carry

        lax.fori_loop(0, nb, p1, 0)

        def p2(jj, carry):
            r0 = pl.multiple_of((rows // SCAN_ROWS - 1 - jj) * SCAN_ROWS, SCAN_ROWS)
            idx = lax.broadcasted_iota(jnp.int32, (8, RNN_BLOCK), 0)
            tiles = []
            for k in range(SCAN_ROWS // 8):
                sl = pl.ds(r0 + 8 * k, 8)
                a, g = a_s[sl, :], lam_s[sl, :]
                tiles.append((g, *_scan8(a, a * g, True)))
            for k in reversed(range(SCAN_ROWS // 8)):
                g, ca, cb_ = tiles[k]
                mu = cb_ + ca * carry
                lam_s[pl.ds(r0 + 8 * k, 8), :] = g + jnp.where(idx < 7, pltpu.roll(mu, 7, 0), carry)
                carry = jnp.broadcast_to(mu[0:1, :], (8, RNN_BLOCK))
            return carry

        lax.fori_loop(0, rows // SCAN_ROWS, p2, jnp.zeros((8, RNN_BLOCK), f32))

        dw_s[...] = jnp.zeros_like(dw_s)

        def p3(i, carry):
            d_bra, d_bri, d_ls = carry
            r0 = pl.multiple_of(i * BLK, BLK)
            sl = pl.ds(r0, BLK)
            grow = rid + r0
            valid = grow >= ROW0
            first = grow == ROW0
            xcv = xc_ref[sl, :]
            xb = xcv.astype(bf16)
            r, ig, a = r_s[sl, :], ig_s[sl, :], a_s[sl, :]
            mult = jnp.where(first, 1.0, mult_s[sl, :])
            lam_t = lam_s[sl, :]
            du = jnp.where(valid, lam_t, 0.0)
            hprev = _shift_rows(hr_ref[pl.ds(pl.multiple_of(jnp.maximum(r0 - 8, 0), 8), 8), :] * (i > 0).astype(f32), hr_ref[sl, :], 1)
            da = lam_t * hprev
            dmult = jnp.where(first, 0.0, du * ig * xcv)
            di = du * mult * xcv
            dxc = du * mult * ig
            ratio = jnp.where(valid & jnp.logical_not(first), a * a / mult, 0.0)
            dla = da * a - dmult * ratio
            dpr = (dla * (LRU_C * ls)) * r * (1.0 - r)
            dpi = di * ig * (1.0 - ig)
            dprb, dpib = dpr.astype(bf16), dpi.astype(bf16)
            dw_s[0] += _dot_tn(xb, dprb)
            dw_s[1] += _dot_tn(xb, dpib)
            dxc_s[sl, :] = dxc + _dot_nt(dprb, w_ra) + _dot_nt(dpib, w_ri)
            return d_bra + _colsum(dpr), d_bri + _colsum(dpi), d_ls + _colsum(dla * (LRU_C * r))

        d_bra, d_bri, d_ls = lax.fori_loop(0, nb, p3, (zrow, zrow, zrow))

        def p4(i, carry):
            d_cb, d_w0, d_w1, d_w2, d_w3 = carry
            r0 = pl.multiple_of(i * BLK, BLK)
            sl = pl.ds(r0, BLK)
            grow = rid + r0
            valid = grow >= ROW0
            dxc = dxc_s[sl, :]
            nxt = dxc_s[pl.ds(pl.multiple_of(jnp.minimum(r0 + BLK, rows - 8), 8), 8), :] * (i < nb - 1).astype(f32)
            ext = jnp.concatenate([dxc, nxt], axis=0)
            dxr = cw[0:1] * dxc
            for k in range(1, CONV_WIDTH):
                dxr = dxr + cw[k:k + 1] * pltpu.roll(ext, BLK + 8 - k, 0)[:BLK, :]
            dxr_ref[sl, :] = jnp.where(valid, dxr, 0.0).astype(bf16)
            cur = jnp.where(valid, xr_ref[sl, :], 0.0)
            prev8 = xr_ref[pl.ds(pl.multiple_of(jnp.maximum(r0 - 8, 0), 8), 8), :] * (i > 0).astype(f32)
            dws = [d_w0 + _colsum(dxc * cur)]
            for k, acc in ((1, d_w1), (2, d_w2), (3, d_w3)):
                dws.append(acc + _colsum(dxc * _shift_rows(prev8, cur, k)))
            return (d_cb + _colsum(dxc), *dws)

        d_cb, d_w0, d_w1, d_w2, d_w3 = lax.fori_loop(0, nb, p4, (zrow,) * 5)

        d_lam = d_ls * _sigmoid(-lam_v)
        vec_ref[...] = jnp.concatenate([d_bra, d_bri, d_lam, d_cb, d_w0, d_w1, d_w2, d_w3], axis=0)
        dw_ref[:, 0] = dw_s[0].astype(bf16).reshape(N_DEV, 32, RNN_BLOCK)
        dw_ref[:, 1] = dw_s[1].astype(bf16).reshape(N_DEV, 32, RNN_BLOCK)

    return pl.pallas_call(
        body, grid=(N_RNN_BLOCKS,),
        in_specs=[blk, blk, blk, col(0), col(OFF_GR)] + _RNN_IN_SPECS(rows),
        out_specs=[blk, blk,
                   pl.BlockSpec((N_DEV, 2, None, 32, RNN_BLOCK), lambda n: (0, 0, n, 0, 0)),
                   pl.BlockSpec((8, RNN_BLOCK), lambda n: (0, n))],
        out_shape=[SDS((rows, D), bf16), SDS((rows, D), bf16),
                   SDS((N_DEV, 2, N_RNN_BLOCKS, 32, RNN_BLOCK), bf16), SDS((8, D), f32)],
        scratch_shapes=[pltpu.VMEM((rows, RNN_BLOCK), f32)] * 6 + [pltpu.VMEM((2, RNN_BLOCK, RNN_BLOCK), f32)],
        name="rnn_bwd", compiler_params=_cp(("arbitrary",), 48),
    )(dya, hr, xc, z, z, smallw, conv_b, wrg, b_ra, b_ri, lam)


def _rope_tables(rows):
    half = jnp.arange(HALF, dtype=f32)
    inv = ROPE_THETA ** (-half / HALF)
    pos = (jnp.arange(rows) - ROW0).astype(f32)
    ang = pos[:, None] * inv[None, :]
    cos, sin = jnp.cos(ang), jnp.sin(ang)
    cos128 = jnp.concatenate([cos, cos, cos, cos], axis=1)
    sin128 = jnp.concatenate([-sin, sin, -sin, sin], axis=1)
    return cos128, sin128


def _rope128(x, cos128, sin128):
    lane = lax.broadcasted_iota(jnp.int32, x.shape, 1)
    swapped = jnp.where(lane % HEAD_DIM < HALF, pltpu.roll(x, 128 - HALF, 1), pltpu.roll(x, HALF, 1))
    return x * cos128 + swapped * sin128


def _qkv_prep(z, cos128, sin128):
    rows = z.shape[0]

    def body(q_ref, kv_ref, c_ref, s_ref, qo_ref, ko_ref, vo_ref):
        c, s = c_ref[...], s_ref[...]
        for g in range(D // 128):
            qo_ref[:, g * 128:(g + 1) * 128] = (_rope128(q_ref[:, g * 128:(g + 1) * 128], c, s)
                                                * (HEAD_DIM ** -0.5)).astype(bf16)
        for g in range(2):
            kr = _rope128(kv_ref[:, g * 128:(g + 1) * 128], c, s)
            for j in range(2):
                ko_ref[2 * g + j] = kr[:, j * HEAD_DIM:(j + 1) * HEAD_DIM].astype(bf16)
        for h in range(N_KV):
            vo_ref[h] = kv_ref[:, 256 + h * HEAD_DIM:256 + (h + 1) * HEAD_DIM].astype(bf16)

    return pl.pallas_call(
        body, grid=(rows // BLK,),
        in_specs=[pl.BlockSpec((BLK, D), lambda i: (i, OFF_Q // D)),
                  pl.BlockSpec((BLK, 512), lambda i: (i, OFF_K // 512)),
                  pl.BlockSpec((BLK, 128), lambda i: (i, 0)),
                  pl.BlockSpec((BLK, 128), lambda i: (i, 0))],
        out_specs=[pl.BlockSpec((BLK, D), lambda i: (i, 0)),
                   pl.BlockSpec((N_KV, BLK, HEAD_DIM), lambda i: (0, i, 0)),
                   pl.BlockSpec((N_KV, BLK, HEAD_DIM), lambda i: (0, i, 0))],
        out_shape=[SDS((rows, D), bf16), SDS((N_KV, rows, HEAD_DIM), bf16), SDS((N_KV, rows, HEAD_DIM), bf16)],
        name="qkv_prep", compiler_params=_cp(("arbitrary",)),
    )(z, z, cos128, sin128)


def _attn_mask(n):
    qi = n * BLK + lax.broadcasted_iota(jnp.int32, (BLK, 2 * BLK + N_META), 0)
    c = lax.broadcasted_iota(jnp.int32, (BLK, 2 * BLK + N_META), 1)
    jb = (n - 1) * BLK + c
    band = (jb >= BLK) & (jb <= qi) & (qi - jb < BLK)
    meta = (ROW0 + c - 2 * BLK) <= qi
    return ((c < 2 * BLK) & band) | ((c >= 2 * BLK) & meta)


N_KEYS = 2 * BLK + N_META


def _stack_heads(t):
    return jnp.concatenate([t[:, g * HEAD_DIM:(g + 1) * HEAD_DIM] for g in range(GROUP)], axis=0)


def _sink_column(sink_ref, h):
    g = lax.broadcasted_iota(jnp.int32, (GROUP, 1, 1), 0)
    col = jnp.zeros((GROUP, 1, 1), f32)
    for j in range(GROUP):
        col = jnp.where(g == j, sink_ref[h * GROUP + j], col)
    return col


def _kv_specs(last):
    cl = lambda n: jnp.minimum(n, last)
    return [pl.BlockSpec((None, N_META, HEAD_DIM), lambda h, n: (h, ROW0 // N_META, 0)),
            pl.BlockSpec((None, BLK, HEAD_DIM), lambda h, n: (h, jnp.maximum(cl(n) - 1, 0), 0)),
            pl.BlockSpec((None, BLK, HEAD_DIM), lambda h, n: (h, cl(n), 0))]


def _attn_fwd(q_r, k_r, v_b, z, sinks):
    rows = q_r.shape[0]
    nb = rows // BLK

    def body(sink_ref, q_ref, km_ref, kp_ref, kc_ref, vm_ref, vp_ref, vc_ref, ga_ref, o_ref, yb_ref, ybt_ref, lse_ref):
        h, n = pl.program_id(0), pl.program_id(1)
        kk = jnp.concatenate([kp_ref[...], kc_ref[...], km_ref[...]], axis=0)
        vv = jnp.concatenate([vp_ref[...], vc_ref[...], vm_ref[...]], axis=0)
        q2 = _stack_heads(q_ref[...])
        s = jnp.where(_attn_mask(n)[None], _dot_nt(q2, kk).reshape(GROUP, BLK, N_KEYS), NEG_INF)
        sink = _sink_column(sink_ref, h)
        m = jnp.maximum(jnp.max(s, axis=-1, keepdims=True), sink)
        p = jnp.exp(s - m)
        den = jnp.sum(p, axis=-1, keepdims=True) + jnp.exp(sink - m)
        o2 = _dot((p / den).astype(bf16).reshape(GROUP * BLK, N_KEYS), vv)
        lse = m + jnp.log(den)
        for g in range(GROUP):
            o_ref[:, g * HEAD_DIM:(g + 1) * HEAD_DIM] = o2[g * BLK:(g + 1) * BLK]
            lse_ref[:, g:g + 1] = lse[g]
        yb = o_ref[...] * _silu_and_grad(ga_ref[...])[0]
        yb_ref[...] = yb.astype(bf16)
        ybt_ref[...] = yb.T.astype(bf16)

    tile = pl.BlockSpec((BLK, 512), lambda h, n: (n, h))
    return pl.pallas_call(
        body, grid=(N_KV, nb),
        in_specs=[pl.BlockSpec(memory_space=pltpu.SMEM), tile] + _kv_specs(nb - 1) + _kv_specs(nb - 1)
                 + [pl.BlockSpec((BLK, 512), lambda h, n: (n, OFF_GA // 512 + h))],
        out_specs=[tile, tile, pl.BlockSpec((512, BLK), lambda h, n: (h, n)),
                   pl.BlockSpec((None, BLK, GROUP), lambda h, n: (h, n, 0))],
        out_shape=[SDS((rows, D), f32), SDS((rows, D), bf16), SDS((D, rows), bf16),
                   SDS((N_KV, rows, GROUP), f32)],
        name="attn_fwd", compiler_params=_cp(("arbitrary", "arbitrary")),
    )(sinks, q_r, k_r, k_r, k_r, v_b, v_b, v_b, z)


def _attn_bwd(dyb, o32, lse, q_r, k_r, v_b, z, sinks):
    rows = q_r.shape[0]
    nb = rows // BLK
    cl = lambda n: jnp.minimum(n, nb - 1)

    def body(sink_ref, dyb_ref, o_ref, lse_ref, q_ref, km_ref, kp_ref, kc_ref, vm_ref, vp_ref, vc_ref, ga_ref,
             dq_ref, dga_ref, dk_ref, dv_ref, dkm_ref, dvm_ref, dsr_ref, ck_s, cv_s):
        h, n = pl.program_id(0), pl.program_id(1)

        @pl.when(n == 0)
        def _():
            dkm_ref[...] = jnp.zeros_like(dkm_ref)
            dvm_ref[...] = jnp.zeros_like(dvm_ref)
            ck_s[...] = jnp.zeros_like(ck_s)
            cv_s[...] = jnp.zeros_like(cv_s)

        @pl.when(n < nb)
        def _():
            kk = jnp.concatenate([kp_ref[...], kc_ref[...], km_ref[...]], axis=0)
            vv = jnp.concatenate([vp_ref[...], vc_ref[...], vm_ref[...]], axis=0)
            sg, dsg = _silu_and_grad(ga_ref[...])
            dyb_v = dyb_ref[...]
            o_v = o_ref[...]
            dga_ref[...] = (dyb_v * o_v * dsg).astype(bf16)
            q2 = _stack_heads(q_ref[...])
            do2 = _stack_heads(dyb_v * sg)
            lse_v = lse_ref[...]
            lse = jnp.concatenate([lse_v[:, g:g + 1] for g in range(GROUP)], axis=0).reshape(GROUP, BLK, 1)
            delta = jnp.sum(do2 * _stack_heads(o_v), axis=-1, keepdims=True).reshape(GROUP, BLK, 1)
            s = jnp.where(_attn_mask(n)[None], _dot_nt(q2, kk).reshape(GROUP, BLK, N_KEYS), NEG_INF)
            p = jnp.exp(s - lse)
            do2b = do2.astype(bf16)
            ds = (p * (_dot_nt(do2b, vv).reshape(GROUP, BLK, N_KEYS) - delta)).astype(bf16)
            ds = ds.reshape(GROUP * BLK, N_KEYS)
            dsr = -jnp.exp(_sink_column(sink_ref, h) - lse) * delta
            dq2 = _dot(ds, kk)
            for g in range(GROUP):
                dq_ref[:, g * HEAD_DIM:(g + 1) * HEAD_DIM] = dq2[g * BLK:(g + 1) * BLK]
                dsr_ref[:, g:g + 1] = dsr[g]
            dkk = _dot_tn(ds, q2)
            dvv = _dot_tn(p.astype(bf16).reshape(GROUP * BLK, N_KEYS), do2b)
            dk_ref[...] = ck_s[...] + dkk[:BLK]
            dv_ref[...] = cv_s[...] + dvv[:BLK]
            ck_s[...] = dkk[BLK:2 * BLK]
            cv_s[...] = dvv[BLK:2 * BLK]
            dkm_ref[...] += dkk[2 * BLK:]
            dvm_ref[...] += dvv[2 * BLK:]

        @pl.when(n == nb)
        def _():
            dk_ref[...] = ck_s[...]
            dv_ref[...] = cv_s[...]

    tile = pl.BlockSpec((BLK, 512), lambda h, n: (cl(n), h))
    kvout = pl.BlockSpec((None, BLK, HEAD_DIM), lambda h, n: (h, jnp.maximum(n - 1, 0), 0))
    mout = pl.BlockSpec((None, N_META, HEAD_DIM), lambda h, n: (h, 0, 0))
    stat = pl.BlockSpec((None, BLK, GROUP), lambda h, n: (h, cl(n), 0))
    return pl.pallas_call(
        body, grid=(N_KV, nb + 1),
        in_specs=[pl.BlockSpec(memory_space=pltpu.SMEM), tile, tile, stat, tile] + _kv_specs(nb - 1)
                 + _kv_specs(nb - 1) + [pl.BlockSpec((BLK, 512), lambda h, n: (cl(n), OFF_GA // 512 + h))],
        out_specs=[tile, tile, kvout, kvout, mout, mout, stat],
        out_shape=[SDS((rows, D), f32), SDS((rows, D), bf16),
                   SDS((N_KV, rows, HEAD_DIM), f32), SDS((N_KV, rows, HEAD_DIM), f32),
                   SDS((N_KV, N_META, HEAD_DIM), f32), SDS((N_KV, N_META, HEAD_DIM), f32),
                   SDS((N_KV, rows, GROUP), f32)],
        scratch_shapes=[pltpu.VMEM((BLK, HEAD_DIM), f32), pltpu.VMEM((BLK, HEAD_DIM), f32)],
        name="attn_bwd", compiler_params=_cp(("arbitrary", "arbitrary")),
    )(sinks, dyb, o32, lse, q_r, k_r, k_r, k_r, v_b, v_b, v_b, z)


def _qkv_finish(dq, dk, dv, dkm, dvm, cos128, sin128):
    rows = dq.shape[0]

    def body(dq_ref, dk_ref, dv_ref, dkm_ref, dvm_ref, c_ref, s_ref, oq_ref, okv_ref):
        first = (pl.program_id(0) == 0).astype(f32)
        c, s = c_ref[...], -s_ref[...]
        for g in range(D // 128):
            oq_ref[:, g * 128:(g + 1) * 128] = (_rope128(dq_ref[:, g * 128:(g + 1) * 128], c, s)
                                                * (HEAD_DIM ** -0.5)).astype(bf16)
        pad = jnp.zeros((ROW0, HEAD_DIM), f32)
        ks = [dk_ref[h] + first * jnp.concatenate([pad, dkm_ref[h]], axis=0) for h in range(N_KV)]
        vs = [dv_ref[h] + first * jnp.concatenate([pad, dvm_ref[h]], axis=0) for h in range(N_KV)]
        for g in range(2):
            kp = jnp.concatenate([ks[2 * g], ks[2 * g + 1]], axis=1)
            okv_ref[:, g * 128:(g + 1) * 128] = _rope128(kp, c, s).astype(bf16)
            okv_ref[:, 256 + g * 128:256 + (g + 1) * 128] = jnp.concatenate([vs[2 * g], vs[2 * g + 1]], axis=1).astype(bf16)

    kv = pl.BlockSpec((N_KV, BLK, HEAD_DIM), lambda i: (0, i, 0))
    mt = pl.BlockSpec((N_KV, N_META, HEAD_DIM), lambda i: (0, 0, 0))
    return pl.pallas_call(
        body, grid=(rows // BLK,),
        in_specs=[pl.BlockSpec((BLK, D), lambda i: (i, 0)), kv, kv, mt, mt,
                  pl.BlockSpec((BLK, 128), lambda i: (i, 0)), pl.BlockSpec((BLK, 128), lambda i: (i, 0))],
        out_specs=[pl.BlockSpec((BLK, D), lambda i: (i, 0)), pl.BlockSpec((BLK, 512), lambda i: (i, 0))],
        out_shape=[SDS((rows, D), bf16), SDS((rows, 512), bf16)],
        name="qkv_finish", compiler_params=_cp(("arbitrary",)),
    )(dq, dk, dv, dkm, dvm, cos128, sin128)


_TW = 512


def _mix_specs(rows):
    tr = _row_chunk(rows)
    tile = pl.BlockSpec((tr, _TW), lambda i, j: (i, j))
    ga = pl.BlockSpec((tr, _TW), lambda i, j: (i, OFF_G // _TW + j))
    gb = pl.BlockSpec((tr, _TW), lambda i, j: (i, (OFF_G + D) // _TW + j))
    return (rows // tr, D // _TW), tile, ga, gb


def _mix_fwd(y_a, y_b, z):
    rows = y_a.shape[0]
    tw = 256
    col = lambda off: pl.BlockSpec((rows, tw), lambda j: (0, off // tw + j))

    def body(ya_ref, yb_ref, ga_ref, gb_ref, o_ref, ot_ref):
        mixed = _sigmoid(ga_ref[...]) * ya_ref[...] + _sigmoid(gb_ref[...]) * yb_ref[...]
        o_ref[...] = mixed.astype(bf16)
        ot_ref[...] = mixed.T.astype(bf16)

    return pl.pallas_call(
        body, grid=(D // tw,), in_specs=[col(0), col(0), col(OFF_G), col(OFF_G + D)],
        out_specs=[col(0), pl.BlockSpec((tw, rows), lambda j: (j, 0))],
        out_shape=[SDS((rows, D), bf16), SDS((D, rows), bf16)],
        name="mix_fwd", compiler_params=_cp(("arbitrary",)),
    )(y_a, y_b, z, z)


def _mix_bwd(dmixed, y_a, y_b, z):
    rows = y_a.shape[0]
    grid, _mix_tile, _mix_ga, _mix_gb = _mix_specs(rows)

    def body(dm_ref, ya_ref, yb_ref, ga_ref, gb_ref, dya_ref, dyb_ref, dga_ref, dgb_ref):
        dm = dm_ref[...]
        sa, sb = _sigmoid(ga_ref[...]), _sigmoid(gb_ref[...])
        dya_ref[...] = (dm * sa).astype(bf16)
        dyb_ref[...] = (dm * sb).astype(bf16)
        dga_ref[...] = (dm * ya_ref[...] * sa * (1.0 - sa)).astype(bf16)
        dgb_ref[...] = (dm * yb_ref[...] * sb * (1.0 - sb)).astype(bf16)

    return pl.pallas_call(
        body, grid=grid, in_specs=[_mix_tile, _mix_tile, _mix_tile, _mix_ga, _mix_gb],
        out_specs=[_mix_tile] * 4, out_shape=[SDS((rows, D), bf16)] * 4,
        name="mix_bwd", compiler_params=_cp(("arbitrary", "arbitrary")),
    )(dmixed, y_a, y_b, z, z)


def _final_ln(out32, h32, tgt, ln_g, ln_b):
    rows = out32.shape[0]

    def body(o_ref, h_ref, t_ref, g_ref, b_ref, du_ref, dub_ref, st_ref):
        i = pl.program_id(0)
        g = g_ref[...]
        y, xhat, rstd = _ln_rows(ALPHA * h_ref[...] + o_ref[...], g, b_ref[...])
        e = jnp.where(i > 0, y - t_ref[0], 0.0)
        dy = e * (1.0 / D)
        du = _ln_rows_bwd(dy, g, xhat, rstd)
        du_ref[...] = du
        dub_ref[...] = du.astype(bf16)
        st = jnp.concatenate([_colsum(dy * xhat), _colsum(dy), _colsum(du), _colsum(e * e) * (0.5 / D),
                              jnp.zeros((4, D), f32)], axis=0)

        @pl.when(i == 0)
        def _():
            st_ref[...] = st

        @pl.when(i > 0)
        def _():
            st_ref[...] += st

    row = pl.BlockSpec((BLK, D), lambda i: (i, 0))
    vec = pl.BlockSpec((1, D), lambda i: (0, 0))
    return pl.pallas_call(
        body, grid=(rows // BLK,),
        in_specs=[row, row, pl.BlockSpec((1, BLK, D), lambda i: (0, jnp.maximum(i - 1, 0), 0)), vec, vec],
        out_specs=[row, row, pl.BlockSpec((8, D), lambda i: (0, 0))],
        out_shape=[SDS((rows, D), f32), SDS((rows, D), bf16), SDS((8, D), f32)],
        name="final_ln", compiler_params=_cp(("arbitrary",)),
    )(out32, h32, tgt, ln_g, ln_b)


def _assemble_dz(dxr, dgr, dq, dkv, dga, dma, dmb):
    rows = dxr.shape[0]
    parts = [(dxr, D), (dgr, D), (dq, D), (dkv, 512), (dga, D), (dma, D), (dmb, D)]

    def body(*refs):
        o_ref = refs[-1]
        off = 0
        for r, (_, w) in zip(refs[:-1], parts):
            o_ref[:, off:off + w] = r[...]
            off += w

    return pl.pallas_call(
        body, grid=(rows // BLK,),
        in_specs=[pl.BlockSpec((BLK, w), lambda i: (i, 0)) for _, w in parts],
        out_specs=pl.BlockSpec((BLK, D_IN), lambda i: (i, 0)),
        out_shape=SDS((rows, D_IN), bf16), name="assemble_dz", compiler_params=_cp(("arbitrary",)),
    )(*[p for p, _ in parts])


def _step_rnn(h32, hb, z, wrg, smallw, p, zero):
    rows = z.shape[0]
    cos128, sin128 = _rope_tables(rows)
    cos128 = cos128 + zero
    xc, hr, ya, ya_t = _rnn_fwd(z, smallw, p["conv_b"] + zero, wrg, p["b_ra"], p["b_ri"], p["lru_lambda"])
    q_r, k_r, v_b = _qkv_prep(z, cos128, sin128)
    return dict(cos128=cos128, sin128=sin128, h32=h32, hb=hb, z=z, xc=xc, hr=hr, ya=ya, ya_t=ya_t,
                q_r=q_r, k_r=k_r, v_b=v_b)


def _step_attn(s, p, zero):
    sinks = p["sinks"].reshape(N_KV * GROUP) + zero[0]
    o32, yb, yb_t, lse = _attn_fwd(s["q_r"], s["k_r"], s["v_b"], s["z"], sinks)
    return dict(s, sinks=sinks, o32=o32, yb=yb, yb_t=yb_t, lse=lse)


def _step_merge(s, tgt, w3, p):
    ya, yb, z = s["ya"], s["yb"], s["z"]
    y_a = _mm(ya, w3, sel=0, name="mm_ya")
    y_b = _mm(yb, w3, sel=1, name="mm_yb")
    mixed, mixed_t = _mix_fwd(y_a, y_b, z)
    out32 = _mm(mixed, w3, sel=2, bias=p["b_o"], name="mm_out")
    du32, dub, st_out = _final_ln(out32, s["h32"], tgt, p["ln_g"], p["ln_b"])

    g_wo = _mm(mixed_t, dub, out_dtype=bf16, name="mm_dwo")
    dmixed = _mm(dub, w3, sel=2, nt=True, name="mm_dmixed")
    dya_b, dyb_b, dma, dmb = _mix_bwd(dmixed, y_a, y_b, z)
    g_wrnn = _mm(s["ya_t"], dya_b, out_dtype=bf16, name="mm_dwrnn")
    g_wattn = _mm(s["yb_t"], dyb_b, out_dtype=bf16, name="mm_dwattn")
    dya = _mm(dya_b, w3, sel=0, nt=True, name="mm_dya")
    dyb = _mm(dyb_b, w3, sel=1, nt=True, name="mm_dyb")
    return dict(du32=du32, st_out=st_out, dma=dma, dmb=dmb, dya=dya, dyb=dyb, g_wo=g_wo, g_wrnn=g_wrnn,
                g_wattn=g_wattn)


def _step_backward(s, t, wrg, smallw, p, conv_b):
    z = s["z"]
    dxr, dgr, g_wrg, vec_rnn = _rnn_bwd(t["dya"], s["hr"], s["xc"], z, smallw, conv_b, wrg, p["b_ra"], p["b_ri"],
                                        p["lru_lambda"])
    dq_r, dga, dk, dv, dkm, dvm, dsr = _attn_bwd(t["dyb"], s["o32"], s["lse"], s["q_r"], s["k_r"], s["v_b"], z,
                                                 s["sinks"])
    dq, dkv = _qkv_finish(dq_r, dk, dv, dkm, dvm, s["cos128"], s["sin128"])
    dz = _assemble_dz(dxr, dgr, dq, dkv, dga, t["dma"], t["dmb"])
    return dict(vec_rnn=vec_rnn, dsr=dsr, g_wrg=g_wrg, dz=dz)


def _step_input_grad(dz, w_t, after, du32, x, smallw, p):
    dh = _mm_dh(dz, w_t, after)
    grad_x, dmeta, st_emb = _ln_emb_bwd(dh, du32, x, smallw, p["ln_emb_g"])
    return dict(grad_x=grad_x, dmeta=dmeta, st_emb=st_emb)


_ANY = pl.BlockSpec(memory_space=pl.ANY)
_VMEM = pl.BlockSpec(memory_space=pltpu.VMEM)


def _place():
    x, y, c = lax.axis_index("x"), lax.axis_index("y"), lax.axis_index("c")
    return x, y, c


def _dev(px, py, pc):
    return 4 * px + 2 * py + pc


def _tile_rows(r):
    return max(t for t in range(16, 321, 16) if r % t == 0) if r > 320 else r


def _cast_w_in(w_in_t):
    tm = _tile_rows(SHARD_IN)

    def body(i_ref, o_ref):
        o_ref[...] = i_ref[...].astype(bf16)

    return pl.pallas_call(
        body, grid=(SHARD_IN // tm,),
        in_specs=[pl.BlockSpec((tm, D), lambda i: (i, 0))],
        out_specs=pl.BlockSpec((tm, D), lambda i: (i, 0)),
        out_shape=SDS((SHARD_IN, D), bf16), name="cast_w_in", compiler_params=_cp(("arbitrary",)),
    )(w_in_t)


def _cast_small(w_rnn_out, w_attn_out, w_o, w_ra, w_ri, meta, conv_w):
    def body(a_ref, b_ref, c_ref, ra_ref, ri_ref, m_ref, cw_ref, w3_ref, wrg_ref, sw_ref):
        w3_ref[0] = a_ref[0].astype(bf16)
        w3_ref[1] = b_ref[0].astype(bf16)
        w3_ref[2] = c_ref[0].astype(bf16)
        wrg_ref[0] = ra_ref[0].astype(bf16)
        wrg_ref[1] = ri_ref[0].astype(bf16)
        sw_ref[...] = jnp.concatenate([m_ref[...], cw_ref[0], jnp.zeros((4, 256), f32)], axis=0)

    return pl.pallas_call(
        body,
        out_shape=[SDS((3, 256, D), bf16), SDS((2, N_RNN_BLOCKS, 32, RNN_BLOCK), bf16), SDS((24, 256), f32)],
        name="cast_small", compiler_params=_cp(None),
    )(w_rnn_out, w_attn_out, w_o, w_ra, w_ri, meta, conv_w)


def _gather_small(shard):
    def body(s_ref, o_ref, send_sems, recv_sems):
        x, y, c = _place()
        me = _dev(x, y, c)
        copies = []
        for k, (fx, fy, fc) in enumerate(_PEER_FLIPS):
            peer = ((x + fx) % 2, (y + fy) % 2, (c + fc) % 2)
            copies.append(_remote(s_ref, o_ref.at[me], send_sems, recv_sems, k, peer))
        for cp in copies:
            cp.start()
        o_ref[me] = s_ref[...]
        for cp in copies:
            cp.wait()

    return pl.pallas_call(
        body, in_specs=[_VMEM], out_specs=_VMEM, out_shape=SDS((N_DEV, *shard.shape), shard.dtype),
        scratch_shapes=[pltpu.SemaphoreType.DMA((7,)), pltpu.SemaphoreType.DMA((7,))],
        name="gather_small",
    )(shard)


def _gather_project(w_s, smalls, later, hb, b_in, order):
    arrays = (w_s, *smalls, *later)
    na, n = len(arrays), 1 + len(smalls)
    rows = hb.shape[0]
    cm = _row_chunk(rows)
    nm = rows // cm
    pair = 2 * SHARD_IN

    def body(order_ref, *refs):
        ins, hb_ref, b_ref = refs[:na], refs[na], refs[na + 1]
        outs, z_ref = refs[na + 2:2 * na + 2], refs[2 * na + 2]
        wbuf, send_sems, recv_sems, local_sems, load_sems = refs[2 * na + 3:]
        k, mi = pl.program_id(0), pl.program_id(1)
        x, y, c = _place()
        me, sibling = (x, y, c), (x, y, 1 - c)
        chips = [(1 - x, y), (x, 1 - y), (1 - x, 1 - y)]

        def copy(a, kk, block, to, src=None):
            dst = outs[a].at[_dev(*block)]
            return pltpu.make_async_remote_copy(
                src_ref=dst if src is None else src, dst_ref=dst,
                send_sem=send_sems.at[a * 7 + kk], recv_sem=recv_sems.at[a * 7 + kk],
                device_id=to, device_id_type=MESH)

        mine = [pltpu.make_async_copy(ins[a], outs[a].at[_dev(*me)], local_sems.at[a]) for a in range(na)]

        def to_sibling():
            return [copy(a, 0, me, sibling, src=ins[a]) for a in range(n)]

        def to_chip(j):
            return [copy(a, 1 + j, me, (*chips[j], c), src=ins[a]) for a in range(n)]

        def load_pair(chip):
            cps = [pltpu.make_async_copy(outs[0].at[_dev(*chip, cc)], wbuf.at[pl.ds(cc * SHARD_IN, SHARD_IN)],
                                         load_sems.at[cc]) for cc in range(2)]
            for cp in cps:
                cp.start()
            for cp in cps:
                cp.wait()

        @pl.when((k == 0) & (mi == 0))
        def _():
            for cp in mine + to_sibling() + to_chip(0) + to_chip(1):
                cp.start()
            mine[0].wait()
            copy(0, 0, sibling, me).wait_recv()
            load_pair((x, y))

        for j, chip in enumerate(chips):
            @pl.when((k == j + 1) & (mi == 0))
            def _():
                for a in range(n):
                    copy(a, 1 + j, (*chip, c), me).wait_recv()
                    copy(a, 4 + j, (*chip, c), sibling).start()
                if j == 0:
                    for cp in to_chip(2):
                        cp.start()
                copy(0, 4 + j, (*chip, 1 - c), me).wait_recv()
                load_pair(chip)

        z_ref[...] = _dot_nt(hb_ref[...], wbuf[...]) + b_ref[...]

        @pl.when((k == len(chips)) & (mi == nm - 1))
        def _():
            for a in range(1, n):
                copy(a, 0, sibling, me).wait_recv()
                for j, chip in enumerate(chips):
                    copy(a, 4 + j, (*chip, 1 - c), me).wait_recv()
            for cp in to_sibling() + to_chip(0) + to_chip(1) + to_chip(2):
                cp.wait_send()
            for a in range(n):
                for j, chip in enumerate(chips):
                    copy(a, 4 + j, (*chip, c), sibling).wait_send()
            for cp in mine[1:]:
                cp.wait()

    res = pl.pallas_call(
        body,
        grid_spec=pltpu.PrefetchScalarGridSpec(
            num_scalar_prefetch=1, grid=(N_DEV // 2, nm),
            in_specs=[_ANY] * na + [pl.BlockSpec((cm, D), lambda k, i, o: (i, 0)),
                                    pl.BlockSpec((1, pair), lambda k, i, o: (0, o[k]))],
            out_specs=[_ANY] * na + [pl.BlockSpec((cm, pair), lambda k, i, o: (i, o[k]))],
            scratch_shapes=[pltpu.VMEM((pair, D), bf16), pltpu.SemaphoreType.DMA((7 * n,)),
                            pltpu.SemaphoreType.DMA((7 * n,)), pltpu.SemaphoreType.DMA((na,)),
                            pltpu.SemaphoreType.DMA((2,))]),
        out_shape=[SDS((N_DEV, *s.shape), s.dtype) for s in arrays] + [SDS((rows, D_IN), f32)],
        name="gather_project", compiler_params=_cp(("arbitrary", "arbitrary"), 48),
    )(order, *arrays, hb, b_in)
    return res[:na], res[na]


_HBM = pl.BlockSpec(memory_space=pltpu.HBM)
_SEM = pl.BlockSpec(memory_space=pltpu.SEMAPHORE)
_PEER_FLIPS = [(f // 4, (f // 2) % 2, f % 2) for f in range(1, N_DEV)]


def _remote(src, dst, send_sems, recv_sems, k, to):
    return pltpu.make_async_remote_copy(src_ref=src, dst_ref=dst, send_sem=send_sems.at[k], recv_sem=recv_sems.at[k],
                                        device_id=to, device_id_type=MESH)


def _copies_direct(same_src):
    def make(srcs, lands, send_sems, recv_sems):
        x, y, c = _place()
        me = _dev(x, y, c)
        out = []
        for a in range(len(srcs)):
            for k, (fx, fy, fc) in enumerate(_PEER_FLIPS):
                peer = ((x + fx) % 2, (y + fy) % 2, (c + fc) % 2)
                src = srcs[a] if same_src else srcs[a].at[_dev(*peer)]
                out.append(_remote(src, lands[a].at[me], send_sems, recv_sems, 7 * a + k, peer))
        return out
    return make


def _copies_gather_chips(srcs, lands, send_sems, recv_sems):
    x, y, c = _place()
    chips = [(1 - x, y), (x, 1 - y), (1 - x, 1 - y)]
    return [_remote(srcs[a], lands[a].at[_dev(x, y, c)], send_sems, recv_sems, 3 * a + j, (qx, qy, c))
            for a in range(len(srcs)) for j, (qx, qy) in enumerate(chips)]


def _copies_gather_sibling(srcs, lands, send_sems, recv_sems):
    x, y, c = _place()
    chips = [(x, y), (1 - x, y), (x, 1 - y), (1 - x, 1 - y)]
    return [_remote(lands[a].at[_dev(qx, qy, c)], lands[a].at[_dev(qx, qy, c)], send_sems, recv_sems, 4 * a + j,
                    (x, y, 1 - c))
            for a in range(len(srcs)) for j, (qx, qy) in enumerate(chips)]


def _copies_siblings(srcs, lands, send_sems, recv_sems):
    x, y, c = _place()
    return [_remote(srcs[a].at[2 * q + (1 - c)], lands[a].at[q], send_sems, recv_sems, 4 * a + q, (x, y, 1 - c))
            for a in range(len(srcs)) for q in range(4)]


def _copies_chips(srcs, lands, send_sems, recv_sems):
    x, y, c = _place()
    chips = [(1 - x, y), (x, 1 - y), (1 - x, 1 - y)]
    return [_remote(srcs[a].at[2 * qx + qy], lands[a].at[j], send_sems, recv_sems, 3 * a + j, (qx, qy, c))
            for a in range(len(srcs)) for j, (qx, qy) in enumerate(chips)]


def _split_start(make, per_array, srcs, lands, dep, name):
    n = len(srcs)

    def body(*refs):
        send_sems, recv_sems, token = refs[2 * n + 1], refs[2 * n + 2], refs[-1]
        for cp in make(refs[:n], refs[n:2 * n], send_sems, recv_sems):
            cp.start()
        token[...] = jnp.zeros_like(token)

    hbm = lambda t: pltpu.with_memory_space_constraint(t, pltpu.HBM)
    res = pl.pallas_call(
        body, name=name,
        out_shape=(pltpu.SemaphoreType.DMA((per_array * n,)), pltpu.SemaphoreType.DMA((per_array * n,)),
                   *[pltpu.HBM(t.shape, t.dtype) for t in (*srcs, *lands)], SDS((8, 128), f32)),
        in_specs=[_HBM] * (2 * n) + [_ANY], out_specs=(_SEM, _SEM, *([_HBM] * (2 * n)), _VMEM),
        input_output_aliases={i: 2 + i for i in range(2 * n)},
        compiler_params=pltpu.CompilerParams(has_side_effects=pltpu.SideEffectType.DATAFLOW_SIDE_EFFECTING),
    )(*[hbm(t) for t in (*srcs, *lands)], dep)
    return res[0], res[1], list(res[2:2 + n]), list(res[2 + n:2 + 2 * n]), res[-1]


def _split_wait(make, send_sems, recv_sems, srcs, lands, after, name):
    n = len(srcs)

    def body(*refs):
        for cp in make(refs[:n], refs[n:2 * n], refs[2 * n], refs[2 * n + 1]):
            cp.wait_send()
            cp.wait_recv()

    res = pl.pallas_call(
        body, name=name,
        out_shape=tuple(pltpu.HBM(t.shape, t.dtype) for t in (*srcs, *lands)),
        in_specs=[_HBM] * (2 * n) + [_SEM, _SEM, _ANY], out_specs=tuple([_HBM] * (2 * n)),
        input_output_aliases={i: i for i in range(2 * n)},
        compiler_params=pltpu.CompilerParams(has_side_effects=pltpu.SideEffectType.DATAFLOW_SIDE_EFFECTING),
    )(*srcs, *lands, send_sems, recv_sems, after)
    return list(res[:n]), list(res[n:])


def _adamw_direct(g, land, me_idx, w, m, v, name):
    r, wd = w.shape
    tr = min(r, 256)

    def body(me_ref, *refs):
        g_ref, peers = refs[0], refs[1:N_DEV]
        w_ref, m_ref, v_ref, g_out, d_out, m_out, v_out = refs[N_DEV:]
        gs = g_ref[...].astype(f32)
        for p_ref in peers:
            gs = gs + p_ref[...].astype(f32)
        d, mn, vn = _adamw(w_ref[...], gs, m_ref[...], v_ref[...])
        g_out[...] = gs
        d_out[...] = d
        m_out[...] = mn
        v_out[...] = vn

    tile = pl.BlockSpec((tr, wd), lambda i, me_ref: (i, 0))
    slot = lambda k: pl.BlockSpec((None, tr, wd), lambda i, me_ref: ((me_ref[0] + k) % N_DEV, i, 0))
    return pl.pallas_call(
        body,
        grid_spec=pltpu.PrefetchScalarGridSpec(
            num_scalar_prefetch=1, grid=(r // tr,),
            in_specs=[slot(0)] + [slot(k) for k in range(1, N_DEV)] + [tile, tile, tile],
            out_specs=[tile] * 4),
        out_shape=[SDS((r, wd), f32)] * 4, name=name, compiler_params=_cp(("arbitrary",), 48),
    )(me_idx, g, *([land] * (N_DEV - 1)), w, m, v)


def _pair_sum(g, r1, c_idx, name):
    _, r, w = g.shape
    tr = _tile_rows(r)

    def body(c_ref, g_ref, r_ref, o_ref):
        o_ref[...] = (g_ref[...].astype(f32) + r_ref[...].astype(f32)).astype(bf16)

    return pl.pallas_call(
        body,
        grid_spec=pltpu.PrefetchScalarGridSpec(
            num_scalar_prefetch=1, grid=(4, r // tr),
            in_specs=[pl.BlockSpec((None, tr, w), lambda q, i, c_ref: (2 * q + c_ref[0], i, 0)),
                      pl.BlockSpec((None, tr, w), lambda q, i, c_ref: (q, i, 0))],
            out_specs=pl.BlockSpec((None, tr, w), lambda q, i, c_ref: (q, i, 0))),
        out_shape=SDS((4, r, w), bf16), name=name, compiler_params=_cp(("arbitrary", "arbitrary")),
    )(c_idx, g, r1)


def _adamw(w, g, m, v):
    m = ADAM_B1 * m + (1.0 - ADAM_B1) * g
    v = ADAM_B2 * v + (1.0 - ADAM_B2) * (g * g)
    m_hat = m / (1.0 - ADAM_B1 ** ADAM_STEP)
    v_hat = v / (1.0 - ADAM_B2 ** ADAM_STEP)
    delta = -ADAM_LR * (m_hat / (jnp.sqrt(v_hat) + ADAM_EPS) + ADAM_WD * w)
    return delta, m, v


def _adamw_big(pieces, q_idx, w, m, v, name, row_off=0):
    r, wd = w.shape
    tr = _tile_rows(r)
    np_ = len(pieces)
    wp = wd // np_

    def body(q_ref, *refs):
        w_ref, m_ref, v_ref, g_out, d_out, m_out, v_out = refs[2 * np_:]
        for k in range(np_):
            @pl.when(pl.program_id(1) == k)
            def _():
                p_ref, r_ref = refs[2 * k], refs[2 * k + 1]
                g = p_ref[...].astype(f32)
                for j in range(3):
                    g = g + r_ref[j].astype(f32)
                d, mn, vn = _adamw(w_ref[...], g, m_ref[...], v_ref[...])
                g_out[...] = g
                d_out[...] = d
                m_out[...] = mn
                v_out[...] = vn

    tile = pl.BlockSpec((tr, wp), lambda i, k, q_ref: (i, k))
    in_specs, args = [], []
    for part, r2 in pieces:
        in_specs += [pl.BlockSpec((None, tr, wp), lambda i, k, q_ref: (q_ref[0], row_off + i, 0)),
                     pl.BlockSpec((3, tr, wp), lambda i, k, q_ref: (0, row_off + i, 0))]
        args += [part, r2]
    return pl.pallas_call(
        body,
        grid_spec=pltpu.PrefetchScalarGridSpec(
            num_scalar_prefetch=1, grid=(r // tr, np_), in_specs=in_specs + [tile, tile, tile],
            out_specs=[tile] * 4),
        out_shape=[SDS((r, wd), f32)] * 4, name=name, compiler_params=_cp(("arbitrary", "arbitrary"), 48),
    )(q_idx, *args, w, m, v)


_SMALL_ROWS = 24


def _pack_small(st_emb, vec_rnn, st_out, dsr, db_in, dmeta):
    def body(se_ref, vr_ref, so_ref, dsr_ref, db_ref, dm_ref, sm_ref, sm2_ref):
        sm_ref[...] = jnp.zeros_like(sm_ref)
        sm2_ref[...] = jnp.zeros_like(sm2_ref)
        sm_ref[0:2, :] = se_ref[0:2, :]
        sm_ref[2:3, :] = vr_ref[3:4, :]
        sm_ref[3:6, :] = vr_ref[0:3, :]
        sm_ref[6:7, :] = so_ref[2:3, :]
        sm_ref[7:9, :] = so_ref[0:2, :]
        for h in range(N_KV):
            sm_ref[9:10, h * GROUP:(h + 1) * GROUP] = _colsum(dsr_ref[h])
        for j in range(6):
            sm_ref[16 + j:17 + j, :] = db_ref[0:1, j * D:(j + 1) * D]
        sm_ref[22:23, 0:D_IN - 6 * D] = db_ref[0:1, 6 * D:D_IN]
        for s in range(N_DEV):
            sm2_ref[s, 0:N_META, :] = dm_ref[:, s * 256:(s + 1) * 256]
            sm2_ref[s, N_META:N_META + CONV_WIDTH, :] = vr_ref[4:8, s * 256:(s + 1) * 256]

    return pl.pallas_call(
        body, out_shape=[SDS((_SMALL_ROWS, D), f32), SDS((N_DEV, 24, 256), f32)],
        name="pack_small", compiler_params=_cp(None),
    )(st_emb, vec_rnn, st_out, dsr, db_in, dmeta)


def _small_allreduce(sm, sm2):
    def body(sm_ref, sm2_ref, o_ref, o2_ref, buf, buf2, send_sems, recv_sems):
        x, y, c = _place()
        me = _dev(x, y, c)
        copies = []
        for f in range(1, N_DEV):
            fx, fy, fc = f // 4, (f // 2) % 2, f % 2
            peer = ((x + fx) % 2, (y + fy) % 2, (c + fc) % 2)
            for t, (src, dst) in enumerate(((sm_ref, buf), (sm2_ref, buf2))):
                k = 2 * (f - 1) + t
                copies.append(pltpu.make_async_remote_copy(
                    src_ref=src, dst_ref=dst.at[me], send_sem=send_sems.at[k], recv_sem=recv_sems.at[k],
                    device_id=peer, device_id_type=MESH))
        for cp in copies:
            cp.start()
        buf[me] = sm_ref[...]
        buf2[me] = sm2_ref[...]
        for cp in copies:
            cp.wait()
        acc, acc2 = buf[0], buf2[0]
        for e in range(1, N_DEV):
            acc, acc2 = acc + buf[e], acc2 + buf2[e]
        o_ref[...] = acc
        o2_ref[...] = acc2

    return pl.pallas_call(
        body, in_specs=[_VMEM, _VMEM], out_specs=[_VMEM, _VMEM],
        out_shape=[SDS(sm.shape, f32), SDS(sm2.shape, f32)],
        scratch_shapes=[pltpu.VMEM((N_DEV, *sm.shape), f32), pltpu.VMEM((N_DEV, *sm2.shape), f32),
                        pltpu.SemaphoreType.DMA((14,)), pltpu.SemaphoreType.DMA((14,))],
        name="small_allreduce",
    )(sm, sm2)


_SMALL_ROW_OF = {"ln_emb_g": 0, "ln_emb_b": 1, "conv_b": 2, "b_ra": 3, "b_ri": 4, "lru_lambda": 5, "b_o": 6,
                 "ln_g": 7, "ln_b": 8}
_SMALL_NAMES = ["ln_emb_g", "ln_emb_b", "conv_b", "b_ra", "b_ri", "lru_lambda", "b_o", "ln_g", "ln_b",
                "sinks", "b_in", "meta_tokens", "conv_w"]


def _small_update(sm, sm2_mine, wmv):
    def grad_of(name, sm_ref, s2_ref):
        if name in _SMALL_ROW_OF:
            r = _SMALL_ROW_OF[name]
            return sm_ref[r:r + 1, :]
        if name == "sinks":
            return sm_ref[9:10, 0:N_KV * GROUP]
        if name == "b_in":
            return jnp.concatenate([sm_ref[16 + j:17 + j, :] for j in range(7)], axis=1)[:, :D_IN]
        if name == "meta_tokens":
            return s2_ref[0:N_META, :]
        return s2_ref[N_META:N_META + CONV_WIDTH, :]

    def body(*refs):
        sm_ref, s2_ref = refs[0], refs[1]
        ins = refs[2:2 + 3 * len(_SMALL_NAMES)]
        outs = refs[2 + 3 * len(_SMALL_NAMES):]
        for i, name in enumerate(_SMALL_NAMES):
            w_ref, m_ref, v_ref = ins[3 * i:3 * i + 3]
            g = grad_of(name, sm_ref, s2_ref)
            d, mn, vn = _adamw(w_ref[...], g, m_ref[...], v_ref[...])
            outs[4 * i][...] = g
            outs[4 * i + 1][...] = d
            outs[4 * i + 2][...] = mn
            outs[4 * i + 3][...] = vn

    args, out_shape = [sm, sm2_mine], []
    for name in _SMALL_NAMES:
        args += list(wmv[name])
        out_shape += [SDS(wmv[name][0].shape, f32)] * 4
    res = pl.pallas_call(body, out_shape=out_shape, name="small_update", compiler_params=_cp(None))(*args)
    return {name: tuple(res[4 * i:4 * i + 4]) for i, name in enumerate(_SMALL_NAMES)}


_WEIGHTS = ["meta_tokens", "ln_emb_g", "ln_emb_b", "w_in", "b_in", "conv_w", "conv_b", "w_ra", "b_ra", "w_ri",
            "b_ri", "lru_lambda", "sinks", "w_rnn_out", "w_attn_out", "w_o", "b_o", "ln_g", "ln_b"]
_SMALL_2D = {"meta_tokens": (N_META, 256), "conv_w": (CONV_WIDTH, 256), "b_in": (1, D_IN), "sinks": (1, N_KV * GROUP)}


def kernel(x, meta_tokens, ln_emb_g, ln_emb_b, w_in, b_in, conv_w, conv_b, w_ra, b_ra, w_ri, b_ri, lru_lambda, sinks, w_rnn_out, w_attn_out, w_o, b_o, ln_g, ln_b, loss_target, m_meta_tokens, m_ln_emb_g, m_ln_emb_b, m_w_in, m_b_in, m_conv_w, m_conv_b, m_w_ra, m_b_ra, m_w_ri, m_b_ri, m_lru_lambda, m_sinks, m_w_rnn_out, m_w_attn_out, m_w_o, m_b_o, m_ln_g, m_ln_b, v_meta_tokens, v_ln_emb_g, v_ln_emb_b, v_w_in, v_b_in, v_conv_w, v_conv_b, v_w_ra, v_b_ra, v_w_ri, v_b_ri, v_lru_lambda, v_sinks, v_w_rnn_out, v_w_attn_out, v_w_o, v_b_o, v_ln_g, v_ln_b):
    w = dict(meta_tokens=meta_tokens, ln_emb_g=ln_emb_g, ln_emb_b=ln_emb_b, w_in=w_in, b_in=b_in, conv_w=conv_w,
             conv_b=conv_b, w_ra=w_ra, b_ra=b_ra, w_ri=w_ri, b_ri=b_ri, lru_lambda=lru_lambda, sinks=sinks,
             w_rnn_out=w_rnn_out, w_attn_out=w_attn_out, w_o=w_o, b_o=b_o, ln_g=ln_g, ln_b=ln_b)
    m = dict(meta_tokens=m_meta_tokens, ln_emb_g=m_ln_emb_g, ln_emb_b=m_ln_emb_b, w_in=m_w_in, b_in=m_b_in,
             conv_w=m_conv_w, conv_b=m_conv_b, w_ra=m_w_ra, b_ra=m_b_ra, w_ri=m_w_ri, b_ri=m_b_ri,
             lru_lambda=m_lru_lambda, sinks=m_sinks, w_rnn_out=m_w_rnn_out, w_attn_out=m_w_attn_out, w_o=m_w_o,
             b_o=m_b_o, ln_g=m_ln_g, ln_b=m_ln_b)
    v = dict(meta_tokens=v_meta_tokens, ln_emb_g=v_ln_emb_g, ln_emb_b=v_ln_emb_b, w_in=v_w_in, b_in=v_b_in,
             conv_w=v_conv_w, conv_b=v_conv_b, w_ra=v_w_ra, b_ra=v_b_ra, w_ri=v_w_ri, b_ri=v_b_ri,
             lru_lambda=v_lru_lambda, sinks=v_sinks, w_rnn_out=v_w_rnn_out, w_attn_out=v_w_attn_out, w_o=v_w_o,
             b_o=v_b_o, ln_g=v_ln_g, ln_b=v_ln_b)
    px, py, pc = _place()
    as_idx = lambda t: jnp.reshape(t, (1,)).astype(jnp.int32)
    c_idx, q_idx, me_idx = as_idx(pc), as_idx(2 * px + py), as_idx(_dev(px, py, pc))

    w3_s, wrg_s, small_s = _cast_small(w_rnn_out, w_attn_out, w_o, w_ra, w_ri, meta_tokens, conv_w)
    vec = lambda name: w[name].reshape(1, -1)
    p = {k: vec(k) for k in ("ln_emb_g", "ln_emb_b", "b_in", "conv_b", "b_ra", "b_ri", "lru_lambda", "sinks",
                             "b_o", "ln_g", "ln_b")}
    w_in_t = lambda a: jnp.swapaxes(a, 1, 2).reshape(SHARD_IN, D)
    smallw = _gather_small(small_s)
    h32, hb = _ln_emb(x, smallw, p["ln_emb_g"], p["ln_emb_b"])
    order = jnp.stack([2 * px + py, 2 * (1 - px) + py, 2 * px + (1 - py), 2 * (1 - px) + (1 - py)]).astype(jnp.int32)
    (wg, wrg, w3_land), z = _gather_project(_cast_w_in(w_in_t(w_in)), [wrg_s], [w3_s], hb, p["b_in"], order)
    w_full = wg.reshape(D_IN, D)

    w3_chips = _split_start(_copies_gather_chips, 3, [w3_s], [w3_land], wrg, "gather_w3_chips_start")
    s = _step_rnn(h32, hb, z, wrg, smallw, p, w3_chips[4][0:1, 0:1])
    w3_src, w3_land = _split_wait(_copies_gather_chips, *w3_chips[:4], s["k_r"], "gather_w3_chips_wait")
    w3_sib = _split_start(_copies_gather_sibling, 4, w3_src, w3_land, s["k_r"], "gather_w3_sibling_start")
    s = _step_attn(s, p, w3_sib[4][0:1, 0:1])
    w3 = _split_wait(_copies_gather_sibling, *w3_sib[:4], s["lse"], "gather_w3_sibling_wait")[1][0]
    t = _step_merge(s, loss_target, w3, p)
    loss = lax.psum(jnp.sum(t["st_out"][3]), ("x", "y", "c"))

    big = {}
    two_d = lambda name: (w[name].shape[-2], w[name].shape[-1])
    proj = ("w_o", "w_rnn_out", "w_attn_out")
    g_proj = [t[k].reshape(N_DEV, 256, D) for k in ("g_wo", "g_wrnn", "g_wattn")]
    g_pending = _split_start(_copies_direct(False), 7, g_proj, [lax.empty((N_DEV, 256, D), bf16) for _ in proj],
                             p["b_o"], "reduce_proj_start")
    u = _step_backward(s, t, wrg, smallw, p, p["conv_b"] + g_pending[4][0:1, 0:1])

    def siblings_start(gs, dep, tag):
        return _split_start(_copies_siblings, 4, gs, [lax.empty((4, *g.shape[1:]), bf16) for g in gs], dep,
                            "reduce_siblings_start_" + tag)

    def chips_start(gs, r1, dep, tag):
        parts = [_pair_sum(g, r, c_idx, "pair_sum_%s%d" % (tag, i)) for i, (g, r) in enumerate(zip(gs, r1))]
        return _split_start(_copies_chips, 3, parts, [lax.empty((3, *q.shape[1:]), bf16) for q in parts], dep,
                            "reduce_chips_start_" + tag)

    g_a, db_in = _mm_dwin(s["hb"], u["dz"], 0, p["b_o"])
    shards = lambda g: g.reshape(N_DEV, SHARD_IN, W_IN_HALF)
    sib_a = siblings_start([shards(g_a), u["g_wrg"].reshape(N_DEV, 2 * RNN_BLOCK, RNN_BLOCK)], db_in, "a")
    g_b, = _mm_dwin(s["hb"], u["dz"], 1, sib_a[4])
    sib_b = siblings_start([shards(g_b)], db_in, "b")
    chp_a = chips_start(*_split_wait(_copies_siblings, *sib_a[:4], sib_b[4], "reduce_siblings_wait_a"), db_in, "a")
    g_proj, g_land = _split_wait(_copies_direct(False), *g_pending[:4], chp_a[4], "reduce_proj_wait")
    for i, name in enumerate(proj):
        res = _adamw_direct(g_proj[i], g_land[i], me_idx, w[name].reshape(two_d(name)), m[name].reshape(two_d(name)),
                            v[name].reshape(two_d(name)), "adamw_" + name)
        big[name] = tuple(r.reshape(w[name].shape) for r in res)
    chp_b = chips_start(*_split_wait(_copies_siblings, *sib_b[:4], big["w_attn_out"][3], "reduce_siblings_wait_b"),
                        db_in, "b")
    u.update(_step_input_grad(u["dz"], w_full, chp_b[4], t["du32"], x, smallw, p))
    u["db_in"] = db_in

    loc = {**t, **u}
    sm, sm2 = _pack_small(loc["st_emb"], loc["vec_rnn"], loc["st_out"], loc["dsr"], loc["db_in"], loc["dmeta"])
    sm, sm2 = _small_allreduce(sm, sm2)
    sm2_mine = lax.dynamic_index_in_dim(sm2, _dev(px, py, pc), 0, keepdims=False)
    two = lambda name, t: t.reshape(_SMALL_2D.get(name, (1, D)))
    small = _small_update(sm, sm2_mine, {k: (two(k, w[k]), two(k, m[k]), two(k, v[k])) for k in _SMALL_NAMES})

    parts_a, r2_a = _split_wait(_copies_chips, *chp_a[:4], small["b_in"][3], "reduce_chips_wait_a")
    parts_b, r2_b = _split_wait(_copies_chips, *chp_b[:4], small["b_in"][2], "reduce_chips_wait_b")
    res = _adamw_big([(parts_a[0], r2_a[0]), (parts_b[0], r2_b[0])], q_idx, w_in_t(w["w_in"]), w_in_t(m["w_in"]),
                     w_in_t(v["w_in"]), "adamw_w_in")
    big["w_in"] = tuple(jnp.swapaxes(r.reshape(1, SHARD_IN, D), 1, 2) for r in res)
    for i, name in enumerate(("w_ra", "w_ri")):
        sq = (RNN_BLOCK, RNN_BLOCK)
        res = _adamw_big([(parts_a[1], r2_a[1])], q_idx, w[name].reshape(sq), m[name].reshape(sq), v[name].reshape(sq),
                         "adamw_" + name, row_off=i)
        big[name] = tuple(r.reshape(w[name].shape) for r in res)
    res = dict(big)
    for k in _SMALL_NAMES:
        res[k] = tuple(t.reshape(w[k].shape) for t in small[k])

    outs = [loss, loc["grad_x"]]
    for j in range(4):
        outs += [res[k][j] for k in _WEIGHTS]
    return tuple(outs)
```

```python
import functools

import jax
import jax.numpy as jnp
from jax import lax
from jax.experimental import pallas as pl
from jax.experimental.pallas import tpu as pltpu

f32, bf16 = jnp.float32, jnp.bfloat16
SDS = jax.ShapeDtypeStruct

N_DEV = 8
D = 2048
N_META = 16
BLK = 128
ROW0 = BLK - N_META
N_RNN_BLOCKS = 8
RNN_BLOCK = D // N_RNN_BLOCKS
CONV_WIDTH = 4
LRU_C = 8.0
HEAD_DIM = 64
N_KV = 4
GROUP = 8
HALF = HEAD_DIM // 2
ROPE_THETA = 10000.0
NEG_INF = -1e30
LN_EPS = 1e-5
ALPHA = 2.0 ** 0.25
D_IN = 12800
SHARD_IN = D_IN // N_DEV
OFF_GR, OFF_Q, OFF_K, OFF_V, OFF_GA, OFF_G = 2048, 4096, 6144, 6400, 6656, 8704
ADAM_LR, ADAM_B1, ADAM_B2, ADAM_EPS, ADAM_WD, ADAM_STEP = 1e-3, 0.9, 0.999, 1e-8, 0.01, 10
VMEM_LIMIT_MB = 56
MESH = pl.DeviceIdType.MESH


def _cp(sem=None, vmem_mb=40):
    return pltpu.CompilerParams(dimension_semantics=sem, vmem_limit_bytes=vmem_mb * 2 ** 20)


def _row_chunk(m):
    best = 16
    for c in range(16, 641, 16):
        if m % c == 0:
            best = c
    return best


def _sigmoid(x):
    return 1.0 / (1.0 + jnp.exp(-x))


def _silu_and_grad(x):
    s = _sigmoid(x)
    return x * s, s * (1.0 + x * (1.0 - s))


def _log_sigmoid(x):
    return jnp.minimum(x, 0.0) - jnp.log1p(jnp.exp(-jnp.abs(x)))


def _ln_rows(v, g, b):
    mu = jnp.mean(v, axis=-1, keepdims=True)
    c = v - mu
    var = jnp.mean(c * c, axis=-1, keepdims=True)
    rstd = lax.rsqrt(var + LN_EPS)
    xhat = c * rstd
    return xhat * g + b, xhat, rstd


def _ln_rows_bwd(dy, g, xhat, rstd):
    dxh = dy * g
    m1 = jnp.mean(dxh, axis=-1, keepdims=True)
    m2 = jnp.mean(dxh * xhat, axis=-1, keepdims=True)
    return rstd * (dxh - m1 - xhat * m2)


def _colsum(v):
    return jnp.sum(v, axis=0, keepdims=True)


def _dot(a, b):
    return jnp.dot(a, b, preferred_element_type=f32)


def _dot_nt(a, b):
    return lax.dot_general(a, b, (((1,), (1,)), ((), ())), preferred_element_type=f32)


def _dot_tn(a, b):
    return lax.dot_general(a, b, (((0,), (0,)), ((), ())), preferred_element_type=f32)


def _meta_full(sw_ref):
    return jnp.concatenate([sw_ref[s, 0:N_META, :] for s in range(N_DEV)], axis=1)


def _ln_emb(x, smallw, g_e, b_e):
    seq = x.shape[1]
    rows = seq + BLK
    nb = rows // BLK

    def body(x_ref, sw_ref, g_ref, b_ref, h32_ref, hb_ref):
        i = pl.program_id(0)
        g, b = g_ref[...], b_ref[...]

        def emit(blk):
            h32_ref[...] = blk
            hb_ref[...] = blk.astype(bf16)

        @pl.when(i == 0)
        def _():
            hm = _ln_rows(_meta_full(sw_ref), g, b)[0]
            emit(jnp.concatenate([jnp.zeros((ROW0, D), f32), hm], axis=0))

        @pl.when(i > 0)
        def _():
            emit(_ln_rows(x_ref[0], g, b)[0])

    return pl.pallas_call(
        body, grid=(nb,),
        in_specs=[pl.BlockSpec((1, BLK, D), lambda i: (0, jnp.maximum(i - 1, 0), 0)),
                  pl.BlockSpec((N_DEV, 24, 256), lambda i: (0, 0, 0)),
                  pl.BlockSpec((1, D), lambda i: (0, 0)),
                  pl.BlockSpec((1, D), lambda i: (0, 0))],
        out_specs=[pl.BlockSpec((BLK, D), lambda i: (i, 0)),
                   pl.BlockSpec((BLK, D), lambda i: (i, 0))],
        out_shape=[SDS((rows, D), f32), SDS((rows, D), bf16)],
        name="ln_emb", compiler_params=_cp(("arbitrary",)),
    )(x, smallw, g_e, b_e)


def _ln_emb_bwd(dh, du32, x, smallw, g_e):
    seq = x.shape[1]
    rows = seq + BLK
    nb = rows // BLK

    def body(dh_ref, du_ref, x_ref, sw_ref, g_ref, gx_ref, dmeta_ref, st_ref):
        i = pl.program_id(0)
        g = g_ref[...]
        dht = dh_ref[...] + ALPHA * du_ref[...]

        @pl.when(i == 0)
        def _():
            v = jnp.concatenate([jnp.zeros((ROW0, D), f32), _meta_full(sw_ref)], axis=0)
            valid = lax.broadcasted_iota(jnp.int32, (BLK, 1), 0) >= ROW0
            d = jnp.where(valid, dht, 0.0)
            _, xhat, rstd = _ln_rows(v, g, 0.0)
            dv = _ln_rows_bwd(d, g, xhat, rstd)
            dmeta_ref[...] = dv[ROW0:, :]
            st_ref[...] = jnp.concatenate([_colsum(d * xhat), _colsum(d), jnp.zeros((6, D), f32)], axis=0)

        @pl.when(i > 0)
        def _():
            _, xhat, rstd = _ln_rows(x_ref[0], g, 0.0)
            gx_ref[0] = _ln_rows_bwd(dht, g, xhat, rstd)
            st_ref[0:1, :] += _colsum(dht * xhat)
            st_ref[1:2, :] += _colsum(dht)

    return pl.pallas_call(
        body, grid=(nb,),
        in_specs=[pl.BlockSpec((BLK, D), lambda i: (i, 0)),
                  pl.BlockSpec((BLK, D), lambda i: (i, 0)),
                  pl.BlockSpec((1, BLK, D), lambda i: (0, jnp.maximum(i - 1, 0), 0)),
                  pl.BlockSpec((N_DEV, 24, 256), lambda i: (0, 0, 0)),
                  pl.BlockSpec((1, D), lambda i: (0, 0))],
        out_specs=[pl.BlockSpec((1, BLK, D), lambda i: (0, jnp.maximum(i - 1, 0), 0)),
                   pl.BlockSpec((N_META, D), lambda i: (0, 0)),
                   pl.BlockSpec((8, D), lambda i: (0, 0))],
        out_shape=[SDS((1, seq, D), f32), SDS((N_META, D), f32), SDS((8, D), f32)],
        name="ln_emb_bwd", compiler_params=_cp(("arbitrary",)),
    )(dh, du32, x, smallw, g_e)


def _mm(a, b, *, name, nt=False, sel=None, bias=None, out_dtype=f32, tn=512):
    m, k = a.shape
    cm = _row_chunk(m)
    stacked = sel is not None
    n = D if stacked else (b.shape[0] if nt else b.shape[1])
    am = m
    if stacked and nt:
        b_spec = pl.BlockSpec((tn // 256, None, 256, D), lambda j, i: (j, sel, 0, 0))
    elif stacked:
        b_spec = pl.BlockSpec((N_DEV, None, 256, tn), lambda j, i: (0, sel, 0, j))
    elif nt:
        b_spec = pl.BlockSpec((tn, k), lambda j, i: (j, 0))
    else:
        b_spec = pl.BlockSpec((k, tn), lambda j, i: (0, j))
    in_specs = [pl.BlockSpec((am, k), lambda j, i: (i, 0)), b_spec]
    args = [a, b]
    if bias is not None:
        in_specs.append(pl.BlockSpec((1, tn), lambda j, i: (0, j)))
        args.append(bias)

    def body(*refs):
        a_ref, b_ref, o_ref = refs[0], refs[1], refs[-1]
        bm = b_ref[...]
        if stacked:
            bm = bm.reshape((tn, D) if nt else (D, tn))
        for c in range(am // cm):
            acc = (_dot_nt if nt else _dot)(a_ref[c * cm:(c + 1) * cm, :], bm)
            if bias is not None:
                acc = acc + refs[2][...]
            o_ref[c * cm:(c + 1) * cm, :] = acc.astype(out_dtype)

    return pl.pallas_call(
        body, grid=(n // tn, m // am), in_specs=in_specs,
        out_specs=pl.BlockSpec((am, tn), lambda j, i: (i, j)),
        out_shape=SDS((m, n), out_dtype), name=name, compiler_params=_cp(("arbitrary", "arbitrary"), 48),
    )(*args)


def _mm_dh(dz, w_t, after):
    rows = dz.shape[0]
    tk, tn = 2560, 512
    cm = _row_chunk(rows)

    def body(a_ref, w_ref, after_ref, o_ref):
        kk = pl.program_id(1)
        for c in range(rows // cm):
            acc = _dot(a_ref[c * cm:(c + 1) * cm, :], w_ref[...])

            @pl.when(kk == 0)
            def _():
                o_ref[c * cm:(c + 1) * cm, :] = acc

            @pl.when(kk > 0)
            def _():
                o_ref[c * cm:(c + 1) * cm, :] += acc

    return pl.pallas_call(
        body, grid=(D // tn, D_IN // tk),
        in_specs=[pl.BlockSpec((rows, tk), lambda j, kk: (0, kk)),
                  pl.BlockSpec((tk, tn), lambda j, kk: (kk, j)),
                  pl.BlockSpec(memory_space=pl.ANY)],
        out_specs=pl.BlockSpec((rows, tn), lambda j, kk: (0, j)),
        out_shape=SDS((rows, D), f32), name="mm_dh", compiler_params=_cp(("arbitrary", "arbitrary"), 48),
    )(dz, w_t, after)


W_IN_HALF = D // 2


def _mm_dwin(hb, dz, half, after):
    rows = dz.shape[0]
    tc = 640
    with_db = half == 0

    def body(dz_ref, h_ref, after_ref, o_ref, *db_ref):
        o_ref[...] = _dot_tn(dz_ref[...], h_ref[...]).astype(bf16)
        if with_db:
            def step(i, s):
                blk = dz_ref[pl.ds(pl.multiple_of(i * BLK, BLK), BLK), :].astype(f32)
                return s + blk.reshape(BLK // 8, 8, tc).sum(axis=0)
            s = lax.fori_loop(0, rows // BLK, step, jnp.zeros((8, tc), f32))
            db_ref[0][...] = jnp.broadcast_to(_colsum(s), (8, tc))

    out_specs = [pl.BlockSpec((tc, W_IN_HALF), lambda j: (j, 0))]
    out_shape = [SDS((D_IN, W_IN_HALF), bf16)]
    if with_db:
        out_specs.append(pl.BlockSpec((8, tc), lambda j: (0, j)))
        out_shape.append(SDS((8, D_IN), f32))
    return pl.pallas_call(
        body, grid=(D_IN // tc,),
        in_specs=[pl.BlockSpec((rows, tc), lambda j: (0, j)),
                  pl.BlockSpec((rows, W_IN_HALF), lambda j: (0, half)),
                  pl.BlockSpec(memory_space=pl.ANY)],
        out_specs=out_specs, out_shape=out_shape,
        name="mm_dwin_%d" % half, compiler_params=_cp(("arbitrary",), 48),
    )(dz, hb, after)


SCAN_ROWS = 32


def _scan8(a, b, reverse):
    idx = lax.broadcasted_iota(jnp.int32, a.shape, 0)
    for s in (1, 2, 4):
        sh = 8 - s if reverse else s
        a_sh, b_sh = pltpu.roll(a, sh, 0), pltpu.roll(b, sh, 0)
        m = (idx < 8 - s) if reverse else (idx >= s)
        b = jnp.where(m, a * b_sh + b, b)
        a = jnp.where(m, a * a_sh, a)
    return a, b


def _shift_rows(prev8, cur, k):
    ext = jnp.concatenate([prev8, cur], axis=0)
    return pltpu.roll(ext, k, 0)[8:, :]


def _gates(xc, w_ra, b_ra, w_ri, b_ri, ls):
    xb = xc.astype(bf16)
    r = _sigmoid(_dot(xb, w_ra) + b_ra)
    ig = _sigmoid(_dot(xb, w_ri) + b_ri)
    la = LRU_C * r * ls
    a = jnp.exp(la)
    mult = jnp.sqrt(jnp.tanh(-la) * (1.0 + a * a))
    return xb, r, ig, a, mult


_RNN_IN_SPECS = lambda rows: [
    pl.BlockSpec((1, 24, 256), lambda n: (n, 0, 0)),
    pl.BlockSpec((1, RNN_BLOCK), lambda n: (0, n)),
    pl.BlockSpec((N_DEV, 2, None, 32, RNN_BLOCK), lambda n: (0, 0, n, 0, 0)),
    pl.BlockSpec((1, RNN_BLOCK), lambda n: (0, n)),
    pl.BlockSpec((1, RNN_BLOCK), lambda n: (0, n)),
    pl.BlockSpec((1, RNN_BLOCK), lambda n: (0, n)),
]


def _rnn_fwd(z, smallw, conv_b, wrg, b_ra, b_ri, lam):
    rows = z.shape[0]
    nb = rows // BLK
    col = lambda off: pl.BlockSpec((rows, RNN_BLOCK), lambda n: (0, off // RNN_BLOCK + n))

    def body(xr_ref, gr_ref, sw_ref, cb_ref, w_ref, bra_ref, bri_ref, lam_ref, xc_ref, hr_ref, ya_ref, yat_ref, a_s):
        cw = sw_ref[0, N_META:24, :]
        cb = cb_ref[...]
        w_ra = w_ref[:, 0].reshape(RNN_BLOCK, RNN_BLOCK)
        w_ri = w_ref[:, 1].reshape(RNN_BLOCK, RNN_BLOCK)
        b_ra_v, b_ri_v = bra_ref[...], bri_ref[...]
        ls = _log_sigmoid(lam_ref[...])
        rid = lax.broadcasted_iota(jnp.int32, (BLK, 1), 0)

        def blk_step(i, carry):
            r0 = pl.multiple_of(i * BLK, BLK)
            grow = rid + r0
            valid = grow >= ROW0
            cur = jnp.where(valid, xr_ref[pl.ds(r0, BLK), :], 0.0)
            prev8 = xr_ref[pl.ds(pl.multiple_of(jnp.maximum(r0 - 8, 0), 8), 8), :] * (i > 0).astype(f32)
            xc = cb + cw[0:1] * cur
            for k in range(1, CONV_WIDTH):
                xc = xc + cw[k:k + 1] * _shift_rows(prev8, cur, k)
            xc_ref[pl.ds(r0, BLK), :] = xc
            _, _, ig, a, mult = _gates(xc, w_ra, b_ra_v, w_ri, b_ri_v, ls)
            mult = jnp.where(grow == ROW0, 1.0, mult)
            a_s[pl.ds(r0, BLK), :] = a
            hr_ref[pl.ds(r0, BLK), :] = jnp.where(valid, mult * ig * xc, 0.0)
            return carry

        lax.fori_loop(0, nb, blk_step, 0)

        def scan_step(j, carry):
            r0 = pl.multiple_of(j * SCAN_ROWS, SCAN_ROWS)
            tiles = [_scan8(a_s[pl.ds(r0 + 8 * k, 8), :], hr_ref[pl.ds(r0 + 8 * k, 8), :], False)
                     for k in range(SCAN_ROWS // 8)]
            for k, (a, b) in enumerate(tiles):
                h = b + a * carry
                hr_ref[pl.ds(r0 + 8 * k, 8), :] = h
                carry = jnp.broadcast_to(h[7:8, :], (8, RNN_BLOCK))
            return carry

        lax.fori_loop(0, rows // SCAN_ROWS, scan_step, jnp.zeros((8, RNN_BLOCK), f32))

        def gate_step(i, carry):
            r0 = pl.multiple_of(i * BLK, BLK)
            ya_ref[pl.ds(r0, BLK), :] = (hr_ref[pl.ds(r0, BLK), :]
                                         * _silu_and_grad(gr_ref[pl.ds(r0, BLK), :])[0]).astype(bf16)
            return carry

        lax.fori_loop(0, nb, gate_step, 0)
        yat_ref[...] = ya_ref[...].astype(f32).T.astype(bf16)

    return pl.pallas_call(
        body, grid=(N_RNN_BLOCKS,),
        in_specs=[col(0), col(OFF_GR)] + _RNN_IN_SPECS(rows),
        out_specs=[pl.BlockSpec((rows, RNN_BLOCK), lambda n: (0, n))] * 3
                  + [pl.BlockSpec((RNN_BLOCK, rows), lambda n: (n, 0))],
        out_shape=[SDS((rows, D), f32), SDS((rows, D), f32), SDS((rows, D), bf16), SDS((D, rows), bf16)],
        scratch_shapes=[pltpu.VMEM((rows, RNN_BLOCK), f32)],
        name="rnn_fwd", compiler_params=_cp(("arbitrary",)),
    )(z, z, smallw, conv_b, wrg, b_ra, b_ri, lam)


def _rnn_bwd(dya, hr, xc, z, smallw, conv_b, wrg, b_ra, b_ri, lam):
    rows = z.shape[0]
    nb = rows // BLK
    col = lambda off: pl.BlockSpec((rows, RNN_BLOCK), lambda n: (0, off // RNN_BLOCK + n))
    blk = pl.BlockSpec((rows, RNN_BLOCK), lambda n: (0, n))

    def body(dya_ref, hr_ref, xc_ref, xr_ref, gr_ref, sw_ref, cb_ref, w_ref, bra_ref, bri_ref, lam_ref,
             dxr_ref, dgr_ref, dw_ref, vec_ref, a_s, lam_s, dxc_s, r_s, ig_s, mult_s, dw_s):
        cw = sw_ref[0, N_META:24, :]
        w_ra = w_ref[:, 0].reshape(RNN_BLOCK, RNN_BLOCK)
        w_ri = w_ref[:, 1].reshape(RNN_BLOCK, RNN_BLOCK)
        b_ra_v, b_ri_v = bra_ref[...], bri_ref[...]
        lam_v = lam_ref[...]
        ls = _log_sigmoid(lam_v)
        rid = lax.broadcasted_iota(jnp.int32, (BLK, 1), 0)
        zrow = jnp.zeros((1, RNN_BLOCK), f32)

        def p1(i, carry):
            r0 = pl.multiple_of(i * BLK, BLK)
            sl = pl.ds(r0, BLK)
            _, r, ig, a, mult = _gates(xc_ref[sl, :], w_ra, b_ra_v, w_ri, b_ri_v, ls)
            a_s[sl, :] = a
            r_s[sl, :] = r
            ig_s[sl, :] = ig
            mult_s[sl, :] = mult
            sg, dsg = _silu_and_grad(gr_ref[sl, :])
            d = dya_ref[sl, :]
            lam_s[sl, :] = d * sg
            dgr_ref[sl, :] = (d * hr_ref[sl, :] * dsg).astype(bf16)
            return carry

        lax.fori_loop(0, nb, p1, 0)

        def p2(jj, carry):
            r0 = pl.multiple_of((rows // SCAN_ROWS - 1 - jj) * SCAN_ROWS, SCAN_ROWS)
            idx = lax.broadcasted_iota(jnp.int32, (8, RNN_BLOCK), 0)
            tiles = []
            for k in range(SCAN_ROWS // 8):
                sl = pl.ds(r0 + 8 * k, 8)
                a, g = a_s[sl, :], lam_s[sl, :]
                tiles.append((g, *_scan8(a, a * g, True)))
            for k in reversed(range(SCAN_ROWS // 8)):
                g, ca, cb_ = tiles[k]
                mu = cb_ + ca * carry
                lam_s[pl.ds(r0 + 8 * k, 8), :] = g + jnp.where(idx < 7, pltpu.roll(mu, 7, 0), carry)
                carry = jnp.broadcast_to(mu[0:1, :], (8, RNN_BLOCK))
            return carry

        lax.fori_loop(0, rows // SCAN_ROWS, p2, jnp.zeros((8, RNN_BLOCK), f32))

        dw_s[...] = jnp.zeros_like(dw_s)

        def p3(i, carry):
            d_bra, d_bri, d_ls = carry
            r0 = pl.multiple_of(i * BLK, BLK)
            sl = pl.ds(r0, BLK)
            grow = rid + r0
            valid = grow >= ROW0
            first = grow == ROW0
            xcv = xc_ref[sl, :]
            xb = xcv.astype(bf16)
            r, ig, a = r_s[sl, :], ig_s[sl, :], a_s[sl, :]
            mult = jnp.where(first, 1.0, mult_s[sl, :])
            lam_t = lam_s[sl, :]
            du = jnp.where(valid, lam_t, 0.0)
            hprev = _shift_rows(hr_ref[pl.ds(pl.multiple_of(jnp.maximum(r0 - 8, 0), 8), 8), :] * (i > 0).astype(f32), hr_ref[sl, :], 1)
            da = lam_t * hprev
            dmult = jnp.where(first, 0.0, du * ig * xcv)
            di = du * mult * xcv
            dxc = du * mult * ig
            ratio = jnp.where(valid & jnp.logical_not(first), a * a / mult, 0.0)
            dla = da * a - dmult * ratio
            dpr = (dla * (LRU_C * ls)) * r * (1.0 - r)
            dpi = di * ig * (1.0 - ig)
            dprb, dpib = dpr.astype(bf16), dpi.astype(bf16)
            dw_s[0] += _dot_tn(xb, dprb)
            dw_s[1] += _dot_tn(xb, dpib)
            dxc_s[sl, :] = dxc + _dot_nt(dprb, w_ra) + _dot_nt(dpib, w_ri)
            return d_bra + _colsum(dpr), d_bri + _colsum(dpi), d_ls + _colsum(dla * (LRU_C * r))

        d_bra, d_bri, d_ls = lax.fori_loop(0, nb, p3, (zrow, zrow, zrow))

        def p4(i, carry):
            d_cb, d_w0, d_w1, d_w2, d_w3 = carry
            r0 = pl.multiple_of(i * BLK, BLK)
            sl = pl.ds(r0, BLK)
            grow = rid + r0
            valid = grow >= ROW0
            dxc = dxc_s[sl, :]
            nxt = dxc_s[pl.ds(pl.multiple_of(jnp.minimum(r0 + BLK, rows - 8), 8), 8), :] * (i < nb - 1).astype(f32)
            ext = jnp.concatenate([dxc, nxt], axis=0)
            dxr = cw[0:1] * dxc
            for k in range(1, CONV_WIDTH):
                dxr = dxr + cw[k:k + 1] * pltpu.roll(ext, BLK + 8 - k, 0)[:BLK, :]
            dxr_ref[sl, :] = jnp.where(valid, dxr, 0.0).astype(bf16)
            cur = jnp.where(valid, xr_ref[sl, :], 0.0)
            prev8 = xr_ref[pl.ds(pl.multiple_of(jnp.maximum(r0 - 8, 0), 8), 8), :] * (i > 0).astype(f32)
            dws = [d_w0 + _colsum(dxc * cur)]
            for k, acc in ((1, d_w1), (2, d_w2), (3, d_w3)):
                dws.append(acc + _colsum(dxc * _shift_rows(prev8, cur, k)))
            return (d_cb + _colsum(dxc), *dws)

        d_cb, d_w0, d_w1, d_w2, d_w3 = lax.fori_loop(0, nb, p4, (zrow,) * 5)

        d_lam = d_ls * _sigmoid(-lam_v)
        vec_ref[...] = jnp.concatenate([d_bra, d_bri, d_lam, d_cb, d_w0, d_w1, d_w2, d_w3], axis=0)
        dw_ref[:, 0] = dw_s[0].astype(bf16).reshape(N_DEV, 32, RNN_BLOCK)
        dw_ref[:, 1] = dw_s[1].astype(bf16).reshape(N_DEV, 32, RNN_BLOCK)

    return pl.pallas_call(
        body, grid=(N_RNN_BLOCKS,),
        in_specs=[blk, blk, blk, col(0), col(OFF_GR)] + _RNN_IN_SPECS(rows),
        out_specs=[blk, blk,
                   pl.BlockSpec((N_DEV, 2, None, 32, RNN_BLOCK), lambda n: (0, 0, n, 0, 0)),
                   pl.BlockSpec((8, RNN_BLOCK), lambda n: (0, n))],
        out_shape=[SDS((rows, D), bf16), SDS((rows, D), bf16),
                   SDS((N_DEV, 2, N_RNN_BLOCKS, 32, RNN_BLOCK), bf16), SDS((8, D), f32)],
        scratch_shapes=[pltpu.VMEM((rows, RNN_BLOCK), f32)] * 6 + [pltpu.VMEM((2, RNN_BLOCK, RNN_BLOCK), f32)],
        name="rnn_bwd", compiler_params=_cp(("arbitrary",), 48),
    )(dya, hr, xc, z, z, smallw, conv_b, wrg, b_ra, b_ri, lam)


def _rope_tables(rows):
    half = jnp.arange(HALF, dtype=f32)
    inv = ROPE_THETA ** (-half / HALF)
    pos = (jnp.arange(rows) - ROW0).astype(f32)
    ang = pos[:, None] * inv[None, :]
    cos, sin = jnp.cos(ang), jnp.sin(ang)
    cos128 = jnp.concatenate([cos, cos, cos, cos], axis=1)
    sin128 = jnp.concatenate([-sin, sin, -sin, sin], axis=1)
    return cos128, sin128


def _rope128(x, cos128, sin128):
    lane = lax.broadcasted_iota(jnp.int32, x.shape, 1)
    swapped = jnp.where(lane % HEAD_DIM < HALF, pltpu.roll(x, 128 - HALF, 1), pltpu.roll(x, HALF, 1))
    return x * cos128 + swapped * sin128


def _qkv_prep(z, cos128, sin128):
    rows = z.shape[0]

    def body(q_ref, kv_ref, c_ref, s_ref, qo_ref, ko_ref, vo_ref):
        c, s = c_ref[...], s_ref[...]
        for g in range(D // 128):
            qo_ref[:, g * 128:(g + 1) * 128] = (_rope128(q_ref[:, g * 128:(g + 1) * 128], c, s)
                                                * (HEAD_DIM ** -0.5)).astype(bf16)
        for g in range(2):
            kr = _rope128(kv_ref[:, g * 128:(g + 1) * 128], c, s)
            for j in range(2):
                ko_ref[2 * g + j] = kr[:, j * HEAD_DIM:(j + 1) * HEAD_DIM].astype(bf16)
        for h in range(N_KV):
            vo_ref[h] = kv_ref[:, 256 + h * HEAD_DIM:256 + (h + 1) * HEAD_DIM].astype(bf16)

    return pl.pallas_call(
        body, grid=(rows // BLK,),
        in_specs=[pl.BlockSpec((BLK, D), lambda i: (i, OFF_Q // D)),
                  pl.BlockSpec((BLK, 512), lambda i: (i, OFF_K // 512)),
                  pl.BlockSpec((BLK, 128), lambda i: (i, 0)),
                  pl.BlockSpec((BLK, 128), lambda i: (i, 0))],
        out_specs=[pl.BlockSpec((BLK, D), lambda i: (i, 0)),
                   pl.BlockSpec((N_KV, BLK, HEAD_DIM), lambda i: (0, i, 0)),
                   pl.BlockSpec((N_KV, BLK, HEAD_DIM), lambda i: (0, i, 0))],
        out_shape=[SDS((rows, D), bf16), SDS((N_KV, rows, HEAD_DIM), bf16), SDS((N_KV, rows, HEAD_DIM), bf16)],
        name="qkv_prep", compiler_params=_cp(("arbitrary",)),
    )(z, z, cos128, sin128)


def _attn_mask(n):
    qi = n * BLK + lax.broadcasted_iota(jnp.int32, (BLK, 2 * BLK + N_META), 0)
    c = lax.broadcasted_iota(jnp.int32, (BLK, 2 * BLK + N_META), 1)
    jb = (n - 1) * BLK + c
    band = (jb >= BLK) & (jb <= qi) & (qi - jb < BLK)
    meta = (ROW0 + c - 2 * BLK) <= qi
    return ((c < 2 * BLK) & band) | ((c >= 2 * BLK) & meta)


N_KEYS = 2 * BLK + N_META


def _stack_heads(t):
    return jnp.concatenate([t[:, g * HEAD_DIM:(g + 1) * HEAD_DIM] for g in range(GROUP)], axis=0)


def _sink_column(sink_ref, h):
    g = lax.broadcasted_iota(jnp.int32, (GROUP, 1, 1), 0)
    col = jnp.zeros((GROUP, 1, 1), f32)
    for j in range(GROUP):
        col = jnp.where(g == j, sink_ref[h * GROUP + j], col)
    return col


def _kv_specs(last):
    cl = lambda n: jnp.minimum(n, last)
    return [pl.BlockSpec((None, N_META, HEAD_DIM), lambda h, n: (h, ROW0 // N_META, 0)),
            pl.BlockSpec((None, BLK, HEAD_DIM), lambda h, n: (h, jnp.maximum(cl(n) - 1, 0), 0)),
            pl.BlockSpec((None, BLK, HEAD_DIM), lambda h, n: (h, cl(n), 0))]


def _attn_fwd(q_r, k_r, v_b, z, sinks):
    rows = q_r.shape[0]
    nb = rows // BLK

    def body(sink_ref, q_ref, km_ref, kp_ref, kc_ref, vm_ref, vp_ref, vc_ref, ga_ref, o_ref, yb_ref, ybt_ref, lse_ref):
        h, n = pl.program_id(0), pl.program_id(1)
        kk = jnp.concatenate([kp_ref[...], kc_ref[...], km_ref[...]], axis=0)
        vv = jnp.concatenate([vp_ref[...], vc_ref[...], vm_ref[...]], axis=0)
        q2 = _stack_heads(q_ref[...])
        s = jnp.where(_attn_mask(n)[None], _dot_nt(q2, kk).reshape(GROUP, BLK, N_KEYS), NEG_INF)
        sink = _sink_column(sink_ref, h)
        m = jnp.maximum(jnp.max(s, axis=-1, keepdims=True), sink)
        p = jnp.exp(s - m)
        den = jnp.sum(p, axis=-1, keepdims=True) + jnp.exp(sink - m)
        o2 = _dot((p / den).astype(bf16).reshape(GROUP * BLK, N_KEYS), vv)
        lse = m + jnp.log(den)
        for g in range(GROUP):
            o_ref[:, g * HEAD_DIM:(g + 1) * HEAD_DIM] = o2[g * BLK:(g + 1) * BLK]
            lse_ref[:, g:g + 1] = lse[g]
        yb = o_ref[...] * _silu_and_grad(ga_ref[...])[0]
        yb_ref[...] = yb.astype(bf16)
        ybt_ref[...] = yb.T.astype(bf16)

    tile = pl.BlockSpec((BLK, 512), lambda h, n: (n, h))
    return pl.pallas_call(
        body, grid=(N_KV, nb),
        in_specs=[pl.BlockSpec(memory_space=pltpu.SMEM), tile] + _kv_specs(nb - 1) + _kv_specs(nb - 1)
                 + [pl.BlockSpec((BLK, 512), lambda h, n: (n, OFF_GA // 512 + h))],
        out_specs=[tile, tile, pl.BlockSpec((512, BLK), lambda h, n: (h, n)),
                   pl.BlockSpec((None, BLK, GROUP), lambda h, n: (h, n, 0))],
        out_shape=[SDS((rows, D), f32), SDS((rows, D), bf16), SDS((D, rows), bf16),
                   SDS((N_KV, rows, GROUP), f32)],
        name="attn_fwd", compiler_params=_cp(("arbitrary", "arbitrary")),
    )(sinks, q_r, k_r, k_r, k_r, v_b, v_b, v_b, z)


def _attn_bwd(dyb, o32, lse, q_r, k_r, v_b, z, sinks):
    rows = q_r.shape[0]
    nb = rows // BLK
    cl = lambda n: jnp.minimum(n, nb - 1)

    def body(sink_ref, dyb_ref, o_ref, lse_ref, q_ref, km_ref, kp_ref, kc_ref, vm_ref, vp_ref, vc_ref, ga_ref,
             dq_ref, dga_ref, dk_ref, dv_ref, dkm_ref, dvm_ref, dsr_ref, ck_s, cv_s):
        h, n = pl.program_id(0), pl.program_id(1)

        @pl.when(n == 0)
        def _():
            dkm_ref[...] = jnp.zeros_like(dkm_ref)
            dvm_ref[...] = jnp.zeros_like(dvm_ref)
            ck_s[...] = jnp.zeros_like(ck_s)
            cv_s[...] = jnp.zeros_like(cv_s)

        @pl.when(n < nb)
        def _():
            kk = jnp.concatenate([kp_ref[...], kc_ref[...], km_ref[...]], axis=0)
            vv = jnp.concatenate([vp_ref[...], vc_ref[...], vm_ref[...]], axis=0)
            sg, dsg = _silu_and_grad(ga_ref[...])
            dyb_v = dyb_ref[...]
            o_v = o_ref[...]
            dga_ref[...] = (dyb_v * o_v * dsg).astype(bf16)
            q2 = _stack_heads(q_ref[...])
            do2 = _stack_heads(dyb_v * sg)
            lse_v = lse_ref[...]
            lse = jnp.concatenate([lse_v[:, g:g + 1] for g in range(GROUP)], axis=0).reshape(GROUP, BLK, 1)
            delta = jnp.sum(do2 * _stack_heads(o_v), axis=-1, keepdims=True).reshape(GROUP, BLK, 1)
            s = jnp.where(_attn_mask(n)[None], _dot_nt(q2, kk).reshape(GROUP, BLK, N_KEYS), NEG_INF)
            p = jnp.exp(s - lse)
            do2b = do2.astype(bf16)
            ds = (p * (_dot_nt(do2b, vv).reshape(GROUP, BLK, N_KEYS) - delta)).astype(bf16)
            ds = ds.reshape(GROUP * BLK, N_KEYS)
            dsr = -jnp.exp(_sink_column(sink_ref, h) - lse) * delta
            dq2 = _dot(ds, kk)
            for g in range(GROUP):
                dq_ref[:, g * HEAD_DIM:(g + 1) * HEAD_DIM] = dq2[g * BLK:(g + 1) * BLK]
                dsr_ref[:, g:g + 1] = dsr[g]
            dkk = _dot_tn(ds, q2)
            dvv = _dot_tn(p.astype(bf16).reshape(GROUP * BLK, N_KEYS), do2b)
            dk_ref[...] = ck_s[...] + dkk[:BLK]
            dv_ref[...] = cv_s[...] + dvv[:BLK]
            ck_s[...] = dkk[BLK:2 * BLK]
            cv_s[...] = dvv[BLK:2 * BLK]
            dkm_ref[...] += dkk[2 * BLK:]
            dvm_ref[...] += dvv[2 * BLK:]

        @pl.when(n == nb)
        def _():
            dk_ref[...] = ck_s[...]
            dv_ref[...] = cv_s[...]

    tile = pl.BlockSpec((BLK, 512), lambda h, n: (cl(n), h))
    kvout = pl.BlockSpec((None, BLK, HEAD_DIM), lambda h, n: (h, jnp.maximum(n - 1, 0), 0))
    mout = pl.BlockSpec((None, N_META, HEAD_DIM), lambda h, n: (h, 0, 0))
    stat = pl.BlockSpec((None, BLK, GROUP), lambda h, n: (h, cl(n), 0))
    return pl.pallas_call(
        body, grid=(N_KV, nb + 1),
        in_specs=[pl.BlockSpec(memory_space=pltpu.SMEM), tile, tile, stat, tile] + _kv_specs(nb - 1)
                 + _kv_specs(nb - 1) + [pl.BlockSpec((BLK, 512), lambda h, n: (cl(n), OFF_GA // 512 + h))],
        out_specs=[tile, tile, kvout, kvout, mout, mout, stat],
        out_shape=[SDS((rows, D), f32), SDS((rows, D), bf16),
                   SDS((N_KV, rows, HEAD_DIM), f32), SDS((N_KV, rows, HEAD_DIM), f32),
                   SDS((N_KV, N_META, HEAD_DIM), f32), SDS((N_KV, N_META, HEAD_DIM), f32),
                   SDS((N_KV, rows, GROUP), f32)],
        scratch_shapes=[pltpu.VMEM((BLK, HEAD_DIM), f32), pltpu.VMEM((BLK, HEAD_DIM), f32)],
        name="attn_bwd", compiler_params=_cp(("arbitrary", "arbitrary")),
    )(sinks, dyb, o32, lse, q_r, k_r, k_r, k_r, v_b, v_b, v_b, z)


def _qkv_finish(dq, dk, dv, dkm, dvm, cos128, sin128):
    rows = dq.shape[0]

    def body(dq_ref, dk_ref, dv_ref, dkm_ref, dvm_ref, c_ref, s_ref, oq_ref, okv_ref):
        first = (pl.program_id(0) == 0).astype(f32)
        c, s = c_ref[...], -s_ref[...]
        for g in range(D // 128):
            oq_ref[:, g * 128:(g + 1) * 128] = (_rope128(dq_ref[:, g * 128:(g + 1) * 128], c, s)
                                                * (HEAD_DIM ** -0.5)).astype(bf16)
        pad = jnp.zeros((ROW0, HEAD_DIM), f32)
        ks = [dk_ref[h] + first * jnp.concatenate([pad, dkm_ref[h]], axis=0) for h in range(N_KV)]
        vs = [dv_ref[h] + first * jnp.concatenate([pad, dvm_ref[h]], axis=0) for h in range(N_KV)]
        for g in range(2):
            kp = jnp.concatenate([ks[2 * g], ks[2 * g + 1]], axis=1)
            okv_ref[:, g * 128:(g + 1) * 128] = _rope128(kp, c, s).astype(bf16)
            okv_ref[:, 256 + g * 128:256 + (g + 1) * 128] = jnp.concatenate([vs[2 * g], vs[2 * g + 1]], axis=1).astype(bf16)

    kv = pl.BlockSpec((N_KV, BLK, HEAD_DIM), lambda i: (0, i, 0))
    mt = pl.BlockSpec((N_KV, N_META, HEAD_DIM), lambda i: (0, 0, 0))
    return pl.pallas_call(
        body, grid=(rows // BLK,),
        in_specs=[pl.BlockSpec((BLK, D), lambda i: (i, 0)), kv, kv, mt, mt,
                  pl.BlockSpec((BLK, 128), lambda i: (i, 0)), pl.BlockSpec((BLK, 128), lambda i: (i, 0))],
        out_specs=[pl.BlockSpec((BLK, D), lambda i: (i, 0)), pl.BlockSpec((BLK, 512), lambda i: (i, 0))],
        out_shape=[SDS((rows, D), bf16), SDS((rows, 512), bf16)],
        name="qkv_finish", compiler_params=_cp(("arbitrary",)),
    )(dq, dk, dv, dkm, dvm, cos128, sin128)


_TW = 512


def _mix_specs(rows):
    tr = _row_chunk(rows)
    tile = pl.BlockSpec((tr, _TW), lambda i, j: (i, j))
    ga = pl.BlockSpec((tr, _TW), lambda i, j: (i, OFF_G // _TW + j))
    gb = pl.BlockSpec((tr, _TW), lambda i, j: (i, (OFF_G + D) // _TW + j))
    return (rows // tr, D // _TW), tile, ga, gb


def _mix_fwd(y_a, y_b, z):
    rows = y_a.shape[0]
    tw = 256
    col = lambda off: pl.BlockSpec((rows, tw), lambda j: (0, off // tw + j))

    def body(ya_ref, yb_ref, ga_ref, gb_ref, o_ref, ot_ref):
        mixed = _sigmoid(ga_ref[...]) * ya_ref[...] + _sigmoid(gb_ref[...]) * yb_ref[...]
        o_ref[...] = mixed.astype(bf16)
        ot_ref[...] = mixed.T.astype(bf16)

    return pl.pallas_call(
        body, grid=(D // tw,), in_specs=[col(0), col(0), col(OFF_G), col(OFF_G + D)],
        out_specs=[col(0), pl.BlockSpec((tw, rows), lambda j: (j, 0))],
        out_shape=[SDS((rows, D), bf16), SDS((D, rows), bf16)],
        name="mix_fwd", compiler_params=_cp(("arbitrary",)),
    )(y_a, y_b, z, z)


def _mix_bwd(dmixed, y_a, y_b, z):
    rows = y_a.shape[0]
    grid, _mix_tile, _mix_ga, _mix_gb = _mix_specs(rows)

    def body(dm_ref, ya_ref, yb_ref, ga_ref, gb_ref, dya_ref, dyb_ref, dga_ref, dgb_ref):
        dm = dm_ref[...]
        sa, sb = _sigmoid(ga_ref[...]), _sigmoid(gb_ref[...])
        dya_ref[...] = (dm * sa).astype(bf16)
        dyb_ref[...] = (dm * sb).astype(bf16)
        dga_ref[...] = (dm * ya_ref[...] * sa * (1.0 - sa)).astype(bf16)
        dgb_ref[...] = (dm * yb_ref[...] * sb * (1.0 - sb)).astype(bf16)

    return pl.pallas_call(
        body, grid=grid, in_specs=[_mix_tile, _mix_tile, _mix_tile, _mix_ga, _mix_gb],
        out_specs=[_mix_tile] * 4, out_shape=[SDS((rows, D), bf16)] * 4,
        name="mix_bwd", compiler_params=_cp(("arbitrary", "arbitrary")),
    )(dmixed, y_a, y_b, z, z)


def _final_ln(out32, h32, tgt, ln_g, ln_b):
    rows = out32.shape[0]

    def body(o_ref, h_ref, t_ref, g_ref, b_ref, du_ref, dub_ref, st_ref):
        i = pl.program_id(0)
        g = g_ref[...]
        y, xhat, rstd = _ln_rows(ALPHA * h_ref[...] + o_ref[...], g, b_ref[...])
        e = jnp.where(i > 0, y - t_ref[0], 0.0)
        dy = e * (1.0 / D)
        du = _ln_rows_bwd(dy, g, xhat, rstd)
        du_ref[...] = du
        dub_ref[...] = du.astype(bf16)
        st = jnp.concatenate([_colsum(dy * xhat), _colsum(dy), _colsum(du), _colsum(e * e) * (0.5 / D),
                              jnp.zeros((4, D), f32)], axis=0)

        @pl.when(i == 0)
        def _():
            st_ref[...] = st

        @pl.when(i > 0)
        def _():
            st_ref[...] += st

    row = pl.BlockSpec((BLK, D), lambda i: (i, 0))
    vec = pl.BlockSpec((1, D), lambda i: (0, 0))
    return pl.pallas_call(
        body, grid=(rows // BLK,),
        in_specs=[row, row, pl.BlockSpec((1, BLK, D), lambda i: (0, jnp.maximum(i - 1, 0), 0)), vec, vec],
        out_specs=[row, row, pl.BlockSpec((8, D), lambda i: (0, 0))],
        out_shape=[SDS((rows, D), f32), SDS((rows, D), bf16), SDS((8, D), f32)],
        name="final_ln", compiler_params=_cp(("arbitrary",)),
    )(out32, h32, tgt, ln_g, ln_b)


def _assemble_dz(dxr, dgr, dq, dkv, dga, dma, dmb):
    rows = dxr.shape[0]
    parts = [(dxr, D), (dgr, D), (dq, D), (dkv, 512), (dga, D), (dma, D), (dmb, D)]

    def body(*refs):
        o_ref = refs[-1]
        off = 0
        for r, (_, w) in zip(refs[:-1], parts):
            o_ref[:, off:off + w] = r[...]
            off += w

    return pl.pallas_call(
        body, grid=(rows // BLK,),
        in_specs=[pl.BlockSpec((BLK, w), lambda i: (i, 0)) for _, w in parts],
        out_specs=pl.BlockSpec((BLK, D_IN), lambda i: (i, 0)),
        out_shape=SDS((rows, D_IN), bf16), name="assemble_dz", compiler_params=_cp(("arbitrary",)),
    )(*[p for p, _ in parts])


def _step_rnn(h32, hb, z, wrg, smallw, p, zero):
    rows = z.shape[0]
    cos128, sin128 = _rope_tables(rows)
    cos128 = cos128 + zero
    xc, hr, ya, ya_t = _rnn_fwd(z, smallw, p["conv_b"] + zero, wrg, p["b_ra"], p["b_ri"], p["lru_lambda"])
    q_r, k_r, v_b = _qkv_prep(z, cos128, sin128)
    return dict(cos128=cos128, sin128=sin128, h32=h32, hb=hb, z=z, xc=xc, hr=hr, ya=ya, ya_t=ya_t,
                q_r=q_r, k_r=k_r, v_b=v_b)


def _step_attn(s, p, zero):
    sinks = p["sinks"].reshape(N_KV * GROUP) + zero[0]
    o32, yb, yb_t, lse = _attn_fwd(s["q_r"], s["k_r"], s["v_b"], s["z"], sinks)
    return dict(s, sinks=sinks, o32=o32, yb=yb, yb_t=yb_t, lse=lse)


def _step_merge(s, tgt, w3, p):
    ya, yb, z = s["ya"], s["yb"], s["z"]
    y_a = _mm(ya, w3, sel=0, name="mm_ya")
    y_b = _mm(yb, w3, sel=1, name="mm_yb")
    mixed, mixed_t = _mix_fwd(y_a, y_b, z)
    out32 = _mm(mixed, w3, sel=2, bias=p["b_o"], name="mm_out")
    du32, dub, st_out = _final_ln(out32, s["h32"], tgt, p["ln_g"], p["ln_b"])

    g_wo = _mm(mixed_t, dub, out_dtype=bf16, name="mm_dwo")
    dmixed = _mm(dub, w3, sel=2, nt=True, name="mm_dmixed")
    dya_b, dyb_b, dma, dmb = _mix_bwd(dmixed, y_a, y_b, z)
    g_wrnn = _mm(s["ya_t"], dya_b, out_dtype=bf16, name="mm_dwrnn")
    g_wattn = _mm(s["yb_t"], dyb_b, out_dtype=bf16, name="mm_dwattn")
    dya = _mm(dya_b, w3, sel=0, nt=True, name="mm_dya")
    dyb = _mm(dyb_b, w3, sel=1, nt=True, name="mm_dyb")
    return dict(du32=du32, st_out=st_out, dma=dma, dmb=dmb, dya=dya, dyb=dyb, g_wo=g_wo, g_wrnn=g_wrnn,
                g_wattn=g_wattn)


def _step_backward(s, t, wrg, smallw, p, conv_b):
    z = s["z"]
    dxr, dgr, g_wrg, vec_rnn = _rnn_bwd(t["dya"], s["hr"], s["xc"], z, smallw, conv_b, wrg, p["b_ra"], p["b_ri"],
                                        p["lru_lambda"])
    dq_r, dga, dk, dv, dkm, dvm, dsr = _attn_bwd(t["dyb"], s["o32"], s["lse"], s["q_r"], s["k_r"], s["v_b"], z,
                                                 s["sinks"])
    dq, dkv = _qkv_finish(dq_r, dk, dv, dkm, dvm, s["cos128"], s["sin128"])
    dz = _assemble_dz(dxr, dgr, dq, dkv, dga, t["dma"], t["dmb"])
    return dict(vec_rnn=vec_rnn, dsr=dsr, g_wrg=g_wrg, dz=dz)


def _step_input_grad(dz, w_t, after, du32, x, smallw, p):
    dh = _mm_dh(dz, w_t, after)
    grad_x, dmeta, st_emb = _ln_emb_bwd(dh, du32, x, smallw, p["ln_emb_g"])
    return dict(grad_x=grad_x, dmeta=dmeta, st_emb=st_emb)


_ANY = pl.BlockSpec(memory_space=pl.ANY)
_VMEM = pl.BlockSpec(memory_space=pltpu.VMEM)


def _place():
    x, y, c = lax.axis_index("x"), lax.axis_index("y"), lax.axis_index("c")
    return x, y, c


def _dev(px, py, pc):
    return 4 * px + 2 * py + pc


def _tile_rows(r):
    return max(t for t in range(16, 321, 16) if r % t == 0) if r > 320 else r


def _cast_w_in(w_in_t):
    tm = _tile_rows(SHARD_IN)

    def body(i_ref, o_ref):
        o_ref[...] = i_ref[...].astype(bf16)

    return pl.pallas_call(
        body, grid=(SHARD_IN // tm,),
        in_specs=[pl.BlockSpec((tm, D), lambda i: (i, 0))],
        out_specs=pl.BlockSpec((tm, D), lambda i: (i, 0)),
        out_shape=SDS((SHARD_IN, D), bf16), name="cast_w_in", compiler_params=_cp(("arbitrary",)),
    )(w_in_t)


def _cast_small(w_rnn_out, w_attn_out, w_o, w_ra, w_ri, meta, conv_w):
    def body(a_ref, b_ref, c_ref, ra_ref, ri_ref, m_ref, cw_ref, w3_ref, wrg_ref, sw_ref):
        w3_ref[0] = a_ref[0].astype(bf16)
        w3_ref[1] = b_ref[0].astype(bf16)
        w3_ref[2] = c_ref[0].astype(bf16)
        wrg_ref[0] = ra_ref[0].astype(bf16)
        wrg_ref[1] = ri_ref[0].astype(bf16)
        sw_ref[...] = jnp.concatenate([m_ref[...], cw_ref[0], jnp.zeros((4, 256), f32)], axis=0)

    return pl.pallas_call(
        body,
        out_shape=[SDS((3, 256, D), bf16), SDS((2, N_RNN_BLOCKS, 32, RNN_BLOCK), bf16), SDS((24, 256), f32)],
        name="cast_small", compiler_params=_cp(None),
    )(w_rnn_out, w_attn_out, w_o, w_ra, w_ri, meta, conv_w)


def _all_gather(shards, later):
    n = len(shards)
    nl = len(later)

    def body(*refs):
        ins, outs = refs[:n], refs[n + nl:2 * n + nl]
        send_sems, recv_sems, local_sems = refs[2 * (n + nl):]
        x, y, c = _place()
        me, sibling = (x, y, c), (x, y, 1 - c)
        chips = [(1 - x, y), (x, 1 - y), (1 - x, 1 - y)]

        def copy(a, k, block, to, src=None):
            dst = outs[a].at[_dev(*block)]
            return pltpu.make_async_remote_copy(
                src_ref=dst if src is None else src, dst_ref=dst,
                send_sem=send_sems.at[a * 7 + k], recv_sem=recv_sems.at[a * 7 + k],
                device_id=to, device_id_type=MESH)

        all_ins, all_outs = refs[:n + nl], refs[n + nl:2 * (n + nl)]
        mine = [pltpu.make_async_copy(all_ins[a], all_outs[a].at[_dev(*me)], local_sems.at[a]) for a in range(n + nl)]
        for cp in mine:
            cp.start()
        first = []
        for a in range(n):
            first.append(copy(a, 0, me, sibling, src=ins[a]))
            first += [copy(a, 1 + j, me, (*chip, c), src=ins[a]) for j, chip in enumerate(chips)]
        for cp in first:
            cp.start()
        passed = []
        for a in range(n):
            for j, chip in enumerate(chips):
                copy(a, 1 + j, (*chip, c), me).wait_recv()
                cp = copy(a, 4 + j, (*chip, c), sibling)
                cp.start()
                passed.append(cp)
        for a in range(n):
            copy(a, 0, sibling, me).wait_recv()
            for j, chip in enumerate(chips):
                copy(a, 4 + j, (*chip, 1 - c), me).wait_recv()
        for cp in first + passed:
            cp.wait_send()
        for cp in mine:
            cp.wait()

    return pl.pallas_call(
        body, in_specs=[_ANY] * (n + nl), out_specs=[_ANY] * (n + nl),
        out_shape=[SDS((N_DEV, *s.shape), s.dtype) for s in (*shards, *later)],
        scratch_shapes=[pltpu.SemaphoreType.DMA((7 * n,)), pltpu.SemaphoreType.DMA((7 * n,)),
                        pltpu.SemaphoreType.DMA((n + nl,))],
        name="all_gather_weights",
    )(*shards, *later)


def _gather_small(shard):
    def body(s_ref, o_ref, send_sems, recv_sems):
        x, y, c = _place()
        me = _dev(x, y, c)
        copies = []
        for k, (fx, fy, fc) in enumerate(_PEER_FLIPS):
            peer = ((x + fx) % 2, (y + fy) % 2, (c + fc) % 2)
            copies.append(_remote(s_ref, o_ref.at[me], send_sems, recv_sems, k, peer))
        for cp in copies:
            cp.start()
        o_ref[me] = s_ref[...]
        for cp in copies:
            cp.wait()

    return pl.pallas_call(
        body, in_specs=[_VMEM], out_specs=_VMEM, out_shape=SDS((N_DEV, *shard.shape), shard.dtype),
        scratch_shapes=[pltpu.SemaphoreType.DMA((7,)), pltpu.SemaphoreType.DMA((7,))],
        name="gather_small",
    )(shard)


def _gather_project(w_s, smalls, later, hb, b_in, order):
    arrays = (w_s, *smalls, *later)
    na, n = len(arrays), 1 + len(smalls)
    rows = hb.shape[0]
    cm = _row_chunk(rows)
    nm = rows // cm
    pair = 2 * SHARD_IN

    def body(order_ref, *refs):
        ins, hb_ref, b_ref = refs[:na], refs[na], refs[na + 1]
        outs, z_ref = refs[na + 2:2 * na + 2], refs[2 * na + 2]
        wbuf, send_sems, recv_sems, local_sems, load_sems = refs[2 * na + 3:]
        k, mi = pl.program_id(0), pl.program_id(1)
        x, y, c = _place()
        me, sibling = (x, y, c), (x, y, 1 - c)
        chips = [(1 - x, y), (x, 1 - y), (1 - x, 1 - y)]

        def copy(a, kk, block, to, src=None):
            dst = outs[a].at[_dev(*block)]
            return pltpu.make_async_remote_copy(
                src_ref=dst if src is None else src, dst_ref=dst,
                send_sem=send_sems.at[a * 7 + kk], recv_sem=recv_sems.at[a * 7 + kk],
                device_id=to, device_id_type=MESH)

        mine = [pltpu.make_async_copy(ins[a], outs[a].at[_dev(*me)], local_sems.at[a]) for a in range(na)]

        def to_sibling():
            return [copy(a, 0, me, sibling, src=ins[a]) for a in range(n)]

        def to_chip(j):
            return [copy(a, 1 + j, me, (*chips[j], c), src=ins[a]) for a in range(n)]

        def load_pair(chip):
            cps = [pltpu.make_async_copy(outs[0].at[_dev(*chip, cc)], wbuf.at[pl.ds(cc * SHARD_IN, SHARD_IN)],
                                         load_sems.at[cc]) for cc in range(2)]
            for cp in cps:
                cp.start()
            for cp in cps:
                cp.wait()

        @pl.when((k == 0) & (mi == 0))
        def _():
            for cp in mine + to_sibling() + to_chip(0) + to_chip(1):
                cp.start()
            mine[0].wait()
            copy(0, 0, sibling, me).wait_recv()
            load_pair((x, y))

        for j, chip in enumerate(chips):
            @pl.when((k == j + 1) & (mi == 0))
            def _():
                for a in range(n):
                    copy(a, 1 + j, (*chip, c), me).wait_recv()
                    copy(a, 4 + j, (*chip, c), sibling).start()
                if j == 0:
                    for cp in to_chip(2):
                        cp.start()
                copy(0, 4 + j, (*chip, 1 - c), me).wait_recv()
                load_pair(chip)

        z_ref[...] = _dot_nt(hb_ref[...], wbuf[...]) + b_ref[...]

        @pl.when((k == len(chips)) & (mi == nm - 1))
        def _():
            for a in range(1, n):
                copy(a, 0, sibling, me).wait_recv()
                for j, chip in enumerate(chips):
                    copy(a, 4 + j, (*chip, 1 - c), me).wait_recv()
            for cp in to_sibling() + to_chip(0) + to_chip(1) + to_chip(2):
                cp.wait_send()
            for a in range(n):
                for j, chip in enumerate(chips):
                    copy(a, 4 + j, (*chip, c), sibling).wait_send()
            for cp in mine[1:]:
                cp.wait()

    res = pl.pallas_call(
        body,
        grid_spec=pltpu.PrefetchScalarGridSpec(
            num_scalar_prefetch=1, grid=(N_DEV // 2, nm),
            in_specs=[_ANY] * na + [pl.BlockSpec((cm, D), lambda k, i, o: (i, 0)),
                                    pl.BlockSpec((1, pair), lambda k, i, o: (0, o[k]))],
            out_specs=[_ANY] * na + [pl.BlockSpec((cm, pair), lambda k, i, o: (i, o[k]))],
            scratch_shapes=[pltpu.VMEM((pair, D), bf16), pltpu.SemaphoreType.DMA((7 * n,)),
                            pltpu.SemaphoreType.DMA((7 * n,)), pltpu.SemaphoreType.DMA((na,)),
                            pltpu.SemaphoreType.DMA((2,))]),
        out_shape=[SDS((N_DEV, *s.shape), s.dtype) for s in arrays] + [SDS((rows, D_IN), f32)],
        name="gather_project", compiler_params=_cp(("arbitrary", "arbitrary"), 48),
    )(order, *arrays, hb, b_in)
    return res[:na], res[na]


_HBM = pl.BlockSpec(memory_space=pltpu.HBM)
_SEM = pl.BlockSpec(memory_space=pltpu.SEMAPHORE)
_PEER_FLIPS = [(f // 4, (f // 2) % 2, f % 2) for f in range(1, N_DEV)]


def _remote(src, dst, send_sems, recv_sems, k, to):
    return pltpu.make_async_remote_copy(src_ref=src, dst_ref=dst, send_sem=send_sems.at[k], recv_sem=recv_sems.at[k],
                                        device_id=to, device_id_type=MESH)


def _copies_direct(same_src):
    def make(srcs, lands, send_sems, recv_sems):
        x, y, c = _place()
        me = _dev(x, y, c)
        out = []
        for a in range(len(srcs)):
            for k, (fx, fy, fc) in enumerate(_PEER_FLIPS):
                peer = ((x + fx) % 2, (y + fy) % 2, (c + fc) % 2)
                src = srcs[a] if same_src else srcs[a].at[_dev(*peer)]
                out.append(_remote(src, lands[a].at[me], send_sems, recv_sems, 7 * a + k, peer))
        return out
    return make


def _copies_gather_chips(srcs, lands, send_sems, recv_sems):
    x, y, c = _place()
    chips = [(1 - x, y), (x, 1 - y), (1 - x, 1 - y)]
    return [_remote(srcs[a], lands[a].at[_dev(x, y, c)], send_sems, recv_sems, 3 * a + j, (qx, qy, c))
            for a in range(len(srcs)) for j, (qx, qy) in enumerate(chips)]


def _copies_gather_sibling(srcs, lands, send_sems, recv_sems):
    x, y, c = _place()
    chips = [(x, y), (1 - x, y), (x, 1 - y), (1 - x, 1 - y)]
    return [_remote(lands[a].at[_dev(qx, qy, c)], lands[a].at[_dev(qx, qy, c)], send_sems, recv_sems, 4 * a + j,
                    (x, y, 1 - c))
            for a in range(len(srcs)) for j, (qx, qy) in enumerate(chips)]


def _copies_siblings(srcs, lands, send_sems, recv_sems):
    x, y, c = _place()
    return [_remote(srcs[a].at[2 * q + (1 - c)], lands[a].at[q], send_sems, recv_sems, 4 * a + q, (x, y, 1 - c))
            for a in range(len(srcs)) for q in range(4)]


def _copies_chips(srcs, lands, send_sems, recv_sems):
    x, y, c = _place()
    chips = [(1 - x, y), (x, 1 - y), (1 - x, 1 - y)]
    return [_remote(srcs[a].at[2 * qx + qy], lands[a].at[j], send_sems, recv_sems, 3 * a + j, (qx, qy, c))
            for a in range(len(srcs)) for j, (qx, qy) in enumerate(chips)]


def _split_start(make, per_array, srcs, lands, dep, name):
    n = len(srcs)

    def body(*refs):
        send_sems, recv_sems, token = refs[2 * n + 1], refs[2 * n + 2], refs[-1]
        for cp in make(refs[:n], refs[n:2 * n], send_sems, recv_sems):
            cp.start()
        token[...] = jnp.zeros_like(token)

    hbm = lambda t: pltpu.with_memory_space_constraint(t, pltpu.HBM)
    res = pl.pallas_call(
        body, name=name,
        out_shape=(pltpu.SemaphoreType.DMA((per_array * n,)), pltpu.SemaphoreType.DMA((per_array * n,)),
                   *[pltpu.HBM(t.shape, t.dtype) for t in (*srcs, *lands)], SDS((8, 128), f32)),
        in_specs=[_HBM] * (2 * n) + [_ANY], out_specs=(_SEM, _SEM, *([_HBM] * (2 * n)), _VMEM),
        input_output_aliases={i: 2 + i for i in range(2 * n)},
        compiler_params=pltpu.CompilerParams(has_side_effects=pltpu.SideEffectType.DATAFLOW_SIDE_EFFECTING),
    )(*[hbm(t) for t in (*srcs, *lands)], dep)
    return res[0], res[1], list(res[2:2 + n]), list(res[2 + n:2 + 2 * n]), res[-1]


def _split_wait(make, send_sems, recv_sems, srcs, lands, after, name):
    n = len(srcs)

    def body(*refs):
        for cp in make(refs[:n], refs[n:2 * n], refs[2 * n], refs[2 * n + 1]):
            cp.wait_send()
            cp.wait_recv()

    res = pl.pallas_call(
        body, name=name,
        out_shape=tuple(pltpu.HBM(t.shape, t.dtype) for t in (*srcs, *lands)),
        in_specs=[_HBM] * (2 * n) + [_SEM, _SEM, _ANY], out_specs=tuple([_HBM] * (2 * n)),
        input_output_aliases={i: i for i in range(2 * n)},
        compiler_params=pltpu.CompilerParams(has_side_effects=pltpu.SideEffectType.DATAFLOW_SIDE_EFFECTING),
    )(*srcs, *lands, send_sems, recv_sems, after)
    return list(res[:n]), list(res[n:])


def _adamw_direct(g, land, me_idx, w, m, v, name):
    r, wd = w.shape
    tr = min(r, 256)

    def body(me_ref, *refs):
        g_ref, peers = refs[0], refs[1:N_DEV]
        w_ref, m_ref, v_ref, g_out, d_out, m_out, v_out = refs[N_DEV:]
        gs = g_ref[...].astype(f32)
        for p_ref in peers:
            gs = gs + p_ref[...].astype(f32)
        d, mn, vn = _adamw(w_ref[...], gs, m_ref[...], v_ref[...])
        g_out[...] = gs
        d_out[...] = d
        m_out[...] = mn
        v_out[...] = vn

    tile = pl.BlockSpec((tr, wd), lambda i, me_ref: (i, 0))
    slot = lambda k: pl.BlockSpec((None, tr, wd), lambda i, me_ref: ((me_ref[0] + k) % N_DEV, i, 0))
    return pl.pallas_call(
        body,
        grid_spec=pltpu.PrefetchScalarGridSpec(
            num_scalar_prefetch=1, grid=(r // tr,),
            in_specs=[slot(0)] + [slot(k) for k in range(1, N_DEV)] + [tile, tile, tile],
            out_specs=[tile] * 4),
        out_shape=[SDS((r, wd), f32)] * 4, name=name, compiler_params=_cp(("arbitrary",), 48),
    )(me_idx, g, *([land] * (N_DEV - 1)), w, m, v)


def _pair_sum(g, r1, c_idx, name):
    _, r, w = g.shape
    tr = _tile_rows(r)

    def body(c_ref, g_ref, r_ref, o_ref):
        o_ref[...] = (g_ref[...].astype(f32) + r_ref[...].astype(f32)).astype(bf16)

    return pl.pallas_call(
        body,
        grid_spec=pltpu.PrefetchScalarGridSpec(
            num_scalar_prefetch=1, grid=(4, r // tr),
            in_specs=[pl.BlockSpec((None, tr, w), lambda q, i, c_ref: (2 * q + c_ref[0], i, 0)),
                      pl.BlockSpec((None, tr, w), lambda q, i, c_ref: (q, i, 0))],
            out_specs=pl.BlockSpec((None, tr, w), lambda q, i, c_ref: (q, i, 0))),
        out_shape=SDS((4, r, w), bf16), name=name, compiler_params=_cp(("arbitrary", "arbitrary")),
    )(c_idx, g, r1)


def _adamw(w, g, m, v):
    m = ADAM_B1 * m + (1.0 - ADAM_B1) * g
    v = ADAM_B2 * v + (1.0 - ADAM_B2) * (g * g)
    m_hat = m / (1.0 - ADAM_B1 ** ADAM_STEP)
    v_hat = v / (1.0 - ADAM_B2 ** ADAM_STEP)
    delta = -ADAM_LR * (m_hat / (jnp.sqrt(v_hat) + ADAM_EPS) + ADAM_WD * w)
    return delta, m, v


def _adamw_big(pieces, q_idx, w, m, v, name, row_off=0):
    r, wd = w.shape
    tr = _tile_rows(r)
    np_ = len(pieces)
    wp = wd // np_

    def body(q_ref, *refs):
        w_ref, m_ref, v_ref, g_out, d_out, m_out, v_out = refs[2 * np_:]
        for k in range(np_):
            @pl.when(pl.program_id(1) == k)
            def _():
                p_ref, r_ref = refs[2 * k], refs[2 * k + 1]
                g = p_ref[...].astype(f32)
                for j in range(3):
                    g = g + r_ref[j].astype(f32)
                d, mn, vn = _adamw(w_ref[...], g, m_ref[...], v_ref[...])
                g_out[...] = g
                d_out[...] = d
                m_out[...] = mn
                v_out[...] = vn

    tile = pl.BlockSpec((tr, wp), lambda i, k, q_ref: (i, k))
    in_specs, args = [], []
    for part, r2 in pieces:
        in_specs += [pl.BlockSpec((None, tr, wp), lambda i, k, q_ref: (q_ref[0], row_off + i, 0)),
                     pl.BlockSpec((3, tr, wp), lambda i, k, q_ref: (0, row_off + i, 0))]
        args += [part, r2]
    return pl.pallas_call(
        body,
        grid_spec=pltpu.PrefetchScalarGridSpec(
            num_scalar_prefetch=1, grid=(r // tr, np_), in_specs=in_specs + [tile, tile, tile],
            out_specs=[tile] * 4),
        out_shape=[SDS((r, wd), f32)] * 4, name=name, compiler_params=_cp(("arbitrary", "arbitrary"), 48),
    )(q_idx, *args, w, m, v)


_SMALL_ROWS = 24


def _pack_small(st_emb, vec_rnn, st_out, dsr, db_in, dmeta):
    def body(se_ref, vr_ref, so_ref, dsr_ref, db_ref, dm_ref, sm_ref, sm2_ref):
        sm_ref[...] = jnp.zeros_like(sm_ref)
        sm2_ref[...] = jnp.zeros_like(sm2_ref)
        sm_ref[0:2, :] = se_ref[0:2, :]
        sm_ref[2:3, :] = vr_ref[3:4, :]
        sm_ref[3:6, :] = vr_ref[0:3, :]
        sm_ref[6:7, :] = so_ref[2:3, :]
        sm_ref[7:9, :] = so_ref[0:2, :]
        sm_ref[10:11, :] = so_ref[3:4, :]
        for h in range(N_KV):
            sm_ref[9:10, h * GROUP:(h + 1) * GROUP] = _colsum(dsr_ref[h])
        for j in range(6):
            sm_ref[16 + j:17 + j, :] = db_ref[0:1, j * D:(j + 1) * D]
        sm_ref[22:23, 0:D_IN - 6 * D] = db_ref[0:1, 6 * D:D_IN]
        for s in range(N_DEV):
            sm2_ref[s, 0:N_META, :] = dm_ref[:, s * 256:(s + 1) * 256]
            sm2_ref[s, N_META:N_META + CONV_WIDTH, :] = vr_ref[4:8, s * 256:(s + 1) * 256]

    return pl.pallas_call(
        body, out_shape=[SDS((_SMALL_ROWS, D), f32), SDS((N_DEV, 24, 256), f32)],
        name="pack_small", compiler_params=_cp(None),
    )(st_emb, vec_rnn, st_out, dsr, db_in, dmeta)


def _small_allreduce(sm, sm2):
    def body(sm_ref, sm2_ref, o_ref, o2_ref, buf, buf2, send_sems, recv_sems):
        x, y, c = _place()
        me = _dev(x, y, c)
        copies = []
        for f in range(1, N_DEV):
            fx, fy, fc = f // 4, (f // 2) % 2, f % 2
            peer = ((x + fx) % 2, (y + fy) % 2, (c + fc) % 2)
            for t, (src, dst) in enumerate(((sm_ref, buf), (sm2_ref, buf2))):
                k = 2 * (f - 1) + t
                copies.append(pltpu.make_async_remote_copy(
                    src_ref=src, dst_ref=dst.at[me], send_sem=send_sems.at[k], recv_sem=recv_sems.at[k],
                    device_id=peer, device_id_type=MESH))
        for cp in copies:
            cp.start()
        buf[me] = sm_ref[...]
        buf2[me] = sm2_ref[...]
        for cp in copies:
            cp.wait()
        acc, acc2 = buf[0], buf2[0]
        for e in range(1, N_DEV):
            acc, acc2 = acc + buf[e], acc2 + buf2[e]
        o_ref[...] = acc
        o2_ref[...] = acc2

    return pl.pallas_call(
        body, in_specs=[_VMEM, _VMEM], out_specs=[_VMEM, _VMEM],
        out_shape=[SDS(sm.shape, f32), SDS(sm2.shape, f32)],
        scratch_shapes=[pltpu.VMEM((N_DEV, *sm.shape), f32), pltpu.VMEM((N_DEV, *sm2.shape), f32),
                        pltpu.SemaphoreType.DMA((14,)), pltpu.SemaphoreType.DMA((14,))],
        name="small_allreduce",
    )(sm, sm2)


_SMALL_ROW_OF = {"ln_emb_g": 0, "ln_emb_b": 1, "conv_b": 2, "b_ra": 3, "b_ri": 4, "lru_lambda": 5, "b_o": 6,
                 "ln_g": 7, "ln_b": 8}
_SMALL_NAMES = ["ln_emb_g", "ln_emb_b", "conv_b", "b_ra", "b_ri", "lru_lambda", "b_o", "ln_g", "ln_b",
                "sinks", "b_in", "meta_tokens", "conv_w"]


def _small_update(sm, sm2_mine, wmv):
    def grad_of(name, sm_ref, s2_ref):
        if name in _SMALL_ROW_OF:
            r = _SMALL_ROW_OF[name]
            return sm_ref[r:r + 1, :]
        if name == "sinks":
            return sm_ref[9:10, 0:N_KV * GROUP]
        if name == "b_in":
            return jnp.concatenate([sm_ref[16 + j:17 + j, :] for j in range(7)], axis=1)[:, :D_IN]
        if name == "meta_tokens":
            return s2_ref[0:N_META, :]
        return s2_ref[N_META:N_META + CONV_WIDTH, :]

    def body(*refs):
        sm_ref, s2_ref = refs[0], refs[1]
        ins = refs[2:2 + 3 * len(_SMALL_NAMES)]
        outs = refs[2 + 3 * len(_SMALL_NAMES):]
        for i, name in enumerate(_SMALL_NAMES):
            w_ref, m_ref, v_ref = ins[3 * i:3 * i + 3]
            g = grad_of(name, sm_ref, s2_ref)
            d, mn, vn = _adamw(w_ref[...], g, m_ref[...], v_ref[...])
            outs[4 * i][...] = g
            outs[4 * i + 1][...] = d
            outs[4 * i + 2][...] = mn
            outs[4 * i + 3][...] = vn
        outs[-1][...] = jnp.broadcast_to(jnp.sum(sm_ref[10:11, :], axis=1, keepdims=True), (8, 128))

    args, out_shape = [sm, sm2_mine], []
    for name in _SMALL_NAMES:
        args += list(wmv[name])
        out_shape += [SDS(wmv[name][0].shape, f32)] * 4
    out_shape.append(SDS((8, 128), f32))
    res = pl.pallas_call(body, out_shape=out_shape, name="small_update", compiler_params=_cp(None))(*args)
    return {name: tuple(res[4 * i:4 * i + 4]) for i, name in enumerate(_SMALL_NAMES)}, res[-1][0, 0]


_WEIGHTS = ["meta_tokens", "ln_emb_g", "ln_emb_b", "w_in", "b_in", "conv_w", "conv_b", "w_ra", "b_ra", "w_ri",
            "b_ri", "lru_lambda", "sinks", "w_rnn_out", "w_attn_out", "w_o", "b_o", "ln_g", "ln_b"]
_SMALL_2D = {"meta_tokens": (N_META, 256), "conv_w": (CONV_WIDTH, 256), "b_in": (1, D_IN), "sinks": (1, N_KV * GROUP)}


def kernel(x, meta_tokens, ln_emb_g, ln_emb_b, w_in, b_in, conv_w, conv_b, w_ra, b_ra, w_ri, b_ri, lru_lambda, sinks, w_rnn_out, w_attn_out, w_o, b_o, ln_g, ln_b, loss_target, m_meta_tokens, m_ln_emb_g, m_ln_emb_b, m_w_in, m_b_in, m_conv_w, m_conv_b, m_w_ra, m_b_ra, m_w_ri, m_b_ri, m_lru_lambda, m_sinks, m_w_rnn_out, m_w_attn_out, m_w_o, m_b_o, m_ln_g, m_ln_b, v_meta_tokens, v_ln_emb_g, v_ln_emb_b, v_w_in, v_b_in, v_conv_w, v_conv_b, v_w_ra, v_b_ra, v_w_ri, v_b_ri, v_lru_lambda, v_sinks, v_w_rnn_out, v_w_attn_out, v_w_o, v_b_o, v_ln_g, v_ln_b):
    w = dict(meta_tokens=meta_tokens, ln_emb_g=ln_emb_g, ln_emb_b=ln_emb_b, w_in=w_in, b_in=b_in, conv_w=conv_w,
             conv_b=conv_b, w_ra=w_ra, b_ra=b_ra, w_ri=w_ri, b_ri=b_ri, lru_lambda=lru_lambda, sinks=sinks,
             w_rnn_out=w_rnn_out, w_attn_out=w_attn_out, w_o=w_o, b_o=b_o, ln_g=ln_g, ln_b=ln_b)
    m = dict(meta_tokens=m_meta_tokens, ln_emb_g=m_ln_emb_g, ln_emb_b=m_ln_emb_b, w_in=m_w_in, b_in=m_b_in,
             conv_w=m_conv_w, conv_b=m_conv_b, w_ra=m_w_ra, b_ra=m_b_ra, w_ri=m_w_ri, b_ri=m_b_ri,
             lru_lambda=m_lru_lambda, sinks=m_sinks, w_rnn_out=m_w_rnn_out, w_attn_out=m_w_attn_out, w_o=m_w_o,
             b_o=m_b_o, ln_g=m_ln_g, ln_b=m_ln_b)
    v = dict(meta_tokens=v_meta_tokens, ln_emb_g=v_ln_emb_g, ln_emb_b=v_ln_emb_b, w_in=v_w_in, b_in=v_b_in,
             conv_w=v_conv_w, conv_b=v_conv_b, w_ra=v_w_ra, b_ra=v_b_ra, w_ri=v_w_ri, b_ri=v_b_ri,
             lru_lambda=v_lru_lambda, sinks=v_sinks, w_rnn_out=v_w_rnn_out, w_attn_out=v_w_attn_out, w_o=v_w_o,
             b_o=v_b_o, ln_g=v_ln_g, ln_b=v_ln_b)
    px, py, pc = _place()
    as_idx = lambda t: jnp.reshape(t, (1,)).astype(jnp.int32)
    c_idx, q_idx, me_idx = as_idx(pc), as_idx(2 * px + py), as_idx(_dev(px, py, pc))

    w3_s, wrg_s, small_s = _cast_small(w_rnn_out, w_attn_out, w_o, w_ra, w_ri, meta_tokens, conv_w)
    vec = lambda name: w[name].reshape(1, -1)
    p = {k: vec(k) for k in ("ln_emb_g", "ln_emb_b", "b_in", "conv_b", "b_ra", "b_ri", "lru_lambda", "sinks",
                             "b_o", "ln_g", "ln_b")}
    w_in_t = lambda a: jnp.swapaxes(a, 1, 2).reshape(SHARD_IN, D)
    wg, wrg, smallw, w3_land = _all_gather([_cast_w_in(w_in_t(w_in)), wrg_s, small_s], [w3_s])
    w3_pending = _split_start(_copies_direct(True), 7, [w3_s], [w3_land], smallw, "gather_w3_start")
    w_full = wg.reshape(D_IN, D)

    zero = w3_pending[4][0:1, 0:1]
    h32, hb = _ln_emb(x, smallw, p["ln_emb_g"], p["ln_emb_b"])
    z = _mm(hb, w_full, nt=True, bias=p["b_in"] + zero, name="mm_z")
    s = _step_attn(_step_rnn(h32, hb, z, wrg, smallw, p, zero), p, zero)
    w3 = _split_wait(_copies_direct(True), *w3_pending[:4], s["lse"], "gather_w3_wait")[1][0]
    t = _step_merge(s, loss_target, w3, p)

    big = {}
    two_d = lambda name: (w[name].shape[-2], w[name].shape[-1])
    proj = ("w_o", "w_rnn_out", "w_attn_out")
    g_proj = [t[k].reshape(N_DEV, 256, D) for k in ("g_wo", "g_wrnn", "g_wattn")]
    g_pending = _split_start(_copies_direct(False), 7, g_proj, [lax.empty((N_DEV, 256, D), bf16) for _ in proj],
                             p["b_o"], "reduce_proj_start")
    u = _step_backward(s, t, wrg, smallw, p, p["conv_b"] + g_pending[4][0:1, 0:1])

    def siblings_start(gs, dep, tag):
        return _split_start(_copies_siblings, 4, gs, [lax.empty((4, *g.shape[1:]), bf16) for g in gs], dep,
                            "reduce_siblings_start_" + tag)

    def chips_start(gs, r1, dep, tag):
        parts = [_pair_sum(g, r, c_idx, "pair_sum_%s%d" % (tag, i)) for i, (g, r) in enumerate(zip(gs, r1))]
        return _split_start(_copies_chips, 3, parts, [lax.empty((3, *q.shape[1:]), bf16) for q in parts], dep,
                            "reduce_chips_start_" + tag)

    g_a, db_in = _mm_dwin(s["hb"], u["dz"], 0, p["b_o"])
    shards = lambda g: g.reshape(N_DEV, SHARD_IN, W_IN_HALF)
    sib_a = siblings_start([shards(g_a), u["g_wrg"].reshape(N_DEV, 2 * RNN_BLOCK, RNN_BLOCK)], db_in, "a")
    g_b, = _mm_dwin(s["hb"], u["dz"], 1, sib_a[4])
    sib_b = siblings_start([shards(g_b)], db_in, "b")
    chp_a = chips_start(*_split_wait(_copies_siblings, *sib_a[:4], sib_b[4], "reduce_siblings_wait_a"), db_in, "a")
    g_proj, g_land = _split_wait(_copies_direct(False), *g_pending[:4], chp_a[4], "reduce_proj_wait")
    for i, name in enumerate(proj):
        res = _adamw_direct(g_proj[i], g_land[i], me_idx, w[name].reshape(two_d(name)), m[name].reshape(two_d(name)),
                            v[name].reshape(two_d(name)), "adamw_" + name)
        big[name] = tuple(r.reshape(w[name].shape) for r in res)
    chp_b = chips_start(*_split_wait(_copies_siblings, *sib_b[:4], big["w_attn_out"][3], "reduce_siblings_wait_b"),
                        db_in, "b")
    u.update(_step_input_grad(u["dz"], w_full, chp_b[4], t["du32"], x, smallw, p))
    u["db_in"] = db_in

    loc = {**t, **u}
    sm, sm2 = _pack_small(loc["st_emb"], loc["vec_rnn"], loc["st_out"], loc["dsr"], loc["db_in"], loc["dmeta"])
    sm, sm2 = _small_allreduce(sm, sm2)
    sm2_mine = lax.dynamic_index_in_dim(sm2, _dev(px, py, pc), 0, keepdims=False)
    two = lambda name, t: t.reshape(_SMALL_2D.get(name, (1, D)))
    small, loss = _small_update(sm, sm2_mine, {k: (two(k, w[k]), two(k, m[k]), two(k, v[k])) for k in _SMALL_NAMES})

    parts_a, r2_a = _split_wait(_copies_chips, *chp_a[:4], small["b_in"][3], "reduce_chips_wait_a")
    parts_b, r2_b = _split_wait(_copies_chips, *chp_b[:4], small["b_in"][2], "reduce_chips_wait_b")
    res = _adamw_big([(parts_a[0], r2_a[0]), (parts_b[0], r2_b[0])], q_idx, w_in_t(w["w_in"]), w_in_t(m["w_in"]),
                     w_in_t(v["w_in"]), "adamw_w_in")
    big["w_in"] = tuple(jnp.swapaxes(r.reshape(1, SHARD_IN, D), 1, 2) for r in res)
    for i, name in enumerate(("w_ra", "w_ri")):
        sq = (RNN_BLOCK, RNN_BLOCK)
        res = _adamw_big([(parts_a[1], r2_a[1])], q_idx, w[name].reshape(sq), m[name].reshape(sq), v[name].reshape(sq),
                         "adamw_" + name, row_off=i)
        big[name] = tuple(r.reshape(w[name].shape) for r in res)
    res = dict(big)
    for k in _SMALL_NAMES:
        res[k] = tuple(t.reshape(w[k].shape) for t in small[k])

    outs = [loss, loc["grad_x"]]
    for j in range(4):
        outs += [res[k][j] for k in _WEIGHTS]
    return tuple(outs)
```

```python
import functools

import jax
import jax.numpy as jnp
from jax import lax
from jax.experimental import pallas as pl
from jax.experimental.pallas import tpu as pltpu

f32, bf16 = jnp.float32, jnp.bfloat16
SDS = jax.ShapeDtypeStruct

N_DEV = 8
D = 2048
N_META = 16
BLK = 128
ROW0 = BLK - N_META
N_RNN_BLOCKS = 8
RNN_BLOCK = D // N_RNN_BLOCKS
CONV_WIDTH = 4
LRU_C = 8.0
HEAD_DIM = 64
N_KV = 4
GROUP = 8
HALF = HEAD_DIM // 2
ROPE_THETA = 10000.0
NEG_INF = -1e30
LN_EPS = 1e-5
ALPHA = 2.0 ** 0.25
D_IN = 12800
SHARD_IN = D_IN // N_DEV
OFF_GR, OFF_Q, OFF_K, OFF_V, OFF_GA, OFF_G = 2048, 4096, 6144, 6400, 6656, 8704
ADAM_LR, ADAM_B1, ADAM_B2, ADAM_EPS, ADAM_WD, ADAM_STEP = 1e-3, 0.9, 0.999, 1e-8, 0.01, 10
VMEM_LIMIT_MB = 56
MESH = pl.DeviceIdType.MESH


def _cp(sem=None, vmem_mb=40):
    return pltpu.CompilerParams(dimension_semantics=sem, vmem_limit_bytes=vmem_mb * 2 ** 20)


def _row_chunk(m):
    best = 16
    for c in range(16, 641, 16):
        if m % c == 0:
            best = c
    return best


def _sigmoid(x):
    return 1.0 / (1.0 + jnp.exp(-x))


def _silu_and_grad(x):
    s = _sigmoid(x)
    return x * s, s * (1.0 + x * (1.0 - s))


def _log_sigmoid(x):
    return jnp.minimum(x, 0.0) - jnp.log1p(jnp.exp(-jnp.abs(x)))


def _ln_rows(v, g, b):
    mu = jnp.mean(v, axis=-1, keepdims=True)
    c = v - mu
    var = jnp.mean(c * c, axis=-1, keepdims=True)
    rstd = lax.rsqrt(var + LN_EPS)
    xhat = c * rstd
    return xhat * g + b, xhat, rstd


def _ln_rows_bwd(dy, g, xhat, rstd):
    dxh = dy * g
    m1 = jnp.mean(dxh, axis=-1, keepdims=True)
    m2 = jnp.mean(dxh * xhat, axis=-1, keepdims=True)
    return rstd * (dxh - m1 - xhat * m2)


def _colsum(v):
    return jnp.sum(v, axis=0, keepdims=True)


def _dot(a, b):
    return jnp.dot(a, b, preferred_element_type=f32)


def _dot_nt(a, b):
    return lax.dot_general(a, b, (((1,), (1,)), ((), ())), preferred_element_type=f32)


def _dot_tn(a, b):
    return lax.dot_general(a, b, (((0,), (0,)), ((), ())), preferred_element_type=f32)


def _meta_full(sw_ref):
    return jnp.concatenate([sw_ref[s, 0:N_META, :] for s in range(N_DEV)], axis=1)


def _ln_emb(x, smallw, g_e, b_e):
    seq = x.shape[1]
    rows = seq + BLK
    nb = rows // BLK

    def body(x_ref, sw_ref, g_ref, b_ref, h32_ref, hb_ref):
        i = pl.program_id(0)
        g, b = g_ref[...], b_ref[...]

        def emit(blk):
            h32_ref[...] = blk
            hb_ref[...] = blk.astype(bf16)

        @pl.when(i == 0)
        def _():
            hm = _ln_rows(_meta_full(sw_ref), g, b)[0]
            emit(jnp.concatenate([jnp.zeros((ROW0, D), f32), hm], axis=0))

        @pl.when(i > 0)
        def _():
            emit(_ln_rows(x_ref[0], g, b)[0])

    return pl.pallas_call(
        body, grid=(nb,),
        in_specs=[pl.BlockSpec((1, BLK, D), lambda i: (0, jnp.maximum(i - 1, 0), 0)),
                  pl.BlockSpec((N_DEV, 24, 256), lambda i: (0, 0, 0)),
                  pl.BlockSpec((1, D), lambda i: (0, 0)),
                  pl.BlockSpec((1, D), lambda i: (0, 0))],
        out_specs=[pl.BlockSpec((BLK, D), lambda i: (i, 0)),
                   pl.BlockSpec((BLK, D), lambda i: (i, 0))],
        out_shape=[SDS((rows, D), f32), SDS((rows, D), bf16)],
        name="ln_emb", compiler_params=_cp(("arbitrary",)),
    )(x, smallw, g_e, b_e)


def _ln_emb_bwd(dh, du32, x, smallw, g_e):
    seq = x.shape[1]
    rows = seq + BLK
    nb = rows // BLK

    def body(dh_ref, du_ref, x_ref, sw_ref, g_ref, gx_ref, dmeta_ref, st_ref):
        i = pl.program_id(0)
        g = g_ref[...]
        dht = dh_ref[...] + ALPHA * du_ref[...]

        @pl.when(i == 0)
        def _():
            v = jnp.concatenate([jnp.zeros((ROW0, D), f32), _meta_full(sw_ref)], axis=0)
            valid = lax.broadcasted_iota(jnp.int32, (BLK, 1), 0) >= ROW0
            d = jnp.where(valid, dht, 0.0)
            _, xhat, rstd = _ln_rows(v, g, 0.0)
            dv = _ln_rows_bwd(d, g, xhat, rstd)
            dmeta_ref[...] = dv[ROW0:, :]
            st_ref[...] = jnp.concatenate([_colsum(d * xhat), _colsum(d), jnp.zeros((6, D), f32)], axis=0)

        @pl.when(i > 0)
        def _():
            _, xhat, rstd = _ln_rows(x_ref[0], g, 0.0)
            gx_ref[0] = _ln_rows_bwd(dht, g, xhat, rstd)
            st_ref[0:1, :] += _colsum(dht * xhat)
            st_ref[1:2, :] += _colsum(dht)

    return pl.pallas_call(
        body, grid=(nb,),
        in_specs=[pl.BlockSpec((BLK, D), lambda i: (i, 0)),
                  pl.BlockSpec((BLK, D), lambda i: (i, 0)),
                  pl.BlockSpec((1, BLK, D), lambda i: (0, jnp.maximum(i - 1, 0), 0)),
                  pl.BlockSpec((N_DEV, 24, 256), lambda i: (0, 0, 0)),
                  pl.BlockSpec((1, D), lambda i: (0, 0))],
        out_specs=[pl.BlockSpec((1, BLK, D), lambda i: (0, jnp.maximum(i - 1, 0), 0)),
                   pl.BlockSpec((N_META, D), lambda i: (0, 0)),
                   pl.BlockSpec((8, D), lambda i: (0, 0))],
        out_shape=[SDS((1, seq, D), f32), SDS((N_META, D), f32), SDS((8, D), f32)],
        name="ln_emb_bwd", compiler_params=_cp(("arbitrary",)),
    )(dh, du32, x, smallw, g_e)


def _mm(a, b, *, name, nt=False, sel=None, bias=None, out_dtype=f32, tn=512):
    m, k = a.shape
    cm = _row_chunk(m)
    stacked = sel is not None
    n = D if stacked else (b.shape[0] if nt else b.shape[1])
    am = m
    if stacked and nt:
        b_spec = pl.BlockSpec((tn // 256, None, 256, D), lambda j, i: (j, sel, 0, 0))
    elif stacked:
        b_spec = pl.BlockSpec((N_DEV, None, 256, tn), lambda j, i: (0, sel, 0, j))
    elif nt:
        b_spec = pl.BlockSpec((tn, k), lambda j, i: (j, 0))
    else:
        b_spec = pl.BlockSpec((k, tn), lambda j, i: (0, j))
    in_specs = [pl.BlockSpec((am, k), lambda j, i: (i, 0)), b_spec]
    args = [a, b]
    if bias is not None:
        in_specs.append(pl.BlockSpec((1, tn), lambda j, i: (0, j)))
        args.append(bias)

    def body(*refs):
        a_ref, b_ref, o_ref = refs[0], refs[1], refs[-1]
        bm = b_ref[...]
        if stacked:
            bm = bm.reshape((tn, D) if nt else (D, tn))
        for c in range(am // cm):
            acc = (_dot_nt if nt else _dot)(a_ref[c * cm:(c + 1) * cm, :], bm)
            if bias is not None:
                acc = acc + refs[2][...]
            o_ref[c * cm:(c + 1) * cm, :] = acc.astype(out_dtype)

    return pl.pallas_call(
        body, grid=(n // tn, m // am), in_specs=in_specs,
        out_specs=pl.BlockSpec((am, tn), lambda j, i: (i, j)),
        out_shape=SDS((m, n), out_dtype), name=name, compiler_params=_cp(("arbitrary", "arbitrary"), 48),
    )(*args)


def _mm_dh(dz, w_t, after):
    rows = dz.shape[0]
    tk, tn = 2560, 512
    cm = _row_chunk(rows)

    def body(a_ref, w_ref, after_ref, o_ref):
        kk = pl.program_id(1)
        for c in range(rows // cm):
            acc = _dot(a_ref[c * cm:(c + 1) * cm, :], w_ref[...])

            @pl.when(kk == 0)
            def _():
                o_ref[c * cm:(c + 1) * cm, :] = acc

            @pl.when(kk > 0)
            def _():
                o_ref[c * cm:(c + 1) * cm, :] += acc

    return pl.pallas_call(
        body, grid=(D // tn, D_IN // tk),
        in_specs=[pl.BlockSpec((rows, tk), lambda j, kk: (0, kk)),
                  pl.BlockSpec((tk, tn), lambda j, kk: (kk, j)),
                  pl.BlockSpec(memory_space=pl.ANY)],
        out_specs=pl.BlockSpec((rows, tn), lambda j, kk: (0, j)),
        out_shape=SDS((rows, D), f32), name="mm_dh", compiler_params=_cp(("arbitrary", "arbitrary"), 48),
    )(dz, w_t, after)


W_IN_HALF = D // 2


def _mm_dwin_parts(hb, parts):
    rows = hb.shape[0]
    tc = 512
    edges = [0]
    for _, w in parts:
        edges.append(edges[-1] + w // tc)

    def body(*refs):
        h_ref, (o_ref, dz_ref, db_ref) = refs[len(parts)], refs[len(parts) + 1:]
        j = pl.program_id(0)
        for p_ref, lo, hi in zip(refs, edges[:-1], edges[1:]):
            @pl.when((j >= lo) & (j < hi))
            def _():
                o_ref[...] = _dot_tn(p_ref[...], h_ref[...]).astype(bf16)
                dz_ref[...] = p_ref[...]

                def step(i, s):
                    blk = p_ref[pl.ds(pl.multiple_of(i * BLK, BLK), BLK), :].astype(f32)
                    return s + blk.reshape(BLK // 8, 8, tc).sum(axis=0)
                s = lax.fori_loop(0, rows // BLK, step, jnp.zeros((8, tc), f32))
                db_ref[...] = jnp.broadcast_to(_colsum(s), (8, tc))

    in_specs = [pl.BlockSpec((rows, tc), lambda j, lo=lo, hi=hi: (0, jnp.clip(j - lo, 0, hi - lo - 1)))
                for lo, hi in zip(edges[:-1], edges[1:])]
    return pl.pallas_call(
        body, grid=(D_IN // tc,),
        in_specs=in_specs + [pl.BlockSpec((rows, W_IN_HALF), lambda j: (0, 0))],
        out_specs=[pl.BlockSpec((tc, W_IN_HALF), lambda j: (j, 0)), pl.BlockSpec((rows, tc), lambda j: (0, j)),
                   pl.BlockSpec((8, tc), lambda j: (0, j))],
        out_shape=[SDS((D_IN, W_IN_HALF), bf16), SDS((rows, D_IN), bf16), SDS((8, D_IN), f32)],
        name="mm_dwin_0", compiler_params=_cp(("arbitrary",), VMEM_LIMIT_MB),
    )(*[a for a, _ in parts], hb)


def _mm_dwin(hb, dz, after):
    rows = dz.shape[0]
    tc = 640

    def body(dz_ref, h_ref, after_ref, o_ref):
        o_ref[...] = _dot_tn(dz_ref[...], h_ref[...]).astype(bf16)

    return pl.pallas_call(
        body, grid=(D_IN // tc,),
        in_specs=[pl.BlockSpec((rows, tc), lambda j: (0, j)),
                  pl.BlockSpec((rows, W_IN_HALF), lambda j: (0, 1)),
                  pl.BlockSpec(memory_space=pl.ANY)],
        out_specs=pl.BlockSpec((tc, W_IN_HALF), lambda j: (j, 0)),
        out_shape=SDS((D_IN, W_IN_HALF), bf16),
        name="mm_dwin_1", compiler_params=_cp(("arbitrary",), 48),
    )(dz, hb, after)


SCAN_ROWS = 32


def _scan8(a, b, reverse):
    idx = lax.broadcasted_iota(jnp.int32, a.shape, 0)
    for s in (1, 2, 4):
        sh = 8 - s if reverse else s
        a_sh, b_sh = pltpu.roll(a, sh, 0), pltpu.roll(b, sh, 0)
        m = (idx < 8 - s) if reverse else (idx >= s)
        b = jnp.where(m, a * b_sh + b, b)
        a = jnp.where(m, a * a_sh, a)
    return a, b


def _shift_rows(prev8, cur, k):
    ext = jnp.concatenate([prev8, cur], axis=0)
    return pltpu.roll(ext, k, 0)[8:, :]


def _gates(xc, w_ra, b_ra, w_ri, b_ri, ls):
    xb = xc.astype(bf16)
    r = _sigmoid(_dot(xb, w_ra) + b_ra)
    ig = _sigmoid(_dot(xb, w_ri) + b_ri)
    la = LRU_C * r * ls
    a = jnp.exp(la)
    mult = jnp.sqrt(jnp.tanh(-la) * (1.0 + a * a))
    return xb, r, ig, a, mult


_RNN_IN_SPECS = lambda rows: [
    pl.BlockSpec((1, 24, 256), lambda n: (n, 0, 0)),
    pl.BlockSpec((1, RNN_BLOCK), lambda n: (0, n)),
    pl.BlockSpec((N_DEV, 2, None, 32, RNN_BLOCK), lambda n: (0, 0, n, 0, 0)),
    pl.BlockSpec((1, RNN_BLOCK), lambda n: (0, n)),
    pl.BlockSpec((1, RNN_BLOCK), lambda n: (0, n)),
    pl.BlockSpec((1, RNN_BLOCK), lambda n: (0, n)),
]


def _rnn_fwd(z, smallw, conv_b, wrg, b_ra, b_ri, lam):
    rows = z.shape[0]
    nb = rows // BLK
    col = lambda off: pl.BlockSpec((rows, RNN_BLOCK), lambda n: (0, off // RNN_BLOCK + n))

    def body(xr_ref, gr_ref, sw_ref, cb_ref, w_ref, bra_ref, bri_ref, lam_ref, xc_ref, hr_ref, ya_ref, yat_ref, a_s):
        cw = sw_ref[0, N_META:24, :]
        cb = cb_ref[...]
        w_ra = w_ref[:, 0].reshape(RNN_BLOCK, RNN_BLOCK)
        w_ri = w_ref[:, 1].reshape(RNN_BLOCK, RNN_BLOCK)
        b_ra_v, b_ri_v = bra_ref[...], bri_ref[...]
        ls = _log_sigmoid(lam_ref[...])
        rid = lax.broadcasted_iota(jnp.int32, (BLK, 1), 0)

        def blk_step(i, carry):
            r0 = pl.multiple_of(i * BLK, BLK)
            grow = rid + r0
            valid = grow >= ROW0
            cur = jnp.where(valid, xr_ref[pl.ds(r0, BLK), :], 0.0)
            prev8 = xr_ref[pl.ds(pl.multiple_of(jnp.maximum(r0 - 8, 0), 8), 8), :] * (i > 0).astype(f32)
            xc = cb + cw[0:1] * cur
            for k in range(1, CONV_WIDTH):
                xc = xc + cw[k:k + 1] * _shift_rows(prev8, cur, k)
            xc_ref[pl.ds(r0, BLK), :] = xc
            _, _, ig, a, mult = _gates(xc, w_ra, b_ra_v, w_ri, b_ri_v, ls)
            mult = jnp.where(grow == ROW0, 1.0, mult)
            a_s[pl.ds(r0, BLK), :] = a
            hr_ref[pl.ds(r0, BLK), :] = jnp.where(valid, mult * ig * xc, 0.0)
            return carry

        lax.fori_loop(0, nb, blk_step, 0)

        def scan_step(j, carry):
            r0 = pl.multiple_of(j * SCAN_ROWS, SCAN_ROWS)
            tiles = [_scan8(a_s[pl.ds(r0 + 8 * k, 8), :], hr_ref[pl.ds(r0 + 8 * k, 8), :], False)
                     for k in range(SCAN_ROWS // 8)]
            for k, (a, b) in enumerate(tiles):
                h = b + a * carry
                hr_ref[pl.ds(r0 + 8 * k, 8), :] = h
                carry = jnp.broadcast_to(h[7:8, :], (8, RNN_BLOCK))
            return carry

        lax.fori_loop(0, rows // SCAN_ROWS, scan_step, jnp.zeros((8, RNN_BLOCK), f32))

        def gate_step(i, carry):
            r0 = pl.multiple_of(i * BLK, BLK)
            ya_ref[pl.ds(r0, BLK), :] = (hr_ref[pl.ds(r0, BLK), :]
                                         * _silu_and_grad(gr_ref[pl.ds(r0, BLK), :])[0]).astype(bf16)
            return carry

        lax.fori_loop(0, nb, gate_step, 0)
        yat_ref[...] = ya_ref[...].astype(f32).T.astype(bf16)

    return pl.pallas_call(
        body, grid=(N_RNN_BLOCKS,),
        in_specs=[col(0), col(OFF_GR)] + _RNN_IN_SPECS(rows),
        out_specs=[pl.BlockSpec((rows, RNN_BLOCK), lambda n: (0, n))] * 3
                  + [pl.BlockSpec((RNN_BLOCK, rows), lambda n: (n, 0))],
        out_shape=[SDS((rows, D), f32), SDS((rows, D), f32), SDS((rows, D), bf16), SDS((D, rows), bf16)],
        scratch_shapes=[pltpu.VMEM((rows, RNN_BLOCK), f32)],
        name="rnn_fwd", compiler_params=_cp(("arbitrary",)),
    )(z, z, smallw, conv_b, wrg, b_ra, b_ri, lam)


def _rnn_bwd(dya, hr, xc, z, smallw, conv_b, wrg, b_ra, b_ri, lam):
    rows = z.shape[0]
    nb = rows // BLK
    col = lambda off: pl.BlockSpec((rows, RNN_BLOCK), lambda n: (0, off // RNN_BLOCK + n))
    blk = pl.BlockSpec((rows, RNN_BLOCK), lambda n: (0, n))

    def body(dya_ref, hr_ref, xc_ref, xr_ref, gr_ref, sw_ref, cb_ref, w_ref, bra_ref, bri_ref, lam_ref,
             dxr_ref, dgr_ref, dw_ref, vec_ref, a_s, lam_s, dxc_s, r_s, ig_s, mult_s, dw_s):
        cw = sw_ref[0, N_META:24, :]
        w_ra = w_ref[:, 0].reshape(RNN_BLOCK, RNN_BLOCK)
        w_ri = w_ref[:, 1].reshape(RNN_BLOCK, RNN_BLOCK)
        b_ra_v, b_ri_v = bra_ref[...], bri_ref[...]
        lam_v = lam_ref[...]
        ls = _log_sigmoid(lam_v)
        rid = lax.broadcasted_iota(jnp.int32, (BLK, 1), 0)
        zrow = jnp.zeros((1, RNN_BLOCK), f32)

        def p1(i, carry):
            r0 = pl.multiple_of(i * BLK, BLK)
            sl = pl.ds(r0, BLK)
            _, r, ig, a, mult = _gates(xc_ref[sl, :], w_ra, b_ra_v, w_ri, b_ri_v, ls)
            a_s[sl, :] = a
            r_s[sl, :] = r
            ig_s[sl, :] = ig
            mult_s[sl, :] = mult
            sg, dsg = _silu_and_grad(gr_ref[sl, :])
            d = dya_ref[sl, :]
            lam_s[sl, :] = d * sg
            dgr_ref[sl, :] = (d * hr_ref[sl, :] * dsg).astype(bf16)
            return carry

        lax.fori_loop(0, nb, p1, 0)

        def p2(jj, carry):
            r0 = pl.multiple_of((rows // SCAN_ROWS - 1 - jj) * SCAN_ROWS, SCAN_ROWS)
            idx = lax.broadcasted_iota(jnp.int32, (8, RNN_BLOCK), 0)
            tiles = []
            for k in range(SCAN_ROWS // 8):
                sl = pl.ds(r0 + 8 * k, 8)
                a, g = a_s[sl, :], lam_s[sl, :]
                tiles.append((g, *_scan8(a, a * g, True)))
            for k in reversed(range(SCAN_ROWS // 8)):
                g, ca, cb_ = tiles[k]
                mu = cb_ + ca * carry
                lam_s[pl.ds(r0 + 8 * k, 8), :] = g + jnp.where(idx < 7, pltpu.roll(mu, 7, 0), carry)
                carry = jnp.broadcast_to(mu[0:1, :], (8, RNN_BLOCK))
            return carry

        lax.fori_loop(0, rows // SCAN_ROWS, p2, jnp.zeros((8, RNN_BLOCK), f32))

        dw_s[...] = jnp.zeros_like(dw_s)

        def p3(i, carry):
            d_bra, d_bri, d_ls = carry
            r0 = pl.multiple_of(i * BLK, BLK)
            sl = pl.ds(r0, BLK)
            grow = rid + r0
            valid = grow >= ROW0
            first = grow == ROW0
            xcv = xc_ref[sl, :]
            xb = xcv.astype(bf16)
            r, ig, a = r_s[sl, :], ig_s[sl, :], a_s[sl, :]
            mult = jnp.where(first, 1.0, mult_s[sl, :])
            lam_t = lam_s[sl, :]
            du = jnp.where(valid, lam_t, 0.0)
            hprev = _shift_rows(hr_ref[pl.ds(pl.multiple_of(jnp.maximum(r0 - 8, 0), 8), 8), :] * (i > 0).astype(f32), hr_ref[sl, :], 1)
            da = lam_t * hprev
            dmult = jnp.where(first, 0.0, du * ig * xcv)
            di = du * mult * xcv
            dxc = du * mult * ig
            ratio = jnp.where(valid & jnp.logical_not(first), a * a / mult, 0.0)
            dla = da * a - dmult * ratio
            dpr = (dla * (LRU_C * ls)) * r * (1.0 - r)
            dpi = di * ig * (1.0 - ig)
            dprb, dpib = dpr.astype(bf16), dpi.astype(bf16)
            dw_s[0] += _dot_tn(xb, dprb)
            dw_s[1] += _dot_tn(xb, dpib)
            dxc_s[sl, :] = dxc + _dot_nt(dprb, w_ra) + _dot_nt(dpib, w_ri)
            return d_bra + _colsum(dpr), d_bri + _colsum(dpi), d_ls + _colsum(dla * (LRU_C * r))

        d_bra, d_bri, d_ls = lax.fori_loop(0, nb, p3, (zrow, zrow, zrow))

        def p4(i, carry):
            d_cb, d_w0, d_w1, d_w2, d_w3 = carry
            r0 = pl.multiple_of(i * BLK, BLK)
            sl = pl.ds(r0, BLK)
            grow = rid + r0
            valid = grow >= ROW0
            dxc = dxc_s[sl, :]
            nxt = dxc_s[pl.ds(pl.multiple_of(jnp.minimum(r0 + BLK, rows - 8), 8), 8), :] * (i < nb - 1).astype(f32)
            ext = jnp.concatenate([dxc, nxt], axis=0)
            dxr = cw[0:1] * dxc
            for k in range(1, CONV_WIDTH):
                dxr = dxr + cw[k:k + 1] * pltpu.roll(ext, BLK + 8 - k, 0)[:BLK, :]
            dxr_ref[sl, :] = jnp.where(valid, dxr, 0.0).astype(bf16)
            cur = jnp.where(valid, xr_ref[sl, :], 0.0)
            prev8 = xr_ref[pl.ds(pl.multiple_of(jnp.maximum(r0 - 8, 0), 8), 8), :] * (i > 0).astype(f32)
            dws = [d_w0 + _colsum(dxc * cur)]
            for k, acc in ((1, d_w1), (2, d_w2), (3, d_w3)):
                dws.append(acc + _colsum(dxc * _shift_rows(prev8, cur, k)))
            return (d_cb + _colsum(dxc), *dws)

        d_cb, d_w0, d_w1, d_w2, d_w3 = lax.fori_loop(0, nb, p4, (zrow,) * 5)

        d_lam = d_ls * _sigmoid(-lam_v)
        vec_ref[...] = jnp.concatenate([d_bra, d_bri, d_lam, d_cb, d_w0, d_w1, d_w2, d_w3], axis=0)
        dw_ref[:, 0] = dw_s[0].astype(bf16).reshape(N_DEV, 32, RNN_BLOCK)
        dw_ref[:, 1] = dw_s[1].astype(bf16).reshape(N_DEV, 32, RNN_BLOCK)

    return pl.pallas_call(
        body, grid=(N_RNN_BLOCKS,),
        in_specs=[blk, blk, blk, col(0), col(OFF_GR)] + _RNN_IN_SPECS(rows),
        out_specs=[blk, blk,
                   pl.BlockSpec((N_DEV, 2, None, 32, RNN_BLOCK), lambda n: (0, 0, n, 0, 0)),
                   pl.BlockSpec((8, RNN_BLOCK), lambda n: (0, n))],
        out_shape=[SDS((rows, D), bf16), SDS((rows, D), bf16),
                   SDS((N_DEV, 2, N_RNN_BLOCKS, 32, RNN_BLOCK), bf16), SDS((8, D), f32)],
        scratch_shapes=[pltpu.VMEM((rows, RNN_BLOCK), f32)] * 6 + [pltpu.VMEM((2, RNN_BLOCK, RNN_BLOCK), f32)],
        name="rnn_bwd", compiler_params=_cp(("arbitrary",), 48),
    )(dya, hr, xc, z, z, smallw, conv_b, wrg, b_ra, b_ri, lam)


def _rope_tables(rows):
    half = jnp.arange(HALF, dtype=f32)
    inv = ROPE_THETA ** (-half / HALF)
    pos = (jnp.arange(rows) - ROW0).astype(f32)
    ang = pos[:, None] * inv[None, :]
    cos, sin = jnp.cos(ang), jnp.sin(ang)
    cos128 = jnp.concatenate([cos, cos, cos, cos], axis=1)
    sin128 = jnp.concatenate([-sin, sin, -sin, sin], axis=1)
    return cos128, sin128


def _rope128(x, cos128, sin128):
    lane = lax.broadcasted_iota(jnp.int32, x.shape, 1)
    swapped = jnp.where(lane % HEAD_DIM < HALF, pltpu.roll(x, 128 - HALF, 1), pltpu.roll(x, HALF, 1))
    return x * cos128 + swapped * sin128


def _qkv_prep(z, cos128, sin128):
    rows = z.shape[0]

    def body(q_ref, kv_ref, c_ref, s_ref, qo_ref, ko_ref, vo_ref):
        c, s = c_ref[...], s_ref[...]
        for g in range(D // 128):
            qo_ref[:, g * 128:(g + 1) * 128] = (_rope128(q_ref[:, g * 128:(g + 1) * 128], c, s)
                                                * (HEAD_DIM ** -0.5)).astype(bf16)
        for g in range(2):
            kr = _rope128(kv_ref[:, g * 128:(g + 1) * 128], c, s)
            for j in range(2):
                ko_ref[2 * g + j] = kr[:, j * HEAD_DIM:(j + 1) * HEAD_DIM].astype(bf16)
        for h in range(N_KV):
            vo_ref[h] = kv_ref[:, 256 + h * HEAD_DIM:256 + (h + 1) * HEAD_DIM].astype(bf16)

    return pl.pallas_call(
        body, grid=(rows // BLK,),
        in_specs=[pl.BlockSpec((BLK, D), lambda i: (i, OFF_Q // D)),
                  pl.BlockSpec((BLK, 512), lambda i: (i, OFF_K // 512)),
                  pl.BlockSpec((BLK, 128), lambda i: (i, 0)),
                  pl.BlockSpec((BLK, 128), lambda i: (i, 0))],
        out_specs=[pl.BlockSpec((BLK, D), lambda i: (i, 0)),
                   pl.BlockSpec((N_KV, BLK, HEAD_DIM), lambda i: (0, i, 0)),
                   pl.BlockSpec((N_KV, BLK, HEAD_DIM), lambda i: (0, i, 0))],
        out_shape=[SDS((rows, D), bf16), SDS((N_KV, rows, HEAD_DIM), bf16), SDS((N_KV, rows, HEAD_DIM), bf16)],
        name="qkv_prep", compiler_params=_cp(("arbitrary",)),
    )(z, z, cos128, sin128)


def _attn_mask(n):
    qi = n * BLK + lax.broadcasted_iota(jnp.int32, (BLK, 2 * BLK + N_META), 0)
    c = lax.broadcasted_iota(jnp.int32, (BLK, 2 * BLK + N_META), 1)
    jb = (n - 1) * BLK + c
    band = (jb >= BLK) & (jb <= qi) & (qi - jb < BLK)
    meta = (ROW0 + c - 2 * BLK) <= qi
    return ((c < 2 * BLK) & band) | ((c >= 2 * BLK) & meta)


N_KEYS = 2 * BLK + N_META


def _stack_heads(t):
    return jnp.concatenate([t[:, g * HEAD_DIM:(g + 1) * HEAD_DIM] for g in range(GROUP)], axis=0)


def _sink_column(sink_ref, h):
    g = lax.broadcasted_iota(jnp.int32, (GROUP, 1, 1), 0)
    col = jnp.zeros((GROUP, 1, 1), f32)
    for j in range(GROUP):
        col = jnp.where(g == j, sink_ref[h * GROUP + j], col)
    return col


def _kv_specs(last):
    cl = lambda n: jnp.minimum(n, last)
    return [pl.BlockSpec((None, N_META, HEAD_DIM), lambda h, n: (h, ROW0 // N_META, 0)),
            pl.BlockSpec((None, BLK, HEAD_DIM), lambda h, n: (h, jnp.maximum(cl(n) - 1, 0), 0)),
            pl.BlockSpec((None, BLK, HEAD_DIM), lambda h, n: (h, cl(n), 0))]


def _attn_fwd(q_r, k_r, v_b, z, sinks):
    rows = q_r.shape[0]
    nb = rows // BLK

    def body(sink_ref, q_ref, km_ref, kp_ref, kc_ref, vm_ref, vp_ref, vc_ref, ga_ref, o_ref, yb_ref, ybt_ref, lse_ref):
        h, n = pl.program_id(0), pl.program_id(1)
        kk = jnp.concatenate([kp_ref[...], kc_ref[...], km_ref[...]], axis=0)
        vv = jnp.concatenate([vp_ref[...], vc_ref[...], vm_ref[...]], axis=0)
        q2 = _stack_heads(q_ref[...])
        s = jnp.where(_attn_mask(n)[None], _dot_nt(q2, kk).reshape(GROUP, BLK, N_KEYS), NEG_INF)
        sink = _sink_column(sink_ref, h)
        m = jnp.maximum(jnp.max(s, axis=-1, keepdims=True), sink)
        p = jnp.exp(s - m)
        den = jnp.sum(p, axis=-1, keepdims=True) + jnp.exp(sink - m)
        o2 = _dot((p / den).astype(bf16).reshape(GROUP * BLK, N_KEYS), vv)
        lse = m + jnp.log(den)
        for g in range(GROUP):
            o_ref[:, g * HEAD_DIM:(g + 1) * HEAD_DIM] = o2[g * BLK:(g + 1) * BLK]
            lse_ref[:, g:g + 1] = lse[g]
        yb = o_ref[...] * _silu_and_grad(ga_ref[...])[0]
        yb_ref[...] = yb.astype(bf16)
        ybt_ref[...] = yb.T.astype(bf16)

    tile = pl.BlockSpec((BLK, 512), lambda h, n: (n, h))
    return pl.pallas_call(
        body, grid=(N_KV, nb),
        in_specs=[pl.BlockSpec(memory_space=pltpu.SMEM), tile] + _kv_specs(nb - 1) + _kv_specs(nb - 1)
                 + [pl.BlockSpec((BLK, 512), lambda h, n: (n, OFF_GA // 512 + h))],
        out_specs=[tile, tile, pl.BlockSpec((512, BLK), lambda h, n: (h, n)),
                   pl.BlockSpec((None, BLK, GROUP), lambda h, n: (h, n, 0))],
        out_shape=[SDS((rows, D), f32), SDS((rows, D), bf16), SDS((D, rows), bf16),
                   SDS((N_KV, rows, GROUP), f32)],
        name="attn_fwd", compiler_params=_cp(("arbitrary", "arbitrary")),
    )(sinks, q_r, k_r, k_r, k_r, v_b, v_b, v_b, z)


def _attn_bwd(dyb, o32, lse, q_r, k_r, v_b, z, sinks):
    rows = q_r.shape[0]
    nb = rows // BLK
    cl = lambda n: jnp.minimum(n, nb - 1)

    def body(sink_ref, dyb_ref, o_ref, lse_ref, q_ref, km_ref, kp_ref, kc_ref, vm_ref, vp_ref, vc_ref, ga_ref,
             dq_ref, dga_ref, dk_ref, dv_ref, dkm_ref, dvm_ref, dsr_ref, ck_s, cv_s):
        h, n = pl.program_id(0), pl.program_id(1)

        @pl.when(n == 0)
        def _():
            dkm_ref[...] = jnp.zeros_like(dkm_ref)
            dvm_ref[...] = jnp.zeros_like(dvm_ref)
            ck_s[...] = jnp.zeros_like(ck_s)
            cv_s[...] = jnp.zeros_like(cv_s)

        @pl.when(n < nb)
        def _():
            kk = jnp.concatenate([kp_ref[...], kc_ref[...], km_ref[...]], axis=0)
            vv = jnp.concatenate([vp_ref[...], vc_ref[...], vm_ref[...]], axis=0)
            sg, dsg = _silu_and_grad(ga_ref[...])
            dyb_v = dyb_ref[...]
            o_v = o_ref[...]
            dga_ref[...] = (dyb_v * o_v * dsg).astype(bf16)
            q2 = _stack_heads(q_ref[...])
            do2 = _stack_heads(dyb_v * sg)
            lse_v = lse_ref[...]
            lse = jnp.concatenate([lse_v[:, g:g + 1] for g in range(GROUP)], axis=0).reshape(GROUP, BLK, 1)
            delta = jnp.sum(do2 * _stack_heads(o_v), axis=-1, keepdims=True).reshape(GROUP, BLK, 1)
            s = jnp.where(_attn_mask(n)[None], _dot_nt(q2, kk).reshape(GROUP, BLK, N_KEYS), NEG_INF)
            p = jnp.exp(s - lse)
            do2b = do2.astype(bf16)
            ds = (p * (_dot_nt(do2b, vv).reshape(GROUP, BLK, N_KEYS) - delta)).astype(bf16)
            ds = ds.reshape(GROUP * BLK, N_KEYS)
            dsr = -jnp.exp(_sink_column(sink_ref, h) - lse) * delta
            dq2 = _dot(ds, kk)
            for g in range(GROUP):
                dq_ref[:, g * HEAD_DIM:(g + 1) * HEAD_DIM] = dq2[g * BLK:(g + 1) * BLK]
                dsr_ref[:, g:g + 1] = dsr[g]
            dkk = _dot_tn(ds, q2)
            dvv = _dot_tn(p.astype(bf16).reshape(GROUP * BLK, N_KEYS), do2b)
            dk_ref[...] = ck_s[...] + dkk[:BLK]
            dv_ref[...] = cv_s[...] + dvv[:BLK]
            ck_s[...] = dkk[BLK:2 * BLK]
            cv_s[...] = dvv[BLK:2 * BLK]
            dkm_ref[...] += dkk[2 * BLK:]
            dvm_ref[...] += dvv[2 * BLK:]

        @pl.when(n == nb)
        def _():
            dk_ref[...] = ck_s[...]
            dv_ref[...] = cv_s[...]

    tile = pl.BlockSpec((BLK, 512), lambda h, n: (cl(n), h))
    kvout = pl.BlockSpec((None, BLK, HEAD_DIM), lambda h, n: (h, jnp.maximum(n - 1, 0), 0))
    mout = pl.BlockSpec((None, N_META, HEAD_DIM), lambda h, n: (h, 0, 0))
    stat = pl.BlockSpec((None, BLK, GROUP), lambda h, n: (h, cl(n), 0))
    return pl.pallas_call(
        body, grid=(N_KV, nb + 1),
        in_specs=[pl.BlockSpec(memory_space=pltpu.SMEM), tile, tile, stat, tile] + _kv_specs(nb - 1)
                 + _kv_specs(nb - 1) + [pl.BlockSpec((BLK, 512), lambda h, n: (cl(n), OFF_GA // 512 + h))],
        out_specs=[tile, tile, kvout, kvout, mout, mout, stat],
        out_shape=[SDS((rows, D), f32), SDS((rows, D), bf16),
                   SDS((N_KV, rows, HEAD_DIM), f32), SDS((N_KV, rows, HEAD_DIM), f32),
                   SDS((N_KV, N_META, HEAD_DIM), f32), SDS((N_KV, N_META, HEAD_DIM), f32),
                   SDS((N_KV, rows, GROUP), f32)],
        scratch_shapes=[pltpu.VMEM((BLK, HEAD_DIM), f32), pltpu.VMEM((BLK, HEAD_DIM), f32)],
        name="attn_bwd", compiler_params=_cp(("arbitrary", "arbitrary")),
    )(sinks, dyb, o32, lse, q_r, k_r, k_r, k_r, v_b, v_b, v_b, z)


def _qkv_finish(dq, dk, dv, dkm, dvm, cos128, sin128):
    rows = dq.shape[0]

    def body(dq_ref, dk_ref, dv_ref, dkm_ref, dvm_ref, c_ref, s_ref, oq_ref, okv_ref):
        first = (pl.program_id(0) == 0).astype(f32)
        c, s = c_ref[...], -s_ref[...]
        for g in range(D // 128):
            oq_ref[:, g * 128:(g + 1) * 128] = (_rope128(dq_ref[:, g * 128:(g + 1) * 128], c, s)
                                                * (HEAD_DIM ** -0.5)).astype(bf16)
        pad = jnp.zeros((ROW0, HEAD_DIM), f32)
        ks = [dk_ref[h] + first * jnp.concatenate([pad, dkm_ref[h]], axis=0) for h in range(N_KV)]
        vs = [dv_ref[h] + first * jnp.concatenate([pad, dvm_ref[h]], axis=0) for h in range(N_KV)]
        for g in range(2):
            kp = jnp.concatenate([ks[2 * g], ks[2 * g + 1]], axis=1)
            okv_ref[:, g * 128:(g + 1) * 128] = _rope128(kp, c, s).astype(bf16)
            okv_ref[:, 256 + g * 128:256 + (g + 1) * 128] = jnp.concatenate([vs[2 * g], vs[2 * g + 1]], axis=1).astype(bf16)

    kv = pl.BlockSpec((N_KV, BLK, HEAD_DIM), lambda i: (0, i, 0))
    mt = pl.BlockSpec((N_KV, N_META, HEAD_DIM), lambda i: (0, 0, 0))
    return pl.pallas_call(
        body, grid=(rows // BLK,),
        in_specs=[pl.BlockSpec((BLK, D), lambda i: (i, 0)), kv, kv, mt, mt,
                  pl.BlockSpec((BLK, 128), lambda i: (i, 0)), pl.BlockSpec((BLK, 128), lambda i: (i, 0))],
        out_specs=[pl.BlockSpec((BLK, D), lambda i: (i, 0)), pl.BlockSpec((BLK, 512), lambda i: (i, 0))],
        out_shape=[SDS((rows, D), bf16), SDS((rows, 512), bf16)],
        name="qkv_finish", compiler_params=_cp(("arbitrary",)),
    )(dq, dk, dv, dkm, dvm, cos128, sin128)


_TW = 512


def _mix_specs(rows):
    tr = _row_chunk(rows)
    tile = pl.BlockSpec((tr, _TW), lambda i, j: (i, j))
    ga = pl.BlockSpec((tr, _TW), lambda i, j: (i, OFF_G // _TW + j))
    gb = pl.BlockSpec((tr, _TW), lambda i, j: (i, (OFF_G + D) // _TW + j))
    return (rows // tr, D // _TW), tile, ga, gb


def _mix_fwd(y_a, y_b, z):
    rows = y_a.shape[0]
    tw = 256
    col = lambda off: pl.BlockSpec((rows, tw), lambda j: (0, off // tw + j))

    def body(ya_ref, yb_ref, ga_ref, gb_ref, o_ref, ot_ref):
        mixed = (_sigmoid(ga_ref[...]) * ya_ref[...].astype(f32)
                 + _sigmoid(gb_ref[...]) * yb_ref[...].astype(f32))
        o_ref[...] = mixed.astype(bf16)
        ot_ref[...] = mixed.T.astype(bf16)

    return pl.pallas_call(
        body, grid=(D // tw,), in_specs=[col(0), col(0), col(OFF_G), col(OFF_G + D)],
        out_specs=[col(0), pl.BlockSpec((tw, rows), lambda j: (j, 0))],
        out_shape=[SDS((rows, D), bf16), SDS((D, rows), bf16)],
        name="mix_fwd", compiler_params=_cp(("arbitrary",)),
    )(y_a, y_b, z, z)


def _mix_bwd(dmixed, y_a, y_b, z):
    rows = y_a.shape[0]
    grid, _mix_tile, _mix_ga, _mix_gb = _mix_specs(rows)

    def body(dm_ref, ya_ref, yb_ref, ga_ref, gb_ref, dya_ref, dyb_ref, dga_ref, dgb_ref):
        dm = dm_ref[...].astype(f32)
        sa, sb = _sigmoid(ga_ref[...]), _sigmoid(gb_ref[...])
        dya_ref[...] = (dm * sa).astype(bf16)
        dyb_ref[...] = (dm * sb).astype(bf16)
        dga_ref[...] = (dm * ya_ref[...].astype(f32) * sa * (1.0 - sa)).astype(bf16)
        dgb_ref[...] = (dm * yb_ref[...].astype(f32) * sb * (1.0 - sb)).astype(bf16)

    return pl.pallas_call(
        body, grid=grid, in_specs=[_mix_tile, _mix_tile, _mix_tile, _mix_ga, _mix_gb],
        out_specs=[_mix_tile] * 4, out_shape=[SDS((rows, D), bf16)] * 4,
        name="mix_bwd", compiler_params=_cp(("arbitrary", "arbitrary")),
    )(dmixed, y_a, y_b, z, z)


def _final_ln(out32, h32, tgt, ln_g, ln_b):
    rows = out32.shape[0]

    def body(o_ref, h_ref, t_ref, g_ref, b_ref, du_ref, dub_ref, st_ref):
        i = pl.program_id(0)
        g = g_ref[...]
        y, xhat, rstd = _ln_rows(ALPHA * h_ref[...] + o_ref[...], g, b_ref[...])
        e = jnp.where(i > 0, y - t_ref[0], 0.0)
        dy = e * (1.0 / D)
        du = _ln_rows_bwd(dy, g, xhat, rstd)
        du_ref[...] = du
        dub_ref[...] = du.astype(bf16)
        st = jnp.concatenate([_colsum(dy * xhat), _colsum(dy), _colsum(du), _colsum(e * e) * (0.5 / D),
                              jnp.zeros((4, D), f32)], axis=0)

        @pl.when(i == 0)
        def _():
            st_ref[...] = st

        @pl.when(i > 0)
        def _():
            st_ref[...] += st

    row = pl.BlockSpec((BLK, D), lambda i: (i, 0))
    vec = pl.BlockSpec((1, D), lambda i: (0, 0))
    return pl.pallas_call(
        body, grid=(rows // BLK,),
        in_specs=[row, row, pl.BlockSpec((1, BLK, D), lambda i: (0, jnp.maximum(i - 1, 0), 0)), vec, vec],
        out_specs=[row, row, pl.BlockSpec((8, D), lambda i: (0, 0))],
        out_shape=[SDS((rows, D), f32), SDS((rows, D), bf16), SDS((8, D), f32)],
        name="final_ln", compiler_params=_cp(("arbitrary",)),
    )(out32, h32, tgt, ln_g, ln_b)


def _step_rnn(h32, hb, z, wrg, smallw, p, zero):
    rows = z.shape[0]
    cos128, sin128 = _rope_tables(rows)
    cos128 = cos128 + zero
    xc, hr, ya, ya_t = _rnn_fwd(z, smallw, p["conv_b"] + zero, wrg, p["b_ra"], p["b_ri"], p["lru_lambda"])
    q_r, k_r, v_b = _qkv_prep(z, cos128, sin128)
    return dict(cos128=cos128, sin128=sin128, h32=h32, hb=hb, z=z, xc=xc, hr=hr, ya=ya, ya_t=ya_t,
                q_r=q_r, k_r=k_r, v_b=v_b)


def _step_attn(s, p, zero):
    sinks = p["sinks"].reshape(N_KV * GROUP) + zero[0]
    o32, yb, yb_t, lse = _attn_fwd(s["q_r"], s["k_r"], s["v_b"], s["z"], sinks)
    return dict(s, sinks=sinks, o32=o32, yb=yb, yb_t=yb_t, lse=lse)


def _step_merge(s, tgt, w3, p):
    ya, yb, z = s["ya"], s["yb"], s["z"]
    y_a = _mm(ya, w3, sel=0, out_dtype=bf16, name="mm_ya")
    y_b = _mm(yb, w3, sel=1, out_dtype=bf16, name="mm_yb")
    mixed, mixed_t = _mix_fwd(y_a, y_b, z)
    out32 = _mm(mixed, w3, sel=2, bias=p["b_o"], name="mm_out")
    du32, dub, st_out = _final_ln(out32, s["h32"], tgt, p["ln_g"], p["ln_b"])

    g_wo = _mm(mixed_t, dub, out_dtype=bf16, name="mm_dwo")
    dmixed = _mm(dub, w3, sel=2, nt=True, out_dtype=bf16, name="mm_dmixed")
    dya_b, dyb_b, dma, dmb = _mix_bwd(dmixed, y_a, y_b, z)
    g_wrnn = _mm(s["ya_t"], dya_b, out_dtype=bf16, name="mm_dwrnn")
    g_wattn = _mm(s["yb_t"], dyb_b, out_dtype=bf16, name="mm_dwattn")
    dya = _mm(dya_b, w3, sel=0, nt=True, name="mm_dya")
    dyb = _mm(dyb_b, w3, sel=1, nt=True, name="mm_dyb")
    return dict(du32=du32, st_out=st_out, dma=dma, dmb=dmb, dya=dya, dyb=dyb, g_wo=g_wo, g_wrnn=g_wrnn,
                g_wattn=g_wattn)


def _step_backward(s, t, wrg, smallw, p, conv_b):
    z = s["z"]
    dxr, dgr, g_wrg, vec_rnn = _rnn_bwd(t["dya"], s["hr"], s["xc"], z, smallw, conv_b, wrg, p["b_ra"], p["b_ri"],
                                        p["lru_lambda"])
    dq_r, dga, dk, dv, dkm, dvm, dsr = _attn_bwd(t["dyb"], s["o32"], s["lse"], s["q_r"], s["k_r"], s["v_b"], z,
                                                 s["sinks"])
    dq, dkv = _qkv_finish(dq_r, dk, dv, dkm, dvm, s["cos128"], s["sin128"])
    dz_parts = [(dxr, D), (dgr, D), (dq, D), (dkv, 512), (dga, D), (t["dma"], D), (t["dmb"], D)]
    return dict(vec_rnn=vec_rnn, dsr=dsr, g_wrg=g_wrg, dz_parts=dz_parts)


def _step_input_grad(dz, w_t, after, du32, x, smallw, p):
    dh = _mm_dh(dz, w_t, after)
    grad_x, dmeta, st_emb = _ln_emb_bwd(dh, du32, x, smallw, p["ln_emb_g"])
    return dict(grad_x=grad_x, dmeta=dmeta, st_emb=st_emb)


_ANY = pl.BlockSpec(memory_space=pl.ANY)
_VMEM = pl.BlockSpec(memory_space=pltpu.VMEM)


def _place():
    x, y, c = lax.axis_index("x"), lax.axis_index("y"), lax.axis_index("c")
    return x, y, c


def _dev(px, py, pc):
    return 4 * px + 2 * py + pc


def _tile_rows(r):
    return max(t for t in range(16, 321, 16) if r % t == 0) if r > 320 else r


def _cast_w_in(w_in_t):
    tm = _tile_rows(SHARD_IN)

    def body(i_ref, o_ref):
        o_ref[...] = i_ref[...].astype(bf16)

    return pl.pallas_call(
        body, grid=(SHARD_IN // tm,),
        in_specs=[pl.BlockSpec((tm, D), lambda i: (i, 0))],
        out_specs=pl.BlockSpec((tm, D), lambda i: (i, 0)),
        out_shape=SDS((SHARD_IN, D), bf16), name="cast_w_in", compiler_params=_cp(("arbitrary",)),
    )(w_in_t)


def _cast_small(w_rnn_out, w_attn_out, w_o, w_ra, w_ri, meta, conv_w):
    def body(a_ref, b_ref, c_ref, ra_ref, ri_ref, m_ref, cw_ref, w3_ref, wrg_ref, sw_ref):
        w3_ref[0] = a_ref[0].astype(bf16)
        w3_ref[1] = b_ref[0].astype(bf16)
        w3_ref[2] = c_ref[0].astype(bf16)
        wrg_ref[0] = ra_ref[0].astype(bf16)
        wrg_ref[1] = ri_ref[0].astype(bf16)
        sw_ref[...] = jnp.concatenate([m_ref[...], cw_ref[0], jnp.zeros((4, 256), f32)], axis=0)

    return pl.pallas_call(
        body,
        out_shape=[SDS((3, 256, D), bf16), SDS((2, N_RNN_BLOCKS, 32, RNN_BLOCK), bf16), SDS((24, 256), f32)],
        name="cast_small", compiler_params=_cp(None),
    )(w_rnn_out, w_attn_out, w_o, w_ra, w_ri, meta, conv_w)


def _all_gather(shards, later):
    n = len(shards)
    nl = len(later)

    def body(*refs):
        ins, outs = refs[:n], refs[n + nl:2 * n + nl]
        send_sems, recv_sems, local_sems = refs[2 * (n + nl):]
        x, y, c = _place()
        me, sibling = (x, y, c), (x, y, 1 - c)
        chips = [(1 - x, y), (x, 1 - y), (1 - x, 1 - y)]

        def copy(a, k, block, to, src=None):
            dst = outs[a].at[_dev(*block)]
            return pltpu.make_async_remote_copy(
                src_ref=dst if src is None else src, dst_ref=dst,
                send_sem=send_sems.at[a * 7 + k], recv_sem=recv_sems.at[a * 7 + k],
                device_id=to, device_id_type=MESH)

        all_ins, all_outs = refs[:n + nl], refs[n + nl:2 * (n + nl)]
        mine = [pltpu.make_async_copy(all_ins[a], all_outs[a].at[_dev(*me)], local_sems.at[a]) for a in range(n + nl)]
        for cp in mine:
            cp.start()
        first = []
        for a in range(n):
            first.append(copy(a, 0, me, sibling, src=ins[a]))
            first += [copy(a, 1 + j, me, (*chip, c), src=ins[a]) for j, chip in enumerate(chips)]
        for cp in first:
            cp.start()
        passed = []
        for a in range(n):
            for j, chip in enumerate(chips):
                copy(a, 1 + j, (*chip, c), me).wait_recv()
                cp = copy(a, 4 + j, (*chip, c), sibling)
                cp.start()
                passed.append(cp)
        for a in range(n):
            copy(a, 0, sibling, me).wait_recv()
            for j, chip in enumerate(chips):
                copy(a, 4 + j, (*chip, 1 - c), me).wait_recv()
        for cp in first + passed:
            cp.wait_send()
        for cp in mine:
            cp.wait()

    return pl.pallas_call(
        body, in_specs=[_ANY] * (n + nl), out_specs=[_ANY] * (n + nl),
        out_shape=[SDS((N_DEV, *s.shape), s.dtype) for s in (*shards, *later)],
        scratch_shapes=[pltpu.SemaphoreType.DMA((7 * n,)), pltpu.SemaphoreType.DMA((7 * n,)),
                        pltpu.SemaphoreType.DMA((n + nl,))],
        name="all_gather_weights",
    )(*shards, *later)


def _gather_small(shard):
    def body(s_ref, o_ref, send_sems, recv_sems):
        x, y, c = _place()
        me = _dev(x, y, c)
        copies = []
        for k, (fx, fy, fc) in enumerate(_PEER_FLIPS):
            peer = ((x + fx) % 2, (y + fy) % 2, (c + fc) % 2)
            copies.append(_remote(s_ref, o_ref.at[me], send_sems, recv_sems, k, peer))
        for cp in copies:
            cp.start()
        o_ref[me] = s_ref[...]
        for cp in copies:
            cp.wait()

    return pl.pallas_call(
        body, in_specs=[_VMEM], out_specs=_VMEM, out_shape=SDS((N_DEV, *shard.shape), shard.dtype),
        scratch_shapes=[pltpu.SemaphoreType.DMA((7,)), pltpu.SemaphoreType.DMA((7,))],
        name="gather_small",
    )(shard)


def _gather_project(w_s, smalls, later, hb, b_in, order):
    arrays = (w_s, *smalls, *later)
    na, n = len(arrays), 1 + len(smalls)
    rows = hb.shape[0]
    cm = _row_chunk(rows)
    nm = rows // cm
    pair = 2 * SHARD_IN

    def body(order_ref, *refs):
        ins, hb_ref, b_ref = refs[:na], refs[na], refs[na + 1]
        outs, z_ref = refs[na + 2:2 * na + 2], refs[2 * na + 2]
        wbuf, send_sems, recv_sems, local_sems, load_sems = refs[2 * na + 3:]
        k, mi = pl.program_id(0), pl.program_id(1)
        x, y, c = _place()
        me, sibling = (x, y, c), (x, y, 1 - c)
        chips = [(1 - x, y), (x, 1 - y), (1 - x, 1 - y)]

        def copy(a, kk, block, to, src=None):
            dst = outs[a].at[_dev(*block)]
            return pltpu.make_async_remote_copy(
                src_ref=dst if src is None else src, dst_ref=dst,
                send_sem=send_sems.at[a * 7 + kk], recv_sem=recv_sems.at[a * 7 + kk],
                device_id=to, device_id_type=MESH)

        mine = [pltpu.make_async_copy(ins[a], outs[a].at[_dev(*me)], local_sems.at[a]) for a in range(na)]

        def to_sibling():
            return [copy(a, 0, me, sibling, src=ins[a]) for a in range(n)]

        def to_chip(j):
            return [copy(a, 1 + j, me, (*chips[j], c), src=ins[a]) for a in range(n)]

        def load_pair(chip):
            cps = [pltpu.make_async_copy(outs[0].at[_dev(*chip, cc)], wbuf.at[pl.ds(cc * SHARD_IN, SHARD_IN)],
                                         load_sems.at[cc]) for cc in range(2)]
            for cp in cps:
                cp.start()
            for cp in cps:
                cp.wait()

        @pl.when((k == 0) & (mi == 0))
        def _():
            for cp in mine + to_sibling() + to_chip(0) + to_chip(1):
                cp.start()
            mine[0].wait()
            copy(0, 0, sibling, me).wait_recv()
            load_pair((x, y))

        for j, chip in enumerate(chips):
            @pl.when((k == j + 1) & (mi == 0))
            def _():
                for a in range(n):
                    copy(a, 1 + j, (*chip, c), me).wait_recv()
                    copy(a, 4 + j, (*chip, c), sibling).start()
                if j == 0:
                    for cp in to_chip(2):
                        cp.start()
                copy(0, 4 + j, (*chip, 1 - c), me).wait_recv()
                load_pair(chip)

        z_ref[...] = _dot_nt(hb_ref[...], wbuf[...]) + b_ref[...]

        @pl.when((k == len(chips)) & (mi == nm - 1))
        def _():
            for a in range(1, n):
                copy(a, 0, sibling, me).wait_recv()
                for j, chip in enumerate(chips):
                    copy(a, 4 + j, (*chip, 1 - c), me).wait_recv()
            for cp in to_sibling() + to_chip(0) + to_chip(1) + to_chip(2):
                cp.wait_send()
            for a in range(n):
                for j, chip in enumerate(chips):
                    copy(a, 4 + j, (*chip, c), sibling).wait_send()
            for cp in mine[1:]:
                cp.wait()

    res = pl.pallas_call(
        body,
        grid_spec=pltpu.PrefetchScalarGridSpec(
            num_scalar_prefetch=1, grid=(N_DEV // 2, nm),
            in_specs=[_ANY] * na + [pl.BlockSpec((cm, D), lambda k, i, o: (i, 0)),
                                    pl.BlockSpec((1, pair), lambda k, i, o: (0, o[k]))],
            out_specs=[_ANY] * na + [pl.BlockSpec((cm, pair), lambda k, i, o: (i, o[k]))],
            scratch_shapes=[pltpu.VMEM((pair, D), bf16), pltpu.SemaphoreType.DMA((7 * n,)),
                            pltpu.SemaphoreType.DMA((7 * n,)), pltpu.SemaphoreType.DMA((na,)),
                            pltpu.SemaphoreType.DMA((2,))]),
        out_shape=[SDS((N_DEV, *s.shape), s.dtype) for s in arrays] + [SDS((rows, D_IN), f32)],
        name="gather_project", compiler_params=_cp(("arbitrary", "arbitrary"), 48),
    )(order, *arrays, hb, b_in)
    return res[:na], res[na]


_HBM = pl.BlockSpec(memory_space=pltpu.HBM)
_SEM = pl.BlockSpec(memory_space=pltpu.SEMAPHORE)
_PEER_FLIPS = [(f // 4, (f // 2) % 2, f % 2) for f in range(1, N_DEV)]


def _remote(src, dst, send_sems, recv_sems, k, to):
    return pltpu.make_async_remote_copy(src_ref=src, dst_ref=dst, send_sem=send_sems.at[k], recv_sem=recv_sems.at[k],
                                        device_id=to, device_id_type=MESH)


def _copies_direct(same_src):
    def make(srcs, lands, send_sems, recv_sems):
        x, y, c = _place()
        me = _dev(x, y, c)
        out = []
        for a in range(len(srcs)):
            for k, (fx, fy, fc) in enumerate(_PEER_FLIPS):
                peer = ((x + fx) % 2, (y + fy) % 2, (c + fc) % 2)
                src = srcs[a] if same_src else srcs[a].at[_dev(*peer)]
                out.append(_remote(src, lands[a].at[me], send_sems, recv_sems, 7 * a + k, peer))
        return out
    return make


def _copies_gather_chips(srcs, lands, send_sems, recv_sems):
    x, y, c = _place()
    chips = [(1 - x, y), (x, 1 - y), (1 - x, 1 - y)]
    return [_remote(srcs[a], lands[a].at[_dev(x, y, c)], send_sems, recv_sems, 3 * a + j, (qx, qy, c))
            for a in range(len(srcs)) for j, (qx, qy) in enumerate(chips)]


def _copies_gather_sibling(srcs, lands, send_sems, recv_sems):
    x, y, c = _place()
    chips = [(x, y), (1 - x, y), (x, 1 - y), (1 - x, 1 - y)]
    return [_remote(lands[a].at[_dev(qx, qy, c)], lands[a].at[_dev(qx, qy, c)], send_sems, recv_sems, 4 * a + j,
                    (x, y, 1 - c))
            for a in range(len(srcs)) for j, (qx, qy) in enumerate(chips)]


def _copies_siblings(srcs, lands, send_sems, recv_sems):
    x, y, c = _place()
    return [_remote(srcs[a].at[2 * q + (1 - c)], lands[a].at[q], send_sems, recv_sems, 4 * a + q, (x, y, 1 - c))
            for a in range(len(srcs)) for q in range(4)]


def _copies_chips(srcs, lands, send_sems, recv_sems):
    x, y, c = _place()
    chips = [(1 - x, y), (x, 1 - y), (1 - x, 1 - y)]
    return [_remote(srcs[a].at[2 * qx + qy], lands[a].at[j], send_sems, recv_sems, 3 * a + j, (qx, qy, c))
            for a in range(len(srcs)) for j, (qx, qy) in enumerate(chips)]


def _split_start(make, per_array, srcs, lands, dep, name):
    n = len(srcs)

    def body(*refs):
        send_sems, recv_sems, token = refs[2 * n + 1], refs[2 * n + 2], refs[-1]
        for cp in make(refs[:n], refs[n:2 * n], send_sems, recv_sems):
            cp.start()
        token[...] = jnp.zeros_like(token)

    hbm = lambda t: pltpu.with_memory_space_constraint(t, pltpu.HBM)
    res = pl.pallas_call(
        body, name=name,
        out_shape=(pltpu.SemaphoreType.DMA((per_array * n,)), pltpu.SemaphoreType.DMA((per_array * n,)),
                   *[pltpu.HBM(t.shape, t.dtype) for t in (*srcs, *lands)], SDS((8, 128), f32)),
        in_specs=[_HBM] * (2 * n) + [_ANY], out_specs=(_SEM, _SEM, *([_HBM] * (2 * n)), _VMEM),
        input_output_aliases={i: 2 + i for i in range(2 * n)},
        compiler_params=pltpu.CompilerParams(has_side_effects=pltpu.SideEffectType.DATAFLOW_SIDE_EFFECTING),
    )(*[hbm(t) for t in (*srcs, *lands)], dep)
    return res[0], res[1], list(res[2:2 + n]), list(res[2 + n:2 + 2 * n]), res[-1]


def _split_wait(make, send_sems, recv_sems, srcs, lands, after, name):
    n = len(srcs)

    def body(*refs):
        for cp in make(refs[:n], refs[n:2 * n], refs[2 * n], refs[2 * n + 1]):
            cp.wait_send()
            cp.wait_recv()

    res = pl.pallas_call(
        body, name=name,
        out_shape=tuple(pltpu.HBM(t.shape, t.dtype) for t in (*srcs, *lands)),
        in_specs=[_HBM] * (2 * n) + [_SEM, _SEM, _ANY], out_specs=tuple([_HBM] * (2 * n)),
        input_output_aliases={i: i for i in range(2 * n)},
        compiler_params=pltpu.CompilerParams(has_side_effects=pltpu.SideEffectType.DATAFLOW_SIDE_EFFECTING),
    )(*srcs, *lands, send_sems, recv_sems, after)
    return list(res[:n]), list(res[n:])


def _adamw_direct(g, land, me_idx, w, m, v, name):
    r, wd = w.shape
    tr = min(r, 256)

    def body(me_ref, *refs):
        g_ref, peers = refs[0], refs[1:N_DEV]
        w_ref, m_ref, v_ref, g_out, d_out, m_out, v_out = refs[N_DEV:]
        gs = g_ref[...].astype(f32)
        for p_ref in peers:
            gs = gs + p_ref[...].astype(f32)
        d, mn, vn = _adamw(w_ref[...], gs, m_ref[...], v_ref[...])
        g_out[...] = gs
        d_out[...] = d
        m_out[...] = mn
        v_out[...] = vn

    tile = pl.BlockSpec((tr, wd), lambda i, me_ref: (i, 0))
    slot = lambda k: pl.BlockSpec((None, tr, wd), lambda i, me_ref: ((me_ref[0] + k) % N_DEV, i, 0))
    return pl.pallas_call(
        body,
        grid_spec=pltpu.PrefetchScalarGridSpec(
            num_scalar_prefetch=1, grid=(r // tr,),
            in_specs=[slot(0)] + [slot(k) for k in range(1, N_DEV)] + [tile, tile, tile],
            out_specs=[tile] * 4),
        out_shape=[SDS((r, wd), f32)] * 4, name=name, compiler_params=_cp(("arbitrary",), 48),
    )(me_idx, g, *([land] * (N_DEV - 1)), w, m, v)


def _pair_sum(g, r1, c_idx, name):
    _, r, w = g.shape
    tr = _tile_rows(r)

    def body(c_ref, g_ref, r_ref, o_ref):
        o_ref[...] = (g_ref[...].astype(f32) + r_ref[...].astype(f32)).astype(bf16)

    return pl.pallas_call(
        body,
        grid_spec=pltpu.PrefetchScalarGridSpec(
            num_scalar_prefetch=1, grid=(4, r // tr),
            in_specs=[pl.BlockSpec((None, tr, w), lambda q, i, c_ref: (2 * q + c_ref[0], i, 0)),
                      pl.BlockSpec((None, tr, w), lambda q, i, c_ref: (q, i, 0))],
            out_specs=pl.BlockSpec((None, tr, w), lambda q, i, c_ref: (q, i, 0))),
        out_shape=SDS((4, r, w), bf16), name=name, compiler_params=_cp(("arbitrary", "arbitrary")),
    )(c_idx, g, r1)


def _adamw(w, g, m, v):
    m = ADAM_B1 * m + (1.0 - ADAM_B1) * g
    v = ADAM_B2 * v + (1.0 - ADAM_B2) * (g * g)
    m_hat = m / (1.0 - ADAM_B1 ** ADAM_STEP)
    v_hat = v / (1.0 - ADAM_B2 ** ADAM_STEP)
    delta = -ADAM_LR * (m_hat / (jnp.sqrt(v_hat) + ADAM_EPS) + ADAM_WD * w)
    return delta, m, v


def _adamw_big(pieces, q_idx, w, m, v, name, row_off=0):
    r, wd = w.shape
    tr = _tile_rows(r)
    np_ = len(pieces)
    wp = wd // np_

    def body(q_ref, *refs):
        w_ref, m_ref, v_ref, g_out, d_out, m_out, v_out = refs[2 * np_:]
        for k in range(np_):
            @pl.when(pl.program_id(1) == k)
            def _():
                p_ref, r_ref = refs[2 * k], refs[2 * k + 1]
                g = p_ref[...].astype(f32)
                for j in range(3):
                    g = g + r_ref[j].astype(f32)
                d, mn, vn = _adamw(w_ref[...], g, m_ref[...], v_ref[...])
                g_out[...] = g
                d_out[...] = d
                m_out[...] = mn
                v_out[...] = vn

    tile = pl.BlockSpec((tr, wp), lambda i, k, q_ref: (i, k))
    in_specs, args = [], []
    for part, r2 in pieces:
        in_specs += [pl.BlockSpec((None, tr, wp), lambda i, k, q_ref: (q_ref[0], row_off + i, 0)),
                     pl.BlockSpec((3, tr, wp), lambda i, k, q_ref: (0, row_off + i, 0))]
        args += [part, r2]
    return pl.pallas_call(
        body,
        grid_spec=pltpu.PrefetchScalarGridSpec(
            num_scalar_prefetch=1, grid=(r // tr, np_), in_specs=in_specs + [tile, tile, tile],
            out_specs=[tile] * 4),
        out_shape=[SDS((r, wd), f32)] * 4, name=name, compiler_params=_cp(("arbitrary", "arbitrary"), 48),
    )(q_idx, *args, w, m, v)


_SMALL_ROWS = 24


def _pack_small(st_emb, vec_rnn, st_out, dsr, db_in, dmeta):
    def body(se_ref, vr_ref, so_ref, dsr_ref, db_ref, dm_ref, sm_ref, sm2_ref):
        sm_ref[...] = jnp.zeros_like(sm_ref)
        sm2_ref[...] = jnp.zeros_like(sm2_ref)
        sm_ref[0:2, :] = se_ref[0:2, :]
        sm_ref[2:3, :] = vr_ref[3:4, :]
        sm_ref[3:6, :] = vr_ref[0:3, :]
        sm_ref[6:7, :] = so_ref[2:3, :]
        sm_ref[7:9, :] = so_ref[0:2, :]
        sm_ref[10:11, :] = so_ref[3:4, :]
        for h in range(N_KV):
            sm_ref[9:10, h * GROUP:(h + 1) * GROUP] = _colsum(dsr_ref[h])
        for j in range(6):
            sm_ref[16 + j:17 + j, :] = db_ref[0:1, j * D:(j + 1) * D]
        sm_ref[22:23, 0:D_IN - 6 * D] = db_ref[0:1, 6 * D:D_IN]
        for s in range(N_DEV):
            sm2_ref[s, 0:N_META, :] = dm_ref[:, s * 256:(s + 1) * 256]
            sm2_ref[s, N_META:N_META + CONV_WIDTH, :] = vr_ref[4:8, s * 256:(s + 1) * 256]

    return pl.pallas_call(
        body, out_shape=[SDS((_SMALL_ROWS, D), f32), SDS((N_DEV, 24, 256), f32)],
        name="pack_small", compiler_params=_cp(None),
    )(st_emb, vec_rnn, st_out, dsr, db_in, dmeta)


def _small_allreduce(sm, sm2):
    def body(sm_ref, sm2_ref, o_ref, o2_ref, buf, buf2, send_sems, recv_sems):
        x, y, c = _place()
        me = _dev(x, y, c)
        copies = []
        for f in range(1, N_DEV):
            fx, fy, fc = f // 4, (f // 2) % 2, f % 2
            peer = ((x + fx) % 2, (y + fy) % 2, (c + fc) % 2)
            for t, (src, dst) in enumerate(((sm_ref, buf), (sm2_ref, buf2))):
                k = 2 * (f - 1) + t
                copies.append(pltpu.make_async_remote_copy(
                    src_ref=src, dst_ref=dst.at[me], send_sem=send_sems.at[k], recv_sem=recv_sems.at[k],
                    device_id=peer, device_id_type=MESH))
        for cp in copies:
            cp.start()
        buf[me] = sm_ref[...]
        buf2[me] = sm2_ref[...]
        for cp in copies:
            cp.wait()
        acc, acc2 = buf[0], buf2[0]
        for e in range(1, N_DEV):
            acc, acc2 = acc + buf[e], acc2 + buf2[e]
        o_ref[...] = acc
        o2_ref[...] = acc2

    return pl.pallas_call(
        body, in_specs=[_VMEM, _VMEM], out_specs=[_VMEM, _VMEM],
        out_shape=[SDS(sm.shape, f32), SDS(sm2.shape, f32)],
        scratch_shapes=[pltpu.VMEM((N_DEV, *sm.shape), f32), pltpu.VMEM((N_DEV, *sm2.shape), f32),
                        pltpu.SemaphoreType.DMA((14,)), pltpu.SemaphoreType.DMA((14,))],
        name="small_allreduce",
    )(sm, sm2)


_SMALL_ROW_OF = {"ln_emb_g": 0, "ln_emb_b": 1, "conv_b": 2, "b_ra": 3, "b_ri": 4, "lru_lambda": 5, "b_o": 6,
                 "ln_g": 7, "ln_b": 8}
_SMALL_NAMES = ["ln_emb_g", "ln_emb_b", "conv_b", "b_ra", "b_ri", "lru_lambda", "b_o", "ln_g", "ln_b",
                "sinks", "b_in", "meta_tokens", "conv_w"]


def _small_update(sm, sm2_mine, wmv):
    def grad_of(name, sm_ref, s2_ref):
        if name in _SMALL_ROW_OF:
            r = _SMALL_ROW_OF[name]
            return sm_ref[r:r + 1, :]
        if name == "sinks":
            return sm_ref[9:10, 0:N_KV * GROUP]
        if name == "b_in":
            return jnp.concatenate([sm_ref[16 + j:17 + j, :] for j in range(7)], axis=1)[:, :D_IN]
        if name == "meta_tokens":
            return s2_ref[0:N_META, :]
        return s2_ref[N_META:N_META + CONV_WIDTH, :]

    def body(*refs):
        sm_ref, s2_ref = refs[0], refs[1]
        ins = refs[2:2 + 3 * len(_SMALL_NAMES)]
        outs = refs[2 + 3 * len(_SMALL_NAMES):]
        for i, name in enumerate(_SMALL_NAMES):
            w_ref, m_ref, v_ref = ins[3 * i:3 * i + 3]
            g = grad_of(name, sm_ref, s2_ref)
            d, mn, vn = _adamw(w_ref[...], g, m_ref[...], v_ref[...])
            outs[4 * i][...] = g
            outs[4 * i + 1][...] = d
            outs[4 * i + 2][...] = mn
            outs[4 * i + 3][...] = vn
        outs[-1][...] = jnp.broadcast_to(jnp.sum(sm_ref[10:11, :], axis=1, keepdims=True), (8, 128))

    args, out_shape = [sm, sm2_mine], []
    for name in _SMALL_NAMES:
        args += list(wmv[name])
        out_shape += [SDS(wmv[name][0].shape, f32)] * 4
    out_shape.append(SDS((8, 128), f32))
    res = pl.pallas_call(body, out_shape=out_shape, name="small_update", compiler_params=_cp(None))(*args)
    return {name: tuple(res[4 * i:4 * i + 4]) for i, name in enumerate(_SMALL_NAMES)}, res[-1][0, 0]


_WEIGHTS = ["meta_tokens", "ln_emb_g", "ln_emb_b", "w_in", "b_in", "conv_w", "conv_b", "w_ra", "b_ra", "w_ri",
            "b_ri", "lru_lambda", "sinks", "w_rnn_out", "w_attn_out", "w_o", "b_o", "ln_g", "ln_b"]
_SMALL_2D = {"meta_tokens": (N_META, 256), "conv_w": (CONV_WIDTH, 256), "b_in": (1, D_IN), "sinks": (1, N_KV * GROUP)}


def kernel(x, meta_tokens, ln_emb_g, ln_emb_b, w_in, b_in, conv_w, conv_b, w_ra, b_ra, w_ri, b_ri, lru_lambda, sinks, w_rnn_out, w_attn_out, w_o, b_o, ln_g, ln_b, loss_target, m_meta_tokens, m_ln_emb_g, m_ln_emb_b, m_w_in, m_b_in, m_conv_w, m_conv_b, m_w_ra, m_b_ra, m_w_ri, m_b_ri, m_lru_lambda, m_sinks, m_w_rnn_out, m_w_attn_out, m_w_o, m_b_o, m_ln_g, m_ln_b, v_meta_tokens, v_ln_emb_g, v_ln_emb_b, v_w_in, v_b_in, v_conv_w, v_conv_b, v_w_ra, v_b_ra, v_w_ri, v_b_ri, v_lru_lambda, v_sinks, v_w_rnn_out, v_w_attn_out, v_w_o, v_b_o, v_ln_g, v_ln_b):
    w = dict(meta_tokens=meta_tokens, ln_emb_g=ln_emb_g, ln_emb_b=ln_emb_b, w_in=w_in, b_in=b_in, conv_w=conv_w,
             conv_b=conv_b, w_ra=w_ra, b_ra=b_ra, w_ri=w_ri, b_ri=b_ri, lru_lambda=lru_lambda, sinks=sinks,
             w_rnn_out=w_rnn_out, w_attn_out=w_attn_out, w_o=w_o, b_o=b_o, ln_g=ln_g, ln_b=ln_b)
    m = dict(meta_tokens=m_meta_tokens, ln_emb_g=m_ln_emb_g, ln_emb_b=m_ln_emb_b, w_in=m_w_in, b_in=m_b_in,
             conv_w=m_conv_w, conv_b=m_conv_b, w_ra=m_w_ra, b_ra=m_b_ra, w_ri=m_w_ri, b_ri=m_b_ri,
             lru_lambda=m_lru_lambda, sinks=m_sinks, w_rnn_out=m_w_rnn_out, w_attn_out=m_w_attn_out, w_o=m_w_o,
             b_o=m_b_o, ln_g=m_ln_g, ln_b=m_ln_b)
    v = dict(meta_tokens=v_meta_tokens, ln_emb_g=v_ln_emb_g, ln_emb_b=v_ln_emb_b, w_in=v_w_in, b_in=v_b_in,
             conv_w=v_conv_w, conv_b=v_conv_b, w_ra=v_w_ra, b_ra=v_b_ra, w_ri=v_w_ri, b_ri=v_b_ri,
             lru_lambda=v_lru_lambda, sinks=v_sinks, w_rnn_out=v_w_rnn_out, w_attn_out=v_w_attn_out, w_o=v_w_o,
             b_o=v_b_o, ln_g=v_ln_g, ln_b=v_ln_b)
    px, py, pc = _place()
    as_idx = lambda t: jnp.reshape(t, (1,)).astype(jnp.int32)
    c_idx, q_idx, me_idx = as_idx(pc), as_idx(2 * px + py), as_idx(_dev(px, py, pc))

    w3_s, wrg_s, small_s = _cast_small(w_rnn_out, w_attn_out, w_o, w_ra, w_ri, meta_tokens, conv_w)
    vec = lambda name: w[name].reshape(1, -1)
    p = {k: vec(k) for k in ("ln_emb_g", "ln_emb_b", "b_in", "conv_b", "b_ra", "b_ri", "lru_lambda", "sinks",
                             "b_o", "ln_g", "ln_b")}
    w_in_t = lambda a: jnp.swapaxes(a, 1, 2).reshape(SHARD_IN, D)
    wg, wrg, smallw, w3_land = _all_gather([_cast_w_in(w_in_t(w_in)), wrg_s, small_s], [w3_s])
    w3_pending = _split_start(_copies_direct(True), 7, [w3_s], [w3_land], smallw, "gather_w3_start")
    w_full = wg.reshape(D_IN, D)

    zero = w3_pending[4][0:1, 0:1]
    h32, hb = _ln_emb(x, smallw, p["ln_emb_g"], p["ln_emb_b"])
    z = _mm(hb, w_full, nt=True, bias=p["b_in"] + zero, name="mm_z")
    s = _step_attn(_step_rnn(h32, hb, z, wrg, smallw, p, zero), p, zero)
    w3 = _split_wait(_copies_direct(True), *w3_pending[:4], s["lse"], "gather_w3_wait")[1][0]
    t = _step_merge(s, loss_target, w3, p)

    big = {}
    two_d = lambda name: (w[name].shape[-2], w[name].shape[-1])
    proj = ("w_o", "w_rnn_out", "w_attn_out")
    g_proj = [t[k].reshape(N_DEV, 256, D) for k in ("g_wo", "g_wrnn", "g_wattn")]
    g_pending = _split_start(_copies_direct(False), 7, g_proj, [lax.empty((N_DEV, 256, D), bf16) for _ in proj],
                             p["b_o"], "reduce_proj_start")
    u = _step_backward(s, t, wrg, smallw, p, p["conv_b"] + g_pending[4][0:1, 0:1])

    def siblings_start(gs, dep, tag):
        return _split_start(_copies_siblings, 4, gs, [lax.empty((4, *g.shape[1:]), bf16) for g in gs], dep,
                            "reduce_siblings_start_" + tag)

    def chips_start(gs, r1, dep, tag):
        parts = [_pair_sum(g, r, c_idx, "pair_sum_%s%d" % (tag, i)) for i, (g, r) in enumerate(zip(gs, r1))]
        return _split_start(_copies_chips, 3, parts, [lax.empty((3, *q.shape[1:]), bf16) for q in parts], dep,
                            "reduce_chips_start_" + tag)

    g_a, dz, db_in = _mm_dwin_parts(s["hb"], u["dz_parts"])
    shards = lambda g: g.reshape(N_DEV, SHARD_IN, W_IN_HALF)
    sib_a = siblings_start([shards(g_a), u["g_wrg"].reshape(N_DEV, 2 * RNN_BLOCK, RNN_BLOCK)], db_in, "a")
    g_b = _mm_dwin(s["hb"], dz, sib_a[4])
    sib_b = siblings_start([shards(g_b)], db_in, "b")
    chp_a = chips_start(*_split_wait(_copies_siblings, *sib_a[:4], sib_b[4], "reduce_siblings_wait_a"), db_in, "a")
    g_proj, g_land = _split_wait(_copies_direct(False), *g_pending[:4], chp_a[4], "reduce_proj_wait")
    for i, name in enumerate(proj):
        res = _adamw_direct(g_proj[i], g_land[i], me_idx, w[name].reshape(two_d(name)), m[name].reshape(two_d(name)),
                            v[name].reshape(two_d(name)), "adamw_" + name)
        big[name] = tuple(r.reshape(w[name].shape) for r in res)
    chp_b = chips_start(*_split_wait(_copies_siblings, *sib_b[:4], big["w_attn_out"][3], "reduce_siblings_wait_b"),
                        db_in, "b")
    u.update(_step_input_grad(dz, w_full, chp_b[4], t["du32"], x, smallw, p))
    u["db_in"] = db_in

    loc = {**t, **u}
    sm, sm2 = _pack_small(loc["st_emb"], loc["vec_rnn"], loc["st_out"], loc["dsr"], loc["db_in"], loc["dmeta"])
    sm, sm2 = _small_allreduce(sm, sm2)
    sm2_mine = lax.dynamic_index_in_dim(sm2, _dev(px, py, pc), 0, keepdims=False)
    two = lambda name, t: t.reshape(_SMALL_2D.get(name, (1, D)))
    small, loss = _small_update(sm, sm2_mine, {k: (two(k, w[k]), two(k, m[k]), two(k, v[k])) for k in _SMALL_NAMES})

    parts_a, r2_a = _split_wait(_copies_chips, *chp_a[:4], small["b_in"][3], "reduce_chips_wait_a")
    parts_b, r2_b = _split_wait(_copies_chips, *chp_b[:4], small["b_in"][2], "reduce_chips_wait_b")
    res = _adamw_big([(parts_a[0], r2_a[0]), (parts_b[0], r2_b[0])], q_idx, w_in_t(w["w_in"]), w_in_t(m["w_in"]),
                     w_in_t(v["w_in"]), "adamw_w_in")
    big["w_in"] = tuple(jnp.swapaxes(r.reshape(1, SHARD_IN, D), 1, 2) for r in res)
    for i, name in enumerate(("w_ra", "w_ri")):
        sq = (RNN_BLOCK, RNN_BLOCK)
        res = _adamw_big([(parts_a[1], r2_a[1])], q_idx, w[name].reshape(sq), m[name].reshape(sq), v[name].reshape(sq),
                         "adamw_" + name, row_off=i)
        big[name] = tuple(r.reshape(w[name].shape) for r in res)
    res = dict(big)
    for k in _SMALL_NAMES:
        res[k] = tuple(t.reshape(w[k].shape) for t in small[k])

    outs = [loss, loc["grad_x"]]
    for j in range(4):
        outs += [res[k][j] for k in _WEIGHTS]
    return tuple(outs)
```

```python
import functools

import jax
import jax.numpy as jnp
from jax import lax
from jax.experimental import pallas as pl
from jax.experimental.pallas import tpu as pltpu

f32, bf16 = jnp.float32, jnp.bfloat16
SDS = jax.ShapeDtypeStruct

N_DEV = 8
D = 2048
N_META = 16
BLK = 128
ROW0 = BLK - N_META
N_RNN_BLOCKS = 8
RNN_BLOCK = D // N_RNN_BLOCKS
CONV_WIDTH = 4
LRU_C = 8.0
HEAD_DIM = 64
N_KV = 4
GROUP = 8
HALF = HEAD_DIM // 2
ROPE_THETA = 10000.0
NEG_INF = -1e30
LN_EPS = 1e-5
ALPHA = 2.0 ** 0.25
D_IN = 12800
SHARD_IN = D_IN // N_DEV
OFF_GR, OFF_Q, OFF_K, OFF_V, OFF_GA, OFF_G = 2048, 4096, 6144, 6400, 6656, 8704
ADAM_LR, ADAM_B1, ADAM_B2, ADAM_EPS, ADAM_WD, ADAM_STEP = 1e-3, 0.9, 0.999, 1e-8, 0.01, 10
VMEM_LIMIT_MB = 56
MESH = pl.DeviceIdType.MESH


def _cp(sem=None, vmem_mb=40):
    return pltpu.CompilerParams(dimension_semantics=sem, vmem_limit_bytes=vmem_mb * 2 ** 20)


def _row_chunk(m):
    best = 16
    for c in range(16, 641, 16):
        if m % c == 0:
            best = c
    return best


def _sigmoid(x):
    return 1.0 / (1.0 + jnp.exp(-x))


def _silu_and_grad(x):
    s = _sigmoid(x)
    return x * s, s * (1.0 + x * (1.0 - s))


def _log_sigmoid(x):
    return jnp.minimum(x, 0.0) - jnp.log1p(jnp.exp(-jnp.abs(x)))


def _ln_rows(v, g, b):
    mu = jnp.mean(v, axis=-1, keepdims=True)
    c = v - mu
    var = jnp.mean(c * c, axis=-1, keepdims=True)
    rstd = lax.rsqrt(var + LN_EPS)
    xhat = c * rstd
    return xhat * g + b, xhat, rstd


def _ln_rows_bwd(dy, g, xhat, rstd):
    dxh = dy * g
    m1 = jnp.mean(dxh, axis=-1, keepdims=True)
    m2 = jnp.mean(dxh * xhat, axis=-1, keepdims=True)
    return rstd * (dxh - m1 - xhat * m2)


def _colsum(v):
    return jnp.sum(v, axis=0, keepdims=True)


def _dot(a, b):
    return jnp.dot(a, b, preferred_element_type=f32)


def _dot_nt(a, b):
    return lax.dot_general(a, b, (((1,), (1,)), ((), ())), preferred_element_type=f32)


def _dot_tn(a, b):
    return lax.dot_general(a, b, (((0,), (0,)), ((), ())), preferred_element_type=f32)


def _meta_full(sw_ref):
    return jnp.concatenate([sw_ref[s, 0:N_META, :] for s in range(N_DEV)], axis=1)


def _ln_emb(x, smallw, g_e, b_e):
    seq = x.shape[1]
    rows = seq + BLK
    nb = rows // BLK

    def body(x_ref, sw_ref, g_ref, b_ref, h32_ref, hb_ref):
        i = pl.program_id(0)
        g, b = g_ref[...], b_ref[...]

        def emit(blk):
            h32_ref[...] = blk
            hb_ref[...] = blk.astype(bf16)

        @pl.when(i == 0)
        def _():
            hm = _ln_rows(_meta_full(sw_ref), g, b)[0]
            emit(jnp.concatenate([jnp.zeros((ROW0, D), f32), hm], axis=0))

        @pl.when(i > 0)
        def _():
            emit(_ln_rows(x_ref[0], g, b)[0])

    return pl.pallas_call(
        body, grid=(nb,),
        in_specs=[pl.BlockSpec((1, BLK, D), lambda i: (0, jnp.maximum(i - 1, 0), 0)),
                  pl.BlockSpec((N_DEV, 24, 256), lambda i: (0, 0, 0)),
                  pl.BlockSpec((1, D), lambda i: (0, 0)),
                  pl.BlockSpec((1, D), lambda i: (0, 0))],
        out_specs=[pl.BlockSpec((BLK, D), lambda i: (i, 0)),
                   pl.BlockSpec((BLK, D), lambda i: (i, 0))],
        out_shape=[SDS((rows, D), f32), SDS((rows, D), bf16)],
        name="ln_emb", compiler_params=_cp(("arbitrary",)),
    )(x, smallw, g_e, b_e)


def _ln_emb_bwd(dh, du32, x, smallw, g_e):
    seq = x.shape[1]
    rows = seq + BLK
    nb = rows // BLK

    def body(dh_ref, du_ref, x_ref, sw_ref, g_ref, gx_ref, dmeta_ref, st_ref):
        i = pl.program_id(0)
        g = g_ref[...]
        dht = dh_ref[...] + ALPHA * du_ref[...]

        @pl.when(i == 0)
        def _():
            v = jnp.concatenate([jnp.zeros((ROW0, D), f32), _meta_full(sw_ref)], axis=0)
            valid = lax.broadcasted_iota(jnp.int32, (BLK, 1), 0) >= ROW0
            d = jnp.where(valid, dht, 0.0)
            _, xhat, rstd = _ln_rows(v, g, 0.0)
            dv = _ln_rows_bwd(d, g, xhat, rstd)
            dmeta_ref[...] = dv[ROW0:, :]
            st_ref[...] = jnp.concatenate([_colsum(d * xhat), _colsum(d), jnp.zeros((6, D), f32)], axis=0)

        @pl.when(i > 0)
        def _():
            _, xhat, rstd = _ln_rows(x_ref[0], g, 0.0)
            gx_ref[0] = _ln_rows_bwd(dht, g, xhat, rstd)
            st_ref[0:1, :] += _colsum(dht * xhat)
            st_ref[1:2, :] += _colsum(dht)

    return pl.pallas_call(
        body, grid=(nb,),
        in_specs=[pl.BlockSpec((BLK, D), lambda i: (i, 0)),
                  pl.BlockSpec((BLK, D), lambda i: (i, 0)),
                  pl.BlockSpec((1, BLK, D), lambda i: (0, jnp.maximum(i - 1, 0), 0)),
                  pl.BlockSpec((N_DEV, 24, 256), lambda i: (0, 0, 0)),
                  pl.BlockSpec((1, D), lambda i: (0, 0))],
        out_specs=[pl.BlockSpec((1, BLK, D), lambda i: (0, jnp.maximum(i - 1, 0), 0)),
                   pl.BlockSpec((N_META, D), lambda i: (0, 0)),
                   pl.BlockSpec((8, D), lambda i: (0, 0))],
        out_shape=[SDS((1, seq, D), f32), SDS((N_META, D), f32), SDS((8, D), f32)],
        name="ln_emb_bwd", compiler_params=_cp(("arbitrary",)),
    )(dh, du32, x, smallw, g_e)


def _mm(a, b, *, name, nt=False, sel=None, bias=None, out_dtype=f32, tn=512):
    m, k = a.shape
    cm = _row_chunk(m)
    stacked = sel is not None
    n = D if stacked else (b.shape[0] if nt else b.shape[1])
    am = m
    if stacked and nt:
        b_spec = pl.BlockSpec((tn // 256, None, 256, D), lambda j, i: (j, sel, 0, 0))
    elif stacked:
        b_spec = pl.BlockSpec((N_DEV, None, 256, tn), lambda j, i: (0, sel, 0, j))
    elif nt:
        b_spec = pl.BlockSpec((tn, k), lambda j, i: (j, 0))
    else:
        b_spec = pl.BlockSpec((k, tn), lambda j, i: (0, j))
    in_specs = [pl.BlockSpec((am, k), lambda j, i: (i, 0)), b_spec]
    args = [a, b]
    if bias is not None:
        in_specs.append(pl.BlockSpec((1, tn), lambda j, i: (0, j)))
        args.append(bias)

    def body(*refs):
        a_ref, b_ref, o_ref = refs[0], refs[1], refs[-1]
        bm = b_ref[...]
        if stacked:
            bm = bm.reshape((tn, D) if nt else (D, tn))
        for c in range(am // cm):
            acc = (_dot_nt if nt else _dot)(a_ref[c * cm:(c + 1) * cm, :], bm)
            if bias is not None:
                acc = acc + refs[2][...]
            o_ref[c * cm:(c + 1) * cm, :] = acc.astype(out_dtype)

    return pl.pallas_call(
        body, grid=(n // tn, m // am), in_specs=in_specs,
        out_specs=pl.BlockSpec((am, tn), lambda j, i: (i, j)),
        out_shape=SDS((m, n), out_dtype), name=name, compiler_params=_cp(("arbitrary", "arbitrary"), 48),
    )(*args)


def _mm_dh(dz, w_t, after):
    rows = dz.shape[0]
    tk, tn = 2560, 512
    cm = _row_chunk(rows)

    def body(a_ref, w_ref, after_ref, o_ref):
        kk = pl.program_id(1)
        for c in range(rows // cm):
            acc = _dot(a_ref[c * cm:(c + 1) * cm, :], w_ref[...])

            @pl.when(kk == 0)
            def _():
                o_ref[c * cm:(c + 1) * cm, :] = acc

            @pl.when(kk > 0)
            def _():
                o_ref[c * cm:(c + 1) * cm, :] += acc

    return pl.pallas_call(
        body, grid=(D // tn, D_IN // tk),
        in_specs=[pl.BlockSpec((rows, tk), lambda j, kk: (0, kk)),
                  pl.BlockSpec((tk, tn), lambda j, kk: (kk, j)),
                  pl.BlockSpec(memory_space=pl.ANY)],
        out_specs=pl.BlockSpec((rows, tn), lambda j, kk: (0, j)),
        out_shape=SDS((rows, D), f32), name="mm_dh", compiler_params=_cp(("arbitrary", "arbitrary"), 48),
    )(dz, w_t, after)


W_IN_HALF = D // 2


def _mm_dwin_parts(hb, parts):
    rows = hb.shape[0]
    tc = 512
    edges = [0]
    for _, w in parts:
        edges.append(edges[-1] + w // tc)

    def body(*refs):
        h_ref, (o_ref, dz_ref, db_ref) = refs[len(parts)], refs[len(parts) + 1:]
        j = pl.program_id(0)
        for p_ref, lo, hi in zip(refs, edges[:-1], edges[1:]):
            @pl.when((j >= lo) & (j < hi))
            def _():
                o_ref[...] = _dot_tn(p_ref[...], h_ref[...]).astype(bf16)
                dz_ref[...] = p_ref[...]

                def step(i, s):
                    blk = p_ref[pl.ds(pl.multiple_of(i * BLK, BLK), BLK), :].astype(f32)
                    return s + blk.reshape(BLK // 8, 8, tc).sum(axis=0)
                s = lax.fori_loop(0, rows // BLK, step, jnp.zeros((8, tc), f32))
                db_ref[...] = jnp.broadcast_to(_colsum(s), (8, tc))

    in_specs = [pl.BlockSpec((rows, tc), lambda j, lo=lo, hi=hi: (0, jnp.clip(j - lo, 0, hi - lo - 1)))
                for lo, hi in zip(edges[:-1], edges[1:])]
    return pl.pallas_call(
        body, grid=(D_IN // tc,),
        in_specs=in_specs + [pl.BlockSpec((rows, W_IN_HALF), lambda j: (0, 0))],
        out_specs=[pl.BlockSpec((tc, W_IN_HALF), lambda j: (j, 0)), pl.BlockSpec((rows, tc), lambda j: (0, j)),
                   pl.BlockSpec((8, tc), lambda j: (0, j))],
        out_shape=[SDS((D_IN, W_IN_HALF), bf16), SDS((rows, D_IN), bf16), SDS((8, D_IN), f32)],
        name="mm_dwin_0", compiler_params=_cp(("arbitrary",), VMEM_LIMIT_MB),
    )(*[a for a, _ in parts], hb)


def _mm_dwin(hb, dz, after):
    rows = dz.shape[0]
    tc = 640

    def body(dz_ref, h_ref, after_ref, o_ref):
        o_ref[...] = _dot_tn(dz_ref[...], h_ref[...]).astype(bf16)

    return pl.pallas_call(
        body, grid=(D_IN // tc,),
        in_specs=[pl.BlockSpec((rows, tc), lambda j: (0, j)),
                  pl.BlockSpec((rows, W_IN_HALF), lambda j: (0, 1)),
                  pl.BlockSpec(memory_space=pl.ANY)],
        out_specs=pl.BlockSpec((tc, W_IN_HALF), lambda j: (j, 0)),
        out_shape=SDS((D_IN, W_IN_HALF), bf16),
        name="mm_dwin_1", compiler_params=_cp(("arbitrary",), 48),
    )(dz, hb, after)


SCAN_ROWS = 32


def _scan8(a, b, reverse):
    idx = lax.broadcasted_iota(jnp.int32, a.shape, 0)
    for s in (1, 2, 4):
        sh = 8 - s if reverse else s
        a_sh, b_sh = pltpu.roll(a, sh, 0), pltpu.roll(b, sh, 0)
        m = (idx < 8 - s) if reverse else (idx >= s)
        b = jnp.where(m, a * b_sh + b, b)
        a = jnp.where(m, a * a_sh, a)
    return a, b


def _shift_rows(prev8, cur, k):
    ext = jnp.concatenate([prev8, cur], axis=0)
    return pltpu.roll(ext, k, 0)[8:, :]


def _gates(xc, w_ra, b_ra, w_ri, b_ri, ls):
    xb = xc.astype(bf16)
    r = _sigmoid(_dot(xb, w_ra) + b_ra)
    ig = _sigmoid(_dot(xb, w_ri) + b_ri)
    la = LRU_C * r * ls
    a = jnp.exp(la)
    mult = jnp.sqrt(jnp.tanh(-la) * (1.0 + a * a))
    return xb, r, ig, a, mult


_RNN_IN_SPECS = lambda rows: [
    pl.BlockSpec((1, 24, 256), lambda n: (n, 0, 0)),
    pl.BlockSpec((1, RNN_BLOCK), lambda n: (0, n)),
    pl.BlockSpec((N_DEV, 2, None, 32, RNN_BLOCK), lambda n: (0, 0, n, 0, 0)),
    pl.BlockSpec((1, RNN_BLOCK), lambda n: (0, n)),
    pl.BlockSpec((1, RNN_BLOCK), lambda n: (0, n)),
    pl.BlockSpec((1, RNN_BLOCK), lambda n: (0, n)),
]


def _rnn_fwd(z, smallw, conv_b, wrg, b_ra, b_ri, lam):
    rows = z.shape[0]
    nb = rows // BLK
    col = lambda off: pl.BlockSpec((rows, RNN_BLOCK), lambda n: (0, off // RNN_BLOCK + n))

    def body(xr_ref, gr_ref, sw_ref, cb_ref, w_ref, bra_ref, bri_ref, lam_ref, xc_ref, hr_ref, ya_ref, yat_ref, a_s):
        cw = sw_ref[0, N_META:24, :]
        cb = cb_ref[...]
        w_ra = w_ref[:, 0].reshape(RNN_BLOCK, RNN_BLOCK)
        w_ri = w_ref[:, 1].reshape(RNN_BLOCK, RNN_BLOCK)
        b_ra_v, b_ri_v = bra_ref[...], bri_ref[...]
        ls = _log_sigmoid(lam_ref[...])
        rid = lax.broadcasted_iota(jnp.int32, (BLK, 1), 0)

        def blk_step(i, carry):
            r0 = pl.multiple_of(i * BLK, BLK)
            grow = rid + r0
            valid = grow >= ROW0
            cur = jnp.where(valid, xr_ref[pl.ds(r0, BLK), :], 0.0)
            prev8 = xr_ref[pl.ds(pl.multiple_of(jnp.maximum(r0 - 8, 0), 8), 8), :] * (i > 0).astype(f32)
            xc = cb + cw[0:1] * cur
            for k in range(1, CONV_WIDTH):
                xc = xc + cw[k:k + 1] * _shift_rows(prev8, cur, k)
            xc_ref[pl.ds(r0, BLK), :] = xc
            _, _, ig, a, mult = _gates(xc, w_ra, b_ra_v, w_ri, b_ri_v, ls)
            mult = jnp.where(grow == ROW0, 1.0, mult)
            a_s[pl.ds(r0, BLK), :] = a
            hr_ref[pl.ds(r0, BLK), :] = jnp.where(valid, mult * ig * xc, 0.0)
            return carry

        lax.fori_loop(0, nb, blk_step, 0)

        def scan_step(j, carry):
            r0 = pl.multiple_of(j * SCAN_ROWS, SCAN_ROWS)
            tiles = [_scan8(a_s[pl.ds(r0 + 8 * k, 8), :], hr_ref[pl.ds(r0 + 8 * k, 8), :], False)
                     for k in range(SCAN_ROWS // 8)]
            for k, (a, b) in enumerate(tiles):
                h = b + a * carry
                hr_ref[pl.ds(r0 + 8 * k, 8), :] = h
                carry = jnp.broadcast_to(h[7:8, :], (8, RNN_BLOCK))
            return carry

        lax.fori_loop(0, rows // SCAN_ROWS, scan_step, jnp.zeros((8, RNN_BLOCK), f32))

        def gate_step(i, carry):
            r0 = pl.multiple_of(i * BLK, BLK)
            ya_ref[pl.ds(r0, BLK), :] = (hr_ref[pl.ds(r0, BLK), :]
                                         * _silu_and_grad(gr_ref[pl.ds(r0, BLK), :])[0]).astype(bf16)
            return carry

        lax.fori_loop(0, nb, gate_step, 0)
        yat_ref[...] = ya_ref[...].astype(f32).T.astype(bf16)

    return pl.pallas_call(
        body, grid=(N_RNN_BLOCKS,),
        in_specs=[col(0), col(OFF_GR)] + _RNN_IN_SPECS(rows),
        out_specs=[pl.BlockSpec((rows, RNN_BLOCK), lambda n: (0, n))] * 3
                  + [pl.BlockSpec((RNN_BLOCK, rows), lambda n: (n, 0))],
        out_shape=[SDS((rows, D), f32), SDS((rows, D), f32), SDS((rows, D), bf16), SDS((D, rows), bf16)],
        scratch_shapes=[pltpu.VMEM((rows, RNN_BLOCK), f32)],
        name="rnn_fwd", compiler_params=_cp(("arbitrary",)),
    )(z, z, smallw, conv_b, wrg, b_ra, b_ri, lam)


def _rnn_bwd(dya, hr, xc, z, smallw, conv_b, wrg, b_ra, b_ri, lam):
    rows = z.shape[0]
    nb = rows // BLK
    col = lambda off: pl.BlockSpec((rows, RNN_BLOCK), lambda n: (0, off // RNN_BLOCK + n))
    blk = pl.BlockSpec((rows, RNN_BLOCK), lambda n: (0, n))

    def body(dya_ref, hr_ref, xc_ref, xr_ref, gr_ref, sw_ref, cb_ref, w_ref, bra_ref, bri_ref, lam_ref,
             dxr_ref, dgr_ref, dw_ref, vec_ref, a_s, lam_s, dxc_s, r_s, ig_s, mult_s, dw_s):
        cw = sw_ref[0, N_META:24, :]
        w_ra = w_ref[:, 0].reshape(RNN_BLOCK, RNN_BLOCK)
        w_ri = w_ref[:, 1].reshape(RNN_BLOCK, RNN_BLOCK)
        b_ra_v, b_ri_v = bra_ref[...], bri_ref[...]
        lam_v = lam_ref[...]
        ls = _log_sigmoid(lam_v)
        rid = lax.broadcasted_iota(jnp.int32, (BLK, 1), 0)
        zrow = jnp.zeros((1, RNN_BLOCK), f32)

        def p1(i, carry):
            r0 = pl.multiple_of(i * BLK, BLK)
            sl = pl.ds(r0, BLK)
            _, r, ig, a, mult = _gates(xc_ref[sl, :], w_ra, b_ra_v, w_ri, b_ri_v, ls)
            a_s[sl, :] = a
            r_s[sl, :] = r
            ig_s[sl, :] = ig
            mult_s[sl, :] = mult
            sg, dsg = _silu_and_grad(gr_ref[sl, :])
            d = dya_ref[sl, :]
            lam_s[sl, :] = d * sg
            dgr_ref[sl, :] = (d * hr_ref[sl, :] * dsg).astype(bf16)
            return carry

        lax.fori_loop(0, nb, p1, 0)

        def p2(jj, carry):
            r0 = pl.multiple_of((rows // SCAN_ROWS - 1 - jj) * SCAN_ROWS, SCAN_ROWS)
            idx = lax.broadcasted_iota(jnp.int32, (8, RNN_BLOCK), 0)
            tiles = []
            for k in range(SCAN_ROWS // 8):
                sl = pl.ds(r0 + 8 * k, 8)
                a, g = a_s[sl, :], lam_s[sl, :]
                tiles.append((g, *_scan8(a, a * g, True)))
            for k in reversed(range(SCAN_ROWS // 8)):
                g, ca, cb_ = tiles[k]
                mu = cb_ + ca * carry
                lam_s[pl.ds(r0 + 8 * k, 8), :] = g + jnp.where(idx < 7, pltpu.roll(mu, 7, 0), carry)
                carry = jnp.broadcast_to(mu[0:1, :], (8, RNN_BLOCK))
            return carry

        lax.fori_loop(0, rows // SCAN_ROWS, p2, jnp.zeros((8, RNN_BLOCK), f32))

        dw_s[...] = jnp.zeros_like(dw_s)

        def p3(i, carry):
            d_bra, d_bri, d_ls = carry
            r0 = pl.multiple_of(i * BLK, BLK)
            sl = pl.ds(r0, BLK)
            grow = rid + r0
            valid = grow >= ROW0
            first = grow == ROW0
            xcv = xc_ref[sl, :]
            xb = xcv.astype(bf16)
            r, ig, a = r_s[sl, :], ig_s[sl, :], a_s[sl, :]
            mult = jnp.where(first, 1.0, mult_s[sl, :])
            lam_t = lam_s[sl, :]
            du = jnp.where(valid, lam_t, 0.0)
            hprev = _shift_rows(hr_ref[pl.ds(pl.multiple_of(jnp.maximum(r0 - 8, 0), 8), 8), :] * (i > 0).astype(f32), hr_ref[sl, :], 1)
            da = lam_t * hprev
            dmult = jnp.where(first, 0.0, du * ig * xcv)
            di = du * mult * xcv
            dxc = du * mult * ig
            ratio = jnp.where(valid & jnp.logical_not(first), a * a / mult, 0.0)
            dla = da * a - dmult * ratio
            dpr = (dla * (LRU_C * ls)) * r * (1.0 - r)
            dpi = di * ig * (1.0 - ig)
            dprb, dpib = dpr.astype(bf16), dpi.astype(bf16)
            dw_s[0] += _dot_tn(xb, dprb)
            dw_s[1] += _dot_tn(xb, dpib)
            dxc_s[sl, :] = dxc + _dot_nt(dprb, w_ra) + _dot_nt(dpib, w_ri)
            return d_bra + _colsum(dpr), d_bri + _colsum(dpi), d_ls + _colsum(dla * (LRU_C * r))

        d_bra, d_bri, d_ls = lax.fori_loop(0, nb, p3, (zrow, zrow, zrow))

        def p4(i, carry):
            d_cb, d_w0, d_w1, d_w2, d_w3 = carry
            r0 = pl.multiple_of(i * BLK, BLK)
            sl = pl.ds(r0, BLK)
            grow = rid + r0
            valid = grow >= ROW0
            dxc = dxc_s[sl, :]
            nxt = dxc_s[pl.ds(pl.multiple_of(jnp.minimum(r0 + BLK, rows - 8), 8), 8), :] * (i < nb - 1).astype(f32)
            ext = jnp.concatenate([dxc, nxt], axis=0)
            dxr = cw[0:1] * dxc
            for k in range(1, CONV_WIDTH):
                dxr = dxr + cw[k:k + 1] * pltpu.roll(ext, BLK + 8 - k, 0)[:BLK, :]
            dxr_ref[sl, :] = jnp.where(valid, dxr, 0.0).astype(bf16)
            cur = jnp.where(valid, xr_ref[sl, :], 0.0)
            prev8 = xr_ref[pl.ds(pl.multiple_of(jnp.maximum(r0 - 8, 0), 8), 8), :] * (i > 0).astype(f32)
            dws = [d_w0 + _colsum(dxc * cur)]
            for k, acc in ((1, d_w1), (2, d_w2), (3, d_w3)):
                dws.append(acc + _colsum(dxc * _shift_rows(prev8, cur, k)))
            return (d_cb + _colsum(dxc), *dws)

        d_cb, d_w0, d_w1, d_w2, d_w3 = lax.fori_loop(0, nb, p4, (zrow,) * 5)

        d_lam = d_ls * _sigmoid(-lam_v)
        vec_ref[...] = jnp.concatenate([d_bra, d_bri, d_lam, d_cb, d_w0, d_w1, d_w2, d_w3], axis=0)
        dw_ref[:, 0] = dw_s[0].astype(bf16).reshape(N_DEV, 32, RNN_BLOCK)
        dw_ref[:, 1] = dw_s[1].astype(bf16).reshape(N_DEV, 32, RNN_BLOCK)

    return pl.pallas_call(
        body, grid=(N_RNN_BLOCKS,),
        in_specs=[blk, blk, blk, col(0), col(OFF_GR)] + _RNN_IN_SPECS(rows),
        out_specs=[blk, blk,
                   pl.BlockSpec((N_DEV, 2, None, 32, RNN_BLOCK), lambda n: (0, 0, n, 0, 0)),
                   pl.BlockSpec((8, RNN_BLOCK), lambda n: (0, n))],
        out_shape=[SDS((rows, D), bf16), SDS((rows, D), bf16),
                   SDS((N_DEV, 2, N_RNN_BLOCKS, 32, RNN_BLOCK), bf16), SDS((8, D), f32)],
        scratch_shapes=[pltpu.VMEM((rows, RNN_BLOCK), f32)] * 6 + [pltpu.VMEM((2, RNN_BLOCK, RNN_BLOCK), f32)],
        name="rnn_bwd", compiler_params=_cp(("arbitrary",), 48),
    )(dya, hr, xc, z, z, smallw, conv_b, wrg, b_ra, b_ri, lam)


def _rope_tables(rows):
    half = jnp.arange(HALF, dtype=f32)
    inv = ROPE_THETA ** (-half / HALF)
    pos = (jnp.arange(rows) - ROW0).astype(f32)
    ang = pos[:, None] * inv[None, :]
    cos, sin = jnp.cos(ang), jnp.sin(ang)
    cos128 = jnp.concatenate([cos, cos, cos, cos], axis=1)
    sin128 = jnp.concatenate([-sin, sin, -sin, sin], axis=1)
    return cos128, sin128


def _rope128(x, cos128, sin128):
    lane = lax.broadcasted_iota(jnp.int32, x.shape, 1)
    swapped = jnp.where(lane % HEAD_DIM < HALF, pltpu.roll(x, 128 - HALF, 1), pltpu.roll(x, HALF, 1))
    return x * cos128 + swapped * sin128


def _qkv_prep(z, cos128, sin128):
    rows = z.shape[0]

    def body(q_ref, kv_ref, c_ref, s_ref, qo_ref, ko_ref, vo_ref):
        c, s = c_ref[...], s_ref[...]
        for g in range(D // 128):
            qo_ref[:, g * 128:(g + 1) * 128] = (_rope128(q_ref[:, g * 128:(g + 1) * 128], c, s)
                                                * (HEAD_DIM ** -0.5)).astype(bf16)
        for g in range(2):
            kr = _rope128(kv_ref[:, g * 128:(g + 1) * 128], c, s)
            for j in range(2):
                ko_ref[2 * g + j] = kr[:, j * HEAD_DIM:(j + 1) * HEAD_DIM].astype(bf16)
        for h in range(N_KV):
            vo_ref[h] = kv_ref[:, 256 + h * HEAD_DIM:256 + (h + 1) * HEAD_DIM].astype(bf16)

    return pl.pallas_call(
        body, grid=(rows // BLK,),
        in_specs=[pl.BlockSpec((BLK, D), lambda i: (i, OFF_Q // D)),
                  pl.BlockSpec((BLK, 512), lambda i: (i, OFF_K // 512)),
                  pl.BlockSpec((BLK, 128), lambda i: (i, 0)),
                  pl.BlockSpec((BLK, 128), lambda i: (i, 0))],
        out_specs=[pl.BlockSpec((BLK, D), lambda i: (i, 0)),
                   pl.BlockSpec((N_KV, BLK, HEAD_DIM), lambda i: (0, i, 0)),
                   pl.BlockSpec((N_KV, BLK, HEAD_DIM), lambda i: (0, i, 0))],
        out_shape=[SDS((rows, D), bf16), SDS((N_KV, rows, HEAD_DIM), bf16), SDS((N_KV, rows, HEAD_DIM), bf16)],
        name="qkv_prep", compiler_params=_cp(("arbitrary",)),
    )(z, z, cos128, sin128)


def _attn_mask(n):
    qi = n * BLK + lax.broadcasted_iota(jnp.int32, (BLK, 2 * BLK + N_META), 0)
    c = lax.broadcasted_iota(jnp.int32, (BLK, 2 * BLK + N_META), 1)
    jb = (n - 1) * BLK + c
    band = (jb >= BLK) & (jb <= qi) & (qi - jb < BLK)
    meta = (ROW0 + c - 2 * BLK) <= qi
    return ((c < 2 * BLK) & band) | ((c >= 2 * BLK) & meta)


N_KEYS = 2 * BLK + N_META


def _stack_heads(t):
    return jnp.concatenate([t[:, g * HEAD_DIM:(g + 1) * HEAD_DIM] for g in range(GROUP)], axis=0)


def _sink_column(sink_ref, h):
    g = lax.broadcasted_iota(jnp.int32, (GROUP, 1, 1), 0)
    col = jnp.zeros((GROUP, 1, 1), f32)
    for j in range(GROUP):
        col = jnp.where(g == j, sink_ref[h * GROUP + j], col)
    return col


def _kv_specs(last):
    cl = lambda n: jnp.minimum(n, last)
    return [pl.BlockSpec((None, N_META, HEAD_DIM), lambda h, n: (h, ROW0 // N_META, 0)),
            pl.BlockSpec((None, BLK, HEAD_DIM), lambda h, n: (h, jnp.maximum(cl(n) - 1, 0), 0)),
            pl.BlockSpec((None, BLK, HEAD_DIM), lambda h, n: (h, cl(n), 0))]


def _attn_fwd(q_r, k_r, v_b, z, sinks):
    rows = q_r.shape[0]
    nb = rows // BLK

    def body(sink_ref, q_ref, km_ref, kp_ref, kc_ref, vm_ref, vp_ref, vc_ref, ga_ref, o_ref, yb_ref, ybt_ref, lse_ref):
        h, n = pl.program_id(0), pl.program_id(1)
        kk = jnp.concatenate([kp_ref[...], kc_ref[...], km_ref[...]], axis=0)
        vv = jnp.concatenate([vp_ref[...], vc_ref[...], vm_ref[...]], axis=0)
        q2 = _stack_heads(q_ref[...])
        s = jnp.where(_attn_mask(n)[None], _dot_nt(q2, kk).reshape(GROUP, BLK, N_KEYS), NEG_INF)
        sink = _sink_column(sink_ref, h)
        m = jnp.maximum(jnp.max(s, axis=-1, keepdims=True), sink)
        p = jnp.exp(s - m)
        den = jnp.sum(p, axis=-1, keepdims=True) + jnp.exp(sink - m)
        o2 = _dot((p / den).astype(bf16).reshape(GROUP * BLK, N_KEYS), vv)
        lse = m + jnp.log(den)
        for g in range(GROUP):
            o_ref[:, g * HEAD_DIM:(g + 1) * HEAD_DIM] = o2[g * BLK:(g + 1) * BLK]
            lse_ref[:, g:g + 1] = lse[g]
        yb = o_ref[...] * _silu_and_grad(ga_ref[...])[0]
        yb_ref[...] = yb.astype(bf16)
        ybt_ref[...] = yb.T.astype(bf16)

    tile = pl.BlockSpec((BLK, 512), lambda h, n: (n, h))
    return pl.pallas_call(
        body, grid=(N_KV, nb),
        in_specs=[pl.BlockSpec(memory_space=pltpu.SMEM), tile] + _kv_specs(nb - 1) + _kv_specs(nb - 1)
                 + [pl.BlockSpec((BLK, 512), lambda h, n: (n, OFF_GA // 512 + h))],
        out_specs=[tile, tile, pl.BlockSpec((512, BLK), lambda h, n: (h, n)),
                   pl.BlockSpec((None, BLK, GROUP), lambda h, n: (h, n, 0))],
        out_shape=[SDS((rows, D), f32), SDS((rows, D), bf16), SDS((D, rows), bf16),
                   SDS((N_KV, rows, GROUP), f32)],
        name="attn_fwd", compiler_params=_cp(("arbitrary", "arbitrary")),
    )(sinks, q_r, k_r, k_r, k_r, v_b, v_b, v_b, z)


def _attn_bwd(dyb, o32, lse, q_r, k_r, v_b, z, sinks):
    rows = q_r.shape[0]
    nb = rows // BLK
    cl = lambda n: jnp.minimum(n, nb - 1)

    def body(sink_ref, dyb_ref, o_ref, lse_ref, q_ref, km_ref, kp_ref, kc_ref, vm_ref, vp_ref, vc_ref, ga_ref,
             dq_ref, dga_ref, dk_ref, dv_ref, dkm_ref, dvm_ref, dsr_ref, ck_s, cv_s):
        h, n = pl.program_id(0), pl.program_id(1)

        @pl.when(n == 0)
        def _():
            dkm_ref[...] = jnp.zeros_like(dkm_ref)
            dvm_ref[...] = jnp.zeros_like(dvm_ref)
            ck_s[...] = jnp.zeros_like(ck_s)
            cv_s[...] = jnp.zeros_like(cv_s)

        @pl.when(n < nb)
        def _():
            kk = jnp.concatenate([kp_ref[...], kc_ref[...], km_ref[...]], axis=0)
            vv = jnp.concatenate([vp_ref[...], vc_ref[...], vm_ref[...]], axis=0)
            sg, dsg = _silu_and_grad(ga_ref[...])
            dyb_v = dyb_ref[...]
            o_v = o_ref[...]
            dga_ref[...] = (dyb_v * o_v * dsg).astype(bf16)
            q2 = _stack_heads(q_ref[...])
            do2 = _stack_heads(dyb_v * sg)
            lse_v = lse_ref[...]
            lse = jnp.concatenate([lse_v[:, g:g + 1] for g in range(GROUP)], axis=0).reshape(GROUP, BLK, 1)
            delta = jnp.sum(do2 * _stack_heads(o_v), axis=-1, keepdims=True).reshape(GROUP, BLK, 1)
            s = jnp.where(_attn_mask(n)[None], _dot_nt(q2, kk).reshape(GROUP, BLK, N_KEYS), NEG_INF)
            p = jnp.exp(s - lse)
            do2b = do2.astype(bf16)
            ds = (p * (_dot_nt(do2b, vv).reshape(GROUP, BLK, N_KEYS) - delta)).astype(bf16)
            ds = ds.reshape(GROUP * BLK, N_KEYS)
            dsr = -jnp.exp(_sink_column(sink_ref, h) - lse) * delta
            dq2 = _dot(ds, kk)
            for g in range(GROUP):
                dq_ref[:, g * HEAD_DIM:(g + 1) * HEAD_DIM] = dq2[g * BLK:(g + 1) * BLK]
                dsr_ref[:, g:g + 1] = dsr[g]
            dkk = _dot_tn(ds, q2)
            dvv = _dot_tn(p.astype(bf16).reshape(GROUP * BLK, N_KEYS), do2b)
            dk_ref[...] = ck_s[...] + dkk[:BLK]
            dv_ref[...] = cv_s[...] + dvv[:BLK]
            ck_s[...] = dkk[BLK:2 * BLK]
            cv_s[...] = dvv[BLK:2 * BLK]
            dkm_ref[...] += dkk[2 * BLK:]
            dvm_ref[...] += dvv[2 * BLK:]

        @pl.when(n == nb)
        def _():
            dk_ref[...] = ck_s[...]
            dv_ref[...] = cv_s[...]

    tile = pl.BlockSpec((BLK, 512), lambda h, n: (cl(n), h))
    kvout = pl.BlockSpec((None, BLK, HEAD_DIM), lambda h, n: (h, jnp.maximum(n - 1, 0), 0))
    mout = pl.BlockSpec((None, N_META, HEAD_DIM), lambda h, n: (h, 0, 0))
    stat = pl.BlockSpec((None, BLK, GROUP), lambda h, n: (h, cl(n), 0))
    return pl.pallas_call(
        body, grid=(N_KV, nb + 1),
        in_specs=[pl.BlockSpec(memory_space=pltpu.SMEM), tile, tile, stat, tile] + _kv_specs(nb - 1)
                 + _kv_specs(nb - 1) + [pl.BlockSpec((BLK, 512), lambda h, n: (cl(n), OFF_GA // 512 + h))],
        out_specs=[tile, tile, kvout, kvout, mout, mout, stat],
        out_shape=[SDS((rows, D), f32), SDS((rows, D), bf16),
                   SDS((N_KV, rows, HEAD_DIM), f32), SDS((N_KV, rows, HEAD_DIM), f32),
                   SDS((N_KV, N_META, HEAD_DIM), f32), SDS((N_KV, N_META, HEAD_DIM), f32),
                   SDS((N_KV, rows, GROUP), f32)],
        scratch_shapes=[pltpu.VMEM((BLK, HEAD_DIM), f32), pltpu.VMEM((BLK, HEAD_DIM), f32)],
        name="attn_bwd", compiler_params=_cp(("arbitrary", "arbitrary")),
    )(sinks, dyb, o32, lse, q_r, k_r, k_r, k_r, v_b, v_b, v_b, z)


def _qkv_finish(dq, dk, dv, dkm, dvm, cos128, sin128):
    rows = dq.shape[0]

    def body(dq_ref, dk_ref, dv_ref, dkm_ref, dvm_ref, c_ref, s_ref, oq_ref, okv_ref):
        first = (pl.program_id(0) == 0).astype(f32)
        c, s = c_ref[...], -s_ref[...]
        for g in range(D // 128):
            oq_ref[:, g * 128:(g + 1) * 128] = (_rope128(dq_ref[:, g * 128:(g + 1) * 128], c, s)
                                                * (HEAD_DIM ** -0.5)).astype(bf16)
        pad = jnp.zeros((ROW0, HEAD_DIM), f32)
        ks = [dk_ref[h] + first * jnp.concatenate([pad, dkm_ref[h]], axis=0) for h in range(N_KV)]
        vs = [dv_ref[h] + first * jnp.concatenate([pad, dvm_ref[h]], axis=0) for h in range(N_KV)]
        for g in range(2):
            kp = jnp.concatenate([ks[2 * g], ks[2 * g + 1]], axis=1)
            okv_ref[:, g * 128:(g + 1) * 128] = _rope128(kp, c, s).astype(bf16)
            okv_ref[:, 256 + g * 128:256 + (g + 1) * 128] = jnp.concatenate([vs[2 * g], vs[2 * g + 1]], axis=1).astype(bf16)

    kv = pl.BlockSpec((N_KV, BLK, HEAD_DIM), lambda i: (0, i, 0))
    mt = pl.BlockSpec((N_KV, N_META, HEAD_DIM), lambda i: (0, 0, 0))
    return pl.pallas_call(
        body, grid=(rows // BLK,),
        in_specs=[pl.BlockSpec((BLK, D), lambda i: (i, 0)), kv, kv, mt, mt,
                  pl.BlockSpec((BLK, 128), lambda i: (i, 0)), pl.BlockSpec((BLK, 128), lambda i: (i, 0))],
        out_specs=[pl.BlockSpec((BLK, D), lambda i: (i, 0)), pl.BlockSpec((BLK, 512), lambda i: (i, 0))],
        out_shape=[SDS((rows, D), bf16), SDS((rows, 512), bf16)],
        name="qkv_finish", compiler_params=_cp(("arbitrary",)),
    )(dq, dk, dv, dkm, dvm, cos128, sin128)


_TW = 512


def _mix_specs(rows):
    tr = _row_chunk(rows)
    tile = pl.BlockSpec((tr, _TW), lambda i, j: (i, j))
    ga = pl.BlockSpec((tr, _TW), lambda i, j: (i, OFF_G // _TW + j))
    gb = pl.BlockSpec((tr, _TW), lambda i, j: (i, (OFF_G + D) // _TW + j))
    return (rows // tr, D // _TW), tile, ga, gb


def _mix_fwd(y_a, y_b, z):
    rows = y_a.shape[0]
    tw = 256
    col = lambda off: pl.BlockSpec((rows, tw), lambda j: (0, off // tw + j))

    def body(ya_ref, yb_ref, ga_ref, gb_ref, o_ref, ot_ref):
        mixed = (_sigmoid(ga_ref[...]) * ya_ref[...].astype(f32)
                 + _sigmoid(gb_ref[...]) * yb_ref[...].astype(f32))
        o_ref[...] = mixed.astype(bf16)
        ot_ref[...] = mixed.T.astype(bf16)

    return pl.pallas_call(
        body, grid=(D // tw,), in_specs=[col(0), col(0), col(OFF_G), col(OFF_G + D)],
        out_specs=[col(0), pl.BlockSpec((tw, rows), lambda j: (j, 0))],
        out_shape=[SDS((rows, D), bf16), SDS((D, rows), bf16)],
        name="mix_fwd", compiler_params=_cp(("arbitrary",)),
    )(y_a, y_b, z, z)


def _mix_bwd(dmixed, y_a, y_b, z):
    rows = y_a.shape[0]
    grid, _mix_tile, _mix_ga, _mix_gb = _mix_specs(rows)

    def body(dm_ref, ya_ref, yb_ref, ga_ref, gb_ref, dya_ref, dyb_ref, dga_ref, dgb_ref):
        dm = dm_ref[...].astype(f32)
        sa, sb = _sigmoid(ga_ref[...]), _sigmoid(gb_ref[...])
        dya_ref[...] = (dm * sa).astype(bf16)
        dyb_ref[...] = (dm * sb).astype(bf16)
        dga_ref[...] = (dm * ya_ref[...].astype(f32) * sa * (1.0 - sa)).astype(bf16)
        dgb_ref[...] = (dm * yb_ref[...].astype(f32) * sb * (1.0 - sb)).astype(bf16)

    return pl.pallas_call(
        body, grid=grid, in_specs=[_mix_tile, _mix_tile, _mix_tile, _mix_ga, _mix_gb],
        out_specs=[_mix_tile] * 4, out_shape=[SDS((rows, D), bf16)] * 4,
        name="mix_bwd", compiler_params=_cp(("arbitrary", "arbitrary")),
    )(dmixed, y_a, y_b, z, z)


def _final_ln(out32, h32, tgt, ln_g, ln_b):
    rows = out32.shape[0]

    def body(o_ref, h_ref, t_ref, g_ref, b_ref, du_ref, dub_ref, st_ref):
        i = pl.program_id(0)
        g = g_ref[...]
        y, xhat, rstd = _ln_rows(ALPHA * h_ref[...] + o_ref[...], g, b_ref[...])
        e = jnp.where(i > 0, y - t_ref[0], 0.0)
        dy = e * (1.0 / D)
        du = _ln_rows_bwd(dy, g, xhat, rstd)
        du_ref[...] = du
        dub_ref[...] = du.astype(bf16)
        st = jnp.concatenate([_colsum(dy * xhat), _colsum(dy), _colsum(du), _colsum(e * e) * (0.5 / D),
                              jnp.zeros((4, D), f32)], axis=0)

        @pl.when(i == 0)
        def _():
            st_ref[...] = st

        @pl.when(i > 0)
        def _():
            st_ref[...] += st

    row = pl.BlockSpec((BLK, D), lambda i: (i, 0))
    vec = pl.BlockSpec((1, D), lambda i: (0, 0))
    return pl.pallas_call(
        body, grid=(rows // BLK,),
        in_specs=[row, row, pl.BlockSpec((1, BLK, D), lambda i: (0, jnp.maximum(i - 1, 0), 0)), vec, vec],
        out_specs=[row, row, pl.BlockSpec((8, D), lambda i: (0, 0))],
        out_shape=[SDS((rows, D), f32), SDS((rows, D), bf16), SDS((8, D), f32)],
        name="final_ln", compiler_params=_cp(("arbitrary",)),
    )(out32, h32, tgt, ln_g, ln_b)


def _step_rnn(h32, hb, z, wrg, smallw, p, zero):
    rows = z.shape[0]
    cos128, sin128 = _rope_tables(rows)
    cos128 = cos128 + zero
    xc, hr, ya, ya_t = _rnn_fwd(z, smallw, p["conv_b"] + zero, wrg, p["b_ra"], p["b_ri"], p["lru_lambda"])
    q_r, k_r, v_b = _qkv_prep(z, cos128, sin128)
    return dict(cos128=cos128, sin128=sin128, h32=h32, hb=hb, z=z, xc=xc, hr=hr, ya=ya, ya_t=ya_t,
                q_r=q_r, k_r=k_r, v_b=v_b)


def _step_attn(s, p, zero):
    sinks = p["sinks"].reshape(N_KV * GROUP) + zero[0]
    o32, yb, yb_t, lse = _attn_fwd(s["q_r"], s["k_r"], s["v_b"], s["z"], sinks)
    return dict(s, sinks=sinks, o32=o32, yb=yb, yb_t=yb_t, lse=lse)


def _step_merge(s, tgt, w3, p):
    ya, yb, z = s["ya"], s["yb"], s["z"]
    y_a = _mm(ya, w3, sel=0, out_dtype=bf16, name="mm_ya")
    y_b = _mm(yb, w3, sel=1, out_dtype=bf16, name="mm_yb")
    mixed, mixed_t = _mix_fwd(y_a, y_b, z)
    out32 = _mm(mixed, w3, sel=2, bias=p["b_o"], name="mm_out")
    du32, dub, st_out = _final_ln(out32, s["h32"], tgt, p["ln_g"], p["ln_b"])

    g_wo = _mm(mixed_t, dub, out_dtype=bf16, name="mm_dwo")
    dmixed = _mm(dub, w3, sel=2, nt=True, out_dtype=bf16, name="mm_dmixed")
    dya_b, dyb_b, dma, dmb = _mix_bwd(dmixed, y_a, y_b, z)
    g_wrnn = _mm(s["ya_t"], dya_b, out_dtype=bf16, name="mm_dwrnn")
    g_wattn = _mm(s["yb_t"], dyb_b, out_dtype=bf16, name="mm_dwattn")
    dya = _mm(dya_b, w3, sel=0, nt=True, name="mm_dya")
    dyb = _mm(dyb_b, w3, sel=1, nt=True, name="mm_dyb")
    return dict(du32=du32, st_out=st_out, dma=dma, dmb=dmb, dya=dya, dyb=dyb, g_wo=g_wo, g_wrnn=g_wrnn,
                g_wattn=g_wattn)


def _step_backward(s, t, wrg, smallw, p, conv_b):
    z = s["z"]
    dxr, dgr, g_wrg, vec_rnn = _rnn_bwd(t["dya"], s["hr"], s["xc"], z, smallw, conv_b, wrg, p["b_ra"], p["b_ri"],
                                        p["lru_lambda"])
    dq_r, dga, dk, dv, dkm, dvm, dsr = _attn_bwd(t["dyb"], s["o32"], s["lse"], s["q_r"], s["k_r"], s["v_b"], z,
                                                 s["sinks"])
    dq, dkv = _qkv_finish(dq_r, dk, dv, dkm, dvm, s["cos128"], s["sin128"])
    dz_parts = [(dxr, D), (dgr, D), (dq, D), (dkv, 512), (dga, D), (t["dma"], D), (t["dmb"], D)]
    return dict(vec_rnn=vec_rnn, dsr=dsr, g_wrg=g_wrg, dz_parts=dz_parts)


def _step_input_grad(dz, w_t, after, du32, x, smallw, p):
    dh = _mm_dh(dz, w_t, after)
    grad_x, dmeta, st_emb = _ln_emb_bwd(dh, du32, x, smallw, p["ln_emb_g"])
    return dict(grad_x=grad_x, dmeta=dmeta, st_emb=st_emb)


_ANY = pl.BlockSpec(memory_space=pl.ANY)
_VMEM = pl.BlockSpec(memory_space=pltpu.VMEM)


def _place():
    x, y, c = lax.axis_index("x"), lax.axis_index("y"), lax.axis_index("c")
    return x, y, c


def _dev(px, py, pc):
    return 4 * px + 2 * py + pc


def _tile_rows(r):
    return max(t for t in range(16, 321, 16) if r % t == 0) if r > 320 else r


def _cast_w_in(w_in_t):
    tm = _tile_rows(SHARD_IN)

    def body(i_ref, o_ref):
        o_ref[...] = i_ref[...].astype(bf16)

    return pl.pallas_call(
        body, grid=(SHARD_IN // tm,),
        in_specs=[pl.BlockSpec((tm, D), lambda i: (i, 0))],
        out_specs=pl.BlockSpec((tm, D), lambda i: (i, 0)),
        out_shape=SDS((SHARD_IN, D), bf16), name="cast_w_in", compiler_params=_cp(("arbitrary",)),
    )(w_in_t)


def _cast_small(w_rnn_out, w_attn_out, w_o, w_ra, w_ri, meta, conv_w):
    def body(a_ref, b_ref, c_ref, ra_ref, ri_ref, m_ref, cw_ref, w3_ref, wrg_ref, sw_ref):
        w3_ref[0] = a_ref[0].astype(bf16)
        w3_ref[1] = b_ref[0].astype(bf16)
        w3_ref[2] = c_ref[0].astype(bf16)
        wrg_ref[0] = ra_ref[0].astype(bf16)
        wrg_ref[1] = ri_ref[0].astype(bf16)
        sw_ref[...] = jnp.concatenate([m_ref[...], cw_ref[0], jnp.zeros((4, 256), f32)], axis=0)

    return pl.pallas_call(
        body,
        out_shape=[SDS((3, 256, D), bf16), SDS((2, N_RNN_BLOCKS, 32, RNN_BLOCK), bf16), SDS((24, 256), f32)],
        name="cast_small", compiler_params=_cp(None),
    )(w_rnn_out, w_attn_out, w_o, w_ra, w_ri, meta, conv_w)


def _all_gather(shards, later):
    n = len(shards)
    nl = len(later)

    def body(*refs):
        ins, outs = refs[:n], refs[n + nl:2 * n + nl]
        send_sems, recv_sems, local_sems = refs[2 * (n + nl):]
        x, y, c = _place()
        me, sibling = (x, y, c), (x, y, 1 - c)
        chips = [(1 - x, y), (x, 1 - y), (1 - x, 1 - y)]

        def copy(a, k, block, to, src=None):
            dst = outs[a].at[_dev(*block)]
            return pltpu.make_async_remote_copy(
                src_ref=dst if src is None else src, dst_ref=dst,
                send_sem=send_sems.at[a * 7 + k], recv_sem=recv_sems.at[a * 7 + k],
                device_id=to, device_id_type=MESH)

        all_ins, all_outs = refs[:n + nl], refs[n + nl:2 * (n + nl)]
        mine = [pltpu.make_async_copy(all_ins[a], all_outs[a].at[_dev(*me)], local_sems.at[a]) for a in range(n + nl)]
        for cp in mine:
            cp.start()
        first = []
        for a in range(n):
            first.append(copy(a, 0, me, sibling, src=ins[a]))
            first += [copy(a, 1 + j, me, (*chip, c), src=ins[a]) for j, chip in enumerate(chips)]
        for cp in first:
            cp.start()
        passed = []
        for a in range(n):
            for j, chip in enumerate(chips):
                copy(a, 1 + j, (*chip, c), me).wait_recv()
                cp = copy(a, 4 + j, (*chip, c), sibling)
                cp.start()
                passed.append(cp)
        for a in range(n):
            copy(a, 0, sibling, me).wait_recv()
            for j, chip in enumerate(chips):
                copy(a, 4 + j, (*chip, 1 - c), me).wait_recv()
        for cp in first + passed:
            cp.wait_send()
        for cp in mine:
            cp.wait()

    return pl.pallas_call(
        body, in_specs=[_ANY] * (n + nl), out_specs=[_ANY] * (n + nl),
        out_shape=[SDS((N_DEV, *s.shape), s.dtype) for s in (*shards, *later)],
        scratch_shapes=[pltpu.SemaphoreType.DMA((7 * n,)), pltpu.SemaphoreType.DMA((7 * n,)),
                        pltpu.SemaphoreType.DMA((n + nl,))],
        name="all_gather_weights",
    )(*shards, *later)


def _gather_small(shard):
    def body(s_ref, o_ref, send_sems, recv_sems):
        x, y, c = _place()
        me = _dev(x, y, c)
        copies = []
        for k, (fx, fy, fc) in enumerate(_PEER_FLIPS):
            peer = ((x + fx) % 2, (y + fy) % 2, (c + fc) % 2)
            copies.append(_remote(s_ref, o_ref.at[me], send_sems, recv_sems, k, peer))
        for cp in copies:
            cp.start()
        o_ref[me] = s_ref[...]
        for cp in copies:
            cp.wait()

    return pl.pallas_call(
        body, in_specs=[_VMEM], out_specs=_VMEM, out_shape=SDS((N_DEV, *shard.shape), shard.dtype),
        scratch_shapes=[pltpu.SemaphoreType.DMA((7,)), pltpu.SemaphoreType.DMA((7,))],
        name="gather_small",
    )(shard)


def _gather_project(w_s, smalls, later, hb, b_in, order):
    arrays = (w_s, *smalls, *later)
    na, n = len(arrays), 1 + len(smalls)
    rows = hb.shape[0]
    cm = _row_chunk(rows)
    nm = rows // cm
    pair = 2 * SHARD_IN

    def body(order_ref, *refs):
        ins, hb_ref, b_ref = refs[:na], refs[na], refs[na + 1]
        outs, z_ref = refs[na + 2:2 * na + 2], refs[2 * na + 2]
        wbuf, send_sems, recv_sems, local_sems, load_sems = refs[2 * na + 3:]
        k, mi = pl.program_id(0), pl.program_id(1)
        x, y, c = _place()
        me, sibling = (x, y, c), (x, y, 1 - c)
        chips = [(1 - x, y), (x, 1 - y), (1 - x, 1 - y)]

        def copy(a, kk, block, to, src=None):
            dst = outs[a].at[_dev(*block)]
            return pltpu.make_async_remote_copy(
                src_ref=dst if src is None else src, dst_ref=dst,
                send_sem=send_sems.at[a * 7 + kk], recv_sem=recv_sems.at[a * 7 + kk],
                device_id=to, device_id_type=MESH)

        mine = [pltpu.make_async_copy(ins[a], outs[a].at[_dev(*me)], local_sems.at[a]) for a in range(na)]

        def to_sibling():
            return [copy(a, 0, me, sibling, src=ins[a]) for a in range(n)]

        def to_chip(j):
            return [copy(a, 1 + j, me, (*chips[j], c), src=ins[a]) for a in range(n)]

        def load_pair(chip):
            cps = [pltpu.make_async_copy(outs[0].at[_dev(*chip, cc)], wbuf.at[pl.ds(cc * SHARD_IN, SHARD_IN)],
                                         load_sems.at[cc]) for cc in range(2)]
            for cp in cps:
                cp.start()
            for cp in cps:
                cp.wait()

        @pl.when((k == 0) & (mi == 0))
        def _():
            for cp in mine + to_sibling() + to_chip(0) + to_chip(1):
                cp.start()
            mine[0].wait()
            copy(0, 0, sibling, me).wait_recv()
            load_pair((x, y))

        for j, chip in enumerate(chips):
            @pl.when((k == j + 1) & (mi == 0))
            def _():
                for a in range(n):
                    copy(a, 1 + j, (*chip, c), me).wait_recv()
                    copy(a, 4 + j, (*chip, c), sibling).start()
                if j == 0:
                    for cp in to_chip(2):
                        cp.start()
                copy(0, 4 + j, (*chip, 1 - c), me).wait_recv()
                load_pair(chip)

        z_ref[...] = _dot_nt(hb_ref[...], wbuf[...]) + b_ref[...]

        @pl.when((k == len(chips)) & (mi == nm - 1))
        def _():
            for a in range(1, n):
                copy(a, 0, sibling, me).wait_recv()
                for j, chip in enumerate(chips):
                    copy(a, 4 + j, (*chip, 1 - c), me).wait_recv()
            for cp in to_sibling() + to_chip(0) + to_chip(1) + to_chip(2):
                cp.wait_send()
            for a in range(n):
                for j, chip in enumerate(chips):
                    copy(a, 4 + j, (*chip, c), sibling).wait_send()
            for cp in mine[1:]:
                cp.wait()

    res = pl.pallas_call(
        body,
        grid_spec=pltpu.PrefetchScalarGridSpec(
            num_scalar_prefetch=1, grid=(N_DEV // 2, nm),
            in_specs=[_ANY] * na + [pl.BlockSpec((cm, D), lambda k, i, o: (i, 0)),
                                    pl.BlockSpec((1, pair), lambda k, i, o: (0, o[k]))],
            out_specs=[_ANY] * na + [pl.BlockSpec((cm, pair), lambda k, i, o: (i, o[k]))],
            scratch_shapes=[pltpu.VMEM((pair, D), bf16), pltpu.SemaphoreType.DMA((7 * n,)),
                            pltpu.SemaphoreType.DMA((7 * n,)), pltpu.SemaphoreType.DMA((na,)),
                            pltpu.SemaphoreType.DMA((2,))]),
        out_shape=[SDS((N_DEV, *s.shape), s.dtype) for s in arrays] + [SDS((rows, D_IN), f32)],
        name="gather_project", compiler_params=_cp(("arbitrary", "arbitrary"), 48),
    )(order, *arrays, hb, b_in)
    return res[:na], res[na]


_HBM = pl.BlockSpec(memory_space=pltpu.HBM)
_SEM = pl.BlockSpec(memory_space=pltpu.SEMAPHORE)
_PEER_FLIPS = [(f // 4, (f // 2) % 2, f % 2) for f in range(1, N_DEV)]


def _remote(src, dst, send_sems, recv_sems, k, to):
    return pltpu.make_async_remote_copy(src_ref=src, dst_ref=dst, send_sem=send_sems.at[k], recv_sem=recv_sems.at[k],
                                        device_id=to, device_id_type=MESH)


def _copies_direct(same_src):
    def make(srcs, lands, send_sems, recv_sems):
        x, y, c = _place()
        me = _dev(x, y, c)
        out = []
        for a in range(len(srcs)):
            for k, (fx, fy, fc) in enumerate(_PEER_FLIPS):
                peer = ((x + fx) % 2, (y + fy) % 2, (c + fc) % 2)
                src = srcs[a] if same_src else srcs[a].at[_dev(*peer)]
                out.append(_remote(src, lands[a].at[me], send_sems, recv_sems, 7 * a + k, peer))
        return out
    return make


def _copies_gather_chips(srcs, lands, send_sems, recv_sems):
    x, y, c = _place()
    chips = [(1 - x, y), (x, 1 - y), (1 - x, 1 - y)]
    return [_remote(srcs[a], lands[a].at[_dev(x, y, c)], send_sems, recv_sems, 3 * a + j, (qx, qy, c))
            for a in range(len(srcs)) for j, (qx, qy) in enumerate(chips)]


def _copies_gather_sibling(srcs, lands, send_sems, recv_sems):
    x, y, c = _place()
    chips = [(x, y), (1 - x, y), (x, 1 - y), (1 - x, 1 - y)]
    return [_remote(lands[a].at[_dev(qx, qy, c)], lands[a].at[_dev(qx, qy, c)], send_sems, recv_sems, 4 * a + j,
                    (x, y, 1 - c))
            for a in range(len(srcs)) for j, (qx, qy) in enumerate(chips)]


def _copies_siblings(srcs, lands, send_sems, recv_sems):
    x, y, c = _place()
    return [_remote(srcs[a].at[2 * q + (1 - c)], lands[a].at[q], send_sems, recv_sems, 4 * a + q, (x, y, 1 - c))
            for a in range(len(srcs)) for q in range(4)]


def _copies_chips(srcs, lands, send_sems, recv_sems):
    x, y, c = _place()
    chips = [(1 - x, y), (x, 1 - y), (1 - x, 1 - y)]
    return [_remote(srcs[a].at[2 * qx + qy], lands[a].at[j], send_sems, recv_sems, 3 * a + j, (qx, qy, c))
            for a in range(len(srcs)) for j, (qx, qy) in enumerate(chips)]


def _split_start(make, per_array, srcs, lands, dep, name):
    n = len(srcs)

    def body(*refs):
        send_sems, recv_sems, token = refs[2 * n + 1], refs[2 * n + 2], refs[-1]
        for cp in make(refs[:n], refs[n:2 * n], send_sems, recv_sems):
            cp.start()
        token[...] = jnp.zeros_like(token)

    hbm = lambda t: pltpu.with_memory_space_constraint(t, pltpu.HBM)
    res = pl.pallas_call(
        body, name=name,
        out_shape=(pltpu.SemaphoreType.DMA((per_array * n,)), pltpu.SemaphoreType.DMA((per_array * n,)),
                   *[pltpu.HBM(t.shape, t.dtype) for t in (*srcs, *lands)], SDS((8, 128), f32)),
        in_specs=[_HBM] * (2 * n) + [_ANY], out_specs=(_SEM, _SEM, *([_HBM] * (2 * n)), _VMEM),
        input_output_aliases={i: 2 + i for i in range(2 * n)},
        compiler_params=pltpu.CompilerParams(has_side_effects=pltpu.SideEffectType.DATAFLOW_SIDE_EFFECTING),
    )(*[hbm(t) for t in (*srcs, *lands)], dep)
    return res[0], res[1], list(res[2:2 + n]), list(res[2 + n:2 + 2 * n]), res[-1]


def _split_wait(make, send_sems, recv_sems, srcs, lands, after, name):
    n = len(srcs)

    def body(*refs):
        for cp in make(refs[:n], refs[n:2 * n], refs[2 * n], refs[2 * n + 1]):
            cp.wait_send()
            cp.wait_recv()

    res = pl.pallas_call(
        body, name=name,
        out_shape=tuple(pltpu.HBM(t.shape, t.dtype) for t in (*srcs, *lands)),
        in_specs=[_HBM] * (2 * n) + [_SEM, _SEM, _ANY], out_specs=tuple([_HBM] * (2 * n)),
        input_output_aliases={i: i for i in range(2 * n)},
        compiler_params=pltpu.CompilerParams(has_side_effects=pltpu.SideEffectType.DATAFLOW_SIDE_EFFECTING),
    )(*srcs, *lands, send_sems, recv_sems, after)
    return list(res[:n]), list(res[n:])


def _adamw_direct(g, land, me_idx, w, m, v, name):
    r, wd = w.shape
    tr = min(r, 256)

    def body(me_ref, *refs):
        g_ref, peers = refs[0], refs[1:N_DEV]
        w_ref, m_ref, v_ref, g_out, d_out, m_out, v_out = refs[N_DEV:]
        gs = g_ref[...].astype(f32)
        for p_ref in peers:
            gs = gs + p_ref[...].astype(f32)
        d, mn, vn = _adamw(w_ref[...], gs, m_ref[...], v_ref[...])
        g_out[...] = gs
        d_out[...] = d
        m_out[...] = mn
        v_out[...] = vn

    tile = pl.BlockSpec((tr, wd), lambda i, me_ref: (i, 0))
    slot = lambda k: pl.BlockSpec((None, tr, wd), lambda i, me_ref: ((me_ref[0] + k) % N_DEV, i, 0))
    return pl.pallas_call(
        body,
        grid_spec=pltpu.PrefetchScalarGridSpec(
            num_scalar_prefetch=1, grid=(r // tr,),
            in_specs=[slot(0)] + [slot(k) for k in range(1, N_DEV)] + [tile, tile, tile],
            out_specs=[tile] * 4),
        out_shape=[SDS((r, wd), f32)] * 4, name=name, compiler_params=_cp(("arbitrary",), 48),
    )(me_idx, g, *([land] * (N_DEV - 1)), w, m, v)


def _pair_sum(g, r1, c_idx, name):
    _, r, w = g.shape
    tr = _tile_rows(r)

    def body(c_ref, g_ref, r_ref, o_ref):
        o_ref[...] = (g_ref[...].astype(f32) + r_ref[...].astype(f32)).astype(bf16)

    return pl.pallas_call(
        body,
        grid_spec=pltpu.PrefetchScalarGridSpec(
            num_scalar_prefetch=1, grid=(4, r // tr),
            in_specs=[pl.BlockSpec((None, tr, w), lambda q, i, c_ref: (2 * q + c_ref[0], i, 0)),
                      pl.BlockSpec((None, tr, w), lambda q, i, c_ref: (q, i, 0))],
            out_specs=pl.BlockSpec((None, tr, w), lambda q, i, c_ref: (q, i, 0))),
        out_shape=SDS((4, r, w), bf16), name=name, compiler_params=_cp(("arbitrary", "arbitrary")),
    )(c_idx, g, r1)


def _adamw(w, g, m, v):
    m = ADAM_B1 * m + (1.0 - ADAM_B1) * g
    v = ADAM_B2 * v + (1.0 - ADAM_B2) * (g * g)
    m_hat = m / (1.0 - ADAM_B1 ** ADAM_STEP)
    v_hat = v / (1.0 - ADAM_B2 ** ADAM_STEP)
    delta = -ADAM_LR * (m_hat / (jnp.sqrt(v_hat) + ADAM_EPS) + ADAM_WD * w)
    return delta, m, v


def _adamw_big(pieces, q_idx, w, m, v, name, row_off=0):
    r, wd = w.shape
    tr = _tile_rows(r)
    np_ = len(pieces)
    wp = wd // np_

    def body(q_ref, *refs):
        w_ref, m_ref, v_ref, g_out, d_out, m_out, v_out = refs[2 * np_:]
        for k in range(np_):
            @pl.when(pl.program_id(1) == k)
            def _():
                p_ref, r_ref = refs[2 * k], refs[2 * k + 1]
                g = p_ref[...].astype(f32)
                for j in range(3):
                    g = g + r_ref[j].astype(f32)
                d, mn, vn = _adamw(w_ref[...], g, m_ref[...], v_ref[...])
                g_out[...] = g
                d_out[...] = d
                m_out[...] = mn
                v_out[...] = vn

    tile = pl.BlockSpec((tr, wp), lambda i, k, q_ref: (i, k))
    in_specs, args = [], []
    for part, r2 in pieces:
        in_specs += [pl.BlockSpec((None, tr, wp), lambda i, k, q_ref: (q_ref[0], row_off + i, 0)),
                     pl.BlockSpec((3, tr, wp), lambda i, k, q_ref: (0, row_off + i, 0))]
        args += [part, r2]
    return pl.pallas_call(
        body,
        grid_spec=pltpu.PrefetchScalarGridSpec(
            num_scalar_prefetch=1, grid=(r // tr, np_), in_specs=in_specs + [tile, tile, tile],
            out_specs=[tile] * 4),
        out_shape=[SDS((r, wd), f32)] * 4, name=name, compiler_params=_cp(("arbitrary", "arbitrary"), 48),
    )(q_idx, *args, w, m, v)


_SMALL_ROWS = 24


def _pack_early(vec_rnn, st_out, dsr, db_in):
    def body(vr_ref, so_ref, dsr_ref, db_ref, sm_ref, sm2_ref):
        sm_ref[...] = jnp.zeros_like(sm_ref)
        sm2_ref[...] = jnp.zeros_like(sm2_ref)
        sm_ref[2:3, :] = vr_ref[3:4, :]
        sm_ref[3:6, :] = vr_ref[0:3, :]
        sm_ref[6:7, :] = so_ref[2:3, :]
        sm_ref[7:9, :] = so_ref[0:2, :]
        sm_ref[10:11, :] = so_ref[3:4, :]
        for h in range(N_KV):
            sm_ref[9:10, h * GROUP:(h + 1) * GROUP] = _colsum(dsr_ref[h])
        for j in range(6):
            sm_ref[16 + j:17 + j, :] = db_ref[0:1, j * D:(j + 1) * D]
        sm_ref[22:23, 0:D_IN - 6 * D] = db_ref[0:1, 6 * D:D_IN]
        for s in range(N_DEV):
            sm2_ref[s, 0:CONV_WIDTH, :] = vr_ref[4:8, s * 256:(s + 1) * 256]

    return pl.pallas_call(
        body, out_shape=[SDS((_SMALL_ROWS, D), f32), SDS((N_DEV, 8, 256), f32)],
        name="pack_early", compiler_params=_cp(None),
    )(vec_rnn, st_out, dsr, db_in)


def _pack_late(st_emb, dmeta):
    def body(se_ref, dm_ref, sm_ref, sm2_ref):
        sm_ref[...] = se_ref[...]
        for s in range(N_DEV):
            sm2_ref[s] = dm_ref[:, s * 256:(s + 1) * 256]

    return pl.pallas_call(
        body, out_shape=[SDS((8, D), f32), SDS((N_DEV, N_META, 256), f32)],
        name="pack_late", compiler_params=_cp(None),
    )(st_emb, dmeta)


def _small_allreduce(sm, sm2):
    def body(sm_ref, sm2_ref, o_ref, o2_ref, buf, buf2, send_sems, recv_sems):
        x, y, c = _place()
        me = _dev(x, y, c)
        copies = []
        for f in range(1, N_DEV):
            fx, fy, fc = f // 4, (f // 2) % 2, f % 2
            peer = ((x + fx) % 2, (y + fy) % 2, (c + fc) % 2)
            for t, (src, dst) in enumerate(((sm_ref, buf), (sm2_ref, buf2))):
                k = 2 * (f - 1) + t
                copies.append(pltpu.make_async_remote_copy(
                    src_ref=src, dst_ref=dst.at[me], send_sem=send_sems.at[k], recv_sem=recv_sems.at[k],
                    device_id=peer, device_id_type=MESH))
        for cp in copies:
            cp.start()
        buf[me] = sm_ref[...]
        buf2[me] = sm2_ref[...]
        for cp in copies:
            cp.wait()
        acc, acc2 = buf[0], buf2[0]
        for e in range(1, N_DEV):
            acc, acc2 = acc + buf[e], acc2 + buf2[e]
        o_ref[...] = acc
        o2_ref[...] = acc2

    return pl.pallas_call(
        body, in_specs=[_VMEM, _VMEM], out_specs=[_VMEM, _VMEM],
        out_shape=[SDS(sm.shape, f32), SDS(sm2.shape, f32)],
        scratch_shapes=[pltpu.VMEM((N_DEV, *sm.shape), f32), pltpu.VMEM((N_DEV, *sm2.shape), f32),
                        pltpu.SemaphoreType.DMA((14,)), pltpu.SemaphoreType.DMA((14,))],
        name="small_allreduce",
    )(sm, sm2)


_SMALL_ROW_OF = {"ln_emb_g": 0, "ln_emb_b": 1, "conv_b": 2, "b_ra": 3, "b_ri": 4, "lru_lambda": 5, "b_o": 6,
                 "ln_g": 7, "ln_b": 8}
_SMALL_NAMES = ["ln_emb_g", "ln_emb_b", "conv_b", "b_ra", "b_ri", "lru_lambda", "b_o", "ln_g", "ln_b",
                "sinks", "b_in", "meta_tokens", "conv_w"]


def _small_update(me_idx, early, late, wmv):
    n_fixed = 7

    def in_order(me, own_ref, land_ref):
        acc = None
        for e in range(N_DEV):
            term = jnp.where(me == e, own_ref[...], land_ref[e])
            acc = term if acc is None else acc + term
        return acc

    def body(*refs):
        me_ref, own_ref, land_ref, cown_ref, cland_ref, late_ref, meta_ref = refs[:n_fixed]
        ins = refs[n_fixed:n_fixed + 3 * len(_SMALL_NAMES)]
        outs = refs[n_fixed + 3 * len(_SMALL_NAMES):]
        me = me_ref[0]
        sm = in_order(me, own_ref, land_ref)
        conv = in_order(me, cown_ref, cland_ref)

        def grad_of(name):
            if name in ("ln_emb_g", "ln_emb_b"):
                r = _SMALL_ROW_OF[name]
                return late_ref[r:r + 1, :]
            if name in _SMALL_ROW_OF:
                r = _SMALL_ROW_OF[name]
                return sm[r:r + 1, :]
            if name == "sinks":
                return sm[9:10, 0:N_KV * GROUP]
            if name == "b_in":
                return jnp.concatenate([sm[16 + j:17 + j, :] for j in range(7)], axis=1)[:, :D_IN]
            if name == "meta_tokens":
                return meta_ref[...]
            return conv[0:CONV_WIDTH, :]

        for i, name in enumerate(_SMALL_NAMES):
            w_ref, m_ref, v_ref = ins[3 * i:3 * i + 3]
            g = grad_of(name)
            d, mn, vn = _adamw(w_ref[...], g, m_ref[...], v_ref[...])
            outs[4 * i][...] = g
            outs[4 * i + 1][...] = d
            outs[4 * i + 2][...] = mn
            outs[4 * i + 3][...] = vn
        outs[-1][...] = jnp.broadcast_to(jnp.sum(sm[10:11, :], axis=1, keepdims=True), (8, 128))

    args, out_shape = [me_idx, *early, *late], []
    for name in _SMALL_NAMES:
        args += list(wmv[name])
        out_shape += [SDS(wmv[name][0].shape, f32)] * 4
    out_shape.append(SDS((8, 128), f32))
    res = pl.pallas_call(
        body, out_shape=out_shape, in_specs=[pl.BlockSpec(memory_space=pltpu.SMEM)] + [_VMEM] * (len(args) - 1),
        name="small_update", compiler_params=_cp(None))(*args)
    return {name: tuple(res[4 * i:4 * i + 4]) for i, name in enumerate(_SMALL_NAMES)}, res[-1][0, 0]


_WEIGHTS = ["meta_tokens", "ln_emb_g", "ln_emb_b", "w_in", "b_in", "conv_w", "conv_b", "w_ra", "b_ra", "w_ri",
            "b_ri", "lru_lambda", "sinks", "w_rnn_out", "w_attn_out", "w_o", "b_o", "ln_g", "ln_b"]
_SMALL_2D = {"meta_tokens": (N_META, 256), "conv_w": (CONV_WIDTH, 256), "b_in": (1, D_IN), "sinks": (1, N_KV * GROUP)}


def kernel(x, meta_tokens, ln_emb_g, ln_emb_b, w_in, b_in, conv_w, conv_b, w_ra, b_ra, w_ri, b_ri, lru_lambda, sinks, w_rnn_out, w_attn_out, w_o, b_o, ln_g, ln_b, loss_target, m_meta_tokens, m_ln_emb_g, m_ln_emb_b, m_w_in, m_b_in, m_conv_w, m_conv_b, m_w_ra, m_b_ra, m_w_ri, m_b_ri, m_lru_lambda, m_sinks, m_w_rnn_out, m_w_attn_out, m_w_o, m_b_o, m_ln_g, m_ln_b, v_meta_tokens, v_ln_emb_g, v_ln_emb_b, v_w_in, v_b_in, v_conv_w, v_conv_b, v_w_ra, v_b_ra, v_w_ri, v_b_ri, v_lru_lambda, v_sinks, v_w_rnn_out, v_w_attn_out, v_w_o, v_b_o, v_ln_g, v_ln_b):
    w = dict(meta_tokens=meta_tokens, ln_emb_g=ln_emb_g, ln_emb_b=ln_emb_b, w_in=w_in, b_in=b_in, conv_w=conv_w,
             conv_b=conv_b, w_ra=w_ra, b_ra=b_ra, w_ri=w_ri, b_ri=b_ri, lru_lambda=lru_lambda, sinks=sinks,
             w_rnn_out=w_rnn_out, w_attn_out=w_attn_out, w_o=w_o, b_o=b_o, ln_g=ln_g, ln_b=ln_b)
    m = dict(meta_tokens=m_meta_tokens, ln_emb_g=m_ln_emb_g, ln_emb_b=m_ln_emb_b, w_in=m_w_in, b_in=m_b_in,
             conv_w=m_conv_w, conv_b=m_conv_b, w_ra=m_w_ra, b_ra=m_b_ra, w_ri=m_w_ri, b_ri=m_b_ri,
             lru_lambda=m_lru_lambda, sinks=m_sinks, w_rnn_out=m_w_rnn_out, w_attn_out=m_w_attn_out, w_o=m_w_o,
             b_o=m_b_o, ln_g=m_ln_g, ln_b=m_ln_b)
    v = dict(meta_tokens=v_meta_tokens, ln_emb_g=v_ln_emb_g, ln_emb_b=v_ln_emb_b, w_in=v_w_in, b_in=v_b_in,
             conv_w=v_conv_w, conv_b=v_conv_b, w_ra=v_w_ra, b_ra=v_b_ra, w_ri=v_w_ri, b_ri=v_b_ri,
             lru_lambda=v_lru_lambda, sinks=v_sinks, w_rnn_out=v_w_rnn_out, w_attn_out=v_w_attn_out, w_o=v_w_o,
             b_o=v_b_o, ln_g=v_ln_g, ln_b=v_ln_b)
    px, py, pc = _place()
    as_idx = lambda t: jnp.reshape(t, (1,)).astype(jnp.int32)
    c_idx, q_idx, me_idx = as_idx(pc), as_idx(2 * px + py), as_idx(_dev(px, py, pc))

    w3_s, wrg_s, small_s = _cast_small(w_rnn_out, w_attn_out, w_o, w_ra, w_ri, meta_tokens, conv_w)
    vec = lambda name: w[name].reshape(1, -1)
    p = {k: vec(k) for k in ("ln_emb_g", "ln_emb_b", "b_in", "conv_b", "b_ra", "b_ri", "lru_lambda", "sinks",
                             "b_o", "ln_g", "ln_b")}
    w_in_t = lambda a: jnp.swapaxes(a, 1, 2).reshape(SHARD_IN, D)
    wg, wrg, smallw, w3_land = _all_gather([_cast_w_in(w_in_t(w_in)), wrg_s, small_s], [w3_s])
    w3_pending = _split_start(_copies_direct(True), 7, [w3_s], [w3_land], smallw, "gather_w3_start")
    w_full = wg.reshape(D_IN, D)

    zero = w3_pending[4][0:1, 0:1]
    h32, hb = _ln_emb(x, smallw, p["ln_emb_g"], p["ln_emb_b"])
    z = _mm(hb, w_full, nt=True, bias=p["b_in"] + zero, name="mm_z")
    s = _step_attn(_step_rnn(h32, hb, z, wrg, smallw, p, zero), p, zero)
    w3 = _split_wait(_copies_direct(True), *w3_pending[:4], s["lse"], "gather_w3_wait")[1][0]
    t = _step_merge(s, loss_target, w3, p)

    big = {}
    two_d = lambda name: (w[name].shape[-2], w[name].shape[-1])
    proj = ("w_o", "w_rnn_out", "w_attn_out")
    g_proj = [t[k].reshape(N_DEV, 256, D) for k in ("g_wo", "g_wrnn", "g_wattn")]
    g_pending = _split_start(_copies_direct(False), 7, g_proj, [lax.empty((N_DEV, 256, D), bf16) for _ in proj],
                             p["b_o"], "reduce_proj_start")
    u = _step_backward(s, t, wrg, smallw, p, p["conv_b"] + g_pending[4][0:1, 0:1])

    def siblings_start(gs, dep, tag):
        return _split_start(_copies_siblings, 4, gs, [lax.empty((4, *g.shape[1:]), bf16) for g in gs], dep,
                            "reduce_siblings_start_" + tag)

    def chips_start(gs, r1, dep, tag):
        parts = [_pair_sum(g, r, c_idx, "pair_sum_%s%d" % (tag, i)) for i, (g, r) in enumerate(zip(gs, r1))]
        return _split_start(_copies_chips, 3, parts, [lax.empty((3, *q.shape[1:]), bf16) for q in parts], dep,
                            "reduce_chips_start_" + tag)

    g_a, dz, db_in = _mm_dwin_parts(s["hb"], u["dz_parts"])
    shards = lambda g: g.reshape(N_DEV, SHARD_IN, W_IN_HALF)
    sib_a = siblings_start([shards(g_a), u["g_wrg"].reshape(N_DEV, 2 * RNN_BLOCK, RNN_BLOCK)], db_in, "a")
    g_b = _mm_dwin(s["hb"], dz, sib_a[4])
    sib_b = siblings_start([shards(g_b)], db_in, "b")
    chp_a = chips_start(*_split_wait(_copies_siblings, *sib_a[:4], sib_b[4], "reduce_siblings_wait_a"), db_in, "a")
    g_proj, g_land = _split_wait(_copies_direct(False), *g_pending[:4], chp_a[4], "reduce_proj_wait")
    for i, name in enumerate(proj):
        res = _adamw_direct(g_proj[i], g_land[i], me_idx, w[name].reshape(two_d(name)), m[name].reshape(two_d(name)),
                            v[name].reshape(two_d(name)), "adamw_" + name)
        big[name] = tuple(r.reshape(w[name].shape) for r in res)
    chp_b = chips_start(*_split_wait(_copies_siblings, *sib_b[:4], big["w_attn_out"][3], "reduce_siblings_wait_b"),
                        db_in, "b")
    sm_e = _pack_early(u["vec_rnn"], t["st_out"], u["dsr"], db_in)
    early = _split_start(_copies_direct(True), 7, list(sm_e),
                         [lax.empty((N_DEV, *a.shape), f32) for a in sm_e], chp_b[4], "small_early_start")
    u.update(_step_input_grad(dz, w_full, early[4], t["du32"], x, smallw, p))
    sm_l, meta_l = _small_allreduce(*_pack_late(u["st_emb"], u["dmeta"]))
    (sm_own, conv_own), (sm_land, conv_land) = _split_wait(_copies_direct(True), *early[:4], sm_l, "small_early_wait")
    me = _dev(px, py, pc)
    mine = lambda a, axis: lax.dynamic_index_in_dim(a, me, axis, keepdims=False)
    two = lambda name, t: t.reshape(_SMALL_2D.get(name, (1, D)))
    small, loss = _small_update(me_idx, (sm_own, sm_land, mine(conv_own, 0), mine(conv_land, 1)),
                                (sm_l, mine(meta_l, 0)),
                                {k: (two(k, w[k]), two(k, m[k]), two(k, v[k])) for k in _SMALL_NAMES})

    parts_a, r2_a = _split_wait(_copies_chips, *chp_a[:4], small["b_in"][3], "reduce_chips_wait_a")
    parts_b, r2_b = _split_wait(_copies_chips, *chp_b[:4], small["b_in"][2], "reduce_chips_wait_b")
    res = _adamw_big([(parts_a[0], r2_a[0]), (parts_b[0], r2_b[0])], q_idx, w_in_t(w["w_in"]), w_in_t(m["w_in"]),
                     w_in_t(v["w_in"]), "adamw_w_in")
    big["w_in"] = tuple(jnp.swapaxes(r.reshape(1, SHARD_IN, D), 1, 2) for r in res)
    for i, name in enumerate(("w_ra", "w_ri")):
        sq = (RNN_BLOCK, RNN_BLOCK)
        res = _adamw_big([(parts_a[1], r2_a[1])], q_idx, w[name].reshape(sq), m[name].reshape(sq), v[name].reshape(sq),
                         "adamw_" + name, row_off=i)
        big[name] = tuple(r.reshape(w[name].shape) for r in res)
    res = dict(big)
    for k in _SMALL_NAMES:
        res[k] = tuple(t.reshape(w[k].shape) for t in small[k])

    outs = [loss, u["grad_x"]]
    for j in range(4):
        outs += [res[k][j] for k in _WEIGHTS]
    return tuple(outs)
```

```python
import functools

import jax
import jax.numpy as jnp
from jax import lax
from jax.experimental import pallas as pl
from jax.experimental.pallas import tpu as pltpu

f32, bf16 = jnp.float32, jnp.bfloat16
SDS = jax.ShapeDtypeStruct

N_DEV = 8
D = 2048
N_META = 16
BLK = 128
ROW0 = BLK - N_META
N_RNN_BLOCKS = 8
RNN_BLOCK = D // N_RNN_BLOCKS
CONV_WIDTH = 4
LRU_C = 8.0
HEAD_DIM = 64
N_KV = 4
GROUP = 8
HALF = HEAD_DIM // 2
ROPE_THETA = 10000.0
NEG_INF = -1e30
LN_EPS = 1e-5
ALPHA = 2.0 ** 0.25
D_IN = 12800
SHARD_IN = D_IN // N_DEV
W_IN_HALF = D // 2
OFF_GR, OFF_Q, OFF_K, OFF_V, OFF_GA, OFF_G = 2048, 4096, 6144, 6400, 6656, 8704
ADAM_LR, ADAM_B1, ADAM_B2, ADAM_EPS, ADAM_WD, ADAM_STEP = 1e-3, 0.9, 0.999, 1e-8, 0.01, 10
VMEM_LIMIT_MB = 56
MESH = pl.DeviceIdType.MESH


def _cp(sem=None, vmem_mb=40):
    return pltpu.CompilerParams(dimension_semantics=sem, vmem_limit_bytes=vmem_mb * 2 ** 20)


def _row_chunk(m):
    best = 16
    for c in range(16, 641, 16):
        if m % c == 0:
            best = c
    return best


def _sigmoid(x):
    return 1.0 / (1.0 + jnp.exp(-x))


def _silu_and_grad(x):
    s = _sigmoid(x)
    return x * s, s * (1.0 + x * (1.0 - s))


def _log_sigmoid(x):
    return jnp.minimum(x, 0.0) - jnp.log1p(jnp.exp(-jnp.abs(x)))


def _ln_rows(v, g, b):
    mu = jnp.mean(v, axis=-1, keepdims=True)
    c = v - mu
    var = jnp.mean(c * c, axis=-1, keepdims=True)
    rstd = lax.rsqrt(var + LN_EPS)
    xhat = c * rstd
    return xhat * g + b, xhat, rstd


def _ln_rows_bwd(dy, g, xhat, rstd):
    dxh = dy * g
    m1 = jnp.mean(dxh, axis=-1, keepdims=True)
    m2 = jnp.mean(dxh * xhat, axis=-1, keepdims=True)
    return rstd * (dxh - m1 - xhat * m2)


def _colsum(v):
    return jnp.sum(v, axis=0, keepdims=True)


def _dot(a, b):
    return jnp.dot(a, b, preferred_element_type=f32)


def _dot_nt(a, b):
    return lax.dot_general(a, b, (((1,), (1,)), ((), ())), preferred_element_type=f32)


def _dot_tn(a, b):
    return lax.dot_general(a, b, (((0,), (0,)), ((), ())), preferred_element_type=f32)


def _meta_full(sw_ref):
    return jnp.concatenate([sw_ref[s, 0:N_META, :] for s in range(N_DEV)], axis=1)


def _ln_emb(x, smallw, g_e, b_e):
    seq = x.shape[1]
    rows = seq + BLK
    nb = rows // BLK

    def body(x_ref, sw_ref, g_ref, b_ref, h32_ref, hb_ref):
        i = pl.program_id(0)
        g, b = g_ref[...], b_ref[...]

        def emit(blk):
            h32_ref[...] = blk
            hb_ref[...] = blk.astype(bf16)

        @pl.when(i == 0)
        def _():
            hm = _ln_rows(_meta_full(sw_ref), g, b)[0]
            emit(jnp.concatenate([jnp.zeros((ROW0, D), f32), hm], axis=0))

        @pl.when(i > 0)
        def _():
            emit(_ln_rows(x_ref[0], g, b)[0])

    return pl.pallas_call(
        body, grid=(nb,),
        in_specs=[pl.BlockSpec((1, BLK, D), lambda i: (0, jnp.maximum(i - 1, 0), 0)),
                  pl.BlockSpec((N_DEV, 24, 256), lambda i: (0, 0, 0)),
                  pl.BlockSpec((1, D), lambda i: (0, 0)),
                  pl.BlockSpec((1, D), lambda i: (0, 0))],
        out_specs=[pl.BlockSpec((BLK, D), lambda i: (i, 0)),
                   pl.BlockSpec((BLK, D), lambda i: (i, 0))],
        out_shape=[SDS((rows, D), f32), SDS((rows, D), bf16)],
        name="ln_emb", compiler_params=_cp(("arbitrary",)),
    )(x, smallw, g_e, b_e)


def _ln_emb_bwd(dh_lo, dh_hi, du32, x, smallw, g_e):
    seq = x.shape[1]
    rows = seq + BLK
    nb = rows // BLK

    def body(dlo_ref, dhi_ref, du_ref, x_ref, sw_ref, g_ref, gx_ref, dmeta_ref, st_ref):
        i = pl.program_id(0)
        g = g_ref[...]
        dht = jnp.concatenate([dlo_ref[...], dhi_ref[...]], axis=1) + ALPHA * du_ref[...]

        @pl.when(i == 0)
        def _():
            v = jnp.concatenate([jnp.zeros((ROW0, D), f32), _meta_full(sw_ref)], axis=0)
            valid = lax.broadcasted_iota(jnp.int32, (BLK, 1), 0) >= ROW0
            d = jnp.where(valid, dht, 0.0)
            _, xhat, rstd = _ln_rows(v, g, 0.0)
            dv = _ln_rows_bwd(d, g, xhat, rstd)
            dmeta_ref[...] = dv[ROW0:, :]
            st_ref[...] = jnp.concatenate([_colsum(d * xhat), _colsum(d), jnp.zeros((6, D), f32)], axis=0)

        @pl.when(i > 0)
        def _():
            _, xhat, rstd = _ln_rows(x_ref[0], g, 0.0)
            gx_ref[0] = _ln_rows_bwd(dht, g, xhat, rstd)
            st_ref[0:1, :] += _colsum(dht * xhat)
            st_ref[1:2, :] += _colsum(dht)

    return pl.pallas_call(
        body, grid=(nb,),
        in_specs=[pl.BlockSpec((BLK, W_IN_HALF), lambda i: (i, 0)),
                  pl.BlockSpec((BLK, W_IN_HALF), lambda i: (i, 0)),
                  pl.BlockSpec((BLK, D), lambda i: (i, 0)),
                  pl.BlockSpec((1, BLK, D), lambda i: (0, jnp.maximum(i - 1, 0), 0)),
                  pl.BlockSpec((N_DEV, 24, 256), lambda i: (0, 0, 0)),
                  pl.BlockSpec((1, D), lambda i: (0, 0))],
        out_specs=[pl.BlockSpec((1, BLK, D), lambda i: (0, jnp.maximum(i - 1, 0), 0)),
                   pl.BlockSpec((N_META, D), lambda i: (0, 0)),
                   pl.BlockSpec((8, D), lambda i: (0, 0))],
        out_shape=[SDS((1, seq, D), f32), SDS((N_META, D), f32), SDS((8, D), f32)],
        name="ln_emb_bwd", compiler_params=_cp(("arbitrary",)),
    )(dh_lo, dh_hi, du32, x, smallw, g_e)


def _mm(a, b, *, name, nt=False, sel=None, bias=None, out_dtype=f32, tn=512):
    m, k = a.shape
    cm = _row_chunk(m)
    stacked = sel is not None
    n = D if stacked else (b.shape[0] if nt else b.shape[1])
    am = m
    if stacked and nt:
        b_spec = pl.BlockSpec((tn // 256, None, 256, D), lambda j, i: (j, sel, 0, 0))
    elif stacked:
        b_spec = pl.BlockSpec((N_DEV, None, 256, tn), lambda j, i: (0, sel, 0, j))
    elif nt:
        b_spec = pl.BlockSpec((tn, k), lambda j, i: (j, 0))
    else:
        b_spec = pl.BlockSpec((k, tn), lambda j, i: (0, j))
    in_specs = [pl.BlockSpec((am, k), lambda j, i: (i, 0)), b_spec]
    args = [a, b]
    if bias is not None:
        in_specs.append(pl.BlockSpec((1, tn), lambda j, i: (0, j)))
        args.append(bias)

    def body(*refs):
        a_ref, b_ref, o_ref = refs[0], refs[1], refs[-1]
        bm = b_ref[...]
        if stacked:
            bm = bm.reshape((tn, D) if nt else (D, tn))
        for c in range(am // cm):
            acc = (_dot_nt if nt else _dot)(a_ref[c * cm:(c + 1) * cm, :], bm)
            if bias is not None:
                acc = acc + refs[2][...]
            o_ref[c * cm:(c + 1) * cm, :] = acc.astype(out_dtype)

    return pl.pallas_call(
        body, grid=(n // tn, m // am), in_specs=in_specs,
        out_specs=pl.BlockSpec((am, tn), lambda j, i: (i, j)),
        out_shape=SDS((m, n), out_dtype), name=name, compiler_params=_cp(("arbitrary", "arbitrary"), 48),
    )(*args)


def _mm_dh(dz, w_t, after, half):
    rows = dz.shape[0]
    tk, tn = 2560, 512
    nt = W_IN_HALF // tn
    cm = _row_chunk(rows)

    def body(a_ref, w_ref, after_ref, o_ref):
        kk = pl.program_id(1)
        for c in range(rows // cm):
            acc = _dot(a_ref[c * cm:(c + 1) * cm, :], w_ref[...])

            @pl.when(kk == 0)
            def _():
                o_ref[c * cm:(c + 1) * cm, :] = acc

            @pl.when(kk > 0)
            def _():
                o_ref[c * cm:(c + 1) * cm, :] += acc

    return pl.pallas_call(
        body, grid=(nt, D_IN // tk),
        in_specs=[pl.BlockSpec((rows, tk), lambda j, kk: (0, kk)),
                  pl.BlockSpec((tk, tn), lambda j, kk: (kk, half * nt + j)),
                  pl.BlockSpec(memory_space=pl.ANY)],
        out_specs=pl.BlockSpec((rows, tn), lambda j, kk: (0, j)),
        out_shape=SDS((rows, W_IN_HALF), f32), name="mm_dh_%d" % half,
        compiler_params=_cp(("arbitrary", "arbitrary"), 48),
    )(dz, w_t, after)


def _mm_dwin_parts(hb, parts):
    rows = hb.shape[0]
    tc = 512
    edges = [0]
    for _, w in parts:
        edges.append(edges[-1] + w // tc)

    def body(*refs):
        h_ref, (o_ref, dz_ref, db_ref) = refs[len(parts)], refs[len(parts) + 1:]
        j = pl.program_id(0)
        for p_ref, lo, hi in zip(refs, edges[:-1], edges[1:]):
            @pl.when((j >= lo) & (j < hi))
            def _():
                o_ref[...] = _dot_tn(p_ref[...], h_ref[...]).astype(bf16)
                dz_ref[...] = p_ref[...]

                def step(i, s):
                    blk = p_ref[pl.ds(pl.multiple_of(i * BLK, BLK), BLK), :].astype(f32)
                    return s + blk.reshape(BLK // 8, 8, tc).sum(axis=0)
                s = lax.fori_loop(0, rows // BLK, step, jnp.zeros((8, tc), f32))
                db_ref[...] = jnp.broadcast_to(_colsum(s), (8, tc))

    in_specs = [pl.BlockSpec((rows, tc), lambda j, lo=lo, hi=hi: (0, jnp.clip(j - lo, 0, hi - lo - 1)))
                for lo, hi in zip(edges[:-1], edges[1:])]
    return pl.pallas_call(
        body, grid=(D_IN // tc,),
        in_specs=in_specs + [pl.BlockSpec((rows, W_IN_HALF), lambda j: (0, 0))],
        out_specs=[pl.BlockSpec((tc, W_IN_HALF), lambda j: (j, 0)), pl.BlockSpec((rows, tc), lambda j: (0, j)),
                   pl.BlockSpec((8, tc), lambda j: (0, j))],
        out_shape=[SDS((D_IN, W_IN_HALF), bf16), SDS((rows, D_IN), bf16), SDS((8, D_IN), f32)],
        name="mm_dwin_0", compiler_params=_cp(("arbitrary",), VMEM_LIMIT_MB),
    )(*[a for a, _ in parts], hb)


def _mm_dwin(hb, dz, after):
    rows = dz.shape[0]
    tc = 640

    def body(dz_ref, h_ref, after_ref, o_ref):
        o_ref[...] = _dot_tn(dz_ref[...], h_ref[...]).astype(bf16)

    return pl.pallas_call(
        body, grid=(D_IN // tc,),
        in_specs=[pl.BlockSpec((rows, tc), lambda j: (0, j)),
                  pl.BlockSpec((rows, W_IN_HALF), lambda j: (0, 1)),
                  pl.BlockSpec(memory_space=pl.ANY)],
        out_specs=pl.BlockSpec((tc, W_IN_HALF), lambda j: (j, 0)),
        out_shape=SDS((D_IN, W_IN_HALF), bf16),
        name="mm_dwin_1", compiler_params=_cp(("arbitrary",), 48),
    )(dz, hb, after)


SCAN_ROWS = 32


def _scan8(a, b, reverse):
    idx = lax.broadcasted_iota(jnp.int32, a.shape, 0)
    for s in (1, 2, 4):
        sh = 8 - s if reverse else s
        a_sh, b_sh = pltpu.roll(a, sh, 0), pltpu.roll(b, sh, 0)
        m = (idx < 8 - s) if reverse else (idx >= s)
        b = jnp.where(m, a * b_sh + b, b)
        a = jnp.where(m, a * a_sh, a)
    return a, b


def _shift_rows(prev8, cur, k):
    ext = jnp.concatenate([prev8, cur], axis=0)
    return pltpu.roll(ext, k, 0)[8:, :]


def _gates(xc, w_ra, b_ra, w_ri, b_ri, ls):
    xb = xc.astype(bf16)
    r = _sigmoid(_dot(xb, w_ra) + b_ra)
    ig = _sigmoid(_dot(xb, w_ri) + b_ri)
    la = LRU_C * r * ls
    a = jnp.exp(la)
    mult = jnp.sqrt(jnp.tanh(-la) * (1.0 + a * a))
    return xb, r, ig, a, mult


_RNN_IN_SPECS = lambda rows: [
    pl.BlockSpec((1, 24, 256), lambda n: (n, 0, 0)),
    pl.BlockSpec((1, RNN_BLOCK), lambda n: (0, n)),
    pl.BlockSpec((N_DEV, 2, None, 32, RNN_BLOCK), lambda n: (0, 0, n, 0, 0)),
    pl.BlockSpec((1, RNN_BLOCK), lambda n: (0, n)),
    pl.BlockSpec((1, RNN_BLOCK), lambda n: (0, n)),
    pl.BlockSpec((1, RNN_BLOCK), lambda n: (0, n)),
]


def _rnn_fwd(z, smallw, conv_b, wrg, b_ra, b_ri, lam):
    rows = z.shape[0]
    nb = rows // BLK
    col = lambda off: pl.BlockSpec((rows, RNN_BLOCK), lambda n: (0, off // RNN_BLOCK + n))

    def body(xr_ref, gr_ref, sw_ref, cb_ref, w_ref, bra_ref, bri_ref, lam_ref, xc_ref, hr_ref, ya_ref, yat_ref, a_s):
        cw = sw_ref[0, N_META:24, :]
        cb = cb_ref[...]
        w_ra = w_ref[:, 0].reshape(RNN_BLOCK, RNN_BLOCK)
        w_ri = w_ref[:, 1].reshape(RNN_BLOCK, RNN_BLOCK)
        b_ra_v, b_ri_v = bra_ref[...], bri_ref[...]
        ls = _log_sigmoid(lam_ref[...])
        rid = lax.broadcasted_iota(jnp.int32, (BLK, 1), 0)

        def blk_step(i, carry):
            r0 = pl.multiple_of(i * BLK, BLK)
            grow = rid + r0
            valid = grow >= ROW0
            cur = jnp.where(valid, xr_ref[pl.ds(r0, BLK), :], 0.0)
            prev8 = xr_ref[pl.ds(pl.multiple_of(jnp.maximum(r0 - 8, 0), 8), 8), :] * (i > 0).astype(f32)
            xc = cb + cw[0:1] * cur
            for k in range(1, CONV_WIDTH):
                xc = xc + cw[k:k + 1] * _shift_rows(prev8, cur, k)
            xc_ref[pl.ds(r0, BLK), :] = xc
            _, _, ig, a, mult = _gates(xc, w_ra, b_ra_v, w_ri, b_ri_v, ls)
            mult = jnp.where(grow == ROW0, 1.0, mult)
            a_s[pl.ds(r0, BLK), :] = a
            hr_ref[pl.ds(r0, BLK), :] = jnp.where(valid, mult * ig * xc, 0.0)
            return carry

        lax.fori_loop(0, nb, blk_step, 0)

        def scan_step(j, carry):
            r0 = pl.multiple_of(j * SCAN_ROWS, SCAN_ROWS)
            tiles = [_scan8(a_s[pl.ds(r0 + 8 * k, 8), :], hr_ref[pl.ds(r0 + 8 * k, 8), :], False)
                     for k in range(SCAN_ROWS // 8)]
            for k, (a, b) in enumerate(tiles):
                h = b + a * carry
                hr_ref[pl.ds(r0 + 8 * k, 8), :] = h
                carry = jnp.broadcast_to(h[7:8, :], (8, RNN_BLOCK))
            return carry

        lax.fori_loop(0, rows // SCAN_ROWS, scan_step, jnp.zeros((8, RNN_BLOCK), f32))

        def gate_step(i, carry):
            r0 = pl.multiple_of(i * BLK, BLK)
            ya_ref[pl.ds(r0, BLK), :] = (hr_ref[pl.ds(r0, BLK), :]
                                         * _silu_and_grad(gr_ref[pl.ds(r0, BLK), :])[0]).astype(bf16)
            return carry

        lax.fori_loop(0, nb, gate_step, 0)
        yat_ref[...] = ya_ref[...].astype(f32).T.astype(bf16)

    return pl.pallas_call(
        body, grid=(N_RNN_BLOCKS,),
        in_specs=[col(0), col(OFF_GR)] + _RNN_IN_SPECS(rows),
        out_specs=[pl.BlockSpec((rows, RNN_BLOCK), lambda n: (0, n))] * 3
                  + [pl.BlockSpec((RNN_BLOCK, rows), lambda n: (n, 0))],
        out_shape=[SDS((rows, D), f32), SDS((rows, D), f32), SDS((rows, D), bf16), SDS((D, rows), bf16)],
        scratch_shapes=[pltpu.VMEM((rows, RNN_BLOCK), f32)],
        name="rnn_fwd", compiler_params=_cp(("arbitrary",)),
    )(z, z, smallw, conv_b, wrg, b_ra, b_ri, lam)


def _rnn_bwd(dya, hr, xc, z, smallw, conv_b, wrg, b_ra, b_ri, lam):
    rows = z.shape[0]
    nb = rows // BLK
    col = lambda off: pl.BlockSpec((rows, RNN_BLOCK), lambda n: (0, off // RNN_BLOCK + n))
    blk = pl.BlockSpec((rows, RNN_BLOCK), lambda n: (0, n))

    def body(dya_ref, hr_ref, xc_ref, xr_ref, gr_ref, sw_ref, cb_ref, w_ref, bra_ref, bri_ref, lam_ref,
             dxr_ref, dgr_ref, dw_ref, vec_ref, a_s, lam_s, dxc_s, r_s, ig_s, mult_s, dw_s):
        cw = sw_ref[0, N_META:24, :]
        w_ra = w_ref[:, 0].reshape(RNN_BLOCK, RNN_BLOCK)
        w_ri = w_ref[:, 1].reshape(RNN_BLOCK, RNN_BLOCK)
        b_ra_v, b_ri_v = bra_ref[...], bri_ref[...]
        lam_v = lam_ref[...]
        ls = _log_sigmoid(lam_v)
        rid = lax.broadcasted_iota(jnp.int32, (BLK, 1), 0)
        zrow = jnp.zeros((1, RNN_BLOCK), f32)

        def p1(i, carry):
            r0 = pl.multiple_of(i * BLK, BLK)
            sl = pl.ds(r0, BLK)
            _, r, ig, a, mult = _gates(xc_ref[sl, :], w_ra, b_ra_v, w_ri, b_ri_v, ls)
            a_s[sl, :] = a
            r_s[sl, :] = r
            ig_s[sl, :] = ig
            mult_s[sl, :] = mult
            sg, dsg = _silu_and_grad(gr_ref[sl, :])
            d = dya_ref[sl, :]
            lam_s[sl, :] = d * sg
            dgr_ref[sl, :] = (d * hr_ref[sl, :] * dsg).astype(bf16)
            return carry

        lax.fori_loop(0, nb, p1, 0)

        def p2(jj, carry):
            r0 = pl.multiple_of((rows // SCAN_ROWS - 1 - jj) * SCAN_ROWS, SCAN_ROWS)
            idx = lax.broadcasted_iota(jnp.int32, (8, RNN_BLOCK), 0)
            tiles = []
            for k in range(SCAN_ROWS // 8):
                sl = pl.ds(r0 + 8 * k, 8)
                a, g = a_s[sl, :], lam_s[sl, :]
                tiles.append((g, *_scan8(a, a * g, True)))
            for k in reversed(range(SCAN_ROWS // 8)):
                g, ca, cb_ = tiles[k]
                mu = cb_ + ca * carry
                lam_s[pl.ds(r0 + 8 * k, 8), :] = g + jnp.where(idx < 7, pltpu.roll(mu, 7, 0), carry)
                carry = jnp.broadcast_to(mu[0:1, :], (8, RNN_BLOCK))
            return carry

        lax.fori_loop(0, rows // SCAN_ROWS, p2, jnp.zeros((8, RNN_BLOCK), f32))

        dw_s[...] = jnp.zeros_like(dw_s)

        def p3(i, carry):
            d_bra, d_bri, d_ls = carry
            r0 = pl.multiple_of(i * BLK, BLK)
            sl = pl.ds(r0, BLK)
            grow = rid + r0
            valid = grow >= ROW0
            first = grow == ROW0
            xcv = xc_ref[sl, :]
            xb = xcv.astype(bf16)
            r, ig, a = r_s[sl, :], ig_s[sl, :], a_s[sl, :]
            mult = jnp.where(first, 1.0, mult_s[sl, :])
            lam_t = lam_s[sl, :]
            du = jnp.where(valid, lam_t, 0.0)
            hprev = _shift_rows(hr_ref[pl.ds(pl.multiple_of(jnp.maximum(r0 - 8, 0), 8), 8), :] * (i > 0).astype(f32), hr_ref[sl, :], 1)
            da = lam_t * hprev
            dmult = jnp.where(first, 0.0, du * ig * xcv)
            di = du * mult * xcv
            dxc = du * mult * ig
            ratio = jnp.where(valid & jnp.logical_not(first), a * a / mult, 0.0)
            dla = da * a - dmult * ratio
            dpr = (dla * (LRU_C * ls)) * r * (1.0 - r)
            dpi = di * ig * (1.0 - ig)
            dprb, dpib = dpr.astype(bf16), dpi.astype(bf16)
            dw_s[0] += _dot_tn(xb, dprb)
            dw_s[1] += _dot_tn(xb, dpib)
            dxc_s[sl, :] = dxc + _dot_nt(dprb, w_ra) + _dot_nt(dpib, w_ri)
            return d_bra + _colsum(dpr), d_bri + _colsum(dpi), d_ls + _colsum(dla * (LRU_C * r))

        d_bra, d_bri, d_ls = lax.fori_loop(0, nb, p3, (zrow, zrow, zrow))

        def p4(i, carry):
            d_cb, d_w0, d_w1, d_w2, d_w3 = carry
            r0 = pl.multiple_of(i * BLK, BLK)
            sl = pl.ds(r0, BLK)
            grow = rid + r0
            valid = grow >= ROW0
            dxc = dxc_s[sl, :]
            nxt = dxc_s[pl.ds(pl.multiple_of(jnp.minimum(r0 + BLK, rows - 8), 8), 8), :] * (i < nb - 1).astype(f32)
            ext = jnp.concatenate([dxc, nxt], axis=0)
            dxr = cw[0:1] * dxc
            for k in range(1, CONV_WIDTH):
                dxr = dxr + cw[k:k + 1] * pltpu.roll(ext, BLK + 8 - k, 0)[:BLK, :]
            dxr_ref[sl, :] = jnp.where(valid, dxr, 0.0).astype(bf16)
            cur = jnp.where(valid, xr_ref[sl, :], 0.0)
            prev8 = xr_ref[pl.ds(pl.multiple_of(jnp.maximum(r0 - 8, 0), 8), 8), :] * (i > 0).astype(f32)
            dws = [d_w0 + _colsum(dxc * cur)]
            for k, acc in ((1, d_w1), (2, d_w2), (3, d_w3)):
                dws.append(acc + _colsum(dxc * _shift_rows(prev8, cur, k)))
            return (d_cb + _colsum(dxc), *dws)

        d_cb, d_w0, d_w1, d_w2, d_w3 = lax.fori_loop(0, nb, p4, (zrow,) * 5)

        d_lam = d_ls * _sigmoid(-lam_v)
        vec_ref[...] = jnp.concatenate([d_bra, d_bri, d_lam, d_cb, d_w0, d_w1, d_w2, d_w3], axis=0)
        dw_ref[:, 0] = dw_s[0].astype(bf16).reshape(N_DEV, 32, RNN_BLOCK)
        dw_ref[:, 1] = dw_s[1].astype(bf16).reshape(N_DEV, 32, RNN_BLOCK)

    return pl.pallas_call(
        body, grid=(N_RNN_BLOCKS,),
        in_specs=[blk, blk, blk, col(0), col(OFF_GR)] + _RNN_IN_SPECS(rows),
        out_specs=[blk, blk,
                   pl.BlockSpec((N_DEV, 2, None, 32, RNN_BLOCK), lambda n: (0, 0, n, 0, 0)),
                   pl.BlockSpec((8, RNN_BLOCK), lambda n: (0, n))],
        out_shape=[SDS((rows, D), bf16), SDS((rows, D), bf16),
                   SDS((N_DEV, 2, N_RNN_BLOCKS, 32, RNN_BLOCK), bf16), SDS((8, D), f32)],
        scratch_shapes=[pltpu.VMEM((rows, RNN_BLOCK), f32)] * 6 + [pltpu.VMEM((2, RNN_BLOCK, RNN_BLOCK), f32)],
        name="rnn_bwd", compiler_params=_cp(("arbitrary",), 48),
    )(dya, hr, xc, z, z, smallw, conv_b, wrg, b_ra, b_ri, lam)


def _rope_tables(rows):
    half = jnp.arange(HALF, dtype=f32)
    inv = ROPE_THETA ** (-half / HALF)
    pos = (jnp.arange(rows) - ROW0).astype(f32)
    ang = pos[:, None] * inv[None, :]
    cos, sin = jnp.cos(ang), jnp.sin(ang)
    cos128 = jnp.concatenate([cos, cos, cos, cos], axis=1)
    sin128 = jnp.concatenate([-sin, sin, -sin, sin], axis=1)
    return cos128, sin128


def _rope128(x, cos128, sin128):
    lane = lax.broadcasted_iota(jnp.int32, x.shape, 1)
    swapped = jnp.where(lane % HEAD_DIM < HALF, pltpu.roll(x, 128 - HALF, 1), pltpu.roll(x, HALF, 1))
    return x * cos128 + swapped * sin128


def _qkv_prep(z, cos128, sin128):
    rows = z.shape[0]

    def body(q_ref, kv_ref, c_ref, s_ref, qo_ref, ko_ref, vo_ref):
        c, s = c_ref[...], s_ref[...]
        for g in range(D // 128):
            qo_ref[:, g * 128:(g + 1) * 128] = (_rope128(q_ref[:, g * 128:(g + 1) * 128], c, s)
                                                * (HEAD_DIM ** -0.5)).astype(bf16)
        for g in range(2):
            kr = _rope128(kv_ref[:, g * 128:(g + 1) * 128], c, s)
            for j in range(2):
                ko_ref[2 * g + j] = kr[:, j * HEAD_DIM:(j + 1) * HEAD_DIM].astype(bf16)
        for h in range(N_KV):
            vo_ref[h] = kv_ref[:, 256 + h * HEAD_DIM:256 + (h + 1) * HEAD_DIM].astype(bf16)

    return pl.pallas_call(
        body, grid=(rows // BLK,),
        in_specs=[pl.BlockSpec((BLK, D), lambda i: (i, OFF_Q // D)),
                  pl.BlockSpec((BLK, 512), lambda i: (i, OFF_K // 512)),
                  pl.BlockSpec((BLK, 128), lambda i: (i, 0)),
                  pl.BlockSpec((BLK, 128), lambda i: (i, 0))],
        out_specs=[pl.BlockSpec((BLK, D), lambda i: (i, 0)),
                   pl.BlockSpec((N_KV, BLK, HEAD_DIM), lambda i: (0, i, 0)),
                   pl.BlockSpec((N_KV, BLK, HEAD_DIM), lambda i: (0, i, 0))],
        out_shape=[SDS((rows, D), bf16), SDS((N_KV, rows, HEAD_DIM), bf16), SDS((N_KV, rows, HEAD_DIM), bf16)],
        name="qkv_prep", compiler_params=_cp(("arbitrary",)),
    )(z, z, cos128, sin128)


def _attn_mask(n):
    qi = n * BLK + lax.broadcasted_iota(jnp.int32, (BLK, 2 * BLK + N_META), 0)
    c = lax.broadcasted_iota(jnp.int32, (BLK, 2 * BLK + N_META), 1)
    jb = (n - 1) * BLK + c
    band = (jb >= BLK) & (jb <= qi) & (qi - jb < BLK)
    meta = (ROW0 + c - 2 * BLK) <= qi
    return ((c < 2 * BLK) & band) | ((c >= 2 * BLK) & meta)


N_KEYS = 2 * BLK + N_META


def _stack_heads(t):
    return jnp.concatenate([t[:, g * HEAD_DIM:(g + 1) * HEAD_DIM] for g in range(GROUP)], axis=0)


def _sink_column(sink_ref, h):
    g = lax.broadcasted_iota(jnp.int32, (GROUP, 1, 1), 0)
    col = jnp.zeros((GROUP, 1, 1), f32)
    for j in range(GROUP):
        col = jnp.where(g == j, sink_ref[h * GROUP + j], col)
    return col


def _kv_specs(last):
    cl = lambda n: jnp.minimum(n, last)
    return [pl.BlockSpec((None, N_META, HEAD_DIM), lambda h, n: (h, ROW0 // N_META, 0)),
            pl.BlockSpec((None, BLK, HEAD_DIM), lambda h, n: (h, jnp.maximum(cl(n) - 1, 0), 0)),
            pl.BlockSpec((None, BLK, HEAD_DIM), lambda h, n: (h, cl(n), 0))]


def _attn_fwd(q_r, k_r, v_b, z, sinks):
    rows = q_r.shape[0]
    nb = rows // BLK

    def body(sink_ref, q_ref, km_ref, kp_ref, kc_ref, vm_ref, vp_ref, vc_ref, ga_ref, o_ref, yb_ref, ybt_ref, lse_ref):
        h, n = pl.program_id(0), pl.program_id(1)
        kk = jnp.concatenate([kp_ref[...], kc_ref[...], km_ref[...]], axis=0)
        vv = jnp.concatenate([vp_ref[...], vc_ref[...], vm_ref[...]], axis=0)
        q2 = _stack_heads(q_ref[...])
        s = jnp.where(_attn_mask(n)[None], _dot_nt(q2, kk).reshape(GROUP, BLK, N_KEYS), NEG_INF)
        sink = _sink_column(sink_ref, h)
        m = jnp.maximum(jnp.max(s, axis=-1, keepdims=True), sink)
        p = jnp.exp(s - m)
        den = jnp.sum(p, axis=-1, keepdims=True) + jnp.exp(sink - m)
        o2 = _dot((p / den).astype(bf16).reshape(GROUP * BLK, N_KEYS), vv)
        lse = m + jnp.log(den)
        for g in range(GROUP):
            o_ref[:, g * HEAD_DIM:(g + 1) * HEAD_DIM] = o2[g * BLK:(g + 1) * BLK]
            lse_ref[:, g:g + 1] = lse[g]
        yb = o_ref[...] * _silu_and_grad(ga_ref[...])[0]
        yb_ref[...] = yb.astype(bf16)
        ybt_ref[...] = yb.T.astype(bf16)

    tile = pl.BlockSpec((BLK, 512), lambda h, n: (n, h))
    return pl.pallas_call(
        body, grid=(N_KV, nb),
        in_specs=[pl.BlockSpec(memory_space=pltpu.SMEM), tile] + _kv_specs(nb - 1) + _kv_specs(nb - 1)
                 + [pl.BlockSpec((BLK, 512), lambda h, n: (n, OFF_GA // 512 + h))],
        out_specs=[tile, tile, pl.BlockSpec((512, BLK), lambda h, n: (h, n)),
                   pl.BlockSpec((None, BLK, GROUP), lambda h, n: (h, n, 0))],
        out_shape=[SDS((rows, D), f32), SDS((rows, D), bf16), SDS((D, rows), bf16),
                   SDS((N_KV, rows, GROUP), f32)],
        name="attn_fwd", compiler_params=_cp(("arbitrary", "arbitrary")),
    )(sinks, q_r, k_r, k_r, k_r, v_b, v_b, v_b, z)


def _attn_bwd(dyb, o32, lse, q_r, k_r, v_b, z, sinks):
    rows = q_r.shape[0]
    nb = rows // BLK
    cl = lambda n: jnp.minimum(n, nb - 1)

    def body(sink_ref, dyb_ref, o_ref, lse_ref, q_ref, km_ref, kp_ref, kc_ref, vm_ref, vp_ref, vc_ref, ga_ref,
             dq_ref, dga_ref, dk_ref, dv_ref, dkm_ref, dvm_ref, dsr_ref, ck_s, cv_s):
        h, n = pl.program_id(0), pl.program_id(1)

        @pl.when(n == 0)
        def _():
            dkm_ref[...] = jnp.zeros_like(dkm_ref)
            dvm_ref[...] = jnp.zeros_like(dvm_ref)
            ck_s[...] = jnp.zeros_like(ck_s)
            cv_s[...] = jnp.zeros_like(cv_s)

        @pl.when(n < nb)
        def _():
            kk = jnp.concatenate([kp_ref[...], kc_ref[...], km_ref[...]], axis=0)
            vv = jnp.concatenate([vp_ref[...], vc_ref[...], vm_ref[...]], axis=0)
            sg, dsg = _silu_and_grad(ga_ref[...])
            dyb_v = dyb_ref[...]
            o_v = o_ref[...]
            dga_ref[...] = (dyb_v * o_v * dsg).astype(bf16)
            q2 = _stack_heads(q_ref[...])
            do2 = _stack_heads(dyb_v * sg)
            lse_v = lse_ref[...]
            lse = jnp.concatenate([lse_v[:, g:g + 1] for g in range(GROUP)], axis=0).reshape(GROUP, BLK, 1)
            delta = jnp.sum(do2 * _stack_heads(o_v), axis=-1, keepdims=True).reshape(GROUP, BLK, 1)
            s = jnp.where(_attn_mask(n)[None], _dot_nt(q2, kk).reshape(GROUP, BLK, N_KEYS), NEG_INF)
            p = jnp.exp(s - lse)
            do2b = do2.astype(bf16)
            ds = (p * (_dot_nt(do2b, vv).reshape(GROUP, BLK, N_KEYS) - delta)).astype(bf16)
            ds = ds.reshape(GROUP * BLK, N_KEYS)
            dsr = -jnp.exp(_sink_column(sink_ref, h) - lse) * delta
            dq2 = _dot(ds, kk)
            for g in range(GROUP):
                dq_ref[:, g * HEAD_DIM:(g + 1) * HEAD_DIM] = dq2[g * BLK:(g + 1) * BLK]
                dsr_ref[:, g:g + 1] = dsr[g]
            dkk = _dot_tn(ds, q2)
            dvv = _dot_tn(p.astype(bf16).reshape(GROUP * BLK, N_KEYS), do2b)
            dk_ref[...] = ck_s[...] + dkk[:BLK]
            dv_ref[...] = cv_s[...] + dvv[:BLK]
            ck_s[...] = dkk[BLK:2 * BLK]
            cv_s[...] = dvv[BLK:2 * BLK]
            dkm_ref[...] += dkk[2 * BLK:]
            dvm_ref[...] += dvv[2 * BLK:]

        @pl.when(n == nb)
        def _():
            dk_ref[...] = ck_s[...]
            dv_ref[...] = cv_s[...]

    tile = pl.BlockSpec((BLK, 512), lambda h, n: (cl(n), h))
    kvout = pl.BlockSpec((None, BLK, HEAD_DIM), lambda h, n: (h, jnp.maximum(n - 1, 0), 0))
    mout = pl.BlockSpec((None, N_META, HEAD_DIM), lambda h, n: (h, 0, 0))
    stat = pl.BlockSpec((None, BLK, GROUP), lambda h, n: (h, cl(n), 0))
    return pl.pallas_call(
        body, grid=(N_KV, nb + 1),
        in_specs=[pl.BlockSpec(memory_space=pltpu.SMEM), tile, tile, stat, tile] + _kv_specs(nb - 1)
                 + _kv_specs(nb - 1) + [pl.BlockSpec((BLK, 512), lambda h, n: (cl(n), OFF_GA // 512 + h))],
        out_specs=[tile, tile, kvout, kvout, mout, mout, stat],
        out_shape=[SDS((rows, D), f32), SDS((rows, D), bf16),
                   SDS((N_KV, rows, HEAD_DIM), f32), SDS((N_KV, rows, HEAD_DIM), f32),
                   SDS((N_KV, N_META, HEAD_DIM), f32), SDS((N_KV, N_META, HEAD_DIM), f32),
                   SDS((N_KV, rows, GROUP), f32)],
        scratch_shapes=[pltpu.VMEM((BLK, HEAD_DIM), f32), pltpu.VMEM((BLK, HEAD_DIM), f32)],
        name="attn_bwd", compiler_params=_cp(("arbitrary", "arbitrary")),
    )(sinks, dyb, o32, lse, q_r, k_r, k_r, k_r, v_b, v_b, v_b, z)


def _qkv_finish(dq, dk, dv, dkm, dvm, cos128, sin128):
    rows = dq.shape[0]

    def body(dq_ref, dk_ref, dv_ref, dkm_ref, dvm_ref, c_ref, s_ref, oq_ref, okv_ref):
        first = (pl.program_id(0) == 0).astype(f32)
        c, s = c_ref[...], -s_ref[...]
        for g in range(D // 128):
            oq_ref[:, g * 128:(g + 1) * 128] = (_rope128(dq_ref[:, g * 128:(g + 1) * 128], c, s)
                                                * (HEAD_DIM ** -0.5)).astype(bf16)
        pad = jnp.zeros((ROW0, HEAD_DIM), f32)
        ks = [dk_ref[h] + first * jnp.concatenate([pad, dkm_ref[h]], axis=0) for h in range(N_KV)]
        vs = [dv_ref[h] + first * jnp.concatenate([pad, dvm_ref[h]], axis=0) for h in range(N_KV)]
        for g in range(2):
            kp = jnp.concatenate([ks[2 * g], ks[2 * g + 1]], axis=1)
            okv_ref[:, g * 128:(g + 1) * 128] = _rope128(kp, c, s).astype(bf16)
            okv_ref[:, 256 + g * 128:256 + (g + 1) * 128] = jnp.concatenate([vs[2 * g], vs[2 * g + 1]], axis=1).astype(bf16)

    kv = pl.BlockSpec((N_KV, BLK, HEAD_DIM), lambda i: (0, i, 0))
    mt = pl.BlockSpec((N_KV, N_META, HEAD_DIM), lambda i: (0, 0, 0))
    return pl.pallas_call(
        body, grid=(rows // BLK,),
        in_specs=[pl.BlockSpec((BLK, D), lambda i: (i, 0)), kv, kv, mt, mt,
                  pl.BlockSpec((BLK, 128), lambda i: (i, 0)), pl.BlockSpec((BLK, 128), lambda i: (i, 0))],
        out_specs=[pl.BlockSpec((BLK, D), lambda i: (i, 0)), pl.BlockSpec((BLK, 512), lambda i: (i, 0))],
        out_shape=[SDS((rows, D), bf16), SDS((rows, 512), bf16)],
        name="qkv_finish", compiler_params=_cp(("arbitrary",)),
    )(dq, dk, dv, dkm, dvm, cos128, sin128)


_TW = 512


def _mix_specs(rows):
    tr = _row_chunk(rows)
    tile = pl.BlockSpec((tr, _TW), lambda i, j: (i, j))
    ga = pl.BlockSpec((tr, _TW), lambda i, j: (i, OFF_G // _TW + j))
    gb = pl.BlockSpec((tr, _TW), lambda i, j: (i, (OFF_G + D) // _TW + j))
    return (rows // tr, D // _TW), tile, ga, gb


def _mix_fwd(y_a, y_b, z):
    rows = y_a.shape[0]
    tw = 256
    col = lambda off: pl.BlockSpec((rows, tw), lambda j: (0, off // tw + j))

    def body(ya_ref, yb_ref, ga_ref, gb_ref, o_ref, ot_ref):
        mixed = (_sigmoid(ga_ref[...]) * ya_ref[...].astype(f32)
                 + _sigmoid(gb_ref[...]) * yb_ref[...].astype(f32))
        o_ref[...] = mixed.astype(bf16)
        ot_ref[...] = mixed.T.astype(bf16)

    return pl.pallas_call(
        body, grid=(D // tw,), in_specs=[col(0), col(0), col(OFF_G), col(OFF_G + D)],
        out_specs=[col(0), pl.BlockSpec((tw, rows), lambda j: (j, 0))],
        out_shape=[SDS((rows, D), bf16), SDS((D, rows), bf16)],
        name="mix_fwd", compiler_params=_cp(("arbitrary",)),
    )(y_a, y_b, z, z)


def _mix_bwd(dmixed, y_a, y_b, z):
    rows = y_a.shape[0]
    grid, _mix_tile, _mix_ga, _mix_gb = _mix_specs(rows)

    def body(dm_ref, ya_ref, yb_ref, ga_ref, gb_ref, dya_ref, dyb_ref, dga_ref, dgb_ref):
        dm = dm_ref[...].astype(f32)
        sa, sb = _sigmoid(ga_ref[...]), _sigmoid(gb_ref[...])
        dya_ref[...] = (dm * sa).astype(bf16)
        dyb_ref[...] = (dm * sb).astype(bf16)
        dga_ref[...] = (dm * ya_ref[...].astype(f32) * sa * (1.0 - sa)).astype(bf16)
        dgb_ref[...] = (dm * yb_ref[...].astype(f32) * sb * (1.0 - sb)).astype(bf16)

    return pl.pallas_call(
        body, grid=grid, in_specs=[_mix_tile, _mix_tile, _mix_tile, _mix_ga, _mix_gb],
        out_specs=[_mix_tile] * 4, out_shape=[SDS((rows, D), bf16)] * 4,
        name="mix_bwd", compiler_params=_cp(("arbitrary", "arbitrary")),
    )(dmixed, y_a, y_b, z, z)


def _final_ln(out32, h32, tgt, ln_g, ln_b):
    rows = out32.shape[0]

    def body(o_ref, h_ref, t_ref, g_ref, b_ref, du_ref, dub_ref, st_ref):
        i = pl.program_id(0)
        g = g_ref[...]
        y, xhat, rstd = _ln_rows(ALPHA * h_ref[...] + o_ref[...], g, b_ref[...])
        e = jnp.where(i > 0, y - t_ref[0], 0.0)
        dy = e * (1.0 / D)
        du = _ln_rows_bwd(dy, g, xhat, rstd)
        du_ref[...] = du
        dub_ref[...] = du.astype(bf16)
        st = jnp.concatenate([_colsum(dy * xhat), _colsum(dy), _colsum(du), _colsum(e * e) * (0.5 / D),
                              jnp.zeros((4, D), f32)], axis=0)

        @pl.when(i == 0)
        def _():
            st_ref[...] = st

        @pl.when(i > 0)
        def _():
            st_ref[...] += st

    row = pl.BlockSpec((BLK, D), lambda i: (i, 0))
    vec = pl.BlockSpec((1, D), lambda i: (0, 0))
    return pl.pallas_call(
        body, grid=(rows // BLK,),
        in_specs=[row, row, pl.BlockSpec((1, BLK, D), lambda i: (0, jnp.maximum(i - 1, 0), 0)), vec, vec],
        out_specs=[row, row, pl.BlockSpec((8, D), lambda i: (0, 0))],
        out_shape=[SDS((rows, D), f32), SDS((rows, D), bf16), SDS((8, D), f32)],
        name="final_ln", compiler_params=_cp(("arbitrary",)),
    )(out32, h32, tgt, ln_g, ln_b)


def _step_rnn(h32, hb, z, wrg, smallw, p, zero):
    rows = z.shape[0]
    cos128, sin128 = _rope_tables(rows)
    cos128 = cos128 + zero
    xc, hr, ya, ya_t = _rnn_fwd(z, smallw, p["conv_b"] + zero, wrg, p["b_ra"], p["b_ri"], p["lru_lambda"])
    q_r, k_r, v_b = _qkv_prep(z, cos128, sin128)
    return dict(cos128=cos128, sin128=sin128, h32=h32, hb=hb, z=z, xc=xc, hr=hr, ya=ya, ya_t=ya_t,
                q_r=q_r, k_r=k_r, v_b=v_b)


def _step_attn(s, p, zero):
    sinks = p["sinks"].reshape(N_KV * GROUP) + zero[0]
    o32, yb, yb_t, lse = _attn_fwd(s["q_r"], s["k_r"], s["v_b"], s["z"], sinks)
    return dict(s, sinks=sinks, o32=o32, yb=yb, yb_t=yb_t, lse=lse)


def _step_merge(s, tgt, w3, p):
    ya, yb, z = s["ya"], s["yb"], s["z"]
    y_a = _mm(ya, w3, sel=0, out_dtype=bf16, name="mm_ya")
    y_b = _mm(yb, w3, sel=1, out_dtype=bf16, name="mm_yb")
    mixed, mixed_t = _mix_fwd(y_a, y_b, z)
    out32 = _mm(mixed, w3, sel=2, bias=p["b_o"], name="mm_out")
    du32, dub, st_out = _final_ln(out32, s["h32"], tgt, p["ln_g"], p["ln_b"])

    g_wo = _mm(mixed_t, dub, out_dtype=bf16, name="mm_dwo")
    dmixed = _mm(dub, w3, sel=2, nt=True, out_dtype=bf16, name="mm_dmixed")
    dya_b, dyb_b, dma, dmb = _mix_bwd(dmixed, y_a, y_b, z)
    g_wrnn = _mm(s["ya_t"], dya_b, out_dtype=bf16, name="mm_dwrnn")
    g_wattn = _mm(s["yb_t"], dyb_b, out_dtype=bf16, name="mm_dwattn")
    dya = _mm(dya_b, w3, sel=0, nt=True, name="mm_dya")
    dyb = _mm(dyb_b, w3, sel=1, nt=True, name="mm_dyb")
    return dict(du32=du32, st_out=st_out, dma=dma, dmb=dmb, dya=dya, dyb=dyb, g_wo=g_wo, g_wrnn=g_wrnn,
                g_wattn=g_wattn)


def _step_backward(s, t, wrg, smallw, p, conv_b):
    z = s["z"]
    dxr, dgr, g_wrg, vec_rnn = _rnn_bwd(t["dya"], s["hr"], s["xc"], z, smallw, conv_b, wrg, p["b_ra"], p["b_ri"],
                                        p["lru_lambda"])
    dq_r, dga, dk, dv, dkm, dvm, dsr = _attn_bwd(t["dyb"], s["o32"], s["lse"], s["q_r"], s["k_r"], s["v_b"], z,
                                                 s["sinks"])
    dq, dkv = _qkv_finish(dq_r, dk, dv, dkm, dvm, s["cos128"], s["sin128"])
    dz_parts = [(dxr, D), (dgr, D), (dq, D), (dkv, 512), (dga, D), (t["dma"], D), (t["dmb"], D)]
    return dict(vec_rnn=vec_rnn, dsr=dsr, g_wrg=g_wrg, dz_parts=dz_parts)


def _step_input_grad(dh_lo, dh_hi, du32, x, smallw, p):
    grad_x, dmeta, st_emb = _ln_emb_bwd(dh_lo, dh_hi, du32, x, smallw, p["ln_emb_g"])
    return dict(grad_x=grad_x, dmeta=dmeta, st_emb=st_emb)


_ANY = pl.BlockSpec(memory_space=pl.ANY)
_VMEM = pl.BlockSpec(memory_space=pltpu.VMEM)


def _place():
    x, y, c = lax.axis_index("x"), lax.axis_index("y"), lax.axis_index("c")
    return x, y, c


def _dev(px, py, pc):
    return 4 * px + 2 * py + pc


def _tile_rows(r):
    return max(t for t in range(16, 321, 16) if r % t == 0) if r > 320 else r


def _cast_w_in(w_in_t):
    tm = _tile_rows(SHARD_IN)

    def body(i_ref, o_ref):
        o_ref[...] = i_ref[...].astype(bf16)

    return pl.pallas_call(
        body, grid=(SHARD_IN // tm,),
        in_specs=[pl.BlockSpec((tm, D), lambda i: (i, 0))],
        out_specs=pl.BlockSpec((tm, D), lambda i: (i, 0)),
        out_shape=SDS((SHARD_IN, D), bf16), name="cast_w_in", compiler_params=_cp(("arbitrary",)),
    )(w_in_t)


def _cast_small(w_rnn_out, w_attn_out, w_o, w_ra, w_ri, meta, conv_w):
    def body(a_ref, b_ref, c_ref, ra_ref, ri_ref, m_ref, cw_ref, w3_ref, wrg_ref, sw_ref):
        w3_ref[0] = a_ref[0].astype(bf16)
        w3_ref[1] = b_ref[0].astype(bf16)
        w3_ref[2] = c_ref[0].astype(bf16)
        wrg_ref[0] = ra_ref[0].astype(bf16)
        wrg_ref[1] = ri_ref[0].astype(bf16)
        sw_ref[...] = jnp.concatenate([m_ref[...], cw_ref[0], jnp.zeros((4, 256), f32)], axis=0)

    return pl.pallas_call(
        body,
        out_shape=[SDS((3, 256, D), bf16), SDS((2, N_RNN_BLOCKS, 32, RNN_BLOCK), bf16), SDS((24, 256), f32)],
        name="cast_small", compiler_params=_cp(None),
    )(w_rnn_out, w_attn_out, w_o, w_ra, w_ri, meta, conv_w)


def _all_gather(shards, later):
    n = len(shards)
    nl = len(later)

    def body(*refs):
        ins, outs = refs[:n], refs[n + nl:2 * n + nl]
        send_sems, recv_sems, local_sems = refs[2 * (n + nl):]
        x, y, c = _place()
        me, sibling = (x, y, c), (x, y, 1 - c)
        chips = [(1 - x, y), (x, 1 - y), (1 - x, 1 - y)]

        def copy(a, k, block, to, src=None):
            dst = outs[a].at[_dev(*block)]
            return pltpu.make_async_remote_copy(
                src_ref=dst if src is None else src, dst_ref=dst,
                send_sem=send_sems.at[a * 7 + k], recv_sem=recv_sems.at[a * 7 + k],
                device_id=to, device_id_type=MESH)

        all_ins, all_outs = refs[:n + nl], refs[n + nl:2 * (n + nl)]
        mine = [pltpu.make_async_copy(all_ins[a], all_outs[a].at[_dev(*me)], local_sems.at[a]) for a in range(n + nl)]
        for cp in mine:
            cp.start()
        first = []
        for a in range(n):
            first.append(copy(a, 0, me, sibling, src=ins[a]))
            first += [copy(a, 1 + j, me, (*chip, c), src=ins[a]) for j, chip in enumerate(chips)]
        for cp in first:
            cp.start()
        passed = []
        for a in range(n):
            for j, chip in enumerate(chips):
                copy(a, 1 + j, (*chip, c), me).wait_recv()
                cp = copy(a, 4 + j, (*chip, c), sibling)
                cp.start()
                passed.append(cp)
        for a in range(n):
            copy(a, 0, sibling, me).wait_recv()
            for j, chip in enumerate(chips):
                copy(a, 4 + j, (*chip, 1 - c), me).wait_recv()
        for cp in first + passed:
            cp.wait_send()
        for cp in mine:
            cp.wait()

    return pl.pallas_call(
        body, in_specs=[_ANY] * (n + nl), out_specs=[_ANY] * (n + nl),
        out_shape=[SDS((N_DEV, *s.shape), s.dtype) for s in (*shards, *later)],
        scratch_shapes=[pltpu.SemaphoreType.DMA((7 * n,)), pltpu.SemaphoreType.DMA((7 * n,)),
                        pltpu.SemaphoreType.DMA((n + nl,))],
        name="all_gather_weights",
    )(*shards, *later)


def _gather_small(shard):
    def body(s_ref, o_ref, send_sems, recv_sems):
        x, y, c = _place()
        me = _dev(x, y, c)
        copies = []
        for k, (fx, fy, fc) in enumerate(_PEER_FLIPS):
            peer = ((x + fx) % 2, (y + fy) % 2, (c + fc) % 2)
            copies.append(_remote(s_ref, o_ref.at[me], send_sems, recv_sems, k, peer))
        for cp in copies:
            cp.start()
        o_ref[me] = s_ref[...]
        for cp in copies:
            cp.wait()

    return pl.pallas_call(
        body, in_specs=[_VMEM], out_specs=_VMEM, out_shape=SDS((N_DEV, *shard.shape), shard.dtype),
        scratch_shapes=[pltpu.SemaphoreType.DMA((7,)), pltpu.SemaphoreType.DMA((7,))],
        name="gather_small",
    )(shard)


def _gather_project(w_s, smalls, later, hb, b_in, order):
    arrays = (w_s, *smalls, *later)
    na, n = len(arrays), 1 + len(smalls)
    rows = hb.shape[0]
    cm = _row_chunk(rows)
    nm = rows // cm
    pair = 2 * SHARD_IN

    def body(order_ref, *refs):
        ins, hb_ref, b_ref = refs[:na], refs[na], refs[na + 1]
        outs, z_ref = refs[na + 2:2 * na + 2], refs[2 * na + 2]
        wbuf, send_sems, recv_sems, local_sems, load_sems = refs[2 * na + 3:]
        k, mi = pl.program_id(0), pl.program_id(1)
        x, y, c = _place()
        me, sibling = (x, y, c), (x, y, 1 - c)
        chips = [(1 - x, y), (x, 1 - y), (1 - x, 1 - y)]

        def copy(a, kk, block, to, src=None):
            dst = outs[a].at[_dev(*block)]
            return pltpu.make_async_remote_copy(
                src_ref=dst if src is None else src, dst_ref=dst,
                send_sem=send_sems.at[a * 7 + kk], recv_sem=recv_sems.at[a * 7 + kk],
                device_id=to, device_id_type=MESH)

        mine = [pltpu.make_async_copy(ins[a], outs[a].at[_dev(*me)], local_sems.at[a]) for a in range(na)]

        def to_sibling():
            return [copy(a, 0, me, sibling, src=ins[a]) for a in range(n)]

        def to_chip(j):
            return [copy(a, 1 + j, me, (*chips[j], c), src=ins[a]) for a in range(n)]

        def load_pair(chip):
            cps = [pltpu.make_async_copy(outs[0].at[_dev(*chip, cc)], wbuf.at[pl.ds(cc * SHARD_IN, SHARD_IN)],
                                         load_sems.at[cc]) for cc in range(2)]
            for cp in cps:
                cp.start()
            for cp in cps:
                cp.wait()

        @pl.when((k == 0) & (mi == 0))
        def _():
            for cp in mine + to_sibling() + to_chip(0) + to_chip(1):
                cp.start()
            mine[0].wait()
            copy(0, 0, sibling, me).wait_recv()
            load_pair((x, y))

        for j, chip in enumerate(chips):
            @pl.when((k == j + 1) & (mi == 0))
            def _():
                for a in range(n):
                    copy(a, 1 + j, (*chip, c), me).wait_recv()
                    copy(a, 4 + j, (*chip, c), sibling).start()
                if j == 0:
                    for cp in to_chip(2):
                        cp.start()
                copy(0, 4 + j, (*chip, 1 - c), me).wait_recv()
                load_pair(chip)

        z_ref[...] = _dot_nt(hb_ref[...], wbuf[...]) + b_ref[...]

        @pl.when((k == len(chips)) & (mi == nm - 1))
        def _():
            for a in range(1, n):
                copy(a, 0, sibling, me).wait_recv()
                for j, chip in enumerate(chips):
                    copy(a, 4 + j, (*chip, 1 - c), me).wait_recv()
            for cp in to_sibling() + to_chip(0) + to_chip(1) + to_chip(2):
                cp.wait_send()
            for a in range(n):
                for j, chip in enumerate(chips):
                    copy(a, 4 + j, (*chip, c), sibling).wait_send()
            for cp in mine[1:]:
                cp.wait()

    res = pl.pallas_call(
        body,
        grid_spec=pltpu.PrefetchScalarGridSpec(
            num_scalar_prefetch=1, grid=(N_DEV // 2, nm),
            in_specs=[_ANY] * na + [pl.BlockSpec((cm, D), lambda k, i, o: (i, 0)),
                                    pl.BlockSpec((1, pair), lambda k, i, o: (0, o[k]))],
            out_specs=[_ANY] * na + [pl.BlockSpec((cm, pair), lambda k, i, o: (i, o[k]))],
            scratch_shapes=[pltpu.VMEM((pair, D), bf16), pltpu.SemaphoreType.DMA((7 * n,)),
                            pltpu.SemaphoreType.DMA((7 * n,)), pltpu.SemaphoreType.DMA((na,)),
                            pltpu.SemaphoreType.DMA((2,))]),
        out_shape=[SDS((N_DEV, *s.shape), s.dtype) for s in arrays] + [SDS((rows, D_IN), f32)],
        name="gather_project", compiler_params=_cp(("arbitrary", "arbitrary"), 48),
    )(order, *arrays, hb, b_in)
    return res[:na], res[na]


_HBM = pl.BlockSpec(memory_space=pltpu.HBM)
_SEM = pl.BlockSpec(memory_space=pltpu.SEMAPHORE)
_PEER_FLIPS = [(f // 4, (f // 2) % 2, f % 2) for f in range(1, N_DEV)]


def _remote(src, dst, send_sems, recv_sems, k, to):
    return pltpu.make_async_remote_copy(src_ref=src, dst_ref=dst, send_sem=send_sems.at[k], recv_sem=recv_sems.at[k],
                                        device_id=to, device_id_type=MESH)


def _copies_direct(same_src):
    def make(srcs, lands, send_sems, recv_sems):
        x, y, c = _place()
        me = _dev(x, y, c)
        out = []
        for a in range(len(srcs)):
            for k, (fx, fy, fc) in enumerate(_PEER_FLIPS):
                peer = ((x + fx) % 2, (y + fy) % 2, (c + fc) % 2)
                src = srcs[a] if same_src else srcs[a].at[_dev(*peer)]
                out.append(_remote(src, lands[a].at[me], send_sems, recv_sems, 7 * a + k, peer))
        return out
    return make


def _copies_gather_chips(srcs, lands, send_sems, recv_sems):
    x, y, c = _place()
    chips = [(1 - x, y), (x, 1 - y), (1 - x, 1 - y)]
    return [_remote(srcs[a], lands[a].at[_dev(x, y, c)], send_sems, recv_sems, 3 * a + j, (qx, qy, c))
            for a in range(len(srcs)) for j, (qx, qy) in enumerate(chips)]


def _copies_gather_sibling(srcs, lands, send_sems, recv_sems):
    x, y, c = _place()
    chips = [(x, y), (1 - x, y), (x, 1 - y), (1 - x, 1 - y)]
    return [_remote(lands[a].at[_dev(qx, qy, c)], lands[a].at[_dev(qx, qy, c)], send_sems, recv_sems, 4 * a + j,
                    (x, y, 1 - c))
            for a in range(len(srcs)) for j, (qx, qy) in enumerate(chips)]


def _copies_siblings(srcs, lands, send_sems, recv_sems):
    x, y, c = _place()
    return [_remote(srcs[a].at[2 * q + (1 - c)], lands[a].at[q], send_sems, recv_sems, 4 * a + q, (x, y, 1 - c))
            for a in range(len(srcs)) for q in range(4)]


def _copies_chips(srcs, lands, send_sems, recv_sems):
    x, y, c = _place()
    chips = [(1 - x, y), (x, 1 - y), (1 - x, 1 - y)]
    return [_remote(srcs[a].at[2 * qx + qy], lands[a].at[j], send_sems, recv_sems, 3 * a + j, (qx, qy, c))
            for a in range(len(srcs)) for j, (qx, qy) in enumerate(chips)]


def _split_start(make, per_array, srcs, lands, dep, name):
    n = len(srcs)

    def body(*refs):
        send_sems, recv_sems, token = refs[2 * n + 1], refs[2 * n + 2], refs[-1]
        for cp in make(refs[:n], refs[n:2 * n], send_sems, recv_sems):
            cp.start()
        token[...] = jnp.zeros_like(token)

    hbm = lambda t: pltpu.with_memory_space_constraint(t, pltpu.HBM)
    res = pl.pallas_call(
        body, name=name,
        out_shape=(pltpu.SemaphoreType.DMA((per_array * n,)), pltpu.SemaphoreType.DMA((per_array * n,)),
                   *[pltpu.HBM(t.shape, t.dtype) for t in (*srcs, *lands)], SDS((8, 128), f32)),
        in_specs=[_HBM] * (2 * n) + [_ANY], out_specs=(_SEM, _SEM, *([_HBM] * (2 * n)), _VMEM),
        input_output_aliases={i: 2 + i for i in range(2 * n)},
        compiler_params=pltpu.CompilerParams(has_side_effects=pltpu.SideEffectType.DATAFLOW_SIDE_EFFECTING),
    )(*[hbm(t) for t in (*srcs, *lands)], dep)
    return res[0], res[1], list(res[2:2 + n]), list(res[2 + n:2 + 2 * n]), res[-1]


def _split_wait(make, send_sems, recv_sems, srcs, lands, after, name):
    n = len(srcs)

    def body(*refs):
        for cp in make(refs[:n], refs[n:2 * n], refs[2 * n], refs[2 * n + 1]):
            cp.wait_send()
            cp.wait_recv()

    res = pl.pallas_call(
        body, name=name,
        out_shape=tuple(pltpu.HBM(t.shape, t.dtype) for t in (*srcs, *lands)),
        in_specs=[_HBM] * (2 * n) + [_SEM, _SEM, _ANY], out_specs=tuple([_HBM] * (2 * n)),
        input_output_aliases={i: i for i in range(2 * n)},
        compiler_params=pltpu.CompilerParams(has_side_effects=pltpu.SideEffectType.DATAFLOW_SIDE_EFFECTING),
    )(*srcs, *lands, send_sems, recv_sems, after)
    return list(res[:n]), list(res[n:])


def _adamw_direct(g, land, me_idx, w, m, v, name):
    r, wd = w.shape
    tr = min(r, 256)

    def body(me_ref, *refs):
        g_ref, peers = refs[0], refs[1:N_DEV]
        w_ref, m_ref, v_ref, g_out, d_out, m_out, v_out = refs[N_DEV:]
        gs = g_ref[...].astype(f32)
        for p_ref in peers:
            gs = gs + p_ref[...].astype(f32)
        d, mn, vn = _adamw(w_ref[...], gs, m_ref[...], v_ref[...])
        g_out[...] = gs
        d_out[...] = d
        m_out[...] = mn
        v_out[...] = vn

    tile = pl.BlockSpec((tr, wd), lambda i, me_ref: (i, 0))
    slot = lambda k: pl.BlockSpec((None, tr, wd), lambda i, me_ref: ((me_ref[0] + k) % N_DEV, i, 0))
    return pl.pallas_call(
        body,
        grid_spec=pltpu.PrefetchScalarGridSpec(
            num_scalar_prefetch=1, grid=(r // tr,),
            in_specs=[slot(0)] + [slot(k) for k in range(1, N_DEV)] + [tile, tile, tile],
            out_specs=[tile] * 4),
        out_shape=[SDS((r, wd), f32)] * 4, name=name, compiler_params=_cp(("arbitrary",), 48),
    )(me_idx, g, *([land] * (N_DEV - 1)), w, m, v)


def _pair_sum(g, r1, c_idx, name):
    _, r, w = g.shape
    tr = _tile_rows(r)

    def body(c_ref, g_ref, r_ref, o_ref):
        o_ref[...] = (g_ref[...].astype(f32) + r_ref[...].astype(f32)).astype(bf16)

    return pl.pallas_call(
        body,
        grid_spec=pltpu.PrefetchScalarGridSpec(
            num_scalar_prefetch=1, grid=(4, r // tr),
            in_specs=[pl.BlockSpec((None, tr, w), lambda q, i, c_ref: (2 * q + c_ref[0], i, 0)),
                      pl.BlockSpec((None, tr, w), lambda q, i, c_ref: (q, i, 0))],
            out_specs=pl.BlockSpec((None, tr, w), lambda q, i, c_ref: (q, i, 0))),
        out_shape=SDS((4, r, w), bf16), name=name, compiler_params=_cp(("arbitrary", "arbitrary")),
    )(c_idx, g, r1)


def _adamw(w, g, m, v):
    m = ADAM_B1 * m + (1.0 - ADAM_B1) * g
    v = ADAM_B2 * v + (1.0 - ADAM_B2) * (g * g)
    m_hat = m / (1.0 - ADAM_B1 ** ADAM_STEP)
    v_hat = v / (1.0 - ADAM_B2 ** ADAM_STEP)
    delta = -ADAM_LR * (m_hat / (jnp.sqrt(v_hat) + ADAM_EPS) + ADAM_WD * w)
    return delta, m, v


def _adamw_big(pieces, q_idx, w, m, v, name, row_off=0):
    r, wd = w.shape
    tr = _tile_rows(r)
    np_ = len(pieces)
    wp = wd // np_

    def body(q_ref, *refs):
        w_ref, m_ref, v_ref, g_out, d_out, m_out, v_out = refs[2 * np_:]
        for k in range(np_):
            @pl.when(pl.program_id(1) == k)
            def _():
                p_ref, r_ref = refs[2 * k], refs[2 * k + 1]
                g = p_ref[...].astype(f32)
                for j in range(3):
                    g = g + r_ref[j].astype(f32)
                d, mn, vn = _adamw(w_ref[...], g, m_ref[...], v_ref[...])
                g_out[...] = g
                d_out[...] = d
                m_out[...] = mn
                v_out[...] = vn

    tile = pl.BlockSpec((tr, wp), lambda i, k, q_ref: (i, k))
    in_specs, args = [], []
    for part, r2 in pieces:
        in_specs += [pl.BlockSpec((None, tr, wp), lambda i, k, q_ref: (q_ref[0], row_off + i, 0)),
                     pl.BlockSpec((3, tr, wp), lambda i, k, q_ref: (0, row_off + i, 0))]
        args += [part, r2]
    return pl.pallas_call(
        body,
        grid_spec=pltpu.PrefetchScalarGridSpec(
            num_scalar_prefetch=1, grid=(r // tr, np_), in_specs=in_specs + [tile, tile, tile],
            out_specs=[tile] * 4),
        out_shape=[SDS((r, wd), f32)] * 4, name=name, compiler_params=_cp(("arbitrary", "arbitrary"), 48),
    )(q_idx, *args, w, m, v)


_SMALL_ROWS = 24


def _pack_early(vec_rnn, st_out, dsr, db_in):
    def body(vr_ref, so_ref, dsr_ref, db_ref, sm_ref, sm2_ref):
        sm_ref[...] = jnp.zeros_like(sm_ref)
        sm2_ref[...] = jnp.zeros_like(sm2_ref)
        sm_ref[2:3, :] = vr_ref[3:4, :]
        sm_ref[3:6, :] = vr_ref[0:3, :]
        sm_ref[6:7, :] = so_ref[2:3, :]
        sm_ref[7:9, :] = so_ref[0:2, :]
        sm_ref[10:11, :] = so_ref[3:4, :]
        for h in range(N_KV):
            sm_ref[9:10, h * GROUP:(h + 1) * GROUP] = _colsum(dsr_ref[h])
        for j in range(6):
            sm_ref[16 + j:17 + j, :] = db_ref[0:1, j * D:(j + 1) * D]
        sm_ref[22:23, 0:D_IN - 6 * D] = db_ref[0:1, 6 * D:D_IN]
        for s in range(N_DEV):
            sm2_ref[s, 0:CONV_WIDTH, :] = vr_ref[4:8, s * 256:(s + 1) * 256]

    return pl.pallas_call(
        body, out_shape=[SDS((_SMALL_ROWS, D), f32), SDS((N_DEV, 8, 256), f32)],
        name="pack_early", compiler_params=_cp(None),
    )(vec_rnn, st_out, dsr, db_in)


def _pack_late(st_emb, dmeta):
    def body(se_ref, dm_ref, sm_ref, sm2_ref):
        sm_ref[...] = se_ref[...]
        for s in range(N_DEV):
            sm2_ref[s] = dm_ref[:, s * 256:(s + 1) * 256]

    return pl.pallas_call(
        body, out_shape=[SDS((8, D), f32), SDS((N_DEV, N_META, 256), f32)],
        name="pack_late", compiler_params=_cp(None),
    )(st_emb, dmeta)


def _small_allreduce(sm, sm2):
    def body(sm_ref, sm2_ref, o_ref, o2_ref, buf, buf2, send_sems, recv_sems):
        x, y, c = _place()
        me = _dev(x, y, c)
        copies = []
        for f in range(1, N_DEV):
            fx, fy, fc = f // 4, (f // 2) % 2, f % 2
            peer = ((x + fx) % 2, (y + fy) % 2, (c + fc) % 2)
            for t, (src, dst) in enumerate(((sm_ref, buf), (sm2_ref, buf2))):
                k = 2 * (f - 1) + t
                copies.append(pltpu.make_async_remote_copy(
                    src_ref=src, dst_ref=dst.at[me], send_sem=send_sems.at[k], recv_sem=recv_sems.at[k],
                    device_id=peer, device_id_type=MESH))
        for cp in copies:
            cp.start()
        buf[me] = sm_ref[...]
        buf2[me] = sm2_ref[...]
        for cp in copies:
            cp.wait()
        acc, acc2 = buf[0], buf2[0]
        for e in range(1, N_DEV):
            acc, acc2 = acc + buf[e], acc2 + buf2[e]
        o_ref[...] = acc
        o2_ref[...] = acc2

    return pl.pallas_call(
        body, in_specs=[_VMEM, _VMEM], out_specs=[_VMEM, _VMEM],
        out_shape=[SDS(sm.shape, f32), SDS(sm2.shape, f32)],
        scratch_shapes=[pltpu.VMEM((N_DEV, *sm.shape), f32), pltpu.VMEM((N_DEV, *sm2.shape), f32),
                        pltpu.SemaphoreType.DMA((14,)), pltpu.SemaphoreType.DMA((14,))],
        name="small_allreduce",
    )(sm, sm2)


_SMALL_ROW_OF = {"ln_emb_g": 0, "ln_emb_b": 1, "conv_b": 2, "b_ra": 3, "b_ri": 4, "lru_lambda": 5, "b_o": 6,
                 "ln_g": 7, "ln_b": 8}
_SMALL_NAMES = ["ln_emb_g", "ln_emb_b", "conv_b", "b_ra", "b_ri", "lru_lambda", "b_o", "ln_g", "ln_b",
                "sinks", "b_in", "meta_tokens", "conv_w"]


def _small_update(me_idx, early, late, wmv):
    n_fixed = 7

    def in_order(me, own_ref, land_ref):
        acc = None
        for e in range(N_DEV):
            term = jnp.where(me == e, own_ref[...], land_ref[e])
            acc = term if acc is None else acc + term
        return acc

    def body(*refs):
        me_ref, own_ref, land_ref, cown_ref, cland_ref, late_ref, meta_ref = refs[:n_fixed]
        ins = refs[n_fixed:n_fixed + 3 * len(_SMALL_NAMES)]
        outs = refs[n_fixed + 3 * len(_SMALL_NAMES):]
        me = me_ref[0]
        sm = in_order(me, own_ref, land_ref)
        conv = in_order(me, cown_ref, cland_ref)

        def grad_of(name):
            if name in ("ln_emb_g", "ln_emb_b"):
                r = _SMALL_ROW_OF[name]
                return late_ref[r:r + 1, :]
            if name in _SMALL_ROW_OF:
                r = _SMALL_ROW_OF[name]
                return sm[r:r + 1, :]
            if name == "sinks":
                return sm[9:10, 0:N_KV * GROUP]
            if name == "b_in":
                return jnp.concatenate([sm[16 + j:17 + j, :] for j in range(7)], axis=1)[:, :D_IN]
            if name == "meta_tokens":
                return meta_ref[...]
            return conv[0:CONV_WIDTH, :]

        for i, name in enumerate(_SMALL_NAMES):
            w_ref, m_ref, v_ref = ins[3 * i:3 * i + 3]
            g = grad_of(name)
            d, mn, vn = _adamw(w_ref[...], g, m_ref[...], v_ref[...])
            outs[4 * i][...] = g
            outs[4 * i + 1][...] = d
            outs[4 * i + 2][...] = mn
            outs[4 * i + 3][...] = vn
        outs[-1][...] = jnp.broadcast_to(jnp.sum(sm[10:11, :], axis=1, keepdims=True), (8, 128))

    args, out_shape = [me_idx, *early, *late], []
    for name in _SMALL_NAMES:
        args += list(wmv[name])
        out_shape += [SDS(wmv[name][0].shape, f32)] * 4
    out_shape.append(SDS((8, 128), f32))
    res = pl.pallas_call(
        body, out_shape=out_shape, in_specs=[pl.BlockSpec(memory_space=pltpu.SMEM)] + [_VMEM] * (len(args) - 1),
        name="small_update", compiler_params=_cp(None))(*args)
    return {name: tuple(res[4 * i:4 * i + 4]) for i, name in enumerate(_SMALL_NAMES)}, res[-1][0, 0]


_WEIGHTS = ["meta_tokens", "ln_emb_g", "ln_emb_b", "w_in", "b_in", "conv_w", "conv_b", "w_ra", "b_ra", "w_ri",
            "b_ri", "lru_lambda", "sinks", "w_rnn_out", "w_attn_out", "w_o", "b_o", "ln_g", "ln_b"]
_SMALL_2D = {"meta_tokens": (N_META, 256), "conv_w": (CONV_WIDTH, 256), "b_in": (1, D_IN), "sinks": (1, N_KV * GROUP)}


def kernel(x, meta_tokens, ln_emb_g, ln_emb_b, w_in, b_in, conv_w, conv_b, w_ra, b_ra, w_ri, b_ri, lru_lambda, sinks, w_rnn_out, w_attn_out, w_o, b_o, ln_g, ln_b, loss_target, m_meta_tokens, m_ln_emb_g, m_ln_emb_b, m_w_in, m_b_in, m_conv_w, m_conv_b, m_w_ra, m_b_ra, m_w_ri, m_b_ri, m_lru_lambda, m_sinks, m_w_rnn_out, m_w_attn_out, m_w_o, m_b_o, m_ln_g, m_ln_b, v_meta_tokens, v_ln_emb_g, v_ln_emb_b, v_w_in, v_b_in, v_conv_w, v_conv_b, v_w_ra, v_b_ra, v_w_ri, v_b_ri, v_lru_lambda, v_sinks, v_w_rnn_out, v_w_attn_out, v_w_o, v_b_o, v_ln_g, v_ln_b):
    w = dict(meta_tokens=meta_tokens, ln_emb_g=ln_emb_g, ln_emb_b=ln_emb_b, w_in=w_in, b_in=b_in, conv_w=conv_w,
             conv_b=conv_b, w_ra=w_ra, b_ra=b_ra, w_ri=w_ri, b_ri=b_ri, lru_lambda=lru_lambda, sinks=sinks,
             w_rnn_out=w_rnn_out, w_attn_out=w_attn_out, w_o=w_o, b_o=b_o, ln_g=ln_g, ln_b=ln_b)
    m = dict(meta_tokens=m_meta_tokens, ln_emb_g=m_ln_emb_g, ln_emb_b=m_ln_emb_b, w_in=m_w_in, b_in=m_b_in,
             conv_w=m_conv_w, conv_b=m_conv_b, w_ra=m_w_ra, b_ra=m_b_ra, w_ri=m_w_ri, b_ri=m_b_ri,
             lru_lambda=m_lru_lambda, sinks=m_sinks, w_rnn_out=m_w_rnn_out, w_attn_out=m_w_attn_out, w_o=m_w_o,
             b_o=m_b_o, ln_g=m_ln_g, ln_b=m_ln_b)
    v = dict(meta_tokens=v_meta_tokens, ln_emb_g=v_ln_emb_g, ln_emb_b=v_ln_emb_b, w_in=v_w_in, b_in=v_b_in,
             conv_w=v_conv_w, conv_b=v_conv_b, w_ra=v_w_ra, b_ra=v_b_ra, w_ri=v_w_ri, b_ri=v_b_ri,
             lru_lambda=v_lru_lambda, sinks=v_sinks, w_rnn_out=v_w_rnn_out, w_attn_out=v_w_attn_out, w_o=v_w_o,
             b_o=v_b_o, ln_g=v_ln_g, ln_b=v_ln_b)
    px, py, pc = _place()
    as_idx = lambda t: jnp.reshape(t, (1,)).astype(jnp.int32)
    c_idx, q_idx, me_idx = as_idx(pc), as_idx(2 * px + py), as_idx(_dev(px, py, pc))

    w3_s, wrg_s, small_s = _cast_small(w_rnn_out, w_attn_out, w_o, w_ra, w_ri, meta_tokens, conv_w)
    vec = lambda name: w[name].reshape(1, -1)
    p = {k: vec(k) for k in ("ln_emb_g", "ln_emb_b", "b_in", "conv_b", "b_ra", "b_ri", "lru_lambda", "sinks",
                             "b_o", "ln_g", "ln_b")}
    w_in_t = lambda a: jnp.swapaxes(a, 1, 2).reshape(SHARD_IN, D)
    wg, wrg, smallw, w3_land = _all_gather([_cast_w_in(w_in_t(w_in)), wrg_s, small_s], [w3_s])
    w3_pending = _split_start(_copies_direct(True), 7, [w3_s], [w3_land], smallw, "gather_w3_start")
    w_full = wg.reshape(D_IN, D)

    zero = w3_pending[4][0:1, 0:1]
    h32, hb = _ln_emb(x, smallw, p["ln_emb_g"], p["ln_emb_b"])
    z = _mm(hb, w_full, nt=True, bias=p["b_in"] + zero, name="mm_z")
    s = _step_attn(_step_rnn(h32, hb, z, wrg, smallw, p, zero), p, zero)
    w3 = _split_wait(_copies_direct(True), *w3_pending[:4], s["lse"], "gather_w3_wait")[1][0]
    t = _step_merge(s, loss_target, w3, p)

    big = {}
    two_d = lambda name: (w[name].shape[-2], w[name].shape[-1])
    proj = ("w_o", "w_rnn_out", "w_attn_out")
    g_proj = [t[k].reshape(N_DEV, 256, D) for k in ("g_wo", "g_wrnn", "g_wattn")]
    g_pending = _split_start(_copies_direct(False), 7, g_proj, [lax.empty((N_DEV, 256, D), bf16) for _ in proj],
                             p["b_o"], "reduce_proj_start")
    u = _step_backward(s, t, wrg, smallw, p, p["conv_b"] + g_pending[4][0:1, 0:1])

    def siblings_start(gs, dep, tag):
        return _split_start(_copies_siblings, 4, gs, [lax.empty((4, *g.shape[1:]), bf16) for g in gs], dep,
                            "reduce_siblings_start_" + tag)

    def chips_start(gs, r1, dep, tag):
        parts = [_pair_sum(g, r, c_idx, "pair_sum_%s%d" % (tag, i)) for i, (g, r) in enumerate(zip(gs, r1))]
        return _split_start(_copies_chips, 3, parts, [lax.empty((3, *q.shape[1:]), bf16) for q in parts], dep,
                            "reduce_chips_start_" + tag)

    g_a, dz, db_in = _mm_dwin_parts(s["hb"], u["dz_parts"])
    shards = lambda g: g.reshape(N_DEV, SHARD_IN, W_IN_HALF)
    sib_a = siblings_start([shards(g_a), u["g_wrg"].reshape(N_DEV, 2 * RNN_BLOCK, RNN_BLOCK)], db_in, "a")
    g_proj, g_land = _split_wait(_copies_direct(False), *g_pending[:4], sib_a[4], "reduce_proj_wait")
    for i, name in enumerate(proj):
        res = _adamw_direct(g_proj[i], g_land[i], me_idx, w[name].reshape(two_d(name)), m[name].reshape(two_d(name)),
                            v[name].reshape(two_d(name)), "adamw_" + name)
        big[name] = tuple(r.reshape(w[name].shape) for r in res)
    chp_a = chips_start(*_split_wait(_copies_siblings, *sib_a[:4], big["w_attn_out"][3], "reduce_siblings_wait_a"),
                        db_in, "a")
    g_b = _mm_dwin(s["hb"], dz, chp_a[4])
    sib_b = siblings_start([shards(g_b)], db_in, "b")
    sm_e = _pack_early(u["vec_rnn"], t["st_out"], u["dsr"], db_in)
    early = _split_start(_copies_direct(True), 7, list(sm_e),
                         [lax.empty((N_DEV, *a.shape), f32) for a in sm_e], sib_b[4], "small_early_start")
    dh_lo = _mm_dh(dz, w_full, early[4], 0)
    chp_b = chips_start(*_split_wait(_copies_siblings, *sib_b[:4], dh_lo, "reduce_siblings_wait_b"), db_in, "b")
    dh_hi = _mm_dh(dz, w_full, chp_b[4], 1)
    u.update(_step_input_grad(dh_lo, dh_hi, t["du32"], x, smallw, p))
    sm_l, meta_l = _small_allreduce(*_pack_late(u["st_emb"], u["dmeta"]))
    (sm_own, conv_own), (sm_land, conv_land) = _split_wait(_copies_direct(True), *early[:4], sm_l, "small_early_wait")
    me = _dev(px, py, pc)
    mine = lambda a, axis: lax.dynamic_index_in_dim(a, me, axis, keepdims=False)
    two = lambda name, t: t.reshape(_SMALL_2D.get(name, (1, D)))
    small, loss = _small_update(me_idx, (sm_own, sm_land, mine(conv_own, 0), mine(conv_land, 1)),
                                (sm_l, mine(meta_l, 0)),
                                {k: (two(k, w[k]), two(k, m[k]), two(k, v[k])) for k in _SMALL_NAMES})

    parts_a, r2_a = _split_wait(_copies_chips, *chp_a[:4], small["b_in"][3], "reduce_chips_wait_a")
    parts_b, r2_b = _split_wait(_copies_chips, *chp_b[:4], small["b_in"][2], "reduce_chips_wait_b")
    res = _adamw_big([(parts_a[0], r2_a[0]), (parts_b[0], r2_b[0])], q_idx, w_in_t(w["w_in"]), w_in_t(m["w_in"]),
                     w_in_t(v["w_in"]), "adamw_w_in")
    big["w_in"] = tuple(jnp.swapaxes(r.reshape(1, SHARD_IN, D), 1, 2) for r in res)
    for i, name in enumerate(("w_ra", "w_ri")):
        sq = (RNN_BLOCK, RNN_BLOCK)
        res = _adamw_big([(parts_a[1], r2_a[1])], q_idx, w[name].reshape(sq), m[name].reshape(sq), v[name].reshape(sq),
                         "adamw_" + name, row_off=i)
        big[name] = tuple(r.reshape(w[name].shape) for r in res)
    res = dict(big)
    for k in _SMALL_NAMES:
        res[k] = tuple(t.reshape(w[k].shape) for t in small[k])

    outs = [loss, u["grad_x"]]
    for j in range(4):
        outs += [res[k][j] for k in _WEIGHTS]
    return tuple(outs)
```

```python
import functools

import jax
import jax.numpy as jnp
from jax import lax
from jax.experimental import pallas as pl
from jax.experimental.pallas import tpu as pltpu

f32, bf16 = jnp.float32, jnp.bfloat16
SDS = jax.ShapeDtypeStruct

N_DEV = 8
D = 2048
N_META = 16
BLK = 128
ROW0 = BLK - N_META
N_RNN_BLOCKS = 8
RNN_BLOCK = D // N_RNN_BLOCKS
CONV_WIDTH = 4
LRU_C = 8.0
HEAD_DIM = 64
N_KV = 4
GROUP = 8
HALF = HEAD_DIM // 2
ROPE_THETA = 10000.0
NEG_INF = -1e30
LN_EPS = 1e-5
ALPHA = 2.0 ** 0.25
D_IN = 12800
SHARD_IN = D_IN // N_DEV
W_IN_HALF = D // 2
OFF_GR, OFF_Q, OFF_K, OFF_V, OFF_GA, OFF_G = 2048, 4096, 6144, 6400, 6656, 8704
ADAM_LR, ADAM_B1, ADAM_B2, ADAM_EPS, ADAM_WD, ADAM_STEP = 1e-3, 0.9, 0.999, 1e-8, 0.01, 10
VMEM_LIMIT_MB = 56
MESH = pl.DeviceIdType.MESH


def _cp(sem=None, vmem_mb=40):
    return pltpu.CompilerParams(dimension_semantics=sem, vmem_limit_bytes=vmem_mb * 2 ** 20)


def _row_chunk(m):
    best = 16
    for c in range(16, 641, 16):
        if m % c == 0:
            best = c
    return best


def _sigmoid(x):
    return 1.0 / (1.0 + jnp.exp(-x))


def _silu_and_grad(x):
    s = _sigmoid(x)
    return x * s, s * (1.0 + x * (1.0 - s))


def _log_sigmoid(x):
    return jnp.minimum(x, 0.0) - jnp.log1p(jnp.exp(-jnp.abs(x)))


def _ln_rows(v, g, b):
    mu = jnp.mean(v, axis=-1, keepdims=True)
    c = v - mu
    var = jnp.mean(c * c, axis=-1, keepdims=True)
    rstd = lax.rsqrt(var + LN_EPS)
    xhat = c * rstd
    return xhat * g + b, xhat, rstd


def _ln_rows_bwd(dy, g, xhat, rstd):
    dxh = dy * g
    m1 = jnp.mean(dxh, axis=-1, keepdims=True)
    m2 = jnp.mean(dxh * xhat, axis=-1, keepdims=True)
    return rstd * (dxh - m1 - xhat * m2)


def _colsum(v):
    return jnp.sum(v, axis=0, keepdims=True)


def _dot(a, b):
    return jnp.dot(a, b, preferred_element_type=f32)


def _dot_nt(a, b):
    return lax.dot_general(a, b, (((1,), (1,)), ((), ())), preferred_element_type=f32)


def _dot_tn(a, b):
    return lax.dot_general(a, b, (((0,), (0,)), ((), ())), preferred_element_type=f32)


def _meta_full(sw_ref):
    return jnp.concatenate([sw_ref[s, 0:N_META, :] for s in range(N_DEV)], axis=1)


def _ln_emb(x, smallw, g_e, b_e):
    seq = x.shape[1]
    rows = seq + BLK
    nb = rows // BLK

    def body(x_ref, sw_ref, g_ref, b_ref, h32_ref, hb_ref):
        i = pl.program_id(0)
        g, b = g_ref[...], b_ref[...]

        def emit(blk):
            h32_ref[...] = blk
            hb_ref[...] = blk.astype(bf16)

        @pl.when(i == 0)
        def _():
            hm = _ln_rows(_meta_full(sw_ref), g, b)[0]
            emit(jnp.concatenate([jnp.zeros((ROW0, D), f32), hm], axis=0))

        @pl.when(i > 0)
        def _():
            emit(_ln_rows(x_ref[0], g, b)[0])

    return pl.pallas_call(
        body, grid=(nb,),
        in_specs=[pl.BlockSpec((1, BLK, D), lambda i: (0, jnp.maximum(i - 1, 0), 0)),
                  pl.BlockSpec((N_DEV, 24, 256), lambda i: (0, 0, 0)),
                  pl.BlockSpec((1, D), lambda i: (0, 0)),
                  pl.BlockSpec((1, D), lambda i: (0, 0))],
        out_specs=[pl.BlockSpec((BLK, D), lambda i: (i, 0)),
                   pl.BlockSpec((BLK, D), lambda i: (i, 0))],
        out_shape=[SDS((rows, D), f32), SDS((rows, D), bf16)],
        name="ln_emb", compiler_params=_cp(("arbitrary",)),
    )(x, smallw, g_e, b_e)


def _ln_emb_bwd(dh_lo, dh_hi, du32, x, smallw, g_e, after):
    seq = x.shape[1]
    rows = seq + BLK
    nb = rows // BLK

    def body(dlo_ref, dhi_ref, du_ref, x_ref, sw_ref, g_ref, after_ref, gx_ref, dmeta_ref, st_ref):
        i = pl.program_id(0)
        g = g_ref[...]
        dht = jnp.concatenate([dlo_ref[...], dhi_ref[...]], axis=1) + ALPHA * du_ref[...]

        @pl.when(i == 0)
        def _():
            v = jnp.concatenate([jnp.zeros((ROW0, D), f32), _meta_full(sw_ref)], axis=0)
            valid = lax.broadcasted_iota(jnp.int32, (BLK, 1), 0) >= ROW0
            d = jnp.where(valid, dht, 0.0)
            _, xhat, rstd = _ln_rows(v, g, 0.0)
            dv = _ln_rows_bwd(d, g, xhat, rstd)
            dmeta_ref[...] = dv[ROW0:, :]
            st_ref[...] = jnp.concatenate([_colsum(d * xhat), _colsum(d), jnp.zeros((6, D), f32)], axis=0)

        @pl.when(i > 0)
        def _():
            _, xhat, rstd = _ln_rows(x_ref[0], g, 0.0)
            gx_ref[0] = _ln_rows_bwd(dht, g, xhat, rstd)
            st_ref[0:1, :] += _colsum(dht * xhat)
            st_ref[1:2, :] += _colsum(dht)

    return pl.pallas_call(
        body, grid=(nb,),
        in_specs=[pl.BlockSpec((BLK, W_IN_HALF), lambda i: (i, 0)),
                  pl.BlockSpec((BLK, W_IN_HALF), lambda i: (i, 0)),
                  pl.BlockSpec((BLK, D), lambda i: (i, 0)),
                  pl.BlockSpec((1, BLK, D), lambda i: (0, jnp.maximum(i - 1, 0), 0)),
                  pl.BlockSpec((N_DEV, 24, 256), lambda i: (0, 0, 0)),
                  pl.BlockSpec((1, D), lambda i: (0, 0)),
                  pl.BlockSpec(memory_space=pl.ANY)],
        out_specs=[pl.BlockSpec((1, BLK, D), lambda i: (0, jnp.maximum(i - 1, 0), 0)),
                   pl.BlockSpec((N_META, D), lambda i: (0, 0)),
                   pl.BlockSpec((8, D), lambda i: (0, 0))],
        out_shape=[SDS((1, seq, D), f32), SDS((N_META, D), f32), SDS((8, D), f32)],
        name="ln_emb_bwd", compiler_params=_cp(("arbitrary",)),
    )(dh_lo, dh_hi, du32, x, smallw, g_e, after)


def _mm(a, b, *, name, nt=False, sel=None, bias=None, out_dtype=f32, tn=512):
    m, k = a.shape
    cm = _row_chunk(m)
    stacked = sel is not None
    n = D if stacked else (b.shape[0] if nt else b.shape[1])
    am = m
    if stacked and nt:
        b_spec = pl.BlockSpec((tn // 256, None, 256, D), lambda j, i: (j, sel, 0, 0))
    elif stacked:
        b_spec = pl.BlockSpec((N_DEV, None, 256, tn), lambda j, i: (0, sel, 0, j))
    elif nt:
        b_spec = pl.BlockSpec((tn, k), lambda j, i: (j, 0))
    else:
        b_spec = pl.BlockSpec((k, tn), lambda j, i: (0, j))
    in_specs = [pl.BlockSpec((am, k), lambda j, i: (i, 0)), b_spec]
    args = [a, b]
    if bias is not None:
        in_specs.append(pl.BlockSpec((1, tn), lambda j, i: (0, j)))
        args.append(bias)

    def body(*refs):
        a_ref, b_ref, o_ref = refs[0], refs[1], refs[-1]
        bm = b_ref[...]
        if stacked:
            bm = bm.reshape((tn, D) if nt else (D, tn))
        for c in range(am // cm):
            acc = (_dot_nt if nt else _dot)(a_ref[c * cm:(c + 1) * cm, :], bm)
            if bias is not None:
                acc = acc + refs[2][...]
            o_ref[c * cm:(c + 1) * cm, :] = acc.astype(out_dtype)

    return pl.pallas_call(
        body, grid=(n // tn, m // am), in_specs=in_specs,
        out_specs=pl.BlockSpec((am, tn), lambda j, i: (i, j)),
        out_shape=SDS((m, n), out_dtype), name=name, compiler_params=_cp(("arbitrary", "arbitrary"), 48),
    )(*args)


def _mm_dh(dz, w_t, after, half):
    rows = dz.shape[0]
    tk, tn = 2560, 512
    nt = W_IN_HALF // tn
    cm = _row_chunk(rows)

    def body(a_ref, w_ref, after_ref, o_ref):
        kk = pl.program_id(1)
        for c in range(rows // cm):
            acc = _dot(a_ref[c * cm:(c + 1) * cm, :], w_ref[...])

            @pl.when(kk == 0)
            def _():
                o_ref[c * cm:(c + 1) * cm, :] = acc

            @pl.when(kk > 0)
            def _():
                o_ref[c * cm:(c + 1) * cm, :] += acc

    return pl.pallas_call(
        body, grid=(nt, D_IN // tk),
        in_specs=[pl.BlockSpec((rows, tk), lambda j, kk: (0, kk)),
                  pl.BlockSpec((tk, tn), lambda j, kk: (kk, half * nt + j)),
                  pl.BlockSpec(memory_space=pl.ANY)],
        out_specs=pl.BlockSpec((rows, tn), lambda j, kk: (0, j)),
        out_shape=SDS((rows, W_IN_HALF), f32), name="mm_dh_%d" % half,
        compiler_params=_cp(("arbitrary", "arbitrary"), 48),
    )(dz, w_t, after)


def _mm_dwin_parts(hb, parts):
    rows = hb.shape[0]
    tc = 512
    edges = [0]
    for _, w in parts:
        edges.append(edges[-1] + w // tc)

    def body(*refs):
        h_ref, (o_ref, dz_ref, db_ref) = refs[len(parts)], refs[len(parts) + 1:]
        j = pl.program_id(0)
        for p_ref, lo, hi in zip(refs, edges[:-1], edges[1:]):
            @pl.when((j >= lo) & (j < hi))
            def _():
                o_ref[...] = _dot_tn(p_ref[...], h_ref[...]).astype(bf16)
                dz_ref[...] = p_ref[...]

                def step(i, s):
                    blk = p_ref[pl.ds(pl.multiple_of(i * BLK, BLK), BLK), :].astype(f32)
                    return s + blk.reshape(BLK // 8, 8, tc).sum(axis=0)
                s = lax.fori_loop(0, rows // BLK, step, jnp.zeros((8, tc), f32))
                db_ref[...] = jnp.broadcast_to(_colsum(s), (8, tc))

    in_specs = [pl.BlockSpec((rows, tc), lambda j, lo=lo, hi=hi: (0, jnp.clip(j - lo, 0, hi - lo - 1)))
                for lo, hi in zip(edges[:-1], edges[1:])]
    return pl.pallas_call(
        body, grid=(D_IN // tc,),
        in_specs=in_specs + [pl.BlockSpec((rows, W_IN_HALF), lambda j: (0, 0))],
        out_specs=[pl.BlockSpec((tc, W_IN_HALF), lambda j: (j, 0)), pl.BlockSpec((rows, tc), lambda j: (0, j)),
                   pl.BlockSpec((8, tc), lambda j: (0, j))],
        out_shape=[SDS((D_IN, W_IN_HALF), bf16), SDS((rows, D_IN), bf16), SDS((8, D_IN), f32)],
        name="mm_dwin_0", compiler_params=_cp(("arbitrary",), VMEM_LIMIT_MB),
    )(*[a for a, _ in parts], hb)


def _mm_dwin(hb, dz, after):
    rows = dz.shape[0]
    tc = 640

    def body(dz_ref, h_ref, after_ref, o_ref):
        o_ref[...] = _dot_tn(dz_ref[...], h_ref[...]).astype(bf16)

    return pl.pallas_call(
        body, grid=(D_IN // tc,),
        in_specs=[pl.BlockSpec((rows, tc), lambda j: (0, j)),
                  pl.BlockSpec((rows, W_IN_HALF), lambda j: (0, 1)),
                  pl.BlockSpec(memory_space=pl.ANY)],
        out_specs=pl.BlockSpec((tc, W_IN_HALF), lambda j: (j, 0)),
        out_shape=SDS((D_IN, W_IN_HALF), bf16),
        name="mm_dwin_1", compiler_params=_cp(("arbitrary",), 48),
    )(dz, hb, after)


SCAN_ROWS = 32


def _scan8(a, b, reverse):
    idx = lax.broadcasted_iota(jnp.int32, a.shape, 0)
    for s in (1, 2, 4):
        sh = 8 - s if reverse else s
        a_sh, b_sh = pltpu.roll(a, sh, 0), pltpu.roll(b, sh, 0)
        m = (idx < 8 - s) if reverse else (idx >= s)
        b = jnp.where(m, a * b_sh + b, b)
        a = jnp.where(m, a * a_sh, a)
    return a, b


def _shift_rows(prev8, cur, k):
    ext = jnp.concatenate([prev8, cur], axis=0)
    return pltpu.roll(ext, k, 0)[8:, :]


def _gates(xc, w_ra, b_ra, w_ri, b_ri, ls):
    xb = xc.astype(bf16)
    r = _sigmoid(_dot(xb, w_ra) + b_ra)
    ig = _sigmoid(_dot(xb, w_ri) + b_ri)
    la = LRU_C * r * ls
    a = jnp.exp(la)
    mult = jnp.sqrt(jnp.tanh(-la) * (1.0 + a * a))
    return xb, r, ig, a, mult


_RNN_IN_SPECS = lambda rows: [
    pl.BlockSpec((1, 24, 256), lambda n: (n, 0, 0)),
    pl.BlockSpec((1, RNN_BLOCK), lambda n: (0, n)),
    pl.BlockSpec((N_DEV, 2, None, 32, RNN_BLOCK), lambda n: (0, 0, n, 0, 0)),
    pl.BlockSpec((1, RNN_BLOCK), lambda n: (0, n)),
    pl.BlockSpec((1, RNN_BLOCK), lambda n: (0, n)),
    pl.BlockSpec((1, RNN_BLOCK), lambda n: (0, n)),
]


def _rnn_fwd(z, smallw, conv_b, wrg, b_ra, b_ri, lam):
    rows = z.shape[0]
    nb = rows // BLK
    col = lambda off: pl.BlockSpec((rows, RNN_BLOCK), lambda n: (0, off // RNN_BLOCK + n))

    def body(xr_ref, gr_ref, sw_ref, cb_ref, w_ref, bra_ref, bri_ref, lam_ref, xc_ref, hr_ref, ya_ref, yat_ref, a_s):
        cw = sw_ref[0, N_META:24, :]
        cb = cb_ref[...]
        w_ra = w_ref[:, 0].reshape(RNN_BLOCK, RNN_BLOCK)
        w_ri = w_ref[:, 1].reshape(RNN_BLOCK, RNN_BLOCK)
        b_ra_v, b_ri_v = bra_ref[...], bri_ref[...]
        ls = _log_sigmoid(lam_ref[...])
        rid = lax.broadcasted_iota(jnp.int32, (BLK, 1), 0)

        def blk_step(i, carry):
            r0 = pl.multiple_of(i * BLK, BLK)
            grow = rid + r0
            valid = grow >= ROW0
            cur = jnp.where(valid, xr_ref[pl.ds(r0, BLK), :], 0.0)
            prev8 = xr_ref[pl.ds(pl.multiple_of(jnp.maximum(r0 - 8, 0), 8), 8), :] * (i > 0).astype(f32)
            xc = cb + cw[0:1] * cur
            for k in range(1, CONV_WIDTH):
                xc = xc + cw[k:k + 1] * _shift_rows(prev8, cur, k)
            xc_ref[pl.ds(r0, BLK), :] = xc
            _, _, ig, a, mult = _gates(xc, w_ra, b_ra_v, w_ri, b_ri_v, ls)
            mult = jnp.where(grow == ROW0, 1.0, mult)
            a_s[pl.ds(r0, BLK), :] = a
            hr_ref[pl.ds(r0, BLK), :] = jnp.where(valid, mult * ig * xc, 0.0)
            return carry

        lax.fori_loop(0, nb, blk_step, 0)

        def scan_step(j, carry):
            r0 = pl.multiple_of(j * SCAN_ROWS, SCAN_ROWS)
            tiles = [_scan8(a_s[pl.ds(r0 + 8 * k, 8), :], hr_ref[pl.ds(r0 + 8 * k, 8), :], False)
                     for k in range(SCAN_ROWS // 8)]
            for k, (a, b) in enumerate(tiles):
                h = b + a * carry
                hr_ref[pl.ds(r0 + 8 * k, 8), :] = h
                carry = jnp.broadcast_to(h[7:8, :], (8, RNN_BLOCK))
            return carry

        lax.fori_loop(0, rows // SCAN_ROWS, scan_step, jnp.zeros((8, RNN_BLOCK), f32))

        def gate_step(i, carry):
            r0 = pl.multiple_of(i * BLK, BLK)
            ya_ref[pl.ds(r0, BLK), :] = (hr_ref[pl.ds(r0, BLK), :]
                                         * _silu_and_grad(gr_ref[pl.ds(r0, BLK), :])[0]).astype(bf16)
            return carry

        lax.fori_loop(0, nb, gate_step, 0)
        yat_ref[...] = ya_ref[...].astype(f32).T.astype(bf16)

    return pl.pallas_call(
        body, grid=(N_RNN_BLOCKS,),
        in_specs=[col(0), col(OFF_GR)] + _RNN_IN_SPECS(rows),
        out_specs=[pl.BlockSpec((rows, RNN_BLOCK), lambda n: (0, n))] * 3
                  + [pl.BlockSpec((RNN_BLOCK, rows), lambda n: (n, 0))],
        out_shape=[SDS((rows, D), f32), SDS((rows, D), f32), SDS((rows, D), bf16), SDS((D, rows), bf16)],
        scratch_shapes=[pltpu.VMEM((rows, RNN_BLOCK), f32)],
        name="rnn_fwd", compiler_params=_cp(("arbitrary",)),
    )(z, z, smallw, conv_b, wrg, b_ra, b_ri, lam)


def _rnn_bwd(dya, hr, xc, z, smallw, conv_b, wrg, b_ra, b_ri, lam):
    rows = z.shape[0]
    nb = rows // BLK
    col = lambda off: pl.BlockSpec((rows, RNN_BLOCK), lambda n: (0, off // RNN_BLOCK + n))
    blk = pl.BlockSpec((rows, RNN_BLOCK), lambda n: (0, n))

    def body(dya_ref, hr_ref, xc_ref, xr_ref, gr_ref, sw_ref, cb_ref, w_ref, bra_ref, bri_ref, lam_ref,
             dxr_ref, dgr_ref, dw_ref, vec_ref, a_s, lam_s, dxc_s, r_s, ig_s, mult_s, dw_s):
        cw = sw_ref[0, N_META:24, :]
        w_ra = w_ref[:, 0].reshape(RNN_BLOCK, RNN_BLOCK)
        w_ri = w_ref[:, 1].reshape(RNN_BLOCK, RNN_BLOCK)
        b_ra_v, b_ri_v = bra_ref[...], bri_ref[...]
        lam_v = lam_ref[...]
        ls = _log_sigmoid(lam_v)
        rid = lax.broadcasted_iota(jnp.int32, (BLK, 1), 0)
        zrow = jnp.zeros((1, RNN_BLOCK), f32)

        def p1(i, carry):
            r0 = pl.multiple_of(i * BLK, BLK)
            sl = pl.ds(r0, BLK)
            _, r, ig, a, mult = _gates(xc_ref[sl, :], w_ra, b_ra_v, w_ri, b_ri_v, ls)
            a_s[sl, :] = a
            r_s[sl, :] = r
            ig_s[sl, :] = ig
            mult_s[sl, :] = mult
            sg, dsg = _silu_and_grad(gr_ref[sl, :])
            d = dya_ref[sl, :]
            lam_s[sl, :] = d * sg
            dgr_ref[sl, :] = (d * hr_ref[sl, :] * dsg).astype(bf16)
            return carry

        lax.fori_loop(0, nb, p1, 0)

        def p2(jj, carry):
            r0 = pl.multiple_of((rows // SCAN_ROWS - 1 - jj) * SCAN_ROWS, SCAN_ROWS)
            idx = lax.broadcasted_iota(jnp.int32, (8, RNN_BLOCK), 0)
            tiles = []
            for k in range(SCAN_ROWS // 8):
                sl = pl.ds(r0 + 8 * k, 8)
                a, g = a_s[sl, :], lam_s[sl, :]
                tiles.append((g, *_scan8(a, a * g, True)))
            for k in reversed(range(SCAN_ROWS // 8)):
                g, ca, cb_ = tiles[k]
                mu = cb_ + ca * carry
                lam_s[pl.ds(r0 + 8 * k, 8), :] = g + jnp.where(idx < 7, pltpu.roll(mu, 7, 0), carry)
                carry = jnp.broadcast_to(mu[0:1, :], (8, RNN_BLOCK))
            return carry

        lax.fori_loop(0, rows // SCAN_ROWS, p2, jnp.zeros((8, RNN_BLOCK), f32))

        dw_s[...] = jnp.zeros_like(dw_s)

        def p3(i, carry):
            d_bra, d_bri, d_ls = carry
            r0 = pl.multiple_of(i * BLK, BLK)
            sl = pl.ds(r0, BLK)
            grow = rid + r0
            valid = grow >= ROW0
            first = grow == ROW0
            xcv = xc_ref[sl, :]
            xb = xcv.astype(bf16)
            r, ig, a = r_s[sl, :], ig_s[sl, :], a_s[sl, :]
            mult = jnp.where(first, 1.0, mult_s[sl, :])
            lam_t = lam_s[sl, :]
            du = jnp.where(valid, lam_t, 0.0)
            hprev = _shift_rows(hr_ref[pl.ds(pl.multiple_of(jnp.maximum(r0 - 8, 0), 8), 8), :] * (i > 0).astype(f32), hr_ref[sl, :], 1)
            da = lam_t * hprev
            dmult = jnp.where(first, 0.0, du * ig * xcv)
            di = du * mult * xcv
            dxc = du * mult * ig
            ratio = jnp.where(valid & jnp.logical_not(first), a * a / mult, 0.0)
            dla = da * a - dmult * ratio
            dpr = (dla * (LRU_C * ls)) * r * (1.0 - r)
            dpi = di * ig * (1.0 - ig)
            dprb, dpib = dpr.astype(bf16), dpi.astype(bf16)
            dw_s[0] += _dot_tn(xb, dprb)
            dw_s[1] += _dot_tn(xb, dpib)
            dxc_s[sl, :] = dxc + _dot_nt(dprb, w_ra) + _dot_nt(dpib, w_ri)
            return d_bra + _colsum(dpr), d_bri + _colsum(dpi), d_ls + _colsum(dla * (LRU_C * r))

        d_bra, d_bri, d_ls = lax.fori_loop(0, nb, p3, (zrow, zrow, zrow))

        def p4(i, carry):
            d_cb, d_w0, d_w1, d_w2, d_w3 = carry
            r0 = pl.multiple_of(i * BLK, BLK)
            sl = pl.ds(r0, BLK)
            grow = rid + r0
            valid = grow >= ROW0
            dxc = dxc_s[sl, :]
            nxt = dxc_s[pl.ds(pl.multiple_of(jnp.minimum(r0 + BLK, rows - 8), 8), 8), :] * (i < nb - 1).astype(f32)
            ext = jnp.concatenate([dxc, nxt], axis=0)
            dxr = cw[0:1] * dxc
            for k in range(1, CONV_WIDTH):
                dxr = dxr + cw[k:k + 1] * pltpu.roll(ext, BLK + 8 - k, 0)[:BLK, :]
            dxr_ref[sl, :] = jnp.where(valid, dxr, 0.0).astype(bf16)
            cur = jnp.where(valid, xr_ref[sl, :], 0.0)
            prev8 = xr_ref[pl.ds(pl.multiple_of(jnp.maximum(r0 - 8, 0), 8), 8), :] * (i > 0).astype(f32)
            dws = [d_w0 + _colsum(dxc * cur)]
            for k, acc in ((1, d_w1), (2, d_w2), (3, d_w3)):
                dws.append(acc + _colsum(dxc * _shift_rows(prev8, cur, k)))
            return (d_cb + _colsum(dxc), *dws)

        d_cb, d_w0, d_w1, d_w2, d_w3 = lax.fori_loop(0, nb, p4, (zrow,) * 5)

        d_lam = d_ls * _sigmoid(-lam_v)
        vec_ref[...] = jnp.concatenate([d_bra, d_bri, d_lam, d_cb, d_w0, d_w1, d_w2, d_w3], axis=0)
        dw_ref[:, 0] = dw_s[0].astype(bf16).reshape(N_DEV, 32, RNN_BLOCK)
        dw_ref[:, 1] = dw_s[1].astype(bf16).reshape(N_DEV, 32, RNN_BLOCK)

    return pl.pallas_call(
        body, grid=(N_RNN_BLOCKS,),
        in_specs=[blk, blk, blk, col(0), col(OFF_GR)] + _RNN_IN_SPECS(rows),
        out_specs=[blk, blk,
                   pl.BlockSpec((N_DEV, 2, None, 32, RNN_BLOCK), lambda n: (0, 0, n, 0, 0)),
                   pl.BlockSpec((8, RNN_BLOCK), lambda n: (0, n))],
        out_shape=[SDS((rows, D), bf16), SDS((rows, D), bf16),
                   SDS((N_DEV, 2, N_RNN_BLOCKS, 32, RNN_BLOCK), bf16), SDS((8, D), f32)],
        scratch_shapes=[pltpu.VMEM((rows, RNN_BLOCK), f32)] * 6 + [pltpu.VMEM((2, RNN_BLOCK, RNN_BLOCK), f32)],
        name="rnn_bwd", compiler_params=_cp(("arbitrary",), 48),
    )(dya, hr, xc, z, z, smallw, conv_b, wrg, b_ra, b_ri, lam)


def _rope_tables(rows):
    half = jnp.arange(HALF, dtype=f32)
    inv = ROPE_THETA ** (-half / HALF)
    pos = (jnp.arange(rows) - ROW0).astype(f32)
    ang = pos[:, None] * inv[None, :]
    cos, sin = jnp.cos(ang), jnp.sin(ang)
    cos128 = jnp.concatenate([cos, cos, cos, cos], axis=1)
    sin128 = jnp.concatenate([-sin, sin, -sin, sin], axis=1)
    return cos128, sin128


def _rope128(x, cos128, sin128):
    lane = lax.broadcasted_iota(jnp.int32, x.shape, 1)
    swapped = jnp.where(lane % HEAD_DIM < HALF, pltpu.roll(x, 128 - HALF, 1), pltpu.roll(x, HALF, 1))
    return x * cos128 + swapped * sin128


def _qkv_prep(z, cos128, sin128):
    rows = z.shape[0]

    def body(q_ref, kv_ref, c_ref, s_ref, qo_ref, ko_ref, vo_ref):
        c, s = c_ref[...], s_ref[...]
        for g in range(D // 128):
            qo_ref[:, g * 128:(g + 1) * 128] = (_rope128(q_ref[:, g * 128:(g + 1) * 128], c, s)
                                                * (HEAD_DIM ** -0.5)).astype(bf16)
        for g in range(2):
            kr = _rope128(kv_ref[:, g * 128:(g + 1) * 128], c, s)
            for j in range(2):
                ko_ref[2 * g + j] = kr[:, j * HEAD_DIM:(j + 1) * HEAD_DIM].astype(bf16)
        for h in range(N_KV):
            vo_ref[h] = kv_ref[:, 256 + h * HEAD_DIM:256 + (h + 1) * HEAD_DIM].astype(bf16)

    return pl.pallas_call(
        body, grid=(rows // BLK,),
        in_specs=[pl.BlockSpec((BLK, D), lambda i: (i, OFF_Q // D)),
                  pl.BlockSpec((BLK, 512), lambda i: (i, OFF_K // 512)),
                  pl.BlockSpec((BLK, 128), lambda i: (i, 0)),
                  pl.BlockSpec((BLK, 128), lambda i: (i, 0))],
        out_specs=[pl.BlockSpec((BLK, D), lambda i: (i, 0)),
                   pl.BlockSpec((N_KV, BLK, HEAD_DIM), lambda i: (0, i, 0)),
                   pl.BlockSpec((N_KV, BLK, HEAD_DIM), lambda i: (0, i, 0))],
        out_shape=[SDS((rows, D), bf16), SDS((N_KV, rows, HEAD_DIM), bf16), SDS((N_KV, rows, HEAD_DIM), bf16)],
        name="qkv_prep", compiler_params=_cp(("arbitrary",)),
    )(z, z, cos128, sin128)


def _attn_mask(n):
    qi = n * BLK + lax.broadcasted_iota(jnp.int32, (BLK, 2 * BLK + N_META), 0)
    c = lax.broadcasted_iota(jnp.int32, (BLK, 2 * BLK + N_META), 1)
    jb = (n - 1) * BLK + c
    band = (jb >= BLK) & (jb <= qi) & (qi - jb < BLK)
    meta = (ROW0 + c - 2 * BLK) <= qi
    return ((c < 2 * BLK) & band) | ((c >= 2 * BLK) & meta)


N_KEYS = 2 * BLK + N_META


def _stack_heads(t):
    return jnp.concatenate([t[:, g * HEAD_DIM:(g + 1) * HEAD_DIM] for g in range(GROUP)], axis=0)


def _sink_column(sink_ref, h):
    g = lax.broadcasted_iota(jnp.int32, (GROUP, 1, 1), 0)
    col = jnp.zeros((GROUP, 1, 1), f32)
    for j in range(GROUP):
        col = jnp.where(g == j, sink_ref[h * GROUP + j], col)
    return col


def _kv_specs(last):
    cl = lambda n: jnp.minimum(n, last)
    return [pl.BlockSpec((None, N_META, HEAD_DIM), lambda h, n: (h, ROW0 // N_META, 0)),
            pl.BlockSpec((None, BLK, HEAD_DIM), lambda h, n: (h, jnp.maximum(cl(n) - 1, 0), 0)),
            pl.BlockSpec((None, BLK, HEAD_DIM), lambda h, n: (h, cl(n), 0))]


def _attn_fwd(q_r, k_r, v_b, z, sinks):
    rows = q_r.shape[0]
    nb = rows // BLK

    def body(sink_ref, q_ref, km_ref, kp_ref, kc_ref, vm_ref, vp_ref, vc_ref, ga_ref, o_ref, yb_ref, ybt_ref, lse_ref):
        h, n = pl.program_id(0), pl.program_id(1)
        kk = jnp.concatenate([kp_ref[...], kc_ref[...], km_ref[...]], axis=0)
        vv = jnp.concatenate([vp_ref[...], vc_ref[...], vm_ref[...]], axis=0)
        q2 = _stack_heads(q_ref[...])
        s = jnp.where(_attn_mask(n)[None], _dot_nt(q2, kk).reshape(GROUP, BLK, N_KEYS), NEG_INF)
        sink = _sink_column(sink_ref, h)
        m = jnp.maximum(jnp.max(s, axis=-1, keepdims=True), sink)
        p = jnp.exp(s - m)
        den = jnp.sum(p, axis=-1, keepdims=True) + jnp.exp(sink - m)
        o2 = _dot((p / den).astype(bf16).reshape(GROUP * BLK, N_KEYS), vv)
        lse = m + jnp.log(den)
        for g in range(GROUP):
            o_ref[:, g * HEAD_DIM:(g + 1) * HEAD_DIM] = o2[g * BLK:(g + 1) * BLK]
            lse_ref[:, g:g + 1] = lse[g]
        yb = o_ref[...] * _silu_and_grad(ga_ref[...])[0]
        yb_ref[...] = yb.astype(bf16)
        ybt_ref[...] = yb.T.astype(bf16)

    tile = pl.BlockSpec((BLK, 512), lambda h, n: (n, h))
    return pl.pallas_call(
        body, grid=(N_KV, nb),
        in_specs=[pl.BlockSpec(memory_space=pltpu.SMEM), tile] + _kv_specs(nb - 1) + _kv_specs(nb - 1)
                 + [pl.BlockSpec((BLK, 512), lambda h, n: (n, OFF_GA // 512 + h))],
        out_specs=[tile, tile, pl.BlockSpec((512, BLK), lambda h, n: (h, n)),
                   pl.BlockSpec((None, BLK, GROUP), lambda h, n: (h, n, 0))],
        out_shape=[SDS((rows, D), f32), SDS((rows, D), bf16), SDS((D, rows), bf16),
                   SDS((N_KV, rows, GROUP), f32)],
        name="attn_fwd", compiler_params=_cp(("arbitrary", "arbitrary")),
    )(sinks, q_r, k_r, k_r, k_r, v_b, v_b, v_b, z)


def _attn_bwd(dyb, o32, lse, q_r, k_r, v_b, z, sinks):
    rows = q_r.shape[0]
    nb = rows // BLK
    cl = lambda n: jnp.minimum(n, nb - 1)

    def body(sink_ref, dyb_ref, o_ref, lse_ref, q_ref, km_ref, kp_ref, kc_ref, vm_ref, vp_ref, vc_ref, ga_ref,
             dq_ref, dga_ref, dk_ref, dv_ref, dkm_ref, dvm_ref, dsr_ref, ck_s, cv_s):
        h, n = pl.program_id(0), pl.program_id(1)

        @pl.when(n == 0)
        def _():
            dkm_ref[...] = jnp.zeros_like(dkm_ref)
            dvm_ref[...] = jnp.zeros_like(dvm_ref)
            ck_s[...] = jnp.zeros_like(ck_s)
            cv_s[...] = jnp.zeros_like(cv_s)

        @pl.when(n < nb)
        def _():
            kk = jnp.concatenate([kp_ref[...], kc_ref[...], km_ref[...]], axis=0)
            vv = jnp.concatenate([vp_ref[...], vc_ref[...], vm_ref[...]], axis=0)
            sg, dsg = _silu_and_grad(ga_ref[...])
            dyb_v = dyb_ref[...]
            o_v = o_ref[...]
            dga_ref[...] = (dyb_v * o_v * dsg).astype(bf16)
            q2 = _stack_heads(q_ref[...])
            do2 = _stack_heads(dyb_v * sg)
            lse_v = lse_ref[...]
            lse = jnp.concatenate([lse_v[:, g:g + 1] for g in range(GROUP)], axis=0).reshape(GROUP, BLK, 1)
            delta = jnp.sum(do2 * _stack_heads(o_v), axis=-1, keepdims=True).reshape(GROUP, BLK, 1)
            s = jnp.where(_attn_mask(n)[None], _dot_nt(q2, kk).reshape(GROUP, BLK, N_KEYS), NEG_INF)
            p = jnp.exp(s - lse)
            do2b = do2.astype(bf16)
            ds = (p * (_dot_nt(do2b, vv).reshape(GROUP, BLK, N_KEYS) - delta)).astype(bf16)
            ds = ds.reshape(GROUP * BLK, N_KEYS)
            dsr = -jnp.exp(_sink_column(sink_ref, h) - lse) * delta
            dq2 = _dot(ds, kk)
            for g in range(GROUP):
                dq_ref[:, g * HEAD_DIM:(g + 1) * HEAD_DIM] = dq2[g * BLK:(g + 1) * BLK]
                dsr_ref[:, g:g + 1] = dsr[g]
            dkk = _dot_tn(ds, q2)
            dvv = _dot_tn(p.astype(bf16).reshape(GROUP * BLK, N_KEYS), do2b)
            dk_ref[...] = ck_s[...] + dkk[:BLK]
            dv_ref[...] = cv_s[...] + dvv[:BLK]
            ck_s[...] = dkk[BLK:2 * BLK]
            cv_s[...] = dvv[BLK:2 * BLK]
            dkm_ref[...] += dkk[2 * BLK:]
            dvm_ref[...] += dvv[2 * BLK:]

        @pl.when(n == nb)
        def _():
            dk_ref[...] = ck_s[...]
            dv_ref[...] = cv_s[...]

    tile = pl.BlockSpec((BLK, 512), lambda h, n: (cl(n), h))
    kvout = pl.BlockSpec((None, BLK, HEAD_DIM), lambda h, n: (h, jnp.maximum(n - 1, 0), 0))
    mout = pl.BlockSpec((None, N_META, HEAD_DIM), lambda h, n: (h, 0, 0))
    stat = pl.BlockSpec((None, BLK, GROUP), lambda h, n: (h, cl(n), 0))
    return pl.pallas_call(
        body, grid=(N_KV, nb + 1),
        in_specs=[pl.BlockSpec(memory_space=pltpu.SMEM), tile, tile, stat, tile] + _kv_specs(nb - 1)
                 + _kv_specs(nb - 1) + [pl.BlockSpec((BLK, 512), lambda h, n: (cl(n), OFF_GA // 512 + h))],
        out_specs=[tile, tile, kvout, kvout, mout, mout, stat],
        out_shape=[SDS((rows, D), f32), SDS((rows, D), bf16),
                   SDS((N_KV, rows, HEAD_DIM), f32), SDS((N_KV, rows, HEAD_DIM), f32),
                   SDS((N_KV, N_META, HEAD_DIM), f32), SDS((N_KV, N_META, HEAD_DIM), f32),
                   SDS((N_KV, rows, GROUP), f32)],
        scratch_shapes=[pltpu.VMEM((BLK, HEAD_DIM), f32), pltpu.VMEM((BLK, HEAD_DIM), f32)],
        name="attn_bwd", compiler_params=_cp(("arbitrary", "arbitrary")),
    )(sinks, dyb, o32, lse, q_r, k_r, k_r, k_r, v_b, v_b, v_b, z)


def _qkv_finish(dq, dk, dv, dkm, dvm, cos128, sin128):
    rows = dq.shape[0]

    def body(dq_ref, dk_ref, dv_ref, dkm_ref, dvm_ref, c_ref, s_ref, oq_ref, okv_ref):
        first = (pl.program_id(0) == 0).astype(f32)
        c, s = c_ref[...], -s_ref[...]
        for g in range(D // 128):
            oq_ref[:, g * 128:(g + 1) * 128] = (_rope128(dq_ref[:, g * 128:(g + 1) * 128], c, s)
                                                * (HEAD_DIM ** -0.5)).astype(bf16)
        pad = jnp.zeros((ROW0, HEAD_DIM), f32)
        ks = [dk_ref[h] + first * jnp.concatenate([pad, dkm_ref[h]], axis=0) for h in range(N_KV)]
        vs = [dv_ref[h] + first * jnp.concatenate([pad, dvm_ref[h]], axis=0) for h in range(N_KV)]
        for g in range(2):
            kp = jnp.concatenate([ks[2 * g], ks[2 * g + 1]], axis=1)
            okv_ref[:, g * 128:(g + 1) * 128] = _rope128(kp, c, s).astype(bf16)
            okv_ref[:, 256 + g * 128:256 + (g + 1) * 128] = jnp.concatenate([vs[2 * g], vs[2 * g + 1]], axis=1).astype(bf16)

    kv = pl.BlockSpec((N_KV, BLK, HEAD_DIM), lambda i: (0, i, 0))
    mt = pl.BlockSpec((N_KV, N_META, HEAD_DIM), lambda i: (0, 0, 0))
    return pl.pallas_call(
        body, grid=(rows // BLK,),
        in_specs=[pl.BlockSpec((BLK, D), lambda i: (i, 0)), kv, kv, mt, mt,
                  pl.BlockSpec((BLK, 128), lambda i: (i, 0)), pl.BlockSpec((BLK, 128), lambda i: (i, 0))],
        out_specs=[pl.BlockSpec((BLK, D), lambda i: (i, 0)), pl.BlockSpec((BLK, 512), lambda i: (i, 0))],
        out_shape=[SDS((rows, D), bf16), SDS((rows, 512), bf16)],
        name="qkv_finish", compiler_params=_cp(("arbitrary",)),
    )(dq, dk, dv, dkm, dvm, cos128, sin128)


_TW = 512


def _mix_specs(rows):
    tr = _row_chunk(rows)
    tile = pl.BlockSpec((tr, _TW), lambda i, j: (i, j))
    ga = pl.BlockSpec((tr, _TW), lambda i, j: (i, OFF_G // _TW + j))
    gb = pl.BlockSpec((tr, _TW), lambda i, j: (i, (OFF_G + D) // _TW + j))
    return (rows // tr, D // _TW), tile, ga, gb


def _mix_fwd(y_a, y_b, z):
    rows = y_a.shape[0]
    tw = 256
    col = lambda off: pl.BlockSpec((rows, tw), lambda j: (0, off // tw + j))

    def body(ya_ref, yb_ref, ga_ref, gb_ref, o_ref, ot_ref):
        mixed = (_sigmoid(ga_ref[...]) * ya_ref[...].astype(f32)
                 + _sigmoid(gb_ref[...]) * yb_ref[...].astype(f32))
        o_ref[...] = mixed.astype(bf16)
        ot_ref[...] = mixed.T.astype(bf16)

    return pl.pallas_call(
        body, grid=(D // tw,), in_specs=[col(0), col(0), col(OFF_G), col(OFF_G + D)],
        out_specs=[col(0), pl.BlockSpec((tw, rows), lambda j: (j, 0))],
        out_shape=[SDS((rows, D), bf16), SDS((D, rows), bf16)],
        name="mix_fwd", compiler_params=_cp(("arbitrary",)),
    )(y_a, y_b, z, z)


def _mix_bwd(dmixed, y_a, y_b, z):
    rows = y_a.shape[0]
    grid, _mix_tile, _mix_ga, _mix_gb = _mix_specs(rows)

    def body(dm_ref, ya_ref, yb_ref, ga_ref, gb_ref, dya_ref, dyb_ref, dga_ref, dgb_ref):
        dm = dm_ref[...].astype(f32)
        sa, sb = _sigmoid(ga_ref[...]), _sigmoid(gb_ref[...])
        dya_ref[...] = (dm * sa).astype(bf16)
        dyb_ref[...] = (dm * sb).astype(bf16)
        dga_ref[...] = (dm * ya_ref[...].astype(f32) * sa * (1.0 - sa)).astype(bf16)
        dgb_ref[...] = (dm * yb_ref[...].astype(f32) * sb * (1.0 - sb)).astype(bf16)

    return pl.pallas_call(
        body, grid=grid, in_specs=[_mix_tile, _mix_tile, _mix_tile, _mix_ga, _mix_gb],
        out_specs=[_mix_tile] * 4, out_shape=[SDS((rows, D), bf16)] * 4,
        name="mix_bwd", compiler_params=_cp(("arbitrary", "arbitrary")),
    )(dmixed, y_a, y_b, z, z)


def _final_ln(out32, h32, tgt, ln_g, ln_b):
    rows = out32.shape[0]

    def body(o_ref, h_ref, t_ref, g_ref, b_ref, du_ref, dub_ref, st_ref):
        i = pl.program_id(0)
        g = g_ref[...]
        y, xhat, rstd = _ln_rows(ALPHA * h_ref[...] + o_ref[...], g, b_ref[...])
        e = jnp.where(i > 0, y - t_ref[0], 0.0)
        dy = e * (1.0 / D)
        du = _ln_rows_bwd(dy, g, xhat, rstd)
        du_ref[...] = du
        dub_ref[...] = du.astype(bf16)
        st = jnp.concatenate([_colsum(dy * xhat), _colsum(dy), _colsum(du), _colsum(e * e) * (0.5 / D),
                              jnp.zeros((4, D), f32)], axis=0)

        @pl.when(i == 0)
        def _():
            st_ref[...] = st

        @pl.when(i > 0)
        def _():
            st_ref[...] += st

    row = pl.BlockSpec((BLK, D), lambda i: (i, 0))
    vec = pl.BlockSpec((1, D), lambda i: (0, 0))
    return pl.pallas_call(
        body, grid=(rows // BLK,),
        in_specs=[row, row, pl.BlockSpec((1, BLK, D), lambda i: (0, jnp.maximum(i - 1, 0), 0)), vec, vec],
        out_specs=[row, row, pl.BlockSpec((8, D), lambda i: (0, 0))],
        out_shape=[SDS((rows, D), f32), SDS((rows, D), bf16), SDS((8, D), f32)],
        name="final_ln", compiler_params=_cp(("arbitrary",)),
    )(out32, h32, tgt, ln_g, ln_b)


def _step_rnn(h32, hb, z, wrg, smallw, p, zero):
    rows = z.shape[0]
    cos128, sin128 = _rope_tables(rows)
    cos128 = cos128 + zero
    xc, hr, ya, ya_t = _rnn_fwd(z, smallw, p["conv_b"] + zero, wrg, p["b_ra"], p["b_ri"], p["lru_lambda"])
    q_r, k_r, v_b = _qkv_prep(z, cos128, sin128)
    return dict(cos128=cos128, sin128=sin128, h32=h32, hb=hb, z=z, xc=xc, hr=hr, ya=ya, ya_t=ya_t,
                q_r=q_r, k_r=k_r, v_b=v_b)


def _step_attn(s, p, zero):
    sinks = p["sinks"].reshape(N_KV * GROUP) + zero[0]
    o32, yb, yb_t, lse = _attn_fwd(s["q_r"], s["k_r"], s["v_b"], s["z"], sinks)
    return dict(s, sinks=sinks, o32=o32, yb=yb, yb_t=yb_t, lse=lse)


def _step_merge(s, tgt, w3, p):
    ya, yb, z = s["ya"], s["yb"], s["z"]
    y_a = _mm(ya, w3, sel=0, out_dtype=bf16, name="mm_ya")
    y_b = _mm(yb, w3, sel=1, out_dtype=bf16, name="mm_yb")
    mixed, mixed_t = _mix_fwd(y_a, y_b, z)
    out32 = _mm(mixed, w3, sel=2, bias=p["b_o"], name="mm_out")
    du32, dub, st_out = _final_ln(out32, s["h32"], tgt, p["ln_g"], p["ln_b"])

    g_wo = _mm(mixed_t, dub, out_dtype=bf16, name="mm_dwo")
    dmixed = _mm(dub, w3, sel=2, nt=True, out_dtype=bf16, name="mm_dmixed")
    dya_b, dyb_b, dma, dmb = _mix_bwd(dmixed, y_a, y_b, z)
    g_wrnn = _mm(s["ya_t"], dya_b, out_dtype=bf16, name="mm_dwrnn")
    g_wattn = _mm(s["yb_t"], dyb_b, out_dtype=bf16, name="mm_dwattn")
    dya = _mm(dya_b, w3, sel=0, nt=True, name="mm_dya")
    dyb = _mm(dyb_b, w3, sel=1, nt=True, name="mm_dyb")
    return dict(du32=du32, st_out=st_out, dma=dma, dmb=dmb, dya=dya, dyb=dyb, g_wo=g_wo, g_wrnn=g_wrnn,
                g_wattn=g_wattn)


def _step_backward(s, t, wrg, smallw, p, conv_b):
    z = s["z"]
    dxr, dgr, g_wrg, vec_rnn = _rnn_bwd(t["dya"], s["hr"], s["xc"], z, smallw, conv_b, wrg, p["b_ra"], p["b_ri"],
                                        p["lru_lambda"])
    dq_r, dga, dk, dv, dkm, dvm, dsr = _attn_bwd(t["dyb"], s["o32"], s["lse"], s["q_r"], s["k_r"], s["v_b"], z,
                                                 s["sinks"])
    dq, dkv = _qkv_finish(dq_r, dk, dv, dkm, dvm, s["cos128"], s["sin128"])
    dz_parts = [(dxr, D), (dgr, D), (dq, D), (dkv, 512), (dga, D), (t["dma"], D), (t["dmb"], D)]
    return dict(vec_rnn=vec_rnn, dsr=dsr, g_wrg=g_wrg, dz_parts=dz_parts)


def _step_input_grad(dh_lo, dh_hi, du32, x, smallw, p, after):
    grad_x, dmeta, st_emb = _ln_emb_bwd(dh_lo, dh_hi, du32, x, smallw, p["ln_emb_g"], after)
    return dict(grad_x=grad_x, dmeta=dmeta, st_emb=st_emb)


_ANY = pl.BlockSpec(memory_space=pl.ANY)
_VMEM = pl.BlockSpec(memory_space=pltpu.VMEM)


def _place():
    x, y, c = lax.axis_index("x"), lax.axis_index("y"), lax.axis_index("c")
    return x, y, c


def _dev(px, py, pc):
    return 4 * px + 2 * py + pc


def _tile_rows(r):
    return max(t for t in range(16, 321, 16) if r % t == 0) if r > 320 else r


def _cast_w_in(w_in_t):
    tm = _tile_rows(SHARD_IN)

    def body(i_ref, o_ref):
        o_ref[...] = i_ref[...].astype(bf16)

    return pl.pallas_call(
        body, grid=(SHARD_IN // tm,),
        in_specs=[pl.BlockSpec((tm, D), lambda i: (i, 0))],
        out_specs=pl.BlockSpec((tm, D), lambda i: (i, 0)),
        out_shape=SDS((SHARD_IN, D), bf16), name="cast_w_in", compiler_params=_cp(("arbitrary",)),
    )(w_in_t)


def _cast_small(w_rnn_out, w_attn_out, w_o, w_ra, w_ri, meta, conv_w):
    def body(a_ref, b_ref, c_ref, ra_ref, ri_ref, m_ref, cw_ref, w3_ref, wrg_ref, sw_ref):
        w3_ref[0] = a_ref[0].astype(bf16)
        w3_ref[1] = b_ref[0].astype(bf16)
        w3_ref[2] = c_ref[0].astype(bf16)
        wrg_ref[0] = ra_ref[0].astype(bf16)
        wrg_ref[1] = ri_ref[0].astype(bf16)
        sw_ref[...] = jnp.concatenate([m_ref[...], cw_ref[0], jnp.zeros((4, 256), f32)], axis=0)

    return pl.pallas_call(
        body,
        out_shape=[SDS((3, 256, D), bf16), SDS((2, N_RNN_BLOCKS, 32, RNN_BLOCK), bf16), SDS((24, 256), f32)],
        name="cast_small", compiler_params=_cp(None),
    )(w_rnn_out, w_attn_out, w_o, w_ra, w_ri, meta, conv_w)


def _all_gather(shards, later):
    n = len(shards)
    nl = len(later)

    def body(*refs):
        ins, outs = refs[:n], refs[n + nl:2 * n + nl]
        send_sems, recv_sems, local_sems = refs[2 * (n + nl):]
        x, y, c = _place()
        me, sibling = (x, y, c), (x, y, 1 - c)
        chips = [(1 - x, y), (x, 1 - y), (1 - x, 1 - y)]

        def copy(a, k, block, to, src=None):
            dst = outs[a].at[_dev(*block)]
            return pltpu.make_async_remote_copy(
                src_ref=dst if src is None else src, dst_ref=dst,
                send_sem=send_sems.at[a * 7 + k], recv_sem=recv_sems.at[a * 7 + k],
                device_id=to, device_id_type=MESH)

        all_ins, all_outs = refs[:n + nl], refs[n + nl:2 * (n + nl)]
        mine = [pltpu.make_async_copy(all_ins[a], all_outs[a].at[_dev(*me)], local_sems.at[a]) for a in range(n + nl)]
        for cp in mine:
            cp.start()
        first = []
        for a in range(n):
            first.append(copy(a, 0, me, sibling, src=ins[a]))
            first += [copy(a, 1 + j, me, (*chip, c), src=ins[a]) for j, chip in enumerate(chips)]
        for cp in first:
            cp.start()
        passed = []
        for a in range(n):
            for j, chip in enumerate(chips):
                copy(a, 1 + j, (*chip, c), me).wait_recv()
                cp = copy(a, 4 + j, (*chip, c), sibling)
                cp.start()
                passed.append(cp)
        for a in range(n):
            copy(a, 0, sibling, me).wait_recv()
            for j, chip in enumerate(chips):
                copy(a, 4 + j, (*chip, 1 - c), me).wait_recv()
        for cp in first + passed:
            cp.wait_send()
        for cp in mine:
            cp.wait()

    return pl.pallas_call(
        body, in_specs=[_ANY] * (n + nl), out_specs=[_ANY] * (n + nl),
        out_shape=[SDS((N_DEV, *s.shape), s.dtype) for s in (*shards, *later)],
        scratch_shapes=[pltpu.SemaphoreType.DMA((7 * n,)), pltpu.SemaphoreType.DMA((7 * n,)),
                        pltpu.SemaphoreType.DMA((n + nl,))],
        name="all_gather_weights",
    )(*shards, *later)


def _gather_small(shard):
    def body(s_ref, o_ref, send_sems, recv_sems):
        x, y, c = _place()
        me = _dev(x, y, c)
        copies = []
        for k, (fx, fy, fc) in enumerate(_PEER_FLIPS):
            peer = ((x + fx) % 2, (y + fy) % 2, (c + fc) % 2)
            copies.append(_remote(s_ref, o_ref.at[me], send_sems, recv_sems, k, peer))
        for cp in copies:
            cp.start()
        o_ref[me] = s_ref[...]
        for cp in copies:
            cp.wait()

    return pl.pallas_call(
        body, in_specs=[_VMEM], out_specs=_VMEM, out_shape=SDS((N_DEV, *shard.shape), shard.dtype),
        scratch_shapes=[pltpu.SemaphoreType.DMA((7,)), pltpu.SemaphoreType.DMA((7,))],
        name="gather_small",
    )(shard)


def _gather_project(w_s, smalls, later, hb, b_in, order):
    arrays = (w_s, *smalls, *later)
    na, n = len(arrays), 1 + len(smalls)
    rows = hb.shape[0]
    cm = _row_chunk(rows)
    nm = rows // cm
    pair = 2 * SHARD_IN

    def body(order_ref, *refs):
        ins, hb_ref, b_ref = refs[:na], refs[na], refs[na + 1]
        outs, z_ref = refs[na + 2:2 * na + 2], refs[2 * na + 2]
        wbuf, send_sems, recv_sems, local_sems, load_sems = refs[2 * na + 3:]
        k, mi = pl.program_id(0), pl.program_id(1)
        x, y, c = _place()
        me, sibling = (x, y, c), (x, y, 1 - c)
        chips = [(1 - x, y), (x, 1 - y), (1 - x, 1 - y)]

        def copy(a, kk, block, to, src=None):
            dst = outs[a].at[_dev(*block)]
            return pltpu.make_async_remote_copy(
                src_ref=dst if src is None else src, dst_ref=dst,
                send_sem=send_sems.at[a * 7 + kk], recv_sem=recv_sems.at[a * 7 + kk],
                device_id=to, device_id_type=MESH)

        mine = [pltpu.make_async_copy(ins[a], outs[a].at[_dev(*me)], local_sems.at[a]) for a in range(na)]

        def to_sibling():
            return [copy(a, 0, me, sibling, src=ins[a]) for a in range(n)]

        def to_chip(j):
            return [copy(a, 1 + j, me, (*chips[j], c), src=ins[a]) for a in range(n)]

        def load_pair(chip):
            cps = [pltpu.make_async_copy(outs[0].at[_dev(*chip, cc)], wbuf.at[pl.ds(cc * SHARD_IN, SHARD_IN)],
                                         load_sems.at[cc]) for cc in range(2)]
            for cp in cps:
                cp.start()
            for cp in cps:
                cp.wait()

        @pl.when((k == 0) & (mi == 0))
        def _():
            for cp in mine + to_sibling() + to_chip(0) + to_chip(1):
                cp.start()
            mine[0].wait()
            copy(0, 0, sibling, me).wait_recv()
            load_pair((x, y))

        for j, chip in enumerate(chips):
            @pl.when((k == j + 1) & (mi == 0))
            def _():
                for a in range(n):
                    copy(a, 1 + j, (*chip, c), me).wait_recv()
                    copy(a, 4 + j, (*chip, c), sibling).start()
                if j == 0:
                    for cp in to_chip(2):
                        cp.start()
                copy(0, 4 + j, (*chip, 1 - c), me).wait_recv()
                load_pair(chip)

        z_ref[...] = _dot_nt(hb_ref[...], wbuf[...]) + b_ref[...]

        @pl.when((k == len(chips)) & (mi == nm - 1))
        def _():
            for a in range(1, n):
                copy(a, 0, sibling, me).wait_recv()
                for j, chip in enumerate(chips):
                    copy(a, 4 + j, (*chip, 1 - c), me).wait_recv()
            for cp in to_sibling() + to_chip(0) + to_chip(1) + to_chip(2):
                cp.wait_send()
            for a in range(n):
                for j, chip in enumerate(chips):
                    copy(a, 4 + j, (*chip, c), sibling).wait_send()
            for cp in mine[1:]:
                cp.wait()

    res = pl.pallas_call(
        body,
        grid_spec=pltpu.PrefetchScalarGridSpec(
            num_scalar_prefetch=1, grid=(N_DEV // 2, nm),
            in_specs=[_ANY] * na + [pl.BlockSpec((cm, D), lambda k, i, o: (i, 0)),
                                    pl.BlockSpec((1, pair), lambda k, i, o: (0, o[k]))],
            out_specs=[_ANY] * na + [pl.BlockSpec((cm, pair), lambda k, i, o: (i, o[k]))],
            scratch_shapes=[pltpu.VMEM((pair, D), bf16), pltpu.SemaphoreType.DMA((7 * n,)),
                            pltpu.SemaphoreType.DMA((7 * n,)), pltpu.SemaphoreType.DMA((na,)),
                            pltpu.SemaphoreType.DMA((2,))]),
        out_shape=[SDS((N_DEV, *s.shape), s.dtype) for s in arrays] + [SDS((rows, D_IN), f32)],
        name="gather_project", compiler_params=_cp(("arbitrary", "arbitrary"), 48),
    )(order, *arrays, hb, b_in)
    return res[:na], res[na]


_HBM = pl.BlockSpec(memory_space=pltpu.HBM)
_SEM = pl.BlockSpec(memory_space=pltpu.SEMAPHORE)
_PEER_FLIPS = [(f // 4, (f // 2) % 2, f % 2) for f in range(1, N_DEV)]


def _remote(src, dst, send_sems, recv_sems, k, to):
    return pltpu.make_async_remote_copy(src_ref=src, dst_ref=dst, send_sem=send_sems.at[k], recv_sem=recv_sems.at[k],
                                        device_id=to, device_id_type=MESH)


def _copies_direct(same_src):
    def make(srcs, lands, send_sems, recv_sems):
        x, y, c = _place()
        me = _dev(x, y, c)
        out = []
        for a in range(len(srcs)):
            for k, (fx, fy, fc) in enumerate(_PEER_FLIPS):
                peer = ((x + fx) % 2, (y + fy) % 2, (c + fc) % 2)
                src = srcs[a] if same_src else srcs[a].at[_dev(*peer)]
                out.append(_remote(src, lands[a].at[me], send_sems, recv_sems, 7 * a + k, peer))
        return out
    return make


def _copies_gather_chips(srcs, lands, send_sems, recv_sems):
    x, y, c = _place()
    chips = [(1 - x, y), (x, 1 - y), (1 - x, 1 - y)]
    return [_remote(srcs[a], lands[a].at[_dev(x, y, c)], send_sems, recv_sems, 3 * a + j, (qx, qy, c))
            for a in range(len(srcs)) for j, (qx, qy) in enumerate(chips)]


def _copies_gather_sibling(srcs, lands, send_sems, recv_sems):
    x, y, c = _place()
    chips = [(x, y), (1 - x, y), (x, 1 - y), (1 - x, 1 - y)]
    return [_remote(lands[a].at[_dev(qx, qy, c)], lands[a].at[_dev(qx, qy, c)], send_sems, recv_sems, 4 * a + j,
                    (x, y, 1 - c))
            for a in range(len(srcs)) for j, (qx, qy) in enumerate(chips)]


def _copies_siblings(srcs, lands, send_sems, recv_sems):
    x, y, c = _place()
    return [_remote(srcs[a].at[2 * q + (1 - c)], lands[a].at[q], send_sems, recv_sems, 4 * a + q, (x, y, 1 - c))
            for a in range(len(srcs)) for q in range(4)]


def _copies_chips(srcs, lands, send_sems, recv_sems):
    x, y, c = _place()
    chips = [(1 - x, y), (x, 1 - y), (1 - x, 1 - y)]
    return [_remote(srcs[a].at[2 * qx + qy], lands[a].at[j], send_sems, recv_sems, 3 * a + j, (qx, qy, c))
            for a in range(len(srcs)) for j, (qx, qy) in enumerate(chips)]


def _split_start(make, per_array, srcs, lands, dep, name):
    n = len(srcs)

    def body(*refs):
        send_sems, recv_sems, token = refs[2 * n + 1], refs[2 * n + 2], refs[-1]
        for cp in make(refs[:n], refs[n:2 * n], send_sems, recv_sems):
            cp.start()
        token[...] = jnp.zeros_like(token)

    hbm = lambda t: pltpu.with_memory_space_constraint(t, pltpu.HBM)
    res = pl.pallas_call(
        body, name=name,
        out_shape=(pltpu.SemaphoreType.DMA((per_array * n,)), pltpu.SemaphoreType.DMA((per_array * n,)),
                   *[pltpu.HBM(t.shape, t.dtype) for t in (*srcs, *lands)], SDS((8, 128), f32)),
        in_specs=[_HBM] * (2 * n) + [_ANY], out_specs=(_SEM, _SEM, *([_HBM] * (2 * n)), _VMEM),
        input_output_aliases={i: 2 + i for i in range(2 * n)},
        compiler_params=pltpu.CompilerParams(has_side_effects=pltpu.SideEffectType.DATAFLOW_SIDE_EFFECTING),
    )(*[hbm(t) for t in (*srcs, *lands)], dep)
    return res[0], res[1], list(res[2:2 + n]), list(res[2 + n:2 + 2 * n]), res[-1]


def _split_wait(make, send_sems, recv_sems, srcs, lands, after, name):
    n = len(srcs)

    def body(*refs):
        for cp in make(refs[:n], refs[n:2 * n], refs[2 * n], refs[2 * n + 1]):
            cp.wait_send()
            cp.wait_recv()

    res = pl.pallas_call(
        body, name=name,
        out_shape=tuple(pltpu.HBM(t.shape, t.dtype) for t in (*srcs, *lands)),
        in_specs=[_HBM] * (2 * n) + [_SEM, _SEM, _ANY], out_specs=tuple([_HBM] * (2 * n)),
        input_output_aliases={i: i for i in range(2 * n)},
        compiler_params=pltpu.CompilerParams(has_side_effects=pltpu.SideEffectType.DATAFLOW_SIDE_EFFECTING),
    )(*srcs, *lands, send_sems, recv_sems, after)
    return list(res[:n]), list(res[n:])


def _adamw_direct(g, land, me_idx, w, m, v, name):
    r, wd = w.shape
    tr = min(r, 256)

    def body(me_ref, *refs):
        g_ref, peers = refs[0], refs[1:N_DEV]
        w_ref, m_ref, v_ref, g_out, d_out, m_out, v_out = refs[N_DEV:]
        gs = g_ref[...].astype(f32)
        for p_ref in peers:
            gs = gs + p_ref[...].astype(f32)
        d, mn, vn = _adamw(w_ref[...], gs, m_ref[...], v_ref[...])
        g_out[...] = gs
        d_out[...] = d
        m_out[...] = mn
        v_out[...] = vn

    tile = pl.BlockSpec((tr, wd), lambda i, me_ref: (i, 0))
    slot = lambda k: pl.BlockSpec((None, tr, wd), lambda i, me_ref: ((me_ref[0] + k) % N_DEV, i, 0))
    return pl.pallas_call(
        body,
        grid_spec=pltpu.PrefetchScalarGridSpec(
            num_scalar_prefetch=1, grid=(r // tr,),
            in_specs=[slot(0)] + [slot(k) for k in range(1, N_DEV)] + [tile, tile, tile],
            out_specs=[tile] * 4),
        out_shape=[SDS((r, wd), f32)] * 4, name=name, compiler_params=_cp(("arbitrary",), 48),
    )(me_idx, g, *([land] * (N_DEV - 1)), w, m, v)


def _pair_sum(g, r1, c_idx, name):
    _, r, w = g.shape
    tr = _tile_rows(r)

    def body(c_ref, g_ref, r_ref, o_ref):
        o_ref[...] = (g_ref[...].astype(f32) + r_ref[...].astype(f32)).astype(bf16)

    return pl.pallas_call(
        body,
        grid_spec=pltpu.PrefetchScalarGridSpec(
            num_scalar_prefetch=1, grid=(4, r // tr),
            in_specs=[pl.BlockSpec((None, tr, w), lambda q, i, c_ref: (2 * q + c_ref[0], i, 0)),
                      pl.BlockSpec((None, tr, w), lambda q, i, c_ref: (q, i, 0))],
            out_specs=pl.BlockSpec((None, tr, w), lambda q, i, c_ref: (q, i, 0))),
        out_shape=SDS((4, r, w), bf16), name=name, compiler_params=_cp(("arbitrary", "arbitrary")),
    )(c_idx, g, r1)


def _adamw(w, g, m, v):
    m = ADAM_B1 * m + (1.0 - ADAM_B1) * g
    v = ADAM_B2 * v + (1.0 - ADAM_B2) * (g * g)
    m_hat = m / (1.0 - ADAM_B1 ** ADAM_STEP)
    v_hat = v / (1.0 - ADAM_B2 ** ADAM_STEP)
    delta = -ADAM_LR * (m_hat / (jnp.sqrt(v_hat) + ADAM_EPS) + ADAM_WD * w)
    return delta, m, v


def _adamw_big(part, r2, q_idx, w, m, v, name, row_off=0, cols=(0, 1), prev=None):
    r, wd = w.shape
    tr = _tile_rows(r)
    k, ncol = cols
    wp = wd // ncol

    def body(q_ref, p_ref, r_ref, w_ref, m_ref, v_ref, *rest):
        g_out, d_out, m_out, v_out = rest[-4:]
        g = p_ref[...].astype(f32)
        for j in range(3):
            g = g + r_ref[j].astype(f32)
        d, mn, vn = _adamw(w_ref[...], g, m_ref[...], v_ref[...])
        g_out[...] = g
        d_out[...] = d
        m_out[...] = mn
        v_out[...] = vn

    tile = pl.BlockSpec((tr, wp), lambda i, q_ref: (i, k))
    prev = list(prev) if prev is not None else []
    return pl.pallas_call(
        body,
        grid_spec=pltpu.PrefetchScalarGridSpec(
            num_scalar_prefetch=1, grid=(r // tr,),
            in_specs=[pl.BlockSpec((None, tr, wp), lambda i, q_ref: (q_ref[0], row_off + i, 0)),
                      pl.BlockSpec((3, tr, wp), lambda i, q_ref: (0, row_off + i, 0)), tile, tile, tile]
                     + [pl.BlockSpec(memory_space=pl.ANY)] * len(prev),
            out_specs=[tile] * 4),
        out_shape=[SDS((r, wd), f32)] * 4, name=name,
        input_output_aliases={6 + i: i for i in range(len(prev))},
        compiler_params=_cp(("arbitrary",), 48),
    )(q_idx, part, r2, w, m, v, *prev)


_SMALL_ROWS = 24


def _pack_early(vec_rnn, st_out, dsr, db_in):
    def body(vr_ref, so_ref, dsr_ref, db_ref, sm_ref, sm2_ref):
        sm_ref[...] = jnp.zeros_like(sm_ref)
        sm2_ref[...] = jnp.zeros_like(sm2_ref)
        sm_ref[2:3, :] = vr_ref[3:4, :]
        sm_ref[3:6, :] = vr_ref[0:3, :]
        sm_ref[6:7, :] = so_ref[2:3, :]
        sm_ref[7:9, :] = so_ref[0:2, :]
        sm_ref[10:11, :] = so_ref[3:4, :]
        for h in range(N_KV):
            sm_ref[9:10, h * GROUP:(h + 1) * GROUP] = _colsum(dsr_ref[h])
        for j in range(6):
            sm_ref[16 + j:17 + j, :] = db_ref[0:1, j * D:(j + 1) * D]
        sm_ref[22:23, 0:D_IN - 6 * D] = db_ref[0:1, 6 * D:D_IN]
        for s in range(N_DEV):
            sm2_ref[s, 0:CONV_WIDTH, :] = vr_ref[4:8, s * 256:(s + 1) * 256]

    return pl.pallas_call(
        body, out_shape=[SDS((_SMALL_ROWS, D), f32), SDS((N_DEV, 8, 256), f32)],
        name="pack_early", compiler_params=_cp(None),
    )(vec_rnn, st_out, dsr, db_in)


def _pack_late(st_emb, dmeta):
    def body(se_ref, dm_ref, sm_ref, sm2_ref):
        sm_ref[...] = se_ref[...]
        for s in range(N_DEV):
            sm2_ref[s] = dm_ref[:, s * 256:(s + 1) * 256]

    return pl.pallas_call(
        body, out_shape=[SDS((8, D), f32), SDS((N_DEV, N_META, 256), f32)],
        name="pack_late", compiler_params=_cp(None),
    )(st_emb, dmeta)


def _small_allreduce(sm, sm2):
    def body(sm_ref, sm2_ref, o_ref, o2_ref, buf, buf2, send_sems, recv_sems):
        x, y, c = _place()
        me = _dev(x, y, c)
        copies = []
        for f in range(1, N_DEV):
            fx, fy, fc = f // 4, (f // 2) % 2, f % 2
            peer = ((x + fx) % 2, (y + fy) % 2, (c + fc) % 2)
            for t, (src, dst) in enumerate(((sm_ref, buf), (sm2_ref, buf2))):
                k = 2 * (f - 1) + t
                copies.append(pltpu.make_async_remote_copy(
                    src_ref=src, dst_ref=dst.at[me], send_sem=send_sems.at[k], recv_sem=recv_sems.at[k],
                    device_id=peer, device_id_type=MESH))
        for cp in copies:
            cp.start()
        buf[me] = sm_ref[...]
        buf2[me] = sm2_ref[...]
        for cp in copies:
            cp.wait()
        acc, acc2 = buf[0], buf2[0]
        for e in range(1, N_DEV):
            acc, acc2 = acc + buf[e], acc2 + buf2[e]
        o_ref[...] = acc
        o2_ref[...] = acc2

    return pl.pallas_call(
        body, in_specs=[_VMEM, _VMEM], out_specs=[_VMEM, _VMEM],
        out_shape=[SDS(sm.shape, f32), SDS(sm2.shape, f32)],
        scratch_shapes=[pltpu.VMEM((N_DEV, *sm.shape), f32), pltpu.VMEM((N_DEV, *sm2.shape), f32),
                        pltpu.SemaphoreType.DMA((14,)), pltpu.SemaphoreType.DMA((14,))],
        name="small_allreduce",
    )(sm, sm2)


_SMALL_ROW_OF = {"ln_emb_g": 0, "ln_emb_b": 1, "conv_b": 2, "b_ra": 3, "b_ri": 4, "lru_lambda": 5, "b_o": 6,
                 "ln_g": 7, "ln_b": 8}
_SMALL_NAMES = ["ln_emb_g", "ln_emb_b", "conv_b", "b_ra", "b_ri", "lru_lambda", "b_o", "ln_g", "ln_b",
                "sinks", "b_in", "meta_tokens", "conv_w"]


def _small_update(me_idx, early, late, wmv):
    n_fixed = 7

    def in_order(me, own_ref, land_ref):
        acc = None
        for e in range(N_DEV):
            term = jnp.where(me == e, own_ref[...], land_ref[e])
            acc = term if acc is None else acc + term
        return acc

    def body(*refs):
        me_ref, own_ref, land_ref, cown_ref, cland_ref, late_ref, meta_ref = refs[:n_fixed]
        ins = refs[n_fixed:n_fixed + 3 * len(_SMALL_NAMES)]
        outs = refs[n_fixed + 3 * len(_SMALL_NAMES):]
        me = me_ref[0]
        sm = in_order(me, own_ref, land_ref)
        conv = in_order(me, cown_ref, cland_ref)

        def grad_of(name):
            if name in ("ln_emb_g", "ln_emb_b"):
                r = _SMALL_ROW_OF[name]
                return late_ref[r:r + 1, :]
            if name in _SMALL_ROW_OF:
                r = _SMALL_ROW_OF[name]
                return sm[r:r + 1, :]
            if name == "sinks":
                return sm[9:10, 0:N_KV * GROUP]
            if name == "b_in":
                return jnp.concatenate([sm[16 + j:17 + j, :] for j in range(7)], axis=1)[:, :D_IN]
            if name == "meta_tokens":
                return meta_ref[...]
            return conv[0:CONV_WIDTH, :]

        for i, name in enumerate(_SMALL_NAMES):
            w_ref, m_ref, v_ref = ins[3 * i:3 * i + 3]
            g = grad_of(name)
            d, mn, vn = _adamw(w_ref[...], g, m_ref[...], v_ref[...])
            outs[4 * i][...] = g
            outs[4 * i + 1][...] = d
            outs[4 * i + 2][...] = mn
            outs[4 * i + 3][...] = vn
        outs[-1][...] = jnp.broadcast_to(jnp.sum(sm[10:11, :], axis=1, keepdims=True), (8, 128))

    args, out_shape = [me_idx, *early, *late], []
    for name in _SMALL_NAMES:
        args += list(wmv[name])
        out_shape += [SDS(wmv[name][0].shape, f32)] * 4
    out_shape.append(SDS((8, 128), f32))
    res = pl.pallas_call(
        body, out_shape=out_shape, in_specs=[pl.BlockSpec(memory_space=pltpu.SMEM)] + [_VMEM] * (len(args) - 1),
        name="small_update", compiler_params=_cp(None))(*args)
    return {name: tuple(res[4 * i:4 * i + 4]) for i, name in enumerate(_SMALL_NAMES)}, res[-1][0, 0]


_WEIGHTS = ["meta_tokens", "ln_emb_g", "ln_emb_b", "w_in", "b_in", "conv_w", "conv_b", "w_ra", "b_ra", "w_ri",
            "b_ri", "lru_lambda", "sinks", "w_rnn_out", "w_attn_out", "w_o", "b_o", "ln_g", "ln_b"]
_SMALL_2D = {"meta_tokens": (N_META, 256), "conv_w": (CONV_WIDTH, 256), "b_in": (1, D_IN), "sinks": (1, N_KV * GROUP)}


def kernel(x, meta_tokens, ln_emb_g, ln_emb_b, w_in, b_in, conv_w, conv_b, w_ra, b_ra, w_ri, b_ri, lru_lambda, sinks, w_rnn_out, w_attn_out, w_o, b_o, ln_g, ln_b, loss_target, m_meta_tokens, m_ln_emb_g, m_ln_emb_b, m_w_in, m_b_in, m_conv_w, m_conv_b, m_w_ra, m_b_ra, m_w_ri, m_b_ri, m_lru_lambda, m_sinks, m_w_rnn_out, m_w_attn_out, m_w_o, m_b_o, m_ln_g, m_ln_b, v_meta_tokens, v_ln_emb_g, v_ln_emb_b, v_w_in, v_b_in, v_conv_w, v_conv_b, v_w_ra, v_b_ra, v_w_ri, v_b_ri, v_lru_lambda, v_sinks, v_w_rnn_out, v_w_attn_out, v_w_o, v_b_o, v_ln_g, v_ln_b):
    w = dict(meta_tokens=meta_tokens, ln_emb_g=ln_emb_g, ln_emb_b=ln_emb_b, w_in=w_in, b_in=b_in, conv_w=conv_w,
             conv_b=conv_b, w_ra=w_ra, b_ra=b_ra, w_ri=w_ri, b_ri=b_ri, lru_lambda=lru_lambda, sinks=sinks,
             w_rnn_out=w_rnn_out, w_attn_out=w_attn_out, w_o=w_o, b_o=b_o, ln_g=ln_g, ln_b=ln_b)
    m = dict(meta_tokens=m_meta_tokens, ln_emb_g=m_ln_emb_g, ln_emb_b=m_ln_emb_b, w_in=m_w_in, b_in=m_b_in,
             conv_w=m_conv_w, conv_b=m_conv_b, w_ra=m_w_ra, b_ra=m_b_ra, w_ri=m_w_ri, b_ri=m_b_ri,
             lru_lambda=m_lru_lambda, sinks=m_sinks, w_rnn_out=m_w_rnn_out, w_attn_out=m_w_attn_out, w_o=m_w_o,
             b_o=m_b_o, ln_g=m_ln_g, ln_b=m_ln_b)
    v = dict(meta_tokens=v_meta_tokens, ln_emb_g=v_ln_emb_g, ln_emb_b=v_ln_emb_b, w_in=v_w_in, b_in=v_b_in,
             conv_w=v_conv_w, conv_b=v_conv_b, w_ra=v_w_ra, b_ra=v_b_ra, w_ri=v_w_ri, b_ri=v_b_ri,
             lru_lambda=v_lru_lambda, sinks=v_sinks, w_rnn_out=v_w_rnn_out, w_attn_out=v_w_attn_out, w_o=v_w_o,
             b_o=v_b_o, ln_g=v_ln_g, ln_b=v_ln_b)
    px, py, pc = _place()
    as_idx = lambda t: jnp.reshape(t, (1,)).astype(jnp.int32)
    c_idx, q_idx, me_idx = as_idx(pc), as_idx(2 * px + py), as_idx(_dev(px, py, pc))

    w3_s, wrg_s, small_s = _cast_small(w_rnn_out, w_attn_out, w_o, w_ra, w_ri, meta_tokens, conv_w)
    vec = lambda name: w[name].reshape(1, -1)
    p = {k: vec(k) for k in ("ln_emb_g", "ln_emb_b", "b_in", "conv_b", "b_ra", "b_ri", "lru_lambda", "sinks",
                             "b_o", "ln_g", "ln_b")}
    w_in_t = lambda a: jnp.swapaxes(a, 1, 2).reshape(SHARD_IN, D)
    wg, wrg, smallw, w3_land = _all_gather([_cast_w_in(w_in_t(w_in)), wrg_s, small_s], [w3_s])
    w3_pending = _split_start(_copies_direct(True), 7, [w3_s], [w3_land], smallw, "gather_w3_start")
    w_full = wg.reshape(D_IN, D)

    zero = w3_pending[4][0:1, 0:1]
    h32, hb = _ln_emb(x, smallw, p["ln_emb_g"], p["ln_emb_b"])
    z = _mm(hb, w_full, nt=True, bias=p["b_in"] + zero, name="mm_z")
    s = _step_attn(_step_rnn(h32, hb, z, wrg, smallw, p, zero), p, zero)
    w3 = _split_wait(_copies_direct(True), *w3_pending[:4], s["lse"], "gather_w3_wait")[1][0]
    t = _step_merge(s, loss_target, w3, p)

    big = {}
    two_d = lambda name: (w[name].shape[-2], w[name].shape[-1])
    proj = ("w_o", "w_rnn_out", "w_attn_out")
    g_proj = [t[k].reshape(N_DEV, 256, D) for k in ("g_wo", "g_wrnn", "g_wattn")]
    g_pending = _split_start(_copies_direct(False), 7, g_proj, [lax.empty((N_DEV, 256, D), bf16) for _ in proj],
                             p["b_o"], "reduce_proj_start")
    u = _step_backward(s, t, wrg, smallw, p, p["conv_b"] + g_pending[4][0:1, 0:1])

    def siblings_start(gs, dep, tag):
        return _split_start(_copies_siblings, 4, gs, [lax.empty((4, *g.shape[1:]), bf16) for g in gs], dep,
                            "reduce_siblings_start_" + tag)

    def chips_start(gs, r1, dep, tag):
        parts = [_pair_sum(g, r, c_idx, "pair_sum_%s%d" % (tag, i)) for i, (g, r) in enumerate(zip(gs, r1))]
        return _split_start(_copies_chips, 3, parts, [lax.empty((3, *q.shape[1:]), bf16) for q in parts], dep,
                            "reduce_chips_start_" + tag)

    g_a, dz, db_in = _mm_dwin_parts(s["hb"], u["dz_parts"])
    shards = lambda g: g.reshape(N_DEV, SHARD_IN, W_IN_HALF)
    sib_a = siblings_start([shards(g_a), u["g_wrg"].reshape(N_DEV, 2 * RNN_BLOCK, RNN_BLOCK)], db_in, "a")
    g_proj, g_land = _split_wait(_copies_direct(False), *g_pending[:4], sib_a[4], "reduce_proj_wait")
    for i, name in enumerate(proj):
        res = _adamw_direct(g_proj[i], g_land[i], me_idx, w[name].reshape(two_d(name)), m[name].reshape(two_d(name)),
                            v[name].reshape(two_d(name)), "adamw_" + name)
        big[name] = tuple(r.reshape(w[name].shape) for r in res)
    chp_a = chips_start(*_split_wait(_copies_siblings, *sib_a[:4], big["w_attn_out"][3], "reduce_siblings_wait_a"),
                        db_in, "a")
    g_b = _mm_dwin(s["hb"], dz, chp_a[4])
    sib_b = siblings_start([shards(g_b)], db_in, "b")
    sm_e = _pack_early(u["vec_rnn"], t["st_out"], u["dsr"], db_in)
    early = _split_start(_copies_direct(True), 7, list(sm_e),
                         [lax.empty((N_DEV, *a.shape), f32) for a in sm_e], sib_b[4], "small_early_start")
    dh_lo = _mm_dh(dz, w_full, early[4], 0)
    chp_b = chips_start(*_split_wait(_copies_siblings, *sib_b[:4], dh_lo, "reduce_siblings_wait_b"), db_in, "b")
    dh_hi = _mm_dh(dz, w_full, chp_b[4], 1)
    parts_a, r2_a = _split_wait(_copies_chips, *chp_a[:4], dh_hi, "reduce_chips_wait_a")
    w_in_res = _adamw_big(parts_a[0], r2_a[0], q_idx, w_in_t(w["w_in"]), w_in_t(m["w_in"]), w_in_t(v["w_in"]),
                          "adamw_w_in_a", cols=(0, 2))
    u.update(_step_input_grad(dh_lo, dh_hi, t["du32"], x, smallw, p, w_in_res[3]))
    sm_l, meta_l = _small_allreduce(*_pack_late(u["st_emb"], u["dmeta"]))
    (sm_own, conv_own), (sm_land, conv_land) = _split_wait(_copies_direct(True), *early[:4], sm_l, "small_early_wait")
    me = _dev(px, py, pc)
    mine = lambda a, axis: lax.dynamic_index_in_dim(a, me, axis, keepdims=False)
    two = lambda name, t: t.reshape(_SMALL_2D.get(name, (1, D)))
    small, loss = _small_update(me_idx, (sm_own, sm_land, mine(conv_own, 0), mine(conv_land, 1)),
                                (sm_l, mine(meta_l, 0)),
                                {k: (two(k, w[k]), two(k, m[k]), two(k, v[k])) for k in _SMALL_NAMES})

    parts_b, r2_b = _split_wait(_copies_chips, *chp_b[:4], small["b_in"][2], "reduce_chips_wait_b")
    res = _adamw_big(parts_b[0], r2_b[0], q_idx, w_in_t(w["w_in"]), w_in_t(m["w_in"]), w_in_t(v["w_in"]),
                     "adamw_w_in_b", cols=(1, 2), prev=w_in_res)
    big["w_in"] = tuple(jnp.swapaxes(r.reshape(1, SHARD_IN, D), 1, 2) for r in res)
    for i, name in enumerate(("w_ra", "w_ri")):
        sq = (RNN_BLOCK, RNN_BLOCK)
        res = _adamw_big(parts_a[1], r2_a[1], q_idx, w[name].reshape(sq), m[name].reshape(sq), v[name].reshape(sq),
                         "adamw_" + name, row_off=i)
        big[name] = tuple(r.reshape(w[name].shape) for r in res)
    res = dict(big)
    for k in _SMALL_NAMES:
        res[k] = tuple(t.reshape(w[k].shape) for t in small[k])

    outs = [loss, u["grad_x"]]
    for j in range(4):
        outs += [res[k][j] for k in _WEIGHTS]
    return tuple(outs)
```

```python
import jax
import jax.numpy as jnp
from jax import lax
from jax.experimental import pallas as pl
from jax.experimental.pallas import tpu as pltpu

f32, bf16 = jnp.float32, jnp.bfloat16
SDS = jax.ShapeDtypeStruct

N_DEV = 8
D = 2048
N_META = 16
BLK = 128
ROW0 = BLK - N_META
N_RNN_BLOCKS = 8
RNN_BLOCK = D // N_RNN_BLOCKS
CONV_WIDTH = 4
LRU_C = 8.0
HEAD_DIM = 64
N_KV = 4
GROUP = 8
HALF = HEAD_DIM // 2
ROPE_THETA = 10000.0
NEG_INF = -1e30
LN_EPS = 1e-5
ALPHA = 2.0 ** 0.25
D_IN = 12800
SHARD_IN = D_IN // N_DEV
W_IN_HALF = D // 2
OFF_GR, OFF_Q, OFF_K, OFF_V, OFF_GA, OFF_G = 2048, 4096, 6144, 6400, 6656, 8704
ADAM_LR, ADAM_B1, ADAM_B2, ADAM_EPS, ADAM_WD, ADAM_STEP = 1e-3, 0.9, 0.999, 1e-8, 0.01, 10
VMEM_LIMIT_MB = 56
MESH = pl.DeviceIdType.MESH


def _cp(sem=None, vmem_mb=40):
    return pltpu.CompilerParams(dimension_semantics=sem, vmem_limit_bytes=vmem_mb * 2 ** 20)


def _row_chunk(m):
    best = 16
    for c in range(16, 641, 16):
        if m % c == 0:
            best = c
    return best


def _sigmoid(x):
    return 1.0 / (1.0 + jnp.exp(-x))


def _silu_and_grad(x):
    s = _sigmoid(x)
    return x * s, s * (1.0 + x * (1.0 - s))


def _log_sigmoid(x):
    return jnp.minimum(x, 0.0) - jnp.log1p(jnp.exp(-jnp.abs(x)))


def _ln_rows(v, g, b):
    mu = jnp.mean(v, axis=-1, keepdims=True)
    c = v - mu
    var = jnp.mean(c * c, axis=-1, keepdims=True)
    rstd = lax.rsqrt(var + LN_EPS)
    xhat = c * rstd
    return xhat * g + b, xhat, rstd


def _ln_rows_bwd(dy, g, xhat, rstd):
    dxh = dy * g
    m1 = jnp.mean(dxh, axis=-1, keepdims=True)
    m2 = jnp.mean(dxh * xhat, axis=-1, keepdims=True)
    return rstd * (dxh - m1 - xhat * m2)


def _colsum(v):
    return jnp.sum(v, axis=0, keepdims=True)


def _dot(a, b):
    return jnp.dot(a, b, preferred_element_type=f32)


def _dot_nt(a, b):
    return lax.dot_general(a, b, (((1,), (1,)), ((), ())), preferred_element_type=f32)


def _dot_tn(a, b):
    return lax.dot_general(a, b, (((0,), (0,)), ((), ())), preferred_element_type=f32)


def _meta_full(sw_ref):
    return jnp.concatenate([sw_ref[s, 0:N_META, :] for s in range(N_DEV)], axis=1)


def _ln_emb(x, smallw, g_e, b_e):
    seq = x.shape[1]
    rows = seq + BLK
    nb = rows // BLK

    def body(x_ref, sw_ref, g_ref, b_ref, h32_ref, hb_ref):
        i = pl.program_id(0)
        g, b = g_ref[...], b_ref[...]

        def emit(blk):
            h32_ref[...] = blk
            hb_ref[...] = blk.astype(bf16)

        @pl.when(i == 0)
        def _():
            hm = _ln_rows(_meta_full(sw_ref), g, b)[0]
            emit(jnp.concatenate([jnp.zeros((ROW0, D), f32), hm], axis=0))

        @pl.when(i > 0)
        def _():
            emit(_ln_rows(x_ref[0], g, b)[0])

    return pl.pallas_call(
        body, grid=(nb,),
        in_specs=[pl.BlockSpec((1, BLK, D), lambda i: (0, jnp.maximum(i - 1, 0), 0)),
                  pl.BlockSpec((N_DEV, 24, 256), lambda i: (0, 0, 0)),
                  pl.BlockSpec((1, D), lambda i: (0, 0)),
                  pl.BlockSpec((1, D), lambda i: (0, 0))],
        out_specs=[pl.BlockSpec((BLK, D), lambda i: (i, 0)),
                   pl.BlockSpec((BLK, D), lambda i: (i, 0))],
        out_shape=[SDS((rows, D), f32), SDS((rows, D), bf16)],
        name="ln_emb", compiler_params=_cp(("arbitrary",)),
    )(x, smallw, g_e, b_e)


def _ln_emb_bwd(dh_lo, dh_hi, du32, x, smallw, g_e, after):
    seq = x.shape[1]
    rows = seq + BLK
    nb = rows // BLK

    def body(dlo_ref, dhi_ref, du_ref, x_ref, sw_ref, g_ref, after_ref, gx_ref, dmeta_ref, st_ref):
        i = pl.program_id(0)
        g = g_ref[...]
        dht = jnp.concatenate([dlo_ref[...], dhi_ref[...]], axis=1) + ALPHA * du_ref[...]

        @pl.when(i == 0)
        def _():
            v = jnp.concatenate([jnp.zeros((ROW0, D), f32), _meta_full(sw_ref)], axis=0)
            valid = lax.broadcasted_iota(jnp.int32, (BLK, 1), 0) >= ROW0
            d = jnp.where(valid, dht, 0.0)
            _, xhat, rstd = _ln_rows(v, g, 0.0)
            dv = _ln_rows_bwd(d, g, xhat, rstd)
            dmeta_ref[...] = dv[ROW0:, :]
            st_ref[...] = jnp.concatenate([_colsum(d * xhat), _colsum(d), jnp.zeros((6, D), f32)], axis=0)

        @pl.when(i > 0)
        def _():
            _, xhat, rstd = _ln_rows(x_ref[0], g, 0.0)
            gx_ref[0] = _ln_rows_bwd(dht, g, xhat, rstd)
            st_ref[0:1, :] += _colsum(dht * xhat)
            st_ref[1:2, :] += _colsum(dht)

    return pl.pallas_call(
        body, grid=(nb,),
        in_specs=[pl.BlockSpec((BLK, W_IN_HALF), lambda i: (i, 0)),
                  pl.BlockSpec((BLK, W_IN_HALF), lambda i: (i, 0)),
                  pl.BlockSpec((BLK, D), lambda i: (i, 0)),
                  pl.BlockSpec((1, BLK, D), lambda i: (0, jnp.maximum(i - 1, 0), 0)),
                  pl.BlockSpec((N_DEV, 24, 256), lambda i: (0, 0, 0)),
                  pl.BlockSpec((1, D), lambda i: (0, 0)),
                  pl.BlockSpec(memory_space=pl.ANY)],
        out_specs=[pl.BlockSpec((1, BLK, D), lambda i: (0, jnp.maximum(i - 1, 0), 0)),
                   pl.BlockSpec((N_META, D), lambda i: (0, 0)),
                   pl.BlockSpec((8, D), lambda i: (0, 0))],
        out_shape=[SDS((1, seq, D), f32), SDS((N_META, D), f32), SDS((8, D), f32)],
        name="ln_emb_bwd", compiler_params=_cp(("arbitrary",)),
    )(dh_lo, dh_hi, du32, x, smallw, g_e, after)


def _mm(a, b, *, name, nt=False, sel=None, bias=None, out_dtype=f32, tn=512):
    m, k = a.shape
    cm = _row_chunk(m)
    stacked = sel is not None
    n = D if stacked else (b.shape[0] if nt else b.shape[1])
    am = m
    if stacked and nt:
        b_spec = pl.BlockSpec((tn // 256, None, 256, D), lambda j, i: (j, sel, 0, 0))
    elif stacked:
        b_spec = pl.BlockSpec((N_DEV, None, 256, tn), lambda j, i: (0, sel, 0, j))
    elif nt:
        b_spec = pl.BlockSpec((tn, k), lambda j, i: (j, 0))
    else:
        b_spec = pl.BlockSpec((k, tn), lambda j, i: (0, j))
    in_specs = [pl.BlockSpec((am, k), lambda j, i: (i, 0)), b_spec]
    args = [a, b]
    if bias is not None:
        in_specs.append(pl.BlockSpec((1, tn), lambda j, i: (0, j)))
        args.append(bias)

    def body(*refs):
        a_ref, b_ref, o_ref = refs[0], refs[1], refs[-1]
        bm = b_ref[...]
        if stacked:
            bm = bm.reshape((tn, D) if nt else (D, tn))
        for c in range(am // cm):
            acc = (_dot_nt if nt else _dot)(a_ref[c * cm:(c + 1) * cm, :], bm)
            if bias is not None:
                acc = acc + refs[2][...]
            o_ref[c * cm:(c + 1) * cm, :] = acc.astype(out_dtype)

    return pl.pallas_call(
        body, grid=(n // tn, m // am), in_specs=in_specs,
        out_specs=pl.BlockSpec((am, tn), lambda j, i: (i, j)),
        out_shape=SDS((m, n), out_dtype), name=name, compiler_params=_cp(("arbitrary", "arbitrary"), 48),
    )(*args)


def _mm_dh(dz, w_t, after, half):
    rows = dz.shape[0]
    tn = 256
    nt = W_IN_HALF // tn
    cm = _row_chunk(rows)

    def body(a_ref, w_ref, after_ref, o_ref):
        o_ref[...] = _dot(a_ref[...], w_ref[...])

    return pl.pallas_call(
        body, grid=(rows // cm, nt),
        in_specs=[pl.BlockSpec((cm, D_IN), lambda i, j: (i, 0)),
                  pl.BlockSpec((D_IN, tn), lambda i, j: (0, half * nt + j)),
                  pl.BlockSpec(memory_space=pl.ANY)],
        out_specs=pl.BlockSpec((cm, tn), lambda i, j: (i, j)),
        out_shape=SDS((rows, W_IN_HALF), f32), name="mm_dh_%d" % half,
        compiler_params=_cp(("arbitrary", "arbitrary"), 48),
    )(dz, w_t, after)


def _mm_dwin_parts(hb, parts):
    rows = hb.shape[0]
    tc = 512
    edges = [0]
    for _, w in parts:
        edges.append(edges[-1] + w // tc)

    def body(*refs):
        h_ref, (o_ref, dz_ref, db_ref) = refs[len(parts)], refs[len(parts) + 1:]
        j = pl.program_id(0)
        for p_ref, lo, hi in zip(refs, edges[:-1], edges[1:]):
            @pl.when((j >= lo) & (j < hi))
            def _():
                o_ref[...] = _dot_tn(p_ref[...], h_ref[...]).astype(bf16)
                dz_ref[...] = p_ref[...]

                def step(i, s):
                    blk = p_ref[pl.ds(pl.multiple_of(i * BLK, BLK), BLK), :].astype(f32)
                    return s + blk.reshape(BLK // 8, 8, tc).sum(axis=0)
                s = lax.fori_loop(0, rows // BLK, step, jnp.zeros((8, tc), f32))
                db_ref[...] = jnp.broadcast_to(_colsum(s), (8, tc))

    in_specs = [pl.BlockSpec((rows, tc), lambda j, lo=lo, hi=hi: (0, jnp.clip(j - lo, 0, hi - lo - 1)))
                for lo, hi in zip(edges[:-1], edges[1:])]
    return pl.pallas_call(
        body, grid=(D_IN // tc,),
        in_specs=in_specs + [pl.BlockSpec((rows, W_IN_HALF), lambda j: (0, 0))],
        out_specs=[pl.BlockSpec((tc, W_IN_HALF), lambda j: (j, 0)), pl.BlockSpec((rows, tc), lambda j: (0, j)),
                   pl.BlockSpec((8, tc), lambda j: (0, j))],
        out_shape=[SDS((D_IN, W_IN_HALF), bf16), SDS((rows, D_IN), bf16), SDS((8, D_IN), f32)],
        name="mm_dwin_0", compiler_params=_cp(("arbitrary",), VMEM_LIMIT_MB),
    )(*[a for a, _ in parts], hb)


def _mm_dwin(hb, dz, after):
    rows = dz.shape[0]
    tc = 640

    def body(dz_ref, h_ref, after_ref, o_ref):
        o_ref[...] = _dot_tn(dz_ref[...], h_ref[...]).astype(bf16)

    return pl.pallas_call(
        body, grid=(D_IN // tc,),
        in_specs=[pl.BlockSpec((rows, tc), lambda j: (0, j)),
                  pl.BlockSpec((rows, W_IN_HALF), lambda j: (0, 1)),
                  pl.BlockSpec(memory_space=pl.ANY)],
        out_specs=pl.BlockSpec((tc, W_IN_HALF), lambda j: (j, 0)),
        out_shape=SDS((D_IN, W_IN_HALF), bf16),
        name="mm_dwin_1", compiler_params=_cp(("arbitrary",), 48),
    )(dz, hb, after)


SCAN_ROWS = 32


def _scan8(a, b, reverse):
    idx = lax.broadcasted_iota(jnp.int32, a.shape, 0)
    for s in (1, 2, 4):
        sh = 8 - s if reverse else s
        a_sh, b_sh = pltpu.roll(a, sh, 0), pltpu.roll(b, sh, 0)
        m = (idx < 8 - s) if reverse else (idx >= s)
        b = jnp.where(m, a * b_sh + b, b)
        a = jnp.where(m, a * a_sh, a)
    return a, b


def _shift_rows(prev8, cur, k):
    ext = jnp.concatenate([prev8, cur], axis=0)
    return pltpu.roll(ext, k, 0)[8:, :]


def _gates(xc, w_ra, b_ra, w_ri, b_ri, ls):
    xb = xc.astype(bf16)
    r = _sigmoid(_dot(xb, w_ra) + b_ra)
    ig = _sigmoid(_dot(xb, w_ri) + b_ri)
    la = LRU_C * r * ls
    a = jnp.exp(la)
    mult = jnp.sqrt(jnp.tanh(-la) * (1.0 + a * a))
    return xb, r, ig, a, mult


_RNN_IN_SPECS = lambda rows: [
    pl.BlockSpec((1, 24, 256), lambda n: (n, 0, 0)),
    pl.BlockSpec((1, RNN_BLOCK), lambda n: (0, n)),
    pl.BlockSpec((N_DEV, 2, None, 32, RNN_BLOCK), lambda n: (0, 0, n, 0, 0)),
    pl.BlockSpec((1, RNN_BLOCK), lambda n: (0, n)),
    pl.BlockSpec((1, RNN_BLOCK), lambda n: (0, n)),
    pl.BlockSpec((1, RNN_BLOCK), lambda n: (0, n)),
]


def _rnn_fwd(z, smallw, conv_b, wrg, b_ra, b_ri, lam):
    rows = z.shape[0]
    nb = rows // BLK
    col = lambda off: pl.BlockSpec((rows, RNN_BLOCK), lambda n: (0, off // RNN_BLOCK + n))

    def body(xr_ref, gr_ref, sw_ref, cb_ref, w_ref, bra_ref, bri_ref, lam_ref, xc_ref, hr_ref, ya_ref, yat_ref, a_s):
        cw = sw_ref[0, N_META:24, :]
        cb = cb_ref[...]
        w_ra = w_ref[:, 0].reshape(RNN_BLOCK, RNN_BLOCK)
        w_ri = w_ref[:, 1].reshape(RNN_BLOCK, RNN_BLOCK)
        b_ra_v, b_ri_v = bra_ref[...], bri_ref[...]
        ls = _log_sigmoid(lam_ref[...])
        rid = lax.broadcasted_iota(jnp.int32, (BLK, 1), 0)

        def blk_step(i, carry):
            r0 = pl.multiple_of(i * BLK, BLK)
            grow = rid + r0
            valid = grow >= ROW0
            cur = jnp.where(valid, xr_ref[pl.ds(r0, BLK), :], 0.0)
            prev8 = xr_ref[pl.ds(pl.multiple_of(jnp.maximum(r0 - 8, 0), 8), 8), :] * (i > 0).astype(f32)
            xc = cb + cw[0:1] * cur
            for k in range(1, CONV_WIDTH):
                xc = xc + cw[k:k + 1] * _shift_rows(prev8, cur, k)
            xc_ref[pl.ds(r0, BLK), :] = xc
            _, _, ig, a, mult = _gates(xc, w_ra, b_ra_v, w_ri, b_ri_v, ls)
            mult = jnp.where(grow == ROW0, 1.0, mult)
            a_s[pl.ds(r0, BLK), :] = a
            hr_ref[pl.ds(r0, BLK), :] = jnp.where(valid, mult * ig * xc, 0.0)
            return carry

        lax.fori_loop(0, nb, blk_step, 0)

        def scan_step(j, carry):
            r0 = pl.multiple_of(j * SCAN_ROWS, SCAN_ROWS)
            tiles = [_scan8(a_s[pl.ds(r0 + 8 * k, 8), :], hr_ref[pl.ds(r0 + 8 * k, 8), :], False)
                     for k in range(SCAN_ROWS // 8)]
            for k, (a, b) in enumerate(tiles):
                h = b + a * carry
                hr_ref[pl.ds(r0 + 8 * k, 8), :] = h
                carry = jnp.broadcast_to(h[7:8, :], (8, RNN_BLOCK))
            return carry

        lax.fori_loop(0, rows // SCAN_ROWS, scan_step, jnp.zeros((8, RNN_BLOCK), f32))

        def gate_step(i, carry):
            r0 = pl.multiple_of(i * BLK, BLK)
            ya_ref[pl.ds(r0, BLK), :] = (hr_ref[pl.ds(r0, BLK), :]
                                         * _silu_and_grad(gr_ref[pl.ds(r0, BLK), :])[0]).astype(bf16)
            return carry

        lax.fori_loop(0, nb, gate_step, 0)
        yat_ref[...] = ya_ref[...].astype(f32).T.astype(bf16)

    return pl.pallas_call(
        body, grid=(N_RNN_BLOCKS,),
        in_specs=[col(0), col(OFF_GR)] + _RNN_IN_SPECS(rows),
        out_specs=[pl.BlockSpec((rows, RNN_BLOCK), lambda n: (0, n))] * 3
                  + [pl.BlockSpec((RNN_BLOCK, rows), lambda n: (n, 0))],
        out_shape=[SDS((rows, D), f32), SDS((rows, D), f32), SDS((rows, D), bf16), SDS((D, rows), bf16)],
        scratch_shapes=[pltpu.VMEM((rows, RNN_BLOCK), f32)],
        name="rnn_fwd", compiler_params=_cp(("arbitrary",)),
    )(z, z, smallw, conv_b, wrg, b_ra, b_ri, lam)


def _rnn_bwd(dya, hr, xc, z, smallw, conv_b, wrg, b_ra, b_ri, lam):
    rows = z.shape[0]
    nb = rows // BLK
    col = lambda off: pl.BlockSpec((rows, RNN_BLOCK), lambda n: (0, off // RNN_BLOCK + n))
    blk = pl.BlockSpec((rows, RNN_BLOCK), lambda n: (0, n))

    def body(dya_ref, hr_ref, xc_ref, xr_ref, gr_ref, sw_ref, cb_ref, w_ref, bra_ref, bri_ref, lam_ref,
             dxr_ref, dgr_ref, dw_ref, vec_ref, a_s, lam_s, dxc_s, r_s, ig_s, mult_s, dw_s):
        cw = sw_ref[0, N_META:24, :]
        w_ra = w_ref[:, 0].reshape(RNN_BLOCK, RNN_BLOCK)
        w_ri = w_ref[:, 1].reshape(RNN_BLOCK, RNN_BLOCK)
        b_ra_v, b_ri_v = bra_ref[...], bri_ref[...]
        lam_v = lam_ref[...]
        ls = _log_sigmoid(lam_v)
        rid = lax.broadcasted_iota(jnp.int32, (BLK, 1), 0)
        zrow = jnp.zeros((1, RNN_BLOCK), f32)

        def p1(i, carry):
            r0 = pl.multiple_of(i * BLK, BLK)
            sl = pl.ds(r0, BLK)
            _, r, ig, a, mult = _gates(xc_ref[sl, :], w_ra, b_ra_v, w_ri, b_ri_v, ls)
            a_s[sl, :] = a
            r_s[sl, :] = r
            ig_s[sl, :] = ig
            mult_s[sl, :] = mult
            sg, dsg = _silu_and_grad(gr_ref[sl, :])
            d = dya_ref[sl, :]
            lam_s[sl, :] = d * sg
            dgr_ref[sl, :] = (d * hr_ref[sl, :] * dsg).astype(bf16)
            return carry

        lax.fori_loop(0, nb, p1, 0)

        def p2(jj, carry):
            r0 = pl.multiple_of((rows // SCAN_ROWS - 1 - jj) * SCAN_ROWS, SCAN_ROWS)
            idx = lax.broadcasted_iota(jnp.int32, (8, RNN_BLOCK), 0)
            tiles = []
            for k in range(SCAN_ROWS // 8):
                sl = pl.ds(r0 + 8 * k, 8)
                a, g = a_s[sl, :], lam_s[sl, :]
                tiles.append((g, *_scan8(a, a * g, True)))
            for k in reversed(range(SCAN_ROWS // 8)):
                g, ca, cb_ = tiles[k]
                mu = cb_ + ca * carry
                lam_s[pl.ds(r0 + 8 * k, 8), :] = g + jnp.where(idx < 7, pltpu.roll(mu, 7, 0), carry)
                carry = jnp.broadcast_to(mu[0:1, :], (8, RNN_BLOCK))
            return carry

        lax.fori_loop(0, rows // SCAN_ROWS, p2, jnp.zeros((8, RNN_BLOCK), f32))

        dw_s[...] = jnp.zeros_like(dw_s)

        def p3(i, carry):
            d_bra, d_bri, d_ls = carry
            r0 = pl.multiple_of(i * BLK, BLK)
            sl = pl.ds(r0, BLK)
            grow = rid + r0
            valid = grow >= ROW0
            first = grow == ROW0
            xcv = xc_ref[sl, :]
            xb = xcv.astype(bf16)
            r, ig, a = r_s[sl, :], ig_s[sl, :], a_s[sl, :]
            mult = jnp.where(first, 1.0, mult_s[sl, :])
            lam_t = lam_s[sl, :]
            du = jnp.where(valid, lam_t, 0.0)
            hprev = _shift_rows(hr_ref[pl.ds(pl.multiple_of(jnp.maximum(r0 - 8, 0), 8), 8), :] * (i > 0).astype(f32), hr_ref[sl, :], 1)
            da = lam_t * hprev
            dmult = jnp.where(first, 0.0, du * ig * xcv)
            di = du * mult * xcv
            dxc = du * mult * ig
            ratio = jnp.where(valid & jnp.logical_not(first), a * a / mult, 0.0)
            dla = da * a - dmult * ratio
            dpr = (dla * (LRU_C * ls)) * r * (1.0 - r)
            dpi = di * ig * (1.0 - ig)
            dprb, dpib = dpr.astype(bf16), dpi.astype(bf16)
            dw_s[0] += _dot_tn(xb, dprb)
            dw_s[1] += _dot_tn(xb, dpib)
            dxc_s[sl, :] = dxc + _dot_nt(dprb, w_ra) + _dot_nt(dpib, w_ri)
            return d_bra + _colsum(dpr), d_bri + _colsum(dpi), d_ls + _colsum(dla * (LRU_C * r))

        d_bra, d_bri, d_ls = lax.fori_loop(0, nb, p3, (zrow, zrow, zrow))

        def p4(i, carry):
            d_cb, d_w0, d_w1, d_w2, d_w3 = carry
            r0 = pl.multiple_of(i * BLK, BLK)
            sl = pl.ds(r0, BLK)
            grow = rid + r0
            valid = grow >= ROW0
            dxc = dxc_s[sl, :]
            nxt = dxc_s[pl.ds(pl.multiple_of(jnp.minimum(r0 + BLK, rows - 8), 8), 8), :] * (i < nb - 1).astype(f32)
            ext = jnp.concatenate([dxc, nxt], axis=0)
            dxr = cw[0:1] * dxc
            for k in range(1, CONV_WIDTH):
                dxr = dxr + cw[k:k + 1] * pltpu.roll(ext, BLK + 8 - k, 0)[:BLK, :]
            dxr_ref[sl, :] = jnp.where(valid, dxr, 0.0).astype(bf16)
            cur = jnp.where(valid, xr_ref[sl, :], 0.0)
            prev8 = xr_ref[pl.ds(pl.multiple_of(jnp.maximum(r0 - 8, 0), 8), 8), :] * (i > 0).astype(f32)
            dws = [d_w0 + _colsum(dxc * cur)]
            for k, acc in ((1, d_w1), (2, d_w2), (3, d_w3)):
                dws.append(acc + _colsum(dxc * _shift_rows(prev8, cur, k)))
            return (d_cb + _colsum(dxc), *dws)

        d_cb, d_w0, d_w1, d_w2, d_w3 = lax.fori_loop(0, nb, p4, (zrow,) * 5)

        d_lam = d_ls * _sigmoid(-lam_v)
        vec_ref[...] = jnp.concatenate([d_bra, d_bri, d_lam, d_cb, d_w0, d_w1, d_w2, d_w3], axis=0)
        dw_ref[:, 0] = dw_s[0].astype(bf16).reshape(N_DEV, 32, RNN_BLOCK)
        dw_ref[:, 1] = dw_s[1].astype(bf16).reshape(N_DEV, 32, RNN_BLOCK)

    return pl.pallas_call(
        body, grid=(N_RNN_BLOCKS,),
        in_specs=[blk, blk, blk, col(0), col(OFF_GR)] + _RNN_IN_SPECS(rows),
        out_specs=[blk, blk,
                   pl.BlockSpec((N_DEV, 2, None, 32, RNN_BLOCK), lambda n: (0, 0, n, 0, 0)),
                   pl.BlockSpec((8, RNN_BLOCK), lambda n: (0, n))],
        out_shape=[SDS((rows, D), bf16), SDS((rows, D), bf16),
                   SDS((N_DEV, 2, N_RNN_BLOCKS, 32, RNN_BLOCK), bf16), SDS((8, D), f32)],
        scratch_shapes=[pltpu.VMEM((rows, RNN_BLOCK), f32)] * 6 + [pltpu.VMEM((2, RNN_BLOCK, RNN_BLOCK), f32)],
        name="rnn_bwd", compiler_params=_cp(("arbitrary",), 48),
    )(dya, hr, xc, z, z, smallw, conv_b, wrg, b_ra, b_ri, lam)


def _rope_tables(rows):
    half = jnp.arange(HALF, dtype=f32)
    inv = ROPE_THETA ** (-half / HALF)
    pos = (jnp.arange(rows) - ROW0).astype(f32)
    ang = pos[:, None] * inv[None, :]
    cos, sin = jnp.cos(ang), jnp.sin(ang)
    cos128 = jnp.concatenate([cos, cos, cos, cos], axis=1)
    sin128 = jnp.concatenate([-sin, sin, -sin, sin], axis=1)
    return cos128, sin128


def _rope128(x, cos128, sin128):
    lane = lax.broadcasted_iota(jnp.int32, x.shape, 1)
    swapped = jnp.where(lane % HEAD_DIM < HALF, pltpu.roll(x, 128 - HALF, 1), pltpu.roll(x, HALF, 1))
    return x * cos128 + swapped * sin128


def _qkv_prep(z, cos128, sin128):
    rows = z.shape[0]

    def body(q_ref, kv_ref, c_ref, s_ref, qo_ref, ko_ref, vo_ref):
        c, s = c_ref[...], s_ref[...]
        for g in range(D // 128):
            qo_ref[:, g * 128:(g + 1) * 128] = (_rope128(q_ref[:, g * 128:(g + 1) * 128], c, s)
                                                * (HEAD_DIM ** -0.5)).astype(bf16)
        for g in range(2):
            kr = _rope128(kv_ref[:, g * 128:(g + 1) * 128], c, s)
            for j in range(2):
                ko_ref[2 * g + j] = kr[:, j * HEAD_DIM:(j + 1) * HEAD_DIM].astype(bf16)
        for h in range(N_KV):
            vo_ref[h] = kv_ref[:, 256 + h * HEAD_DIM:256 + (h + 1) * HEAD_DIM].astype(bf16)

    return pl.pallas_call(
        body, grid=(rows // BLK,),
        in_specs=[pl.BlockSpec((BLK, D), lambda i: (i, OFF_Q // D)),
                  pl.BlockSpec((BLK, 512), lambda i: (i, OFF_K // 512)),
                  pl.BlockSpec((BLK, 128), lambda i: (i, 0)),
                  pl.BlockSpec((BLK, 128), lambda i: (i, 0))],
        out_specs=[pl.BlockSpec((BLK, D), lambda i: (i, 0)),
                   pl.BlockSpec((N_KV, BLK, HEAD_DIM), lambda i: (0, i, 0)),
                   pl.BlockSpec((N_KV, BLK, HEAD_DIM), lambda i: (0, i, 0))],
        out_shape=[SDS((rows, D), bf16), SDS((N_KV, rows, HEAD_DIM), bf16), SDS((N_KV, rows, HEAD_DIM), bf16)],
        name="qkv_prep", compiler_params=_cp(("arbitrary",)),
    )(z, z, cos128, sin128)


def _attn_mask(n):
    qi = n * BLK + lax.broadcasted_iota(jnp.int32, (BLK, 2 * BLK + N_META), 0)
    c = lax.broadcasted_iota(jnp.int32, (BLK, 2 * BLK + N_META), 1)
    jb = (n - 1) * BLK + c
    band = (jb >= BLK) & (jb <= qi) & (qi - jb < BLK)
    meta = (ROW0 + c - 2 * BLK) <= qi
    return ((c < 2 * BLK) & band) | ((c >= 2 * BLK) & meta)


N_KEYS = 2 * BLK + N_META


def _stack_heads(t):
    return jnp.concatenate([t[:, g * HEAD_DIM:(g + 1) * HEAD_DIM] for g in range(GROUP)], axis=0)


def _sink_column(sink_ref, h):
    g = lax.broadcasted_iota(jnp.int32, (GROUP, 1, 1), 0)
    col = jnp.zeros((GROUP, 1, 1), f32)
    for j in range(GROUP):
        col = jnp.where(g == j, sink_ref[h * GROUP + j], col)
    return col


def _kv_specs(last):
    cl = lambda n: jnp.minimum(n, last)
    return [pl.BlockSpec((None, N_META, HEAD_DIM), lambda h, n: (h, ROW0 // N_META, 0)),
            pl.BlockSpec((None, BLK, HEAD_DIM), lambda h, n: (h, jnp.maximum(cl(n) - 1, 0), 0)),
            pl.BlockSpec((None, BLK, HEAD_DIM), lambda h, n: (h, cl(n), 0))]


def _attn_fwd(q_r, k_r, v_b, z, sinks):
    rows = q_r.shape[0]
    nb = rows // BLK

    def body(sink_ref, q_ref, km_ref, kp_ref, kc_ref, vm_ref, vp_ref, vc_ref, ga_ref, o_ref, yb_ref, ybt_ref, lse_ref):
        h, n = pl.program_id(0), pl.program_id(1)
        kk = jnp.concatenate([kp_ref[...], kc_ref[...], km_ref[...]], axis=0)
        vv = jnp.concatenate([vp_ref[...], vc_ref[...], vm_ref[...]], axis=0)
        q2 = _stack_heads(q_ref[...])
        s = jnp.where(_attn_mask(n)[None], _dot_nt(q2, kk).reshape(GROUP, BLK, N_KEYS), NEG_INF)
        sink = _sink_column(sink_ref, h)
        m = jnp.maximum(jnp.max(s, axis=-1, keepdims=True), sink)
        p = jnp.exp(s - m)
        den = jnp.sum(p, axis=-1, keepdims=True) + jnp.exp(sink - m)
        o2 = _dot((p / den).astype(bf16).reshape(GROUP * BLK, N_KEYS), vv)
        lse = m + jnp.log(den)
        for g in range(GROUP):
            o_ref[:, g * HEAD_DIM:(g + 1) * HEAD_DIM] = o2[g * BLK:(g + 1) * BLK]
            lse_ref[:, g:g + 1] = lse[g]
        yb = o_ref[...] * _silu_and_grad(ga_ref[...])[0]
        yb_ref[...] = yb.astype(bf16)
        ybt_ref[...] = yb.T.astype(bf16)

    tile = pl.BlockSpec((BLK, 512), lambda h, n: (n, h))
    return pl.pallas_call(
        body, grid=(N_KV, nb),
        in_specs=[pl.BlockSpec(memory_space=pltpu.SMEM), tile] + _kv_specs(nb - 1) + _kv_specs(nb - 1)
                 + [pl.BlockSpec((BLK, 512), lambda h, n: (n, OFF_GA // 512 + h))],
        out_specs=[tile, tile, pl.BlockSpec((512, BLK), lambda h, n: (h, n)),
                   pl.BlockSpec((None, BLK, GROUP), lambda h, n: (h, n, 0))],
        out_shape=[SDS((rows, D), f32), SDS((rows, D), bf16), SDS((D, rows), bf16),
                   SDS((N_KV, rows, GROUP), f32)],
        name="attn_fwd", compiler_params=_cp(("arbitrary", "arbitrary")),
    )(sinks, q_r, k_r, k_r, k_r, v_b, v_b, v_b, z)


def _attn_bwd(dyb, o32, lse, q_r, k_r, v_b, z, sinks):
    rows = q_r.shape[0]
    nb = rows // BLK
    cl = lambda n: jnp.minimum(n, nb - 1)

    def body(sink_ref, dyb_ref, o_ref, lse_ref, q_ref, km_ref, kp_ref, kc_ref, vm_ref, vp_ref, vc_ref, ga_ref,
             dq_ref, dga_ref, dk_ref, dv_ref, dkm_ref, dvm_ref, dsr_ref, ck_s, cv_s):
        h, n = pl.program_id(0), pl.program_id(1)

        @pl.when(n == 0)
        def _():
            dkm_ref[...] = jnp.zeros_like(dkm_ref)
            dvm_ref[...] = jnp.zeros_like(dvm_ref)
            ck_s[...] = jnp.zeros_like(ck_s)
            cv_s[...] = jnp.zeros_like(cv_s)

        @pl.when(n < nb)
        def _():
            kk = jnp.concatenate([kp_ref[...], kc_ref[...], km_ref[...]], axis=0)
            vv = jnp.concatenate([vp_ref[...], vc_ref[...], vm_ref[...]], axis=0)
            sg, dsg = _silu_and_grad(ga_ref[...])
            dyb_v = dyb_ref[...]
            o_v = o_ref[...]
            dga_ref[...] = (dyb_v * o_v * dsg).astype(bf16)
            q2 = _stack_heads(q_ref[...])
            do2 = _stack_heads(dyb_v * sg)
            lse_v = lse_ref[...]
            lse = jnp.concatenate([lse_v[:, g:g + 1] for g in range(GROUP)], axis=0).reshape(GROUP, BLK, 1)
            delta = jnp.sum(do2 * _stack_heads(o_v), axis=-1, keepdims=True).reshape(GROUP, BLK, 1)
            s = jnp.where(_attn_mask(n)[None], _dot_nt(q2, kk).reshape(GROUP, BLK, N_KEYS), NEG_INF)
            p = jnp.exp(s - lse)
            do2b = do2.astype(bf16)
            ds = (p * (_dot_nt(do2b, vv).reshape(GROUP, BLK, N_KEYS) - delta)).astype(bf16)
            ds = ds.reshape(GROUP * BLK, N_KEYS)
            dsr = -jnp.exp(_sink_column(sink_ref, h) - lse) * delta
            dq2 = _dot(ds, kk)
            for g in range(GROUP):
                dq_ref[:, g * HEAD_DIM:(g + 1) * HEAD_DIM] = dq2[g * BLK:(g + 1) * BLK]
                dsr_ref[:, g:g + 1] = dsr[g]
            dkk = _dot_tn(ds, q2)
            dvv = _dot_tn(p.astype(bf16).reshape(GROUP * BLK, N_KEYS), do2b)
            dk_ref[...] = ck_s[...] + dkk[:BLK]
            dv_ref[...] = cv_s[...] + dvv[:BLK]
            ck_s[...] = dkk[BLK:2 * BLK]
            cv_s[...] = dvv[BLK:2 * BLK]
            dkm_ref[...] += dkk[2 * BLK:]
            dvm_ref[...] += dvv[2 * BLK:]

        @pl.when(n == nb)
        def _():
            dk_ref[...] = ck_s[...]
            dv_ref[...] = cv_s[...]

    tile = pl.BlockSpec((BLK, 512), lambda h, n: (cl(n), h))
    kvout = pl.BlockSpec((None, BLK, HEAD_DIM), lambda h, n: (h, jnp.maximum(n - 1, 0), 0))
    mout = pl.BlockSpec((None, N_META, HEAD_DIM), lambda h, n: (h, 0, 0))
    stat = pl.BlockSpec((None, BLK, GROUP), lambda h, n: (h, cl(n), 0))
    return pl.pallas_call(
        body, grid=(N_KV, nb + 1),
        in_specs=[pl.BlockSpec(memory_space=pltpu.SMEM), tile, tile, stat, tile] + _kv_specs(nb - 1)
                 + _kv_specs(nb - 1) + [pl.BlockSpec((BLK, 512), lambda h, n: (cl(n), OFF_GA // 512 + h))],
        out_specs=[tile, tile, kvout, kvout, mout, mout, stat],
        out_shape=[SDS((rows, D), f32), SDS((rows, D), bf16),
                   SDS((N_KV, rows, HEAD_DIM), f32), SDS((N_KV, rows, HEAD_DIM), f32),
                   SDS((N_KV, N_META, HEAD_DIM), f32), SDS((N_KV, N_META, HEAD_DIM), f32),
                   SDS((N_KV, rows, GROUP), f32)],
        scratch_shapes=[pltpu.VMEM((BLK, HEAD_DIM), f32), pltpu.VMEM((BLK, HEAD_DIM), f32)],
        name="attn_bwd", compiler_params=_cp(("arbitrary", "arbitrary")),
    )(sinks, dyb, o32, lse, q_r, k_r, k_r, k_r, v_b, v_b, v_b, z)


def _qkv_finish(dq, dk, dv, dkm, dvm, cos128, sin128):
    rows = dq.shape[0]

    def body(dq_ref, dk_ref, dv_ref, dkm_ref, dvm_ref, c_ref, s_ref, oq_ref, okv_ref):
        first = (pl.program_id(0) == 0).astype(f32)
        c, s = c_ref[...], -s_ref[...]
        for g in range(D // 128):
            oq_ref[:, g * 128:(g + 1) * 128] = (_rope128(dq_ref[:, g * 128:(g + 1) * 128], c, s)
                                                * (HEAD_DIM ** -0.5)).astype(bf16)
        pad = jnp.zeros((ROW0, HEAD_DIM), f32)
        ks = [dk_ref[h] + first * jnp.concatenate([pad, dkm_ref[h]], axis=0) for h in range(N_KV)]
        vs = [dv_ref[h] + first * jnp.concatenate([pad, dvm_ref[h]], axis=0) for h in range(N_KV)]
        for g in range(2):
            kp = jnp.concatenate([ks[2 * g], ks[2 * g + 1]], axis=1)
            okv_ref[:, g * 128:(g + 1) * 128] = _rope128(kp, c, s).astype(bf16)
            okv_ref[:, 256 + g * 128:256 + (g + 1) * 128] = jnp.concatenate([vs[2 * g], vs[2 * g + 1]], axis=1).astype(bf16)

    kv = pl.BlockSpec((N_KV, BLK, HEAD_DIM), lambda i: (0, i, 0))
    mt = pl.BlockSpec((N_KV, N_META, HEAD_DIM), lambda i: (0, 0, 0))
    return pl.pallas_call(
        body, grid=(rows // BLK,),
        in_specs=[pl.BlockSpec((BLK, D), lambda i: (i, 0)), kv, kv, mt, mt,
                  pl.BlockSpec((BLK, 128), lambda i: (i, 0)), pl.BlockSpec((BLK, 128), lambda i: (i, 0))],
        out_specs=[pl.BlockSpec((BLK, D), lambda i: (i, 0)), pl.BlockSpec((BLK, 512), lambda i: (i, 0))],
        out_shape=[SDS((rows, D), bf16), SDS((rows, 512), bf16)],
        name="qkv_finish", compiler_params=_cp(("arbitrary",)),
    )(dq, dk, dv, dkm, dvm, cos128, sin128)


_TW = 512


def _mix_specs(rows):
    tr = _row_chunk(rows)
    tile = pl.BlockSpec((tr, _TW), lambda i, j: (i, j))
    ga = pl.BlockSpec((tr, _TW), lambda i, j: (i, OFF_G // _TW + j))
    gb = pl.BlockSpec((tr, _TW), lambda i, j: (i, (OFF_G + D) // _TW + j))
    return (rows // tr, D // _TW), tile, ga, gb


def _mix_fwd(y_a, y_b, z):
    rows = y_a.shape[0]
    tw = 256
    col = lambda off: pl.BlockSpec((rows, tw), lambda j: (0, off // tw + j))

    def body(ya_ref, yb_ref, ga_ref, gb_ref, o_ref, ot_ref):
        mixed = (_sigmoid(ga_ref[...]) * ya_ref[...].astype(f32)
                 + _sigmoid(gb_ref[...]) * yb_ref[...].astype(f32))
        o_ref[...] = mixed.astype(bf16)
        ot_ref[...] = mixed.T.astype(bf16)

    return pl.pallas_call(
        body, grid=(D // tw,), in_specs=[col(0), col(0), col(OFF_G), col(OFF_G + D)],
        out_specs=[col(0), pl.BlockSpec((tw, rows), lambda j: (j, 0))],
        out_shape=[SDS((rows, D), bf16), SDS((D, rows), bf16)],
        name="mix_fwd", compiler_params=_cp(("arbitrary",)),
    )(y_a, y_b, z, z)


def _mix_bwd(dmixed, y_a, y_b, z):
    rows = y_a.shape[0]
    grid, _mix_tile, _mix_ga, _mix_gb = _mix_specs(rows)

    def body(dm_ref, ya_ref, yb_ref, ga_ref, gb_ref, dya_ref, dyb_ref, dga_ref, dgb_ref):
        dm = dm_ref[...].astype(f32)
        sa, sb = _sigmoid(ga_ref[...]), _sigmoid(gb_ref[...])
        dya_ref[...] = (dm * sa).astype(bf16)
        dyb_ref[...] = (dm * sb).astype(bf16)
        dga_ref[...] = (dm * ya_ref[...].astype(f32) * sa * (1.0 - sa)).astype(bf16)
        dgb_ref[...] = (dm * yb_ref[...].astype(f32) * sb * (1.0 - sb)).astype(bf16)

    return pl.pallas_call(
        body, grid=grid, in_specs=[_mix_tile, _mix_tile, _mix_tile, _mix_ga, _mix_gb],
        out_specs=[_mix_tile] * 4, out_shape=[SDS((rows, D), bf16)] * 4,
        name="mix_bwd", compiler_params=_cp(("arbitrary", "arbitrary")),
    )(dmixed, y_a, y_b, z, z)


def _final_ln(out32, h32, tgt, ln_g, ln_b):
    rows = out32.shape[0]

    def body(o_ref, h_ref, t_ref, g_ref, b_ref, du_ref, dub_ref, st_ref):
        i = pl.program_id(0)
        g = g_ref[...]
        y, xhat, rstd = _ln_rows(ALPHA * h_ref[...] + o_ref[...], g, b_ref[...])
        e = jnp.where(i > 0, y - t_ref[0], 0.0)
        dy = e * (1.0 / D)
        du = _ln_rows_bwd(dy, g, xhat, rstd)
        du_ref[...] = du
        dub_ref[...] = du.astype(bf16)
        st = jnp.concatenate([_colsum(dy * xhat), _colsum(dy), _colsum(du), _colsum(e * e) * (0.5 / D),
                              jnp.zeros((4, D), f32)], axis=0)

        @pl.when(i == 0)
        def _():
            st_ref[...] = st

        @pl.when(i > 0)
        def _():
            st_ref[...] += st

    row = pl.BlockSpec((BLK, D), lambda i: (i, 0))
    vec = pl.BlockSpec((1, D), lambda i: (0, 0))
    return pl.pallas_call(
        body, grid=(rows // BLK,),
        in_specs=[row, row, pl.BlockSpec((1, BLK, D), lambda i: (0, jnp.maximum(i - 1, 0), 0)), vec, vec],
        out_specs=[row, row, pl.BlockSpec((8, D), lambda i: (0, 0))],
        out_shape=[SDS((rows, D), f32), SDS((rows, D), bf16), SDS((8, D), f32)],
        name="final_ln", compiler_params=_cp(("arbitrary",)),
    )(out32, h32, tgt, ln_g, ln_b)


def _step_rnn(h32, hb, z, wrg, smallw, p, zero):
    rows = z.shape[0]
    cos128, sin128 = _rope_tables(rows)
    cos128 = cos128 + zero
    xc, hr, ya, ya_t = _rnn_fwd(z, smallw, p["conv_b"] + zero, wrg, p["b_ra"], p["b_ri"], p["lru_lambda"])
    q_r, k_r, v_b = _qkv_prep(z, cos128, sin128)
    return dict(cos128=cos128, sin128=sin128, h32=h32, hb=hb, z=z, xc=xc, hr=hr, ya=ya, ya_t=ya_t,
                q_r=q_r, k_r=k_r, v_b=v_b)


def _step_attn(s, p, zero):
    sinks = p["sinks"].reshape(N_KV * GROUP) + zero[0]
    o32, yb, yb_t, lse = _attn_fwd(s["q_r"], s["k_r"], s["v_b"], s["z"], sinks)
    return dict(s, sinks=sinks, o32=o32, yb=yb, yb_t=yb_t, lse=lse)


def _step_merge(s, tgt, w3, p):
    ya, yb, z = s["ya"], s["yb"], s["z"]
    y_a = _mm(ya, w3, sel=0, out_dtype=bf16, name="mm_ya")
    y_b = _mm(yb, w3, sel=1, out_dtype=bf16, name="mm_yb")
    mixed, mixed_t = _mix_fwd(y_a, y_b, z)
    out32 = _mm(mixed, w3, sel=2, bias=p["b_o"], name="mm_out")
    du32, dub, st_out = _final_ln(out32, s["h32"], tgt, p["ln_g"], p["ln_b"])

    g_wo = _mm(mixed_t, dub, out_dtype=bf16, name="mm_dwo")
    dmixed = _mm(dub, w3, sel=2, nt=True, out_dtype=bf16, name="mm_dmixed")
    dya_b, dyb_b, dma, dmb = _mix_bwd(dmixed, y_a, y_b, z)
    g_wrnn = _mm(s["ya_t"], dya_b, out_dtype=bf16, name="mm_dwrnn")
    g_wattn = _mm(s["yb_t"], dyb_b, out_dtype=bf16, name="mm_dwattn")
    dya = _mm(dya_b, w3, sel=0, nt=True, name="mm_dya")
    dyb = _mm(dyb_b, w3, sel=1, nt=True, name="mm_dyb")
    return dict(du32=du32, st_out=st_out, dma=dma, dmb=dmb, dya=dya, dyb=dyb, g_wo=g_wo, g_wrnn=g_wrnn,
                g_wattn=g_wattn)


def _step_backward(s, t, wrg, smallw, p, conv_b):
    z = s["z"]
    dxr, dgr, g_wrg, vec_rnn = _rnn_bwd(t["dya"], s["hr"], s["xc"], z, smallw, conv_b, wrg, p["b_ra"], p["b_ri"],
                                        p["lru_lambda"])
    dq_r, dga, dk, dv, dkm, dvm, dsr = _attn_bwd(t["dyb"], s["o32"], s["lse"], s["q_r"], s["k_r"], s["v_b"], z,
                                                 s["sinks"])
    dq, dkv = _qkv_finish(dq_r, dk, dv, dkm, dvm, s["cos128"], s["sin128"])
    dz_parts = [(dxr, D), (dgr, D), (dq, D), (dkv, 512), (dga, D), (t["dma"], D), (t["dmb"], D)]
    return dict(vec_rnn=vec_rnn, dsr=dsr, g_wrg=g_wrg, dz_parts=dz_parts)


def _step_input_grad(dh_lo, dh_hi, du32, x, smallw, p, after):
    grad_x, dmeta, st_emb = _ln_emb_bwd(dh_lo, dh_hi, du32, x, smallw, p["ln_emb_g"], after)
    return dict(grad_x=grad_x, dmeta=dmeta, st_emb=st_emb)


_ANY = pl.BlockSpec(memory_space=pl.ANY)
_VMEM = pl.BlockSpec(memory_space=pltpu.VMEM)


def _place():
    x, y, c = lax.axis_index("x"), lax.axis_index("y"), lax.axis_index("c")
    return x, y, c


def _dev(px, py, pc):
    return 4 * px + 2 * py + pc


def _tile_rows(r):
    return max(t for t in range(16, 321, 16) if r % t == 0) if r > 320 else r


def _cast_w_in(w_in_t):
    tm = _tile_rows(SHARD_IN)

    def body(i_ref, o_ref):
        o_ref[...] = i_ref[...].astype(bf16)

    return pl.pallas_call(
        body, grid=(SHARD_IN // tm,),
        in_specs=[pl.BlockSpec((tm, D), lambda i: (i, 0))],
        out_specs=pl.BlockSpec((tm, D), lambda i: (i, 0)),
        out_shape=SDS((SHARD_IN, D), bf16), name="cast_w_in", compiler_params=_cp(("arbitrary",)),
    )(w_in_t)


def _cast_small(w_rnn_out, w_attn_out, w_o, w_ra, w_ri, meta, conv_w):
    def body(a_ref, b_ref, c_ref, ra_ref, ri_ref, m_ref, cw_ref, w3_ref, wrg_ref, sw_ref):
        w3_ref[0] = a_ref[0].astype(bf16)
        w3_ref[1] = b_ref[0].astype(bf16)
        w3_ref[2] = c_ref[0].astype(bf16)
        wrg_ref[0] = ra_ref[0].astype(bf16)
        wrg_ref[1] = ri_ref[0].astype(bf16)
        sw_ref[...] = jnp.concatenate([m_ref[...], cw_ref[0], jnp.zeros((4, 256), f32)], axis=0)

    return pl.pallas_call(
        body,
        out_shape=[SDS((3, 256, D), bf16), SDS((2, N_RNN_BLOCKS, 32, RNN_BLOCK), bf16), SDS((24, 256), f32)],
        name="cast_small", compiler_params=_cp(None),
    )(w_rnn_out, w_attn_out, w_o, w_ra, w_ri, meta, conv_w)


def _all_gather(shards, later):
    n = len(shards)
    nl = len(later)

    def body(*refs):
        ins, outs = refs[:n], refs[n + nl:2 * n + nl]
        send_sems, recv_sems, local_sems = refs[2 * (n + nl):]
        x, y, c = _place()
        me, sibling = (x, y, c), (x, y, 1 - c)
        chips = [(1 - x, y), (x, 1 - y), (1 - x, 1 - y)]

        def copy(a, k, block, to, src=None):
            dst = outs[a].at[_dev(*block)]
            return pltpu.make_async_remote_copy(
                src_ref=dst if src is None else src, dst_ref=dst,
                send_sem=send_sems.at[a * 7 + k], recv_sem=recv_sems.at[a * 7 + k],
                device_id=to, device_id_type=MESH)

        all_ins, all_outs = refs[:n + nl], refs[n + nl:2 * (n + nl)]
        mine = [pltpu.make_async_copy(all_ins[a], all_outs[a].at[_dev(*me)], local_sems.at[a]) for a in range(n + nl)]
        for cp in mine:
            cp.start()
        first = []
        for a in range(n):
            first.append(copy(a, 0, me, sibling, src=ins[a]))
            first += [copy(a, 1 + j, me, (*chip, c), src=ins[a]) for j, chip in enumerate(chips)]
        for cp in first:
            cp.start()
        passed = []
        for a in range(n):
            for j, chip in enumerate(chips):
                copy(a, 1 + j, (*chip, c), me).wait_recv()
                cp = copy(a, 4 + j, (*chip, c), sibling)
                cp.start()
                passed.append(cp)
        for a in range(n):
            copy(a, 0, sibling, me).wait_recv()
            for j, chip in enumerate(chips):
                copy(a, 4 + j, (*chip, 1 - c), me).wait_recv()
        for cp in first + passed:
            cp.wait_send()
        for cp in mine:
            cp.wait()

    return pl.pallas_call(
        body, in_specs=[_ANY] * (n + nl), out_specs=[_ANY] * (n + nl),
        out_shape=[SDS((N_DEV, *s.shape), s.dtype) for s in (*shards, *later)],
        scratch_shapes=[pltpu.SemaphoreType.DMA((7 * n,)), pltpu.SemaphoreType.DMA((7 * n,)),
                        pltpu.SemaphoreType.DMA((n + nl,))],
        name="all_gather_weights",
    )(*shards, *later)


_HBM = pl.BlockSpec(memory_space=pltpu.HBM)
_SEM = pl.BlockSpec(memory_space=pltpu.SEMAPHORE)
_PEER_FLIPS = [(f // 4, (f // 2) % 2, f % 2) for f in range(1, N_DEV)]


def _remote(src, dst, send_sems, recv_sems, k, to):
    return pltpu.make_async_remote_copy(src_ref=src, dst_ref=dst, send_sem=send_sems.at[k], recv_sem=recv_sems.at[k],
                                        device_id=to, device_id_type=MESH)


def _copies_direct(same_src):
    def make(srcs, lands, send_sems, recv_sems):
        x, y, c = _place()
        me = _dev(x, y, c)
        out = []
        for a in range(len(srcs)):
            for k, (fx, fy, fc) in enumerate(_PEER_FLIPS):
                peer = ((x + fx) % 2, (y + fy) % 2, (c + fc) % 2)
                src = srcs[a] if same_src else srcs[a].at[_dev(*peer)]
                out.append(_remote(src, lands[a].at[me], send_sems, recv_sems, 7 * a + k, peer))
        return out
    return make


def _copies_siblings(srcs, lands, send_sems, recv_sems):
    x, y, c = _place()
    return [_remote(srcs[a].at[2 * q + (1 - c)], lands[a].at[q], send_sems, recv_sems, 4 * a + q, (x, y, 1 - c))
            for a in range(len(srcs)) for q in range(4)]


def _copies_chips(srcs, lands, send_sems, recv_sems):
    x, y, c = _place()
    chips = [(1 - x, y), (x, 1 - y), (1 - x, 1 - y)]
    return [_remote(srcs[a].at[2 * qx + qy], lands[a].at[j], send_sems, recv_sems, 3 * a + j, (qx, qy, c))
            for a in range(len(srcs)) for j, (qx, qy) in enumerate(chips)]


def _split_start(make, per_array, srcs, lands, dep, name):
    n = len(srcs)

    def body(*refs):
        send_sems, recv_sems, token = refs[2 * n + 1], refs[2 * n + 2], refs[-1]
        for cp in make(refs[:n], refs[n:2 * n], send_sems, recv_sems):
            cp.start()
        token[...] = jnp.zeros_like(token)

    hbm = lambda t: pltpu.with_memory_space_constraint(t, pltpu.HBM)
    res = pl.pallas_call(
        body, name=name,
        out_shape=(pltpu.SemaphoreType.DMA((per_array * n,)), pltpu.SemaphoreType.DMA((per_array * n,)),
                   *[pltpu.HBM(t.shape, t.dtype) for t in (*srcs, *lands)], SDS((8, 128), f32)),
        in_specs=[_HBM] * (2 * n) + [_ANY], out_specs=(_SEM, _SEM, *([_HBM] * (2 * n)), _VMEM),
        input_output_aliases={i: 2 + i for i in range(2 * n)},
        compiler_params=pltpu.CompilerParams(has_side_effects=pltpu.SideEffectType.DATAFLOW_SIDE_EFFECTING),
    )(*[hbm(t) for t in (*srcs, *lands)], dep)
    return res[0], res[1], list(res[2:2 + n]), list(res[2 + n:2 + 2 * n]), res[-1]


def _split_wait(make, send_sems, recv_sems, srcs, lands, after, name):
    n = len(srcs)

    def body(*refs):
        for cp in make(refs[:n], refs[n:2 * n], refs[2 * n], refs[2 * n + 1]):
            cp.wait_send()
            cp.wait_recv()

    res = pl.pallas_call(
        body, name=name,
        out_shape=tuple(pltpu.HBM(t.shape, t.dtype) for t in (*srcs, *lands)),
        in_specs=[_HBM] * (2 * n) + [_SEM, _SEM, _ANY], out_specs=tuple([_HBM] * (2 * n)),
        input_output_aliases={i: i for i in range(2 * n)},
        compiler_params=pltpu.CompilerParams(has_side_effects=pltpu.SideEffectType.DATAFLOW_SIDE_EFFECTING),
    )(*srcs, *lands, send_sems, recv_sems, after)
    return list(res[:n]), list(res[n:])


def _adamw_direct(g, land, me_idx, w, m, v, name):
    r, wd = w.shape
    tr = min(r, 256)

    def body(me_ref, *refs):
        g_ref, peers = refs[0], refs[1:N_DEV]
        w_ref, m_ref, v_ref, g_out, d_out, m_out, v_out = refs[N_DEV:]
        gs = g_ref[...].astype(f32)
        for p_ref in peers:
            gs = gs + p_ref[...].astype(f32)
        d, mn, vn = _adamw(w_ref[...], gs, m_ref[...], v_ref[...])
        g_out[...] = gs
        d_out[...] = d
        m_out[...] = mn
        v_out[...] = vn

    tile = pl.BlockSpec((tr, wd), lambda i, me_ref: (i, 0))
    slot = lambda k: pl.BlockSpec((None, tr, wd), lambda i, me_ref: ((me_ref[0] + k) % N_DEV, i, 0))
    return pl.pallas_call(
        body,
        grid_spec=pltpu.PrefetchScalarGridSpec(
            num_scalar_prefetch=1, grid=(r // tr,),
            in_specs=[slot(0)] + [slot(k) for k in range(1, N_DEV)] + [tile, tile, tile],
            out_specs=[tile] * 4),
        out_shape=[SDS((r, wd), f32)] * 4, name=name, compiler_params=_cp(("arbitrary",), 48),
    )(me_idx, g, *([land] * (N_DEV - 1)), w, m, v)


def _pair_sum(g, r1, c_idx, name):
    _, r, w = g.shape
    tr = _tile_rows(r)

    def body(c_ref, g_ref, r_ref, o_ref):
        o_ref[...] = (g_ref[...].astype(f32) + r_ref[...].astype(f32)).astype(bf16)

    return pl.pallas_call(
        body,
        grid_spec=pltpu.PrefetchScalarGridSpec(
            num_scalar_prefetch=1, grid=(4, r // tr),
            in_specs=[pl.BlockSpec((None, tr, w), lambda q, i, c_ref: (2 * q + c_ref[0], i, 0)),
                      pl.BlockSpec((None, tr, w), lambda q, i, c_ref: (q, i, 0))],
            out_specs=pl.BlockSpec((None, tr, w), lambda q, i, c_ref: (q, i, 0))),
        out_shape=SDS((4, r, w), bf16), name=name, compiler_params=_cp(("arbitrary", "arbitrary")),
    )(c_idx, g, r1)


def _adamw(w, g, m, v):
    m = ADAM_B1 * m + (1.0 - ADAM_B1) * g
    v = ADAM_B2 * v + (1.0 - ADAM_B2) * (g * g)
    m_hat = m / (1.0 - ADAM_B1 ** ADAM_STEP)
    v_hat = v / (1.0 - ADAM_B2 ** ADAM_STEP)
    delta = -ADAM_LR * (m_hat / (jnp.sqrt(v_hat) + ADAM_EPS) + ADAM_WD * w)
    return delta, m, v


def _adamw_big(part, r2, q_idx, w, m, v, name, row_off=0, cols=(0, 1), prev=None):
    r, wd = w.shape
    tr = _tile_rows(r)
    k, ncol = cols
    wp = wd // ncol

    def body(q_ref, p_ref, r_ref, w_ref, m_ref, v_ref, *rest):
        g_out, d_out, m_out, v_out = rest[-4:]
        g = p_ref[...].astype(f32)
        for j in range(3):
            g = g + r_ref[j].astype(f32)
        d, mn, vn = _adamw(w_ref[...], g, m_ref[...], v_ref[...])
        g_out[...] = g
        d_out[...] = d
        m_out[...] = mn
        v_out[...] = vn

    tile = pl.BlockSpec((tr, wp), lambda i, q_ref: (i, k))
    prev = list(prev) if prev is not None else []
    return pl.pallas_call(
        body,
        grid_spec=pltpu.PrefetchScalarGridSpec(
            num_scalar_prefetch=1, grid=(r // tr,),
            in_specs=[pl.BlockSpec((None, tr, wp), lambda i, q_ref: (q_ref[0], row_off + i, 0)),
                      pl.BlockSpec((3, tr, wp), lambda i, q_ref: (0, row_off + i, 0)), tile, tile, tile]
                     + [pl.BlockSpec(memory_space=pl.ANY)] * len(prev),
            out_specs=[tile] * 4),
        out_shape=[SDS((r, wd), f32)] * 4, name=name,
        input_output_aliases={6 + i: i for i in range(len(prev))},
        compiler_params=_cp(("arbitrary",), 48),
    )(q_idx, part, r2, w, m, v, *prev)


_SMALL_ROWS = 24


def _pack_early(vec_rnn, st_out, dsr, db_in):
    def body(vr_ref, so_ref, dsr_ref, db_ref, sm_ref, sm2_ref):
        sm_ref[...] = jnp.zeros_like(sm_ref)
        sm2_ref[...] = jnp.zeros_like(sm2_ref)
        sm_ref[2:3, :] = vr_ref[3:4, :]
        sm_ref[3:6, :] = vr_ref[0:3, :]
        sm_ref[6:7, :] = so_ref[2:3, :]
        sm_ref[7:9, :] = so_ref[0:2, :]
        sm_ref[10:11, :] = so_ref[3:4, :]
        for h in range(N_KV):
            sm_ref[9:10, h * GROUP:(h + 1) * GROUP] = _colsum(dsr_ref[h])
        for j in range(6):
            sm_ref[16 + j:17 + j, :] = db_ref[0:1, j * D:(j + 1) * D]
        sm_ref[22:23, 0:D_IN - 6 * D] = db_ref[0:1, 6 * D:D_IN]
        for s in range(N_DEV):
            sm2_ref[s, 0:CONV_WIDTH, :] = vr_ref[4:8, s * 256:(s + 1) * 256]

    return pl.pallas_call(
        body, out_shape=[SDS((_SMALL_ROWS, D), f32), SDS((N_DEV, 8, 256), f32)],
        name="pack_early", compiler_params=_cp(None),
    )(vec_rnn, st_out, dsr, db_in)


def _pack_late(st_emb, dmeta):
    def body(se_ref, dm_ref, sm_ref, sm2_ref):
        sm_ref[...] = se_ref[...]
        for s in range(N_DEV):
            sm2_ref[s] = dm_ref[:, s * 256:(s + 1) * 256]

    return pl.pallas_call(
        body, out_shape=[SDS((8, D), f32), SDS((N_DEV, N_META, 256), f32)],
        name="pack_late", compiler_params=_cp(None),
    )(st_emb, dmeta)


def _small_allreduce(sm, sm2):
    def body(sm_ref, sm2_ref, o_ref, o2_ref, buf, buf2, send_sems, recv_sems):
        x, y, c = _place()
        me = _dev(x, y, c)
        copies = []
        for f in range(1, N_DEV):
            fx, fy, fc = f // 4, (f // 2) % 2, f % 2
            peer = ((x + fx) % 2, (y + fy) % 2, (c + fc) % 2)
            for t, (src, dst) in enumerate(((sm_ref, buf), (sm2_ref, buf2))):
                k = 2 * (f - 1) + t
                copies.append(pltpu.make_async_remote_copy(
                    src_ref=src, dst_ref=dst.at[me], send_sem=send_sems.at[k], recv_sem=recv_sems.at[k],
                    device_id=peer, device_id_type=MESH))
        for cp in copies:
            cp.start()
        buf[me] = sm_ref[...]
        buf2[me] = sm2_ref[...]
        for cp in copies:
            cp.wait()
        acc, acc2 = buf[0], buf2[0]
        for e in range(1, N_DEV):
            acc, acc2 = acc + buf[e], acc2 + buf2[e]
        o_ref[...] = acc
        o2_ref[...] = acc2

    return pl.pallas_call(
        body, in_specs=[_VMEM, _VMEM], out_specs=[_VMEM, _VMEM],
        out_shape=[SDS(sm.shape, f32), SDS(sm2.shape, f32)],
        scratch_shapes=[pltpu.VMEM((N_DEV, *sm.shape), f32), pltpu.VMEM((N_DEV, *sm2.shape), f32),
                        pltpu.SemaphoreType.DMA((14,)), pltpu.SemaphoreType.DMA((14,))],
        name="small_allreduce",
    )(sm, sm2)


_SMALL_ROW_OF = {"ln_emb_g": 0, "ln_emb_b": 1, "conv_b": 2, "b_ra": 3, "b_ri": 4, "lru_lambda": 5, "b_o": 6,
                 "ln_g": 7, "ln_b": 8}
_SMALL_NAMES = ["ln_emb_g", "ln_emb_b", "conv_b", "b_ra", "b_ri", "lru_lambda", "b_o", "ln_g", "ln_b",
                "sinks", "b_in", "meta_tokens", "conv_w"]


def _small_update(me_idx, early, late, wmv):
    n_fixed = 7

    def in_order(me, own_ref, land_ref):
        acc = None
        for e in range(N_DEV):
            term = jnp.where(me == e, own_ref[...], land_ref[e])
            acc = term if acc is None else acc + term
        return acc

    def body(*refs):
        me_ref, own_ref, land_ref, cown_ref, cland_ref, late_ref, meta_ref = refs[:n_fixed]
        ins = refs[n_fixed:n_fixed + 3 * len(_SMALL_NAMES)]
        outs = refs[n_fixed + 3 * len(_SMALL_NAMES):]
        me = me_ref[0]
        sm = in_order(me, own_ref, land_ref)
        conv = in_order(me, cown_ref, cland_ref)

        def grad_of(name):
            if name in ("ln_emb_g", "ln_emb_b"):
                r = _SMALL_ROW_OF[name]
                return late_ref[r:r + 1, :]
            if name in _SMALL_ROW_OF:
                r = _SMALL_ROW_OF[name]
                return sm[r:r + 1, :]
            if name == "sinks":
                return sm[9:10, 0:N_KV * GROUP]
            if name == "b_in":
                return jnp.concatenate([sm[16 + j:17 + j, :] for j in range(7)], axis=1)[:, :D_IN]
            if name == "meta_tokens":
                return meta_ref[...]
            return conv[0:CONV_WIDTH, :]

        for i, name in enumerate(_SMALL_NAMES):
            w_ref, m_ref, v_ref = ins[3 * i:3 * i + 3]
            g = grad_of(name)
            d, mn, vn = _adamw(w_ref[...], g, m_ref[...], v_ref[...])
            outs[4 * i][...] = g
            outs[4 * i + 1][...] = d
            outs[4 * i + 2][...] = mn
            outs[4 * i + 3][...] = vn
        outs[-1][...] = jnp.broadcast_to(jnp.sum(sm[10:11, :], axis=1, keepdims=True), (8, 128))

    args, out_shape = [me_idx, *early, *late], []
    for name in _SMALL_NAMES:
        args += list(wmv[name])
        out_shape += [SDS(wmv[name][0].shape, f32)] * 4
    out_shape.append(SDS((8, 128), f32))
    res = pl.pallas_call(
        body, out_shape=out_shape, in_specs=[pl.BlockSpec(memory_space=pltpu.SMEM)] + [_VMEM] * (len(args) - 1),
        name="small_update", compiler_params=_cp(None))(*args)
    return {name: tuple(res[4 * i:4 * i + 4]) for i, name in enumerate(_SMALL_NAMES)}, res[-1][0, 0]


_WEIGHTS = ["meta_tokens", "ln_emb_g", "ln_emb_b", "w_in", "b_in", "conv_w", "conv_b", "w_ra", "b_ra", "w_ri",
            "b_ri", "lru_lambda", "sinks", "w_rnn_out", "w_attn_out", "w_o", "b_o", "ln_g", "ln_b"]
_SMALL_2D = {"meta_tokens": (N_META, 256), "conv_w": (CONV_WIDTH, 256), "b_in": (1, D_IN), "sinks": (1, N_KV * GROUP)}


def kernel(x, meta_tokens, ln_emb_g, ln_emb_b, w_in, b_in, conv_w, conv_b, w_ra, b_ra, w_ri, b_ri, lru_lambda, sinks, w_rnn_out, w_attn_out, w_o, b_o, ln_g, ln_b, loss_target, m_meta_tokens, m_ln_emb_g, m_ln_emb_b, m_w_in, m_b_in, m_conv_w, m_conv_b, m_w_ra, m_b_ra, m_w_ri, m_b_ri, m_lru_lambda, m_sinks, m_w_rnn_out, m_w_attn_out, m_w_o, m_b_o, m_ln_g, m_ln_b, v_meta_tokens, v_ln_emb_g, v_ln_emb_b, v_w_in, v_b_in, v_conv_w, v_conv_b, v_w_ra, v_b_ra, v_w_ri, v_b_ri, v_lru_lambda, v_sinks, v_w_rnn_out, v_w_attn_out, v_w_o, v_b_o, v_ln_g, v_ln_b):
    w = dict(meta_tokens=meta_tokens, ln_emb_g=ln_emb_g, ln_emb_b=ln_emb_b, w_in=w_in, b_in=b_in, conv_w=conv_w,
             conv_b=conv_b, w_ra=w_ra, b_ra=b_ra, w_ri=w_ri, b_ri=b_ri, lru_lambda=lru_lambda, sinks=sinks,
             w_rnn_out=w_rnn_out, w_attn_out=w_attn_out, w_o=w_o, b_o=b_o, ln_g=ln_g, ln_b=ln_b)
    m = dict(meta_tokens=m_meta_tokens, ln_emb_g=m_ln_emb_g, ln_emb_b=m_ln_emb_b, w_in=m_w_in, b_in=m_b_in,
             conv_w=m_conv_w, conv_b=m_conv_b, w_ra=m_w_ra, b_ra=m_b_ra, w_ri=m_w_ri, b_ri=m_b_ri,
             lru_lambda=m_lru_lambda, sinks=m_sinks, w_rnn_out=m_w_rnn_out, w_attn_out=m_w_attn_out, w_o=m_w_o,
             b_o=m_b_o, ln_g=m_ln_g, ln_b=m_ln_b)
    v = dict(meta_tokens=v_meta_tokens, ln_emb_g=v_ln_emb_g, ln_emb_b=v_ln_emb_b, w_in=v_w_in, b_in=v_b_in,
             conv_w=v_conv_w, conv_b=v_conv_b, w_ra=v_w_ra, b_ra=v_b_ra, w_ri=v_w_ri, b_ri=v_b_ri,
             lru_lambda=v_lru_lambda, sinks=v_sinks, w_rnn_out=v_w_rnn_out, w_attn_out=v_w_attn_out, w_o=v_w_o,
             b_o=v_b_o, ln_g=v_ln_g, ln_b=v_ln_b)
    px, py, pc = _place()
    as_idx = lambda t: jnp.reshape(t, (1,)).astype(jnp.int32)
    c_idx, q_idx, me_idx = as_idx(pc), as_idx(2 * px + py), as_idx(_dev(px, py, pc))

    w3_s, wrg_s, small_s = _cast_small(w_rnn_out, w_attn_out, w_o, w_ra, w_ri, meta_tokens, conv_w)
    vec = lambda name: w[name].reshape(1, -1)
    p = {k: vec(k) for k in ("ln_emb_g", "ln_emb_b", "b_in", "conv_b", "b_ra", "b_ri", "lru_lambda", "sinks",
                             "b_o", "ln_g", "ln_b")}
    w_in_t = lambda a: jnp.swapaxes(a, 1, 2).reshape(SHARD_IN, D)
    wg, wrg, smallw, w3_land = _all_gather([_cast_w_in(w_in_t(w_in)), wrg_s, small_s], [w3_s])
    w3_pending = _split_start(_copies_direct(True), 7, [w3_s], [w3_land], smallw, "gather_w3_start")
    w_full = wg.reshape(D_IN, D)

    zero = w3_pending[4][0:1, 0:1]
    h32, hb = _ln_emb(x, smallw, p["ln_emb_g"], p["ln_emb_b"])
    z = _mm(hb, w_full, nt=True, bias=p["b_in"] + zero, name="mm_z")
    s = _step_attn(_step_rnn(h32, hb, z, wrg, smallw, p, zero), p, zero)
    w3 = _split_wait(_copies_direct(True), *w3_pending[:4], s["lse"], "gather_w3_wait")[1][0]
    t = _step_merge(s, loss_target, w3, p)

    big = {}
    two_d = lambda name: (w[name].shape[-2], w[name].shape[-1])
    proj = ("w_o", "w_rnn_out", "w_attn_out")
    g_proj = [t[k].reshape(N_DEV, 256, D) for k in ("g_wo", "g_wrnn", "g_wattn")]
    g_pending = _split_start(_copies_direct(False), 7, g_proj, [lax.empty((N_DEV, 256, D), bf16) for _ in proj],
                             p["b_o"], "reduce_proj_start")
    u = _step_backward(s, t, wrg, smallw, p, p["conv_b"] + g_pending[4][0:1, 0:1])

    def siblings_start(gs, dep, tag):
        return _split_start(_copies_siblings, 4, gs, [lax.empty((4, *g.shape[1:]), bf16) for g in gs], dep,
                            "reduce_siblings_start_" + tag)

    def chips_start(gs, r1, dep, tag):
        parts = [_pair_sum(g, r, c_idx, "pair_sum_%s%d" % (tag, i)) for i, (g, r) in enumerate(zip(gs, r1))]
        return _split_start(_copies_chips, 3, parts, [lax.empty((3, *q.shape[1:]), bf16) for q in parts], dep,
                            "reduce_chips_start_" + tag)

    g_a, dz, db_in = _mm_dwin_parts(s["hb"], u["dz_parts"])
    shards = lambda g: g.reshape(N_DEV, SHARD_IN, W_IN_HALF)
    sib_a = siblings_start([shards(g_a), u["g_wrg"].reshape(N_DEV, 2 * RNN_BLOCK, RNN_BLOCK)], db_in, "a")
    g_proj, g_land = _split_wait(_copies_direct(False), *g_pending[:4], sib_a[4], "reduce_proj_wait")
    for i, name in enumerate(proj):
        res = _adamw_direct(g_proj[i], g_land[i], me_idx, w[name].reshape(two_d(name)), m[name].reshape(two_d(name)),
                            v[name].reshape(two_d(name)), "adamw_" + name)
        big[name] = tuple(r.reshape(w[name].shape) for r in res)
    chp_a = chips_start(*_split_wait(_copies_siblings, *sib_a[:4], big["w_attn_out"][3], "reduce_siblings_wait_a"),
                        db_in, "a")
    g_b = _mm_dwin(s["hb"], dz, chp_a[4])
    sib_b = siblings_start([shards(g_b)], db_in, "b")
    sm_e = _pack_early(u["vec_rnn"], t["st_out"], u["dsr"], db_in)
    early = _split_start(_copies_direct(True), 7, list(sm_e),
                         [lax.empty((N_DEV, *a.shape), f32) for a in sm_e], sib_b[4], "small_early_start")
    dh_lo = _mm_dh(dz, w_full, early[4], 0)
    chp_b = chips_start(*_split_wait(_copies_siblings, *sib_b[:4], dh_lo, "reduce_siblings_wait_b"), db_in, "b")
    dh_hi = _mm_dh(dz, w_full, chp_b[4], 1)
    parts_a, r2_a = _split_wait(_copies_chips, *chp_a[:4], dh_hi, "reduce_chips_wait_a")
    w_in_res = _adamw_big(parts_a[0], r2_a[0], q_idx, w_in_t(w["w_in"]), w_in_t(m["w_in"]), w_in_t(v["w_in"]),
                          "adamw_w_in_a", cols=(0, 2))
    u.update(_step_input_grad(dh_lo, dh_hi, t["du32"], x, smallw, p, w_in_res[3]))
    sm_l, meta_l = _small_allreduce(*_pack_late(u["st_emb"], u["dmeta"]))
    (sm_own, conv_own), (sm_land, conv_land) = _split_wait(_copies_direct(True), *early[:4], sm_l, "small_early_wait")
    me = _dev(px, py, pc)
    mine = lambda a, axis: lax.dynamic_index_in_dim(a, me, axis, keepdims=False)
    two = lambda name, t: t.reshape(_SMALL_2D.get(name, (1, D)))
    small, loss = _small_update(me_idx, (sm_own, sm_land, mine(conv_own, 0), mine(conv_land, 1)),
                                (sm_l, mine(meta_l, 0)),
                                {k: (two(k, w[k]), two(k, m[k]), two(k, v[k])) for k in _SMALL_NAMES})

    parts_b, r2_b = _split_wait(_copies_chips, *chp_b[:4], small["b_in"][2], "reduce_chips_wait_b")
    res = _adamw_big(parts_b[0], r2_b[0], q_idx, w_in_t(w["w_in"]), w_in_t(m["w_in"]), w_in_t(v["w_in"]),
                     "adamw_w_in_b", cols=(1, 2), prev=w_in_res)
    big["w_in"] = tuple(jnp.swapaxes(r.reshape(1, SHARD_IN, D), 1, 2) for r in res)
    for i, name in enumerate(("w_ra", "w_ri")):
        sq = (RNN_BLOCK, RNN_BLOCK)
        res = _adamw_big(parts_a[1], r2_a[1], q_idx, w[name].reshape(sq), m[name].reshape(sq), v[name].reshape(sq),
                         "adamw_" + name, row_off=i)
        big[name] = tuple(r.reshape(w[name].shape) for r in res)
    res = dict(big)
    for k in _SMALL_NAMES:
        res[k] = tuple(t.reshape(w[k].shape) for t in small[k])

    outs = [loss, u["grad_x"]]
    for j in range(4):
        outs += [res[k][j] for k in _WEIGHTS]
    return tuple(outs)
```

```python
import jax
import jax.numpy as jnp
from jax import lax
from jax.experimental import pallas as pl
from jax.experimental.pallas import tpu as pltpu

f32, bf16 = jnp.float32, jnp.bfloat16
SDS = jax.ShapeDtypeStruct

N_DEV = 8
D = 2048
N_META = 16
BLK = 128
ROW0 = BLK - N_META
N_RNN_BLOCKS = 8
RNN_BLOCK = D // N_RNN_BLOCKS
CONV_WIDTH = 4
LRU_C = 8.0
HEAD_DIM = 64
N_KV = 4
GROUP = 8
HALF = HEAD_DIM // 2
ROPE_THETA = 10000.0
NEG_INF = -1e30
LN_EPS = 1e-5
ALPHA = 2.0 ** 0.25
D_IN = 12800
SHARD_IN = D_IN // N_DEV
W_IN_HALF = D // 2
OFF_GR, OFF_Q, OFF_K, OFF_V, OFF_GA, OFF_G = 2048, 4096, 6144, 6400, 6656, 8704
ADAM_LR, ADAM_B1, ADAM_B2, ADAM_EPS, ADAM_WD, ADAM_STEP = 1e-3, 0.9, 0.999, 1e-8, 0.01, 10
VMEM_LIMIT_MB = 56
MESH = pl.DeviceIdType.MESH


def _cp(sem=None, vmem_mb=40):
    return pltpu.CompilerParams(dimension_semantics=sem, vmem_limit_bytes=vmem_mb * 2 ** 20)


def _row_chunk(m):
    best = 16
    for c in range(16, 641, 16):
        if m % c == 0:
            best = c
    return best


def _sigmoid(x):
    return 1.0 / (1.0 + jnp.exp(-x))


def _silu_and_grad(x):
    s = _sigmoid(x)
    return x * s, s * (1.0 + x * (1.0 - s))


def _log_sigmoid(x):
    return jnp.minimum(x, 0.0) - jnp.log1p(jnp.exp(-jnp.abs(x)))


def _ln_rows(v, g, b):
    mu = jnp.mean(v, axis=-1, keepdims=True)
    c = v - mu
    var = jnp.mean(c * c, axis=-1, keepdims=True)
    rstd = lax.rsqrt(var + LN_EPS)
    xhat = c * rstd
    return xhat * g + b, xhat, rstd


def _ln_rows_bwd(dy, g, xhat, rstd):
    dxh = dy * g
    m1 = jnp.mean(dxh, axis=-1, keepdims=True)
    m2 = jnp.mean(dxh * xhat, axis=-1, keepdims=True)
    return rstd * (dxh - m1 - xhat * m2)


def _colsum(v):
    return jnp.sum(v, axis=0, keepdims=True)


def _dot(a, b):
    return jnp.dot(a, b, preferred_element_type=f32)


def _dot_nt(a, b):
    return lax.dot_general(a, b, (((1,), (1,)), ((), ())), preferred_element_type=f32)


def _dot_tn(a, b):
    return lax.dot_general(a, b, (((0,), (0,)), ((), ())), preferred_element_type=f32)


def _meta_full(sw_ref):
    return jnp.concatenate([sw_ref[s, 0:N_META, :] for s in range(N_DEV)], axis=1)


def _ln_emb(x, smallw, g_e, b_e):
    seq = x.shape[1]
    rows = seq + BLK
    nb = rows // BLK

    def body(x_ref, sw_ref, g_ref, b_ref, h32_ref, hb_ref):
        i = pl.program_id(0)
        g, b = g_ref[...], b_ref[...]

        def emit(blk):
            h32_ref[...] = blk
            hb_ref[...] = blk.astype(bf16)

        @pl.when(i == 0)
        def _():
            hm = _ln_rows(_meta_full(sw_ref), g, b)[0]
            emit(jnp.concatenate([jnp.zeros((ROW0, D), f32), hm], axis=0))

        @pl.when(i > 0)
        def _():
            emit(_ln_rows(x_ref[0], g, b)[0])

    return pl.pallas_call(
        body, grid=(nb,),
        in_specs=[pl.BlockSpec((1, BLK, D), lambda i: (0, jnp.maximum(i - 1, 0), 0)),
                  pl.BlockSpec((N_DEV, 24, 256), lambda i: (0, 0, 0)),
                  pl.BlockSpec((1, D), lambda i: (0, 0)),
                  pl.BlockSpec((1, D), lambda i: (0, 0))],
        out_specs=[pl.BlockSpec((BLK, D), lambda i: (i, 0)),
                   pl.BlockSpec((BLK, D), lambda i: (i, 0))],
        out_shape=[SDS((rows, D), f32), SDS((rows, D), bf16)],
        name="ln_emb", compiler_params=_cp(("arbitrary",)),
    )(x, smallw, g_e, b_e)


def _ln_emb_bwd(dh_lo, dh_hi, du32, x, smallw, g_e, after):
    seq = x.shape[1]
    rows = seq + BLK
    nb = rows // BLK

    def body(dlo_ref, dhi_ref, du_ref, x_ref, sw_ref, g_ref, after_ref, gx_ref, dmeta_ref, st_ref):
        i = pl.program_id(0)
        g = g_ref[...]
        dht = jnp.concatenate([dlo_ref[...], dhi_ref[...]], axis=1) + ALPHA * du_ref[...]

        @pl.when(i == 0)
        def _():
            v = jnp.concatenate([jnp.zeros((ROW0, D), f32), _meta_full(sw_ref)], axis=0)
            valid = lax.broadcasted_iota(jnp.int32, (BLK, 1), 0) >= ROW0
            d = jnp.where(valid, dht, 0.0)
            _, xhat, rstd = _ln_rows(v, g, 0.0)
            dv = _ln_rows_bwd(d, g, xhat, rstd)
            dmeta_ref[...] = dv[ROW0:, :]
            st_ref[...] = jnp.concatenate([_colsum(d * xhat), _colsum(d), jnp.zeros((6, D), f32)], axis=0)

        @pl.when(i > 0)
        def _():
            _, xhat, rstd = _ln_rows(x_ref[0], g, 0.0)
            gx_ref[0] = _ln_rows_bwd(dht, g, xhat, rstd)
            st_ref[0:1, :] += _colsum(dht * xhat)
            st_ref[1:2, :] += _colsum(dht)

    return pl.pallas_call(
        body, grid=(nb,),
        in_specs=[pl.BlockSpec((BLK, W_IN_HALF), lambda i: (i, 0)),
                  pl.BlockSpec((BLK, W_IN_HALF), lambda i: (i, 0)),
                  pl.BlockSpec((BLK, D), lambda i: (i, 0)),
                  pl.BlockSpec((1, BLK, D), lambda i: (0, jnp.maximum(i - 1, 0), 0)),
                  pl.BlockSpec((N_DEV, 24, 256), lambda i: (0, 0, 0)),
                  pl.BlockSpec((1, D), lambda i: (0, 0)),
                  pl.BlockSpec(memory_space=pl.ANY)],
        out_specs=[pl.BlockSpec((1, BLK, D), lambda i: (0, jnp.maximum(i - 1, 0), 0)),
                   pl.BlockSpec((N_META, D), lambda i: (0, 0)),
                   pl.BlockSpec((8, D), lambda i: (0, 0))],
        out_shape=[SDS((1, seq, D), f32), SDS((N_META, D), f32), SDS((8, D), f32)],
        name="ln_emb_bwd", compiler_params=_cp(("arbitrary",)),
    )(dh_lo, dh_hi, du32, x, smallw, g_e, after)


def _mm(a, b, *, name, nt=False, sel=None, bias=None, out_dtype=f32, tn=512):
    m, k = a.shape
    cm = _row_chunk(m)
    stacked = sel is not None
    n = D if stacked else (b.shape[0] if nt else b.shape[1])
    am = m
    if stacked and nt:
        b_spec = pl.BlockSpec((tn // 256, None, 256, D), lambda j, i: (j, sel, 0, 0))
    elif stacked:
        b_spec = pl.BlockSpec((N_DEV, None, 256, tn), lambda j, i: (0, sel, 0, j))
    elif nt:
        b_spec = pl.BlockSpec((tn, k), lambda j, i: (j, 0))
    else:
        b_spec = pl.BlockSpec((k, tn), lambda j, i: (0, j))
    in_specs = [pl.BlockSpec((am, k), lambda j, i: (i, 0)), b_spec]
    args = [a, b]
    if bias is not None:
        in_specs.append(pl.BlockSpec((1, tn), lambda j, i: (0, j)))
        args.append(bias)

    def body(*refs):
        a_ref, b_ref, o_ref = refs[0], refs[1], refs[-1]
        bm = b_ref[...]
        if stacked:
            bm = bm.reshape((tn, D) if nt else (D, tn))
        for c in range(am // cm):
            acc = (_dot_nt if nt else _dot)(a_ref[c * cm:(c + 1) * cm, :], bm)
            if bias is not None:
                acc = acc + refs[2][...]
            o_ref[c * cm:(c + 1) * cm, :] = acc.astype(out_dtype)

    return pl.pallas_call(
        body, grid=(n // tn, m // am), in_specs=in_specs,
        out_specs=pl.BlockSpec((am, tn), lambda j, i: (i, j)),
        out_shape=SDS((m, n), out_dtype), name=name, compiler_params=_cp(("arbitrary", "arbitrary"), 48),
    )(*args)


def _mm_dh(dz, w_t, after, half):
    rows = dz.shape[0]
    tn = 512
    nt = W_IN_HALF // tn
    cm = _row_chunk(rows) // 2

    def body(a_ref, w_ref, after_ref, o_ref):
        o_ref[...] = _dot(a_ref[...], w_ref[...])

    return pl.pallas_call(
        body, grid=(nt, rows // cm),
        in_specs=[pl.BlockSpec((cm, D_IN), lambda j, i: (i, 0)),
                  pl.BlockSpec((D_IN, tn), lambda j, i: (0, half * nt + j)),
                  pl.BlockSpec(memory_space=pl.ANY)],
        out_specs=pl.BlockSpec((cm, tn), lambda j, i: (i, j)),
        out_shape=SDS((rows, W_IN_HALF), f32), name="mm_dh_%d" % half,
        compiler_params=_cp(("arbitrary", "arbitrary"), 48),
    )(dz, w_t, after)


def _mm_dwin_parts(hb, parts):
    rows = hb.shape[0]
    tc = 512
    edges = [0]
    for _, w in parts:
        edges.append(edges[-1] + w // tc)

    def body(*refs):
        h_ref, (o_ref, dz_ref, db_ref) = refs[len(parts)], refs[len(parts) + 1:]
        j = pl.program_id(0)
        for p_ref, lo, hi in zip(refs, edges[:-1], edges[1:]):
            @pl.when((j >= lo) & (j < hi))
            def _():
                o_ref[...] = _dot_tn(p_ref[...], h_ref[...]).astype(bf16)
                dz_ref[...] = p_ref[...]

                def step(i, s):
                    blk = p_ref[pl.ds(pl.multiple_of(i * BLK, BLK), BLK), :].astype(f32)
                    return s + blk.reshape(BLK // 8, 8, tc).sum(axis=0)
                s = lax.fori_loop(0, rows // BLK, step, jnp.zeros((8, tc), f32))
                db_ref[...] = jnp.broadcast_to(_colsum(s), (8, tc))

    in_specs = [pl.BlockSpec((rows, tc), lambda j, lo=lo, hi=hi: (0, jnp.clip(j - lo, 0, hi - lo - 1)))
                for lo, hi in zip(edges[:-1], edges[1:])]
    return pl.pallas_call(
        body, grid=(D_IN // tc,),
        in_specs=in_specs + [pl.BlockSpec((rows, W_IN_HALF), lambda j: (0, 0))],
        out_specs=[pl.BlockSpec((tc, W_IN_HALF), lambda j: (j, 0)), pl.BlockSpec((rows, tc), lambda j: (0, j)),
                   pl.BlockSpec((8, tc), lambda j: (0, j))],
        out_shape=[SDS((D_IN, W_IN_HALF), bf16), SDS((rows, D_IN), bf16), SDS((8, D_IN), f32)],
        name="mm_dwin_0", compiler_params=_cp(("arbitrary",), VMEM_LIMIT_MB),
    )(*[a for a, _ in parts], hb)


def _mm_dwin(hb, dz, after):
    rows = dz.shape[0]
    tc = 640

    def body(dz_ref, h_ref, after_ref, o_ref):
        o_ref[...] = _dot_tn(dz_ref[...], h_ref[...]).astype(bf16)

    return pl.pallas_call(
        body, grid=(D_IN // tc,),
        in_specs=[pl.BlockSpec((rows, tc), lambda j: (0, j)),
                  pl.BlockSpec((rows, W_IN_HALF), lambda j: (0, 1)),
                  pl.BlockSpec(memory_space=pl.ANY)],
        out_specs=pl.BlockSpec((tc, W_IN_HALF), lambda j: (j, 0)),
        out_shape=SDS((D_IN, W_IN_HALF), bf16),
        name="mm_dwin_1", compiler_params=_cp(("arbitrary",), 48),
    )(dz, hb, after)


SCAN_ROWS = 32


def _scan8(a, b, reverse):
    idx = lax.broadcasted_iota(jnp.int32, a.shape, 0)
    for s in (1, 2, 4):
        sh = 8 - s if reverse else s
        a_sh, b_sh = pltpu.roll(a, sh, 0), pltpu.roll(b, sh, 0)
        m = (idx < 8 - s) if reverse else (idx >= s)
        b = jnp.where(m, a * b_sh + b, b)
        a = jnp.where(m, a * a_sh, a)
    return a, b


def _shift_rows(prev8, cur, k):
    ext = jnp.concatenate([prev8, cur], axis=0)
    return pltpu.roll(ext, k, 0)[8:, :]


def _gates(xc, w_ra, b_ra, w_ri, b_ri, ls):
    xb = xc.astype(bf16)
    r = _sigmoid(_dot(xb, w_ra) + b_ra)
    ig = _sigmoid(_dot(xb, w_ri) + b_ri)
    la = LRU_C * r * ls
    a = jnp.exp(la)
    mult = jnp.sqrt(jnp.tanh(-la) * (1.0 + a * a))
    return xb, r, ig, a, mult


_RNN_IN_SPECS = lambda rows: [
    pl.BlockSpec((1, 24, 256), lambda n: (n, 0, 0)),
    pl.BlockSpec((1, RNN_BLOCK), lambda n: (0, n)),
    pl.BlockSpec((N_DEV, 2, None, 32, RNN_BLOCK), lambda n: (0, 0, n, 0, 0)),
    pl.BlockSpec((1, RNN_BLOCK), lambda n: (0, n)),
    pl.BlockSpec((1, RNN_BLOCK), lambda n: (0, n)),
    pl.BlockSpec((1, RNN_BLOCK), lambda n: (0, n)),
]


def _rnn_fwd(z, smallw, conv_b, wrg, b_ra, b_ri, lam):
    rows = z.shape[0]
    nb = rows // BLK
    col = lambda off: pl.BlockSpec((rows, RNN_BLOCK), lambda n: (0, off // RNN_BLOCK + n))

    def body(xr_ref, gr_ref, sw_ref, cb_ref, w_ref, bra_ref, bri_ref, lam_ref, xc_ref, hr_ref, ya_ref, yat_ref, a_s):
        cw = sw_ref[0, N_META:24, :]
        cb = cb_ref[...]
        w_ra = w_ref[:, 0].reshape(RNN_BLOCK, RNN_BLOCK)
        w_ri = w_ref[:, 1].reshape(RNN_BLOCK, RNN_BLOCK)
        b_ra_v, b_ri_v = bra_ref[...], bri_ref[...]
        ls = _log_sigmoid(lam_ref[...])
        rid = lax.broadcasted_iota(jnp.int32, (BLK, 1), 0)

        def blk_step(i, carry):
            r0 = pl.multiple_of(i * BLK, BLK)
            grow = rid + r0
            valid = grow >= ROW0
            cur = jnp.where(valid, xr_ref[pl.ds(r0, BLK), :], 0.0)
            prev8 = xr_ref[pl.ds(pl.multiple_of(jnp.maximum(r0 - 8, 0), 8), 8), :] * (i > 0).astype(f32)
            xc = cb + cw[0:1] * cur
            for k in range(1, CONV_WIDTH):
                xc = xc + cw[k:k + 1] * _shift_rows(prev8, cur, k)
            xc_ref[pl.ds(r0, BLK), :] = xc
            _, _, ig, a, mult = _gates(xc, w_ra, b_ra_v, w_ri, b_ri_v, ls)
            mult = jnp.where(grow == ROW0, 1.0, mult)
            a_s[pl.ds(r0, BLK), :] = a
            hr_ref[pl.ds(r0, BLK), :] = jnp.where(valid, mult * ig * xc, 0.0)
            return carry

        lax.fori_loop(0, nb, blk_step, 0)

        def scan_step(j, carry):
            r0 = pl.multiple_of(j * SCAN_ROWS, SCAN_ROWS)
            tiles = [_scan8(a_s[pl.ds(r0 + 8 * k, 8), :], hr_ref[pl.ds(r0 + 8 * k, 8), :], False)
                     for k in range(SCAN_ROWS // 8)]
            for k, (a, b) in enumerate(tiles):
                h = b + a * carry
                hr_ref[pl.ds(r0 + 8 * k, 8), :] = h
                carry = jnp.broadcast_to(h[7:8, :], (8, RNN_BLOCK))
            return carry

        lax.fori_loop(0, rows // SCAN_ROWS, scan_step, jnp.zeros((8, RNN_BLOCK), f32))

        def gate_step(i, carry):
            r0 = pl.multiple_of(i * BLK, BLK)
            ya_ref[pl.ds(r0, BLK), :] = (hr_ref[pl.ds(r0, BLK), :]
                                         * _silu_and_grad(gr_ref[pl.ds(r0, BLK), :])[0]).astype(bf16)
            return carry

        lax.fori_loop(0, nb, gate_step, 0)
        yat_ref[...] = ya_ref[...].astype(f32).T.astype(bf16)

    return pl.pallas_call(
        body, grid=(N_RNN_BLOCKS,),
        in_specs=[col(0), col(OFF_GR)] + _RNN_IN_SPECS(rows),
        out_specs=[pl.BlockSpec((rows, RNN_BLOCK), lambda n: (0, n))] * 3
                  + [pl.BlockSpec((RNN_BLOCK, rows), lambda n: (n, 0))],
        out_shape=[SDS((rows, D), f32), SDS((rows, D), f32), SDS((rows, D), bf16), SDS((D, rows), bf16)],
        scratch_shapes=[pltpu.VMEM((rows, RNN_BLOCK), f32)],
        name="rnn_fwd", compiler_params=_cp(("arbitrary",)),
    )(z, z, smallw, conv_b, wrg, b_ra, b_ri, lam)


def _rnn_bwd(dya, hr, xc, z, smallw, conv_b, wrg, b_ra, b_ri, lam):
    rows = z.shape[0]
    nb = rows // BLK
    col = lambda off: pl.BlockSpec((rows, RNN_BLOCK), lambda n: (0, off // RNN_BLOCK + n))
    blk = pl.BlockSpec((rows, RNN_BLOCK), lambda n: (0, n))

    def body(dya_ref, hr_ref, xc_ref, xr_ref, gr_ref, sw_ref, cb_ref, w_ref, bra_ref, bri_ref, lam_ref,
             dxr_ref, dgr_ref, dw_ref, vec_ref, a_s, lam_s, dxc_s, r_s, ig_s, mult_s, dw_s):
        cw = sw_ref[0, N_META:24, :]
        w_ra = w_ref[:, 0].reshape(RNN_BLOCK, RNN_BLOCK)
        w_ri = w_ref[:, 1].reshape(RNN_BLOCK, RNN_BLOCK)
        b_ra_v, b_ri_v = bra_ref[...], bri_ref[...]
        lam_v = lam_ref[...]
        ls = _log_sigmoid(lam_v)
        rid = lax.broadcasted_iota(jnp.int32, (BLK, 1), 0)
        zrow = jnp.zeros((1, RNN_BLOCK), f32)

        def p1(i, carry):
            r0 = pl.multiple_of(i * BLK, BLK)
            sl = pl.ds(r0, BLK)
            _, r, ig, a, mult = _gates(xc_ref[sl, :], w_ra, b_ra_v, w_ri, b_ri_v, ls)
            a_s[sl, :] = a
            r_s[sl, :] = r
            ig_s[sl, :] = ig
            mult_s[sl, :] = mult
            sg, dsg = _silu_and_grad(gr_ref[sl, :])
            d = dya_ref[sl, :]
            lam_s[sl, :] = d * sg
            dgr_ref[sl, :] = (d * hr_ref[sl, :] * dsg).astype(bf16)
            return carry

        lax.fori_loop(0, nb, p1, 0)

        def p2(jj, carry):
            r0 = pl.multiple_of((rows // SCAN_ROWS - 1 - jj) * SCAN_ROWS, SCAN_ROWS)
            idx = lax.broadcasted_iota(jnp.int32, (8, RNN_BLOCK), 0)
            tiles = []
            for k in range(SCAN_ROWS // 8):
                sl = pl.ds(r0 + 8 * k, 8)
                a, g = a_s[sl, :], lam_s[sl, :]
                tiles.append((g, *_scan8(a, a * g, True)))
            for k in reversed(range(SCAN_ROWS // 8)):
                g, ca, cb_ = tiles[k]
                mu = cb_ + ca * carry
                lam_s[pl.ds(r0 + 8 * k, 8), :] = g + jnp.where(idx < 7, pltpu.roll(mu, 7, 0), carry)
                carry = jnp.broadcast_to(mu[0:1, :], (8, RNN_BLOCK))
            return carry

        lax.fori_loop(0, rows // SCAN_ROWS, p2, jnp.zeros((8, RNN_BLOCK), f32))

        dw_s[...] = jnp.zeros_like(dw_s)

        def p3(i, carry):
            d_bra, d_bri, d_ls = carry
            r0 = pl.multiple_of(i * BLK, BLK)
            sl = pl.ds(r0, BLK)
            grow = rid + r0
            valid = grow >= ROW0
            first = grow == ROW0
            xcv = xc_ref[sl, :]
            xb = xcv.astype(bf16)
            r, ig, a = r_s[sl, :], ig_s[sl, :], a_s[sl, :]
            mult = jnp.where(first, 1.0, mult_s[sl, :])
            lam_t = lam_s[sl, :]
            du = jnp.where(valid, lam_t, 0.0)
            hprev = _shift_rows(hr_ref[pl.ds(pl.multiple_of(jnp.maximum(r0 - 8, 0), 8), 8), :] * (i > 0).astype(f32), hr_ref[sl, :], 1)
            da = lam_t * hprev
            dmult = jnp.where(first, 0.0, du * ig * xcv)
            di = du * mult * xcv
            dxc = du * mult * ig
            ratio = jnp.where(valid & jnp.logical_not(first), a * a / mult, 0.0)
            dla = da * a - dmult * ratio
            dpr = (dla * (LRU_C * ls)) * r * (1.0 - r)
            dpi = di * ig * (1.0 - ig)
            dprb, dpib = dpr.astype(bf16), dpi.astype(bf16)
            dw_s[0] += _dot_tn(xb, dprb)
            dw_s[1] += _dot_tn(xb, dpib)
            dxc_s[sl, :] = dxc + _dot_nt(dprb, w_ra) + _dot_nt(dpib, w_ri)
            return d_bra + _colsum(dpr), d_bri + _colsum(dpi), d_ls + _colsum(dla * (LRU_C * r))

        d_bra, d_bri, d_ls = lax.fori_loop(0, nb, p3, (zrow, zrow, zrow))

        def p4(i, carry):
            d_cb, d_w0, d_w1, d_w2, d_w3 = carry
            r0 = pl.multiple_of(i * BLK, BLK)
            sl = pl.ds(r0, BLK)
            grow = rid + r0
            valid = grow >= ROW0
            dxc = dxc_s[sl, :]
            nxt = dxc_s[pl.ds(pl.multiple_of(jnp.minimum(r0 + BLK, rows - 8), 8), 8), :] * (i < nb - 1).astype(f32)
            ext = jnp.concatenate([dxc, nxt], axis=0)
            dxr = cw[0:1] * dxc
            for k in range(1, CONV_WIDTH):
                dxr = dxr + cw[k:k + 1] * pltpu.roll(ext, BLK + 8 - k, 0)[:BLK, :]
            dxr_ref[sl, :] = jnp.where(valid, dxr, 0.0).astype(bf16)
            cur = jnp.where(valid, xr_ref[sl, :], 0.0)
            prev8 = xr_ref[pl.ds(pl.multiple_of(jnp.maximum(r0 - 8, 0), 8), 8), :] * (i > 0).astype(f32)
            dws = [d_w0 + _colsum(dxc * cur)]
            for k, acc in ((1, d_w1), (2, d_w2), (3, d_w3)):
                dws.append(acc + _colsum(dxc * _shift_rows(prev8, cur, k)))
            return (d_cb + _colsum(dxc), *dws)

        d_cb, d_w0, d_w1, d_w2, d_w3 = lax.fori_loop(0, nb, p4, (zrow,) * 5)

        d_lam = d_ls * _sigmoid(-lam_v)
        vec_ref[...] = jnp.concatenate([d_bra, d_bri, d_lam, d_cb, d_w0, d_w1, d_w2, d_w3], axis=0)
        dw_ref[:, 0] = dw_s[0].astype(bf16).reshape(N_DEV, 32, RNN_BLOCK)
        dw_ref[:, 1] = dw_s[1].astype(bf16).reshape(N_DEV, 32, RNN_BLOCK)

    return pl.pallas_call(
        body, grid=(N_RNN_BLOCKS,),
        in_specs=[blk, blk, blk, col(0), col(OFF_GR)] + _RNN_IN_SPECS(rows),
        out_specs=[blk, blk,
                   pl.BlockSpec((N_DEV, 2, None, 32, RNN_BLOCK), lambda n: (0, 0, n, 0, 0)),
                   pl.BlockSpec((8, RNN_BLOCK), lambda n: (0, n))],
        out_shape=[SDS((rows, D), bf16), SDS((rows, D), bf16),
                   SDS((N_DEV, 2, N_RNN_BLOCKS, 32, RNN_BLOCK), bf16), SDS((8, D), f32)],
        scratch_shapes=[pltpu.VMEM((rows, RNN_BLOCK), f32)] * 6 + [pltpu.VMEM((2, RNN_BLOCK, RNN_BLOCK), f32)],
        name="rnn_bwd", compiler_params=_cp(("arbitrary",), 48),
    )(dya, hr, xc, z, z, smallw, conv_b, wrg, b_ra, b_ri, lam)


def _rope_tables(rows):
    half = jnp.arange(HALF, dtype=f32)
    inv = ROPE_THETA ** (-half / HALF)
    pos = (jnp.arange(rows) - ROW0).astype(f32)
    ang = pos[:, None] * inv[None, :]
    cos, sin = jnp.cos(ang), jnp.sin(ang)
    cos128 = jnp.concatenate([cos, cos, cos, cos], axis=1)
    sin128 = jnp.concatenate([-sin, sin, -sin, sin], axis=1)
    return cos128, sin128


def _rope128(x, cos128, sin128):
    lane = lax.broadcasted_iota(jnp.int32, x.shape, 1)
    swapped = jnp.where(lane % HEAD_DIM < HALF, pltpu.roll(x, 128 - HALF, 1), pltpu.roll(x, HALF, 1))
    return x * cos128 + swapped * sin128


def _qkv_prep(z, cos128, sin128):
    rows = z.shape[0]

    def body(q_ref, kv_ref, c_ref, s_ref, qo_ref, ko_ref, vo_ref):
        c, s = c_ref[...], s_ref[...]
        for g in range(D // 128):
            qo_ref[:, g * 128:(g + 1) * 128] = (_rope128(q_ref[:, g * 128:(g + 1) * 128], c, s)
                                                * (HEAD_DIM ** -0.5)).astype(bf16)
        for g in range(2):
            kr = _rope128(kv_ref[:, g * 128:(g + 1) * 128], c, s)
            for j in range(2):
                ko_ref[2 * g + j] = kr[:, j * HEAD_DIM:(j + 1) * HEAD_DIM].astype(bf16)
        for h in range(N_KV):
            vo_ref[h] = kv_ref[:, 256 + h * HEAD_DIM:256 + (h + 1) * HEAD_DIM].astype(bf16)

    return pl.pallas_call(
        body, grid=(rows // BLK,),
        in_specs=[pl.BlockSpec((BLK, D), lambda i: (i, OFF_Q // D)),
                  pl.BlockSpec((BLK, 512), lambda i: (i, OFF_K // 512)),
                  pl.BlockSpec((BLK, 128), lambda i: (i, 0)),
                  pl.BlockSpec((BLK, 128), lambda i: (i, 0))],
        out_specs=[pl.BlockSpec((BLK, D), lambda i: (i, 0)),
                   pl.BlockSpec((N_KV, BLK, HEAD_DIM), lambda i: (0, i, 0)),
                   pl.BlockSpec((N_KV, BLK, HEAD_DIM), lambda i: (0, i, 0))],
        out_shape=[SDS((rows, D), bf16), SDS((N_KV, rows, HEAD_DIM), bf16), SDS((N_KV, rows, HEAD_DIM), bf16)],
        name="qkv_prep", compiler_params=_cp(("arbitrary",)),
    )(z, z, cos128, sin128)


def _attn_mask(n):
    qi = n * BLK + lax.broadcasted_iota(jnp.int32, (BLK, 2 * BLK + N_META), 0)
    c = lax.broadcasted_iota(jnp.int32, (BLK, 2 * BLK + N_META), 1)
    jb = (n - 1) * BLK + c
    band = (jb >= BLK) & (jb <= qi) & (qi - jb < BLK)
    meta = (ROW0 + c - 2 * BLK) <= qi
    return ((c < 2 * BLK) & band) | ((c >= 2 * BLK) & meta)


N_KEYS = 2 * BLK + N_META


def _stack_heads(t):
    return jnp.concatenate([t[:, g * HEAD_DIM:(g + 1) * HEAD_DIM] for g in range(GROUP)], axis=0)


def _sink_column(sink_ref, h):
    g = lax.broadcasted_iota(jnp.int32, (GROUP, 1, 1), 0)
    col = jnp.zeros((GROUP, 1, 1), f32)
    for j in range(GROUP):
        col = jnp.where(g == j, sink_ref[h * GROUP + j], col)
    return col


def _kv_specs(last):
    cl = lambda n: jnp.minimum(n, last)
    return [pl.BlockSpec((None, N_META, HEAD_DIM), lambda h, n: (h, ROW0 // N_META, 0)),
            pl.BlockSpec((None, BLK, HEAD_DIM), lambda h, n: (h, jnp.maximum(cl(n) - 1, 0), 0)),
            pl.BlockSpec((None, BLK, HEAD_DIM), lambda h, n: (h, cl(n), 0))]


def _attn_fwd(q_r, k_r, v_b, z, sinks):
    rows = q_r.shape[0]
    nb = rows // BLK

    def body(sink_ref, q_ref, km_ref, kp_ref, kc_ref, vm_ref, vp_ref, vc_ref, ga_ref, o_ref, yb_ref, ybt_ref, lse_ref):
        h, n = pl.program_id(0), pl.program_id(1)
        kk = jnp.concatenate([kp_ref[...], kc_ref[...], km_ref[...]], axis=0)
        vv = jnp.concatenate([vp_ref[...], vc_ref[...], vm_ref[...]], axis=0)
        q2 = _stack_heads(q_ref[...])
        s = jnp.where(_attn_mask(n)[None], _dot_nt(q2, kk).reshape(GROUP, BLK, N_KEYS), NEG_INF)
        sink = _sink_column(sink_ref, h)
        m = jnp.maximum(jnp.max(s, axis=-1, keepdims=True), sink)
        p = jnp.exp(s - m)
        den = jnp.sum(p, axis=-1, keepdims=True) + jnp.exp(sink - m)
        o2 = _dot((p / den).astype(bf16).reshape(GROUP * BLK, N_KEYS), vv)
        lse = m + jnp.log(den)
        for g in range(GROUP):
            o_ref[:, g * HEAD_DIM:(g + 1) * HEAD_DIM] = o2[g * BLK:(g + 1) * BLK]
            lse_ref[:, g:g + 1] = lse[g]
        yb = o_ref[...] * _silu_and_grad(ga_ref[...])[0]
        yb_ref[...] = yb.astype(bf16)
        ybt_ref[...] = yb.T.astype(bf16)

    tile = pl.BlockSpec((BLK, 512), lambda h, n: (n, h))
    return pl.pallas_call(
        body, grid=(N_KV, nb),
        in_specs=[pl.BlockSpec(memory_space=pltpu.SMEM), tile] + _kv_specs(nb - 1) + _kv_specs(nb - 1)
                 + [pl.BlockSpec((BLK, 512), lambda h, n: (n, OFF_GA // 512 + h))],
        out_specs=[tile, tile, pl.BlockSpec((512, BLK), lambda h, n: (h, n)),
                   pl.BlockSpec((None, BLK, GROUP), lambda h, n: (h, n, 0))],
        out_shape=[SDS((rows, D), f32), SDS((rows, D), bf16), SDS((D, rows), bf16),
                   SDS((N_KV, rows, GROUP), f32)],
        name="attn_fwd", compiler_params=_cp(("arbitrary", "arbitrary")),
    )(sinks, q_r, k_r, k_r, k_r, v_b, v_b, v_b, z)


def _attn_bwd(dyb, o32, lse, q_r, k_r, v_b, z, sinks):
    rows = q_r.shape[0]
    nb = rows // BLK
    cl = lambda n: jnp.minimum(n, nb - 1)

    def body(sink_ref, dyb_ref, o_ref, lse_ref, q_ref, km_ref, kp_ref, kc_ref, vm_ref, vp_ref, vc_ref, ga_ref,
             dq_ref, dga_ref, dk_ref, dv_ref, dkm_ref, dvm_ref, dsr_ref, ck_s, cv_s):
        h, n = pl.program_id(0), pl.program_id(1)

        @pl.when(n == 0)
        def _():
            dkm_ref[...] = jnp.zeros_like(dkm_ref)
            dvm_ref[...] = jnp.zeros_like(dvm_ref)
            ck_s[...] = jnp.zeros_like(ck_s)
            cv_s[...] = jnp.zeros_like(cv_s)

        @pl.when(n < nb)
        def _():
            kk = jnp.concatenate([kp_ref[...], kc_ref[...], km_ref[...]], axis=0)
            vv = jnp.concatenate([vp_ref[...], vc_ref[...], vm_ref[...]], axis=0)
            sg, dsg = _silu_and_grad(ga_ref[...])
            dyb_v = dyb_ref[...]
            o_v = o_ref[...]
            dga_ref[...] = (dyb_v * o_v * dsg).astype(bf16)
            q2 = _stack_heads(q_ref[...])
            do2 = _stack_heads(dyb_v * sg)
            lse_v = lse_ref[...]
            lse = jnp.concatenate([lse_v[:, g:g + 1] for g in range(GROUP)], axis=0).reshape(GROUP, BLK, 1)
            delta = jnp.sum(do2 * _stack_heads(o_v), axis=-1, keepdims=True).reshape(GROUP, BLK, 1)
            s = jnp.where(_attn_mask(n)[None], _dot_nt(q2, kk).reshape(GROUP, BLK, N_KEYS), NEG_INF)
            p = jnp.exp(s - lse)
            do2b = do2.astype(bf16)
            ds = (p * (_dot_nt(do2b, vv).reshape(GROUP, BLK, N_KEYS) - delta)).astype(bf16)
            ds = ds.reshape(GROUP * BLK, N_KEYS)
            dsr = -jnp.exp(_sink_column(sink_ref, h) - lse) * delta
            dq2 = _dot(ds, kk)
            for g in range(GROUP):
                dq_ref[:, g * HEAD_DIM:(g + 1) * HEAD_DIM] = dq2[g * BLK:(g + 1) * BLK]
                dsr_ref[:, g:g + 1] = dsr[g]
            dkk = _dot_tn(ds, q2)
            dvv = _dot_tn(p.astype(bf16).reshape(GROUP * BLK, N_KEYS), do2b)
            dk_ref[...] = ck_s[...] + dkk[:BLK]
            dv_ref[...] = cv_s[...] + dvv[:BLK]
            ck_s[...] = dkk[BLK:2 * BLK]
            cv_s[...] = dvv[BLK:2 * BLK]
            dkm_ref[...] += dkk[2 * BLK:]
            dvm_ref[...] += dvv[2 * BLK:]

        @pl.when(n == nb)
        def _():
            dk_ref[...] = ck_s[...]
            dv_ref[...] = cv_s[...]

    tile = pl.BlockSpec((BLK, 512), lambda h, n: (cl(n), h))
    kvout = pl.BlockSpec((None, BLK, HEAD_DIM), lambda h, n: (h, jnp.maximum(n - 1, 0), 0))
    mout = pl.BlockSpec((None, N_META, HEAD_DIM), lambda h, n: (h, 0, 0))
    stat = pl.BlockSpec((None, BLK, GROUP), lambda h, n: (h, cl(n), 0))
    return pl.pallas_call(
        body, grid=(N_KV, nb + 1),
        in_specs=[pl.BlockSpec(memory_space=pltpu.SMEM), tile, tile, stat, tile] + _kv_specs(nb - 1)
                 + _kv_specs(nb - 1) + [pl.BlockSpec((BLK, 512), lambda h, n: (cl(n), OFF_GA // 512 + h))],
        out_specs=[tile, tile, kvout, kvout, mout, mout, stat],
        out_shape=[SDS((rows, D), f32), SDS((rows, D), bf16),
                   SDS((N_KV, rows, HEAD_DIM), f32), SDS((N_KV, rows, HEAD_DIM), f32),
                   SDS((N_KV, N_META, HEAD_DIM), f32), SDS((N_KV, N_META, HEAD_DIM), f32),
                   SDS((N_KV, rows, GROUP), f32)],
        scratch_shapes=[pltpu.VMEM((BLK, HEAD_DIM), f32), pltpu.VMEM((BLK, HEAD_DIM), f32)],
        name="attn_bwd", compiler_params=_cp(("arbitrary", "arbitrary")),
    )(sinks, dyb, o32, lse, q_r, k_r, k_r, k_r, v_b, v_b, v_b, z)


def _qkv_finish(dq, dk, dv, dkm, dvm, cos128, sin128):
    rows = dq.shape[0]

    def body(dq_ref, dk_ref, dv_ref, dkm_ref, dvm_ref, c_ref, s_ref, oq_ref, okv_ref):
        first = (pl.program_id(0) == 0).astype(f32)
        c, s = c_ref[...], -s_ref[...]
        for g in range(D // 128):
            oq_ref[:, g * 128:(g + 1) * 128] = (_rope128(dq_ref[:, g * 128:(g + 1) * 128], c, s)
                                                * (HEAD_DIM ** -0.5)).astype(bf16)
        pad = jnp.zeros((ROW0, HEAD_DIM), f32)
        ks = [dk_ref[h] + first * jnp.concatenate([pad, dkm_ref[h]], axis=0) for h in range(N_KV)]
        vs = [dv_ref[h] + first * jnp.concatenate([pad, dvm_ref[h]], axis=0) for h in range(N_KV)]
        for g in range(2):
            kp = jnp.concatenate([ks[2 * g], ks[2 * g + 1]], axis=1)
            okv_ref[:, g * 128:(g + 1) * 128] = _rope128(kp, c, s).astype(bf16)
            okv_ref[:, 256 + g * 128:256 + (g + 1) * 128] = jnp.concatenate([vs[2 * g], vs[2 * g + 1]], axis=1).astype(bf16)

    kv = pl.BlockSpec((N_KV, BLK, HEAD_DIM), lambda i: (0, i, 0))
    mt = pl.BlockSpec((N_KV, N_META, HEAD_DIM), lambda i: (0, 0, 0))
    return pl.pallas_call(
        body, grid=(rows // BLK,),
        in_specs=[pl.BlockSpec((BLK, D), lambda i: (i, 0)), kv, kv, mt, mt,
                  pl.BlockSpec((BLK, 128), lambda i: (i, 0)), pl.BlockSpec((BLK, 128), lambda i: (i, 0))],
        out_specs=[pl.BlockSpec((BLK, D), lambda i: (i, 0)), pl.BlockSpec((BLK, 512), lambda i: (i, 0))],
        out_shape=[SDS((rows, D), bf16), SDS((rows, 512), bf16)],
        name="qkv_finish", compiler_params=_cp(("arbitrary",)),
    )(dq, dk, dv, dkm, dvm, cos128, sin128)


_TW = 512


def _mix_specs(rows):
    tr = _row_chunk(rows)
    tile = pl.BlockSpec((tr, _TW), lambda i, j: (i, j))
    ga = pl.BlockSpec((tr, _TW), lambda i, j: (i, OFF_G // _TW + j))
    gb = pl.BlockSpec((tr, _TW), lambda i, j: (i, (OFF_G + D) // _TW + j))
    return (rows // tr, D // _TW), tile, ga, gb


def _mix_fwd(y_a, y_b, z):
    rows = y_a.shape[0]
    tw = 256
    col = lambda off: pl.BlockSpec((rows, tw), lambda j: (0, off // tw + j))

    def body(ya_ref, yb_ref, ga_ref, gb_ref, o_ref, ot_ref):
        mixed = (_sigmoid(ga_ref[...]) * ya_ref[...].astype(f32)
                 + _sigmoid(gb_ref[...]) * yb_ref[...].astype(f32))
        o_ref[...] = mixed.astype(bf16)
        ot_ref[...] = mixed.T.astype(bf16)

    return pl.pallas_call(
        body, grid=(D // tw,), in_specs=[col(0), col(0), col(OFF_G), col(OFF_G + D)],
        out_specs=[col(0), pl.BlockSpec((tw, rows), lambda j: (j, 0))],
        out_shape=[SDS((rows, D), bf16), SDS((D, rows), bf16)],
        name="mix_fwd", compiler_params=_cp(("arbitrary",)),
    )(y_a, y_b, z, z)


def _mix_bwd(dmixed, y_a, y_b, z):
    rows = y_a.shape[0]
    grid, _mix_tile, _mix_ga, _mix_gb = _mix_specs(rows)

    def body(dm_ref, ya_ref, yb_ref, ga_ref, gb_ref, dya_ref, dyb_ref, dga_ref, dgb_ref):
        dm = dm_ref[...].astype(f32)
        sa, sb = _sigmoid(ga_ref[...]), _sigmoid(gb_ref[...])
        dya_ref[...] = (dm * sa).astype(bf16)
        dyb_ref[...] = (dm * sb).astype(bf16)
        dga_ref[...] = (dm * ya_ref[...].astype(f32) * sa * (1.0 - sa)).astype(bf16)
        dgb_ref[...] = (dm * yb_ref[...].astype(f32) * sb * (1.0 - sb)).astype(bf16)

    return pl.pallas_call(
        body, grid=grid, in_specs=[_mix_tile, _mix_tile, _mix_tile, _mix_ga, _mix_gb],
        out_specs=[_mix_tile] * 4, out_shape=[SDS((rows, D), bf16)] * 4,
        name="mix_bwd", compiler_params=_cp(("arbitrary", "arbitrary")),
    )(dmixed, y_a, y_b, z, z)


def _final_ln(out32, h32, tgt, ln_g, ln_b):
    rows = out32.shape[0]

    def body(o_ref, h_ref, t_ref, g_ref, b_ref, du_ref, dub_ref, st_ref):
        i = pl.program_id(0)
        g = g_ref[...]
        y, xhat, rstd = _ln_rows(ALPHA * h_ref[...] + o_ref[...], g, b_ref[...])
        e = jnp.where(i > 0, y - t_ref[0], 0.0)
        dy = e * (1.0 / D)
        du = _ln_rows_bwd(dy, g, xhat, rstd)
        du_ref[...] = du
        dub_ref[...] = du.astype(bf16)
        st = jnp.concatenate([_colsum(dy * xhat), _colsum(dy), _colsum(du), _colsum(e * e) * (0.5 / D),
                              jnp.zeros((4, D), f32)], axis=0)

        @pl.when(i == 0)
        def _():
            st_ref[...] = st

        @pl.when(i > 0)
        def _():
            st_ref[...] += st

    row = pl.BlockSpec((BLK, D), lambda i: (i, 0))
    vec = pl.BlockSpec((1, D), lambda i: (0, 0))
    return pl.pallas_call(
        body, grid=(rows // BLK,),
        in_specs=[row, row, pl.BlockSpec((1, BLK, D), lambda i: (0, jnp.maximum(i - 1, 0), 0)), vec, vec],
        out_specs=[row, row, pl.BlockSpec((8, D), lambda i: (0, 0))],
        out_shape=[SDS((rows, D), f32), SDS((rows, D), bf16), SDS((8, D), f32)],
        name="final_ln", compiler_params=_cp(("arbitrary",)),
    )(out32, h32, tgt, ln_g, ln_b)


def _step_rnn(h32, hb, z, wrg, smallw, p, zero):
    rows = z.shape[0]
    cos128, sin128 = _rope_tables(rows)
    cos128 = cos128 + zero
    xc, hr, ya, ya_t = _rnn_fwd(z, smallw, p["conv_b"] + zero, wrg, p["b_ra"], p["b_ri"], p["lru_lambda"])
    q_r, k_r, v_b = _qkv_prep(z, cos128, sin128)
    return dict(cos128=cos128, sin128=sin128, h32=h32, hb=hb, z=z, xc=xc, hr=hr, ya=ya, ya_t=ya_t,
                q_r=q_r, k_r=k_r, v_b=v_b)


def _step_attn(s, p, zero):
    sinks = p["sinks"].reshape(N_KV * GROUP) + zero[0]
    o32, yb, yb_t, lse = _attn_fwd(s["q_r"], s["k_r"], s["v_b"], s["z"], sinks)
    return dict(s, sinks=sinks, o32=o32, yb=yb, yb_t=yb_t, lse=lse)


def _step_merge(s, tgt, w3, p):
    ya, yb, z = s["ya"], s["yb"], s["z"]
    y_a = _mm(ya, w3, sel=0, out_dtype=bf16, name="mm_ya")
    y_b = _mm(yb, w3, sel=1, out_dtype=bf16, name="mm_yb")
    mixed, mixed_t = _mix_fwd(y_a, y_b, z)
    out32 = _mm(mixed, w3, sel=2, bias=p["b_o"], name="mm_out")
    du32, dub, st_out = _final_ln(out32, s["h32"], tgt, p["ln_g"], p["ln_b"])

    g_wo = _mm(mixed_t, dub, out_dtype=bf16, name="mm_dwo")
    dmixed = _mm(dub, w3, sel=2, nt=True, out_dtype=bf16, name="mm_dmixed")
    dya_b, dyb_b, dma, dmb = _mix_bwd(dmixed, y_a, y_b, z)
    g_wrnn = _mm(s["ya_t"], dya_b, out_dtype=bf16, name="mm_dwrnn")
    g_wattn = _mm(s["yb_t"], dyb_b, out_dtype=bf16, name="mm_dwattn")
    dya = _mm(dya_b, w3, sel=0, nt=True, name="mm_dya")
    dyb = _mm(dyb_b, w3, sel=1, nt=True, name="mm_dyb")
    return dict(du32=du32, st_out=st_out, dma=dma, dmb=dmb, dya=dya, dyb=dyb, g_wo=g_wo, g_wrnn=g_wrnn,
                g_wattn=g_wattn)


def _step_backward(s, t, wrg, smallw, p, conv_b):
    z = s["z"]
    dxr, dgr, g_wrg, vec_rnn = _rnn_bwd(t["dya"], s["hr"], s["xc"], z, smallw, conv_b, wrg, p["b_ra"], p["b_ri"],
                                        p["lru_lambda"])
    dq_r, dga, dk, dv, dkm, dvm, dsr = _attn_bwd(t["dyb"], s["o32"], s["lse"], s["q_r"], s["k_r"], s["v_b"], z,
                                                 s["sinks"])
    dq, dkv = _qkv_finish(dq_r, dk, dv, dkm, dvm, s["cos128"], s["sin128"])
    dz_parts = [(dxr, D), (dgr, D), (dq, D), (dkv, 512), (dga, D), (t["dma"], D), (t["dmb"], D)]
    return dict(vec_rnn=vec_rnn, dsr=dsr, g_wrg=g_wrg, dz_parts=dz_parts)


def _step_input_grad(dh_lo, dh_hi, du32, x, smallw, p, after):
    grad_x, dmeta, st_emb = _ln_emb_bwd(dh_lo, dh_hi, du32, x, smallw, p["ln_emb_g"], after)
    return dict(grad_x=grad_x, dmeta=dmeta, st_emb=st_emb)


_ANY = pl.BlockSpec(memory_space=pl.ANY)
_VMEM = pl.BlockSpec(memory_space=pltpu.VMEM)


def _place():
    x, y, c = lax.axis_index("x"), lax.axis_index("y"), lax.axis_index("c")
    return x, y, c


def _dev(px, py, pc):
    return 4 * px + 2 * py + pc


def _tile_rows(r):
    return max(t for t in range(16, 321, 16) if r % t == 0) if r > 320 else r


def _cast_w_in(w_in_t):
    tm = _tile_rows(SHARD_IN)

    def body(i_ref, o_ref):
        o_ref[...] = i_ref[...].astype(bf16)

    return pl.pallas_call(
        body, grid=(SHARD_IN // tm,),
        in_specs=[pl.BlockSpec((tm, D), lambda i: (i, 0))],
        out_specs=pl.BlockSpec((tm, D), lambda i: (i, 0)),
        out_shape=SDS((SHARD_IN, D), bf16), name="cast_w_in", compiler_params=_cp(("arbitrary",)),
    )(w_in_t)


def _cast_small(w_rnn_out, w_attn_out, w_o, w_ra, w_ri, meta, conv_w):
    def body(a_ref, b_ref, c_ref, ra_ref, ri_ref, m_ref, cw_ref, w3_ref, wrg_ref, sw_ref):
        w3_ref[0] = a_ref[0].astype(bf16)
        w3_ref[1] = b_ref[0].astype(bf16)
        w3_ref[2] = c_ref[0].astype(bf16)
        wrg_ref[0] = ra_ref[0].astype(bf16)
        wrg_ref[1] = ri_ref[0].astype(bf16)
        sw_ref[...] = jnp.concatenate([m_ref[...], cw_ref[0], jnp.zeros((4, 256), f32)], axis=0)

    return pl.pallas_call(
        body,
        out_shape=[SDS((3, 256, D), bf16), SDS((2, N_RNN_BLOCKS, 32, RNN_BLOCK), bf16), SDS((24, 256), f32)],
        name="cast_small", compiler_params=_cp(None),
    )(w_rnn_out, w_attn_out, w_o, w_ra, w_ri, meta, conv_w)


def _all_gather(shards, later):
    n = len(shards)
    nl = len(later)

    def body(*refs):
        ins, outs = refs[:n], refs[n + nl:2 * n + nl]
        send_sems, recv_sems, local_sems = refs[2 * (n + nl):]
        x, y, c = _place()
        me, sibling = (x, y, c), (x, y, 1 - c)
        chips = [(1 - x, y), (x, 1 - y), (1 - x, 1 - y)]

        def copy(a, k, block, to, src=None):
            dst = outs[a].at[_dev(*block)]
            return pltpu.make_async_remote_copy(
                src_ref=dst if src is None else src, dst_ref=dst,
                send_sem=send_sems.at[a * 7 + k], recv_sem=recv_sems.at[a * 7 + k],
                device_id=to, device_id_type=MESH)

        all_ins, all_outs = refs[:n + nl], refs[n + nl:2 * (n + nl)]
        mine = [pltpu.make_async_copy(all_ins[a], all_outs[a].at[_dev(*me)], local_sems.at[a]) for a in range(n + nl)]
        for cp in mine:
            cp.start()
        first = []
        for a in range(n):
            first.append(copy(a, 0, me, sibling, src=ins[a]))
            first += [copy(a, 1 + j, me, (*chip, c), src=ins[a]) for j, chip in enumerate(chips)]
        for cp in first:
            cp.start()
        passed = []
        for a in range(n):
            for j, chip in enumerate(chips):
                copy(a, 1 + j, (*chip, c), me).wait_recv()
                cp = copy(a, 4 + j, (*chip, c), sibling)
                cp.start()
                passed.append(cp)
        for a in range(n):
            copy(a, 0, sibling, me).wait_recv()
            for j, chip in enumerate(chips):
                copy(a, 4 + j, (*chip, 1 - c), me).wait_recv()
        for cp in first + passed:
            cp.wait_send()
        for cp in mine:
            cp.wait()

    return pl.pallas_call(
        body, in_specs=[_ANY] * (n + nl), out_specs=[_ANY] * (n + nl),
        out_shape=[SDS((N_DEV, *s.shape), s.dtype) for s in (*shards, *later)],
        scratch_shapes=[pltpu.SemaphoreType.DMA((7 * n,)), pltpu.SemaphoreType.DMA((7 * n,)),
                        pltpu.SemaphoreType.DMA((n + nl,))],
        name="all_gather_weights",
    )(*shards, *later)


_HBM = pl.BlockSpec(memory_space=pltpu.HBM)
_SEM = pl.BlockSpec(memory_space=pltpu.SEMAPHORE)
_PEER_FLIPS = [(f // 4, (f // 2) % 2, f % 2) for f in range(1, N_DEV)]


def _remote(src, dst, send_sems, recv_sems, k, to):
    return pltpu.make_async_remote_copy(src_ref=src, dst_ref=dst, send_sem=send_sems.at[k], recv_sem=recv_sems.at[k],
                                        device_id=to, device_id_type=MESH)


def _copies_direct(same_src):
    def make(srcs, lands, send_sems, recv_sems):
        x, y, c = _place()
        me = _dev(x, y, c)
        out = []
        for a in range(len(srcs)):
            for k, (fx, fy, fc) in enumerate(_PEER_FLIPS):
                peer = ((x + fx) % 2, (y + fy) % 2, (c + fc) % 2)
                src = srcs[a] if same_src else srcs[a].at[_dev(*peer)]
                out.append(_remote(src, lands[a].at[me], send_sems, recv_sems, 7 * a + k, peer))
        return out
    return make


def _copies_siblings(srcs, lands, send_sems, recv_sems):
    x, y, c = _place()
    return [_remote(srcs[a].at[2 * q + (1 - c)], lands[a].at[q], send_sems, recv_sems, 4 * a + q, (x, y, 1 - c))
            for a in range(len(srcs)) for q in range(4)]


def _copies_chips(srcs, lands, send_sems, recv_sems):
    x, y, c = _place()
    chips = [(1 - x, y), (x, 1 - y), (1 - x, 1 - y)]
    return [_remote(srcs[a].at[2 * qx + qy], lands[a].at[j], send_sems, recv_sems, 3 * a + j, (qx, qy, c))
            for a in range(len(srcs)) for j, (qx, qy) in enumerate(chips)]


def _split_start(make, per_array, srcs, lands, dep, name):
    n = len(srcs)

    def body(*refs):
        send_sems, recv_sems, token = refs[2 * n + 1], refs[2 * n + 2], refs[-1]
        for cp in make(refs[:n], refs[n:2 * n], send_sems, recv_sems):
            cp.start()
        token[...] = jnp.zeros_like(token)

    hbm = lambda t: pltpu.with_memory_space_constraint(t, pltpu.HBM)
    res = pl.pallas_call(
        body, name=name,
        out_shape=(pltpu.SemaphoreType.DMA((per_array * n,)), pltpu.SemaphoreType.DMA((per_array * n,)),
                   *[pltpu.HBM(t.shape, t.dtype) for t in (*srcs, *lands)], SDS((8, 128), f32)),
        in_specs=[_HBM] * (2 * n) + [_ANY], out_specs=(_SEM, _SEM, *([_HBM] * (2 * n)), _VMEM),
        input_output_aliases={i: 2 + i for i in range(2 * n)},
        compiler_params=pltpu.CompilerParams(has_side_effects=pltpu.SideEffectType.DATAFLOW_SIDE_EFFECTING),
    )(*[hbm(t) for t in (*srcs, *lands)], dep)
    return res[0], res[1], list(res[2:2 + n]), list(res[2 + n:2 + 2 * n]), res[-1]


def _split_wait(make, send_sems, recv_sems, srcs, lands, after, name):
    n = len(srcs)

    def body(*refs):
        for cp in make(refs[:n], refs[n:2 * n], refs[2 * n], refs[2 * n + 1]):
            cp.wait_send()
            cp.wait_recv()

    res = pl.pallas_call(
        body, name=name,
        out_shape=tuple(pltpu.HBM(t.shape, t.dtype) for t in (*srcs, *lands)),
        in_specs=[_HBM] * (2 * n) + [_SEM, _SEM, _ANY], out_specs=tuple([_HBM] * (2 * n)),
        input_output_aliases={i: i for i in range(2 * n)},
        compiler_params=pltpu.CompilerParams(has_side_effects=pltpu.SideEffectType.DATAFLOW_SIDE_EFFECTING),
    )(*srcs, *lands, send_sems, recv_sems, after)
    return list(res[:n]), list(res[n:])


def _adamw_direct(g, land, me_idx, w, m, v, name):
    r, wd = w.shape
    tr = min(r, 256)

    def body(me_ref, *refs):
        g_ref, peers = refs[0], refs[1:N_DEV]
        w_ref, m_ref, v_ref, g_out, d_out, m_out, v_out = refs[N_DEV:]
        gs = g_ref[...].astype(f32)
        for p_ref in peers:
            gs = gs + p_ref[...].astype(f32)
        d, mn, vn = _adamw(w_ref[...], gs, m_ref[...], v_ref[...])
        g_out[...] = gs
        d_out[...] = d
        m_out[...] = mn
        v_out[...] = vn

    tile = pl.BlockSpec((tr, wd), lambda i, me_ref: (i, 0))
    slot = lambda k: pl.BlockSpec((None, tr, wd), lambda i, me_ref: ((me_ref[0] + k) % N_DEV, i, 0))
    return pl.pallas_call(
        body,
        grid_spec=pltpu.PrefetchScalarGridSpec(
            num_scalar_prefetch=1, grid=(r // tr,),
            in_specs=[slot(0)] + [slot(k) for k in range(1, N_DEV)] + [tile, tile, tile],
            out_specs=[tile] * 4),
        out_shape=[SDS((r, wd), f32)] * 4, name=name, compiler_params=_cp(("arbitrary",), 48),
    )(me_idx, g, *([land] * (N_DEV - 1)), w, m, v)


def _pair_sum(g, r1, c_idx, name):
    _, r, w = g.shape
    tr = _tile_rows(r)

    def body(c_ref, g_ref, r_ref, o_ref):
        o_ref[...] = (g_ref[...].astype(f32) + r_ref[...].astype(f32)).astype(bf16)

    return pl.pallas_call(
        body,
        grid_spec=pltpu.PrefetchScalarGridSpec(
            num_scalar_prefetch=1, grid=(4, r // tr),
            in_specs=[pl.BlockSpec((None, tr, w), lambda q, i, c_ref: (2 * q + c_ref[0], i, 0)),
                      pl.BlockSpec((None, tr, w), lambda q, i, c_ref: (q, i, 0))],
            out_specs=pl.BlockSpec((None, tr, w), lambda q, i, c_ref: (q, i, 0))),
        out_shape=SDS((4, r, w), bf16), name=name, compiler_params=_cp(("arbitrary", "arbitrary")),
    )(c_idx, g, r1)


def _adamw(w, g, m, v):
    m = ADAM_B1 * m + (1.0 - ADAM_B1) * g
    v = ADAM_B2 * v + (1.0 - ADAM_B2) * (g * g)
    m_hat = m / (1.0 - ADAM_B1 ** ADAM_STEP)
    v_hat = v / (1.0 - ADAM_B2 ** ADAM_STEP)
    delta = -ADAM_LR * (m_hat / (jnp.sqrt(v_hat) + ADAM_EPS) + ADAM_WD * w)
    return delta, m, v


def _adamw_big(part, r2, q_idx, w, m, v, name, row_off=0, cols=(0, 1), prev=None):
    r, wd = w.shape
    tr = _tile_rows(r)
    k, ncol = cols
    wp = wd // ncol

    def body(q_ref, p_ref, r_ref, w_ref, m_ref, v_ref, *rest):
        g_out, d_out, m_out, v_out = rest[-4:]
        g = p_ref[...].astype(f32)
        for j in range(3):
            g = g + r_ref[j].astype(f32)
        d, mn, vn = _adamw(w_ref[...], g, m_ref[...], v_ref[...])
        g_out[...] = g
        d_out[...] = d
        m_out[...] = mn
        v_out[...] = vn

    tile = pl.BlockSpec((tr, wp), lambda i, q_ref: (i, k))
    prev = list(prev) if prev is not None else []
    return pl.pallas_call(
        body,
        grid_spec=pltpu.PrefetchScalarGridSpec(
            num_scalar_prefetch=1, grid=(r // tr,),
            in_specs=[pl.BlockSpec((None, tr, wp), lambda i, q_ref: (q_ref[0], row_off + i, 0)),
                      pl.BlockSpec((3, tr, wp), lambda i, q_ref: (0, row_off + i, 0)), tile, tile, tile]
                     + [pl.BlockSpec(memory_space=pl.ANY)] * len(prev),
            out_specs=[tile] * 4),
        out_shape=[SDS((r, wd), f32)] * 4, name=name,
        input_output_aliases={6 + i: i for i in range(len(prev))},
        compiler_params=_cp(("arbitrary",), 48),
    )(q_idx, part, r2, w, m, v, *prev)


_SMALL_ROWS = 24


def _pack_early(vec_rnn, st_out, dsr, db_in):
    def body(vr_ref, so_ref, dsr_ref, db_ref, sm_ref, sm2_ref):
        sm_ref[...] = jnp.zeros_like(sm_ref)
        sm2_ref[...] = jnp.zeros_like(sm2_ref)
        sm_ref[2:3, :] = vr_ref[3:4, :]
        sm_ref[3:6, :] = vr_ref[0:3, :]
        sm_ref[6:7, :] = so_ref[2:3, :]
        sm_ref[7:9, :] = so_ref[0:2, :]
        sm_ref[10:11, :] = so_ref[3:4, :]
        for h in range(N_KV):
            sm_ref[9:10, h * GROUP:(h + 1) * GROUP] = _colsum(dsr_ref[h])
        for j in range(6):
            sm_ref[16 + j:17 + j, :] = db_ref[0:1, j * D:(j + 1) * D]
        sm_ref[22:23, 0:D_IN - 6 * D] = db_ref[0:1, 6 * D:D_IN]
        for s in range(N_DEV):
            sm2_ref[s, 0:CONV_WIDTH, :] = vr_ref[4:8, s * 256:(s + 1) * 256]

    return pl.pallas_call(
        body, out_shape=[SDS((_SMALL_ROWS, D), f32), SDS((N_DEV, 8, 256), f32)],
        name="pack_early", compiler_params=_cp(None),
    )(vec_rnn, st_out, dsr, db_in)


def _pack_late(st_emb, dmeta):
    def body(se_ref, dm_ref, sm_ref, sm2_ref):
        sm_ref[...] = se_ref[...]
        for s in range(N_DEV):
            sm2_ref[s] = dm_ref[:, s * 256:(s + 1) * 256]

    return pl.pallas_call(
        body, out_shape=[SDS((8, D), f32), SDS((N_DEV, N_META, 256), f32)],
        name="pack_late", compiler_params=_cp(None),
    )(st_emb, dmeta)


def _small_allreduce(sm, sm2):
    def body(sm_ref, sm2_ref, o_ref, o2_ref, buf, buf2, send_sems, recv_sems):
        x, y, c = _place()
        me = _dev(x, y, c)
        copies = []
        for f in range(1, N_DEV):
            fx, fy, fc = f // 4, (f // 2) % 2, f % 2
            peer = ((x + fx) % 2, (y + fy) % 2, (c + fc) % 2)
            for t, (src, dst) in enumerate(((sm_ref, buf), (sm2_ref, buf2))):
                k = 2 * (f - 1) + t
                copies.append(pltpu.make_async_remote_copy(
                    src_ref=src, dst_ref=dst.at[me], send_sem=send_sems.at[k], recv_sem=recv_sems.at[k],
                    device_id=peer, device_id_type=MESH))
        for cp in copies:
            cp.start()
        buf[me] = sm_ref[...]
        buf2[me] = sm2_ref[...]
        for cp in copies:
            cp.wait()
        acc, acc2 = buf[0], buf2[0]
        for e in range(1, N_DEV):
            acc, acc2 = acc + buf[e], acc2 + buf2[e]
        o_ref[...] = acc
        o2_ref[...] = acc2

    return pl.pallas_call(
        body, in_specs=[_VMEM, _VMEM], out_specs=[_VMEM, _VMEM],
        out_shape=[SDS(sm.shape, f32), SDS(sm2.shape, f32)],
        scratch_shapes=[pltpu.VMEM((N_DEV, *sm.shape), f32), pltpu.VMEM((N_DEV, *sm2.shape), f32),
                        pltpu.SemaphoreType.DMA((14,)), pltpu.SemaphoreType.DMA((14,))],
        name="small_allreduce",
    )(sm, sm2)


_SMALL_ROW_OF = {"ln_emb_g": 0, "ln_emb_b": 1, "conv_b": 2, "b_ra": 3, "b_ri": 4, "lru_lambda": 5, "b_o": 6,
                 "ln_g": 7, "ln_b": 8}
_SMALL_NAMES = ["ln_emb_g", "ln_emb_b", "conv_b", "b_ra", "b_ri", "lru_lambda", "b_o", "ln_g", "ln_b",
                "sinks", "b_in", "meta_tokens", "conv_w"]


def _small_update(me_idx, early, late, wmv):
    n_fixed = 7

    def in_order(me, own_ref, land_ref):
        acc = None
        for e in range(N_DEV):
            term = jnp.where(me == e, own_ref[...], land_ref[e])
            acc = term if acc is None else acc + term
        return acc

    def body(*refs):
        me_ref, own_ref, land_ref, cown_ref, cland_ref, late_ref, meta_ref = refs[:n_fixed]
        ins = refs[n_fixed:n_fixed + 3 * len(_SMALL_NAMES)]
        outs = refs[n_fixed + 3 * len(_SMALL_NAMES):]
        me = me_ref[0]
        sm = in_order(me, own_ref, land_ref)
        conv = in_order(me, cown_ref, cland_ref)

        def grad_of(name):
            if name in ("ln_emb_g", "ln_emb_b"):
                r = _SMALL_ROW_OF[name]
                return late_ref[r:r + 1, :]
            if name in _SMALL_ROW_OF:
                r = _SMALL_ROW_OF[name]
                return sm[r:r + 1, :]
            if name == "sinks":
                return sm[9:10, 0:N_KV * GROUP]
            if name == "b_in":
                return jnp.concatenate([sm[16 + j:17 + j, :] for j in range(7)], axis=1)[:, :D_IN]
            if name == "meta_tokens":
                return meta_ref[...]
            return conv[0:CONV_WIDTH, :]

        for i, name in enumerate(_SMALL_NAMES):
            w_ref, m_ref, v_ref = ins[3 * i:3 * i + 3]
            g = grad_of(name)
            d, mn, vn = _adamw(w_ref[...], g, m_ref[...], v_ref[...])
            outs[4 * i][...] = g
            outs[4 * i + 1][...] = d
            outs[4 * i + 2][...] = mn
            outs[4 * i + 3][...] = vn
        outs[-1][...] = jnp.broadcast_to(jnp.sum(sm[10:11, :], axis=1, keepdims=True), (8, 128))

    args, out_shape = [me_idx, *early, *late], []
    for name in _SMALL_NAMES:
        args += list(wmv[name])
        out_shape += [SDS(wmv[name][0].shape, f32)] * 4
    out_shape.append(SDS((8, 128), f32))
    res = pl.pallas_call(
        body, out_shape=out_shape, in_specs=[pl.BlockSpec(memory_space=pltpu.SMEM)] + [_VMEM] * (len(args) - 1),
        name="small_update", compiler_params=_cp(None))(*args)
    return {name: tuple(res[4 * i:4 * i + 4]) for i, name in enumerate(_SMALL_NAMES)}, res[-1][0, 0]


_WEIGHTS = ["meta_tokens", "ln_emb_g", "ln_emb_b", "w_in", "b_in", "conv_w", "conv_b", "w_ra", "b_ra", "w_ri",
            "b_ri", "lru_lambda", "sinks", "w_rnn_out", "w_attn_out", "w_o", "b_o", "ln_g", "ln_b"]
_SMALL_2D = {"meta_tokens": (N_META, 256), "conv_w": (CONV_WIDTH, 256), "b_in": (1, D_IN), "sinks": (1, N_KV * GROUP)}


def kernel(x, meta_tokens, ln_emb_g, ln_emb_b, w_in, b_in, conv_w, conv_b, w_ra, b_ra, w_ri, b_ri, lru_lambda, sinks, w_rnn_out, w_attn_out, w_o, b_o, ln_g, ln_b, loss_target, m_meta_tokens, m_ln_emb_g, m_ln_emb_b, m_w_in, m_b_in, m_conv_w, m_conv_b, m_w_ra, m_b_ra, m_w_ri, m_b_ri, m_lru_lambda, m_sinks, m_w_rnn_out, m_w_attn_out, m_w_o, m_b_o, m_ln_g, m_ln_b, v_meta_tokens, v_ln_emb_g, v_ln_emb_b, v_w_in, v_b_in, v_conv_w, v_conv_b, v_w_ra, v_b_ra, v_w_ri, v_b_ri, v_lru_lambda, v_sinks, v_w_rnn_out, v_w_attn_out, v_w_o, v_b_o, v_ln_g, v_ln_b):
    w = dict(meta_tokens=meta_tokens, ln_emb_g=ln_emb_g, ln_emb_b=ln_emb_b, w_in=w_in, b_in=b_in, conv_w=conv_w,
             conv_b=conv_b, w_ra=w_ra, b_ra=b_ra, w_ri=w_ri, b_ri=b_ri, lru_lambda=lru_lambda, sinks=sinks,
             w_rnn_out=w_rnn_out, w_attn_out=w_attn_out, w_o=w_o, b_o=b_o, ln_g=ln_g, ln_b=ln_b)
    m = dict(meta_tokens=m_meta_tokens, ln_emb_g=m_ln_emb_g, ln_emb_b=m_ln_emb_b, w_in=m_w_in, b_in=m_b_in,
             conv_w=m_conv_w, conv_b=m_conv_b, w_ra=m_w_ra, b_ra=m_b_ra, w_ri=m_w_ri, b_ri=m_b_ri,
             lru_lambda=m_lru_lambda, sinks=m_sinks, w_rnn_out=m_w_rnn_out, w_attn_out=m_w_attn_out, w_o=m_w_o,
             b_o=m_b_o, ln_g=m_ln_g, ln_b=m_ln_b)
    v = dict(meta_tokens=v_meta_tokens, ln_emb_g=v_ln_emb_g, ln_emb_b=v_ln_emb_b, w_in=v_w_in, b_in=v_b_in,
             conv_w=v_conv_w, conv_b=v_conv_b, w_ra=v_w_ra, b_ra=v_b_ra, w_ri=v_w_ri, b_ri=v_b_ri,
             lru_lambda=v_lru_lambda, sinks=v_sinks, w_rnn_out=v_w_rnn_out, w_attn_out=v_w_attn_out, w_o=v_w_o,
             b_o=v_b_o, ln_g=v_ln_g, ln_b=v_ln_b)
    px, py, pc = _place()
    as_idx = lambda t: jnp.reshape(t, (1,)).astype(jnp.int32)
    c_idx, q_idx, me_idx = as_idx(pc), as_idx(2 * px + py), as_idx(_dev(px, py, pc))

    w3_s, wrg_s, small_s = _cast_small(w_rnn_out, w_attn_out, w_o, w_ra, w_ri, meta_tokens, conv_w)
    vec = lambda name: w[name].reshape(1, -1)
    p = {k: vec(k) for k in ("ln_emb_g", "ln_emb_b", "b_in", "conv_b", "b_ra", "b_ri", "lru_lambda", "sinks",
                             "b_o", "ln_g", "ln_b")}
    w_in_t = lambda a: jnp.swapaxes(a, 1, 2).reshape(SHARD_IN, D)
    wg, wrg, smallw, w3_land = _all_gather([_cast_w_in(w_in_t(w_in)), wrg_s, small_s], [w3_s])
    w3_pending = _split_start(_copies_direct(True), 7, [w3_s], [w3_land], smallw, "gather_w3_start")
    w_full = wg.reshape(D_IN, D)

    zero = w3_pending[4][0:1, 0:1]
    h32, hb = _ln_emb(x, smallw, p["ln_emb_g"], p["ln_emb_b"])
    z = _mm(hb, w_full, nt=True, bias=p["b_in"] + zero, name="mm_z")
    s = _step_attn(_step_rnn(h32, hb, z, wrg, smallw, p, zero), p, zero)
    w3 = _split_wait(_copies_direct(True), *w3_pending[:4], s["lse"], "gather_w3_wait")[1][0]
    t = _step_merge(s, loss_target, w3, p)

    big = {}
    two_d = lambda name: (w[name].shape[-2], w[name].shape[-1])
    proj = ("w_o", "w_rnn_out", "w_attn_out")
    g_proj = [t[k].reshape(N_DEV, 256, D) for k in ("g_wo", "g_wrnn", "g_wattn")]
    g_pending = _split_start(_copies_direct(False), 7, g_proj, [lax.empty((N_DEV, 256, D), bf16) for _ in proj],
                             p["b_o"], "reduce_proj_start")
    u = _step_backward(s, t, wrg, smallw, p, p["conv_b"] + g_pending[4][0:1, 0:1])

    def siblings_start(gs, dep, tag):
        return _split_start(_copies_siblings, 4, gs, [lax.empty((4, *g.shape[1:]), bf16) for g in gs], dep,
                            "reduce_siblings_start_" + tag)

    def chips_start(gs, r1, dep, tag):
        parts = [_pair_sum(g, r, c_idx, "pair_sum_%s%d" % (tag, i)) for i, (g, r) in enumerate(zip(gs, r1))]
        return _split_start(_copies_chips, 3, parts, [lax.empty((3, *q.shape[1:]), bf16) for q in parts], dep,
                            "reduce_chips_start_" + tag)

    g_a, dz, db_in = _mm_dwin_parts(s["hb"], u["dz_parts"])
    shards = lambda g: g.reshape(N_DEV, SHARD_IN, W_IN_HALF)
    sib_a = siblings_start([shards(g_a), u["g_wrg"].reshape(N_DEV, 2 * RNN_BLOCK, RNN_BLOCK)], db_in, "a")
    g_proj, g_land = _split_wait(_copies_direct(False), *g_pending[:4], sib_a[4], "reduce_proj_wait")
    for i, name in enumerate(proj):
        res = _adamw_direct(g_proj[i], g_land[i], me_idx, w[name].reshape(two_d(name)), m[name].reshape(two_d(name)),
                            v[name].reshape(two_d(name)), "adamw_" + name)
        big[name] = tuple(r.reshape(w[name].shape) for r in res)
    chp_a = chips_start(*_split_wait(_copies_siblings, *sib_a[:4], big["w_attn_out"][3], "reduce_siblings_wait_a"),
                        db_in, "a")
    g_b = _mm_dwin(s["hb"], dz, chp_a[4])
    sib_b = siblings_start([shards(g_b)], db_in, "b")
    sm_e = _pack_early(u["vec_rnn"], t["st_out"], u["dsr"], db_in)
    early = _split_start(_copies_direct(True), 7, list(sm_e),
                         [lax.empty((N_DEV, *a.shape), f32) for a in sm_e], sib_b[4], "small_early_start")
    dh_lo = _mm_dh(dz, w_full, early[4], 0)
    chp_b = chips_start(*_split_wait(_copies_siblings, *sib_b[:4], dh_lo, "reduce_siblings_wait_b"), db_in, "b")
    dh_hi = _mm_dh(dz, w_full, chp_b[4], 1)
    parts_a, r2_a = _split_wait(_copies_chips, *chp_a[:4], dh_hi, "reduce_chips_wait_a")
    w_in_res = _adamw_big(parts_a[0], r2_a[0], q_idx, w_in_t(w["w_in"]), w_in_t(m["w_in"]), w_in_t(v["w_in"]),
                          "adamw_w_in_a", cols=(0, 2))
    u.update(_step_input_grad(dh_lo, dh_hi, t["du32"], x, smallw, p, w_in_res[3]))
    sm_l, meta_l = _small_allreduce(*_pack_late(u["st_emb"], u["dmeta"]))
    (sm_own, conv_own), (sm_land, conv_land) = _split_wait(_copies_direct(True), *early[:4], sm_l, "small_early_wait")
    me = _dev(px, py, pc)
    mine = lambda a, axis: lax.dynamic_index_in_dim(a, me, axis, keepdims=False)
    two = lambda name, t: t.reshape(_SMALL_2D.get(name, (1, D)))
    small, loss = _small_update(me_idx, (sm_own, sm_land, mine(conv_own, 0), mine(conv_land, 1)),
                                (sm_l, mine(meta_l, 0)),
                                {k: (two(k, w[k]), two(k, m[k]), two(k, v[k])) for k in _SMALL_NAMES})

    parts_b, r2_b = _split_wait(_copies_chips, *chp_b[:4], small["b_in"][2], "reduce_chips_wait_b")
    res = _adamw_big(parts_b[0], r2_b[0], q_idx, w_in_t(w["w_in"]), w_in_t(m["w_in"]), w_in_t(v["w_in"]),
                     "adamw_w_in_b", cols=(1, 2), prev=w_in_res)
    big["w_in"] = tuple(jnp.swapaxes(r.reshape(1, SHARD_IN, D), 1, 2) for r in res)
    for i, name in enumerate(("w_ra", "w_ri")):
        sq = (RNN_BLOCK, RNN_BLOCK)
        res = _adamw_big(parts_a[1], r2_a[1], q_idx, w[name].reshape(sq), m[name].reshape(sq), v[name].reshape(sq),
                         "adamw_" + name, row_off=i)
        big[name] = tuple(r.reshape(w[name].shape) for r in res)
    res = dict(big)
    for k in _SMALL_NAMES:
        res[k] = tuple(t.reshape(w[k].shape) for t in small[k])

    outs = [loss, u["grad_x"]]
    for j in range(4):
        outs += [res[k][j] for k in _WEIGHTS]
    return tuple(outs)
```

```python
import jax
import jax.numpy as jnp
from jax import lax
from jax.experimental import pallas as pl
from jax.experimental.pallas import tpu as pltpu

f32, bf16 = jnp.float32, jnp.bfloat16
SDS = jax.ShapeDtypeStruct

N_DEV = 8
D = 2048
N_META = 16
BLK = 128
ROW0 = BLK - N_META
N_RNN_BLOCKS = 8
RNN_BLOCK = D // N_RNN_BLOCKS
CONV_WIDTH = 4
LRU_C = 8.0
HEAD_DIM = 64
N_KV = 4
GROUP = 8
HALF = HEAD_DIM // 2
ROPE_THETA = 10000.0
NEG_INF = -1e30
LN_EPS = 1e-5
ALPHA = 2.0 ** 0.25
D_IN = 12800
SHARD_IN = D_IN // N_DEV
W_IN_HALF = D // 2
OFF_GR, OFF_Q, OFF_K, OFF_V, OFF_GA, OFF_G = 2048, 4096, 6144, 6400, 6656, 8704
ADAM_LR, ADAM_B1, ADAM_B2, ADAM_EPS, ADAM_WD, ADAM_STEP = 1e-3, 0.9, 0.999, 1e-8, 0.01, 10
VMEM_LIMIT_MB = 56
MESH = pl.DeviceIdType.MESH


def _cp(sem=None, vmem_mb=40):
    return pltpu.CompilerParams(dimension_semantics=sem, vmem_limit_bytes=vmem_mb * 2 ** 20)


def _row_chunk(m):
    best = 16
    for c in range(16, 641, 16):
        if m % c == 0:
            best = c
    return best


def _sigmoid(x):
    return 1.0 / (1.0 + jnp.exp(-x))


def _silu_and_grad(x):
    s = _sigmoid(x)
    return x * s, s * (1.0 + x * (1.0 - s))


def _log_sigmoid(x):
    return jnp.minimum(x, 0.0) - jnp.log1p(jnp.exp(-jnp.abs(x)))


def _ln_rows(v, g, b):
    mu = jnp.mean(v, axis=-1, keepdims=True)
    c = v - mu
    var = jnp.mean(c * c, axis=-1, keepdims=True)
    rstd = lax.rsqrt(var + LN_EPS)
    xhat = c * rstd
    return xhat * g + b, xhat, rstd


def _ln_rows_bwd(dy, g, xhat, rstd):
    dxh = dy * g
    m1 = jnp.mean(dxh, axis=-1, keepdims=True)
    m2 = jnp.mean(dxh * xhat, axis=-1, keepdims=True)
    return rstd * (dxh - m1 - xhat * m2)


def _colsum(v):
    return jnp.sum(v, axis=0, keepdims=True)


def _dot(a, b):
    return jnp.dot(a, b, preferred_element_type=f32)


def _dot_nt(a, b):
    return lax.dot_general(a, b, (((1,), (1,)), ((), ())), preferred_element_type=f32)


def _dot_tn(a, b):
    return lax.dot_general(a, b, (((0,), (0,)), ((), ())), preferred_element_type=f32)


def _meta_full(sw_ref):
    return jnp.concatenate([sw_ref[s, 0:N_META, :] for s in range(N_DEV)], axis=1)


def _ln_emb(x, smallw, g_e, b_e):
    seq = x.shape[1]
    rows = seq + BLK
    nb = rows // BLK

    def body(x_ref, sw_ref, g_ref, b_ref, h32_ref, hb_ref):
        i = pl.program_id(0)
        g, b = g_ref[...], b_ref[...]

        def emit(blk):
            h32_ref[...] = blk
            hb_ref[...] = blk.astype(bf16)

        @pl.when(i == 0)
        def _():
            hm = _ln_rows(_meta_full(sw_ref), g, b)[0]
            emit(jnp.concatenate([jnp.zeros((ROW0, D), f32), hm], axis=0))

        @pl.when(i > 0)
        def _():
            emit(_ln_rows(x_ref[0], g, b)[0])

    return pl.pallas_call(
        body, grid=(nb,),
        in_specs=[pl.BlockSpec((1, BLK, D), lambda i: (0, jnp.maximum(i - 1, 0), 0)),
                  pl.BlockSpec((N_DEV, 24, 256), lambda i: (0, 0, 0)),
                  pl.BlockSpec((1, D), lambda i: (0, 0)),
                  pl.BlockSpec((1, D), lambda i: (0, 0))],
        out_specs=[pl.BlockSpec((BLK, D), lambda i: (i, 0)),
                   pl.BlockSpec((BLK, D), lambda i: (i, 0))],
        out_shape=[SDS((rows, D), f32), SDS((rows, D), bf16)],
        name="ln_emb", compiler_params=_cp(("arbitrary",)),
    )(x, smallw, g_e, b_e)


def _ln_emb_bwd(dh_lo, dh_hi, du32, x, smallw, g_e, after):
    seq = x.shape[1]
    rows = seq + BLK
    nb = rows // BLK

    def body(dlo_ref, dhi_ref, du_ref, x_ref, sw_ref, g_ref, after_ref, gx_ref, dmeta_ref, st_ref):
        i = pl.program_id(0)
        g = g_ref[...]
        dht = jnp.concatenate([dlo_ref[...], dhi_ref[...]], axis=1) + ALPHA * du_ref[...]

        @pl.when(i == 0)
        def _():
            v = jnp.concatenate([jnp.zeros((ROW0, D), f32), _meta_full(sw_ref)], axis=0)
            valid = lax.broadcasted_iota(jnp.int32, (BLK, 1), 0) >= ROW0
            d = jnp.where(valid, dht, 0.0)
            _, xhat, rstd = _ln_rows(v, g, 0.0)
            dv = _ln_rows_bwd(d, g, xhat, rstd)
            dmeta_ref[...] = dv[ROW0:, :]
            st_ref[...] = jnp.concatenate([_colsum(d * xhat), _colsum(d), jnp.zeros((6, D), f32)], axis=0)

        @pl.when(i > 0)
        def _():
            _, xhat, rstd = _ln_rows(x_ref[0], g, 0.0)
            gx_ref[0] = _ln_rows_bwd(dht, g, xhat, rstd)
            st_ref[0:1, :] += _colsum(dht * xhat)
            st_ref[1:2, :] += _colsum(dht)

    return pl.pallas_call(
        body, grid=(nb,),
        in_specs=[pl.BlockSpec((BLK, W_IN_HALF), lambda i: (i, 0)),
                  pl.BlockSpec((BLK, W_IN_HALF), lambda i: (i, 0)),
                  pl.BlockSpec((BLK, D), lambda i: (i, 0)),
                  pl.BlockSpec((1, BLK, D), lambda i: (0, jnp.maximum(i - 1, 0), 0)),
                  pl.BlockSpec((N_DEV, 24, 256), lambda i: (0, 0, 0)),
                  pl.BlockSpec((1, D), lambda i: (0, 0)),
                  pl.BlockSpec(memory_space=pl.ANY)],
        out_specs=[pl.BlockSpec((1, BLK, D), lambda i: (0, jnp.maximum(i - 1, 0), 0)),
                   pl.BlockSpec((N_META, D), lambda i: (0, 0)),
                   pl.BlockSpec((8, D), lambda i: (0, 0))],
        out_shape=[SDS((1, seq, D), f32), SDS((N_META, D), f32), SDS((8, D), f32)],
        name="ln_emb_bwd", compiler_params=_cp(("arbitrary",)),
    )(dh_lo, dh_hi, du32, x, smallw, g_e, after)


def _mm(a, b, *, name, nt=False, sel=None, bias=None, out_dtype=f32, tn=512):
    m, k = a.shape
    cm = _row_chunk(m)
    stacked = sel is not None
    n = D if stacked else (b.shape[0] if nt else b.shape[1])
    am = m
    if stacked and nt:
        b_spec = pl.BlockSpec((tn // 256, None, 256, D), lambda j, i: (j, sel, 0, 0))
    elif stacked:
        b_spec = pl.BlockSpec((N_DEV, None, 256, tn), lambda j, i: (0, sel, 0, j))
    elif nt:
        b_spec = pl.BlockSpec((tn, k), lambda j, i: (j, 0))
    else:
        b_spec = pl.BlockSpec((k, tn), lambda j, i: (0, j))
    in_specs = [pl.BlockSpec((am, k), lambda j, i: (i, 0)), b_spec]
    args = [a, b]
    if bias is not None:
        in_specs.append(pl.BlockSpec((1, tn), lambda j, i: (0, j)))
        args.append(bias)

    def body(*refs):
        a_ref, b_ref, o_ref = refs[0], refs[1], refs[-1]
        bm = b_ref[...]
        if stacked:
            bm = bm.reshape((tn, D) if nt else (D, tn))
        for c in range(am // cm):
            acc = (_dot_nt if nt else _dot)(a_ref[c * cm:(c + 1) * cm, :], bm)
            if bias is not None:
                acc = acc + refs[2][...]
            o_ref[c * cm:(c + 1) * cm, :] = acc.astype(out_dtype)

    return pl.pallas_call(
        body, grid=(n // tn, m // am), in_specs=in_specs,
        out_specs=pl.BlockSpec((am, tn), lambda j, i: (i, j)),
        out_shape=SDS((m, n), out_dtype), name=name, compiler_params=_cp(("arbitrary", "arbitrary"), 48),
    )(*args)


def _mm_dh(parts, w_t, after, half):
    rows = parts[0][0].shape[0]
    tn = 512
    nt = W_IN_HALF // tn
    cm = _row_chunk(rows) // 2

    def body(*refs):
        w_ref, o_ref = refs[len(parts)], refs[-1]
        a = jnp.concatenate([r[...] for r in refs[:len(parts)]], axis=1)
        o_ref[...] = _dot(a, w_ref[...])

    return pl.pallas_call(
        body, grid=(nt, rows // cm),
        in_specs=[pl.BlockSpec((cm, w), lambda j, i: (i, 0)) for _, w in parts]
                 + [pl.BlockSpec((D_IN, tn), lambda j, i: (0, half * nt + j)),
                    pl.BlockSpec(memory_space=pl.ANY)],
        out_specs=pl.BlockSpec((cm, tn), lambda j, i: (i, j)),
        out_shape=SDS((rows, W_IN_HALF), f32), name="mm_dh_%d" % half,
        compiler_params=_cp(("arbitrary", "arbitrary"), VMEM_LIMIT_MB),
    )(*[a for a, _ in parts], w_t, after)


def _mm_dwin(hb, parts, half, after):
    rows = hb.shape[0]
    tc = 512
    with_db = half == 0
    edges = [0]
    for _, w in parts:
        edges.append(edges[-1] + w // tc)

    def body(*refs):
        h_ref, o_ref = refs[len(parts)], refs[len(parts) + 2]
        j = pl.program_id(0)
        for p_ref, lo, hi in zip(refs, edges[:-1], edges[1:]):
            @pl.when((j >= lo) & (j < hi))
            def _():
                o_ref[...] = _dot_tn(p_ref[...], h_ref[...]).astype(bf16)
                if with_db:
                    def step(i, s):
                        blk = p_ref[pl.ds(pl.multiple_of(i * BLK, BLK), BLK), :].astype(f32)
                        return s + blk.reshape(BLK // 8, 8, tc).sum(axis=0)
                    s = lax.fori_loop(0, rows // BLK, step, jnp.zeros((8, tc), f32))
                    refs[-1][...] = jnp.broadcast_to(_colsum(s), (8, tc))

    in_specs = [pl.BlockSpec((rows, tc), lambda j, lo=lo, hi=hi: (0, jnp.clip(j - lo, 0, hi - lo - 1)))
                for lo, hi in zip(edges[:-1], edges[1:])]
    out_specs = [pl.BlockSpec((tc, W_IN_HALF), lambda j: (j, 0))]
    out_shape = [SDS((D_IN, W_IN_HALF), bf16)]
    if with_db:
        out_specs.append(pl.BlockSpec((8, tc), lambda j: (0, j)))
        out_shape.append(SDS((8, D_IN), f32))
    return pl.pallas_call(
        body, grid=(D_IN // tc,),
        in_specs=in_specs + [pl.BlockSpec((rows, W_IN_HALF), lambda j: (0, half)),
                             pl.BlockSpec(memory_space=pl.ANY)],
        out_specs=out_specs, out_shape=out_shape,
        name="mm_dwin_%d" % half, compiler_params=_cp(("arbitrary",), VMEM_LIMIT_MB),
    )(*[a for a, _ in parts], hb, after)


SCAN_ROWS = 32


def _scan8(a, b, reverse):
    idx = lax.broadcasted_iota(jnp.int32, a.shape, 0)
    for s in (1, 2, 4):
        sh = 8 - s if reverse else s
        a_sh, b_sh = pltpu.roll(a, sh, 0), pltpu.roll(b, sh, 0)
        m = (idx < 8 - s) if reverse else (idx >= s)
        b = jnp.where(m, a * b_sh + b, b)
        a = jnp.where(m, a * a_sh, a)
    return a, b


def _shift_rows(prev8, cur, k):
    ext = jnp.concatenate([prev8, cur], axis=0)
    return pltpu.roll(ext, k, 0)[8:, :]


def _gates(xc, w_ra, b_ra, w_ri, b_ri, ls):
    xb = xc.astype(bf16)
    r = _sigmoid(_dot(xb, w_ra) + b_ra)
    ig = _sigmoid(_dot(xb, w_ri) + b_ri)
    la = LRU_C * r * ls
    a = jnp.exp(la)
    mult = jnp.sqrt(jnp.tanh(-la) * (1.0 + a * a))
    return xb, r, ig, a, mult


_RNN_IN_SPECS = lambda rows: [
    pl.BlockSpec((1, 24, 256), lambda n: (n, 0, 0)),
    pl.BlockSpec((1, RNN_BLOCK), lambda n: (0, n)),
    pl.BlockSpec((N_DEV, 2, None, 32, RNN_BLOCK), lambda n: (0, 0, n, 0, 0)),
    pl.BlockSpec((1, RNN_BLOCK), lambda n: (0, n)),
    pl.BlockSpec((1, RNN_BLOCK), lambda n: (0, n)),
    pl.BlockSpec((1, RNN_BLOCK), lambda n: (0, n)),
]


def _rnn_fwd(z, smallw, conv_b, wrg, b_ra, b_ri, lam):
    rows = z.shape[0]
    nb = rows // BLK
    col = lambda off: pl.BlockSpec((rows, RNN_BLOCK), lambda n: (0, off // RNN_BLOCK + n))

    def body(xr_ref, gr_ref, sw_ref, cb_ref, w_ref, bra_ref, bri_ref, lam_ref, xc_ref, hr_ref, ya_ref, yat_ref, a_s):
        cw = sw_ref[0, N_META:24, :]
        cb = cb_ref[...]
        w_ra = w_ref[:, 0].reshape(RNN_BLOCK, RNN_BLOCK)
        w_ri = w_ref[:, 1].reshape(RNN_BLOCK, RNN_BLOCK)
        b_ra_v, b_ri_v = bra_ref[...], bri_ref[...]
        ls = _log_sigmoid(lam_ref[...])
        rid = lax.broadcasted_iota(jnp.int32, (BLK, 1), 0)

        def blk_step(i, carry):
            r0 = pl.multiple_of(i * BLK, BLK)
            grow = rid + r0
            valid = grow >= ROW0
            cur = jnp.where(valid, xr_ref[pl.ds(r0, BLK), :], 0.0)
            prev8 = xr_ref[pl.ds(pl.multiple_of(jnp.maximum(r0 - 8, 0), 8), 8), :] * (i > 0).astype(f32)
            xc = cb + cw[0:1] * cur
            for k in range(1, CONV_WIDTH):
                xc = xc + cw[k:k + 1] * _shift_rows(prev8, cur, k)
            xc_ref[pl.ds(r0, BLK), :] = xc
            _, _, ig, a, mult = _gates(xc, w_ra, b_ra_v, w_ri, b_ri_v, ls)
            mult = jnp.where(grow == ROW0, 1.0, mult)
            a_s[pl.ds(r0, BLK), :] = a
            hr_ref[pl.ds(r0, BLK), :] = jnp.where(valid, mult * ig * xc, 0.0)
            return carry

        lax.fori_loop(0, nb, blk_step, 0)

        def scan_step(j, carry):
            r0 = pl.multiple_of(j * SCAN_ROWS, SCAN_ROWS)
            tiles = [_scan8(a_s[pl.ds(r0 + 8 * k, 8), :], hr_ref[pl.ds(r0 + 8 * k, 8), :], False)
                     for k in range(SCAN_ROWS // 8)]
            for k, (a, b) in enumerate(tiles):
                h = b + a * carry
                hr_ref[pl.ds(r0 + 8 * k, 8), :] = h
                carry = jnp.broadcast_to(h[7:8, :], (8, RNN_BLOCK))
            return carry

        lax.fori_loop(0, rows // SCAN_ROWS, scan_step, jnp.zeros((8, RNN_BLOCK), f32))

        def gate_step(i, carry):
            r0 = pl.multiple_of(i * BLK, BLK)
            ya_ref[pl.ds(r0, BLK), :] = (hr_ref[pl.ds(r0, BLK), :]
                                         * _silu_and_grad(gr_ref[pl.ds(r0, BLK), :])[0]).astype(bf16)
            return carry

        lax.fori_loop(0, nb, gate_step, 0)
        yat_ref[...] = ya_ref[...].astype(f32).T.astype(bf16)

    return pl.pallas_call(
        body, grid=(N_RNN_BLOCKS,),
        in_specs=[col(0), col(OFF_GR)] + _RNN_IN_SPECS(rows),
        out_specs=[pl.BlockSpec((rows, RNN_BLOCK), lambda n: (0, n))] * 3
                  + [pl.BlockSpec((RNN_BLOCK, rows), lambda n: (n, 0))],
        out_shape=[SDS((rows, D), f32), SDS((rows, D), f32), SDS((rows, D), bf16), SDS((D, rows), bf16)],
        scratch_shapes=[pltpu.VMEM((rows, RNN_BLOCK), f32)],
        name="rnn_fwd", compiler_params=_cp(("arbitrary",)),
    )(z, z, smallw, conv_b, wrg, b_ra, b_ri, lam)


def _rnn_bwd(dya, hr, xc, z, smallw, conv_b, wrg, b_ra, b_ri, lam):
    rows = z.shape[0]
    nb = rows // BLK
    col = lambda off: pl.BlockSpec((rows, RNN_BLOCK), lambda n: (0, off // RNN_BLOCK + n))
    blk = pl.BlockSpec((rows, RNN_BLOCK), lambda n: (0, n))

    def body(dya_ref, hr_ref, xc_ref, xr_ref, gr_ref, sw_ref, cb_ref, w_ref, bra_ref, bri_ref, lam_ref,
             dxr_ref, dgr_ref, dw_ref, vec_ref, a_s, lam_s, dxc_s, r_s, ig_s, mult_s, dw_s):
        cw = sw_ref[0, N_META:24, :]
        w_ra = w_ref[:, 0].reshape(RNN_BLOCK, RNN_BLOCK)
        w_ri = w_ref[:, 1].reshape(RNN_BLOCK, RNN_BLOCK)
        b_ra_v, b_ri_v = bra_ref[...], bri_ref[...]
        lam_v = lam_ref[...]
        ls = _log_sigmoid(lam_v)
        rid = lax.broadcasted_iota(jnp.int32, (BLK, 1), 0)
        zrow = jnp.zeros((1, RNN_BLOCK), f32)

        def p1(i, carry):
            r0 = pl.multiple_of(i * BLK, BLK)
            sl = pl.ds(r0, BLK)
            _, r, ig, a, mult = _gates(xc_ref[sl, :], w_ra, b_ra_v, w_ri, b_ri_v, ls)
            a_s[sl, :] = a
            r_s[sl, :] = r
            ig_s[sl, :] = ig
            mult_s[sl, :] = mult
            sg, dsg = _silu_and_grad(gr_ref[sl, :])
            d = dya_ref[sl, :]
            lam_s[sl, :] = d * sg
            dgr_ref[sl, :] = (d * hr_ref[sl, :] * dsg).astype(bf16)
            return carry

        lax.fori_loop(0, nb, p1, 0)

        def p2(jj, carry):
            r0 = pl.multiple_of((rows // SCAN_ROWS - 1 - jj) * SCAN_ROWS, SCAN_ROWS)
            idx = lax.broadcasted_iota(jnp.int32, (8, RNN_BLOCK), 0)
            tiles = []
            for k in range(SCAN_ROWS // 8):
                sl = pl.ds(r0 + 8 * k, 8)
                a, g = a_s[sl, :], lam_s[sl, :]
                tiles.append((g, *_scan8(a, a * g, True)))
            for k in reversed(range(SCAN_ROWS // 8)):
                g, ca, cb_ = tiles[k]
                mu = cb_ + ca * carry
                lam_s[pl.ds(r0 + 8 * k, 8), :] = g + jnp.where(idx < 7, pltpu.roll(mu, 7, 0), carry)
                carry = jnp.broadcast_to(mu[0:1, :], (8, RNN_BLOCK))
            return carry

        lax.fori_loop(0, rows // SCAN_ROWS, p2, jnp.zeros((8, RNN_BLOCK), f32))

        dw_s[...] = jnp.zeros_like(dw_s)

        def p3(i, carry):
            d_bra, d_bri, d_ls = carry
            r0 = pl.multiple_of(i * BLK, BLK)
            sl = pl.ds(r0, BLK)
            grow = rid + r0
            valid = grow >= ROW0
            first = grow == ROW0
            xcv = xc_ref[sl, :]
            xb = xcv.astype(bf16)
            r, ig, a = r_s[sl, :], ig_s[sl, :], a_s[sl, :]
            mult = jnp.where(first, 1.0, mult_s[sl, :])
            lam_t = lam_s[sl, :]
            du = jnp.where(valid, lam_t, 0.0)
            hprev = _shift_rows(hr_ref[pl.ds(pl.multiple_of(jnp.maximum(r0 - 8, 0), 8), 8), :] * (i > 0).astype(f32), hr_ref[sl, :], 1)
            da = lam_t * hprev
            dmult = jnp.where(first, 0.0, du * ig * xcv)
            di = du * mult * xcv
            dxc = du * mult * ig
            ratio = jnp.where(valid & jnp.logical_not(first), a * a / mult, 0.0)
            dla = da * a - dmult * ratio
            dpr = (dla * (LRU_C * ls)) * r * (1.0 - r)
            dpi = di * ig * (1.0 - ig)
            dprb, dpib = dpr.astype(bf16), dpi.astype(bf16)
            dw_s[0] += _dot_tn(xb, dprb)
            dw_s[1] += _dot_tn(xb, dpib)
            dxc_s[sl, :] = dxc + _dot_nt(dprb, w_ra) + _dot_nt(dpib, w_ri)
            return d_bra + _colsum(dpr), d_bri + _colsum(dpi), d_ls + _colsum(dla * (LRU_C * r))

        d_bra, d_bri, d_ls = lax.fori_loop(0, nb, p3, (zrow, zrow, zrow))

        def p4(i, carry):
            d_cb, d_w0, d_w1, d_w2, d_w3 = carry
            r0 = pl.multiple_of(i * BLK, BLK)
            sl = pl.ds(r0, BLK)
            grow = rid + r0
            valid = grow >= ROW0
            dxc = dxc_s[sl, :]
            nxt = dxc_s[pl.ds(pl.multiple_of(jnp.minimum(r0 + BLK, rows - 8), 8), 8), :] * (i < nb - 1).astype(f32)
            ext = jnp.concatenate([dxc, nxt], axis=0)
            dxr = cw[0:1] * dxc
            for k in range(1, CONV_WIDTH):
                dxr = dxr + cw[k:k + 1] * pltpu.roll(ext, BLK + 8 - k, 0)[:BLK, :]
            dxr_ref[sl, :] = jnp.where(valid, dxr, 0.0).astype(bf16)
            cur = jnp.where(valid, xr_ref[sl, :], 0.0)
            prev8 = xr_ref[pl.ds(pl.multiple_of(jnp.maximum(r0 - 8, 0), 8), 8), :] * (i > 0).astype(f32)
            dws = [d_w0 + _colsum(dxc * cur)]
            for k, acc in ((1, d_w1), (2, d_w2), (3, d_w3)):
                dws.append(acc + _colsum(dxc * _shift_rows(prev8, cur, k)))
            return (d_cb + _colsum(dxc), *dws)

        d_cb, d_w0, d_w1, d_w2, d_w3 = lax.fori_loop(0, nb, p4, (zrow,) * 5)

        d_lam = d_ls * _sigmoid(-lam_v)
        vec_ref[...] = jnp.concatenate([d_bra, d_bri, d_lam, d_cb, d_w0, d_w1, d_w2, d_w3], axis=0)
        dw_ref[:, 0] = dw_s[0].astype(bf16).reshape(N_DEV, 32, RNN_BLOCK)
        dw_ref[:, 1] = dw_s[1].astype(bf16).reshape(N_DEV, 32, RNN_BLOCK)

    return pl.pallas_call(
        body, grid=(N_RNN_BLOCKS,),
        in_specs=[blk, blk, blk, col(0), col(OFF_GR)] + _RNN_IN_SPECS(rows),
        out_specs=[blk, blk,
                   pl.BlockSpec((N_DEV, 2, None, 32, RNN_BLOCK), lambda n: (0, 0, n, 0, 0)),
                   pl.BlockSpec((8, RNN_BLOCK), lambda n: (0, n))],
        out_shape=[SDS((rows, D), bf16), SDS((rows, D), bf16),
                   SDS((N_DEV, 2, N_RNN_BLOCKS, 32, RNN_BLOCK), bf16), SDS((8, D), f32)],
        scratch_shapes=[pltpu.VMEM((rows, RNN_BLOCK), f32)] * 6 + [pltpu.VMEM((2, RNN_BLOCK, RNN_BLOCK), f32)],
        name="rnn_bwd", compiler_params=_cp(("arbitrary",), 48),
    )(dya, hr, xc, z, z, smallw, conv_b, wrg, b_ra, b_ri, lam)


def _rope_tables(rows):
    half = jnp.arange(HALF, dtype=f32)
    inv = ROPE_THETA ** (-half / HALF)
    pos = (jnp.arange(rows) - ROW0).astype(f32)
    ang = pos[:, None] * inv[None, :]
    cos, sin = jnp.cos(ang), jnp.sin(ang)
    cos128 = jnp.concatenate([cos, cos, cos, cos], axis=1)
    sin128 = jnp.concatenate([-sin, sin, -sin, sin], axis=1)
    return cos128, sin128


def _rope128(x, cos128, sin128):
    lane = lax.broadcasted_iota(jnp.int32, x.shape, 1)
    swapped = jnp.where(lane % HEAD_DIM < HALF, pltpu.roll(x, 128 - HALF, 1), pltpu.roll(x, HALF, 1))
    return x * cos128 + swapped * sin128


def _qkv_prep(z, cos128, sin128):
    rows = z.shape[0]

    def body(q_ref, kv_ref, c_ref, s_ref, qo_ref, ko_ref, vo_ref):
        c, s = c_ref[...], s_ref[...]
        for g in range(D // 128):
            qo_ref[:, g * 128:(g + 1) * 128] = (_rope128(q_ref[:, g * 128:(g + 1) * 128], c, s)
                                                * (HEAD_DIM ** -0.5)).astype(bf16)
        for g in range(2):
            kr = _rope128(kv_ref[:, g * 128:(g + 1) * 128], c, s)
            for j in range(2):
                ko_ref[2 * g + j] = kr[:, j * HEAD_DIM:(j + 1) * HEAD_DIM].astype(bf16)
        for h in range(N_KV):
            vo_ref[h] = kv_ref[:, 256 + h * HEAD_DIM:256 + (h + 1) * HEAD_DIM].astype(bf16)

    return pl.pallas_call(
        body, grid=(rows // BLK,),
        in_specs=[pl.BlockSpec((BLK, D), lambda i: (i, OFF_Q // D)),
                  pl.BlockSpec((BLK, 512), lambda i: (i, OFF_K // 512)),
                  pl.BlockSpec((BLK, 128), lambda i: (i, 0)),
                  pl.BlockSpec((BLK, 128), lambda i: (i, 0))],
        out_specs=[pl.BlockSpec((BLK, D), lambda i: (i, 0)),
                   pl.BlockSpec((N_KV, BLK, HEAD_DIM), lambda i: (0, i, 0)),
                   pl.BlockSpec((N_KV, BLK, HEAD_DIM), lambda i: (0, i, 0))],
        out_shape=[SDS((rows, D), bf16), SDS((N_KV, rows, HEAD_DIM), bf16), SDS((N_KV, rows, HEAD_DIM), bf16)],
        name="qkv_prep", compiler_params=_cp(("arbitrary",)),
    )(z, z, cos128, sin128)


def _attn_mask(n):
    qi = n * BLK + lax.broadcasted_iota(jnp.int32, (BLK, 2 * BLK + N_META), 0)
    c = lax.broadcasted_iota(jnp.int32, (BLK, 2 * BLK + N_META), 1)
    jb = (n - 1) * BLK + c
    band = (jb >= BLK) & (jb <= qi) & (qi - jb < BLK)
    meta = (ROW0 + c - 2 * BLK) <= qi
    return ((c < 2 * BLK) & band) | ((c >= 2 * BLK) & meta)


N_KEYS = 2 * BLK + N_META


def _stack_heads(t):
    return jnp.concatenate([t[:, g * HEAD_DIM:(g + 1) * HEAD_DIM] for g in range(GROUP)], axis=0)


def _sink_column(sink_ref, h):
    g = lax.broadcasted_iota(jnp.int32, (GROUP, 1, 1), 0)
    col = jnp.zeros((GROUP, 1, 1), f32)
    for j in range(GROUP):
        col = jnp.where(g == j, sink_ref[h * GROUP + j], col)
    return col


def _kv_specs(last):
    cl = lambda n: jnp.minimum(n, last)
    return [pl.BlockSpec((None, N_META, HEAD_DIM), lambda h, n: (h, ROW0 // N_META, 0)),
            pl.BlockSpec((None, BLK, HEAD_DIM), lambda h, n: (h, jnp.maximum(cl(n) - 1, 0), 0)),
            pl.BlockSpec((None, BLK, HEAD_DIM), lambda h, n: (h, cl(n), 0))]


def _attn_fwd(q_r, k_r, v_b, z, sinks):
    rows = q_r.shape[0]
    nb = rows // BLK

    def body(sink_ref, q_ref, km_ref, kp_ref, kc_ref, vm_ref, vp_ref, vc_ref, ga_ref, o_ref, yb_ref, ybt_ref, lse_ref):
        h, n = pl.program_id(0), pl.program_id(1)
        kk = jnp.concatenate([kp_ref[...], kc_ref[...], km_ref[...]], axis=0)
        vv = jnp.concatenate([vp_ref[...], vc_ref[...], vm_ref[...]], axis=0)
        q2 = _stack_heads(q_ref[...])
        s = jnp.where(_attn_mask(n)[None], _dot_nt(q2, kk).reshape(GROUP, BLK, N_KEYS), NEG_INF)
        sink = _sink_column(sink_ref, h)
        m = jnp.maximum(jnp.max(s, axis=-1, keepdims=True), sink)
        p = jnp.exp(s - m)
        den = jnp.sum(p, axis=-1, keepdims=True) + jnp.exp(sink - m)
        o2 = _dot((p / den).astype(bf16).reshape(GROUP * BLK, N_KEYS), vv)
        lse = m + jnp.log(den)
        for g in range(GROUP):
            o_ref[:, g * HEAD_DIM:(g + 1) * HEAD_DIM] = o2[g * BLK:(g + 1) * BLK]
            lse_ref[:, g:g + 1] = lse[g]
        yb = o_ref[...] * _silu_and_grad(ga_ref[...])[0]
        yb_ref[...] = yb.astype(bf16)
        ybt_ref[...] = yb.T.astype(bf16)

    tile = pl.BlockSpec((BLK, 512), lambda h, n: (n, h))
    return pl.pallas_call(
        body, grid=(N_KV, nb),
        in_specs=[pl.BlockSpec(memory_space=pltpu.SMEM), tile] + _kv_specs(nb - 1) + _kv_specs(nb - 1)
                 + [pl.BlockSpec((BLK, 512), lambda h, n: (n, OFF_GA // 512 + h))],
        out_specs=[tile, tile, pl.BlockSpec((512, BLK), lambda h, n: (h, n)),
                   pl.BlockSpec((None, BLK, GROUP), lambda h, n: (h, n, 0))],
        out_shape=[SDS((rows, D), f32), SDS((rows, D), bf16), SDS((D, rows), bf16),
                   SDS((N_KV, rows, GROUP), f32)],
        name="attn_fwd", compiler_params=_cp(("arbitrary", "arbitrary")),
    )(sinks, q_r, k_r, k_r, k_r, v_b, v_b, v_b, z)


def _attn_bwd(dyb, o32, lse, q_r, k_r, v_b, z, sinks):
    rows = q_r.shape[0]
    nb = rows // BLK
    cl = lambda n: jnp.minimum(n, nb - 1)

    def body(sink_ref, dyb_ref, o_ref, lse_ref, q_ref, km_ref, kp_ref, kc_ref, vm_ref, vp_ref, vc_ref, ga_ref,
             dq_ref, dga_ref, dk_ref, dv_ref, dkm_ref, dvm_ref, dsr_ref, ck_s, cv_s):
        h, n = pl.program_id(0), pl.program_id(1)

        @pl.when(n == 0)
        def _():
            dkm_ref[...] = jnp.zeros_like(dkm_ref)
            dvm_ref[...] = jnp.zeros_like(dvm_ref)
            ck_s[...] = jnp.zeros_like(ck_s)
            cv_s[...] = jnp.zeros_like(cv_s)

        @pl.when(n < nb)
        def _():
            kk = jnp.concatenate([kp_ref[...], kc_ref[...], km_ref[...]], axis=0)
            vv = jnp.concatenate([vp_ref[...], vc_ref[...], vm_ref[...]], axis=0)
            sg, dsg = _silu_and_grad(ga_ref[...])
            dyb_v = dyb_ref[...]
            o_v = o_ref[...]
            dga_ref[...] = (dyb_v * o_v * dsg).astype(bf16)
            q2 = _stack_heads(q_ref[...])
            do2 = _stack_heads(dyb_v * sg)
            lse_v = lse_ref[...]
            lse = jnp.concatenate([lse_v[:, g:g + 1] for g in range(GROUP)], axis=0).reshape(GROUP, BLK, 1)
            delta = jnp.sum(do2 * _stack_heads(o_v), axis=-1, keepdims=True).reshape(GROUP, BLK, 1)
            s = jnp.where(_attn_mask(n)[None], _dot_nt(q2, kk).reshape(GROUP, BLK, N_KEYS), NEG_INF)
            p = jnp.exp(s - lse)
            do2b = do2.astype(bf16)
            ds = (p * (_dot_nt(do2b, vv).reshape(GROUP, BLK, N_KEYS) - delta)).astype(bf16)
            ds = ds.reshape(GROUP * BLK, N_KEYS)
            dsr = -jnp.exp(_sink_column(sink_ref, h) - lse) * delta
            dq2 = _dot(ds, kk)
            for g in range(GROUP):
                dq_ref[:, g * HEAD_DIM:(g + 1) * HEAD_DIM] = dq2[g * BLK:(g + 1) * BLK]
                dsr_ref[:, g:g + 1] = dsr[g]
            dkk = _dot_tn(ds, q2)
            dvv = _dot_tn(p.astype(bf16).reshape(GROUP * BLK, N_KEYS), do2b)
            dk_ref[...] = ck_s[...] + dkk[:BLK]
            dv_ref[...] = cv_s[...] + dvv[:BLK]
            ck_s[...] = dkk[BLK:2 * BLK]
            cv_s[...] = dvv[BLK:2 * BLK]
            dkm_ref[...] += dkk[2 * BLK:]
            dvm_ref[...] += dvv[2 * BLK:]

        @pl.when(n == nb)
        def _():
            dk_ref[...] = ck_s[...]
            dv_ref[...] = cv_s[...]

    tile = pl.BlockSpec((BLK, 512), lambda h, n: (cl(n), h))
    kvout = pl.BlockSpec((None, BLK, HEAD_DIM), lambda h, n: (h, jnp.maximum(n - 1, 0), 0))
    mout = pl.BlockSpec((None, N_META, HEAD_DIM), lambda h, n: (h, 0, 0))
    stat = pl.BlockSpec((None, BLK, GROUP), lambda h, n: (h, cl(n), 0))
    return pl.pallas_call(
        body, grid=(N_KV, nb + 1),
        in_specs=[pl.BlockSpec(memory_space=pltpu.SMEM), tile, tile, stat, tile] + _kv_specs(nb - 1)
                 + _kv_specs(nb - 1) + [pl.BlockSpec((BLK, 512), lambda h, n: (cl(n), OFF_GA // 512 + h))],
        out_specs=[tile, tile, kvout, kvout, mout, mout, stat],
        out_shape=[SDS((rows, D), f32), SDS((rows, D), bf16),
                   SDS((N_KV, rows, HEAD_DIM), f32), SDS((N_KV, rows, HEAD_DIM), f32),
                   SDS((N_KV, N_META, HEAD_DIM), f32), SDS((N_KV, N_META, HEAD_DIM), f32),
                   SDS((N_KV, rows, GROUP), f32)],
        scratch_shapes=[pltpu.VMEM((BLK, HEAD_DIM), f32), pltpu.VMEM((BLK, HEAD_DIM), f32)],
        name="attn_bwd", compiler_params=_cp(("arbitrary", "arbitrary")),
    )(sinks, dyb, o32, lse, q_r, k_r, k_r, k_r, v_b, v_b, v_b, z)


def _qkv_finish(dq, dk, dv, dkm, dvm, cos128, sin128):
    rows = dq.shape[0]

    def body(dq_ref, dk_ref, dv_ref, dkm_ref, dvm_ref, c_ref, s_ref, oq_ref, okv_ref):
        first = (pl.program_id(0) == 0).astype(f32)
        c, s = c_ref[...], -s_ref[...]
        for g in range(D // 128):
            oq_ref[:, g * 128:(g + 1) * 128] = (_rope128(dq_ref[:, g * 128:(g + 1) * 128], c, s)
                                                * (HEAD_DIM ** -0.5)).astype(bf16)
        pad = jnp.zeros((ROW0, HEAD_DIM), f32)
        ks = [dk_ref[h] + first * jnp.concatenate([pad, dkm_ref[h]], axis=0) for h in range(N_KV)]
        vs = [dv_ref[h] + first * jnp.concatenate([pad, dvm_ref[h]], axis=0) for h in range(N_KV)]
        for g in range(2):
            kp = jnp.concatenate([ks[2 * g], ks[2 * g + 1]], axis=1)
            okv_ref[:, g * 128:(g + 1) * 128] = _rope128(kp, c, s).astype(bf16)
            okv_ref[:, 256 + g * 128:256 + (g + 1) * 128] = jnp.concatenate([vs[2 * g], vs[2 * g + 1]], axis=1).astype(bf16)

    kv = pl.BlockSpec((N_KV, BLK, HEAD_DIM), lambda i: (0, i, 0))
    mt = pl.BlockSpec((N_KV, N_META, HEAD_DIM), lambda i: (0, 0, 0))
    return pl.pallas_call(
        body, grid=(rows // BLK,),
        in_specs=[pl.BlockSpec((BLK, D), lambda i: (i, 0)), kv, kv, mt, mt,
                  pl.BlockSpec((BLK, 128), lambda i: (i, 0)), pl.BlockSpec((BLK, 128), lambda i: (i, 0))],
        out_specs=[pl.BlockSpec((BLK, D), lambda i: (i, 0)), pl.BlockSpec((BLK, 512), lambda i: (i, 0))],
        out_shape=[SDS((rows, D), bf16), SDS((rows, 512), bf16)],
        name="qkv_finish", compiler_params=_cp(("arbitrary",)),
    )(dq, dk, dv, dkm, dvm, cos128, sin128)


_TW = 512


def _mix_specs(rows):
    tr = _row_chunk(rows)
    tile = pl.BlockSpec((tr, _TW), lambda i, j: (i, j))
    ga = pl.BlockSpec((tr, _TW), lambda i, j: (i, OFF_G // _TW + j))
    gb = pl.BlockSpec((tr, _TW), lambda i, j: (i, (OFF_G + D) // _TW + j))
    return (rows // tr, D // _TW), tile, ga, gb


def _mix_fwd(y_a, y_b, z):
    rows = y_a.shape[0]
    tw = 256
    col = lambda off: pl.BlockSpec((rows, tw), lambda j: (0, off // tw + j))

    def body(ya_ref, yb_ref, ga_ref, gb_ref, o_ref, ot_ref):
        mixed = (_sigmoid(ga_ref[...]) * ya_ref[...].astype(f32)
                 + _sigmoid(gb_ref[...]) * yb_ref[...].astype(f32))
        o_ref[...] = mixed.astype(bf16)
        ot_ref[...] = mixed.T.astype(bf16)

    return pl.pallas_call(
        body, grid=(D // tw,), in_specs=[col(0), col(0), col(OFF_G), col(OFF_G + D)],
        out_specs=[col(0), pl.BlockSpec((tw, rows), lambda j: (j, 0))],
        out_shape=[SDS((rows, D), bf16), SDS((D, rows), bf16)],
        name="mix_fwd", compiler_params=_cp(("arbitrary",)),
    )(y_a, y_b, z, z)


def _mix_bwd(dmixed, y_a, y_b, z):
    rows = y_a.shape[0]
    grid, _mix_tile, _mix_ga, _mix_gb = _mix_specs(rows)

    def body(dm_ref, ya_ref, yb_ref, ga_ref, gb_ref, dya_ref, dyb_ref, dga_ref, dgb_ref):
        dm = dm_ref[...].astype(f32)
        sa, sb = _sigmoid(ga_ref[...]), _sigmoid(gb_ref[...])
        dya_ref[...] = (dm * sa).astype(bf16)
        dyb_ref[...] = (dm * sb).astype(bf16)
        dga_ref[...] = (dm * ya_ref[...].astype(f32) * sa * (1.0 - sa)).astype(bf16)
        dgb_ref[...] = (dm * yb_ref[...].astype(f32) * sb * (1.0 - sb)).astype(bf16)

    return pl.pallas_call(
        body, grid=grid, in_specs=[_mix_tile, _mix_tile, _mix_tile, _mix_ga, _mix_gb],
        out_specs=[_mix_tile] * 4, out_shape=[SDS((rows, D), bf16)] * 4,
        name="mix_bwd", compiler_params=_cp(("arbitrary", "arbitrary")),
    )(dmixed, y_a, y_b, z, z)


def _final_ln(out32, h32, tgt, ln_g, ln_b):
    rows = out32.shape[0]

    def body(o_ref, h_ref, t_ref, g_ref, b_ref, du_ref, dub_ref, st_ref):
        i = pl.program_id(0)
        g = g_ref[...]
        y, xhat, rstd = _ln_rows(ALPHA * h_ref[...] + o_ref[...], g, b_ref[...])
        e = jnp.where(i > 0, y - t_ref[0], 0.0)
        dy = e * (1.0 / D)
        du = _ln_rows_bwd(dy, g, xhat, rstd)
        du_ref[...] = du
        dub_ref[...] = du.astype(bf16)
        st = jnp.concatenate([_colsum(dy * xhat), _colsum(dy), _colsum(du), _colsum(e * e) * (0.5 / D),
                              jnp.zeros((4, D), f32)], axis=0)

        @pl.when(i == 0)
        def _():
            st_ref[...] = st

        @pl.when(i > 0)
        def _():
            st_ref[...] += st

    row = pl.BlockSpec((BLK, D), lambda i: (i, 0))
    vec = pl.BlockSpec((1, D), lambda i: (0, 0))
    return pl.pallas_call(
        body, grid=(rows // BLK,),
        in_specs=[row, row, pl.BlockSpec((1, BLK, D), lambda i: (0, jnp.maximum(i - 1, 0), 0)), vec, vec],
        out_specs=[row, row, pl.BlockSpec((8, D), lambda i: (0, 0))],
        out_shape=[SDS((rows, D), f32), SDS((rows, D), bf16), SDS((8, D), f32)],
        name="final_ln", compiler_params=_cp(("arbitrary",)),
    )(out32, h32, tgt, ln_g, ln_b)


def _step_rnn(h32, hb, z, wrg, smallw, p, zero):
    rows = z.shape[0]
    cos128, sin128 = _rope_tables(rows)
    cos128 = cos128 + zero
    xc, hr, ya, ya_t = _rnn_fwd(z, smallw, p["conv_b"] + zero, wrg, p["b_ra"], p["b_ri"], p["lru_lambda"])
    q_r, k_r, v_b = _qkv_prep(z, cos128, sin128)
    return dict(cos128=cos128, sin128=sin128, h32=h32, hb=hb, z=z, xc=xc, hr=hr, ya=ya, ya_t=ya_t,
                q_r=q_r, k_r=k_r, v_b=v_b)


def _step_attn(s, p, zero):
    sinks = p["sinks"].reshape(N_KV * GROUP) + zero[0]
    o32, yb, yb_t, lse = _attn_fwd(s["q_r"], s["k_r"], s["v_b"], s["z"], sinks)
    return dict(s, sinks=sinks, o32=o32, yb=yb, yb_t=yb_t, lse=lse)


def _step_merge(s, tgt, w3, p):
    ya, yb, z = s["ya"], s["yb"], s["z"]
    y_a = _mm(ya, w3, sel=0, out_dtype=bf16, name="mm_ya")
    y_b = _mm(yb, w3, sel=1, out_dtype=bf16, name="mm_yb")
    mixed, mixed_t = _mix_fwd(y_a, y_b, z)
    out32 = _mm(mixed, w3, sel=2, bias=p["b_o"], name="mm_out")
    du32, dub, st_out = _final_ln(out32, s["h32"], tgt, p["ln_g"], p["ln_b"])

    g_wo = _mm(mixed_t, dub, out_dtype=bf16, name="mm_dwo")
    dmixed = _mm(dub, w3, sel=2, nt=True, out_dtype=bf16, name="mm_dmixed")
    dya_b, dyb_b, dma, dmb = _mix_bwd(dmixed, y_a, y_b, z)
    g_wrnn = _mm(s["ya_t"], dya_b, out_dtype=bf16, name="mm_dwrnn")
    g_wattn = _mm(s["yb_t"], dyb_b, out_dtype=bf16, name="mm_dwattn")
    dya = _mm(dya_b, w3, sel=0, nt=True, name="mm_dya")
    dyb = _mm(dyb_b, w3, sel=1, nt=True, name="mm_dyb")
    return dict(du32=du32, st_out=st_out, dma=dma, dmb=dmb, dya=dya, dyb=dyb, g_wo=g_wo, g_wrnn=g_wrnn,
                g_wattn=g_wattn)


def _step_backward(s, t, wrg, smallw, p, conv_b):
    z = s["z"]
    dxr, dgr, g_wrg, vec_rnn = _rnn_bwd(t["dya"], s["hr"], s["xc"], z, smallw, conv_b, wrg, p["b_ra"], p["b_ri"],
                                        p["lru_lambda"])
    dq_r, dga, dk, dv, dkm, dvm, dsr = _attn_bwd(t["dyb"], s["o32"], s["lse"], s["q_r"], s["k_r"], s["v_b"], z,
                                                 s["sinks"])
    dq, dkv = _qkv_finish(dq_r, dk, dv, dkm, dvm, s["cos128"], s["sin128"])
    dz_parts = [(dxr, D), (dgr, D), (dq, D), (dkv, 512), (dga, D), (t["dma"], D), (t["dmb"], D)]
    return dict(vec_rnn=vec_rnn, dsr=dsr, g_wrg=g_wrg, dz_parts=dz_parts)


def _step_input_grad(dh_lo, dh_hi, du32, x, smallw, p, after):
    grad_x, dmeta, st_emb = _ln_emb_bwd(dh_lo, dh_hi, du32, x, smallw, p["ln_emb_g"], after)
    return dict(grad_x=grad_x, dmeta=dmeta, st_emb=st_emb)


_ANY = pl.BlockSpec(memory_space=pl.ANY)
_VMEM = pl.BlockSpec(memory_space=pltpu.VMEM)


def _place():
    x, y, c = lax.axis_index("x"), lax.axis_index("y"), lax.axis_index("c")
    return x, y, c


def _dev(px, py, pc):
    return 4 * px + 2 * py + pc


def _tile_rows(r):
    return max(t for t in range(16, 321, 16) if r % t == 0) if r > 320 else r


def _cast_w_in(w_in_t):
    tm = _tile_rows(SHARD_IN)

    def body(i_ref, o_ref):
        o_ref[...] = i_ref[...].astype(bf16)

    return pl.pallas_call(
        body, grid=(SHARD_IN // tm,),
        in_specs=[pl.BlockSpec((tm, D), lambda i: (i, 0))],
        out_specs=pl.BlockSpec((tm, D), lambda i: (i, 0)),
        out_shape=SDS((SHARD_IN, D), bf16), name="cast_w_in", compiler_params=_cp(("arbitrary",)),
    )(w_in_t)


def _cast_small(w_rnn_out, w_attn_out, w_o, w_ra, w_ri, meta, conv_w):
    def body(a_ref, b_ref, c_ref, ra_ref, ri_ref, m_ref, cw_ref, w3_ref, wrg_ref, sw_ref):
        w3_ref[0] = a_ref[0].astype(bf16)
        w3_ref[1] = b_ref[0].astype(bf16)
        w3_ref[2] = c_ref[0].astype(bf16)
        wrg_ref[0] = ra_ref[0].astype(bf16)
        wrg_ref[1] = ri_ref[0].astype(bf16)
        sw_ref[...] = jnp.concatenate([m_ref[...], cw_ref[0], jnp.zeros((4, 256), f32)], axis=0)

    return pl.pallas_call(
        body,
        out_shape=[SDS((3, 256, D), bf16), SDS((2, N_RNN_BLOCKS, 32, RNN_BLOCK), bf16), SDS((24, 256), f32)],
        name="cast_small", compiler_params=_cp(None),
    )(w_rnn_out, w_attn_out, w_o, w_ra, w_ri, meta, conv_w)


def _all_gather(shards, later):
    n = len(shards)
    nl = len(later)

    def body(*refs):
        ins, outs = refs[:n], refs[n + nl:2 * n + nl]
        send_sems, recv_sems, local_sems = refs[2 * (n + nl):]
        x, y, c = _place()
        me, sibling = (x, y, c), (x, y, 1 - c)
        chips = [(1 - x, y), (x, 1 - y), (1 - x, 1 - y)]

        def copy(a, k, block, to, src=None):
            dst = outs[a].at[_dev(*block)]
            return pltpu.make_async_remote_copy(
                src_ref=dst if src is None else src, dst_ref=dst,
                send_sem=send_sems.at[a * 7 + k], recv_sem=recv_sems.at[a * 7 + k],
                device_id=to, device_id_type=MESH)

        all_ins, all_outs = refs[:n + nl], refs[n + nl:2 * (n + nl)]
        mine = [pltpu.make_async_copy(all_ins[a], all_outs[a].at[_dev(*me)], local_sems.at[a]) for a in range(n + nl)]
        for cp in mine:
            cp.start()
        first = []
        for a in range(n):
            first.append(copy(a, 0, me, sibling, src=ins[a]))
            first += [copy(a, 1 + j, me, (*chip, c), src=ins[a]) for j, chip in enumerate(chips)]
        for cp in first:
            cp.start()
        passed = []
        for a in range(n):
            for j, chip in enumerate(chips):
                copy(a, 1 + j, (*chip, c), me).wait_recv()
                cp = copy(a, 4 + j, (*chip, c), sibling)
                cp.start()
                passed.append(cp)
        for a in range(n):
            copy(a, 0, sibling, me).wait_recv()
            for j, chip in enumerate(chips):
                copy(a, 4 + j, (*chip, 1 - c), me).wait_recv()
        for cp in first + passed:
            cp.wait_send()
        for cp in mine:
            cp.wait()

    return pl.pallas_call(
        body, in_specs=[_ANY] * (n + nl), out_specs=[_ANY] * (n + nl),
        out_shape=[SDS((N_DEV, *s.shape), s.dtype) for s in (*shards, *later)],
        scratch_shapes=[pltpu.SemaphoreType.DMA((7 * n,)), pltpu.SemaphoreType.DMA((7 * n,)),
                        pltpu.SemaphoreType.DMA((n + nl,))],
        name="all_gather_weights",
    )(*shards, *later)


_HBM = pl.BlockSpec(memory_space=pltpu.HBM)
_SEM = pl.BlockSpec(memory_space=pltpu.SEMAPHORE)
_PEER_FLIPS = [(f // 4, (f // 2) % 2, f % 2) for f in range(1, N_DEV)]


def _remote(src, dst, send_sems, recv_sems, k, to):
    return pltpu.make_async_remote_copy(src_ref=src, dst_ref=dst, send_sem=send_sems.at[k], recv_sem=recv_sems.at[k],
                                        device_id=to, device_id_type=MESH)


def _copies_direct(same_src):
    def make(srcs, lands, send_sems, recv_sems):
        x, y, c = _place()
        me = _dev(x, y, c)
        out = []
        for a in range(len(srcs)):
            for k, (fx, fy, fc) in enumerate(_PEER_FLIPS):
                peer = ((x + fx) % 2, (y + fy) % 2, (c + fc) % 2)
                src = srcs[a] if same_src else srcs[a].at[_dev(*peer)]
                out.append(_remote(src, lands[a].at[me], send_sems, recv_sems, 7 * a + k, peer))
        return out
    return make


def _copies_siblings(srcs, lands, send_sems, recv_sems):
    x, y, c = _place()
    return [_remote(srcs[a].at[2 * q + (1 - c)], lands[a].at[q], send_sems, recv_sems, 4 * a + q, (x, y, 1 - c))
            for a in range(len(srcs)) for q in range(4)]


def _copies_chips(srcs, lands, send_sems, recv_sems):
    x, y, c = _place()
    chips = [(1 - x, y), (x, 1 - y), (1 - x, 1 - y)]
    return [_remote(srcs[a].at[2 * qx + qy], lands[a].at[j], send_sems, recv_sems, 3 * a + j, (qx, qy, c))
            for a in range(len(srcs)) for j, (qx, qy) in enumerate(chips)]


def _split_start(make, per_array, srcs, lands, dep, name):
    n = len(srcs)

    def body(*refs):
        send_sems, recv_sems, token = refs[2 * n + 1], refs[2 * n + 2], refs[-1]
        for cp in make(refs[:n], refs[n:2 * n], send_sems, recv_sems):
            cp.start()
        token[...] = jnp.zeros_like(token)

    hbm = lambda t: pltpu.with_memory_space_constraint(t, pltpu.HBM)
    res = pl.pallas_call(
        body, name=name,
        out_shape=(pltpu.SemaphoreType.DMA((per_array * n,)), pltpu.SemaphoreType.DMA((per_array * n,)),
                   *[pltpu.HBM(t.shape, t.dtype) for t in (*srcs, *lands)], SDS((8, 128), f32)),
        in_specs=[_HBM] * (2 * n) + [_ANY], out_specs=(_SEM, _SEM, *([_HBM] * (2 * n)), _VMEM),
        input_output_aliases={i: 2 + i for i in range(2 * n)},
        compiler_params=pltpu.CompilerParams(has_side_effects=pltpu.SideEffectType.DATAFLOW_SIDE_EFFECTING),
    )(*[hbm(t) for t in (*srcs, *lands)], dep)
    return res[0], res[1], list(res[2:2 + n]), list(res[2 + n:2 + 2 * n]), res[-1]


def _split_wait(make, send_sems, recv_sems, srcs, lands, after, name):
    n = len(srcs)

    def body(*refs):
        for cp in make(refs[:n], refs[n:2 * n], refs[2 * n], refs[2 * n + 1]):
            cp.wait_send()
            cp.wait_recv()

    res = pl.pallas_call(
        body, name=name,
        out_shape=tuple(pltpu.HBM(t.shape, t.dtype) for t in (*srcs, *lands)),
        in_specs=[_HBM] * (2 * n) + [_SEM, _SEM, _ANY], out_specs=tuple([_HBM] * (2 * n)),
        input_output_aliases={i: i for i in range(2 * n)},
        compiler_params=pltpu.CompilerParams(has_side_effects=pltpu.SideEffectType.DATAFLOW_SIDE_EFFECTING),
    )(*srcs, *lands, send_sems, recv_sems, after)
    return list(res[:n]), list(res[n:])


def _adamw_direct(g, land, me_idx, w, m, v, name):
    r, wd = w.shape
    tr = min(r, 256)

    def body(me_ref, *refs):
        g_ref, peers = refs[0], refs[1:N_DEV]
        w_ref, m_ref, v_ref, g_out, d_out, m_out, v_out = refs[N_DEV:]
        gs = g_ref[...].astype(f32)
        for p_ref in peers:
            gs = gs + p_ref[...].astype(f32)
        d, mn, vn = _adamw(w_ref[...], gs, m_ref[...], v_ref[...])
        g_out[...] = gs
        d_out[...] = d
        m_out[...] = mn
        v_out[...] = vn

    tile = pl.BlockSpec((tr, wd), lambda i, me_ref: (i, 0))
    slot = lambda k: pl.BlockSpec((None, tr, wd), lambda i, me_ref: ((me_ref[0] + k) % N_DEV, i, 0))
    return pl.pallas_call(
        body,
        grid_spec=pltpu.PrefetchScalarGridSpec(
            num_scalar_prefetch=1, grid=(r // tr,),
            in_specs=[slot(0)] + [slot(k) for k in range(1, N_DEV)] + [tile, tile, tile],
            out_specs=[tile] * 4),
        out_shape=[SDS((r, wd), f32)] * 4, name=name, compiler_params=_cp(("arbitrary",), 48),
    )(me_idx, g, *([land] * (N_DEV - 1)), w, m, v)


def _pair_sum(g, r1, c_idx, name):
    _, r, w = g.shape
    tr = _tile_rows(r)

    def body(c_ref, g_ref, r_ref, o_ref):
        o_ref[...] = (g_ref[...].astype(f32) + r_ref[...].astype(f32)).astype(bf16)

    return pl.pallas_call(
        body,
        grid_spec=pltpu.PrefetchScalarGridSpec(
            num_scalar_prefetch=1, grid=(4, r // tr),
            in_specs=[pl.BlockSpec((None, tr, w), lambda q, i, c_ref: (2 * q + c_ref[0], i, 0)),
                      pl.BlockSpec((None, tr, w), lambda q, i, c_ref: (q, i, 0))],
            out_specs=pl.BlockSpec((None, tr, w), lambda q, i, c_ref: (q, i, 0))),
        out_shape=SDS((4, r, w), bf16), name=name, compiler_params=_cp(("arbitrary", "arbitrary")),
    )(c_idx, g, r1)


def _adamw(w, g, m, v):
    m = ADAM_B1 * m + (1.0 - ADAM_B1) * g
    v = ADAM_B2 * v + (1.0 - ADAM_B2) * (g * g)
    m_hat = m / (1.0 - ADAM_B1 ** ADAM_STEP)
    v_hat = v / (1.0 - ADAM_B2 ** ADAM_STEP)
    delta = -ADAM_LR * (m_hat / (jnp.sqrt(v_hat) + ADAM_EPS) + ADAM_WD * w)
    return delta, m, v


def _adamw_big(part, r2, q_idx, w, m, v, name, row_off=0, cols=(0, 1), prev=None):
    r, wd = w.shape
    tr = _tile_rows(r)
    k, ncol = cols
    wp = wd // ncol

    def body(q_ref, p_ref, r_ref, w_ref, m_ref, v_ref, *rest):
        g_out, d_out, m_out, v_out = rest[-4:]
        g = p_ref[...].astype(f32)
        for j in range(3):
            g = g + r_ref[j].astype(f32)
        d, mn, vn = _adamw(w_ref[...], g, m_ref[...], v_ref[...])
        g_out[...] = g
        d_out[...] = d
        m_out[...] = mn
        v_out[...] = vn

    tile = pl.BlockSpec((tr, wp), lambda i, q_ref: (i, k))
    prev = list(prev) if prev is not None else []
    return pl.pallas_call(
        body,
        grid_spec=pltpu.PrefetchScalarGridSpec(
            num_scalar_prefetch=1, grid=(r // tr,),
            in_specs=[pl.BlockSpec((None, tr, wp), lambda i, q_ref: (q_ref[0], row_off + i, 0)),
                      pl.BlockSpec((3, tr, wp), lambda i, q_ref: (0, row_off + i, 0)), tile, tile, tile]
                     + [pl.BlockSpec(memory_space=pl.ANY)] * len(prev),
            out_specs=[tile] * 4),
        out_shape=[SDS((r, wd), f32)] * 4, name=name,
        input_output_aliases={6 + i: i for i in range(len(prev))},
        compiler_params=_cp(("arbitrary",), 48),
    )(q_idx, part, r2, w, m, v, *prev)


_SMALL_ROWS = 24


def _pack_early(vec_rnn, st_out, dsr, db_in):
    def body(vr_ref, so_ref, dsr_ref, db_ref, sm_ref, sm2_ref):
        sm_ref[...] = jnp.zeros_like(sm_ref)
        sm2_ref[...] = jnp.zeros_like(sm2_ref)
        sm_ref[2:3, :] = vr_ref[3:4, :]
        sm_ref[3:6, :] = vr_ref[0:3, :]
        sm_ref[6:7, :] = so_ref[2:3, :]
        sm_ref[7:9, :] = so_ref[0:2, :]
        sm_ref[10:11, :] = so_ref[3:4, :]
        for h in range(N_KV):
            sm_ref[9:10, h * GROUP:(h + 1) * GROUP] = _colsum(dsr_ref[h])
        for j in range(6):
            sm_ref[16 + j:17 + j, :] = db_ref[0:1, j * D:(j + 1) * D]
        sm_ref[22:23, 0:D_IN - 6 * D] = db_ref[0:1, 6 * D:D_IN]
        for s in range(N_DEV):
            sm2_ref[s, 0:CONV_WIDTH, :] = vr_ref[4:8, s * 256:(s + 1) * 256]

    return pl.pallas_call(
        body, out_shape=[SDS((_SMALL_ROWS, D), f32), SDS((N_DEV, 8, 256), f32)],
        name="pack_early", compiler_params=_cp(None),
    )(vec_rnn, st_out, dsr, db_in)


def _pack_late(st_emb, dmeta):
    def body(se_ref, dm_ref, sm_ref, sm2_ref):
        sm_ref[...] = se_ref[...]
        for s in range(N_DEV):
            sm2_ref[s] = dm_ref[:, s * 256:(s + 1) * 256]

    return pl.pallas_call(
        body, out_shape=[SDS((8, D), f32), SDS((N_DEV, N_META, 256), f32)],
        name="pack_late", compiler_params=_cp(None),
    )(st_emb, dmeta)


def _small_allreduce(sm, sm2):
    def body(sm_ref, sm2_ref, o_ref, o2_ref, buf, buf2, send_sems, recv_sems):
        x, y, c = _place()
        me = _dev(x, y, c)
        copies = []
        for f in range(1, N_DEV):
            fx, fy, fc = f // 4, (f // 2) % 2, f % 2
            peer = ((x + fx) % 2, (y + fy) % 2, (c + fc) % 2)
            for t, (src, dst) in enumerate(((sm_ref, buf), (sm2_ref, buf2))):
                k = 2 * (f - 1) + t
                copies.append(pltpu.make_async_remote_copy(
                    src_ref=src, dst_ref=dst.at[me], send_sem=send_sems.at[k], recv_sem=recv_sems.at[k],
                    device_id=peer, device_id_type=MESH))
        for cp in copies:
            cp.start()
        buf[me] = sm_ref[...]
        buf2[me] = sm2_ref[...]
        for cp in copies:
            cp.wait()
        acc, acc2 = buf[0], buf2[0]
        for e in range(1, N_DEV):
            acc, acc2 = acc + buf[e], acc2 + buf2[e]
        o_ref[...] = acc
        o2_ref[...] = acc2

    return pl.pallas_call(
        body, in_specs=[_VMEM, _VMEM], out_specs=[_VMEM, _VMEM],
        out_shape=[SDS(sm.shape, f32), SDS(sm2.shape, f32)],
        scratch_shapes=[pltpu.VMEM((N_DEV, *sm.shape), f32), pltpu.VMEM((N_DEV, *sm2.shape), f32),
                        pltpu.SemaphoreType.DMA((14,)), pltpu.SemaphoreType.DMA((14,))],
        name="small_allreduce",
    )(sm, sm2)


_SMALL_ROW_OF = {"ln_emb_g": 0, "ln_emb_b": 1, "conv_b": 2, "b_ra": 3, "b_ri": 4, "lru_lambda": 5, "b_o": 6,
                 "ln_g": 7, "ln_b": 8}
_SMALL_NAMES = ["ln_emb_g", "ln_emb_b", "conv_b", "b_ra", "b_ri", "lru_lambda", "b_o", "ln_g", "ln_b",
                "sinks", "b_in", "meta_tokens", "conv_w"]


def _small_update(me_idx, early, late, wmv):
    n_fixed = 7

    def in_order(me, own_ref, land_ref):
        acc = None
        for e in range(N_DEV):
            term = jnp.where(me == e, own_ref[...], land_ref[e])
            acc = term if acc is None else acc + term
        return acc

    def body(*refs):
        me_ref, own_ref, land_ref, cown_ref, cland_ref, late_ref, meta_ref = refs[:n_fixed]
        ins = refs[n_fixed:n_fixed + 3 * len(_SMALL_NAMES)]
        outs = refs[n_fixed + 3 * len(_SMALL_NAMES):]
        me = me_ref[0]
        sm = in_order(me, own_ref, land_ref)
        conv = in_order(me, cown_ref, cland_ref)

        def grad_of(name):
            if name in ("ln_emb_g", "ln_emb_b"):
                r = _SMALL_ROW_OF[name]
                return late_ref[r:r + 1, :]
            if name in _SMALL_ROW_OF:
                r = _SMALL_ROW_OF[name]
                return sm[r:r + 1, :]
            if name == "sinks":
                return sm[9:10, 0:N_KV * GROUP]
            if name == "b_in":
                return jnp.concatenate([sm[16 + j:17 + j, :] for j in range(7)], axis=1)[:, :D_IN]
            if name == "meta_tokens":
                return meta_ref[...]
            return conv[0:CONV_WIDTH, :]

        for i, name in enumerate(_SMALL_NAMES):
            w_ref, m_ref, v_ref = ins[3 * i:3 * i + 3]
            g = grad_of(name)
            d, mn, vn = _adamw(w_ref[...], g, m_ref[...], v_ref[...])
            outs[4 * i][...] = g
            outs[4 * i + 1][...] = d
            outs[4 * i + 2][...] = mn
            outs[4 * i + 3][...] = vn
        outs[-1][...] = jnp.broadcast_to(jnp.sum(sm[10:11, :], axis=1, keepdims=True), (8, 128))

    args, out_shape = [me_idx, *early, *late], []
    for name in _SMALL_NAMES:
        args += list(wmv[name])
        out_shape += [SDS(wmv[name][0].shape, f32)] * 4
    out_shape.append(SDS((8, 128), f32))
    res = pl.pallas_call(
        body, out_shape=out_shape, in_specs=[pl.BlockSpec(memory_space=pltpu.SMEM)] + [_VMEM] * (len(args) - 1),
        name="small_update", compiler_params=_cp(None))(*args)
    return {name: tuple(res[4 * i:4 * i + 4]) for i, name in enumerate(_SMALL_NAMES)}, res[-1][0, 0]


_WEIGHTS = ["meta_tokens", "ln_emb_g", "ln_emb_b", "w_in", "b_in", "conv_w", "conv_b", "w_ra", "b_ra", "w_ri",
            "b_ri", "lru_lambda", "sinks", "w_rnn_out", "w_attn_out", "w_o", "b_o", "ln_g", "ln_b"]
_SMALL_2D = {"meta_tokens": (N_META, 256), "conv_w": (CONV_WIDTH, 256), "b_in": (1, D_IN), "sinks": (1, N_KV * GROUP)}


def kernel(x, meta_tokens, ln_emb_g, ln_emb_b, w_in, b_in, conv_w, conv_b, w_ra, b_ra, w_ri, b_ri, lru_lambda, sinks, w_rnn_out, w_attn_out, w_o, b_o, ln_g, ln_b, loss_target, m_meta_tokens, m_ln_emb_g, m_ln_emb_b, m_w_in, m_b_in, m_conv_w, m_conv_b, m_w_ra, m_b_ra, m_w_ri, m_b_ri, m_lru_lambda, m_sinks, m_w_rnn_out, m_w_attn_out, m_w_o, m_b_o, m_ln_g, m_ln_b, v_meta_tokens, v_ln_emb_g, v_ln_emb_b, v_w_in, v_b_in, v_conv_w, v_conv_b, v_w_ra, v_b_ra, v_w_ri, v_b_ri, v_lru_lambda, v_sinks, v_w_rnn_out, v_w_attn_out, v_w_o, v_b_o, v_ln_g, v_ln_b):
    w = dict(meta_tokens=meta_tokens, ln_emb_g=ln_emb_g, ln_emb_b=ln_emb_b, w_in=w_in, b_in=b_in, conv_w=conv_w,
             conv_b=conv_b, w_ra=w_ra, b_ra=b_ra, w_ri=w_ri, b_ri=b_ri, lru_lambda=lru_lambda, sinks=sinks,
             w_rnn_out=w_rnn_out, w_attn_out=w_attn_out, w_o=w_o, b_o=b_o, ln_g=ln_g, ln_b=ln_b)
    m = dict(meta_tokens=m_meta_tokens, ln_emb_g=m_ln_emb_g, ln_emb_b=m_ln_emb_b, w_in=m_w_in, b_in=m_b_in,
             conv_w=m_conv_w, conv_b=m_conv_b, w_ra=m_w_ra, b_ra=m_b_ra, w_ri=m_w_ri, b_ri=m_b_ri,
             lru_lambda=m_lru_lambda, sinks=m_sinks, w_rnn_out=m_w_rnn_out, w_attn_out=m_w_attn_out, w_o=m_w_o,
             b_o=m_b_o, ln_g=m_ln_g, ln_b=m_ln_b)
    v = dict(meta_tokens=v_meta_tokens, ln_emb_g=v_ln_emb_g, ln_emb_b=v_ln_emb_b, w_in=v_w_in, b_in=v_b_in,
             conv_w=v_conv_w, conv_b=v_conv_b, w_ra=v_w_ra, b_ra=v_b_ra, w_ri=v_w_ri, b_ri=v_b_ri,
             lru_lambda=v_lru_lambda, sinks=v_sinks, w_rnn_out=v_w_rnn_out, w_attn_out=v_w_attn_out, w_o=v_w_o,
             b_o=v_b_o, ln_g=v_ln_g, ln_b=v_ln_b)
    px, py, pc = _place()
    as_idx = lambda t: jnp.reshape(t, (1,)).astype(jnp.int32)
    c_idx, q_idx, me_idx = as_idx(pc), as_idx(2 * px + py), as_idx(_dev(px, py, pc))

    w3_s, wrg_s, small_s = _cast_small(w_rnn_out, w_attn_out, w_o, w_ra, w_ri, meta_tokens, conv_w)
    vec = lambda name: w[name].reshape(1, -1)
    p = {k: vec(k) for k in ("ln_emb_g", "ln_emb_b", "b_in", "conv_b", "b_ra", "b_ri", "lru_lambda", "sinks",
                             "b_o", "ln_g", "ln_b")}
    w_in_t = lambda a: jnp.swapaxes(a, 1, 2).reshape(SHARD_IN, D)
    wg, wrg, smallw, w3_land = _all_gather([_cast_w_in(w_in_t(w_in)), wrg_s, small_s], [w3_s])
    w3_pending = _split_start(_copies_direct(True), 7, [w3_s], [w3_land], smallw, "gather_w3_start")
    w_full = wg.reshape(D_IN, D)

    zero = w3_pending[4][0:1, 0:1]
    h32, hb = _ln_emb(x, smallw, p["ln_emb_g"], p["ln_emb_b"])
    z = _mm(hb, w_full, nt=True, bias=p["b_in"] + zero, name="mm_z")
    s = _step_attn(_step_rnn(h32, hb, z, wrg, smallw, p, zero), p, zero)
    w3 = _split_wait(_copies_direct(True), *w3_pending[:4], s["lse"], "gather_w3_wait")[1][0]
    t = _step_merge(s, loss_target, w3, p)

    big = {}
    two_d = lambda name: (w[name].shape[-2], w[name].shape[-1])
    proj = ("w_o", "w_rnn_out", "w_attn_out")
    g_proj = [t[k].reshape(N_DEV, 256, D) for k in ("g_wo", "g_wrnn", "g_wattn")]
    g_pending = _split_start(_copies_direct(False), 7, g_proj, [lax.empty((N_DEV, 256, D), bf16) for _ in proj],
                             p["b_o"], "reduce_proj_start")
    u = _step_backward(s, t, wrg, smallw, p, p["conv_b"] + g_pending[4][0:1, 0:1])

    def siblings_start(gs, dep, tag):
        return _split_start(_copies_siblings, 4, gs, [lax.empty((4, *g.shape[1:]), bf16) for g in gs], dep,
                            "reduce_siblings_start_" + tag)

    def chips_start(gs, r1, dep, tag):
        parts = [_pair_sum(g, r, c_idx, "pair_sum_%s%d" % (tag, i)) for i, (g, r) in enumerate(zip(gs, r1))]
        return _split_start(_copies_chips, 3, parts, [lax.empty((3, *q.shape[1:]), bf16) for q in parts], dep,
                            "reduce_chips_start_" + tag)

    dz = u["dz_parts"]
    g_a, db_in = _mm_dwin(s["hb"], dz, 0, p["b_o"])
    shards = lambda g: g.reshape(N_DEV, SHARD_IN, W_IN_HALF)
    sib_a = siblings_start([shards(g_a), u["g_wrg"].reshape(N_DEV, 2 * RNN_BLOCK, RNN_BLOCK)], db_in, "a")
    g_proj, g_land = _split_wait(_copies_direct(False), *g_pending[:4], sib_a[4], "reduce_proj_wait")
    for i, name in enumerate(proj):
        res = _adamw_direct(g_proj[i], g_land[i], me_idx, w[name].reshape(two_d(name)), m[name].reshape(two_d(name)),
                            v[name].reshape(two_d(name)), "adamw_" + name)
        big[name] = tuple(r.reshape(w[name].shape) for r in res)
    chp_a = chips_start(*_split_wait(_copies_siblings, *sib_a[:4], big["w_attn_out"][3], "reduce_siblings_wait_a"),
                        db_in, "a")
    g_b, = _mm_dwin(s["hb"], dz, 1, chp_a[4])
    sib_b = siblings_start([shards(g_b)], db_in, "b")
    sm_e = _pack_early(u["vec_rnn"], t["st_out"], u["dsr"], db_in)
    early = _split_start(_copies_direct(True), 7, list(sm_e),
                         [lax.empty((N_DEV, *a.shape), f32) for a in sm_e], sib_b[4], "small_early_start")
    dh_lo = _mm_dh(dz, w_full, early[4], 0)
    chp_b = chips_start(*_split_wait(_copies_siblings, *sib_b[:4], dh_lo, "reduce_siblings_wait_b"), db_in, "b")
    dh_hi = _mm_dh(dz, w_full, chp_b[4], 1)
    parts_a, r2_a = _split_wait(_copies_chips, *chp_a[:4], dh_hi, "reduce_chips_wait_a")
    w_in_res = _adamw_big(parts_a[0], r2_a[0], q_idx, w_in_t(w["w_in"]), w_in_t(m["w_in"]), w_in_t(v["w_in"]),
                          "adamw_w_in_a", cols=(0, 2))
    u.update(_step_input_grad(dh_lo, dh_hi, t["du32"], x, smallw, p, w_in_res[3]))
    sm_l, meta_l = _small_allreduce(*_pack_late(u["st_emb"], u["dmeta"]))
    (sm_own, conv_own), (sm_land, conv_land) = _split_wait(_copies_direct(True), *early[:4], sm_l, "small_early_wait")
    me = _dev(px, py, pc)
    mine = lambda a, axis: lax.dynamic_index_in_dim(a, me, axis, keepdims=False)
    two = lambda name, t: t.reshape(_SMALL_2D.get(name, (1, D)))
    small, loss = _small_update(me_idx, (sm_own, sm_land, mine(conv_own, 0), mine(conv_land, 1)),
                                (sm_l, mine(meta_l, 0)),
                                {k: (two(k, w[k]), two(k, m[k]), two(k, v[k])) for k in _SMALL_NAMES})

    parts_b, r2_b = _split_wait(_copies_chips, *chp_b[:4], small["b_in"][2], "reduce_chips_wait_b")
    res = _adamw_big(parts_b[0], r2_b[0], q_idx, w_in_t(w["w_in"]), w_in_t(m["w_in"]), w_in_t(v["w_in"]),
                     "adamw_w_in_b", cols=(1, 2), prev=w_in_res)
    big["w_in"] = tuple(jnp.swapaxes(r.reshape(1, SHARD_IN, D), 1, 2) for r in res)
    for i, name in enumerate(("w_ra", "w_ri")):
        sq = (RNN_BLOCK, RNN_BLOCK)
        res = _adamw_big(parts_a[1], r2_a[1], q_idx, w[name].reshape(sq), m[name].reshape(sq), v[name].reshape(sq),
                         "adamw_" + name, row_off=i)
        big[name] = tuple(r.reshape(w[name].shape) for r in res)
    res = dict(big)
    for k in _SMALL_NAMES:
        res[k] = tuple(t.reshape(w[k].shape) for t in small[k])

    outs = [loss, u["grad_x"]]
    for j in range(4):
        outs += [res[k][j] for k in _WEIGHTS]
    return tuple(outs)
```

```python
import jax
import jax.numpy as jnp
from jax import lax
from jax.experimental import pallas as pl
from jax.experimental.pallas import tpu as pltpu

f32, bf16 = jnp.float32, jnp.bfloat16
SDS = jax.ShapeDtypeStruct

N_DEV = 8
D = 2048
N_META = 16
BLK = 128
ROW0 = BLK - N_META
N_RNN_BLOCKS = 8
RNN_BLOCK = D // N_RNN_BLOCKS
CONV_WIDTH = 4
LRU_C = 8.0
HEAD_DIM = 64
N_KV = 4
GROUP = 8
HALF = HEAD_DIM // 2
ROPE_THETA = 10000.0
NEG_INF = -1e30
LN_EPS = 1e-5
ALPHA = 2.0 ** 0.25
D_IN = 12800
SHARD_IN = D_IN // N_DEV
W_IN_HALF = D // 2
OFF_GR, OFF_Q, OFF_K, OFF_V, OFF_GA, OFF_G = 2048, 4096, 6144, 6400, 6656, 8704
ADAM_LR, ADAM_B1, ADAM_B2, ADAM_EPS, ADAM_WD, ADAM_STEP = 1e-3, 0.9, 0.999, 1e-8, 0.01, 10
VMEM_LIMIT_MB = 56
MESH = pl.DeviceIdType.MESH


def _cp(sem=None, vmem_mb=40):
    return pltpu.CompilerParams(dimension_semantics=sem, vmem_limit_bytes=vmem_mb * 2 ** 20)


def _row_chunk(m):
    best = 16
    for c in range(16, 641, 16):
        if m % c == 0:
            best = c
    return best


def _sigmoid(x):
    return 1.0 / (1.0 + jnp.exp(-x))


def _silu_and_grad(x):
    s = _sigmoid(x)
    return x * s, s * (1.0 + x * (1.0 - s))


def _log_sigmoid(x):
    return jnp.minimum(x, 0.0) - jnp.log1p(jnp.exp(-jnp.abs(x)))


def _ln_rows(v, g, b):
    mu = jnp.mean(v, axis=-1, keepdims=True)
    c = v - mu
    var = jnp.mean(c * c, axis=-1, keepdims=True)
    rstd = lax.rsqrt(var + LN_EPS)
    xhat = c * rstd
    return xhat * g + b, xhat, rstd


def _ln_rows_bwd(dy, g, xhat, rstd):
    dxh = dy * g
    m1 = jnp.mean(dxh, axis=-1, keepdims=True)
    m2 = jnp.mean(dxh * xhat, axis=-1, keepdims=True)
    return rstd * (dxh - m1 - xhat * m2)


def _colsum(v):
    return jnp.sum(v, axis=0, keepdims=True)


def _dot(a, b):
    return jnp.dot(a, b, preferred_element_type=f32)


def _dot_nt(a, b):
    return lax.dot_general(a, b, (((1,), (1,)), ((), ())), preferred_element_type=f32)


def _dot_tn(a, b):
    return lax.dot_general(a, b, (((0,), (0,)), ((), ())), preferred_element_type=f32)


def _meta_full(sw_ref):
    return jnp.concatenate([sw_ref[s, 0:N_META, :] for s in range(N_DEV)], axis=1)


def _ln_emb(x, smallw, g_e, b_e):
    seq = x.shape[1]
    rows = seq + BLK
    nb = rows // BLK

    def body(x_ref, sw_ref, g_ref, b_ref, h32_ref, hb_ref):
        i = pl.program_id(0)
        g, b = g_ref[...], b_ref[...]

        def emit(blk):
            h32_ref[...] = blk
            hb_ref[...] = blk.astype(bf16)

        @pl.when(i == 0)
        def _():
            hm = _ln_rows(_meta_full(sw_ref), g, b)[0]
            emit(jnp.concatenate([jnp.zeros((ROW0, D), f32), hm], axis=0))

        @pl.when(i > 0)
        def _():
            emit(_ln_rows(x_ref[0], g, b)[0])

    return pl.pallas_call(
        body, grid=(nb,),
        in_specs=[pl.BlockSpec((1, BLK, D), lambda i: (0, jnp.maximum(i - 1, 0), 0)),
                  pl.BlockSpec((N_DEV, 24, 256), lambda i: (0, 0, 0)),
                  pl.BlockSpec((1, D), lambda i: (0, 0)),
                  pl.BlockSpec((1, D), lambda i: (0, 0))],
        out_specs=[pl.BlockSpec((BLK, D), lambda i: (i, 0)),
                   pl.BlockSpec((BLK, D), lambda i: (i, 0))],
        out_shape=[SDS((rows, D), f32), SDS((rows, D), bf16)],
        name="ln_emb", compiler_params=_cp(("arbitrary",)),
    )(x, smallw, g_e, b_e)


def _ln_emb_bwd(dh_lo, dh_hi, du32, x, smallw, g_e, after):
    seq = x.shape[1]
    rows = seq + BLK
    nb = rows // BLK

    def body(dlo_ref, dhi_ref, du_ref, x_ref, sw_ref, g_ref, after_ref, gx_ref, dmeta_ref, st_ref):
        i = pl.program_id(0)
        g = g_ref[...]
        dht = jnp.concatenate([dlo_ref[...], dhi_ref[...]], axis=1) + ALPHA * du_ref[...]

        @pl.when(i == 0)
        def _():
            v = jnp.concatenate([jnp.zeros((ROW0, D), f32), _meta_full(sw_ref)], axis=0)
            valid = lax.broadcasted_iota(jnp.int32, (BLK, 1), 0) >= ROW0
            d = jnp.where(valid, dht, 0.0)
            _, xhat, rstd = _ln_rows(v, g, 0.0)
            dv = _ln_rows_bwd(d, g, xhat, rstd)
            dmeta_ref[...] = dv[ROW0:, :]
            st_ref[...] = jnp.concatenate([_colsum(d * xhat), _colsum(d), jnp.zeros((6, D), f32)], axis=0)

        @pl.when(i > 0)
        def _():
            _, xhat, rstd = _ln_rows(x_ref[0], g, 0.0)
            gx_ref[0] = _ln_rows_bwd(dht, g, xhat, rstd)
            st_ref[0:1, :] += _colsum(dht * xhat)
            st_ref[1:2, :] += _colsum(dht)

    return pl.pallas_call(
        body, grid=(nb,),
        in_specs=[pl.BlockSpec((BLK, W_IN_HALF), lambda i: (i, 0)),
                  pl.BlockSpec((BLK, W_IN_HALF), lambda i: (i, 0)),
                  pl.BlockSpec((BLK, D), lambda i: (i, 0)),
                  pl.BlockSpec((1, BLK, D), lambda i: (0, jnp.maximum(i - 1, 0), 0)),
                  pl.BlockSpec((N_DEV, 24, 256), lambda i: (0, 0, 0)),
                  pl.BlockSpec((1, D), lambda i: (0, 0)),
                  pl.BlockSpec(memory_space=pl.ANY)],
        out_specs=[pl.BlockSpec((1, BLK, D), lambda i: (0, jnp.maximum(i - 1, 0), 0)),
                   pl.BlockSpec((N_META, D), lambda i: (0, 0)),
                   pl.BlockSpec((8, D), lambda i: (0, 0))],
        out_shape=[SDS((1, seq, D), f32), SDS((N_META, D), f32), SDS((8, D), f32)],
        name="ln_emb_bwd", compiler_params=_cp(("arbitrary",)),
    )(dh_lo, dh_hi, du32, x, smallw, g_e, after)


def _mm(a, b, *, name, nt=False, sel=None, bias=None, out_dtype=f32, tn=512):
    m, k = a.shape
    cm = _row_chunk(m)
    stacked = sel is not None
    n = D if stacked else (b.shape[0] if nt else b.shape[1])
    am = m
    if stacked and nt:
        b_spec = pl.BlockSpec((tn // 256, None, 256, D), lambda j, i: (j, sel, 0, 0))
    elif stacked:
        b_spec = pl.BlockSpec((N_DEV, None, 256, tn), lambda j, i: (0, sel, 0, j))
    elif nt:
        b_spec = pl.BlockSpec((tn, k), lambda j, i: (j, 0))
    else:
        b_spec = pl.BlockSpec((k, tn), lambda j, i: (0, j))
    in_specs = [pl.BlockSpec((am, k), lambda j, i: (i, 0)), b_spec]
    args = [a, b]
    if bias is not None:
        in_specs.append(pl.BlockSpec((1, tn), lambda j, i: (0, j)))
        args.append(bias)

    def body(*refs):
        a_ref, b_ref, o_ref = refs[0], refs[1], refs[-1]
        bm = b_ref[...]
        if stacked:
            bm = bm.reshape((tn, D) if nt else (D, tn))
        for c in range(am // cm):
            acc = (_dot_nt if nt else _dot)(a_ref[c * cm:(c + 1) * cm, :], bm)
            if bias is not None:
                acc = acc + refs[2][...]
            o_ref[c * cm:(c + 1) * cm, :] = acc.astype(out_dtype)

    return pl.pallas_call(
        body, grid=(n // tn, m // am), in_specs=in_specs,
        out_specs=pl.BlockSpec((am, tn), lambda j, i: (i, j)),
        out_shape=SDS((m, n), out_dtype), name=name, compiler_params=_cp(("arbitrary", "arbitrary"), 48),
    )(*args)


def _mm_dh(parts, w_t, after, half):
    rows = parts[0][0].shape[0]
    tn = 512
    nt = W_IN_HALF // tn
    cm = _row_chunk(rows) // 2

    def body(*refs):
        w_ref, o_ref = refs[len(parts)], refs[-1]
        a = jnp.concatenate([r[...] for r in refs[:len(parts)]], axis=1)
        o_ref[...] = _dot(a, w_ref[...])

    return pl.pallas_call(
        body, grid=(nt, rows // cm),
        in_specs=[pl.BlockSpec((cm, w), lambda j, i: (i, 0)) for _, w in parts]
                 + [pl.BlockSpec((D_IN, tn), lambda j, i: (0, half * nt + j)),
                    pl.BlockSpec(memory_space=pl.ANY)],
        out_specs=pl.BlockSpec((cm, tn), lambda j, i: (i, j)),
        out_shape=SDS((rows, W_IN_HALF), f32), name="mm_dh_%d" % half,
        compiler_params=_cp(("arbitrary", "arbitrary"), VMEM_LIMIT_MB),
    )(*[a for a, _ in parts], w_t, after)


def _mm_dwin(hb, parts_t, half, after):
    rows = hb.shape[0]
    tc = 512
    with_db = half == 0
    edges = [0]
    for _, w in parts_t:
        edges.append(edges[-1] + w // tc)

    def body(*refs):
        h_ref, o_ref = refs[len(parts_t)], refs[len(parts_t) + 2]
        j = pl.program_id(0)
        for p_ref, lo, hi in zip(refs, edges[:-1], edges[1:]):
            @pl.when((j >= lo) & (j < hi))
            def _():
                o_ref[...] = _dot(p_ref[...], h_ref[...]).astype(bf16)
                if with_db:
                    s = jnp.sum(p_ref[...].astype(f32), axis=1, keepdims=True)
                    refs[-1][...] = jnp.broadcast_to(s, (tc, 8))

    in_specs = [pl.BlockSpec((tc, rows), lambda j, lo=lo, hi=hi: (jnp.clip(j - lo, 0, hi - lo - 1), 0))
                for lo, hi in zip(edges[:-1], edges[1:])]
    out_specs = [pl.BlockSpec((tc, W_IN_HALF), lambda j: (j, 0))]
    out_shape = [SDS((D_IN, W_IN_HALF), bf16)]
    if with_db:
        out_specs.append(pl.BlockSpec((tc, 8), lambda j: (j, 0)))
        out_shape.append(SDS((D_IN, 8), f32))
    return pl.pallas_call(
        body, grid=(D_IN // tc,),
        in_specs=in_specs + [pl.BlockSpec((rows, W_IN_HALF), lambda j: (0, half)),
                             pl.BlockSpec(memory_space=pl.ANY)],
        out_specs=out_specs, out_shape=out_shape,
        name="mm_dwin_%d" % half, compiler_params=_cp(("arbitrary",), VMEM_LIMIT_MB),
    )(*[a for a, _ in parts_t], hb, after)


SCAN_ROWS = 32


def _scan8(a, b, reverse):
    idx = lax.broadcasted_iota(jnp.int32, a.shape, 0)
    for s in (1, 2, 4):
        sh = 8 - s if reverse else s
        a_sh, b_sh = pltpu.roll(a, sh, 0), pltpu.roll(b, sh, 0)
        m = (idx < 8 - s) if reverse else (idx >= s)
        b = jnp.where(m, a * b_sh + b, b)
        a = jnp.where(m, a * a_sh, a)
    return a, b


def _shift_rows(prev8, cur, k):
    ext = jnp.concatenate([prev8, cur], axis=0)
    return pltpu.roll(ext, k, 0)[8:, :]


def _gates(xc, w_ra, b_ra, w_ri, b_ri, ls):
    xb = xc.astype(bf16)
    r = _sigmoid(_dot(xb, w_ra) + b_ra)
    ig = _sigmoid(_dot(xb, w_ri) + b_ri)
    la = LRU_C * r * ls
    a = jnp.exp(la)
    mult = jnp.sqrt(jnp.tanh(-la) * (1.0 + a * a))
    return xb, r, ig, a, mult


_RNN_IN_SPECS = lambda rows: [
    pl.BlockSpec((1, 24, 256), lambda n: (n, 0, 0)),
    pl.BlockSpec((1, RNN_BLOCK), lambda n: (0, n)),
    pl.BlockSpec((N_DEV, 2, None, 32, RNN_BLOCK), lambda n: (0, 0, n, 0, 0)),
    pl.BlockSpec((1, RNN_BLOCK), lambda n: (0, n)),
    pl.BlockSpec((1, RNN_BLOCK), lambda n: (0, n)),
    pl.BlockSpec((1, RNN_BLOCK), lambda n: (0, n)),
]


def _rnn_fwd(z, smallw, conv_b, wrg, b_ra, b_ri, lam):
    rows = z.shape[0]
    nb = rows // BLK
    col = lambda off: pl.BlockSpec((rows, RNN_BLOCK), lambda n: (0, off // RNN_BLOCK + n))

    def body(xr_ref, gr_ref, sw_ref, cb_ref, w_ref, bra_ref, bri_ref, lam_ref, xc_ref, hr_ref, ya_ref, yat_ref, a_s):
        cw = sw_ref[0, N_META:24, :]
        cb = cb_ref[...]
        w_ra = w_ref[:, 0].reshape(RNN_BLOCK, RNN_BLOCK)
        w_ri = w_ref[:, 1].reshape(RNN_BLOCK, RNN_BLOCK)
        b_ra_v, b_ri_v = bra_ref[...], bri_ref[...]
        ls = _log_sigmoid(lam_ref[...])
        rid = lax.broadcasted_iota(jnp.int32, (BLK, 1), 0)

        def blk_step(i, carry):
            r0 = pl.multiple_of(i * BLK, BLK)
            grow = rid + r0
            valid = grow >= ROW0
            cur = jnp.where(valid, xr_ref[pl.ds(r0, BLK), :], 0.0)
            prev8 = xr_ref[pl.ds(pl.multiple_of(jnp.maximum(r0 - 8, 0), 8), 8), :] * (i > 0).astype(f32)
            xc = cb + cw[0:1] * cur
            for k in range(1, CONV_WIDTH):
                xc = xc + cw[k:k + 1] * _shift_rows(prev8, cur, k)
            xc_ref[pl.ds(r0, BLK), :] = xc
            _, _, ig, a, mult = _gates(xc, w_ra, b_ra_v, w_ri, b_ri_v, ls)
            mult = jnp.where(grow == ROW0, 1.0, mult)
            a_s[pl.ds(r0, BLK), :] = a
            hr_ref[pl.ds(r0, BLK), :] = jnp.where(valid, mult * ig * xc, 0.0)
            return carry

        lax.fori_loop(0, nb, blk_step, 0)

        def scan_step(j, carry):
            r0 = pl.multiple_of(j * SCAN_ROWS, SCAN_ROWS)
            tiles = [_scan8(a_s[pl.ds(r0 + 8 * k, 8), :], hr_ref[pl.ds(r0 + 8 * k, 8), :], False)
                     for k in range(SCAN_ROWS // 8)]
            for k, (a, b) in enumerate(tiles):
                h = b + a * carry
                hr_ref[pl.ds(r0 + 8 * k, 8), :] = h
                carry = jnp.broadcast_to(h[7:8, :], (8, RNN_BLOCK))
            return carry

        lax.fori_loop(0, rows // SCAN_ROWS, scan_step, jnp.zeros((8, RNN_BLOCK), f32))

        def gate_step(i, carry):
            r0 = pl.multiple_of(i * BLK, BLK)
            ya_ref[pl.ds(r0, BLK), :] = (hr_ref[pl.ds(r0, BLK), :]
                                         * _silu_and_grad(gr_ref[pl.ds(r0, BLK), :])[0]).astype(bf16)
            return carry

        lax.fori_loop(0, nb, gate_step, 0)
        yat_ref[...] = ya_ref[...].astype(f32).T.astype(bf16)

    return pl.pallas_call(
        body, grid=(N_RNN_BLOCKS,),
        in_specs=[col(0), col(OFF_GR)] + _RNN_IN_SPECS(rows),
        out_specs=[pl.BlockSpec((rows, RNN_BLOCK), lambda n: (0, n))] * 3
                  + [pl.BlockSpec((RNN_BLOCK, rows), lambda n: (n, 0))],
        out_shape=[SDS((rows, D), f32), SDS((rows, D), f32), SDS((rows, D), bf16), SDS((D, rows), bf16)],
        scratch_shapes=[pltpu.VMEM((rows, RNN_BLOCK), f32)],
        name="rnn_fwd", compiler_params=_cp(("arbitrary",)),
    )(z, z, smallw, conv_b, wrg, b_ra, b_ri, lam)


def _rnn_bwd(dya, hr, xc, z, smallw, conv_b, wrg, b_ra, b_ri, lam):
    rows = z.shape[0]
    nb = rows // BLK
    col = lambda off: pl.BlockSpec((rows, RNN_BLOCK), lambda n: (0, off // RNN_BLOCK + n))
    blk = pl.BlockSpec((rows, RNN_BLOCK), lambda n: (0, n))

    def body(dya_ref, hr_ref, xc_ref, xr_ref, gr_ref, sw_ref, cb_ref, w_ref, bra_ref, bri_ref, lam_ref,
             dxr_ref, dgr_ref, dw_ref, vec_ref, dxrt_ref, dgrt_ref, a_s, lam_s, dxc_s, r_s, ig_s, mult_s, dw_s):
        cw = sw_ref[0, N_META:24, :]
        w_ra = w_ref[:, 0].reshape(RNN_BLOCK, RNN_BLOCK)
        w_ri = w_ref[:, 1].reshape(RNN_BLOCK, RNN_BLOCK)
        b_ra_v, b_ri_v = bra_ref[...], bri_ref[...]
        lam_v = lam_ref[...]
        ls = _log_sigmoid(lam_v)
        rid = lax.broadcasted_iota(jnp.int32, (BLK, 1), 0)
        zrow = jnp.zeros((1, RNN_BLOCK), f32)

        def p1(i, carry):
            r0 = pl.multiple_of(i * BLK, BLK)
            sl = pl.ds(r0, BLK)
            _, r, ig, a, mult = _gates(xc_ref[sl, :], w_ra, b_ra_v, w_ri, b_ri_v, ls)
            a_s[sl, :] = a
            r_s[sl, :] = r
            ig_s[sl, :] = ig
            mult_s[sl, :] = mult
            sg, dsg = _silu_and_grad(gr_ref[sl, :])
            d = dya_ref[sl, :]
            lam_s[sl, :] = d * sg
            dgr_ref[sl, :] = (d * hr_ref[sl, :] * dsg).astype(bf16)
            return carry

        lax.fori_loop(0, nb, p1, 0)

        def p2(jj, carry):
            r0 = pl.multiple_of((rows // SCAN_ROWS - 1 - jj) * SCAN_ROWS, SCAN_ROWS)
            idx = lax.broadcasted_iota(jnp.int32, (8, RNN_BLOCK), 0)
            tiles = []
            for k in range(SCAN_ROWS // 8):
                sl = pl.ds(r0 + 8 * k, 8)
                a, g = a_s[sl, :], lam_s[sl, :]
                tiles.append((g, *_scan8(a, a * g, True)))
            for k in reversed(range(SCAN_ROWS // 8)):
                g, ca, cb_ = tiles[k]
                mu = cb_ + ca * carry
                lam_s[pl.ds(r0 + 8 * k, 8), :] = g + jnp.where(idx < 7, pltpu.roll(mu, 7, 0), carry)
                carry = jnp.broadcast_to(mu[0:1, :], (8, RNN_BLOCK))
            return carry

        lax.fori_loop(0, rows // SCAN_ROWS, p2, jnp.zeros((8, RNN_BLOCK), f32))

        dw_s[...] = jnp.zeros_like(dw_s)

        def p3(i, carry):
            d_bra, d_bri, d_ls = carry
            r0 = pl.multiple_of(i * BLK, BLK)
            sl = pl.ds(r0, BLK)
            grow = rid + r0
            valid = grow >= ROW0
            first = grow == ROW0
            xcv = xc_ref[sl, :]
            xb = xcv.astype(bf16)
            r, ig, a = r_s[sl, :], ig_s[sl, :], a_s[sl, :]
            mult = jnp.where(first, 1.0, mult_s[sl, :])
            lam_t = lam_s[sl, :]
            du = jnp.where(valid, lam_t, 0.0)
            hprev = _shift_rows(hr_ref[pl.ds(pl.multiple_of(jnp.maximum(r0 - 8, 0), 8), 8), :] * (i > 0).astype(f32), hr_ref[sl, :], 1)
            da = lam_t * hprev
            dmult = jnp.where(first, 0.0, du * ig * xcv)
            di = du * mult * xcv
            dxc = du * mult * ig
            ratio = jnp.where(valid & jnp.logical_not(first), a * a / mult, 0.0)
            dla = da * a - dmult * ratio
            dpr = (dla * (LRU_C * ls)) * r * (1.0 - r)
            dpi = di * ig * (1.0 - ig)
            dprb, dpib = dpr.astype(bf16), dpi.astype(bf16)
            dw_s[0] += _dot_tn(xb, dprb)
            dw_s[1] += _dot_tn(xb, dpib)
            dxc_s[sl, :] = dxc + _dot_nt(dprb, w_ra) + _dot_nt(dpib, w_ri)
            return d_bra + _colsum(dpr), d_bri + _colsum(dpi), d_ls + _colsum(dla * (LRU_C * r))

        d_bra, d_bri, d_ls = lax.fori_loop(0, nb, p3, (zrow, zrow, zrow))

        def p4(i, carry):
            d_cb, d_w0, d_w1, d_w2, d_w3 = carry
            r0 = pl.multiple_of(i * BLK, BLK)
            sl = pl.ds(r0, BLK)
            grow = rid + r0
            valid = grow >= ROW0
            dxc = dxc_s[sl, :]
            nxt = dxc_s[pl.ds(pl.multiple_of(jnp.minimum(r0 + BLK, rows - 8), 8), 8), :] * (i < nb - 1).astype(f32)
            ext = jnp.concatenate([dxc, nxt], axis=0)
            dxr = cw[0:1] * dxc
            for k in range(1, CONV_WIDTH):
                dxr = dxr + cw[k:k + 1] * pltpu.roll(ext, BLK + 8 - k, 0)[:BLK, :]
            dxr_ref[sl, :] = jnp.where(valid, dxr, 0.0).astype(bf16)
            cur = jnp.where(valid, xr_ref[sl, :], 0.0)
            prev8 = xr_ref[pl.ds(pl.multiple_of(jnp.maximum(r0 - 8, 0), 8), 8), :] * (i > 0).astype(f32)
            dws = [d_w0 + _colsum(dxc * cur)]
            for k, acc in ((1, d_w1), (2, d_w2), (3, d_w3)):
                dws.append(acc + _colsum(dxc * _shift_rows(prev8, cur, k)))
            return (d_cb + _colsum(dxc), *dws)

        d_cb, d_w0, d_w1, d_w2, d_w3 = lax.fori_loop(0, nb, p4, (zrow,) * 5)

        d_lam = d_ls * _sigmoid(-lam_v)
        vec_ref[...] = jnp.concatenate([d_bra, d_bri, d_lam, d_cb, d_w0, d_w1, d_w2, d_w3], axis=0)
        dw_ref[:, 0] = dw_s[0].astype(bf16).reshape(N_DEV, 32, RNN_BLOCK)
        dw_ref[:, 1] = dw_s[1].astype(bf16).reshape(N_DEV, 32, RNN_BLOCK)
        dxrt_ref[...] = dxr_ref[...].astype(f32).T.astype(bf16)
        dgrt_ref[...] = dgr_ref[...].astype(f32).T.astype(bf16)

    blk_t = pl.BlockSpec((RNN_BLOCK, rows), lambda n: (n, 0))
    return pl.pallas_call(
        body, grid=(N_RNN_BLOCKS,),
        in_specs=[blk, blk, blk, col(0), col(OFF_GR)] + _RNN_IN_SPECS(rows),
        out_specs=[blk, blk,
                   pl.BlockSpec((N_DEV, 2, None, 32, RNN_BLOCK), lambda n: (0, 0, n, 0, 0)),
                   pl.BlockSpec((8, RNN_BLOCK), lambda n: (0, n)), blk_t, blk_t],
        out_shape=[SDS((rows, D), bf16), SDS((rows, D), bf16),
                   SDS((N_DEV, 2, N_RNN_BLOCKS, 32, RNN_BLOCK), bf16), SDS((8, D), f32),
                   SDS((D, rows), bf16), SDS((D, rows), bf16)],
        scratch_shapes=[pltpu.VMEM((rows, RNN_BLOCK), f32)] * 6 + [pltpu.VMEM((2, RNN_BLOCK, RNN_BLOCK), f32)],
        name="rnn_bwd", compiler_params=_cp(("arbitrary",), 48),
    )(dya, hr, xc, z, z, smallw, conv_b, wrg, b_ra, b_ri, lam)


def _rope_tables(rows):
    half = jnp.arange(HALF, dtype=f32)
    inv = ROPE_THETA ** (-half / HALF)
    pos = (jnp.arange(rows) - ROW0).astype(f32)
    ang = pos[:, None] * inv[None, :]
    cos, sin = jnp.cos(ang), jnp.sin(ang)
    cos128 = jnp.concatenate([cos, cos, cos, cos], axis=1)
    sin128 = jnp.concatenate([-sin, sin, -sin, sin], axis=1)
    return cos128, sin128


def _rope128(x, cos128, sin128):
    lane = lax.broadcasted_iota(jnp.int32, x.shape, 1)
    swapped = jnp.where(lane % HEAD_DIM < HALF, pltpu.roll(x, 128 - HALF, 1), pltpu.roll(x, HALF, 1))
    return x * cos128 + swapped * sin128


def _qkv_prep(z, cos128, sin128):
    rows = z.shape[0]

    def body(q_ref, kv_ref, c_ref, s_ref, qo_ref, ko_ref, vo_ref):
        c, s = c_ref[...], s_ref[...]
        for g in range(D // 128):
            qo_ref[:, g * 128:(g + 1) * 128] = (_rope128(q_ref[:, g * 128:(g + 1) * 128], c, s)
                                                * (HEAD_DIM ** -0.5)).astype(bf16)
        for g in range(2):
            kr = _rope128(kv_ref[:, g * 128:(g + 1) * 128], c, s)
            for j in range(2):
                ko_ref[2 * g + j] = kr[:, j * HEAD_DIM:(j + 1) * HEAD_DIM].astype(bf16)
        for h in range(N_KV):
            vo_ref[h] = kv_ref[:, 256 + h * HEAD_DIM:256 + (h + 1) * HEAD_DIM].astype(bf16)

    return pl.pallas_call(
        body, grid=(rows // BLK,),
        in_specs=[pl.BlockSpec((BLK, D), lambda i: (i, OFF_Q // D)),
                  pl.BlockSpec((BLK, 512), lambda i: (i, OFF_K // 512)),
                  pl.BlockSpec((BLK, 128), lambda i: (i, 0)),
                  pl.BlockSpec((BLK, 128), lambda i: (i, 0))],
        out_specs=[pl.BlockSpec((BLK, D), lambda i: (i, 0)),
                   pl.BlockSpec((N_KV, BLK, HEAD_DIM), lambda i: (0, i, 0)),
                   pl.BlockSpec((N_KV, BLK, HEAD_DIM), lambda i: (0, i, 0))],
        out_shape=[SDS((rows, D), bf16), SDS((N_KV, rows, HEAD_DIM), bf16), SDS((N_KV, rows, HEAD_DIM), bf16)],
        name="qkv_prep", compiler_params=_cp(("arbitrary",)),
    )(z, z, cos128, sin128)


def _attn_mask(n):
    qi = n * BLK + lax.broadcasted_iota(jnp.int32, (BLK, 2 * BLK + N_META), 0)
    c = lax.broadcasted_iota(jnp.int32, (BLK, 2 * BLK + N_META), 1)
    jb = (n - 1) * BLK + c
    band = (jb >= BLK) & (jb <= qi) & (qi - jb < BLK)
    meta = (ROW0 + c - 2 * BLK) <= qi
    return ((c < 2 * BLK) & band) | ((c >= 2 * BLK) & meta)


N_KEYS = 2 * BLK + N_META


def _stack_heads(t):
    return jnp.concatenate([t[:, g * HEAD_DIM:(g + 1) * HEAD_DIM] for g in range(GROUP)], axis=0)


def _sink_column(sink_ref, h):
    g = lax.broadcasted_iota(jnp.int32, (GROUP, 1, 1), 0)
    col = jnp.zeros((GROUP, 1, 1), f32)
    for j in range(GROUP):
        col = jnp.where(g == j, sink_ref[h * GROUP + j], col)
    return col


def _kv_specs(last):
    cl = lambda n: jnp.minimum(n, last)
    return [pl.BlockSpec((None, N_META, HEAD_DIM), lambda h, n: (h, ROW0 // N_META, 0)),
            pl.BlockSpec((None, BLK, HEAD_DIM), lambda h, n: (h, jnp.maximum(cl(n) - 1, 0), 0)),
            pl.BlockSpec((None, BLK, HEAD_DIM), lambda h, n: (h, cl(n), 0))]


def _attn_fwd(q_r, k_r, v_b, z, sinks):
    rows = q_r.shape[0]
    nb = rows // BLK

    def body(sink_ref, q_ref, km_ref, kp_ref, kc_ref, vm_ref, vp_ref, vc_ref, ga_ref, o_ref, yb_ref, ybt_ref, lse_ref):
        h, n = pl.program_id(0), pl.program_id(1)
        kk = jnp.concatenate([kp_ref[...], kc_ref[...], km_ref[...]], axis=0)
        vv = jnp.concatenate([vp_ref[...], vc_ref[...], vm_ref[...]], axis=0)
        q2 = _stack_heads(q_ref[...])
        s = jnp.where(_attn_mask(n)[None], _dot_nt(q2, kk).reshape(GROUP, BLK, N_KEYS), NEG_INF)
        sink = _sink_column(sink_ref, h)
        m = jnp.maximum(jnp.max(s, axis=-1, keepdims=True), sink)
        p = jnp.exp(s - m)
        den = jnp.sum(p, axis=-1, keepdims=True) + jnp.exp(sink - m)
        o2 = _dot((p / den).astype(bf16).reshape(GROUP * BLK, N_KEYS), vv)
        lse = m + jnp.log(den)
        for g in range(GROUP):
            o_ref[:, g * HEAD_DIM:(g + 1) * HEAD_DIM] = o2[g * BLK:(g + 1) * BLK]
            lse_ref[:, g:g + 1] = lse[g]
        yb = o_ref[...] * _silu_and_grad(ga_ref[...])[0]
        yb_ref[...] = yb.astype(bf16)
        ybt_ref[...] = yb.T.astype(bf16)

    tile = pl.BlockSpec((BLK, 512), lambda h, n: (n, h))
    return pl.pallas_call(
        body, grid=(N_KV, nb),
        in_specs=[pl.BlockSpec(memory_space=pltpu.SMEM), tile] + _kv_specs(nb - 1) + _kv_specs(nb - 1)
                 + [pl.BlockSpec((BLK, 512), lambda h, n: (n, OFF_GA // 512 + h))],
        out_specs=[tile, tile, pl.BlockSpec((512, BLK), lambda h, n: (h, n)),
                   pl.BlockSpec((None, BLK, GROUP), lambda h, n: (h, n, 0))],
        out_shape=[SDS((rows, D), f32), SDS((rows, D), bf16), SDS((D, rows), bf16),
                   SDS((N_KV, rows, GROUP), f32)],
        name="attn_fwd", compiler_params=_cp(("arbitrary", "arbitrary")),
    )(sinks, q_r, k_r, k_r, k_r, v_b, v_b, v_b, z)


def _attn_bwd(dyb, o32, lse, q_r, k_r, v_b, z, sinks):
    rows = q_r.shape[0]
    nb = rows // BLK
    cl = lambda n: jnp.minimum(n, nb - 1)

    def body(sink_ref, dyb_ref, o_ref, lse_ref, q_ref, km_ref, kp_ref, kc_ref, vm_ref, vp_ref, vc_ref, ga_ref,
             dq_ref, dga_ref, dk_ref, dv_ref, dkm_ref, dvm_ref, dsr_ref, dgat_ref, ck_s, cv_s):
        h, n = pl.program_id(0), pl.program_id(1)

        @pl.when(n == 0)
        def _():
            dkm_ref[...] = jnp.zeros_like(dkm_ref)
            dvm_ref[...] = jnp.zeros_like(dvm_ref)
            ck_s[...] = jnp.zeros_like(ck_s)
            cv_s[...] = jnp.zeros_like(cv_s)

        @pl.when(n < nb)
        def _():
            kk = jnp.concatenate([kp_ref[...], kc_ref[...], km_ref[...]], axis=0)
            vv = jnp.concatenate([vp_ref[...], vc_ref[...], vm_ref[...]], axis=0)
            sg, dsg = _silu_and_grad(ga_ref[...])
            dyb_v = dyb_ref[...]
            o_v = o_ref[...]
            dga = dyb_v * o_v * dsg
            dga_ref[...] = dga.astype(bf16)
            dgat_ref[...] = dga.T.astype(bf16)
            q2 = _stack_heads(q_ref[...])
            do2 = _stack_heads(dyb_v * sg)
            lse_v = lse_ref[...]
            lse = jnp.concatenate([lse_v[:, g:g + 1] for g in range(GROUP)], axis=0).reshape(GROUP, BLK, 1)
            delta = jnp.sum(do2 * _stack_heads(o_v), axis=-1, keepdims=True).reshape(GROUP, BLK, 1)
            s = jnp.where(_attn_mask(n)[None], _dot_nt(q2, kk).reshape(GROUP, BLK, N_KEYS), NEG_INF)
            p = jnp.exp(s - lse)
            do2b = do2.astype(bf16)
            ds = (p * (_dot_nt(do2b, vv).reshape(GROUP, BLK, N_KEYS) - delta)).astype(bf16)
            ds = ds.reshape(GROUP * BLK, N_KEYS)
            dsr = -jnp.exp(_sink_column(sink_ref, h) - lse) * delta
            dq2 = _dot(ds, kk)
            for g in range(GROUP):
                dq_ref[:, g * HEAD_DIM:(g + 1) * HEAD_DIM] = dq2[g * BLK:(g + 1) * BLK]
                dsr_ref[:, g:g + 1] = dsr[g]
            dkk = _dot_tn(ds, q2)
            dvv = _dot_tn(p.astype(bf16).reshape(GROUP * BLK, N_KEYS), do2b)
            dk_ref[...] = ck_s[...] + dkk[:BLK]
            dv_ref[...] = cv_s[...] + dvv[:BLK]
            ck_s[...] = dkk[BLK:2 * BLK]
            cv_s[...] = dvv[BLK:2 * BLK]
            dkm_ref[...] += dkk[2 * BLK:]
            dvm_ref[...] += dvv[2 * BLK:]

        @pl.when(n == nb)
        def _():
            dk_ref[...] = ck_s[...]
            dv_ref[...] = cv_s[...]

    tile = pl.BlockSpec((BLK, 512), lambda h, n: (cl(n), h))
    kvout = pl.BlockSpec((None, BLK, HEAD_DIM), lambda h, n: (h, jnp.maximum(n - 1, 0), 0))
    mout = pl.BlockSpec((None, N_META, HEAD_DIM), lambda h, n: (h, 0, 0))
    stat = pl.BlockSpec((None, BLK, GROUP), lambda h, n: (h, cl(n), 0))
    return pl.pallas_call(
        body, grid=(N_KV, nb + 1),
        in_specs=[pl.BlockSpec(memory_space=pltpu.SMEM), tile, tile, stat, tile] + _kv_specs(nb - 1)
                 + _kv_specs(nb - 1) + [pl.BlockSpec((BLK, 512), lambda h, n: (cl(n), OFF_GA // 512 + h))],
        out_specs=[tile, tile, kvout, kvout, mout, mout, stat, pl.BlockSpec((512, BLK), lambda h, n: (h, cl(n)))],
        out_shape=[SDS((rows, D), f32), SDS((rows, D), bf16),
                   SDS((N_KV, rows, HEAD_DIM), f32), SDS((N_KV, rows, HEAD_DIM), f32),
                   SDS((N_KV, N_META, HEAD_DIM), f32), SDS((N_KV, N_META, HEAD_DIM), f32),
                   SDS((N_KV, rows, GROUP), f32), SDS((D, rows), bf16)],
        scratch_shapes=[pltpu.VMEM((BLK, HEAD_DIM), f32), pltpu.VMEM((BLK, HEAD_DIM), f32)],
        name="attn_bwd", compiler_params=_cp(("arbitrary", "arbitrary")),
    )(sinks, dyb, o32, lse, q_r, k_r, k_r, k_r, v_b, v_b, v_b, z)


def _qkv_finish(dq, dk, dv, dkm, dvm, cos128, sin128):
    rows = dq.shape[0]

    def body(dq_ref, dk_ref, dv_ref, dkm_ref, dvm_ref, c_ref, s_ref, oq_ref, okv_ref, oqt_ref, okvt_ref):
        first = (pl.program_id(0) == 0).astype(f32)
        c, s = c_ref[...], -s_ref[...]
        for g in range(D // 128):
            oq_ref[:, g * 128:(g + 1) * 128] = (_rope128(dq_ref[:, g * 128:(g + 1) * 128], c, s)
                                                * (HEAD_DIM ** -0.5)).astype(bf16)
        pad = jnp.zeros((ROW0, HEAD_DIM), f32)
        ks = [dk_ref[h] + first * jnp.concatenate([pad, dkm_ref[h]], axis=0) for h in range(N_KV)]
        vs = [dv_ref[h] + first * jnp.concatenate([pad, dvm_ref[h]], axis=0) for h in range(N_KV)]
        for g in range(2):
            kp = jnp.concatenate([ks[2 * g], ks[2 * g + 1]], axis=1)
            okv_ref[:, g * 128:(g + 1) * 128] = _rope128(kp, c, s).astype(bf16)
            okv_ref[:, 256 + g * 128:256 + (g + 1) * 128] = jnp.concatenate([vs[2 * g], vs[2 * g + 1]], axis=1).astype(bf16)
        oqt_ref[...] = oq_ref[...].astype(f32).T.astype(bf16)
        okvt_ref[...] = okv_ref[...].astype(f32).T.astype(bf16)

    kv = pl.BlockSpec((N_KV, BLK, HEAD_DIM), lambda i: (0, i, 0))
    mt = pl.BlockSpec((N_KV, N_META, HEAD_DIM), lambda i: (0, 0, 0))
    return pl.pallas_call(
        body, grid=(rows // BLK,),
        in_specs=[pl.BlockSpec((BLK, D), lambda i: (i, 0)), kv, kv, mt, mt,
                  pl.BlockSpec((BLK, 128), lambda i: (i, 0)), pl.BlockSpec((BLK, 128), lambda i: (i, 0))],
        out_specs=[pl.BlockSpec((BLK, D), lambda i: (i, 0)), pl.BlockSpec((BLK, 512), lambda i: (i, 0)),
                   pl.BlockSpec((D, BLK), lambda i: (0, i)), pl.BlockSpec((512, BLK), lambda i: (0, i))],
        out_shape=[SDS((rows, D), bf16), SDS((rows, 512), bf16), SDS((D, rows), bf16), SDS((512, rows), bf16)],
        name="qkv_finish", compiler_params=_cp(("arbitrary",)),
    )(dq, dk, dv, dkm, dvm, cos128, sin128)


def _mix_fwd(y_a, y_b, z):
    rows = y_a.shape[0]
    tw = 256
    col = lambda off: pl.BlockSpec((rows, tw), lambda j: (0, off // tw + j))

    def body(ya_ref, yb_ref, ga_ref, gb_ref, o_ref, ot_ref):
        mixed = (_sigmoid(ga_ref[...]) * ya_ref[...].astype(f32)
                 + _sigmoid(gb_ref[...]) * yb_ref[...].astype(f32))
        o_ref[...] = mixed.astype(bf16)
        ot_ref[...] = mixed.T.astype(bf16)

    return pl.pallas_call(
        body, grid=(D // tw,), in_specs=[col(0), col(0), col(OFF_G), col(OFF_G + D)],
        out_specs=[col(0), pl.BlockSpec((tw, rows), lambda j: (j, 0))],
        out_shape=[SDS((rows, D), bf16), SDS((D, rows), bf16)],
        name="mix_fwd", compiler_params=_cp(("arbitrary",)),
    )(y_a, y_b, z, z)


def _mix_bwd(dmixed, y_a, y_b, z):
    rows = y_a.shape[0]
    tw = 256
    col = lambda off: pl.BlockSpec((rows, tw), lambda j: (0, off // tw + j))
    col_t = pl.BlockSpec((tw, rows), lambda j: (j, 0))

    def body(dm_ref, ya_ref, yb_ref, ga_ref, gb_ref, dya_ref, dyb_ref, dga_ref, dgb_ref, dgat_ref, dgbt_ref):
        dm = dm_ref[...].astype(f32)
        sa, sb = _sigmoid(ga_ref[...]), _sigmoid(gb_ref[...])
        dya_ref[...] = (dm * sa).astype(bf16)
        dyb_ref[...] = (dm * sb).astype(bf16)
        dga = dm * ya_ref[...].astype(f32) * sa * (1.0 - sa)
        dgb = dm * yb_ref[...].astype(f32) * sb * (1.0 - sb)
        dga_ref[...] = dga.astype(bf16)
        dgb_ref[...] = dgb.astype(bf16)
        dgat_ref[...] = dga.T.astype(bf16)
        dgbt_ref[...] = dgb.T.astype(bf16)

    return pl.pallas_call(
        body, grid=(D // tw,), in_specs=[col(0), col(0), col(0), col(OFF_G), col(OFF_G + D)],
        out_specs=[col(0)] * 4 + [col_t] * 2,
        out_shape=[SDS((rows, D), bf16)] * 4 + [SDS((D, rows), bf16)] * 2,
        name="mix_bwd", compiler_params=_cp(("arbitrary",)),
    )(dmixed, y_a, y_b, z, z)


def _final_ln(out32, h32, tgt, ln_g, ln_b):
    rows = out32.shape[0]

    def body(o_ref, h_ref, t_ref, g_ref, b_ref, du_ref, dub_ref, st_ref):
        i = pl.program_id(0)
        g = g_ref[...]
        y, xhat, rstd = _ln_rows(ALPHA * h_ref[...] + o_ref[...], g, b_ref[...])
        e = jnp.where(i > 0, y - t_ref[0], 0.0)
        dy = e * (1.0 / D)
        du = _ln_rows_bwd(dy, g, xhat, rstd)
        du_ref[...] = du
        dub_ref[...] = du.astype(bf16)
        st = jnp.concatenate([_colsum(dy * xhat), _colsum(dy), _colsum(du), _colsum(e * e) * (0.5 / D),
                              jnp.zeros((4, D), f32)], axis=0)

        @pl.when(i == 0)
        def _():
            st_ref[...] = st

        @pl.when(i > 0)
        def _():
            st_ref[...] += st

    row = pl.BlockSpec((BLK, D), lambda i: (i, 0))
    vec = pl.BlockSpec((1, D), lambda i: (0, 0))
    return pl.pallas_call(
        body, grid=(rows // BLK,),
        in_specs=[row, row, pl.BlockSpec((1, BLK, D), lambda i: (0, jnp.maximum(i - 1, 0), 0)), vec, vec],
        out_specs=[row, row, pl.BlockSpec((8, D), lambda i: (0, 0))],
        out_shape=[SDS((rows, D), f32), SDS((rows, D), bf16), SDS((8, D), f32)],
        name="final_ln", compiler_params=_cp(("arbitrary",)),
    )(out32, h32, tgt, ln_g, ln_b)


def _step_rnn(h32, hb, z, wrg, smallw, p, zero):
    rows = z.shape[0]
    cos128, sin128 = _rope_tables(rows)
    cos128 = cos128 + zero
    xc, hr, ya, ya_t = _rnn_fwd(z, smallw, p["conv_b"] + zero, wrg, p["b_ra"], p["b_ri"], p["lru_lambda"])
    q_r, k_r, v_b = _qkv_prep(z, cos128, sin128)
    return dict(cos128=cos128, sin128=sin128, h32=h32, hb=hb, z=z, xc=xc, hr=hr, ya=ya, ya_t=ya_t,
                q_r=q_r, k_r=k_r, v_b=v_b)


def _step_attn(s, p, zero):
    sinks = p["sinks"].reshape(N_KV * GROUP) + zero[0]
    o32, yb, yb_t, lse = _attn_fwd(s["q_r"], s["k_r"], s["v_b"], s["z"], sinks)
    return dict(s, sinks=sinks, o32=o32, yb=yb, yb_t=yb_t, lse=lse)


def _step_merge(s, tgt, w3, p):
    ya, yb, z = s["ya"], s["yb"], s["z"]
    y_a = _mm(ya, w3, sel=0, out_dtype=bf16, name="mm_ya")
    y_b = _mm(yb, w3, sel=1, out_dtype=bf16, name="mm_yb")
    mixed, mixed_t = _mix_fwd(y_a, y_b, z)
    out32 = _mm(mixed, w3, sel=2, bias=p["b_o"], name="mm_out")
    du32, dub, st_out = _final_ln(out32, s["h32"], tgt, p["ln_g"], p["ln_b"])

    g_wo = _mm(mixed_t, dub, out_dtype=bf16, name="mm_dwo")
    dmixed = _mm(dub, w3, sel=2, nt=True, out_dtype=bf16, name="mm_dmixed")
    dya_b, dyb_b, dma, dmb, dma_t, dmb_t = _mix_bwd(dmixed, y_a, y_b, z)
    g_wrnn = _mm(s["ya_t"], dya_b, out_dtype=bf16, name="mm_dwrnn")
    g_wattn = _mm(s["yb_t"], dyb_b, out_dtype=bf16, name="mm_dwattn")
    dya = _mm(dya_b, w3, sel=0, nt=True, name="mm_dya")
    dyb = _mm(dyb_b, w3, sel=1, nt=True, name="mm_dyb")
    return dict(du32=du32, st_out=st_out, dma=dma, dmb=dmb, dma_t=dma_t, dmb_t=dmb_t, dya=dya, dyb=dyb, g_wo=g_wo,
                g_wrnn=g_wrnn, g_wattn=g_wattn)


def _step_backward(s, t, wrg, smallw, p, conv_b):
    z = s["z"]
    dxr, dgr, g_wrg, vec_rnn, dxr_t, dgr_t = _rnn_bwd(t["dya"], s["hr"], s["xc"], z, smallw, conv_b, wrg, p["b_ra"],
                                                      p["b_ri"], p["lru_lambda"])
    dq_r, dga, dk, dv, dkm, dvm, dsr, dga_t = _attn_bwd(t["dyb"], s["o32"], s["lse"], s["q_r"], s["k_r"], s["v_b"],
                                                        z, s["sinks"])
    dq, dkv, dq_t, dkv_t = _qkv_finish(dq_r, dk, dv, dkm, dvm, s["cos128"], s["sin128"])
    dz_parts = [(dxr, D), (dgr, D), (dq, D), (dkv, 512), (dga, D), (t["dma"], D), (t["dmb"], D)]
    dz_parts_t = [(dxr_t, D), (dgr_t, D), (dq_t, D), (dkv_t, 512), (dga_t, D), (t["dma_t"], D), (t["dmb_t"], D)]
    return dict(vec_rnn=vec_rnn, dsr=dsr, g_wrg=g_wrg, dz_parts=dz_parts, dz_parts_t=dz_parts_t)


def _step_input_grad(dh_lo, dh_hi, du32, x, smallw, p, after):
    grad_x, dmeta, st_emb = _ln_emb_bwd(dh_lo, dh_hi, du32, x, smallw, p["ln_emb_g"], after)
    return dict(grad_x=grad_x, dmeta=dmeta, st_emb=st_emb)


_ANY = pl.BlockSpec(memory_space=pl.ANY)
_VMEM = pl.BlockSpec(memory_space=pltpu.VMEM)


def _place():
    x, y, c = lax.axis_index("x"), lax.axis_index("y"), lax.axis_index("c")
    return x, y, c


def _dev(px, py, pc):
    return 4 * px + 2 * py + pc


def _tile_rows(r):
    return max(t for t in range(16, 321, 16) if r % t == 0) if r > 320 else r


def _cast_w_in(w_in_t):
    tm = _tile_rows(SHARD_IN)

    def body(i_ref, o_ref):
        o_ref[...] = i_ref[...].astype(bf16)

    return pl.pallas_call(
        body, grid=(SHARD_IN // tm,),
        in_specs=[pl.BlockSpec((tm, D), lambda i: (i, 0))],
        out_specs=pl.BlockSpec((tm, D), lambda i: (i, 0)),
        out_shape=SDS((SHARD_IN, D), bf16), name="cast_w_in", compiler_params=_cp(("arbitrary",)),
    )(w_in_t)


def _cast_small(w_rnn_out, w_attn_out, w_o, w_ra, w_ri, meta, conv_w):
    def body(a_ref, b_ref, c_ref, ra_ref, ri_ref, m_ref, cw_ref, w3_ref, wrg_ref, sw_ref):
        w3_ref[0] = a_ref[0].astype(bf16)
        w3_ref[1] = b_ref[0].astype(bf16)
        w3_ref[2] = c_ref[0].astype(bf16)
        wrg_ref[0] = ra_ref[0].astype(bf16)
        wrg_ref[1] = ri_ref[0].astype(bf16)
        sw_ref[...] = jnp.concatenate([m_ref[...], cw_ref[0], jnp.zeros((4, 256), f32)], axis=0)

    return pl.pallas_call(
        body,
        out_shape=[SDS((3, 256, D), bf16), SDS((2, N_RNN_BLOCKS, 32, RNN_BLOCK), bf16), SDS((24, 256), f32)],
        name="cast_small", compiler_params=_cp(None),
    )(w_rnn_out, w_attn_out, w_o, w_ra, w_ri, meta, conv_w)


def _all_gather(shards, later):
    n = len(shards)
    nl = len(later)

    def body(*refs):
        ins, outs = refs[:n], refs[n + nl:2 * n + nl]
        send_sems, recv_sems, local_sems = refs[2 * (n + nl):]
        x, y, c = _place()
        me, sibling = (x, y, c), (x, y, 1 - c)
        chips = [(1 - x, y), (x, 1 - y), (1 - x, 1 - y)]

        def copy(a, k, block, to, src=None):
            dst = outs[a].at[_dev(*block)]
            return pltpu.make_async_remote_copy(
                src_ref=dst if src is None else src, dst_ref=dst,
                send_sem=send_sems.at[a * 7 + k], recv_sem=recv_sems.at[a * 7 + k],
                device_id=to, device_id_type=MESH)

        all_ins, all_outs = refs[:n + nl], refs[n + nl:2 * (n + nl)]
        mine = [pltpu.make_async_copy(all_ins[a], all_outs[a].at[_dev(*me)], local_sems.at[a]) for a in range(n + nl)]
        for cp in mine:
            cp.start()
        first = []
        for a in range(n):
            first.append(copy(a, 0, me, sibling, src=ins[a]))
            first += [copy(a, 1 + j, me, (*chip, c), src=ins[a]) for j, chip in enumerate(chips)]
        for cp in first:
            cp.start()
        passed = []
        for a in range(n):
            for j, chip in enumerate(chips):
                copy(a, 1 + j, (*chip, c), me).wait_recv()
                cp = copy(a, 4 + j, (*chip, c), sibling)
                cp.start()
                passed.append(cp)
        for a in range(n):
            copy(a, 0, sibling, me).wait_recv()
            for j, chip in enumerate(chips):
                copy(a, 4 + j, (*chip, 1 - c), me).wait_recv()
        for cp in first + passed:
            cp.wait_send()
        for cp in mine:
            cp.wait()

    return pl.pallas_call(
        body, in_specs=[_ANY] * (n + nl), out_specs=[_ANY] * (n + nl),
        out_shape=[SDS((N_DEV, *s.shape), s.dtype) for s in (*shards, *later)],
        scratch_shapes=[pltpu.SemaphoreType.DMA((7 * n,)), pltpu.SemaphoreType.DMA((7 * n,)),
                        pltpu.SemaphoreType.DMA((n + nl,))],
        name="all_gather_weights",
    )(*shards, *later)


_HBM = pl.BlockSpec(memory_space=pltpu.HBM)
_SEM = pl.BlockSpec(memory_space=pltpu.SEMAPHORE)
_PEER_FLIPS = [(f // 4, (f // 2) % 2, f % 2) for f in range(1, N_DEV)]


def _remote(src, dst, send_sems, recv_sems, k, to):
    return pltpu.make_async_remote_copy(src_ref=src, dst_ref=dst, send_sem=send_sems.at[k], recv_sem=recv_sems.at[k],
                                        device_id=to, device_id_type=MESH)


def _copies_direct(same_src):
    def make(srcs, lands, send_sems, recv_sems):
        x, y, c = _place()
        me = _dev(x, y, c)
        out = []
        for a in range(len(srcs)):
            for k, (fx, fy, fc) in enumerate(_PEER_FLIPS):
                peer = ((x + fx) % 2, (y + fy) % 2, (c + fc) % 2)
                src = srcs[a] if same_src else srcs[a].at[_dev(*peer)]
                out.append(_remote(src, lands[a].at[me], send_sems, recv_sems, 7 * a + k, peer))
        return out
    return make


def _copies_siblings(srcs, lands, send_sems, recv_sems):
    x, y, c = _place()
    return [_remote(srcs[a].at[2 * q + (1 - c)], lands[a].at[q], send_sems, recv_sems, 4 * a + q, (x, y, 1 - c))
            for a in range(len(srcs)) for q in range(4)]


def _copies_chips(srcs, lands, send_sems, recv_sems):
    x, y, c = _place()
    chips = [(1 - x, y), (x, 1 - y), (1 - x, 1 - y)]
    return [_remote(srcs[a].at[2 * qx + qy], lands[a].at[j], send_sems, recv_sems, 3 * a + j, (qx, qy, c))
            for a in range(len(srcs)) for j, (qx, qy) in enumerate(chips)]


def _split_start(make, per_array, srcs, lands, dep, name):
    n = len(srcs)

    def body(*refs):
        send_sems, recv_sems, token = refs[2 * n + 1], refs[2 * n + 2], refs[-1]
        for cp in make(refs[:n], refs[n:2 * n], send_sems, recv_sems):
            cp.start()
        token[...] = jnp.zeros_like(token)

    hbm = lambda t: pltpu.with_memory_space_constraint(t, pltpu.HBM)
    res = pl.pallas_call(
        body, name=name,
        out_shape=(pltpu.SemaphoreType.DMA((per_array * n,)), pltpu.SemaphoreType.DMA((per_array * n,)),
                   *[pltpu.HBM(t.shape, t.dtype) for t in (*srcs, *lands)], SDS((8, 128), f32)),
        in_specs=[_HBM] * (2 * n) + [_ANY], out_specs=(_SEM, _SEM, *([_HBM] * (2 * n)), _VMEM),
        input_output_aliases={i: 2 + i for i in range(2 * n)},
        compiler_params=pltpu.CompilerParams(has_side_effects=pltpu.SideEffectType.DATAFLOW_SIDE_EFFECTING),
    )(*[hbm(t) for t in (*srcs, *lands)], dep)
    return res[0], res[1], list(res[2:2 + n]), list(res[2 + n:2 + 2 * n]), res[-1]


def _split_wait(make, send_sems, recv_sems, srcs, lands, after, name):
    n = len(srcs)

    def body(*refs):
        for cp in make(refs[:n], refs[n:2 * n], refs[2 * n], refs[2 * n + 1]):
            cp.wait_send()
            cp.wait_recv()

    res = pl.pallas_call(
        body, name=name,
        out_shape=tuple(pltpu.HBM(t.shape, t.dtype) for t in (*srcs, *lands)),
        in_specs=[_HBM] * (2 * n) + [_SEM, _SEM, _ANY], out_specs=tuple([_HBM] * (2 * n)),
        input_output_aliases={i: i for i in range(2 * n)},
        compiler_params=pltpu.CompilerParams(has_side_effects=pltpu.SideEffectType.DATAFLOW_SIDE_EFFECTING),
    )(*srcs, *lands, send_sems, recv_sems, after)
    return list(res[:n]), list(res[n:])


def _adamw_direct(g, land, me_idx, w, m, v, name):
    r, wd = w.shape
    tr = min(r, 256)

    def body(me_ref, *refs):
        g_ref, peers = refs[0], refs[1:N_DEV]
        w_ref, m_ref, v_ref, g_out, d_out, m_out, v_out = refs[N_DEV:]
        gs = g_ref[...].astype(f32)
        for p_ref in peers:
            gs = gs + p_ref[...].astype(f32)
        d, mn, vn = _adamw(w_ref[...], gs, m_ref[...], v_ref[...])
        g_out[...] = gs
        d_out[...] = d
        m_out[...] = mn
        v_out[...] = vn

    tile = pl.BlockSpec((tr, wd), lambda i, me_ref: (i, 0))
    slot = lambda k: pl.BlockSpec((None, tr, wd), lambda i, me_ref: ((me_ref[0] + k) % N_DEV, i, 0))
    return pl.pallas_call(
        body,
        grid_spec=pltpu.PrefetchScalarGridSpec(
            num_scalar_prefetch=1, grid=(r // tr,),
            in_specs=[slot(0)] + [slot(k) for k in range(1, N_DEV)] + [tile, tile, tile],
            out_specs=[tile] * 4),
        out_shape=[SDS((r, wd), f32)] * 4, name=name, compiler_params=_cp(("arbitrary",), 48),
    )(me_idx, g, *([land] * (N_DEV - 1)), w, m, v)


def _pair_sum(g, r1, c_idx, name):
    _, r, w = g.shape
    tr = _tile_rows(r)

    def body(c_ref, g_ref, r_ref, o_ref):
        o_ref[...] = (g_ref[...].astype(f32) + r_ref[...].astype(f32)).astype(bf16)

    return pl.pallas_call(
        body,
        grid_spec=pltpu.PrefetchScalarGridSpec(
            num_scalar_prefetch=1, grid=(4, r // tr),
            in_specs=[pl.BlockSpec((None, tr, w), lambda q, i, c_ref: (2 * q + c_ref[0], i, 0)),
                      pl.BlockSpec((None, tr, w), lambda q, i, c_ref: (q, i, 0))],
            out_specs=pl.BlockSpec((None, tr, w), lambda q, i, c_ref: (q, i, 0))),
        out_shape=SDS((4, r, w), bf16), name=name, compiler_params=_cp(("arbitrary", "arbitrary")),
    )(c_idx, g, r1)


def _adamw(w, g, m, v):
    m = ADAM_B1 * m + (1.0 - ADAM_B1) * g
    v = ADAM_B2 * v + (1.0 - ADAM_B2) * (g * g)
    m_hat = m / (1.0 - ADAM_B1 ** ADAM_STEP)
    v_hat = v / (1.0 - ADAM_B2 ** ADAM_STEP)
    delta = -ADAM_LR * (m_hat / (jnp.sqrt(v_hat) + ADAM_EPS) + ADAM_WD * w)
    return delta, m, v


def _adamw_big(part, r2, q_idx, w, m, v, name, row_off=0, cols=(0, 1), prev=None):
    r, wd = w.shape
    tr = _tile_rows(r)
    k, ncol = cols
    wp = wd // ncol

    def body(q_ref, p_ref, r_ref, w_ref, m_ref, v_ref, *rest):
        g_out, d_out, m_out, v_out = rest[-4:]
        g = p_ref[...].astype(f32)
        for j in range(3):
            g = g + r_ref[j].astype(f32)
        d, mn, vn = _adamw(w_ref[...], g, m_ref[...], v_ref[...])
        g_out[...] = g
        d_out[...] = d
        m_out[...] = mn
        v_out[...] = vn

    tile = pl.BlockSpec((tr, wp), lambda i, q_ref: (i, k))
    prev = list(prev) if prev is not None else []
    return pl.pallas_call(
        body,
        grid_spec=pltpu.PrefetchScalarGridSpec(
            num_scalar_prefetch=1, grid=(r // tr,),
            in_specs=[pl.BlockSpec((None, tr, wp), lambda i, q_ref: (q_ref[0], row_off + i, 0)),
                      pl.BlockSpec((3, tr, wp), lambda i, q_ref: (0, row_off + i, 0)), tile, tile, tile]
                     + [pl.BlockSpec(memory_space=pl.ANY)] * len(prev),
            out_specs=[tile] * 4),
        out_shape=[SDS((r, wd), f32)] * 4, name=name,
        input_output_aliases={6 + i: i for i in range(len(prev))},
        compiler_params=_cp(("arbitrary",), 48),
    )(q_idx, part, r2, w, m, v, *prev)


_SMALL_ROWS = 24


def _pack_early(vec_rnn, st_out, dsr, db_in):
    def body(vr_ref, so_ref, dsr_ref, db_ref, sm_ref, sm2_ref):
        sm_ref[...] = jnp.zeros_like(sm_ref)
        sm2_ref[...] = jnp.zeros_like(sm2_ref)
        sm_ref[2:3, :] = vr_ref[3:4, :]
        sm_ref[3:6, :] = vr_ref[0:3, :]
        sm_ref[6:7, :] = so_ref[2:3, :]
        sm_ref[7:9, :] = so_ref[0:2, :]
        sm_ref[10:11, :] = so_ref[3:4, :]
        for h in range(N_KV):
            sm_ref[9:10, h * GROUP:(h + 1) * GROUP] = _colsum(dsr_ref[h])
        for j in range(6):
            sm_ref[16 + j:17 + j, :] = db_ref[0:1, j * D:(j + 1) * D]
        sm_ref[22:23, 0:D_IN - 6 * D] = db_ref[0:1, 6 * D:D_IN]
        for s in range(N_DEV):
            sm2_ref[s, 0:CONV_WIDTH, :] = vr_ref[4:8, s * 256:(s + 1) * 256]

    return pl.pallas_call(
        body, out_shape=[SDS((_SMALL_ROWS, D), f32), SDS((N_DEV, 8, 256), f32)],
        name="pack_early", compiler_params=_cp(None),
    )(vec_rnn, st_out, dsr, db_in)


def _pack_late(st_emb, dmeta):
    def body(se_ref, dm_ref, sm_ref, sm2_ref):
        sm_ref[...] = se_ref[...]
        for s in range(N_DEV):
            sm2_ref[s] = dm_ref[:, s * 256:(s + 1) * 256]

    return pl.pallas_call(
        body, out_shape=[SDS((8, D), f32), SDS((N_DEV, N_META, 256), f32)],
        name="pack_late", compiler_params=_cp(None),
    )(st_emb, dmeta)


def _small_allreduce(sm, sm2):
    def body(sm_ref, sm2_ref, o_ref, o2_ref, buf, buf2, send_sems, recv_sems):
        x, y, c = _place()
        me = _dev(x, y, c)
        copies = []
        for f in range(1, N_DEV):
            fx, fy, fc = f // 4, (f // 2) % 2, f % 2
            peer = ((x + fx) % 2, (y + fy) % 2, (c + fc) % 2)
            for t, (src, dst) in enumerate(((sm_ref, buf), (sm2_ref, buf2))):
                k = 2 * (f - 1) + t
                copies.append(pltpu.make_async_remote_copy(
                    src_ref=src, dst_ref=dst.at[me], send_sem=send_sems.at[k], recv_sem=recv_sems.at[k],
                    device_id=peer, device_id_type=MESH))
        for cp in copies:
            cp.start()
        buf[me] = sm_ref[...]
        buf2[me] = sm2_ref[...]
        for cp in copies:
            cp.wait()
        acc, acc2 = buf[0], buf2[0]
        for e in range(1, N_DEV):
            acc, acc2 = acc + buf[e], acc2 + buf2[e]
        o_ref[...] = acc
        o2_ref[...] = acc2

    return pl.pallas_call(
        body, in_specs=[_VMEM, _VMEM], out_specs=[_VMEM, _VMEM],
        out_shape=[SDS(sm.shape, f32), SDS(sm2.shape, f32)],
        scratch_shapes=[pltpu.VMEM((N_DEV, *sm.shape), f32), pltpu.VMEM((N_DEV, *sm2.shape), f32),
                        pltpu.SemaphoreType.DMA((14,)), pltpu.SemaphoreType.DMA((14,))],
        name="small_allreduce",
    )(sm, sm2)


_SMALL_ROW_OF = {"ln_emb_g": 0, "ln_emb_b": 1, "conv_b": 2, "b_ra": 3, "b_ri": 4, "lru_lambda": 5, "b_o": 6,
                 "ln_g": 7, "ln_b": 8}
_SMALL_NAMES = ["ln_emb_g", "ln_emb_b", "conv_b", "b_ra", "b_ri", "lru_lambda", "b_o", "ln_g", "ln_b",
                "sinks", "b_in", "meta_tokens", "conv_w"]


def _small_update(me_idx, early, late, wmv):
    n_fixed = 7

    def in_order(me, own_ref, land_ref):
        acc = None
        for e in range(N_DEV):
            term = jnp.where(me == e, own_ref[...], land_ref[e])
            acc = term if acc is None else acc + term
        return acc

    def body(*refs):
        me_ref, own_ref, land_ref, cown_ref, cland_ref, late_ref, meta_ref = refs[:n_fixed]
        ins = refs[n_fixed:n_fixed + 3 * len(_SMALL_NAMES)]
        outs = refs[n_fixed + 3 * len(_SMALL_NAMES):]
        me = me_ref[0]
        sm = in_order(me, own_ref, land_ref)
        conv = in_order(me, cown_ref, cland_ref)

        def grad_of(name):
            if name in ("ln_emb_g", "ln_emb_b"):
                r = _SMALL_ROW_OF[name]
                return late_ref[r:r + 1, :]
            if name in _SMALL_ROW_OF:
                r = _SMALL_ROW_OF[name]
                return sm[r:r + 1, :]
            if name == "sinks":
                return sm[9:10, 0:N_KV * GROUP]
            if name == "b_in":
                return jnp.concatenate([sm[16 + j:17 + j, :] for j in range(7)], axis=1)[:, :D_IN]
            if name == "meta_tokens":
                return meta_ref[...]
            return conv[0:CONV_WIDTH, :]

        for i, name in enumerate(_SMALL_NAMES):
            w_ref, m_ref, v_ref = ins[3 * i:3 * i + 3]
            g = grad_of(name)
            d, mn, vn = _adamw(w_ref[...], g, m_ref[...], v_ref[...])
            outs[4 * i][...] = g
            outs[4 * i + 1][...] = d
            outs[4 * i + 2][...] = mn
            outs[4 * i + 3][...] = vn
        outs[-1][...] = jnp.broadcast_to(jnp.sum(sm[10:11, :], axis=1, keepdims=True), (8, 128))

    args, out_shape = [me_idx, *early, *late], []
    for name in _SMALL_NAMES:
        args += list(wmv[name])
        out_shape += [SDS(wmv[name][0].shape, f32)] * 4
    out_shape.append(SDS((8, 128), f32))
    res = pl.pallas_call(
        body, out_shape=out_shape, in_specs=[pl.BlockSpec(memory_space=pltpu.SMEM)] + [_VMEM] * (len(args) - 1),
        name="small_update", compiler_params=_cp(None))(*args)
    return {name: tuple(res[4 * i:4 * i + 4]) for i, name in enumerate(_SMALL_NAMES)}, res[-1][0, 0]


_WEIGHTS = ["meta_tokens", "ln_emb_g", "ln_emb_b", "w_in", "b_in", "conv_w", "conv_b", "w_ra", "b_ra", "w_ri",
            "b_ri", "lru_lambda", "sinks", "w_rnn_out", "w_attn_out", "w_o", "b_o", "ln_g", "ln_b"]
_SMALL_2D = {"meta_tokens": (N_META, 256), "conv_w": (CONV_WIDTH, 256), "b_in": (1, D_IN), "sinks": (1, N_KV * GROUP)}


def kernel(x, meta_tokens, ln_emb_g, ln_emb_b, w_in, b_in, conv_w, conv_b, w_ra, b_ra, w_ri, b_ri, lru_lambda, sinks, w_rnn_out, w_attn_out, w_o, b_o, ln_g, ln_b, loss_target, m_meta_tokens, m_ln_emb_g, m_ln_emb_b, m_w_in, m_b_in, m_conv_w, m_conv_b, m_w_ra, m_b_ra, m_w_ri, m_b_ri, m_lru_lambda, m_sinks, m_w_rnn_out, m_w_attn_out, m_w_o, m_b_o, m_ln_g, m_ln_b, v_meta_tokens, v_ln_emb_g, v_ln_emb_b, v_w_in, v_b_in, v_conv_w, v_conv_b, v_w_ra, v_b_ra, v_w_ri, v_b_ri, v_lru_lambda, v_sinks, v_w_rnn_out, v_w_attn_out, v_w_o, v_b_o, v_ln_g, v_ln_b):
    w = dict(meta_tokens=meta_tokens, ln_emb_g=ln_emb_g, ln_emb_b=ln_emb_b, w_in=w_in, b_in=b_in, conv_w=conv_w,
             conv_b=conv_b, w_ra=w_ra, b_ra=b_ra, w_ri=w_ri, b_ri=b_ri, lru_lambda=lru_lambda, sinks=sinks,
             w_rnn_out=w_rnn_out, w_attn_out=w_attn_out, w_o=w_o, b_o=b_o, ln_g=ln_g, ln_b=ln_b)
    m = dict(meta_tokens=m_meta_tokens, ln_emb_g=m_ln_emb_g, ln_emb_b=m_ln_emb_b, w_in=m_w_in, b_in=m_b_in,
             conv_w=m_conv_w, conv_b=m_conv_b, w_ra=m_w_ra, b_ra=m_b_ra, w_ri=m_w_ri, b_ri=m_b_ri,
             lru_lambda=m_lru_lambda, sinks=m_sinks, w_rnn_out=m_w_rnn_out, w_attn_out=m_w_attn_out, w_o=m_w_o,
             b_o=m_b_o, ln_g=m_ln_g, ln_b=m_ln_b)
    v = dict(meta_tokens=v_meta_tokens, ln_emb_g=v_ln_emb_g, ln_emb_b=v_ln_emb_b, w_in=v_w_in, b_in=v_b_in,
             conv_w=v_conv_w, conv_b=v_conv_b, w_ra=v_w_ra, b_ra=v_b_ra, w_ri=v_w_ri, b_ri=v_b_ri,
             lru_lambda=v_lru_lambda, sinks=v_sinks, w_rnn_out=v_w_rnn_out, w_attn_out=v_w_attn_out, w_o=v_w_o,
             b_o=v_b_o, ln_g=v_ln_g, ln_b=v_ln_b)
    px, py, pc = _place()
    as_idx = lambda t: jnp.reshape(t, (1,)).astype(jnp.int32)
    c_idx, q_idx, me_idx = as_idx(pc), as_idx(2 * px + py), as_idx(_dev(px, py, pc))

    w3_s, wrg_s, small_s = _cast_small(w_rnn_out, w_attn_out, w_o, w_ra, w_ri, meta_tokens, conv_w)
    vec = lambda name: w[name].reshape(1, -1)
    p = {k: vec(k) for k in ("ln_emb_g", "ln_emb_b", "b_in", "conv_b", "b_ra", "b_ri", "lru_lambda", "sinks",
                             "b_o", "ln_g", "ln_b")}
    w_in_t = lambda a: jnp.swapaxes(a, 1, 2).reshape(SHARD_IN, D)
    wg, wrg, smallw, w3_land = _all_gather([_cast_w_in(w_in_t(w_in)), wrg_s, small_s], [w3_s])
    w3_pending = _split_start(_copies_direct(True), 7, [w3_s], [w3_land], smallw, "gather_w3_start")
    w_full = wg.reshape(D_IN, D)

    zero = w3_pending[4][0:1, 0:1]
    h32, hb = _ln_emb(x, smallw, p["ln_emb_g"], p["ln_emb_b"])
    z = _mm(hb, w_full, nt=True, bias=p["b_in"] + zero, name="mm_z")
    s = _step_attn(_step_rnn(h32, hb, z, wrg, smallw, p, zero), p, zero)
    w3 = _split_wait(_copies_direct(True), *w3_pending[:4], s["lse"], "gather_w3_wait")[1][0]
    t = _step_merge(s, loss_target, w3, p)

    big = {}
    two_d = lambda name: (w[name].shape[-2], w[name].shape[-1])
    proj = ("w_o", "w_rnn_out", "w_attn_out")
    g_proj = [t[k].reshape(N_DEV, 256, D) for k in ("g_wo", "g_wrnn", "g_wattn")]
    g_pending = _split_start(_copies_direct(False), 7, g_proj, [lax.empty((N_DEV, 256, D), bf16) for _ in proj],
                             p["b_o"], "reduce_proj_start")
    u = _step_backward(s, t, wrg, smallw, p, p["conv_b"] + g_pending[4][0:1, 0:1])

    def siblings_start(gs, dep, tag):
        return _split_start(_copies_siblings, 4, gs, [lax.empty((4, *g.shape[1:]), bf16) for g in gs], dep,
                            "reduce_siblings_start_" + tag)

    def chips_start(gs, r1, dep, tag):
        parts = [_pair_sum(g, r, c_idx, "pair_sum_%s%d" % (tag, i)) for i, (g, r) in enumerate(zip(gs, r1))]
        return _split_start(_copies_chips, 3, parts, [lax.empty((3, *q.shape[1:]), bf16) for q in parts], dep,
                            "reduce_chips_start_" + tag)

    dz, dz_t = u["dz_parts"], u["dz_parts_t"]
    g_a, db_cols = _mm_dwin(s["hb"], dz_t, 0, p["b_o"])
    db_in = db_cols[:, 0].reshape(1, D_IN)
    shards = lambda g: g.reshape(N_DEV, SHARD_IN, W_IN_HALF)
    sib_a = siblings_start([shards(g_a), u["g_wrg"].reshape(N_DEV, 2 * RNN_BLOCK, RNN_BLOCK)], db_in, "a")
    g_proj, g_land = _split_wait(_copies_direct(False), *g_pending[:4], sib_a[4], "reduce_proj_wait")
    for i, name in enumerate(proj):
        res = _adamw_direct(g_proj[i], g_land[i], me_idx, w[name].reshape(two_d(name)), m[name].reshape(two_d(name)),
                            v[name].reshape(two_d(name)), "adamw_" + name)
        big[name] = tuple(r.reshape(w[name].shape) for r in res)
    chp_a = chips_start(*_split_wait(_copies_siblings, *sib_a[:4], big["w_attn_out"][3], "reduce_siblings_wait_a"),
                        db_in, "a")
    g_b, = _mm_dwin(s["hb"], dz_t, 1, chp_a[4])
    sib_b = siblings_start([shards(g_b)], db_in, "b")
    sm_e = _pack_early(u["vec_rnn"], t["st_out"], u["dsr"], db_in)
    early = _split_start(_copies_direct(True), 7, list(sm_e),
                         [lax.empty((N_DEV, *a.shape), f32) for a in sm_e], sib_b[4], "small_early_start")
    dh_lo = _mm_dh(dz, w_full, early[4], 0)
    chp_b = chips_start(*_split_wait(_copies_siblings, *sib_b[:4], dh_lo, "reduce_siblings_wait_b"), db_in, "b")
    dh_hi = _mm_dh(dz, w_full, chp_b[4], 1)
    parts_a, r2_a = _split_wait(_copies_chips, *chp_a[:4], dh_hi, "reduce_chips_wait_a")
    w_in_res = _adamw_big(parts_a[0], r2_a[0], q_idx, w_in_t(w["w_in"]), w_in_t(m["w_in"]), w_in_t(v["w_in"]),
                          "adamw_w_in_a", cols=(0, 2))
    u.update(_step_input_grad(dh_lo, dh_hi, t["du32"], x, smallw, p, w_in_res[3]))
    sm_l, meta_l = _small_allreduce(*_pack_late(u["st_emb"], u["dmeta"]))
    (sm_own, conv_own), (sm_land, conv_land) = _split_wait(_copies_direct(True), *early[:4], sm_l, "small_early_wait")
    me = _dev(px, py, pc)
    mine = lambda a, axis: lax.dynamic_index_in_dim(a, me, axis, keepdims=False)
    two = lambda name, t: t.reshape(_SMALL_2D.get(name, (1, D)))
    small, loss = _small_update(me_idx, (sm_own, sm_land, mine(conv_own, 0), mine(conv_land, 1)),
                                (sm_l, mine(meta_l, 0)),
                                {k: (two(k, w[k]), two(k, m[k]), two(k, v[k])) for k in _SMALL_NAMES})

    parts_b, r2_b = _split_wait(_copies_chips, *chp_b[:4], small["b_in"][2], "reduce_chips_wait_b")
    res = _adamw_big(parts_b[0], r2_b[0], q_idx, w_in_t(w["w_in"]), w_in_t(m["w_in"]), w_in_t(v["w_in"]),
                     "adamw_w_in_b", cols=(1, 2), prev=w_in_res)
    big["w_in"] = tuple(jnp.swapaxes(r.reshape(1, SHARD_IN, D), 1, 2) for r in res)
    for i, name in enumerate(("w_ra", "w_ri")):
        sq = (RNN_BLOCK, RNN_BLOCK)
        res = _adamw_big(parts_a[1], r2_a[1], q_idx, w[name].reshape(sq), m[name].reshape(sq), v[name].reshape(sq),
                         "adamw_" + name, row_off=i)
        big[name] = tuple(r.reshape(w[name].shape) for r in res)
    res = dict(big)
    for k in _SMALL_NAMES:
        res[k] = tuple(t.reshape(w[k].shape) for t in small[k])

    outs = [loss, u["grad_x"]]
    for j in range(4):
        outs += [res[k][j] for k in _WEIGHTS]
    return tuple(outs)
```

```python
import jax
import jax.numpy as jnp
from jax import lax
from jax.experimental import pallas as pl
from jax.experimental.pallas import tpu as pltpu

f32, bf16 = jnp.float32, jnp.bfloat16
SDS = jax.ShapeDtypeStruct

N_DEV = 8
D = 2048
N_META = 16
BLK = 128
ROW0 = BLK - N_META
N_RNN_BLOCKS = 8
RNN_BLOCK = D // N_RNN_BLOCKS
CONV_WIDTH = 4
LRU_C = 8.0
HEAD_DIM = 64
N_KV = 4
GROUP = 8
HALF = HEAD_DIM // 2
ROPE_THETA = 10000.0
NEG_INF = -1e30
LN_EPS = 1e-5
ALPHA = 2.0 ** 0.25
D_IN = 12800
SHARD_IN = D_IN // N_DEV
W_IN_HALF = D // 2
W_IN_LATE = 640
W_IN_EARLY = SHARD_IN - W_IN_LATE
OFF_GR, OFF_Q, OFF_K, OFF_V, OFF_GA, OFF_G = 2048, 4096, 6144, 6400, 6656, 8704
ADAM_LR, ADAM_B1, ADAM_B2, ADAM_EPS, ADAM_WD, ADAM_STEP = 1e-3, 0.9, 0.999, 1e-8, 0.01, 10
VMEM_LIMIT_MB = 56
MESH = pl.DeviceIdType.MESH


def _cp(sem=None, vmem_mb=40):
    return pltpu.CompilerParams(dimension_semantics=sem, vmem_limit_bytes=vmem_mb * 2 ** 20)


def _row_chunk(m):
    best = 16
    for c in range(16, 641, 16):
        if m % c == 0:
            best = c
    return best


def _sigmoid(x):
    return 1.0 / (1.0 + jnp.exp(-x))


def _silu_and_grad(x):
    s = _sigmoid(x)
    return x * s, s * (1.0 + x * (1.0 - s))


def _log_sigmoid(x):
    return jnp.minimum(x, 0.0) - jnp.log1p(jnp.exp(-jnp.abs(x)))


def _ln_rows(v, g, b):
    mu = jnp.mean(v, axis=-1, keepdims=True)
    c = v - mu
    var = jnp.mean(c * c, axis=-1, keepdims=True)
    rstd = lax.rsqrt(var + LN_EPS)
    xhat = c * rstd
    return xhat * g + b, xhat, rstd


def _ln_rows_bwd(dy, g, xhat, rstd):
    dxh = dy * g
    m1 = jnp.mean(dxh, axis=-1, keepdims=True)
    m2 = jnp.mean(dxh * xhat, axis=-1, keepdims=True)
    return rstd * (dxh - m1 - xhat * m2)


def _colsum(v):
    return jnp.sum(v, axis=0, keepdims=True)


def _dot(a, b):
    return jnp.dot(a, b, preferred_element_type=f32)


def _dot_nt(a, b):
    return lax.dot_general(a, b, (((1,), (1,)), ((), ())), preferred_element_type=f32)


def _dot_tn(a, b):
    return lax.dot_general(a, b, (((0,), (0,)), ((), ())), preferred_element_type=f32)


def _meta_full(sw_ref):
    return jnp.concatenate([sw_ref[s, 0:N_META, :] for s in range(N_DEV)], axis=1)


def _ln_emb(x, smallw, g_e, b_e):
    seq = x.shape[1]
    rows = seq + BLK
    nb = rows // BLK

    def body(x_ref, sw_ref, g_ref, b_ref, h32_ref, hb_ref):
        i = pl.program_id(0)
        g, b = g_ref[...], b_ref[...]

        def emit(blk):
            h32_ref[...] = blk
            hb_ref[...] = blk.astype(bf16)

        @pl.when(i == 0)
        def _():
            hm = _ln_rows(_meta_full(sw_ref), g, b)[0]
            emit(jnp.concatenate([jnp.zeros((ROW0, D), f32), hm], axis=0))

        @pl.when(i > 0)
        def _():
            emit(_ln_rows(x_ref[0], g, b)[0])

    return pl.pallas_call(
        body, grid=(nb,),
        in_specs=[pl.BlockSpec((1, BLK, D), lambda i: (0, jnp.maximum(i - 1, 0), 0)),
                  pl.BlockSpec((N_DEV, 24, 256), lambda i: (0, 0, 0)),
                  pl.BlockSpec((1, D), lambda i: (0, 0)),
                  pl.BlockSpec((1, D), lambda i: (0, 0))],
        out_specs=[pl.BlockSpec((BLK, D), lambda i: (i, 0)),
                   pl.BlockSpec((BLK, D), lambda i: (i, 0))],
        out_shape=[SDS((rows, D), f32), SDS((rows, D), bf16)],
        name="ln_emb", compiler_params=_cp(("arbitrary",)),
    )(x, smallw, g_e, b_e)


def _ln_emb_bwd(dh_lo, dh_hi, du32, x, smallw, g_e, after):
    seq = x.shape[1]
    rows = seq + BLK
    nb = rows // BLK

    def body(dlo_ref, dhi_ref, du_ref, x_ref, sw_ref, g_ref, after_ref, gx_ref, dmeta_ref, st_ref):
        i = pl.program_id(0)
        g = g_ref[...]
        dht = jnp.concatenate([dlo_ref[...], dhi_ref[...]], axis=1) + ALPHA * du_ref[...]

        @pl.when(i == 0)
        def _():
            v = jnp.concatenate([jnp.zeros((ROW0, D), f32), _meta_full(sw_ref)], axis=0)
            valid = lax.broadcasted_iota(jnp.int32, (BLK, 1), 0) >= ROW0
            d = jnp.where(valid, dht, 0.0)
            _, xhat, rstd = _ln_rows(v, g, 0.0)
            dv = _ln_rows_bwd(d, g, xhat, rstd)
            dmeta_ref[...] = dv[ROW0:, :]
            st_ref[...] = jnp.concatenate([_colsum(d * xhat), _colsum(d), jnp.zeros((6, D), f32)], axis=0)

        @pl.when(i > 0)
        def _():
            _, xhat, rstd = _ln_rows(x_ref[0], g, 0.0)
            gx_ref[0] = _ln_rows_bwd(dht, g, xhat, rstd)
            st_ref[0:1, :] += _colsum(dht * xhat)
            st_ref[1:2, :] += _colsum(dht)

    return pl.pallas_call(
        body, grid=(nb,),
        in_specs=[pl.BlockSpec((BLK, W_IN_HALF), lambda i: (i, 0)),
                  pl.BlockSpec((BLK, W_IN_HALF), lambda i: (i, 0)),
                  pl.BlockSpec((BLK, D), lambda i: (i, 0)),
                  pl.BlockSpec((1, BLK, D), lambda i: (0, jnp.maximum(i - 1, 0), 0)),
                  pl.BlockSpec((N_DEV, 24, 256), lambda i: (0, 0, 0)),
                  pl.BlockSpec((1, D), lambda i: (0, 0)),
                  pl.BlockSpec(memory_space=pl.ANY)],
        out_specs=[pl.BlockSpec((1, BLK, D), lambda i: (0, jnp.maximum(i - 1, 0), 0)),
                   pl.BlockSpec((N_META, D), lambda i: (0, 0)),
                   pl.BlockSpec((8, D), lambda i: (0, 0))],
        out_shape=[SDS((1, seq, D), f32), SDS((N_META, D), f32), SDS((8, D), f32)],
        name="ln_emb_bwd", compiler_params=_cp(("arbitrary",)),
    )(dh_lo, dh_hi, du32, x, smallw, g_e, after)


def _mm(a, b, *, name, nt=False, sel=None, bias=None, out_dtype=f32, tn=512):
    m, k = a.shape
    cm = _row_chunk(m)
    stacked = sel is not None
    n = D if stacked else (b.shape[0] if nt else b.shape[1])
    am = m
    if stacked and nt:
        b_spec = pl.BlockSpec((tn // 256, None, 256, D), lambda j, i: (j, sel, 0, 0))
    elif stacked:
        b_spec = pl.BlockSpec((N_DEV, None, 256, tn), lambda j, i: (0, sel, 0, j))
    elif nt:
        b_spec = pl.BlockSpec((tn, k), lambda j, i: (j, 0))
    else:
        b_spec = pl.BlockSpec((k, tn), lambda j, i: (0, j))
    in_specs = [pl.BlockSpec((am, k), lambda j, i: (i, 0)), b_spec]
    args = [a, b]
    if bias is not None:
        in_specs.append(pl.BlockSpec((1, tn), lambda j, i: (0, j)))
        args.append(bias)

    def body(*refs):
        a_ref, b_ref, o_ref = refs[0], refs[1], refs[-1]
        bm = b_ref[...]
        if stacked:
            bm = bm.reshape((tn, D) if nt else (D, tn))
        for c in range(am // cm):
            acc = (_dot_nt if nt else _dot)(a_ref[c * cm:(c + 1) * cm, :], bm)
            if bias is not None:
                acc = acc + refs[2][...]
            o_ref[c * cm:(c + 1) * cm, :] = acc.astype(out_dtype)

    return pl.pallas_call(
        body, grid=(n // tn, m // am), in_specs=in_specs,
        out_specs=pl.BlockSpec((am, tn), lambda j, i: (i, j)),
        out_shape=SDS((m, n), out_dtype), name=name, compiler_params=_cp(("arbitrary", "arbitrary"), 48),
    )(*args)


def _mm_z(hb, w_t, bias, side, name, z_prev=None):
    rows, k = hb.shape
    tn = W_IN_LATE
    cm = _row_chunk(rows)
    per = 2 * SHARD_IN // tn
    if side is None:
        side, count = jnp.zeros((1,), jnp.int32), per - 2
        tile = lambda q, t, s_ref: per * q + 1 + t
    else:
        count = 1
        tile = lambda q, t, s_ref: per * q + (per - 1) * s_ref[0]

    def body(s_ref, a_ref, b_ref, bias_ref, *rest):
        o_ref = rest[-1]
        for c in range(rows // cm):
            o_ref[c * cm:(c + 1) * cm, :] = _dot_nt(a_ref[c * cm:(c + 1) * cm, :], b_ref[...]) + bias_ref[...]

    in_specs = [pl.BlockSpec((rows, k), lambda q, t, s_ref: (0, 0)),
                pl.BlockSpec((tn, k), lambda q, t, s_ref: (tile(q, t, s_ref), 0)),
                pl.BlockSpec((1, tn), lambda q, t, s_ref: (0, tile(q, t, s_ref)))]
    args = [side, hb, w_t, bias]
    if z_prev is not None:
        in_specs.append(pl.BlockSpec(memory_space=pl.ANY))
        args.append(z_prev)
    return pl.pallas_call(
        body,
        grid_spec=pltpu.PrefetchScalarGridSpec(
            num_scalar_prefetch=1, grid=(N_DEV // 2, count), in_specs=in_specs,
            out_specs=pl.BlockSpec((rows, tn), lambda q, t, s_ref: (0, tile(q, t, s_ref)))),
        out_shape=SDS((rows, D_IN), f32), name=name,
        input_output_aliases={} if z_prev is None else {4: 0},
        compiler_params=_cp(("arbitrary", "arbitrary"), 48),
    )(*args)


def _mm_dh(dz, w_t, after, half):
    rows = dz.shape[0]
    tn = 512
    nt = W_IN_HALF // tn
    cm = _row_chunk(rows) // 2

    def body(a_ref, w_ref, after_ref, o_ref):
        o_ref[...] = _dot(a_ref[...], w_ref[...])

    return pl.pallas_call(
        body, grid=(nt, rows // cm),
        in_specs=[pl.BlockSpec((cm, D_IN), lambda j, i: (i, 0)),
                  pl.BlockSpec((D_IN, tn), lambda j, i: (0, half * nt + j)),
                  pl.BlockSpec(memory_space=pl.ANY)],
        out_specs=pl.BlockSpec((cm, tn), lambda j, i: (i, j)),
        out_shape=SDS((rows, W_IN_HALF), f32), name="mm_dh_%d" % half,
        compiler_params=_cp(("arbitrary", "arbitrary"), 48),
    )(dz, w_t, after)


def _mm_dwin_parts(hb, parts):
    rows = hb.shape[0]
    tc = 512
    edges = [0]
    for _, w in parts:
        edges.append(edges[-1] + w // tc)

    def body(*refs):
        h_ref, (o_ref, dz_ref, db_ref) = refs[len(parts)], refs[len(parts) + 1:]
        j = pl.program_id(0)
        for p_ref, lo, hi in zip(refs, edges[:-1], edges[1:]):
            @pl.when((j >= lo) & (j < hi))
            def _():
                o_ref[...] = _dot_tn(p_ref[...], h_ref[...]).astype(bf16)
                dz_ref[...] = p_ref[...]

                def step(i, s):
                    blk = p_ref[pl.ds(pl.multiple_of(i * BLK, BLK), BLK), :].astype(f32)
                    return s + blk.reshape(BLK // 8, 8, tc).sum(axis=0)
                s = lax.fori_loop(0, rows // BLK, step, jnp.zeros((8, tc), f32))
                db_ref[...] = jnp.broadcast_to(_colsum(s), (8, tc))

    in_specs = [pl.BlockSpec((rows, tc), lambda j, lo=lo, hi=hi: (0, jnp.clip(j - lo, 0, hi - lo - 1)))
                for lo, hi in zip(edges[:-1], edges[1:])]
    return pl.pallas_call(
        body, grid=(D_IN // tc,),
        in_specs=in_specs + [pl.BlockSpec((rows, W_IN_HALF), lambda j: (0, 0))],
        out_specs=[pl.BlockSpec((tc, W_IN_HALF), lambda j: (j, 0)), pl.BlockSpec((rows, tc), lambda j: (0, j)),
                   pl.BlockSpec((8, tc), lambda j: (0, j))],
        out_shape=[SDS((D_IN, W_IN_HALF), bf16), SDS((rows, D_IN), bf16), SDS((8, D_IN), f32)],
        name="mm_dwin_0", compiler_params=_cp(("arbitrary",), VMEM_LIMIT_MB),
    )(*[a for a, _ in parts], hb)


def _mm_dwin(hb, dz, after):
    rows = dz.shape[0]
    tc = 640

    def body(dz_ref, h_ref, after_ref, o_ref):
        o_ref[...] = _dot_tn(dz_ref[...], h_ref[...]).astype(bf16)

    return pl.pallas_call(
        body, grid=(D_IN // tc,),
        in_specs=[pl.BlockSpec((rows, tc), lambda j: (0, j)),
                  pl.BlockSpec((rows, W_IN_HALF), lambda j: (0, 1)),
                  pl.BlockSpec(memory_space=pl.ANY)],
        out_specs=pl.BlockSpec((tc, W_IN_HALF), lambda j: (j, 0)),
        out_shape=SDS((D_IN, W_IN_HALF), bf16),
        name="mm_dwin_1", compiler_params=_cp(("arbitrary",), 48),
    )(dz, hb, after)


SCAN_ROWS = 32


def _scan8(a, b, reverse):
    idx = lax.broadcasted_iota(jnp.int32, a.shape, 0)
    for s in (1, 2, 4):
        sh = 8 - s if reverse else s
        a_sh, b_sh = pltpu.roll(a, sh, 0), pltpu.roll(b, sh, 0)
        m = (idx < 8 - s) if reverse else (idx >= s)
        b = jnp.where(m, a * b_sh + b, b)
        a = jnp.where(m, a * a_sh, a)
    return a, b


def _shift_rows(prev8, cur, k):
    ext = jnp.concatenate([prev8, cur], axis=0)
    return pltpu.roll(ext, k, 0)[8:, :]


def _gates(xc, w_ra, b_ra, w_ri, b_ri, ls):
    xb = xc.astype(bf16)
    r = _sigmoid(_dot(xb, w_ra) + b_ra)
    ig = _sigmoid(_dot(xb, w_ri) + b_ri)
    la = LRU_C * r * ls
    a = jnp.exp(la)
    mult = jnp.sqrt(jnp.tanh(-la) * (1.0 + a * a))
    return xb, r, ig, a, mult


_RNN_IN_SPECS = lambda rows: [
    pl.BlockSpec((1, 24, 256), lambda n: (n, 0, 0)),
    pl.BlockSpec((1, RNN_BLOCK), lambda n: (0, n)),
    pl.BlockSpec((N_DEV, 2, None, 32, RNN_BLOCK), lambda n: (0, 0, n, 0, 0)),
    pl.BlockSpec((1, RNN_BLOCK), lambda n: (0, n)),
    pl.BlockSpec((1, RNN_BLOCK), lambda n: (0, n)),
    pl.BlockSpec((1, RNN_BLOCK), lambda n: (0, n)),
]


def _rnn_fwd(z, smallw, conv_b, wrg, b_ra, b_ri, lam):
    rows = z.shape[0]
    nb = rows // BLK
    col = lambda off: pl.BlockSpec((rows, RNN_BLOCK), lambda n: (0, off // RNN_BLOCK + n))

    def body(xr_ref, gr_ref, sw_ref, cb_ref, w_ref, bra_ref, bri_ref, lam_ref, xc_ref, hr_ref, ya_ref, yat_ref, a_s):
        cw = sw_ref[0, N_META:24, :]
        cb = cb_ref[...]
        w_ra = w_ref[:, 0].reshape(RNN_BLOCK, RNN_BLOCK)
        w_ri = w_ref[:, 1].reshape(RNN_BLOCK, RNN_BLOCK)
        b_ra_v, b_ri_v = bra_ref[...], bri_ref[...]
        ls = _log_sigmoid(lam_ref[...])
        rid = lax.broadcasted_iota(jnp.int32, (BLK, 1), 0)

        def blk_step(i, carry):
            r0 = pl.multiple_of(i * BLK, BLK)
            grow = rid + r0
            valid = grow >= ROW0
            cur = jnp.where(valid, xr_ref[pl.ds(r0, BLK), :], 0.0)
            prev8 = xr_ref[pl.ds(pl.multiple_of(jnp.maximum(r0 - 8, 0), 8), 8), :] * (i > 0).astype(f32)
            xc = cb + cw[0:1] * cur
            for k in range(1, CONV_WIDTH):
                xc = xc + cw[k:k + 1] * _shift_rows(prev8, cur, k)
            xc_ref[pl.ds(r0, BLK), :] = xc
            _, _, ig, a, mult = _gates(xc, w_ra, b_ra_v, w_ri, b_ri_v, ls)
            mult = jnp.where(grow == ROW0, 1.0, mult)
            a_s[pl.ds(r0, BLK), :] = a
            hr_ref[pl.ds(r0, BLK), :] = jnp.where(valid, mult * ig * xc, 0.0)
            return carry

        lax.fori_loop(0, nb, blk_step, 0)

        def scan_step(j, carry):
            r0 = pl.multiple_of(j * SCAN_ROWS, SCAN_ROWS)
            tiles = [_scan8(a_s[pl.ds(r0 + 8 * k, 8), :], hr_ref[pl.ds(r0 + 8 * k, 8), :], False)
                     for k in range(SCAN_ROWS // 8)]
            for k, (a, b) in enumerate(tiles):
                h = b + a * carry
                hr_ref[pl.ds(r0 + 8 * k, 8), :] = h
                carry = jnp.broadcast_to(h[7:8, :], (8, RNN_BLOCK))
            return carry

        lax.fori_loop(0, rows // SCAN_ROWS, scan_step, jnp.zeros((8, RNN_BLOCK), f32))

        def gate_step(i, carry):
            r0 = pl.multiple_of(i * BLK, BLK)
            ya_ref[pl.ds(r0, BLK), :] = (hr_ref[pl.ds(r0, BLK), :]
                                         * _silu_and_grad(gr_ref[pl.ds(r0, BLK), :])[0]).astype(bf16)
            return carry

        lax.fori_loop(0, nb, gate_step, 0)
        yat_ref[...] = ya_ref[...].astype(f32).T.astype(bf16)

    return pl.pallas_call(
        body, grid=(N_RNN_BLOCKS,),
        in_specs=[col(0), col(OFF_GR)] + _RNN_IN_SPECS(rows),
        out_specs=[pl.BlockSpec((rows, RNN_BLOCK), lambda n: (0, n))] * 3
                  + [pl.BlockSpec((RNN_BLOCK, rows), lambda n: (n, 0))],
        out_shape=[SDS((rows, D), f32), SDS((rows, D), f32), SDS((rows, D), bf16), SDS((D, rows), bf16)],
        scratch_shapes=[pltpu.VMEM((rows, RNN_BLOCK), f32)],
        name="rnn_fwd", compiler_params=_cp(("arbitrary",)),
    )(z, z, smallw, conv_b, wrg, b_ra, b_ri, lam)


def _rnn_bwd(dya, hr, xc, z, smallw, conv_b, wrg, b_ra, b_ri, lam):
    rows = z.shape[0]
    nb = rows // BLK
    col = lambda off: pl.BlockSpec((rows, RNN_BLOCK), lambda n: (0, off // RNN_BLOCK + n))
    blk = pl.BlockSpec((rows, RNN_BLOCK), lambda n: (0, n))

    def body(dya_ref, hr_ref, xc_ref, xr_ref, gr_ref, sw_ref, cb_ref, w_ref, bra_ref, bri_ref, lam_ref,
             dxr_ref, dgr_ref, dw_ref, vec_ref, a_s, lam_s, dxc_s, r_s, ig_s, mult_s, dw_s):
        cw = sw_ref[0, N_META:24, :]
        w_ra = w_ref[:, 0].reshape(RNN_BLOCK, RNN_BLOCK)
        w_ri = w_ref[:, 1].reshape(RNN_BLOCK, RNN_BLOCK)
        b_ra_v, b_ri_v = bra_ref[...], bri_ref[...]
        lam_v = lam_ref[...]
        ls = _log_sigmoid(lam_v)
        rid = lax.broadcasted_iota(jnp.int32, (BLK, 1), 0)
        zrow = jnp.zeros((1, RNN_BLOCK), f32)

        def p1(i, carry):
            r0 = pl.multiple_of(i * BLK, BLK)
            sl = pl.ds(r0, BLK)
            _, r, ig, a, mult = _gates(xc_ref[sl, :], w_ra, b_ra_v, w_ri, b_ri_v, ls)
            a_s[sl, :] = a
            r_s[sl, :] = r
            ig_s[sl, :] = ig
            mult_s[sl, :] = mult
            sg, dsg = _silu_and_grad(gr_ref[sl, :])
            d = dya_ref[sl, :]
            lam_s[sl, :] = d * sg
            dgr_ref[sl, :] = (d * hr_ref[sl, :] * dsg).astype(bf16)
            return carry

        lax.fori_loop(0, nb, p1, 0)

        def p2(jj, carry):
            r0 = pl.multiple_of((rows // SCAN_ROWS - 1 - jj) * SCAN_ROWS, SCAN_ROWS)
            idx = lax.broadcasted_iota(jnp.int32, (8, RNN_BLOCK), 0)
            tiles = []
            for k in range(SCAN_ROWS // 8):
                sl = pl.ds(r0 + 8 * k, 8)
                a, g = a_s[sl, :], lam_s[sl, :]
                tiles.append((g, *_scan8(a, a * g, True)))
            for k in reversed(range(SCAN_ROWS // 8)):
                g, ca, cb_ = tiles[k]
                mu = cb_ + ca * carry
                lam_s[pl.ds(r0 + 8 * k, 8), :] = g + jnp.where(idx < 7, pltpu.roll(mu, 7, 0), carry)
                carry = jnp.broadcast_to(mu[0:1, :], (8, RNN_BLOCK))
            return carry

        lax.fori_loop(0, rows // SCAN_ROWS, p2, jnp.zeros((8, RNN_BLOCK), f32))

        dw_s[...] = jnp.zeros_like(dw_s)

        def p3(i, carry):
            d_bra, d_bri, d_ls = carry
            r0 = pl.multiple_of(i * BLK, BLK)
            sl = pl.ds(r0, BLK)
            grow = rid + r0
            valid = grow >= ROW0
            first = grow == ROW0
            xcv = xc_ref[sl, :]
            xb = xcv.astype(bf16)
            r, ig, a = r_s[sl, :], ig_s[sl, :], a_s[sl, :]
            mult = jnp.where(first, 1.0, mult_s[sl, :])
            lam_t = lam_s[sl, :]
            du = jnp.where(valid, lam_t, 0.0)
            hprev = _shift_rows(hr_ref[pl.ds(pl.multiple_of(jnp.maximum(r0 - 8, 0), 8), 8), :] * (i > 0).astype(f32), hr_ref[sl, :], 1)
            da = lam_t * hprev
            dmult = jnp.where(first, 0.0, du * ig * xcv)
            di = du * mult * xcv
            dxc = du * mult * ig
            ratio = jnp.where(valid & jnp.logical_not(first), a * a / mult, 0.0)
            dla = da * a - dmult * ratio
            dpr = (dla * (LRU_C * ls)) * r * (1.0 - r)
            dpi = di * ig * (1.0 - ig)
            dprb, dpib = dpr.astype(bf16), dpi.astype(bf16)
            dw_s[0] += _dot_tn(xb, dprb)
            dw_s[1] += _dot_tn(xb, dpib)
            dxc_s[sl, :] = dxc + _dot_nt(dprb, w_ra) + _dot_nt(dpib, w_ri)
            return d_bra + _colsum(dpr), d_bri + _colsum(dpi), d_ls + _colsum(dla * (LRU_C * r))

        d_bra, d_bri, d_ls = lax.fori_loop(0, nb, p3, (zrow, zrow, zrow))

        def p4(i, carry):
            d_cb, d_w0, d_w1, d_w2, d_w3 = carry
            r0 = pl.multiple_of(i * BLK, BLK)
            sl = pl.ds(r0, BLK)
            grow = rid + r0
            valid = grow >= ROW0
            dxc = dxc_s[sl, :]
            nxt = dxc_s[pl.ds(pl.multiple_of(jnp.minimum(r0 + BLK, rows - 8), 8), 8), :] * (i < nb - 1).astype(f32)
            ext = jnp.concatenate([dxc, nxt], axis=0)
            dxr = cw[0:1] * dxc
            for k in range(1, CONV_WIDTH):
                dxr = dxr + cw[k:k + 1] * pltpu.roll(ext, BLK + 8 - k, 0)[:BLK, :]
            dxr_ref[sl, :] = jnp.where(valid, dxr, 0.0).astype(bf16)
            cur = jnp.where(valid, xr_ref[sl, :], 0.0)
            prev8 = xr_ref[pl.ds(pl.multiple_of(jnp.maximum(r0 - 8, 0), 8), 8), :] * (i > 0).astype(f32)
            dws = [d_w0 + _colsum(dxc * cur)]
            for k, acc in ((1, d_w1), (2, d_w2), (3, d_w3)):
                dws.append(acc + _colsum(dxc * _shift_rows(prev8, cur, k)))
            return (d_cb + _colsum(dxc), *dws)

        d_cb, d_w0, d_w1, d_w2, d_w3 = lax.fori_loop(0, nb, p4, (zrow,) * 5)

        d_lam = d_ls * _sigmoid(-lam_v)
        vec_ref[...] = jnp.concatenate([d_bra, d_bri, d_lam, d_cb, d_w0, d_w1, d_w2, d_w3], axis=0)
        dw_ref[:, 0] = dw_s[0].astype(bf16).reshape(N_DEV, 32, RNN_BLOCK)
        dw_ref[:, 1] = dw_s[1].astype(bf16).reshape(N_DEV, 32, RNN_BLOCK)

    return pl.pallas_call(
        body, grid=(N_RNN_BLOCKS,),
        in_specs=[blk, blk, blk, col(0), col(OFF_GR)] + _RNN_IN_SPECS(rows),
        out_specs=[blk, blk,
                   pl.BlockSpec((N_DEV, 2, None, 32, RNN_BLOCK), lambda n: (0, 0, n, 0, 0)),
                   pl.BlockSpec((8, RNN_BLOCK), lambda n: (0, n))],
        out_shape=[SDS((rows, D), bf16), SDS((rows, D), bf16),
                   SDS((N_DEV, 2, N_RNN_BLOCKS, 32, RNN_BLOCK), bf16), SDS((8, D), f32)],
        scratch_shapes=[pltpu.VMEM((rows, RNN_BLOCK), f32)] * 6 + [pltpu.VMEM((2, RNN_BLOCK, RNN_BLOCK), f32)],
        name="rnn_bwd", compiler_params=_cp(("arbitrary",), 48),
    )(dya, hr, xc, z, z, smallw, conv_b, wrg, b_ra, b_ri, lam)


def _rope_tables(rows):
    half = jnp.arange(HALF, dtype=f32)
    inv = ROPE_THETA ** (-half / HALF)
    pos = (jnp.arange(rows) - ROW0).astype(f32)
    ang = pos[:, None] * inv[None, :]
    cos, sin = jnp.cos(ang), jnp.sin(ang)
    cos128 = jnp.concatenate([cos, cos, cos, cos], axis=1)
    sin128 = jnp.concatenate([-sin, sin, -sin, sin], axis=1)
    return cos128, sin128


def _rope128(x, cos128, sin128):
    lane = lax.broadcasted_iota(jnp.int32, x.shape, 1)
    swapped = jnp.where(lane % HEAD_DIM < HALF, pltpu.roll(x, 128 - HALF, 1), pltpu.roll(x, HALF, 1))
    return x * cos128 + swapped * sin128


def _qkv_prep(z, cos128, sin128):
    rows = z.shape[0]

    def body(q_ref, kv_ref, c_ref, s_ref, qo_ref, ko_ref, vo_ref):
        c, s = c_ref[...], s_ref[...]
        for g in range(D // 128):
            qo_ref[:, g * 128:(g + 1) * 128] = (_rope128(q_ref[:, g * 128:(g + 1) * 128], c, s)
                                                * (HEAD_DIM ** -0.5)).astype(bf16)
        for g in range(2):
            kr = _rope128(kv_ref[:, g * 128:(g + 1) * 128], c, s)
            for j in range(2):
                ko_ref[2 * g + j] = kr[:, j * HEAD_DIM:(j + 1) * HEAD_DIM].astype(bf16)
        for h in range(N_KV):
            vo_ref[h] = kv_ref[:, 256 + h * HEAD_DIM:256 + (h + 1) * HEAD_DIM].astype(bf16)

    return pl.pallas_call(
        body, grid=(rows // BLK,),
        in_specs=[pl.BlockSpec((BLK, D), lambda i: (i, OFF_Q // D)),
                  pl.BlockSpec((BLK, 512), lambda i: (i, OFF_K // 512)),
                  pl.BlockSpec((BLK, 128), lambda i: (i, 0)),
                  pl.BlockSpec((BLK, 128), lambda i: (i, 0))],
        out_specs=[pl.BlockSpec((BLK, D), lambda i: (i, 0)),
                   pl.BlockSpec((N_KV, BLK, HEAD_DIM), lambda i: (0, i, 0)),
                   pl.BlockSpec((N_KV, BLK, HEAD_DIM), lambda i: (0, i, 0))],
        out_shape=[SDS((rows, D), bf16), SDS((N_KV, rows, HEAD_DIM), bf16), SDS((N_KV, rows, HEAD_DIM), bf16)],
        name="qkv_prep", compiler_params=_cp(("arbitrary",)),
    )(z, z, cos128, sin128)


def _attn_mask(n):
    qi = n * BLK + lax.broadcasted_iota(jnp.int32, (BLK, 2 * BLK + N_META), 0)
    c = lax.broadcasted_iota(jnp.int32, (BLK, 2 * BLK + N_META), 1)
    jb = (n - 1) * BLK + c
    band = (jb >= BLK) & (jb <= qi) & (qi - jb < BLK)
    meta = (ROW0 + c - 2 * BLK) <= qi
    return ((c < 2 * BLK) & band) | ((c >= 2 * BLK) & meta)


N_KEYS = 2 * BLK + N_META


def _stack_heads(t):
    return jnp.concatenate([t[:, g * HEAD_DIM:(g + 1) * HEAD_DIM] for g in range(GROUP)], axis=0)


def _sink_column(sink_ref, h):
    g = lax.broadcasted_iota(jnp.int32, (GROUP, 1, 1), 0)
    col = jnp.zeros((GROUP, 1, 1), f32)
    for j in range(GROUP):
        col = jnp.where(g == j, sink_ref[h * GROUP + j], col)
    return col


def _kv_specs(last):
    cl = lambda n: jnp.minimum(n, last)
    return [pl.BlockSpec((None, N_META, HEAD_DIM), lambda h, n: (h, ROW0 // N_META, 0)),
            pl.BlockSpec((None, BLK, HEAD_DIM), lambda h, n: (h, jnp.maximum(cl(n) - 1, 0), 0)),
            pl.BlockSpec((None, BLK, HEAD_DIM), lambda h, n: (h, cl(n), 0))]


def _attn_fwd(q_r, k_r, v_b, z, sinks):
    rows = q_r.shape[0]
    nb = rows // BLK

    def body(sink_ref, q_ref, km_ref, kp_ref, kc_ref, vm_ref, vp_ref, vc_ref, ga_ref, o_ref, yb_ref, ybt_ref, lse_ref):
        h, n = pl.program_id(0), pl.program_id(1)
        kk = jnp.concatenate([kp_ref[...], kc_ref[...], km_ref[...]], axis=0)
        vv = jnp.concatenate([vp_ref[...], vc_ref[...], vm_ref[...]], axis=0)
        q2 = _stack_heads(q_ref[...])
        s = jnp.where(_attn_mask(n)[None], _dot_nt(q2, kk).reshape(GROUP, BLK, N_KEYS), NEG_INF)
        sink = _sink_column(sink_ref, h)
        m = jnp.maximum(jnp.max(s, axis=-1, keepdims=True), sink)
        p = jnp.exp(s - m)
        den = jnp.sum(p, axis=-1, keepdims=True) + jnp.exp(sink - m)
        o2 = _dot((p / den).astype(bf16).reshape(GROUP * BLK, N_KEYS), vv)
        lse = m + jnp.log(den)
        for g in range(GROUP):
            o_ref[:, g * HEAD_DIM:(g + 1) * HEAD_DIM] = o2[g * BLK:(g + 1) * BLK]
            lse_ref[:, g:g + 1] = lse[g]
        yb = o_ref[...] * _silu_and_grad(ga_ref[...])[0]
        yb_ref[...] = yb.astype(bf16)
        ybt_ref[...] = yb.T.astype(bf16)

    tile = pl.BlockSpec((BLK, 512), lambda h, n: (n, h))
    return pl.pallas_call(
        body, grid=(N_KV, nb),
        in_specs=[pl.BlockSpec(memory_space=pltpu.SMEM), tile] + _kv_specs(nb - 1) + _kv_specs(nb - 1)
                 + [pl.BlockSpec((BLK, 512), lambda h, n: (n, OFF_GA // 512 + h))],
        out_specs=[tile, tile, pl.BlockSpec((512, BLK), lambda h, n: (h, n)),
                   pl.BlockSpec((None, BLK, GROUP), lambda h, n: (h, n, 0))],
        out_shape=[SDS((rows, D), f32), SDS((rows, D), bf16), SDS((D, rows), bf16),
                   SDS((N_KV, rows, GROUP), f32)],
        name="attn_fwd", compiler_params=_cp(("arbitrary", "arbitrary")),
    )(sinks, q_r, k_r, k_r, k_r, v_b, v_b, v_b, z)


def _attn_bwd(dyb, o32, lse, q_r, k_r, v_b, z, sinks):
    rows = q_r.shape[0]
    nb = rows // BLK
    cl = lambda n: jnp.minimum(n, nb - 1)

    def body(sink_ref, dyb_ref, o_ref, lse_ref, q_ref, km_ref, kp_ref, kc_ref, vm_ref, vp_ref, vc_ref, ga_ref,
             dq_ref, dga_ref, dk_ref, dv_ref, dkm_ref, dvm_ref, dsr_ref, ck_s, cv_s):
        h, n = pl.program_id(0), pl.program_id(1)

        @pl.when(n == 0)
        def _():
            dkm_ref[...] = jnp.zeros_like(dkm_ref)
            dvm_ref[...] = jnp.zeros_like(dvm_ref)
            ck_s[...] = jnp.zeros_like(ck_s)
            cv_s[...] = jnp.zeros_like(cv_s)

        @pl.when(n < nb)
        def _():
            kk = jnp.concatenate([kp_ref[...], kc_ref[...], km_ref[...]], axis=0)
            vv = jnp.concatenate([vp_ref[...], vc_ref[...], vm_ref[...]], axis=0)
            sg, dsg = _silu_and_grad(ga_ref[...])
            dyb_v = dyb_ref[...]
            o_v = o_ref[...]
            dga_ref[...] = (dyb_v * o_v * dsg).astype(bf16)
            q2 = _stack_heads(q_ref[...])
            do2 = _stack_heads(dyb_v * sg)
            lse_v = lse_ref[...]
            lse = jnp.concatenate([lse_v[:, g:g + 1] for g in range(GROUP)], axis=0).reshape(GROUP, BLK, 1)
            delta = jnp.sum(do2 * _stack_heads(o_v), axis=-1, keepdims=True).reshape(GROUP, BLK, 1)
            s = jnp.where(_attn_mask(n)[None], _dot_nt(q2, kk).reshape(GROUP, BLK, N_KEYS), NEG_INF)
            p = jnp.exp(s - lse)
            do2b = do2.astype(bf16)
            ds = (p * (_dot_nt(do2b, vv).reshape(GROUP, BLK, N_KEYS) - delta)).astype(bf16)
            ds = ds.reshape(GROUP * BLK, N_KEYS)
            dsr = -jnp.exp(_sink_column(sink_ref, h) - lse) * delta
            dq2 = _dot(ds, kk)
            for g in range(GROUP):
                dq_ref[:, g * HEAD_DIM:(g + 1) * HEAD_DIM] = dq2[g * BLK:(g + 1) * BLK]
                dsr_ref[:, g:g + 1] = dsr[g]
            dkk = _dot_tn(ds, q2)
            dvv = _dot_tn(p.astype(bf16).reshape(GROUP * BLK, N_KEYS), do2b)
            dk_ref[...] = ck_s[...] + dkk[:BLK]
            dv_ref[...] = cv_s[...] + dvv[:BLK]
            ck_s[...] = dkk[BLK:2 * BLK]
            cv_s[...] = dvv[BLK:2 * BLK]
            dkm_ref[...] += dkk[2 * BLK:]
            dvm_ref[...] += dvv[2 * BLK:]

        @pl.when(n == nb)
        def _():
            dk_ref[...] = ck_s[...]
            dv_ref[...] = cv_s[...]

    tile = pl.BlockSpec((BLK, 512), lambda h, n: (cl(n), h))
    kvout = pl.BlockSpec((None, BLK, HEAD_DIM), lambda h, n: (h, jnp.maximum(n - 1, 0), 0))
    mout = pl.BlockSpec((None, N_META, HEAD_DIM), lambda h, n: (h, 0, 0))
    stat = pl.BlockSpec((None, BLK, GROUP), lambda h, n: (h, cl(n), 0))
    return pl.pallas_call(
        body, grid=(N_KV, nb + 1),
        in_specs=[pl.BlockSpec(memory_space=pltpu.SMEM), tile, tile, stat, tile] + _kv_specs(nb - 1)
                 + _kv_specs(nb - 1) + [pl.BlockSpec((BLK, 512), lambda h, n: (cl(n), OFF_GA // 512 + h))],
        out_specs=[tile, tile, kvout, kvout, mout, mout, stat],
        out_shape=[SDS((rows, D), f32), SDS((rows, D), bf16),
                   SDS((N_KV, rows, HEAD_DIM), f32), SDS((N_KV, rows, HEAD_DIM), f32),
                   SDS((N_KV, N_META, HEAD_DIM), f32), SDS((N_KV, N_META, HEAD_DIM), f32),
                   SDS((N_KV, rows, GROUP), f32)],
        scratch_shapes=[pltpu.VMEM((BLK, HEAD_DIM), f32), pltpu.VMEM((BLK, HEAD_DIM), f32)],
        name="attn_bwd", compiler_params=_cp(("arbitrary", "arbitrary")),
    )(sinks, dyb, o32, lse, q_r, k_r, k_r, k_r, v_b, v_b, v_b, z)


def _qkv_finish(dq, dk, dv, dkm, dvm, cos128, sin128):
    rows = dq.shape[0]

    def body(dq_ref, dk_ref, dv_ref, dkm_ref, dvm_ref, c_ref, s_ref, oq_ref, okv_ref):
        first = (pl.program_id(0) == 0).astype(f32)
        c, s = c_ref[...], -s_ref[...]
        for g in range(D // 128):
            oq_ref[:, g * 128:(g + 1) * 128] = (_rope128(dq_ref[:, g * 128:(g + 1) * 128], c, s)
                                                * (HEAD_DIM ** -0.5)).astype(bf16)
        pad = jnp.zeros((ROW0, HEAD_DIM), f32)
        ks = [dk_ref[h] + first * jnp.concatenate([pad, dkm_ref[h]], axis=0) for h in range(N_KV)]
        vs = [dv_ref[h] + first * jnp.concatenate([pad, dvm_ref[h]], axis=0) for h in range(N_KV)]
        for g in range(2):
            kp = jnp.concatenate([ks[2 * g], ks[2 * g + 1]], axis=1)
            okv_ref[:, g * 128:(g + 1) * 128] = _rope128(kp, c, s).astype(bf16)
            okv_ref[:, 256 + g * 128:256 + (g + 1) * 128] = jnp.concatenate([vs[2 * g], vs[2 * g + 1]], axis=1).astype(bf16)

    kv = pl.BlockSpec((N_KV, BLK, HEAD_DIM), lambda i: (0, i, 0))
    mt = pl.BlockSpec((N_KV, N_META, HEAD_DIM), lambda i: (0, 0, 0))
    return pl.pallas_call(
        body, grid=(rows // BLK,),
        in_specs=[pl.BlockSpec((BLK, D), lambda i: (i, 0)), kv, kv, mt, mt,
                  pl.BlockSpec((BLK, 128), lambda i: (i, 0)), pl.BlockSpec((BLK, 128), lambda i: (i, 0))],
        out_specs=[pl.BlockSpec((BLK, D), lambda i: (i, 0)), pl.BlockSpec((BLK, 512), lambda i: (i, 0))],
        out_shape=[SDS((rows, D), bf16), SDS((rows, 512), bf16)],
        name="qkv_finish", compiler_params=_cp(("arbitrary",)),
    )(dq, dk, dv, dkm, dvm, cos128, sin128)


_TW = 512


def _mix_specs(rows):
    tr = _row_chunk(rows)
    tile = pl.BlockSpec((tr, _TW), lambda i, j: (i, j))
    ga = pl.BlockSpec((tr, _TW), lambda i, j: (i, OFF_G // _TW + j))
    gb = pl.BlockSpec((tr, _TW), lambda i, j: (i, (OFF_G + D) // _TW + j))
    return (rows // tr, D // _TW), tile, ga, gb


def _mix_fwd(y_a, y_b, z):
    rows = y_a.shape[0]
    tw = 256
    col = lambda off: pl.BlockSpec((rows, tw), lambda j: (0, off // tw + j))

    def body(ya_ref, yb_ref, ga_ref, gb_ref, o_ref, ot_ref):
        mixed = (_sigmoid(ga_ref[...]) * ya_ref[...].astype(f32)
                 + _sigmoid(gb_ref[...]) * yb_ref[...].astype(f32))
        o_ref[...] = mixed.astype(bf16)
        ot_ref[...] = mixed.T.astype(bf16)

    return pl.pallas_call(
        body, grid=(D // tw,), in_specs=[col(0), col(0), col(OFF_G), col(OFF_G + D)],
        out_specs=[col(0), pl.BlockSpec((tw, rows), lambda j: (j, 0))],
        out_shape=[SDS((rows, D), bf16), SDS((D, rows), bf16)],
        name="mix_fwd", compiler_params=_cp(("arbitrary",)),
    )(y_a, y_b, z, z)


def _mix_bwd(dmixed, y_a, y_b, z):
    rows = y_a.shape[0]
    grid, _mix_tile, _mix_ga, _mix_gb = _mix_specs(rows)

    def body(dm_ref, ya_ref, yb_ref, ga_ref, gb_ref, dya_ref, dyb_ref, dga_ref, dgb_ref):
        dm = dm_ref[...].astype(f32)
        sa, sb = _sigmoid(ga_ref[...]), _sigmoid(gb_ref[...])
        dya_ref[...] = (dm * sa).astype(bf16)
        dyb_ref[...] = (dm * sb).astype(bf16)
        dga_ref[...] = (dm * ya_ref[...].astype(f32) * sa * (1.0 - sa)).astype(bf16)
        dgb_ref[...] = (dm * yb_ref[...].astype(f32) * sb * (1.0 - sb)).astype(bf16)

    return pl.pallas_call(
        body, grid=grid, in_specs=[_mix_tile, _mix_tile, _mix_tile, _mix_ga, _mix_gb],
        out_specs=[_mix_tile] * 4, out_shape=[SDS((rows, D), bf16)] * 4,
        name="mix_bwd", compiler_params=_cp(("arbitrary", "arbitrary")),
    )(dmixed, y_a, y_b, z, z)


def _final_ln(out32, h32, tgt, ln_g, ln_b):
    rows = out32.shape[0]

    def body(o_ref, h_ref, t_ref, g_ref, b_ref, du_ref, dub_ref, st_ref):
        i = pl.program_id(0)
        g = g_ref[...]
        y, xhat, rstd = _ln_rows(ALPHA * h_ref[...] + o_ref[...], g, b_ref[...])
        e = jnp.where(i > 0, y - t_ref[0], 0.0)
        dy = e * (1.0 / D)
        du = _ln_rows_bwd(dy, g, xhat, rstd)
        du_ref[...] = du
        dub_ref[...] = du.astype(bf16)
        st = jnp.concatenate([_colsum(dy * xhat), _colsum(dy), _colsum(du), _colsum(e * e) * (0.5 / D),
                              jnp.zeros((4, D), f32)], axis=0)

        @pl.when(i == 0)
        def _():
            st_ref[...] = st

        @pl.when(i > 0)
        def _():
            st_ref[...] += st

    row = pl.BlockSpec((BLK, D), lambda i: (i, 0))
    vec = pl.BlockSpec((1, D), lambda i: (0, 0))
    return pl.pallas_call(
        body, grid=(rows // BLK,),
        in_specs=[row, row, pl.BlockSpec((1, BLK, D), lambda i: (0, jnp.maximum(i - 1, 0), 0)), vec, vec],
        out_specs=[row, row, pl.BlockSpec((8, D), lambda i: (0, 0))],
        out_shape=[SDS((rows, D), f32), SDS((rows, D), bf16), SDS((8, D), f32)],
        name="final_ln", compiler_params=_cp(("arbitrary",)),
    )(out32, h32, tgt, ln_g, ln_b)


def _step_rnn(h32, hb, z, wrg, smallw, p, zero):
    rows = z.shape[0]
    cos128, sin128 = _rope_tables(rows)
    cos128 = cos128 + zero
    xc, hr, ya, ya_t = _rnn_fwd(z, smallw, p["conv_b"] + zero, wrg, p["b_ra"], p["b_ri"], p["lru_lambda"])
    q_r, k_r, v_b = _qkv_prep(z, cos128, sin128)
    return dict(cos128=cos128, sin128=sin128, h32=h32, hb=hb, z=z, xc=xc, hr=hr, ya=ya, ya_t=ya_t,
                q_r=q_r, k_r=k_r, v_b=v_b)


def _step_attn(s, p, zero):
    sinks = p["sinks"].reshape(N_KV * GROUP) + zero[0]
    o32, yb, yb_t, lse = _attn_fwd(s["q_r"], s["k_r"], s["v_b"], s["z"], sinks)
    return dict(s, sinks=sinks, o32=o32, yb=yb, yb_t=yb_t, lse=lse)


def _step_merge(s, tgt, w3, p):
    ya, yb, z = s["ya"], s["yb"], s["z"]
    y_a = _mm(ya, w3, sel=0, out_dtype=bf16, name="mm_ya")
    y_b = _mm(yb, w3, sel=1, out_dtype=bf16, name="mm_yb")
    mixed, mixed_t = _mix_fwd(y_a, y_b, z)
    out32 = _mm(mixed, w3, sel=2, bias=p["b_o"], name="mm_out")
    du32, dub, st_out = _final_ln(out32, s["h32"], tgt, p["ln_g"], p["ln_b"])

    g_wo = _mm(mixed_t, dub, out_dtype=bf16, name="mm_dwo")
    dmixed = _mm(dub, w3, sel=2, nt=True, out_dtype=bf16, name="mm_dmixed")
    dya_b, dyb_b, dma, dmb = _mix_bwd(dmixed, y_a, y_b, z)
    g_wrnn = _mm(s["ya_t"], dya_b, out_dtype=bf16, name="mm_dwrnn")
    g_wattn = _mm(s["yb_t"], dyb_b, out_dtype=bf16, name="mm_dwattn")
    dya = _mm(dya_b, w3, sel=0, nt=True, name="mm_dya")
    dyb = _mm(dyb_b, w3, sel=1, nt=True, name="mm_dyb")
    return dict(du32=du32, st_out=st_out, dma=dma, dmb=dmb, dya=dya, dyb=dyb, g_wo=g_wo, g_wrnn=g_wrnn,
                g_wattn=g_wattn)


def _step_backward(s, t, wrg, smallw, p, conv_b):
    z = s["z"]
    dxr, dgr, g_wrg, vec_rnn = _rnn_bwd(t["dya"], s["hr"], s["xc"], z, smallw, conv_b, wrg, p["b_ra"], p["b_ri"],
                                        p["lru_lambda"])
    dq_r, dga, dk, dv, dkm, dvm, dsr = _attn_bwd(t["dyb"], s["o32"], s["lse"], s["q_r"], s["k_r"], s["v_b"], z,
                                                 s["sinks"])
    dq, dkv = _qkv_finish(dq_r, dk, dv, dkm, dvm, s["cos128"], s["sin128"])
    dz_parts = [(dxr, D), (dgr, D), (dq, D), (dkv, 512), (dga, D), (t["dma"], D), (t["dmb"], D)]
    return dict(vec_rnn=vec_rnn, dsr=dsr, g_wrg=g_wrg, dz_parts=dz_parts)


def _step_input_grad(dh_lo, dh_hi, du32, x, smallw, p, after):
    grad_x, dmeta, st_emb = _ln_emb_bwd(dh_lo, dh_hi, du32, x, smallw, p["ln_emb_g"], after)
    return dict(grad_x=grad_x, dmeta=dmeta, st_emb=st_emb)


_ANY = pl.BlockSpec(memory_space=pl.ANY)
_VMEM = pl.BlockSpec(memory_space=pltpu.VMEM)
_HBM = pl.BlockSpec(memory_space=pltpu.HBM)
_SEM = pl.BlockSpec(memory_space=pltpu.SEMAPHORE)


def _place():
    x, y, c = lax.axis_index("x"), lax.axis_index("y"), lax.axis_index("c")
    return x, y, c


def _dev(px, py, pc):
    return 4 * px + 2 * py + pc


def _tile_rows(r):
    return max(t for t in range(16, 321, 16) if r % t == 0) if r > 320 else r


def _cast_w_in(w_in_t):
    tm = _tile_rows(SHARD_IN)

    def body(i_ref, o_ref):
        o_ref[...] = i_ref[...].astype(bf16)

    return pl.pallas_call(
        body, grid=(SHARD_IN // tm,),
        in_specs=[pl.BlockSpec((tm, D), lambda i: (i, 0))],
        out_specs=pl.BlockSpec((tm, D), lambda i: (i, 0)),
        out_shape=SDS((SHARD_IN, D), bf16), name="cast_w_in", compiler_params=_cp(("arbitrary",)),
    )(w_in_t)


def _cast_small(w_rnn_out, w_attn_out, w_o, w_ra, w_ri, meta, conv_w):
    def body(a_ref, b_ref, c_ref, ra_ref, ri_ref, m_ref, cw_ref, w3_ref, wrg_ref, sw_ref):
        w3_ref[0] = a_ref[0].astype(bf16)
        w3_ref[1] = b_ref[0].astype(bf16)
        w3_ref[2] = c_ref[0].astype(bf16)
        wrg_ref[0] = ra_ref[0].astype(bf16)
        wrg_ref[1] = ri_ref[0].astype(bf16)
        sw_ref[...] = jnp.concatenate([m_ref[...], cw_ref[0], jnp.zeros((4, 256), f32)], axis=0)

    return pl.pallas_call(
        body,
        out_shape=[SDS((3, 256, D), bf16), SDS((2, N_RNN_BLOCKS, 32, RNN_BLOCK), bf16), SDS((24, 256), f32)],
        name="cast_small", compiler_params=_cp(None),
    )(w_rnn_out, w_attn_out, w_o, w_ra, w_ri, meta, conv_w)


def _w_in_rows(core, early):
    if early:
        return pl.ds(pl.multiple_of((1 - core) * W_IN_LATE, 64), W_IN_EARLY)
    return pl.ds(pl.multiple_of(core * W_IN_EARLY, 64), W_IN_LATE)


def _remote(src, dst, send_sems, recv_sems, k, to):
    return pltpu.make_async_remote_copy(src_ref=src, dst_ref=dst, send_sem=send_sems.at[k], recv_sem=recv_sems.at[k],
                                        device_id=to, device_id_type=MESH)


def _all_gather(shards, later):
    n = len(shards)
    nl = len(later)

    def body(*refs):
        ins, outs = refs[:n], refs[n + nl:2 * n + nl]
        send_sems, recv_sems, local_sems = refs[2 * (n + nl):]
        x, y, c = _place()
        me, sibling = (x, y, c), (x, y, 1 - c)
        chips = [(1 - x, y), (x, 1 - y), (1 - x, 1 - y)]

        def copy(a, k, block, to, src=None):
            dst = outs[a].at[_dev(*block)]
            src = dst if src is None else src
            if a == 0:
                rows = _w_in_rows(block[2], True)
                src, dst = src.at[rows], dst.at[rows]
            return pltpu.make_async_remote_copy(
                src_ref=src, dst_ref=dst,
                send_sem=send_sems.at[a * 7 + k], recv_sem=recv_sems.at[a * 7 + k],
                device_id=to, device_id_type=MESH)

        all_ins, all_outs = refs[:n + nl], refs[n + nl:2 * (n + nl)]
        mine = [pltpu.make_async_copy(all_ins[a], all_outs[a].at[_dev(*me)], local_sems.at[a]) for a in range(n + nl)]
        for cp in mine:
            cp.start()
        first = []
        for a in range(n):
            first.append(copy(a, 0, me, sibling, src=ins[a]))
            first += [copy(a, 1 + j, me, (*chip, c), src=ins[a]) for j, chip in enumerate(chips)]
        for cp in first:
            cp.start()
        passed = []
        for a in range(n):
            for j, chip in enumerate(chips):
                copy(a, 1 + j, (*chip, c), me).wait_recv()
                cp = copy(a, 4 + j, (*chip, c), sibling)
                cp.start()
                passed.append(cp)
        for a in range(n):
            copy(a, 0, sibling, me).wait_recv()
            for j, chip in enumerate(chips):
                copy(a, 4 + j, (*chip, 1 - c), me).wait_recv()
        for cp in first + passed:
            cp.wait_send()
        for cp in mine:
            cp.wait()

    return pl.pallas_call(
        body, in_specs=[_ANY] * (n + nl), out_specs=[_ANY] * (n + nl),
        out_shape=[SDS((N_DEV, *s.shape), s.dtype) for s in (*shards, *later)],
        scratch_shapes=[pltpu.SemaphoreType.DMA((7 * n,)), pltpu.SemaphoreType.DMA((7 * n,)),
                        pltpu.SemaphoreType.DMA((n + nl,))],
        name="all_gather_weights",
    )(*shards, *later)


def _copies_late_own(srcs, lands, send_sems, recv_sems):
    x, y, c = _place()
    blk = srcs[0].at[_dev(x, y, c)].at[_w_in_rows(c, False)]
    peers = [(x, y, 1 - c), (1 - x, y, c), (x, 1 - y, c), (1 - x, 1 - y, c)]
    return [_remote(blk, blk, send_sems, recv_sems, k, to) for k, to in enumerate(peers)]


def _copies_late_pass(srcs, lands, send_sems, recv_sems):
    x, y, c = _place()
    out = []
    for j, chip in enumerate([(1 - x, y), (x, 1 - y), (1 - x, 1 - y)]):
        blk = srcs[0].at[_dev(*chip, c)].at[_w_in_rows(c, False)]
        out.append(_remote(blk, blk, send_sems, recv_sems, j, (x, y, 1 - c)))
    return out


_PEER_FLIPS = [(f // 4, (f // 2) % 2, f % 2) for f in range(1, N_DEV)]


def _copies_direct(same_src):
    def make(srcs, lands, send_sems, recv_sems):
        x, y, c = _place()
        me = _dev(x, y, c)
        out = []
        for a in range(len(srcs)):
            for k, (fx, fy, fc) in enumerate(_PEER_FLIPS):
                peer = ((x + fx) % 2, (y + fy) % 2, (c + fc) % 2)
                src = srcs[a] if same_src else srcs[a].at[_dev(*peer)]
                out.append(_remote(src, lands[a].at[me], send_sems, recv_sems, 7 * a + k, peer))
        return out
    return make


def _copies_siblings(srcs, lands, send_sems, recv_sems):
    x, y, c = _place()
    return [_remote(srcs[a].at[2 * q + (1 - c)], lands[a].at[q], send_sems, recv_sems, 4 * a + q, (x, y, 1 - c))
            for a in range(len(srcs)) for q in range(4)]


def _copies_chips(srcs, lands, send_sems, recv_sems):
    x, y, c = _place()
    chips = [(1 - x, y), (x, 1 - y), (1 - x, 1 - y)]
    return [_remote(srcs[a].at[2 * qx + qy], lands[a].at[j], send_sems, recv_sems, 3 * a + j, (qx, qy, c))
            for a in range(len(srcs)) for j, (qx, qy) in enumerate(chips)]


def _split_start(make, per_array, srcs, lands, dep, name):
    n, tot = len(srcs), len(srcs) + len(lands)

    def body(*refs):
        send_sems, recv_sems, token = refs[tot + 1], refs[tot + 2], refs[-1]
        for cp in make(refs[:n], refs[n:tot], send_sems, recv_sems):
            cp.start()
        token[...] = jnp.zeros_like(token)

    hbm = lambda t: pltpu.with_memory_space_constraint(t, pltpu.HBM)
    res = pl.pallas_call(
        body, name=name,
        out_shape=(pltpu.SemaphoreType.DMA((per_array * n,)), pltpu.SemaphoreType.DMA((per_array * n,)),
                   *[pltpu.HBM(t.shape, t.dtype) for t in (*srcs, *lands)], SDS((8, 128), f32)),
        in_specs=[_HBM] * tot + [_ANY], out_specs=(_SEM, _SEM, *([_HBM] * tot), _VMEM),
        input_output_aliases={i: 2 + i for i in range(tot)},
        compiler_params=pltpu.CompilerParams(has_side_effects=pltpu.SideEffectType.DATAFLOW_SIDE_EFFECTING),
    )(*[hbm(t) for t in (*srcs, *lands)], dep)
    return res[0], res[1], list(res[2:2 + n]), list(res[2 + n:2 + tot]), res[-1]


def _split_wait(make, send_sems, recv_sems, srcs, lands, after, name):
    n, tot = len(srcs), len(srcs) + len(lands)

    def body(*refs):
        for cp in make(refs[:n], refs[n:tot], refs[tot], refs[tot + 1]):
            cp.wait_send()
            cp.wait_recv()

    res = pl.pallas_call(
        body, name=name,
        out_shape=tuple(pltpu.HBM(t.shape, t.dtype) for t in (*srcs, *lands)),
        in_specs=[_HBM] * tot + [_SEM, _SEM, _ANY], out_specs=tuple([_HBM] * tot),
        input_output_aliases={i: i for i in range(tot)},
        compiler_params=pltpu.CompilerParams(has_side_effects=pltpu.SideEffectType.DATAFLOW_SIDE_EFFECTING),
    )(*srcs, *lands, send_sems, recv_sems, after)
    return list(res[:n]), list(res[n:])


def _adamw_direct(g, land, me_idx, w, m, v, name):
    r, wd = w.shape
    tr = min(r, 256)

    def body(me_ref, *refs):
        g_ref, peers = refs[0], refs[1:N_DEV]
        w_ref, m_ref, v_ref, g_out, d_out, m_out, v_out = refs[N_DEV:]
        gs = g_ref[...].astype(f32)
        for p_ref in peers:
            gs = gs + p_ref[...].astype(f32)
        d, mn, vn = _adamw(w_ref[...], gs, m_ref[...], v_ref[...])
        g_out[...] = gs
        d_out[...] = d
        m_out[...] = mn
        v_out[...] = vn

    tile = pl.BlockSpec((tr, wd), lambda i, me_ref: (i, 0))
    slot = lambda k: pl.BlockSpec((None, tr, wd), lambda i, me_ref: ((me_ref[0] + k) % N_DEV, i, 0))
    return pl.pallas_call(
        body,
        grid_spec=pltpu.PrefetchScalarGridSpec(
            num_scalar_prefetch=1, grid=(r // tr,),
            in_specs=[slot(0)] + [slot(k) for k in range(1, N_DEV)] + [tile, tile, tile],
            out_specs=[tile] * 4),
        out_shape=[SDS((r, wd), f32)] * 4, name=name, compiler_params=_cp(("arbitrary",), 48),
    )(me_idx, g, *([land] * (N_DEV - 1)), w, m, v)


def _pair_sum(g, r1, c_idx, name):
    _, r, w = g.shape
    tr = _tile_rows(r)

    def body(c_ref, g_ref, r_ref, o_ref):
        o_ref[...] = (g_ref[...].astype(f32) + r_ref[...].astype(f32)).astype(bf16)

    return pl.pallas_call(
        body,
        grid_spec=pltpu.PrefetchScalarGridSpec(
            num_scalar_prefetch=1, grid=(4, r // tr),
            in_specs=[pl.BlockSpec((None, tr, w), lambda q, i, c_ref: (2 * q + c_ref[0], i, 0)),
                      pl.BlockSpec((None, tr, w), lambda q, i, c_ref: (q, i, 0))],
            out_specs=pl.BlockSpec((None, tr, w), lambda q, i, c_ref: (q, i, 0))),
        out_shape=SDS((4, r, w), bf16), name=name, compiler_params=_cp(("arbitrary", "arbitrary")),
    )(c_idx, g, r1)


def _adamw(w, g, m, v):
    m = ADAM_B1 * m + (1.0 - ADAM_B1) * g
    v = ADAM_B2 * v + (1.0 - ADAM_B2) * (g * g)
    m_hat = m / (1.0 - ADAM_B1 ** ADAM_STEP)
    v_hat = v / (1.0 - ADAM_B2 ** ADAM_STEP)
    delta = -ADAM_LR * (m_hat / (jnp.sqrt(v_hat) + ADAM_EPS) + ADAM_WD * w)
    return delta, m, v


def _adamw_big(part, r2, q_idx, w, m, v, name, row_off=0, cols=(0, 1), prev=None):
    r, wd = w.shape
    tr = _tile_rows(r)
    k, ncol = cols
    wp = wd // ncol

    def body(q_ref, p_ref, r_ref, w_ref, m_ref, v_ref, *rest):
        g_out, d_out, m_out, v_out = rest[-4:]
        g = p_ref[...].astype(f32)
        for j in range(3):
            g = g + r_ref[j].astype(f32)
        d, mn, vn = _adamw(w_ref[...], g, m_ref[...], v_ref[...])
        g_out[...] = g
        d_out[...] = d
        m_out[...] = mn
        v_out[...] = vn

    tile = pl.BlockSpec((tr, wp), lambda i, q_ref: (i, k))
    prev = list(prev) if prev is not None else []
    return pl.pallas_call(
        body,
        grid_spec=pltpu.PrefetchScalarGridSpec(
            num_scalar_prefetch=1, grid=(r // tr,),
            in_specs=[pl.BlockSpec((None, tr, wp), lambda i, q_ref: (q_ref[0], row_off + i, 0)),
                      pl.BlockSpec((3, tr, wp), lambda i, q_ref: (0, row_off + i, 0)), tile, tile, tile]
                     + [pl.BlockSpec(memory_space=pl.ANY)] * len(prev),
            out_specs=[tile] * 4),
        out_shape=[SDS((r, wd), f32)] * 4, name=name,
        input_output_aliases={6 + i: i for i in range(len(prev))},
        compiler_params=_cp(("arbitrary",), 48),
    )(q_idx, part, r2, w, m, v, *prev)


_SMALL_ROWS = 24


def _pack_early(vec_rnn, st_out, dsr, db_in):
    def body(vr_ref, so_ref, dsr_ref, db_ref, sm_ref, sm2_ref):
        sm_ref[...] = jnp.zeros_like(sm_ref)
        sm2_ref[...] = jnp.zeros_like(sm2_ref)
        sm_ref[2:3, :] = vr_ref[3:4, :]
        sm_ref[3:6, :] = vr_ref[0:3, :]
        sm_ref[6:7, :] = so_ref[2:3, :]
        sm_ref[7:9, :] = so_ref[0:2, :]
        sm_ref[10:11, :] = so_ref[3:4, :]
        for h in range(N_KV):
            sm_ref[9:10, h * GROUP:(h + 1) * GROUP] = _colsum(dsr_ref[h])
        for j in range(6):
            sm_ref[16 + j:17 + j, :] = db_ref[0:1, j * D:(j + 1) * D]
        sm_ref[22:23, 0:D_IN - 6 * D] = db_ref[0:1, 6 * D:D_IN]
        for s in range(N_DEV):
            sm2_ref[s, 0:CONV_WIDTH, :] = vr_ref[4:8, s * 256:(s + 1) * 256]

    return pl.pallas_call(
        body, out_shape=[SDS((_SMALL_ROWS, D), f32), SDS((N_DEV, 8, 256), f32)],
        name="pack_early", compiler_params=_cp(None),
    )(vec_rnn, st_out, dsr, db_in)


def _pack_late(st_emb, dmeta):
    def body(se_ref, dm_ref, sm_ref, sm2_ref):
        sm_ref[...] = se_ref[...]
        for s in range(N_DEV):
            sm2_ref[s] = dm_ref[:, s * 256:(s + 1) * 256]

    return pl.pallas_call(
        body, out_shape=[SDS((8, D), f32), SDS((N_DEV, N_META, 256), f32)],
        name="pack_late", compiler_params=_cp(None),
    )(st_emb, dmeta)


def _small_allreduce(sm, sm2):
    def body(sm_ref, sm2_ref, o_ref, o2_ref, buf, buf2, send_sems, recv_sems):
        x, y, c = _place()
        me = _dev(x, y, c)
        copies = []
        for f in range(1, N_DEV):
            fx, fy, fc = f // 4, (f // 2) % 2, f % 2
            peer = ((x + fx) % 2, (y + fy) % 2, (c + fc) % 2)
            for t, (src, dst) in enumerate(((sm_ref, buf), (sm2_ref, buf2))):
                k = 2 * (f - 1) + t
                copies.append(pltpu.make_async_remote_copy(
                    src_ref=src, dst_ref=dst.at[me], send_sem=send_sems.at[k], recv_sem=recv_sems.at[k],
                    device_id=peer, device_id_type=MESH))
        for cp in copies:
            cp.start()
        buf[me] = sm_ref[...]
        buf2[me] = sm2_ref[...]
        for cp in copies:
            cp.wait()
        acc, acc2 = buf[0], buf2[0]
        for e in range(1, N_DEV):
            acc, acc2 = acc + buf[e], acc2 + buf2[e]
        o_ref[...] = acc
        o2_ref[...] = acc2

    return pl.pallas_call(
        body, in_specs=[_VMEM, _VMEM], out_specs=[_VMEM, _VMEM],
        out_shape=[SDS(sm.shape, f32), SDS(sm2.shape, f32)],
        scratch_shapes=[pltpu.VMEM((N_DEV, *sm.shape), f32), pltpu.VMEM((N_DEV, *sm2.shape), f32),
                        pltpu.SemaphoreType.DMA((14,)), pltpu.SemaphoreType.DMA((14,))],
        name="small_allreduce",
    )(sm, sm2)


_SMALL_ROW_OF = {"ln_emb_g": 0, "ln_emb_b": 1, "conv_b": 2, "b_ra": 3, "b_ri": 4, "lru_lambda": 5, "b_o": 6,
                 "ln_g": 7, "ln_b": 8}
_SMALL_NAMES = ["ln_emb_g", "ln_emb_b", "conv_b", "b_ra", "b_ri", "lru_lambda", "b_o", "ln_g", "ln_b",
                "sinks", "b_in", "meta_tokens", "conv_w"]


def _small_update(me_idx, early, late, wmv):
    n_fixed = 7

    def in_order(me, own_ref, land_ref):
        acc = None
        for e in range(N_DEV):
            term = jnp.where(me == e, own_ref[...], land_ref[e])
            acc = term if acc is None else acc + term
        return acc

    def body(*refs):
        me_ref, own_ref, land_ref, cown_ref, cland_ref, late_ref, meta_ref = refs[:n_fixed]
        ins = refs[n_fixed:n_fixed + 3 * len(_SMALL_NAMES)]
        outs = refs[n_fixed + 3 * len(_SMALL_NAMES):]
        me = me_ref[0]
        sm = in_order(me, own_ref, land_ref)
        conv = in_order(me, cown_ref, cland_ref)

        def grad_of(name):
            if name in ("ln_emb_g", "ln_emb_b"):
                r = _SMALL_ROW_OF[name]
                return late_ref[r:r + 1, :]
            if name in _SMALL_ROW_OF:
                r = _SMALL_ROW_OF[name]
                return sm[r:r + 1, :]
            if name == "sinks":
                return sm[9:10, 0:N_KV * GROUP]
            if name == "b_in":
                return jnp.concatenate([sm[16 + j:17 + j, :] for j in range(7)], axis=1)[:, :D_IN]
            if name == "meta_tokens":
                return meta_ref[...]
            return conv[0:CONV_WIDTH, :]

        for i, name in enumerate(_SMALL_NAMES):
            w_ref, m_ref, v_ref = ins[3 * i:3 * i + 3]
            g = grad_of(name)
            d, mn, vn = _adamw(w_ref[...], g, m_ref[...], v_ref[...])
            outs[4 * i][...] = g
            outs[4 * i + 1][...] = d
            outs[4 * i + 2][...] = mn
            outs[4 * i + 3][...] = vn
        outs[-1][...] = jnp.broadcast_to(jnp.sum(sm[10:11, :], axis=1, keepdims=True), (8, 128))

    args, out_shape = [me_idx, *early, *late], []
    for name in _SMALL_NAMES:
        args += list(wmv[name])
        out_shape += [SDS(wmv[name][0].shape, f32)] * 4
    out_shape.append(SDS((8, 128), f32))
    res = pl.pallas_call(
        body, out_shape=out_shape, in_specs=[pl.BlockSpec(memory_space=pltpu.SMEM)] + [_VMEM] * (len(args) - 1),
        name="small_update", compiler_params=_cp(None))(*args)
    return {name: tuple(res[4 * i:4 * i + 4]) for i, name in enumerate(_SMALL_NAMES)}, res[-1][0, 0]


_WEIGHTS = ["meta_tokens", "ln_emb_g", "ln_emb_b", "w_in", "b_in", "conv_w", "conv_b", "w_ra", "b_ra", "w_ri",
            "b_ri", "lru_lambda", "sinks", "w_rnn_out", "w_attn_out", "w_o", "b_o", "ln_g", "ln_b"]
_SMALL_2D = {"meta_tokens": (N_META, 256), "conv_w": (CONV_WIDTH, 256), "b_in": (1, D_IN), "sinks": (1, N_KV * GROUP)}


def kernel(x, meta_tokens, ln_emb_g, ln_emb_b, w_in, b_in, conv_w, conv_b, w_ra, b_ra, w_ri, b_ri, lru_lambda, sinks, w_rnn_out, w_attn_out, w_o, b_o, ln_g, ln_b, loss_target, m_meta_tokens, m_ln_emb_g, m_ln_emb_b, m_w_in, m_b_in, m_conv_w, m_conv_b, m_w_ra, m_b_ra, m_w_ri, m_b_ri, m_lru_lambda, m_sinks, m_w_rnn_out, m_w_attn_out, m_w_o, m_b_o, m_ln_g, m_ln_b, v_meta_tokens, v_ln_emb_g, v_ln_emb_b, v_w_in, v_b_in, v_conv_w, v_conv_b, v_w_ra, v_b_ra, v_w_ri, v_b_ri, v_lru_lambda, v_sinks, v_w_rnn_out, v_w_attn_out, v_w_o, v_b_o, v_ln_g, v_ln_b):
    w = dict(meta_tokens=meta_tokens, ln_emb_g=ln_emb_g, ln_emb_b=ln_emb_b, w_in=w_in, b_in=b_in, conv_w=conv_w,
             conv_b=conv_b, w_ra=w_ra, b_ra=b_ra, w_ri=w_ri, b_ri=b_ri, lru_lambda=lru_lambda, sinks=sinks,
             w_rnn_out=w_rnn_out, w_attn_out=w_attn_out, w_o=w_o, b_o=b_o, ln_g=ln_g, ln_b=ln_b)
    m = dict(meta_tokens=m_meta_tokens, ln_emb_g=m_ln_emb_g, ln_emb_b=m_ln_emb_b, w_in=m_w_in, b_in=m_b_in,
             conv_w=m_conv_w, conv_b=m_conv_b, w_ra=m_w_ra, b_ra=m_b_ra, w_ri=m_w_ri, b_ri=m_b_ri,
             lru_lambda=m_lru_lambda, sinks=m_sinks, w_rnn_out=m_w_rnn_out, w_attn_out=m_w_attn_out, w_o=m_w_o,
             b_o=m_b_o, ln_g=m_ln_g, ln_b=m_ln_b)
    v = dict(meta_tokens=v_meta_tokens, ln_emb_g=v_ln_emb_g, ln_emb_b=v_ln_emb_b, w_in=v_w_in, b_in=v_b_in,
             conv_w=v_conv_w, conv_b=v_conv_b, w_ra=v_w_ra, b_ra=v_b_ra, w_ri=v_w_ri, b_ri=v_b_ri,
             lru_lambda=v_lru_lambda, sinks=v_sinks, w_rnn_out=v_w_rnn_out, w_attn_out=v_w_attn_out, w_o=v_w_o,
             b_o=v_b_o, ln_g=v_ln_g, ln_b=v_ln_b)
    px, py, pc = _place()
    as_idx = lambda t: jnp.reshape(t, (1,)).astype(jnp.int32)
    c_idx, q_idx, me_idx = as_idx(pc), as_idx(2 * px + py), as_idx(_dev(px, py, pc))

    w3_s, wrg_s, small_s = _cast_small(w_rnn_out, w_attn_out, w_o, w_ra, w_ri, meta_tokens, conv_w)
    vec = lambda name: w[name].reshape(1, -1)
    p = {k: vec(k) for k in ("ln_emb_g", "ln_emb_b", "b_in", "conv_b", "b_ra", "b_ri", "lru_lambda", "sinks",
                             "b_o", "ln_g", "ln_b")}
    w_in_t = lambda a: jnp.swapaxes(a, 1, 2).reshape(SHARD_IN, D)
    wg, wrg, smallw, w3_land = _all_gather([_cast_w_in(w_in_t(w_in)), wrg_s, small_s], [w3_s])
    late = _split_start(_copies_late_own, 4, [wg], [], smallw, "gather_late_start")
    h32, hb = _ln_emb(x, smallw, p["ln_emb_g"], p["ln_emb_b"] + late[4][0:1, 0:1])
    z = _mm_z(hb, late[2][0].reshape(D_IN, D), p["b_in"], None, "mm_z_early")
    (wg,), _ = _split_wait(_copies_late_own, late[0], late[1], late[2], [], z, "gather_late_wait")
    passed = _split_start(_copies_late_pass, 3, [wg], [], smallw, "gather_pass_start")
    z = _mm_z(hb, passed[2][0].reshape(D_IN, D), p["b_in"] + passed[4][0:1, 0:1], c_idx, "mm_z_late_own", z)
    (wg,), _ = _split_wait(_copies_late_pass, passed[0], passed[1], passed[2], [], z, "gather_pass_wait")
    w3_pending = _split_start(_copies_direct(True), 7, [w3_s], [w3_land], wg, "gather_w3_start")
    w_full = wg.reshape(D_IN, D)

    zero = w3_pending[4][0:1, 0:1]
    z = _mm_z(hb, w_full, p["b_in"] + zero, 1 - c_idx, "mm_z_late_other", z)
    s = _step_attn(_step_rnn(h32, hb, z, wrg, smallw, p, zero), p, zero)
    w3 = _split_wait(_copies_direct(True), *w3_pending[:4], s["lse"], "gather_w3_wait")[1][0]
    t = _step_merge(s, loss_target, w3, p)

    big = {}
    two_d = lambda name: (w[name].shape[-2], w[name].shape[-1])
    proj = ("w_o", "w_rnn_out", "w_attn_out")
    g_proj = [t[k].reshape(N_DEV, 256, D) for k in ("g_wo", "g_wrnn", "g_wattn")]
    g_pending = _split_start(_copies_direct(False), 7, g_proj, [lax.empty((N_DEV, 256, D), bf16) for _ in proj],
                             p["b_o"], "reduce_proj_start")
    u = _step_backward(s, t, wrg, smallw, p, p["conv_b"] + g_pending[4][0:1, 0:1])

    def siblings_start(gs, dep, tag):
        return _split_start(_copies_siblings, 4, gs, [lax.empty((4, *g.shape[1:]), bf16) for g in gs], dep,
                            "reduce_siblings_start_" + tag)

    def chips_start(gs, r1, dep, tag):
        parts = [_pair_sum(g, r, c_idx, "pair_sum_%s%d" % (tag, i)) for i, (g, r) in enumerate(zip(gs, r1))]
        return _split_start(_copies_chips, 3, parts, [lax.empty((3, *q.shape[1:]), bf16) for q in parts], dep,
                            "reduce_chips_start_" + tag)

    g_a, dz, db_in = _mm_dwin_parts(s["hb"], u["dz_parts"])
    shards = lambda g: g.reshape(N_DEV, SHARD_IN, W_IN_HALF)
    sib_a = siblings_start([shards(g_a), u["g_wrg"].reshape(N_DEV, 2 * RNN_BLOCK, RNN_BLOCK)], db_in, "a")
    g_proj, g_land = _split_wait(_copies_direct(False), *g_pending[:4], sib_a[4], "reduce_proj_wait")
    for i, name in enumerate(proj):
        res = _adamw_direct(g_proj[i], g_land[i], me_idx, w[name].reshape(two_d(name)), m[name].reshape(two_d(name)),
                            v[name].reshape(two_d(name)), "adamw_" + name)
        big[name] = tuple(r.reshape(w[name].shape) for r in res)
    chp_a = chips_start(*_split_wait(_copies_siblings, *sib_a[:4], big["w_attn_out"][3], "reduce_siblings_wait_a"),
                        db_in, "a")
    g_b = _mm_dwin(s["hb"], dz, chp_a[4])
    sib_b = siblings_start([shards(g_b)], db_in, "b")
    sm_e = _pack_early(u["vec_rnn"], t["st_out"], u["dsr"], db_in)
    early = _split_start(_copies_direct(True), 7, list(sm_e),
                         [lax.empty((N_DEV, *a.shape), f32) for a in sm_e], sib_b[4], "small_early_start")
    dh_lo = _mm_dh(dz, w_full, early[4], 0)
    chp_b = chips_start(*_split_wait(_copies_siblings, *sib_b[:4], dh_lo, "reduce_siblings_wait_b"), db_in, "b")
    dh_hi = _mm_dh(dz, w_full, chp_b[4], 1)
    parts_a, r2_a = _split_wait(_copies_chips, *chp_a[:4], dh_hi, "reduce_chips_wait_a")
    w_in_res = _adamw_big(parts_a[0], r2_a[0], q_idx, w_in_t(w["w_in"]), w_in_t(m["w_in"]), w_in_t(v["w_in"]),
                          "adamw_w_in_a", cols=(0, 2))
    u.update(_step_input_grad(dh_lo, dh_hi, t["du32"], x, smallw, p, w_in_res[3]))
    sm_l, meta_l = _small_allreduce(*_pack_late(u["st_emb"], u["dmeta"]))
    (sm_own, conv_own), (sm_land, conv_land) = _split_wait(_copies_direct(True), *early[:4], sm_l, "small_early_wait")
    me = _dev(px, py, pc)
    mine = lambda a, axis: lax.dynamic_index_in_dim(a, me, axis, keepdims=False)
    two = lambda name, t: t.reshape(_SMALL_2D.get(name, (1, D)))
    small, loss = _small_update(me_idx, (sm_own, sm_land, mine(conv_own, 0), mine(conv_land, 1)),
                                (sm_l, mine(meta_l, 0)),
                                {k: (two(k, w[k]), two(k, m[k]), two(k, v[k])) for k in _SMALL_NAMES})

    parts_b, r2_b = _split_wait(_copies_chips, *chp_b[:4], small["b_in"][2], "reduce_chips_wait_b")
    res = _adamw_big(parts_b[0], r2_b[0], q_idx, w_in_t(w["w_in"]), w_in_t(m["w_in"]), w_in_t(v["w_in"]),
                     "adamw_w_in_b", cols=(1, 2), prev=w_in_res)
    big["w_in"] = tuple(jnp.swapaxes(r.reshape(1, SHARD_IN, D), 1, 2) for r in res)
    for i, name in enumerate(("w_ra", "w_ri")):
        sq = (RNN_BLOCK, RNN_BLOCK)
        res = _adamw_big(parts_a[1], r2_a[1], q_idx, w[name].reshape(sq), m[name].reshape(sq), v[name].reshape(sq),
                         "adamw_" + name, row_off=i)
        big[name] = tuple(r.reshape(w[name].shape) for r in res)
    res = dict(big)
    for k in _SMALL_NAMES:
        res[k] = tuple(t.reshape(w[k].shape) for t in small[k])

    outs = [loss, u["grad_x"]]
    for j in range(4):
        outs += [res[k][j] for k in _WEIGHTS]
    return tuple(outs)
```

```python
import jax
import jax.numpy as jnp
from jax import lax
from jax.experimental import pallas as pl
from jax.experimental.pallas import tpu as pltpu

f32, bf16 = jnp.float32, jnp.bfloat16
SDS = jax.ShapeDtypeStruct

N_DEV = 8
D = 2048
N_META = 16
BLK = 128
ROW0 = BLK - N_META
N_RNN_BLOCKS = 8
RNN_BLOCK = D // N_RNN_BLOCKS
CONV_WIDTH = 4
LRU_C = 8.0
HEAD_DIM = 64
N_KV = 4
GROUP = 8
HALF = HEAD_DIM // 2
ROPE_THETA = 10000.0
NEG_INF = -1e30
LN_EPS = 1e-5
ALPHA = 2.0 ** 0.25
D_IN = 12800
SHARD_IN = D_IN // N_DEV
W_IN_HALF = D // 2
OFF_GR, OFF_Q, OFF_K, OFF_V, OFF_GA, OFF_G = 2048, 4096, 6144, 6400, 6656, 8704
ADAM_LR, ADAM_B1, ADAM_B2, ADAM_EPS, ADAM_WD, ADAM_STEP = 1e-3, 0.9, 0.999, 1e-8, 0.01, 10
VMEM_LIMIT_MB = 56
MESH = pl.DeviceIdType.MESH


def _cp(sem=None, vmem_mb=40):
    return pltpu.CompilerParams(dimension_semantics=sem, vmem_limit_bytes=vmem_mb * 2 ** 20)


def _row_chunk(m):
    best = 16
    for c in range(16, 641, 16):
        if m % c == 0:
            best = c
    return best


def _sigmoid(x):
    return 1.0 / (1.0 + jnp.exp(-x))


def _silu_and_grad(x):
    s = _sigmoid(x)
    return x * s, s * (1.0 + x * (1.0 - s))


def _log_sigmoid(x):
    return jnp.minimum(x, 0.0) - jnp.log1p(jnp.exp(-jnp.abs(x)))


def _ln_rows(v, g, b):
    mu = jnp.mean(v, axis=-1, keepdims=True)
    c = v - mu
    var = jnp.mean(c * c, axis=-1, keepdims=True)
    rstd = lax.rsqrt(var + LN_EPS)
    xhat = c * rstd
    return xhat * g + b, xhat, rstd


def _ln_rows_bwd(dy, g, xhat, rstd):
    dxh = dy * g
    m1 = jnp.mean(dxh, axis=-1, keepdims=True)
    m2 = jnp.mean(dxh * xhat, axis=-1, keepdims=True)
    return rstd * (dxh - m1 - xhat * m2)


def _colsum(v):
    return jnp.sum(v, axis=0, keepdims=True)


def _dot(a, b):
    return jnp.dot(a, b, preferred_element_type=f32)


def _dot_nt(a, b):
    return lax.dot_general(a, b, (((1,), (1,)), ((), ())), preferred_element_type=f32)


def _dot_tn(a, b):
    return lax.dot_general(a, b, (((0,), (0,)), ((), ())), preferred_element_type=f32)


def _meta_full(sw_ref):
    return jnp.concatenate([sw_ref[s, 0:N_META, :] for s in range(N_DEV)], axis=1)


def _ln_emb(x, smallw, g_e, b_e):
    seq = x.shape[1]
    rows = seq + BLK
    nb = rows // BLK

    def body(x_ref, sw_ref, g_ref, b_ref, h32_ref, hb_ref):
        i = pl.program_id(0)
        g, b = g_ref[...], b_ref[...]

        def emit(blk):
            h32_ref[...] = blk
            hb_ref[...] = blk.astype(bf16)

        @pl.when(i == 0)
        def _():
            hm = _ln_rows(_meta_full(sw_ref), g, b)[0]
            emit(jnp.concatenate([jnp.zeros((ROW0, D), f32), hm], axis=0))

        @pl.when(i > 0)
        def _():
            emit(_ln_rows(x_ref[0], g, b)[0])

    return pl.pallas_call(
        body, grid=(nb,),
        in_specs=[pl.BlockSpec((1, BLK, D), lambda i: (0, jnp.maximum(i - 1, 0), 0)),
                  pl.BlockSpec((N_DEV, 24, 256), lambda i: (0, 0, 0)),
                  pl.BlockSpec((1, D), lambda i: (0, 0)),
                  pl.BlockSpec((1, D), lambda i: (0, 0))],
        out_specs=[pl.BlockSpec((BLK, D), lambda i: (i, 0)),
                   pl.BlockSpec((BLK, D), lambda i: (i, 0))],
        out_shape=[SDS((rows, D), f32), SDS((rows, D), bf16)],
        name="ln_emb", compiler_params=_cp(("arbitrary",)),
    )(x, smallw, g_e, b_e)


def _ln_emb_bwd(dh_lo, dh_hi, du32, x, smallw, g_e, after):
    seq = x.shape[1]
    rows = seq + BLK
    nb = rows // BLK

    def body(dlo_ref, dhi_ref, du_ref, x_ref, sw_ref, g_ref, after_ref, gx_ref, dmeta_ref, st_ref):
        i = pl.program_id(0)
        g = g_ref[...]
        dht = jnp.concatenate([dlo_ref[...], dhi_ref[...]], axis=1) + ALPHA * du_ref[...]

        @pl.when(i == 0)
        def _():
            v = jnp.concatenate([jnp.zeros((ROW0, D), f32), _meta_full(sw_ref)], axis=0)
            valid = lax.broadcasted_iota(jnp.int32, (BLK, 1), 0) >= ROW0
            d = jnp.where(valid, dht, 0.0)
            _, xhat, rstd = _ln_rows(v, g, 0.0)
            dv = _ln_rows_bwd(d, g, xhat, rstd)
            dmeta_ref[...] = dv[ROW0:, :]
            st_ref[...] = jnp.concatenate([_colsum(d * xhat), _colsum(d), jnp.zeros((6, D), f32)], axis=0)

        @pl.when(i > 0)
        def _():
            _, xhat, rstd = _ln_rows(x_ref[0], g, 0.0)
            gx_ref[0] = _ln_rows_bwd(dht, g, xhat, rstd)
            st_ref[0:1, :] += _colsum(dht * xhat)
            st_ref[1:2, :] += _colsum(dht)

    return pl.pallas_call(
        body, grid=(nb,),
        in_specs=[pl.BlockSpec((BLK, W_IN_HALF), lambda i: (i, 0)),
                  pl.BlockSpec((BLK, W_IN_HALF), lambda i: (i, 0)),
                  pl.BlockSpec((BLK, D), lambda i: (i, 0)),
                  pl.BlockSpec((1, BLK, D), lambda i: (0, jnp.maximum(i - 1, 0), 0)),
                  pl.BlockSpec((N_DEV, 24, 256), lambda i: (0, 0, 0)),
                  pl.BlockSpec((1, D), lambda i: (0, 0)),
                  pl.BlockSpec(memory_space=pl.ANY)],
        out_specs=[pl.BlockSpec((1, BLK, D), lambda i: (0, jnp.maximum(i - 1, 0), 0)),
                   pl.BlockSpec((N_META, D), lambda i: (0, 0)),
                   pl.BlockSpec((8, D), lambda i: (0, 0))],
        out_shape=[SDS((1, seq, D), f32), SDS((N_META, D), f32), SDS((8, D), f32)],
        name="ln_emb_bwd", compiler_params=_cp(("arbitrary",)),
    )(dh_lo, dh_hi, du32, x, smallw, g_e, after)


def _mm(a, b, *, name, nt=False, sel=None, bias=None, out_dtype=f32, tn=512):
    m, k = a.shape
    cm = _row_chunk(m)
    stacked = sel is not None
    n = D if stacked else (b.shape[0] if nt else b.shape[1])
    am = m
    if stacked and nt:
        b_spec = pl.BlockSpec((tn // 256, None, 256, D), lambda j, i: (j, sel, 0, 0))
    elif stacked:
        b_spec = pl.BlockSpec((N_DEV, None, 256, tn), lambda j, i: (0, sel, 0, j))
    elif nt:
        b_spec = pl.BlockSpec((tn, k), lambda j, i: (j, 0))
    else:
        b_spec = pl.BlockSpec((k, tn), lambda j, i: (0, j))
    in_specs = [pl.BlockSpec((am, k), lambda j, i: (i, 0)), b_spec]
    args = [a, b]
    if bias is not None:
        in_specs.append(pl.BlockSpec((1, tn), lambda j, i: (0, j)))
        args.append(bias)

    def body(*refs):
        a_ref, b_ref, o_ref = refs[0], refs[1], refs[-1]
        bm = b_ref[...]
        if stacked:
            bm = bm.reshape((tn, D) if nt else (D, tn))
        for c in range(am // cm):
            acc = (_dot_nt if nt else _dot)(a_ref[c * cm:(c + 1) * cm, :], bm)
            if bias is not None:
                acc = acc + refs[2][...]
            o_ref[c * cm:(c + 1) * cm, :] = acc.astype(out_dtype)

    return pl.pallas_call(
        body, grid=(n // tn, m // am), in_specs=in_specs,
        out_specs=pl.BlockSpec((am, tn), lambda j, i: (i, j)),
        out_shape=SDS((m, n), out_dtype), name=name, compiler_params=_cp(("arbitrary", "arbitrary"), 48),
    )(*args)


def _mm_dh(dz, w_t, after, half):
    rows = dz.shape[0]
    tn = 512
    nt = W_IN_HALF // tn
    cm = _row_chunk(rows) // 2

    def body(a_ref, w_ref, after_ref, o_ref):
        o_ref[...] = _dot(a_ref[...], w_ref[...])

    return pl.pallas_call(
        body, grid=(nt, rows // cm),
        in_specs=[pl.BlockSpec((cm, D_IN), lambda j, i: (i, 0)),
                  pl.BlockSpec((D_IN, tn), lambda j, i: (0, half * nt + j)),
                  pl.BlockSpec(memory_space=pl.ANY)],
        out_specs=pl.BlockSpec((cm, tn), lambda j, i: (i, j)),
        out_shape=SDS((rows, W_IN_HALF), f32), name="mm_dh_%d" % half,
        compiler_params=_cp(("arbitrary", "arbitrary"), 48),
    )(dz, w_t, after)


def _mm_dwin_parts(hb, parts):
    rows = hb.shape[0]
    tc = 512
    edges = [0]
    for _, w in parts:
        edges.append(edges[-1] + w // tc)

    def body(*refs):
        h_ref, (o_ref, dz_ref, db_ref) = refs[len(parts)], refs[len(parts) + 1:]
        j = pl.program_id(0)
        for p_ref, lo, hi in zip(refs, edges[:-1], edges[1:]):
            @pl.when((j >= lo) & (j < hi))
            def _():
                o_ref[...] = _dot_tn(p_ref[...], h_ref[...]).astype(bf16)
                dz_ref[...] = p_ref[...]

                def step(i, s):
                    blk = p_ref[pl.ds(pl.multiple_of(i * BLK, BLK), BLK), :].astype(f32)
                    return s + blk.reshape(BLK // 8, 8, tc).sum(axis=0)
                s = lax.fori_loop(0, rows // BLK, step, jnp.zeros((8, tc), f32))
                db_ref[...] = jnp.broadcast_to(_colsum(s), (8, tc))

    in_specs = [pl.BlockSpec((rows, tc), lambda j, lo=lo, hi=hi: (0, jnp.clip(j - lo, 0, hi - lo - 1)))
                for lo, hi in zip(edges[:-1], edges[1:])]
    return pl.pallas_call(
        body, grid=(D_IN // tc,),
        in_specs=in_specs + [pl.BlockSpec((rows, W_IN_HALF), lambda j: (0, 0))],
        out_specs=[pl.BlockSpec((tc, W_IN_HALF), lambda j: (j, 0)), pl.BlockSpec((rows, tc), lambda j: (0, j)),
                   pl.BlockSpec((8, tc), lambda j: (0, j))],
        out_shape=[SDS((D_IN, W_IN_HALF), bf16), SDS((rows, D_IN), bf16), SDS((8, D_IN), f32)],
        name="mm_dwin_0", compiler_params=_cp(("arbitrary",), VMEM_LIMIT_MB),
    )(*[a for a, _ in parts], hb)


def _mm_dwin(hb, dz, after):
    rows = dz.shape[0]
    tc = 640

    def body(dz_ref, h_ref, after_ref, o_ref):
        o_ref[...] = _dot_tn(dz_ref[...], h_ref[...]).astype(bf16)

    return pl.pallas_call(
        body, grid=(D_IN // tc,),
        in_specs=[pl.BlockSpec((rows, tc), lambda j: (0, j)),
                  pl.BlockSpec((rows, W_IN_HALF), lambda j: (0, 1)),
                  pl.BlockSpec(memory_space=pl.ANY)],
        out_specs=pl.BlockSpec((tc, W_IN_HALF), lambda j: (j, 0)),
        out_shape=SDS((D_IN, W_IN_HALF), bf16),
        name="mm_dwin_1", compiler_params=_cp(("arbitrary",), 48),
    )(dz, hb, after)


SCAN_ROWS = 32


def _scan8(a, b, reverse):
    idx = lax.broadcasted_iota(jnp.int32, a.shape, 0)
    for s in (1, 2, 4):
        sh = 8 - s if reverse else s
        a_sh, b_sh = pltpu.roll(a, sh, 0), pltpu.roll(b, sh, 0)
        m = (idx < 8 - s) if reverse else (idx >= s)
        b = jnp.where(m, a * b_sh + b, b)
        a = jnp.where(m, a * a_sh, a)
    return a, b


def _shift_rows(prev8, cur, k):
    ext = jnp.concatenate([prev8, cur], axis=0)
    return pltpu.roll(ext, k, 0)[8:, :]


def _gates(xc, w_ra, b_ra, w_ri, b_ri, ls):
    xb = xc.astype(bf16)
    r = _sigmoid(_dot(xb, w_ra) + b_ra)
    ig = _sigmoid(_dot(xb, w_ri) + b_ri)
    la = LRU_C * r * ls
    a = jnp.exp(la)
    mult = jnp.sqrt(jnp.tanh(-la) * (1.0 + a * a))
    return xb, r, ig, a, mult


_RNN_IN_SPECS = lambda rows: [
    pl.BlockSpec((1, 24, 256), lambda n: (n, 0, 0)),
    pl.BlockSpec((1, RNN_BLOCK), lambda n: (0, n)),
    pl.BlockSpec((N_DEV, 2, None, 32, RNN_BLOCK), lambda n: (0, 0, n, 0, 0)),
    pl.BlockSpec((1, RNN_BLOCK), lambda n: (0, n)),
    pl.BlockSpec((1, RNN_BLOCK), lambda n: (0, n)),
    pl.BlockSpec((1, RNN_BLOCK), lambda n: (0, n)),
]


def _rnn_fwd(z, smallw, conv_b, wrg, b_ra, b_ri, lam):
    rows = z.shape[0]
    nb = rows // BLK
    col = lambda off: pl.BlockSpec((rows, RNN_BLOCK), lambda n: (0, off // RNN_BLOCK + n))

    def body(xr_ref, gr_ref, sw_ref, cb_ref, w_ref, bra_ref, bri_ref, lam_ref, xc_ref, hr_ref, ya_ref, yat_ref, a_s):
        cw = sw_ref[0, N_META:24, :]
        cb = cb_ref[...]
        w_ra = w_ref[:, 0].reshape(RNN_BLOCK, RNN_BLOCK)
        w_ri = w_ref[:, 1].reshape(RNN_BLOCK, RNN_BLOCK)
        b_ra_v, b_ri_v = bra_ref[...], bri_ref[...]
        ls = _log_sigmoid(lam_ref[...])
        rid = lax.broadcasted_iota(jnp.int32, (BLK, 1), 0)

        def blk_step(i, carry):
            r0 = pl.multiple_of(i * BLK, BLK)
            grow = rid + r0
            valid = grow >= ROW0
            cur = jnp.where(valid, xr_ref[pl.ds(r0, BLK), :], 0.0)
            prev8 = xr_ref[pl.ds(pl.multiple_of(jnp.maximum(r0 - 8, 0), 8), 8), :] * (i > 0).astype(f32)
            xc = cb + cw[0:1] * cur
            for k in range(1, CONV_WIDTH):
                xc = xc + cw[k:k + 1] * _shift_rows(prev8, cur, k)
            xc_ref[pl.ds(r0, BLK), :] = xc
            _, _, ig, a, mult = _gates(xc, w_ra, b_ra_v, w_ri, b_ri_v, ls)
            mult = jnp.where(grow == ROW0, 1.0, mult)
            a_s[pl.ds(r0, BLK), :] = a
            hr_ref[pl.ds(r0, BLK), :] = jnp.where(valid, mult * ig * xc, 0.0)
            return carry

        lax.fori_loop(0, nb, blk_step, 0)

        def scan_step(j, carry):
            r0 = pl.multiple_of(j * SCAN_ROWS, SCAN_ROWS)
            tiles = [_scan8(a_s[pl.ds(r0 + 8 * k, 8), :], hr_ref[pl.ds(r0 + 8 * k, 8), :], False)
                     for k in range(SCAN_ROWS // 8)]
            for k, (a, b) in enumerate(tiles):
                h = b + a * carry
                hr_ref[pl.ds(r0 + 8 * k, 8), :] = h
                carry = jnp.broadcast_to(h[7:8, :], (8, RNN_BLOCK))
            return carry

        lax.fori_loop(0, rows // SCAN_ROWS, scan_step, jnp.zeros((8, RNN_BLOCK), f32))

        def gate_step(i, carry):
            r0 = pl.multiple_of(i * BLK, BLK)
            ya_ref[pl.ds(r0, BLK), :] = (hr_ref[pl.ds(r0, BLK), :]
                                         * _silu_and_grad(gr_ref[pl.ds(r0, BLK), :])[0]).astype(bf16)
            return carry

        lax.fori_loop(0, nb, gate_step, 0)
        yat_ref[...] = ya_ref[...].astype(f32).T.astype(bf16)

    return pl.pallas_call(
        body, grid=(N_RNN_BLOCKS,),
        in_specs=[col(0), col(OFF_GR)] + _RNN_IN_SPECS(rows),
        out_specs=[pl.BlockSpec((rows, RNN_BLOCK), lambda n: (0, n))] * 3
                  + [pl.BlockSpec((RNN_BLOCK, rows), lambda n: (n, 0))],
        out_shape=[SDS((rows, D), f32), SDS((rows, D), f32), SDS((rows, D), bf16), SDS((D, rows), bf16)],
        scratch_shapes=[pltpu.VMEM((rows, RNN_BLOCK), f32)],
        name="rnn_fwd", compiler_params=_cp(("arbitrary",)),
    )(z, z, smallw, conv_b, wrg, b_ra, b_ri, lam)


def _rnn_bwd(dya, hr, xc, z, smallw, conv_b, wrg, b_ra, b_ri, lam):
    rows = z.shape[0]
    nb = rows // BLK
    col = lambda off: pl.BlockSpec((rows, RNN_BLOCK), lambda n: (0, off // RNN_BLOCK + n))
    blk = pl.BlockSpec((rows, RNN_BLOCK), lambda n: (0, n))

    def body(dya_ref, hr_ref, xc_ref, xr_ref, gr_ref, sw_ref, cb_ref, w_ref, bra_ref, bri_ref, lam_ref,
             dxr_ref, dgr_ref, dw_ref, vec_ref, a_s, lam_s, dxc_s, r_s, ig_s, mult_s, dw_s):
        cw = sw_ref[0, N_META:24, :]
        w_ra = w_ref[:, 0].reshape(RNN_BLOCK, RNN_BLOCK)
        w_ri = w_ref[:, 1].reshape(RNN_BLOCK, RNN_BLOCK)
        b_ra_v, b_ri_v = bra_ref[...], bri_ref[...]
        lam_v = lam_ref[...]
        ls = _log_sigmoid(lam_v)
        rid = lax.broadcasted_iota(jnp.int32, (BLK, 1), 0)
        zrow = jnp.zeros((1, RNN_BLOCK), f32)

        def p1(i, carry):
            r0 = pl.multiple_of(i * BLK, BLK)
            sl = pl.ds(r0, BLK)
            _, r, ig, a, mult = _gates(xc_ref[sl, :], w_ra, b_ra_v, w_ri, b_ri_v, ls)
            a_s[sl, :] = a
            r_s[sl, :] = r
            ig_s[sl, :] = ig
            mult_s[sl, :] = mult
            sg, dsg = _silu_and_grad(gr_ref[sl, :])
            d = dya_ref[sl, :]
            lam_s[sl, :] = d * sg
            dgr_ref[sl, :] = (d * hr_ref[sl, :] * dsg).astype(bf16)
            return carry

        lax.fori_loop(0, nb, p1, 0)

        def p2(jj, carry):
            r0 = pl.multiple_of((rows // SCAN_ROWS - 1 - jj) * SCAN_ROWS, SCAN_ROWS)
            idx = lax.broadcasted_iota(jnp.int32, (8, RNN_BLOCK), 0)
            tiles = []
            for k in range(SCAN_ROWS // 8):
                sl = pl.ds(r0 + 8 * k, 8)
                a, g = a_s[sl, :], lam_s[sl, :]
                tiles.append((g, *_scan8(a, a * g, True)))
            for k in reversed(range(SCAN_ROWS // 8)):
                g, ca, cb_ = tiles[k]
                mu = cb_ + ca * carry
                lam_s[pl.ds(r0 + 8 * k, 8), :] = g + jnp.where(idx < 7, pltpu.roll(mu, 7, 0), carry)
                carry = jnp.broadcast_to(mu[0:1, :], (8, RNN_BLOCK))
            return carry

        lax.fori_loop(0, rows // SCAN_ROWS, p2, jnp.zeros((8, RNN_BLOCK), f32))

        dw_s[...] = jnp.zeros_like(dw_s)

        def p3(i, carry):
            d_bra, d_bri, d_ls = carry
            r0 = pl.multiple_of(i * BLK, BLK)
            sl = pl.ds(r0, BLK)
            grow = rid + r0
            valid = grow >= ROW0
            first = grow == ROW0
            xcv = xc_ref[sl, :]
            xb = xcv.astype(bf16)
            r, ig, a = r_s[sl, :], ig_s[sl, :], a_s[sl, :]
            mult = jnp.where(first, 1.0, mult_s[sl, :])
            lam_t = lam_s[sl, :]
            du = jnp.where(valid, lam_t, 0.0)
            hprev = _shift_rows(hr_ref[pl.ds(pl.multiple_of(jnp.maximum(r0 - 8, 0), 8), 8), :] * (i > 0).astype(f32), hr_ref[sl, :], 1)
            da = lam_t * hprev
            dmult = jnp.where(first, 0.0, du * ig * xcv)
            di = du * mult * xcv
            dxc = du * mult * ig
            ratio = jnp.where(valid & jnp.logical_not(first), a * a / mult, 0.0)
            dla = da * a - dmult * ratio
            dpr = (dla * (LRU_C * ls)) * r * (1.0 - r)
            dpi = di * ig * (1.0 - ig)
            dprb, dpib = dpr.astype(bf16), dpi.astype(bf16)
            dw_s[0] += _dot_tn(xb, dprb)
            dw_s[1] += _dot_tn(xb, dpib)
            dxc_s[sl, :] = dxc + _dot_nt(dprb, w_ra) + _dot_nt(dpib, w_ri)
            return d_bra + _colsum(dpr), d_bri + _colsum(dpi), d_ls + _colsum(dla * (LRU_C * r))

        d_bra, d_bri, d_ls = lax.fori_loop(0, nb, p3, (zrow, zrow, zrow))

        def p4(i, carry):
            d_cb, d_w0, d_w1, d_w2, d_w3 = carry
            r0 = pl.multiple_of(i * BLK, BLK)
            sl = pl.ds(r0, BLK)
            grow = rid + r0
            valid = grow >= ROW0
            dxc = dxc_s[sl, :]
            nxt = dxc_s[pl.ds(pl.multiple_of(jnp.minimum(r0 + BLK, rows - 8), 8), 8), :] * (i < nb - 1).astype(f32)
            ext = jnp.concatenate([dxc, nxt], axis=0)
            dxr = cw[0:1] * dxc
            for k in range(1, CONV_WIDTH):
                dxr = dxr + cw[k:k + 1] * pltpu.roll(ext, BLK + 8 - k, 0)[:BLK, :]
            dxr_ref[sl, :] = jnp.where(valid, dxr, 0.0).astype(bf16)
            cur = jnp.where(valid, xr_ref[sl, :], 0.0)
            prev8 = xr_ref[pl.ds(pl.multiple_of(jnp.maximum(r0 - 8, 0), 8), 8), :] * (i > 0).astype(f32)
            dws = [d_w0 + _colsum(dxc * cur)]
            for k, acc in ((1, d_w1), (2, d_w2), (3, d_w3)):
                dws.append(acc + _colsum(dxc * _shift_rows(prev8, cur, k)))
            return (d_cb + _colsum(dxc), *dws)

        d_cb, d_w0, d_w1, d_w2, d_w3 = lax.fori_loop(0, nb, p4, (zrow,) * 5)

        d_lam = d_ls * _sigmoid(-lam_v)
        vec_ref[...] = jnp.concatenate([d_bra, d_bri, d_lam, d_cb, d_w0, d_w1, d_w2, d_w3], axis=0)
        dw_ref[:, 0] = dw_s[0].astype(bf16).reshape(N_DEV, 32, RNN_BLOCK)
        dw_ref[:, 1] = dw_s[1].astype(bf16).reshape(N_DEV, 32, RNN_BLOCK)

    return pl.pallas_call(
        body, grid=(N_RNN_BLOCKS,),
        in_specs=[blk, blk, blk, col(0), col(OFF_GR)] + _RNN_IN_SPECS(rows),
        out_specs=[blk, blk,
                   pl.BlockSpec((N_DEV, 2, None, 32, RNN_BLOCK), lambda n: (0, 0, n, 0, 0)),
                   pl.BlockSpec((8, RNN_BLOCK), lambda n: (0, n))],
        out_shape=[SDS((rows, D), bf16), SDS((rows, D), bf16),
                   SDS((N_DEV, 2, N_RNN_BLOCKS, 32, RNN_BLOCK), bf16), SDS((8, D), f32)],
        scratch_shapes=[pltpu.VMEM((rows, RNN_BLOCK), f32)] * 6 + [pltpu.VMEM((2, RNN_BLOCK, RNN_BLOCK), f32)],
        name="rnn_bwd", compiler_params=_cp(("arbitrary",), 48),
    )(dya, hr, xc, z, z, smallw, conv_b, wrg, b_ra, b_ri, lam)


def _rope_tables(rows):
    half = jnp.arange(HALF, dtype=f32)
    inv = ROPE_THETA ** (-half / HALF)
    pos = (jnp.arange(rows) - ROW0).astype(f32)
    ang = pos[:, None] * inv[None, :]
    cos, sin = jnp.cos(ang), jnp.sin(ang)
    cos128 = jnp.concatenate([cos, cos, cos, cos], axis=1)
    sin128 = jnp.concatenate([-sin, sin, -sin, sin], axis=1)
    return cos128, sin128


def _rope128(x, cos128, sin128):
    lane = lax.broadcasted_iota(jnp.int32, x.shape, 1)
    swapped = jnp.where(lane % HEAD_DIM < HALF, pltpu.roll(x, 128 - HALF, 1), pltpu.roll(x, HALF, 1))
    return x * cos128 + swapped * sin128


def _qkv_prep(z, cos128, sin128):
    rows = z.shape[0]

    def body(q_ref, kv_ref, c_ref, s_ref, qo_ref, ko_ref, vo_ref):
        c, s = c_ref[...], s_ref[...]
        for g in range(D // 128):
            qo_ref[:, g * 128:(g + 1) * 128] = (_rope128(q_ref[:, g * 128:(g + 1) * 128], c, s)
                                                * (HEAD_DIM ** -0.5)).astype(bf16)
        for g in range(2):
            kr = _rope128(kv_ref[:, g * 128:(g + 1) * 128], c, s)
            for j in range(2):
                ko_ref[2 * g + j] = kr[:, j * HEAD_DIM:(j + 1) * HEAD_DIM].astype(bf16)
        for h in range(N_KV):
            vo_ref[h] = kv_ref[:, 256 + h * HEAD_DIM:256 + (h + 1) * HEAD_DIM].astype(bf16)

    return pl.pallas_call(
        body, grid=(rows // BLK,),
        in_specs=[pl.BlockSpec((BLK, D), lambda i: (i, OFF_Q // D)),
                  pl.BlockSpec((BLK, 512), lambda i: (i, OFF_K // 512)),
                  pl.BlockSpec((BLK, 128), lambda i: (i, 0)),
                  pl.BlockSpec((BLK, 128), lambda i: (i, 0))],
        out_specs=[pl.BlockSpec((BLK, D), lambda i: (i, 0)),
                   pl.BlockSpec((N_KV, BLK, HEAD_DIM), lambda i: (0, i, 0)),
                   pl.BlockSpec((N_KV, BLK, HEAD_DIM), lambda i: (0, i, 0))],
        out_shape=[SDS((rows, D), bf16), SDS((N_KV, rows, HEAD_DIM), bf16), SDS((N_KV, rows, HEAD_DIM), bf16)],
        name="qkv_prep", compiler_params=_cp(("arbitrary",)),
    )(z, z, cos128, sin128)


def _attn_mask(n):
    qi = n * BLK + lax.broadcasted_iota(jnp.int32, (BLK, 2 * BLK + N_META), 0)
    c = lax.broadcasted_iota(jnp.int32, (BLK, 2 * BLK + N_META), 1)
    jb = (n - 1) * BLK + c
    band = (jb >= BLK) & (jb <= qi) & (qi - jb < BLK)
    meta = (ROW0 + c - 2 * BLK) <= qi
    return ((c < 2 * BLK) & band) | ((c >= 2 * BLK) & meta)


N_KEYS = 2 * BLK + N_META


def _stack_heads(t):
    return jnp.concatenate([t[:, g * HEAD_DIM:(g + 1) * HEAD_DIM] for g in range(GROUP)], axis=0)


def _sink_column(sink_ref, h):
    g = lax.broadcasted_iota(jnp.int32, (GROUP, 1, 1), 0)
    col = jnp.zeros((GROUP, 1, 1), f32)
    for j in range(GROUP):
        col = jnp.where(g == j, sink_ref[h * GROUP + j], col)
    return col


def _kv_specs(last):
    cl = lambda n: jnp.minimum(n, last)
    return [pl.BlockSpec((None, N_META, HEAD_DIM), lambda h, n: (h, ROW0 // N_META, 0)),
            pl.BlockSpec((None, BLK, HEAD_DIM), lambda h, n: (h, jnp.maximum(cl(n) - 1, 0), 0)),
            pl.BlockSpec((None, BLK, HEAD_DIM), lambda h, n: (h, cl(n), 0))]


def _attn_fwd(q_r, k_r, v_b, z, sinks):
    rows = q_r.shape[0]
    nb = rows // BLK

    def body(sink_ref, q_ref, km_ref, kp_ref, kc_ref, vm_ref, vp_ref, vc_ref, ga_ref, o_ref, yb_ref, ybt_ref, lse_ref):
        h, n = pl.program_id(0), pl.program_id(1)
        kk = jnp.concatenate([kp_ref[...], kc_ref[...], km_ref[...]], axis=0)
        vv = jnp.concatenate([vp_ref[...], vc_ref[...], vm_ref[...]], axis=0)
        q2 = _stack_heads(q_ref[...])
        s = jnp.where(_attn_mask(n)[None], _dot_nt(q2, kk).reshape(GROUP, BLK, N_KEYS), NEG_INF)
        sink = _sink_column(sink_ref, h)
        m = jnp.maximum(jnp.max(s, axis=-1, keepdims=True), sink)
        p = jnp.exp(s - m)
        den = jnp.sum(p, axis=-1, keepdims=True) + jnp.exp(sink - m)
        o2 = _dot((p / den).astype(bf16).reshape(GROUP * BLK, N_KEYS), vv)
        lse = m + jnp.log(den)
        for g in range(GROUP):
            o_ref[:, g * HEAD_DIM:(g + 1) * HEAD_DIM] = o2[g * BLK:(g + 1) * BLK]
            lse_ref[:, g:g + 1] = lse[g]
        yb = o_ref[...] * _silu_and_grad(ga_ref[...])[0]
        yb_ref[...] = yb.astype(bf16)
        ybt_ref[...] = yb.T.astype(bf16)

    tile = pl.BlockSpec((BLK, 512), lambda h, n: (n, h))
    return pl.pallas_call(
        body, grid=(N_KV, nb),
        in_specs=[pl.BlockSpec(memory_space=pltpu.SMEM), tile] + _kv_specs(nb - 1) + _kv_specs(nb - 1)
                 + [pl.BlockSpec((BLK, 512), lambda h, n: (n, OFF_GA // 512 + h))],
        out_specs=[tile, tile, pl.BlockSpec((512, BLK), lambda h, n: (h, n)),
                   pl.BlockSpec((None, BLK, GROUP), lambda h, n: (h, n, 0))],
        out_shape=[SDS((rows, D), f32), SDS((rows, D), bf16), SDS((D, rows), bf16),
                   SDS((N_KV, rows, GROUP), f32)],
        name="attn_fwd", compiler_params=_cp(("arbitrary", "arbitrary")),
    )(sinks, q_r, k_r, k_r, k_r, v_b, v_b, v_b, z)


def _attn_bwd(dyb, o32, lse, q_r, k_r, v_b, z, sinks):
    rows = q_r.shape[0]
    nb = rows // BLK
    cl = lambda n: jnp.minimum(n, nb - 1)

    def body(sink_ref, dyb_ref, o_ref, lse_ref, q_ref, km_ref, kp_ref, kc_ref, vm_ref, vp_ref, vc_ref, ga_ref,
             dq_ref, dga_ref, dk_ref, dv_ref, dkm_ref, dvm_ref, dsr_ref, ck_s, cv_s):
        h, n = pl.program_id(0), pl.program_id(1)

        @pl.when(n == 0)
        def _():
            dkm_ref[...] = jnp.zeros_like(dkm_ref)
            dvm_ref[...] = jnp.zeros_like(dvm_ref)
            ck_s[...] = jnp.zeros_like(ck_s)
            cv_s[...] = jnp.zeros_like(cv_s)

        @pl.when(n < nb)
        def _():
            kk = jnp.concatenate([kp_ref[...], kc_ref[...], km_ref[...]], axis=0)
            vv = jnp.concatenate([vp_ref[...], vc_ref[...], vm_ref[...]], axis=0)
            sg, dsg = _silu_and_grad(ga_ref[...])
            dyb_v = dyb_ref[...]
            o_v = o_ref[...]
            dga_ref[...] = (dyb_v * o_v * dsg).astype(bf16)
            q2 = _stack_heads(q_ref[...])
            do2 = _stack_heads(dyb_v * sg)
            lse_v = lse_ref[...]
            lse = jnp.concatenate([lse_v[:, g:g + 1] for g in range(GROUP)], axis=0).reshape(GROUP, BLK, 1)
            delta = jnp.sum(do2 * _stack_heads(o_v), axis=-1, keepdims=True).reshape(GROUP, BLK, 1)
            s = jnp.where(_attn_mask(n)[None], _dot_nt(q2, kk).reshape(GROUP, BLK, N_KEYS), NEG_INF)
            p = jnp.exp(s - lse)
            do2b = do2.astype(bf16)
            ds = (p * (_dot_nt(do2b, vv).reshape(GROUP, BLK, N_KEYS) - delta)).astype(bf16)
            ds = ds.reshape(GROUP * BLK, N_KEYS)
            dsr = -jnp.exp(_sink_column(sink_ref, h) - lse) * delta
            dq2 = _dot(ds, kk)
            for g in range(GROUP):
                dq_ref[:, g * HEAD_DIM:(g + 1) * HEAD_DIM] = dq2[g * BLK:(g + 1) * BLK]
                dsr_ref[:, g:g + 1] = dsr[g]
            dkk = _dot_tn(ds, q2)
            dvv = _dot_tn(p.astype(bf16).reshape(GROUP * BLK, N_KEYS), do2b)
            dk_ref[...] = ck_s[...] + dkk[:BLK]
            dv_ref[...] = cv_s[...] + dvv[:BLK]
            ck_s[...] = dkk[BLK:2 * BLK]
            cv_s[...] = dvv[BLK:2 * BLK]
            dkm_ref[...] += dkk[2 * BLK:]
            dvm_ref[...] += dvv[2 * BLK:]

        @pl.when(n == nb)
        def _():
            dk_ref[...] = ck_s[...]
            dv_ref[...] = cv_s[...]

    tile = pl.BlockSpec((BLK, 512), lambda h, n: (cl(n), h))
    kvout = pl.BlockSpec((None, BLK, HEAD_DIM), lambda h, n: (h, jnp.maximum(n - 1, 0), 0))
    mout = pl.BlockSpec((None, N_META, HEAD_DIM), lambda h, n: (h, 0, 0))
    stat = pl.BlockSpec((None, BLK, GROUP), lambda h, n: (h, cl(n), 0))
    return pl.pallas_call(
        body, grid=(N_KV, nb + 1),
        in_specs=[pl.BlockSpec(memory_space=pltpu.SMEM), tile, tile, stat, tile] + _kv_specs(nb - 1)
                 + _kv_specs(nb - 1) + [pl.BlockSpec((BLK, 512), lambda h, n: (cl(n), OFF_GA // 512 + h))],
        out_specs=[tile, tile, kvout, kvout, mout, mout, stat],
        out_shape=[SDS((rows, D), f32), SDS((rows, D), bf16),
                   SDS((N_KV, rows, HEAD_DIM), f32), SDS((N_KV, rows, HEAD_DIM), f32),
                   SDS((N_KV, N_META, HEAD_DIM), f32), SDS((N_KV, N_META, HEAD_DIM), f32),
                   SDS((N_KV, rows, GROUP), f32)],
        scratch_shapes=[pltpu.VMEM((BLK, HEAD_DIM), f32), pltpu.VMEM((BLK, HEAD_DIM), f32)],
        name="attn_bwd", compiler_params=_cp(("arbitrary", "arbitrary")),
    )(sinks, dyb, o32, lse, q_r, k_r, k_r, k_r, v_b, v_b, v_b, z)


def _qkv_finish(dq, dk, dv, dkm, dvm, cos128, sin128):
    rows = dq.shape[0]

    def body(dq_ref, dk_ref, dv_ref, dkm_ref, dvm_ref, c_ref, s_ref, oq_ref, okv_ref):
        first = (pl.program_id(0) == 0).astype(f32)
        c, s = c_ref[...], -s_ref[...]
        for g in range(D // 128):
            oq_ref[:, g * 128:(g + 1) * 128] = (_rope128(dq_ref[:, g * 128:(g + 1) * 128], c, s)
                                                * (HEAD_DIM ** -0.5)).astype(bf16)
        pad = jnp.zeros((ROW0, HEAD_DIM), f32)
        ks = [dk_ref[h] + first * jnp.concatenate([pad, dkm_ref[h]], axis=0) for h in range(N_KV)]
        vs = [dv_ref[h] + first * jnp.concatenate([pad, dvm_ref[h]], axis=0) for h in range(N_KV)]
        for g in range(2):
            kp = jnp.concatenate([ks[2 * g], ks[2 * g + 1]], axis=1)
            okv_ref[:, g * 128:(g + 1) * 128] = _rope128(kp, c, s).astype(bf16)
            okv_ref[:, 256 + g * 128:256 + (g + 1) * 128] = jnp.concatenate([vs[2 * g], vs[2 * g + 1]], axis=1).astype(bf16)

    kv = pl.BlockSpec((N_KV, BLK, HEAD_DIM), lambda i: (0, i, 0))
    mt = pl.BlockSpec((N_KV, N_META, HEAD_DIM), lambda i: (0, 0, 0))
    return pl.pallas_call(
        body, grid=(rows // BLK,),
        in_specs=[pl.BlockSpec((BLK, D), lambda i: (i, 0)), kv, kv, mt, mt,
                  pl.BlockSpec((BLK, 128), lambda i: (i, 0)), pl.BlockSpec((BLK, 128), lambda i: (i, 0))],
        out_specs=[pl.BlockSpec((BLK, D), lambda i: (i, 0)), pl.BlockSpec((BLK, 512), lambda i: (i, 0))],
        out_shape=[SDS((rows, D), bf16), SDS((rows, 512), bf16)],
        name="qkv_finish", compiler_params=_cp(("arbitrary",)),
    )(dq, dk, dv, dkm, dvm, cos128, sin128)


_TW = 512


def _mix_specs(rows):
    tr = _row_chunk(rows)
    tile = pl.BlockSpec((tr, _TW), lambda i, j: (i, j))
    ga = pl.BlockSpec((tr, _TW), lambda i, j: (i, OFF_G // _TW + j))
    gb = pl.BlockSpec((tr, _TW), lambda i, j: (i, (OFF_G + D) // _TW + j))
    return (rows // tr, D // _TW), tile, ga, gb


def _mix_fwd(y_a, y_b, z):
    rows = y_a.shape[0]
    tw = 256
    col = lambda off: pl.BlockSpec((rows, tw), lambda j: (0, off // tw + j))

    def body(ya_ref, yb_ref, ga_ref, gb_ref, o_ref, ot_ref):
        mixed = (_sigmoid(ga_ref[...]) * ya_ref[...].astype(f32)
                 + _sigmoid(gb_ref[...]) * yb_ref[...].astype(f32))
        o_ref[...] = mixed.astype(bf16)
        ot_ref[...] = mixed.T.astype(bf16)

    return pl.pallas_call(
        body, grid=(D // tw,), in_specs=[col(0), col(0), col(OFF_G), col(OFF_G + D)],
        out_specs=[col(0), pl.BlockSpec((tw, rows), lambda j: (j, 0))],
        out_shape=[SDS((rows, D), bf16), SDS((D, rows), bf16)],
        name="mix_fwd", compiler_params=_cp(("arbitrary",)),
    )(y_a, y_b, z, z)


def _mix_bwd(dmixed, y_a, y_b, z):
    rows = y_a.shape[0]
    grid, _mix_tile, _mix_ga, _mix_gb = _mix_specs(rows)

    def body(dm_ref, ya_ref, yb_ref, ga_ref, gb_ref, dya_ref, dyb_ref, dga_ref, dgb_ref):
        dm = dm_ref[...].astype(f32)
        sa, sb = _sigmoid(ga_ref[...]), _sigmoid(gb_ref[...])
        dya_ref[...] = (dm * sa).astype(bf16)
        dyb_ref[...] = (dm * sb).astype(bf16)
        dga_ref[...] = (dm * ya_ref[...].astype(f32) * sa * (1.0 - sa)).astype(bf16)
        dgb_ref[...] = (dm * yb_ref[...].astype(f32) * sb * (1.0 - sb)).astype(bf16)

    return pl.pallas_call(
        body, grid=grid, in_specs=[_mix_tile, _mix_tile, _mix_tile, _mix_ga, _mix_gb],
        out_specs=[_mix_tile] * 4, out_shape=[SDS((rows, D), bf16)] * 4,
        name="mix_bwd", compiler_params=_cp(("arbitrary", "arbitrary")),
    )(dmixed, y_a, y_b, z, z)


def _final_ln(out32, h32, tgt, ln_g, ln_b):
    rows = out32.shape[0]

    def body(o_ref, h_ref, t_ref, g_ref, b_ref, du_ref, dub_ref, st_ref):
        i = pl.program_id(0)
        g = g_ref[...]
        y, xhat, rstd = _ln_rows(ALPHA * h_ref[...] + o_ref[...], g, b_ref[...])
        e = jnp.where(i > 0, y - t_ref[0], 0.0)
        dy = e * (1.0 / D)
        du = _ln_rows_bwd(dy, g, xhat, rstd)
        du_ref[...] = du
        dub_ref[...] = du.astype(bf16)
        st = jnp.concatenate([_colsum(dy * xhat), _colsum(dy), _colsum(du), _colsum(e * e) * (0.5 / D),
                              jnp.zeros((4, D), f32)], axis=0)

        @pl.when(i == 0)
        def _():
            st_ref[...] = st

        @pl.when(i > 0)
        def _():
            st_ref[...] += st

    row = pl.BlockSpec((BLK, D), lambda i: (i, 0))
    vec = pl.BlockSpec((1, D), lambda i: (0, 0))
    return pl.pallas_call(
        body, grid=(rows // BLK,),
        in_specs=[row, row, pl.BlockSpec((1, BLK, D), lambda i: (0, jnp.maximum(i - 1, 0), 0)), vec, vec],
        out_specs=[row, row, pl.BlockSpec((8, D), lambda i: (0, 0))],
        out_shape=[SDS((rows, D), f32), SDS((rows, D), bf16), SDS((8, D), f32)],
        name="final_ln", compiler_params=_cp(("arbitrary",)),
    )(out32, h32, tgt, ln_g, ln_b)


def _step_rnn(h32, hb, z, wrg, smallw, p, zero):
    rows = z.shape[0]
    cos128, sin128 = _rope_tables(rows)
    cos128 = cos128 + zero
    xc, hr, ya, ya_t = _rnn_fwd(z, smallw, p["conv_b"] + zero, wrg, p["b_ra"], p["b_ri"], p["lru_lambda"])
    q_r, k_r, v_b = _qkv_prep(z, cos128, sin128)
    return dict(cos128=cos128, sin128=sin128, h32=h32, hb=hb, z=z, xc=xc, hr=hr, ya=ya, ya_t=ya_t,
                q_r=q_r, k_r=k_r, v_b=v_b)


def _step_attn(s, p, zero):
    sinks = p["sinks"].reshape(N_KV * GROUP) + zero[0]
    o32, yb, yb_t, lse = _attn_fwd(s["q_r"], s["k_r"], s["v_b"], s["z"], sinks)
    return dict(s, sinks=sinks, o32=o32, yb=yb, yb_t=yb_t, lse=lse)


def _step_merge(s, tgt, w3, p):
    ya, yb, z = s["ya"], s["yb"], s["z"]
    y_a = _mm(ya, w3, sel=0, out_dtype=bf16, name="mm_ya")
    y_b = _mm(yb, w3, sel=1, out_dtype=bf16, name="mm_yb")
    mixed, mixed_t = _mix_fwd(y_a, y_b, z)
    out32 = _mm(mixed, w3, sel=2, bias=p["b_o"], name="mm_out")
    du32, dub, st_out = _final_ln(out32, s["h32"], tgt, p["ln_g"], p["ln_b"])

    g_wo = _mm(mixed_t, dub, out_dtype=bf16, name="mm_dwo")
    dmixed = _mm(dub, w3, sel=2, nt=True, out_dtype=bf16, name="mm_dmixed")
    dya_b, dyb_b, dma, dmb = _mix_bwd(dmixed, y_a, y_b, z)
    g_wrnn = _mm(s["ya_t"], dya_b, out_dtype=bf16, name="mm_dwrnn")
    g_wattn = _mm(s["yb_t"], dyb_b, out_dtype=bf16, name="mm_dwattn")
    dya = _mm(dya_b, w3, sel=0, nt=True, name="mm_dya")
    dyb = _mm(dyb_b, w3, sel=1, nt=True, name="mm_dyb")
    return dict(du32=du32, st_out=st_out, dma=dma, dmb=dmb, dya=dya, dyb=dyb, g_wo=g_wo, g_wrnn=g_wrnn,
                g_wattn=g_wattn)


def _step_backward(s, t, wrg, smallw, p, conv_b):
    z = s["z"]
    dxr, dgr, g_wrg, vec_rnn = _rnn_bwd(t["dya"], s["hr"], s["xc"], z, smallw, conv_b, wrg, p["b_ra"], p["b_ri"],
                                        p["lru_lambda"])
    dq_r, dga, dk, dv, dkm, dvm, dsr = _attn_bwd(t["dyb"], s["o32"], s["lse"], s["q_r"], s["k_r"], s["v_b"], z,
                                                 s["sinks"])
    dq, dkv = _qkv_finish(dq_r, dk, dv, dkm, dvm, s["cos128"], s["sin128"])
    dz_parts = [(dxr, D), (dgr, D), (dq, D), (dkv, 512), (dga, D), (t["dma"], D), (t["dmb"], D)]
    return dict(vec_rnn=vec_rnn, dsr=dsr, g_wrg=g_wrg, dz_parts=dz_parts)


def _step_input_grad(dh_lo, dh_hi, du32, x, smallw, p, after):
    grad_x, dmeta, st_emb = _ln_emb_bwd(dh_lo, dh_hi, du32, x, smallw, p["ln_emb_g"], after)
    return dict(grad_x=grad_x, dmeta=dmeta, st_emb=st_emb)


_ANY = pl.BlockSpec(memory_space=pl.ANY)
_VMEM = pl.BlockSpec(memory_space=pltpu.VMEM)
_HBM = pl.BlockSpec(memory_space=pltpu.HBM)
_SEM = pl.BlockSpec(memory_space=pltpu.SEMAPHORE)


def _place():
    x, y, c = lax.axis_index("x"), lax.axis_index("y"), lax.axis_index("c")
    return x, y, c


def _dev(px, py, pc):
    return 4 * px + 2 * py + pc


def _tile_rows(r):
    return max(t for t in range(16, 321, 16) if r % t == 0) if r > 320 else r


def _cast_w_in(w_in_t, me_idx):
    tm = _tile_rows(SHARD_IN)

    def body(me_ref, i_ref, o_ref):
        o_ref[...] = i_ref[...].astype(bf16)

    return pl.pallas_call(
        body,
        grid_spec=pltpu.PrefetchScalarGridSpec(
            num_scalar_prefetch=1, grid=(SHARD_IN // tm,),
            in_specs=[pl.BlockSpec((tm, D), lambda i, me_ref: (i, 0))],
            out_specs=pl.BlockSpec((None, tm, D), lambda i, me_ref: (me_ref[0], i, 0))),
        out_shape=SDS((N_DEV, SHARD_IN, D), bf16), name="cast_w_in", compiler_params=_cp(("arbitrary",)),
    )(me_idx, w_in_t)


def _cast_small(me_idx, w_rnn_out, w_attn_out, w_o, w_ra, w_ri, meta, conv_w):
    def body(me_ref, a_ref, b_ref, c_ref, ra_ref, ri_ref, m_ref, cw_ref, w3_ref, wrg_ref, sw_ref):
        w3_ref[0] = a_ref[0].astype(bf16)
        w3_ref[1] = b_ref[0].astype(bf16)
        w3_ref[2] = c_ref[0].astype(bf16)
        wrg_ref[0] = ra_ref[0].astype(bf16)
        wrg_ref[1] = ri_ref[0].astype(bf16)
        sw_ref[...] = jnp.concatenate([m_ref[...], cw_ref[0], jnp.zeros((4, 256), f32)], axis=0)

    args = (w_rnn_out, w_attn_out, w_o, w_ra, w_ri, meta, conv_w)
    whole = lambda shape: pl.BlockSpec(shape, lambda i, me_ref: (0,) * len(shape))
    slot = lambda shape: pl.BlockSpec((None, *shape), lambda i, me_ref: (me_ref[0], *([0] * len(shape))))
    shapes = [(3, 256, D), (2, N_RNN_BLOCKS, 32, RNN_BLOCK), (24, 256)]
    return pl.pallas_call(
        body,
        grid_spec=pltpu.PrefetchScalarGridSpec(
            num_scalar_prefetch=1, grid=(1,), in_specs=[whole(a.shape) for a in args],
            out_specs=[slot(sh) for sh in shapes]),
        out_shape=[SDS((N_DEV, *sh), dt) for sh, dt in zip(shapes, (bf16, bf16, f32))],
        name="cast_small", compiler_params=_cp(("arbitrary",)),
    )(me_idx, *args)


def _remote(src, dst, send_sems, recv_sems, k, to):
    return pltpu.make_async_remote_copy(src_ref=src, dst_ref=dst, send_sem=send_sems.at[k], recv_sem=recv_sems.at[k],
                                        device_id=to, device_id_type=MESH)


def _all_gather(bufs):
    n = len(bufs)

    def body(*refs):
        outs = refs[n:2 * n]
        send_sems, recv_sems = refs[2 * n:]
        x, y, c = _place()
        me, sibling = (x, y, c), (x, y, 1 - c)
        chips = [(1 - x, y), (x, 1 - y), (1 - x, 1 - y)]

        def copy(a, k, block, to):
            blk = outs[a].at[_dev(*block)]
            return _remote(blk, blk, send_sems, recv_sems, a * 7 + k, to)

        first = []
        for a in range(n):
            first.append(copy(a, 0, me, sibling))
            first += [copy(a, 1 + j, me, (*chip, c)) for j, chip in enumerate(chips)]
        for cp in first:
            cp.start()
        passed = []
        for a in range(n):
            for j, chip in enumerate(chips):
                copy(a, 1 + j, (*chip, c), me).wait_recv()
                cp = copy(a, 4 + j, (*chip, c), sibling)
                cp.start()
                passed.append(cp)
        for a in range(n):
            copy(a, 0, sibling, me).wait_recv()
            for j, chip in enumerate(chips):
                copy(a, 4 + j, (*chip, 1 - c), me).wait_recv()
        for cp in first + passed:
            cp.wait_send()

    return pl.pallas_call(
        body, in_specs=[_ANY] * n, out_specs=[_ANY] * n,
        out_shape=[SDS(b.shape, b.dtype) for b in bufs],
        input_output_aliases={a: a for a in range(n)},
        scratch_shapes=[pltpu.SemaphoreType.DMA((7 * n,)), pltpu.SemaphoreType.DMA((7 * n,))],
        name="all_gather_weights",
    )(*bufs)


def _copies_own_slot(srcs, lands, send_sems, recv_sems):
    x, y, c = _place()
    out = []
    for a in range(len(srcs)):
        blk = srcs[a].at[_dev(x, y, c)]
        for k, (fx, fy, fc) in enumerate(_PEER_FLIPS):
            out.append(_remote(blk, blk, send_sems, recv_sems, 7 * a + k, ((x + fx) % 2, (y + fy) % 2, (c + fc) % 2)))
    return out


_PEER_FLIPS = [(f // 4, (f // 2) % 2, f % 2) for f in range(1, N_DEV)]


def _copies_direct(same_src):
    def make(srcs, lands, send_sems, recv_sems):
        x, y, c = _place()
        me = _dev(x, y, c)
        out = []
        for a in range(len(srcs)):
            for k, (fx, fy, fc) in enumerate(_PEER_FLIPS):
                peer = ((x + fx) % 2, (y + fy) % 2, (c + fc) % 2)
                src = srcs[a] if same_src else srcs[a].at[_dev(*peer)]
                out.append(_remote(src, lands[a].at[me], send_sems, recv_sems, 7 * a + k, peer))
        return out
    return make


def _copies_siblings(srcs, lands, send_sems, recv_sems):
    x, y, c = _place()
    return [_remote(srcs[a].at[2 * q + (1 - c)], lands[a].at[q], send_sems, recv_sems, 4 * a + q, (x, y, 1 - c))
            for a in range(len(srcs)) for q in range(4)]


def _copies_chips(srcs, lands, send_sems, recv_sems):
    x, y, c = _place()
    chips = [(1 - x, y), (x, 1 - y), (1 - x, 1 - y)]
    return [_remote(srcs[a].at[2 * qx + qy], lands[a].at[j], send_sems, recv_sems, 3 * a + j, (qx, qy, c))
            for a in range(len(srcs)) for j, (qx, qy) in enumerate(chips)]


def _split_start(make, per_array, srcs, lands, dep, name):
    n, tot = len(srcs), len(srcs) + len(lands)

    def body(*refs):
        send_sems, recv_sems, token = refs[tot + 1], refs[tot + 2], refs[-1]
        for cp in make(refs[:n], refs[n:tot], send_sems, recv_sems):
            cp.start()
        token[...] = jnp.zeros_like(token)

    hbm = lambda t: pltpu.with_memory_space_constraint(t, pltpu.HBM)
    res = pl.pallas_call(
        body, name=name,
        out_shape=(pltpu.SemaphoreType.DMA((per_array * n,)), pltpu.SemaphoreType.DMA((per_array * n,)),
                   *[pltpu.HBM(t.shape, t.dtype) for t in (*srcs, *lands)], SDS((8, 128), f32)),
        in_specs=[_HBM] * tot + [_ANY], out_specs=(_SEM, _SEM, *([_HBM] * tot), _VMEM),
        input_output_aliases={i: 2 + i for i in range(tot)},
        compiler_params=pltpu.CompilerParams(has_side_effects=pltpu.SideEffectType.DATAFLOW_SIDE_EFFECTING),
    )(*[hbm(t) for t in (*srcs, *lands)], dep)
    return res[0], res[1], list(res[2:2 + n]), list(res[2 + n:2 + tot]), res[-1]


def _split_wait(make, send_sems, recv_sems, srcs, lands, after, name):
    n, tot = len(srcs), len(srcs) + len(lands)

    def body(*refs):
        for cp in make(refs[:n], refs[n:tot], refs[tot], refs[tot + 1]):
            cp.wait_send()
            cp.wait_recv()

    res = pl.pallas_call(
        body, name=name,
        out_shape=tuple(pltpu.HBM(t.shape, t.dtype) for t in (*srcs, *lands)),
        in_specs=[_HBM] * tot + [_SEM, _SEM, _ANY], out_specs=tuple([_HBM] * tot),
        input_output_aliases={i: i for i in range(tot)},
        compiler_params=pltpu.CompilerParams(has_side_effects=pltpu.SideEffectType.DATAFLOW_SIDE_EFFECTING),
    )(*srcs, *lands, send_sems, recv_sems, after)
    return list(res[:n]), list(res[n:])


def _adamw_direct(g, land, me_idx, w, m, v, name):
    r, wd = w.shape
    tr = min(r, 256)

    def body(me_ref, *refs):
        g_ref, peers = refs[0], refs[1:N_DEV]
        w_ref, m_ref, v_ref, g_out, d_out, m_out, v_out = refs[N_DEV:]
        gs = g_ref[...].astype(f32)
        for p_ref in peers:
            gs = gs + p_ref[...].astype(f32)
        d, mn, vn = _adamw(w_ref[...], gs, m_ref[...], v_ref[...])
        g_out[...] = gs
        d_out[...] = d
        m_out[...] = mn
        v_out[...] = vn

    tile = pl.BlockSpec((tr, wd), lambda i, me_ref: (i, 0))
    slot = lambda k: pl.BlockSpec((None, tr, wd), lambda i, me_ref: ((me_ref[0] + k) % N_DEV, i, 0))
    return pl.pallas_call(
        body,
        grid_spec=pltpu.PrefetchScalarGridSpec(
            num_scalar_prefetch=1, grid=(r // tr,),
            in_specs=[slot(0)] + [slot(k) for k in range(1, N_DEV)] + [tile, tile, tile],
            out_specs=[tile] * 4),
        out_shape=[SDS((r, wd), f32)] * 4, name=name, compiler_params=_cp(("arbitrary",), 48),
    )(me_idx, g, *([land] * (N_DEV - 1)), w, m, v)


def _pair_sum(g, r1, c_idx, name):
    _, r, w = g.shape
    tr = _tile_rows(r)

    def body(c_ref, g_ref, r_ref, o_ref):
        o_ref[...] = (g_ref[...].astype(f32) + r_ref[...].astype(f32)).astype(bf16)

    return pl.pallas_call(
        body,
        grid_spec=pltpu.PrefetchScalarGridSpec(
            num_scalar_prefetch=1, grid=(4, r // tr),
            in_specs=[pl.BlockSpec((None, tr, w), lambda q, i, c_ref: (2 * q + c_ref[0], i, 0)),
                      pl.BlockSpec((None, tr, w), lambda q, i, c_ref: (q, i, 0))],
            out_specs=pl.BlockSpec((None, tr, w), lambda q, i, c_ref: (q, i, 0))),
        out_shape=SDS((4, r, w), bf16), name=name, compiler_params=_cp(("arbitrary", "arbitrary")),
    )(c_idx, g, r1)


def _adamw(w, g, m, v):
    m = ADAM_B1 * m + (1.0 - ADAM_B1) * g
    v = ADAM_B2 * v + (1.0 - ADAM_B2) * (g * g)
    m_hat = m / (1.0 - ADAM_B1 ** ADAM_STEP)
    v_hat = v / (1.0 - ADAM_B2 ** ADAM_STEP)
    delta = -ADAM_LR * (m_hat / (jnp.sqrt(v_hat) + ADAM_EPS) + ADAM_WD * w)
    return delta, m, v


def _adamw_big(part, r2, q_idx, w, m, v, name, row_off=0, cols=(0, 1), prev=None):
    r, wd = w.shape
    tr = _tile_rows(r)
    k, ncol = cols
    wp = wd // ncol

    def body(q_ref, p_ref, r_ref, w_ref, m_ref, v_ref, *rest):
        g_out, d_out, m_out, v_out = rest[-4:]
        g = p_ref[...].astype(f32)
        for j in range(3):
            g = g + r_ref[j].astype(f32)
        d, mn, vn = _adamw(w_ref[...], g, m_ref[...], v_ref[...])
        g_out[...] = g
        d_out[...] = d
        m_out[...] = mn
        v_out[...] = vn

    tile = pl.BlockSpec((tr, wp), lambda i, q_ref: (i, k))
    prev = list(prev) if prev is not None else []
    return pl.pallas_call(
        body,
        grid_spec=pltpu.PrefetchScalarGridSpec(
            num_scalar_prefetch=1, grid=(r // tr,),
            in_specs=[pl.BlockSpec((None, tr, wp), lambda i, q_ref: (q_ref[0], row_off + i, 0)),
                      pl.BlockSpec((3, tr, wp), lambda i, q_ref: (0, row_off + i, 0)), tile, tile, tile]
                     + [pl.BlockSpec(memory_space=pl.ANY)] * len(prev),
            out_specs=[tile] * 4),
        out_shape=[SDS((r, wd), f32)] * 4, name=name,
        input_output_aliases={6 + i: i for i in range(len(prev))},
        compiler_params=_cp(("arbitrary",), 48),
    )(q_idx, part, r2, w, m, v, *prev)


_SMALL_ROWS = 24


def _pack_early(vec_rnn, st_out, dsr, db_in):
    def body(vr_ref, so_ref, dsr_ref, db_ref, sm_ref, sm2_ref):
        sm_ref[...] = jnp.zeros_like(sm_ref)
        sm2_ref[...] = jnp.zeros_like(sm2_ref)
        sm_ref[2:3, :] = vr_ref[3:4, :]
        sm_ref[3:6, :] = vr_ref[0:3, :]
        sm_ref[6:7, :] = so_ref[2:3, :]
        sm_ref[7:9, :] = so_ref[0:2, :]
        sm_ref[10:11, :] = so_ref[3:4, :]
        for h in range(N_KV):
            sm_ref[9:10, h * GROUP:(h + 1) * GROUP] = _colsum(dsr_ref[h])
        for j in range(6):
            sm_ref[16 + j:17 + j, :] = db_ref[0:1, j * D:(j + 1) * D]
        sm_ref[22:23, 0:D_IN - 6 * D] = db_ref[0:1, 6 * D:D_IN]
        for s in range(N_DEV):
            sm2_ref[s, 0:CONV_WIDTH, :] = vr_ref[4:8, s * 256:(s + 1) * 256]

    return pl.pallas_call(
        body, out_shape=[SDS((_SMALL_ROWS, D), f32), SDS((N_DEV, 8, 256), f32)],
        name="pack_early", compiler_params=_cp(None),
    )(vec_rnn, st_out, dsr, db_in)


def _pack_late(st_emb, dmeta):
    def body(se_ref, dm_ref, sm_ref, sm2_ref):
        sm_ref[...] = se_ref[...]
        for s in range(N_DEV):
            sm2_ref[s] = dm_ref[:, s * 256:(s + 1) * 256]

    return pl.pallas_call(
        body, out_shape=[SDS((8, D), f32), SDS((N_DEV, N_META, 256), f32)],
        name="pack_late", compiler_params=_cp(None),
    )(st_emb, dmeta)


def _small_allreduce(sm, sm2):
    def body(sm_ref, sm2_ref, o_ref, o2_ref, buf, buf2, send_sems, recv_sems):
        x, y, c = _place()
        me = _dev(x, y, c)
        copies = []
        for f in range(1, N_DEV):
            fx, fy, fc = f // 4, (f // 2) % 2, f % 2
            peer = ((x + fx) % 2, (y + fy) % 2, (c + fc) % 2)
            for t, (src, dst) in enumerate(((sm_ref, buf), (sm2_ref, buf2))):
                k = 2 * (f - 1) + t
                copies.append(pltpu.make_async_remote_copy(
                    src_ref=src, dst_ref=dst.at[me], send_sem=send_sems.at[k], recv_sem=recv_sems.at[k],
                    device_id=peer, device_id_type=MESH))
        for cp in copies:
            cp.start()
        buf[me] = sm_ref[...]
        buf2[me] = sm2_ref[...]
        for cp in copies:
            cp.wait()
        acc, acc2 = buf[0], buf2[0]
        for e in range(1, N_DEV):
            acc, acc2 = acc + buf[e], acc2 + buf2[e]
        o_ref[...] = acc
        o2_ref[...] = acc2

    return pl.pallas_call(
        body, in_specs=[_VMEM, _VMEM], out_specs=[_VMEM, _VMEM],
        out_shape=[SDS(sm.shape, f32), SDS(sm2.shape, f32)],
        scratch_shapes=[pltpu.VMEM((N_DEV, *sm.shape), f32), pltpu.VMEM((N_DEV, *sm2.shape), f32),
                        pltpu.SemaphoreType.DMA((14,)), pltpu.SemaphoreType.DMA((14,))],
        name="small_allreduce",
    )(sm, sm2)


_SMALL_ROW_OF = {"ln_emb_g": 0, "ln_emb_b": 1, "conv_b": 2, "b_ra": 3, "b_ri": 4, "lru_lambda": 5, "b_o": 6,
                 "ln_g": 7, "ln_b": 8}
_SMALL_NAMES = ["ln_emb_g", "ln_emb_b", "conv_b", "b_ra", "b_ri", "lru_lambda", "b_o", "ln_g", "ln_b",
                "sinks", "b_in", "meta_tokens", "conv_w"]


def _small_update(me_idx, early, late, wmv):
    n_fixed = 7

    def in_order(me, own_ref, land_ref):
        acc = None
        for e in range(N_DEV):
            term = jnp.where(me == e, own_ref[...], land_ref[e])
            acc = term if acc is None else acc + term
        return acc

    def body(*refs):
        me_ref, own_ref, land_ref, cown_ref, cland_ref, late_ref, meta_ref = refs[:n_fixed]
        ins = refs[n_fixed:n_fixed + 3 * len(_SMALL_NAMES)]
        outs = refs[n_fixed + 3 * len(_SMALL_NAMES):]
        me = me_ref[0]
        sm = in_order(me, own_ref, land_ref)
        conv = in_order(me, cown_ref, cland_ref)

        def grad_of(name):
            if name in ("ln_emb_g", "ln_emb_b"):
                r = _SMALL_ROW_OF[name]
                return late_ref[r:r + 1, :]
            if name in _SMALL_ROW_OF:
                r = _SMALL_ROW_OF[name]
                return sm[r:r + 1, :]
            if name == "sinks":
                return sm[9:10, 0:N_KV * GROUP]
            if name == "b_in":
                return jnp.concatenate([sm[16 + j:17 + j, :] for j in range(7)], axis=1)[:, :D_IN]
            if name == "meta_tokens":
                return meta_ref[...]
            return conv[0:CONV_WIDTH, :]

        for i, name in enumerate(_SMALL_NAMES):
            w_ref, m_ref, v_ref = ins[3 * i:3 * i + 3]
            g = grad_of(name)
            d, mn, vn = _adamw(w_ref[...], g, m_ref[...], v_ref[...])
            outs[4 * i][...] = g
            outs[4 * i + 1][...] = d
            outs[4 * i + 2][...] = mn
            outs[4 * i + 3][...] = vn
        outs[-1][...] = jnp.broadcast_to(jnp.sum(sm[10:11, :], axis=1, keepdims=True), (8, 128))

    args, out_shape = [me_idx, *early, *late], []
    for name in _SMALL_NAMES:
        args += list(wmv[name])
        out_shape += [SDS(wmv[name][0].shape, f32)] * 4
    out_shape.append(SDS((8, 128), f32))
    res = pl.pallas_call(
        body, out_shape=out_shape, in_specs=[pl.BlockSpec(memory_space=pltpu.SMEM)] + [_VMEM] * (len(args) - 1),
        name="small_update", compiler_params=_cp(None))(*args)
    return {name: tuple(res[4 * i:4 * i + 4]) for i, name in enumerate(_SMALL_NAMES)}, res[-1][0, 0]


_WEIGHTS = ["meta_tokens", "ln_emb_g", "ln_emb_b", "w_in", "b_in", "conv_w", "conv_b", "w_ra", "b_ra", "w_ri",
            "b_ri", "lru_lambda", "sinks", "w_rnn_out", "w_attn_out", "w_o", "b_o", "ln_g", "ln_b"]
_SMALL_2D = {"meta_tokens": (N_META, 256), "conv_w": (CONV_WIDTH, 256), "b_in": (1, D_IN), "sinks": (1, N_KV * GROUP)}


def kernel(x, meta_tokens, ln_emb_g, ln_emb_b, w_in, b_in, conv_w, conv_b, w_ra, b_ra, w_ri, b_ri, lru_lambda, sinks, w_rnn_out, w_attn_out, w_o, b_o, ln_g, ln_b, loss_target, m_meta_tokens, m_ln_emb_g, m_ln_emb_b, m_w_in, m_b_in, m_conv_w, m_conv_b, m_w_ra, m_b_ra, m_w_ri, m_b_ri, m_lru_lambda, m_sinks, m_w_rnn_out, m_w_attn_out, m_w_o, m_b_o, m_ln_g, m_ln_b, v_meta_tokens, v_ln_emb_g, v_ln_emb_b, v_w_in, v_b_in, v_conv_w, v_conv_b, v_w_ra, v_b_ra, v_w_ri, v_b_ri, v_lru_lambda, v_sinks, v_w_rnn_out, v_w_attn_out, v_w_o, v_b_o, v_ln_g, v_ln_b):
    w = dict(meta_tokens=meta_tokens, ln_emb_g=ln_emb_g, ln_emb_b=ln_emb_b, w_in=w_in, b_in=b_in, conv_w=conv_w,
             conv_b=conv_b, w_ra=w_ra, b_ra=b_ra, w_ri=w_ri, b_ri=b_ri, lru_lambda=lru_lambda, sinks=sinks,
             w_rnn_out=w_rnn_out, w_attn_out=w_attn_out, w_o=w_o, b_o=b_o, ln_g=ln_g, ln_b=ln_b)
    m = dict(meta_tokens=m_meta_tokens, ln_emb_g=m_ln_emb_g, ln_emb_b=m_ln_emb_b, w_in=m_w_in, b_in=m_b_in,
             conv_w=m_conv_w, conv_b=m_conv_b, w_ra=m_w_ra, b_ra=m_b_ra, w_ri=m_w_ri, b_ri=m_b_ri,
             lru_lambda=m_lru_lambda, sinks=m_sinks, w_rnn_out=m_w_rnn_out, w_attn_out=m_w_attn_out, w_o=m_w_o,
             b_o=m_b_o, ln_g=m_ln_g, ln_b=m_ln_b)
    v = dict(meta_tokens=v_meta_tokens, ln_emb_g=v_ln_emb_g, ln_emb_b=v_ln_emb_b, w_in=v_w_in, b_in=v_b_in,
             conv_w=v_conv_w, conv_b=v_conv_b, w_ra=v_w_ra, b_ra=v_b_ra, w_ri=v_w_ri, b_ri=v_b_ri,
             lru_lambda=v_lru_lambda, sinks=v_sinks, w_rnn_out=v_w_rnn_out, w_attn_out=v_w_attn_out, w_o=v_w_o,
             b_o=v_b_o, ln_g=v_ln_g, ln_b=v_ln_b)
    px, py, pc = _place()
    as_idx = lambda t: jnp.reshape(t, (1,)).astype(jnp.int32)
    c_idx, q_idx, me_idx = as_idx(pc), as_idx(2 * px + py), as_idx(_dev(px, py, pc))

    w3_s, wrg_s, small_s = _cast_small(me_idx, w_rnn_out, w_attn_out, w_o, w_ra, w_ri, meta_tokens, conv_w)
    vec = lambda name: w[name].reshape(1, -1)
    p = {k: vec(k) for k in ("ln_emb_g", "ln_emb_b", "b_in", "conv_b", "b_ra", "b_ri", "lru_lambda", "sinks",
                             "b_o", "ln_g", "ln_b")}
    w_in_t = lambda a: jnp.swapaxes(a, 1, 2).reshape(SHARD_IN, D)
    wg, wrg, smallw = _all_gather([_cast_w_in(w_in_t(w_in), me_idx), wrg_s, small_s])
    w3_pending = _split_start(_copies_own_slot, 7, [w3_s], [], smallw, "gather_w3_start")
    w_full = wg.reshape(D_IN, D)

    zero = w3_pending[4][0:1, 0:1]
    h32, hb = _ln_emb(x, smallw, p["ln_emb_g"], p["ln_emb_b"])
    z = _mm(hb, w_full, nt=True, bias=p["b_in"] + zero, name="mm_z")
    s = _step_attn(_step_rnn(h32, hb, z, wrg, smallw, p, zero), p, zero)
    w3 = _split_wait(_copies_own_slot, *w3_pending[:4], s["lse"], "gather_w3_wait")[0][0]
    t = _step_merge(s, loss_target, w3, p)

    big = {}
    two_d = lambda name: (w[name].shape[-2], w[name].shape[-1])
    proj = ("w_o", "w_rnn_out", "w_attn_out")
    g_proj = [t[k].reshape(N_DEV, 256, D) for k in ("g_wo", "g_wrnn", "g_wattn")]
    g_pending = _split_start(_copies_direct(False), 7, g_proj, [lax.empty((N_DEV, 256, D), bf16) for _ in proj],
                             p["b_o"], "reduce_proj_start")
    u = _step_backward(s, t, wrg, smallw, p, p["conv_b"] + g_pending[4][0:1, 0:1])

    def siblings_start(gs, dep, tag):
        return _split_start(_copies_siblings, 4, gs, [lax.empty((4, *g.shape[1:]), bf16) for g in gs], dep,
                            "reduce_siblings_start_" + tag)

    def chips_start(gs, r1, dep, tag):
        parts = [_pair_sum(g, r, c_idx, "pair_sum_%s%d" % (tag, i)) for i, (g, r) in enumerate(zip(gs, r1))]
        return _split_start(_copies_chips, 3, parts, [lax.empty((3, *q.shape[1:]), bf16) for q in parts], dep,
                            "reduce_chips_start_" + tag)

    g_a, dz, db_in = _mm_dwin_parts(s["hb"], u["dz_parts"])
    shards = lambda g: g.reshape(N_DEV, SHARD_IN, W_IN_HALF)
    sib_a = siblings_start([shards(g_a), u["g_wrg"].reshape(N_DEV, 2 * RNN_BLOCK, RNN_BLOCK)], db_in, "a")
    g_proj, g_land = _split_wait(_copies_direct(False), *g_pending[:4], sib_a[4], "reduce_proj_wait")
    for i, name in enumerate(proj):
        res = _adamw_direct(g_proj[i], g_land[i], me_idx, w[name].reshape(two_d(name)), m[name].reshape(two_d(name)),
                            v[name].reshape(two_d(name)), "adamw_" + name)
        big[name] = tuple(r.reshape(w[name].shape) for r in res)
    chp_a = chips_start(*_split_wait(_copies_siblings, *sib_a[:4], big["w_attn_out"][3], "reduce_siblings_wait_a"),
                        db_in, "a")
    g_b = _mm_dwin(s["hb"], dz, chp_a[4])
    sib_b = siblings_start([shards(g_b)], db_in, "b")
    sm_e = _pack_early(u["vec_rnn"], t["st_out"], u["dsr"], db_in)
    early = _split_start(_copies_direct(True), 7, list(sm_e),
                         [lax.empty((N_DEV, *a.shape), f32) for a in sm_e], sib_b[4], "small_early_start")
    dh_lo = _mm_dh(dz, w_full, early[4], 0)
    chp_b = chips_start(*_split_wait(_copies_siblings, *sib_b[:4], dh_lo, "reduce_siblings_wait_b"), db_in, "b")
    dh_hi = _mm_dh(dz, w_full, chp_b[4], 1)
    parts_a, r2_a = _split_wait(_copies_chips, *chp_a[:4], dh_hi, "reduce_chips_wait_a")
    w_in_res = _adamw_big(parts_a[0], r2_a[0], q_idx, w_in_t(w["w_in"]), w_in_t(m["w_in"]), w_in_t(v["w_in"]),
                          "adamw_w_in_a", cols=(0, 2))
    u.update(_step_input_grad(dh_lo, dh_hi, t["du32"], x, smallw, p, w_in_res[3]))
    sm_l, meta_l = _small_allreduce(*_pack_late(u["st_emb"], u["dmeta"]))
    (sm_own, conv_own), (sm_land, conv_land) = _split_wait(_copies_direct(True), *early[:4], sm_l, "small_early_wait")
    me = _dev(px, py, pc)
    mine = lambda a, axis: lax.dynamic_index_in_dim(a, me, axis, keepdims=False)
    two = lambda name, t: t.reshape(_SMALL_2D.get(name, (1, D)))
    small, loss = _small_update(me_idx, (sm_own, sm_land, mine(conv_own, 0), mine(conv_land, 1)),
                                (sm_l, mine(meta_l, 0)),
                                {k: (two(k, w[k]), two(k, m[k]), two(k, v[k])) for k in _SMALL_NAMES})

    parts_b, r2_b = _split_wait(_copies_chips, *chp_b[:4], small["b_in"][2], "reduce_chips_wait_b")
    res = _adamw_big(parts_b[0], r2_b[0], q_idx, w_in_t(w["w_in"]), w_in_t(m["w_in"]), w_in_t(v["w_in"]),
                     "adamw_w_in_b", cols=(1, 2), prev=w_in_res)
    big["w_in"] = tuple(jnp.swapaxes(r.reshape(1, SHARD_IN, D), 1, 2) for r in res)
    for i, name in enumerate(("w_ra", "w_ri")):
        sq = (RNN_BLOCK, RNN_BLOCK)
        res = _adamw_big(parts_a[1], r2_a[1], q_idx, w[name].reshape(sq), m[name].reshape(sq), v[name].reshape(sq),
                         "adamw_" + name, row_off=i)
        big[name] = tuple(r.reshape(w[name].shape) for r in res)
    res = dict(big)
    for k in _SMALL_NAMES:
        res[k] = tuple(t.reshape(w[k].shape) for t in small[k])

    outs = [loss, u["grad_x"]]
    for j in range(4):
        outs += [res[k][j] for k in _WEIGHTS]
    return tuple(outs)
```

```python
import jax
import jax.numpy as jnp
from jax import lax
from jax.experimental import pallas as pl
from jax.experimental.pallas import tpu as pltpu

f32, bf16 = jnp.float32, jnp.bfloat16
SDS = jax.ShapeDtypeStruct

N_DEV = 8
D = 2048
N_META = 16
BLK = 128
ROW0 = BLK - N_META
N_RNN_BLOCKS = 8
RNN_BLOCK = D // N_RNN_BLOCKS
CONV_WIDTH = 4
LRU_C = 8.0
HEAD_DIM = 64
N_KV = 4
GROUP = 8
HALF = HEAD_DIM // 2
ROPE_THETA = 10000.0
NEG_INF = -1e30
LN_EPS = 1e-5
ALPHA = 2.0 ** 0.25
D_IN = 12800
SHARD_IN = D_IN // N_DEV
W_IN_HALF = D // 2
OFF_GR, OFF_Q, OFF_K, OFF_V, OFF_GA, OFF_G = 2048, 4096, 6144, 6400, 6656, 8704
ADAM_LR, ADAM_B1, ADAM_B2, ADAM_EPS, ADAM_WD, ADAM_STEP = 1e-3, 0.9, 0.999, 1e-8, 0.01, 10
VMEM_LIMIT_MB = 56
MESH = pl.DeviceIdType.MESH


def _cp(sem=None, vmem_mb=40):
    return pltpu.CompilerParams(dimension_semantics=sem, vmem_limit_bytes=vmem_mb * 2 ** 20)


def _row_chunk(m):
    best = 16
    for c in range(16, 641, 16):
        if m % c == 0:
            best = c
    return best


def _sigmoid(x):
    return 1.0 / (1.0 + jnp.exp(-x))


def _silu_and_grad(x):
    s = _sigmoid(x)
    return x * s, s * (1.0 + x * (1.0 - s))


def _log_sigmoid(x):
    return jnp.minimum(x, 0.0) - jnp.log1p(jnp.exp(-jnp.abs(x)))


def _ln_rows(v, g, b):
    mu = jnp.mean(v, axis=-1, keepdims=True)
    c = v - mu
    var = jnp.mean(c * c, axis=-1, keepdims=True)
    rstd = lax.rsqrt(var + LN_EPS)
    xhat = c * rstd
    return xhat * g + b, xhat, rstd


def _ln_rows_bwd(dy, g, xhat, rstd):
    dxh = dy * g
    m1 = jnp.mean(dxh, axis=-1, keepdims=True)
    m2 = jnp.mean(dxh * xhat, axis=-1, keepdims=True)
    return rstd * (dxh - m1 - xhat * m2)


def _colsum(v):
    return jnp.sum(v, axis=0, keepdims=True)


def _dot(a, b):
    return jnp.dot(a, b, preferred_element_type=f32)


def _dot_nt(a, b):
    return lax.dot_general(a, b, (((1,), (1,)), ((), ())), preferred_element_type=f32)


def _dot_tn(a, b):
    return lax.dot_general(a, b, (((0,), (0,)), ((), ())), preferred_element_type=f32)


def _meta_full(sw_ref):
    return jnp.concatenate([sw_ref[s, 0:N_META, :] for s in range(N_DEV)], axis=1)


def _ln_emb(x, smallw, g_e, b_e):
    seq = x.shape[1]
    rows = seq + BLK
    nb = rows // BLK

    def body(x_ref, sw_ref, g_ref, b_ref, h32_ref, hb_ref):
        i = pl.program_id(0)
        g, b = g_ref[...], b_ref[...]

        def emit(blk):
            h32_ref[...] = blk
            hb_ref[...] = blk.astype(bf16)

        @pl.when(i == 0)
        def _():
            hm = _ln_rows(_meta_full(sw_ref), g, b)[0]
            emit(jnp.concatenate([jnp.zeros((ROW0, D), f32), hm], axis=0))

        @pl.when(i > 0)
        def _():
            emit(_ln_rows(x_ref[0], g, b)[0])

    return pl.pallas_call(
        body, grid=(nb,),
        in_specs=[pl.BlockSpec((1, BLK, D), lambda i: (0, jnp.maximum(i - 1, 0), 0)),
                  pl.BlockSpec((N_DEV, 24, 256), lambda i: (0, 0, 0)),
                  pl.BlockSpec((1, D), lambda i: (0, 0)),
                  pl.BlockSpec((1, D), lambda i: (0, 0))],
        out_specs=[pl.BlockSpec((BLK, D), lambda i: (i, 0)),
                   pl.BlockSpec((BLK, D), lambda i: (i, 0))],
        out_shape=[SDS((rows, D), f32), SDS((rows, D), bf16)],
        name="ln_emb", compiler_params=_cp(("arbitrary",)),
    )(x, smallw, g_e, b_e)


def _ln_emb_bwd(dh_lo, dh_hi, du32, x, smallw, g_e, after):
    seq = x.shape[1]
    rows = seq + BLK
    nb = rows // BLK

    def body(dlo_ref, dhi_ref, du_ref, x_ref, sw_ref, g_ref, after_ref, gx_ref, dmeta_ref, st_ref):
        i = pl.program_id(0)
        g = g_ref[...]
        dht = jnp.concatenate([dlo_ref[...], dhi_ref[...]], axis=1) + ALPHA * du_ref[...]

        @pl.when(i == 0)
        def _():
            v = jnp.concatenate([jnp.zeros((ROW0, D), f32), _meta_full(sw_ref)], axis=0)
            valid = lax.broadcasted_iota(jnp.int32, (BLK, 1), 0) >= ROW0
            d = jnp.where(valid, dht, 0.0)
            _, xhat, rstd = _ln_rows(v, g, 0.0)
            dv = _ln_rows_bwd(d, g, xhat, rstd)
            dmeta_ref[...] = dv[ROW0:, :]
            st_ref[...] = jnp.concatenate([_colsum(d * xhat), _colsum(d), jnp.zeros((6, D), f32)], axis=0)

        @pl.when(i > 0)
        def _():
            _, xhat, rstd = _ln_rows(x_ref[0], g, 0.0)
            gx_ref[0] = _ln_rows_bwd(dht, g, xhat, rstd)
            st_ref[0:1, :] += _colsum(dht * xhat)
            st_ref[1:2, :] += _colsum(dht)

    return pl.pallas_call(
        body, grid=(nb,),
        in_specs=[pl.BlockSpec((BLK, W_IN_HALF), lambda i: (i, 0)),
                  pl.BlockSpec((BLK, W_IN_HALF), lambda i: (i, 0)),
                  pl.BlockSpec((BLK, D), lambda i: (i, 0)),
                  pl.BlockSpec((1, BLK, D), lambda i: (0, jnp.maximum(i - 1, 0), 0)),
                  pl.BlockSpec((N_DEV, 24, 256), lambda i: (0, 0, 0)),
                  pl.BlockSpec((1, D), lambda i: (0, 0)),
                  pl.BlockSpec(memory_space=pl.ANY)],
        out_specs=[pl.BlockSpec((1, BLK, D), lambda i: (0, jnp.maximum(i - 1, 0), 0)),
                   pl.BlockSpec((N_META, D), lambda i: (0, 0)),
                   pl.BlockSpec((8, D), lambda i: (0, 0))],
        out_shape=[SDS((1, seq, D), f32), SDS((N_META, D), f32), SDS((8, D), f32)],
        name="ln_emb_bwd", compiler_params=_cp(("arbitrary",)),
    )(dh_lo, dh_hi, du32, x, smallw, g_e, after)


def _mm(a, b, *, name, nt=False, sel=None, bias=None, out_dtype=f32, tn=512):
    m, k = a.shape
    cm = _row_chunk(m)
    stacked = sel is not None
    n = D if stacked else (b.shape[0] if nt else b.shape[1])
    am = m
    if stacked and nt:
        b_spec = pl.BlockSpec((tn // 256, None, 256, D), lambda j, i: (j, sel, 0, 0))
    elif stacked:
        b_spec = pl.BlockSpec((N_DEV, None, 256, tn), lambda j, i: (0, sel, 0, j))
    elif nt:
        b_spec = pl.BlockSpec((tn, k), lambda j, i: (j, 0))
    else:
        b_spec = pl.BlockSpec((k, tn), lambda j, i: (0, j))
    in_specs = [pl.BlockSpec((am, k), lambda j, i: (i, 0)), b_spec]
    args = [a, b]
    if bias is not None:
        in_specs.append(pl.BlockSpec((1, tn), lambda j, i: (0, j)))
        args.append(bias)

    def body(*refs):
        a_ref, b_ref, o_ref = refs[0], refs[1], refs[-1]
        bm = b_ref[...]
        if stacked:
            bm = bm.reshape((tn, D) if nt else (D, tn))
        for c in range(am // cm):
            acc = (_dot_nt if nt else _dot)(a_ref[c * cm:(c + 1) * cm, :], bm)
            if bias is not None:
                acc = acc + refs[2][...]
            o_ref[c * cm:(c + 1) * cm, :] = acc.astype(out_dtype)

    return pl.pallas_call(
        body, grid=(n // tn, m // am), in_specs=in_specs,
        out_specs=pl.BlockSpec((am, tn), lambda j, i: (i, j)),
        out_shape=SDS((m, n), out_dtype), name=name, compiler_params=_cp(("arbitrary", "arbitrary"), 48),
    )(*args)


def _mm_dh(dz, w_t, after, half):
    rows = dz.shape[0]
    tn = 512
    nt = W_IN_HALF // tn
    cm = _row_chunk(rows) // 2

    def body(a_ref, w_ref, after_ref, o_ref):
        o_ref[...] = _dot(a_ref[...], w_ref[...])

    return pl.pallas_call(
        body, grid=(nt, rows // cm),
        in_specs=[pl.BlockSpec((cm, D_IN), lambda j, i: (i, 0)),
                  pl.BlockSpec((D_IN, tn), lambda j, i: (0, half * nt + j)),
                  pl.BlockSpec(memory_space=pl.ANY)],
        out_specs=pl.BlockSpec((cm, tn), lambda j, i: (i, j)),
        out_shape=SDS((rows, W_IN_HALF), f32), name="mm_dh_%d" % half,
        compiler_params=_cp(("arbitrary", "arbitrary"), 48),
    )(dz, w_t, after)


def _mm_dwin_parts(hb, parts):
    rows = hb.shape[0]
    tc = 512
    edges = [0]
    for _, w in parts:
        edges.append(edges[-1] + w // tc)

    def body(*refs):
        h_ref, (o_ref, dz_ref, db_ref) = refs[len(parts)], refs[len(parts) + 1:]
        j = pl.program_id(0)
        for p_ref, lo, hi in zip(refs, edges[:-1], edges[1:]):
            @pl.when((j >= lo) & (j < hi))
            def _():
                o_ref[...] = _dot_tn(p_ref[...], h_ref[...]).astype(bf16)
                dz_ref[...] = p_ref[...]

                def step(i, s):
                    blk = p_ref[pl.ds(pl.multiple_of(i * BLK, BLK), BLK), :].astype(f32)
                    return s + blk.reshape(BLK // 8, 8, tc).sum(axis=0)
                s = lax.fori_loop(0, rows // BLK, step, jnp.zeros((8, tc), f32))
                db_ref[...] = jnp.broadcast_to(_colsum(s), (8, tc))

    in_specs = [pl.BlockSpec((rows, tc), lambda j, lo=lo, hi=hi: (0, jnp.clip(j - lo, 0, hi - lo - 1)))
                for lo, hi in zip(edges[:-1], edges[1:])]
    return pl.pallas_call(
        body, grid=(D_IN // tc,),
        in_specs=in_specs + [pl.BlockSpec((rows, W_IN_HALF), lambda j: (0, 0))],
        out_specs=[pl.BlockSpec((tc, W_IN_HALF), lambda j: (j, 0)), pl.BlockSpec((rows, tc), lambda j: (0, j)),
                   pl.BlockSpec((8, tc), lambda j: (0, j))],
        out_shape=[SDS((D_IN, W_IN_HALF), bf16), SDS((rows, D_IN), bf16), SDS((8, D_IN), f32)],
        name="mm_dwin_0", compiler_params=_cp(("arbitrary",), VMEM_LIMIT_MB),
    )(*[a for a, _ in parts], hb)


def _mm_dwin(hb, dz, after):
    rows = dz.shape[0]
    tc = 640

    def body(dz_ref, h_ref, after_ref, o_ref):
        o_ref[...] = _dot_tn(dz_ref[...], h_ref[...]).astype(bf16)

    return pl.pallas_call(
        body, grid=(D_IN // tc,),
        in_specs=[pl.BlockSpec((rows, tc), lambda j: (0, j)),
                  pl.BlockSpec((rows, W_IN_HALF), lambda j: (0, 1)),
                  pl.BlockSpec(memory_space=pl.ANY)],
        out_specs=pl.BlockSpec((tc, W_IN_HALF), lambda j: (j, 0)),
        out_shape=SDS((D_IN, W_IN_HALF), bf16),
        name="mm_dwin_1", compiler_params=_cp(("arbitrary",), 48),
    )(dz, hb, after)


SCAN_ROWS = 32


def _scan8(a, b, reverse):
    idx = lax.broadcasted_iota(jnp.int32, a.shape, 0)
    for s in (1, 2, 4):
        sh = 8 - s if reverse else s
        a_sh, b_sh = pltpu.roll(a, sh, 0), pltpu.roll(b, sh, 0)
        m = (idx < 8 - s) if reverse else (idx >= s)
        b = jnp.where(m, a * b_sh + b, b)
        a = jnp.where(m, a * a_sh, a)
    return a, b


def _shift_rows(prev8, cur, k):
    ext = jnp.concatenate([prev8, cur], axis=0)
    return pltpu.roll(ext, k, 0)[8:, :]


def _gates(xc, w_ra, b_ra, w_ri, b_ri, ls):
    xb = xc.astype(bf16)
    r = _sigmoid(_dot(xb, w_ra) + b_ra)
    ig = _sigmoid(_dot(xb, w_ri) + b_ri)
    la = LRU_C * r * ls
    a = jnp.exp(la)
    mult = jnp.sqrt(jnp.tanh(-la) * (1.0 + a * a))
    return xb, r, ig, a, mult


_RNN_IN_SPECS = lambda rows: [
    pl.BlockSpec((1, 24, 256), lambda n: (n, 0, 0)),
    pl.BlockSpec((1, RNN_BLOCK), lambda n: (0, n)),
    pl.BlockSpec((N_DEV, 2, None, 32, RNN_BLOCK), lambda n: (0, 0, n, 0, 0)),
    pl.BlockSpec((1, RNN_BLOCK), lambda n: (0, n)),
    pl.BlockSpec((1, RNN_BLOCK), lambda n: (0, n)),
    pl.BlockSpec((1, RNN_BLOCK), lambda n: (0, n)),
]


def _rnn_fwd(z, smallw, conv_b, wrg, b_ra, b_ri, lam):
    rows = z.shape[0]
    nb = rows // BLK
    col = lambda off: pl.BlockSpec((rows, RNN_BLOCK), lambda n: (0, off // RNN_BLOCK + n))

    def body(xr_ref, gr_ref, sw_ref, cb_ref, w_ref, bra_ref, bri_ref, lam_ref, xc_ref, hr_ref, ya_ref, yat_ref, a_s):
        cw = sw_ref[0, N_META:24, :]
        cb = cb_ref[...]
        w_ra = w_ref[:, 0].reshape(RNN_BLOCK, RNN_BLOCK)
        w_ri = w_ref[:, 1].reshape(RNN_BLOCK, RNN_BLOCK)
        b_ra_v, b_ri_v = bra_ref[...], bri_ref[...]
        ls = _log_sigmoid(lam_ref[...])
        rid = lax.broadcasted_iota(jnp.int32, (BLK, 1), 0)

        def blk_step(i, carry):
            r0 = pl.multiple_of(i * BLK, BLK)
            grow = rid + r0
            valid = grow >= ROW0
            cur = jnp.where(valid, xr_ref[pl.ds(r0, BLK), :], 0.0)
            prev8 = xr_ref[pl.ds(pl.multiple_of(jnp.maximum(r0 - 8, 0), 8), 8), :] * (i > 0).astype(f32)
            xc = cb + cw[0:1] * cur
            for k in range(1, CONV_WIDTH):
                xc = xc + cw[k:k + 1] * _shift_rows(prev8, cur, k)
            xc_ref[pl.ds(r0, BLK), :] = xc
            _, _, ig, a, mult = _gates(xc, w_ra, b_ra_v, w_ri, b_ri_v, ls)
            mult = jnp.where(grow == ROW0, 1.0, mult)
            a_s[pl.ds(r0, BLK), :] = a
            hr_ref[pl.ds(r0, BLK), :] = jnp.where(valid, mult * ig * xc, 0.0)
            return carry

        lax.fori_loop(0, nb, blk_step, 0)

        def scan_step(j, carry):
            r0 = pl.multiple_of(j * SCAN_ROWS, SCAN_ROWS)
            tiles = [_scan8(a_s[pl.ds(r0 + 8 * k, 8), :], hr_ref[pl.ds(r0 + 8 * k, 8), :], False)
                     for k in range(SCAN_ROWS // 8)]
            for k, (a, b) in enumerate(tiles):
                h = b + a * carry
                hr_ref[pl.ds(r0 + 8 * k, 8), :] = h
                carry = jnp.broadcast_to(h[7:8, :], (8, RNN_BLOCK))
            return carry

        lax.fori_loop(0, rows // SCAN_ROWS, scan_step, jnp.zeros((8, RNN_BLOCK), f32))

        def gate_step(i, carry):
            r0 = pl.multiple_of(i * BLK, BLK)
            ya_ref[pl.ds(r0, BLK), :] = (hr_ref[pl.ds(r0, BLK), :]
                                         * _silu_and_grad(gr_ref[pl.ds(r0, BLK), :])[0]).astype(bf16)
            return carry

        lax.fori_loop(0, nb, gate_step, 0)
        yat_ref[...] = ya_ref[...].astype(f32).T.astype(bf16)

    return pl.pallas_call(
        body, grid=(N_RNN_BLOCKS,),
        in_specs=[col(0), col(OFF_GR)] + _RNN_IN_SPECS(rows),
        out_specs=[pl.BlockSpec((rows, RNN_BLOCK), lambda n: (0, n))] * 3
                  + [pl.BlockSpec((RNN_BLOCK, rows), lambda n: (n, 0))],
        out_shape=[SDS((rows, D), f32), SDS((rows, D), f32), SDS((rows, D), bf16), SDS((D, rows), bf16)],
        scratch_shapes=[pltpu.VMEM((rows, RNN_BLOCK), f32)],
        name="rnn_fwd", compiler_params=_cp(("arbitrary",)),
    )(z, z, smallw, conv_b, wrg, b_ra, b_ri, lam)


def _rnn_bwd(dya, hr, xc, z, smallw, conv_b, wrg, b_ra, b_ri, lam):
    rows = z.shape[0]
    nb = rows // BLK
    col = lambda off: pl.BlockSpec((rows, RNN_BLOCK), lambda n: (0, off // RNN_BLOCK + n))
    blk = pl.BlockSpec((rows, RNN_BLOCK), lambda n: (0, n))

    def body(dya_ref, hr_ref, xc_ref, xr_ref, gr_ref, sw_ref, cb_ref, w_ref, bra_ref, bri_ref, lam_ref,
             dxr_ref, dgr_ref, dw_ref, vec_ref, a_s, lam_s, dxc_s, r_s, ig_s, mult_s, dw_s):
        cw = sw_ref[0, N_META:24, :]
        w_ra = w_ref[:, 0].reshape(RNN_BLOCK, RNN_BLOCK)
        w_ri = w_ref[:, 1].reshape(RNN_BLOCK, RNN_BLOCK)
        b_ra_v, b_ri_v = bra_ref[...], bri_ref[...]
        lam_v = lam_ref[...]
        ls = _log_sigmoid(lam_v)
        rid = lax.broadcasted_iota(jnp.int32, (BLK, 1), 0)
        zrow = jnp.zeros((1, RNN_BLOCK), f32)

        def p1(i, carry):
            r0 = pl.multiple_of(i * BLK, BLK)
            sl = pl.ds(r0, BLK)
            _, r, ig, a, mult = _gates(xc_ref[sl, :], w_ra, b_ra_v, w_ri, b_ri_v, ls)
            a_s[sl, :] = a
            r_s[sl, :] = r
            ig_s[sl, :] = ig
            mult_s[sl, :] = mult
            sg, dsg = _silu_and_grad(gr_ref[sl, :])
            d = dya_ref[sl, :]
            lam_s[sl, :] = d * sg
            dgr_ref[sl, :] = (d * hr_ref[sl, :] * dsg).astype(bf16)
            return carry

        lax.fori_loop(0, nb, p1, 0)

        def p2(jj, carry):
            r0 = pl.multiple_of((rows // SCAN_ROWS - 1 - jj) * SCAN_ROWS, SCAN_ROWS)
            idx = lax.broadcasted_iota(jnp.int32, (8, RNN_BLOCK), 0)
            tiles = []
            for k in range(SCAN_ROWS // 8):
                sl = pl.ds(r0 + 8 * k, 8)
                a, g = a_s[sl, :], lam_s[sl, :]
                tiles.append((g, *_scan8(a, a * g, True)))
            for k in reversed(range(SCAN_ROWS // 8)):
                g, ca, cb_ = tiles[k]
                mu = cb_ + ca * carry
                lam_s[pl.ds(r0 + 8 * k, 8), :] = g + jnp.where(idx < 7, pltpu.roll(mu, 7, 0), carry)
                carry = jnp.broadcast_to(mu[0:1, :], (8, RNN_BLOCK))
            return carry

        lax.fori_loop(0, rows // SCAN_ROWS, p2, jnp.zeros((8, RNN_BLOCK), f32))

        dw_s[...] = jnp.zeros_like(dw_s)

        def p3(i, carry):
            d_bra, d_bri, d_ls = carry
            r0 = pl.multiple_of(i * BLK, BLK)
            sl = pl.ds(r0, BLK)
            grow = rid + r0
            valid = grow >= ROW0
            first = grow == ROW0
            xcv = xc_ref[sl, :]
            xb = xcv.astype(bf16)
            r, ig, a = r_s[sl, :], ig_s[sl, :], a_s[sl, :]
            mult = jnp.where(first, 1.0, mult_s[sl, :])
            lam_t = lam_s[sl, :]
            du = jnp.where(valid, lam_t, 0.0)
            hprev = _shift_rows(hr_ref[pl.ds(pl.multiple_of(jnp.maximum(r0 - 8, 0), 8), 8), :] * (i > 0).astype(f32), hr_ref[sl, :], 1)
            da = lam_t * hprev
            dmult = jnp.where(first, 0.0, du * ig * xcv)
            di = du * mult * xcv
            dxc = du * mult * ig
            ratio = jnp.where(valid & jnp.logical_not(first), a * a / mult, 0.0)
            dla = da * a - dmult * ratio
            dpr = (dla * (LRU_C * ls)) * r * (1.0 - r)
            dpi = di * ig * (1.0 - ig)
            dprb, dpib = dpr.astype(bf16), dpi.astype(bf16)
            dw_s[0] += _dot_tn(xb, dprb)
            dw_s[1] += _dot_tn(xb, dpib)
            dxc_s[sl, :] = dxc + _dot_nt(dprb, w_ra) + _dot_nt(dpib, w_ri)
            return d_bra + _colsum(dpr), d_bri + _colsum(dpi), d_ls + _colsum(dla * (LRU_C * r))

        d_bra, d_bri, d_ls = lax.fori_loop(0, nb, p3, (zrow, zrow, zrow))

        def p4(i, carry):
            d_cb, d_w0, d_w1, d_w2, d_w3 = carry
            r0 = pl.multiple_of(i * BLK, BLK)
            sl = pl.ds(r0, BLK)
            grow = rid + r0
            valid = grow >= ROW0
            dxc = dxc_s[sl, :]
            nxt = dxc_s[pl.ds(pl.multiple_of(jnp.minimum(r0 + BLK, rows - 8), 8), 8), :] * (i < nb - 1).astype(f32)
            ext = jnp.concatenate([dxc, nxt], axis=0)
            dxr = cw[0:1] * dxc
            for k in range(1, CONV_WIDTH):
                dxr = dxr + cw[k:k + 1] * pltpu.roll(ext, BLK + 8 - k, 0)[:BLK, :]
            dxr_ref[sl, :] = jnp.where(valid, dxr, 0.0).astype(bf16)
            cur = jnp.where(valid, xr_ref[sl, :], 0.0)
            prev8 = xr_ref[pl.ds(pl.multiple_of(jnp.maximum(r0 - 8, 0), 8), 8), :] * (i > 0).astype(f32)
            dws = [d_w0 + _colsum(dxc * cur)]
            for k, acc in ((1, d_w1), (2, d_w2), (3, d_w3)):
                dws.append(acc + _colsum(dxc * _shift_rows(prev8, cur, k)))
            return (d_cb + _colsum(dxc), *dws)

        d_cb, d_w0, d_w1, d_w2, d_w3 = lax.fori_loop(0, nb, p4, (zrow,) * 5)

        d_lam = d_ls * _sigmoid(-lam_v)
        vec_ref[...] = jnp.concatenate([d_bra, d_bri, d_lam, d_cb, d_w0, d_w1, d_w2, d_w3], axis=0)
        dw_ref[:, 0] = dw_s[0].astype(bf16).reshape(N_DEV, 32, RNN_BLOCK)
        dw_ref[:, 1] = dw_s[1].astype(bf16).reshape(N_DEV, 32, RNN_BLOCK)

    return pl.pallas_call(
        body, grid=(N_RNN_BLOCKS,),
        in_specs=[blk, blk, blk, col(0), col(OFF_GR)] + _RNN_IN_SPECS(rows),
        out_specs=[blk, blk,
                   pl.BlockSpec((N_DEV, 2, None, 32, RNN_BLOCK), lambda n: (0, 0, n, 0, 0)),
                   pl.BlockSpec((8, RNN_BLOCK), lambda n: (0, n))],
        out_shape=[SDS((rows, D), bf16), SDS((rows, D), bf16),
                   SDS((N_DEV, 2, N_RNN_BLOCKS, 32, RNN_BLOCK), bf16), SDS((8, D), f32)],
        scratch_shapes=[pltpu.VMEM((rows, RNN_BLOCK), f32)] * 6 + [pltpu.VMEM((2, RNN_BLOCK, RNN_BLOCK), f32)],
        name="rnn_bwd", compiler_params=_cp(("arbitrary",), 48),
    )(dya, hr, xc, z, z, smallw, conv_b, wrg, b_ra, b_ri, lam)


def _rope_tables(rows):
    half = jnp.arange(HALF, dtype=f32)
    inv = ROPE_THETA ** (-half / HALF)
    pos = (jnp.arange(rows) - ROW0).astype(f32)
    ang = pos[:, None] * inv[None, :]
    cos, sin = jnp.cos(ang), jnp.sin(ang)
    cos128 = jnp.concatenate([cos, cos, cos, cos], axis=1)
    sin128 = jnp.concatenate([-sin, sin, -sin, sin], axis=1)
    return cos128, sin128


def _rope128(x, cos128, sin128):
    lane = lax.broadcasted_iota(jnp.int32, x.shape, 1)
    swapped = jnp.where(lane % HEAD_DIM < HALF, pltpu.roll(x, 128 - HALF, 1), pltpu.roll(x, HALF, 1))
    return x * cos128 + swapped * sin128


def _qkv_prep(z, cos128, sin128):
    rows = z.shape[0]

    def body(q_ref, kv_ref, c_ref, s_ref, qo_ref, ko_ref, vo_ref):
        c, s = c_ref[...], s_ref[...]
        for g in range(D // 128):
            qo_ref[:, g * 128:(g + 1) * 128] = (_rope128(q_ref[:, g * 128:(g + 1) * 128], c, s)
                                                * (HEAD_DIM ** -0.5)).astype(bf16)
        for g in range(2):
            kr = _rope128(kv_ref[:, g * 128:(g + 1) * 128], c, s)
            for j in range(2):
                ko_ref[2 * g + j] = kr[:, j * HEAD_DIM:(j + 1) * HEAD_DIM].astype(bf16)
        for h in range(N_KV):
            vo_ref[h] = kv_ref[:, 256 + h * HEAD_DIM:256 + (h + 1) * HEAD_DIM].astype(bf16)

    return pl.pallas_call(
        body, grid=(rows // BLK,),
        in_specs=[pl.BlockSpec((BLK, D), lambda i: (i, OFF_Q // D)),
                  pl.BlockSpec((BLK, 512), lambda i: (i, OFF_K // 512)),
                  pl.BlockSpec((BLK, 128), lambda i: (i, 0)),
                  pl.BlockSpec((BLK, 128), lambda i: (i, 0))],
        out_specs=[pl.BlockSpec((BLK, D), lambda i: (i, 0)),
                   pl.BlockSpec((N_KV, BLK, HEAD_DIM), lambda i: (0, i, 0)),
                   pl.BlockSpec((N_KV, BLK, HEAD_DIM), lambda i: (0, i, 0))],
        out_shape=[SDS((rows, D), bf16), SDS((N_KV, rows, HEAD_DIM), bf16), SDS((N_KV, rows, HEAD_DIM), bf16)],
        name="qkv_prep", compiler_params=_cp(("arbitrary",)),
    )(z, z, cos128, sin128)


def _attn_mask(n):
    qi = n * BLK + lax.broadcasted_iota(jnp.int32, (BLK, 2 * BLK + N_META), 0)
    c = lax.broadcasted_iota(jnp.int32, (BLK, 2 * BLK + N_META), 1)
    jb = (n - 1) * BLK + c
    band = (jb >= BLK) & (jb <= qi) & (qi - jb < BLK)
    meta = (ROW0 + c - 2 * BLK) <= qi
    return ((c < 2 * BLK) & band) | ((c >= 2 * BLK) & meta)


N_KEYS = 2 * BLK + N_META


def _stack_heads(t):
    return jnp.concatenate([t[:, g * HEAD_DIM:(g + 1) * HEAD_DIM] for g in range(GROUP)], axis=0)


def _sink_column(sink_ref, h):
    g = lax.broadcasted_iota(jnp.int32, (GROUP, 1, 1), 0)
    col = jnp.zeros((GROUP, 1, 1), f32)
    for j in range(GROUP):
        col = jnp.where(g == j, sink_ref[h * GROUP + j], col)
    return col


def _kv_specs(last):
    cl = lambda n: jnp.minimum(n, last)
    return [pl.BlockSpec((None, N_META, HEAD_DIM), lambda h, n: (h, ROW0 // N_META, 0)),
            pl.BlockSpec((None, BLK, HEAD_DIM), lambda h, n: (h, jnp.maximum(cl(n) - 1, 0), 0)),
            pl.BlockSpec((None, BLK, HEAD_DIM), lambda h, n: (h, cl(n), 0))]


def _attn_fwd(q_r, k_r, v_b, z, sinks):
    rows = q_r.shape[0]
    nb = rows // BLK

    def body(sink_ref, q_ref, km_ref, kp_ref, kc_ref, vm_ref, vp_ref, vc_ref, ga_ref, o_ref, yb_ref, ybt_ref, lse_ref):
        h, n = pl.program_id(0), pl.program_id(1)
        kk = jnp.concatenate([kp_ref[...], kc_ref[...], km_ref[...]], axis=0)
        vv = jnp.concatenate([vp_ref[...], vc_ref[...], vm_ref[...]], axis=0)
        q2 = _stack_heads(q_ref[...])
        s = jnp.where(_attn_mask(n)[None], _dot_nt(q2, kk).reshape(GROUP, BLK, N_KEYS), NEG_INF)
        sink = _sink_column(sink_ref, h)
        m = jnp.maximum(jnp.max(s, axis=-1, keepdims=True), sink)
        p = jnp.exp(s - m)
        den = jnp.sum(p, axis=-1, keepdims=True) + jnp.exp(sink - m)
        o2 = _dot((p / den).astype(bf16).reshape(GROUP * BLK, N_KEYS), vv)
        lse = m + jnp.log(den)
        for g in range(GROUP):
            o_ref[:, g * HEAD_DIM:(g + 1) * HEAD_DIM] = o2[g * BLK:(g + 1) * BLK]
            lse_ref[:, g:g + 1] = lse[g]
        yb = o_ref[...] * _silu_and_grad(ga_ref[...])[0]
        yb_ref[...] = yb.astype(bf16)
        ybt_ref[...] = yb.T.astype(bf16)

    tile = pl.BlockSpec((BLK, 512), lambda h, n: (n, h))
    return pl.pallas_call(
        body, grid=(N_KV, nb),
        in_specs=[pl.BlockSpec(memory_space=pltpu.SMEM), tile] + _kv_specs(nb - 1) + _kv_specs(nb - 1)
                 + [pl.BlockSpec((BLK, 512), lambda h, n: (n, OFF_GA // 512 + h))],
        out_specs=[tile, tile, pl.BlockSpec((512, BLK), lambda h, n: (h, n)),
                   pl.BlockSpec((None, BLK, GROUP), lambda h, n: (h, n, 0))],
        out_shape=[SDS((rows, D), f32), SDS((rows, D), bf16), SDS((D, rows), bf16),
                   SDS((N_KV, rows, GROUP), f32)],
        name="attn_fwd", compiler_params=_cp(("arbitrary", "arbitrary")),
    )(sinks, q_r, k_r, k_r, k_r, v_b, v_b, v_b, z)


def _attn_bwd(dyb, o32, lse, q_r, k_r, v_b, z, sinks):
    rows = q_r.shape[0]
    nb = rows // BLK
    cl = lambda n: jnp.minimum(n, nb - 1)

    def body(sink_ref, dyb_ref, o_ref, lse_ref, q_ref, km_ref, kp_ref, kc_ref, vm_ref, vp_ref, vc_ref, ga_ref,
             dq_ref, dga_ref, dk_ref, dv_ref, dkm_ref, dvm_ref, dsr_ref, ck_s, cv_s):
        h, n = pl.program_id(0), pl.program_id(1)

        @pl.when(n == 0)
        def _():
            dkm_ref[...] = jnp.zeros_like(dkm_ref)
            dvm_ref[...] = jnp.zeros_like(dvm_ref)
            ck_s[...] = jnp.zeros_like(ck_s)
            cv_s[...] = jnp.zeros_like(cv_s)

        @pl.when(n < nb)
        def _():
            kk = jnp.concatenate([kp_ref[...], kc_ref[...], km_ref[...]], axis=0)
            vv = jnp.concatenate([vp_ref[...], vc_ref[...], vm_ref[...]], axis=0)
            sg, dsg = _silu_and_grad(ga_ref[...])
            dyb_v = dyb_ref[...]
            o_v = o_ref[...]
            dga_ref[...] = (dyb_v * o_v * dsg).astype(bf16)
            q2 = _stack_heads(q_ref[...])
            do2 = _stack_heads(dyb_v * sg)
            lse_v = lse_ref[...]
            lse = jnp.concatenate([lse_v[:, g:g + 1] for g in range(GROUP)], axis=0).reshape(GROUP, BLK, 1)
            delta = jnp.sum(do2 * _stack_heads(o_v), axis=-1, keepdims=True).reshape(GROUP, BLK, 1)
            s = jnp.where(_attn_mask(n)[None], _dot_nt(q2, kk).reshape(GROUP, BLK, N_KEYS), NEG_INF)
            p = jnp.exp(s - lse)
            do2b = do2.astype(bf16)
            ds = (p * (_dot_nt(do2b, vv).reshape(GROUP, BLK, N_KEYS) - delta)).astype(bf16)
            ds = ds.reshape(GROUP * BLK, N_KEYS)
            dsr = -jnp.exp(_sink_column(sink_ref, h) - lse) * delta
            dq2 = _dot(ds, kk)
            for g in range(GROUP):
                dq_ref[:, g * HEAD_DIM:(g + 1) * HEAD_DIM] = dq2[g * BLK:(g + 1) * BLK]
                dsr_ref[:, g:g + 1] = dsr[g]
            dkk = _dot_tn(ds, q2)
            dvv = _dot_tn(p.astype(bf16).reshape(GROUP * BLK, N_KEYS), do2b)
            dk_ref[...] = ck_s[...] + dkk[:BLK]
            dv_ref[...] = cv_s[...] + dvv[:BLK]
            ck_s[...] = dkk[BLK:2 * BLK]
            cv_s[...] = dvv[BLK:2 * BLK]
            dkm_ref[...] += dkk[2 * BLK:]
            dvm_ref[...] += dvv[2 * BLK:]

        @pl.when(n == nb)
        def _():
            dk_ref[...] = ck_s[...]
            dv_ref[...] = cv_s[...]

    tile = pl.BlockSpec((BLK, 512), lambda h, n: (cl(n), h))
    kvout = pl.BlockSpec((None, BLK, HEAD_DIM), lambda h, n: (h, jnp.maximum(n - 1, 0), 0))
    mout = pl.BlockSpec((None, N_META, HEAD_DIM), lambda h, n: (h, 0, 0))
    stat = pl.BlockSpec((None, BLK, GROUP), lambda h, n: (h, cl(n), 0))
    return pl.pallas_call(
        body, grid=(N_KV, nb + 1),
        in_specs=[pl.BlockSpec(memory_space=pltpu.SMEM), tile, tile, stat, tile] + _kv_specs(nb - 1)
                 + _kv_specs(nb - 1) + [pl.BlockSpec((BLK, 512), lambda h, n: (cl(n), OFF_GA // 512 + h))],
        out_specs=[tile, tile, kvout, kvout, mout, mout, stat],
        out_shape=[SDS((rows, D), f32), SDS((rows, D), bf16),
                   SDS((N_KV, rows, HEAD_DIM), f32), SDS((N_KV, rows, HEAD_DIM), f32),
                   SDS((N_KV, N_META, HEAD_DIM), f32), SDS((N_KV, N_META, HEAD_DIM), f32),
                   SDS((N_KV, rows, GROUP), f32)],
        scratch_shapes=[pltpu.VMEM((BLK, HEAD_DIM), f32), pltpu.VMEM((BLK, HEAD_DIM), f32)],
        name="attn_bwd", compiler_params=_cp(("arbitrary", "arbitrary")),
    )(sinks, dyb, o32, lse, q_r, k_r, k_r, k_r, v_b, v_b, v_b, z)


def _qkv_finish(dq, dk, dv, dkm, dvm, cos128, sin128):
    rows = dq.shape[0]

    def body(dq_ref, dk_ref, dv_ref, dkm_ref, dvm_ref, c_ref, s_ref, oq_ref, okv_ref):
        first = (pl.program_id(0) == 0).astype(f32)
        c, s = c_ref[...], -s_ref[...]
        for g in range(D // 128):
            oq_ref[:, g * 128:(g + 1) * 128] = (_rope128(dq_ref[:, g * 128:(g + 1) * 128], c, s)
                                                * (HEAD_DIM ** -0.5)).astype(bf16)
        pad = jnp.zeros((ROW0, HEAD_DIM), f32)
        ks = [dk_ref[h] + first * jnp.concatenate([pad, dkm_ref[h]], axis=0) for h in range(N_KV)]
        vs = [dv_ref[h] + first * jnp.concatenate([pad, dvm_ref[h]], axis=0) for h in range(N_KV)]
        for g in range(2):
            kp = jnp.concatenate([ks[2 * g], ks[2 * g + 1]], axis=1)
            okv_ref[:, g * 128:(g + 1) * 128] = _rope128(kp, c, s).astype(bf16)
            okv_ref[:, 256 + g * 128:256 + (g + 1) * 128] = jnp.concatenate([vs[2 * g], vs[2 * g + 1]], axis=1).astype(bf16)

    kv = pl.BlockSpec((N_KV, BLK, HEAD_DIM), lambda i: (0, i, 0))
    mt = pl.BlockSpec((N_KV, N_META, HEAD_DIM), lambda i: (0, 0, 0))
    return pl.pallas_call(
        body, grid=(rows // BLK,),
        in_specs=[pl.BlockSpec((BLK, D), lambda i: (i, 0)), kv, kv, mt, mt,
                  pl.BlockSpec((BLK, 128), lambda i: (i, 0)), pl.BlockSpec((BLK, 128), lambda i: (i, 0))],
        out_specs=[pl.BlockSpec((BLK, D), lambda i: (i, 0)), pl.BlockSpec((BLK, 512), lambda i: (i, 0))],
        out_shape=[SDS((rows, D), bf16), SDS((rows, 512), bf16)],
        name="qkv_finish", compiler_params=_cp(("arbitrary",)),
    )(dq, dk, dv, dkm, dvm, cos128, sin128)


_TW = 512


def _mix_specs(rows):
    tr = _row_chunk(rows)
    tile = pl.BlockSpec((tr, _TW), lambda i, j: (i, j))
    ga = pl.BlockSpec((tr, _TW), lambda i, j: (i, OFF_G // _TW + j))
    gb = pl.BlockSpec((tr, _TW), lambda i, j: (i, (OFF_G + D) // _TW + j))
    return (rows // tr, D // _TW), tile, ga, gb


def _mix_fwd(y_a, y_b, z):
    rows = y_a.shape[0]
    tw = 256
    col = lambda off: pl.BlockSpec((rows, tw), lambda j: (0, off // tw + j))

    def body(ya_ref, yb_ref, ga_ref, gb_ref, o_ref, ot_ref):
        mixed = (_sigmoid(ga_ref[...]) * ya_ref[...].astype(f32)
                 + _sigmoid(gb_ref[...]) * yb_ref[...].astype(f32))
        o_ref[...] = mixed.astype(bf16)
        ot_ref[...] = mixed.T.astype(bf16)

    return pl.pallas_call(
        body, grid=(D // tw,), in_specs=[col(0), col(0), col(OFF_G), col(OFF_G + D)],
        out_specs=[col(0), pl.BlockSpec((tw, rows), lambda j: (j, 0))],
        out_shape=[SDS((rows, D), bf16), SDS((D, rows), bf16)],
        name="mix_fwd", compiler_params=_cp(("arbitrary",)),
    )(y_a, y_b, z, z)


def _mix_bwd(dmixed, y_a, y_b, z):
    rows = y_a.shape[0]
    grid, _mix_tile, _mix_ga, _mix_gb = _mix_specs(rows)

    def body(dm_ref, ya_ref, yb_ref, ga_ref, gb_ref, dya_ref, dyb_ref, dga_ref, dgb_ref):
        dm = dm_ref[...].astype(f32)
        sa, sb = _sigmoid(ga_ref[...]), _sigmoid(gb_ref[...])
        dya_ref[...] = (dm * sa).astype(bf16)
        dyb_ref[...] = (dm * sb).astype(bf16)
        dga_ref[...] = (dm * ya_ref[...].astype(f32) * sa * (1.0 - sa)).astype(bf16)
        dgb_ref[...] = (dm * yb_ref[...].astype(f32) * sb * (1.0 - sb)).astype(bf16)

    return pl.pallas_call(
        body, grid=grid, in_specs=[_mix_tile, _mix_tile, _mix_tile, _mix_ga, _mix_gb],
        out_specs=[_mix_tile] * 4, out_shape=[SDS((rows, D), bf16)] * 4,
        name="mix_bwd", compiler_params=_cp(("arbitrary", "arbitrary")),
    )(dmixed, y_a, y_b, z, z)


def _final_ln(out32, h32, tgt, ln_g, ln_b):
    rows = out32.shape[0]

    def body(o_ref, h_ref, t_ref, g_ref, b_ref, du_ref, dub_ref, st_ref):
        i = pl.program_id(0)
        g = g_ref[...]
        y, xhat, rstd = _ln_rows(ALPHA * h_ref[...] + o_ref[...], g, b_ref[...])
        e = jnp.where(i > 0, y - t_ref[0], 0.0)
        dy = e * (1.0 / D)
        du = _ln_rows_bwd(dy, g, xhat, rstd)
        du_ref[...] = du
        dub_ref[...] = du.astype(bf16)
        st = jnp.concatenate([_colsum(dy * xhat), _colsum(dy), _colsum(du), _colsum(e * e) * (0.5 / D),
                              jnp.zeros((4, D), f32)], axis=0)

        @pl.when(i == 0)
        def _():
            st_ref[...] = st

        @pl.when(i > 0)
        def _():
            st_ref[...] += st

    row = pl.BlockSpec((BLK, D), lambda i: (i, 0))
    vec = pl.BlockSpec((1, D), lambda i: (0, 0))
    return pl.pallas_call(
        body, grid=(rows // BLK,),
        in_specs=[row, row, pl.BlockSpec((1, BLK, D), lambda i: (0, jnp.maximum(i - 1, 0), 0)), vec, vec],
        out_specs=[row, row, pl.BlockSpec((8, D), lambda i: (0, 0))],
        out_shape=[SDS((rows, D), f32), SDS((rows, D), bf16), SDS((8, D), f32)],
        name="final_ln", compiler_params=_cp(("arbitrary",)),
    )(out32, h32, tgt, ln_g, ln_b)


def _step_rnn(h32, hb, z, wrg, smallw, p, zero):
    rows = z.shape[0]
    cos128, sin128 = _rope_tables(rows)
    cos128 = cos128 + zero
    xc, hr, ya, ya_t = _rnn_fwd(z, smallw, p["conv_b"] + zero, wrg, p["b_ra"], p["b_ri"], p["lru_lambda"])
    q_r, k_r, v_b = _qkv_prep(z, cos128, sin128)
    return dict(cos128=cos128, sin128=sin128, h32=h32, hb=hb, z=z, xc=xc, hr=hr, ya=ya, ya_t=ya_t,
                q_r=q_r, k_r=k_r, v_b=v_b)


def _step_attn(s, p, zero):
    sinks = p["sinks"].reshape(N_KV * GROUP) + zero[0]
    o32, yb, yb_t, lse = _attn_fwd(s["q_r"], s["k_r"], s["v_b"], s["z"], sinks)
    return dict(s, sinks=sinks, o32=o32, yb=yb, yb_t=yb_t, lse=lse)


def _step_merge(s, tgt, w3, p):
    ya, yb, z = s["ya"], s["yb"], s["z"]
    y_a = _mm(ya, w3, sel=0, out_dtype=bf16, name="mm_ya")
    y_b = _mm(yb, w3, sel=1, out_dtype=bf16, name="mm_yb")
    mixed, mixed_t = _mix_fwd(y_a, y_b, z)
    out32 = _mm(mixed, w3, sel=2, bias=p["b_o"], name="mm_out")
    du32, dub, st_out = _final_ln(out32, s["h32"], tgt, p["ln_g"], p["ln_b"])

    g_wo = _mm(mixed_t, dub, out_dtype=bf16, name="mm_dwo")
    dmixed = _mm(dub, w3, sel=2, nt=True, out_dtype=bf16, name="mm_dmixed")
    dya_b, dyb_b, dma, dmb = _mix_bwd(dmixed, y_a, y_b, z)
    g_wrnn = _mm(s["ya_t"], dya_b, out_dtype=bf16, name="mm_dwrnn")
    g_wattn = _mm(s["yb_t"], dyb_b, out_dtype=bf16, name="mm_dwattn")
    dya = _mm(dya_b, w3, sel=0, nt=True, name="mm_dya")
    dyb = _mm(dyb_b, w3, sel=1, nt=True, name="mm_dyb")
    return dict(du32=du32, st_out=st_out, dma=dma, dmb=dmb, dya=dya, dyb=dyb, g_wo=g_wo, g_wrnn=g_wrnn,
                g_wattn=g_wattn)


def _step_backward(s, t, wrg, smallw, p, conv_b):
    z = s["z"]
    dxr, dgr, g_wrg, vec_rnn = _rnn_bwd(t["dya"], s["hr"], s["xc"], z, smallw, conv_b, wrg, p["b_ra"], p["b_ri"],
                                        p["lru_lambda"])
    dq_r, dga, dk, dv, dkm, dvm, dsr = _attn_bwd(t["dyb"], s["o32"], s["lse"], s["q_r"], s["k_r"], s["v_b"], z,
                                                 s["sinks"])
    dq, dkv = _qkv_finish(dq_r, dk, dv, dkm, dvm, s["cos128"], s["sin128"])
    dz_parts = [(dxr, D), (dgr, D), (dq, D), (dkv, 512), (dga, D), (t["dma"], D), (t["dmb"], D)]
    return dict(vec_rnn=vec_rnn, dsr=dsr, g_wrg=g_wrg, dz_parts=dz_parts)


def _step_input_grad(dh_lo, dh_hi, du32, x, smallw, p, after):
    grad_x, dmeta, st_emb = _ln_emb_bwd(dh_lo, dh_hi, du32, x, smallw, p["ln_emb_g"], after)
    return dict(grad_x=grad_x, dmeta=dmeta, st_emb=st_emb)


_ANY = pl.BlockSpec(memory_space=pl.ANY)
_VMEM = pl.BlockSpec(memory_space=pltpu.VMEM)
_HBM = pl.BlockSpec(memory_space=pltpu.HBM)
_SEM = pl.BlockSpec(memory_space=pltpu.SEMAPHORE)


def _place():
    x, y, c = lax.axis_index("x"), lax.axis_index("y"), lax.axis_index("c")
    return x, y, c


def _dev(px, py, pc):
    return 4 * px + 2 * py + pc


def _tile_rows(r):
    return max(t for t in range(16, 321, 16) if r % t == 0) if r > 320 else r


def _cast_w_in(w_in_t, me_idx):
    tm = _tile_rows(SHARD_IN)

    def body(me_ref, i_ref, o_ref):
        o_ref[...] = i_ref[...].astype(bf16)

    return pl.pallas_call(
        body,
        grid_spec=pltpu.PrefetchScalarGridSpec(
            num_scalar_prefetch=1, grid=(SHARD_IN // tm,),
            in_specs=[pl.BlockSpec((tm, D), lambda i, me_ref: (i, 0))],
            out_specs=pl.BlockSpec((None, tm, D), lambda i, me_ref: (me_ref[0], i, 0))),
        out_shape=SDS((N_DEV, SHARD_IN, D), bf16), name="cast_w_in", compiler_params=_cp(("arbitrary",)),
    )(me_idx, w_in_t)


def _cast_small(me_idx, w_rnn_out, w_attn_out, w_o, w_ra, w_ri, meta, conv_w):
    def body(me_ref, a_ref, b_ref, c_ref, ra_ref, ri_ref, m_ref, cw_ref, w3_ref, wrg_ref, sw_ref):
        w3_ref[0] = a_ref[0].astype(bf16)
        w3_ref[1] = b_ref[0].astype(bf16)
        w3_ref[2] = c_ref[0].astype(bf16)
        wrg_ref[0] = ra_ref[0].astype(bf16)
        wrg_ref[1] = ri_ref[0].astype(bf16)
        sw_ref[...] = jnp.concatenate([m_ref[...], cw_ref[0], jnp.zeros((4, 256), f32)], axis=0)

    args = (w_rnn_out, w_attn_out, w_o, w_ra, w_ri, meta, conv_w)
    whole = lambda shape: pl.BlockSpec(shape, lambda i, me_ref: (0,) * len(shape))
    slot = lambda shape: pl.BlockSpec((None, *shape), lambda i, me_ref: (me_ref[0], *([0] * len(shape))))
    shapes = [(3, 256, D), (2, N_RNN_BLOCKS, 32, RNN_BLOCK), (24, 256)]
    return pl.pallas_call(
        body,
        grid_spec=pltpu.PrefetchScalarGridSpec(
            num_scalar_prefetch=1, grid=(1,), in_specs=[whole(a.shape) for a in args],
            out_specs=[slot(sh) for sh in shapes]),
        out_shape=[SDS((N_DEV, *sh), dt) for sh, dt in zip(shapes, (bf16, bf16, f32))],
        name="cast_small", compiler_params=_cp(("arbitrary",)),
    )(me_idx, *args)


def _remote(src, dst, send_sems, recv_sems, k, to):
    return pltpu.make_async_remote_copy(src_ref=src, dst_ref=dst, send_sem=send_sems.at[k], recv_sem=recv_sems.at[k],
                                        device_id=to, device_id_type=MESH)


def _all_gather(bufs, chunks):
    n = len(bufs)
    base = [0]
    for ch in chunks:
        base.append(base[-1] + 7 * ch)

    def body(*refs):
        outs = refs[n:2 * n]
        send_sems, recv_sems = refs[2 * n:]
        x, y, c = _place()
        me, sibling = (x, y, c), (x, y, 1 - c)
        chips = [(1 - x, y), (x, 1 - y), (1 - x, 1 - y)]

        def copy(a, i, k, block, to):
            blk = outs[a].at[_dev(*block)]
            if chunks[a] > 1:
                r = bufs[a].shape[1] // chunks[a]
                blk = blk.at[pl.ds(i * r, r)]
            return _remote(blk, blk, send_sems, recv_sems, base[a] + 7 * i + k, to)

        pieces = [(a, i) for a in range(n) for i in range(chunks[a])]
        first = []
        for a, i in pieces:
            first.append(copy(a, i, 0, me, sibling))
            first += [copy(a, i, 1 + j, me, (*chip, c)) for j, chip in enumerate(chips)]
        for cp in first:
            cp.start()
        passed = []
        for a, i in pieces:
            for j, chip in enumerate(chips):
                copy(a, i, 1 + j, (*chip, c), me).wait_recv()
                cp = copy(a, i, 4 + j, (*chip, c), sibling)
                cp.start()
                passed.append(cp)
        for a, i in pieces:
            copy(a, i, 0, sibling, me).wait_recv()
            for j, chip in enumerate(chips):
                copy(a, i, 4 + j, (*chip, 1 - c), me).wait_recv()
        for cp in first + passed:
            cp.wait_send()

    return pl.pallas_call(
        body, in_specs=[_ANY] * n, out_specs=[_ANY] * n,
        out_shape=[SDS(b.shape, b.dtype) for b in bufs],
        input_output_aliases={a: a for a in range(n)},
        scratch_shapes=[pltpu.SemaphoreType.DMA((base[-1],)), pltpu.SemaphoreType.DMA((base[-1],))],
        name="all_gather_weights",
    )(*bufs)


def _copies_own_slot(srcs, lands, send_sems, recv_sems):
    x, y, c = _place()
    out = []
    for a in range(len(srcs)):
        blk = srcs[a].at[_dev(x, y, c)]
        for k, (fx, fy, fc) in enumerate(_PEER_FLIPS):
            out.append(_remote(blk, blk, send_sems, recv_sems, 7 * a + k, ((x + fx) % 2, (y + fy) % 2, (c + fc) % 2)))
    return out


_PEER_FLIPS = [(f // 4, (f // 2) % 2, f % 2) for f in range(1, N_DEV)]


def _copies_direct(same_src):
    def make(srcs, lands, send_sems, recv_sems):
        x, y, c = _place()
        me = _dev(x, y, c)
        out = []
        for a in range(len(srcs)):
            for k, (fx, fy, fc) in enumerate(_PEER_FLIPS):
                peer = ((x + fx) % 2, (y + fy) % 2, (c + fc) % 2)
                src = srcs[a] if same_src else srcs[a].at[_dev(*peer)]
                out.append(_remote(src, lands[a].at[me], send_sems, recv_sems, 7 * a + k, peer))
        return out
    return make


def _copies_siblings(srcs, lands, send_sems, recv_sems):
    x, y, c = _place()
    return [_remote(srcs[a].at[2 * q + (1 - c)], lands[a].at[q], send_sems, recv_sems, 4 * a + q, (x, y, 1 - c))
            for a in range(len(srcs)) for q in range(4)]


def _copies_chips(srcs, lands, send_sems, recv_sems):
    x, y, c = _place()
    chips = [(1 - x, y), (x, 1 - y), (1 - x, 1 - y)]
    return [_remote(srcs[a].at[2 * qx + qy], lands[a].at[j], send_sems, recv_sems, 3 * a + j, (qx, qy, c))
            for a in range(len(srcs)) for j, (qx, qy) in enumerate(chips)]


def _split_start(make, per_array, srcs, lands, dep, name):
    n, tot = len(srcs), len(srcs) + len(lands)

    def body(*refs):
        send_sems, recv_sems, token = refs[tot + 1], refs[tot + 2], refs[-1]
        for cp in make(refs[:n], refs[n:tot], send_sems, recv_sems):
            cp.start()
        token[...] = jnp.zeros_like(token)

    hbm = lambda t: pltpu.with_memory_space_constraint(t, pltpu.HBM)
    res = pl.pallas_call(
        body, name=name,
        out_shape=(pltpu.SemaphoreType.DMA((per_array * n,)), pltpu.SemaphoreType.DMA((per_array * n,)),
                   *[pltpu.HBM(t.shape, t.dtype) for t in (*srcs, *lands)], SDS((8, 128), f32)),
        in_specs=[_HBM] * tot + [_ANY], out_specs=(_SEM, _SEM, *([_HBM] * tot), _VMEM),
        input_output_aliases={i: 2 + i for i in range(tot)},
        compiler_params=pltpu.CompilerParams(has_side_effects=pltpu.SideEffectType.DATAFLOW_SIDE_EFFECTING),
    )(*[hbm(t) for t in (*srcs, *lands)], dep)
    return res[0], res[1], list(res[2:2 + n]), list(res[2 + n:2 + tot]), res[-1]


def _split_wait(make, send_sems, recv_sems, srcs, lands, after, name):
    n, tot = len(srcs), len(srcs) + len(lands)

    def body(*refs):
        for cp in make(refs[:n], refs[n:tot], refs[tot], refs[tot + 1]):
            cp.wait_send()
            cp.wait_recv()

    res = pl.pallas_call(
        body, name=name,
        out_shape=tuple(pltpu.HBM(t.shape, t.dtype) for t in (*srcs, *lands)),
        in_specs=[_HBM] * tot + [_SEM, _SEM, _ANY], out_specs=tuple([_HBM] * tot),
        input_output_aliases={i: i for i in range(tot)},
        compiler_params=pltpu.CompilerParams(has_side_effects=pltpu.SideEffectType.DATAFLOW_SIDE_EFFECTING),
    )(*srcs, *lands, send_sems, recv_sems, after)
    return list(res[:n]), list(res[n:])


def _adamw_direct(g, land, me_idx, w, m, v, name):
    r, wd = w.shape
    tr = min(r, 256)

    def body(me_ref, *refs):
        g_ref, peers = refs[0], refs[1:N_DEV]
        w_ref, m_ref, v_ref, g_out, d_out, m_out, v_out = refs[N_DEV:]
        gs = g_ref[...].astype(f32)
        for p_ref in peers:
            gs = gs + p_ref[...].astype(f32)
        d, mn, vn = _adamw(w_ref[...], gs, m_ref[...], v_ref[...])
        g_out[...] = gs
        d_out[...] = d
        m_out[...] = mn
        v_out[...] = vn

    tile = pl.BlockSpec((tr, wd), lambda i, me_ref: (i, 0))
    slot = lambda k: pl.BlockSpec((None, tr, wd), lambda i, me_ref: ((me_ref[0] + k) % N_DEV, i, 0))
    return pl.pallas_call(
        body,
        grid_spec=pltpu.PrefetchScalarGridSpec(
            num_scalar_prefetch=1, grid=(r // tr,),
            in_specs=[slot(0)] + [slot(k) for k in range(1, N_DEV)] + [tile, tile, tile],
            out_specs=[tile] * 4),
        out_shape=[SDS((r, wd), f32)] * 4, name=name, compiler_params=_cp(("arbitrary",), 48),
    )(me_idx, g, *([land] * (N_DEV - 1)), w, m, v)


def _pair_sum(g, r1, c_idx, name):
    _, r, w = g.shape
    tr = _tile_rows(r)

    def body(c_ref, g_ref, r_ref, o_ref):
        o_ref[...] = (g_ref[...].astype(f32) + r_ref[...].astype(f32)).astype(bf16)

    return pl.pallas_call(
        body,
        grid_spec=pltpu.PrefetchScalarGridSpec(
            num_scalar_prefetch=1, grid=(4, r // tr),
            in_specs=[pl.BlockSpec((None, tr, w), lambda q, i, c_ref: (2 * q + c_ref[0], i, 0)),
                      pl.BlockSpec((None, tr, w), lambda q, i, c_ref: (q, i, 0))],
            out_specs=pl.BlockSpec((None, tr, w), lambda q, i, c_ref: (q, i, 0))),
        out_shape=SDS((4, r, w), bf16), name=name, compiler_params=_cp(("arbitrary", "arbitrary")),
    )(c_idx, g, r1)


def _adamw(w, g, m, v):
    m = ADAM_B1 * m + (1.0 - ADAM_B1) * g
    v = ADAM_B2 * v + (1.0 - ADAM_B2) * (g * g)
    m_hat = m / (1.0 - ADAM_B1 ** ADAM_STEP)
    v_hat = v / (1.0 - ADAM_B2 ** ADAM_STEP)
    delta = -ADAM_LR * (m_hat / (jnp.sqrt(v_hat) + ADAM_EPS) + ADAM_WD * w)
    return delta, m, v


def _adamw_big(part, r2, q_idx, w, m, v, name, row_off=0, cols=(0, 1), prev=None):
    r, wd = w.shape
    tr = _tile_rows(r)
    k, ncol = cols
    wp = wd // ncol

    def body(q_ref, p_ref, r_ref, w_ref, m_ref, v_ref, *rest):
        g_out, d_out, m_out, v_out = rest[-4:]
        g = p_ref[...].astype(f32)
        for j in range(3):
            g = g + r_ref[j].astype(f32)
        d, mn, vn = _adamw(w_ref[...], g, m_ref[...], v_ref[...])
        g_out[...] = g
        d_out[...] = d
        m_out[...] = mn
        v_out[...] = vn

    tile = pl.BlockSpec((tr, wp), lambda i, q_ref: (i, k))
    prev = list(prev) if prev is not None else []
    return pl.pallas_call(
        body,
        grid_spec=pltpu.PrefetchScalarGridSpec(
            num_scalar_prefetch=1, grid=(r // tr,),
            in_specs=[pl.BlockSpec((None, tr, wp), lambda i, q_ref: (q_ref[0], row_off + i, 0)),
                      pl.BlockSpec((3, tr, wp), lambda i, q_ref: (0, row_off + i, 0)), tile, tile, tile]
                     + [pl.BlockSpec(memory_space=pl.ANY)] * len(prev),
            out_specs=[tile] * 4),
        out_shape=[SDS((r, wd), f32)] * 4, name=name,
        input_output_aliases={6 + i: i for i in range(len(prev))},
        compiler_params=_cp(("arbitrary",), 48),
    )(q_idx, part, r2, w, m, v, *prev)


_SMALL_ROWS = 24


def _pack_early(vec_rnn, st_out, dsr, db_in):
    def body(vr_ref, so_ref, dsr_ref, db_ref, sm_ref, sm2_ref):
        sm_ref[...] = jnp.zeros_like(sm_ref)
        sm2_ref[...] = jnp.zeros_like(sm2_ref)
        sm_ref[2:3, :] = vr_ref[3:4, :]
        sm_ref[3:6, :] = vr_ref[0:3, :]
        sm_ref[6:7, :] = so_ref[2:3, :]
        sm_ref[7:9, :] = so_ref[0:2, :]
        sm_ref[10:11, :] = so_ref[3:4, :]
        for h in range(N_KV):
            sm_ref[9:10, h * GROUP:(h + 1) * GROUP] = _colsum(dsr_ref[h])
        for j in range(6):
            sm_ref[16 + j:17 + j, :] = db_ref[0:1, j * D:(j + 1) * D]
        sm_ref[22:23, 0:D_IN - 6 * D] = db_ref[0:1, 6 * D:D_IN]
        for s in range(N_DEV):
            sm2_ref[s, 0:CONV_WIDTH, :] = vr_ref[4:8, s * 256:(s + 1) * 256]

    return pl.pallas_call(
        body, out_shape=[SDS((_SMALL_ROWS, D), f32), SDS((N_DEV, 8, 256), f32)],
        name="pack_early", compiler_params=_cp(None),
    )(vec_rnn, st_out, dsr, db_in)


def _pack_late(st_emb, dmeta):
    def body(se_ref, dm_ref, sm_ref, sm2_ref):
        sm_ref[...] = se_ref[...]
        for s in range(N_DEV):
            sm2_ref[s] = dm_ref[:, s * 256:(s + 1) * 256]

    return pl.pallas_call(
        body, out_shape=[SDS((8, D), f32), SDS((N_DEV, N_META, 256), f32)],
        name="pack_late", compiler_params=_cp(None),
    )(st_emb, dmeta)


def _small_allreduce(sm, sm2):
    def body(sm_ref, sm2_ref, o_ref, o2_ref, buf, buf2, send_sems, recv_sems):
        x, y, c = _place()
        me = _dev(x, y, c)
        copies = []
        for f in range(1, N_DEV):
            fx, fy, fc = f // 4, (f // 2) % 2, f % 2
            peer = ((x + fx) % 2, (y + fy) % 2, (c + fc) % 2)
            for t, (src, dst) in enumerate(((sm_ref, buf), (sm2_ref, buf2))):
                k = 2 * (f - 1) + t
                copies.append(pltpu.make_async_remote_copy(
                    src_ref=src, dst_ref=dst.at[me], send_sem=send_sems.at[k], recv_sem=recv_sems.at[k],
                    device_id=peer, device_id_type=MESH))
        for cp in copies:
            cp.start()
        buf[me] = sm_ref[...]
        buf2[me] = sm2_ref[...]
        for cp in copies:
            cp.wait()
        acc, acc2 = buf[0], buf2[0]
        for e in range(1, N_DEV):
            acc, acc2 = acc + buf[e], acc2 + buf2[e]
        o_ref[...] = acc
        o2_ref[...] = acc2

    return pl.pallas_call(
        body, in_specs=[_VMEM, _VMEM], out_specs=[_VMEM, _VMEM],
        out_shape=[SDS(sm.shape, f32), SDS(sm2.shape, f32)],
        scratch_shapes=[pltpu.VMEM((N_DEV, *sm.shape), f32), pltpu.VMEM((N_DEV, *sm2.shape), f32),
                        pltpu.SemaphoreType.DMA((14,)), pltpu.SemaphoreType.DMA((14,))],
        name="small_allreduce",
    )(sm, sm2)


_SMALL_ROW_OF = {"ln_emb_g": 0, "ln_emb_b": 1, "conv_b": 2, "b_ra": 3, "b_ri": 4, "lru_lambda": 5, "b_o": 6,
                 "ln_g": 7, "ln_b": 8}
_SMALL_NAMES = ["ln_emb_g", "ln_emb_b", "conv_b", "b_ra", "b_ri", "lru_lambda", "b_o", "ln_g", "ln_b",
                "sinks", "b_in", "meta_tokens", "conv_w"]


def _small_update(me_idx, early, late, wmv):
    n_fixed = 7

    def in_order(me, own_ref, land_ref):
        acc = None
        for e in range(N_DEV):
            term = jnp.where(me == e, own_ref[...], land_ref[e])
            acc = term if acc is None else acc + term
        return acc

    def body(*refs):
        me_ref, own_ref, land_ref, cown_ref, cland_ref, late_ref, meta_ref = refs[:n_fixed]
        ins = refs[n_fixed:n_fixed + 3 * len(_SMALL_NAMES)]
        outs = refs[n_fixed + 3 * len(_SMALL_NAMES):]
        me = me_ref[0]
        sm = in_order(me, own_ref, land_ref)
        conv = in_order(me, cown_ref, cland_ref)

        def grad_of(name):
            if name in ("ln_emb_g", "ln_emb_b"):
                r = _SMALL_ROW_OF[name]
                return late_ref[r:r + 1, :]
            if name in _SMALL_ROW_OF:
                r = _SMALL_ROW_OF[name]
                return sm[r:r + 1, :]
            if name == "sinks":
                return sm[9:10, 0:N_KV * GROUP]
            if name == "b_in":
                return jnp.concatenate([sm[16 + j:17 + j, :] for j in range(7)], axis=1)[:, :D_IN]
            if name == "meta_tokens":
                return meta_ref[...]
            return conv[0:CONV_WIDTH, :]

        for i, name in enumerate(_SMALL_NAMES):
            w_ref, m_ref, v_ref = ins[3 * i:3 * i + 3]
            g = grad_of(name)
            d, mn, vn = _adamw(w_ref[...], g, m_ref[...], v_ref[...])
            outs[4 * i][...] = g
            outs[4 * i + 1][...] = d
            outs[4 * i + 2][...] = mn
            outs[4 * i + 3][...] = vn
        outs[-1][...] = jnp.broadcast_to(jnp.sum(sm[10:11, :], axis=1, keepdims=True), (8, 128))

    args, out_shape = [me_idx, *early, *late], []
    for name in _SMALL_NAMES:
        args += list(wmv[name])
        out_shape += [SDS(wmv[name][0].shape, f32)] * 4
    out_shape.append(SDS((8, 128), f32))
    res = pl.pallas_call(
        body, out_shape=out_shape, in_specs=[pl.BlockSpec(memory_space=pltpu.SMEM)] + [_VMEM] * (len(args) - 1),
        name="small_update", compiler_params=_cp(None))(*args)
    return {name: tuple(res[4 * i:4 * i + 4]) for i, name in enumerate(_SMALL_NAMES)}, res[-1][0, 0]


_WEIGHTS = ["meta_tokens", "ln_emb_g", "ln_emb_b", "w_in", "b_in", "conv_w", "conv_b", "w_ra", "b_ra", "w_ri",
            "b_ri", "lru_lambda", "sinks", "w_rnn_out", "w_attn_out", "w_o", "b_o", "ln_g", "ln_b"]
_SMALL_2D = {"meta_tokens": (N_META, 256), "conv_w": (CONV_WIDTH, 256), "b_in": (1, D_IN), "sinks": (1, N_KV * GROUP)}


def kernel(x, meta_tokens, ln_emb_g, ln_emb_b, w_in, b_in, conv_w, conv_b, w_ra, b_ra, w_ri, b_ri, lru_lambda, sinks, w_rnn_out, w_attn_out, w_o, b_o, ln_g, ln_b, loss_target, m_meta_tokens, m_ln_emb_g, m_ln_emb_b, m_w_in, m_b_in, m_conv_w, m_conv_b, m_w_ra, m_b_ra, m_w_ri, m_b_ri, m_lru_lambda, m_sinks, m_w_rnn_out, m_w_attn_out, m_w_o, m_b_o, m_ln_g, m_ln_b, v_meta_tokens, v_ln_emb_g, v_ln_emb_b, v_w_in, v_b_in, v_conv_w, v_conv_b, v_w_ra, v_b_ra, v_w_ri, v_b_ri, v_lru_lambda, v_sinks, v_w_rnn_out, v_w_attn_out, v_w_o, v_b_o, v_ln_g, v_ln_b):
    w = dict(meta_tokens=meta_tokens, ln_emb_g=ln_emb_g, ln_emb_b=ln_emb_b, w_in=w_in, b_in=b_in, conv_w=conv_w,
             conv_b=conv_b, w_ra=w_ra, b_ra=b_ra, w_ri=w_ri, b_ri=b_ri, lru_lambda=lru_lambda, sinks=sinks,
             w_rnn_out=w_rnn_out, w_attn_out=w_attn_out, w_o=w_o, b_o=b_o, ln_g=ln_g, ln_b=ln_b)
    m = dict(meta_tokens=m_meta_tokens, ln_emb_g=m_ln_emb_g, ln_emb_b=m_ln_emb_b, w_in=m_w_in, b_in=m_b_in,
             conv_w=m_conv_w, conv_b=m_conv_b, w_ra=m_w_ra, b_ra=m_b_ra, w_ri=m_w_ri, b_ri=m_b_ri,
             lru_lambda=m_lru_lambda, sinks=m_sinks, w_rnn_out=m_w_rnn_out, w_attn_out=m_w_attn_out, w_o=m_w_o,
             b_o=m_b_o, ln_g=m_ln_g, ln_b=m_ln_b)
    v = dict(meta_tokens=v_meta_tokens, ln_emb_g=v_ln_emb_g, ln_emb_b=v_ln_emb_b, w_in=v_w_in, b_in=v_b_in,
             conv_w=v_conv_w, conv_b=v_conv_b, w_ra=v_w_ra, b_ra=v_b_ra, w_ri=v_w_ri, b_ri=v_b_ri,
             lru_lambda=v_lru_lambda, sinks=v_sinks, w_rnn_out=v_w_rnn_out, w_attn_out=v_w_attn_out, w_o=v_w_o,
             b_o=v_b_o, ln_g=v_ln_g, ln_b=v_ln_b)
    px, py, pc = _place()
    as_idx = lambda t: jnp.reshape(t, (1,)).astype(jnp.int32)
    c_idx, q_idx, me_idx = as_idx(pc), as_idx(2 * px + py), as_idx(_dev(px, py, pc))

    w3_s, wrg_s, small_s = _cast_small(me_idx, w_rnn_out, w_attn_out, w_o, w_ra, w_ri, meta_tokens, conv_w)
    vec = lambda name: w[name].reshape(1, -1)
    p = {k: vec(k) for k in ("ln_emb_g", "ln_emb_b", "b_in", "conv_b", "b_ra", "b_ri", "lru_lambda", "sinks",
                             "b_o", "ln_g", "ln_b")}
    w_in_t = lambda a: jnp.swapaxes(a, 1, 2).reshape(SHARD_IN, D)
    wg, wrg, smallw = _all_gather([_cast_w_in(w_in_t(w_in), me_idx), wrg_s, small_s], [10, 1, 1])
    w3_pending = _split_start(_copies_own_slot, 7, [w3_s], [], smallw, "gather_w3_start")
    w_full = wg.reshape(D_IN, D)

    zero = w3_pending[4][0:1, 0:1]
    h32, hb = _ln_emb(x, smallw, p["ln_emb_g"], p["ln_emb_b"])
    z = _mm(hb, w_full, nt=True, bias=p["b_in"] + zero, name="mm_z")
    s = _step_attn(_step_rnn(h32, hb, z, wrg, smallw, p, zero), p, zero)
    w3 = _split_wait(_copies_own_slot, *w3_pending[:4], s["lse"], "gather_w3_wait")[0][0]
    t = _step_merge(s, loss_target, w3, p)

    big = {}
    two_d = lambda name: (w[name].shape[-2], w[name].shape[-1])
    proj = ("w_o", "w_rnn_out", "w_attn_out")
    g_proj = [t[k].reshape(N_DEV, 256, D) for k in ("g_wo", "g_wrnn", "g_wattn")]
    g_pending = _split_start(_copies_direct(False), 7, g_proj, [lax.empty((N_DEV, 256, D), bf16) for _ in proj],
                             p["b_o"], "reduce_proj_start")
    u = _step_backward(s, t, wrg, smallw, p, p["conv_b"] + g_pending[4][0:1, 0:1])

    def siblings_start(gs, dep, tag):
        return _split_start(_copies_siblings, 4, gs, [lax.empty((4, *g.shape[1:]), bf16) for g in gs], dep,
                            "reduce_siblings_start_" + tag)

    def chips_start(gs, r1, dep, tag):
        parts = [_pair_sum(g, r, c_idx, "pair_sum_%s%d" % (tag, i)) for i, (g, r) in enumerate(zip(gs, r1))]
        return _split_start(_copies_chips, 3, parts, [lax.empty((3, *q.shape[1:]), bf16) for q in parts], dep,
                            "reduce_chips_start_" + tag)

    g_a, dz, db_in = _mm_dwin_parts(s["hb"], u["dz_parts"])
    shards = lambda g: g.reshape(N_DEV, SHARD_IN, W_IN_HALF)
    sib_a = siblings_start([shards(g_a), u["g_wrg"].reshape(N_DEV, 2 * RNN_BLOCK, RNN_BLOCK)], db_in, "a")
    g_proj, g_land = _split_wait(_copies_direct(False), *g_pending[:4], sib_a[4], "reduce_proj_wait")
    for i, name in enumerate(proj):
        res = _adamw_direct(g_proj[i], g_land[i], me_idx, w[name].reshape(two_d(name)), m[name].reshape(two_d(name)),
                            v[name].reshape(two_d(name)), "adamw_" + name)
        big[name] = tuple(r.reshape(w[name].shape) for r in res)
    chp_a = chips_start(*_split_wait(_copies_siblings, *sib_a[:4], big["w_attn_out"][3], "reduce_siblings_wait_a"),
                        db_in, "a")
    g_b = _mm_dwin(s["hb"], dz, chp_a[4])
    sib_b = siblings_start([shards(g_b)], db_in, "b")
    sm_e = _pack_early(u["vec_rnn"], t["st_out"], u["dsr"], db_in)
    early = _split_start(_copies_direct(True), 7, list(sm_e),
                         [lax.empty((N_DEV, *a.shape), f32) for a in sm_e], sib_b[4], "small_early_start")
    dh_lo = _mm_dh(dz, w_full, early[4], 0)
    chp_b = chips_start(*_split_wait(_copies_siblings, *sib_b[:4], dh_lo, "reduce_siblings_wait_b"), db_in, "b")
    dh_hi = _mm_dh(dz, w_full, chp_b[4], 1)
    parts_a, r2_a = _split_wait(_copies_chips, *chp_a[:4], dh_hi, "reduce_chips_wait_a")
    w_in_res = _adamw_big(parts_a[0], r2_a[0], q_idx, w_in_t(w["w_in"]), w_in_t(m["w_in"]), w_in_t(v["w_in"]),
                          "adamw_w_in_a", cols=(0, 2))
    u.update(_step_input_grad(dh_lo, dh_hi, t["du32"], x, smallw, p, w_in_res[3]))
    sm_l, meta_l = _small_allreduce(*_pack_late(u["st_emb"], u["dmeta"]))
    (sm_own, conv_own), (sm_land, conv_land) = _split_wait(_copies_direct(True), *early[:4], sm_l, "small_early_wait")
    me = _dev(px, py, pc)
    mine = lambda a, axis: lax.dynamic_index_in_dim(a, me, axis, keepdims=False)
    two = lambda name, t: t.reshape(_SMALL_2D.get(name, (1, D)))
    small, loss = _small_update(me_idx, (sm_own, sm_land, mine(conv_own, 0), mine(conv_land, 1)),
                                (sm_l, mine(meta_l, 0)),
                                {k: (two(k, w[k]), two(k, m[k]), two(k, v[k])) for k in _SMALL_NAMES})

    parts_b, r2_b = _split_wait(_copies_chips, *chp_b[:4], small["b_in"][2], "reduce_chips_wait_b")
    res = _adamw_big(parts_b[0], r2_b[0], q_idx, w_in_t(w["w_in"]), w_in_t(m["w_in"]), w_in_t(v["w_in"]),
                     "adamw_w_in_b", cols=(1, 2), prev=w_in_res)
    big["w_in"] = tuple(jnp.swapaxes(r.reshape(1, SHARD_IN, D), 1, 2) for r in res)
    for i, name in enumerate(("w_ra", "w_ri")):
        sq = (RNN_BLOCK, RNN_BLOCK)
        res = _adamw_big(parts_a[1], r2_a[1], q_idx, w[name].reshape(sq), m[name].reshape(sq), v[name].reshape(sq),
                         "adamw_" + name, row_off=i)
        big[name] = tuple(r.reshape(w[name].shape) for r in res)
    res = dict(big)
    for k in _SMALL_NAMES:
        res[k] = tuple(t.reshape(w[k].shape) for t in small[k])

    outs = [loss, u["grad_x"]]
    for j in range(4):
        outs += [res[k][j] for k in _WEIGHTS]
    return tuple(outs)
```

```python
import jax
import jax.numpy as jnp
from jax import lax
from jax.experimental import pallas as pl
from jax.experimental.pallas import tpu as pltpu

f32, bf16 = jnp.float32, jnp.bfloat16
SDS = jax.ShapeDtypeStruct

N_DEV = 8
D = 2048
N_META = 16
BLK = 128
ROW0 = BLK - N_META
N_RNN_BLOCKS = 8
RNN_BLOCK = D // N_RNN_BLOCKS
CONV_WIDTH = 4
LRU_C = 8.0
HEAD_DIM = 64
N_KV = 4
GROUP = 8
HALF = HEAD_DIM // 2
ROPE_THETA = 10000.0
NEG_INF = -1e30
LN_EPS = 1e-5
ALPHA = 2.0 ** 0.25
D_IN = 12800
SHARD_IN = D_IN // N_DEV
W_IN_HALF = D // 2
W_IN_LATE = 640
W_IN_EARLY = SHARD_IN - W_IN_LATE
OFF_GR, OFF_Q, OFF_K, OFF_V, OFF_GA, OFF_G = 2048, 4096, 6144, 6400, 6656, 8704
ADAM_LR, ADAM_B1, ADAM_B2, ADAM_EPS, ADAM_WD, ADAM_STEP = 1e-3, 0.9, 0.999, 1e-8, 0.01, 10
VMEM_LIMIT_MB = 56
MESH = pl.DeviceIdType.MESH


def _cp(sem=None, vmem_mb=40):
    return pltpu.CompilerParams(dimension_semantics=sem, vmem_limit_bytes=vmem_mb * 2 ** 20)


def _row_chunk(m):
    best = 16
    for c in range(16, 641, 16):
        if m % c == 0:
            best = c
    return best


def _sigmoid(x):
    return 1.0 / (1.0 + jnp.exp(-x))


def _silu_and_grad(x):
    s = _sigmoid(x)
    return x * s, s * (1.0 + x * (1.0 - s))


def _log_sigmoid(x):
    return jnp.minimum(x, 0.0) - jnp.log1p(jnp.exp(-jnp.abs(x)))


def _ln_rows(v, g, b):
    mu = jnp.mean(v, axis=-1, keepdims=True)
    c = v - mu
    var = jnp.mean(c * c, axis=-1, keepdims=True)
    rstd = lax.rsqrt(var + LN_EPS)
    xhat = c * rstd
    return xhat * g + b, xhat, rstd


def _ln_rows_bwd(dy, g, xhat, rstd):
    dxh = dy * g
    m1 = jnp.mean(dxh, axis=-1, keepdims=True)
    m2 = jnp.mean(dxh * xhat, axis=-1, keepdims=True)
    return rstd * (dxh - m1 - xhat * m2)


def _colsum(v):
    return jnp.sum(v, axis=0, keepdims=True)


def _dot(a, b):
    return jnp.dot(a, b, preferred_element_type=f32)


def _dot_nt(a, b):
    return lax.dot_general(a, b, (((1,), (1,)), ((), ())), preferred_element_type=f32)


def _dot_tn(a, b):
    return lax.dot_general(a, b, (((0,), (0,)), ((), ())), preferred_element_type=f32)


def _meta_full(sw_ref):
    return jnp.concatenate([sw_ref[s, 0:N_META, :] for s in range(N_DEV)], axis=1)


def _ln_emb(x, smallw, g_e, b_e):
    seq = x.shape[1]
    rows = seq + BLK
    nb = rows // BLK

    def body(x_ref, sw_ref, g_ref, b_ref, h32_ref, hb_ref):
        i = pl.program_id(0)
        g, b = g_ref[...], b_ref[...]

        def emit(blk):
            h32_ref[...] = blk
            hb_ref[...] = blk.astype(bf16)

        @pl.when(i == 0)
        def _():
            hm = _ln_rows(_meta_full(sw_ref), g, b)[0]
            emit(jnp.concatenate([jnp.zeros((ROW0, D), f32), hm], axis=0))

        @pl.when(i > 0)
        def _():
            emit(_ln_rows(x_ref[0], g, b)[0])

    return pl.pallas_call(
        body, grid=(nb,),
        in_specs=[pl.BlockSpec((1, BLK, D), lambda i: (0, jnp.maximum(i - 1, 0), 0)),
                  pl.BlockSpec((N_DEV, 24, 256), lambda i: (0, 0, 0)),
                  pl.BlockSpec((1, D), lambda i: (0, 0)),
                  pl.BlockSpec((1, D), lambda i: (0, 0))],
        out_specs=[pl.BlockSpec((BLK, D), lambda i: (i, 0)),
                   pl.BlockSpec((BLK, D), lambda i: (i, 0))],
        out_shape=[SDS((rows, D), f32), SDS((rows, D), bf16)],
        name="ln_emb", compiler_params=_cp(("arbitrary",)),
    )(x, smallw, g_e, b_e)


def _ln_emb_bwd(dh_lo, dh_hi, du32, x, smallw, g_e, after):
    seq = x.shape[1]
    rows = seq + BLK
    nb = rows // BLK

    def body(dlo_ref, dhi_ref, du_ref, x_ref, sw_ref, g_ref, after_ref, gx_ref, dmeta_ref, st_ref):
        i = pl.program_id(0)
        g = g_ref[...]
        dht = jnp.concatenate([dlo_ref[...], dhi_ref[...]], axis=1) + ALPHA * du_ref[...]

        @pl.when(i == 0)
        def _():
            v = jnp.concatenate([jnp.zeros((ROW0, D), f32), _meta_full(sw_ref)], axis=0)
            valid = lax.broadcasted_iota(jnp.int32, (BLK, 1), 0) >= ROW0
            d = jnp.where(valid, dht, 0.0)
            _, xhat, rstd = _ln_rows(v, g, 0.0)
            dv = _ln_rows_bwd(d, g, xhat, rstd)
            dmeta_ref[...] = dv[ROW0:, :]
            st_ref[...] = jnp.concatenate([_colsum(d * xhat), _colsum(d), jnp.zeros((6, D), f32)], axis=0)

        @pl.when(i > 0)
        def _():
            _, xhat, rstd = _ln_rows(x_ref[0], g, 0.0)
            gx_ref[0] = _ln_rows_bwd(dht, g, xhat, rstd)
            st_ref[0:1, :] += _colsum(dht * xhat)
            st_ref[1:2, :] += _colsum(dht)

    return pl.pallas_call(
        body, grid=(nb,),
        in_specs=[pl.BlockSpec((BLK, W_IN_HALF), lambda i: (i, 0)),
                  pl.BlockSpec((BLK, W_IN_HALF), lambda i: (i, 0)),
                  pl.BlockSpec((BLK, D), lambda i: (i, 0)),
                  pl.BlockSpec((1, BLK, D), lambda i: (0, jnp.maximum(i - 1, 0), 0)),
                  pl.BlockSpec((N_DEV, 24, 256), lambda i: (0, 0, 0)),
                  pl.BlockSpec((1, D), lambda i: (0, 0)),
                  pl.BlockSpec(memory_space=pl.ANY)],
        out_specs=[pl.BlockSpec((1, BLK, D), lambda i: (0, jnp.maximum(i - 1, 0), 0)),
                   pl.BlockSpec((N_META, D), lambda i: (0, 0)),
                   pl.BlockSpec((8, D), lambda i: (0, 0))],
        out_shape=[SDS((1, seq, D), f32), SDS((N_META, D), f32), SDS((8, D), f32)],
        name="ln_emb_bwd", compiler_params=_cp(("arbitrary",)),
    )(dh_lo, dh_hi, du32, x, smallw, g_e, after)


def _mm(a, b, *, name, nt=False, sel=None, bias=None, out_dtype=f32, tn=512):
    m, k = a.shape
    cm = _row_chunk(m)
    stacked = sel is not None
    n = D if stacked else (b.shape[0] if nt else b.shape[1])
    am = m
    if stacked and nt:
        b_spec = pl.BlockSpec((tn // 256, None, 256, D), lambda j, i: (j, sel, 0, 0))
    elif stacked:
        b_spec = pl.BlockSpec((N_DEV, None, 256, tn), lambda j, i: (0, sel, 0, j))
    elif nt:
        b_spec = pl.BlockSpec((tn, k), lambda j, i: (j, 0))
    else:
        b_spec = pl.BlockSpec((k, tn), lambda j, i: (0, j))
    in_specs = [pl.BlockSpec((am, k), lambda j, i: (i, 0)), b_spec]
    args = [a, b]
    if bias is not None:
        in_specs.append(pl.BlockSpec((1, tn), lambda j, i: (0, j)))
        args.append(bias)

    def body(*refs):
        a_ref, b_ref, o_ref = refs[0], refs[1], refs[-1]
        bm = b_ref[...]
        if stacked:
            bm = bm.reshape((tn, D) if nt else (D, tn))
        for c in range(am // cm):
            acc = (_dot_nt if nt else _dot)(a_ref[c * cm:(c + 1) * cm, :], bm)
            if bias is not None:
                acc = acc + refs[2][...]
            o_ref[c * cm:(c + 1) * cm, :] = acc.astype(out_dtype)

    return pl.pallas_call(
        body, grid=(n // tn, m // am), in_specs=in_specs,
        out_specs=pl.BlockSpec((am, tn), lambda j, i: (i, j)),
        out_shape=SDS((m, n), out_dtype), name=name, compiler_params=_cp(("arbitrary", "arbitrary"), 48),
    )(*args)


def _mm_z(hb, w_t, bias, side, name, z_prev=None):
    rows, k = hb.shape
    tn = W_IN_LATE
    cm = _row_chunk(rows)
    per = 2 * SHARD_IN // tn
    if side is None:
        side, count = jnp.zeros((1,), jnp.int32), per - 2
        tile = lambda q, t, s_ref: per * q + 1 + t
    else:
        count = 1
        tile = lambda q, t, s_ref: per * q + (per - 1) * s_ref[0]

    def body(s_ref, a_ref, b_ref, bias_ref, *rest):
        o_ref = rest[-1]
        for c in range(rows // cm):
            o_ref[c * cm:(c + 1) * cm, :] = _dot_nt(a_ref[c * cm:(c + 1) * cm, :], b_ref[...]) + bias_ref[...]

    in_specs = [pl.BlockSpec((rows, k), lambda q, t, s_ref: (0, 0)),
                pl.BlockSpec((tn, k), lambda q, t, s_ref: (tile(q, t, s_ref), 0)),
                pl.BlockSpec((1, tn), lambda q, t, s_ref: (0, tile(q, t, s_ref)))]
    args = [side, hb, w_t, bias]
    if z_prev is not None:
        in_specs.append(pl.BlockSpec(memory_space=pl.ANY))
        args.append(z_prev)
    return pl.pallas_call(
        body,
        grid_spec=pltpu.PrefetchScalarGridSpec(
            num_scalar_prefetch=1, grid=(N_DEV // 2, count), in_specs=in_specs,
            out_specs=pl.BlockSpec((rows, tn), lambda q, t, s_ref: (0, tile(q, t, s_ref)))),
        out_shape=SDS((rows, D_IN), f32), name=name,
        input_output_aliases={} if z_prev is None else {4: 0},
        compiler_params=_cp(("arbitrary", "arbitrary"), 48),
    )(*args)


def _mm_dh(dz, w_t, after, half):
    rows = dz.shape[0]
    tn = 512
    nt = W_IN_HALF // tn
    cm = _row_chunk(rows) // 2

    def body(a_ref, w_ref, after_ref, o_ref):
        o_ref[...] = _dot(a_ref[...], w_ref[...])

    return pl.pallas_call(
        body, grid=(nt, rows // cm),
        in_specs=[pl.BlockSpec((cm, D_IN), lambda j, i: (i, 0)),
                  pl.BlockSpec((D_IN, tn), lambda j, i: (0, half * nt + j)),
                  pl.BlockSpec(memory_space=pl.ANY)],
        out_specs=pl.BlockSpec((cm, tn), lambda j, i: (i, j)),
        out_shape=SDS((rows, W_IN_HALF), f32), name="mm_dh_%d" % half,
        compiler_params=_cp(("arbitrary", "arbitrary"), 48),
    )(dz, w_t, after)


def _mm_dwin_parts(hb, parts):
    rows = hb.shape[0]
    tc = 512
    edges = [0]
    for _, w in parts:
        edges.append(edges[-1] + w // tc)

    def body(*refs):
        h_ref, (o_ref, dz_ref, db_ref) = refs[len(parts)], refs[len(parts) + 1:]
        j = pl.program_id(0)
        for p_ref, lo, hi in zip(refs, edges[:-1], edges[1:]):
            @pl.when((j >= lo) & (j < hi))
            def _():
                o_ref[...] = _dot_tn(p_ref[...], h_ref[...]).astype(bf16)
                dz_ref[...] = p_ref[...]

                def step(i, s):
                    blk = p_ref[pl.ds(pl.multiple_of(i * BLK, BLK), BLK), :].astype(f32)
                    return s + blk.reshape(BLK // 8, 8, tc).sum(axis=0)
                s = lax.fori_loop(0, rows // BLK, step, jnp.zeros((8, tc), f32))
                db_ref[...] = jnp.broadcast_to(_colsum(s), (8, tc))

    in_specs = [pl.BlockSpec((rows, tc), lambda j, lo=lo, hi=hi: (0, jnp.clip(j - lo, 0, hi - lo - 1)))
                for lo, hi in zip(edges[:-1], edges[1:])]
    return pl.pallas_call(
        body, grid=(D_IN // tc,),
        in_specs=in_specs + [pl.BlockSpec((rows, W_IN_HALF), lambda j: (0, 0))],
        out_specs=[pl.BlockSpec((tc, W_IN_HALF), lambda j: (j, 0)), pl.BlockSpec((rows, tc), lambda j: (0, j)),
                   pl.BlockSpec((8, tc), lambda j: (0, j))],
        out_shape=[SDS((D_IN, W_IN_HALF), bf16), SDS((rows, D_IN), bf16), SDS((8, D_IN), f32)],
        name="mm_dwin_0", compiler_params=_cp(("arbitrary",), VMEM_LIMIT_MB),
    )(*[a for a, _ in parts], hb)


def _mm_dwin(hb, dz, after):
    rows = dz.shape[0]
    tc = 640

    def body(dz_ref, h_ref, after_ref, o_ref):
        o_ref[...] = _dot_tn(dz_ref[...], h_ref[...]).astype(bf16)

    return pl.pallas_call(
        body, grid=(D_IN // tc,),
        in_specs=[pl.BlockSpec((rows, tc), lambda j: (0, j)),
                  pl.BlockSpec((rows, W_IN_HALF), lambda j: (0, 1)),
                  pl.BlockSpec(memory_space=pl.ANY)],
        out_specs=pl.BlockSpec((tc, W_IN_HALF), lambda j: (j, 0)),
        out_shape=SDS((D_IN, W_IN_HALF), bf16),
        name="mm_dwin_1", compiler_params=_cp(("arbitrary",), 48),
    )(dz, hb, after)


SCAN_ROWS = 32


def _scan8(a, b, reverse):
    idx = lax.broadcasted_iota(jnp.int32, a.shape, 0)
    for s in (1, 2, 4):
        sh = 8 - s if reverse else s
        a_sh, b_sh = pltpu.roll(a, sh, 0), pltpu.roll(b, sh, 0)
        m = (idx < 8 - s) if reverse else (idx >= s)
        b = jnp.where(m, a * b_sh + b, b)
        a = jnp.where(m, a * a_sh, a)
    return a, b


def _shift_rows(prev8, cur, k):
    ext = jnp.concatenate([prev8, cur], axis=0)
    return pltpu.roll(ext, k, 0)[8:, :]


def _gates(xc, w_ra, b_ra, w_ri, b_ri, ls):
    xb = xc.astype(bf16)
    r = _sigmoid(_dot(xb, w_ra) + b_ra)
    ig = _sigmoid(_dot(xb, w_ri) + b_ri)
    la = LRU_C * r * ls
    a = jnp.exp(la)
    mult = jnp.sqrt(jnp.tanh(-la) * (1.0 + a * a))
    return xb, r, ig, a, mult


_RNN_IN_SPECS = lambda rows: [
    pl.BlockSpec((1, 24, 256), lambda n: (n, 0, 0)),
    pl.BlockSpec((1, RNN_BLOCK), lambda n: (0, n)),
    pl.BlockSpec((N_DEV, 2, None, 32, RNN_BLOCK), lambda n: (0, 0, n, 0, 0)),
    pl.BlockSpec((1, RNN_BLOCK), lambda n: (0, n)),
    pl.BlockSpec((1, RNN_BLOCK), lambda n: (0, n)),
    pl.BlockSpec((1, RNN_BLOCK), lambda n: (0, n)),
]


def _rnn_fwd(z, smallw, conv_b, wrg, b_ra, b_ri, lam):
    rows = z.shape[0]
    nb = rows // BLK
    col = lambda off: pl.BlockSpec((rows, RNN_BLOCK), lambda n: (0, off // RNN_BLOCK + n))

    def body(xr_ref, gr_ref, sw_ref, cb_ref, w_ref, bra_ref, bri_ref, lam_ref, xc_ref, hr_ref, ya_ref, yat_ref, a_s):
        cw = sw_ref[0, N_META:24, :]
        cb = cb_ref[...]
        w_ra = w_ref[:, 0].reshape(RNN_BLOCK, RNN_BLOCK)
        w_ri = w_ref[:, 1].reshape(RNN_BLOCK, RNN_BLOCK)
        b_ra_v, b_ri_v = bra_ref[...], bri_ref[...]
        ls = _log_sigmoid(lam_ref[...])
        rid = lax.broadcasted_iota(jnp.int32, (BLK, 1), 0)

        def blk_step(i, carry):
            r0 = pl.multiple_of(i * BLK, BLK)
            grow = rid + r0
            valid = grow >= ROW0
            cur = jnp.where(valid, xr_ref[pl.ds(r0, BLK), :], 0.0)
            prev8 = xr_ref[pl.ds(pl.multiple_of(jnp.maximum(r0 - 8, 0), 8), 8), :] * (i > 0).astype(f32)
            xc = cb + cw[0:1] * cur
            for k in range(1, CONV_WIDTH):
                xc = xc + cw[k:k + 1] * _shift_rows(prev8, cur, k)
            xc_ref[pl.ds(r0, BLK), :] = xc
            _, _, ig, a, mult = _gates(xc, w_ra, b_ra_v, w_ri, b_ri_v, ls)
            mult = jnp.where(grow == ROW0, 1.0, mult)
            a_s[pl.ds(r0, BLK), :] = a
            hr_ref[pl.ds(r0, BLK), :] = jnp.where(valid, mult * ig * xc, 0.0)
            return carry

        lax.fori_loop(0, nb, blk_step, 0)

        def scan_step(j, carry):
            r0 = pl.multiple_of(j * SCAN_ROWS, SCAN_ROWS)
            tiles = [_scan8(a_s[pl.ds(r0 + 8 * k, 8), :], hr_ref[pl.ds(r0 + 8 * k, 8), :], False)
                     for k in range(SCAN_ROWS // 8)]
            for k, (a, b) in enumerate(tiles):
                h = b + a * carry
                hr_ref[pl.ds(r0 + 8 * k, 8), :] = h
                carry = jnp.broadcast_to(h[7:8, :], (8, RNN_BLOCK))
            return carry

        lax.fori_loop(0, rows // SCAN_ROWS, scan_step, jnp.zeros((8, RNN_BLOCK), f32))

        def gate_step(i, carry):
            r0 = pl.multiple_of(i * BLK, BLK)
            ya_ref[pl.ds(r0, BLK), :] = (hr_ref[pl.ds(r0, BLK), :]
                                         * _silu_and_grad(gr_ref[pl.ds(r0, BLK), :])[0]).astype(bf16)
            return carry

        lax.fori_loop(0, nb, gate_step, 0)
        yat_ref[...] = ya_ref[...].astype(f32).T.astype(bf16)

    return pl.pallas_call(
        body, grid=(N_RNN_BLOCKS,),
        in_specs=[col(0), col(OFF_GR)] + _RNN_IN_SPECS(rows),
        out_specs=[pl.BlockSpec((rows, RNN_BLOCK), lambda n: (0, n))] * 3
                  + [pl.BlockSpec((RNN_BLOCK, rows), lambda n: (n, 0))],
        out_shape=[SDS((rows, D), f32), SDS((rows, D), f32), SDS((rows, D), bf16), SDS((D, rows), bf16)],
        scratch_shapes=[pltpu.VMEM((rows, RNN_BLOCK), f32)],
        name="rnn_fwd", compiler_params=_cp(("arbitrary",)),
    )(z, z, smallw, conv_b, wrg, b_ra, b_ri, lam)


def _rnn_bwd(dya, hr, xc, z, smallw, conv_b, wrg, b_ra, b_ri, lam):
    rows = z.shape[0]
    nb = rows // BLK
    col = lambda off: pl.BlockSpec((rows, RNN_BLOCK), lambda n: (0, off // RNN_BLOCK + n))
    blk = pl.BlockSpec((rows, RNN_BLOCK), lambda n: (0, n))

    def body(dya_ref, hr_ref, xc_ref, xr_ref, gr_ref, sw_ref, cb_ref, w_ref, bra_ref, bri_ref, lam_ref,
             dxr_ref, dgr_ref, dw_ref, vec_ref, a_s, lam_s, dxc_s, r_s, ig_s, mult_s, dw_s):
        cw = sw_ref[0, N_META:24, :]
        w_ra = w_ref[:, 0].reshape(RNN_BLOCK, RNN_BLOCK)
        w_ri = w_ref[:, 1].reshape(RNN_BLOCK, RNN_BLOCK)
        b_ra_v, b_ri_v = bra_ref[...], bri_ref[...]
        lam_v = lam_ref[...]
        ls = _log_sigmoid(lam_v)
        rid = lax.broadcasted_iota(jnp.int32, (BLK, 1), 0)
        zrow = jnp.zeros((1, RNN_BLOCK), f32)

        def p1(i, carry):
            r0 = pl.multiple_of(i * BLK, BLK)
            sl = pl.ds(r0, BLK)
            _, r, ig, a, mult = _gates(xc_ref[sl, :], w_ra, b_ra_v, w_ri, b_ri_v, ls)
            a_s[sl, :] = a
            r_s[sl, :] = r
            ig_s[sl, :] = ig
            mult_s[sl, :] = mult
            sg, dsg = _silu_and_grad(gr_ref[sl, :])
            d = dya_ref[sl, :]
            lam_s[sl, :] = d * sg
            dgr_ref[sl, :] = (d * hr_ref[sl, :] * dsg).astype(bf16)
            return carry

        lax.fori_loop(0, nb, p1, 0)

        def p2(jj, carry):
            r0 = pl.multiple_of((rows // SCAN_ROWS - 1 - jj) * SCAN_ROWS, SCAN_ROWS)
            idx = lax.broadcasted_iota(jnp.int32, (8, RNN_BLOCK), 0)
            tiles = []
            for k in range(SCAN_ROWS // 8):
                sl = pl.ds(r0 + 8 * k, 8)
                a, g = a_s[sl, :], lam_s[sl, :]
                tiles.append((g, *_scan8(a, a * g, True)))
            for k in reversed(range(SCAN_ROWS // 8)):
                g, ca, cb_ = tiles[k]
                mu = cb_ + ca * carry
                lam_s[pl.ds(r0 + 8 * k, 8), :] = g + jnp.where(idx < 7, pltpu.roll(mu, 7, 0), carry)
                carry = jnp.broadcast_to(mu[0:1, :], (8, RNN_BLOCK))
            return carry

        lax.fori_loop(0, rows // SCAN_ROWS, p2, jnp.zeros((8, RNN_BLOCK), f32))

        dw_s[...] = jnp.zeros_like(dw_s)

        def p3(i, carry):
            d_bra, d_bri, d_ls = carry
            r0 = pl.multiple_of(i * BLK, BLK)
            sl = pl.ds(r0, BLK)
            grow = rid + r0
            valid = grow >= ROW0
            first = grow == ROW0
            xcv = xc_ref[sl, :]
            xb = xcv.astype(bf16)
            r, ig, a = r_s[sl, :], ig_s[sl, :], a_s[sl, :]
            mult = jnp.where(first, 1.0, mult_s[sl, :])
            lam_t = lam_s[sl, :]
            du = jnp.where(valid, lam_t, 0.0)
            hprev = _shift_rows(hr_ref[pl.ds(pl.multiple_of(jnp.maximum(r0 - 8, 0), 8), 8), :] * (i > 0).astype(f32), hr_ref[sl, :], 1)
            da = lam_t * hprev
            dmult = jnp.where(first, 0.0, du * ig * xcv)
            di = du * mult * xcv
            dxc = du * mult * ig
            ratio = jnp.where(valid & jnp.logical_not(first), a * a / mult, 0.0)
            dla = da * a - dmult * ratio
            dpr = (dla * (LRU_C * ls)) * r * (1.0 - r)
            dpi = di * ig * (1.0 - ig)
            dprb, dpib = dpr.astype(bf16), dpi.astype(bf16)
            dw_s[0] += _dot_tn(xb, dprb)
            dw_s[1] += _dot_tn(xb, dpib)
            dxc_s[sl, :] = dxc + _dot_nt(dprb, w_ra) + _dot_nt(dpib, w_ri)
            return d_bra + _colsum(dpr), d_bri + _colsum(dpi), d_ls + _colsum(dla * (LRU_C * r))

        d_bra, d_bri, d_ls = lax.fori_loop(0, nb, p3, (zrow, zrow, zrow))

        def p4(i, carry):
            d_cb, d_w0, d_w1, d_w2, d_w3 = carry
            r0 = pl.multiple_of(i * BLK, BLK)
            sl = pl.ds(r0, BLK)
            grow = rid + r0
            valid = grow >= ROW0
            dxc = dxc_s[sl, :]
            nxt = dxc_s[pl.ds(pl.multiple_of(jnp.minimum(r0 + BLK, rows - 8), 8), 8), :] * (i < nb - 1).astype(f32)
            ext = jnp.concatenate([dxc, nxt], axis=0)
            dxr = cw[0:1] * dxc
            for k in range(1, CONV_WIDTH):
                dxr = dxr + cw[k:k + 1] * pltpu.roll(ext, BLK + 8 - k, 0)[:BLK, :]
            dxr_ref[sl, :] = jnp.where(valid, dxr, 0.0).astype(bf16)
            cur = jnp.where(valid, xr_ref[sl, :], 0.0)
            prev8 = xr_ref[pl.ds(pl.multiple_of(jnp.maximum(r0 - 8, 0), 8), 8), :] * (i > 0).astype(f32)
            dws = [d_w0 + _colsum(dxc * cur)]
            for k, acc in ((1, d_w1), (2, d_w2), (3, d_w3)):
                dws.append(acc + _colsum(dxc * _shift_rows(prev8, cur, k)))
            return (d_cb + _colsum(dxc), *dws)

        d_cb, d_w0, d_w1, d_w2, d_w3 = lax.fori_loop(0, nb, p4, (zrow,) * 5)

        d_lam = d_ls * _sigmoid(-lam_v)
        vec_ref[...] = jnp.concatenate([d_bra, d_bri, d_lam, d_cb, d_w0, d_w1, d_w2, d_w3], axis=0)
        dw_ref[:, 0] = dw_s[0].astype(bf16).reshape(N_DEV, 32, RNN_BLOCK)
        dw_ref[:, 1] = dw_s[1].astype(bf16).reshape(N_DEV, 32, RNN_BLOCK)

    return pl.pallas_call(
        body, grid=(N_RNN_BLOCKS,),
        in_specs=[blk, blk, blk, col(0), col(OFF_GR)] + _RNN_IN_SPECS(rows),
        out_specs=[blk, blk,
                   pl.BlockSpec((N_DEV, 2, None, 32, RNN_BLOCK), lambda n: (0, 0, n, 0, 0)),
                   pl.BlockSpec((8, RNN_BLOCK), lambda n: (0, n))],
        out_shape=[SDS((rows, D), bf16), SDS((rows, D), bf16),
                   SDS((N_DEV, 2, N_RNN_BLOCKS, 32, RNN_BLOCK), bf16), SDS((8, D), f32)],
        scratch_shapes=[pltpu.VMEM((rows, RNN_BLOCK), f32)] * 6 + [pltpu.VMEM((2, RNN_BLOCK, RNN_BLOCK), f32)],
        name="rnn_bwd", compiler_params=_cp(("arbitrary",), 48),
    )(dya, hr, xc, z, z, smallw, conv_b, wrg, b_ra, b_ri, lam)


def _rope_tables(rows):
    half = jnp.arange(HALF, dtype=f32)
    inv = ROPE_THETA ** (-half / HALF)
    pos = (jnp.arange(rows) - ROW0).astype(f32)
    ang = pos[:, None] * inv[None, :]
    cos, sin = jnp.cos(ang), jnp.sin(ang)
    cos128 = jnp.concatenate([cos, cos, cos, cos], axis=1)
    sin128 = jnp.concatenate([-sin, sin, -sin, sin], axis=1)
    return cos128, sin128


def _rope128(x, cos128, sin128):
    lane = lax.broadcasted_iota(jnp.int32, x.shape, 1)
    swapped = jnp.where(lane % HEAD_DIM < HALF, pltpu.roll(x, 128 - HALF, 1), pltpu.roll(x, HALF, 1))
    return x * cos128 + swapped * sin128


def _qkv_prep(z, cos128, sin128):
    rows = z.shape[0]

    def body(q_ref, kv_ref, c_ref, s_ref, qo_ref, ko_ref, vo_ref):
        c, s = c_ref[...], s_ref[...]
        for g in range(D // 128):
            qo_ref[:, g * 128:(g + 1) * 128] = (_rope128(q_ref[:, g * 128:(g + 1) * 128], c, s)
                                                * (HEAD_DIM ** -0.5)).astype(bf16)
        for g in range(2):
            kr = _rope128(kv_ref[:, g * 128:(g + 1) * 128], c, s)
            for j in range(2):
                ko_ref[2 * g + j] = kr[:, j * HEAD_DIM:(j + 1) * HEAD_DIM].astype(bf16)
        for h in range(N_KV):
            vo_ref[h] = kv_ref[:, 256 + h * HEAD_DIM:256 + (h + 1) * HEAD_DIM].astype(bf16)

    return pl.pallas_call(
        body, grid=(rows // BLK,),
        in_specs=[pl.BlockSpec((BLK, D), lambda i: (i, OFF_Q // D)),
                  pl.BlockSpec((BLK, 512), lambda i: (i, OFF_K // 512)),
                  pl.BlockSpec((BLK, 128), lambda i: (i, 0)),
                  pl.BlockSpec((BLK, 128), lambda i: (i, 0))],
        out_specs=[pl.BlockSpec((BLK, D), lambda i: (i, 0)),
                   pl.BlockSpec((N_KV, BLK, HEAD_DIM), lambda i: (0, i, 0)),
                   pl.BlockSpec((N_KV, BLK, HEAD_DIM), lambda i: (0, i, 0))],
        out_shape=[SDS((rows, D), bf16), SDS((N_KV, rows, HEAD_DIM), bf16), SDS((N_KV, rows, HEAD_DIM), bf16)],
        name="qkv_prep", compiler_params=_cp(("arbitrary",)),
    )(z, z, cos128, sin128)


def _attn_mask(n):
    qi = n * BLK + lax.broadcasted_iota(jnp.int32, (BLK, 2 * BLK + N_META), 0)
    c = lax.broadcasted_iota(jnp.int32, (BLK, 2 * BLK + N_META), 1)
    jb = (n - 1) * BLK + c
    band = (jb >= BLK) & (jb <= qi) & (qi - jb < BLK)
    meta = (ROW0 + c - 2 * BLK) <= qi
    return ((c < 2 * BLK) & band) | ((c >= 2 * BLK) & meta)


N_KEYS = 2 * BLK + N_META


def _stack_heads(t):
    return jnp.concatenate([t[:, g * HEAD_DIM:(g + 1) * HEAD_DIM] for g in range(GROUP)], axis=0)


def _sink_column(sink_ref, h):
    g = lax.broadcasted_iota(jnp.int32, (GROUP, 1, 1), 0)
    col = jnp.zeros((GROUP, 1, 1), f32)
    for j in range(GROUP):
        col = jnp.where(g == j, sink_ref[h * GROUP + j], col)
    return col


def _kv_specs(last):
    cl = lambda n: jnp.minimum(n, last)
    return [pl.BlockSpec((None, N_META, HEAD_DIM), lambda h, n: (h, ROW0 // N_META, 0)),
            pl.BlockSpec((None, BLK, HEAD_DIM), lambda h, n: (h, jnp.maximum(cl(n) - 1, 0), 0)),
            pl.BlockSpec((None, BLK, HEAD_DIM), lambda h, n: (h, cl(n), 0))]


def _attn_fwd(q_r, k_r, v_b, z, sinks):
    rows = q_r.shape[0]
    nb = rows // BLK

    def body(sink_ref, q_ref, km_ref, kp_ref, kc_ref, vm_ref, vp_ref, vc_ref, ga_ref, o_ref, yb_ref, ybt_ref, lse_ref):
        h, n = pl.program_id(0), pl.program_id(1)
        kk = jnp.concatenate([kp_ref[...], kc_ref[...], km_ref[...]], axis=0)
        vv = jnp.concatenate([vp_ref[...], vc_ref[...], vm_ref[...]], axis=0)
        q2 = _stack_heads(q_ref[...])
        s = jnp.where(_attn_mask(n)[None], _dot_nt(q2, kk).reshape(GROUP, BLK, N_KEYS), NEG_INF)
        sink = _sink_column(sink_ref, h)
        m = jnp.maximum(jnp.max(s, axis=-1, keepdims=True), sink)
        p = jnp.exp(s - m)
        den = jnp.sum(p, axis=-1, keepdims=True) + jnp.exp(sink - m)
        o2 = _dot((p / den).astype(bf16).reshape(GROUP * BLK, N_KEYS), vv)
        lse = m + jnp.log(den)
        for g in range(GROUP):
            o_ref[:, g * HEAD_DIM:(g + 1) * HEAD_DIM] = o2[g * BLK:(g + 1) * BLK]
            lse_ref[:, g:g + 1] = lse[g]
        yb = o_ref[...] * _silu_and_grad(ga_ref[...])[0]
        yb_ref[...] = yb.astype(bf16)
        ybt_ref[...] = yb.T.astype(bf16)

    tile = pl.BlockSpec((BLK, 512), lambda h, n: (n, h))
    return pl.pallas_call(
        body, grid=(N_KV, nb),
        in_specs=[pl.BlockSpec(memory_space=pltpu.SMEM), tile] + _kv_specs(nb - 1) + _kv_specs(nb - 1)
                 + [pl.BlockSpec((BLK, 512), lambda h, n: (n, OFF_GA // 512 + h))],
        out_specs=[tile, tile, pl.BlockSpec((512, BLK), lambda h, n: (h, n)),
                   pl.BlockSpec((None, BLK, GROUP), lambda h, n: (h, n, 0))],
        out_shape=[SDS((rows, D), f32), SDS((rows, D), bf16), SDS((D, rows), bf16),
                   SDS((N_KV, rows, GROUP), f32)],
        name="attn_fwd", compiler_params=_cp(("arbitrary", "arbitrary")),
    )(sinks, q_r, k_r, k_r, k_r, v_b, v_b, v_b, z)


def _attn_bwd(dyb, o32, lse, q_r, k_r, v_b, z, sinks):
    rows = q_r.shape[0]
    nb = rows // BLK
    cl = lambda n: jnp.minimum(n, nb - 1)

    def body(sink_ref, dyb_ref, o_ref, lse_ref, q_ref, km_ref, kp_ref, kc_ref, vm_ref, vp_ref, vc_ref, ga_ref,
             dq_ref, dga_ref, dk_ref, dv_ref, dkm_ref, dvm_ref, dsr_ref, ck_s, cv_s):
        h, n = pl.program_id(0), pl.program_id(1)

        @pl.when(n == 0)
        def _():
            dkm_ref[...] = jnp.zeros_like(dkm_ref)
            dvm_ref[...] = jnp.zeros_like(dvm_ref)
            ck_s[...] = jnp.zeros_like(ck_s)
            cv_s[...] = jnp.zeros_like(cv_s)

        @pl.when(n < nb)
        def _():
            kk = jnp.concatenate([kp_ref[...], kc_ref[...], km_ref[...]], axis=0)
            vv = jnp.concatenate([vp_ref[...], vc_ref[...], vm_ref[...]], axis=0)
            sg, dsg = _silu_and_grad(ga_ref[...])
            dyb_v = dyb_ref[...]
            o_v = o_ref[...]
            dga_ref[...] = (dyb_v * o_v * dsg).astype(bf16)
            q2 = _stack_heads(q_ref[...])
            do2 = _stack_heads(dyb_v * sg)
            lse_v = lse_ref[...]
            lse = jnp.concatenate([lse_v[:, g:g + 1] for g in range(GROUP)], axis=0).reshape(GROUP, BLK, 1)
            delta = jnp.sum(do2 * _stack_heads(o_v), axis=-1, keepdims=True).reshape(GROUP, BLK, 1)
            s = jnp.where(_attn_mask(n)[None], _dot_nt(q2, kk).reshape(GROUP, BLK, N_KEYS), NEG_INF)
            p = jnp.exp(s - lse)
            do2b = do2.astype(bf16)
            ds = (p * (_dot_nt(do2b, vv).reshape(GROUP, BLK, N_KEYS) - delta)).astype(bf16)
            ds = ds.reshape(GROUP * BLK, N_KEYS)
            dsr = -jnp.exp(_sink_column(sink_ref, h) - lse) * delta
            dq2 = _dot(ds, kk)
            for g in range(GROUP):
                dq_ref[:, g * HEAD_DIM:(g + 1) * HEAD_DIM] = dq2[g * BLK:(g + 1) * BLK]
                dsr_ref[:, g:g + 1] = dsr[g]
            dkk = _dot_tn(ds, q2)
            dvv = _dot_tn(p.astype(bf16).reshape(GROUP * BLK, N_KEYS), do2b)
            dk_ref[...] = ck_s[...] + dkk[:BLK]
            dv_ref[...] = cv_s[...] + dvv[:BLK]
            ck_s[...] = dkk[BLK:2 * BLK]
            cv_s[...] = dvv[BLK:2 * BLK]
            dkm_ref[...] += dkk[2 * BLK:]
            dvm_ref[...] += dvv[2 * BLK:]

        @pl.when(n == nb)
        def _():
            dk_ref[...] = ck_s[...]
            dv_ref[...] = cv_s[...]

    tile = pl.BlockSpec((BLK, 512), lambda h, n: (cl(n), h))
    kvout = pl.BlockSpec((None, BLK, HEAD_DIM), lambda h, n: (h, jnp.maximum(n - 1, 0), 0))
    mout = pl.BlockSpec((None, N_META, HEAD_DIM), lambda h, n: (h, 0, 0))
    stat = pl.BlockSpec((None, BLK, GROUP), lambda h, n: (h, cl(n), 0))
    return pl.pallas_call(
        body, grid=(N_KV, nb + 1),
        in_specs=[pl.BlockSpec(memory_space=pltpu.SMEM), tile, tile, stat, tile] + _kv_specs(nb - 1)
                 + _kv_specs(nb - 1) + [pl.BlockSpec((BLK, 512), lambda h, n: (cl(n), OFF_GA // 512 + h))],
        out_specs=[tile, tile, kvout, kvout, mout, mout, stat],
        out_shape=[SDS((rows, D), f32), SDS((rows, D), bf16),
                   SDS((N_KV, rows, HEAD_DIM), f32), SDS((N_KV, rows, HEAD_DIM), f32),
                   SDS((N_KV, N_META, HEAD_DIM), f32), SDS((N_KV, N_META, HEAD_DIM), f32),
                   SDS((N_KV, rows, GROUP), f32)],
        scratch_shapes=[pltpu.VMEM((BLK, HEAD_DIM), f32), pltpu.VMEM((BLK, HEAD_DIM), f32)],
        name="attn_bwd", compiler_params=_cp(("arbitrary", "arbitrary")),
    )(sinks, dyb, o32, lse, q_r, k_r, k_r, k_r, v_b, v_b, v_b, z)


def _qkv_finish(dq, dk, dv, dkm, dvm, cos128, sin128):
    rows = dq.shape[0]

    def body(dq_ref, dk_ref, dv_ref, dkm_ref, dvm_ref, c_ref, s_ref, oq_ref, okv_ref):
        first = (pl.program_id(0) == 0).astype(f32)
        c, s = c_ref[...], -s_ref[...]
        for g in range(D // 128):
            oq_ref[:, g * 128:(g + 1) * 128] = (_rope128(dq_ref[:, g * 128:(g + 1) * 128], c, s)
                                                * (HEAD_DIM ** -0.5)).astype(bf16)
        pad = jnp.zeros((ROW0, HEAD_DIM), f32)
        ks = [dk_ref[h] + first * jnp.concatenate([pad, dkm_ref[h]], axis=0) for h in range(N_KV)]
        vs = [dv_ref[h] + first * jnp.concatenate([pad, dvm_ref[h]], axis=0) for h in range(N_KV)]
        for g in range(2):
            kp = jnp.concatenate([ks[2 * g], ks[2 * g + 1]], axis=1)
            okv_ref[:, g * 128:(g + 1) * 128] = _rope128(kp, c, s).astype(bf16)
            okv_ref[:, 256 + g * 128:256 + (g + 1) * 128] = jnp.concatenate([vs[2 * g], vs[2 * g + 1]], axis=1).astype(bf16)

    kv = pl.BlockSpec((N_KV, BLK, HEAD_DIM), lambda i: (0, i, 0))
    mt = pl.BlockSpec((N_KV, N_META, HEAD_DIM), lambda i: (0, 0, 0))
    return pl.pallas_call(
        body, grid=(rows // BLK,),
        in_specs=[pl.BlockSpec((BLK, D), lambda i: (i, 0)), kv, kv, mt, mt,
                  pl.BlockSpec((BLK, 128), lambda i: (i, 0)), pl.BlockSpec((BLK, 128), lambda i: (i, 0))],
        out_specs=[pl.BlockSpec((BLK, D), lambda i: (i, 0)), pl.BlockSpec((BLK, 512), lambda i: (i, 0))],
        out_shape=[SDS((rows, D), bf16), SDS((rows, 512), bf16)],
        name="qkv_finish", compiler_params=_cp(("arbitrary",)),
    )(dq, dk, dv, dkm, dvm, cos128, sin128)


_TW = 512


def _mix_specs(rows):
    tr = _row_chunk(rows)
    tile = pl.BlockSpec((tr, _TW), lambda i, j: (i, j))
    ga = pl.BlockSpec((tr, _TW), lambda i, j: (i, OFF_G // _TW + j))
    gb = pl.BlockSpec((tr, _TW), lambda i, j: (i, (OFF_G + D) // _TW + j))
    return (rows // tr, D // _TW), tile, ga, gb


def _mix_fwd(y_a, y_b, z):
    rows = y_a.shape[0]
    tw = 256
    col = lambda off: pl.BlockSpec((rows, tw), lambda j: (0, off // tw + j))

    def body(ya_ref, yb_ref, ga_ref, gb_ref, o_ref, ot_ref):
        mixed = (_sigmoid(ga_ref[...]) * ya_ref[...].astype(f32)
                 + _sigmoid(gb_ref[...]) * yb_ref[...].astype(f32))
        o_ref[...] = mixed.astype(bf16)
        ot_ref[...] = mixed.T.astype(bf16)

    return pl.pallas_call(
        body, grid=(D // tw,), in_specs=[col(0), col(0), col(OFF_G), col(OFF_G + D)],
        out_specs=[col(0), pl.BlockSpec((tw, rows), lambda j: (j, 0))],
        out_shape=[SDS((rows, D), bf16), SDS((D, rows), bf16)],
        name="mix_fwd", compiler_params=_cp(("arbitrary",)),
    )(y_a, y_b, z, z)


def _mix_bwd(dmixed, y_a, y_b, z):
    rows = y_a.shape[0]
    grid, _mix_tile, _mix_ga, _mix_gb = _mix_specs(rows)

    def body(dm_ref, ya_ref, yb_ref, ga_ref, gb_ref, dya_ref, dyb_ref, dga_ref, dgb_ref):
        dm = dm_ref[...].astype(f32)
        sa, sb = _sigmoid(ga_ref[...]), _sigmoid(gb_ref[...])
        dya_ref[...] = (dm * sa).astype(bf16)
        dyb_ref[...] = (dm * sb).astype(bf16)
        dga_ref[...] = (dm * ya_ref[...].astype(f32) * sa * (1.0 - sa)).astype(bf16)
        dgb_ref[...] = (dm * yb_ref[...].astype(f32) * sb * (1.0 - sb)).astype(bf16)

    return pl.pallas_call(
        body, grid=grid, in_specs=[_mix_tile, _mix_tile, _mix_tile, _mix_ga, _mix_gb],
        out_specs=[_mix_tile] * 4, out_shape=[SDS((rows, D), bf16)] * 4,
        name="mix_bwd", compiler_params=_cp(("arbitrary", "arbitrary")),
    )(dmixed, y_a, y_b, z, z)


def _final_ln(out32, h32, tgt, ln_g, ln_b):
    rows = out32.shape[0]

    def body(o_ref, h_ref, t_ref, g_ref, b_ref, du_ref, dub_ref, st_ref):
        i = pl.program_id(0)
        g = g_ref[...]
        y, xhat, rstd = _ln_rows(ALPHA * h_ref[...] + o_ref[...], g, b_ref[...])
        e = jnp.where(i > 0, y - t_ref[0], 0.0)
        dy = e * (1.0 / D)
        du = _ln_rows_bwd(dy, g, xhat, rstd)
        du_ref[...] = du
        dub_ref[...] = du.astype(bf16)
        st = jnp.concatenate([_colsum(dy * xhat), _colsum(dy), _colsum(du), _colsum(e * e) * (0.5 / D),
                              jnp.zeros((4, D), f32)], axis=0)

        @pl.when(i == 0)
        def _():
            st_ref[...] = st

        @pl.when(i > 0)
        def _():
            st_ref[...] += st

    row = pl.BlockSpec((BLK, D), lambda i: (i, 0))
    vec = pl.BlockSpec((1, D), lambda i: (0, 0))
    return pl.pallas_call(
        body, grid=(rows // BLK,),
        in_specs=[row, row, pl.BlockSpec((1, BLK, D), lambda i: (0, jnp.maximum(i - 1, 0), 0)), vec, vec],
        out_specs=[row, row, pl.BlockSpec((8, D), lambda i: (0, 0))],
        out_shape=[SDS((rows, D), f32), SDS((rows, D), bf16), SDS((8, D), f32)],
        name="final_ln", compiler_params=_cp(("arbitrary",)),
    )(out32, h32, tgt, ln_g, ln_b)


def _step_rnn(h32, hb, z, wrg, smallw, p, zero):
    rows = z.shape[0]
    cos128, sin128 = _rope_tables(rows)
    cos128 = cos128 + zero
    xc, hr, ya, ya_t = _rnn_fwd(z, smallw, p["conv_b"] + zero, wrg, p["b_ra"], p["b_ri"], p["lru_lambda"])
    q_r, k_r, v_b = _qkv_prep(z, cos128, sin128)
    return dict(cos128=cos128, sin128=sin128, h32=h32, hb=hb, z=z, xc=xc, hr=hr, ya=ya, ya_t=ya_t,
                q_r=q_r, k_r=k_r, v_b=v_b)


def _step_attn(s, p, zero):
    sinks = p["sinks"].reshape(N_KV * GROUP) + zero[0]
    o32, yb, yb_t, lse = _attn_fwd(s["q_r"], s["k_r"], s["v_b"], s["z"], sinks)
    return dict(s, sinks=sinks, o32=o32, yb=yb, yb_t=yb_t, lse=lse)


def _step_merge(s, tgt, w3, p):
    ya, yb, z = s["ya"], s["yb"], s["z"]
    y_a = _mm(ya, w3, sel=0, out_dtype=bf16, name="mm_ya")
    y_b = _mm(yb, w3, sel=1, out_dtype=bf16, name="mm_yb")
    mixed, mixed_t = _mix_fwd(y_a, y_b, z)
    out32 = _mm(mixed, w3, sel=2, bias=p["b_o"], name="mm_out")
    du32, dub, st_out = _final_ln(out32, s["h32"], tgt, p["ln_g"], p["ln_b"])

    g_wo = _mm(mixed_t, dub, out_dtype=bf16, name="mm_dwo")
    dmixed = _mm(dub, w3, sel=2, nt=True, out_dtype=bf16, name="mm_dmixed")
    dya_b, dyb_b, dma, dmb = _mix_bwd(dmixed, y_a, y_b, z)
    g_wrnn = _mm(s["ya_t"], dya_b, out_dtype=bf16, name="mm_dwrnn")
    g_wattn = _mm(s["yb_t"], dyb_b, out_dtype=bf16, name="mm_dwattn")
    dya = _mm(dya_b, w3, sel=0, nt=True, name="mm_dya")
    dyb = _mm(dyb_b, w3, sel=1, nt=True, name="mm_dyb")
    return dict(du32=du32, st_out=st_out, dma=dma, dmb=dmb, dya=dya, dyb=dyb, g_wo=g_wo, g_wrnn=g_wrnn,
                g_wattn=g_wattn)


def _step_backward(s, t, wrg, smallw, p, conv_b):
    z = s["z"]
    dxr, dgr, g_wrg, vec_rnn = _rnn_bwd(t["dya"], s["hr"], s["xc"], z, smallw, conv_b, wrg, p["b_ra"], p["b_ri"],
                                        p["lru_lambda"])
    dq_r, dga, dk, dv, dkm, dvm, dsr = _attn_bwd(t["dyb"], s["o32"], s["lse"], s["q_r"], s["k_r"], s["v_b"], z,
                                                 s["sinks"])
    dq, dkv = _qkv_finish(dq_r, dk, dv, dkm, dvm, s["cos128"], s["sin128"])
    dz_parts = [(dxr, D), (dgr, D), (dq, D), (dkv, 512), (dga, D), (t["dma"], D), (t["dmb"], D)]
    return dict(vec_rnn=vec_rnn, dsr=dsr, g_wrg=g_wrg, dz_parts=dz_parts)


def _step_input_grad(dh_lo, dh_hi, du32, x, smallw, p, after):
    grad_x, dmeta, st_emb = _ln_emb_bwd(dh_lo, dh_hi, du32, x, smallw, p["ln_emb_g"], after)
    return dict(grad_x=grad_x, dmeta=dmeta, st_emb=st_emb)


_ANY = pl.BlockSpec(memory_space=pl.ANY)
_VMEM = pl.BlockSpec(memory_space=pltpu.VMEM)
_HBM = pl.BlockSpec(memory_space=pltpu.HBM)
_SEM = pl.BlockSpec(memory_space=pltpu.SEMAPHORE)


def _place():
    x, y, c = lax.axis_index("x"), lax.axis_index("y"), lax.axis_index("c")
    return x, y, c


def _dev(px, py, pc):
    return 4 * px + 2 * py + pc


def _tile_rows(r):
    return max(t for t in range(16, 321, 16) if r % t == 0) if r > 320 else r


def _cast_w_in(w_in_t, me_idx):
    tm = _tile_rows(SHARD_IN)

    def body(me_ref, i_ref, o_ref):
        o_ref[...] = i_ref[...].astype(bf16)

    return pl.pallas_call(
        body,
        grid_spec=pltpu.PrefetchScalarGridSpec(
            num_scalar_prefetch=1, grid=(SHARD_IN // tm,),
            in_specs=[pl.BlockSpec((tm, D), lambda i, me_ref: (i, 0))],
            out_specs=pl.BlockSpec((None, tm, D), lambda i, me_ref: (me_ref[0], i, 0))),
        out_shape=SDS((N_DEV, SHARD_IN, D), bf16), name="cast_w_in", compiler_params=_cp(("arbitrary",)),
    )(me_idx, w_in_t)


def _cast_small(me_idx, w_rnn_out, w_attn_out, w_o, w_ra, w_ri, meta, conv_w):
    def body(me_ref, a_ref, b_ref, c_ref, ra_ref, ri_ref, m_ref, cw_ref, w3_ref, wrg_ref, sw_ref):
        w3_ref[0] = a_ref[0].astype(bf16)
        w3_ref[1] = b_ref[0].astype(bf16)
        w3_ref[2] = c_ref[0].astype(bf16)
        wrg_ref[0] = ra_ref[0].astype(bf16)
        wrg_ref[1] = ri_ref[0].astype(bf16)
        sw_ref[...] = jnp.concatenate([m_ref[...], cw_ref[0], jnp.zeros((4, 256), f32)], axis=0)

    args = (w_rnn_out, w_attn_out, w_o, w_ra, w_ri, meta, conv_w)
    whole = lambda shape: pl.BlockSpec(shape, lambda i, me_ref: (0,) * len(shape))
    slot = lambda shape: pl.BlockSpec((None, *shape), lambda i, me_ref: (me_ref[0], *([0] * len(shape))))
    shapes = [(3, 256, D), (2, N_RNN_BLOCKS, 32, RNN_BLOCK), (24, 256)]
    return pl.pallas_call(
        body,
        grid_spec=pltpu.PrefetchScalarGridSpec(
            num_scalar_prefetch=1, grid=(1,), in_specs=[whole(a.shape) for a in args],
            out_specs=[slot(sh) for sh in shapes]),
        out_shape=[SDS((N_DEV, *sh), dt) for sh, dt in zip(shapes, (bf16, bf16, f32))],
        name="cast_small", compiler_params=_cp(("arbitrary",)),
    )(me_idx, *args)


def _remote(src, dst, send_sems, recv_sems, k, to):
    return pltpu.make_async_remote_copy(src_ref=src, dst_ref=dst, send_sem=send_sems.at[k], recv_sem=recv_sems.at[k],
                                        device_id=to, device_id_type=MESH)


def _w_in_rows(core, early):
    if early:
        return (1 - core) * W_IN_LATE, W_IN_EARLY
    return core * W_IN_EARLY, W_IN_LATE


def _all_gather(bufs, chunks):
    n = len(bufs)
    base = [0]
    for ch in chunks:
        base.append(base[-1] + 7 * ch)

    def body(*refs):
        outs = refs[n:2 * n]
        send_sems, recv_sems = refs[2 * n:]
        x, y, c = _place()
        me, sibling = (x, y, c), (x, y, 1 - c)
        chips = [(1 - x, y), (x, 1 - y), (1 - x, 1 - y)]

        def copy(a, i, k, block, to):
            blk = outs[a].at[_dev(*block)]
            if a == 0:
                r0, r = _w_in_rows(block[2], True)
                r = r // chunks[a]
                blk = blk.at[pl.ds(pl.multiple_of(r0 + i * r, 32), r)]
            return _remote(blk, blk, send_sems, recv_sems, base[a] + 7 * i + k, to)

        pieces = [(a, i) for a in range(n) for i in range(chunks[a])]
        first = []
        for a, i in pieces:
            first.append(copy(a, i, 0, me, sibling))
            first += [copy(a, i, 1 + j, me, (*chip, c)) for j, chip in enumerate(chips)]
        for cp in first:
            cp.start()
        passed = []
        for a, i in pieces:
            for j, chip in enumerate(chips):
                copy(a, i, 1 + j, (*chip, c), me).wait_recv()
                cp = copy(a, i, 4 + j, (*chip, c), sibling)
                cp.start()
                passed.append(cp)
        for a, i in pieces:
            copy(a, i, 0, sibling, me).wait_recv()
            for j, chip in enumerate(chips):
                copy(a, i, 4 + j, (*chip, 1 - c), me).wait_recv()
        for cp in first + passed:
            cp.wait_send()

    return pl.pallas_call(
        body, in_specs=[_ANY] * n, out_specs=[_ANY] * n,
        out_shape=[SDS(b.shape, b.dtype) for b in bufs],
        input_output_aliases={a: a for a in range(n)},
        scratch_shapes=[pltpu.SemaphoreType.DMA((base[-1],)), pltpu.SemaphoreType.DMA((base[-1],))],
        name="all_gather_weights",
    )(*bufs)


def _late_rows(buf, block):
    r0, r = _w_in_rows(block[2], False)
    return buf.at[_dev(*block)].at[pl.ds(pl.multiple_of(r0, 64), r)]


def _copies_late_own(srcs, lands, send_sems, recv_sems):
    x, y, c = _place()
    blk = _late_rows(srcs[0], (x, y, c))
    peers = [(x, y, 1 - c), (1 - x, y, c), (x, 1 - y, c), (1 - x, 1 - y, c)]
    return [_remote(blk, blk, send_sems, recv_sems, k, to) for k, to in enumerate(peers)]


def _copies_late_pass(srcs, lands, send_sems, recv_sems):
    x, y, c = _place()
    out = []
    for j, chip in enumerate([(1 - x, y), (x, 1 - y), (1 - x, 1 - y)]):
        blk = _late_rows(srcs[0], (*chip, c))
        out.append(_remote(blk, blk, send_sems, recv_sems, j, (x, y, 1 - c)))
    return out


def _copies_own_slot(srcs, lands, send_sems, recv_sems):
    x, y, c = _place()
    out = []
    for a in range(len(srcs)):
        blk = srcs[a].at[_dev(x, y, c)]
        for k, (fx, fy, fc) in enumerate(_PEER_FLIPS):
            out.append(_remote(blk, blk, send_sems, recv_sems, 7 * a + k, ((x + fx) % 2, (y + fy) % 2, (c + fc) % 2)))
    return out


_PEER_FLIPS = [(f // 4, (f // 2) % 2, f % 2) for f in range(1, N_DEV)]


def _copies_direct(same_src):
    def make(srcs, lands, send_sems, recv_sems):
        x, y, c = _place()
        me = _dev(x, y, c)
        out = []
        for a in range(len(srcs)):
            for k, (fx, fy, fc) in enumerate(_PEER_FLIPS):
                peer = ((x + fx) % 2, (y + fy) % 2, (c + fc) % 2)
                src = srcs[a] if same_src else srcs[a].at[_dev(*peer)]
                out.append(_remote(src, lands[a].at[me], send_sems, recv_sems, 7 * a + k, peer))
        return out
    return make


def _copies_siblings(srcs, lands, send_sems, recv_sems):
    x, y, c = _place()
    return [_remote(srcs[a].at[2 * q + (1 - c)], lands[a].at[q], send_sems, recv_sems, 4 * a + q, (x, y, 1 - c))
            for a in range(len(srcs)) for q in range(4)]


def _copies_chips(srcs, lands, send_sems, recv_sems):
    x, y, c = _place()
    chips = [(1 - x, y), (x, 1 - y), (1 - x, 1 - y)]
    return [_remote(srcs[a].at[2 * qx + qy], lands[a].at[j], send_sems, recv_sems, 3 * a + j, (qx, qy, c))
            for a in range(len(srcs)) for j, (qx, qy) in enumerate(chips)]


def _split_start(make, per_array, srcs, lands, dep, name):
    n, tot = len(srcs), len(srcs) + len(lands)

    def body(*refs):
        send_sems, recv_sems, token = refs[tot + 1], refs[tot + 2], refs[-1]
        for cp in make(refs[:n], refs[n:tot], send_sems, recv_sems):
            cp.start()
        token[...] = jnp.zeros_like(token)

    hbm = lambda t: pltpu.with_memory_space_constraint(t, pltpu.HBM)
    res = pl.pallas_call(
        body, name=name,
        out_shape=(pltpu.SemaphoreType.DMA((per_array * n,)), pltpu.SemaphoreType.DMA((per_array * n,)),
                   *[pltpu.HBM(t.shape, t.dtype) for t in (*srcs, *lands)], SDS((8, 128), f32)),
        in_specs=[_HBM] * tot + [_ANY], out_specs=(_SEM, _SEM, *([_HBM] * tot), _VMEM),
        input_output_aliases={i: 2 + i for i in range(tot)},
        compiler_params=pltpu.CompilerParams(has_side_effects=pltpu.SideEffectType.DATAFLOW_SIDE_EFFECTING),
    )(*[hbm(t) for t in (*srcs, *lands)], dep)
    return res[0], res[1], list(res[2:2 + n]), list(res[2 + n:2 + tot]), res[-1]


def _split_wait(make, send_sems, recv_sems, srcs, lands, after, name):
    n, tot = len(srcs), len(srcs) + len(lands)

    def body(*refs):
        for cp in make(refs[:n], refs[n:tot], refs[tot], refs[tot + 1]):
            cp.wait_send()
            cp.wait_recv()

    res = pl.pallas_call(
        body, name=name,
        out_shape=tuple(pltpu.HBM(t.shape, t.dtype) for t in (*srcs, *lands)),
        in_specs=[_HBM] * tot + [_SEM, _SEM, _ANY], out_specs=tuple([_HBM] * tot),
        input_output_aliases={i: i for i in range(tot)},
        compiler_params=pltpu.CompilerParams(has_side_effects=pltpu.SideEffectType.DATAFLOW_SIDE_EFFECTING),
    )(*srcs, *lands, send_sems, recv_sems, after)
    return list(res[:n]), list(res[n:])


def _adamw_direct(g, land, me_idx, w, m, v, name):
    r, wd = w.shape
    tr = min(r, 256)

    def body(me_ref, *refs):
        g_ref, peers = refs[0], refs[1:N_DEV]
        w_ref, m_ref, v_ref, g_out, d_out, m_out, v_out = refs[N_DEV:]
        gs = g_ref[...].astype(f32)
        for p_ref in peers:
            gs = gs + p_ref[...].astype(f32)
        d, mn, vn = _adamw(w_ref[...], gs, m_ref[...], v_ref[...])
        g_out[...] = gs
        d_out[...] = d
        m_out[...] = mn
        v_out[...] = vn

    tile = pl.BlockSpec((tr, wd), lambda i, me_ref: (i, 0))
    slot = lambda k: pl.BlockSpec((None, tr, wd), lambda i, me_ref: ((me_ref[0] + k) % N_DEV, i, 0))
    return pl.pallas_call(
        body,
        grid_spec=pltpu.PrefetchScalarGridSpec(
            num_scalar_prefetch=1, grid=(r // tr,),
            in_specs=[slot(0)] + [slot(k) for k in range(1, N_DEV)] + [tile, tile, tile],
            out_specs=[tile] * 4),
        out_shape=[SDS((r, wd), f32)] * 4, name=name, compiler_params=_cp(("arbitrary",), 48),
    )(me_idx, g, *([land] * (N_DEV - 1)), w, m, v)


def _pair_sum(g, r1, c_idx, name):
    _, r, w = g.shape
    tr = _tile_rows(r)

    def body(c_ref, g_ref, r_ref, o_ref):
        o_ref[...] = (g_ref[...].astype(f32) + r_ref[...].astype(f32)).astype(bf16)

    return pl.pallas_call(
        body,
        grid_spec=pltpu.PrefetchScalarGridSpec(
            num_scalar_prefetch=1, grid=(4, r // tr),
            in_specs=[pl.BlockSpec((None, tr, w), lambda q, i, c_ref: (2 * q + c_ref[0], i, 0)),
                      pl.BlockSpec((None, tr, w), lambda q, i, c_ref: (q, i, 0))],
            out_specs=pl.BlockSpec((None, tr, w), lambda q, i, c_ref: (q, i, 0))),
        out_shape=SDS((4, r, w), bf16), name=name, compiler_params=_cp(("arbitrary", "arbitrary")),
    )(c_idx, g, r1)


def _adamw(w, g, m, v):
    m = ADAM_B1 * m + (1.0 - ADAM_B1) * g
    v = ADAM_B2 * v + (1.0 - ADAM_B2) * (g * g)
    m_hat = m / (1.0 - ADAM_B1 ** ADAM_STEP)
    v_hat = v / (1.0 - ADAM_B2 ** ADAM_STEP)
    delta = -ADAM_LR * (m_hat / (jnp.sqrt(v_hat) + ADAM_EPS) + ADAM_WD * w)
    return delta, m, v


def _adamw_big(part, r2, q_idx, w, m, v, name, row_off=0, cols=(0, 1), prev=None):
    r, wd = w.shape
    tr = _tile_rows(r)
    k, ncol = cols
    wp = wd // ncol

    def body(q_ref, p_ref, r_ref, w_ref, m_ref, v_ref, *rest):
        g_out, d_out, m_out, v_out = rest[-4:]
        g = p_ref[...].astype(f32)
        for j in range(3):
            g = g + r_ref[j].astype(f32)
        d, mn, vn = _adamw(w_ref[...], g, m_ref[...], v_ref[...])
        g_out[...] = g
        d_out[...] = d
        m_out[...] = mn
        v_out[...] = vn

    tile = pl.BlockSpec((tr, wp), lambda i, q_ref: (i, k))
    prev = list(prev) if prev is not None else []
    return pl.pallas_call(
        body,
        grid_spec=pltpu.PrefetchScalarGridSpec(
            num_scalar_prefetch=1, grid=(r // tr,),
            in_specs=[pl.BlockSpec((None, tr, wp), lambda i, q_ref: (q_ref[0], row_off + i, 0)),
                      pl.BlockSpec((3, tr, wp), lambda i, q_ref: (0, row_off + i, 0)), tile, tile, tile]
                     + [pl.BlockSpec(memory_space=pl.ANY)] * len(prev),
            out_specs=[tile] * 4),
        out_shape=[SDS((r, wd), f32)] * 4, name=name,
        input_output_aliases={6 + i: i for i in range(len(prev))},
        compiler_params=_cp(("arbitrary",), 48),
    )(q_idx, part, r2, w, m, v, *prev)


_SMALL_ROWS = 24


def _pack_early(vec_rnn, st_out, dsr, db_in):
    def body(vr_ref, so_ref, dsr_ref, db_ref, sm_ref, sm2_ref):
        sm_ref[...] = jnp.zeros_like(sm_ref)
        sm2_ref[...] = jnp.zeros_like(sm2_ref)
        sm_ref[2:3, :] = vr_ref[3:4, :]
        sm_ref[3:6, :] = vr_ref[0:3, :]
        sm_ref[6:7, :] = so_ref[2:3, :]
        sm_ref[7:9, :] = so_ref[0:2, :]
        sm_ref[10:11, :] = so_ref[3:4, :]
        for h in range(N_KV):
            sm_ref[9:10, h * GROUP:(h + 1) * GROUP] = _colsum(dsr_ref[h])
        for j in range(6):
            sm_ref[16 + j:17 + j, :] = db_ref[0:1, j * D:(j + 1) * D]
        sm_ref[22:23, 0:D_IN - 6 * D] = db_ref[0:1, 6 * D:D_IN]
        for s in range(N_DEV):
            sm2_ref[s, 0:CONV_WIDTH, :] = vr_ref[4:8, s * 256:(s + 1) * 256]

    return pl.pallas_call(
        body, out_shape=[SDS((_SMALL_ROWS, D), f32), SDS((N_DEV, 8, 256), f32)],
        name="pack_early", compiler_params=_cp(None),
    )(vec_rnn, st_out, dsr, db_in)


def _pack_late(st_emb, dmeta):
    def body(se_ref, dm_ref, sm_ref, sm2_ref):
        sm_ref[...] = se_ref[...]
        for s in range(N_DEV):
            sm2_ref[s] = dm_ref[:, s * 256:(s + 1) * 256]

    return pl.pallas_call(
        body, out_shape=[SDS((8, D), f32), SDS((N_DEV, N_META, 256), f32)],
        name="pack_late", compiler_params=_cp(None),
    )(st_emb, dmeta)


def _small_allreduce(sm, sm2):
    def body(sm_ref, sm2_ref, o_ref, o2_ref, buf, buf2, send_sems, recv_sems):
        x, y, c = _place()
        me = _dev(x, y, c)
        copies = []
        for f in range(1, N_DEV):
            fx, fy, fc = f // 4, (f // 2) % 2, f % 2
            peer = ((x + fx) % 2, (y + fy) % 2, (c + fc) % 2)
            for t, (src, dst) in enumerate(((sm_ref, buf), (sm2_ref, buf2))):
                k = 2 * (f - 1) + t
                copies.append(pltpu.make_async_remote_copy(
                    src_ref=src, dst_ref=dst.at[me], send_sem=send_sems.at[k], recv_sem=recv_sems.at[k],
                    device_id=peer, device_id_type=MESH))
        for cp in copies:
            cp.start()
        buf[me] = sm_ref[...]
        buf2[me] = sm2_ref[...]
        for cp in copies:
            cp.wait()
        acc, acc2 = buf[0], buf2[0]
        for e in range(1, N_DEV):
            acc, acc2 = acc + buf[e], acc2 + buf2[e]
        o_ref[...] = acc
        o2_ref[...] = acc2

    return pl.pallas_call(
        body, in_specs=[_VMEM, _VMEM], out_specs=[_VMEM, _VMEM],
        out_shape=[SDS(sm.shape, f32), SDS(sm2.shape, f32)],
        scratch_shapes=[pltpu.VMEM((N_DEV, *sm.shape), f32), pltpu.VMEM((N_DEV, *sm2.shape), f32),
                        pltpu.SemaphoreType.DMA((14,)), pltpu.SemaphoreType.DMA((14,))],
        name="small_allreduce",
    )(sm, sm2)


_SMALL_ROW_OF = {"ln_emb_g": 0, "ln_emb_b": 1, "conv_b": 2, "b_ra": 3, "b_ri": 4, "lru_lambda": 5, "b_o": 6,
                 "ln_g": 7, "ln_b": 8}
_SMALL_NAMES = ["ln_emb_g", "ln_emb_b", "conv_b", "b_ra", "b_ri", "lru_lambda", "b_o", "ln_g", "ln_b",
                "sinks", "b_in", "meta_tokens", "conv_w"]


def _small_update(me_idx, early, late, wmv):
    n_fixed = 7

    def in_order(me, own_ref, land_ref):
        acc = None
        for e in range(N_DEV):
            term = jnp.where(me == e, own_ref[...], land_ref[e])
            acc = term if acc is None else acc + term
        return acc

    def body(*refs):
        me_ref, own_ref, land_ref, cown_ref, cland_ref, late_ref, meta_ref = refs[:n_fixed]
        ins = refs[n_fixed:n_fixed + 3 * len(_SMALL_NAMES)]
        outs = refs[n_fixed + 3 * len(_SMALL_NAMES):]
        me = me_ref[0]
        sm = in_order(me, own_ref, land_ref)
        conv = in_order(me, cown_ref, cland_ref)

        def grad_of(name):
            if name in ("ln_emb_g", "ln_emb_b"):
                r = _SMALL_ROW_OF[name]
                return late_ref[r:r + 1, :]
            if name in _SMALL_ROW_OF:
                r = _SMALL_ROW_OF[name]
                return sm[r:r + 1, :]
            if name == "sinks":
                return sm[9:10, 0:N_KV * GROUP]
            if name == "b_in":
                return jnp.concatenate([sm[16 + j:17 + j, :] for j in range(7)], axis=1)[:, :D_IN]
            if name == "meta_tokens":
                return meta_ref[...]
            return conv[0:CONV_WIDTH, :]

        for i, name in enumerate(_SMALL_NAMES):
            w_ref, m_ref, v_ref = ins[3 * i:3 * i + 3]
            g = grad_of(name)
            d, mn, vn = _adamw(w_ref[...], g, m_ref[...], v_ref[...])
            outs[4 * i][...] = g
            outs[4 * i + 1][...] = d
            outs[4 * i + 2][...] = mn
            outs[4 * i + 3][...] = vn
        outs[-1][...] = jnp.broadcast_to(jnp.sum(sm[10:11, :], axis=1, keepdims=True), (8, 128))

    args, out_shape = [me_idx, *early, *late], []
    for name in _SMALL_NAMES:
        args += list(wmv[name])
        out_shape += [SDS(wmv[name][0].shape, f32)] * 4
    out_shape.append(SDS((8, 128), f32))
    res = pl.pallas_call(
        body, out_shape=out_shape, in_specs=[pl.BlockSpec(memory_space=pltpu.SMEM)] + [_VMEM] * (len(args) - 1),
        name="small_update", compiler_params=_cp(None))(*args)
    return {name: tuple(res[4 * i:4 * i + 4]) for i, name in enumerate(_SMALL_NAMES)}, res[-1][0, 0]


_WEIGHTS = ["meta_tokens", "ln_emb_g", "ln_emb_b", "w_in", "b_in", "conv_w", "conv_b", "w_ra", "b_ra", "w_ri",
            "b_ri", "lru_lambda", "sinks", "w_rnn_out", "w_attn_out", "w_o", "b_o", "ln_g", "ln_b"]
_SMALL_2D = {"meta_tokens": (N_META, 256), "conv_w": (CONV_WIDTH, 256), "b_in": (1, D_IN), "sinks": (1, N_KV * GROUP)}


def kernel(x, meta_tokens, ln_emb_g, ln_emb_b, w_in, b_in, conv_w, conv_b, w_ra, b_ra, w_ri, b_ri, lru_lambda, sinks, w_rnn_out, w_attn_out, w_o, b_o, ln_g, ln_b, loss_target, m_meta_tokens, m_ln_emb_g, m_ln_emb_b, m_w_in, m_b_in, m_conv_w, m_conv_b, m_w_ra, m_b_ra, m_w_ri, m_b_ri, m_lru_lambda, m_sinks, m_w_rnn_out, m_w_attn_out, m_w_o, m_b_o, m_ln_g, m_ln_b, v_meta_tokens, v_ln_emb_g, v_ln_emb_b, v_w_in, v_b_in, v_conv_w, v_conv_b, v_w_ra, v_b_ra, v_w_ri, v_b_ri, v_lru_lambda, v_sinks, v_w_rnn_out, v_w_attn_out, v_w_o, v_b_o, v_ln_g, v_ln_b):
    w = dict(meta_tokens=meta_tokens, ln_emb_g=ln_emb_g, ln_emb_b=ln_emb_b, w_in=w_in, b_in=b_in, conv_w=conv_w,
             conv_b=conv_b, w_ra=w_ra, b_ra=b_ra, w_ri=w_ri, b_ri=b_ri, lru_lambda=lru_lambda, sinks=sinks,
             w_rnn_out=w_rnn_out, w_attn_out=w_attn_out, w_o=w_o, b_o=b_o, ln_g=ln_g, ln_b=ln_b)
    m = dict(meta_tokens=m_meta_tokens, ln_emb_g=m_ln_emb_g, ln_emb_b=m_ln_emb_b, w_in=m_w_in, b_in=m_b_in,
             conv_w=m_conv_w, conv_b=m_conv_b, w_ra=m_w_ra, b_ra=m_b_ra, w_ri=m_w_ri, b_ri=m_b_ri,
             lru_lambda=m_lru_lambda, sinks=m_sinks, w_rnn_out=m_w_rnn_out, w_attn_out=m_w_attn_out, w_o=m_w_o,
             b_o=m_b_o, ln_g=m_ln_g, ln_b=m_ln_b)
    v = dict(meta_tokens=v_meta_tokens, ln_emb_g=v_ln_emb_g, ln_emb_b=v_ln_emb_b, w_in=v_w_in, b_in=v_b_in,
             conv_w=v_conv_w, conv_b=v_conv_b, w_ra=v_w_ra, b_ra=v_b_ra, w_ri=v_w_ri, b_ri=v_b_ri,
             lru_lambda=v_lru_lambda, sinks=v_sinks, w_rnn_out=v_w_rnn_out, w_attn_out=v_w_attn_out, w_o=v_w_o,
             b_o=v_b_o, ln_g=v_ln_g, ln_b=v_ln_b)
    px, py, pc = _place()
    as_idx = lambda t: jnp.reshape(t, (1,)).astype(jnp.int32)
    c_idx, q_idx, me_idx = as_idx(pc), as_idx(2 * px + py), as_idx(_dev(px, py, pc))

    w3_s, wrg_s, small_s = _cast_small(me_idx, w_rnn_out, w_attn_out, w_o, w_ra, w_ri, meta_tokens, conv_w)
    vec = lambda name: w[name].reshape(1, -1)
    p = {k: vec(k) for k in ("ln_emb_g", "ln_emb_b", "b_in", "conv_b", "b_ra", "b_ri", "lru_lambda", "sinks",
                             "b_o", "ln_g", "ln_b")}
    w_in_t = lambda a: jnp.swapaxes(a, 1, 2).reshape(SHARD_IN, D)
    wg, wrg, smallw = _all_gather([_cast_w_in(w_in_t(w_in), me_idx), wrg_s, small_s], [6, 1, 1])
    late = _split_start(_copies_late_own, 4, [wg], [], smallw, "gather_late_start")
    h32, hb = _ln_emb(x, smallw, p["ln_emb_g"], p["ln_emb_b"] + late[4][0:1, 0:1])
    z = _mm_z(hb, late[2][0].reshape(D_IN, D), p["b_in"], None, "mm_z_early")
    (wg,), _ = _split_wait(_copies_late_own, late[0], late[1], late[2], [], z, "gather_late_wait")
    passed = _split_start(_copies_late_pass, 3, [wg], [], smallw, "gather_pass_start")
    z = _mm_z(hb, passed[2][0].reshape(D_IN, D), p["b_in"] + passed[4][0:1, 0:1], c_idx, "mm_z_late_own", z)
    (wg,), _ = _split_wait(_copies_late_pass, passed[0], passed[1], passed[2], [], z, "gather_pass_wait")
    w3_pending = _split_start(_copies_own_slot, 7, [w3_s], [], wg, "gather_w3_start")
    w_full = wg.reshape(D_IN, D)

    zero = w3_pending[4][0:1, 0:1]
    z = _mm_z(hb, w_full, p["b_in"] + zero, 1 - c_idx, "mm_z_late_other", z)
    s = _step_attn(_step_rnn(h32, hb, z, wrg, smallw, p, zero), p, zero)
    w3 = _split_wait(_copies_own_slot, *w3_pending[:4], s["lse"], "gather_w3_wait")[0][0]
    t = _step_merge(s, loss_target, w3, p)

    big = {}
    two_d = lambda name: (w[name].shape[-2], w[name].shape[-1])
    proj = ("w_o", "w_rnn_out", "w_attn_out")
    g_proj = [t[k].reshape(N_DEV, 256, D) for k in ("g_wo", "g_wrnn", "g_wattn")]
    g_pending = _split_start(_copies_direct(False), 7, g_proj, [lax.empty((N_DEV, 256, D), bf16) for _ in proj],
                             p["b_o"], "reduce_proj_start")
    u = _step_backward(s, t, wrg, smallw, p, p["conv_b"] + g_pending[4][0:1, 0:1])

    def siblings_start(gs, dep, tag):
        return _split_start(_copies_siblings, 4, gs, [lax.empty((4, *g.shape[1:]), bf16) for g in gs], dep,
                            "reduce_siblings_start_" + tag)

    def chips_start(gs, r1, dep, tag):
        parts = [_pair_sum(g, r, c_idx, "pair_sum_%s%d" % (tag, i)) for i, (g, r) in enumerate(zip(gs, r1))]
        return _split_start(_copies_chips, 3, parts, [lax.empty((3, *q.shape[1:]), bf16) for q in parts], dep,
                            "reduce_chips_start_" + tag)

    g_a, dz, db_in = _mm_dwin_parts(s["hb"], u["dz_parts"])
    shards = lambda g: g.reshape(N_DEV, SHARD_IN, W_IN_HALF)
    sib_a = siblings_start([shards(g_a), u["g_wrg"].reshape(N_DEV, 2 * RNN_BLOCK, RNN_BLOCK)], db_in, "a")
    g_proj, g_land = _split_wait(_copies_direct(False), *g_pending[:4], sib_a[4], "reduce_proj_wait")
    for i, name in enumerate(proj):
        res = _adamw_direct(g_proj[i], g_land[i], me_idx, w[name].reshape(two_d(name)), m[name].reshape(two_d(name)),
                            v[name].reshape(two_d(name)), "adamw_" + name)
        big[name] = tuple(r.reshape(w[name].shape) for r in res)
    chp_a = chips_start(*_split_wait(_copies_siblings, *sib_a[:4], big["w_attn_out"][3], "reduce_siblings_wait_a"),
                        db_in, "a")
    g_b = _mm_dwin(s["hb"], dz, chp_a[4])
    sib_b = siblings_start([shards(g_b)], db_in, "b")
    sm_e = _pack_early(u["vec_rnn"], t["st_out"], u["dsr"], db_in)
    early = _split_start(_copies_direct(True), 7, list(sm_e),
                         [lax.empty((N_DEV, *a.shape), f32) for a in sm_e], sib_b[4], "small_early_start")
    dh_lo = _mm_dh(dz, w_full, early[4], 0)
    chp_b = chips_start(*_split_wait(_copies_siblings, *sib_b[:4], dh_lo, "reduce_siblings_wait_b"), db_in, "b")
    dh_hi = _mm_dh(dz, w_full, chp_b[4], 1)
    parts_a, r2_a = _split_wait(_copies_chips, *chp_a[:4], dh_hi, "reduce_chips_wait_a")
    w_in_res = _adamw_big(parts_a[0], r2_a[0], q_idx, w_in_t(w["w_in"]), w_in_t(m["w_in"]), w_in_t(v["w_in"]),
                          "adamw_w_in_a", cols=(0, 2))
    u.update(_step_input_grad(dh_lo, dh_hi, t["du32"], x, smallw, p, w_in_res[3]))
    sm_l, meta_l = _small_allreduce(*_pack_late(u["st_emb"], u["dmeta"]))
    (sm_own, conv_own), (sm_land, conv_land) = _split_wait(_copies_direct(True), *early[:4], sm_l, "small_early_wait")
    me = _dev(px, py, pc)
    mine = lambda a, axis: lax.dynamic_index_in_dim(a, me, axis, keepdims=False)
    two = lambda name, t: t.reshape(_SMALL_2D.get(name, (1, D)))
    small, loss = _small_update(me_idx, (sm_own, sm_land, mine(conv_own, 0), mine(conv_land, 1)),
                                (sm_l, mine(meta_l, 0)),
                                {k: (two(k, w[k]), two(k, m[k]), two(k, v[k])) for k in _SMALL_NAMES})

    parts_b, r2_b = _split_wait(_copies_chips, *chp_b[:4], small["b_in"][2], "reduce_chips_wait_b")
    res = _adamw_big(parts_b[0], r2_b[0], q_idx, w_in_t(w["w_in"]), w_in_t(m["w_in"]), w_in_t(v["w_in"]),
                     "adamw_w_in_b", cols=(1, 2), prev=w_in_res)
    big["w_in"] = tuple(jnp.swapaxes(r.reshape(1, SHARD_IN, D), 1, 2) for r in res)
    for i, name in enumerate(("w_ra", "w_ri")):
        sq = (RNN_BLOCK, RNN_BLOCK)
        res = _adamw_big(parts_a[1], r2_a[1], q_idx, w[name].reshape(sq), m[name].reshape(sq), v[name].reshape(sq),
                         "adamw_" + name, row_off=i)
        big[name] = tuple(r.reshape(w[name].shape) for r in res)
    res = dict(big)
    for k in _SMALL_NAMES:
        res[k] = tuple(t.reshape(w[k].shape) for t in small[k])

    outs = [loss, u["grad_x"]]
    for j in range(4):
        outs += [res[k][j] for k in _WEIGHTS]
    return tuple(outs)
```

```python
import jax
import jax.numpy as jnp
from jax import lax
from jax.experimental import pallas as pl
from jax.experimental.pallas import tpu as pltpu

f32, bf16 = jnp.float32, jnp.bfloat16
SDS = jax.ShapeDtypeStruct

N_DEV = 8
D = 2048
N_META = 16
BLK = 128
ROW0 = BLK - N_META
N_RNN_BLOCKS = 8
RNN_BLOCK = D // N_RNN_BLOCKS
CONV_WIDTH = 4
LRU_C = 8.0
HEAD_DIM = 64
N_KV = 4
GROUP = 8
HALF = HEAD_DIM // 2
ROPE_THETA = 10000.0
NEG_INF = -1e30
LN_EPS = 1e-5
ALPHA = 2.0 ** 0.25
D_IN = 12800
SHARD_IN = D_IN // N_DEV
W_IN_HALF = D // 2
W_IN_LATE = 640
W_IN_EARLY = SHARD_IN - W_IN_LATE
OFF_GR, OFF_Q, OFF_K, OFF_V, OFF_GA, OFF_G = 2048, 4096, 6144, 6400, 6656, 8704
ADAM_LR, ADAM_B1, ADAM_B2, ADAM_EPS, ADAM_WD, ADAM_STEP = 1e-3, 0.9, 0.999, 1e-8, 0.01, 10
VMEM_LIMIT_MB = 56
MESH = pl.DeviceIdType.MESH


def _cp(sem=None, vmem_mb=40):
    return pltpu.CompilerParams(dimension_semantics=sem, vmem_limit_bytes=vmem_mb * 2 ** 20)


def _row_chunk(m):
    best = 16
    for c in range(16, 641, 16):
        if m % c == 0:
            best = c
    return best


def _sigmoid(x):
    return 1.0 / (1.0 + jnp.exp(-x))


def _silu_and_grad(x):
    s = _sigmoid(x)
    return x * s, s * (1.0 + x * (1.0 - s))


def _log_sigmoid(x):
    return jnp.minimum(x, 0.0) - jnp.log1p(jnp.exp(-jnp.abs(x)))


def _ln_rows(v, g, b):
    mu = jnp.mean(v, axis=-1, keepdims=True)
    c = v - mu
    var = jnp.mean(c * c, axis=-1, keepdims=True)
    rstd = lax.rsqrt(var + LN_EPS)
    xhat = c * rstd
    return xhat * g + b, xhat, rstd


def _ln_rows_bwd(dy, g, xhat, rstd):
    dxh = dy * g
    m1 = jnp.mean(dxh, axis=-1, keepdims=True)
    m2 = jnp.mean(dxh * xhat, axis=-1, keepdims=True)
    return rstd * (dxh - m1 - xhat * m2)


def _colsum(v):
    return jnp.sum(v, axis=0, keepdims=True)


def _dot(a, b):
    return jnp.dot(a, b, preferred_element_type=f32)


def _dot_nt(a, b):
    return lax.dot_general(a, b, (((1,), (1,)), ((), ())), preferred_element_type=f32)


def _dot_tn(a, b):
    return lax.dot_general(a, b, (((0,), (0,)), ((), ())), preferred_element_type=f32)


def _meta_full(sw_ref):
    return jnp.concatenate([sw_ref[s, 0:N_META, :] for s in range(N_DEV)], axis=1)


def _ln_emb(x, smallw, g_e, b_e):
    seq = x.shape[1]
    rows = seq + BLK
    nb = rows // BLK

    def body(x_ref, sw_ref, g_ref, b_ref, h32_ref, hb_ref):
        i = pl.program_id(0)
        g, b = g_ref[...], b_ref[...]

        def emit(blk):
            h32_ref[...] = blk
            hb_ref[...] = blk.astype(bf16)

        @pl.when(i == 0)
        def _():
            hm = _ln_rows(_meta_full(sw_ref), g, b)[0]
            emit(jnp.concatenate([jnp.zeros((ROW0, D), f32), hm], axis=0))

        @pl.when(i > 0)
        def _():
            emit(_ln_rows(x_ref[0], g, b)[0])

    return pl.pallas_call(
        body, grid=(nb,),
        in_specs=[pl.BlockSpec((1, BLK, D), lambda i: (0, jnp.maximum(i - 1, 0), 0)),
                  pl.BlockSpec((N_DEV, 24, 256), lambda i: (0, 0, 0)),
                  pl.BlockSpec((1, D), lambda i: (0, 0)),
                  pl.BlockSpec((1, D), lambda i: (0, 0))],
        out_specs=[pl.BlockSpec((BLK, D), lambda i: (i, 0)),
                   pl.BlockSpec((BLK, D), lambda i: (i, 0))],
        out_shape=[SDS((rows, D), f32), SDS((rows, D), bf16)],
        name="ln_emb", compiler_params=_cp(("arbitrary",)),
    )(x, smallw, g_e, b_e)


def _ln_emb_bwd(dh_lo, dh_hi, du32, x, smallw, g_e, after):
    seq = x.shape[1]
    rows = seq + BLK
    nb = rows // BLK

    def body(dlo_ref, dhi_ref, du_ref, x_ref, sw_ref, g_ref, after_ref, gx_ref, dmeta_ref, st_ref):
        i = pl.program_id(0)
        g = g_ref[...]
        dht = jnp.concatenate([dlo_ref[...], dhi_ref[...]], axis=1) + ALPHA * du_ref[...]

        @pl.when(i == 0)
        def _():
            v = jnp.concatenate([jnp.zeros((ROW0, D), f32), _meta_full(sw_ref)], axis=0)
            valid = lax.broadcasted_iota(jnp.int32, (BLK, 1), 0) >= ROW0
            d = jnp.where(valid, dht, 0.0)
            _, xhat, rstd = _ln_rows(v, g, 0.0)
            dv = _ln_rows_bwd(d, g, xhat, rstd)
            dmeta_ref[...] = dv[ROW0:, :]
            st_ref[...] = jnp.concatenate([_colsum(d * xhat), _colsum(d), jnp.zeros((6, D), f32)], axis=0)

        @pl.when(i > 0)
        def _():
            _, xhat, rstd = _ln_rows(x_ref[0], g, 0.0)
            gx_ref[0] = _ln_rows_bwd(dht, g, xhat, rstd)
            st_ref[0:1, :] += _colsum(dht * xhat)
            st_ref[1:2, :] += _colsum(dht)

    return pl.pallas_call(
        body, grid=(nb,),
        in_specs=[pl.BlockSpec((BLK, W_IN_HALF), lambda i: (i, 0)),
                  pl.BlockSpec((BLK, W_IN_HALF), lambda i: (i, 0)),
                  pl.BlockSpec((BLK, D), lambda i: (i, 0)),
                  pl.BlockSpec((1, BLK, D), lambda i: (0, jnp.maximum(i - 1, 0), 0)),
                  pl.BlockSpec((N_DEV, 24, 256), lambda i: (0, 0, 0)),
                  pl.BlockSpec((1, D), lambda i: (0, 0)),
                  pl.BlockSpec(memory_space=pl.ANY)],
        out_specs=[pl.BlockSpec((1, BLK, D), lambda i: (0, jnp.maximum(i - 1, 0), 0)),
                   pl.BlockSpec((N_META, D), lambda i: (0, 0)),
                   pl.BlockSpec((8, D), lambda i: (0, 0))],
        out_shape=[SDS((1, seq, D), f32), SDS((N_META, D), f32), SDS((8, D), f32)],
        name="ln_emb_bwd", compiler_params=_cp(("arbitrary",)),
    )(dh_lo, dh_hi, du32, x, smallw, g_e, after)


def _mm(a, b, *, name, nt=False, sel=None, bias=None, out_dtype=f32, tn=512):
    m, k = a.shape
    cm = _row_chunk(m)
    stacked = sel is not None
    n = D if stacked else (b.shape[0] if nt else b.shape[1])
    am = m
    if stacked and nt:
        b_spec = pl.BlockSpec((tn // 256, None, 256, D), lambda j, i: (j, sel, 0, 0))
    elif stacked:
        b_spec = pl.BlockSpec((N_DEV, None, 256, tn), lambda j, i: (0, sel, 0, j))
    elif nt:
        b_spec = pl.BlockSpec((tn, k), lambda j, i: (j, 0))
    else:
        b_spec = pl.BlockSpec((k, tn), lambda j, i: (0, j))
    in_specs = [pl.BlockSpec((am, k), lambda j, i: (i, 0)), b_spec]
    args = [a, b]
    if bias is not None:
        in_specs.append(pl.BlockSpec((1, tn), lambda j, i: (0, j)))
        args.append(bias)

    def body(*refs):
        a_ref, b_ref, o_ref = refs[0], refs[1], refs[-1]
        bm = b_ref[...]
        if stacked:
            bm = bm.reshape((tn, D) if nt else (D, tn))
        for c in range(am // cm):
            acc = (_dot_nt if nt else _dot)(a_ref[c * cm:(c + 1) * cm, :], bm)
            if bias is not None:
                acc = acc + refs[2][...]
            o_ref[c * cm:(c + 1) * cm, :] = acc.astype(out_dtype)

    return pl.pallas_call(
        body, grid=(n // tn, m // am), in_specs=in_specs,
        out_specs=pl.BlockSpec((am, tn), lambda j, i: (i, j)),
        out_shape=SDS((m, n), out_dtype), name=name, compiler_params=_cp(("arbitrary", "arbitrary"), 48),
    )(*args)


def _mm_z(hb, w_t, bias, side, name, z_prev=None):
    rows, k = hb.shape
    tn = W_IN_LATE
    cm = _row_chunk(rows)
    per = 2 * SHARD_IN // tn
    if side is None:
        side, count = jnp.zeros((1,), jnp.int32), per - 2
        tile = lambda q, t, s_ref: per * q + 1 + t
    else:
        count = 1
        tile = lambda q, t, s_ref: per * q + (per - 1) * s_ref[0]

    def body(s_ref, a_ref, b_ref, bias_ref, *rest):
        o_ref = rest[-1]
        for c in range(rows // cm):
            o_ref[c * cm:(c + 1) * cm, :] = _dot_nt(a_ref[c * cm:(c + 1) * cm, :], b_ref[...]) + bias_ref[...]

    in_specs = [pl.BlockSpec((rows, k), lambda q, t, s_ref: (0, 0)),
                pl.BlockSpec((tn, k), lambda q, t, s_ref: (tile(q, t, s_ref), 0)),
                pl.BlockSpec((1, tn), lambda q, t, s_ref: (0, tile(q, t, s_ref)))]
    args = [side, hb, w_t, bias]
    if z_prev is not None:
        in_specs.append(pl.BlockSpec(memory_space=pl.ANY))
        args.append(z_prev)
    return pl.pallas_call(
        body,
        grid_spec=pltpu.PrefetchScalarGridSpec(
            num_scalar_prefetch=1, grid=(N_DEV // 2, count), in_specs=in_specs,
            out_specs=pl.BlockSpec((rows, tn), lambda q, t, s_ref: (0, tile(q, t, s_ref)))),
        out_shape=SDS((rows, D_IN), f32), name=name,
        input_output_aliases={} if z_prev is None else {4: 0},
        compiler_params=_cp(("arbitrary", "arbitrary"), 48),
    )(*args)


def _mm_dh(dz, w_t, after, half):
    rows = dz.shape[0]
    tn = 512
    nt = W_IN_HALF // tn
    cm = _row_chunk(rows) // 2

    def body(a_ref, w_ref, after_ref, o_ref):
        o_ref[...] = _dot(a_ref[...], w_ref[...])

    return pl.pallas_call(
        body, grid=(nt, rows // cm),
        in_specs=[pl.BlockSpec((cm, D_IN), lambda j, i: (i, 0)),
                  pl.BlockSpec((D_IN, tn), lambda j, i: (0, half * nt + j)),
                  pl.BlockSpec(memory_space=pl.ANY)],
        out_specs=pl.BlockSpec((cm, tn), lambda j, i: (i, j)),
        out_shape=SDS((rows, W_IN_HALF), f32), name="mm_dh_%d" % half,
        compiler_params=_cp(("arbitrary", "arbitrary"), 48),
    )(dz, w_t, after)


def _mm_dwin_parts(hb, parts):
    rows = hb.shape[0]
    tc = 512
    edges = [0]
    for _, w in parts:
        edges.append(edges[-1] + w // tc)

    def body(*refs):
        h_ref, (o_ref, dz_ref, db_ref) = refs[len(parts)], refs[len(parts) + 1:]
        j = pl.program_id(0)
        for p_ref, lo, hi in zip(refs, edges[:-1], edges[1:]):
            @pl.when((j >= lo) & (j < hi))
            def _():
                o_ref[...] = _dot_tn(p_ref[...], h_ref[...]).astype(bf16)
                dz_ref[...] = p_ref[...]

                def step(i, s):
                    blk = p_ref[pl.ds(pl.multiple_of(i * BLK, BLK), BLK), :].astype(f32)
                    return s + blk.reshape(BLK // 8, 8, tc).sum(axis=0)
                s = lax.fori_loop(0, rows // BLK, step, jnp.zeros((8, tc), f32))
                db_ref[...] = jnp.broadcast_to(_colsum(s), (8, tc))

    in_specs = [pl.BlockSpec((rows, tc), lambda j, lo=lo, hi=hi: (0, jnp.clip(j - lo, 0, hi - lo - 1)))
                for lo, hi in zip(edges[:-1], edges[1:])]
    return pl.pallas_call(
        body, grid=(D_IN // tc,),
        in_specs=in_specs + [pl.BlockSpec((rows, W_IN_HALF), lambda j: (0, 0))],
        out_specs=[pl.BlockSpec((tc, W_IN_HALF), lambda j: (j, 0)), pl.BlockSpec((rows, tc), lambda j: (0, j)),
                   pl.BlockSpec((8, tc), lambda j: (0, j))],
        out_shape=[SDS((D_IN, W_IN_HALF), bf16), SDS((rows, D_IN), bf16), SDS((8, D_IN), f32)],
        name="mm_dwin_0", compiler_params=_cp(("arbitrary",), VMEM_LIMIT_MB),
    )(*[a for a, _ in parts], hb)


def _mm_dwin(hb, dz, after):
    rows = dz.shape[0]
    tc = 640

    def body(dz_ref, h_ref, after_ref, o_ref):
        o_ref[...] = _dot_tn(dz_ref[...], h_ref[...]).astype(bf16)

    return pl.pallas_call(
        body, grid=(D_IN // tc,),
        in_specs=[pl.BlockSpec((rows, tc), lambda j: (0, j)),
                  pl.BlockSpec((rows, W_IN_HALF), lambda j: (0, 1)),
                  pl.BlockSpec(memory_space=pl.ANY)],
        out_specs=pl.BlockSpec((tc, W_IN_HALF), lambda j: (j, 0)),
        out_shape=SDS((D_IN, W_IN_HALF), bf16),
        name="mm_dwin_1", compiler_params=_cp(("arbitrary",), 48),
    )(dz, hb, after)


SCAN_ROWS = 32


def _scan8(a, b, reverse):
    idx = lax.broadcasted_iota(jnp.int32, a.shape, 0)
    for s in (1, 2, 4):
        sh = 8 - s if reverse else s
        a_sh, b_sh = pltpu.roll(a, sh, 0), pltpu.roll(b, sh, 0)
        m = (idx < 8 - s) if reverse else (idx >= s)
        b = jnp.where(m, a * b_sh + b, b)
        a = jnp.where(m, a * a_sh, a)
    return a, b


def _shift_rows(prev8, cur, k):
    ext = jnp.concatenate([prev8, cur], axis=0)
    return pltpu.roll(ext, k, 0)[8:, :]


def _gates(xc, w_ra, b_ra, w_ri, b_ri, ls):
    xb = xc.astype(bf16)
    r = _sigmoid(_dot(xb, w_ra) + b_ra)
    ig = _sigmoid(_dot(xb, w_ri) + b_ri)
    la = LRU_C * r * ls
    a = jnp.exp(la)
    mult = jnp.sqrt(jnp.tanh(-la) * (1.0 + a * a))
    return xb, r, ig, a, mult


_RNN_IN_SPECS = lambda rows: [
    pl.BlockSpec((1, 24, 256), lambda n: (n, 0, 0)),
    pl.BlockSpec((1, RNN_BLOCK), lambda n: (0, n)),
    pl.BlockSpec((N_DEV, 2, None, 32, RNN_BLOCK), lambda n: (0, 0, n, 0, 0)),
    pl.BlockSpec((1, RNN_BLOCK), lambda n: (0, n)),
    pl.BlockSpec((1, RNN_BLOCK), lambda n: (0, n)),
    pl.BlockSpec((1, RNN_BLOCK), lambda n: (0, n)),
]


def _rnn_fwd(z, smallw, conv_b, wrg, b_ra, b_ri, lam):
    rows = z.shape[0]
    nb = rows // BLK
    col = lambda off: pl.BlockSpec((rows, RNN_BLOCK), lambda n: (0, off // RNN_BLOCK + n))

    def body(xr_ref, gr_ref, sw_ref, cb_ref, w_ref, bra_ref, bri_ref, lam_ref, xc_ref, hr_ref, ya_ref, yat_ref, a_s):
        cw = sw_ref[0, N_META:24, :]
        cb = cb_ref[...]
        w_ra = w_ref[:, 0].reshape(RNN_BLOCK, RNN_BLOCK)
        w_ri = w_ref[:, 1].reshape(RNN_BLOCK, RNN_BLOCK)
        b_ra_v, b_ri_v = bra_ref[...], bri_ref[...]
        ls = _log_sigmoid(lam_ref[...])
        rid = lax.broadcasted_iota(jnp.int32, (BLK, 1), 0)

        def blk_step(i, carry):
            r0 = pl.multiple_of(i * BLK, BLK)
            grow = rid + r0
            valid = grow >= ROW0
            cur = jnp.where(valid, xr_ref[pl.ds(r0, BLK), :], 0.0)
            prev8 = xr_ref[pl.ds(pl.multiple_of(jnp.maximum(r0 - 8, 0), 8), 8), :] * (i > 0).astype(f32)
            xc = cb + cw[0:1] * cur
            for k in range(1, CONV_WIDTH):
                xc = xc + cw[k:k + 1] * _shift_rows(prev8, cur, k)
            xc_ref[pl.ds(r0, BLK), :] = xc
            _, _, ig, a, mult = _gates(xc, w_ra, b_ra_v, w_ri, b_ri_v, ls)
            mult = jnp.where(grow == ROW0, 1.0, mult)
            a_s[pl.ds(r0, BLK), :] = a
            hr_ref[pl.ds(r0, BLK), :] = jnp.where(valid, mult * ig * xc, 0.0)
            return carry

        lax.fori_loop(0, nb, blk_step, 0)

        def scan_step(j, carry):
            r0 = pl.multiple_of(j * SCAN_ROWS, SCAN_ROWS)
            tiles = [_scan8(a_s[pl.ds(r0 + 8 * k, 8), :], hr_ref[pl.ds(r0 + 8 * k, 8), :], False)
                     for k in range(SCAN_ROWS // 8)]
            for k, (a, b) in enumerate(tiles):
                h = b + a * carry
                hr_ref[pl.ds(r0 + 8 * k, 8), :] = h
                carry = jnp.broadcast_to(h[7:8, :], (8, RNN_BLOCK))
            return carry

        lax.fori_loop(0, rows // SCAN_ROWS, scan_step, jnp.zeros((8, RNN_BLOCK), f32))

        def gate_step(i, carry):
            r0 = pl.multiple_of(i * BLK, BLK)
            ya_ref[pl.ds(r0, BLK), :] = (hr_ref[pl.ds(r0, BLK), :]
                                         * _silu_and_grad(gr_ref[pl.ds(r0, BLK), :])[0]).astype(bf16)
            return carry

        lax.fori_loop(0, nb, gate_step, 0)
        yat_ref[...] = ya_ref[...].astype(f32).T.astype(bf16)

    return pl.pallas_call(
        body, grid=(N_RNN_BLOCKS,),
        in_specs=[col(0), col(OFF_GR)] + _RNN_IN_SPECS(rows),
        out_specs=[pl.BlockSpec((rows, RNN_BLOCK), lambda n: (0, n))] * 3
                  + [pl.BlockSpec((RNN_BLOCK, rows), lambda n: (n, 0))],
        out_shape=[SDS((rows, D), f32), SDS((rows, D), f32), SDS((rows, D), bf16), SDS((D, rows), bf16)],
        scratch_shapes=[pltpu.VMEM((rows, RNN_BLOCK), f32)],
        name="rnn_fwd", compiler_params=_cp(("arbitrary",)),
    )(z, z, smallw, conv_b, wrg, b_ra, b_ri, lam)


def _rnn_bwd(dya, hr, xc, z, smallw, conv_b, wrg, b_ra, b_ri, lam):
    rows = z.shape[0]
    nb = rows // BLK
    col = lambda off: pl.BlockSpec((rows, RNN_BLOCK), lambda n: (0, off // RNN_BLOCK + n))
    blk = pl.BlockSpec((rows, RNN_BLOCK), lambda n: (0, n))

    def body(dya_ref, hr_ref, xc_ref, xr_ref, gr_ref, sw_ref, cb_ref, w_ref, bra_ref, bri_ref, lam_ref,
             dxr_ref, dgr_ref, dw_ref, vec_ref, a_s, lam_s, dxc_s, r_s, ig_s, mult_s, dw_s):
        cw = sw_ref[0, N_META:24, :]
        w_ra = w_ref[:, 0].reshape(RNN_BLOCK, RNN_BLOCK)
        w_ri = w_ref[:, 1].reshape(RNN_BLOCK, RNN_BLOCK)
        b_ra_v, b_ri_v = bra_ref[...], bri_ref[...]
        lam_v = lam_ref[...]
        ls = _log_sigmoid(lam_v)
        rid = lax.broadcasted_iota(jnp.int32, (BLK, 1), 0)
        zrow = jnp.zeros((1, RNN_BLOCK), f32)

        def p1(i, carry):
            r0 = pl.multiple_of(i * BLK, BLK)
            sl = pl.ds(r0, BLK)
            _, r, ig, a, mult = _gates(xc_ref[sl, :], w_ra, b_ra_v, w_ri, b_ri_v, ls)
            a_s[sl, :] = a
            r_s[sl, :] = r
            ig_s[sl, :] = ig
            mult_s[sl, :] = mult
            sg, dsg = _silu_and_grad(gr_ref[sl, :])
            d = dya_ref[sl, :]
            lam_s[sl, :] = d * sg
            dgr_ref[sl, :] = (d * hr_ref[sl, :] * dsg).astype(bf16)
            return carry

        lax.fori_loop(0, nb, p1, 0)

        def p2(jj, carry):
            r0 = pl.multiple_of((rows // SCAN_ROWS - 1 - jj) * SCAN_ROWS, SCAN_ROWS)
            idx = lax.broadcasted_iota(jnp.int32, (8, RNN_BLOCK), 0)
            tiles = []
            for k in range(SCAN_ROWS // 8):
                sl = pl.ds(r0 + 8 * k, 8)
                a, g = a_s[sl, :], lam_s[sl, :]
                tiles.append((g, *_scan8(a, a * g, True)))
            for k in reversed(range(SCAN_ROWS // 8)):
                g, ca, cb_ = tiles[k]
                mu = cb_ + ca * carry
                lam_s[pl.ds(r0 + 8 * k, 8), :] = g + jnp.where(idx < 7, pltpu.roll(mu, 7, 0), carry)
                carry = jnp.broadcast_to(mu[0:1, :], (8, RNN_BLOCK))
            return carry

        lax.fori_loop(0, rows // SCAN_ROWS, p2, jnp.zeros((8, RNN_BLOCK), f32))

        dw_s[...] = jnp.zeros_like(dw_s)

        def p3(i, carry):
            d_bra, d_bri, d_ls = carry
            r0 = pl.multiple_of(i * BLK, BLK)
            sl = pl.ds(r0, BLK)
            grow = rid + r0
            valid = grow >= ROW0
            first = grow == ROW0
            xcv = xc_ref[sl, :]
            xb = xcv.astype(bf16)
            r, ig, a = r_s[sl, :], ig_s[sl, :], a_s[sl, :]
            mult = jnp.where(first, 1.0, mult_s[sl, :])
            lam_t = lam_s[sl, :]
            du = jnp.where(valid, lam_t, 0.0)
            hprev = _shift_rows(hr_ref[pl.ds(pl.multiple_of(jnp.maximum(r0 - 8, 0), 8), 8), :] * (i > 0).astype(f32), hr_ref[sl, :], 1)
            da = lam_t * hprev
            dmult = jnp.where(first, 0.0, du * ig * xcv)
            di = du * mult * xcv
            dxc = du * mult * ig
            ratio = jnp.where(valid & jnp.logical_not(first), a * a / mult, 0.0)
            dla = da * a - dmult * ratio
            dpr = (dla * (LRU_C * ls)) * r * (1.0 - r)
            dpi = di * ig * (1.0 - ig)
            dprb, dpib = dpr.astype(bf16), dpi.astype(bf16)
            dw_s[0] += _dot_tn(xb, dprb)
            dw_s[1] += _dot_tn(xb, dpib)
            dxc_s[sl, :] = dxc + _dot_nt(dprb, w_ra) + _dot_nt(dpib, w_ri)
            return d_bra + _colsum(dpr), d_bri + _colsum(dpi), d_ls + _colsum(dla * (LRU_C * r))

        d_bra, d_bri, d_ls = lax.fori_loop(0, nb, p3, (zrow, zrow, zrow))

        def p4(i, carry):
            d_cb, d_w0, d_w1, d_w2, d_w3 = carry
            r0 = pl.multiple_of(i * BLK, BLK)
            sl = pl.ds(r0, BLK)
            grow = rid + r0
            valid = grow >= ROW0
            dxc = dxc_s[sl, :]
            nxt = dxc_s[pl.ds(pl.multiple_of(jnp.minimum(r0 + BLK, rows - 8), 8), 8), :] * (i < nb - 1).astype(f32)
            ext = jnp.concatenate([dxc, nxt], axis=0)
            dxr = cw[0:1] * dxc
            for k in range(1, CONV_WIDTH):
                dxr = dxr + cw[k:k + 1] * pltpu.roll(ext, BLK + 8 - k, 0)[:BLK, :]
            dxr_ref[sl, :] = jnp.where(valid, dxr, 0.0).astype(bf16)
            cur = jnp.where(valid, xr_ref[sl, :], 0.0)
            prev8 = xr_ref[pl.ds(pl.multiple_of(jnp.maximum(r0 - 8, 0), 8), 8), :] * (i > 0).astype(f32)
            dws = [d_w0 + _colsum(dxc * cur)]
            for k, acc in ((1, d_w1), (2, d_w2), (3, d_w3)):
                dws.append(acc + _colsum(dxc * _shift_rows(prev8, cur, k)))
            return (d_cb + _colsum(dxc), *dws)

        d_cb, d_w0, d_w1, d_w2, d_w3 = lax.fori_loop(0, nb, p4, (zrow,) * 5)

        d_lam = d_ls * _sigmoid(-lam_v)
        vec_ref[...] = jnp.concatenate([d_bra, d_bri, d_lam, d_cb, d_w0, d_w1, d_w2, d_w3], axis=0)
        dw_ref[:, 0] = dw_s[0].astype(bf16).reshape(N_DEV, 32, RNN_BLOCK)
        dw_ref[:, 1] = dw_s[1].astype(bf16).reshape(N_DEV, 32, RNN_BLOCK)

    return pl.pallas_call(
        body, grid=(N_RNN_BLOCKS,),
        in_specs=[blk, blk, blk, col(0), col(OFF_GR)] + _RNN_IN_SPECS(rows),
        out_specs=[blk, blk,
                   pl.BlockSpec((N_DEV, 2, None, 32, RNN_BLOCK), lambda n: (0, 0, n, 0, 0)),
                   pl.BlockSpec((8, RNN_BLOCK), lambda n: (0, n))],
        out_shape=[SDS((rows, D), bf16), SDS((rows, D), bf16),
                   SDS((N_DEV, 2, N_RNN_BLOCKS, 32, RNN_BLOCK), bf16), SDS((8, D), f32)],
        scratch_shapes=[pltpu.VMEM((rows, RNN_BLOCK), f32)] * 6 + [pltpu.VMEM((2, RNN_BLOCK, RNN_BLOCK), f32)],
        name="rnn_bwd", compiler_params=_cp(("arbitrary",), 48),
    )(dya, hr, xc, z, z, smallw, conv_b, wrg, b_ra, b_ri, lam)


def _rope_tables(rows):
    half = jnp.arange(HALF, dtype=f32)
    inv = ROPE_THETA ** (-half / HALF)
    pos = (jnp.arange(rows) - ROW0).astype(f32)
    ang = pos[:, None] * inv[None, :]
    cos, sin = jnp.cos(ang), jnp.sin(ang)
    cos128 = jnp.concatenate([cos, cos, cos, cos], axis=1)
    sin128 = jnp.concatenate([-sin, sin, -sin, sin], axis=1)
    return cos128, sin128


def _rope128(x, cos128, sin128):
    lane = lax.broadcasted_iota(jnp.int32, x.shape, 1)
    swapped = jnp.where(lane % HEAD_DIM < HALF, pltpu.roll(x, 128 - HALF, 1), pltpu.roll(x, HALF, 1))
    return x * cos128 + swapped * sin128


def _qkv_prep(z, cos128, sin128):
    rows = z.shape[0]

    def body(q_ref, kv_ref, c_ref, s_ref, qo_ref, ko_ref, vo_ref):
        c, s = c_ref[...], s_ref[...]
        for g in range(D // 128):
            qo_ref[:, g * 128:(g + 1) * 128] = (_rope128(q_ref[:, g * 128:(g + 1) * 128], c, s)
                                                * (HEAD_DIM ** -0.5)).astype(bf16)
        for g in range(2):
            kr = _rope128(kv_ref[:, g * 128:(g + 1) * 128], c, s)
            for j in range(2):
                ko_ref[2 * g + j] = kr[:, j * HEAD_DIM:(j + 1) * HEAD_DIM].astype(bf16)
        for h in range(N_KV):
            vo_ref[h] = kv_ref[:, 256 + h * HEAD_DIM:256 + (h + 1) * HEAD_DIM].astype(bf16)

    return pl.pallas_call(
        body, grid=(rows // BLK,),
        in_specs=[pl.BlockSpec((BLK, D), lambda i: (i, OFF_Q // D)),
                  pl.BlockSpec((BLK, 512), lambda i: (i, OFF_K // 512)),
                  pl.BlockSpec((BLK, 128), lambda i: (i, 0)),
                  pl.BlockSpec((BLK, 128), lambda i: (i, 0))],
        out_specs=[pl.BlockSpec((BLK, D), lambda i: (i, 0)),
                   pl.BlockSpec((N_KV, BLK, HEAD_DIM), lambda i: (0, i, 0)),
                   pl.BlockSpec((N_KV, BLK, HEAD_DIM), lambda i: (0, i, 0))],
        out_shape=[SDS((rows, D), bf16), SDS((N_KV, rows, HEAD_DIM), bf16), SDS((N_KV, rows, HEAD_DIM), bf16)],
        name="qkv_prep", compiler_params=_cp(("arbitrary",)),
    )(z, z, cos128, sin128)


def _attn_mask(n):
    qi = n * BLK + lax.broadcasted_iota(jnp.int32, (BLK, 2 * BLK + N_META), 0)
    c = lax.broadcasted_iota(jnp.int32, (BLK, 2 * BLK + N_META), 1)
    jb = (n - 1) * BLK + c
    band = (jb >= BLK) & (jb <= qi) & (qi - jb < BLK)
    meta = (ROW0 + c - 2 * BLK) <= qi
    return ((c < 2 * BLK) & band) | ((c >= 2 * BLK) & meta)


N_KEYS = 2 * BLK + N_META


def _stack_heads(t):
    return jnp.concatenate([t[:, g * HEAD_DIM:(g + 1) * HEAD_DIM] for g in range(GROUP)], axis=0)


def _sink_column(sink_ref, h):
    g = lax.broadcasted_iota(jnp.int32, (GROUP, 1, 1), 0)
    col = jnp.zeros((GROUP, 1, 1), f32)
    for j in range(GROUP):
        col = jnp.where(g == j, sink_ref[h * GROUP + j], col)
    return col


def _kv_specs(last):
    cl = lambda n: jnp.minimum(n, last)
    return [pl.BlockSpec((None, N_META, HEAD_DIM), lambda h, n: (h, ROW0 // N_META, 0)),
            pl.BlockSpec((None, BLK, HEAD_DIM), lambda h, n: (h, jnp.maximum(cl(n) - 1, 0), 0)),
            pl.BlockSpec((None, BLK, HEAD_DIM), lambda h, n: (h, cl(n), 0))]


def _attn_fwd(q_r, k_r, v_b, z, sinks):
    rows = q_r.shape[0]
    nb = rows // BLK

    def body(sink_ref, q_ref, km_ref, kp_ref, kc_ref, vm_ref, vp_ref, vc_ref, ga_ref, o_ref, yb_ref, ybt_ref, lse_ref):
        h, n = pl.program_id(0), pl.program_id(1)
        kk = jnp.concatenate([kp_ref[...], kc_ref[...], km_ref[...]], axis=0)
        vv = jnp.concatenate([vp_ref[...], vc_ref[...], vm_ref[...]], axis=0)
        q2 = _stack_heads(q_ref[...])
        s = jnp.where(_attn_mask(n)[None], _dot_nt(q2, kk).reshape(GROUP, BLK, N_KEYS), NEG_INF)
        sink = _sink_column(sink_ref, h)
        m = jnp.maximum(jnp.max(s, axis=-1, keepdims=True), sink)
        p = jnp.exp(s - m)
        den = jnp.sum(p, axis=-1, keepdims=True) + jnp.exp(sink - m)
        o2 = _dot((p / den).astype(bf16).reshape(GROUP * BLK, N_KEYS), vv)
        lse = m + jnp.log(den)
        for g in range(GROUP):
            o_ref[:, g * HEAD_DIM:(g + 1) * HEAD_DIM] = o2[g * BLK:(g + 1) * BLK]
            lse_ref[:, g:g + 1] = lse[g]
        yb = o_ref[...] * _silu_and_grad(ga_ref[...])[0]
        yb_ref[...] = yb.astype(bf16)
        ybt_ref[...] = yb.T.astype(bf16)

    tile = pl.BlockSpec((BLK, 512), lambda h, n: (n, h))
    return pl.pallas_call(
        body, grid=(N_KV, nb),
        in_specs=[pl.BlockSpec(memory_space=pltpu.SMEM), tile] + _kv_specs(nb - 1) + _kv_specs(nb - 1)
                 + [pl.BlockSpec((BLK, 512), lambda h, n: (n, OFF_GA // 512 + h))],
        out_specs=[tile, tile, pl.BlockSpec((512, BLK), lambda h, n: (h, n)),
                   pl.BlockSpec((None, BLK, GROUP), lambda h, n: (h, n, 0))],
        out_shape=[SDS((rows, D), f32), SDS((rows, D), bf16), SDS((D, rows), bf16),
                   SDS((N_KV, rows, GROUP), f32)],
        name="attn_fwd", compiler_params=_cp(("arbitrary", "arbitrary")),
    )(sinks, q_r, k_r, k_r, k_r, v_b, v_b, v_b, z)


def _attn_bwd(dyb, o32, lse, q_r, k_r, v_b, z, sinks):
    rows = q_r.shape[0]
    nb = rows // BLK
    cl = lambda n: jnp.minimum(n, nb - 1)

    def body(sink_ref, dyb_ref, o_ref, lse_ref, q_ref, km_ref, kp_ref, kc_ref, vm_ref, vp_ref, vc_ref, ga_ref,
             dq_ref, dga_ref, dk_ref, dv_ref, dkm_ref, dvm_ref, dsr_ref, ck_s, cv_s):
        h, n = pl.program_id(0), pl.program_id(1)

        @pl.when(n == 0)
        def _():
            dkm_ref[...] = jnp.zeros_like(dkm_ref)
            dvm_ref[...] = jnp.zeros_like(dvm_ref)
            ck_s[...] = jnp.zeros_like(ck_s)
            cv_s[...] = jnp.zeros_like(cv_s)

        @pl.when(n < nb)
        def _():
            kk = jnp.concatenate([kp_ref[...], kc_ref[...], km_ref[...]], axis=0)
            vv = jnp.concatenate([vp_ref[...], vc_ref[...], vm_ref[...]], axis=0)
            sg, dsg = _silu_and_grad(ga_ref[...])
            dyb_v = dyb_ref[...]
            o_v = o_ref[...]
            dga_ref[...] = (dyb_v * o_v * dsg).astype(bf16)
            q2 = _stack_heads(q_ref[...])
            do2 = _stack_heads(dyb_v * sg)
            lse_v = lse_ref[...]
            lse = jnp.concatenate([lse_v[:, g:g + 1] for g in range(GROUP)], axis=0).reshape(GROUP, BLK, 1)
            delta = jnp.sum(do2 * _stack_heads(o_v), axis=-1, keepdims=True).reshape(GROUP, BLK, 1)
            s = jnp.where(_attn_mask(n)[None], _dot_nt(q2, kk).reshape(GROUP, BLK, N_KEYS), NEG_INF)
            p = jnp.exp(s - lse)
            do2b = do2.astype(bf16)
            ds = (p * (_dot_nt(do2b, vv).reshape(GROUP, BLK, N_KEYS) - delta)).astype(bf16)
            ds = ds.reshape(GROUP * BLK, N_KEYS)
            dsr = -jnp.exp(_sink_column(sink_ref, h) - lse) * delta
            dq2 = _dot(ds, kk)
            for g in range(GROUP):
                dq_ref[:, g * HEAD_DIM:(g + 1) * HEAD_DIM] = dq2[g * BLK:(g + 1) * BLK]
                dsr_ref[:, g:g + 1] = dsr[g]
            dkk = _dot_tn(ds, q2)
            dvv = _dot_tn(p.astype(bf16).reshape(GROUP * BLK, N_KEYS), do2b)
            dk_ref[...] = ck_s[...] + dkk[:BLK]
            dv_ref[...] = cv_s[...] + dvv[:BLK]
            ck_s[...] = dkk[BLK:2 * BLK]
            cv_s[...] = dvv[BLK:2 * BLK]
            dkm_ref[...] += dkk[2 * BLK:]
            dvm_ref[...] += dvv[2 * BLK:]

        @pl.when(n == nb)
        def _():
            dk_ref[...] = ck_s[...]
            dv_ref[...] = cv_s[...]

    tile = pl.BlockSpec((BLK, 512), lambda h, n: (cl(n), h))
    kvout = pl.BlockSpec((None, BLK, HEAD_DIM), lambda h, n: (h, jnp.maximum(n - 1, 0), 0))
    mout = pl.BlockSpec((None, N_META, HEAD_DIM), lambda h, n: (h, 0, 0))
    stat = pl.BlockSpec((None, BLK, GROUP), lambda h, n: (h, cl(n), 0))
    return pl.pallas_call(
        body, grid=(N_KV, nb + 1),
        in_specs=[pl.BlockSpec(memory_space=pltpu.SMEM), tile, tile, stat, tile] + _kv_specs(nb - 1)
                 + _kv_specs(nb - 1) + [pl.BlockSpec((BLK, 512), lambda h, n: (cl(n), OFF_GA // 512 + h))],
        out_specs=[tile, tile, kvout, kvout, mout, mout, stat],
        out_shape=[SDS((rows, D), f32), SDS((rows, D), bf16),
                   SDS((N_KV, rows, HEAD_DIM), f32), SDS((N_KV, rows, HEAD_DIM), f32),
                   SDS((N_KV, N_META, HEAD_DIM), f32), SDS((N_KV, N_META, HEAD_DIM), f32),
                   SDS((N_KV, rows, GROUP), f32)],
        scratch_shapes=[pltpu.VMEM((BLK, HEAD_DIM), f32), pltpu.VMEM((BLK, HEAD_DIM), f32)],
        name="attn_bwd", compiler_params=_cp(("arbitrary", "arbitrary")),
    )(sinks, dyb, o32, lse, q_r, k_r, k_r, k_r, v_b, v_b, v_b, z)


def _qkv_finish(dq, dk, dv, dkm, dvm, cos128, sin128):
    rows = dq.shape[0]

    def body(dq_ref, dk_ref, dv_ref, dkm_ref, dvm_ref, c_ref, s_ref, oq_ref, okv_ref):
        first = (pl.program_id(0) == 0).astype(f32)
        c, s = c_ref[...], -s_ref[...]
        for g in range(D // 128):
            oq_ref[:, g * 128:(g + 1) * 128] = (_rope128(dq_ref[:, g * 128:(g + 1) * 128], c, s)
                                                * (HEAD_DIM ** -0.5)).astype(bf16)
        pad = jnp.zeros((ROW0, HEAD_DIM), f32)
        ks = [dk_ref[h] + first * jnp.concatenate([pad, dkm_ref[h]], axis=0) for h in range(N_KV)]
        vs = [dv_ref[h] + first * jnp.concatenate([pad, dvm_ref[h]], axis=0) for h in range(N_KV)]
        for g in range(2):
            kp = jnp.concatenate([ks[2 * g], ks[2 * g + 1]], axis=1)
            okv_ref[:, g * 128:(g + 1) * 128] = _rope128(kp, c, s).astype(bf16)
            okv_ref[:, 256 + g * 128:256 + (g + 1) * 128] = jnp.concatenate([vs[2 * g], vs[2 * g + 1]], axis=1).astype(bf16)

    kv = pl.BlockSpec((N_KV, BLK, HEAD_DIM), lambda i: (0, i, 0))
    mt = pl.BlockSpec((N_KV, N_META, HEAD_DIM), lambda i: (0, 0, 0))
    return pl.pallas_call(
        body, grid=(rows // BLK,),
        in_specs=[pl.BlockSpec((BLK, D), lambda i: (i, 0)), kv, kv, mt, mt,
                  pl.BlockSpec((BLK, 128), lambda i: (i, 0)), pl.BlockSpec((BLK, 128), lambda i: (i, 0))],
        out_specs=[pl.BlockSpec((BLK, D), lambda i: (i, 0)), pl.BlockSpec((BLK, 512), lambda i: (i, 0))],
        out_shape=[SDS((rows, D), bf16), SDS((rows, 512), bf16)],
        name="qkv_finish", compiler_params=_cp(("arbitrary",)),
    )(dq, dk, dv, dkm, dvm, cos128, sin128)


_TW = 512


def _mix_specs(rows):
    tr = _row_chunk(rows)
    tile = pl.BlockSpec((tr, _TW), lambda i, j: (i, j))
    ga = pl.BlockSpec((tr, _TW), lambda i, j: (i, OFF_G // _TW + j))
    gb = pl.BlockSpec((tr, _TW), lambda i, j: (i, (OFF_G + D) // _TW + j))
    return (rows // tr, D // _TW), tile, ga, gb


def _mix_fwd(y_a, y_b, z):
    rows = y_a.shape[0]
    tw = 256
    col = lambda off: pl.BlockSpec((rows, tw), lambda j: (0, off // tw + j))

    def body(ya_ref, yb_ref, ga_ref, gb_ref, o_ref, ot_ref):
        mixed = (_sigmoid(ga_ref[...]) * ya_ref[...].astype(f32)
                 + _sigmoid(gb_ref[...]) * yb_ref[...].astype(f32))
        o_ref[...] = mixed.astype(bf16)
        ot_ref[...] = mixed.T.astype(bf16)

    return pl.pallas_call(
        body, grid=(D // tw,), in_specs=[col(0), col(0), col(OFF_G), col(OFF_G + D)],
        out_specs=[col(0), pl.BlockSpec((tw, rows), lambda j: (j, 0))],
        out_shape=[SDS((rows, D), bf16), SDS((D, rows), bf16)],
        name="mix_fwd", compiler_params=_cp(("arbitrary",)),
    )(y_a, y_b, z, z)


def _mix_bwd(dmixed, y_a, y_b, z):
    rows = y_a.shape[0]
    grid, _mix_tile, _mix_ga, _mix_gb = _mix_specs(rows)

    def body(dm_ref, ya_ref, yb_ref, ga_ref, gb_ref, dya_ref, dyb_ref, dga_ref, dgb_ref):
        dm = dm_ref[...].astype(f32)
        sa, sb = _sigmoid(ga_ref[...]), _sigmoid(gb_ref[...])
        dya_ref[...] = (dm * sa).astype(bf16)
        dyb_ref[...] = (dm * sb).astype(bf16)
        dga_ref[...] = (dm * ya_ref[...].astype(f32) * sa * (1.0 - sa)).astype(bf16)
        dgb_ref[...] = (dm * yb_ref[...].astype(f32) * sb * (1.0 - sb)).astype(bf16)

    return pl.pallas_call(
        body, grid=grid, in_specs=[_mix_tile, _mix_tile, _mix_tile, _mix_ga, _mix_gb],
        out_specs=[_mix_tile] * 4, out_shape=[SDS((rows, D), bf16)] * 4,
        name="mix_bwd", compiler_params=_cp(("arbitrary", "arbitrary")),
    )(dmixed, y_a, y_b, z, z)


def _final_ln(out32, h32, tgt, ln_g, ln_b):
    rows = out32.shape[0]

    def body(o_ref, h_ref, t_ref, g_ref, b_ref, du_ref, dub_ref, st_ref):
        i = pl.program_id(0)
        g = g_ref[...]
        y, xhat, rstd = _ln_rows(ALPHA * h_ref[...] + o_ref[...], g, b_ref[...])
        e = jnp.where(i > 0, y - t_ref[0], 0.0)
        dy = e * (1.0 / D)
        du = _ln_rows_bwd(dy, g, xhat, rstd)
        du_ref[...] = du
        dub_ref[...] = du.astype(bf16)
        st = jnp.concatenate([_colsum(dy * xhat), _colsum(dy), _colsum(du), _colsum(e * e) * (0.5 / D),
                              jnp.zeros((4, D), f32)], axis=0)

        @pl.when(i == 0)
        def _():
            st_ref[...] = st

        @pl.when(i > 0)
        def _():
            st_ref[...] += st

    row = pl.BlockSpec((BLK, D), lambda i: (i, 0))
    vec = pl.BlockSpec((1, D), lambda i: (0, 0))
    return pl.pallas_call(
        body, grid=(rows // BLK,),
        in_specs=[row, row, pl.BlockSpec((1, BLK, D), lambda i: (0, jnp.maximum(i - 1, 0), 0)), vec, vec],
        out_specs=[row, row, pl.BlockSpec((8, D), lambda i: (0, 0))],
        out_shape=[SDS((rows, D), f32), SDS((rows, D), bf16), SDS((8, D), f32)],
        name="final_ln", compiler_params=_cp(("arbitrary",)),
    )(out32, h32, tgt, ln_g, ln_b)


def _step_rnn(h32, hb, z, wrg, smallw, p, zero):
    rows = z.shape[0]
    cos128, sin128 = _rope_tables(rows)
    cos128 = cos128 + zero
    xc, hr, ya, ya_t = _rnn_fwd(z, smallw, p["conv_b"] + zero, wrg, p["b_ra"], p["b_ri"], p["lru_lambda"])
    q_r, k_r, v_b = _qkv_prep(z, cos128, sin128)
    return dict(cos128=cos128, sin128=sin128, h32=h32, hb=hb, z=z, xc=xc, hr=hr, ya=ya, ya_t=ya_t,
                q_r=q_r, k_r=k_r, v_b=v_b)


def _step_attn(s, p, zero):
    sinks = p["sinks"].reshape(N_KV * GROUP) + zero[0]
    o32, yb, yb_t, lse = _attn_fwd(s["q_r"], s["k_r"], s["v_b"], s["z"], sinks)
    return dict(s, sinks=sinks, o32=o32, yb=yb, yb_t=yb_t, lse=lse)


def _step_merge(s, tgt, w3, p):
    ya, yb, z = s["ya"], s["yb"], s["z"]
    y_a = _mm(ya, w3, sel=0, out_dtype=bf16, name="mm_ya")
    y_b = _mm(yb, w3, sel=1, out_dtype=bf16, name="mm_yb")
    mixed, mixed_t = _mix_fwd(y_a, y_b, z)
    out32 = _mm(mixed, w3, sel=2, bias=p["b_o"], name="mm_out")
    du32, dub, st_out = _final_ln(out32, s["h32"], tgt, p["ln_g"], p["ln_b"])

    g_wo = _mm(mixed_t, dub, out_dtype=bf16, name="mm_dwo")
    dmixed = _mm(dub, w3, sel=2, nt=True, out_dtype=bf16, name="mm_dmixed")
    dya_b, dyb_b, dma, dmb = _mix_bwd(dmixed, y_a, y_b, z)
    g_wrnn = _mm(s["ya_t"], dya_b, out_dtype=bf16, name="mm_dwrnn")
    g_wattn = _mm(s["yb_t"], dyb_b, out_dtype=bf16, name="mm_dwattn")
    dya = _mm(dya_b, w3, sel=0, nt=True, name="mm_dya")
    dyb = _mm(dyb_b, w3, sel=1, nt=True, name="mm_dyb")
    return dict(du32=du32, st_out=st_out, dma=dma, dmb=dmb, dya=dya, dyb=dyb, g_wo=g_wo, g_wrnn=g_wrnn,
                g_wattn=g_wattn)


def _step_backward(s, t, wrg, smallw, p, conv_b):
    z = s["z"]
    dxr, dgr, g_wrg, vec_rnn = _rnn_bwd(t["dya"], s["hr"], s["xc"], z, smallw, conv_b, wrg, p["b_ra"], p["b_ri"],
                                        p["lru_lambda"])
    dq_r, dga, dk, dv, dkm, dvm, dsr = _attn_bwd(t["dyb"], s["o32"], s["lse"], s["q_r"], s["k_r"], s["v_b"], z,
                                                 s["sinks"])
    dq, dkv = _qkv_finish(dq_r, dk, dv, dkm, dvm, s["cos128"], s["sin128"])
    dz_parts = [(dxr, D), (dgr, D), (dq, D), (dkv, 512), (dga, D), (t["dma"], D), (t["dmb"], D)]
    return dict(vec_rnn=vec_rnn, dsr=dsr, g_wrg=g_wrg, dz_parts=dz_parts)


def _step_input_grad(dh_lo, dh_hi, du32, x, smallw, p, after):
    grad_x, dmeta, st_emb = _ln_emb_bwd(dh_lo, dh_hi, du32, x, smallw, p["ln_emb_g"], after)
    return dict(grad_x=grad_x, dmeta=dmeta, st_emb=st_emb)


_ANY = pl.BlockSpec(memory_space=pl.ANY)
_VMEM = pl.BlockSpec(memory_space=pltpu.VMEM)
_HBM = pl.BlockSpec(memory_space=pltpu.HBM)
_SEM = pl.BlockSpec(memory_space=pltpu.SEMAPHORE)


def _place():
    x, y, c = lax.axis_index("x"), lax.axis_index("y"), lax.axis_index("c")
    return x, y, c


def _dev(px, py, pc):
    return 4 * px + 2 * py + pc


def _tile_rows(r):
    return max(t for t in range(16, 321, 16) if r % t == 0) if r > 320 else r


def _cast_w_in(w_in_t, me_idx):
    tm = _tile_rows(SHARD_IN)

    def body(me_ref, i_ref, o_ref):
        o_ref[...] = i_ref[...].astype(bf16)

    return pl.pallas_call(
        body,
        grid_spec=pltpu.PrefetchScalarGridSpec(
            num_scalar_prefetch=1, grid=(SHARD_IN // tm,),
            in_specs=[pl.BlockSpec((tm, D), lambda i, me_ref: (i, 0))],
            out_specs=pl.BlockSpec((None, tm, D), lambda i, me_ref: (me_ref[0], i, 0))),
        out_shape=SDS((N_DEV, SHARD_IN, D), bf16), name="cast_w_in", compiler_params=_cp(("arbitrary",)),
    )(me_idx, w_in_t)


def _cast_small(me_idx, w_rnn_out, w_attn_out, w_o, w_ra, w_ri, meta, conv_w):
    def body(me_ref, a_ref, b_ref, c_ref, ra_ref, ri_ref, m_ref, cw_ref, w3_ref, wrg_ref, sw_ref):
        w3_ref[0] = a_ref[0].astype(bf16)
        w3_ref[1] = b_ref[0].astype(bf16)
        w3_ref[2] = c_ref[0].astype(bf16)
        wrg_ref[0] = ra_ref[0].astype(bf16)
        wrg_ref[1] = ri_ref[0].astype(bf16)
        sw_ref[...] = jnp.concatenate([m_ref[...], cw_ref[0], jnp.zeros((4, 256), f32)], axis=0)

    args = (w_rnn_out, w_attn_out, w_o, w_ra, w_ri, meta, conv_w)
    whole = lambda shape: pl.BlockSpec(shape, lambda i, me_ref: (0,) * len(shape))
    slot = lambda shape: pl.BlockSpec((None, *shape), lambda i, me_ref: (me_ref[0], *([0] * len(shape))))
    shapes = [(3, 256, D), (2, N_RNN_BLOCKS, 32, RNN_BLOCK), (24, 256)]
    return pl.pallas_call(
        body,
        grid_spec=pltpu.PrefetchScalarGridSpec(
            num_scalar_prefetch=1, grid=(1,), in_specs=[whole(a.shape) for a in args],
            out_specs=[slot(sh) for sh in shapes]),
        out_shape=[SDS((N_DEV, *sh), dt) for sh, dt in zip(shapes, (bf16, bf16, f32))],
        name="cast_small", compiler_params=_cp(("arbitrary",)),
    )(me_idx, *args)


def _remote(src, dst, send_sems, recv_sems, k, to):
    return pltpu.make_async_remote_copy(src_ref=src, dst_ref=dst, send_sem=send_sems.at[k], recv_sem=recv_sems.at[k],
                                        device_id=to, device_id_type=MESH)


def _w_in_rows(core, early):
    if early:
        return (1 - core) * W_IN_LATE, W_IN_EARLY
    return core * W_IN_EARLY, W_IN_LATE


def _all_gather(bufs, chunks):
    n = len(bufs)
    base = [0]
    for ch in chunks:
        base.append(base[-1] + 7 * ch)

    def body(*refs):
        outs = refs[n:2 * n]
        send_sems, recv_sems = refs[2 * n:]
        x, y, c = _place()
        me, sibling = (x, y, c), (x, y, 1 - c)
        chips = [(1 - x, y), (x, 1 - y), (1 - x, 1 - y)]

        def copy(a, i, k, block, to):
            blk = outs[a].at[_dev(*block)]
            if a == 0:
                r0, r = _w_in_rows(block[2], True)
                r = r // chunks[a]
                blk = blk.at[pl.ds(pl.multiple_of(r0 + i * r, 32), r)]
            return _remote(blk, blk, send_sems, recv_sems, base[a] + 7 * i + k, to)

        pieces = [(a, i) for a in range(n) for i in range(chunks[a])]
        first = []
        for a, i in pieces:
            first.append(copy(a, i, 0, me, sibling))
            first += [copy(a, i, 1 + j, me, (*chip, c)) for j, chip in enumerate(chips)]
        for cp in first:
            cp.start()
        passed = []
        for a, i in pieces:
            for j, chip in enumerate(chips):
                copy(a, i, 1 + j, (*chip, c), me).wait_recv()
                cp = copy(a, i, 4 + j, (*chip, c), sibling)
                cp.start()
                passed.append(cp)
        for a, i in pieces:
            copy(a, i, 0, sibling, me).wait_recv()
            for j, chip in enumerate(chips):
                copy(a, i, 4 + j, (*chip, 1 - c), me).wait_recv()
        for cp in first + passed:
            cp.wait_send()

    return pl.pallas_call(
        body, in_specs=[_ANY] * n, out_specs=[_ANY] * n,
        out_shape=[SDS(b.shape, b.dtype) for b in bufs],
        input_output_aliases={a: a for a in range(n)},
        scratch_shapes=[pltpu.SemaphoreType.DMA((base[-1],)), pltpu.SemaphoreType.DMA((base[-1],))],
        name="all_gather_weights",
    )(*bufs)


def _late_rows(buf, block):
    r0, r = _w_in_rows(block[2], False)
    return buf.at[_dev(*block)].at[pl.ds(pl.multiple_of(r0, 64), r)]


def _whole_block(buf, block):
    return buf.at[_dev(*block)]


def _copies_own(part):
    def make(srcs, lands, send_sems, recv_sems):
        x, y, c = _place()
        peers = [(x, y, 1 - c), (1 - x, y, c), (x, 1 - y, c), (1 - x, 1 - y, c)]
        out = []
        for a in range(len(srcs)):
            blk = part(srcs[a], (x, y, c))
            out += [_remote(blk, blk, send_sems, recv_sems, 4 * a + k, to) for k, to in enumerate(peers)]
        return out
    return make


def _copies_pass(part):
    def make(srcs, lands, send_sems, recv_sems):
        x, y, c = _place()
        out = []
        for a in range(len(srcs)):
            for j, chip in enumerate([(1 - x, y), (x, 1 - y), (1 - x, 1 - y)]):
                blk = part(srcs[a], (*chip, c))
                out.append(_remote(blk, blk, send_sems, recv_sems, 3 * a + j, (x, y, 1 - c)))
        return out
    return make


_PEER_FLIPS = [(f // 4, (f // 2) % 2, f % 2) for f in range(1, N_DEV)]


def _copies_direct(same_src):
    def make(srcs, lands, send_sems, recv_sems):
        x, y, c = _place()
        me = _dev(x, y, c)
        out = []
        for a in range(len(srcs)):
            for k, (fx, fy, fc) in enumerate(_PEER_FLIPS):
                peer = ((x + fx) % 2, (y + fy) % 2, (c + fc) % 2)
                src = srcs[a] if same_src else srcs[a].at[_dev(*peer)]
                out.append(_remote(src, lands[a].at[me], send_sems, recv_sems, 7 * a + k, peer))
        return out
    return make


def _copies_siblings(srcs, lands, send_sems, recv_sems):
    x, y, c = _place()
    return [_remote(srcs[a].at[2 * q + (1 - c)], lands[a].at[q], send_sems, recv_sems, 4 * a + q, (x, y, 1 - c))
            for a in range(len(srcs)) for q in range(4)]


def _copies_chips(srcs, lands, send_sems, recv_sems):
    x, y, c = _place()
    chips = [(1 - x, y), (x, 1 - y), (1 - x, 1 - y)]
    return [_remote(srcs[a].at[2 * qx + qy], lands[a].at[j], send_sems, recv_sems, 3 * a + j, (qx, qy, c))
            for a in range(len(srcs)) for j, (qx, qy) in enumerate(chips)]


def _split_start(make, per_array, srcs, lands, dep, name):
    n, tot = len(srcs), len(srcs) + len(lands)

    def body(*refs):
        send_sems, recv_sems, token = refs[tot + 1], refs[tot + 2], refs[-1]
        for cp in make(refs[:n], refs[n:tot], send_sems, recv_sems):
            cp.start()
        token[...] = jnp.zeros_like(token)

    hbm = lambda t: pltpu.with_memory_space_constraint(t, pltpu.HBM)
    res = pl.pallas_call(
        body, name=name,
        out_shape=(pltpu.SemaphoreType.DMA((per_array * n,)), pltpu.SemaphoreType.DMA((per_array * n,)),
                   *[pltpu.HBM(t.shape, t.dtype) for t in (*srcs, *lands)], SDS((8, 128), f32)),
        in_specs=[_HBM] * tot + [_ANY], out_specs=(_SEM, _SEM, *([_HBM] * tot), _VMEM),
        input_output_aliases={i: 2 + i for i in range(tot)},
        compiler_params=pltpu.CompilerParams(has_side_effects=pltpu.SideEffectType.DATAFLOW_SIDE_EFFECTING),
    )(*[hbm(t) for t in (*srcs, *lands)], dep)
    return res[0], res[1], list(res[2:2 + n]), list(res[2 + n:2 + tot]), res[-1]


def _split_wait(make, send_sems, recv_sems, srcs, lands, after, name):
    n, tot = len(srcs), len(srcs) + len(lands)

    def body(*refs):
        for cp in make(refs[:n], refs[n:tot], refs[tot], refs[tot + 1]):
            cp.wait_send()
            cp.wait_recv()

    res = pl.pallas_call(
        body, name=name,
        out_shape=tuple(pltpu.HBM(t.shape, t.dtype) for t in (*srcs, *lands)),
        in_specs=[_HBM] * tot + [_SEM, _SEM, _ANY], out_specs=tuple([_HBM] * tot),
        input_output_aliases={i: i for i in range(tot)},
        compiler_params=pltpu.CompilerParams(has_side_effects=pltpu.SideEffectType.DATAFLOW_SIDE_EFFECTING),
    )(*srcs, *lands, send_sems, recv_sems, after)
    return list(res[:n]), list(res[n:])


def _adamw_direct(g, land, me_idx, w, m, v, name):
    r, wd = w.shape
    tr = min(r, 256)

    def body(me_ref, *refs):
        g_ref, peers = refs[0], refs[1:N_DEV]
        w_ref, m_ref, v_ref, g_out, d_out, m_out, v_out = refs[N_DEV:]
        gs = g_ref[...].astype(f32)
        for p_ref in peers:
            gs = gs + p_ref[...].astype(f32)
        d, mn, vn = _adamw(w_ref[...], gs, m_ref[...], v_ref[...])
        g_out[...] = gs
        d_out[...] = d
        m_out[...] = mn
        v_out[...] = vn

    tile = pl.BlockSpec((tr, wd), lambda i, me_ref: (i, 0))
    slot = lambda k: pl.BlockSpec((None, tr, wd), lambda i, me_ref: ((me_ref[0] + k) % N_DEV, i, 0))
    return pl.pallas_call(
        body,
        grid_spec=pltpu.PrefetchScalarGridSpec(
            num_scalar_prefetch=1, grid=(r // tr,),
            in_specs=[slot(0)] + [slot(k) for k in range(1, N_DEV)] + [tile, tile, tile],
            out_specs=[tile] * 4),
        out_shape=[SDS((r, wd), f32)] * 4, name=name, compiler_params=_cp(("arbitrary",), 48),
    )(me_idx, g, *([land] * (N_DEV - 1)), w, m, v)


def _pair_sum(g, r1, c_idx, name):
    _, r, w = g.shape
    tr = _tile_rows(r)

    def body(c_ref, g_ref, r_ref, o_ref):
        o_ref[...] = (g_ref[...].astype(f32) + r_ref[...].astype(f32)).astype(bf16)

    return pl.pallas_call(
        body,
        grid_spec=pltpu.PrefetchScalarGridSpec(
            num_scalar_prefetch=1, grid=(4, r // tr),
            in_specs=[pl.BlockSpec((None, tr, w), lambda q, i, c_ref: (2 * q + c_ref[0], i, 0)),
                      pl.BlockSpec((None, tr, w), lambda q, i, c_ref: (q, i, 0))],
            out_specs=pl.BlockSpec((None, tr, w), lambda q, i, c_ref: (q, i, 0))),
        out_shape=SDS((4, r, w), bf16), name=name, compiler_params=_cp(("arbitrary", "arbitrary")),
    )(c_idx, g, r1)


def _adamw(w, g, m, v):
    m = ADAM_B1 * m + (1.0 - ADAM_B1) * g
    v = ADAM_B2 * v + (1.0 - ADAM_B2) * (g * g)
    m_hat = m / (1.0 - ADAM_B1 ** ADAM_STEP)
    v_hat = v / (1.0 - ADAM_B2 ** ADAM_STEP)
    delta = -ADAM_LR * (m_hat / (jnp.sqrt(v_hat) + ADAM_EPS) + ADAM_WD * w)
    return delta, m, v


def _adamw_big(part, r2, q_idx, w, m, v, name, row_off=0, cols=(0, 1), prev=None):
    r, wd = w.shape
    tr = _tile_rows(r)
    k, ncol = cols
    wp = wd // ncol

    def body(q_ref, p_ref, r_ref, w_ref, m_ref, v_ref, *rest):
        g_out, d_out, m_out, v_out = rest[-4:]
        g = p_ref[...].astype(f32)
        for j in range(3):
            g = g + r_ref[j].astype(f32)
        d, mn, vn = _adamw(w_ref[...], g, m_ref[...], v_ref[...])
        g_out[...] = g
        d_out[...] = d
        m_out[...] = mn
        v_out[...] = vn

    tile = pl.BlockSpec((tr, wp), lambda i, q_ref: (i, k))
    prev = list(prev) if prev is not None else []
    return pl.pallas_call(
        body,
        grid_spec=pltpu.PrefetchScalarGridSpec(
            num_scalar_prefetch=1, grid=(r // tr,),
            in_specs=[pl.BlockSpec((None, tr, wp), lambda i, q_ref: (q_ref[0], row_off + i, 0)),
                      pl.BlockSpec((3, tr, wp), lambda i, q_ref: (0, row_off + i, 0)), tile, tile, tile]
                     + [pl.BlockSpec(memory_space=pl.ANY)] * len(prev),
            out_specs=[tile] * 4),
        out_shape=[SDS((r, wd), f32)] * 4, name=name,
        input_output_aliases={6 + i: i for i in range(len(prev))},
        compiler_params=_cp(("arbitrary",), 48),
    )(q_idx, part, r2, w, m, v, *prev)


_SMALL_ROWS = 24


def _pack_early(vec_rnn, st_out, dsr, db_in):
    def body(vr_ref, so_ref, dsr_ref, db_ref, sm_ref, sm2_ref):
        sm_ref[...] = jnp.zeros_like(sm_ref)
        sm2_ref[...] = jnp.zeros_like(sm2_ref)
        sm_ref[2:3, :] = vr_ref[3:4, :]
        sm_ref[3:6, :] = vr_ref[0:3, :]
        sm_ref[6:7, :] = so_ref[2:3, :]
        sm_ref[7:9, :] = so_ref[0:2, :]
        sm_ref[10:11, :] = so_ref[3:4, :]
        for h in range(N_KV):
            sm_ref[9:10, h * GROUP:(h + 1) * GROUP] = _colsum(dsr_ref[h])
        for j in range(6):
            sm_ref[16 + j:17 + j, :] = db_ref[0:1, j * D:(j + 1) * D]
        sm_ref[22:23, 0:D_IN - 6 * D] = db_ref[0:1, 6 * D:D_IN]
        for s in range(N_DEV):
            sm2_ref[s, 0:CONV_WIDTH, :] = vr_ref[4:8, s * 256:(s + 1) * 256]

    return pl.pallas_call(
        body, out_shape=[SDS((_SMALL_ROWS, D), f32), SDS((N_DEV, 8, 256), f32)],
        name="pack_early", compiler_params=_cp(None),
    )(vec_rnn, st_out, dsr, db_in)


def _pack_late(st_emb, dmeta):
    def body(se_ref, dm_ref, sm_ref, sm2_ref):
        sm_ref[...] = se_ref[...]
        for s in range(N_DEV):
            sm2_ref[s] = dm_ref[:, s * 256:(s + 1) * 256]

    return pl.pallas_call(
        body, out_shape=[SDS((8, D), f32), SDS((N_DEV, N_META, 256), f32)],
        name="pack_late", compiler_params=_cp(None),
    )(st_emb, dmeta)


def _small_allreduce(sm, sm2):
    def body(sm_ref, sm2_ref, o_ref, o2_ref, buf, buf2, send_sems, recv_sems):
        x, y, c = _place()
        me = _dev(x, y, c)
        copies = []
        for f in range(1, N_DEV):
            fx, fy, fc = f // 4, (f // 2) % 2, f % 2
            peer = ((x + fx) % 2, (y + fy) % 2, (c + fc) % 2)
            for t, (src, dst) in enumerate(((sm_ref, buf), (sm2_ref, buf2))):
                k = 2 * (f - 1) + t
                copies.append(pltpu.make_async_remote_copy(
                    src_ref=src, dst_ref=dst.at[me], send_sem=send_sems.at[k], recv_sem=recv_sems.at[k],
                    device_id=peer, device_id_type=MESH))
        for cp in copies:
            cp.start()
        buf[me] = sm_ref[...]
        buf2[me] = sm2_ref[...]
        for cp in copies:
            cp.wait()
        acc, acc2 = buf[0], buf2[0]
        for e in range(1, N_DEV):
            acc, acc2 = acc + buf[e], acc2 + buf2[e]
        o_ref[...] = acc
        o2_ref[...] = acc2

    return pl.pallas_call(
        body, in_specs=[_VMEM, _VMEM], out_specs=[_VMEM, _VMEM],
        out_shape=[SDS(sm.shape, f32), SDS(sm2.shape, f32)],
        scratch_shapes=[pltpu.VMEM((N_DEV, *sm.shape), f32), pltpu.VMEM((N_DEV, *sm2.shape), f32),
                        pltpu.SemaphoreType.DMA((14,)), pltpu.SemaphoreType.DMA((14,))],
        name="small_allreduce",
    )(sm, sm2)


_SMALL_ROW_OF = {"ln_emb_g": 0, "ln_emb_b": 1, "conv_b": 2, "b_ra": 3, "b_ri": 4, "lru_lambda": 5, "b_o": 6,
                 "ln_g": 7, "ln_b": 8}
_SMALL_NAMES = ["ln_emb_g", "ln_emb_b", "conv_b", "b_ra", "b_ri", "lru_lambda", "b_o", "ln_g", "ln_b",
                "sinks", "b_in", "meta_tokens", "conv_w"]


def _small_update(me_idx, early, late, wmv):
    n_fixed = 7

    def in_order(me, own_ref, land_ref):
        acc = None
        for e in range(N_DEV):
            term = jnp.where(me == e, own_ref[...], land_ref[e])
            acc = term if acc is None else acc + term
        return acc

    def body(*refs):
        me_ref, own_ref, land_ref, cown_ref, cland_ref, late_ref, meta_ref = refs[:n_fixed]
        ins = refs[n_fixed:n_fixed + 3 * len(_SMALL_NAMES)]
        outs = refs[n_fixed + 3 * len(_SMALL_NAMES):]
        me = me_ref[0]
        sm = in_order(me, own_ref, land_ref)
        conv = in_order(me, cown_ref, cland_ref)

        def grad_of(name):
            if name in ("ln_emb_g", "ln_emb_b"):
                r = _SMALL_ROW_OF[name]
                return late_ref[r:r + 1, :]
            if name in _SMALL_ROW_OF:
                r = _SMALL_ROW_OF[name]
                return sm[r:r + 1, :]
            if name == "sinks":
                return sm[9:10, 0:N_KV * GROUP]
            if name == "b_in":
                return jnp.concatenate([sm[16 + j:17 + j, :] for j in range(7)], axis=1)[:, :D_IN]
            if name == "meta_tokens":
                return meta_ref[...]
            return conv[0:CONV_WIDTH, :]

        for i, name in enumerate(_SMALL_NAMES):
            w_ref, m_ref, v_ref = ins[3 * i:3 * i + 3]
            g = grad_of(name)
            d, mn, vn = _adamw(w_ref[...], g, m_ref[...], v_ref[...])
            outs[4 * i][...] = g
            outs[4 * i + 1][...] = d
            outs[4 * i + 2][...] = mn
            outs[4 * i + 3][...] = vn
        outs[-1][...] = jnp.broadcast_to(jnp.sum(sm[10:11, :], axis=1, keepdims=True), (8, 128))

    args, out_shape = [me_idx, *early, *late], []
    for name in _SMALL_NAMES:
        args += list(wmv[name])
        out_shape += [SDS(wmv[name][0].shape, f32)] * 4
    out_shape.append(SDS((8, 128), f32))
    res = pl.pallas_call(
        body, out_shape=out_shape, in_specs=[pl.BlockSpec(memory_space=pltpu.SMEM)] + [_VMEM] * (len(args) - 1),
        name="small_update", compiler_params=_cp(None))(*args)
    return {name: tuple(res[4 * i:4 * i + 4]) for i, name in enumerate(_SMALL_NAMES)}, res[-1][0, 0]


_WEIGHTS = ["meta_tokens", "ln_emb_g", "ln_emb_b", "w_in", "b_in", "conv_w", "conv_b", "w_ra", "b_ra", "w_ri",
            "b_ri", "lru_lambda", "sinks", "w_rnn_out", "w_attn_out", "w_o", "b_o", "ln_g", "ln_b"]
_SMALL_2D = {"meta_tokens": (N_META, 256), "conv_w": (CONV_WIDTH, 256), "b_in": (1, D_IN), "sinks": (1, N_KV * GROUP)}


def kernel(x, meta_tokens, ln_emb_g, ln_emb_b, w_in, b_in, conv_w, conv_b, w_ra, b_ra, w_ri, b_ri, lru_lambda, sinks, w_rnn_out, w_attn_out, w_o, b_o, ln_g, ln_b, loss_target, m_meta_tokens, m_ln_emb_g, m_ln_emb_b, m_w_in, m_b_in, m_conv_w, m_conv_b, m_w_ra, m_b_ra, m_w_ri, m_b_ri, m_lru_lambda, m_sinks, m_w_rnn_out, m_w_attn_out, m_w_o, m_b_o, m_ln_g, m_ln_b, v_meta_tokens, v_ln_emb_g, v_ln_emb_b, v_w_in, v_b_in, v_conv_w, v_conv_b, v_w_ra, v_b_ra, v_w_ri, v_b_ri, v_lru_lambda, v_sinks, v_w_rnn_out, v_w_attn_out, v_w_o, v_b_o, v_ln_g, v_ln_b):
    w = dict(meta_tokens=meta_tokens, ln_emb_g=ln_emb_g, ln_emb_b=ln_emb_b, w_in=w_in, b_in=b_in, conv_w=conv_w,
             conv_b=conv_b, w_ra=w_ra, b_ra=b_ra, w_ri=w_ri, b_ri=b_ri, lru_lambda=lru_lambda, sinks=sinks,
             w_rnn_out=w_rnn_out, w_attn_out=w_attn_out, w_o=w_o, b_o=b_o, ln_g=ln_g, ln_b=ln_b)
    m = dict(meta_tokens=m_meta_tokens, ln_emb_g=m_ln_emb_g, ln_emb_b=m_ln_emb_b, w_in=m_w_in, b_in=m_b_in,
             conv_w=m_conv_w, conv_b=m_conv_b, w_ra=m_w_ra, b_ra=m_b_ra, w_ri=m_w_ri, b_ri=m_b_ri,
             lru_lambda=m_lru_lambda, sinks=m_sinks, w_rnn_out=m_w_rnn_out, w_attn_out=m_w_attn_out, w_o=m_w_o,
             b_o=m_b_o, ln_g=m_ln_g, ln_b=m_ln_b)
    v = dict(meta_tokens=v_meta_tokens, ln_emb_g=v_ln_emb_g, ln_emb_b=v_ln_emb_b, w_in=v_w_in, b_in=v_b_in,
             conv_w=v_conv_w, conv_b=v_conv_b, w_ra=v_w_ra, b_ra=v_b_ra, w_ri=v_w_ri, b_ri=v_b_ri,
             lru_lambda=v_lru_lambda, sinks=v_sinks, w_rnn_out=v_w_rnn_out, w_attn_out=v_w_attn_out, w_o=v_w_o,
             b_o=v_b_o, ln_g=v_ln_g, ln_b=v_ln_b)
    px, py, pc = _place()
    as_idx = lambda t: jnp.reshape(t, (1,)).astype(jnp.int32)
    c_idx, q_idx, me_idx = as_idx(pc), as_idx(2 * px + py), as_idx(_dev(px, py, pc))

    w3_s, wrg_s, small_s = _cast_small(me_idx, w_rnn_out, w_attn_out, w_o, w_ra, w_ri, meta_tokens, conv_w)
    vec = lambda name: w[name].reshape(1, -1)
    p = {k: vec(k) for k in ("ln_emb_g", "ln_emb_b", "b_in", "conv_b", "b_ra", "b_ri", "lru_lambda", "sinks",
                             "b_o", "ln_g", "ln_b")}
    w_in_t = lambda a: jnp.swapaxes(a, 1, 2).reshape(SHARD_IN, D)
    wg, wrg, smallw = _all_gather([_cast_w_in(w_in_t(w_in), me_idx), wrg_s, small_s], [6, 1, 1])
    late = _split_start(_copies_own(_late_rows), 4, [wg], [], smallw, "gather_late_start")
    h32, hb = _ln_emb(x, smallw, p["ln_emb_g"], p["ln_emb_b"] + late[4][0:1, 0:1])
    z = _mm_z(hb, late[2][0].reshape(D_IN, D), p["b_in"], None, "mm_z_early")
    (wg,), _ = _split_wait(_copies_own(_late_rows), late[0], late[1], late[2], [], z, "gather_late_wait")
    passed = _split_start(_copies_pass(_late_rows), 3, [wg], [], smallw, "gather_pass_start")
    w3_own = _split_start(_copies_own(_whole_block), 4, [w3_s], [], passed[4], "gather_w3_start")
    zero = w3_own[4][0:1, 0:1]
    z = _mm_z(hb, passed[2][0].reshape(D_IN, D), p["b_in"] + zero, c_idx, "mm_z_late_own", z)
    (wg,), _ = _split_wait(_copies_pass(_late_rows), passed[0], passed[1], passed[2], [], z, "gather_pass_wait")
    w_full = wg.reshape(D_IN, D)

    z = _mm_z(hb, w_full, p["b_in"], 1 - c_idx, "mm_z_late_other", z)
    s = _step_rnn(h32, hb, z, wrg, smallw, p, zero)
    (w3,), _ = _split_wait(_copies_own(_whole_block), w3_own[0], w3_own[1], w3_own[2], [], s["q_r"], "gather_w3_wait")
    w3_pass = _split_start(_copies_pass(_whole_block), 3, [w3], [], smallw, "gather_w3_pass_start")
    s = _step_attn(s, p, w3_pass[4][0:1, 0:1])
    (w3,), _ = _split_wait(_copies_pass(_whole_block), w3_pass[0], w3_pass[1], w3_pass[2], [], s["lse"],
                           "gather_w3_pass_wait")
    t = _step_merge(s, loss_target, w3, p)

    big = {}
    two_d = lambda name: (w[name].shape[-2], w[name].shape[-1])
    proj = ("w_o", "w_rnn_out", "w_attn_out")
    g_proj = [t[k].reshape(N_DEV, 256, D) for k in ("g_wo", "g_wrnn", "g_wattn")]
    g_pending = _split_start(_copies_direct(False), 7, g_proj, [lax.empty((N_DEV, 256, D), bf16) for _ in proj],
                             p["b_o"], "reduce_proj_start")
    u = _step_backward(s, t, wrg, smallw, p, p["conv_b"] + g_pending[4][0:1, 0:1])

    def siblings_start(gs, dep, tag):
        return _split_start(_copies_siblings, 4, gs, [lax.empty((4, *g.shape[1:]), bf16) for g in gs], dep,
                            "reduce_siblings_start_" + tag)

    def chips_start(gs, r1, dep, tag):
        parts = [_pair_sum(g, r, c_idx, "pair_sum_%s%d" % (tag, i)) for i, (g, r) in enumerate(zip(gs, r1))]
        return _split_start(_copies_chips, 3, parts, [lax.empty((3, *q.shape[1:]), bf16) for q in parts], dep,
                            "reduce_chips_start_" + tag)

    g_a, dz, db_in = _mm_dwin_parts(s["hb"], u["dz_parts"])
    shards = lambda g: g.reshape(N_DEV, SHARD_IN, W_IN_HALF)
    sib_a = siblings_start([shards(g_a), u["g_wrg"].reshape(N_DEV, 2 * RNN_BLOCK, RNN_BLOCK)], db_in, "a")
    g_proj, g_land = _split_wait(_copies_direct(False), *g_pending[:4], sib_a[4], "reduce_proj_wait")
    for i, name in enumerate(proj):
        res = _adamw_direct(g_proj[i], g_land[i], me_idx, w[name].reshape(two_d(name)), m[name].reshape(two_d(name)),
                            v[name].reshape(two_d(name)), "adamw_" + name)
        big[name] = tuple(r.reshape(w[name].shape) for r in res)
    chp_a = chips_start(*_split_wait(_copies_siblings, *sib_a[:4], big["w_attn_out"][3], "reduce_siblings_wait_a"),
                        db_in, "a")
    g_b = _mm_dwin(s["hb"], dz, chp_a[4])
    sib_b = siblings_start([shards(g_b)], db_in, "b")
    sm_e = _pack_early(u["vec_rnn"], t["st_out"], u["dsr"], db_in)
    early = _split_start(_copies_direct(True), 7, list(sm_e),
                         [lax.empty((N_DEV, *a.shape), f32) for a in sm_e], sib_b[4], "small_early_start")
    dh_lo = _mm_dh(dz, w_full, early[4], 0)
    chp_b = chips_start(*_split_wait(_copies_siblings, *sib_b[:4], dh_lo, "reduce_siblings_wait_b"), db_in, "b")
    dh_hi = _mm_dh(dz, w_full, chp_b[4], 1)
    parts_a, r2_a = _split_wait(_copies_chips, *chp_a[:4], dh_hi, "reduce_chips_wait_a")
    w_in_res = _adamw_big(parts_a[0], r2_a[0], q_idx, w_in_t(w["w_in"]), w_in_t(m["w_in"]), w_in_t(v["w_in"]),
                          "adamw_w_in_a", cols=(0, 2))
    u.update(_step_input_grad(dh_lo, dh_hi, t["du32"], x, smallw, p, w_in_res[3]))
    sm_l, meta_l = _small_allreduce(*_pack_late(u["st_emb"], u["dmeta"]))
    (sm_own, conv_own), (sm_land, conv_land) = _split_wait(_copies_direct(True), *early[:4], sm_l, "small_early_wait")
    me = _dev(px, py, pc)
    mine = lambda a, axis: lax.dynamic_index_in_dim(a, me, axis, keepdims=False)
    two = lambda name, t: t.reshape(_SMALL_2D.get(name, (1, D)))
    small, loss = _small_update(me_idx, (sm_own, sm_land, mine(conv_own, 0), mine(conv_land, 1)),
                                (sm_l, mine(meta_l, 0)),
                                {k: (two(k, w[k]), two(k, m[k]), two(k, v[k])) for k in _SMALL_NAMES})

    parts_b, r2_b = _split_wait(_copies_chips, *chp_b[:4], small["b_in"][2], "reduce_chips_wait_b")
    res = _adamw_big(parts_b[0], r2_b[0], q_idx, w_in_t(w["w_in"]), w_in_t(m["w_in"]), w_in_t(v["w_in"]),
                     "adamw_w_in_b", cols=(1, 2), prev=w_in_res)
    big["w_in"] = tuple(jnp.swapaxes(r.reshape(1, SHARD_IN, D), 1, 2) for r in res)
    for i, name in enumerate(("w_ra", "w_ri")):
        sq = (RNN_BLOCK, RNN_BLOCK)
        res = _adamw_big(parts_a[1], r2_a[1], q_idx, w[name].reshape(sq), m[name].reshape(sq), v[name].reshape(sq),
                         "adamw_" + name, row_off=i)
        big[name] = tuple(r.reshape(w[name].shape) for r in res)
    res = dict(big)
    for k in _SMALL_NAMES:
        res[k] = tuple(t.reshape(w[k].shape) for t in small[k])

    outs = [loss, u["grad_x"]]
    for j in range(4):
        outs += [res[k][j] for k in _WEIGHTS]
    return tuple(outs)
```

```python
import jax
import jax.numpy as jnp
from jax import lax
from jax.experimental import pallas as pl
from jax.experimental.pallas import tpu as pltpu

f32, bf16 = jnp.float32, jnp.bfloat16
SDS = jax.ShapeDtypeStruct

N_DEV = 8
D = 2048
N_META = 16
BLK = 128
ROW0 = BLK - N_META
N_RNN_BLOCKS = 8
RNN_BLOCK = D // N_RNN_BLOCKS
CONV_WIDTH = 4
LRU_C = 8.0
HEAD_DIM = 64
N_KV = 4
GROUP = 8
HALF = HEAD_DIM // 2
ROPE_THETA = 10000.0
NEG_INF = -1e30
LN_EPS = 1e-5
ALPHA = 2.0 ** 0.25
D_IN = 12800
SHARD_IN = D_IN // N_DEV
W_IN_HALF = D // 2
W_IN_LATE = 640
W_IN_EARLY = SHARD_IN - W_IN_LATE
OFF_GR, OFF_Q, OFF_K, OFF_V, OFF_GA, OFF_G = 2048, 4096, 6144, 6400, 6656, 8704
ADAM_LR, ADAM_B1, ADAM_B2, ADAM_EPS, ADAM_WD, ADAM_STEP = 1e-3, 0.9, 0.999, 1e-8, 0.01, 10
VMEM_LIMIT_MB = 56
MESH = pl.DeviceIdType.MESH


def _cp(sem=None, vmem_mb=40):
    return pltpu.CompilerParams(dimension_semantics=sem, vmem_limit_bytes=vmem_mb * 2 ** 20)


def _row_chunk(m):
    best = 16
    for c in range(16, 641, 16):
        if m % c == 0:
            best = c
    return best


def _sigmoid(x):
    return 1.0 / (1.0 + jnp.exp(-x))


def _silu_and_grad(x):
    s = _sigmoid(x)
    return x * s, s * (1.0 + x * (1.0 - s))


def _log_sigmoid(x):
    return jnp.minimum(x, 0.0) - jnp.log1p(jnp.exp(-jnp.abs(x)))


def _ln_rows(v, g, b):
    mu = jnp.mean(v, axis=-1, keepdims=True)
    c = v - mu
    var = jnp.mean(c * c, axis=-1, keepdims=True)
    rstd = lax.rsqrt(var + LN_EPS)
    xhat = c * rstd
    return xhat * g + b, xhat, rstd


def _ln_rows_bwd(dy, g, xhat, rstd):
    dxh = dy * g
    m1 = jnp.mean(dxh, axis=-1, keepdims=True)
    m2 = jnp.mean(dxh * xhat, axis=-1, keepdims=True)
    return rstd * (dxh - m1 - xhat * m2)


def _colsum(v):
    return jnp.sum(v, axis=0, keepdims=True)


def _dot(a, b):
    return jnp.dot(a, b, preferred_element_type=f32)


def _dot_nt(a, b):
    return lax.dot_general(a, b, (((1,), (1,)), ((), ())), preferred_element_type=f32)


def _dot_tn(a, b):
    return lax.dot_general(a, b, (((0,), (0,)), ((), ())), preferred_element_type=f32)


def _meta_full(sw_ref):
    return jnp.concatenate([sw_ref[s, 0:N_META, :] for s in range(N_DEV)], axis=1)


def _ln_emb(x, smallw, g_e, b_e):
    seq = x.shape[1]
    rows = seq + BLK
    nb = rows // BLK

    def body(x_ref, sw_ref, g_ref, b_ref, h32_ref, hb_ref):
        i = pl.program_id(0)
        g, b = g_ref[...], b_ref[...]

        def emit(blk):
            h32_ref[...] = blk
            hb_ref[...] = blk.astype(bf16)

        @pl.when(i == 0)
        def _():
            hm = _ln_rows(_meta_full(sw_ref), g, b)[0]
            emit(jnp.concatenate([jnp.zeros((ROW0, D), f32), hm], axis=0))

        @pl.when(i > 0)
        def _():
            emit(_ln_rows(x_ref[0], g, b)[0])

    return pl.pallas_call(
        body, grid=(nb,),
        in_specs=[pl.BlockSpec((1, BLK, D), lambda i: (0, jnp.maximum(i - 1, 0), 0)),
                  pl.BlockSpec((N_DEV, 24, 256), lambda i: (0, 0, 0)),
                  pl.BlockSpec((1, D), lambda i: (0, 0)),
                  pl.BlockSpec((1, D), lambda i: (0, 0))],
        out_specs=[pl.BlockSpec((BLK, D), lambda i: (i, 0)),
                   pl.BlockSpec((BLK, D), lambda i: (i, 0))],
        out_shape=[SDS((rows, D), f32), SDS((rows, D), bf16)],
        name="ln_emb", compiler_params=_cp(("arbitrary",)),
    )(x, smallw, g_e, b_e)


def _ln_emb_bwd(dh_lo, dh_hi, du32, x, smallw, g_e, after):
    seq = x.shape[1]
    rows = seq + BLK
    nb = rows // BLK

    def body(dlo_ref, dhi_ref, du_ref, x_ref, sw_ref, g_ref, after_ref, gx_ref, dmeta_ref, st_ref):
        i = pl.program_id(0)
        g = g_ref[...]
        dht = jnp.concatenate([dlo_ref[...], dhi_ref[...]], axis=1) + ALPHA * du_ref[...]

        @pl.when(i == 0)
        def _():
            v = jnp.concatenate([jnp.zeros((ROW0, D), f32), _meta_full(sw_ref)], axis=0)
            valid = lax.broadcasted_iota(jnp.int32, (BLK, 1), 0) >= ROW0
            d = jnp.where(valid, dht, 0.0)
            _, xhat, rstd = _ln_rows(v, g, 0.0)
            dv = _ln_rows_bwd(d, g, xhat, rstd)
            dmeta_ref[...] = dv[ROW0:, :]
            st_ref[...] = jnp.concatenate([_colsum(d * xhat), _colsum(d), jnp.zeros((6, D), f32)], axis=0)

        @pl.when(i > 0)
        def _():
            _, xhat, rstd = _ln_rows(x_ref[0], g, 0.0)
            gx_ref[0] = _ln_rows_bwd(dht, g, xhat, rstd)
            st_ref[0:1, :] += _colsum(dht * xhat)
            st_ref[1:2, :] += _colsum(dht)

    return pl.pallas_call(
        body, grid=(nb,),
        in_specs=[pl.BlockSpec((BLK, W_IN_HALF), lambda i: (i, 0)),
                  pl.BlockSpec((BLK, W_IN_HALF), lambda i: (i, 0)),
                  pl.BlockSpec((BLK, D), lambda i: (i, 0)),
                  pl.BlockSpec((1, BLK, D), lambda i: (0, jnp.maximum(i - 1, 0), 0)),
                  pl.BlockSpec((N_DEV, 24, 256), lambda i: (0, 0, 0)),
                  pl.BlockSpec((1, D), lambda i: (0, 0)),
                  pl.BlockSpec(memory_space=pl.ANY)],
        out_specs=[pl.BlockSpec((1, BLK, D), lambda i: (0, jnp.maximum(i - 1, 0), 0)),
                   pl.BlockSpec((N_META, D), lambda i: (0, 0)),
                   pl.BlockSpec((8, D), lambda i: (0, 0))],
        out_shape=[SDS((1, seq, D), f32), SDS((N_META, D), f32), SDS((8, D), f32)],
        name="ln_emb_bwd", compiler_params=_cp(("arbitrary",)),
    )(dh_lo, dh_hi, du32, x, smallw, g_e, after)


def _mm(a, b, *, name, nt=False, sel=None, bias=None, out_dtype=f32, tn=512):
    m, k = a.shape
    cm = _row_chunk(m)
    stacked = sel is not None
    n = D if stacked else (b.shape[0] if nt else b.shape[1])
    am = m
    if stacked and nt:
        b_spec = pl.BlockSpec((tn // 256, None, 256, D), lambda j, i: (j, sel, 0, 0))
    elif stacked:
        b_spec = pl.BlockSpec((N_DEV, None, 256, tn), lambda j, i: (0, sel, 0, j))
    elif nt:
        b_spec = pl.BlockSpec((tn, k), lambda j, i: (j, 0))
    else:
        b_spec = pl.BlockSpec((k, tn), lambda j, i: (0, j))
    in_specs = [pl.BlockSpec((am, k), lambda j, i: (i, 0)), b_spec]
    args = [a, b]
    if bias is not None:
        in_specs.append(pl.BlockSpec((1, tn), lambda j, i: (0, j)))
        args.append(bias)

    def body(*refs):
        a_ref, b_ref, o_ref = refs[0], refs[1], refs[-1]
        bm = b_ref[...]
        if stacked:
            bm = bm.reshape((tn, D) if nt else (D, tn))
        for c in range(am // cm):
            acc = (_dot_nt if nt else _dot)(a_ref[c * cm:(c + 1) * cm, :], bm)
            if bias is not None:
                acc = acc + refs[2][...]
            o_ref[c * cm:(c + 1) * cm, :] = acc.astype(out_dtype)

    return pl.pallas_call(
        body, grid=(n // tn, m // am), in_specs=in_specs,
        out_specs=pl.BlockSpec((am, tn), lambda j, i: (i, j)),
        out_shape=SDS((m, n), out_dtype), name=name, compiler_params=_cp(("arbitrary", "arbitrary"), 48),
    )(*args)


def _mm_z(hb, w_t, bias, side, name, z_prev=None):
    rows, k = hb.shape
    tn = W_IN_LATE
    cm = _row_chunk(rows)
    per = 2 * SHARD_IN // tn
    if side is None:
        side, count = jnp.zeros((1,), jnp.int32), per - 2
        tile = lambda q, t, s_ref: per * q + 1 + t
    else:
        count = 1
        tile = lambda q, t, s_ref: per * q + (per - 1) * s_ref[0]

    def body(s_ref, a_ref, b_ref, bias_ref, *rest):
        o_ref = rest[-1]
        for c in range(rows // cm):
            o_ref[c * cm:(c + 1) * cm, :] = _dot_nt(a_ref[c * cm:(c + 1) * cm, :], b_ref[...]) + bias_ref[...]

    in_specs = [pl.BlockSpec((rows, k), lambda q, t, s_ref: (0, 0)),
                pl.BlockSpec((tn, k), lambda q, t, s_ref: (tile(q, t, s_ref), 0)),
                pl.BlockSpec((1, tn), lambda q, t, s_ref: (0, tile(q, t, s_ref)))]
    args = [side, hb, w_t, bias]
    if z_prev is not None:
        in_specs.append(pl.BlockSpec(memory_space=pl.ANY))
        args.append(z_prev)
    return pl.pallas_call(
        body,
        grid_spec=pltpu.PrefetchScalarGridSpec(
            num_scalar_prefetch=1, grid=(N_DEV // 2, count), in_specs=in_specs,
            out_specs=pl.BlockSpec((rows, tn), lambda q, t, s_ref: (0, tile(q, t, s_ref)))),
        out_shape=SDS((rows, D_IN), f32), name=name,
        input_output_aliases={} if z_prev is None else {4: 0},
        compiler_params=_cp(("arbitrary", "arbitrary"), 48),
    )(*args)


def _mm_dh(dz, w_t, after, half):
    rows = dz.shape[0]
    tn = 512
    nt = W_IN_HALF // tn
    cm = _row_chunk(rows) // 2

    def body(a_ref, w_ref, after_ref, o_ref):
        o_ref[...] = _dot(a_ref[...], w_ref[...])

    return pl.pallas_call(
        body, grid=(nt, rows // cm),
        in_specs=[pl.BlockSpec((cm, D_IN), lambda j, i: (i, 0)),
                  pl.BlockSpec((D_IN, tn), lambda j, i: (0, half * nt + j)),
                  pl.BlockSpec(memory_space=pl.ANY)],
        out_specs=pl.BlockSpec((cm, tn), lambda j, i: (i, j)),
        out_shape=SDS((rows, W_IN_HALF), f32), name="mm_dh_%d" % half,
        compiler_params=_cp(("arbitrary", "arbitrary"), 48),
    )(dz, w_t, after)


def _mm_dwin_parts(hb, parts):
    rows = hb.shape[0]
    tc = 512
    edges = [0]
    for _, w in parts:
        edges.append(edges[-1] + w // tc)

    def body(*refs):
        h_ref, (o_ref, dz_ref, db_ref) = refs[len(parts)], refs[len(parts) + 1:]
        j = pl.program_id(0)
        for p_ref, lo, hi in zip(refs, edges[:-1], edges[1:]):
            @pl.when((j >= lo) & (j < hi))
            def _():
                o_ref[...] = _dot_tn(p_ref[...], h_ref[...]).astype(bf16)
                dz_ref[...] = p_ref[...]

                def step(i, s):
                    blk = p_ref[pl.ds(pl.multiple_of(i * BLK, BLK), BLK), :].astype(f32)
                    return s + blk.reshape(BLK // 8, 8, tc).sum(axis=0)
                s = lax.fori_loop(0, rows // BLK, step, jnp.zeros((8, tc), f32))
                db_ref[...] = jnp.broadcast_to(_colsum(s), (8, tc))

    in_specs = [pl.BlockSpec((rows, tc), lambda j, lo=lo, hi=hi: (0, jnp.clip(j - lo, 0, hi - lo - 1)))
                for lo, hi in zip(edges[:-1], edges[1:])]
    return pl.pallas_call(
        body, grid=(D_IN // tc,),
        in_specs=in_specs + [pl.BlockSpec((rows, W_IN_HALF), lambda j: (0, 0))],
        out_specs=[pl.BlockSpec((tc, W_IN_HALF), lambda j: (j, 0)), pl.BlockSpec((rows, tc), lambda j: (0, j)),
                   pl.BlockSpec((8, tc), lambda j: (0, j))],
        out_shape=[SDS((D_IN, W_IN_HALF), bf16), SDS((rows, D_IN), bf16), SDS((8, D_IN), f32)],
        name="mm_dwin_0", compiler_params=_cp(("arbitrary",), VMEM_LIMIT_MB),
    )(*[a for a, _ in parts], hb)


def _mm_dwin(hb, dz, after):
    rows = dz.shape[0]
    tc = 640

    def body(dz_ref, h_ref, after_ref, o_ref):
        o_ref[...] = _dot_tn(dz_ref[...], h_ref[...]).astype(bf16)

    return pl.pallas_call(
        body, grid=(D_IN // tc,),
        in_specs=[pl.BlockSpec((rows, tc), lambda j: (0, j)),
                  pl.BlockSpec((rows, W_IN_HALF), lambda j: (0, 1)),
                  pl.BlockSpec(memory_space=pl.ANY)],
        out_specs=pl.BlockSpec((tc, W_IN_HALF), lambda j: (j, 0)),
        out_shape=SDS((D_IN, W_IN_HALF), bf16),
        name="mm_dwin_1", compiler_params=_cp(("arbitrary",), 48),
    )(dz, hb, after)


SCAN_ROWS = 32


def _scan8(a, b, reverse):
    idx = lax.broadcasted_iota(jnp.int32, a.shape, 0)
    for s in (1, 2, 4):
        sh = 8 - s if reverse else s
        a_sh, b_sh = pltpu.roll(a, sh, 0), pltpu.roll(b, sh, 0)
        m = (idx < 8 - s) if reverse else (idx >= s)
        b = jnp.where(m, a * b_sh + b, b)
        a = jnp.where(m, a * a_sh, a)
    return a, b


def _shift_rows(prev8, cur, k):
    ext = jnp.concatenate([prev8, cur], axis=0)
    return pltpu.roll(ext, k, 0)[8:, :]


def _gates(xc, w_ra, b_ra, w_ri, b_ri, ls):
    xb = xc.astype(bf16)
    r = _sigmoid(_dot(xb, w_ra) + b_ra)
    ig = _sigmoid(_dot(xb, w_ri) + b_ri)
    la = LRU_C * r * ls
    a = jnp.exp(la)
    mult = jnp.sqrt(jnp.tanh(-la) * (1.0 + a * a))
    return xb, r, ig, a, mult


_RNN_IN_SPECS = lambda rows: [
    pl.BlockSpec((1, 24, 256), lambda n: (n, 0, 0)),
    pl.BlockSpec((1, RNN_BLOCK), lambda n: (0, n)),
    pl.BlockSpec((N_DEV, 2, None, 32, RNN_BLOCK), lambda n: (0, 0, n, 0, 0)),
    pl.BlockSpec((1, RNN_BLOCK), lambda n: (0, n)),
    pl.BlockSpec((1, RNN_BLOCK), lambda n: (0, n)),
    pl.BlockSpec((1, RNN_BLOCK), lambda n: (0, n)),
]


def _rnn_fwd(z, smallw, conv_b, wrg, b_ra, b_ri, lam):
    rows = z.shape[0]
    nb = rows // BLK
    col = lambda off: pl.BlockSpec((rows, RNN_BLOCK), lambda n: (0, off // RNN_BLOCK + n))

    def body(xr_ref, gr_ref, sw_ref, cb_ref, w_ref, bra_ref, bri_ref, lam_ref, xc_ref, hr_ref, ya_ref, yat_ref, a_s):
        cw = sw_ref[0, N_META:24, :]
        cb = cb_ref[...]
        w_ra = w_ref[:, 0].reshape(RNN_BLOCK, RNN_BLOCK)
        w_ri = w_ref[:, 1].reshape(RNN_BLOCK, RNN_BLOCK)
        b_ra_v, b_ri_v = bra_ref[...], bri_ref[...]
        ls = _log_sigmoid(lam_ref[...])
        rid = lax.broadcasted_iota(jnp.int32, (BLK, 1), 0)

        def blk_step(i, carry):
            r0 = pl.multiple_of(i * BLK, BLK)
            grow = rid + r0
            valid = grow >= ROW0
            cur = jnp.where(valid, xr_ref[pl.ds(r0, BLK), :], 0.0)
            prev8 = xr_ref[pl.ds(pl.multiple_of(jnp.maximum(r0 - 8, 0), 8), 8), :] * (i > 0).astype(f32)
            xc = cb + cw[0:1] * cur
            for k in range(1, CONV_WIDTH):
                xc = xc + cw[k:k + 1] * _shift_rows(prev8, cur, k)
            xc_ref[pl.ds(r0, BLK), :] = xc
            _, _, ig, a, mult = _gates(xc, w_ra, b_ra_v, w_ri, b_ri_v, ls)
            mult = jnp.where(grow == ROW0, 1.0, mult)
            a_s[pl.ds(r0, BLK), :] = a
            hr_ref[pl.ds(r0, BLK), :] = jnp.where(valid, mult * ig * xc, 0.0)
            return carry

        lax.fori_loop(0, nb, blk_step, 0)

        def scan_step(j, carry):
            r0 = pl.multiple_of(j * SCAN_ROWS, SCAN_ROWS)
            tiles = [_scan8(a_s[pl.ds(r0 + 8 * k, 8), :], hr_ref[pl.ds(r0 + 8 * k, 8), :], False)
                     for k in range(SCAN_ROWS // 8)]
            for k, (a, b) in enumerate(tiles):
                h = b + a * carry
                hr_ref[pl.ds(r0 + 8 * k, 8), :] = h
                carry = jnp.broadcast_to(h[7:8, :], (8, RNN_BLOCK))
            return carry

        lax.fori_loop(0, rows // SCAN_ROWS, scan_step, jnp.zeros((8, RNN_BLOCK), f32))

        def gate_step(i, carry):
            r0 = pl.multiple_of(i * BLK, BLK)
            ya_ref[pl.ds(r0, BLK), :] = (hr_ref[pl.ds(r0, BLK), :]
                                         * _silu_and_grad(gr_ref[pl.ds(r0, BLK), :])[0]).astype(bf16)
            return carry

        lax.fori_loop(0, nb, gate_step, 0)
        yat_ref[...] = ya_ref[...].astype(f32).T.astype(bf16)

    return pl.pallas_call(
        body, grid=(N_RNN_BLOCKS,),
        in_specs=[col(0), col(OFF_GR)] + _RNN_IN_SPECS(rows),
        out_specs=[pl.BlockSpec((rows, RNN_BLOCK), lambda n: (0, n))] * 3
                  + [pl.BlockSpec((RNN_BLOCK, rows), lambda n: (n, 0))],
        out_shape=[SDS((rows, D), f32), SDS((rows, D), f32), SDS((rows, D), bf16), SDS((D, rows), bf16)],
        scratch_shapes=[pltpu.VMEM((rows, RNN_BLOCK), f32)],
        name="rnn_fwd", compiler_params=_cp(("arbitrary",)),
    )(z, z, smallw, conv_b, wrg, b_ra, b_ri, lam)


def _rnn_bwd(dya, hr, xc, z, smallw, conv_b, wrg, b_ra, b_ri, lam):
    rows = z.shape[0]
    nb = rows // BLK
    col = lambda off: pl.BlockSpec((rows, RNN_BLOCK), lambda n: (0, off // RNN_BLOCK + n))
    blk = pl.BlockSpec((rows, RNN_BLOCK), lambda n: (0, n))

    def body(dya_ref, hr_ref, xc_ref, xr_ref, gr_ref, sw_ref, cb_ref, w_ref, bra_ref, bri_ref, lam_ref,
             dxr_ref, dgr_ref, dw_ref, vec_ref, a_s, lam_s, dxc_s, r_s, ig_s, mult_s, dw_s):
        cw = sw_ref[0, N_META:24, :]
        w_ra = w_ref[:, 0].reshape(RNN_BLOCK, RNN_BLOCK)
        w_ri = w_ref[:, 1].reshape(RNN_BLOCK, RNN_BLOCK)
        b_ra_v, b_ri_v = bra_ref[...], bri_ref[...]
        lam_v = lam_ref[...]
        ls = _log_sigmoid(lam_v)
        rid = lax.broadcasted_iota(jnp.int32, (BLK, 1), 0)
        zrow = jnp.zeros((1, RNN_BLOCK), f32)

        def p1(i, carry):
            r0 = pl.multiple_of(i * BLK, BLK)
            sl = pl.ds(r0, BLK)
            _, r, ig, a, mult = _gates(xc_ref[sl, :], w_ra, b_ra_v, w_ri, b_ri_v, ls)
            a_s[sl, :] = a
            r_s[sl, :] = r
            ig_s[sl, :] = ig
            mult_s[sl, :] = mult
            sg, dsg = _silu_and_grad(gr_ref[sl, :])
            d = dya_ref[sl, :]
            lam_s[sl, :] = d * sg
            dgr_ref[sl, :] = (d * hr_ref[sl, :] * dsg).astype(bf16)
            return carry

        lax.fori_loop(0, nb, p1, 0)

        def p2(jj, carry):
            r0 = pl.multiple_of((rows // SCAN_ROWS - 1 - jj) * SCAN_ROWS, SCAN_ROWS)
            idx = lax.broadcasted_iota(jnp.int32, (8, RNN_BLOCK), 0)
            tiles = []
            for k in range(SCAN_ROWS // 8):
                sl = pl.ds(r0 + 8 * k, 8)
                a, g = a_s[sl, :], lam_s[sl, :]
                tiles.append((g, *_scan8(a, a * g, True)))
            for k in reversed(range(SCAN_ROWS // 8)):
                g, ca, cb_ = tiles[k]
                mu = cb_ + ca * carry
                lam_s[pl.ds(r0 + 8 * k, 8), :] = g + jnp.where(idx < 7, pltpu.roll(mu, 7, 0), carry)
                carry = jnp.broadcast_to(mu[0:1, :], (8, RNN_BLOCK))
            return carry

        lax.fori_loop(0, rows // SCAN_ROWS, p2, jnp.zeros((8, RNN_BLOCK), f32))

        dw_s[...] = jnp.zeros_like(dw_s)

        def p3(i, carry):
            d_bra, d_bri, d_ls = carry
            r0 = pl.multiple_of(i * BLK, BLK)
            sl = pl.ds(r0, BLK)
            grow = rid + r0
            valid = grow >= ROW0
            first = grow == ROW0
            xcv = xc_ref[sl, :]
            xb = xcv.astype(bf16)
            r, ig, a = r_s[sl, :], ig_s[sl, :], a_s[sl, :]
            mult = jnp.where(first, 1.0, mult_s[sl, :])
            lam_t = lam_s[sl, :]
            du = jnp.where(valid, lam_t, 0.0)
            hprev = _shift_rows(hr_ref[pl.ds(pl.multiple_of(jnp.maximum(r0 - 8, 0), 8), 8), :] * (i > 0).astype(f32), hr_ref[sl, :], 1)
            da = lam_t * hprev
            dmult = jnp.where(first, 0.0, du * ig * xcv)
            di = du * mult * xcv
            dxc = du * mult * ig
            ratio = jnp.where(valid & jnp.logical_not(first), a * a / mult, 0.0)
            dla = da * a - dmult * ratio
            dpr = (dla * (LRU_C * ls)) * r * (1.0 - r)
            dpi = di * ig * (1.0 - ig)
            dprb, dpib = dpr.astype(bf16), dpi.astype(bf16)
            dw_s[0] += _dot_tn(xb, dprb)
            dw_s[1] += _dot_tn(xb, dpib)
            dxc_s[sl, :] = dxc + _dot_nt(dprb, w_ra) + _dot_nt(dpib, w_ri)
            return d_bra + _colsum(dpr), d_bri + _colsum(dpi), d_ls + _colsum(dla * (LRU_C * r))

        d_bra, d_bri, d_ls = lax.fori_loop(0, nb, p3, (zrow, zrow, zrow))

        def p4(i, carry):
            d_cb, d_w0, d_w1, d_w2, d_w3 = carry
            r0 = pl.multiple_of(i * BLK, BLK)
            sl = pl.ds(r0, BLK)
            grow = rid + r0
            valid = grow >= ROW0
            dxc = dxc_s[sl, :]
            nxt = dxc_s[pl.ds(pl.multiple_of(jnp.minimum(r0 + BLK, rows - 8), 8), 8), :] * (i < nb - 1).astype(f32)
            ext = jnp.concatenate([dxc, nxt], axis=0)
            dxr = cw[0:1] * dxc
            for k in range(1, CONV_WIDTH):
                dxr = dxr + cw[k:k + 1] * pltpu.roll(ext, BLK + 8 - k, 0)[:BLK, :]
            dxr_ref[sl, :] = jnp.where(valid, dxr, 0.0).astype(bf16)
            cur = jnp.where(valid, xr_ref[sl, :], 0.0)
            prev8 = xr_ref[pl.ds(pl.multiple_of(jnp.maximum(r0 - 8, 0), 8), 8), :] * (i > 0).astype(f32)
            dws = [d_w0 + _colsum(dxc * cur)]
            for k, acc in ((1, d_w1), (2, d_w2), (3, d_w3)):
                dws.append(acc + _colsum(dxc * _shift_rows(prev8, cur, k)))
            return (d_cb + _colsum(dxc), *dws)

        d_cb, d_w0, d_w1, d_w2, d_w3 = lax.fori_loop(0, nb, p4, (zrow,) * 5)

        d_lam = d_ls * _sigmoid(-lam_v)
        vec_ref[...] = jnp.concatenate([d_bra, d_bri, d_lam, d_cb, d_w0, d_w1, d_w2, d_w3], axis=0)
        dw_ref[:, 0] = dw_s[0].astype(bf16).reshape(N_DEV, 32, RNN_BLOCK)
        dw_ref[:, 1] = dw_s[1].astype(bf16).reshape(N_DEV, 32, RNN_BLOCK)

    return pl.pallas_call(
        body, grid=(N_RNN_BLOCKS,),
        in_specs=[blk, blk, blk, col(0), col(OFF_GR)] + _RNN_IN_SPECS(rows),
        out_specs=[blk, blk,
                   pl.BlockSpec((N_DEV, 2, None, 32, RNN_BLOCK), lambda n: (0, 0, n, 0, 0)),
                   pl.BlockSpec((8, RNN_BLOCK), lambda n: (0, n))],
        out_shape=[SDS((rows, D), bf16), SDS((rows, D), bf16),
                   SDS((N_DEV, 2, N_RNN_BLOCKS, 32, RNN_BLOCK), bf16), SDS((8, D), f32)],
        scratch_shapes=[pltpu.VMEM((rows, RNN_BLOCK), f32)] * 6 + [pltpu.VMEM((2, RNN_BLOCK, RNN_BLOCK), f32)],
        name="rnn_bwd", compiler_params=_cp(("arbitrary",), 48),
    )(dya, hr, xc, z, z, smallw, conv_b, wrg, b_ra, b_ri, lam)


def _rope_tables(rows):
    half = jnp.arange(HALF, dtype=f32)
    inv = ROPE_THETA ** (-half / HALF)
    pos = (jnp.arange(rows) - ROW0).astype(f32)
    ang = pos[:, None] * inv[None, :]
    cos, sin = jnp.cos(ang), jnp.sin(ang)
    cos128 = jnp.concatenate([cos, cos, cos, cos], axis=1)
    sin128 = jnp.concatenate([-sin, sin, -sin, sin], axis=1)
    return cos128, sin128


def _rope128(x, cos128, sin128):
    lane = lax.broadcasted_iota(jnp.int32, x.shape, 1)
    swapped = jnp.where(lane % HEAD_DIM < HALF, pltpu.roll(x, 128 - HALF, 1), pltpu.roll(x, HALF, 1))
    return x * cos128 + swapped * sin128


def _qkv_prep(z, cos128, sin128):
    rows = z.shape[0]

    def body(q_ref, kv_ref, c_ref, s_ref, qo_ref, ko_ref, vo_ref):
        c, s = c_ref[...], s_ref[...]
        for g in range(D // 128):
            qo_ref[:, g * 128:(g + 1) * 128] = (_rope128(q_ref[:, g * 128:(g + 1) * 128], c, s)
                                                * (HEAD_DIM ** -0.5)).astype(bf16)
        for g in range(2):
            kr = _rope128(kv_ref[:, g * 128:(g + 1) * 128], c, s)
            for j in range(2):
                ko_ref[2 * g + j] = kr[:, j * HEAD_DIM:(j + 1) * HEAD_DIM].astype(bf16)
        for h in range(N_KV):
            vo_ref[h] = kv_ref[:, 256 + h * HEAD_DIM:256 + (h + 1) * HEAD_DIM].astype(bf16)

    return pl.pallas_call(
        body, grid=(rows // BLK,),
        in_specs=[pl.BlockSpec((BLK, D), lambda i: (i, OFF_Q // D)),
                  pl.BlockSpec((BLK, 512), lambda i: (i, OFF_K // 512)),
                  pl.BlockSpec((BLK, 128), lambda i: (i, 0)),
                  pl.BlockSpec((BLK, 128), lambda i: (i, 0))],
        out_specs=[pl.BlockSpec((BLK, D), lambda i: (i, 0)),
                   pl.BlockSpec((N_KV, BLK, HEAD_DIM), lambda i: (0, i, 0)),
                   pl.BlockSpec((N_KV, BLK, HEAD_DIM), lambda i: (0, i, 0))],
        out_shape=[SDS((rows, D), bf16), SDS((N_KV, rows, HEAD_DIM), bf16), SDS((N_KV, rows, HEAD_DIM), bf16)],
        name="qkv_prep", compiler_params=_cp(("arbitrary",)),
    )(z, z, cos128, sin128)


def _attn_mask(n):
    qi = n * BLK + lax.broadcasted_iota(jnp.int32, (BLK, 2 * BLK + N_META), 0)
    c = lax.broadcasted_iota(jnp.int32, (BLK, 2 * BLK + N_META), 1)
    jb = (n - 1) * BLK + c
    band = (jb >= BLK) & (jb <= qi) & (qi - jb < BLK)
    meta = (ROW0 + c - 2 * BLK) <= qi
    return ((c < 2 * BLK) & band) | ((c >= 2 * BLK) & meta)


N_KEYS = 2 * BLK + N_META


def _stack_heads(t):
    return jnp.concatenate([t[:, g * HEAD_DIM:(g + 1) * HEAD_DIM] for g in range(GROUP)], axis=0)


def _sink_column(sink_ref, h):
    g = lax.broadcasted_iota(jnp.int32, (GROUP, 1, 1), 0)
    col = jnp.zeros((GROUP, 1, 1), f32)
    for j in range(GROUP):
        col = jnp.where(g == j, sink_ref[h * GROUP + j], col)
    return col


def _kv_specs(last):
    cl = lambda n: jnp.minimum(n, last)
    return [pl.BlockSpec((None, N_META, HEAD_DIM), lambda h, n: (h, ROW0 // N_META, 0)),
            pl.BlockSpec((None, BLK, HEAD_DIM), lambda h, n: (h, jnp.maximum(cl(n) - 1, 0), 0)),
            pl.BlockSpec((None, BLK, HEAD_DIM), lambda h, n: (h, cl(n), 0))]


def _attn_fwd(q_r, k_r, v_b, z, sinks):
    rows = q_r.shape[0]
    nb = rows // BLK

    def body(sink_ref, q_ref, km_ref, kp_ref, kc_ref, vm_ref, vp_ref, vc_ref, ga_ref, o_ref, yb_ref, ybt_ref, lse_ref):
        h, n = pl.program_id(0), pl.program_id(1)
        kk = jnp.concatenate([kp_ref[...], kc_ref[...], km_ref[...]], axis=0)
        vv = jnp.concatenate([vp_ref[...], vc_ref[...], vm_ref[...]], axis=0)
        q2 = _stack_heads(q_ref[...])
        s = jnp.where(_attn_mask(n)[None], _dot_nt(q2, kk).reshape(GROUP, BLK, N_KEYS), NEG_INF)
        sink = _sink_column(sink_ref, h)
        m = jnp.maximum(jnp.max(s, axis=-1, keepdims=True), sink)
        p = jnp.exp(s - m)
        den = jnp.sum(p, axis=-1, keepdims=True) + jnp.exp(sink - m)
        o2 = _dot((p / den).astype(bf16).reshape(GROUP * BLK, N_KEYS), vv)
        lse = m + jnp.log(den)
        for g in range(GROUP):
            o_ref[:, g * HEAD_DIM:(g + 1) * HEAD_DIM] = o2[g * BLK:(g + 1) * BLK]
            lse_ref[:, g:g + 1] = lse[g]
        yb = o_ref[...] * _silu_and_grad(ga_ref[...])[0]
        yb_ref[...] = yb.astype(bf16)
        ybt_ref[...] = yb.T.astype(bf16)

    tile = pl.BlockSpec((BLK, 512), lambda h, n: (n, h))
    return pl.pallas_call(
        body, grid=(N_KV, nb),
        in_specs=[pl.BlockSpec(memory_space=pltpu.SMEM), tile] + _kv_specs(nb - 1) + _kv_specs(nb - 1)
                 + [pl.BlockSpec((BLK, 512), lambda h, n: (n, OFF_GA // 512 + h))],
        out_specs=[tile, tile, pl.BlockSpec((512, BLK), lambda h, n: (h, n)),
                   pl.BlockSpec((None, BLK, GROUP), lambda h, n: (h, n, 0))],
        out_shape=[SDS((rows, D), f32), SDS((rows, D), bf16), SDS((D, rows), bf16),
                   SDS((N_KV, rows, GROUP), f32)],
        name="attn_fwd", compiler_params=_cp(("arbitrary", "arbitrary")),
    )(sinks, q_r, k_r, k_r, k_r, v_b, v_b, v_b, z)


def _attn_bwd(dyb, o32, lse, q_r, k_r, v_b, z, sinks):
    rows = q_r.shape[0]
    nb = rows // BLK
    cl = lambda n: jnp.minimum(n, nb - 1)

    def body(sink_ref, dyb_ref, o_ref, lse_ref, q_ref, km_ref, kp_ref, kc_ref, vm_ref, vp_ref, vc_ref, ga_ref,
             dq_ref, dga_ref, dk_ref, dv_ref, dkm_ref, dvm_ref, dsr_ref, ck_s, cv_s):
        h, n = pl.program_id(0), pl.program_id(1)

        @pl.when(n == 0)
        def _():
            dkm_ref[...] = jnp.zeros_like(dkm_ref)
            dvm_ref[...] = jnp.zeros_like(dvm_ref)
            ck_s[...] = jnp.zeros_like(ck_s)
            cv_s[...] = jnp.zeros_like(cv_s)

        @pl.when(n < nb)
        def _():
            kk = jnp.concatenate([kp_ref[...], kc_ref[...], km_ref[...]], axis=0)
            vv = jnp.concatenate([vp_ref[...], vc_ref[...], vm_ref[...]], axis=0)
            sg, dsg = _silu_and_grad(ga_ref[...])
            dyb_v = dyb_ref[...]
            o_v = o_ref[...]
            dga_ref[...] = (dyb_v * o_v * dsg).astype(bf16)
            q2 = _stack_heads(q_ref[...])
            do2 = _stack_heads(dyb_v * sg)
            lse_v = lse_ref[...]
            lse = jnp.concatenate([lse_v[:, g:g + 1] for g in range(GROUP)], axis=0).reshape(GROUP, BLK, 1)
            delta = jnp.sum(do2 * _stack_heads(o_v), axis=-1, keepdims=True).reshape(GROUP, BLK, 1)
            s = jnp.where(_attn_mask(n)[None], _dot_nt(q2, kk).reshape(GROUP, BLK, N_KEYS), NEG_INF)
            p = jnp.exp(s - lse)
            do2b = do2.astype(bf16)
            ds = (p * (_dot_nt(do2b, vv).reshape(GROUP, BLK, N_KEYS) - delta)).astype(bf16)
            ds = ds.reshape(GROUP * BLK, N_KEYS)
            dsr = -jnp.exp(_sink_column(sink_ref, h) - lse) * delta
            dq2 = _dot(ds, kk)
            for g in range(GROUP):
                dq_ref[:, g * HEAD_DIM:(g + 1) * HEAD_DIM] = dq2[g * BLK:(g + 1) * BLK]
                dsr_ref[:, g:g + 1] = dsr[g]
            dkk = _dot_tn(ds, q2)
            dvv = _dot_tn(p.astype(bf16).reshape(GROUP * BLK, N_KEYS), do2b)
            dk_ref[...] = ck_s[...] + dkk[:BLK]
            dv_ref[...] = cv_s[...] + dvv[:BLK]
            ck_s[...] = dkk[BLK:2 * BLK]
            cv_s[...] = dvv[BLK:2 * BLK]
            dkm_ref[...] += dkk[2 * BLK:]
            dvm_ref[...] += dvv[2 * BLK:]

        @pl.when(n == nb)
        def _():
            dk_ref[...] = ck_s[...]
            dv_ref[...] = cv_s[...]

    tile = pl.BlockSpec((BLK, 512), lambda h, n: (cl(n), h))
    kvout = pl.BlockSpec((None, BLK, HEAD_DIM), lambda h, n: (h, jnp.maximum(n - 1, 0), 0))
    mout = pl.BlockSpec((None, N_META, HEAD_DIM), lambda h, n: (h, 0, 0))
    stat = pl.BlockSpec((None, BLK, GROUP), lambda h, n: (h, cl(n), 0))
    return pl.pallas_call(
        body, grid=(N_KV, nb + 1),
        in_specs=[pl.BlockSpec(memory_space=pltpu.SMEM), tile, tile, stat, tile] + _kv_specs(nb - 1)
                 + _kv_specs(nb - 1) + [pl.BlockSpec((BLK, 512), lambda h, n: (cl(n), OFF_GA // 512 + h))],
        out_specs=[tile, tile, kvout, kvout, mout, mout, stat],
        out_shape=[SDS((rows, D), f32), SDS((rows, D), bf16),
                   SDS((N_KV, rows, HEAD_DIM), f32), SDS((N_KV, rows, HEAD_DIM), f32),
                   SDS((N_KV, N_META, HEAD_DIM), f32), SDS((N_KV, N_META, HEAD_DIM), f32),
                   SDS((N_KV, rows, GROUP), f32)],
        scratch_shapes=[pltpu.VMEM((BLK, HEAD_DIM), f32), pltpu.VMEM((BLK, HEAD_DIM), f32)],
        name="attn_bwd", compiler_params=_cp(("arbitrary", "arbitrary")),
    )(sinks, dyb, o32, lse, q_r, k_r, k_r, k_r, v_b, v_b, v_b, z)


def _qkv_finish(dq, dk, dv, dkm, dvm, cos128, sin128):
    rows = dq.shape[0]

    def body(dq_ref, dk_ref, dv_ref, dkm_ref, dvm_ref, c_ref, s_ref, oq_ref, okv_ref):
        first = (pl.program_id(0) == 0).astype(f32)
        c, s = c_ref[...], -s_ref[...]
        for g in range(D // 128):
            oq_ref[:, g * 128:(g + 1) * 128] = (_rope128(dq_ref[:, g * 128:(g + 1) * 128], c, s)
                                                * (HEAD_DIM ** -0.5)).astype(bf16)
        pad = jnp.zeros((ROW0, HEAD_DIM), f32)
        ks = [dk_ref[h] + first * jnp.concatenate([pad, dkm_ref[h]], axis=0) for h in range(N_KV)]
        vs = [dv_ref[h] + first * jnp.concatenate([pad, dvm_ref[h]], axis=0) for h in range(N_KV)]
        for g in range(2):
            kp = jnp.concatenate([ks[2 * g], ks[2 * g + 1]], axis=1)
            okv_ref[:, g * 128:(g + 1) * 128] = _rope128(kp, c, s).astype(bf16)
            okv_ref[:, 256 + g * 128:256 + (g + 1) * 128] = jnp.concatenate([vs[2 * g], vs[2 * g + 1]], axis=1).astype(bf16)

    kv = pl.BlockSpec((N_KV, BLK, HEAD_DIM), lambda i: (0, i, 0))
    mt = pl.BlockSpec((N_KV, N_META, HEAD_DIM), lambda i: (0, 0, 0))
    return pl.pallas_call(
        body, grid=(rows // BLK,),
        in_specs=[pl.BlockSpec((BLK, D), lambda i: (i, 0)), kv, kv, mt, mt,
                  pl.BlockSpec((BLK, 128), lambda i: (i, 0)), pl.BlockSpec((BLK, 128), lambda i: (i, 0))],
        out_specs=[pl.BlockSpec((BLK, D), lambda i: (i, 0)), pl.BlockSpec((BLK, 512), lambda i: (i, 0))],
        out_shape=[SDS((rows, D), bf16), SDS((rows, 512), bf16)],
        name="qkv_finish", compiler_params=_cp(("arbitrary",)),
    )(dq, dk, dv, dkm, dvm, cos128, sin128)


_TW = 512


def _mix_specs(rows):
    tr = _row_chunk(rows)
    tile = pl.BlockSpec((tr, _TW), lambda i, j: (i, j))
    ga = pl.BlockSpec((tr, _TW), lambda i, j: (i, OFF_G // _TW + j))
    gb = pl.BlockSpec((tr, _TW), lambda i, j: (i, (OFF_G + D) // _TW + j))
    return (rows // tr, D // _TW), tile, ga, gb


def _mix_fwd(y_a, y_b, z):
    rows = y_a.shape[0]
    tw = 256
    col = lambda off: pl.BlockSpec((rows, tw), lambda j: (0, off // tw + j))

    def body(ya_ref, yb_ref, ga_ref, gb_ref, o_ref, ot_ref):
        mixed = (_sigmoid(ga_ref[...]) * ya_ref[...].astype(f32)
                 + _sigmoid(gb_ref[...]) * yb_ref[...].astype(f32))
        o_ref[...] = mixed.astype(bf16)
        ot_ref[...] = mixed.T.astype(bf16)

    return pl.pallas_call(
        body, grid=(D // tw,), in_specs=[col(0), col(0), col(OFF_G), col(OFF_G + D)],
        out_specs=[col(0), pl.BlockSpec((tw, rows), lambda j: (j, 0))],
        out_shape=[SDS((rows, D), bf16), SDS((D, rows), bf16)],
        name="mix_fwd", compiler_params=_cp(("arbitrary",)),
    )(y_a, y_b, z, z)


def _mix_bwd(dmixed, y_a, y_b, z):
    rows = y_a.shape[0]
    grid, _mix_tile, _mix_ga, _mix_gb = _mix_specs(rows)

    def body(dm_ref, ya_ref, yb_ref, ga_ref, gb_ref, dya_ref, dyb_ref, dga_ref, dgb_ref):
        dm = dm_ref[...].astype(f32)
        sa, sb = _sigmoid(ga_ref[...]), _sigmoid(gb_ref[...])
        dya_ref[...] = (dm * sa).astype(bf16)
        dyb_ref[...] = (dm * sb).astype(bf16)
        dga_ref[...] = (dm * ya_ref[...].astype(f32) * sa * (1.0 - sa)).astype(bf16)
        dgb_ref[...] = (dm * yb_ref[...].astype(f32) * sb * (1.0 - sb)).astype(bf16)

    return pl.pallas_call(
        body, grid=grid, in_specs=[_mix_tile, _mix_tile, _mix_tile, _mix_ga, _mix_gb],
        out_specs=[_mix_tile] * 4, out_shape=[SDS((rows, D), bf16)] * 4,
        name="mix_bwd", compiler_params=_cp(("arbitrary", "arbitrary")),
    )(dmixed, y_a, y_b, z, z)


def _final_ln(out32, h32, tgt, ln_g, ln_b):
    rows = out32.shape[0]

    def body(o_ref, h_ref, t_ref, g_ref, b_ref, du_ref, dub_ref, st_ref):
        i = pl.program_id(0)
        g = g_ref[...]
        y, xhat, rstd = _ln_rows(ALPHA * h_ref[...] + o_ref[...], g, b_ref[...])
        e = jnp.where(i > 0, y - t_ref[0], 0.0)
        dy = e * (1.0 / D)
        du = _ln_rows_bwd(dy, g, xhat, rstd)
        du_ref[...] = du
        dub_ref[...] = du.astype(bf16)
        st = jnp.concatenate([_colsum(dy * xhat), _colsum(dy), _colsum(du), _colsum(e * e) * (0.5 / D),
                              jnp.zeros((4, D), f32)], axis=0)

        @pl.when(i == 0)
        def _():
            st_ref[...] = st

        @pl.when(i > 0)
        def _():
            st_ref[...] += st

    row = pl.BlockSpec((BLK, D), lambda i: (i, 0))
    vec = pl.BlockSpec((1, D), lambda i: (0, 0))
    return pl.pallas_call(
        body, grid=(rows // BLK,),
        in_specs=[row, row, pl.BlockSpec((1, BLK, D), lambda i: (0, jnp.maximum(i - 1, 0), 0)), vec, vec],
        out_specs=[row, row, pl.BlockSpec((8, D), lambda i: (0, 0))],
        out_shape=[SDS((rows, D), f32), SDS((rows, D), bf16), SDS((8, D), f32)],
        name="final_ln", compiler_params=_cp(("arbitrary",)),
    )(out32, h32, tgt, ln_g, ln_b)


def _step_rnn(h32, hb, z, wrg, smallw, p, zero):
    rows = z.shape[0]
    cos128, sin128 = _rope_tables(rows)
    cos128 = cos128 + zero
    xc, hr, ya, ya_t = _rnn_fwd(z, smallw, p["conv_b"] + zero, wrg, p["b_ra"], p["b_ri"], p["lru_lambda"])
    q_r, k_r, v_b = _qkv_prep(z, cos128, sin128)
    return dict(cos128=cos128, sin128=sin128, h32=h32, hb=hb, z=z, xc=xc, hr=hr, ya=ya, ya_t=ya_t,
                q_r=q_r, k_r=k_r, v_b=v_b)


def _step_attn(s, p, zero):
    sinks = p["sinks"].reshape(N_KV * GROUP) + zero[0]
    o32, yb, yb_t, lse = _attn_fwd(s["q_r"], s["k_r"], s["v_b"], s["z"], sinks)
    return dict(s, sinks=sinks, o32=o32, yb=yb, yb_t=yb_t, lse=lse)


def _step_merge(s, tgt, w3, p):
    ya, yb, z = s["ya"], s["yb"], s["z"]
    y_a = _mm(ya, w3, sel=0, out_dtype=bf16, name="mm_ya")
    y_b = _mm(yb, w3, sel=1, out_dtype=bf16, name="mm_yb")
    mixed, mixed_t = _mix_fwd(y_a, y_b, z)
    out32 = _mm(mixed, w3, sel=2, bias=p["b_o"], name="mm_out")
    du32, dub, st_out = _final_ln(out32, s["h32"], tgt, p["ln_g"], p["ln_b"])

    g_wo = _mm(mixed_t, dub, out_dtype=bf16, name="mm_dwo")
    dmixed = _mm(dub, w3, sel=2, nt=True, out_dtype=bf16, name="mm_dmixed")
    dya_b, dyb_b, dma, dmb = _mix_bwd(dmixed, y_a, y_b, z)
    g_wrnn = _mm(s["ya_t"], dya_b, out_dtype=bf16, name="mm_dwrnn")
    g_wattn = _mm(s["yb_t"], dyb_b, out_dtype=bf16, name="mm_dwattn")
    dya = _mm(dya_b, w3, sel=0, nt=True, name="mm_dya")
    dyb = _mm(dyb_b, w3, sel=1, nt=True, name="mm_dyb")
    return dict(du32=du32, st_out=st_out, dma=dma, dmb=dmb, dya=dya, dyb=dyb, g_wo=g_wo, g_wrnn=g_wrnn,
                g_wattn=g_wattn)


def _step_backward(s, t, wrg, smallw, p, conv_b):
    z = s["z"]
    dxr, dgr, g_wrg, vec_rnn = _rnn_bwd(t["dya"], s["hr"], s["xc"], z, smallw, conv_b, wrg, p["b_ra"], p["b_ri"],
                                        p["lru_lambda"])
    dq_r, dga, dk, dv, dkm, dvm, dsr = _attn_bwd(t["dyb"], s["o32"], s["lse"], s["q_r"], s["k_r"], s["v_b"], z,
                                                 s["sinks"])
    dq, dkv = _qkv_finish(dq_r, dk, dv, dkm, dvm, s["cos128"], s["sin128"])
    dz_parts = [(dxr, D), (dgr, D), (dq, D), (dkv, 512), (dga, D), (t["dma"], D), (t["dmb"], D)]
    return dict(vec_rnn=vec_rnn, dsr=dsr, g_wrg=g_wrg, dz_parts=dz_parts)


def _step_input_grad(dh_lo, dh_hi, du32, x, smallw, p, after):
    grad_x, dmeta, st_emb = _ln_emb_bwd(dh_lo, dh_hi, du32, x, smallw, p["ln_emb_g"], after)
    return dict(grad_x=grad_x, dmeta=dmeta, st_emb=st_emb)


_ANY = pl.BlockSpec(memory_space=pl.ANY)
_VMEM = pl.BlockSpec(memory_space=pltpu.VMEM)
_HBM = pl.BlockSpec(memory_space=pltpu.HBM)
_SEM = pl.BlockSpec(memory_space=pltpu.SEMAPHORE)


def _place():
    x, y, c = lax.axis_index("x"), lax.axis_index("y"), lax.axis_index("c")
    return x, y, c


def _dev(px, py, pc):
    return 4 * px + 2 * py + pc


def _tile_rows(r):
    return max(t for t in range(16, 321, 16) if r % t == 0) if r > 320 else r


def _cast_w_in(w_in_t, me_idx):
    tm = _tile_rows(SHARD_IN)

    def body(me_ref, i_ref, o_ref):
        o_ref[...] = i_ref[...].astype(bf16)

    return pl.pallas_call(
        body,
        grid_spec=pltpu.PrefetchScalarGridSpec(
            num_scalar_prefetch=1, grid=(SHARD_IN // tm,),
            in_specs=[pl.BlockSpec((tm, D), lambda i, me_ref: (i, 0))],
            out_specs=pl.BlockSpec((None, tm, D), lambda i, me_ref: (me_ref[0], i, 0))),
        out_shape=SDS((N_DEV, SHARD_IN, D), bf16), name="cast_w_in", compiler_params=_cp(("arbitrary",)),
    )(me_idx, w_in_t)


def _cast_small(me_idx, w_rnn_out, w_attn_out, w_o, w_ra, w_ri, meta, conv_w):
    def body(me_ref, a_ref, b_ref, c_ref, ra_ref, ri_ref, m_ref, cw_ref, w3_ref, wrg_ref, sw_ref):
        w3_ref[0] = a_ref[0].astype(bf16)
        w3_ref[1] = b_ref[0].astype(bf16)
        w3_ref[2] = c_ref[0].astype(bf16)
        wrg_ref[0] = ra_ref[0].astype(bf16)
        wrg_ref[1] = ri_ref[0].astype(bf16)
        sw_ref[...] = jnp.concatenate([m_ref[...], cw_ref[0], jnp.zeros((4, 256), f32)], axis=0)

    args = (w_rnn_out, w_attn_out, w_o, w_ra, w_ri, meta, conv_w)
    whole = lambda shape: pl.BlockSpec(shape, lambda i, me_ref: (0,) * len(shape))
    slot = lambda shape: pl.BlockSpec((None, *shape), lambda i, me_ref: (me_ref[0], *([0] * len(shape))))
    shapes = [(3, 256, D), (2, N_RNN_BLOCKS, 32, RNN_BLOCK), (24, 256)]
    return pl.pallas_call(
        body,
        grid_spec=pltpu.PrefetchScalarGridSpec(
            num_scalar_prefetch=1, grid=(1,), in_specs=[whole(a.shape) for a in args],
            out_specs=[slot(sh) for sh in shapes]),
        out_shape=[SDS((N_DEV, *sh), dt) for sh, dt in zip(shapes, (bf16, bf16, f32))],
        name="cast_small", compiler_params=_cp(("arbitrary",)),
    )(me_idx, *args)


def _remote(src, dst, send_sems, recv_sems, k, to):
    return pltpu.make_async_remote_copy(src_ref=src, dst_ref=dst, send_sem=send_sems.at[k], recv_sem=recv_sems.at[k],
                                        device_id=to, device_id_type=MESH)


def _w_in_rows(core, early):
    if early:
        return (1 - core) * W_IN_LATE, W_IN_EARLY
    return core * W_IN_EARLY, W_IN_LATE


def _all_gather(bufs, chunks):
    n = len(bufs)
    base = [0]
    for ch in chunks:
        base.append(base[-1] + 7 * ch)

    def body(*refs):
        outs = refs[n:2 * n]
        send_sems, recv_sems = refs[2 * n:]
        x, y, c = _place()
        me, sibling = (x, y, c), (x, y, 1 - c)
        chips = [(1 - x, y), (x, 1 - y), (1 - x, 1 - y)]

        def copy(a, i, k, block, to):
            blk = outs[a].at[_dev(*block)]
            if a == 0:
                r0, r = _w_in_rows(block[2], True)
                r = r // chunks[a]
                blk = blk.at[pl.ds(pl.multiple_of(r0 + i * r, 32), r)]
            return _remote(blk, blk, send_sems, recv_sems, base[a] + 7 * i + k, to)

        pieces = [(a, i) for a in range(n) for i in range(chunks[a])]
        first = []
        for a, i in pieces:
            first.append(copy(a, i, 0, me, sibling))
            first += [copy(a, i, 1 + j, me, (*chip, c)) for j, chip in enumerate(chips)]
        for cp in first:
            cp.start()
        passed = []
        for a, i in pieces:
            for j, chip in enumerate(chips):
                copy(a, i, 1 + j, (*chip, c), me).wait_recv()
                cp = copy(a, i, 4 + j, (*chip, c), sibling)
                cp.start()
                passed.append(cp)
        for a, i in pieces:
            copy(a, i, 0, sibling, me).wait_recv()
            for j, chip in enumerate(chips):
                copy(a, i, 4 + j, (*chip, 1 - c), me).wait_recv()
        for cp in first + passed:
            cp.wait_send()

    return pl.pallas_call(
        body, in_specs=[_ANY] * n, out_specs=[_ANY] * n,
        out_shape=[SDS(b.shape, b.dtype) for b in bufs],
        input_output_aliases={a: a for a in range(n)},
        scratch_shapes=[pltpu.SemaphoreType.DMA((base[-1],)), pltpu.SemaphoreType.DMA((base[-1],))],
        name="all_gather_weights",
    )(*bufs)


def _late_rows(buf, block):
    r0, r = _w_in_rows(block[2], False)
    return buf.at[_dev(*block)].at[pl.ds(pl.multiple_of(r0, 64), r)]


def _whole_block(buf, block):
    return buf.at[_dev(*block)]


def _copies_own(part):
    def make(srcs, lands, send_sems, recv_sems):
        x, y, c = _place()
        peers = [(x, y, 1 - c), (1 - x, y, c), (x, 1 - y, c), (1 - x, 1 - y, c)]
        out = []
        for a in range(len(srcs)):
            blk = part(srcs[a], (x, y, c))
            out += [_remote(blk, blk, send_sems, recv_sems, 4 * a + k, to) for k, to in enumerate(peers)]
        return out
    return make


def _copies_pass(part):
    def make(srcs, lands, send_sems, recv_sems):
        x, y, c = _place()
        out = []
        for a in range(len(srcs)):
            for j, chip in enumerate([(1 - x, y), (x, 1 - y), (1 - x, 1 - y)]):
                blk = part(srcs[a], (*chip, c))
                out.append(_remote(blk, blk, send_sems, recv_sems, 3 * a + j, (x, y, 1 - c)))
        return out
    return make


_PEER_FLIPS = [(f // 4, (f // 2) % 2, f % 2) for f in range(1, N_DEV)]


def _copies_direct(same_src):
    def make(srcs, lands, send_sems, recv_sems):
        x, y, c = _place()
        me = _dev(x, y, c)
        out = []
        for a in range(len(srcs)):
            for k, (fx, fy, fc) in enumerate(_PEER_FLIPS):
                peer = ((x + fx) % 2, (y + fy) % 2, (c + fc) % 2)
                src = srcs[a] if same_src else srcs[a].at[_dev(*peer)]
                out.append(_remote(src, lands[a].at[me], send_sems, recv_sems, 7 * a + k, peer))
        return out
    return make


def _copies_siblings(srcs, lands, send_sems, recv_sems):
    x, y, c = _place()
    return [_remote(srcs[a].at[2 * q + (1 - c)], lands[a].at[q], send_sems, recv_sems, 4 * a + q, (x, y, 1 - c))
            for a in range(len(srcs)) for q in range(4)]


def _copies_chips(srcs, lands, send_sems, recv_sems):
    x, y, c = _place()
    chips = [(1 - x, y), (x, 1 - y), (1 - x, 1 - y)]
    return [_remote(srcs[a].at[2 * qx + qy], lands[a].at[j], send_sems, recv_sems, 3 * a + j, (qx, qy, c))
            for a in range(len(srcs)) for j, (qx, qy) in enumerate(chips)]


def _split_start(make, per_array, srcs, lands, dep, name):
    n, tot = len(srcs), len(srcs) + len(lands)

    def body(*refs):
        send_sems, recv_sems, token = refs[tot + 1], refs[tot + 2], refs[-1]
        for cp in make(refs[:n], refs[n:tot], send_sems, recv_sems):
            cp.start()
        token[...] = jnp.zeros_like(token)

    hbm = lambda t: pltpu.with_memory_space_constraint(t, pltpu.HBM)
    res = pl.pallas_call(
        body, name=name,
        out_shape=(pltpu.SemaphoreType.DMA((per_array * n,)), pltpu.SemaphoreType.DMA((per_array * n,)),
                   *[pltpu.HBM(t.shape, t.dtype) for t in (*srcs, *lands)], SDS((8, 128), f32)),
        in_specs=[_HBM] * tot + [_ANY], out_specs=(_SEM, _SEM, *([_HBM] * tot), _VMEM),
        input_output_aliases={i: 2 + i for i in range(tot)},
        compiler_params=pltpu.CompilerParams(has_side_effects=pltpu.SideEffectType.DATAFLOW_SIDE_EFFECTING),
    )(*[hbm(t) for t in (*srcs, *lands)], dep)
    return res[0], res[1], list(res[2:2 + n]), list(res[2 + n:2 + tot]), res[-1]


def _split_wait(make, send_sems, recv_sems, srcs, lands, after, name):
    n, tot = len(srcs), len(srcs) + len(lands)

    def body(*refs):
        for cp in make(refs[:n], refs[n:tot], refs[tot], refs[tot + 1]):
            cp.wait_send()
            cp.wait_recv()

    res = pl.pallas_call(
        body, name=name,
        out_shape=tuple(pltpu.HBM(t.shape, t.dtype) for t in (*srcs, *lands)),
        in_specs=[_HBM] * tot + [_SEM, _SEM, _ANY], out_specs=tuple([_HBM] * tot),
        input_output_aliases={i: i for i in range(tot)},
        compiler_params=pltpu.CompilerParams(has_side_effects=pltpu.SideEffectType.DATAFLOW_SIDE_EFFECTING),
    )(*srcs, *lands, send_sems, recv_sems, after)
    return list(res[:n]), list(res[n:])


def _adamw_direct(g, land, me_idx, w, m, v, name):
    r, wd = w.shape
    tr = min(r, 256)

    def body(me_ref, *refs):
        g_ref, peers = refs[0], refs[1:N_DEV]
        w_ref, m_ref, v_ref, g_out, d_out, m_out, v_out = refs[N_DEV:]
        gs = g_ref[...].astype(f32)
        for p_ref in peers:
            gs = gs + p_ref[...].astype(f32)
        d, mn, vn = _adamw(w_ref[...], gs, m_ref[...], v_ref[...])
        g_out[...] = gs
        d_out[...] = d
        m_out[...] = mn
        v_out[...] = vn

    tile = pl.BlockSpec((tr, wd), lambda i, me_ref: (i, 0))
    slot = lambda k: pl.BlockSpec((None, tr, wd), lambda i, me_ref: ((me_ref[0] + k) % N_DEV, i, 0))
    return pl.pallas_call(
        body,
        grid_spec=pltpu.PrefetchScalarGridSpec(
            num_scalar_prefetch=1, grid=(r // tr,),
            in_specs=[slot(0)] + [slot(k) for k in range(1, N_DEV)] + [tile, tile, tile],
            out_specs=[tile] * 4),
        out_shape=[SDS((r, wd), f32)] * 4, name=name, compiler_params=_cp(("arbitrary",), 48),
    )(me_idx, g, *([land] * (N_DEV - 1)), w, m, v)


def _pair_sum(g, r1, c_idx, name):
    _, r, w = g.shape
    tr = _tile_rows(r)

    def body(c_ref, g_ref, r_ref, o_ref):
        o_ref[...] = (g_ref[...].astype(f32) + r_ref[...].astype(f32)).astype(bf16)

    return pl.pallas_call(
        body,
        grid_spec=pltpu.PrefetchScalarGridSpec(
            num_scalar_prefetch=1, grid=(4, r // tr),
            in_specs=[pl.BlockSpec((None, tr, w), lambda q, i, c_ref: (2 * q + c_ref[0], i, 0)),
                      pl.BlockSpec((None, tr, w), lambda q, i, c_ref: (q, i, 0))],
            out_specs=pl.BlockSpec((None, tr, w), lambda q, i, c_ref: (q, i, 0))),
        out_shape=SDS((4, r, w), bf16), name=name, compiler_params=_cp(("arbitrary", "arbitrary")),
    )(c_idx, g, r1)


def _adamw(w, g, m, v):
    m = ADAM_B1 * m + (1.0 - ADAM_B1) * g
    v = ADAM_B2 * v + (1.0 - ADAM_B2) * (g * g)
    m_hat = m / (1.0 - ADAM_B1 ** ADAM_STEP)
    v_hat = v / (1.0 - ADAM_B2 ** ADAM_STEP)
    delta = -ADAM_LR * (m_hat / (jnp.sqrt(v_hat) + ADAM_EPS) + ADAM_WD * w)
    return delta, m, v


def _adamw_big(part, r2, q_idx, w, m, v, name, row_off=0, cols=(0, 1), prev=None):
    r, wd = w.shape
    tr = _tile_rows(r)
    k, ncol = cols
    wp = wd // ncol

    def body(q_ref, p_ref, r_ref, w_ref, m_ref, v_ref, *rest):
        g_out, d_out, m_out, v_out = rest[-4:]
        g = p_ref[...].astype(f32)
        for j in range(3):
            g = g + r_ref[j].astype(f32)
        d, mn, vn = _adamw(w_ref[...], g, m_ref[...], v_ref[...])
        g_out[...] = g
        d_out[...] = d
        m_out[...] = mn
        v_out[...] = vn

    tile = pl.BlockSpec((tr, wp), lambda i, q_ref: (i, k))
    prev = list(prev) if prev is not None else []
    return pl.pallas_call(
        body,
        grid_spec=pltpu.PrefetchScalarGridSpec(
            num_scalar_prefetch=1, grid=(r // tr,),
            in_specs=[pl.BlockSpec((None, tr, wp), lambda i, q_ref: (q_ref[0], row_off + i, 0)),
                      pl.BlockSpec((3, tr, wp), lambda i, q_ref: (0, row_off + i, 0)), tile, tile, tile]
                     + [pl.BlockSpec(memory_space=pl.ANY)] * len(prev),
            out_specs=[tile] * 4),
        out_shape=[SDS((r, wd), f32)] * 4, name=name,
        input_output_aliases={6 + i: i for i in range(len(prev))},
        compiler_params=_cp(("arbitrary",), 48),
    )(q_idx, part, r2, w, m, v, *prev)


_SMALL_ROWS = 24


def _pack_early(vec_rnn, st_out, dsr, db_in):
    def body(vr_ref, so_ref, dsr_ref, db_ref, sm_ref, sm2_ref):
        sm_ref[...] = jnp.zeros_like(sm_ref)
        sm2_ref[...] = jnp.zeros_like(sm2_ref)
        sm_ref[2:3, :] = vr_ref[3:4, :]
        sm_ref[3:6, :] = vr_ref[0:3, :]
        sm_ref[6:7, :] = so_ref[2:3, :]
        sm_ref[7:9, :] = so_ref[0:2, :]
        sm_ref[10:11, :] = so_ref[3:4, :]
        for h in range(N_KV):
            sm_ref[9:10, h * GROUP:(h + 1) * GROUP] = _colsum(dsr_ref[h])
        for j in range(6):
            sm_ref[16 + j:17 + j, :] = db_ref[0:1, j * D:(j + 1) * D]
        sm_ref[22:23, 0:D_IN - 6 * D] = db_ref[0:1, 6 * D:D_IN]
        for s in range(N_DEV):
            sm2_ref[s, 0:CONV_WIDTH, :] = vr_ref[4:8, s * 256:(s + 1) * 256]

    return pl.pallas_call(
        body, out_shape=[SDS((_SMALL_ROWS, D), f32), SDS((N_DEV, 8, 256), f32)],
        name="pack_early", compiler_params=_cp(None),
    )(vec_rnn, st_out, dsr, db_in)


def _pack_late(st_emb, dmeta):
    def body(se_ref, dm_ref, sm_ref, sm2_ref):
        sm_ref[...] = se_ref[...]
        for s in range(N_DEV):
            sm2_ref[s] = dm_ref[:, s * 256:(s + 1) * 256]

    return pl.pallas_call(
        body, out_shape=[SDS((8, D), f32), SDS((N_DEV, N_META, 256), f32)],
        name="pack_late", compiler_params=_cp(None),
    )(st_emb, dmeta)


def _small_allreduce(sm, sm2):
    def body(sm_ref, sm2_ref, o_ref, o2_ref, buf, buf2, send_sems, recv_sems):
        x, y, c = _place()
        me = _dev(x, y, c)
        copies = []
        for f in range(1, N_DEV):
            fx, fy, fc = f // 4, (f // 2) % 2, f % 2
            peer = ((x + fx) % 2, (y + fy) % 2, (c + fc) % 2)
            for t, (src, dst) in enumerate(((sm_ref, buf), (sm2_ref, buf2))):
                k = 2 * (f - 1) + t
                copies.append(pltpu.make_async_remote_copy(
                    src_ref=src, dst_ref=dst.at[me], send_sem=send_sems.at[k], recv_sem=recv_sems.at[k],
                    device_id=peer, device_id_type=MESH))
        for cp in copies:
            cp.start()
        buf[me] = sm_ref[...]
        buf2[me] = sm2_ref[...]
        for cp in copies:
            cp.wait()
        acc, acc2 = buf[0], buf2[0]
        for e in range(1, N_DEV):
            acc, acc2 = acc + buf[e], acc2 + buf2[e]
        o_ref[...] = acc
        o2_ref[...] = acc2

    return pl.pallas_call(
        body, in_specs=[_VMEM, _VMEM], out_specs=[_VMEM, _VMEM],
        out_shape=[SDS(sm.shape, f32), SDS(sm2.shape, f32)],
        scratch_shapes=[pltpu.VMEM((N_DEV, *sm.shape), f32), pltpu.VMEM((N_DEV, *sm2.shape), f32),
                        pltpu.SemaphoreType.DMA((14,)), pltpu.SemaphoreType.DMA((14,))],
        name="small_allreduce",
    )(sm, sm2)


_SMALL_ROW_OF = {"ln_emb_g": 0, "ln_emb_b": 1, "conv_b": 2, "b_ra": 3, "b_ri": 4, "lru_lambda": 5, "b_o": 6,
                 "ln_g": 7, "ln_b": 8}
_SMALL_NAMES = ["ln_emb_g", "ln_emb_b", "conv_b", "b_ra", "b_ri", "lru_lambda", "b_o", "ln_g", "ln_b",
                "sinks", "b_in", "meta_tokens", "conv_w"]


def _small_update(me_idx, early, late, wmv):
    n_fixed = 7

    def in_order(me, own_ref, land_ref):
        acc = None
        for e in range(N_DEV):
            term = jnp.where(me == e, own_ref[...], land_ref[e])
            acc = term if acc is None else acc + term
        return acc

    def body(*refs):
        me_ref, own_ref, land_ref, cown_ref, cland_ref, late_ref, meta_ref = refs[:n_fixed]
        ins = refs[n_fixed:n_fixed + 3 * len(_SMALL_NAMES)]
        outs = refs[n_fixed + 3 * len(_SMALL_NAMES):]
        me = me_ref[0]
        sm = in_order(me, own_ref, land_ref)
        conv = in_order(me, cown_ref, cland_ref)

        def grad_of(name):
            if name in ("ln_emb_g", "ln_emb_b"):
                r = _SMALL_ROW_OF[name]
                return late_ref[r:r + 1, :]
            if name in _SMALL_ROW_OF:
                r = _SMALL_ROW_OF[name]
                return sm[r:r + 1, :]
            if name == "sinks":
                return sm[9:10, 0:N_KV * GROUP]
            if name == "b_in":
                return jnp.concatenate([sm[16 + j:17 + j, :] for j in range(7)], axis=1)[:, :D_IN]
            if name == "meta_tokens":
                return meta_ref[...]
            return conv[0:CONV_WIDTH, :]

        for i, name in enumerate(_SMALL_NAMES):
            w_ref, m_ref, v_ref = ins[3 * i:3 * i + 3]
            g = grad_of(name)
            d, mn, vn = _adamw(w_ref[...], g, m_ref[...], v_ref[...])
            outs[4 * i][...] = g
            outs[4 * i + 1][...] = d
            outs[4 * i + 2][...] = mn
            outs[4 * i + 3][...] = vn
        outs[-1][...] = jnp.broadcast_to(jnp.sum(sm[10:11, :], axis=1, keepdims=True), (8, 128))

    args, out_shape = [me_idx, *early, *late], []
    for name in _SMALL_NAMES:
        args += list(wmv[name])
        out_shape += [SDS(wmv[name][0].shape, f32)] * 4
    out_shape.append(SDS((8, 128), f32))
    res = pl.pallas_call(
        body, out_shape=out_shape, in_specs=[pl.BlockSpec(memory_space=pltpu.SMEM)] + [_VMEM] * (len(args) - 1),
        name="small_update", compiler_params=_cp(None))(*args)
    return {name: tuple(res[4 * i:4 * i + 4]) for i, name in enumerate(_SMALL_NAMES)}, res[-1][0, 0]


_WEIGHTS = ["meta_tokens", "ln_emb_g", "ln_emb_b", "w_in", "b_in", "conv_w", "conv_b", "w_ra", "b_ra", "w_ri",
            "b_ri", "lru_lambda", "sinks", "w_rnn_out", "w_attn_out", "w_o", "b_o", "ln_g", "ln_b"]
_SMALL_2D = {"meta_tokens": (N_META, 256), "conv_w": (CONV_WIDTH, 256), "b_in": (1, D_IN), "sinks": (1, N_KV * GROUP)}


def kernel(x, meta_tokens, ln_emb_g, ln_emb_b, w_in, b_in, conv_w, conv_b, w_ra, b_ra, w_ri, b_ri, lru_lambda, sinks, w_rnn_out, w_attn_out, w_o, b_o, ln_g, ln_b, loss_target, m_meta_tokens, m_ln_emb_g, m_ln_emb_b, m_w_in, m_b_in, m_conv_w, m_conv_b, m_w_ra, m_b_ra, m_w_ri, m_b_ri, m_lru_lambda, m_sinks, m_w_rnn_out, m_w_attn_out, m_w_o, m_b_o, m_ln_g, m_ln_b, v_meta_tokens, v_ln_emb_g, v_ln_emb_b, v_w_in, v_b_in, v_conv_w, v_conv_b, v_w_ra, v_b_ra, v_w_ri, v_b_ri, v_lru_lambda, v_sinks, v_w_rnn_out, v_w_attn_out, v_w_o, v_b_o, v_ln_g, v_ln_b):
    w = dict(meta_tokens=meta_tokens, ln_emb_g=ln_emb_g, ln_emb_b=ln_emb_b, w_in=w_in, b_in=b_in, conv_w=conv_w,
             conv_b=conv_b, w_ra=w_ra, b_ra=b_ra, w_ri=w_ri, b_ri=b_ri, lru_lambda=lru_lambda, sinks=sinks,
             w_rnn_out=w_rnn_out, w_attn_out=w_attn_out, w_o=w_o, b_o=b_o, ln_g=ln_g, ln_b=ln_b)
    m = dict(meta_tokens=m_meta_tokens, ln_emb_g=m_ln_emb_g, ln_emb_b=m_ln_emb_b, w_in=m_w_in, b_in=m_b_in,
             conv_w=m_conv_w, conv_b=m_conv_b, w_ra=m_w_ra, b_ra=m_b_ra, w_ri=m_w_ri, b_ri=m_b_ri,
             lru_lambda=m_lru_lambda, sinks=m_sinks, w_rnn_out=m_w_rnn_out, w_attn_out=m_w_attn_out, w_o=m_w_o,
             b_o=m_b_o, ln_g=m_ln_g, ln_b=m_ln_b)
    v = dict(meta_tokens=v_meta_tokens, ln_emb_g=v_ln_emb_g, ln_emb_b=v_ln_emb_b, w_in=v_w_in, b_in=v_b_in,
             conv_w=v_conv_w, conv_b=v_conv_b, w_ra=v_w_ra, b_ra=v_b_ra, w_ri=v_w_ri, b_ri=v_b_ri,
             lru_lambda=v_lru_lambda, sinks=v_sinks, w_rnn_out=v_w_rnn_out, w_attn_out=v_w_attn_out, w_o=v_w_o,
             b_o=v_b_o, ln_g=v_ln_g, ln_b=v_ln_b)
    px, py, pc = _place()
    as_idx = lambda t: jnp.reshape(t, (1,)).astype(jnp.int32)
    c_idx, q_idx, me_idx = as_idx(pc), as_idx(2 * px + py), as_idx(_dev(px, py, pc))

    w3_s, wrg_s, small_s = _cast_small(me_idx, w_rnn_out, w_attn_out, w_o, w_ra, w_ri, meta_tokens, conv_w)
    vec = lambda name: w[name].reshape(1, -1)
    p = {k: vec(k) for k in ("ln_emb_g", "ln_emb_b", "b_in", "conv_b", "b_ra", "b_ri", "lru_lambda", "sinks",
                             "b_o", "ln_g", "ln_b")}
    w_in_t = lambda a: jnp.swapaxes(a, 1, 2).reshape(SHARD_IN, D)
    wg, wrg, smallw = _all_gather([_cast_w_in(w_in_t(w_in), me_idx), wrg_s, small_s], [6, 1, 1])
    late = _split_start(_copies_own(_late_rows), 4, [wg], [], smallw, "gather_late_start")
    h32, hb = _ln_emb(x, smallw, p["ln_emb_g"], p["ln_emb_b"] + late[4][0:1, 0:1])
    z = _mm_z(hb, late[2][0].reshape(D_IN, D), p["b_in"], None, "mm_z_early")
    (wg,), _ = _split_wait(_copies_own(_late_rows), late[0], late[1], late[2], [], z, "gather_late_wait")
    passed = _split_start(_copies_pass(_late_rows), 3, [wg], [], smallw, "gather_pass_start")
    w3_own = _split_start(_copies_own(_whole_block), 4, [w3_s], [], passed[4], "gather_w3_start")
    zero = w3_own[4][0:1, 0:1]
    z = _mm_z(hb, passed[2][0].reshape(D_IN, D), p["b_in"] + zero, c_idx, "mm_z_late_own", z)
    (wg,), _ = _split_wait(_copies_pass(_late_rows), passed[0], passed[1], passed[2], [], z, "gather_pass_wait")
    w_full = wg.reshape(D_IN, D)

    z = _mm_z(hb, w_full, p["b_in"], 1 - c_idx, "mm_z_late_other", z)
    s = _step_rnn(h32, hb, z, wrg, smallw, p, zero)
    (w3,), _ = _split_wait(_copies_own(_whole_block), w3_own[0], w3_own[1], w3_own[2], [], s["ya"], "gather_w3_wait")
    w3_pass = _split_start(_copies_pass(_whole_block), 3, [w3], [], smallw, "gather_w3_pass_start")
    s = _step_attn(s, p, w3_pass[4][0:1, 0:1])
    (w3,), _ = _split_wait(_copies_pass(_whole_block), w3_pass[0], w3_pass[1], w3_pass[2], [], s["lse"],
                           "gather_w3_pass_wait")
    t = _step_merge(s, loss_target, w3, p)

    big = {}
    two_d = lambda name: (w[name].shape[-2], w[name].shape[-1])
    proj = ("w_o", "w_rnn_out", "w_attn_out")
    g_proj = [t[k].reshape(N_DEV, 256, D) for k in ("g_wo", "g_wrnn", "g_wattn")]
    g_pending = _split_start(_copies_direct(False), 7, g_proj, [lax.empty((N_DEV, 256, D), bf16) for _ in proj],
                             p["b_o"], "reduce_proj_start")
    u = _step_backward(s, t, wrg, smallw, p, p["conv_b"] + g_pending[4][0:1, 0:1])

    def siblings_start(gs, dep, tag):
        return _split_start(_copies_siblings, 4, gs, [lax.empty((4, *g.shape[1:]), bf16) for g in gs], dep,
                            "reduce_siblings_start_" + tag)

    def chips_start(gs, r1, dep, tag):
        parts = [_pair_sum(g, r, c_idx, "pair_sum_%s%d" % (tag, i)) for i, (g, r) in enumerate(zip(gs, r1))]
        return _split_start(_copies_chips, 3, parts, [lax.empty((3, *q.shape[1:]), bf16) for q in parts], dep,
                            "reduce_chips_start_" + tag)

    g_a, dz, db_in = _mm_dwin_parts(s["hb"], u["dz_parts"])
    shards = lambda g: g.reshape(N_DEV, SHARD_IN, W_IN_HALF)
    sib_a = siblings_start([shards(g_a), u["g_wrg"].reshape(N_DEV, 2 * RNN_BLOCK, RNN_BLOCK)], db_in, "a")
    g_proj, g_land = _split_wait(_copies_direct(False), *g_pending[:4], sib_a[4], "reduce_proj_wait")
    for i, name in enumerate(proj):
        res = _adamw_direct(g_proj[i], g_land[i], me_idx, w[name].reshape(two_d(name)), m[name].reshape(two_d(name)),
                            v[name].reshape(two_d(name)), "adamw_" + name)
        big[name] = tuple(r.reshape(w[name].shape) for r in res)
    chp_a = chips_start(*_split_wait(_copies_siblings, *sib_a[:4], big["w_attn_out"][3], "reduce_siblings_wait_a"),
                        db_in, "a")
    g_b = _mm_dwin(s["hb"], dz, chp_a[4])
    sib_b = siblings_start([shards(g_b)], db_in, "b")
    sm_e = _pack_early(u["vec_rnn"], t["st_out"], u["dsr"], db_in)
    early = _split_start(_copies_direct(True), 7, list(sm_e),
                         [lax.empty((N_DEV, *a.shape), f32) for a in sm_e], sib_b[4], "small_early_start")
    dh_lo = _mm_dh(dz, w_full, early[4], 0)
    chp_b = chips_start(*_split_wait(_copies_siblings, *sib_b[:4], dh_lo, "reduce_siblings_wait_b"), db_in, "b")
    dh_hi = _mm_dh(dz, w_full, chp_b[4], 1)
    parts_a, r2_a = _split_wait(_copies_chips, *chp_a[:4], dh_hi, "reduce_chips_wait_a")
    w_in_res = _adamw_big(parts_a[0], r2_a[0], q_idx, w_in_t(w["w_in"]), w_in_t(m["w_in"]), w_in_t(v["w_in"]),
                          "adamw_w_in_a", cols=(0, 2))
    u.update(_step_input_grad(dh_lo, dh_hi, t["du32"], x, smallw, p, w_in_res[3]))
    sm_l, meta_l = _small_allreduce(*_pack_late(u["st_emb"], u["dmeta"]))
    (sm_own, conv_own), (sm_land, conv_land) = _split_wait(_copies_direct(True), *early[:4], sm_l, "small_early_wait")
    me = _dev(px, py, pc)
    mine = lambda a, axis: lax.dynamic_index_in_dim(a, me, axis, keepdims=False)
    two = lambda name, t: t.reshape(_SMALL_2D.get(name, (1, D)))
    small, loss = _small_update(me_idx, (sm_own, sm_land, mine(conv_own, 0), mine(conv_land, 1)),
                                (sm_l, mine(meta_l, 0)),
                                {k: (two(k, w[k]), two(k, m[k]), two(k, v[k])) for k in _SMALL_NAMES})

    parts_b, r2_b = _split_wait(_copies_chips, *chp_b[:4], small["b_in"][2], "reduce_chips_wait_b")
    res = _adamw_big(parts_b[0], r2_b[0], q_idx, w_in_t(w["w_in"]), w_in_t(m["w_in"]), w_in_t(v["w_in"]),
                     "adamw_w_in_b", cols=(1, 2), prev=w_in_res)
    big["w_in"] = tuple(jnp.swapaxes(r.reshape(1, SHARD_IN, D), 1, 2) for r in res)
    for i, name in enumerate(("w_ra", "w_ri")):
        sq = (RNN_BLOCK, RNN_BLOCK)
        res = _adamw_big(parts_a[1], r2_a[1], q_idx, w[name].reshape(sq), m[name].reshape(sq), v[name].reshape(sq),
                         "adamw_" + name, row_off=i)
        big[name] = tuple(r.reshape(w[name].shape) for r in res)
    res = dict(big)
    for k in _SMALL_NAMES:
        res[k] = tuple(t.reshape(w[k].shape) for t in small[k])

    outs = [loss, u["grad_x"]]
    for j in range(4):
        outs += [res[k][j] for k in _WEIGHTS]
    return tuple(outs)
```

```python
import jax
import jax.numpy as jnp
from jax import lax
from jax.experimental import pallas as pl
from jax.experimental.pallas import tpu as pltpu

f32, bf16 = jnp.float32, jnp.bfloat16
SDS = jax.ShapeDtypeStruct

N_DEV = 8
D = 2048
N_META = 16
BLK = 128
ROW0 = BLK - N_META
N_RNN_BLOCKS = 8
RNN_BLOCK = D // N_RNN_BLOCKS
CONV_WIDTH = 4
LRU_C = 8.0
HEAD_DIM = 64
N_KV = 4
GROUP = 8
HALF = HEAD_DIM // 2
ROPE_THETA = 10000.0
NEG_INF = -1e30
LN_EPS = 1e-5
ALPHA = 2.0 ** 0.25
D_IN = 12800
SHARD_IN = D_IN // N_DEV
W_IN_HALF = D // 2
W_IN_LATE = 640
W_IN_EARLY = SHARD_IN - W_IN_LATE
OFF_GR, OFF_Q, OFF_K, OFF_V, OFF_GA, OFF_G = 2048, 4096, 6144, 6400, 6656, 8704
ADAM_LR, ADAM_B1, ADAM_B2, ADAM_EPS, ADAM_WD, ADAM_STEP = 1e-3, 0.9, 0.999, 1e-8, 0.01, 10
VMEM_LIMIT_MB = 56
MESH = pl.DeviceIdType.MESH


def _cp(sem=None, vmem_mb=40):
    return pltpu.CompilerParams(dimension_semantics=sem, vmem_limit_bytes=vmem_mb * 2 ** 20)


def _row_chunk(m):
    best = 16
    for c in range(16, 641, 16):
        if m % c == 0:
            best = c
    return best


def _sigmoid(x):
    return 1.0 / (1.0 + jnp.exp(-x))


def _silu_and_grad(x):
    s = _sigmoid(x)
    return x * s, s * (1.0 + x * (1.0 - s))


def _log_sigmoid(x):
    return jnp.minimum(x, 0.0) - jnp.log1p(jnp.exp(-jnp.abs(x)))


def _ln_rows(v, g, b):
    mu = jnp.mean(v, axis=-1, keepdims=True)
    c = v - mu
    var = jnp.mean(c * c, axis=-1, keepdims=True)
    rstd = lax.rsqrt(var + LN_EPS)
    xhat = c * rstd
    return xhat * g + b, xhat, rstd


def _ln_rows_bwd(dy, g, xhat, rstd):
    dxh = dy * g
    m1 = jnp.mean(dxh, axis=-1, keepdims=True)
    m2 = jnp.mean(dxh * xhat, axis=-1, keepdims=True)
    return rstd * (dxh - m1 - xhat * m2)


def _colsum(v):
    return jnp.sum(v, axis=0, keepdims=True)


def _dot(a, b):
    return jnp.dot(a, b, preferred_element_type=f32)


def _dot_nt(a, b):
    return lax.dot_general(a, b, (((1,), (1,)), ((), ())), preferred_element_type=f32)


def _dot_tn(a, b):
    return lax.dot_general(a, b, (((0,), (0,)), ((), ())), preferred_element_type=f32)


def _meta_full(sw_ref):
    return jnp.concatenate([sw_ref[s, 0:N_META, :] for s in range(N_DEV)], axis=1)


def _ln_emb(x, smallw, g_e, b_e):
    seq = x.shape[1]
    rows = seq + BLK
    nb = rows // BLK

    def body(x_ref, sw_ref, g_ref, b_ref, h32_ref, hb_ref):
        i = pl.program_id(0)
        g, b = g_ref[...], b_ref[...]

        def emit(blk):
            h32_ref[...] = blk
            hb_ref[...] = blk.astype(bf16)

        @pl.when(i == 0)
        def _():
            hm = _ln_rows(_meta_full(sw_ref), g, b)[0]
            emit(jnp.concatenate([jnp.zeros((ROW0, D), f32), hm], axis=0))

        @pl.when(i > 0)
        def _():
            emit(_ln_rows(x_ref[0], g, b)[0])

    return pl.pallas_call(
        body, grid=(nb,),
        in_specs=[pl.BlockSpec((1, BLK, D), lambda i: (0, jnp.maximum(i - 1, 0), 0)),
                  pl.BlockSpec((N_DEV, 24, 256), lambda i: (0, 0, 0)),
                  pl.BlockSpec((1, D), lambda i: (0, 0)),
                  pl.BlockSpec((1, D), lambda i: (0, 0))],
        out_specs=[pl.BlockSpec((BLK, D), lambda i: (i, 0)),
                   pl.BlockSpec((BLK, D), lambda i: (i, 0))],
        out_shape=[SDS((rows, D), f32), SDS((rows, D), bf16)],
        name="ln_emb", compiler_params=_cp(("arbitrary",)),
    )(x, smallw, g_e, b_e)


def _ln_emb_bwd(dh_lo, dh_hi, du32, x, smallw, g_e, after):
    seq = x.shape[1]
    rows = seq + BLK
    nb = rows // BLK

    def body(dlo_ref, dhi_ref, du_ref, x_ref, sw_ref, g_ref, after_ref, gx_ref, dmeta_ref, st_ref):
        i = pl.program_id(0)
        g = g_ref[...]
        dht = jnp.concatenate([dlo_ref[...], dhi_ref[...]], axis=1) + ALPHA * du_ref[...]

        @pl.when(i == 0)
        def _():
            v = jnp.concatenate([jnp.zeros((ROW0, D), f32), _meta_full(sw_ref)], axis=0)
            valid = lax.broadcasted_iota(jnp.int32, (BLK, 1), 0) >= ROW0
            d = jnp.where(valid, dht, 0.0)
            _, xhat, rstd = _ln_rows(v, g, 0.0)
            dv = _ln_rows_bwd(d, g, xhat, rstd)
            dmeta_ref[...] = dv[ROW0:, :]
            st_ref[...] = jnp.concatenate([_colsum(d * xhat), _colsum(d), jnp.zeros((6, D), f32)], axis=0)

        @pl.when(i > 0)
        def _():
            _, xhat, rstd = _ln_rows(x_ref[0], g, 0.0)
            gx_ref[0] = _ln_rows_bwd(dht, g, xhat, rstd)
            st_ref[0:1, :] += _colsum(dht * xhat)
            st_ref[1:2, :] += _colsum(dht)

    return pl.pallas_call(
        body, grid=(nb,),
        in_specs=[pl.BlockSpec((BLK, W_IN_HALF), lambda i: (i, 0)),
                  pl.BlockSpec((BLK, W_IN_HALF), lambda i: (i, 0)),
                  pl.BlockSpec((BLK, D), lambda i: (i, 0)),
                  pl.BlockSpec((1, BLK, D), lambda i: (0, jnp.maximum(i - 1, 0), 0)),
                  pl.BlockSpec((N_DEV, 24, 256), lambda i: (0, 0, 0)),
                  pl.BlockSpec((1, D), lambda i: (0, 0)),
                  pl.BlockSpec(memory_space=pl.ANY)],
        out_specs=[pl.BlockSpec((1, BLK, D), lambda i: (0, jnp.maximum(i - 1, 0), 0)),
                   pl.BlockSpec((N_META, D), lambda i: (0, 0)),
                   pl.BlockSpec((8, D), lambda i: (0, 0))],
        out_shape=[SDS((1, seq, D), f32), SDS((N_META, D), f32), SDS((8, D), f32)],
        name="ln_emb_bwd", compiler_params=_cp(("arbitrary",)),
    )(dh_lo, dh_hi, du32, x, smallw, g_e, after)


def _mm(a, b, *, name, nt=False, sel=None, bias=None, out_dtype=f32, tn=512):
    m, k = a.shape
    cm = _row_chunk(m)
    stacked = sel is not None
    n = D if stacked else (b.shape[0] if nt else b.shape[1])
    am = m
    if stacked and nt:
        b_spec = pl.BlockSpec((tn // 256, None, 256, D), lambda j, i: (j, sel, 0, 0))
    elif stacked:
        b_spec = pl.BlockSpec((N_DEV, None, 256, tn), lambda j, i: (0, sel, 0, j))
    elif nt:
        b_spec = pl.BlockSpec((tn, k), lambda j, i: (j, 0))
    else:
        b_spec = pl.BlockSpec((k, tn), lambda j, i: (0, j))
    in_specs = [pl.BlockSpec((am, k), lambda j, i: (i, 0)), b_spec]
    args = [a, b]
    if bias is not None:
        in_specs.append(pl.BlockSpec((1, tn), lambda j, i: (0, j)))
        args.append(bias)

    def body(*refs):
        a_ref, b_ref, o_ref = refs[0], refs[1], refs[-1]
        bm = b_ref[...]
        if stacked:
            bm = bm.reshape((tn, D) if nt else (D, tn))
        for c in range(am // cm):
            acc = (_dot_nt if nt else _dot)(a_ref[c * cm:(c + 1) * cm, :], bm)
            if bias is not None:
                acc = acc + refs[2][...]
            o_ref[c * cm:(c + 1) * cm, :] = acc.astype(out_dtype)

    return pl.pallas_call(
        body, grid=(n // tn, m // am), in_specs=in_specs,
        out_specs=pl.BlockSpec((am, tn), lambda j, i: (i, j)),
        out_shape=SDS((m, n), out_dtype), name=name, compiler_params=_cp(("arbitrary", "arbitrary"), 48),
    )(*args)


def _mm_z(hb, w_t, bias, side, name, z_prev=None):
    rows, k = hb.shape
    tn = W_IN_LATE
    cm = _row_chunk(rows)
    per = 2 * SHARD_IN // tn
    if side is None:
        side, count = jnp.zeros((1,), jnp.int32), per - 2
        tile = lambda q, t, s_ref: per * q + 1 + t
    else:
        count = 1
        tile = lambda q, t, s_ref: per * q + (per - 1) * s_ref[0]

    def body(s_ref, a_ref, b_ref, bias_ref, *rest):
        o_ref = rest[-1]
        for c in range(rows // cm):
            o_ref[c * cm:(c + 1) * cm, :] = _dot_nt(a_ref[c * cm:(c + 1) * cm, :], b_ref[...]) + bias_ref[...]

    in_specs = [pl.BlockSpec((rows, k), lambda q, t, s_ref: (0, 0)),
                pl.BlockSpec((tn, k), lambda q, t, s_ref: (tile(q, t, s_ref), 0)),
                pl.BlockSpec((1, tn), lambda q, t, s_ref: (0, tile(q, t, s_ref)))]
    args = [side, hb, w_t, bias]
    if z_prev is not None:
        in_specs.append(pl.BlockSpec(memory_space=pl.ANY))
        args.append(z_prev)
    return pl.pallas_call(
        body,
        grid_spec=pltpu.PrefetchScalarGridSpec(
            num_scalar_prefetch=1, grid=(N_DEV // 2, count), in_specs=in_specs,
            out_specs=pl.BlockSpec((rows, tn), lambda q, t, s_ref: (0, tile(q, t, s_ref)))),
        out_shape=SDS((rows, D_IN), f32), name=name,
        input_output_aliases={} if z_prev is None else {4: 0},
        compiler_params=_cp(("arbitrary", "arbitrary"), 48),
    )(*args)


def _mm_dh(dz, w_t, after, half):
    rows = dz.shape[0]
    tn = 512
    nt = W_IN_HALF // tn
    cm = _row_chunk(rows) // 2

    def body(a_ref, w_ref, after_ref, o_ref):
        o_ref[...] = _dot(a_ref[...], w_ref[...])

    return pl.pallas_call(
        body, grid=(nt, rows // cm),
        in_specs=[pl.BlockSpec((cm, D_IN), lambda j, i: (i, 0)),
                  pl.BlockSpec((D_IN, tn), lambda j, i: (0, half * nt + j)),
                  pl.BlockSpec(memory_space=pl.ANY)],
        out_specs=pl.BlockSpec((cm, tn), lambda j, i: (i, j)),
        out_shape=SDS((rows, W_IN_HALF), f32), name="mm_dh_%d" % half,
        compiler_params=_cp(("arbitrary", "arbitrary"), 48),
    )(dz, w_t, after)


def _mm_dwin_parts(hb, parts):
    rows = hb.shape[0]
    tc = 512
    edges = [0]
    for _, w in parts:
        edges.append(edges[-1] + w // tc)

    def body(*refs):
        h_ref, (o_ref, dz_ref, db_ref) = refs[len(parts)], refs[len(parts) + 1:]
        j = pl.program_id(0)
        for p_ref, lo, hi in zip(refs, edges[:-1], edges[1:]):
            @pl.when((j >= lo) & (j < hi))
            def _():
                o_ref[...] = _dot_tn(p_ref[...], h_ref[...]).astype(bf16)
                dz_ref[...] = p_ref[...]

                def step(i, s):
                    blk = p_ref[pl.ds(pl.multiple_of(i * BLK, BLK), BLK), :].astype(f32)
                    return s + blk.reshape(BLK // 8, 8, tc).sum(axis=0)
                s = lax.fori_loop(0, rows // BLK, step, jnp.zeros((8, tc), f32))
                db_ref[...] = jnp.broadcast_to(_colsum(s), (8, tc))

    in_specs = [pl.BlockSpec((rows, tc), lambda j, lo=lo, hi=hi: (0, jnp.clip(j - lo, 0, hi - lo - 1)))
                for lo, hi in zip(edges[:-1], edges[1:])]
    return pl.pallas_call(
        body, grid=(D_IN // tc,),
        in_specs=in_specs + [pl.BlockSpec((rows, W_IN_HALF), lambda j: (0, 0))],
        out_specs=[pl.BlockSpec((tc, W_IN_HALF), lambda j: (j, 0)), pl.BlockSpec((rows, tc), lambda j: (0, j)),
                   pl.BlockSpec((8, tc), lambda j: (0, j))],
        out_shape=[SDS((D_IN, W_IN_HALF), bf16), SDS((rows, D_IN), bf16), SDS((8, D_IN), f32)],
        name="mm_dwin_0", compiler_params=_cp(("arbitrary",), VMEM_LIMIT_MB),
    )(*[a for a, _ in parts], hb)


def _mm_dwin(hb, dz, after):
    rows = dz.shape[0]
    tc = 640

    def body(dz_ref, h_ref, after_ref, o_ref):
        o_ref[...] = _dot_tn(dz_ref[...], h_ref[...]).astype(bf16)

    return pl.pallas_call(
        body, grid=(D_IN // tc,),
        in_specs=[pl.BlockSpec((rows, tc), lambda j: (0, j)),
                  pl.BlockSpec((rows, W_IN_HALF), lambda j: (0, 1)),
                  pl.BlockSpec(memory_space=pl.ANY)],
        out_specs=pl.BlockSpec((tc, W_IN_HALF), lambda j: (j, 0)),
        out_shape=SDS((D_IN, W_IN_HALF), bf16),
        name="mm_dwin_1", compiler_params=_cp(("arbitrary",), 48),
    )(dz, hb, after)


SCAN_ROWS = 32


def _scan8(a, b, reverse):
    idx = lax.broadcasted_iota(jnp.int32, a.shape, 0)
    for s in (1, 2, 4):
        sh = 8 - s if reverse else s
        a_sh, b_sh = pltpu.roll(a, sh, 0), pltpu.roll(b, sh, 0)
        m = (idx < 8 - s) if reverse else (idx >= s)
        b = jnp.where(m, a * b_sh + b, b)
        a = jnp.where(m, a * a_sh, a)
    return a, b


def _shift_rows(prev8, cur, k):
    ext = jnp.concatenate([prev8, cur], axis=0)
    return pltpu.roll(ext, k, 0)[8:, :]


def _gates(xc, w_ra, b_ra, w_ri, b_ri, ls):
    xb = xc.astype(bf16)
    r = _sigmoid(_dot(xb, w_ra) + b_ra)
    ig = _sigmoid(_dot(xb, w_ri) + b_ri)
    la = LRU_C * r * ls
    a = jnp.exp(la)
    mult = jnp.sqrt(jnp.tanh(-la) * (1.0 + a * a))
    return xb, r, ig, a, mult


_RNN_IN_SPECS = lambda rows: [
    pl.BlockSpec((1, 24, 256), lambda n: (n, 0, 0)),
    pl.BlockSpec((1, RNN_BLOCK), lambda n: (0, n)),
    pl.BlockSpec((N_DEV, 2, None, 32, RNN_BLOCK), lambda n: (0, 0, n, 0, 0)),
    pl.BlockSpec((1, RNN_BLOCK), lambda n: (0, n)),
    pl.BlockSpec((1, RNN_BLOCK), lambda n: (0, n)),
    pl.BlockSpec((1, RNN_BLOCK), lambda n: (0, n)),
]


def _rnn_fwd(z, smallw, conv_b, wrg, b_ra, b_ri, lam):
    rows = z.shape[0]
    nb = rows // BLK
    col = lambda off: pl.BlockSpec((rows, RNN_BLOCK), lambda n: (0, off // RNN_BLOCK + n))

    def body(xr_ref, gr_ref, sw_ref, cb_ref, w_ref, bra_ref, bri_ref, lam_ref, xc_ref, hr_ref, ya_ref, yat_ref, a_s):
        cw = sw_ref[0, N_META:24, :]
        cb = cb_ref[...]
        w_ra = w_ref[:, 0].reshape(RNN_BLOCK, RNN_BLOCK)
        w_ri = w_ref[:, 1].reshape(RNN_BLOCK, RNN_BLOCK)
        b_ra_v, b_ri_v = bra_ref[...], bri_ref[...]
        ls = _log_sigmoid(lam_ref[...])
        rid = lax.broadcasted_iota(jnp.int32, (BLK, 1), 0)

        def blk_step(i, carry):
            r0 = pl.multiple_of(i * BLK, BLK)
            grow = rid + r0
            valid = grow >= ROW0
            cur = jnp.where(valid, xr_ref[pl.ds(r0, BLK), :], 0.0)
            prev8 = xr_ref[pl.ds(pl.multiple_of(jnp.maximum(r0 - 8, 0), 8), 8), :] * (i > 0).astype(f32)
            xc = cb + cw[0:1] * cur
            for k in range(1, CONV_WIDTH):
                xc = xc + cw[k:k + 1] * _shift_rows(prev8, cur, k)
            xc_ref[pl.ds(r0, BLK), :] = xc
            _, _, ig, a, mult = _gates(xc, w_ra, b_ra_v, w_ri, b_ri_v, ls)
            mult = jnp.where(grow == ROW0, 1.0, mult)
            a_s[pl.ds(r0, BLK), :] = a
            hr_ref[pl.ds(r0, BLK), :] = jnp.where(valid, mult * ig * xc, 0.0)
            return carry

        lax.fori_loop(0, nb, blk_step, 0)

        def scan_step(j, carry):
            r0 = pl.multiple_of(j * SCAN_ROWS, SCAN_ROWS)
            tiles = [_scan8(a_s[pl.ds(r0 + 8 * k, 8), :], hr_ref[pl.ds(r0 + 8 * k, 8), :], False)
                     for k in range(SCAN_ROWS // 8)]
            for k, (a, b) in enumerate(tiles):
                h = b + a * carry
                hr_ref[pl.ds(r0 + 8 * k, 8), :] = h
                carry = jnp.broadcast_to(h[7:8, :], (8, RNN_BLOCK))
            return carry

        lax.fori_loop(0, rows // SCAN_ROWS, scan_step, jnp.zeros((8, RNN_BLOCK), f32))

        def gate_step(i, carry):
            r0 = pl.multiple_of(i * BLK, BLK)
            ya_ref[pl.ds(r0, BLK), :] = (hr_ref[pl.ds(r0, BLK), :]
                                         * _silu_and_grad(gr_ref[pl.ds(r0, BLK), :])[0]).astype(bf16)
            return carry

        lax.fori_loop(0, nb, gate_step, 0)
        yat_ref[...] = ya_ref[...].astype(f32).T.astype(bf16)

    return pl.pallas_call(
        body, grid=(N_RNN_BLOCKS,),
        in_specs=[col(0), col(OFF_GR)] + _RNN_IN_SPECS(rows),
        out_specs=[pl.BlockSpec((rows, RNN_BLOCK), lambda n: (0, n))] * 3
                  + [pl.BlockSpec((RNN_BLOCK, rows), lambda n: (n, 0))],
        out_shape=[SDS((rows, D), f32), SDS((rows, D), f32), SDS((rows, D), bf16), SDS((D, rows), bf16)],
        scratch_shapes=[pltpu.VMEM((rows, RNN_BLOCK), f32)],
        name="rnn_fwd", compiler_params=_cp(("arbitrary",)),
    )(z, z, smallw, conv_b, wrg, b_ra, b_ri, lam)


def _rnn_bwd(dya, hr, xc, z, smallw, conv_b, wrg, b_ra, b_ri, lam):
    rows = z.shape[0]
    nb = rows // BLK
    col = lambda off: pl.BlockSpec((rows, RNN_BLOCK), lambda n: (0, off // RNN_BLOCK + n))
    blk = pl.BlockSpec((rows, RNN_BLOCK), lambda n: (0, n))

    def body(dya_ref, hr_ref, xc_ref, xr_ref, gr_ref, sw_ref, cb_ref, w_ref, bra_ref, bri_ref, lam_ref,
             dxr_ref, dgr_ref, dw_ref, vec_ref, a_s, lam_s, dxc_s, r_s, ig_s, mult_s, dw_s):
        cw = sw_ref[0, N_META:24, :]
        w_ra = w_ref[:, 0].reshape(RNN_BLOCK, RNN_BLOCK)
        w_ri = w_ref[:, 1].reshape(RNN_BLOCK, RNN_BLOCK)
        b_ra_v, b_ri_v = bra_ref[...], bri_ref[...]
        lam_v = lam_ref[...]
        ls = _log_sigmoid(lam_v)
        rid = lax.broadcasted_iota(jnp.int32, (BLK, 1), 0)
        zrow = jnp.zeros((1, RNN_BLOCK), f32)

        def p1(i, carry):
            r0 = pl.multiple_of(i * BLK, BLK)
            sl = pl.ds(r0, BLK)
            _, r, ig, a, mult = _gates(xc_ref[sl, :], w_ra, b_ra_v, w_ri, b_ri_v, ls)
            a_s[sl, :] = a
            r_s[sl, :] = r
            ig_s[sl, :] = ig
            mult_s[sl, :] = mult
            sg, dsg = _silu_and_grad(gr_ref[sl, :])
            d = dya_ref[sl, :]
            lam_s[sl, :] = d * sg
            dgr_ref[sl, :] = (d * hr_ref[sl, :] * dsg).astype(bf16)
            return carry

        lax.fori_loop(0, nb, p1, 0)

        def p2(jj, carry):
            r0 = pl.multiple_of((rows // SCAN_ROWS - 1 - jj) * SCAN_ROWS, SCAN_ROWS)
            idx = lax.broadcasted_iota(jnp.int32, (8, RNN_BLOCK), 0)
            tiles = []
            for k in range(SCAN_ROWS // 8):
                sl = pl.ds(r0 + 8 * k, 8)
                a, g = a_s[sl, :], lam_s[sl, :]
                tiles.append((g, *_scan8(a, a * g, True)))
            for k in reversed(range(SCAN_ROWS // 8)):
                g, ca, cb_ = tiles[k]
                mu = cb_ + ca * carry
                lam_s[pl.ds(r0 + 8 * k, 8), :] = g + jnp.where(idx < 7, pltpu.roll(mu, 7, 0), carry)
                carry = jnp.broadcast_to(mu[0:1, :], (8, RNN_BLOCK))
            return carry

        lax.fori_loop(0, rows // SCAN_ROWS, p2, jnp.zeros((8, RNN_BLOCK), f32))

        dw_s[...] = jnp.zeros_like(dw_s)

        def p3(i, carry):
            d_bra, d_bri, d_ls = carry
            r0 = pl.multiple_of(i * BLK, BLK)
            sl = pl.ds(r0, BLK)
            grow = rid + r0
            valid = grow >= ROW0
            first = grow == ROW0
            xcv = xc_ref[sl, :]
            xb = xcv.astype(bf16)
            r, ig, a = r_s[sl, :], ig_s[sl, :], a_s[sl, :]
            mult = jnp.where(first, 1.0, mult_s[sl, :])
            lam_t = lam_s[sl, :]
            du = jnp.where(valid, lam_t, 0.0)
            hprev = _shift_rows(hr_ref[pl.ds(pl.multiple_of(jnp.maximum(r0 - 8, 0), 8), 8), :] * (i > 0).astype(f32), hr_ref[sl, :], 1)
            da = lam_t * hprev
            dmult = jnp.where(first, 0.0, du * ig * xcv)
            di = du * mult * xcv
            dxc = du * mult * ig
            ratio = jnp.where(valid & jnp.logical_not(first), a * a / mult, 0.0)
            dla = da * a - dmult * ratio
            dpr = (dla * (LRU_C * ls)) * r * (1.0 - r)
            dpi = di * ig * (1.0 - ig)
            dprb, dpib = dpr.astype(bf16), dpi.astype(bf16)
            dw_s[0] += _dot_tn(xb, dprb)
            dw_s[1] += _dot_tn(xb, dpib)
            dxc_s[sl, :] = dxc + _dot_nt(dprb, w_ra) + _dot_nt(dpib, w_ri)
            return d_bra + _colsum(dpr), d_bri + _colsum(dpi), d_ls + _colsum(dla * (LRU_C * r))

        d_bra, d_bri, d_ls = lax.fori_loop(0, nb, p3, (zrow, zrow, zrow))

        def p4(i, carry):
            d_cb, d_w0, d_w1, d_w2, d_w3 = carry
            r0 = pl.multiple_of(i * BLK, BLK)
            sl = pl.ds(r0, BLK)
            grow = rid + r0
            valid = grow >= ROW0
            dxc = dxc_s[sl, :]
            nxt = dxc_s[pl.ds(pl.multiple_of(jnp.minimum(r0 + BLK, rows - 8), 8), 8), :] * (i < nb - 1).astype(f32)
            ext = jnp.concatenate([dxc, nxt], axis=0)
            dxr = cw[0:1] * dxc
            for k in range(1, CONV_WIDTH):
                dxr = dxr + cw[k:k + 1] * pltpu.roll(ext, BLK + 8 - k, 0)[:BLK, :]
            dxr_ref[sl, :] = jnp.where(valid, dxr, 0.0).astype(bf16)
            cur = jnp.where(valid, xr_ref[sl, :], 0.0)
            prev8 = xr_ref[pl.ds(pl.multiple_of(jnp.maximum(r0 - 8, 0), 8), 8), :] * (i > 0).astype(f32)
            dws = [d_w0 + _colsum(dxc * cur)]
            for k, acc in ((1, d_w1), (2, d_w2), (3, d_w3)):
                dws.append(acc + _colsum(dxc * _shift_rows(prev8, cur, k)))
            return (d_cb + _colsum(dxc), *dws)

        d_cb, d_w0, d_w1, d_w2, d_w3 = lax.fori_loop(0, nb, p4, (zrow,) * 5)

        d_lam = d_ls * _sigmoid(-lam_v)
        vec_ref[...] = jnp.concatenate([d_bra, d_bri, d_lam, d_cb, d_w0, d_w1, d_w2, d_w3], axis=0)
        dw_ref[:, 0] = dw_s[0].astype(bf16).reshape(N_DEV, 32, RNN_BLOCK)
        dw_ref[:, 1] = dw_s[1].astype(bf16).reshape(N_DEV, 32, RNN_BLOCK)

    return pl.pallas_call(
        body, grid=(N_RNN_BLOCKS,),
        in_specs=[blk, blk, blk, col(0), col(OFF_GR)] + _RNN_IN_SPECS(rows),
        out_specs=[blk, blk,
                   pl.BlockSpec((N_DEV, 2, None, 32, RNN_BLOCK), lambda n: (0, 0, n, 0, 0)),
                   pl.BlockSpec((8, RNN_BLOCK), lambda n: (0, n))],
        out_shape=[SDS((rows, D), bf16), SDS((rows, D), bf16),
                   SDS((N_DEV, 2, N_RNN_BLOCKS, 32, RNN_BLOCK), bf16), SDS((8, D), f32)],
        scratch_shapes=[pltpu.VMEM((rows, RNN_BLOCK), f32)] * 6 + [pltpu.VMEM((2, RNN_BLOCK, RNN_BLOCK), f32)],
        name="rnn_bwd", compiler_params=_cp(("arbitrary",), 48),
    )(dya, hr, xc, z, z, smallw, conv_b, wrg, b_ra, b_ri, lam)


def _rope_tables(rows):
    half = jnp.arange(HALF, dtype=f32)
    inv = ROPE_THETA ** (-half / HALF)
    pos = (jnp.arange(rows) - ROW0).astype(f32)
    ang = pos[:, None] * inv[None, :]
    cos, sin = jnp.cos(ang), jnp.sin(ang)
    cos128 = jnp.concatenate([cos, cos, cos, cos], axis=1)
    sin128 = jnp.concatenate([-sin, sin, -sin, sin], axis=1)
    return cos128, sin128


def _rope128(x, cos128, sin128):
    lane = lax.broadcasted_iota(jnp.int32, x.shape, 1)
    swapped = jnp.where(lane % HEAD_DIM < HALF, pltpu.roll(x, 128 - HALF, 1), pltpu.roll(x, HALF, 1))
    return x * cos128 + swapped * sin128


def _qkv_prep(z, cos128, sin128):
    rows = z.shape[0]

    def body(q_ref, kv_ref, c_ref, s_ref, qo_ref, ko_ref, vo_ref):
        c, s = c_ref[...], s_ref[...]
        for g in range(D // 128):
            qo_ref[:, g * 128:(g + 1) * 128] = (_rope128(q_ref[:, g * 128:(g + 1) * 128], c, s)
                                                * (HEAD_DIM ** -0.5)).astype(bf16)
        for g in range(2):
            kr = _rope128(kv_ref[:, g * 128:(g + 1) * 128], c, s)
            for j in range(2):
                ko_ref[2 * g + j] = kr[:, j * HEAD_DIM:(j + 1) * HEAD_DIM].astype(bf16)
        for h in range(N_KV):
            vo_ref[h] = kv_ref[:, 256 + h * HEAD_DIM:256 + (h + 1) * HEAD_DIM].astype(bf16)

    return pl.pallas_call(
        body, grid=(rows // BLK,),
        in_specs=[pl.BlockSpec((BLK, D), lambda i: (i, OFF_Q // D)),
                  pl.BlockSpec((BLK, 512), lambda i: (i, OFF_K // 512)),
                  pl.BlockSpec((BLK, 128), lambda i: (i, 0)),
                  pl.BlockSpec((BLK, 128), lambda i: (i, 0))],
        out_specs=[pl.BlockSpec((BLK, D), lambda i: (i, 0)),
                   pl.BlockSpec((N_KV, BLK, HEAD_DIM), lambda i: (0, i, 0)),
                   pl.BlockSpec((N_KV, BLK, HEAD_DIM), lambda i: (0, i, 0))],
        out_shape=[SDS((rows, D), bf16), SDS((N_KV, rows, HEAD_DIM), bf16), SDS((N_KV, rows, HEAD_DIM), bf16)],
        name="qkv_prep", compiler_params=_cp(("arbitrary",)),
    )(z, z, cos128, sin128)


def _attn_mask(n):
    qi = n * BLK + lax.broadcasted_iota(jnp.int32, (BLK, 2 * BLK + N_META), 0)
    c = lax.broadcasted_iota(jnp.int32, (BLK, 2 * BLK + N_META), 1)
    jb = (n - 1) * BLK + c
    band = (jb >= BLK) & (jb <= qi) & (qi - jb < BLK)
    meta = (ROW0 + c - 2 * BLK) <= qi
    return ((c < 2 * BLK) & band) | ((c >= 2 * BLK) & meta)


N_KEYS = 2 * BLK + N_META


def _stack_heads(t):
    return jnp.concatenate([t[:, g * HEAD_DIM:(g + 1) * HEAD_DIM] for g in range(GROUP)], axis=0)


def _sink_column(sink_ref, h):
    g = lax.broadcasted_iota(jnp.int32, (GROUP, 1, 1), 0)
    col = jnp.zeros((GROUP, 1, 1), f32)
    for j in range(GROUP):
        col = jnp.where(g == j, sink_ref[h * GROUP + j], col)
    return col


def _kv_specs(last):
    cl = lambda n: jnp.minimum(n, last)
    return [pl.BlockSpec((None, N_META, HEAD_DIM), lambda h, n: (h, ROW0 // N_META, 0)),
            pl.BlockSpec((None, BLK, HEAD_DIM), lambda h, n: (h, jnp.maximum(cl(n) - 1, 0), 0)),
            pl.BlockSpec((None, BLK, HEAD_DIM), lambda h, n: (h, cl(n), 0))]


def _attn_fwd(q_r, k_r, v_b, z, sinks):
    rows = q_r.shape[0]
    nb = rows // BLK

    def body(sink_ref, q_ref, km_ref, kp_ref, kc_ref, vm_ref, vp_ref, vc_ref, ga_ref, o_ref, yb_ref, ybt_ref, lse_ref):
        h, n = pl.program_id(0), pl.program_id(1)
        kk = jnp.concatenate([kp_ref[...], kc_ref[...], km_ref[...]], axis=0)
        vv = jnp.concatenate([vp_ref[...], vc_ref[...], vm_ref[...]], axis=0)
        q2 = _stack_heads(q_ref[...])
        s = jnp.where(_attn_mask(n)[None], _dot_nt(q2, kk).reshape(GROUP, BLK, N_KEYS), NEG_INF)
        sink = _sink_column(sink_ref, h)
        m = jnp.maximum(jnp.max(s, axis=-1, keepdims=True), sink)
        p = jnp.exp(s - m)
        den = jnp.sum(p, axis=-1, keepdims=True) + jnp.exp(sink - m)
        o2 = _dot((p / den).astype(bf16).reshape(GROUP * BLK, N_KEYS), vv)
        lse = m + jnp.log(den)
        for g in range(GROUP):
            o_ref[:, g * HEAD_DIM:(g + 1) * HEAD_DIM] = o2[g * BLK:(g + 1) * BLK]
            lse_ref[:, g:g + 1] = lse[g]
        yb = o_ref[...] * _silu_and_grad(ga_ref[...])[0]
        yb_ref[...] = yb.astype(bf16)
        ybt_ref[...] = yb.T.astype(bf16)

    tile = pl.BlockSpec((BLK, 512), lambda h, n: (n, h))
    return pl.pallas_call(
        body, grid=(N_KV, nb),
        in_specs=[pl.BlockSpec(memory_space=pltpu.SMEM), tile] + _kv_specs(nb - 1) + _kv_specs(nb - 1)
                 + [pl.BlockSpec((BLK, 512), lambda h, n: (n, OFF_GA // 512 + h))],
        out_specs=[tile, tile, pl.BlockSpec((512, BLK), lambda h, n: (h, n)),
                   pl.BlockSpec((None, BLK, GROUP), lambda h, n: (h, n, 0))],
        out_shape=[SDS((rows, D), f32), SDS((rows, D), bf16), SDS((D, rows), bf16),
                   SDS((N_KV, rows, GROUP), f32)],
        name="attn_fwd", compiler_params=_cp(("arbitrary", "arbitrary")),
    )(sinks, q_r, k_r, k_r, k_r, v_b, v_b, v_b, z)


def _attn_bwd(dyb, o32, lse, q_r, k_r, v_b, z, sinks):
    rows = q_r.shape[0]
    nb = rows // BLK
    cl = lambda n: jnp.minimum(n, nb - 1)

    def body(sink_ref, dyb_ref, o_ref, lse_ref, q_ref, km_ref, kp_ref, kc_ref, vm_ref, vp_ref, vc_ref, ga_ref,
             dq_ref, dga_ref, dk_ref, dv_ref, dkm_ref, dvm_ref, dsr_ref, ck_s, cv_s):
        h, n = pl.program_id(0), pl.program_id(1)

        @pl.when(n == 0)
        def _():
            dkm_ref[...] = jnp.zeros_like(dkm_ref)
            dvm_ref[...] = jnp.zeros_like(dvm_ref)
            ck_s[...] = jnp.zeros_like(ck_s)
            cv_s[...] = jnp.zeros_like(cv_s)

        @pl.when(n < nb)
        def _():
            kk = jnp.concatenate([kp_ref[...], kc_ref[...], km_ref[...]], axis=0)
            vv = jnp.concatenate([vp_ref[...], vc_ref[...], vm_ref[...]], axis=0)
            sg, dsg = _silu_and_grad(ga_ref[...])
            dyb_v = dyb_ref[...]
            o_v = o_ref[...]
            dga_ref[...] = (dyb_v * o_v * dsg).astype(bf16)
            q2 = _stack_heads(q_ref[...])
            do2 = _stack_heads(dyb_v * sg)
            lse_v = lse_ref[...]
            lse = jnp.concatenate([lse_v[:, g:g + 1] for g in range(GROUP)], axis=0).reshape(GROUP, BLK, 1)
            delta = jnp.sum(do2 * _stack_heads(o_v), axis=-1, keepdims=True).reshape(GROUP, BLK, 1)
            s = jnp.where(_attn_mask(n)[None], _dot_nt(q2, kk).reshape(GROUP, BLK, N_KEYS), NEG_INF)
            p = jnp.exp(s - lse)
            do2b = do2.astype(bf16)
            ds = (p * (_dot_nt(do2b, vv).reshape(GROUP, BLK, N_KEYS) - delta)).astype(bf16)
            ds = ds.reshape(GROUP * BLK, N_KEYS)
            dsr = -jnp.exp(_sink_column(sink_ref, h) - lse) * delta
            dq2 = _dot(ds, kk)
            for g in range(GROUP):
                dq_ref[:, g * HEAD_DIM:(g + 1) * HEAD_DIM] = dq2[g * BLK:(g + 1) * BLK]
                dsr_ref[:, g:g + 1] = dsr[g]
            dkk = _dot_tn(ds, q2)
            dvv = _dot_tn(p.astype(bf16).reshape(GROUP * BLK, N_KEYS), do2b)
            dk_ref[...] = ck_s[...] + dkk[:BLK]
            dv_ref[...] = cv_s[...] + dvv[:BLK]
            ck_s[...] = dkk[BLK:2 * BLK]
            cv_s[...] = dvv[BLK:2 * BLK]
            dkm_ref[...] += dkk[2 * BLK:]
            dvm_ref[...] += dvv[2 * BLK:]

        @pl.when(n == nb)
        def _():
            dk_ref[...] = ck_s[...]
            dv_ref[...] = cv_s[...]

    tile = pl.BlockSpec((BLK, 512), lambda h, n: (cl(n), h))
    kvout = pl.BlockSpec((None, BLK, HEAD_DIM), lambda h, n: (h, jnp.maximum(n - 1, 0), 0))
    mout = pl.BlockSpec((None, N_META, HEAD_DIM), lambda h, n: (h, 0, 0))
    stat = pl.BlockSpec((None, BLK, GROUP), lambda h, n: (h, cl(n), 0))
    return pl.pallas_call(
        body, grid=(N_KV, nb + 1),
        in_specs=[pl.BlockSpec(memory_space=pltpu.SMEM), tile, tile, stat, tile] + _kv_specs(nb - 1)
                 + _kv_specs(nb - 1) + [pl.BlockSpec((BLK, 512), lambda h, n: (cl(n), OFF_GA // 512 + h))],
        out_specs=[tile, tile, kvout, kvout, mout, mout, stat],
        out_shape=[SDS((rows, D), f32), SDS((rows, D), bf16),
                   SDS((N_KV, rows, HEAD_DIM), f32), SDS((N_KV, rows, HEAD_DIM), f32),
                   SDS((N_KV, N_META, HEAD_DIM), f32), SDS((N_KV, N_META, HEAD_DIM), f32),
                   SDS((N_KV, rows, GROUP), f32)],
        scratch_shapes=[pltpu.VMEM((BLK, HEAD_DIM), f32), pltpu.VMEM((BLK, HEAD_DIM), f32)],
        name="attn_bwd", compiler_params=_cp(("arbitrary", "arbitrary")),
    )(sinks, dyb, o32, lse, q_r, k_r, k_r, k_r, v_b, v_b, v_b, z)


def _qkv_finish(dq, dk, dv, dkm, dvm, cos128, sin128):
    rows = dq.shape[0]

    def body(dq_ref, dk_ref, dv_ref, dkm_ref, dvm_ref, c_ref, s_ref, oq_ref, okv_ref):
        first = (pl.program_id(0) == 0).astype(f32)
        c, s = c_ref[...], -s_ref[...]
        for g in range(D // 128):
            oq_ref[:, g * 128:(g + 1) * 128] = (_rope128(dq_ref[:, g * 128:(g + 1) * 128], c, s)
                                                * (HEAD_DIM ** -0.5)).astype(bf16)
        pad = jnp.zeros((ROW0, HEAD_DIM), f32)
        ks = [dk_ref[h] + first * jnp.concatenate([pad, dkm_ref[h]], axis=0) for h in range(N_KV)]
        vs = [dv_ref[h] + first * jnp.concatenate([pad, dvm_ref[h]], axis=0) for h in range(N_KV)]
        for g in range(2):
            kp = jnp.concatenate([ks[2 * g], ks[2 * g + 1]], axis=1)
            okv_ref[:, g * 128:(g + 1) * 128] = _rope128(kp, c, s).astype(bf16)
            okv_ref[:, 256 + g * 128:256 + (g + 1) * 128] = jnp.concatenate([vs[2 * g], vs[2 * g + 1]], axis=1).astype(bf16)

    kv = pl.BlockSpec((N_KV, BLK, HEAD_DIM), lambda i: (0, i, 0))
    mt = pl.BlockSpec((N_KV, N_META, HEAD_DIM), lambda i: (0, 0, 0))
    return pl.pallas_call(
        body, grid=(rows // BLK,),
        in_specs=[pl.BlockSpec((BLK, D), lambda i: (i, 0)), kv, kv, mt, mt,
                  pl.BlockSpec((BLK, 128), lambda i: (i, 0)), pl.BlockSpec((BLK, 128), lambda i: (i, 0))],
        out_specs=[pl.BlockSpec((BLK, D), lambda i: (i, 0)), pl.BlockSpec((BLK, 512), lambda i: (i, 0))],
        out_shape=[SDS((rows, D), bf16), SDS((rows, 512), bf16)],
        name="qkv_finish", compiler_params=_cp(("arbitrary",)),
    )(dq, dk, dv, dkm, dvm, cos128, sin128)


_TW = 512


def _mix_specs(rows):
    tr = _row_chunk(rows)
    tile = pl.BlockSpec((tr, _TW), lambda i, j: (i, j))
    ga = pl.BlockSpec((tr, _TW), lambda i, j: (i, OFF_G // _TW + j))
    gb = pl.BlockSpec((tr, _TW), lambda i, j: (i, (OFF_G + D) // _TW + j))
    return (rows // tr, D // _TW), tile, ga, gb


def _mix_fwd(y_a, y_b, z):
    rows = y_a.shape[0]
    tw = 256
    col = lambda off: pl.BlockSpec((rows, tw), lambda j: (0, off // tw + j))

    def body(ya_ref, yb_ref, ga_ref, gb_ref, o_ref, ot_ref):
        mixed = (_sigmoid(ga_ref[...]) * ya_ref[...].astype(f32)
                 + _sigmoid(gb_ref[...]) * yb_ref[...].astype(f32))
        o_ref[...] = mixed.astype(bf16)
        ot_ref[...] = mixed.T.astype(bf16)

    return pl.pallas_call(
        body, grid=(D // tw,), in_specs=[col(0), col(0), col(OFF_G), col(OFF_G + D)],
        out_specs=[col(0), pl.BlockSpec((tw, rows), lambda j: (j, 0))],
        out_shape=[SDS((rows, D), bf16), SDS((D, rows), bf16)],
        name="mix_fwd", compiler_params=_cp(("arbitrary",)),
    )(y_a, y_b, z, z)


def _mix_bwd(dmixed, y_a, y_b, z):
    rows = y_a.shape[0]
    grid, _mix_tile, _mix_ga, _mix_gb = _mix_specs(rows)

    def body(dm_ref, ya_ref, yb_ref, ga_ref, gb_ref, dya_ref, dyb_ref, dga_ref, dgb_ref):
        dm = dm_ref[...].astype(f32)
        sa, sb = _sigmoid(ga_ref[...]), _sigmoid(gb_ref[...])
        dya_ref[...] = (dm * sa).astype(bf16)
        dyb_ref[...] = (dm * sb).astype(bf16)
        dga_ref[...] = (dm * ya_ref[...].astype(f32) * sa * (1.0 - sa)).astype(bf16)
        dgb_ref[...] = (dm * yb_ref[...].astype(f32) * sb * (1.0 - sb)).astype(bf16)

    return pl.pallas_call(
        body, grid=grid, in_specs=[_mix_tile, _mix_tile, _mix_tile, _mix_ga, _mix_gb],
        out_specs=[_mix_tile] * 4, out_shape=[SDS((rows, D), bf16)] * 4,
        name="mix_bwd", compiler_params=_cp(("arbitrary", "arbitrary")),
    )(dmixed, y_a, y_b, z, z)


def _final_ln(out32, h32, tgt, ln_g, ln_b):
    rows = out32.shape[0]

    def body(o_ref, h_ref, t_ref, g_ref, b_ref, du_ref, dub_ref, st_ref):
        i = pl.program_id(0)
        g = g_ref[...]
        y, xhat, rstd = _ln_rows(ALPHA * h_ref[...] + o_ref[...], g, b_ref[...])
        e = jnp.where(i > 0, y - t_ref[0], 0.0)
        dy = e * (1.0 / D)
        du = _ln_rows_bwd(dy, g, xhat, rstd)
        du_ref[...] = du
        dub_ref[...] = du.astype(bf16)
        st = jnp.concatenate([_colsum(dy * xhat), _colsum(dy), _colsum(du), _colsum(e * e) * (0.5 / D),
                              jnp.zeros((4, D), f32)], axis=0)

        @pl.when(i == 0)
        def _():
            st_ref[...] = st

        @pl.when(i > 0)
        def _():
            st_ref[...] += st

    row = pl.BlockSpec((BLK, D), lambda i: (i, 0))
    vec = pl.BlockSpec((1, D), lambda i: (0, 0))
    return pl.pallas_call(
        body, grid=(rows // BLK,),
        in_specs=[row, row, pl.BlockSpec((1, BLK, D), lambda i: (0, jnp.maximum(i - 1, 0), 0)), vec, vec],
        out_specs=[row, row, pl.BlockSpec((8, D), lambda i: (0, 0))],
        out_shape=[SDS((rows, D), f32), SDS((rows, D), bf16), SDS((8, D), f32)],
        name="final_ln", compiler_params=_cp(("arbitrary",)),
    )(out32, h32, tgt, ln_g, ln_b)


def _step_rnn(h32, hb, z, wrg, smallw, p, zero):
    rows = z.shape[0]
    cos128, sin128 = _rope_tables(rows)
    cos128 = cos128 + zero
    xc, hr, ya, ya_t = _rnn_fwd(z, smallw, p["conv_b"] + zero, wrg, p["b_ra"], p["b_ri"], p["lru_lambda"])
    q_r, k_r, v_b = _qkv_prep(z, cos128, sin128)
    return dict(cos128=cos128, sin128=sin128, h32=h32, hb=hb, z=z, xc=xc, hr=hr, ya=ya, ya_t=ya_t,
                q_r=q_r, k_r=k_r, v_b=v_b)


def _step_attn(s, p, zero):
    sinks = p["sinks"].reshape(N_KV * GROUP) + zero[0]
    o32, yb, yb_t, lse = _attn_fwd(s["q_r"], s["k_r"], s["v_b"], s["z"], sinks)
    return dict(s, sinks=sinks, o32=o32, yb=yb, yb_t=yb_t, lse=lse)


def _step_merge(s, tgt, w3, p):
    ya, yb, z = s["ya"], s["yb"], s["z"]
    y_a = _mm(ya, w3, sel=0, out_dtype=bf16, name="mm_ya")
    y_b = _mm(yb, w3, sel=1, out_dtype=bf16, name="mm_yb")
    mixed, mixed_t = _mix_fwd(y_a, y_b, z)
    out32 = _mm(mixed, w3, sel=2, bias=p["b_o"], name="mm_out")
    du32, dub, st_out = _final_ln(out32, s["h32"], tgt, p["ln_g"], p["ln_b"])

    g_wo = _mm(mixed_t, dub, out_dtype=bf16, name="mm_dwo")
    dmixed = _mm(dub, w3, sel=2, nt=True, out_dtype=bf16, name="mm_dmixed")
    dya_b, dyb_b, dma, dmb = _mix_bwd(dmixed, y_a, y_b, z)
    g_wrnn = _mm(s["ya_t"], dya_b, out_dtype=bf16, name="mm_dwrnn")
    g_wattn = _mm(s["yb_t"], dyb_b, out_dtype=bf16, name="mm_dwattn")
    dya = _mm(dya_b, w3, sel=0, nt=True, name="mm_dya")
    dyb = _mm(dyb_b, w3, sel=1, nt=True, name="mm_dyb")
    return dict(du32=du32, st_out=st_out, dma=dma, dmb=dmb, dya=dya, dyb=dyb, g_wo=g_wo, g_wrnn=g_wrnn,
                g_wattn=g_wattn)


def _step_backward(s, t, wrg, smallw, p, conv_b):
    z = s["z"]
    dxr, dgr, g_wrg, vec_rnn = _rnn_bwd(t["dya"], s["hr"], s["xc"], z, smallw, conv_b, wrg, p["b_ra"], p["b_ri"],
                                        p["lru_lambda"])
    dq_r, dga, dk, dv, dkm, dvm, dsr = _attn_bwd(t["dyb"], s["o32"], s["lse"], s["q_r"], s["k_r"], s["v_b"], z,
                                                 s["sinks"])
    dq, dkv = _qkv_finish(dq_r, dk, dv, dkm, dvm, s["cos128"], s["sin128"])
    dz_parts = [(dxr, D), (dgr, D), (dq, D), (dkv, 512), (dga, D), (t["dma"], D), (t["dmb"], D)]
    return dict(vec_rnn=vec_rnn, dsr=dsr, g_wrg=g_wrg, dz_parts=dz_parts)


def _step_input_grad(dh_lo, dh_hi, du32, x, smallw, p, after):
    grad_x, dmeta, st_emb = _ln_emb_bwd(dh_lo, dh_hi, du32, x, smallw, p["ln_emb_g"], after)
    return dict(grad_x=grad_x, dmeta=dmeta, st_emb=st_emb)


_ANY = pl.BlockSpec(memory_space=pl.ANY)
_VMEM = pl.BlockSpec(memory_space=pltpu.VMEM)
_HBM = pl.BlockSpec(memory_space=pltpu.HBM)
_SEM = pl.BlockSpec(memory_space=pltpu.SEMAPHORE)


def _place():
    x, y, c = lax.axis_index("x"), lax.axis_index("y"), lax.axis_index("c")
    return x, y, c


def _dev(px, py, pc):
    return 4 * px + 2 * py + pc


def _tile_rows(r):
    return max(t for t in range(16, 321, 16) if r % t == 0) if r > 320 else r


def _cast_w_in(w_in_t, me_idx):
    tm = _tile_rows(SHARD_IN)

    def body(me_ref, i_ref, o_ref):
        o_ref[...] = i_ref[...].astype(bf16)

    return pl.pallas_call(
        body,
        grid_spec=pltpu.PrefetchScalarGridSpec(
            num_scalar_prefetch=1, grid=(SHARD_IN // tm,),
            in_specs=[pl.BlockSpec((tm, D), lambda i, me_ref: (i, 0))],
            out_specs=pl.BlockSpec((None, tm, D), lambda i, me_ref: (me_ref[0], i, 0))),
        out_shape=SDS((N_DEV, SHARD_IN, D), bf16), name="cast_w_in", compiler_params=_cp(("arbitrary",)),
    )(me_idx, w_in_t)


def _cast_small(me_idx, w_rnn_out, w_attn_out, w_o, w_ra, w_ri, meta, conv_w):
    def body(me_ref, a_ref, b_ref, c_ref, ra_ref, ri_ref, m_ref, cw_ref, w3_ref, wrg_ref, sw_ref):
        w3_ref[0] = a_ref[0].astype(bf16)
        w3_ref[1] = b_ref[0].astype(bf16)
        w3_ref[2] = c_ref[0].astype(bf16)
        wrg_ref[0] = ra_ref[0].astype(bf16)
        wrg_ref[1] = ri_ref[0].astype(bf16)
        sw_ref[...] = jnp.concatenate([m_ref[...], cw_ref[0], jnp.zeros((4, 256), f32)], axis=0)

    args = (w_rnn_out, w_attn_out, w_o, w_ra, w_ri, meta, conv_w)
    whole = lambda shape: pl.BlockSpec(shape, lambda i, me_ref: (0,) * len(shape))
    slot = lambda shape: pl.BlockSpec((None, *shape), lambda i, me_ref: (me_ref[0], *([0] * len(shape))))
    shapes = [(3, 256, D), (2, N_RNN_BLOCKS, 32, RNN_BLOCK), (24, 256)]
    return pl.pallas_call(
        body,
        grid_spec=pltpu.PrefetchScalarGridSpec(
            num_scalar_prefetch=1, grid=(1,), in_specs=[whole(a.shape) for a in args],
            out_specs=[slot(sh) for sh in shapes]),
        out_shape=[SDS((N_DEV, *sh), dt) for sh, dt in zip(shapes, (bf16, bf16, f32))],
        name="cast_small", compiler_params=_cp(("arbitrary",)),
    )(me_idx, *args)


def _remote(src, dst, send_sems, recv_sems, k, to):
    return pltpu.make_async_remote_copy(src_ref=src, dst_ref=dst, send_sem=send_sems.at[k], recv_sem=recv_sems.at[k],
                                        device_id=to, device_id_type=MESH)


def _w_in_rows(core, early):
    if early:
        return (1 - core) * W_IN_LATE, W_IN_EARLY
    return core * W_IN_EARLY, W_IN_LATE


def _all_gather(bufs, chunks):
    n = len(bufs)
    base = [0]
    for ch in chunks:
        base.append(base[-1] + 7 * ch)

    def body(*refs):
        outs = refs[n:2 * n]
        send_sems, recv_sems = refs[2 * n:]
        x, y, c = _place()
        me, sibling = (x, y, c), (x, y, 1 - c)
        chips = [(1 - x, y), (x, 1 - y), (1 - x, 1 - y)]

        def copy(a, i, k, block, to):
            blk = outs[a].at[_dev(*block)]
            if a == 0:
                r0, r = _w_in_rows(block[2], True)
                r = r // chunks[a]
                blk = blk.at[pl.ds(pl.multiple_of(r0 + i * r, 32), r)]
            return _remote(blk, blk, send_sems, recv_sems, base[a] + 7 * i + k, to)

        pieces = [(a, i) for a in range(n) for i in range(chunks[a])]
        first = []
        for a, i in pieces:
            first.append(copy(a, i, 0, me, sibling))
            first += [copy(a, i, 1 + j, me, (*chip, c)) for j, chip in enumerate(chips)]
        for cp in first:
            cp.start()
        passed = []
        for a, i in pieces:
            for j, chip in enumerate(chips):
                copy(a, i, 1 + j, (*chip, c), me).wait_recv()
                cp = copy(a, i, 4 + j, (*chip, c), sibling)
                cp.start()
                passed.append(cp)
        for a, i in pieces:
            copy(a, i, 0, sibling, me).wait_recv()
            for j, chip in enumerate(chips):
                copy(a, i, 4 + j, (*chip, 1 - c), me).wait_recv()
        for cp in first + passed:
            cp.wait_send()

    return pl.pallas_call(
        body, in_specs=[_ANY] * n, out_specs=[_ANY] * n,
        out_shape=[SDS(b.shape, b.dtype) for b in bufs],
        input_output_aliases={a: a for a in range(n)},
        scratch_shapes=[pltpu.SemaphoreType.DMA((base[-1],)), pltpu.SemaphoreType.DMA((base[-1],))],
        name="all_gather_weights",
    )(*bufs)


def _late_rows(buf, block):
    r0, r = _w_in_rows(block[2], False)
    return buf.at[_dev(*block)].at[pl.ds(pl.multiple_of(r0, 64), r)]


def _whole_block(buf, block):
    return buf.at[_dev(*block)]


def _copies_own(part):
    def make(srcs, lands, send_sems, recv_sems):
        x, y, c = _place()
        peers = [(x, y, 1 - c), (1 - x, y, c), (x, 1 - y, c), (1 - x, 1 - y, c)]
        out = []
        for a in range(len(srcs)):
            blk = part(srcs[a], (x, y, c))
            out += [_remote(blk, blk, send_sems, recv_sems, 4 * a + k, to) for k, to in enumerate(peers)]
        return out
    return make


def _copies_pass(part):
    def make(srcs, lands, send_sems, recv_sems):
        x, y, c = _place()
        out = []
        for a in range(len(srcs)):
            for j, chip in enumerate([(1 - x, y), (x, 1 - y), (1 - x, 1 - y)]):
                blk = part(srcs[a], (*chip, c))
                out.append(_remote(blk, blk, send_sems, recv_sems, 3 * a + j, (x, y, 1 - c)))
        return out
    return make


_PEER_FLIPS = [(f // 4, (f // 2) % 2, f % 2) for f in range(1, N_DEV)]


def _copies_direct(same_src):
    def make(srcs, lands, send_sems, recv_sems):
        x, y, c = _place()
        me = _dev(x, y, c)
        out = []
        for a in range(len(srcs)):
            for k, (fx, fy, fc) in enumerate(_PEER_FLIPS):
                peer = ((x + fx) % 2, (y + fy) % 2, (c + fc) % 2)
                src = srcs[a] if same_src else srcs[a].at[_dev(*peer)]
                out.append(_remote(src, lands[a].at[me], send_sems, recv_sems, 7 * a + k, peer))
        return out
    return make


def _copies_siblings(srcs, lands, send_sems, recv_sems):
    x, y, c = _place()
    return [_remote(srcs[a].at[2 * q + (1 - c)], lands[a].at[q], send_sems, recv_sems, 4 * a + q, (x, y, 1 - c))
            for a in range(len(srcs)) for q in range(4)]


def _copies_chips(srcs, lands, send_sems, recv_sems):
    x, y, c = _place()
    chips = [(1 - x, y), (x, 1 - y), (1 - x, 1 - y)]
    return [_remote(srcs[a].at[2 * qx + qy], lands[a].at[j], send_sems, recv_sems, 3 * a + j, (qx, qy, c))
            for a in range(len(srcs)) for j, (qx, qy) in enumerate(chips)]


def _split_start(make, per_array, srcs, lands, dep, name):
    n, tot = len(srcs), len(srcs) + len(lands)

    def body(*refs):
        send_sems, recv_sems, token = refs[tot + 1], refs[tot + 2], refs[-1]
        for cp in make(refs[:n], refs[n:tot], send_sems, recv_sems):
            cp.start()
        token[...] = jnp.zeros_like(token)

    hbm = lambda t: pltpu.with_memory_space_constraint(t, pltpu.HBM)
    res = pl.pallas_call(
        body, name=name,
        out_shape=(pltpu.SemaphoreType.DMA((per_array * n,)), pltpu.SemaphoreType.DMA((per_array * n,)),
                   *[pltpu.HBM(t.shape, t.dtype) for t in (*srcs, *lands)], SDS((8, 128), f32)),
        in_specs=[_HBM] * tot + [_ANY], out_specs=(_SEM, _SEM, *([_HBM] * tot), _VMEM),
        input_output_aliases={i: 2 + i for i in range(tot)},
        compiler_params=pltpu.CompilerParams(has_side_effects=pltpu.SideEffectType.DATAFLOW_SIDE_EFFECTING),
    )(*[hbm(t) for t in (*srcs, *lands)], dep)
    return res[0], res[1], list(res[2:2 + n]), list(res[2 + n:2 + tot]), res[-1]


def _split_wait(make, send_sems, recv_sems, srcs, lands, after, name):
    n, tot = len(srcs), len(srcs) + len(lands)

    def body(*refs):
        for cp in make(refs[:n], refs[n:tot], refs[tot], refs[tot + 1]):
            cp.wait_send()
            cp.wait_recv()

    res = pl.pallas_call(
        body, name=name,
        out_shape=tuple(pltpu.HBM(t.shape, t.dtype) for t in (*srcs, *lands)),
        in_specs=[_HBM] * tot + [_SEM, _SEM, _ANY], out_specs=tuple([_HBM] * tot),
        input_output_aliases={i: i for i in range(tot)},
        compiler_params=pltpu.CompilerParams(has_side_effects=pltpu.SideEffectType.DATAFLOW_SIDE_EFFECTING),
    )(*srcs, *lands, send_sems, recv_sems, after)
    return list(res[:n]), list(res[n:])


def _adamw_direct(g, land, me_idx, w, m, v, name):
    r, wd = w.shape
    tr = min(r, 256)

    def body(me_ref, *refs):
        g_ref, peers = refs[0], refs[1:N_DEV]
        w_ref, m_ref, v_ref, g_out, d_out, m_out, v_out = refs[N_DEV:]
        gs = g_ref[...].astype(f32)
        for p_ref in peers:
            gs = gs + p_ref[...].astype(f32)
        d, mn, vn = _adamw(w_ref[...], gs, m_ref[...], v_ref[...])
        g_out[...] = gs
        d_out[...] = d
        m_out[...] = mn
        v_out[...] = vn

    tile = pl.BlockSpec((tr, wd), lambda i, me_ref: (i, 0))
    slot = lambda k: pl.BlockSpec((None, tr, wd), lambda i, me_ref: ((me_ref[0] + k) % N_DEV, i, 0))
    return pl.pallas_call(
        body,
        grid_spec=pltpu.PrefetchScalarGridSpec(
            num_scalar_prefetch=1, grid=(r // tr,),
            in_specs=[slot(0)] + [slot(k) for k in range(1, N_DEV)] + [tile, tile, tile],
            out_specs=[tile] * 4),
        out_shape=[SDS((r, wd), f32)] * 4, name=name, compiler_params=_cp(("arbitrary",), 48),
    )(me_idx, g, *([land] * (N_DEV - 1)), w, m, v)


def _pair_sum(g, r1, c_idx, name):
    _, r, w = g.shape
    tr = _tile_rows(r)

    def body(c_ref, g_ref, r_ref, o_ref):
        o_ref[...] = (g_ref[...].astype(f32) + r_ref[...].astype(f32)).astype(bf16)

    return pl.pallas_call(
        body,
        grid_spec=pltpu.PrefetchScalarGridSpec(
            num_scalar_prefetch=1, grid=(4, r // tr),
            in_specs=[pl.BlockSpec((None, tr, w), lambda q, i, c_ref: (2 * q + c_ref[0], i, 0)),
                      pl.BlockSpec((None, tr, w), lambda q, i, c_ref: (q, i, 0))],
            out_specs=pl.BlockSpec((None, tr, w), lambda q, i, c_ref: (q, i, 0))),
        out_shape=SDS((4, r, w), bf16), name=name, compiler_params=_cp(("arbitrary", "arbitrary")),
    )(c_idx, g, r1)


def _adamw(w, g, m, v):
    m = ADAM_B1 * m + (1.0 - ADAM_B1) * g
    v = ADAM_B2 * v + (1.0 - ADAM_B2) * (g * g)
    m_hat = m / (1.0 - ADAM_B1 ** ADAM_STEP)
    v_hat = v / (1.0 - ADAM_B2 ** ADAM_STEP)
    delta = -ADAM_LR * (m_hat / (jnp.sqrt(v_hat) + ADAM_EPS) + ADAM_WD * w)
    return delta, m, v


def _adamw_big(part, r2, q_idx, w, m, v, name, row_off=0, cols=(0, 1), prev=None):
    r, wd = w.shape
    tr = _tile_rows(r)
    k, ncol = cols
    wp = wd // ncol

    def body(q_ref, p_ref, r_ref, w_ref, m_ref, v_ref, *rest):
        g_out, d_out, m_out, v_out = rest[-4:]
        g = p_ref[...].astype(f32)
        for j in range(3):
            g = g + r_ref[j].astype(f32)
        d, mn, vn = _adamw(w_ref[...], g, m_ref[...], v_ref[...])
        g_out[...] = g
        d_out[...] = d
        m_out[...] = mn
        v_out[...] = vn

    tile = pl.BlockSpec((tr, wp), lambda i, q_ref: (i, k))
    prev = list(prev) if prev is not None else []
    return pl.pallas_call(
        body,
        grid_spec=pltpu.PrefetchScalarGridSpec(
            num_scalar_prefetch=1, grid=(r // tr,),
            in_specs=[pl.BlockSpec((None, tr, wp), lambda i, q_ref: (q_ref[0], row_off + i, 0)),
                      pl.BlockSpec((3, tr, wp), lambda i, q_ref: (0, row_off + i, 0)), tile, tile, tile]
                     + [pl.BlockSpec(memory_space=pl.ANY)] * len(prev),
            out_specs=[tile] * 4),
        out_shape=[SDS((r, wd), f32)] * 4, name=name,
        input_output_aliases={6 + i: i for i in range(len(prev))},
        compiler_params=_cp(("arbitrary",), 48),
    )(q_idx, part, r2, w, m, v, *prev)


_SMALL_ROWS = 24


def _pack_early(vec_rnn, st_out, dsr, db_in):
    def body(vr_ref, so_ref, dsr_ref, db_ref, sm_ref, sm2_ref):
        sm_ref[...] = jnp.zeros_like(sm_ref)
        sm2_ref[...] = jnp.zeros_like(sm2_ref)
        sm_ref[2:3, :] = vr_ref[3:4, :]
        sm_ref[3:6, :] = vr_ref[0:3, :]
        sm_ref[6:7, :] = so_ref[2:3, :]
        sm_ref[7:9, :] = so_ref[0:2, :]
        sm_ref[10:11, :] = so_ref[3:4, :]
        for h in range(N_KV):
            sm_ref[9:10, h * GROUP:(h + 1) * GROUP] = _colsum(dsr_ref[h])
        for j in range(6):
            sm_ref[16 + j:17 + j, :] = db_ref[0:1, j * D:(j + 1) * D]
        sm_ref[22:23, 0:D_IN - 6 * D] = db_ref[0:1, 6 * D:D_IN]
        for s in range(N_DEV):
            sm2_ref[s, 0:CONV_WIDTH, :] = vr_ref[4:8, s * 256:(s + 1) * 256]

    return pl.pallas_call(
        body, out_shape=[SDS((_SMALL_ROWS, D), f32), SDS((N_DEV, 8, 256), f32)],
        name="pack_early", compiler_params=_cp(None),
    )(vec_rnn, st_out, dsr, db_in)


def _pack_late(st_emb, dmeta):
    def body(se_ref, dm_ref, sm_ref, sm2_ref):
        sm_ref[...] = se_ref[...]
        for s in range(N_DEV):
            sm2_ref[s] = dm_ref[:, s * 256:(s + 1) * 256]

    return pl.pallas_call(
        body, out_shape=[SDS((8, D), f32), SDS((N_DEV, N_META, 256), f32)],
        name="pack_late", compiler_params=_cp(None),
    )(st_emb, dmeta)


_SMALL_ROW_OF = {"ln_emb_g": 0, "ln_emb_b": 1, "conv_b": 2, "b_ra": 3, "b_ri": 4, "lru_lambda": 5, "b_o": 6,
                 "ln_g": 7, "ln_b": 8}
_SMALL_NAMES = ["ln_emb_g", "ln_emb_b", "conv_b", "b_ra", "b_ri", "lru_lambda", "b_o", "ln_g", "ln_b",
                "sinks", "b_in", "meta_tokens", "conv_w"]


def _small_update(me_idx, early, late, wmv):
    n_fixed = 9

    def in_order(me, own_ref, land_ref):
        acc = None
        for e in range(N_DEV):
            term = jnp.where(me == e, own_ref[...], land_ref[e])
            acc = term if acc is None else acc + term
        return acc

    def body(*refs):
        me_ref, own_ref, land_ref, cown_ref, cland_ref, lown_ref, lland_ref, mown_ref, mland_ref = refs[:n_fixed]
        ins = refs[n_fixed:n_fixed + 3 * len(_SMALL_NAMES)]
        outs = refs[n_fixed + 3 * len(_SMALL_NAMES):]
        me = me_ref[0]
        sm = in_order(me, own_ref, land_ref)
        conv = in_order(me, cown_ref, cland_ref)
        late = in_order(me, lown_ref, lland_ref)
        meta = in_order(me, mown_ref, mland_ref)

        def grad_of(name):
            if name in ("ln_emb_g", "ln_emb_b"):
                r = _SMALL_ROW_OF[name]
                return late[r:r + 1, :]
            if name in _SMALL_ROW_OF:
                r = _SMALL_ROW_OF[name]
                return sm[r:r + 1, :]
            if name == "sinks":
                return sm[9:10, 0:N_KV * GROUP]
            if name == "b_in":
                return jnp.concatenate([sm[16 + j:17 + j, :] for j in range(7)], axis=1)[:, :D_IN]
            if name == "meta_tokens":
                return meta
            return conv[0:CONV_WIDTH, :]

        for i, name in enumerate(_SMALL_NAMES):
            w_ref, m_ref, v_ref = ins[3 * i:3 * i + 3]
            g = grad_of(name)
            d, mn, vn = _adamw(w_ref[...], g, m_ref[...], v_ref[...])
            outs[4 * i][...] = g
            outs[4 * i + 1][...] = d
            outs[4 * i + 2][...] = mn
            outs[4 * i + 3][...] = vn
        outs[-1][...] = jnp.broadcast_to(jnp.sum(sm[10:11, :], axis=1, keepdims=True), (8, 128))

    args, out_shape = [me_idx, *early, *late], []
    for name in _SMALL_NAMES:
        args += list(wmv[name])
        out_shape += [SDS(wmv[name][0].shape, f32)] * 4
    out_shape.append(SDS((8, 128), f32))
    res = pl.pallas_call(
        body, out_shape=out_shape, in_specs=[pl.BlockSpec(memory_space=pltpu.SMEM)] + [_VMEM] * (len(args) - 1),
        name="small_update", compiler_params=_cp(None))(*args)
    return {name: tuple(res[4 * i:4 * i + 4]) for i, name in enumerate(_SMALL_NAMES)}, res[-1][0, 0]


_WEIGHTS = ["meta_tokens", "ln_emb_g", "ln_emb_b", "w_in", "b_in", "conv_w", "conv_b", "w_ra", "b_ra", "w_ri",
            "b_ri", "lru_lambda", "sinks", "w_rnn_out", "w_attn_out", "w_o", "b_o", "ln_g", "ln_b"]
_SMALL_2D = {"meta_tokens": (N_META, 256), "conv_w": (CONV_WIDTH, 256), "b_in": (1, D_IN), "sinks": (1, N_KV * GROUP)}


def kernel(x, meta_tokens, ln_emb_g, ln_emb_b, w_in, b_in, conv_w, conv_b, w_ra, b_ra, w_ri, b_ri, lru_lambda, sinks, w_rnn_out, w_attn_out, w_o, b_o, ln_g, ln_b, loss_target, m_meta_tokens, m_ln_emb_g, m_ln_emb_b, m_w_in, m_b_in, m_conv_w, m_conv_b, m_w_ra, m_b_ra, m_w_ri, m_b_ri, m_lru_lambda, m_sinks, m_w_rnn_out, m_w_attn_out, m_w_o, m_b_o, m_ln_g, m_ln_b, v_meta_tokens, v_ln_emb_g, v_ln_emb_b, v_w_in, v_b_in, v_conv_w, v_conv_b, v_w_ra, v_b_ra, v_w_ri, v_b_ri, v_lru_lambda, v_sinks, v_w_rnn_out, v_w_attn_out, v_w_o, v_b_o, v_ln_g, v_ln_b):
    w = dict(meta_tokens=meta_tokens, ln_emb_g=ln_emb_g, ln_emb_b=ln_emb_b, w_in=w_in, b_in=b_in, conv_w=conv_w,
             conv_b=conv_b, w_ra=w_ra, b_ra=b_ra, w_ri=w_ri, b_ri=b_ri, lru_lambda=lru_lambda, sinks=sinks,
             w_rnn_out=w_rnn_out, w_attn_out=w_attn_out, w_o=w_o, b_o=b_o, ln_g=ln_g, ln_b=ln_b)
    m = dict(meta_tokens=m_meta_tokens, ln_emb_g=m_ln_emb_g, ln_emb_b=m_ln_emb_b, w_in=m_w_in, b_in=m_b_in,
             conv_w=m_conv_w, conv_b=m_conv_b, w_ra=m_w_ra, b_ra=m_b_ra, w_ri=m_w_ri, b_ri=m_b_ri,
             lru_lambda=m_lru_lambda, sinks=m_sinks, w_rnn_out=m_w_rnn_out, w_attn_out=m_w_attn_out, w_o=m_w_o,
             b_o=m_b_o, ln_g=m_ln_g, ln_b=m_ln_b)
    v = dict(meta_tokens=v_meta_tokens, ln_emb_g=v_ln_emb_g, ln_emb_b=v_ln_emb_b, w_in=v_w_in, b_in=v_b_in,
             conv_w=v_conv_w, conv_b=v_conv_b, w_ra=v_w_ra, b_ra=v_b_ra, w_ri=v_w_ri, b_ri=v_b_ri,
             lru_lambda=v_lru_lambda, sinks=v_sinks, w_rnn_out=v_w_rnn_out, w_attn_out=v_w_attn_out, w_o=v_w_o,
             b_o=v_b_o, ln_g=v_ln_g, ln_b=v_ln_b)
    px, py, pc = _place()
    as_idx = lambda t: jnp.reshape(t, (1,)).astype(jnp.int32)
    c_idx, q_idx, me_idx = as_idx(pc), as_idx(2 * px + py), as_idx(_dev(px, py, pc))

    w3_s, wrg_s, small_s = _cast_small(me_idx, w_rnn_out, w_attn_out, w_o, w_ra, w_ri, meta_tokens, conv_w)
    vec = lambda name: w[name].reshape(1, -1)
    p = {k: vec(k) for k in ("ln_emb_g", "ln_emb_b", "b_in", "conv_b", "b_ra", "b_ri", "lru_lambda", "sinks",
                             "b_o", "ln_g", "ln_b")}
    w_in_t = lambda a: jnp.swapaxes(a, 1, 2).reshape(SHARD_IN, D)
    wg, wrg, smallw = _all_gather([_cast_w_in(w_in_t(w_in), me_idx), wrg_s, small_s], [6, 1, 1])
    late = _split_start(_copies_own(_late_rows), 4, [wg], [], smallw, "gather_late_start")
    h32, hb = _ln_emb(x, smallw, p["ln_emb_g"], p["ln_emb_b"] + late[4][0:1, 0:1])
    z = _mm_z(hb, late[2][0].reshape(D_IN, D), p["b_in"], None, "mm_z_early")
    (wg,), _ = _split_wait(_copies_own(_late_rows), late[0], late[1], late[2], [], z, "gather_late_wait")
    passed = _split_start(_copies_pass(_late_rows), 3, [wg], [], smallw, "gather_pass_start")
    w3_own = _split_start(_copies_own(_whole_block), 4, [w3_s], [], passed[4], "gather_w3_start")
    zero = w3_own[4][0:1, 0:1]
    z = _mm_z(hb, passed[2][0].reshape(D_IN, D), p["b_in"] + zero, c_idx, "mm_z_late_own", z)
    (wg,), _ = _split_wait(_copies_pass(_late_rows), passed[0], passed[1], passed[2], [], z, "gather_pass_wait")
    w_full = wg.reshape(D_IN, D)

    z = _mm_z(hb, w_full, p["b_in"], 1 - c_idx, "mm_z_late_other", z)
    s = _step_rnn(h32, hb, z, wrg, smallw, p, zero)
    (w3,), _ = _split_wait(_copies_own(_whole_block), w3_own[0], w3_own[1], w3_own[2], [], s["ya"], "gather_w3_wait")
    w3_pass = _split_start(_copies_pass(_whole_block), 3, [w3], [], smallw, "gather_w3_pass_start")
    s = _step_attn(s, p, w3_pass[4][0:1, 0:1])
    (w3,), _ = _split_wait(_copies_pass(_whole_block), w3_pass[0], w3_pass[1], w3_pass[2], [], s["lse"],
                           "gather_w3_pass_wait")
    t = _step_merge(s, loss_target, w3, p)

    big = {}
    two_d = lambda name: (w[name].shape[-2], w[name].shape[-1])
    proj = ("w_o", "w_rnn_out", "w_attn_out")
    g_proj = [t[k].reshape(N_DEV, 256, D) for k in ("g_wo", "g_wrnn", "g_wattn")]
    g_pending = _split_start(_copies_direct(False), 7, g_proj, [lax.empty((N_DEV, 256, D), bf16) for _ in proj],
                             p["b_o"], "reduce_proj_start")
    u = _step_backward(s, t, wrg, smallw, p, p["conv_b"] + g_pending[4][0:1, 0:1])

    def siblings_start(gs, dep, tag):
        return _split_start(_copies_siblings, 4, gs, [lax.empty((4, *g.shape[1:]), bf16) for g in gs], dep,
                            "reduce_siblings_start_" + tag)

    def chips_start(gs, r1, dep, tag):
        parts = [_pair_sum(g, r, c_idx, "pair_sum_%s%d" % (tag, i)) for i, (g, r) in enumerate(zip(gs, r1))]
        return _split_start(_copies_chips, 3, parts, [lax.empty((3, *q.shape[1:]), bf16) for q in parts], dep,
                            "reduce_chips_start_" + tag)

    g_a, dz, db_in = _mm_dwin_parts(s["hb"], u["dz_parts"])
    shards = lambda g: g.reshape(N_DEV, SHARD_IN, W_IN_HALF)
    sib_a = siblings_start([shards(g_a), u["g_wrg"].reshape(N_DEV, 2 * RNN_BLOCK, RNN_BLOCK)], db_in, "a")
    g_proj, g_land = _split_wait(_copies_direct(False), *g_pending[:4], sib_a[4], "reduce_proj_wait")
    for i, name in enumerate(proj):
        res = _adamw_direct(g_proj[i], g_land[i], me_idx, w[name].reshape(two_d(name)), m[name].reshape(two_d(name)),
                            v[name].reshape(two_d(name)), "adamw_" + name)
        big[name] = tuple(r.reshape(w[name].shape) for r in res)
    chp_a = chips_start(*_split_wait(_copies_siblings, *sib_a[:4], big["w_attn_out"][3], "reduce_siblings_wait_a"),
                        db_in, "a")
    g_b = _mm_dwin(s["hb"], dz, chp_a[4])
    sib_b = siblings_start([shards(g_b)], db_in, "b")
    sm_e = _pack_early(u["vec_rnn"], t["st_out"], u["dsr"], db_in)
    early = _split_start(_copies_direct(True), 7, list(sm_e),
                         [lax.empty((N_DEV, *a.shape), f32) for a in sm_e], sib_b[4], "small_early_start")
    dh_lo = _mm_dh(dz, w_full, early[4], 0)
    chp_b = chips_start(*_split_wait(_copies_siblings, *sib_b[:4], dh_lo, "reduce_siblings_wait_b"), db_in, "b")
    dh_hi = _mm_dh(dz, w_full, chp_b[4], 1)
    parts_a, r2_a = _split_wait(_copies_chips, *chp_a[:4], dh_hi, "reduce_chips_wait_a")
    w_in_res = _adamw_big(parts_a[0], r2_a[0], q_idx, w_in_t(w["w_in"]), w_in_t(m["w_in"]), w_in_t(v["w_in"]),
                          "adamw_w_in_a", cols=(0, 2))
    u.update(_step_input_grad(dh_lo, dh_hi, t["du32"], x, smallw, p, w_in_res[3]))
    sm_l = _pack_late(u["st_emb"], u["dmeta"])
    late_x = _split_start(_copies_direct(True), 7, list(sm_l), [lax.empty((N_DEV, *a.shape), f32) for a in sm_l],
                          p["b_o"], "small_late_start")
    parts_b, r2_b = _split_wait(_copies_chips, *chp_b[:4], late_x[4], "reduce_chips_wait_b")
    res = _adamw_big(parts_b[0], r2_b[0], q_idx, w_in_t(w["w_in"]), w_in_t(m["w_in"]), w_in_t(v["w_in"]),
                     "adamw_w_in_b", cols=(1, 2), prev=w_in_res)
    big["w_in"] = tuple(jnp.swapaxes(r.reshape(1, SHARD_IN, D), 1, 2) for r in res)
    (sm_own, conv_own), (sm_land, conv_land) = _split_wait(_copies_direct(True), *early[:4], res[3], "small_early_wait")
    (l_own, meta_own), (l_land, meta_land) = _split_wait(_copies_direct(True), *late_x[:4], sm_land, "small_late_wait")
    me = _dev(px, py, pc)
    mine = lambda a, axis: lax.dynamic_index_in_dim(a, me, axis, keepdims=False)
    two = lambda name, t: t.reshape(_SMALL_2D.get(name, (1, D)))
    small, loss = _small_update(me_idx, (sm_own, sm_land, mine(conv_own, 0), mine(conv_land, 1)),
                                (l_own, l_land, mine(meta_own, 0), mine(meta_land, 1)),
                                {k: (two(k, w[k]), two(k, m[k]), two(k, v[k])) for k in _SMALL_NAMES})
    for i, name in enumerate(("w_ra", "w_ri")):
        sq = (RNN_BLOCK, RNN_BLOCK)
        res = _adamw_big(parts_a[1], r2_a[1], q_idx, w[name].reshape(sq), m[name].reshape(sq), v[name].reshape(sq),
                         "adamw_" + name, row_off=i)
        big[name] = tuple(r.reshape(w[name].shape) for r in res)
    res = dict(big)
    for k in _SMALL_NAMES:
        res[k] = tuple(t.reshape(w[k].shape) for t in small[k])

    outs = [loss, u["grad_x"]]
    for j in range(4):
        outs += [res[k][j] for k in _WEIGHTS]
    return tuple(outs)
```

```python
import jax
import jax.numpy as jnp
from jax import lax
from jax.experimental import pallas as pl
from jax.experimental.pallas import tpu as pltpu

f32, bf16 = jnp.float32, jnp.bfloat16
SDS = jax.ShapeDtypeStruct
HBM = pltpu.HBM

N_DEV = 8
D = 2048
N_META = 16
BLK = 128
ROW0 = BLK - N_META
N_RNN_BLOCKS = 8
RNN_BLOCK = D // N_RNN_BLOCKS
CONV_WIDTH = 4
LRU_C = 8.0
HEAD_DIM = 64
N_KV = 4
GROUP = 8
HALF = HEAD_DIM // 2
ROPE_THETA = 10000.0
NEG_INF = -1e30
LN_EPS = 1e-5
ALPHA = 2.0 ** 0.25
D_IN = 12800
SHARD_IN = D_IN // N_DEV
W_IN_HALF = D // 2
W_IN_LATE = 640
W_IN_EARLY = SHARD_IN - W_IN_LATE
OFF_GR, OFF_Q, OFF_K, OFF_V, OFF_GA, OFF_G = 2048, 4096, 6144, 6400, 6656, 8704
ADAM_LR, ADAM_B1, ADAM_B2, ADAM_EPS, ADAM_WD, ADAM_STEP = 1e-3, 0.9, 0.999, 1e-8, 0.01, 10
VMEM_LIMIT_MB = 56
MESH = pl.DeviceIdType.MESH


def _cp(sem=None, vmem_mb=40):
    return pltpu.CompilerParams(dimension_semantics=sem, vmem_limit_bytes=vmem_mb * 2 ** 20)


def _row_chunk(m):
    best = 16
    for c in range(16, 641, 16):
        if m % c == 0:
            best = c
    return best


def _sigmoid(x):
    return 1.0 / (1.0 + jnp.exp(-x))


def _silu_and_grad(x):
    s = _sigmoid(x)
    return x * s, s * (1.0 + x * (1.0 - s))


def _log_sigmoid(x):
    return jnp.minimum(x, 0.0) - jnp.log1p(jnp.exp(-jnp.abs(x)))


def _ln_rows(v, g, b):
    mu = jnp.mean(v, axis=-1, keepdims=True)
    c = v - mu
    var = jnp.mean(c * c, axis=-1, keepdims=True)
    rstd = lax.rsqrt(var + LN_EPS)
    xhat = c * rstd
    return xhat * g + b, xhat, rstd


def _ln_rows_bwd(dy, g, xhat, rstd):
    dxh = dy * g
    m1 = jnp.mean(dxh, axis=-1, keepdims=True)
    m2 = jnp.mean(dxh * xhat, axis=-1, keepdims=True)
    return rstd * (dxh - m1 - xhat * m2)


def _colsum(v):
    return jnp.sum(v, axis=0, keepdims=True)


def _dot(a, b):
    return jnp.dot(a, b, preferred_element_type=f32)


def _dot_nt(a, b):
    return lax.dot_general(a, b, (((1,), (1,)), ((), ())), preferred_element_type=f32)


def _dot_tn(a, b):
    return lax.dot_general(a, b, (((0,), (0,)), ((), ())), preferred_element_type=f32)


def _meta_full(sw_ref):
    return jnp.concatenate([sw_ref[s, 0:N_META, :] for s in range(N_DEV)], axis=1)


def _ln_emb(x, smallw, g_e, b_e):
    seq = x.shape[1]
    rows = seq + BLK
    nb = rows // BLK

    def body(x_ref, sw_ref, g_ref, b_ref, h32_ref, hb_ref):
        i = pl.program_id(0)
        g, b = g_ref[...], b_ref[...]

        def emit(blk):
            h32_ref[...] = blk
            hb_ref[...] = blk.astype(bf16)

        @pl.when(i == 0)
        def _():
            hm = _ln_rows(_meta_full(sw_ref), g, b)[0]
            emit(jnp.concatenate([jnp.zeros((ROW0, D), f32), hm], axis=0))

        @pl.when(i > 0)
        def _():
            emit(_ln_rows(x_ref[0], g, b)[0])

    return pl.pallas_call(
        body, grid=(nb,),
        in_specs=[pl.BlockSpec((1, BLK, D), lambda i: (0, jnp.maximum(i - 1, 0), 0)),
                  pl.BlockSpec((N_DEV, 24, 256), lambda i: (0, 0, 0)),
                  pl.BlockSpec((1, D), lambda i: (0, 0)),
                  pl.BlockSpec((1, D), lambda i: (0, 0))],
        out_specs=[pl.BlockSpec((BLK, D), lambda i: (i, 0)),
                   pl.BlockSpec((BLK, D), lambda i: (i, 0))],
        out_shape=[SDS((rows, D), f32), HBM((rows, D), bf16)],
        name="ln_emb", compiler_params=_cp(("arbitrary",)),
    )(x, smallw, g_e, b_e)


def _ln_emb_bwd(dh_lo, dh_hi, du32, x, smallw, g_e, after):
    seq = x.shape[1]
    rows = seq + BLK
    nb = rows // BLK

    def body(dlo_ref, dhi_ref, du_ref, x_ref, sw_ref, g_ref, after_ref, gx_ref, dmeta_ref, st_ref):
        i = pl.program_id(0)
        g = g_ref[...]
        dht = jnp.concatenate([dlo_ref[...], dhi_ref[...]], axis=1) + ALPHA * du_ref[...]

        @pl.when(i == 0)
        def _():
            v = jnp.concatenate([jnp.zeros((ROW0, D), f32), _meta_full(sw_ref)], axis=0)
            valid = lax.broadcasted_iota(jnp.int32, (BLK, 1), 0) >= ROW0
            d = jnp.where(valid, dht, 0.0)
            _, xhat, rstd = _ln_rows(v, g, 0.0)
            dv = _ln_rows_bwd(d, g, xhat, rstd)
            dmeta_ref[...] = dv[ROW0:, :]
            st_ref[...] = jnp.concatenate([_colsum(d * xhat), _colsum(d), jnp.zeros((6, D), f32)], axis=0)

        @pl.when(i > 0)
        def _():
            _, xhat, rstd = _ln_rows(x_ref[0], g, 0.0)
            gx_ref[0] = _ln_rows_bwd(dht, g, xhat, rstd)
            st_ref[0:1, :] += _colsum(dht * xhat)
            st_ref[1:2, :] += _colsum(dht)

    return pl.pallas_call(
        body, grid=(nb,),
        in_specs=[pl.BlockSpec((BLK, W_IN_HALF), lambda i: (i, 0)),
                  pl.BlockSpec((BLK, W_IN_HALF), lambda i: (i, 0)),
                  pl.BlockSpec((BLK, D), lambda i: (i, 0)),
                  pl.BlockSpec((1, BLK, D), lambda i: (0, jnp.maximum(i - 1, 0), 0)),
                  pl.BlockSpec((N_DEV, 24, 256), lambda i: (0, 0, 0)),
                  pl.BlockSpec((1, D), lambda i: (0, 0)),
                  pl.BlockSpec(memory_space=pl.ANY)],
        out_specs=[pl.BlockSpec((1, BLK, D), lambda i: (0, jnp.maximum(i - 1, 0), 0)),
                   pl.BlockSpec((N_META, D), lambda i: (0, 0)),
                   pl.BlockSpec((8, D), lambda i: (0, 0))],
        out_shape=[SDS((1, seq, D), f32), SDS((N_META, D), f32), SDS((8, D), f32)],
        name="ln_emb_bwd", compiler_params=_cp(("arbitrary",)),
    )(dh_lo, dh_hi, du32, x, smallw, g_e, after)


def _mm(a, b, *, name, nt=False, sel=None, bias=None, out_dtype=f32, tn=512):
    m, k = a.shape
    cm = _row_chunk(m)
    stacked = sel is not None
    n = D if stacked else (b.shape[0] if nt else b.shape[1])
    am = m
    if stacked and nt:
        b_spec = pl.BlockSpec((tn // 256, None, 256, D), lambda j, i: (j, sel, 0, 0))
    elif stacked:
        b_spec = pl.BlockSpec((N_DEV, None, 256, tn), lambda j, i: (0, sel, 0, j))
    elif nt:
        b_spec = pl.BlockSpec((tn, k), lambda j, i: (j, 0))
    else:
        b_spec = pl.BlockSpec((k, tn), lambda j, i: (0, j))
    in_specs = [pl.BlockSpec((am, k), lambda j, i: (i, 0)), b_spec]
    args = [a, b]
    if bias is not None:
        in_specs.append(pl.BlockSpec((1, tn), lambda j, i: (0, j)))
        args.append(bias)

    def body(*refs):
        a_ref, b_ref, o_ref = refs[0], refs[1], refs[-1]
        bm = b_ref[...]
        if stacked:
            bm = bm.reshape((tn, D) if nt else (D, tn))
        for c in range(am // cm):
            acc = (_dot_nt if nt else _dot)(a_ref[c * cm:(c + 1) * cm, :], bm)
            if bias is not None:
                acc = acc + refs[2][...]
            o_ref[c * cm:(c + 1) * cm, :] = acc.astype(out_dtype)

    return pl.pallas_call(
        body, grid=(n // tn, m // am), in_specs=in_specs,
        out_specs=pl.BlockSpec((am, tn), lambda j, i: (i, j)),
        out_shape=HBM((m, n), out_dtype), name=name, compiler_params=_cp(("arbitrary", "arbitrary"), 48),
    )(*args)


def _mm_z(hb, w_t, bias, side, name, z_prev=None):
    rows, k = hb.shape
    tn = W_IN_LATE
    cm = _row_chunk(rows)
    per = 2 * SHARD_IN // tn
    if side is None:
        side, count = jnp.zeros((1,), jnp.int32), per - 2
        tile = lambda q, t, s_ref: per * q + 1 + t
    else:
        count = 1
        tile = lambda q, t, s_ref: per * q + (per - 1) * s_ref[0]

    def body(s_ref, a_ref, b_ref, bias_ref, *rest):
        o_ref = rest[-1]
        for c in range(rows // cm):
            o_ref[c * cm:(c + 1) * cm, :] = _dot_nt(a_ref[c * cm:(c + 1) * cm, :], b_ref[...]) + bias_ref[...]

    in_specs = [pl.BlockSpec((rows, k), lambda q, t, s_ref: (0, 0)),
                pl.BlockSpec((tn, k), lambda q, t, s_ref: (tile(q, t, s_ref), 0)),
                pl.BlockSpec((1, tn), lambda q, t, s_ref: (0, tile(q, t, s_ref)))]
    args = [side, hb, w_t, bias]
    if z_prev is not None:
        in_specs.append(pl.BlockSpec(memory_space=pl.ANY))
        args.append(z_prev)
    return pl.pallas_call(
        body,
        grid_spec=pltpu.PrefetchScalarGridSpec(
            num_scalar_prefetch=1, grid=(N_DEV // 2, count), in_specs=in_specs,
            out_specs=pl.BlockSpec((rows, tn), lambda q, t, s_ref: (0, tile(q, t, s_ref)))),
        out_shape=SDS((rows, D_IN), f32), name=name,
        input_output_aliases={} if z_prev is None else {4: 0},
        compiler_params=_cp(("arbitrary", "arbitrary"), 48),
    )(*args)


def _mm_dh(dz, w_t, after, half):
    rows = dz.shape[0]
    tn = 512
    nt = W_IN_HALF // tn
    cm = _row_chunk(rows) // 2

    def body(a_ref, w_ref, after_ref, o_ref):
        o_ref[...] = _dot(a_ref[...], w_ref[...])

    return pl.pallas_call(
        body, grid=(nt, rows // cm),
        in_specs=[pl.BlockSpec((cm, D_IN), lambda j, i: (i, 0)),
                  pl.BlockSpec((D_IN, tn), lambda j, i: (0, half * nt + j)),
                  pl.BlockSpec(memory_space=pl.ANY)],
        out_specs=pl.BlockSpec((cm, tn), lambda j, i: (i, j)),
        out_shape=SDS((rows, W_IN_HALF), f32), name="mm_dh_%d" % half,
        compiler_params=_cp(("arbitrary", "arbitrary"), 48),
    )(dz, w_t, after)


def _mm_dwin_parts(hb, parts):
    rows = hb.shape[0]
    tc = 512
    edges = [0]
    for _, w in parts:
        edges.append(edges[-1] + w // tc)

    def body(*refs):
        h_ref, (o_ref, dz_ref, db_ref) = refs[len(parts)], refs[len(parts) + 1:]
        j = pl.program_id(0)
        for p_ref, lo, hi in zip(refs, edges[:-1], edges[1:]):
            @pl.when((j >= lo) & (j < hi))
            def _():
                o_ref[...] = _dot_tn(p_ref[...], h_ref[...]).astype(bf16)
                dz_ref[...] = p_ref[...]

                def step(i, s):
                    blk = p_ref[pl.ds(pl.multiple_of(i * BLK, BLK), BLK), :].astype(f32)
                    return s + blk.reshape(BLK // 8, 8, tc).sum(axis=0)
                s = lax.fori_loop(0, rows // BLK, step, jnp.zeros((8, tc), f32))
                db_ref[...] = jnp.broadcast_to(_colsum(s), (8, tc))

    in_specs = [pl.BlockSpec((rows, tc), lambda j, lo=lo, hi=hi: (0, jnp.clip(j - lo, 0, hi - lo - 1)))
                for lo, hi in zip(edges[:-1], edges[1:])]
    return pl.pallas_call(
        body, grid=(D_IN // tc,),
        in_specs=in_specs + [pl.BlockSpec((rows, W_IN_HALF), lambda j: (0, 0))],
        out_specs=[pl.BlockSpec((tc, W_IN_HALF), lambda j: (j, 0)), pl.BlockSpec((rows, tc), lambda j: (0, j)),
                   pl.BlockSpec((8, tc), lambda j: (0, j))],
        out_shape=[SDS((D_IN, W_IN_HALF), bf16), SDS((rows, D_IN), bf16), SDS((8, D_IN), f32)],
        name="mm_dwin_0", compiler_params=_cp(("arbitrary",), VMEM_LIMIT_MB),
    )(*[a for a, _ in parts], hb)


def _mm_dwin(hb, dz, after):
    rows = dz.shape[0]
    tc = 640

    def body(dz_ref, h_ref, after_ref, o_ref):
        o_ref[...] = _dot_tn(dz_ref[...], h_ref[...]).astype(bf16)

    return pl.pallas_call(
        body, grid=(D_IN // tc,),
        in_specs=[pl.BlockSpec((rows, tc), lambda j: (0, j)),
                  pl.BlockSpec((rows, W_IN_HALF), lambda j: (0, 1)),
                  pl.BlockSpec(memory_space=pl.ANY)],
        out_specs=pl.BlockSpec((tc, W_IN_HALF), lambda j: (j, 0)),
        out_shape=SDS((D_IN, W_IN_HALF), bf16),
        name="mm_dwin_1", compiler_params=_cp(("arbitrary",), 48),
    )(dz, hb, after)


SCAN_ROWS = 32


def _scan8(a, b, reverse):
    idx = lax.broadcasted_iota(jnp.int32, a.shape, 0)
    for s in (1, 2, 4):
        sh = 8 - s if reverse else s
        a_sh, b_sh = pltpu.roll(a, sh, 0), pltpu.roll(b, sh, 0)
        m = (idx < 8 - s) if reverse else (idx >= s)
        b = jnp.where(m, a * b_sh + b, b)
        a = jnp.where(m, a * a_sh, a)
    return a, b


def _shift_rows(prev8, cur, k):
    ext = jnp.concatenate([prev8, cur], axis=0)
    return pltpu.roll(ext, k, 0)[8:, :]


def _gates(xc, w_ra, b_ra, w_ri, b_ri, ls):
    xb = xc.astype(bf16)
    r = _sigmoid(_dot(xb, w_ra) + b_ra)
    ig = _sigmoid(_dot(xb, w_ri) + b_ri)
    la = LRU_C * r * ls
    a = jnp.exp(la)
    mult = jnp.sqrt(jnp.tanh(-la) * (1.0 + a * a))
    return xb, r, ig, a, mult


_RNN_IN_SPECS = lambda rows: [
    pl.BlockSpec((1, 24, 256), lambda n: (n, 0, 0)),
    pl.BlockSpec((1, RNN_BLOCK), lambda n: (0, n)),
    pl.BlockSpec((N_DEV, 2, None, 32, RNN_BLOCK), lambda n: (0, 0, n, 0, 0)),
    pl.BlockSpec((1, RNN_BLOCK), lambda n: (0, n)),
    pl.BlockSpec((1, RNN_BLOCK), lambda n: (0, n)),
    pl.BlockSpec((1, RNN_BLOCK), lambda n: (0, n)),
]


def _rnn_fwd(z, smallw, conv_b, wrg, b_ra, b_ri, lam):
    rows = z.shape[0]
    nb = rows // BLK
    col = lambda off: pl.BlockSpec((rows, RNN_BLOCK), lambda n: (0, off // RNN_BLOCK + n))

    def body(xr_ref, gr_ref, sw_ref, cb_ref, w_ref, bra_ref, bri_ref, lam_ref, xc_ref, hr_ref, ya_ref, yat_ref, a_s):
        cw = sw_ref[0, N_META:24, :]
        cb = cb_ref[...]
        w_ra = w_ref[:, 0].reshape(RNN_BLOCK, RNN_BLOCK)
        w_ri = w_ref[:, 1].reshape(RNN_BLOCK, RNN_BLOCK)
        b_ra_v, b_ri_v = bra_ref[...], bri_ref[...]
        ls = _log_sigmoid(lam_ref[...])
        rid = lax.broadcasted_iota(jnp.int32, (BLK, 1), 0)

        def blk_step(i, carry):
            r0 = pl.multiple_of(i * BLK, BLK)
            grow = rid + r0
            valid = grow >= ROW0
            cur = jnp.where(valid, xr_ref[pl.ds(r0, BLK), :], 0.0)
            prev8 = xr_ref[pl.ds(pl.multiple_of(jnp.maximum(r0 - 8, 0), 8), 8), :] * (i > 0).astype(f32)
            xc = cb + cw[0:1] * cur
            for k in range(1, CONV_WIDTH):
                xc = xc + cw[k:k + 1] * _shift_rows(prev8, cur, k)
            xc_ref[pl.ds(r0, BLK), :] = xc
            _, _, ig, a, mult = _gates(xc, w_ra, b_ra_v, w_ri, b_ri_v, ls)
            mult = jnp.where(grow == ROW0, 1.0, mult)
            a_s[pl.ds(r0, BLK), :] = a
            hr_ref[pl.ds(r0, BLK), :] = jnp.where(valid, mult * ig * xc, 0.0)
            return carry

        lax.fori_loop(0, nb, blk_step, 0)

        def scan_step(j, carry):
            r0 = pl.multiple_of(j * SCAN_ROWS, SCAN_ROWS)
            tiles = [_scan8(a_s[pl.ds(r0 + 8 * k, 8), :], hr_ref[pl.ds(r0 + 8 * k, 8), :], False)
                     for k in range(SCAN_ROWS // 8)]
            for k, (a, b) in enumerate(tiles):
                h = b + a * carry
                hr_ref[pl.ds(r0 + 8 * k, 8), :] = h
                carry = jnp.broadcast_to(h[7:8, :], (8, RNN_BLOCK))
            return carry

        lax.fori_loop(0, rows // SCAN_ROWS, scan_step, jnp.zeros((8, RNN_BLOCK), f32))

        def gate_step(i, carry):
            r0 = pl.multiple_of(i * BLK, BLK)
            ya_ref[pl.ds(r0, BLK), :] = (hr_ref[pl.ds(r0, BLK), :]
                                         * _silu_and_grad(gr_ref[pl.ds(r0, BLK), :])[0]).astype(bf16)
            return carry

        lax.fori_loop(0, nb, gate_step, 0)
        yat_ref[...] = ya_ref[...].astype(f32).T.astype(bf16)

    return pl.pallas_call(
        body, grid=(N_RNN_BLOCKS,),
        in_specs=[col(0), col(OFF_GR)] + _RNN_IN_SPECS(rows),
        out_specs=[pl.BlockSpec((rows, RNN_BLOCK), lambda n: (0, n))] * 3
                  + [pl.BlockSpec((RNN_BLOCK, rows), lambda n: (n, 0))],
        out_shape=[SDS((rows, D), f32), SDS((rows, D), f32), HBM((rows, D), bf16), SDS((D, rows), bf16)],
        scratch_shapes=[pltpu.VMEM((rows, RNN_BLOCK), f32)],
        name="rnn_fwd", compiler_params=_cp(("arbitrary",)),
    )(z, z, smallw, conv_b, wrg, b_ra, b_ri, lam)


def _rnn_bwd(dya, hr, xc, z, smallw, conv_b, wrg, b_ra, b_ri, lam):
    rows = z.shape[0]
    nb = rows // BLK
    col = lambda off: pl.BlockSpec((rows, RNN_BLOCK), lambda n: (0, off // RNN_BLOCK + n))
    blk = pl.BlockSpec((rows, RNN_BLOCK), lambda n: (0, n))

    def body(dya_ref, hr_ref, xc_ref, xr_ref, gr_ref, sw_ref, cb_ref, w_ref, bra_ref, bri_ref, lam_ref,
             dxr_ref, dgr_ref, dw_ref, vec_ref, a_s, lam_s, dxc_s, r_s, ig_s, mult_s, dw_s):
        cw = sw_ref[0, N_META:24, :]
        w_ra = w_ref[:, 0].reshape(RNN_BLOCK, RNN_BLOCK)
        w_ri = w_ref[:, 1].reshape(RNN_BLOCK, RNN_BLOCK)
        b_ra_v, b_ri_v = bra_ref[...], bri_ref[...]
        lam_v = lam_ref[...]
        ls = _log_sigmoid(lam_v)
        rid = lax.broadcasted_iota(jnp.int32, (BLK, 1), 0)
        zrow = jnp.zeros((1, RNN_BLOCK), f32)

        def p1(i, carry):
            r0 = pl.multiple_of(i * BLK, BLK)
            sl = pl.ds(r0, BLK)
            _, r, ig, a, mult = _gates(xc_ref[sl, :], w_ra, b_ra_v, w_ri, b_ri_v, ls)
            a_s[sl, :] = a
            r_s[sl, :] = r
            ig_s[sl, :] = ig
            mult_s[sl, :] = mult
            sg, dsg = _silu_and_grad(gr_ref[sl, :])
            d = dya_ref[sl, :]
            lam_s[sl, :] = d * sg
            dgr_ref[sl, :] = (d * hr_ref[sl, :] * dsg).astype(bf16)
            return carry

        lax.fori_loop(0, nb, p1, 0)

        def p2(jj, carry):
            r0 = pl.multiple_of((rows // SCAN_ROWS - 1 - jj) * SCAN_ROWS, SCAN_ROWS)
            idx = lax.broadcasted_iota(jnp.int32, (8, RNN_BLOCK), 0)
            tiles = []
            for k in range(SCAN_ROWS // 8):
                sl = pl.ds(r0 + 8 * k, 8)
                a, g = a_s[sl, :], lam_s[sl, :]
                tiles.append((g, *_scan8(a, a * g, True)))
            for k in reversed(range(SCAN_ROWS // 8)):
                g, ca, cb_ = tiles[k]
                mu = cb_ + ca * carry
                lam_s[pl.ds(r0 + 8 * k, 8), :] = g + jnp.where(idx < 7, pltpu.roll(mu, 7, 0), carry)
                carry = jnp.broadcast_to(mu[0:1, :], (8, RNN_BLOCK))
            return carry

        lax.fori_loop(0, rows // SCAN_ROWS, p2, jnp.zeros((8, RNN_BLOCK), f32))

        dw_s[...] = jnp.zeros_like(dw_s)

        def p3(i, carry):
            d_bra, d_bri, d_ls = carry
            r0 = pl.multiple_of(i * BLK, BLK)
            sl = pl.ds(r0, BLK)
            grow = rid + r0
            valid = grow >= ROW0
            first = grow == ROW0
            xcv = xc_ref[sl, :]
            xb = xcv.astype(bf16)
            r, ig, a = r_s[sl, :], ig_s[sl, :], a_s[sl, :]
            mult = jnp.where(first, 1.0, mult_s[sl, :])
            lam_t = lam_s[sl, :]
            du = jnp.where(valid, lam_t, 0.0)
            hprev = _shift_rows(hr_ref[pl.ds(pl.multiple_of(jnp.maximum(r0 - 8, 0), 8), 8), :] * (i > 0).astype(f32), hr_ref[sl, :], 1)
            da = lam_t * hprev
            dmult = jnp.where(first, 0.0, du * ig * xcv)
            di = du * mult * xcv
            dxc = du * mult * ig
            ratio = jnp.where(valid & jnp.logical_not(first), a * a / mult, 0.0)
            dla = da * a - dmult * ratio
            dpr = (dla * (LRU_C * ls)) * r * (1.0 - r)
            dpi = di * ig * (1.0 - ig)
            dprb, dpib = dpr.astype(bf16), dpi.astype(bf16)
            dw_s[0] += _dot_tn(xb, dprb)
            dw_s[1] += _dot_tn(xb, dpib)
            dxc_s[sl, :] = dxc + _dot_nt(dprb, w_ra) + _dot_nt(dpib, w_ri)
            return d_bra + _colsum(dpr), d_bri + _colsum(dpi), d_ls + _colsum(dla * (LRU_C * r))

        d_bra, d_bri, d_ls = lax.fori_loop(0, nb, p3, (zrow, zrow, zrow))

        def p4(i, carry):
            d_cb, d_w0, d_w1, d_w2, d_w3 = carry
            r0 = pl.multiple_of(i * BLK, BLK)
            sl = pl.ds(r0, BLK)
            grow = rid + r0
            valid = grow >= ROW0
            dxc = dxc_s[sl, :]
            nxt = dxc_s[pl.ds(pl.multiple_of(jnp.minimum(r0 + BLK, rows - 8), 8), 8), :] * (i < nb - 1).astype(f32)
            ext = jnp.concatenate([dxc, nxt], axis=0)
            dxr = cw[0:1] * dxc
            for k in range(1, CONV_WIDTH):
                dxr = dxr + cw[k:k + 1] * pltpu.roll(ext, BLK + 8 - k, 0)[:BLK, :]
            dxr_ref[sl, :] = jnp.where(valid, dxr, 0.0).astype(bf16)
            cur = jnp.where(valid, xr_ref[sl, :], 0.0)
            prev8 = xr_ref[pl.ds(pl.multiple_of(jnp.maximum(r0 - 8, 0), 8), 8), :] * (i > 0).astype(f32)
            dws = [d_w0 + _colsum(dxc * cur)]
            for k, acc in ((1, d_w1), (2, d_w2), (3, d_w3)):
                dws.append(acc + _colsum(dxc * _shift_rows(prev8, cur, k)))
            return (d_cb + _colsum(dxc), *dws)

        d_cb, d_w0, d_w1, d_w2, d_w3 = lax.fori_loop(0, nb, p4, (zrow,) * 5)

        d_lam = d_ls * _sigmoid(-lam_v)
        vec_ref[...] = jnp.concatenate([d_bra, d_bri, d_lam, d_cb, d_w0, d_w1, d_w2, d_w3], axis=0)
        dw_ref[:, 0] = dw_s[0].astype(bf16).reshape(N_DEV, 32, RNN_BLOCK)
        dw_ref[:, 1] = dw_s[1].astype(bf16).reshape(N_DEV, 32, RNN_BLOCK)

    return pl.pallas_call(
        body, grid=(N_RNN_BLOCKS,),
        in_specs=[blk, blk, blk, col(0), col(OFF_GR)] + _RNN_IN_SPECS(rows),
        out_specs=[blk, blk,
                   pl.BlockSpec((N_DEV, 2, None, 32, RNN_BLOCK), lambda n: (0, 0, n, 0, 0)),
                   pl.BlockSpec((8, RNN_BLOCK), lambda n: (0, n))],
        out_shape=[SDS((rows, D), bf16), SDS((rows, D), bf16),
                   SDS((N_DEV, 2, N_RNN_BLOCKS, 32, RNN_BLOCK), bf16), SDS((8, D), f32)],
        scratch_shapes=[pltpu.VMEM((rows, RNN_BLOCK), f32)] * 6 + [pltpu.VMEM((2, RNN_BLOCK, RNN_BLOCK), f32)],
        name="rnn_bwd", compiler_params=_cp(("arbitrary",), 48),
    )(dya, hr, xc, z, z, smallw, conv_b, wrg, b_ra, b_ri, lam)


def _rope_tables(rows):
    half = jnp.arange(HALF, dtype=f32)
    inv = ROPE_THETA ** (-half / HALF)
    pos = (jnp.arange(rows) - ROW0).astype(f32)
    ang = pos[:, None] * inv[None, :]
    cos, sin = jnp.cos(ang), jnp.sin(ang)
    cos128 = jnp.concatenate([cos, cos, cos, cos], axis=1)
    sin128 = jnp.concatenate([-sin, sin, -sin, sin], axis=1)
    return cos128, sin128


def _rope128(x, cos128, sin128):
    lane = lax.broadcasted_iota(jnp.int32, x.shape, 1)
    swapped = jnp.where(lane % HEAD_DIM < HALF, pltpu.roll(x, 128 - HALF, 1), pltpu.roll(x, HALF, 1))
    return x * cos128 + swapped * sin128


def _qkv_prep(z, cos128, sin128):
    rows = z.shape[0]

    def body(q_ref, kv_ref, c_ref, s_ref, qo_ref, ko_ref, vo_ref):
        c, s = c_ref[...], s_ref[...]
        for g in range(D // 128):
            qo_ref[:, g * 128:(g + 1) * 128] = (_rope128(q_ref[:, g * 128:(g + 1) * 128], c, s)
                                                * (HEAD_DIM ** -0.5)).astype(bf16)
        for g in range(2):
            kr = _rope128(kv_ref[:, g * 128:(g + 1) * 128], c, s)
            for j in range(2):
                ko_ref[2 * g + j] = kr[:, j * HEAD_DIM:(j + 1) * HEAD_DIM].astype(bf16)
        for h in range(N_KV):
            vo_ref[h] = kv_ref[:, 256 + h * HEAD_DIM:256 + (h + 1) * HEAD_DIM].astype(bf16)

    return pl.pallas_call(
        body, grid=(rows // BLK,),
        in_specs=[pl.BlockSpec((BLK, D), lambda i: (i, OFF_Q // D)),
                  pl.BlockSpec((BLK, 512), lambda i: (i, OFF_K // 512)),
                  pl.BlockSpec((BLK, 128), lambda i: (i, 0)),
                  pl.BlockSpec((BLK, 128), lambda i: (i, 0))],
        out_specs=[pl.BlockSpec((BLK, D), lambda i: (i, 0)),
                   pl.BlockSpec((N_KV, BLK, HEAD_DIM), lambda i: (0, i, 0)),
                   pl.BlockSpec((N_KV, BLK, HEAD_DIM), lambda i: (0, i, 0))],
        out_shape=[SDS((rows, D), bf16), SDS((N_KV, rows, HEAD_DIM), bf16), SDS((N_KV, rows, HEAD_DIM), bf16)],
        name="qkv_prep", compiler_params=_cp(("arbitrary",)),
    )(z, z, cos128, sin128)


def _attn_mask(n):
    qi = n * BLK + lax.broadcasted_iota(jnp.int32, (BLK, 2 * BLK + N_META), 0)
    c = lax.broadcasted_iota(jnp.int32, (BLK, 2 * BLK + N_META), 1)
    jb = (n - 1) * BLK + c
    band = (jb >= BLK) & (jb <= qi) & (qi - jb < BLK)
    meta = (ROW0 + c - 2 * BLK) <= qi
    return ((c < 2 * BLK) & band) | ((c >= 2 * BLK) & meta)


N_KEYS = 2 * BLK + N_META


def _stack_heads(t):
    return jnp.concatenate([t[:, g * HEAD_DIM:(g + 1) * HEAD_DIM] for g in range(GROUP)], axis=0)


def _sink_column(sink_ref, h):
    g = lax.broadcasted_iota(jnp.int32, (GROUP, 1, 1), 0)
    col = jnp.zeros((GROUP, 1, 1), f32)
    for j in range(GROUP):
        col = jnp.where(g == j, sink_ref[h * GROUP + j], col)
    return col


def _kv_specs(last):
    cl = lambda n: jnp.minimum(n, last)
    return [pl.BlockSpec((None, N_META, HEAD_DIM), lambda h, n: (h, ROW0 // N_META, 0)),
            pl.BlockSpec((None, BLK, HEAD_DIM), lambda h, n: (h, jnp.maximum(cl(n) - 1, 0), 0)),
            pl.BlockSpec((None, BLK, HEAD_DIM), lambda h, n: (h, cl(n), 0))]


def _attn_fwd(q_r, k_r, v_b, z, sinks):
    rows = q_r.shape[0]
    nb = rows // BLK

    def body(sink_ref, q_ref, km_ref, kp_ref, kc_ref, vm_ref, vp_ref, vc_ref, ga_ref, o_ref, yb_ref, ybt_ref, lse_ref):
        h, n = pl.program_id(0), pl.program_id(1)
        kk = jnp.concatenate([kp_ref[...], kc_ref[...], km_ref[...]], axis=0)
        vv = jnp.concatenate([vp_ref[...], vc_ref[...], vm_ref[...]], axis=0)
        q2 = _stack_heads(q_ref[...])
        s = jnp.where(_attn_mask(n)[None], _dot_nt(q2, kk).reshape(GROUP, BLK, N_KEYS), NEG_INF)
        sink = _sink_column(sink_ref, h)
        m = jnp.maximum(jnp.max(s, axis=-1, keepdims=True), sink)
        p = jnp.exp(s - m)
        den = jnp.sum(p, axis=-1, keepdims=True) + jnp.exp(sink - m)
        o2 = _dot((p / den).astype(bf16).reshape(GROUP * BLK, N_KEYS), vv)
        lse = m + jnp.log(den)
        for g in range(GROUP):
            o_ref[:, g * HEAD_DIM:(g + 1) * HEAD_DIM] = o2[g * BLK:(g + 1) * BLK]
            lse_ref[:, g:g + 1] = lse[g]
        yb = o_ref[...] * _silu_and_grad(ga_ref[...])[0]
        yb_ref[...] = yb.astype(bf16)
        ybt_ref[...] = yb.T.astype(bf16)

    tile = pl.BlockSpec((BLK, 512), lambda h, n: (n, h))
    return pl.pallas_call(
        body, grid=(N_KV, nb),
        in_specs=[pl.BlockSpec(memory_space=pltpu.SMEM), tile] + _kv_specs(nb - 1) + _kv_specs(nb - 1)
                 + [pl.BlockSpec((BLK, 512), lambda h, n: (n, OFF_GA // 512 + h))],
        out_specs=[tile, tile, pl.BlockSpec((512, BLK), lambda h, n: (h, n)),
                   pl.BlockSpec((None, BLK, GROUP), lambda h, n: (h, n, 0))],
        out_shape=[SDS((rows, D), f32), SDS((rows, D), bf16), SDS((D, rows), bf16),
                   SDS((N_KV, rows, GROUP), f32)],
        name="attn_fwd", compiler_params=_cp(("arbitrary", "arbitrary")),
    )(sinks, q_r, k_r, k_r, k_r, v_b, v_b, v_b, z)


def _attn_bwd(dyb, o32, lse, q_r, k_r, v_b, z, sinks):
    rows = q_r.shape[0]
    nb = rows // BLK
    cl = lambda n: jnp.minimum(n, nb - 1)

    def body(sink_ref, dyb_ref, o_ref, lse_ref, q_ref, km_ref, kp_ref, kc_ref, vm_ref, vp_ref, vc_ref, ga_ref,
             dq_ref, dga_ref, dk_ref, dv_ref, dkm_ref, dvm_ref, dsr_ref, ck_s, cv_s):
        h, n = pl.program_id(0), pl.program_id(1)

        @pl.when(n == 0)
        def _():
            dkm_ref[...] = jnp.zeros_like(dkm_ref)
            dvm_ref[...] = jnp.zeros_like(dvm_ref)
            ck_s[...] = jnp.zeros_like(ck_s)
            cv_s[...] = jnp.zeros_like(cv_s)

        @pl.when(n < nb)
        def _():
            kk = jnp.concatenate([kp_ref[...], kc_ref[...], km_ref[...]], axis=0)
            vv = jnp.concatenate([vp_ref[...], vc_ref[...], vm_ref[...]], axis=0)
            sg, dsg = _silu_and_grad(ga_ref[...])
            dyb_v = dyb_ref[...]
            o_v = o_ref[...]
            dga_ref[...] = (dyb_v * o_v * dsg).astype(bf16)
            q2 = _stack_heads(q_ref[...])
            do2 = _stack_heads(dyb_v * sg)
            lse_v = lse_ref[...]
            lse = jnp.concatenate([lse_v[:, g:g + 1] for g in range(GROUP)], axis=0).reshape(GROUP, BLK, 1)
            delta = jnp.sum(do2 * _stack_heads(o_v), axis=-1, keepdims=True).reshape(GROUP, BLK, 1)
            s = jnp.where(_attn_mask(n)[None], _dot_nt(q2, kk).reshape(GROUP, BLK, N_KEYS), NEG_INF)
            p = jnp.exp(s - lse)
            do2b = do2.astype(bf16)
            ds = (p * (_dot_nt(do2b, vv).reshape(GROUP, BLK, N_KEYS) - delta)).astype(bf16)
            ds = ds.reshape(GROUP * BLK, N_KEYS)
            dsr = -jnp.exp(_sink_column(sink_ref, h) - lse) * delta
            dq2 = _dot(ds, kk)
            for g in range(GROUP):
                dq_ref[:, g * HEAD_DIM:(g + 1) * HEAD_DIM] = dq2[g * BLK:(g + 1) * BLK]
                dsr_ref[:, g:g + 1] = dsr[g]
            dkk = _dot_tn(ds, q2)
            dvv = _dot_tn(p.astype(bf16).reshape(GROUP * BLK, N_KEYS), do2b)
            dk_ref[...] = ck_s[...] + dkk[:BLK]
            dv_ref[...] = cv_s[...] + dvv[:BLK]
            ck_s[...] = dkk[BLK:2 * BLK]
            cv_s[...] = dvv[BLK:2 * BLK]
            dkm_ref[...] += dkk[2 * BLK:]
            dvm_ref[...] += dvv[2 * BLK:]

        @pl.when(n == nb)
        def _():
            dk_ref[...] = ck_s[...]
            dv_ref[...] = cv_s[...]

    tile = pl.BlockSpec((BLK, 512), lambda h, n: (cl(n), h))
    kvout = pl.BlockSpec((None, BLK, HEAD_DIM), lambda h, n: (h, jnp.maximum(n - 1, 0), 0))
    mout = pl.BlockSpec((None, N_META, HEAD_DIM), lambda h, n: (h, 0, 0))
    stat = pl.BlockSpec((None, BLK, GROUP), lambda h, n: (h, cl(n), 0))
    return pl.pallas_call(
        body, grid=(N_KV, nb + 1),
        in_specs=[pl.BlockSpec(memory_space=pltpu.SMEM), tile, tile, stat, tile] + _kv_specs(nb - 1)
                 + _kv_specs(nb - 1) + [pl.BlockSpec((BLK, 512), lambda h, n: (cl(n), OFF_GA // 512 + h))],
        out_specs=[tile, tile, kvout, kvout, mout, mout, stat],
        out_shape=[SDS((rows, D), f32), SDS((rows, D), bf16),
                   SDS((N_KV, rows, HEAD_DIM), f32), SDS((N_KV, rows, HEAD_DIM), f32),
                   SDS((N_KV, N_META, HEAD_DIM), f32), SDS((N_KV, N_META, HEAD_DIM), f32),
                   SDS((N_KV, rows, GROUP), f32)],
        scratch_shapes=[pltpu.VMEM((BLK, HEAD_DIM), f32), pltpu.VMEM((BLK, HEAD_DIM), f32)],
        name="attn_bwd", compiler_params=_cp(("arbitrary", "arbitrary")),
    )(sinks, dyb, o32, lse, q_r, k_r, k_r, k_r, v_b, v_b, v_b, z)


def _qkv_finish(dq, dk, dv, dkm, dvm, cos128, sin128):
    rows = dq.shape[0]

    def body(dq_ref, dk_ref, dv_ref, dkm_ref, dvm_ref, c_ref, s_ref, oq_ref, okv_ref):
        first = (pl.program_id(0) == 0).astype(f32)
        c, s = c_ref[...], -s_ref[...]
        for g in range(D // 128):
            oq_ref[:, g * 128:(g + 1) * 128] = (_rope128(dq_ref[:, g * 128:(g + 1) * 128], c, s)
                                                * (HEAD_DIM ** -0.5)).astype(bf16)
        pad = jnp.zeros((ROW0, HEAD_DIM), f32)
        ks = [dk_ref[h] + first * jnp.concatenate([pad, dkm_ref[h]], axis=0) for h in range(N_KV)]
        vs = [dv_ref[h] + first * jnp.concatenate([pad, dvm_ref[h]], axis=0) for h in range(N_KV)]
        for g in range(2):
            kp = jnp.concatenate([ks[2 * g], ks[2 * g + 1]], axis=1)
            okv_ref[:, g * 128:(g + 1) * 128] = _rope128(kp, c, s).astype(bf16)
            okv_ref[:, 256 + g * 128:256 + (g + 1) * 128] = jnp.concatenate([vs[2 * g], vs[2 * g + 1]], axis=1).astype(bf16)

    kv = pl.BlockSpec((N_KV, BLK, HEAD_DIM), lambda i: (0, i, 0))
    mt = pl.BlockSpec((N_KV, N_META, HEAD_DIM), lambda i: (0, 0, 0))
    return pl.pallas_call(
        body, grid=(rows // BLK,),
        in_specs=[pl.BlockSpec((BLK, D), lambda i: (i, 0)), kv, kv, mt, mt,
                  pl.BlockSpec((BLK, 128), lambda i: (i, 0)), pl.BlockSpec((BLK, 128), lambda i: (i, 0))],
        out_specs=[pl.BlockSpec((BLK, D), lambda i: (i, 0)), pl.BlockSpec((BLK, 512), lambda i: (i, 0))],
        out_shape=[SDS((rows, D), bf16), SDS((rows, 512), bf16)],
        name="qkv_finish", compiler_params=_cp(("arbitrary",)),
    )(dq, dk, dv, dkm, dvm, cos128, sin128)


_TW = 512


def _mix_specs(rows):
    tr = _row_chunk(rows)
    tile = pl.BlockSpec((tr, _TW), lambda i, j: (i, j))
    ga = pl.BlockSpec((tr, _TW), lambda i, j: (i, OFF_G // _TW + j))
    gb = pl.BlockSpec((tr, _TW), lambda i, j: (i, (OFF_G + D) // _TW + j))
    return (rows // tr, D // _TW), tile, ga, gb


def _mix_fwd(y_a, y_b, z):
    rows = y_a.shape[0]
    tw = 256
    col = lambda off: pl.BlockSpec((rows, tw), lambda j: (0, off // tw + j))

    def body(ya_ref, yb_ref, ga_ref, gb_ref, o_ref, ot_ref):
        mixed = (_sigmoid(ga_ref[...]) * ya_ref[...].astype(f32)
                 + _sigmoid(gb_ref[...]) * yb_ref[...].astype(f32))
        o_ref[...] = mixed.astype(bf16)
        ot_ref[...] = mixed.T.astype(bf16)

    return pl.pallas_call(
        body, grid=(D // tw,), in_specs=[col(0), col(0), col(OFF_G), col(OFF_G + D)],
        out_specs=[col(0), pl.BlockSpec((tw, rows), lambda j: (j, 0))],
        out_shape=[HBM((rows, D), bf16), SDS((D, rows), bf16)],
        name="mix_fwd", compiler_params=_cp(("arbitrary",)),
    )(y_a, y_b, z, z)


def _mix_bwd(dmixed, y_a, y_b, z):
    rows = y_a.shape[0]
    grid, _mix_tile, _mix_ga, _mix_gb = _mix_specs(rows)

    def body(dm_ref, ya_ref, yb_ref, ga_ref, gb_ref, dya_ref, dyb_ref, dga_ref, dgb_ref):
        dm = dm_ref[...].astype(f32)
        sa, sb = _sigmoid(ga_ref[...]), _sigmoid(gb_ref[...])
        dya_ref[...] = (dm * sa).astype(bf16)
        dyb_ref[...] = (dm * sb).astype(bf16)
        dga_ref[...] = (dm * ya_ref[...].astype(f32) * sa * (1.0 - sa)).astype(bf16)
        dgb_ref[...] = (dm * yb_ref[...].astype(f32) * sb * (1.0 - sb)).astype(bf16)

    return pl.pallas_call(
        body, grid=grid, in_specs=[_mix_tile, _mix_tile, _mix_tile, _mix_ga, _mix_gb],
        out_specs=[_mix_tile] * 4, out_shape=[HBM((rows, D), bf16)] * 4,
        name="mix_bwd", compiler_params=_cp(("arbitrary", "arbitrary")),
    )(dmixed, y_a, y_b, z, z)


def _final_ln(out32, h32, tgt, ln_g, ln_b):
    rows = out32.shape[0]

    def body(o_ref, h_ref, t_ref, g_ref, b_ref, du_ref, dub_ref, st_ref):
        i = pl.program_id(0)
        g = g_ref[...]
        y, xhat, rstd = _ln_rows(ALPHA * h_ref[...] + o_ref[...], g, b_ref[...])
        e = jnp.where(i > 0, y - t_ref[0], 0.0)
        dy = e * (1.0 / D)
        du = _ln_rows_bwd(dy, g, xhat, rstd)
        du_ref[...] = du
        dub_ref[...] = du.astype(bf16)
        st = jnp.concatenate([_colsum(dy * xhat), _colsum(dy), _colsum(du), _colsum(e * e) * (0.5 / D),
                              jnp.zeros((4, D), f32)], axis=0)

        @pl.when(i == 0)
        def _():
            st_ref[...] = st

        @pl.when(i > 0)
        def _():
            st_ref[...] += st

    row = pl.BlockSpec((BLK, D), lambda i: (i, 0))
    vec = pl.BlockSpec((1, D), lambda i: (0, 0))
    return pl.pallas_call(
        body, grid=(rows // BLK,),
        in_specs=[row, row, pl.BlockSpec((1, BLK, D), lambda i: (0, jnp.maximum(i - 1, 0), 0)), vec, vec],
        out_specs=[row, row, pl.BlockSpec((8, D), lambda i: (0, 0))],
        out_shape=[SDS((rows, D), f32), HBM((rows, D), bf16), SDS((8, D), f32)],
        name="final_ln", compiler_params=_cp(("arbitrary",)),
    )(out32, h32, tgt, ln_g, ln_b)


def _step_rnn(h32, hb, z, wrg, smallw, p, zero):
    rows = z.shape[0]
    cos128, sin128 = _rope_tables(rows)
    cos128 = cos128 + zero
    xc, hr, ya, ya_t = _rnn_fwd(z, smallw, p["conv_b"] + zero, wrg, p["b_ra"], p["b_ri"], p["lru_lambda"])
    q_r, k_r, v_b = _qkv_prep(z, cos128, sin128)
    return dict(cos128=cos128, sin128=sin128, h32=h32, hb=hb, z=z, xc=xc, hr=hr, ya=ya, ya_t=ya_t,
                q_r=q_r, k_r=k_r, v_b=v_b)


def _step_attn(s, p, zero):
    sinks = p["sinks"].reshape(N_KV * GROUP) + zero[0]
    o32, yb, yb_t, lse = _attn_fwd(s["q_r"], s["k_r"], s["v_b"], s["z"], sinks)
    return dict(s, sinks=sinks, o32=o32, yb=yb, yb_t=yb_t, lse=lse)


def _step_merge(s, tgt, w3, p):
    ya, yb, z = s["ya"], s["yb"], s["z"]
    y_a = _mm(ya, w3, sel=0, out_dtype=bf16, name="mm_ya")
    y_b = _mm(yb, w3, sel=1, out_dtype=bf16, name="mm_yb")
    mixed, mixed_t = _mix_fwd(y_a, y_b, z)
    out32 = _mm(mixed, w3, sel=2, bias=p["b_o"], name="mm_out")
    du32, dub, st_out = _final_ln(out32, s["h32"], tgt, p["ln_g"], p["ln_b"])

    g_wo = _mm(mixed_t, dub, out_dtype=bf16, name="mm_dwo")
    dmixed = _mm(dub, w3, sel=2, nt=True, out_dtype=bf16, name="mm_dmixed")
    dya_b, dyb_b, dma, dmb = _mix_bwd(dmixed, y_a, y_b, z)
    g_wrnn = _mm(s["ya_t"], dya_b, out_dtype=bf16, name="mm_dwrnn")
    g_wattn = _mm(s["yb_t"], dyb_b, out_dtype=bf16, name="mm_dwattn")
    dya = _mm(dya_b, w3, sel=0, nt=True, name="mm_dya")
    dyb = _mm(dyb_b, w3, sel=1, nt=True, name="mm_dyb")
    return dict(du32=du32, st_out=st_out, dma=dma, dmb=dmb, dya=dya, dyb=dyb, g_wo=g_wo, g_wrnn=g_wrnn,
                g_wattn=g_wattn)


def _step_backward(s, t, wrg, smallw, p, conv_b):
    z = s["z"]
    dxr, dgr, g_wrg, vec_rnn = _rnn_bwd(t["dya"], s["hr"], s["xc"], z, smallw, conv_b, wrg, p["b_ra"], p["b_ri"],
                                        p["lru_lambda"])
    dq_r, dga, dk, dv, dkm, dvm, dsr = _attn_bwd(t["dyb"], s["o32"], s["lse"], s["q_r"], s["k_r"], s["v_b"], z,
                                                 s["sinks"])
    dq, dkv = _qkv_finish(dq_r, dk, dv, dkm, dvm, s["cos128"], s["sin128"])
    dz_parts = [(dxr, D), (dgr, D), (dq, D), (dkv, 512), (dga, D), (t["dma"], D), (t["dmb"], D)]
    return dict(vec_rnn=vec_rnn, dsr=dsr, g_wrg=g_wrg, dz_parts=dz_parts)


def _step_input_grad(dh_lo, dh_hi, du32, x, smallw, p, after):
    grad_x, dmeta, st_emb = _ln_emb_bwd(dh_lo, dh_hi, du32, x, smallw, p["ln_emb_g"], after)
    return dict(grad_x=grad_x, dmeta=dmeta, st_emb=st_emb)


_ANY = pl.BlockSpec(memory_space=pl.ANY)
_VMEM = pl.BlockSpec(memory_space=pltpu.VMEM)
_HBM = pl.BlockSpec(memory_space=pltpu.HBM)
_SEM = pl.BlockSpec(memory_space=pltpu.SEMAPHORE)


def _place():
    x, y, c = lax.axis_index("x"), lax.axis_index("y"), lax.axis_index("c")
    return x, y, c


def _dev(px, py, pc):
    return 4 * px + 2 * py + pc


def _tile_rows(r):
    return max(t for t in range(16, 321, 16) if r % t == 0) if r > 320 else r


def _cast_w_in(w_in_t, me_idx):
    tm = _tile_rows(SHARD_IN)

    def body(me_ref, i_ref, o_ref):
        o_ref[...] = i_ref[...].astype(bf16)

    return pl.pallas_call(
        body,
        grid_spec=pltpu.PrefetchScalarGridSpec(
            num_scalar_prefetch=1, grid=(SHARD_IN // tm,),
            in_specs=[pl.BlockSpec((tm, D), lambda i, me_ref: (i, 0))],
            out_specs=pl.BlockSpec((None, tm, D), lambda i, me_ref: (me_ref[0], i, 0))),
        out_shape=SDS((N_DEV, SHARD_IN, D), bf16), name="cast_w_in", compiler_params=_cp(("arbitrary",)),
    )(me_idx, pltpu.with_memory_space_constraint(w_in_t, pltpu.HBM))


def _cast_small(me_idx, w_rnn_out, w_attn_out, w_o, w_ra, w_ri, meta, conv_w):
    def body(me_ref, a_ref, b_ref, c_ref, ra_ref, ri_ref, m_ref, cw_ref, w3_ref, wrg_ref, sw_ref):
        w3_ref[0] = a_ref[0].astype(bf16)
        w3_ref[1] = b_ref[0].astype(bf16)
        w3_ref[2] = c_ref[0].astype(bf16)
        wrg_ref[0] = ra_ref[0].astype(bf16)
        wrg_ref[1] = ri_ref[0].astype(bf16)
        sw_ref[...] = jnp.concatenate([m_ref[...], cw_ref[0], jnp.zeros((4, 256), f32)], axis=0)

    args = (w_rnn_out, w_attn_out, w_o, w_ra, w_ri, meta, conv_w)
    whole = lambda shape: pl.BlockSpec(shape, lambda i, me_ref: (0,) * len(shape))
    slot = lambda shape: pl.BlockSpec((None, *shape), lambda i, me_ref: (me_ref[0], *([0] * len(shape))))
    shapes = [(3, 256, D), (2, N_RNN_BLOCKS, 32, RNN_BLOCK), (24, 256)]
    return pl.pallas_call(
        body,
        grid_spec=pltpu.PrefetchScalarGridSpec(
            num_scalar_prefetch=1, grid=(1,), in_specs=[whole(a.shape) for a in args],
            out_specs=[slot(sh) for sh in shapes]),
        out_shape=[SDS((N_DEV, *sh), dt) for sh, dt in zip(shapes, (bf16, bf16, f32))],
        name="cast_small", compiler_params=_cp(("arbitrary",)),
    )(me_idx, *args)


def _remote(src, dst, send_sems, recv_sems, k, to):
    return pltpu.make_async_remote_copy(src_ref=src, dst_ref=dst, send_sem=send_sems.at[k], recv_sem=recv_sems.at[k],
                                        device_id=to, device_id_type=MESH)


def _w_in_rows(core, early):
    if early:
        return (1 - core) * W_IN_LATE, W_IN_EARLY
    return core * W_IN_EARLY, W_IN_LATE


def _all_gather(bufs, chunks):
    n = len(bufs)
    base = [0]
    for ch in chunks:
        base.append(base[-1] + 7 * ch)

    def body(*refs):
        outs = refs[n:2 * n]
        send_sems, recv_sems = refs[2 * n:]
        x, y, c = _place()
        me, sibling = (x, y, c), (x, y, 1 - c)
        chips = [(1 - x, y), (x, 1 - y), (1 - x, 1 - y)]

        def copy(a, i, k, block, to):
            blk = outs[a].at[_dev(*block)]
            if a == 0:
                r0, r = _w_in_rows(block[2], True)
                r = r // chunks[a]
                blk = blk.at[pl.ds(pl.multiple_of(r0 + i * r, 32), r)]
            return _remote(blk, blk, send_sems, recv_sems, base[a] + 7 * i + k, to)

        pieces = [(a, i) for a in range(n) for i in range(chunks[a])]
        first = []
        for a, i in pieces:
            first.append(copy(a, i, 0, me, sibling))
            first += [copy(a, i, 1 + j, me, (*chip, c)) for j, chip in enumerate(chips)]
        for cp in first:
            cp.start()
        passed = []
        for a, i in pieces:
            for j, chip in enumerate(chips):
                copy(a, i, 1 + j, (*chip, c), me).wait_recv()
                cp = copy(a, i, 4 + j, (*chip, c), sibling)
                cp.start()
                passed.append(cp)
        for a, i in pieces:
            copy(a, i, 0, sibling, me).wait_recv()
            for j, chip in enumerate(chips):
                copy(a, i, 4 + j, (*chip, 1 - c), me).wait_recv()
        for cp in first + passed:
            cp.wait_send()

    return pl.pallas_call(
        body, in_specs=[_ANY] * n, out_specs=[_ANY] * n,
        out_shape=[SDS(b.shape, b.dtype) for b in bufs],
        input_output_aliases={a: a for a in range(n)},
        scratch_shapes=[pltpu.SemaphoreType.DMA((base[-1],)), pltpu.SemaphoreType.DMA((base[-1],))],
        name="all_gather_weights",
    )(*bufs)


def _late_rows(buf, block):
    r0, r = _w_in_rows(block[2], False)
    return buf.at[_dev(*block)].at[pl.ds(pl.multiple_of(r0, 64), r)]


def _whole_block(buf, block):
    return buf.at[_dev(*block)]


def _copies_own(part):
    def make(srcs, lands, send_sems, recv_sems):
        x, y, c = _place()
        peers = [(x, y, 1 - c), (1 - x, y, c), (x, 1 - y, c), (1 - x, 1 - y, c)]
        out = []
        for a in range(len(srcs)):
            blk = part(srcs[a], (x, y, c))
            out += [_remote(blk, blk, send_sems, recv_sems, 4 * a + k, to) for k, to in enumerate(peers)]
        return out
    return make


def _copies_pass(part):
    def make(srcs, lands, send_sems, recv_sems):
        x, y, c = _place()
        out = []
        for a in range(len(srcs)):
            for j, chip in enumerate([(1 - x, y), (x, 1 - y), (1 - x, 1 - y)]):
                blk = part(srcs[a], (*chip, c))
                out.append(_remote(blk, blk, send_sems, recv_sems, 3 * a + j, (x, y, 1 - c)))
        return out
    return make


_PEER_FLIPS = [(f // 4, (f // 2) % 2, f % 2) for f in range(1, N_DEV)]


def _copies_direct(same_src):
    def make(srcs, lands, send_sems, recv_sems):
        x, y, c = _place()
        me = _dev(x, y, c)
        out = []
        for a in range(len(srcs)):
            for k, (fx, fy, fc) in enumerate(_PEER_FLIPS):
                peer = ((x + fx) % 2, (y + fy) % 2, (c + fc) % 2)
                src = srcs[a] if same_src else srcs[a].at[_dev(*peer)]
                out.append(_remote(src, lands[a].at[me], send_sems, recv_sems, 7 * a + k, peer))
        return out
    return make


def _copies_siblings(srcs, lands, send_sems, recv_sems):
    x, y, c = _place()
    return [_remote(srcs[a].at[2 * q + (1 - c)], lands[a].at[q], send_sems, recv_sems, 4 * a + q, (x, y, 1 - c))
            for a in range(len(srcs)) for q in range(4)]


def _copies_chips(srcs, lands, send_sems, recv_sems):
    x, y, c = _place()
    chips = [(1 - x, y), (x, 1 - y), (1 - x, 1 - y)]
    return [_remote(srcs[a].at[2 * qx + qy], lands[a].at[j], send_sems, recv_sems, 3 * a + j, (qx, qy, c))
            for a in range(len(srcs)) for j, (qx, qy) in enumerate(chips)]


def _split_start(make, per_array, srcs, lands, dep, name):
    n, tot = len(srcs), len(srcs) + len(lands)

    def body(*refs):
        send_sems, recv_sems, token = refs[tot + 1], refs[tot + 2], refs[-1]
        for cp in make(refs[:n], refs[n:tot], send_sems, recv_sems):
            cp.start()
        token[...] = jnp.zeros_like(token)

    hbm = lambda t: pltpu.with_memory_space_constraint(t, pltpu.HBM)
    res = pl.pallas_call(
        body, name=name,
        out_shape=(pltpu.SemaphoreType.DMA((per_array * n,)), pltpu.SemaphoreType.DMA((per_array * n,)),
                   *[pltpu.HBM(t.shape, t.dtype) for t in (*srcs, *lands)], SDS((8, 128), f32)),
        in_specs=[_HBM] * tot + [_ANY], out_specs=(_SEM, _SEM, *([_HBM] * tot), _VMEM),
        input_output_aliases={i: 2 + i for i in range(tot)},
        compiler_params=pltpu.CompilerParams(has_side_effects=pltpu.SideEffectType.DATAFLOW_SIDE_EFFECTING),
    )(*[hbm(t) for t in (*srcs, *lands)], dep)
    return res[0], res[1], list(res[2:2 + n]), list(res[2 + n:2 + tot]), res[-1]


def _split_wait(make, send_sems, recv_sems, srcs, lands, after, name):
    n, tot = len(srcs), len(srcs) + len(lands)

    def body(*refs):
        for cp in make(refs[:n], refs[n:tot], refs[tot], refs[tot + 1]):
            cp.wait_send()
            cp.wait_recv()

    res = pl.pallas_call(
        body, name=name,
        out_shape=tuple(pltpu.HBM(t.shape, t.dtype) for t in (*srcs, *lands)),
        in_specs=[_HBM] * tot + [_SEM, _SEM, _ANY], out_specs=tuple([_HBM] * tot),
        input_output_aliases={i: i for i in range(tot)},
        compiler_params=pltpu.CompilerParams(has_side_effects=pltpu.SideEffectType.DATAFLOW_SIDE_EFFECTING),
    )(*srcs, *lands, send_sems, recv_sems, after)
    return list(res[:n]), list(res[n:])


def _adamw_direct(g, land, me_idx, w, m, v, name):
    r, wd = w.shape
    tr = min(r, 256)

    def body(me_ref, *refs):
        g_ref, peers = refs[0], refs[1:N_DEV]
        w_ref, m_ref, v_ref, g_out, d_out, m_out, v_out = refs[N_DEV:]
        gs = g_ref[...].astype(f32)
        for p_ref in peers:
            gs = gs + p_ref[...].astype(f32)
        d, mn, vn = _adamw(w_ref[...], gs, m_ref[...], v_ref[...])
        g_out[...] = gs
        d_out[...] = d
        m_out[...] = mn
        v_out[...] = vn

    tile = pl.BlockSpec((tr, wd), lambda i, me_ref: (i, 0))
    slot = lambda k: pl.BlockSpec((None, tr, wd), lambda i, me_ref: ((me_ref[0] + k) % N_DEV, i, 0))
    return pl.pallas_call(
        body,
        grid_spec=pltpu.PrefetchScalarGridSpec(
            num_scalar_prefetch=1, grid=(r // tr,),
            in_specs=[slot(0)] + [slot(k) for k in range(1, N_DEV)] + [tile, tile, tile],
            out_specs=[tile] * 4),
        out_shape=[SDS((r, wd), f32)] * 4, name=name, compiler_params=_cp(("arbitrary",), 48),
    )(me_idx, g, *([land] * (N_DEV - 1)), w, m, v)


def _pair_sum(g, r1, c_idx, name):
    _, r, w = g.shape
    tr = _tile_rows(r)

    def body(c_ref, g_ref, r_ref, o_ref):
        o_ref[...] = (g_ref[...].astype(f32) + r_ref[...].astype(f32)).astype(bf16)

    return pl.pallas_call(
        body,
        grid_spec=pltpu.PrefetchScalarGridSpec(
            num_scalar_prefetch=1, grid=(4, r // tr),
            in_specs=[pl.BlockSpec((None, tr, w), lambda q, i, c_ref: (2 * q + c_ref[0], i, 0)),
                      pl.BlockSpec((None, tr, w), lambda q, i, c_ref: (q, i, 0))],
            out_specs=pl.BlockSpec((None, tr, w), lambda q, i, c_ref: (q, i, 0))),
        out_shape=HBM((4, r, w), bf16), name=name, compiler_params=_cp(("arbitrary", "arbitrary")),
    )(c_idx, g, r1)


def _adamw(w, g, m, v):
    m = ADAM_B1 * m + (1.0 - ADAM_B1) * g
    v = ADAM_B2 * v + (1.0 - ADAM_B2) * (g * g)
    m_hat = m / (1.0 - ADAM_B1 ** ADAM_STEP)
    v_hat = v / (1.0 - ADAM_B2 ** ADAM_STEP)
    delta = -ADAM_LR * (m_hat / (jnp.sqrt(v_hat) + ADAM_EPS) + ADAM_WD * w)
    return delta, m, v


def _adamw_big(part, r2, q_idx, w, m, v, name, row_off=0, cols=(0, 1), prev=None):
    r, wd = w.shape
    tr = _tile_rows(r)
    k, ncol = cols
    wp = wd // ncol

    def body(q_ref, p_ref, r_ref, w_ref, m_ref, v_ref, *rest):
        g_out, d_out, m_out, v_out = rest[-4:]
        g = p_ref[...].astype(f32)
        for j in range(3):
            g = g + r_ref[j].astype(f32)
        d, mn, vn = _adamw(w_ref[...], g, m_ref[...], v_ref[...])
        g_out[...] = g
        d_out[...] = d
        m_out[...] = mn
        v_out[...] = vn

    tile = pl.BlockSpec((tr, wp), lambda i, q_ref: (i, k))
    prev = list(prev) if prev is not None else []
    return pl.pallas_call(
        body,
        grid_spec=pltpu.PrefetchScalarGridSpec(
            num_scalar_prefetch=1, grid=(r // tr,),
            in_specs=[pl.BlockSpec((None, tr, wp), lambda i, q_ref: (q_ref[0], row_off + i, 0)),
                      pl.BlockSpec((3, tr, wp), lambda i, q_ref: (0, row_off + i, 0)), tile, tile, tile]
                     + [pl.BlockSpec(memory_space=pl.ANY)] * len(prev),
            out_specs=[tile] * 4),
        out_shape=[SDS((r, wd), f32)] * 4, name=name,
        input_output_aliases={6 + i: i for i in range(len(prev))},
        compiler_params=_cp(("arbitrary",), 48),
    )(q_idx, part, r2, w, m, v, *prev)


_SMALL_ROWS = 24


def _pack_early(vec_rnn, st_out, dsr, db_in):
    def body(vr_ref, so_ref, dsr_ref, db_ref, sm_ref, sm2_ref):
        sm_ref[...] = jnp.zeros_like(sm_ref)
        sm2_ref[...] = jnp.zeros_like(sm2_ref)
        sm_ref[2:3, :] = vr_ref[3:4, :]
        sm_ref[3:6, :] = vr_ref[0:3, :]
        sm_ref[6:7, :] = so_ref[2:3, :]
        sm_ref[7:9, :] = so_ref[0:2, :]
        sm_ref[10:11, :] = so_ref[3:4, :]
        for h in range(N_KV):
            sm_ref[9:10, h * GROUP:(h + 1) * GROUP] = _colsum(dsr_ref[h])
        for j in range(6):
            sm_ref[16 + j:17 + j, :] = db_ref[0:1, j * D:(j + 1) * D]
        sm_ref[22:23, 0:D_IN - 6 * D] = db_ref[0:1, 6 * D:D_IN]
        for s in range(N_DEV):
            sm2_ref[s, 0:CONV_WIDTH, :] = vr_ref[4:8, s * 256:(s + 1) * 256]

    return pl.pallas_call(
        body, out_shape=[SDS((_SMALL_ROWS, D), f32), SDS((N_DEV, 8, 256), f32)],
        name="pack_early", compiler_params=_cp(None),
    )(vec_rnn, st_out, dsr, db_in)


def _pack_late(st_emb, dmeta):
    def body(se_ref, dm_ref, sm_ref, sm2_ref):
        sm_ref[...] = se_ref[...]
        for s in range(N_DEV):
            sm2_ref[s] = dm_ref[:, s * 256:(s + 1) * 256]

    return pl.pallas_call(
        body, out_shape=[SDS((8, D), f32), SDS((N_DEV, N_META, 256), f32)],
        name="pack_late", compiler_params=_cp(None),
    )(st_emb, dmeta)


_SMALL_ROW_OF = {"ln_emb_g": 0, "ln_emb_b": 1, "conv_b": 2, "b_ra": 3, "b_ri": 4, "lru_lambda": 5, "b_o": 6,
                 "ln_g": 7, "ln_b": 8}
_SMALL_NAMES = ["ln_emb_g", "ln_emb_b", "conv_b", "b_ra", "b_ri", "lru_lambda", "b_o", "ln_g", "ln_b",
                "sinks", "b_in", "meta_tokens", "conv_w"]


def _small_update(me_idx, early, late, wmv):
    n_fixed = 9

    def in_order(me, own_ref, land_ref):
        acc = None
        for e in range(N_DEV):
            term = jnp.where(me == e, own_ref[...], land_ref[e])
            acc = term if acc is None else acc + term
        return acc

    def body(*refs):
        me_ref, own_ref, land_ref, cown_ref, cland_ref, lown_ref, lland_ref, mown_ref, mland_ref = refs[:n_fixed]
        ins = refs[n_fixed:n_fixed + 3 * len(_SMALL_NAMES)]
        outs = refs[n_fixed + 3 * len(_SMALL_NAMES):]
        me = me_ref[0]
        sm = in_order(me, own_ref, land_ref)
        conv = in_order(me, cown_ref, cland_ref)
        late = in_order(me, lown_ref, lland_ref)
        meta = in_order(me, mown_ref, mland_ref)

        def grad_of(name):
            if name in ("ln_emb_g", "ln_emb_b"):
                r = _SMALL_ROW_OF[name]
                return late[r:r + 1, :]
            if name in _SMALL_ROW_OF:
                r = _SMALL_ROW_OF[name]
                return sm[r:r + 1, :]
            if name == "sinks":
                return sm[9:10, 0:N_KV * GROUP]
            if name == "b_in":
                return jnp.concatenate([sm[16 + j:17 + j, :] for j in range(7)], axis=1)[:, :D_IN]
            if name == "meta_tokens":
                return meta
            return conv[0:CONV_WIDTH, :]

        for i, name in enumerate(_SMALL_NAMES):
            w_ref, m_ref, v_ref = ins[3 * i:3 * i + 3]
            g = grad_of(name)
            d, mn, vn = _adamw(w_ref[...], g, m_ref[...], v_ref[...])
            outs[4 * i][...] = g
            outs[4 * i + 1][...] = d
            outs[4 * i + 2][...] = mn
            outs[4 * i + 3][...] = vn
        outs[-1][...] = jnp.broadcast_to(jnp.sum(sm[10:11, :], axis=1, keepdims=True), (8, 128))

    args, out_shape = [me_idx, *early, *late], []
    for name in _SMALL_NAMES:
        args += list(wmv[name])
        out_shape += [SDS(wmv[name][0].shape, f32)] * 4
    out_shape.append(SDS((8, 128), f32))
    res = pl.pallas_call(
        body, out_shape=out_shape, in_specs=[pl.BlockSpec(memory_space=pltpu.SMEM)] + [_VMEM] * (len(args) - 1),
        name="small_update", compiler_params=_cp(None))(*args)
    return {name: tuple(res[4 * i:4 * i + 4]) for i, name in enumerate(_SMALL_NAMES)}, res[-1][0, 0]


_WEIGHTS = ["meta_tokens", "ln_emb_g", "ln_emb_b", "w_in", "b_in", "conv_w", "conv_b", "w_ra", "b_ra", "w_ri",
            "b_ri", "lru_lambda", "sinks", "w_rnn_out", "w_attn_out", "w_o", "b_o", "ln_g", "ln_b"]
_SMALL_2D = {"meta_tokens": (N_META, 256), "conv_w": (CONV_WIDTH, 256), "b_in": (1, D_IN), "sinks": (1, N_KV * GROUP)}


def kernel(x, meta_tokens, ln_emb_g, ln_emb_b, w_in, b_in, conv_w, conv_b, w_ra, b_ra, w_ri, b_ri, lru_lambda, sinks, w_rnn_out, w_attn_out, w_o, b_o, ln_g, ln_b, loss_target, m_meta_tokens, m_ln_emb_g, m_ln_emb_b, m_w_in, m_b_in, m_conv_w, m_conv_b, m_w_ra, m_b_ra, m_w_ri, m_b_ri, m_lru_lambda, m_sinks, m_w_rnn_out, m_w_attn_out, m_w_o, m_b_o, m_ln_g, m_ln_b, v_meta_tokens, v_ln_emb_g, v_ln_emb_b, v_w_in, v_b_in, v_conv_w, v_conv_b, v_w_ra, v_b_ra, v_w_ri, v_b_ri, v_lru_lambda, v_sinks, v_w_rnn_out, v_w_attn_out, v_w_o, v_b_o, v_ln_g, v_ln_b):
    w = dict(meta_tokens=meta_tokens, ln_emb_g=ln_emb_g, ln_emb_b=ln_emb_b, w_in=w_in, b_in=b_in, conv_w=conv_w,
             conv_b=conv_b, w_ra=w_ra, b_ra=b_ra, w_ri=w_ri, b_ri=b_ri, lru_lambda=lru_lambda, sinks=sinks,
             w_rnn_out=w_rnn_out, w_attn_out=w_attn_out, w_o=w_o, b_o=b_o, ln_g=ln_g, ln_b=ln_b)
    m = dict(meta_tokens=m_meta_tokens, ln_emb_g=m_ln_emb_g, ln_emb_b=m_ln_emb_b, w_in=m_w_in, b_in=m_b_in,
             conv_w=m_conv_w, conv_b=m_conv_b, w_ra=m_w_ra, b_ra=m_b_ra, w_ri=m_w_ri, b_ri=m_b_ri,
             lru_lambda=m_lru_lambda, sinks=m_sinks, w_rnn_out=m_w_rnn_out, w_attn_out=m_w_attn_out, w_o=m_w_o,
             b_o=m_b_o, ln_g=m_ln_g, ln_b=m_ln_b)
    v = dict(meta_tokens=v_meta_tokens, ln_emb_g=v_ln_emb_g, ln_emb_b=v_ln_emb_b, w_in=v_w_in, b_in=v_b_in,
             conv_w=v_conv_w, conv_b=v_conv_b, w_ra=v_w_ra, b_ra=v_b_ra, w_ri=v_w_ri, b_ri=v_b_ri,
             lru_lambda=v_lru_lambda, sinks=v_sinks, w_rnn_out=v_w_rnn_out, w_attn_out=v_w_attn_out, w_o=v_w_o,
             b_o=v_b_o, ln_g=v_ln_g, ln_b=v_ln_b)
    px, py, pc = _place()
    as_idx = lambda t: jnp.reshape(t, (1,)).astype(jnp.int32)
    c_idx, q_idx, me_idx = as_idx(pc), as_idx(2 * px + py), as_idx(_dev(px, py, pc))

    w3_s, wrg_s, small_s = _cast_small(me_idx, w_rnn_out, w_attn_out, w_o, w_ra, w_ri, meta_tokens, conv_w)
    vec = lambda name: w[name].reshape(1, -1)
    p = {k: vec(k) for k in ("ln_emb_g", "ln_emb_b", "b_in", "conv_b", "b_ra", "b_ri", "lru_lambda", "sinks",
                             "b_o", "ln_g", "ln_b")}
    w_in_t = lambda a: jnp.swapaxes(a, 1, 2).reshape(SHARD_IN, D)
    wg, wrg, smallw = _all_gather([_cast_w_in(w_in_t(w_in), me_idx), wrg_s, small_s], [6, 1, 1])
    late = _split_start(_copies_own(_late_rows), 4, [wg], [], smallw, "gather_late_start")
    h32, hb = _ln_emb(x, smallw, p["ln_emb_g"], p["ln_emb_b"] + late[4][0:1, 0:1])
    z = _mm_z(hb, late[2][0].reshape(D_IN, D), p["b_in"], None, "mm_z_early")
    (wg,), _ = _split_wait(_copies_own(_late_rows), late[0], late[1], late[2], [], z, "gather_late_wait")
    passed = _split_start(_copies_pass(_late_rows), 3, [wg], [], smallw, "gather_pass_start")
    w3_own = _split_start(_copies_own(_whole_block), 4, [w3_s], [], passed[4], "gather_w3_start")
    zero = w3_own[4][0:1, 0:1]
    z = _mm_z(hb, passed[2][0].reshape(D_IN, D), p["b_in"] + zero, c_idx, "mm_z_late_own", z)
    (wg,), _ = _split_wait(_copies_pass(_late_rows), passed[0], passed[1], passed[2], [], z, "gather_pass_wait")
    w_full = wg.reshape(D_IN, D)

    z = _mm_z(hb, w_full, p["b_in"], 1 - c_idx, "mm_z_late_other", z)
    s = _step_rnn(h32, hb, z, wrg, smallw, p, zero)
    (w3,), _ = _split_wait(_copies_own(_whole_block), w3_own[0], w3_own[1], w3_own[2], [], s["ya"], "gather_w3_wait")
    w3_pass = _split_start(_copies_pass(_whole_block), 3, [w3], [], smallw, "gather_w3_pass_start")
    s = _step_attn(s, p, w3_pass[4][0:1, 0:1])
    (w3,), _ = _split_wait(_copies_pass(_whole_block), w3_pass[0], w3_pass[1], w3_pass[2], [], s["lse"],
                           "gather_w3_pass_wait")
    t = _step_merge(s, loss_target, w3, p)

    big = {}
    two_d = lambda name: (w[name].shape[-2], w[name].shape[-1])
    proj = ("w_o", "w_rnn_out", "w_attn_out")
    g_proj = [t[k].reshape(N_DEV, 256, D) for k in ("g_wo", "g_wrnn", "g_wattn")]
    g_pending = _split_start(_copies_direct(False), 7, g_proj, [lax.empty((N_DEV, 256, D), bf16) for _ in proj],
                             p["b_o"], "reduce_proj_start")
    u = _step_backward(s, t, wrg, smallw, p, p["conv_b"] + g_pending[4][0:1, 0:1])

    def siblings_start(gs, dep, tag):
        return _split_start(_copies_siblings, 4, gs, [lax.empty((4, *g.shape[1:]), bf16) for g in gs], dep,
                            "reduce_siblings_start_" + tag)

    def chips_start(gs, r1, dep, tag):
        parts = [_pair_sum(g, r, c_idx, "pair_sum_%s%d" % (tag, i)) for i, (g, r) in enumerate(zip(gs, r1))]
        return _split_start(_copies_chips, 3, parts, [lax.empty((3, *q.shape[1:]), bf16) for q in parts], dep,
                            "reduce_chips_start_" + tag)

    g_a, dz, db_in = _mm_dwin_parts(s["hb"], u["dz_parts"])
    shards = lambda g: g.reshape(N_DEV, SHARD_IN, W_IN_HALF)
    sib_a = siblings_start([shards(g_a), u["g_wrg"].reshape(N_DEV, 2 * RNN_BLOCK, RNN_BLOCK)], db_in, "a")
    g_proj, g_land = _split_wait(_copies_direct(False), *g_pending[:4], sib_a[4], "reduce_proj_wait")
    for i, name in enumerate(proj):
        res = _adamw_direct(g_proj[i], g_land[i], me_idx, w[name].reshape(two_d(name)), m[name].reshape(two_d(name)),
                            v[name].reshape(two_d(name)), "adamw_" + name)
        big[name] = tuple(r.reshape(w[name].shape) for r in res)
    chp_a = chips_start(*_split_wait(_copies_siblings, *sib_a[:4], big["w_attn_out"][3], "reduce_siblings_wait_a"),
                        db_in, "a")
    g_b = _mm_dwin(s["hb"], dz, chp_a[4])
    sib_b = siblings_start([shards(g_b)], db_in, "b")
    sm_e = _pack_early(u["vec_rnn"], t["st_out"], u["dsr"], db_in)
    early = _split_start(_copies_direct(True), 7, list(sm_e),
                         [lax.empty((N_DEV, *a.shape), f32) for a in sm_e], sib_b[4], "small_early_start")
    dh_lo = _mm_dh(dz, w_full, early[4], 0)
    chp_b = chips_start(*_split_wait(_copies_siblings, *sib_b[:4], dh_lo, "reduce_siblings_wait_b"), db_in, "b")
    dh_hi = _mm_dh(dz, w_full, chp_b[4], 1)
    parts_a, r2_a = _split_wait(_copies_chips, *chp_a[:4], dh_hi, "reduce_chips_wait_a")
    w_in_res = _adamw_big(parts_a[0], r2_a[0], q_idx, w_in_t(w["w_in"]), w_in_t(m["w_in"]), w_in_t(v["w_in"]),
                          "adamw_w_in_a", cols=(0, 2))
    u.update(_step_input_grad(dh_lo, dh_hi, t["du32"], x, smallw, p, w_in_res[3]))
    sm_l = _pack_late(u["st_emb"], u["dmeta"])
    late_x = _split_start(_copies_direct(True), 7, list(sm_l), [lax.empty((N_DEV, *a.shape), f32) for a in sm_l],
                          p["b_o"], "small_late_start")
    parts_b, r2_b = _split_wait(_copies_chips, *chp_b[:4], late_x[4], "reduce_chips_wait_b")
    res = _adamw_big(parts_b[0], r2_b[0], q_idx, w_in_t(w["w_in"]), w_in_t(m["w_in"]), w_in_t(v["w_in"]),
                     "adamw_w_in_b", cols=(1, 2), prev=w_in_res)
    big["w_in"] = tuple(jnp.swapaxes(r.reshape(1, SHARD_IN, D), 1, 2) for r in res)
    (sm_own, conv_own), (sm_land, conv_land) = _split_wait(_copies_direct(True), *early[:4], res[3], "small_early_wait")
    (l_own, meta_own), (l_land, meta_land) = _split_wait(_copies_direct(True), *late_x[:4], sm_land, "small_late_wait")
    me = _dev(px, py, pc)
    mine = lambda a, axis: lax.dynamic_index_in_dim(a, me, axis, keepdims=False)
    two = lambda name, t: t.reshape(_SMALL_2D.get(name, (1, D)))
    small, loss = _small_update(me_idx, (sm_own, sm_land, mine(conv_own, 0), mine(conv_land, 1)),
                                (l_own, l_land, mine(meta_own, 0), mine(meta_land, 1)),
                                {k: (two(k, w[k]), two(k, m[k]), two(k, v[k])) for k in _SMALL_NAMES})
    for i, name in enumerate(("w_ra", "w_ri")):
        sq = (RNN_BLOCK, RNN_BLOCK)
        res = _adamw_big(parts_a[1], r2_a[1], q_idx, w[name].reshape(sq), m[name].reshape(sq), v[name].reshape(sq),
                         "adamw_" + name, row_off=i)
        big[name] = tuple(r.reshape(w[name].shape) for r in res)
    res = dict(big)
    for k in _SMALL_NAMES:
        res[k] = tuple(t.reshape(w[k].shape) for t in small[k])

    outs = [loss, u["grad_x"]]
    for j in range(4):
        outs += [res[k][j] for k in _WEIGHTS]
    return tuple(outs)
```

```python
import jax
import jax.numpy as jnp
from jax import lax
from jax.experimental import pallas as pl
from jax.experimental.pallas import tpu as pltpu

f32, bf16 = jnp.float32, jnp.bfloat16
SDS = jax.ShapeDtypeStruct

N_DEV = 8
D = 2048
N_META = 16
BLK = 128
ROW0 = BLK - N_META
N_RNN_BLOCKS = 8
RNN_BLOCK = D // N_RNN_BLOCKS
CONV_WIDTH = 4
LRU_C = 8.0
HEAD_DIM = 64
N_KV = 4
GROUP = 8
HALF = HEAD_DIM // 2
ROPE_THETA = 10000.0
NEG_INF = -1e30
LN_EPS = 1e-5
ALPHA = 2.0 ** 0.25
D_IN = 12800
SHARD_IN = D_IN // N_DEV
W_IN_HALF = D // 2
W_IN_LATE = 640
W_IN_EARLY = SHARD_IN - W_IN_LATE
OFF_GR, OFF_Q, OFF_K, OFF_V, OFF_GA, OFF_G = 2048, 4096, 6144, 6400, 6656, 8704
ADAM_LR, ADAM_B1, ADAM_B2, ADAM_EPS, ADAM_WD, ADAM_STEP = 1e-3, 0.9, 0.999, 1e-8, 0.01, 10
VMEM_LIMIT_MB = 56
MESH = pl.DeviceIdType.MESH


def _cp(sem=None, vmem_mb=40):
    return pltpu.CompilerParams(dimension_semantics=sem, vmem_limit_bytes=vmem_mb * 2 ** 20)


def _row_chunk(m):
    best = 16
    for c in range(16, 641, 16):
        if m % c == 0:
            best = c
    return best


def _sigmoid(x):
    return 1.0 / (1.0 + jnp.exp(-x))


def _silu_and_grad(x):
    s = _sigmoid(x)
    return x * s, s * (1.0 + x * (1.0 - s))


def _log_sigmoid(x):
    return jnp.minimum(x, 0.0) - jnp.log1p(jnp.exp(-jnp.abs(x)))


def _ln_rows(v, g, b):
    mu = jnp.mean(v, axis=-1, keepdims=True)
    c = v - mu
    var = jnp.mean(c * c, axis=-1, keepdims=True)
    rstd = lax.rsqrt(var + LN_EPS)
    xhat = c * rstd
    return xhat * g + b, xhat, rstd


def _ln_rows_bwd(dy, g, xhat, rstd):
    dxh = dy * g
    m1 = jnp.mean(dxh, axis=-1, keepdims=True)
    m2 = jnp.mean(dxh * xhat, axis=-1, keepdims=True)
    return rstd * (dxh - m1 - xhat * m2)


def _colsum(v):
    return jnp.sum(v, axis=0, keepdims=True)


def _dot(a, b):
    return jnp.dot(a, b, preferred_element_type=f32)


def _dot_nt(a, b):
    return lax.dot_general(a, b, (((1,), (1,)), ((), ())), preferred_element_type=f32)


def _dot_tn(a, b):
    return lax.dot_general(a, b, (((0,), (0,)), ((), ())), preferred_element_type=f32)


def _meta_full(sw_ref):
    return jnp.concatenate([sw_ref[s, 0:N_META, :] for s in range(N_DEV)], axis=1)


def _ln_emb(x, smallw, g_e, b_e):
    seq = x.shape[1]
    rows = seq + BLK
    nb = rows // BLK

    def body(x_ref, sw_ref, g_ref, b_ref, h32_ref, hb_ref):
        i = pl.program_id(0)
        g, b = g_ref[...], b_ref[...]

        def emit(blk):
            h32_ref[...] = blk
            hb_ref[...] = blk.astype(bf16)

        @pl.when(i == 0)
        def _():
            hm = _ln_rows(_meta_full(sw_ref), g, b)[0]
            emit(jnp.concatenate([jnp.zeros((ROW0, D), f32), hm], axis=0))

        @pl.when(i > 0)
        def _():
            emit(_ln_rows(x_ref[0], g, b)[0])

    return pl.pallas_call(
        body, grid=(nb,),
        in_specs=[pl.BlockSpec((1, BLK, D), lambda i: (0, jnp.maximum(i - 1, 0), 0)),
                  pl.BlockSpec((N_DEV, 24, 256), lambda i: (0, 0, 0)),
                  pl.BlockSpec((1, D), lambda i: (0, 0)),
                  pl.BlockSpec((1, D), lambda i: (0, 0))],
        out_specs=[pl.BlockSpec((BLK, D), lambda i: (i, 0)),
                   pl.BlockSpec((BLK, D), lambda i: (i, 0))],
        out_shape=[SDS((rows, D), f32), SDS((rows, D), bf16)],
        name="ln_emb", compiler_params=_cp(("arbitrary",)),
    )(x, smallw, g_e, b_e)


def _ln_emb_bwd(dh_lo, dh_hi, du32, x, smallw, g_e, after):
    seq = x.shape[1]
    rows = seq + BLK
    nb = rows // BLK

    def body(dlo_ref, dhi_ref, du_ref, x_ref, sw_ref, g_ref, after_ref, gx_ref, dmeta_ref, st_ref):
        i = pl.program_id(0)
        g = g_ref[...]
        dht = jnp.concatenate([dlo_ref[...], dhi_ref[...]], axis=1) + ALPHA * du_ref[...]

        @pl.when(i == 0)
        def _():
            v = jnp.concatenate([jnp.zeros((ROW0, D), f32), _meta_full(sw_ref)], axis=0)
            valid = lax.broadcasted_iota(jnp.int32, (BLK, 1), 0) >= ROW0
            d = jnp.where(valid, dht, 0.0)
            _, xhat, rstd = _ln_rows(v, g, 0.0)
            dv = _ln_rows_bwd(d, g, xhat, rstd)
            dmeta_ref[...] = dv[ROW0:, :]
            st_ref[...] = jnp.concatenate([_colsum(d * xhat), _colsum(d), jnp.zeros((6, D), f32)], axis=0)

        @pl.when(i > 0)
        def _():
            _, xhat, rstd = _ln_rows(x_ref[0], g, 0.0)
            gx_ref[0] = _ln_rows_bwd(dht, g, xhat, rstd)
            st_ref[0:1, :] += _colsum(dht * xhat)
            st_ref[1:2, :] += _colsum(dht)

    return pl.pallas_call(
        body, grid=(nb,),
        in_specs=[pl.BlockSpec((BLK, W_IN_HALF), lambda i: (i, 0)),
                  pl.BlockSpec((BLK, W_IN_HALF), lambda i: (i, 0)),
                  pl.BlockSpec((BLK, D), lambda i: (i, 0)),
                  pl.BlockSpec((1, BLK, D), lambda i: (0, jnp.maximum(i - 1, 0), 0)),
                  pl.BlockSpec((N_DEV, 24, 256), lambda i: (0, 0, 0)),
                  pl.BlockSpec((1, D), lambda i: (0, 0)),
                  pl.BlockSpec(memory_space=pl.ANY)],
        out_specs=[pl.BlockSpec((1, BLK, D), lambda i: (0, jnp.maximum(i - 1, 0), 0)),
                   pl.BlockSpec((N_META, D), lambda i: (0, 0)),
                   pl.BlockSpec((8, D), lambda i: (0, 0))],
        out_shape=[SDS((1, seq, D), f32), SDS((N_META, D), f32), SDS((8, D), f32)],
        name="ln_emb_bwd", compiler_params=_cp(("arbitrary",)),
    )(dh_lo, dh_hi, du32, x, smallw, g_e, after)


def _mm(a, b, *, name, nt=False, sel=None, bias=None, out_dtype=f32, tn=512):
    m, k = a.shape
    cm = _row_chunk(m)
    stacked = sel is not None
    n = D if stacked else (b.shape[0] if nt else b.shape[1])
    am = m
    if stacked and nt:
        b_spec = pl.BlockSpec((tn // 256, None, 256, D), lambda j, i: (j, sel, 0, 0))
    elif stacked:
        b_spec = pl.BlockSpec((N_DEV, None, 256, tn), lambda j, i: (0, sel, 0, j))
    elif nt:
        b_spec = pl.BlockSpec((tn, k), lambda j, i: (j, 0))
    else:
        b_spec = pl.BlockSpec((k, tn), lambda j, i: (0, j))
    in_specs = [pl.BlockSpec((am, k), lambda j, i: (i, 0)), b_spec]
    args = [a, b]
    if bias is not None:
        in_specs.append(pl.BlockSpec((1, tn), lambda j, i: (0, j)))
        args.append(bias)

    def body(*refs):
        a_ref, b_ref, o_ref = refs[0], refs[1], refs[-1]
        bm = b_ref[...]
        if stacked:
            bm = bm.reshape((tn, D) if nt else (D, tn))
        for c in range(am // cm):
            acc = (_dot_nt if nt else _dot)(a_ref[c * cm:(c + 1) * cm, :], bm)
            if bias is not None:
                acc = acc + refs[2][...]
            o_ref[c * cm:(c + 1) * cm, :] = acc.astype(out_dtype)

    return pl.pallas_call(
        body, grid=(n // tn, m // am), in_specs=in_specs,
        out_specs=pl.BlockSpec((am, tn), lambda j, i: (i, j)),
        out_shape=SDS((m, n), out_dtype), name=name, compiler_params=_cp(("arbitrary", "arbitrary"), 48),
    )(*args)


def _mm_z(hb, w_t, bias, side, name, z_prev=None):
    rows, k = hb.shape
    tn = W_IN_LATE
    cm = _row_chunk(rows)
    per = 2 * SHARD_IN // tn
    if side is None:
        side, count = jnp.zeros((1,), jnp.int32), per - 2
        tile = lambda q, t, s_ref: per * q + 1 + t
    else:
        count = 1
        tile = lambda q, t, s_ref: per * q + (per - 1) * s_ref[0]

    def body(s_ref, a_ref, b_ref, bias_ref, *rest):
        o_ref = rest[-1]
        for c in range(rows // cm):
            o_ref[c * cm:(c + 1) * cm, :] = _dot_nt(a_ref[c * cm:(c + 1) * cm, :], b_ref[...]) + bias_ref[...]

    in_specs = [pl.BlockSpec((rows, k), lambda q, t, s_ref: (0, 0)),
                pl.BlockSpec((tn, k), lambda q, t, s_ref: (tile(q, t, s_ref), 0)),
                pl.BlockSpec((1, tn), lambda q, t, s_ref: (0, tile(q, t, s_ref)))]
    args = [side, hb, w_t, bias]
    if z_prev is not None:
        in_specs.append(pl.BlockSpec(memory_space=pl.ANY))
        args.append(z_prev)
    return pl.pallas_call(
        body,
        grid_spec=pltpu.PrefetchScalarGridSpec(
            num_scalar_prefetch=1, grid=(N_DEV // 2, count), in_specs=in_specs,
            out_specs=pl.BlockSpec((rows, tn), lambda q, t, s_ref: (0, tile(q, t, s_ref)))),
        out_shape=SDS((rows, D_IN), f32), name=name,
        input_output_aliases={} if z_prev is None else {4: 0},
        compiler_params=_cp(("arbitrary", "arbitrary"), 48),
    )(*args)


def _mm_dh(dz, w_t, after, half):
    rows = dz.shape[0]
    tn = 512
    nt = W_IN_HALF // tn
    cm = _row_chunk(rows) // 2
    nch = 8
    rc = D_IN // nch

    def body(a_ref, w_ref, after_ref, o_ref, wbuf, sems):
        j, i = pl.program_id(0), pl.program_id(1)

        def tile(t):
            return [pltpu.make_async_copy(w_ref.at[pl.ds(k * rc, rc), pl.ds((half * nt + t) * tn, tn)],
                                          wbuf.at[t, pl.ds(k * rc, rc)], sems.at[t, k]) for k in range(nch)]

        @pl.when((j == 0) & (i == 0))
        def _():
            for t in range(nt):
                for k, cp in enumerate(tile(t)):
                    cp.start(priority=k % 2)

        for t in range(nt):
            @pl.when((j == t) & (i == 0))
            def _():
                for cp in tile(t):
                    cp.wait()

        for t in range(nt):
            @pl.when(j == t)
            def _():
                o_ref[...] = _dot(a_ref[...], wbuf[t])

    return pl.pallas_call(
        body, grid=(nt, rows // cm),
        in_specs=[pl.BlockSpec((cm, D_IN), lambda j, i: (i, 0)),
                  pl.BlockSpec(memory_space=pl.ANY),
                  pl.BlockSpec(memory_space=pl.ANY)],
        out_specs=pl.BlockSpec((cm, tn), lambda j, i: (i, j)),
        out_shape=SDS((rows, W_IN_HALF), f32), name="mm_dh_%d" % half,
        scratch_shapes=[pltpu.VMEM((nt, D_IN, tn), bf16), pltpu.SemaphoreType.DMA((nt, nch))],
        compiler_params=_cp(("arbitrary", "arbitrary"), VMEM_LIMIT_MB),
    )(dz, w_t, after)


def _mm_dwin_parts(hb, parts):
    rows = hb.shape[0]
    tc = 512
    edges = [0]
    for _, w in parts:
        edges.append(edges[-1] + w // tc)

    def body(*refs):
        h_ref, (o_ref, dz_ref, db_ref) = refs[len(parts)], refs[len(parts) + 1:]
        j = pl.program_id(0)
        for p_ref, lo, hi in zip(refs, edges[:-1], edges[1:]):
            @pl.when((j >= lo) & (j < hi))
            def _():
                o_ref[...] = _dot_tn(p_ref[...], h_ref[...]).astype(bf16)
                dz_ref[...] = p_ref[...]

                def step(i, s):
                    blk = p_ref[pl.ds(pl.multiple_of(i * BLK, BLK), BLK), :].astype(f32)
                    return s + blk.reshape(BLK // 8, 8, tc).sum(axis=0)
                s = lax.fori_loop(0, rows // BLK, step, jnp.zeros((8, tc), f32))
                db_ref[...] = jnp.broadcast_to(_colsum(s), (8, tc))

    in_specs = [pl.BlockSpec((rows, tc), lambda j, lo=lo, hi=hi: (0, jnp.clip(j - lo, 0, hi - lo - 1)))
                for lo, hi in zip(edges[:-1], edges[1:])]
    return pl.pallas_call(
        body, grid=(D_IN // tc,),
        in_specs=in_specs + [pl.BlockSpec((rows, W_IN_HALF), lambda j: (0, 0))],
        out_specs=[pl.BlockSpec((tc, W_IN_HALF), lambda j: (j, 0)), pl.BlockSpec((rows, tc), lambda j: (0, j)),
                   pl.BlockSpec((8, tc), lambda j: (0, j))],
        out_shape=[SDS((D_IN, W_IN_HALF), bf16), SDS((rows, D_IN), bf16), SDS((8, D_IN), f32)],
        name="mm_dwin_0", compiler_params=_cp(("arbitrary",), VMEM_LIMIT_MB),
    )(*[a for a, _ in parts], hb)


def _mm_dwin(hb, dz, after):
    rows = dz.shape[0]
    tc = 640

    def body(dz_ref, h_ref, after_ref, o_ref):
        o_ref[...] = _dot_tn(dz_ref[...], h_ref[...]).astype(bf16)

    return pl.pallas_call(
        body, grid=(D_IN // tc,),
        in_specs=[pl.BlockSpec((rows, tc), lambda j: (0, j)),
                  pl.BlockSpec((rows, W_IN_HALF), lambda j: (0, 1)),
                  pl.BlockSpec(memory_space=pl.ANY)],
        out_specs=pl.BlockSpec((tc, W_IN_HALF), lambda j: (j, 0)),
        out_shape=SDS((D_IN, W_IN_HALF), bf16),
        name="mm_dwin_1", compiler_params=_cp(("arbitrary",), 48),
    )(dz, hb, after)


SCAN_ROWS = 32


def _scan8(a, b, reverse):
    idx = lax.broadcasted_iota(jnp.int32, a.shape, 0)
    for s in (1, 2, 4):
        sh = 8 - s if reverse else s
        a_sh, b_sh = pltpu.roll(a, sh, 0), pltpu.roll(b, sh, 0)
        m = (idx < 8 - s) if reverse else (idx >= s)
        b = jnp.where(m, a * b_sh + b, b)
        a = jnp.where(m, a * a_sh, a)
    return a, b


def _shift_rows(prev8, cur, k):
    ext = jnp.concatenate([prev8, cur], axis=0)
    return pltpu.roll(ext, k, 0)[8:, :]


def _gates(xc, w_ra, b_ra, w_ri, b_ri, ls):
    xb = xc.astype(bf16)
    r = _sigmoid(_dot(xb, w_ra) + b_ra)
    ig = _sigmoid(_dot(xb, w_ri) + b_ri)
    la = LRU_C * r * ls
    a = jnp.exp(la)
    mult = jnp.sqrt(jnp.tanh(-la) * (1.0 + a * a))
    return xb, r, ig, a, mult


_RNN_IN_SPECS = lambda rows: [
    pl.BlockSpec((1, 24, 256), lambda n: (n, 0, 0)),
    pl.BlockSpec((1, RNN_BLOCK), lambda n: (0, n)),
    pl.BlockSpec((N_DEV, 2, None, 32, RNN_BLOCK), lambda n: (0, 0, n, 0, 0)),
    pl.BlockSpec((1, RNN_BLOCK), lambda n: (0, n)),
    pl.BlockSpec((1, RNN_BLOCK), lambda n: (0, n)),
    pl.BlockSpec((1, RNN_BLOCK), lambda n: (0, n)),
]


def _rnn_fwd(z, smallw, conv_b, wrg, b_ra, b_ri, lam):
    rows = z.shape[0]
    nb = rows // BLK
    col = lambda off: pl.BlockSpec((rows, RNN_BLOCK), lambda n: (0, off // RNN_BLOCK + n))

    def body(xr_ref, gr_ref, sw_ref, cb_ref, w_ref, bra_ref, bri_ref, lam_ref, xc_ref, hr_ref, ya_ref, yat_ref, a_s):
        cw = sw_ref[0, N_META:24, :]
        cb = cb_ref[...]
        w_ra = w_ref[:, 0].reshape(RNN_BLOCK, RNN_BLOCK)
        w_ri = w_ref[:, 1].reshape(RNN_BLOCK, RNN_BLOCK)
        b_ra_v, b_ri_v = bra_ref[...], bri_ref[...]
        ls = _log_sigmoid(lam_ref[...])
        rid = lax.broadcasted_iota(jnp.int32, (BLK, 1), 0)

        def blk_step(i, carry):
            r0 = pl.multiple_of(i * BLK, BLK)
            grow = rid + r0
            valid = grow >= ROW0
            cur = jnp.where(valid, xr_ref[pl.ds(r0, BLK), :], 0.0)
            prev8 = xr_ref[pl.ds(pl.multiple_of(jnp.maximum(r0 - 8, 0), 8), 8), :] * (i > 0).astype(f32)
            xc = cb + cw[0:1] * cur
            for k in range(1, CONV_WIDTH):
                xc = xc + cw[k:k + 1] * _shift_rows(prev8, cur, k)
            xc_ref[pl.ds(r0, BLK), :] = xc
            _, _, ig, a, mult = _gates(xc, w_ra, b_ra_v, w_ri, b_ri_v, ls)
            mult = jnp.where(grow == ROW0, 1.0, mult)
            a_s[pl.ds(r0, BLK), :] = a
            hr_ref[pl.ds(r0, BLK), :] = jnp.where(valid, mult * ig * xc, 0.0)
            return carry

        lax.fori_loop(0, nb, blk_step, 0)

        def scan_step(j, carry):
            r0 = pl.multiple_of(j * SCAN_ROWS, SCAN_ROWS)
            tiles = [_scan8(a_s[pl.ds(r0 + 8 * k, 8), :], hr_ref[pl.ds(r0 + 8 * k, 8), :], False)
                     for k in range(SCAN_ROWS // 8)]
            for k, (a, b) in enumerate(tiles):
                h = b + a * carry
                hr_ref[pl.ds(r0 + 8 * k, 8), :] = h
                carry = jnp.broadcast_to(h[7:8, :], (8, RNN_BLOCK))
            return carry

        lax.fori_loop(0, rows // SCAN_ROWS, scan_step, jnp.zeros((8, RNN_BLOCK), f32))

        def gate_step(i, carry):
            r0 = pl.multiple_of(i * BLK, BLK)
            ya_ref[pl.ds(r0, BLK), :] = (hr_ref[pl.ds(r0, BLK), :]
                                         * _silu_and_grad(gr_ref[pl.ds(r0, BLK), :])[0]).astype(bf16)
            return carry

        lax.fori_loop(0, nb, gate_step, 0)
        yat_ref[...] = ya_ref[...].astype(f32).T.astype(bf16)

    return pl.pallas_call(
        body, grid=(N_RNN_BLOCKS,),
        in_specs=[col(0), col(OFF_GR)] + _RNN_IN_SPECS(rows),
        out_specs=[pl.BlockSpec((rows, RNN_BLOCK), lambda n: (0, n))] * 3
                  + [pl.BlockSpec((RNN_BLOCK, rows), lambda n: (n, 0))],
        out_shape=[SDS((rows, D), f32), SDS((rows, D), f32), SDS((rows, D), bf16), SDS((D, rows), bf16)],
        scratch_shapes=[pltpu.VMEM((rows, RNN_BLOCK), f32)],
        name="rnn_fwd", compiler_params=_cp(("arbitrary",)),
    )(z, z, smallw, conv_b, wrg, b_ra, b_ri, lam)


def _rnn_bwd(dya, hr, xc, z, smallw, conv_b, wrg, b_ra, b_ri, lam):
    rows = z.shape[0]
    nb = rows // BLK
    col = lambda off: pl.BlockSpec((rows, RNN_BLOCK), lambda n: (0, off // RNN_BLOCK + n))
    blk = pl.BlockSpec((rows, RNN_BLOCK), lambda n: (0, n))

    def body(dya_ref, hr_ref, xc_ref, xr_ref, gr_ref, sw_ref, cb_ref, w_ref, bra_ref, bri_ref, lam_ref,
             dxr_ref, dgr_ref, dw_ref, vec_ref, a_s, lam_s, dxc_s, r_s, ig_s, mult_s, dw_s):
        cw = sw_ref[0, N_META:24, :]
        w_ra = w_ref[:, 0].reshape(RNN_BLOCK, RNN_BLOCK)
        w_ri = w_ref[:, 1].reshape(RNN_BLOCK, RNN_BLOCK)
        b_ra_v, b_ri_v = bra_ref[...], bri_ref[...]
        lam_v = lam_ref[...]
        ls = _log_sigmoid(lam_v)
        rid = lax.broadcasted_iota(jnp.int32, (BLK, 1), 0)
        zrow = jnp.zeros((1, RNN_BLOCK), f32)

        def p1(i, carry):
            r0 = pl.multiple_of(i * BLK, BLK)
            sl = pl.ds(r0, BLK)
            _, r, ig, a, mult = _gates(xc_ref[sl, :], w_ra, b_ra_v, w_ri, b_ri_v, ls)
            a_s[sl, :] = a
            r_s[sl, :] = r
            ig_s[sl, :] = ig
            mult_s[sl, :] = mult
            sg, dsg = _silu_and_grad(gr_ref[sl, :])
            d = dya_ref[sl, :]
            lam_s[sl, :] = d * sg
            dgr_ref[sl, :] = (d * hr_ref[sl, :] * dsg).astype(bf16)
            return carry

        lax.fori_loop(0, nb, p1, 0)

        def p2(jj, carry):
            r0 = pl.multiple_of((rows // SCAN_ROWS - 1 - jj) * SCAN_ROWS, SCAN_ROWS)
            idx = lax.broadcasted_iota(jnp.int32, (8, RNN_BLOCK), 0)
            tiles = []
            for k in range(SCAN_ROWS // 8):
                sl = pl.ds(r0 + 8 * k, 8)
                a, g = a_s[sl, :], lam_s[sl, :]
                tiles.append((g, *_scan8(a, a * g, True)))
            for k in reversed(range(SCAN_ROWS // 8)):
                g, ca, cb_ = tiles[k]
                mu = cb_ + ca * carry
                lam_s[pl.ds(r0 + 8 * k, 8), :] = g + jnp.where(idx < 7, pltpu.roll(mu, 7, 0), carry)
                carry = jnp.broadcast_to(mu[0:1, :], (8, RNN_BLOCK))
            return carry

        lax.fori_loop(0, rows // SCAN_ROWS, p2, jnp.zeros((8, RNN_BLOCK), f32))

        dw_s[...] = jnp.zeros_like(dw_s)

        def p3(i, carry):
            d_bra, d_bri, d_ls = carry
            r0 = pl.multiple_of(i * BLK, BLK)
            sl = pl.ds(r0, BLK)
            grow = rid + r0
            valid = grow >= ROW0
            first = grow == ROW0
            xcv = xc_ref[sl, :]
            xb = xcv.astype(bf16)
            r, ig, a = r_s[sl, :], ig_s[sl, :], a_s[sl, :]
            mult = jnp.where(first, 1.0, mult_s[sl, :])
            lam_t = lam_s[sl, :]
            du = jnp.where(valid, lam_t, 0.0)
            hprev = _shift_rows(hr_ref[pl.ds(pl.multiple_of(jnp.maximum(r0 - 8, 0), 8), 8), :] * (i > 0).astype(f32), hr_ref[sl, :], 1)
            da = lam_t * hprev
            dmult = jnp.where(first, 0.0, du * ig * xcv)
            di = du * mult * xcv
            dxc = du * mult * ig
            ratio = jnp.where(valid & jnp.logical_not(first), a * a / mult, 0.0)
            dla = da * a - dmult * ratio
            dpr = (dla * (LRU_C * ls)) * r * (1.0 - r)
            dpi = di * ig * (1.0 - ig)
            dprb, dpib = dpr.astype(bf16), dpi.astype(bf16)
            dw_s[0] += _dot_tn(xb, dprb)
            dw_s[1] += _dot_tn(xb, dpib)
            dxc_s[sl, :] = dxc + _dot_nt(dprb, w_ra) + _dot_nt(dpib, w_ri)
            return d_bra + _colsum(dpr), d_bri + _colsum(dpi), d_ls + _colsum(dla * (LRU_C * r))

        d_bra, d_bri, d_ls = lax.fori_loop(0, nb, p3, (zrow, zrow, zrow))

        def p4(i, carry):
            d_cb, d_w0, d_w1, d_w2, d_w3 = carry
            r0 = pl.multiple_of(i * BLK, BLK)
            sl = pl.ds(r0, BLK)
            grow = rid + r0
            valid = grow >= ROW0
            dxc = dxc_s[sl, :]
            nxt = dxc_s[pl.ds(pl.multiple_of(jnp.minimum(r0 + BLK, rows - 8), 8), 8), :] * (i < nb - 1).astype(f32)
            ext = jnp.concatenate([dxc, nxt], axis=0)
            dxr = cw[0:1] * dxc
            for k in range(1, CONV_WIDTH):
                dxr = dxr + cw[k:k + 1] * pltpu.roll(ext, BLK + 8 - k, 0)[:BLK, :]
            dxr_ref[sl, :] = jnp.where(valid, dxr, 0.0).astype(bf16)
            cur = jnp.where(valid, xr_ref[sl, :], 0.0)
            prev8 = xr_ref[pl.ds(pl.multiple_of(jnp.maximum(r0 - 8, 0), 8), 8), :] * (i > 0).astype(f32)
            dws = [d_w0 + _colsum(dxc * cur)]
            for k, acc in ((1, d_w1), (2, d_w2), (3, d_w3)):
                dws.append(acc + _colsum(dxc * _shift_rows(prev8, cur, k)))
            return (d_cb + _colsum(dxc), *dws)

        d_cb, d_w0, d_w1, d_w2, d_w3 = lax.fori_loop(0, nb, p4, (zrow,) * 5)

        d_lam = d_ls * _sigmoid(-lam_v)
        vec_ref[...] = jnp.concatenate([d_bra, d_bri, d_lam, d_cb, d_w0, d_w1, d_w2, d_w3], axis=0)
        dw_ref[:, 0] = dw_s[0].astype(bf16).reshape(N_DEV, 32, RNN_BLOCK)
        dw_ref[:, 1] = dw_s[1].astype(bf16).reshape(N_DEV, 32, RNN_BLOCK)

    return pl.pallas_call(
        body, grid=(N_RNN_BLOCKS,),
        in_specs=[blk, blk, blk, col(0), col(OFF_GR)] + _RNN_IN_SPECS(rows),
        out_specs=[blk, blk,
                   pl.BlockSpec((N_DEV, 2, None, 32, RNN_BLOCK), lambda n: (0, 0, n, 0, 0)),
                   pl.BlockSpec((8, RNN_BLOCK), lambda n: (0, n))],
        out_shape=[SDS((rows, D), bf16), SDS((rows, D), bf16),
                   SDS((N_DEV, 2, N_RNN_BLOCKS, 32, RNN_BLOCK), bf16), SDS((8, D), f32)],
        scratch_shapes=[pltpu.VMEM((rows, RNN_BLOCK), f32)] * 6 + [pltpu.VMEM((2, RNN_BLOCK, RNN_BLOCK), f32)],
        name="rnn_bwd", compiler_params=_cp(("arbitrary",), 48),
    )(dya, hr, xc, z, z, smallw, conv_b, wrg, b_ra, b_ri, lam)


def _rope_tables(rows):
    half = jnp.arange(HALF, dtype=f32)
    inv = ROPE_THETA ** (-half / HALF)
    pos = (jnp.arange(rows) - ROW0).astype(f32)
    ang = pos[:, None] * inv[None, :]
    cos, sin = jnp.cos(ang), jnp.sin(ang)
    cos128 = jnp.concatenate([cos, cos, cos, cos], axis=1)
    sin128 = jnp.concatenate([-sin, sin, -sin, sin], axis=1)
    return cos128, sin128


def _rope128(x, cos128, sin128):
    lane = lax.broadcasted_iota(jnp.int32, x.shape, 1)
    swapped = jnp.where(lane % HEAD_DIM < HALF, pltpu.roll(x, 128 - HALF, 1), pltpu.roll(x, HALF, 1))
    return x * cos128 + swapped * sin128


def _qkv_prep(z, cos128, sin128):
    rows = z.shape[0]

    def body(q_ref, kv_ref, c_ref, s_ref, qo_ref, ko_ref, vo_ref):
        c, s = c_ref[...], s_ref[...]
        for g in range(D // 128):
            qo_ref[:, g * 128:(g + 1) * 128] = (_rope128(q_ref[:, g * 128:(g + 1) * 128], c, s)
                                                * (HEAD_DIM ** -0.5)).astype(bf16)
        for g in range(2):
            kr = _rope128(kv_ref[:, g * 128:(g + 1) * 128], c, s)
            for j in range(2):
                ko_ref[2 * g + j] = kr[:, j * HEAD_DIM:(j + 1) * HEAD_DIM].astype(bf16)
        for h in range(N_KV):
            vo_ref[h] = kv_ref[:, 256 + h * HEAD_DIM:256 + (h + 1) * HEAD_DIM].astype(bf16)

    return pl.pallas_call(
        body, grid=(rows // BLK,),
        in_specs=[pl.BlockSpec((BLK, D), lambda i: (i, OFF_Q // D)),
                  pl.BlockSpec((BLK, 512), lambda i: (i, OFF_K // 512)),
                  pl.BlockSpec((BLK, 128), lambda i: (i, 0)),
                  pl.BlockSpec((BLK, 128), lambda i: (i, 0))],
        out_specs=[pl.BlockSpec((BLK, D), lambda i: (i, 0)),
                   pl.BlockSpec((N_KV, BLK, HEAD_DIM), lambda i: (0, i, 0)),
                   pl.BlockSpec((N_KV, BLK, HEAD_DIM), lambda i: (0, i, 0))],
        out_shape=[SDS((rows, D), bf16), SDS((N_KV, rows, HEAD_DIM), bf16), SDS((N_KV, rows, HEAD_DIM), bf16)],
        name="qkv_prep", compiler_params=_cp(("arbitrary",)),
    )(z, z, cos128, sin128)


def _attn_mask(n):
    qi = n * BLK + lax.broadcasted_iota(jnp.int32, (BLK, 2 * BLK + N_META), 0)
    c = lax.broadcasted_iota(jnp.int32, (BLK, 2 * BLK + N_META), 1)
    jb = (n - 1) * BLK + c
    band = (jb >= BLK) & (jb <= qi) & (qi - jb < BLK)
    meta = (ROW0 + c - 2 * BLK) <= qi
    return ((c < 2 * BLK) & band) | ((c >= 2 * BLK) & meta)


N_KEYS = 2 * BLK + N_META


def _stack_heads(t):
    return jnp.concatenate([t[:, g * HEAD_DIM:(g + 1) * HEAD_DIM] for g in range(GROUP)], axis=0)


def _sink_column(sink_ref, h):
    g = lax.broadcasted_iota(jnp.int32, (GROUP, 1, 1), 0)
    col = jnp.zeros((GROUP, 1, 1), f32)
    for j in range(GROUP):
        col = jnp.where(g == j, sink_ref[h * GROUP + j], col)
    return col


def _kv_specs(last):
    cl = lambda n: jnp.minimum(n, last)
    return [pl.BlockSpec((None, N_META, HEAD_DIM), lambda h, n: (h, ROW0 // N_META, 0)),
            pl.BlockSpec((None, BLK, HEAD_DIM), lambda h, n: (h, jnp.maximum(cl(n) - 1, 0), 0)),
            pl.BlockSpec((None, BLK, HEAD_DIM), lambda h, n: (h, cl(n), 0))]


def _attn_fwd(q_r, k_r, v_b, z, sinks):
    rows = q_r.shape[0]
    nb = rows // BLK

    def body(sink_ref, q_ref, km_ref, kp_ref, kc_ref, vm_ref, vp_ref, vc_ref, ga_ref, o_ref, yb_ref, ybt_ref, lse_ref):
        h, n = pl.program_id(0), pl.program_id(1)
        kk = jnp.concatenate([kp_ref[...], kc_ref[...], km_ref[...]], axis=0)
        vv = jnp.concatenate([vp_ref[...], vc_ref[...], vm_ref[...]], axis=0)
        q2 = _stack_heads(q_ref[...])
        s = jnp.where(_attn_mask(n)[None], _dot_nt(q2, kk).reshape(GROUP, BLK, N_KEYS), NEG_INF)
        sink = _sink_column(sink_ref, h)
        m = jnp.maximum(jnp.max(s, axis=-1, keepdims=True), sink)
        p = jnp.exp(s - m)
        den = jnp.sum(p, axis=-1, keepdims=True) + jnp.exp(sink - m)
        o2 = _dot((p / den).astype(bf16).reshape(GROUP * BLK, N_KEYS), vv)
        lse = m + jnp.log(den)
        for g in range(GROUP):
            o_ref[:, g * HEAD_DIM:(g + 1) * HEAD_DIM] = o2[g * BLK:(g + 1) * BLK]
            lse_ref[:, g:g + 1] = lse[g]
        yb = o_ref[...] * _silu_and_grad(ga_ref[...])[0]
        yb_ref[...] = yb.astype(bf16)
        ybt_ref[...] = yb.T.astype(bf16)

    tile = pl.BlockSpec((BLK, 512), lambda h, n: (n, h))
    return pl.pallas_call(
        body, grid=(N_KV, nb),
        in_specs=[pl.BlockSpec(memory_space=pltpu.SMEM), tile] + _kv_specs(nb - 1) + _kv_specs(nb - 1)
                 + [pl.BlockSpec((BLK, 512), lambda h, n: (n, OFF_GA // 512 + h))],
        out_specs=[tile, tile, pl.BlockSpec((512, BLK), lambda h, n: (h, n)),
                   pl.BlockSpec((None, BLK, GROUP), lambda h, n: (h, n, 0))],
        out_shape=[SDS((rows, D), f32), SDS((rows, D), bf16), SDS((D, rows), bf16),
                   SDS((N_KV, rows, GROUP), f32)],
        name="attn_fwd", compiler_params=_cp(("arbitrary", "arbitrary")),
    )(sinks, q_r, k_r, k_r, k_r, v_b, v_b, v_b, z)


def _attn_bwd(dyb, o32, lse, q_r, k_r, v_b, z, sinks):
    rows = q_r.shape[0]
    nb = rows // BLK
    cl = lambda n: jnp.minimum(n, nb - 1)

    def body(sink_ref, dyb_ref, o_ref, lse_ref, q_ref, km_ref, kp_ref, kc_ref, vm_ref, vp_ref, vc_ref, ga_ref,
             dq_ref, dga_ref, dk_ref, dv_ref, dkm_ref, dvm_ref, dsr_ref, ck_s, cv_s):
        h, n = pl.program_id(0), pl.program_id(1)

        @pl.when(n == 0)
        def _():
            dkm_ref[...] = jnp.zeros_like(dkm_ref)
            dvm_ref[...] = jnp.zeros_like(dvm_ref)
            ck_s[...] = jnp.zeros_like(ck_s)
            cv_s[...] = jnp.zeros_like(cv_s)

        @pl.when(n < nb)
        def _():
            kk = jnp.concatenate([kp_ref[...], kc_ref[...], km_ref[...]], axis=0)
            vv = jnp.concatenate([vp_ref[...], vc_ref[...], vm_ref[...]], axis=0)
            sg, dsg = _silu_and_grad(ga_ref[...])
            dyb_v = dyb_ref[...]
            o_v = o_ref[...]
            dga_ref[...] = (dyb_v * o_v * dsg).astype(bf16)
            q2 = _stack_heads(q_ref[...])
            do2 = _stack_heads(dyb_v * sg)
            lse_v = lse_ref[...]
            lse = jnp.concatenate([lse_v[:, g:g + 1] for g in range(GROUP)], axis=0).reshape(GROUP, BLK, 1)
            delta = jnp.sum(do2 * _stack_heads(o_v), axis=-1, keepdims=True).reshape(GROUP, BLK, 1)
            s = jnp.where(_attn_mask(n)[None], _dot_nt(q2, kk).reshape(GROUP, BLK, N_KEYS), NEG_INF)
            p = jnp.exp(s - lse)
            do2b = do2.astype(bf16)
            ds = (p * (_dot_nt(do2b, vv).reshape(GROUP, BLK, N_KEYS) - delta)).astype(bf16)
            ds = ds.reshape(GROUP * BLK, N_KEYS)
            dsr = -jnp.exp(_sink_column(sink_ref, h) - lse) * delta
            dq2 = _dot(ds, kk)
            for g in range(GROUP):
                dq_ref[:, g * HEAD_DIM:(g + 1) * HEAD_DIM] = dq2[g * BLK:(g + 1) * BLK]
                dsr_ref[:, g:g + 1] = dsr[g]
            dkk = _dot_tn(ds, q2)
            dvv = _dot_tn(p.astype(bf16).reshape(GROUP * BLK, N_KEYS), do2b)
            dk_ref[...] = ck_s[...] + dkk[:BLK]
            dv_ref[...] = cv_s[...] + dvv[:BLK]
            ck_s[...] = dkk[BLK:2 * BLK]
            cv_s[...] = dvv[BLK:2 * BLK]
            dkm_ref[...] += dkk[2 * BLK:]
            dvm_ref[...] += dvv[2 * BLK:]

        @pl.when(n == nb)
        def _():
            dk_ref[...] = ck_s[...]
            dv_ref[...] = cv_s[...]

    tile = pl.BlockSpec((BLK, 512), lambda h, n: (cl(n), h))
    kvout = pl.BlockSpec((None, BLK, HEAD_DIM), lambda h, n: (h, jnp.maximum(n - 1, 0), 0))
    mout = pl.BlockSpec((None, N_META, HEAD_DIM), lambda h, n: (h, 0, 0))
    stat = pl.BlockSpec((None, BLK, GROUP), lambda h, n: (h, cl(n), 0))
    return pl.pallas_call(
        body, grid=(N_KV, nb + 1),
        in_specs=[pl.BlockSpec(memory_space=pltpu.SMEM), tile, tile, stat, tile] + _kv_specs(nb - 1)
                 + _kv_specs(nb - 1) + [pl.BlockSpec((BLK, 512), lambda h, n: (cl(n), OFF_GA // 512 + h))],
        out_specs=[tile, tile, kvout, kvout, mout, mout, stat],
        out_shape=[SDS((rows, D), f32), SDS((rows, D), bf16),
                   SDS((N_KV, rows, HEAD_DIM), f32), SDS((N_KV, rows, HEAD_DIM), f32),
                   SDS((N_KV, N_META, HEAD_DIM), f32), SDS((N_KV, N_META, HEAD_DIM), f32),
                   SDS((N_KV, rows, GROUP), f32)],
        scratch_shapes=[pltpu.VMEM((BLK, HEAD_DIM), f32), pltpu.VMEM((BLK, HEAD_DIM), f32)],
        name="attn_bwd", compiler_params=_cp(("arbitrary", "arbitrary")),
    )(sinks, dyb, o32, lse, q_r, k_r, k_r, k_r, v_b, v_b, v_b, z)


def _qkv_finish(dq, dk, dv, dkm, dvm, cos128, sin128):
    rows = dq.shape[0]

    def body(dq_ref, dk_ref, dv_ref, dkm_ref, dvm_ref, c_ref, s_ref, oq_ref, okv_ref):
        first = (pl.program_id(0) == 0).astype(f32)
        c, s = c_ref[...], -s_ref[...]
        for g in range(D // 128):
            oq_ref[:, g * 128:(g + 1) * 128] = (_rope128(dq_ref[:, g * 128:(g + 1) * 128], c, s)
                                                * (HEAD_DIM ** -0.5)).astype(bf16)
        pad = jnp.zeros((ROW0, HEAD_DIM), f32)
        ks = [dk_ref[h] + first * jnp.concatenate([pad, dkm_ref[h]], axis=0) for h in range(N_KV)]
        vs = [dv_ref[h] + first * jnp.concatenate([pad, dvm_ref[h]], axis=0) for h in range(N_KV)]
        for g in range(2):
            kp = jnp.concatenate([ks[2 * g], ks[2 * g + 1]], axis=1)
            okv_ref[:, g * 128:(g + 1) * 128] = _rope128(kp, c, s).astype(bf16)
            okv_ref[:, 256 + g * 128:256 + (g + 1) * 128] = jnp.concatenate([vs[2 * g], vs[2 * g + 1]], axis=1).astype(bf16)

    kv = pl.BlockSpec((N_KV, BLK, HEAD_DIM), lambda i: (0, i, 0))
    mt = pl.BlockSpec((N_KV, N_META, HEAD_DIM), lambda i: (0, 0, 0))
    return pl.pallas_call(
        body, grid=(rows // BLK,),
        in_specs=[pl.BlockSpec((BLK, D), lambda i: (i, 0)), kv, kv, mt, mt,
                  pl.BlockSpec((BLK, 128), lambda i: (i, 0)), pl.BlockSpec((BLK, 128), lambda i: (i, 0))],
        out_specs=[pl.BlockSpec((BLK, D), lambda i: (i, 0)), pl.BlockSpec((BLK, 512), lambda i: (i, 0))],
        out_shape=[SDS((rows, D), bf16), SDS((rows, 512), bf16)],
        name="qkv_finish", compiler_params=_cp(("arbitrary",)),
    )(dq, dk, dv, dkm, dvm, cos128, sin128)


_TW = 512


def _mix_specs(rows):
    tr = _row_chunk(rows)
    tile = pl.BlockSpec((tr, _TW), lambda i, j: (i, j))
    ga = pl.BlockSpec((tr, _TW), lambda i, j: (i, OFF_G // _TW + j))
    gb = pl.BlockSpec((tr, _TW), lambda i, j: (i, (OFF_G + D) // _TW + j))
    return (rows // tr, D // _TW), tile, ga, gb


def _mix_fwd(y_a, y_b, z):
    rows = y_a.shape[0]
    tw = 256
    col = lambda off: pl.BlockSpec((rows, tw), lambda j: (0, off // tw + j))

    def body(ya_ref, yb_ref, ga_ref, gb_ref, o_ref, ot_ref):
        mixed = (_sigmoid(ga_ref[...]) * ya_ref[...].astype(f32)
                 + _sigmoid(gb_ref[...]) * yb_ref[...].astype(f32))
        o_ref[...] = mixed.astype(bf16)
        ot_ref[...] = mixed.T.astype(bf16)

    return pl.pallas_call(
        body, grid=(D // tw,), in_specs=[col(0), col(0), col(OFF_G), col(OFF_G + D)],
        out_specs=[col(0), pl.BlockSpec((tw, rows), lambda j: (j, 0))],
        out_shape=[SDS((rows, D), bf16), SDS((D, rows), bf16)],
        name="mix_fwd", compiler_params=_cp(("arbitrary",)),
    )(y_a, y_b, z, z)


def _mix_bwd(dmixed, y_a, y_b, z):
    rows = y_a.shape[0]
    grid, _mix_tile, _mix_ga, _mix_gb = _mix_specs(rows)

    def body(dm_ref, ya_ref, yb_ref, ga_ref, gb_ref, dya_ref, dyb_ref, dga_ref, dgb_ref):
        dm = dm_ref[...].astype(f32)
        sa, sb = _sigmoid(ga_ref[...]), _sigmoid(gb_ref[...])
        dya_ref[...] = (dm * sa).astype(bf16)
        dyb_ref[...] = (dm * sb).astype(bf16)
        dga_ref[...] = (dm * ya_ref[...].astype(f32) * sa * (1.0 - sa)).astype(bf16)
        dgb_ref[...] = (dm * yb_ref[...].astype(f32) * sb * (1.0 - sb)).astype(bf16)

    return pl.pallas_call(
        body, grid=grid, in_specs=[_mix_tile, _mix_tile, _mix_tile, _mix_ga, _mix_gb],
        out_specs=[_mix_tile] * 4, out_shape=[SDS((rows, D), bf16)] * 4,
        name="mix_bwd", compiler_params=_cp(("arbitrary", "arbitrary")),
    )(dmixed, y_a, y_b, z, z)


def _final_ln(out32, h32, tgt, ln_g, ln_b):
    rows = out32.shape[0]

    def body(o_ref, h_ref, t_ref, g_ref, b_ref, du_ref, dub_ref, st_ref):
        i = pl.program_id(0)
        g = g_ref[...]
        y, xhat, rstd = _ln_rows(ALPHA * h_ref[...] + o_ref[...], g, b_ref[...])
        e = jnp.where(i > 0, y - t_ref[0], 0.0)
        dy = e * (1.0 / D)
        du = _ln_rows_bwd(dy, g, xhat, rstd)
        du_ref[...] = du
        dub_ref[...] = du.astype(bf16)
        st = jnp.concatenate([_colsum(dy * xhat), _colsum(dy), _colsum(du), _colsum(e * e) * (0.5 / D),
                              jnp.zeros((4, D), f32)], axis=0)

        @pl.when(i == 0)
        def _():
            st_ref[...] = st

        @pl.when(i > 0)
        def _():
            st_ref[...] += st

    row = pl.BlockSpec((BLK, D), lambda i: (i, 0))
    vec = pl.BlockSpec((1, D), lambda i: (0, 0))
    return pl.pallas_call(
        body, grid=(rows // BLK,),
        in_specs=[row, row, pl.BlockSpec((1, BLK, D), lambda i: (0, jnp.maximum(i - 1, 0), 0)), vec, vec],
        out_specs=[row, row, pl.BlockSpec((8, D), lambda i: (0, 0))],
        out_shape=[SDS((rows, D), f32), SDS((rows, D), bf16), SDS((8, D), f32)],
        name="final_ln", compiler_params=_cp(("arbitrary",)),
    )(out32, h32, tgt, ln_g, ln_b)


def _step_rnn(h32, hb, z, wrg, smallw, p, zero):
    rows = z.shape[0]
    cos128, sin128 = _rope_tables(rows)
    cos128 = cos128 + zero
    xc, hr, ya, ya_t = _rnn_fwd(z, smallw, p["conv_b"] + zero, wrg, p["b_ra"], p["b_ri"], p["lru_lambda"])
    q_r, k_r, v_b = _qkv_prep(z, cos128, sin128)
    return dict(cos128=cos128, sin128=sin128, h32=h32, hb=hb, z=z, xc=xc, hr=hr, ya=ya, ya_t=ya_t,
                q_r=q_r, k_r=k_r, v_b=v_b)


def _step_attn(s, p, zero):
    sinks = p["sinks"].reshape(N_KV * GROUP) + zero[0]
    o32, yb, yb_t, lse = _attn_fwd(s["q_r"], s["k_r"], s["v_b"], s["z"], sinks)
    return dict(s, sinks=sinks, o32=o32, yb=yb, yb_t=yb_t, lse=lse)


def _step_merge(s, tgt, w3, p):
    ya, yb, z = s["ya"], s["yb"], s["z"]
    y_a = _mm(ya, w3, sel=0, out_dtype=bf16, name="mm_ya")
    y_b = _mm(yb, w3, sel=1, out_dtype=bf16, name="mm_yb")
    mixed, mixed_t = _mix_fwd(y_a, y_b, z)
    out32 = _mm(mixed, w3, sel=2, bias=p["b_o"], name="mm_out")
    du32, dub, st_out = _final_ln(out32, s["h32"], tgt, p["ln_g"], p["ln_b"])

    g_wo = _mm(mixed_t, dub, out_dtype=bf16, name="mm_dwo")
    dmixed = _mm(dub, w3, sel=2, nt=True, out_dtype=bf16, name="mm_dmixed")
    dya_b, dyb_b, dma, dmb = _mix_bwd(dmixed, y_a, y_b, z)
    g_wrnn = _mm(s["ya_t"], dya_b, out_dtype=bf16, name="mm_dwrnn")
    g_wattn = _mm(s["yb_t"], dyb_b, out_dtype=bf16, name="mm_dwattn")
    dya = _mm(dya_b, w3, sel=0, nt=True, name="mm_dya")
    dyb = _mm(dyb_b, w3, sel=1, nt=True, name="mm_dyb")
    return dict(du32=du32, st_out=st_out, dma=dma, dmb=dmb, dya=dya, dyb=dyb, g_wo=g_wo, g_wrnn=g_wrnn,
                g_wattn=g_wattn)


def _step_backward(s, t, wrg, smallw, p, conv_b):
    z = s["z"]
    dxr, dgr, g_wrg, vec_rnn = _rnn_bwd(t["dya"], s["hr"], s["xc"], z, smallw, conv_b, wrg, p["b_ra"], p["b_ri"],
                                        p["lru_lambda"])
    dq_r, dga, dk, dv, dkm, dvm, dsr = _attn_bwd(t["dyb"], s["o32"], s["lse"], s["q_r"], s["k_r"], s["v_b"], z,
                                                 s["sinks"])
    dq, dkv = _qkv_finish(dq_r, dk, dv, dkm, dvm, s["cos128"], s["sin128"])
    dz_parts = [(dxr, D), (dgr, D), (dq, D), (dkv, 512), (dga, D), (t["dma"], D), (t["dmb"], D)]
    return dict(vec_rnn=vec_rnn, dsr=dsr, g_wrg=g_wrg, dz_parts=dz_parts)


def _step_input_grad(dh_lo, dh_hi, du32, x, smallw, p, after):
    grad_x, dmeta, st_emb = _ln_emb_bwd(dh_lo, dh_hi, du32, x, smallw, p["ln_emb_g"], after)
    return dict(grad_x=grad_x, dmeta=dmeta, st_emb=st_emb)


_ANY = pl.BlockSpec(memory_space=pl.ANY)
_VMEM = pl.BlockSpec(memory_space=pltpu.VMEM)
_HBM = pl.BlockSpec(memory_space=pltpu.HBM)
_SEM = pl.BlockSpec(memory_space=pltpu.SEMAPHORE)


def _place():
    x, y, c = lax.axis_index("x"), lax.axis_index("y"), lax.axis_index("c")
    return x, y, c


def _dev(px, py, pc):
    return 4 * px + 2 * py + pc


def _tile_rows(r):
    return max(t for t in range(16, 321, 16) if r % t == 0) if r > 320 else r


def _cast_w_in(w_in_t, me_idx):
    tm = _tile_rows(SHARD_IN)

    def body(me_ref, i_ref, o_ref):
        o_ref[...] = i_ref[...].astype(bf16)

    return pl.pallas_call(
        body,
        grid_spec=pltpu.PrefetchScalarGridSpec(
            num_scalar_prefetch=1, grid=(SHARD_IN // tm,),
            in_specs=[pl.BlockSpec((tm, D), lambda i, me_ref: (i, 0))],
            out_specs=pl.BlockSpec((None, tm, D), lambda i, me_ref: (me_ref[0], i, 0))),
        out_shape=SDS((N_DEV, SHARD_IN, D), bf16), name="cast_w_in", compiler_params=_cp(("arbitrary",)),
    )(me_idx, w_in_t)


def _cast_small(me_idx, w_rnn_out, w_attn_out, w_o, w_ra, w_ri, meta, conv_w):
    def body(me_ref, a_ref, b_ref, c_ref, ra_ref, ri_ref, m_ref, cw_ref, w3_ref, wrg_ref, sw_ref):
        w3_ref[0] = a_ref[0].astype(bf16)
        w3_ref[1] = b_ref[0].astype(bf16)
        w3_ref[2] = c_ref[0].astype(bf16)
        wrg_ref[0] = ra_ref[0].astype(bf16)
        wrg_ref[1] = ri_ref[0].astype(bf16)
        sw_ref[...] = jnp.concatenate([m_ref[...], cw_ref[0], jnp.zeros((4, 256), f32)], axis=0)

    args = (w_rnn_out, w_attn_out, w_o, w_ra, w_ri, meta, conv_w)
    whole = lambda shape: pl.BlockSpec(shape, lambda i, me_ref: (0,) * len(shape))
    slot = lambda shape: pl.BlockSpec((None, *shape), lambda i, me_ref: (me_ref[0], *([0] * len(shape))))
    shapes = [(3, 256, D), (2, N_RNN_BLOCKS, 32, RNN_BLOCK), (24, 256)]
    return pl.pallas_call(
        body,
        grid_spec=pltpu.PrefetchScalarGridSpec(
            num_scalar_prefetch=1, grid=(1,), in_specs=[whole(a.shape) for a in args],
            out_specs=[slot(sh) for sh in shapes]),
        out_shape=[SDS((N_DEV, *sh), dt) for sh, dt in zip(shapes, (bf16, bf16, f32))],
        name="cast_small", compiler_params=_cp(("arbitrary",)),
    )(me_idx, *args)


def _remote(src, dst, send_sems, recv_sems, k, to):
    return pltpu.make_async_remote_copy(src_ref=src, dst_ref=dst, send_sem=send_sems.at[k], recv_sem=recv_sems.at[k],
                                        device_id=to, device_id_type=MESH)


def _w_in_rows(core, early):
    if early:
        return (1 - core) * W_IN_LATE, W_IN_EARLY
    return core * W_IN_EARLY, W_IN_LATE


def _all_gather(bufs, chunks):
    n = len(bufs)
    base = [0]
    for ch in chunks:
        base.append(base[-1] + 7 * ch)

    def body(*refs):
        outs = refs[n:2 * n]
        send_sems, recv_sems = refs[2 * n:]
        x, y, c = _place()
        me, sibling = (x, y, c), (x, y, 1 - c)
        chips = [(1 - x, y), (x, 1 - y), (1 - x, 1 - y)]

        def copy(a, i, k, block, to):
            blk = outs[a].at[_dev(*block)]
            if a == 0:
                r0, r = _w_in_rows(block[2], True)
                r = r // chunks[a]
                blk = blk.at[pl.ds(pl.multiple_of(r0 + i * r, 32), r)]
            return _remote(blk, blk, send_sems, recv_sems, base[a] + 7 * i + k, to)

        pieces = [(a, i) for a in range(n) for i in range(chunks[a])]
        first = []
        for a, i in pieces:
            first.append(copy(a, i, 0, me, sibling))
            first += [copy(a, i, 1 + j, me, (*chip, c)) for j, chip in enumerate(chips)]
        for cp in first:
            cp.start()
        passed = []
        for a, i in pieces:
            for j, chip in enumerate(chips):
                copy(a, i, 1 + j, (*chip, c), me).wait_recv()
                cp = copy(a, i, 4 + j, (*chip, c), sibling)
                cp.start()
                passed.append(cp)
        for a, i in pieces:
            copy(a, i, 0, sibling, me).wait_recv()
            for j, chip in enumerate(chips):
                copy(a, i, 4 + j, (*chip, 1 - c), me).wait_recv()
        for cp in first + passed:
            cp.wait_send()

    return pl.pallas_call(
        body, in_specs=[_ANY] * n, out_specs=[_ANY] * n,
        out_shape=[SDS(b.shape, b.dtype) for b in bufs],
        input_output_aliases={a: a for a in range(n)},
        scratch_shapes=[pltpu.SemaphoreType.DMA((base[-1],)), pltpu.SemaphoreType.DMA((base[-1],))],
        name="all_gather_weights",
    )(*bufs)


def _late_rows(buf, block):
    r0, r = _w_in_rows(block[2], False)
    return buf.at[_dev(*block)].at[pl.ds(pl.multiple_of(r0, 64), r)]


def _whole_block(buf, block):
    return buf.at[_dev(*block)]


def _copies_own(part):
    def make(srcs, lands, send_sems, recv_sems):
        x, y, c = _place()
        peers = [(x, y, 1 - c), (1 - x, y, c), (x, 1 - y, c), (1 - x, 1 - y, c)]
        out = []
        for a in range(len(srcs)):
            blk = part(srcs[a], (x, y, c))
            out += [_remote(blk, blk, send_sems, recv_sems, 4 * a + k, to) for k, to in enumerate(peers)]
        return out
    return make


def _copies_pass(part):
    def make(srcs, lands, send_sems, recv_sems):
        x, y, c = _place()
        out = []
        for a in range(len(srcs)):
            for j, chip in enumerate([(1 - x, y), (x, 1 - y), (1 - x, 1 - y)]):
                blk = part(srcs[a], (*chip, c))
                out.append(_remote(blk, blk, send_sems, recv_sems, 3 * a + j, (x, y, 1 - c)))
        return out
    return make


_PEER_FLIPS = [(f // 4, (f // 2) % 2, f % 2) for f in range(1, N_DEV)]


def _copies_direct(same_src):
    def make(srcs, lands, send_sems, recv_sems):
        x, y, c = _place()
        me = _dev(x, y, c)
        out = []
        for a in range(len(srcs)):
            for k, (fx, fy, fc) in enumerate(_PEER_FLIPS):
                peer = ((x + fx) % 2, (y + fy) % 2, (c + fc) % 2)
                src = srcs[a] if same_src else srcs[a].at[_dev(*peer)]
                out.append(_remote(src, lands[a].at[me], send_sems, recv_sems, 7 * a + k, peer))
        return out
    return make


def _copies_siblings(srcs, lands, send_sems, recv_sems):
    x, y, c = _place()
    return [_remote(srcs[a].at[2 * q + (1 - c)], lands[a].at[q], send_sems, recv_sems, 4 * a + q, (x, y, 1 - c))
            for a in range(len(srcs)) for q in range(4)]


def _copies_chips(srcs, lands, send_sems, recv_sems):
    x, y, c = _place()
    chips = [(1 - x, y), (x, 1 - y), (1 - x, 1 - y)]
    return [_remote(srcs[a].at[2 * qx + qy], lands[a].at[j], send_sems, recv_sems, 3 * a + j, (qx, qy, c))
            for a in range(len(srcs)) for j, (qx, qy) in enumerate(chips)]


def _split_start(make, per_array, srcs, lands, dep, name):
    n, tot = len(srcs), len(srcs) + len(lands)

    def body(*refs):
        send_sems, recv_sems, token = refs[tot + 1], refs[tot + 2], refs[-1]
        for cp in make(refs[:n], refs[n:tot], send_sems, recv_sems):
            cp.start()
        token[...] = jnp.zeros_like(token)

    hbm = lambda t: pltpu.with_memory_space_constraint(t, pltpu.HBM)
    res = pl.pallas_call(
        body, name=name,
        out_shape=(pltpu.SemaphoreType.DMA((per_array * n,)), pltpu.SemaphoreType.DMA((per_array * n,)),
                   *[pltpu.HBM(t.shape, t.dtype) for t in (*srcs, *lands)], SDS((8, 128), f32)),
        in_specs=[_HBM] * tot + [_ANY], out_specs=(_SEM, _SEM, *([_HBM] * tot), _VMEM),
        input_output_aliases={i: 2 + i for i in range(tot)},
        compiler_params=pltpu.CompilerParams(has_side_effects=pltpu.SideEffectType.DATAFLOW_SIDE_EFFECTING),
    )(*[hbm(t) for t in (*srcs, *lands)], dep)
    return res[0], res[1], list(res[2:2 + n]), list(res[2 + n:2 + tot]), res[-1]


def _split_wait(make, send_sems, recv_sems, srcs, lands, after, name):
    n, tot = len(srcs), len(srcs) + len(lands)

    def body(*refs):
        for cp in make(refs[:n], refs[n:tot], refs[tot], refs[tot + 1]):
            cp.wait_send()
            cp.wait_recv()

    res = pl.pallas_call(
        body, name=name,
        out_shape=tuple(pltpu.HBM(t.shape, t.dtype) for t in (*srcs, *lands)),
        in_specs=[_HBM] * tot + [_SEM, _SEM, _ANY], out_specs=tuple([_HBM] * tot),
        input_output_aliases={i: i for i in range(tot)},
        compiler_params=pltpu.CompilerParams(has_side_effects=pltpu.SideEffectType.DATAFLOW_SIDE_EFFECTING),
    )(*srcs, *lands, send_sems, recv_sems, after)
    return list(res[:n]), list(res[n:])


def _adamw_direct(g, land, me_idx, w, m, v, name):
    r, wd = w.shape
    tr = min(r, 256)

    def body(me_ref, *refs):
        g_ref, peers = refs[0], refs[1:N_DEV]
        w_ref, m_ref, v_ref, g_out, d_out, m_out, v_out = refs[N_DEV:]
        gs = g_ref[...].astype(f32)
        for p_ref in peers:
            gs = gs + p_ref[...].astype(f32)
        d, mn, vn = _adamw(w_ref[...], gs, m_ref[...], v_ref[...])
        g_out[...] = gs
        d_out[...] = d
        m_out[...] = mn
        v_out[...] = vn

    tile = pl.BlockSpec((tr, wd), lambda i, me_ref: (i, 0))
    slot = lambda k: pl.BlockSpec((None, tr, wd), lambda i, me_ref: ((me_ref[0] + k) % N_DEV, i, 0))
    return pl.pallas_call(
        body,
        grid_spec=pltpu.PrefetchScalarGridSpec(
            num_scalar_prefetch=1, grid=(r // tr,),
            in_specs=[slot(0)] + [slot(k) for k in range(1, N_DEV)] + [tile, tile, tile],
            out_specs=[tile] * 4),
        out_shape=[SDS((r, wd), f32)] * 4, name=name, compiler_params=_cp(("arbitrary",), 48),
    )(me_idx, g, *([land] * (N_DEV - 1)), w, m, v)


def _pair_sum(g, r1, c_idx, name):
    _, r, w = g.shape
    tr = _tile_rows(r)

    def body(c_ref, g_ref, r_ref, o_ref):
        o_ref[...] = (g_ref[...].astype(f32) + r_ref[...].astype(f32)).astype(bf16)

    return pl.pallas_call(
        body,
        grid_spec=pltpu.PrefetchScalarGridSpec(
            num_scalar_prefetch=1, grid=(4, r // tr),
            in_specs=[pl.BlockSpec((None, tr, w), lambda q, i, c_ref: (2 * q + c_ref[0], i, 0)),
                      pl.BlockSpec((None, tr, w), lambda q, i, c_ref: (q, i, 0))],
            out_specs=pl.BlockSpec((None, tr, w), lambda q, i, c_ref: (q, i, 0))),
        out_shape=SDS((4, r, w), bf16), name=name, compiler_params=_cp(("arbitrary", "arbitrary")),
    )(c_idx, g, r1)


def _adamw(w, g, m, v):
    m = ADAM_B1 * m + (1.0 - ADAM_B1) * g
    v = ADAM_B2 * v + (1.0 - ADAM_B2) * (g * g)
    m_hat = m / (1.0 - ADAM_B1 ** ADAM_STEP)
    v_hat = v / (1.0 - ADAM_B2 ** ADAM_STEP)
    delta = -ADAM_LR * (m_hat / (jnp.sqrt(v_hat) + ADAM_EPS) + ADAM_WD * w)
    return delta, m, v


def _adamw_big(part, r2, q_idx, w, m, v, name, row_off=0, cols=(0, 1), prev=None):
    r, wd = w.shape
    tr = _tile_rows(r)
    k, ncol = cols
    wp = wd // ncol

    def body(q_ref, p_ref, r_ref, w_ref, m_ref, v_ref, *rest):
        g_out, d_out, m_out, v_out = rest[-4:]
        g = p_ref[...].astype(f32)
        for j in range(3):
            g = g + r_ref[j].astype(f32)
        d, mn, vn = _adamw(w_ref[...], g, m_ref[...], v_ref[...])
        g_out[...] = g
        d_out[...] = d
        m_out[...] = mn
        v_out[...] = vn

    tile = pl.BlockSpec((tr, wp), lambda i, q_ref: (i, k))
    prev = list(prev) if prev is not None else []
    return pl.pallas_call(
        body,
        grid_spec=pltpu.PrefetchScalarGridSpec(
            num_scalar_prefetch=1, grid=(r // tr,),
            in_specs=[pl.BlockSpec((None, tr, wp), lambda i, q_ref: (q_ref[0], row_off + i, 0)),
                      pl.BlockSpec((3, tr, wp), lambda i, q_ref: (0, row_off + i, 0)), tile, tile, tile]
                     + [pl.BlockSpec(memory_space=pl.ANY)] * len(prev),
            out_specs=[tile] * 4),
        out_shape=[SDS((r, wd), f32)] * 4, name=name,
        input_output_aliases={6 + i: i for i in range(len(prev))},
        compiler_params=_cp(("arbitrary",), 48),
    )(q_idx, part, r2, w, m, v, *prev)


_SMALL_ROWS = 24


def _pack_early(vec_rnn, st_out, dsr, db_in):
    def body(vr_ref, so_ref, dsr_ref, db_ref, sm_ref, sm2_ref):
        sm_ref[...] = jnp.zeros_like(sm_ref)
        sm2_ref[...] = jnp.zeros_like(sm2_ref)
        sm_ref[2:3, :] = vr_ref[3:4, :]
        sm_ref[3:6, :] = vr_ref[0:3, :]
        sm_ref[6:7, :] = so_ref[2:3, :]
        sm_ref[7:9, :] = so_ref[0:2, :]
        sm_ref[10:11, :] = so_ref[3:4, :]
        for h in range(N_KV):
            sm_ref[9:10, h * GROUP:(h + 1) * GROUP] = _colsum(dsr_ref[h])
        for j in range(6):
            sm_ref[16 + j:17 + j, :] = db_ref[0:1, j * D:(j + 1) * D]
        sm_ref[22:23, 0:D_IN - 6 * D] = db_ref[0:1, 6 * D:D_IN]
        for s in range(N_DEV):
            sm2_ref[s, 0:CONV_WIDTH, :] = vr_ref[4:8, s * 256:(s + 1) * 256]

    return pl.pallas_call(
        body, out_shape=[SDS((_SMALL_ROWS, D), f32), SDS((N_DEV, 8, 256), f32)],
        name="pack_early", compiler_params=_cp(None),
    )(vec_rnn, st_out, dsr, db_in)


def _pack_late(st_emb, dmeta):
    def body(se_ref, dm_ref, sm_ref, sm2_ref):
        sm_ref[...] = se_ref[...]
        for s in range(N_DEV):
            sm2_ref[s] = dm_ref[:, s * 256:(s + 1) * 256]

    return pl.pallas_call(
        body, out_shape=[SDS((8, D), f32), SDS((N_DEV, N_META, 256), f32)],
        name="pack_late", compiler_params=_cp(None),
    )(st_emb, dmeta)


_SMALL_ROW_OF = {"ln_emb_g": 0, "ln_emb_b": 1, "conv_b": 2, "b_ra": 3, "b_ri": 4, "lru_lambda": 5, "b_o": 6,
                 "ln_g": 7, "ln_b": 8}
_SMALL_NAMES = ["ln_emb_g", "ln_emb_b", "conv_b", "b_ra", "b_ri", "lru_lambda", "b_o", "ln_g", "ln_b",
                "sinks", "b_in", "meta_tokens", "conv_w"]


def _small_update(me_idx, early, late, wmv):
    n_fixed = 9

    def in_order(me, own_ref, land_ref):
        acc = None
        for e in range(N_DEV):
            term = jnp.where(me == e, own_ref[...], land_ref[e])
            acc = term if acc is None else acc + term
        return acc

    def body(*refs):
        me_ref, own_ref, land_ref, cown_ref, cland_ref, lown_ref, lland_ref, mown_ref, mland_ref = refs[:n_fixed]
        ins = refs[n_fixed:n_fixed + 3 * len(_SMALL_NAMES)]
        outs = refs[n_fixed + 3 * len(_SMALL_NAMES):]
        me = me_ref[0]
        sm = in_order(me, own_ref, land_ref)
        conv = in_order(me, cown_ref, cland_ref)
        late = in_order(me, lown_ref, lland_ref)
        meta = in_order(me, mown_ref, mland_ref)

        def grad_of(name):
            if name in ("ln_emb_g", "ln_emb_b"):
                r = _SMALL_ROW_OF[name]
                return late[r:r + 1, :]
            if name in _SMALL_ROW_OF:
                r = _SMALL_ROW_OF[name]
                return sm[r:r + 1, :]
            if name == "sinks":
                return sm[9:10, 0:N_KV * GROUP]
            if name == "b_in":
                return jnp.concatenate([sm[16 + j:17 + j, :] for j in range(7)], axis=1)[:, :D_IN]
            if name == "meta_tokens":
                return meta
            return conv[0:CONV_WIDTH, :]

        for i, name in enumerate(_SMALL_NAMES):
            w_ref, m_ref, v_ref = ins[3 * i:3 * i + 3]
            g = grad_of(name)
            d, mn, vn = _adamw(w_ref[...], g, m_ref[...], v_ref[...])
            outs[4 * i][...] = g
            outs[4 * i + 1][...] = d
            outs[4 * i + 2][...] = mn
            outs[4 * i + 3][...] = vn
        outs[-1][...] = jnp.broadcast_to(jnp.sum(sm[10:11, :], axis=1, keepdims=True), (8, 128))

    args, out_shape = [me_idx, *early, *late], []
    for name in _SMALL_NAMES:
        args += list(wmv[name])
        out_shape += [SDS(wmv[name][0].shape, f32)] * 4
    out_shape.append(SDS((8, 128), f32))
    res = pl.pallas_call(
        body, out_shape=out_shape, in_specs=[pl.BlockSpec(memory_space=pltpu.SMEM)] + [_VMEM] * (len(args) - 1),
        name="small_update", compiler_params=_cp(None))(*args)
    return {name: tuple(res[4 * i:4 * i + 4]) for i, name in enumerate(_SMALL_NAMES)}, res[-1][0, 0]


_WEIGHTS = ["meta_tokens", "ln_emb_g", "ln_emb_b", "w_in", "b_in", "conv_w", "conv_b", "w_ra", "b_ra", "w_ri",
            "b_ri", "lru_lambda", "sinks", "w_rnn_out", "w_attn_out", "w_o", "b_o", "ln_g", "ln_b"]
_SMALL_2D = {"meta_tokens": (N_META, 256), "conv_w": (CONV_WIDTH, 256), "b_in": (1, D_IN), "sinks": (1, N_KV * GROUP)}


def kernel(x, meta_tokens, ln_emb_g, ln_emb_b, w_in, b_in, conv_w, conv_b, w_ra, b_ra, w_ri, b_ri, lru_lambda, sinks, w_rnn_out, w_attn_out, w_o, b_o, ln_g, ln_b, loss_target, m_meta_tokens, m_ln_emb_g, m_ln_emb_b, m_w_in, m_b_in, m_conv_w, m_conv_b, m_w_ra, m_b_ra, m_w_ri, m_b_ri, m_lru_lambda, m_sinks, m_w_rnn_out, m_w_attn_out, m_w_o, m_b_o, m_ln_g, m_ln_b, v_meta_tokens, v_ln_emb_g, v_ln_emb_b, v_w_in, v_b_in, v_conv_w, v_conv_b, v_w_ra, v_b_ra, v_w_ri, v_b_ri, v_lru_lambda, v_sinks, v_w_rnn_out, v_w_attn_out, v_w_o, v_b_o, v_ln_g, v_ln_b):
    w = dict(meta_tokens=meta_tokens, ln_emb_g=ln_emb_g, ln_emb_b=ln_emb_b, w_in=w_in, b_in=b_in, conv_w=conv_w,
             conv_b=conv_b, w_ra=w_ra, b_ra=b_ra, w_ri=w_ri, b_ri=b_ri, lru_lambda=lru_lambda, sinks=sinks,
             w_rnn_out=w_rnn_out, w_attn_out=w_attn_out, w_o=w_o, b_o=b_o, ln_g=ln_g, ln_b=ln_b)
    m = dict(meta_tokens=m_meta_tokens, ln_emb_g=m_ln_emb_g, ln_emb_b=m_ln_emb_b, w_in=m_w_in, b_in=m_b_in,
             conv_w=m_conv_w, conv_b=m_conv_b, w_ra=m_w_ra, b_ra=m_b_ra, w_ri=m_w_ri, b_ri=m_b_ri,
             lru_lambda=m_lru_lambda, sinks=m_sinks, w_rnn_out=m_w_rnn_out, w_attn_out=m_w_attn_out, w_o=m_w_o,
             b_o=m_b_o, ln_g=m_ln_g, ln_b=m_ln_b)
    v = dict(meta_tokens=v_meta_tokens, ln_emb_g=v_ln_emb_g, ln_emb_b=v_ln_emb_b, w_in=v_w_in, b_in=v_b_in,
             conv_w=v_conv_w, conv_b=v_conv_b, w_ra=v_w_ra, b_ra=v_b_ra, w_ri=v_w_ri, b_ri=v_b_ri,
             lru_lambda=v_lru_lambda, sinks=v_sinks, w_rnn_out=v_w_rnn_out, w_attn_out=v_w_attn_out, w_o=v_w_o,
             b_o=v_b_o, ln_g=v_ln_g, ln_b=v_ln_b)
    px, py, pc = _place()
    as_idx = lambda t: jnp.reshape(t, (1,)).astype(jnp.int32)
    c_idx, q_idx, me_idx = as_idx(pc), as_idx(2 * px + py), as_idx(_dev(px, py, pc))

    w3_s, wrg_s, small_s = _cast_small(me_idx, w_rnn_out, w_attn_out, w_o, w_ra, w_ri, meta_tokens, conv_w)
    vec = lambda name: w[name].reshape(1, -1)
    p = {k: vec(k) for k in ("ln_emb_g", "ln_emb_b", "b_in", "conv_b", "b_ra", "b_ri", "lru_lambda", "sinks",
                             "b_o", "ln_g", "ln_b")}
    w_in_t = lambda a: jnp.swapaxes(a, 1, 2).reshape(SHARD_IN, D)
    wg, wrg, smallw = _all_gather([_cast_w_in(w_in_t(w_in), me_idx), wrg_s, small_s], [6, 1, 1])
    late = _split_start(_copies_own(_late_rows), 4, [wg], [], smallw, "gather_late_start")
    h32, hb = _ln_emb(x, smallw, p["ln_emb_g"], p["ln_emb_b"] + late[4][0:1, 0:1])
    z = _mm_z(hb, late[2][0].reshape(D_IN, D), p["b_in"], None, "mm_z_early")
    (wg,), _ = _split_wait(_copies_own(_late_rows), late[0], late[1], late[2], [], z, "gather_late_wait")
    passed = _split_start(_copies_pass(_late_rows), 3, [wg], [], smallw, "gather_pass_start")
    w3_own = _split_start(_copies_own(_whole_block), 4, [w3_s], [], passed[4], "gather_w3_start")
    zero = w3_own[4][0:1, 0:1]
    z = _mm_z(hb, passed[2][0].reshape(D_IN, D), p["b_in"] + zero, c_idx, "mm_z_late_own", z)
    (wg,), _ = _split_wait(_copies_pass(_late_rows), passed[0], passed[1], passed[2], [], z, "gather_pass_wait")
    w_full = wg.reshape(D_IN, D)

    z = _mm_z(hb, w_full, p["b_in"], 1 - c_idx, "mm_z_late_other", z)
    s = _step_rnn(h32, hb, z, wrg, smallw, p, zero)
    (w3,), _ = _split_wait(_copies_own(_whole_block), w3_own[0], w3_own[1], w3_own[2], [], s["ya"], "gather_w3_wait")
    w3_pass = _split_start(_copies_pass(_whole_block), 3, [w3], [], smallw, "gather_w3_pass_start")
    s = _step_attn(s, p, w3_pass[4][0:1, 0:1])
    (w3,), _ = _split_wait(_copies_pass(_whole_block), w3_pass[0], w3_pass[1], w3_pass[2], [], s["lse"],
                           "gather_w3_pass_wait")
    t = _step_merge(s, loss_target, w3, p)

    big = {}
    two_d = lambda name: (w[name].shape[-2], w[name].shape[-1])
    proj = ("w_o", "w_rnn_out", "w_attn_out")
    g_proj = [t[k].reshape(N_DEV, 256, D) for k in ("g_wo", "g_wrnn", "g_wattn")]
    g_pending = _split_start(_copies_direct(False), 7, g_proj, [lax.empty((N_DEV, 256, D), bf16) for _ in proj],
                             p["b_o"], "reduce_proj_start")
    u = _step_backward(s, t, wrg, smallw, p, p["conv_b"] + g_pending[4][0:1, 0:1])

    def siblings_start(gs, dep, tag):
        return _split_start(_copies_siblings, 4, gs, [lax.empty((4, *g.shape[1:]), bf16) for g in gs], dep,
                            "reduce_siblings_start_" + tag)

    def chips_start(gs, r1, dep, tag):
        parts = [_pair_sum(g, r, c_idx, "pair_sum_%s%d" % (tag, i)) for i, (g, r) in enumerate(zip(gs, r1))]
        return _split_start(_copies_chips, 3, parts, [lax.empty((3, *q.shape[1:]), bf16) for q in parts], dep,
                            "reduce_chips_start_" + tag)

    g_a, dz, db_in = _mm_dwin_parts(s["hb"], u["dz_parts"])
    shards = lambda g: g.reshape(N_DEV, SHARD_IN, W_IN_HALF)
    sib_a = siblings_start([shards(g_a), u["g_wrg"].reshape(N_DEV, 2 * RNN_BLOCK, RNN_BLOCK)], db_in, "a")
    g_proj, g_land = _split_wait(_copies_direct(False), *g_pending[:4], sib_a[4], "reduce_proj_wait")
    for i, name in enumerate(proj):
        res = _adamw_direct(g_proj[i], g_land[i], me_idx, w[name].reshape(two_d(name)), m[name].reshape(two_d(name)),
                            v[name].reshape(two_d(name)), "adamw_" + name)
        big[name] = tuple(r.reshape(w[name].shape) for r in res)
    chp_a = chips_start(*_split_wait(_copies_siblings, *sib_a[:4], big["w_attn_out"][3], "reduce_siblings_wait_a"),
                        db_in, "a")
    g_b = _mm_dwin(s["hb"], dz, chp_a[4])
    sib_b = siblings_start([shards(g_b)], db_in, "b")
    sm_e = _pack_early(u["vec_rnn"], t["st_out"], u["dsr"], db_in)
    early = _split_start(_copies_direct(True), 7, list(sm_e),
                         [lax.empty((N_DEV, *a.shape), f32) for a in sm_e], sib_b[4], "small_early_start")
    dh_lo = _mm_dh(dz, w_full, early[4], 0)
    chp_b = chips_start(*_split_wait(_copies_siblings, *sib_b[:4], dh_lo, "reduce_siblings_wait_b"), db_in, "b")
    dh_hi = _mm_dh(dz, w_full, chp_b[4], 1)
    parts_a, r2_a = _split_wait(_copies_chips, *chp_a[:4], dh_hi, "reduce_chips_wait_a")
    w_in_res = _adamw_big(parts_a[0], r2_a[0], q_idx, w_in_t(w["w_in"]), w_in_t(m["w_in"]), w_in_t(v["w_in"]),
                          "adamw_w_in_a", cols=(0, 2))
    u.update(_step_input_grad(dh_lo, dh_hi, t["du32"], x, smallw, p, w_in_res[3]))
    sm_l = _pack_late(u["st_emb"], u["dmeta"])
    late_x = _split_start(_copies_direct(True), 7, list(sm_l), [lax.empty((N_DEV, *a.shape), f32) for a in sm_l],
                          p["b_o"], "small_late_start")
    parts_b, r2_b = _split_wait(_copies_chips, *chp_b[:4], late_x[4], "reduce_chips_wait_b")
    res = _adamw_big(parts_b[0], r2_b[0], q_idx, w_in_t(w["w_in"]), w_in_t(m["w_in"]), w_in_t(v["w_in"]),
                     "adamw_w_in_b", cols=(1, 2), prev=w_in_res)
    big["w_in"] = tuple(jnp.swapaxes(r.reshape(1, SHARD_IN, D), 1, 2) for r in res)
    (sm_own, conv_own), (sm_land, conv_land) = _split_wait(_copies_direct(True), *early[:4], res[3], "small_early_wait")
    (l_own, meta_own), (l_land, meta_land) = _split_wait(_copies_direct(True), *late_x[:4], sm_land, "small_late_wait")
    me = _dev(px, py, pc)
    mine = lambda a, axis: lax.dynamic_index_in_dim(a, me, axis, keepdims=False)
    two = lambda name, t: t.reshape(_SMALL_2D.get(name, (1, D)))
    small, loss = _small_update(me_idx, (sm_own, sm_land, mine(conv_own, 0), mine(conv_land, 1)),
                                (l_own, l_land, mine(meta_own, 0), mine(meta_land, 1)),
                                {k: (two(k, w[k]), two(k, m[k]), two(k, v[k])) for k in _SMALL_NAMES})
    for i, name in enumerate(("w_ra", "w_ri")):
        sq = (RNN_BLOCK, RNN_BLOCK)
        res = _adamw_big(parts_a[1], r2_a[1], q_idx, w[name].reshape(sq), m[name].reshape(sq), v[name].reshape(sq),
                         "adamw_" + name, row_off=i)
        big[name] = tuple(r.reshape(w[name].shape) for r in res)
    res = dict(big)
    for k in _SMALL_NAMES:
        res[k] = tuple(t.reshape(w[k].shape) for t in small[k])

    outs = [loss, u["grad_x"]]
    for j in range(4):
        outs += [res[k][j] for k in _WEIGHTS]
    return tuple(outs)
```

```python
import jax
import jax.numpy as jnp
from jax import lax
from jax.experimental import pallas as pl
from jax.experimental.pallas import tpu as pltpu

f32, bf16 = jnp.float32, jnp.bfloat16
SDS = jax.ShapeDtypeStruct

N_DEV = 8
D = 2048
N_META = 16
BLK = 128
ROW0 = BLK - N_META
N_RNN_BLOCKS = 8
RNN_BLOCK = D // N_RNN_BLOCKS
CONV_WIDTH = 4
LRU_C = 8.0
HEAD_DIM = 64
N_KV = 4
GROUP = 8
HALF = HEAD_DIM // 2
ROPE_THETA = 10000.0
NEG_INF = -1e30
LN_EPS = 1e-5
ALPHA = 2.0 ** 0.25
D_IN = 12800
SHARD_IN = D_IN // N_DEV
W_IN_HALF = D // 2
W_IN_LATE = 640
W_IN_EARLY = SHARD_IN - W_IN_LATE
OFF_GR, OFF_Q, OFF_K, OFF_V, OFF_GA, OFF_G = 2048, 4096, 6144, 6400, 6656, 8704
ADAM_LR, ADAM_B1, ADAM_B2, ADAM_EPS, ADAM_WD, ADAM_STEP = 1e-3, 0.9, 0.999, 1e-8, 0.01, 10
VMEM_LIMIT_MB = 56
MESH = pl.DeviceIdType.MESH


def _cp(sem=None, vmem_mb=40):
    return pltpu.CompilerParams(dimension_semantics=sem, vmem_limit_bytes=vmem_mb * 2 ** 20)


def _row_chunk(m):
    best = 16
    for c in range(16, 641, 16):
        if m % c == 0:
            best = c
    return best


def _sigmoid(x):
    return 1.0 / (1.0 + jnp.exp(-x))


def _silu_and_grad(x):
    s = _sigmoid(x)
    return x * s, s * (1.0 + x * (1.0 - s))


def _log_sigmoid(x):
    return jnp.minimum(x, 0.0) - jnp.log1p(jnp.exp(-jnp.abs(x)))


def _ln_rows(v, g, b):
    mu = jnp.mean(v, axis=-1, keepdims=True)
    c = v - mu
    var = jnp.mean(c * c, axis=-1, keepdims=True)
    rstd = lax.rsqrt(var + LN_EPS)
    xhat = c * rstd
    return xhat * g + b, xhat, rstd


def _ln_rows_bwd(dy, g, xhat, rstd):
    dxh = dy * g
    m1 = jnp.mean(dxh, axis=-1, keepdims=True)
    m2 = jnp.mean(dxh * xhat, axis=-1, keepdims=True)
    return rstd * (dxh - m1 - xhat * m2)


def _colsum(v):
    return jnp.sum(v, axis=0, keepdims=True)


def _dot(a, b):
    return jnp.dot(a, b, preferred_element_type=f32)


def _dot_nt(a, b):
    return lax.dot_general(a, b, (((1,), (1,)), ((), ())), preferred_element_type=f32)


def _dot_tn(a, b):
    return lax.dot_general(a, b, (((0,), (0,)), ((), ())), preferred_element_type=f32)


def _meta_full(sw_ref):
    return jnp.concatenate([sw_ref[s, 0:N_META, :] for s in range(N_DEV)], axis=1)


def _ln_emb(x, smallw, g_e, b_e):
    seq = x.shape[1]
    rows = seq + BLK
    nb = rows // BLK

    def body(x_ref, sw_ref, g_ref, b_ref, h32_ref, hb_ref):
        i = pl.program_id(0)
        g, b = g_ref[...], b_ref[...]

        def emit(blk):
            h32_ref[...] = blk
            hb_ref[...] = blk.astype(bf16)

        @pl.when(i == 0)
        def _():
            hm = _ln_rows(_meta_full(sw_ref), g, b)[0]
            emit(jnp.concatenate([jnp.zeros((ROW0, D), f32), hm], axis=0))

        @pl.when(i > 0)
        def _():
            emit(_ln_rows(x_ref[0], g, b)[0])

    return pl.pallas_call(
        body, grid=(nb,),
        in_specs=[pl.BlockSpec((1, BLK, D), lambda i: (0, jnp.maximum(i - 1, 0), 0)),
                  pl.BlockSpec((N_DEV, 24, 256), lambda i: (0, 0, 0)),
                  pl.BlockSpec((1, D), lambda i: (0, 0)),
                  pl.BlockSpec((1, D), lambda i: (0, 0))],
        out_specs=[pl.BlockSpec((BLK, D), lambda i: (i, 0)),
                   pl.BlockSpec((BLK, D), lambda i: (i, 0))],
        out_shape=[SDS((rows, D), f32), SDS((rows, D), bf16)],
        name="ln_emb", compiler_params=_cp(("arbitrary",)),
    )(x, smallw, g_e, b_e)


def _ln_emb_bwd(dh_lo, dh_hi, du32, x, smallw, g_e, after):
    seq = x.shape[1]
    rows = seq + BLK
    nb = rows // BLK

    def body(dlo_ref, dhi_ref, du_ref, x_ref, sw_ref, g_ref, after_ref, gx_ref, dmeta_ref, st_ref):
        i = pl.program_id(0)
        g = g_ref[...]
        dht = jnp.concatenate([dlo_ref[...], dhi_ref[...]], axis=1) + ALPHA * du_ref[...]

        @pl.when(i == 0)
        def _():
            v = jnp.concatenate([jnp.zeros((ROW0, D), f32), _meta_full(sw_ref)], axis=0)
            valid = lax.broadcasted_iota(jnp.int32, (BLK, 1), 0) >= ROW0
            d = jnp.where(valid, dht, 0.0)
            _, xhat, rstd = _ln_rows(v, g, 0.0)
            dv = _ln_rows_bwd(d, g, xhat, rstd)
            dmeta_ref[...] = dv[ROW0:, :]
            st_ref[...] = jnp.concatenate([_colsum(d * xhat), _colsum(d), jnp.zeros((6, D), f32)], axis=0)

        @pl.when(i > 0)
        def _():
            _, xhat, rstd = _ln_rows(x_ref[0], g, 0.0)
            gx_ref[0] = _ln_rows_bwd(dht, g, xhat, rstd)
            st_ref[0:1, :] += _colsum(dht * xhat)
            st_ref[1:2, :] += _colsum(dht)

    return pl.pallas_call(
        body, grid=(nb,),
        in_specs=[pl.BlockSpec((BLK, W_IN_HALF), lambda i: (i, 0)),
                  pl.BlockSpec((BLK, W_IN_HALF), lambda i: (i, 0)),
                  pl.BlockSpec((BLK, D), lambda i: (i, 0)),
                  pl.BlockSpec((1, BLK, D), lambda i: (0, jnp.maximum(i - 1, 0), 0)),
                  pl.BlockSpec((N_DEV, 24, 256), lambda i: (0, 0, 0)),
                  pl.BlockSpec((1, D), lambda i: (0, 0)),
                  pl.BlockSpec(memory_space=pl.ANY)],
        out_specs=[pl.BlockSpec((1, BLK, D), lambda i: (0, jnp.maximum(i - 1, 0), 0)),
                   pl.BlockSpec((N_META, D), lambda i: (0, 0)),
                   pl.BlockSpec((8, D), lambda i: (0, 0))],
        out_shape=[SDS((1, seq, D), f32), SDS((N_META, D), f32), SDS((8, D), f32)],
        name="ln_emb_bwd", compiler_params=_cp(("arbitrary",)),
    )(dh_lo, dh_hi, du32, x, smallw, g_e, after)


def _mm(a, b, *, name, nt=False, sel=None, bias=None, out_dtype=f32, tn=512):
    m, k = a.shape
    cm = _row_chunk(m)
    stacked = sel is not None
    n = D if stacked else (b.shape[0] if nt else b.shape[1])
    am = m
    if stacked and nt:
        b_spec = pl.BlockSpec((tn // 256, None, 256, D), lambda j, i: (j, sel, 0, 0))
    elif stacked:
        b_spec = pl.BlockSpec((N_DEV, None, 256, tn), lambda j, i: (0, sel, 0, j))
    elif nt:
        b_spec = pl.BlockSpec((tn, k), lambda j, i: (j, 0))
    else:
        b_spec = pl.BlockSpec((k, tn), lambda j, i: (0, j))
    in_specs = [pl.BlockSpec((am, k), lambda j, i: (i, 0)), b_spec]
    args = [a, b]
    if bias is not None:
        in_specs.append(pl.BlockSpec((1, tn), lambda j, i: (0, j)))
        args.append(bias)

    def body(*refs):
        a_ref, b_ref, o_ref = refs[0], refs[1], refs[-1]
        bm = b_ref[...]
        if stacked:
            bm = bm.reshape((tn, D) if nt else (D, tn))
        for c in range(am // cm):
            acc = (_dot_nt if nt else _dot)(a_ref[c * cm:(c + 1) * cm, :], bm)
            if bias is not None:
                acc = acc + refs[2][...]
            o_ref[c * cm:(c + 1) * cm, :] = acc.astype(out_dtype)

    return pl.pallas_call(
        body, grid=(n // tn, m // am), in_specs=in_specs,
        out_specs=pl.BlockSpec((am, tn), lambda j, i: (i, j)),
        out_shape=SDS((m, n), out_dtype), name=name, compiler_params=_cp(("arbitrary", "arbitrary"), 48),
    )(*args)


def _mm_z(hb, w_t, bias, side, name, z_prev=None):
    rows, k = hb.shape
    tn = W_IN_LATE
    cm = _row_chunk(rows)
    per = 2 * SHARD_IN // tn
    if side is None:
        side, count = jnp.zeros((1,), jnp.int32), per - 2
        tile = lambda q, t, s_ref: per * q + 1 + t
    else:
        count = 1
        tile = lambda q, t, s_ref: per * q + (per - 1) * s_ref[0]

    def body(s_ref, a_ref, b_ref, bias_ref, *rest):
        o_ref = rest[-1]
        for c in range(rows // cm):
            o_ref[c * cm:(c + 1) * cm, :] = _dot_nt(a_ref[c * cm:(c + 1) * cm, :], b_ref[...]) + bias_ref[...]

    in_specs = [pl.BlockSpec((rows, k), lambda q, t, s_ref: (0, 0)),
                pl.BlockSpec((tn, k), lambda q, t, s_ref: (tile(q, t, s_ref), 0)),
                pl.BlockSpec((1, tn), lambda q, t, s_ref: (0, tile(q, t, s_ref)))]
    args = [side, hb, w_t, bias]
    if z_prev is not None:
        in_specs.append(pl.BlockSpec(memory_space=pl.ANY))
        args.append(z_prev)
    return pl.pallas_call(
        body,
        grid_spec=pltpu.PrefetchScalarGridSpec(
            num_scalar_prefetch=1, grid=(N_DEV // 2, count), in_specs=in_specs,
            out_specs=pl.BlockSpec((rows, tn), lambda q, t, s_ref: (0, tile(q, t, s_ref)))),
        out_shape=SDS((rows, D_IN), f32), name=name,
        input_output_aliases={} if z_prev is None else {4: 0},
        compiler_params=_cp(("arbitrary", "arbitrary"), 48),
    )(*args)


def _mm_dh(dz, w_t, after, half):
    rows = dz.shape[0]
    tn = 512
    nt = W_IN_HALF // tn
    cm = _row_chunk(rows) // 2

    def body(a_ref, w_ref, after_ref, o_ref):
        o_ref[...] = _dot(a_ref[...], w_ref[...])

    return pl.pallas_call(
        body, grid=(nt, rows // cm),
        in_specs=[pl.BlockSpec((cm, D_IN), lambda j, i: (i, 0)),
                  pl.BlockSpec((D_IN, tn), lambda j, i: (0, half * nt + j)),
                  pl.BlockSpec(memory_space=pl.ANY)],
        out_specs=pl.BlockSpec((cm, tn), lambda j, i: (i, j)),
        out_shape=SDS((rows, W_IN_HALF), f32), name="mm_dh_%d" % half,
        compiler_params=_cp(("arbitrary", "arbitrary"), 48),
    )(dz, w_t, after)


def _mm_dwin_parts(hb, parts):
    rows = hb.shape[0]
    tc = 512
    edges = [0]
    for _, w in parts:
        edges.append(edges[-1] + w // tc)

    def body(*refs):
        h_ref, (o_ref, dz_ref, db_ref, haug) = refs[len(parts)], refs[len(parts) + 1:]
        j = pl.program_id(0)

        @pl.when(j == 0)
        def _():
            haug[:, :W_IN_HALF] = h_ref[...]
            haug[:, W_IN_HALF:] = jnp.ones((rows, BLK), bf16)

        for p_ref, lo, hi in zip(refs, edges[:-1], edges[1:]):
            @pl.when((j >= lo) & (j < hi))
            def _():
                acc = _dot_tn(p_ref[...], haug[...])
                o_ref[...] = acc[:, :W_IN_HALF].astype(bf16)
                dz_ref[...] = p_ref[...]
                db_ref[...] = acc[:, W_IN_HALF:].T[0:8, :]

    in_specs = [pl.BlockSpec((rows, tc), lambda j, lo=lo, hi=hi: (0, jnp.clip(j - lo, 0, hi - lo - 1)))
                for lo, hi in zip(edges[:-1], edges[1:])]
    return pl.pallas_call(
        body, grid=(D_IN // tc,),
        in_specs=in_specs + [pl.BlockSpec((rows, W_IN_HALF), lambda j: (0, 0), pipeline_mode=pl.Buffered(1))],
        out_specs=[pl.BlockSpec((tc, W_IN_HALF), lambda j: (j, 0)), pl.BlockSpec((rows, tc), lambda j: (0, j)),
                   pl.BlockSpec((8, tc), lambda j: (0, j))],
        out_shape=[SDS((D_IN, W_IN_HALF), bf16), SDS((rows, D_IN), bf16), SDS((8, D_IN), f32)],
        scratch_shapes=[pltpu.VMEM((rows, W_IN_HALF + BLK), bf16)],
        name="mm_dwin_0", compiler_params=_cp(("arbitrary",), VMEM_LIMIT_MB),
    )(*[a for a, _ in parts], hb)


def _mm_dwin(hb, dz, after):
    rows = dz.shape[0]
    tc = 640

    def body(dz_ref, h_ref, after_ref, o_ref):
        o_ref[...] = _dot_tn(dz_ref[...], h_ref[...]).astype(bf16)

    return pl.pallas_call(
        body, grid=(D_IN // tc,),
        in_specs=[pl.BlockSpec((rows, tc), lambda j: (0, j)),
                  pl.BlockSpec((rows, W_IN_HALF), lambda j: (0, 1)),
                  pl.BlockSpec(memory_space=pl.ANY)],
        out_specs=pl.BlockSpec((tc, W_IN_HALF), lambda j: (j, 0)),
        out_shape=SDS((D_IN, W_IN_HALF), bf16),
        name="mm_dwin_1", compiler_params=_cp(("arbitrary",), 48),
    )(dz, hb, after)


SCAN_ROWS = 32


def _scan8(a, b, reverse):
    idx = lax.broadcasted_iota(jnp.int32, a.shape, 0)
    for s in (1, 2, 4):
        sh = 8 - s if reverse else s
        a_sh, b_sh = pltpu.roll(a, sh, 0), pltpu.roll(b, sh, 0)
        m = (idx < 8 - s) if reverse else (idx >= s)
        b = jnp.where(m, a * b_sh + b, b)
        a = jnp.where(m, a * a_sh, a)
    return a, b


def _shift_rows(prev8, cur, k):
    ext = jnp.concatenate([prev8, cur], axis=0)
    return pltpu.roll(ext, k, 0)[8:, :]


def _gates(xc, w_ra, b_ra, w_ri, b_ri, ls):
    xb = xc.astype(bf16)
    r = _sigmoid(_dot(xb, w_ra) + b_ra)
    ig = _sigmoid(_dot(xb, w_ri) + b_ri)
    la = LRU_C * r * ls
    a = jnp.exp(la)
    mult = jnp.sqrt(jnp.tanh(-la) * (1.0 + a * a))
    return xb, r, ig, a, mult


_RNN_IN_SPECS = lambda rows: [
    pl.BlockSpec((1, 24, 256), lambda n: (n, 0, 0)),
    pl.BlockSpec((1, RNN_BLOCK), lambda n: (0, n)),
    pl.BlockSpec((N_DEV, 2, None, 32, RNN_BLOCK), lambda n: (0, 0, n, 0, 0)),
    pl.BlockSpec((1, RNN_BLOCK), lambda n: (0, n)),
    pl.BlockSpec((1, RNN_BLOCK), lambda n: (0, n)),
    pl.BlockSpec((1, RNN_BLOCK), lambda n: (0, n)),
]


def _rnn_fwd(z, smallw, conv_b, wrg, b_ra, b_ri, lam):
    rows = z.shape[0]
    nb = rows // BLK
    col = lambda off: pl.BlockSpec((rows, RNN_BLOCK), lambda n: (0, off // RNN_BLOCK + n))

    def body(xr_ref, gr_ref, sw_ref, cb_ref, w_ref, bra_ref, bri_ref, lam_ref, xc_ref, hr_ref, ya_ref, yat_ref, a_s):
        cw = sw_ref[0, N_META:24, :]
        cb = cb_ref[...]
        w_ra = w_ref[:, 0].reshape(RNN_BLOCK, RNN_BLOCK)
        w_ri = w_ref[:, 1].reshape(RNN_BLOCK, RNN_BLOCK)
        b_ra_v, b_ri_v = bra_ref[...], bri_ref[...]
        ls = _log_sigmoid(lam_ref[...])
        rid = lax.broadcasted_iota(jnp.int32, (BLK, 1), 0)

        def blk_step(i, carry):
            r0 = pl.multiple_of(i * BLK, BLK)
            grow = rid + r0
            valid = grow >= ROW0
            cur = jnp.where(valid, xr_ref[pl.ds(r0, BLK), :], 0.0)
            prev8 = xr_ref[pl.ds(pl.multiple_of(jnp.maximum(r0 - 8, 0), 8), 8), :] * (i > 0).astype(f32)
            xc = cb + cw[0:1] * cur
            for k in range(1, CONV_WIDTH):
                xc = xc + cw[k:k + 1] * _shift_rows(prev8, cur, k)
            xc_ref[pl.ds(r0, BLK), :] = xc
            _, _, ig, a, mult = _gates(xc, w_ra, b_ra_v, w_ri, b_ri_v, ls)
            mult = jnp.where(grow == ROW0, 1.0, mult)
            a_s[pl.ds(r0, BLK), :] = a
            hr_ref[pl.ds(r0, BLK), :] = jnp.where(valid, mult * ig * xc, 0.0)
            return carry

        lax.fori_loop(0, nb, blk_step, 0)

        def scan_step(j, carry):
            r0 = pl.multiple_of(j * SCAN_ROWS, SCAN_ROWS)
            tiles = [_scan8(a_s[pl.ds(r0 + 8 * k, 8), :], hr_ref[pl.ds(r0 + 8 * k, 8), :], False)
                     for k in range(SCAN_ROWS // 8)]
            for k, (a, b) in enumerate(tiles):
                h = b + a * carry
                hr_ref[pl.ds(r0 + 8 * k, 8), :] = h
                carry = jnp.broadcast_to(h[7:8, :], (8, RNN_BLOCK))
            return carry

        lax.fori_loop(0, rows // SCAN_ROWS, scan_step, jnp.zeros((8, RNN_BLOCK), f32))

        def gate_step(i, carry):
            r0 = pl.multiple_of(i * BLK, BLK)
            ya_ref[pl.ds(r0, BLK), :] = (hr_ref[pl.ds(r0, BLK), :]
                                         * _silu_and_grad(gr_ref[pl.ds(r0, BLK), :])[0]).astype(bf16)
            return carry

        lax.fori_loop(0, nb, gate_step, 0)
        yat_ref[...] = ya_ref[...].astype(f32).T.astype(bf16)

    return pl.pallas_call(
        body, grid=(N_RNN_BLOCKS,),
        in_specs=[col(0), col(OFF_GR)] + _RNN_IN_SPECS(rows),
        out_specs=[pl.BlockSpec((rows, RNN_BLOCK), lambda n: (0, n))] * 3
                  + [pl.BlockSpec((RNN_BLOCK, rows), lambda n: (n, 0))],
        out_shape=[SDS((rows, D), f32), SDS((rows, D), f32), SDS((rows, D), bf16), SDS((D, rows), bf16)],
        scratch_shapes=[pltpu.VMEM((rows, RNN_BLOCK), f32)],
        name="rnn_fwd", compiler_params=_cp(("arbitrary",)),
    )(z, z, smallw, conv_b, wrg, b_ra, b_ri, lam)


def _rnn_bwd(dya, hr, xc, z, smallw, conv_b, wrg, b_ra, b_ri, lam):
    rows = z.shape[0]
    nb = rows // BLK
    col = lambda off: pl.BlockSpec((rows, RNN_BLOCK), lambda n: (0, off // RNN_BLOCK + n))
    blk = pl.BlockSpec((rows, RNN_BLOCK), lambda n: (0, n))

    def body(dya_ref, hr_ref, xc_ref, xr_ref, gr_ref, sw_ref, cb_ref, w_ref, bra_ref, bri_ref, lam_ref,
             dxr_ref, dgr_ref, dw_ref, vec_ref, a_s, lam_s, dxc_s, r_s, ig_s, mult_s, dw_s):
        cw = sw_ref[0, N_META:24, :]
        w_ra = w_ref[:, 0].reshape(RNN_BLOCK, RNN_BLOCK)
        w_ri = w_ref[:, 1].reshape(RNN_BLOCK, RNN_BLOCK)
        b_ra_v, b_ri_v = bra_ref[...], bri_ref[...]
        lam_v = lam_ref[...]
        ls = _log_sigmoid(lam_v)
        rid = lax.broadcasted_iota(jnp.int32, (BLK, 1), 0)
        zrow = jnp.zeros((1, RNN_BLOCK), f32)

        def p1(i, carry):
            r0 = pl.multiple_of(i * BLK, BLK)
            sl = pl.ds(r0, BLK)
            _, r, ig, a, mult = _gates(xc_ref[sl, :], w_ra, b_ra_v, w_ri, b_ri_v, ls)
            a_s[sl, :] = a
            r_s[sl, :] = r
            ig_s[sl, :] = ig
            mult_s[sl, :] = mult
            sg, dsg = _silu_and_grad(gr_ref[sl, :])
            d = dya_ref[sl, :]
            lam_s[sl, :] = d * sg
            dgr_ref[sl, :] = (d * hr_ref[sl, :] * dsg).astype(bf16)
            return carry

        lax.fori_loop(0, nb, p1, 0)

        def p2(jj, carry):
            r0 = pl.multiple_of((rows // SCAN_ROWS - 1 - jj) * SCAN_ROWS, SCAN_ROWS)
            idx = lax.broadcasted_iota(jnp.int32, (8, RNN_BLOCK), 0)
            tiles = []
            for k in range(SCAN_ROWS // 8):
                sl = pl.ds(r0 + 8 * k, 8)
                a, g = a_s[sl, :], lam_s[sl, :]
                tiles.append((g, *_scan8(a, a * g, True)))
            for k in reversed(range(SCAN_ROWS // 8)):
                g, ca, cb_ = tiles[k]
                mu = cb_ + ca * carry
                lam_s[pl.ds(r0 + 8 * k, 8), :] = g + jnp.where(idx < 7, pltpu.roll(mu, 7, 0), carry)
                carry = jnp.broadcast_to(mu[0:1, :], (8, RNN_BLOCK))
            return carry

        lax.fori_loop(0, rows // SCAN_ROWS, p2, jnp.zeros((8, RNN_BLOCK), f32))

        dw_s[...] = jnp.zeros_like(dw_s)

        def p3(i, carry):
            d_bra, d_bri, d_ls = carry
            r0 = pl.multiple_of(i * BLK, BLK)
            sl = pl.ds(r0, BLK)
            grow = rid + r0
            valid = grow >= ROW0
            first = grow == ROW0
            xcv = xc_ref[sl, :]
            xb = xcv.astype(bf16)
            r, ig, a = r_s[sl, :], ig_s[sl, :], a_s[sl, :]
            mult = jnp.where(first, 1.0, mult_s[sl, :])
            lam_t = lam_s[sl, :]
            du = jnp.where(valid, lam_t, 0.0)
            hprev = _shift_rows(hr_ref[pl.ds(pl.multiple_of(jnp.maximum(r0 - 8, 0), 8), 8), :] * (i > 0).astype(f32), hr_ref[sl, :], 1)
            da = lam_t * hprev
            dmult = jnp.where(first, 0.0, du * ig * xcv)
            di = du * mult * xcv
            dxc = du * mult * ig
            ratio = jnp.where(valid & jnp.logical_not(first), a * a / mult, 0.0)
            dla = da * a - dmult * ratio
            dpr = (dla * (LRU_C * ls)) * r * (1.0 - r)
            dpi = di * ig * (1.0 - ig)
            dprb, dpib = dpr.astype(bf16), dpi.astype(bf16)
            dw_s[0] += _dot_tn(xb, dprb)
            dw_s[1] += _dot_tn(xb, dpib)
            dxc_s[sl, :] = dxc + _dot_nt(dprb, w_ra) + _dot_nt(dpib, w_ri)
            return d_bra + _colsum(dpr), d_bri + _colsum(dpi), d_ls + _colsum(dla * (LRU_C * r))

        d_bra, d_bri, d_ls = lax.fori_loop(0, nb, p3, (zrow, zrow, zrow))

        def p4(i, carry):
            d_cb, d_w0, d_w1, d_w2, d_w3 = carry
            r0 = pl.multiple_of(i * BLK, BLK)
            sl = pl.ds(r0, BLK)
            grow = rid + r0
            valid = grow >= ROW0
            dxc = dxc_s[sl, :]
            nxt = dxc_s[pl.ds(pl.multiple_of(jnp.minimum(r0 + BLK, rows - 8), 8), 8), :] * (i < nb - 1).astype(f32)
            ext = jnp.concatenate([dxc, nxt], axis=0)
            dxr = cw[0:1] * dxc
            for k in range(1, CONV_WIDTH):
                dxr = dxr + cw[k:k + 1] * pltpu.roll(ext, BLK + 8 - k, 0)[:BLK, :]
            dxr_ref[sl, :] = jnp.where(valid, dxr, 0.0).astype(bf16)
            cur = jnp.where(valid, xr_ref[sl, :], 0.0)
            prev8 = xr_ref[pl.ds(pl.multiple_of(jnp.maximum(r0 - 8, 0), 8), 8), :] * (i > 0).astype(f32)
            dws = [d_w0 + _colsum(dxc * cur)]
            for k, acc in ((1, d_w1), (2, d_w2), (3, d_w3)):
                dws.append(acc + _colsum(dxc * _shift_rows(prev8, cur, k)))
            return (d_cb + _colsum(dxc), *dws)

        d_cb, d_w0, d_w1, d_w2, d_w3 = lax.fori_loop(0, nb, p4, (zrow,) * 5)

        d_lam = d_ls * _sigmoid(-lam_v)
        vec_ref[...] = jnp.concatenate([d_bra, d_bri, d_lam, d_cb, d_w0, d_w1, d_w2, d_w3], axis=0)
        dw_ref[:, 0] = dw_s[0].astype(bf16).reshape(N_DEV, 32, RNN_BLOCK)
        dw_ref[:, 1] = dw_s[1].astype(bf16).reshape(N_DEV, 32, RNN_BLOCK)

    return pl.pallas_call(
        body, grid=(N_RNN_BLOCKS,),
        in_specs=[blk, blk, blk, col(0), col(OFF_GR)] + _RNN_IN_SPECS(rows),
        out_specs=[blk, blk,
                   pl.BlockSpec((N_DEV, 2, None, 32, RNN_BLOCK), lambda n: (0, 0, n, 0, 0)),
                   pl.BlockSpec((8, RNN_BLOCK), lambda n: (0, n))],
        out_shape=[SDS((rows, D), bf16), SDS((rows, D), bf16),
                   SDS((N_DEV, 2, N_RNN_BLOCKS, 32, RNN_BLOCK), bf16), SDS((8, D), f32)],
        scratch_shapes=[pltpu.VMEM((rows, RNN_BLOCK), f32)] * 6 + [pltpu.VMEM((2, RNN_BLOCK, RNN_BLOCK), f32)],
        name="rnn_bwd", compiler_params=_cp(("arbitrary",), 48),
    )(dya, hr, xc, z, z, smallw, conv_b, wrg, b_ra, b_ri, lam)


def _rope_tables(rows):
    half = jnp.arange(HALF, dtype=f32)
    inv = ROPE_THETA ** (-half / HALF)
    pos = (jnp.arange(rows) - ROW0).astype(f32)
    ang = pos[:, None] * inv[None, :]
    cos, sin = jnp.cos(ang), jnp.sin(ang)
    cos128 = jnp.concatenate([cos, cos, cos, cos], axis=1)
    sin128 = jnp.concatenate([-sin, sin, -sin, sin], axis=1)
    return cos128, sin128


def _rope128(x, cos128, sin128):
    lane = lax.broadcasted_iota(jnp.int32, x.shape, 1)
    swapped = jnp.where(lane % HEAD_DIM < HALF, pltpu.roll(x, 128 - HALF, 1), pltpu.roll(x, HALF, 1))
    return x * cos128 + swapped * sin128


def _qkv_prep(z, cos128, sin128):
    rows = z.shape[0]

    def body(q_ref, kv_ref, c_ref, s_ref, qo_ref, ko_ref, vo_ref):
        c, s = c_ref[...], s_ref[...]
        for g in range(D // 128):
            qo_ref[:, g * 128:(g + 1) * 128] = (_rope128(q_ref[:, g * 128:(g + 1) * 128], c, s)
                                                * (HEAD_DIM ** -0.5)).astype(bf16)
        for g in range(2):
            kr = _rope128(kv_ref[:, g * 128:(g + 1) * 128], c, s)
            for j in range(2):
                ko_ref[2 * g + j] = kr[:, j * HEAD_DIM:(j + 1) * HEAD_DIM].astype(bf16)
        for h in range(N_KV):
            vo_ref[h] = kv_ref[:, 256 + h * HEAD_DIM:256 + (h + 1) * HEAD_DIM].astype(bf16)

    return pl.pallas_call(
        body, grid=(rows // BLK,),
        in_specs=[pl.BlockSpec((BLK, D), lambda i: (i, OFF_Q // D)),
                  pl.BlockSpec((BLK, 512), lambda i: (i, OFF_K // 512)),
                  pl.BlockSpec((BLK, 128), lambda i: (i, 0)),
                  pl.BlockSpec((BLK, 128), lambda i: (i, 0))],
        out_specs=[pl.BlockSpec((BLK, D), lambda i: (i, 0)),
                   pl.BlockSpec((N_KV, BLK, HEAD_DIM), lambda i: (0, i, 0)),
                   pl.BlockSpec((N_KV, BLK, HEAD_DIM), lambda i: (0, i, 0))],
        out_shape=[SDS((rows, D), bf16), SDS((N_KV, rows, HEAD_DIM), bf16), SDS((N_KV, rows, HEAD_DIM), bf16)],
        name="qkv_prep", compiler_params=_cp(("arbitrary",)),
    )(z, z, cos128, sin128)


def _attn_mask(n):
    qi = n * BLK + lax.broadcasted_iota(jnp.int32, (BLK, 2 * BLK + N_META), 0)
    c = lax.broadcasted_iota(jnp.int32, (BLK, 2 * BLK + N_META), 1)
    jb = (n - 1) * BLK + c
    band = (jb >= BLK) & (jb <= qi) & (qi - jb < BLK)
    meta = (ROW0 + c - 2 * BLK) <= qi
    return ((c < 2 * BLK) & band) | ((c >= 2 * BLK) & meta)


N_KEYS = 2 * BLK + N_META


def _stack_heads(t):
    return jnp.concatenate([t[:, g * HEAD_DIM:(g + 1) * HEAD_DIM] for g in range(GROUP)], axis=0)


def _sink_column(sink_ref, h):
    g = lax.broadcasted_iota(jnp.int32, (GROUP, 1, 1), 0)
    col = jnp.zeros((GROUP, 1, 1), f32)
    for j in range(GROUP):
        col = jnp.where(g == j, sink_ref[h * GROUP + j], col)
    return col


def _kv_specs(last):
    cl = lambda n: jnp.minimum(n, last)
    return [pl.BlockSpec((None, N_META, HEAD_DIM), lambda h, n: (h, ROW0 // N_META, 0)),
            pl.BlockSpec((None, BLK, HEAD_DIM), lambda h, n: (h, jnp.maximum(cl(n) - 1, 0), 0)),
            pl.BlockSpec((None, BLK, HEAD_DIM), lambda h, n: (h, cl(n), 0))]


def _attn_fwd(q_r, k_r, v_b, z, sinks):
    rows = q_r.shape[0]
    nb = rows // BLK

    def body(sink_ref, q_ref, km_ref, kp_ref, kc_ref, vm_ref, vp_ref, vc_ref, ga_ref, o_ref, yb_ref, ybt_ref, lse_ref):
        h, n = pl.program_id(0), pl.program_id(1)
        kk = jnp.concatenate([kp_ref[...], kc_ref[...], km_ref[...]], axis=0)
        vv = jnp.concatenate([vp_ref[...], vc_ref[...], vm_ref[...]], axis=0)
        q2 = _stack_heads(q_ref[...])
        s = jnp.where(_attn_mask(n)[None], _dot_nt(q2, kk).reshape(GROUP, BLK, N_KEYS), NEG_INF)
        sink = _sink_column(sink_ref, h)
        m = jnp.maximum(jnp.max(s, axis=-1, keepdims=True), sink)
        p = jnp.exp(s - m)
        den = jnp.sum(p, axis=-1, keepdims=True) + jnp.exp(sink - m)
        o2 = _dot((p / den).astype(bf16).reshape(GROUP * BLK, N_KEYS), vv)
        lse = m + jnp.log(den)
        for g in range(GROUP):
            o_ref[:, g * HEAD_DIM:(g + 1) * HEAD_DIM] = o2[g * BLK:(g + 1) * BLK]
            lse_ref[:, g:g + 1] = lse[g]
        yb = o_ref[...] * _silu_and_grad(ga_ref[...])[0]
        yb_ref[...] = yb.astype(bf16)
        ybt_ref[...] = yb.T.astype(bf16)

    tile = pl.BlockSpec((BLK, 512), lambda h, n: (n, h))
    return pl.pallas_call(
        body, grid=(N_KV, nb),
        in_specs=[pl.BlockSpec(memory_space=pltpu.SMEM), tile] + _kv_specs(nb - 1) + _kv_specs(nb - 1)
                 + [pl.BlockSpec((BLK, 512), lambda h, n: (n, OFF_GA // 512 + h))],
        out_specs=[tile, tile, pl.BlockSpec((512, BLK), lambda h, n: (h, n)),
                   pl.BlockSpec((None, BLK, GROUP), lambda h, n: (h, n, 0))],
        out_shape=[SDS((rows, D), f32), SDS((rows, D), bf16), SDS((D, rows), bf16),
                   SDS((N_KV, rows, GROUP), f32)],
        name="attn_fwd", compiler_params=_cp(("arbitrary", "arbitrary")),
    )(sinks, q_r, k_r, k_r, k_r, v_b, v_b, v_b, z)


def _attn_bwd(dyb, o32, lse, q_r, k_r, v_b, z, sinks):
    rows = q_r.shape[0]
    nb = rows // BLK
    cl = lambda n: jnp.minimum(n, nb - 1)

    def body(sink_ref, dyb_ref, o_ref, lse_ref, q_ref, km_ref, kp_ref, kc_ref, vm_ref, vp_ref, vc_ref, ga_ref,
             dq_ref, dga_ref, dk_ref, dv_ref, dkm_ref, dvm_ref, dsr_ref, ck_s, cv_s):
        h, n = pl.program_id(0), pl.program_id(1)

        @pl.when(n == 0)
        def _():
            dkm_ref[...] = jnp.zeros_like(dkm_ref)
            dvm_ref[...] = jnp.zeros_like(dvm_ref)
            ck_s[...] = jnp.zeros_like(ck_s)
            cv_s[...] = jnp.zeros_like(cv_s)

        @pl.when(n < nb)
        def _():
            kk = jnp.concatenate([kp_ref[...], kc_ref[...], km_ref[...]], axis=0)
            vv = jnp.concatenate([vp_ref[...], vc_ref[...], vm_ref[...]], axis=0)
            sg, dsg = _silu_and_grad(ga_ref[...])
            dyb_v = dyb_ref[...]
            o_v = o_ref[...]
            dga_ref[...] = (dyb_v * o_v * dsg).astype(bf16)
            q2 = _stack_heads(q_ref[...])
            do2 = _stack_heads(dyb_v * sg)
            lse_v = lse_ref[...]
            lse = jnp.concatenate([lse_v[:, g:g + 1] for g in range(GROUP)], axis=0).reshape(GROUP, BLK, 1)
            delta = jnp.sum(do2 * _stack_heads(o_v), axis=-1, keepdims=True).reshape(GROUP, BLK, 1)
            s = jnp.where(_attn_mask(n)[None], _dot_nt(q2, kk).reshape(GROUP, BLK, N_KEYS), NEG_INF)
            p = jnp.exp(s - lse)
            do2b = do2.astype(bf16)
            ds = (p * (_dot_nt(do2b, vv).reshape(GROUP, BLK, N_KEYS) - delta)).astype(bf16)
            ds = ds.reshape(GROUP * BLK, N_KEYS)
            dsr = -jnp.exp(_sink_column(sink_ref, h) - lse) * delta
            dq2 = _dot(ds, kk)
            for g in range(GROUP):
                dq_ref[:, g * HEAD_DIM:(g + 1) * HEAD_DIM] = dq2[g * BLK:(g + 1) * BLK]
                dsr_ref[:, g:g + 1] = dsr[g]
            dkk = _dot_tn(ds, q2)
            dvv = _dot_tn(p.astype(bf16).reshape(GROUP * BLK, N_KEYS), do2b)
            dk_ref[...] = ck_s[...] + dkk[:BLK]
            dv_ref[...] = cv_s[...] + dvv[:BLK]
            ck_s[...] = dkk[BLK:2 * BLK]
            cv_s[...] = dvv[BLK:2 * BLK]
            dkm_ref[...] += dkk[2 * BLK:]
            dvm_ref[...] += dvv[2 * BLK:]

        @pl.when(n == nb)
        def _():
            dk_ref[...] = ck_s[...]
            dv_ref[...] = cv_s[...]

    tile = pl.BlockSpec((BLK, 512), lambda h, n: (cl(n), h))
    kvout = pl.BlockSpec((None, BLK, HEAD_DIM), lambda h, n: (h, jnp.maximum(n - 1, 0), 0))
    mout = pl.BlockSpec((None, N_META, HEAD_DIM), lambda h, n: (h, 0, 0))
    stat = pl.BlockSpec((None, BLK, GROUP), lambda h, n: (h, cl(n), 0))
    return pl.pallas_call(
        body, grid=(N_KV, nb + 1),
        in_specs=[pl.BlockSpec(memory_space=pltpu.SMEM), tile, tile, stat, tile] + _kv_specs(nb - 1)
                 + _kv_specs(nb - 1) + [pl.BlockSpec((BLK, 512), lambda h, n: (cl(n), OFF_GA // 512 + h))],
        out_specs=[tile, tile, kvout, kvout, mout, mout, stat],
        out_shape=[SDS((rows, D), f32), SDS((rows, D), bf16),
                   SDS((N_KV, rows, HEAD_DIM), f32), SDS((N_KV, rows, HEAD_DIM), f32),
                   SDS((N_KV, N_META, HEAD_DIM), f32), SDS((N_KV, N_META, HEAD_DIM), f32),
                   SDS((N_KV, rows, GROUP), f32)],
        scratch_shapes=[pltpu.VMEM((BLK, HEAD_DIM), f32), pltpu.VMEM((BLK, HEAD_DIM), f32)],
        name="attn_bwd", compiler_params=_cp(("arbitrary", "arbitrary")),
    )(sinks, dyb, o32, lse, q_r, k_r, k_r, k_r, v_b, v_b, v_b, z)


def _qkv_finish(dq, dk, dv, dkm, dvm, cos128, sin128):
    rows = dq.shape[0]

    def body(dq_ref, dk_ref, dv_ref, dkm_ref, dvm_ref, c_ref, s_ref, oq_ref, okv_ref):
        first = (pl.program_id(0) == 0).astype(f32)
        c, s = c_ref[...], -s_ref[...]
        for g in range(D // 128):
            oq_ref[:, g * 128:(g + 1) * 128] = (_rope128(dq_ref[:, g * 128:(g + 1) * 128], c, s)
                                                * (HEAD_DIM ** -0.5)).astype(bf16)
        pad = jnp.zeros((ROW0, HEAD_DIM), f32)
        ks = [dk_ref[h] + first * jnp.concatenate([pad, dkm_ref[h]], axis=0) for h in range(N_KV)]
        vs = [dv_ref[h] + first * jnp.concatenate([pad, dvm_ref[h]], axis=0) for h in range(N_KV)]
        for g in range(2):
            kp = jnp.concatenate([ks[2 * g], ks[2 * g + 1]], axis=1)
            okv_ref[:, g * 128:(g + 1) * 128] = _rope128(kp, c, s).astype(bf16)
            okv_ref[:, 256 + g * 128:256 + (g + 1) * 128] = jnp.concatenate([vs[2 * g], vs[2 * g + 1]], axis=1).astype(bf16)

    kv = pl.BlockSpec((N_KV, BLK, HEAD_DIM), lambda i: (0, i, 0))
    mt = pl.BlockSpec((N_KV, N_META, HEAD_DIM), lambda i: (0, 0, 0))
    return pl.pallas_call(
        body, grid=(rows // BLK,),
        in_specs=[pl.BlockSpec((BLK, D), lambda i: (i, 0)), kv, kv, mt, mt,
                  pl.BlockSpec((BLK, 128), lambda i: (i, 0)), pl.BlockSpec((BLK, 128), lambda i: (i, 0))],
        out_specs=[pl.BlockSpec((BLK, D), lambda i: (i, 0)), pl.BlockSpec((BLK, 512), lambda i: (i, 0))],
        out_shape=[SDS((rows, D), bf16), SDS((rows, 512), bf16)],
        name="qkv_finish", compiler_params=_cp(("arbitrary",)),
    )(dq, dk, dv, dkm, dvm, cos128, sin128)


_TW = 512


def _mix_specs(rows):
    tr = _row_chunk(rows)
    tile = pl.BlockSpec((tr, _TW), lambda i, j: (i, j))
    ga = pl.BlockSpec((tr, _TW), lambda i, j: (i, OFF_G // _TW + j))
    gb = pl.BlockSpec((tr, _TW), lambda i, j: (i, (OFF_G + D) // _TW + j))
    return (rows // tr, D // _TW), tile, ga, gb


def _mix_fwd(y_a, y_b, z):
    rows = y_a.shape[0]
    tw = 256
    col = lambda off: pl.BlockSpec((rows, tw), lambda j: (0, off // tw + j))

    def body(ya_ref, yb_ref, ga_ref, gb_ref, o_ref, ot_ref):
        mixed = (_sigmoid(ga_ref[...]) * ya_ref[...].astype(f32)
                 + _sigmoid(gb_ref[...]) * yb_ref[...].astype(f32))
        o_ref[...] = mixed.astype(bf16)
        ot_ref[...] = mixed.T.astype(bf16)

    return pl.pallas_call(
        body, grid=(D // tw,), in_specs=[col(0), col(0), col(OFF_G), col(OFF_G + D)],
        out_specs=[col(0), pl.BlockSpec((tw, rows), lambda j: (j, 0))],
        out_shape=[SDS((rows, D), bf16), SDS((D, rows), bf16)],
        name="mix_fwd", compiler_params=_cp(("arbitrary",)),
    )(y_a, y_b, z, z)


def _mix_bwd(dmixed, y_a, y_b, z):
    rows = y_a.shape[0]
    grid, _mix_tile, _mix_ga, _mix_gb = _mix_specs(rows)

    def body(dm_ref, ya_ref, yb_ref, ga_ref, gb_ref, dya_ref, dyb_ref, dga_ref, dgb_ref):
        dm = dm_ref[...].astype(f32)
        sa, sb = _sigmoid(ga_ref[...]), _sigmoid(gb_ref[...])
        dya_ref[...] = (dm * sa).astype(bf16)
        dyb_ref[...] = (dm * sb).astype(bf16)
        dga_ref[...] = (dm * ya_ref[...].astype(f32) * sa * (1.0 - sa)).astype(bf16)
        dgb_ref[...] = (dm * yb_ref[...].astype(f32) * sb * (1.0 - sb)).astype(bf16)

    return pl.pallas_call(
        body, grid=grid, in_specs=[_mix_tile, _mix_tile, _mix_tile, _mix_ga, _mix_gb],
        out_specs=[_mix_tile] * 4, out_shape=[SDS((rows, D), bf16)] * 4,
        name="mix_bwd", compiler_params=_cp(("arbitrary", "arbitrary")),
    )(dmixed, y_a, y_b, z, z)


def _final_ln(out32, h32, tgt, ln_g, ln_b):
    rows = out32.shape[0]

    def body(o_ref, h_ref, t_ref, g_ref, b_ref, du_ref, dub_ref, st_ref):
        i = pl.program_id(0)
        g = g_ref[...]
        y, xhat, rstd = _ln_rows(ALPHA * h_ref[...] + o_ref[...], g, b_ref[...])
        e = jnp.where(i > 0, y - t_ref[0], 0.0)
        dy = e * (1.0 / D)
        du = _ln_rows_bwd(dy, g, xhat, rstd)
        du_ref[...] = du
        dub_ref[...] = du.astype(bf16)
        st = jnp.concatenate([_colsum(dy * xhat), _colsum(dy), _colsum(du), _colsum(e * e) * (0.5 / D),
                              jnp.zeros((4, D), f32)], axis=0)

        @pl.when(i == 0)
        def _():
            st_ref[...] = st

        @pl.when(i > 0)
        def _():
            st_ref[...] += st

    row = pl.BlockSpec((BLK, D), lambda i: (i, 0))
    vec = pl.BlockSpec((1, D), lambda i: (0, 0))
    return pl.pallas_call(
        body, grid=(rows // BLK,),
        in_specs=[row, row, pl.BlockSpec((1, BLK, D), lambda i: (0, jnp.maximum(i - 1, 0), 0)), vec, vec],
        out_specs=[row, row, pl.BlockSpec((8, D), lambda i: (0, 0))],
        out_shape=[SDS((rows, D), f32), SDS((rows, D), bf16), SDS((8, D), f32)],
        name="final_ln", compiler_params=_cp(("arbitrary",)),
    )(out32, h32, tgt, ln_g, ln_b)


def _step_rnn(h32, hb, z, wrg, smallw, p, zero):
    rows = z.shape[0]
    cos128, sin128 = _rope_tables(rows)
    cos128 = cos128 + zero
    xc, hr, ya, ya_t = _rnn_fwd(z, smallw, p["conv_b"] + zero, wrg, p["b_ra"], p["b_ri"], p["lru_lambda"])
    q_r, k_r, v_b = _qkv_prep(z, cos128, sin128)
    return dict(cos128=cos128, sin128=sin128, h32=h32, hb=hb, z=z, xc=xc, hr=hr, ya=ya, ya_t=ya_t,
                q_r=q_r, k_r=k_r, v_b=v_b)


def _step_attn(s, p, zero):
    sinks = p["sinks"].reshape(N_KV * GROUP) + zero[0]
    o32, yb, yb_t, lse = _attn_fwd(s["q_r"], s["k_r"], s["v_b"], s["z"], sinks)
    return dict(s, sinks=sinks, o32=o32, yb=yb, yb_t=yb_t, lse=lse)


def _step_merge(s, tgt, w3, p):
    ya, yb, z = s["ya"], s["yb"], s["z"]
    y_a = _mm(ya, w3, sel=0, out_dtype=bf16, name="mm_ya")
    y_b = _mm(yb, w3, sel=1, out_dtype=bf16, name="mm_yb")
    mixed, mixed_t = _mix_fwd(y_a, y_b, z)
    out32 = _mm(mixed, w3, sel=2, bias=p["b_o"], name="mm_out")
    du32, dub, st_out = _final_ln(out32, s["h32"], tgt, p["ln_g"], p["ln_b"])

    g_wo = _mm(mixed_t, dub, out_dtype=bf16, name="mm_dwo")
    dmixed = _mm(dub, w3, sel=2, nt=True, out_dtype=bf16, name="mm_dmixed")
    dya_b, dyb_b, dma, dmb = _mix_bwd(dmixed, y_a, y_b, z)
    g_wrnn = _mm(s["ya_t"], dya_b, out_dtype=bf16, name="mm_dwrnn")
    g_wattn = _mm(s["yb_t"], dyb_b, out_dtype=bf16, name="mm_dwattn")
    dya = _mm(dya_b, w3, sel=0, nt=True, name="mm_dya")
    dyb = _mm(dyb_b, w3, sel=1, nt=True, name="mm_dyb")
    return dict(du32=du32, st_out=st_out, dma=dma, dmb=dmb, dya=dya, dyb=dyb, g_wo=g_wo, g_wrnn=g_wrnn,
                g_wattn=g_wattn)


def _step_backward(s, t, wrg, smallw, p, conv_b):
    z = s["z"]
    dxr, dgr, g_wrg, vec_rnn = _rnn_bwd(t["dya"], s["hr"], s["xc"], z, smallw, conv_b, wrg, p["b_ra"], p["b_ri"],
                                        p["lru_lambda"])
    dq_r, dga, dk, dv, dkm, dvm, dsr = _attn_bwd(t["dyb"], s["o32"], s["lse"], s["q_r"], s["k_r"], s["v_b"], z,
                                                 s["sinks"])
    dq, dkv = _qkv_finish(dq_r, dk, dv, dkm, dvm, s["cos128"], s["sin128"])
    dz_parts = [(dxr, D), (dgr, D), (dq, D), (dkv, 512), (dga, D), (t["dma"], D), (t["dmb"], D)]
    return dict(vec_rnn=vec_rnn, dsr=dsr, g_wrg=g_wrg, dz_parts=dz_parts)


def _step_input_grad(dh_lo, dh_hi, du32, x, smallw, p, after):
    grad_x, dmeta, st_emb = _ln_emb_bwd(dh_lo, dh_hi, du32, x, smallw, p["ln_emb_g"], after)
    return dict(grad_x=grad_x, dmeta=dmeta, st_emb=st_emb)


_ANY = pl.BlockSpec(memory_space=pl.ANY)
_VMEM = pl.BlockSpec(memory_space=pltpu.VMEM)
_HBM = pl.BlockSpec(memory_space=pltpu.HBM)
_SEM = pl.BlockSpec(memory_space=pltpu.SEMAPHORE)


def _place():
    x, y, c = lax.axis_index("x"), lax.axis_index("y"), lax.axis_index("c")
    return x, y, c


def _dev(px, py, pc):
    return 4 * px + 2 * py + pc


def _tile_rows(r):
    return max(t for t in range(16, 321, 16) if r % t == 0) if r > 320 else r


def _cast_w_in(w_in_t, me_idx):
    tm = _tile_rows(SHARD_IN)

    def body(me_ref, i_ref, o_ref):
        o_ref[...] = i_ref[...].astype(bf16)

    return pl.pallas_call(
        body,
        grid_spec=pltpu.PrefetchScalarGridSpec(
            num_scalar_prefetch=1, grid=(SHARD_IN // tm,),
            in_specs=[pl.BlockSpec((tm, D), lambda i, me_ref: (i, 0))],
            out_specs=pl.BlockSpec((None, tm, D), lambda i, me_ref: (me_ref[0], i, 0))),
        out_shape=SDS((N_DEV, SHARD_IN, D), bf16), name="cast_w_in", compiler_params=_cp(("arbitrary",)),
    )(me_idx, w_in_t)


def _cast_small(me_idx, w_rnn_out, w_attn_out, w_o, w_ra, w_ri, meta, conv_w):
    def body(me_ref, a_ref, b_ref, c_ref, ra_ref, ri_ref, m_ref, cw_ref, w3_ref, wrg_ref, sw_ref):
        w3_ref[0] = a_ref[0].astype(bf16)
        w3_ref[1] = b_ref[0].astype(bf16)
        w3_ref[2] = c_ref[0].astype(bf16)
        wrg_ref[0] = ra_ref[0].astype(bf16)
        wrg_ref[1] = ri_ref[0].astype(bf16)
        sw_ref[...] = jnp.concatenate([m_ref[...], cw_ref[0], jnp.zeros((4, 256), f32)], axis=0)

    args = (w_rnn_out, w_attn_out, w_o, w_ra, w_ri, meta, conv_w)
    whole = lambda shape: pl.BlockSpec(shape, lambda i, me_ref: (0,) * len(shape))
    slot = lambda shape: pl.BlockSpec((None, *shape), lambda i, me_ref: (me_ref[0], *([0] * len(shape))))
    shapes = [(3, 256, D), (2, N_RNN_BLOCKS, 32, RNN_BLOCK), (24, 256)]
    return pl.pallas_call(
        body,
        grid_spec=pltpu.PrefetchScalarGridSpec(
            num_scalar_prefetch=1, grid=(1,), in_specs=[whole(a.shape) for a in args],
            out_specs=[slot(sh) for sh in shapes]),
        out_shape=[SDS((N_DEV, *sh), dt) for sh, dt in zip(shapes, (bf16, bf16, f32))],
        name="cast_small", compiler_params=_cp(("arbitrary",)),
    )(me_idx, *args)


def _remote(src, dst, send_sems, recv_sems, k, to):
    return pltpu.make_async_remote_copy(src_ref=src, dst_ref=dst, send_sem=send_sems.at[k], recv_sem=recv_sems.at[k],
                                        device_id=to, device_id_type=MESH)


def _w_in_rows(core, early):
    if early:
        return (1 - core) * W_IN_LATE, W_IN_EARLY
    return core * W_IN_EARLY, W_IN_LATE


def _all_gather(bufs, chunks):
    n = len(bufs)
    base = [0]
    for ch in chunks:
        base.append(base[-1] + 7 * ch)

    def body(*refs):
        outs = refs[n:2 * n]
        send_sems, recv_sems = refs[2 * n:]
        x, y, c = _place()
        me, sibling = (x, y, c), (x, y, 1 - c)
        chips = [(1 - x, y), (x, 1 - y), (1 - x, 1 - y)]

        def copy(a, i, k, block, to):
            blk = outs[a].at[_dev(*block)]
            if a == 0:
                r0, r = _w_in_rows(block[2], True)
                r = r // chunks[a]
                blk = blk.at[pl.ds(pl.multiple_of(r0 + i * r, 32), r)]
            return _remote(blk, blk, send_sems, recv_sems, base[a] + 7 * i + k, to)

        pieces = [(a, i) for a in range(n) for i in range(chunks[a])]
        first = []
        for a, i in pieces:
            first.append(copy(a, i, 0, me, sibling))
            first += [copy(a, i, 1 + j, me, (*chip, c)) for j, chip in enumerate(chips)]
        for cp in first:
            cp.start()
        passed = []
        for a, i in pieces:
            for j, chip in enumerate(chips):
                copy(a, i, 1 + j, (*chip, c), me).wait_recv()
                cp = copy(a, i, 4 + j, (*chip, c), sibling)
                cp.start()
                passed.append(cp)
        for a, i in pieces:
            copy(a, i, 0, sibling, me).wait_recv()
            for j, chip in enumerate(chips):
                copy(a, i, 4 + j, (*chip, 1 - c), me).wait_recv()
        for cp in first + passed:
            cp.wait_send()

    return pl.pallas_call(
        body, in_specs=[_ANY] * n, out_specs=[_ANY] * n,
        out_shape=[SDS(b.shape, b.dtype) for b in bufs],
        input_output_aliases={a: a for a in range(n)},
        scratch_shapes=[pltpu.SemaphoreType.DMA((base[-1],)), pltpu.SemaphoreType.DMA((base[-1],))],
        name="all_gather_weights",
    )(*bufs)


def _late_rows(buf, block):
    r0, r = _w_in_rows(block[2], False)
    return buf.at[_dev(*block)].at[pl.ds(pl.multiple_of(r0, 64), r)]


def _whole_block(buf, block):
    return buf.at[_dev(*block)]


def _copies_own(part):
    def make(srcs, lands, send_sems, recv_sems):
        x, y, c = _place()
        peers = [(x, y, 1 - c), (1 - x, y, c), (x, 1 - y, c), (1 - x, 1 - y, c)]
        out = []
        for a in range(len(srcs)):
            blk = part(srcs[a], (x, y, c))
            out += [_remote(blk, blk, send_sems, recv_sems, 4 * a + k, to) for k, to in enumerate(peers)]
        return out
    return make


def _copies_pass(part):
    def make(srcs, lands, send_sems, recv_sems):
        x, y, c = _place()
        out = []
        for a in range(len(srcs)):
            for j, chip in enumerate([(1 - x, y), (x, 1 - y), (1 - x, 1 - y)]):
                blk = part(srcs[a], (*chip, c))
                out.append(_remote(blk, blk, send_sems, recv_sems, 3 * a + j, (x, y, 1 - c)))
        return out
    return make


_PEER_FLIPS = [(f // 4, (f // 2) % 2, f % 2) for f in range(1, N_DEV)]


def _copies_direct(same_src):
    def make(srcs, lands, send_sems, recv_sems):
        x, y, c = _place()
        me = _dev(x, y, c)
        out = []
        for a in range(len(srcs)):
            for k, (fx, fy, fc) in enumerate(_PEER_FLIPS):
                peer = ((x + fx) % 2, (y + fy) % 2, (c + fc) % 2)
                src = srcs[a] if same_src else srcs[a].at[_dev(*peer)]
                out.append(_remote(src, lands[a].at[me], send_sems, recv_sems, 7 * a + k, peer))
        return out
    return make


def _copies_siblings(srcs, lands, send_sems, recv_sems):
    x, y, c = _place()
    return [_remote(srcs[a].at[2 * q + (1 - c)], lands[a].at[q], send_sems, recv_sems, 4 * a + q, (x, y, 1 - c))
            for a in range(len(srcs)) for q in range(4)]


def _copies_chips(srcs, lands, send_sems, recv_sems):
    x, y, c = _place()
    chips = [(1 - x, y), (x, 1 - y), (1 - x, 1 - y)]
    return [_remote(srcs[a].at[2 * qx + qy], lands[a].at[j], send_sems, recv_sems, 3 * a + j, (qx, qy, c))
            for a in range(len(srcs)) for j, (qx, qy) in enumerate(chips)]


def _split_start(make, per_array, srcs, lands, dep, name):
    n, tot = len(srcs), len(srcs) + len(lands)

    def body(*refs):
        send_sems, recv_sems, token = refs[tot + 1], refs[tot + 2], refs[-1]
        for cp in make(refs[:n], refs[n:tot], send_sems, recv_sems):
            cp.start()
        token[...] = jnp.zeros_like(token)

    hbm = lambda t: pltpu.with_memory_space_constraint(t, pltpu.HBM)
    res = pl.pallas_call(
        body, name=name,
        out_shape=(pltpu.SemaphoreType.DMA((per_array * n,)), pltpu.SemaphoreType.DMA((per_array * n,)),
                   *[pltpu.HBM(t.shape, t.dtype) for t in (*srcs, *lands)], SDS((8, 128), f32)),
        in_specs=[_HBM] * tot + [_ANY], out_specs=(_SEM, _SEM, *([_HBM] * tot), _VMEM),
        input_output_aliases={i: 2 + i for i in range(tot)},
        compiler_params=pltpu.CompilerParams(has_side_effects=pltpu.SideEffectType.DATAFLOW_SIDE_EFFECTING),
    )(*[hbm(t) for t in (*srcs, *lands)], dep)
    return res[0], res[1], list(res[2:2 + n]), list(res[2 + n:2 + tot]), res[-1]


def _split_wait(make, send_sems, recv_sems, srcs, lands, after, name):
    n, tot = len(srcs), len(srcs) + len(lands)

    def body(*refs):
        for cp in make(refs[:n], refs[n:tot], refs[tot], refs[tot + 1]):
            cp.wait_send()
            cp.wait_recv()

    res = pl.pallas_call(
        body, name=name,
        out_shape=tuple(pltpu.HBM(t.shape, t.dtype) for t in (*srcs, *lands)),
        in_specs=[_HBM] * tot + [_SEM, _SEM, _ANY], out_specs=tuple([_HBM] * tot),
        input_output_aliases={i: i for i in range(tot)},
        compiler_params=pltpu.CompilerParams(has_side_effects=pltpu.SideEffectType.DATAFLOW_SIDE_EFFECTING),
    )(*srcs, *lands, send_sems, recv_sems, after)
    return list(res[:n]), list(res[n:])


def _adamw_direct(g, land, me_idx, w, m, v, name):
    r, wd = w.shape
    tr = min(r, 256)

    def body(me_ref, *refs):
        g_ref, peers = refs[0], refs[1:N_DEV]
        w_ref, m_ref, v_ref, g_out, d_out, m_out, v_out = refs[N_DEV:]
        gs = g_ref[...].astype(f32)
        for p_ref in peers:
            gs = gs + p_ref[...].astype(f32)
        d, mn, vn = _adamw(w_ref[...], gs, m_ref[...], v_ref[...])
        g_out[...] = gs
        d_out[...] = d
        m_out[...] = mn
        v_out[...] = vn

    tile = pl.BlockSpec((tr, wd), lambda i, me_ref: (i, 0))
    slot = lambda k: pl.BlockSpec((None, tr, wd), lambda i, me_ref: ((me_ref[0] + k) % N_DEV, i, 0))
    return pl.pallas_call(
        body,
        grid_spec=pltpu.PrefetchScalarGridSpec(
            num_scalar_prefetch=1, grid=(r // tr,),
            in_specs=[slot(0)] + [slot(k) for k in range(1, N_DEV)] + [tile, tile, tile],
            out_specs=[tile] * 4),
        out_shape=[SDS((r, wd), f32)] * 4, name=name, compiler_params=_cp(("arbitrary",), 48),
    )(me_idx, g, *([land] * (N_DEV - 1)), w, m, v)


def _pair_sum(g, r1, c_idx, name):
    _, r, w = g.shape
    tr = _tile_rows(r)

    def body(c_ref, g_ref, r_ref, o_ref):
        o_ref[...] = (g_ref[...].astype(f32) + r_ref[...].astype(f32)).astype(bf16)

    return pl.pallas_call(
        body,
        grid_spec=pltpu.PrefetchScalarGridSpec(
            num_scalar_prefetch=1, grid=(4, r // tr),
            in_specs=[pl.BlockSpec((None, tr, w), lambda q, i, c_ref: (2 * q + c_ref[0], i, 0)),
                      pl.BlockSpec((None, tr, w), lambda q, i, c_ref: (q, i, 0))],
            out_specs=pl.BlockSpec((None, tr, w), lambda q, i, c_ref: (q, i, 0))),
        out_shape=SDS((4, r, w), bf16), name=name, compiler_params=_cp(("arbitrary", "arbitrary")),
    )(c_idx, g, r1)


def _adamw(w, g, m, v):
    m = ADAM_B1 * m + (1.0 - ADAM_B1) * g
    v = ADAM_B2 * v + (1.0 - ADAM_B2) * (g * g)
    m_hat = m / (1.0 - ADAM_B1 ** ADAM_STEP)
    v_hat = v / (1.0 - ADAM_B2 ** ADAM_STEP)
    delta = -ADAM_LR * (m_hat / (jnp.sqrt(v_hat) + ADAM_EPS) + ADAM_WD * w)
    return delta, m, v


def _adamw_big(part, r2, q_idx, w, m, v, name, row_off=0, cols=(0, 1), prev=None):
    r, wd = w.shape
    tr = _tile_rows(r)
    k, ncol = cols
    wp = wd // ncol

    def body(q_ref, p_ref, r_ref, w_ref, m_ref, v_ref, *rest):
        g_out, d_out, m_out, v_out = rest[-4:]
        g = p_ref[...].astype(f32)
        for j in range(3):
            g = g + r_ref[j].astype(f32)
        d, mn, vn = _adamw(w_ref[...], g, m_ref[...], v_ref[...])
        g_out[...] = g
        d_out[...] = d
        m_out[...] = mn
        v_out[...] = vn

    tile = pl.BlockSpec((tr, wp), lambda i, q_ref: (i, k))
    prev = list(prev) if prev is not None else []
    return pl.pallas_call(
        body,
        grid_spec=pltpu.PrefetchScalarGridSpec(
            num_scalar_prefetch=1, grid=(r // tr,),
            in_specs=[pl.BlockSpec((None, tr, wp), lambda i, q_ref: (q_ref[0], row_off + i, 0)),
                      pl.BlockSpec((3, tr, wp), lambda i, q_ref: (0, row_off + i, 0)), tile, tile, tile]
                     + [pl.BlockSpec(memory_space=pl.ANY)] * len(prev),
            out_specs=[tile] * 4),
        out_shape=[SDS((r, wd), f32)] * 4, name=name,
        input_output_aliases={6 + i: i for i in range(len(prev))},
        compiler_params=_cp(("arbitrary",), 48),
    )(q_idx, part, r2, w, m, v, *prev)


_SMALL_ROWS = 24


def _pack_early(vec_rnn, st_out, dsr, db_in):
    def body(vr_ref, so_ref, dsr_ref, db_ref, sm_ref, sm2_ref):
        sm_ref[...] = jnp.zeros_like(sm_ref)
        sm2_ref[...] = jnp.zeros_like(sm2_ref)
        sm_ref[2:3, :] = vr_ref[3:4, :]
        sm_ref[3:6, :] = vr_ref[0:3, :]
        sm_ref[6:7, :] = so_ref[2:3, :]
        sm_ref[7:9, :] = so_ref[0:2, :]
        sm_ref[10:11, :] = so_ref[3:4, :]
        for h in range(N_KV):
            sm_ref[9:10, h * GROUP:(h + 1) * GROUP] = _colsum(dsr_ref[h])
        for j in range(6):
            sm_ref[16 + j:17 + j, :] = db_ref[0:1, j * D:(j + 1) * D]
        sm_ref[22:23, 0:D_IN - 6 * D] = db_ref[0:1, 6 * D:D_IN]
        for s in range(N_DEV):
            sm2_ref[s, 0:CONV_WIDTH, :] = vr_ref[4:8, s * 256:(s + 1) * 256]

    return pl.pallas_call(
        body, out_shape=[SDS((_SMALL_ROWS, D), f32), SDS((N_DEV, 8, 256), f32)],
        name="pack_early", compiler_params=_cp(None),
    )(vec_rnn, st_out, dsr, db_in)


def _pack_late(st_emb, dmeta):
    def body(se_ref, dm_ref, sm_ref, sm2_ref):
        sm_ref[...] = se_ref[...]
        for s in range(N_DEV):
            sm2_ref[s] = dm_ref[:, s * 256:(s + 1) * 256]

    return pl.pallas_call(
        body, out_shape=[SDS((8, D), f32), SDS((N_DEV, N_META, 256), f32)],
        name="pack_late", compiler_params=_cp(None),
    )(st_emb, dmeta)


_SMALL_ROW_OF = {"ln_emb_g": 0, "ln_emb_b": 1, "conv_b": 2, "b_ra": 3, "b_ri": 4, "lru_lambda": 5, "b_o": 6,
                 "ln_g": 7, "ln_b": 8}
_SMALL_NAMES = ["ln_emb_g", "ln_emb_b", "conv_b", "b_ra", "b_ri", "lru_lambda", "b_o", "ln_g", "ln_b",
                "sinks", "b_in", "meta_tokens", "conv_w"]


def _small_update(me_idx, early, late, wmv):
    n_fixed = 9

    def in_order(me, own_ref, land_ref):
        acc = None
        for e in range(N_DEV):
            term = jnp.where(me == e, own_ref[...], land_ref[e])
            acc = term if acc is None else acc + term
        return acc

    def body(*refs):
        me_ref, own_ref, land_ref, cown_ref, cland_ref, lown_ref, lland_ref, mown_ref, mland_ref = refs[:n_fixed]
        ins = refs[n_fixed:n_fixed + 3 * len(_SMALL_NAMES)]
        outs = refs[n_fixed + 3 * len(_SMALL_NAMES):]
        me = me_ref[0]
        sm = in_order(me, own_ref, land_ref)
        conv = in_order(me, cown_ref, cland_ref)
        late = in_order(me, lown_ref, lland_ref)
        meta = in_order(me, mown_ref, mland_ref)

        def grad_of(name):
            if name in ("ln_emb_g", "ln_emb_b"):
                r = _SMALL_ROW_OF[name]
                return late[r:r + 1, :]
            if name in _SMALL_ROW_OF:
                r = _SMALL_ROW_OF[name]
                return sm[r:r + 1, :]
            if name == "sinks":
                return sm[9:10, 0:N_KV * GROUP]
            if name == "b_in":
                return jnp.concatenate([sm[16 + j:17 + j, :] for j in range(7)], axis=1)[:, :D_IN]
            if name == "meta_tokens":
                return meta
            return conv[0:CONV_WIDTH, :]

        for i, name in enumerate(_SMALL_NAMES):
            w_ref, m_ref, v_ref = ins[3 * i:3 * i + 3]
            g = grad_of(name)
            d, mn, vn = _adamw(w_ref[...], g, m_ref[...], v_ref[...])
            outs[4 * i][...] = g
            outs[4 * i + 1][...] = d
            outs[4 * i + 2][...] = mn
            outs[4 * i + 3][...] = vn
        outs[-1][...] = jnp.broadcast_to(jnp.sum(sm[10:11, :], axis=1, keepdims=True), (8, 128))

    args, out_shape = [me_idx, *early, *late], []
    for name in _SMALL_NAMES:
        args += list(wmv[name])
        out_shape += [SDS(wmv[name][0].shape, f32)] * 4
    out_shape.append(SDS((8, 128), f32))
    res = pl.pallas_call(
        body, out_shape=out_shape, in_specs=[pl.BlockSpec(memory_space=pltpu.SMEM)] + [_VMEM] * (len(args) - 1),
        name="small_update", compiler_params=_cp(None))(*args)
    return {name: tuple(res[4 * i:4 * i + 4]) for i, name in enumerate(_SMALL_NAMES)}, res[-1][0, 0]


_WEIGHTS = ["meta_tokens", "ln_emb_g", "ln_emb_b", "w_in", "b_in", "conv_w", "conv_b", "w_ra", "b_ra", "w_ri",
            "b_ri", "lru_lambda", "sinks", "w_rnn_out", "w_attn_out", "w_o", "b_o", "ln_g", "ln_b"]
_SMALL_2D = {"meta_tokens": (N_META, 256), "conv_w": (CONV_WIDTH, 256), "b_in": (1, D_IN), "sinks": (1, N_KV * GROUP)}


def kernel(x, meta_tokens, ln_emb_g, ln_emb_b, w_in, b_in, conv_w, conv_b, w_ra, b_ra, w_ri, b_ri, lru_lambda, sinks, w_rnn_out, w_attn_out, w_o, b_o, ln_g, ln_b, loss_target, m_meta_tokens, m_ln_emb_g, m_ln_emb_b, m_w_in, m_b_in, m_conv_w, m_conv_b, m_w_ra, m_b_ra, m_w_ri, m_b_ri, m_lru_lambda, m_sinks, m_w_rnn_out, m_w_attn_out, m_w_o, m_b_o, m_ln_g, m_ln_b, v_meta_tokens, v_ln_emb_g, v_ln_emb_b, v_w_in, v_b_in, v_conv_w, v_conv_b, v_w_ra, v_b_ra, v_w_ri, v_b_ri, v_lru_lambda, v_sinks, v_w_rnn_out, v_w_attn_out, v_w_o, v_b_o, v_ln_g, v_ln_b):
    w = dict(meta_tokens=meta_tokens, ln_emb_g=ln_emb_g, ln_emb_b=ln_emb_b, w_in=w_in, b_in=b_in, conv_w=conv_w,
             conv_b=conv_b, w_ra=w_ra, b_ra=b_ra, w_ri=w_ri, b_ri=b_ri, lru_lambda=lru_lambda, sinks=sinks,
             w_rnn_out=w_rnn_out, w_attn_out=w_attn_out, w_o=w_o, b_o=b_o, ln_g=ln_g, ln_b=ln_b)
    m = dict(meta_tokens=m_meta_tokens, ln_emb_g=m_ln_emb_g, ln_emb_b=m_ln_emb_b, w_in=m_w_in, b_in=m_b_in,
             conv_w=m_conv_w, conv_b=m_conv_b, w_ra=m_w_ra, b_ra=m_b_ra, w_ri=m_w_ri, b_ri=m_b_ri,
             lru_lambda=m_lru_lambda, sinks=m_sinks, w_rnn_out=m_w_rnn_out, w_attn_out=m_w_attn_out, w_o=m_w_o,
             b_o=m_b_o, ln_g=m_ln_g, ln_b=m_ln_b)
    v = dict(meta_tokens=v_meta_tokens, ln_emb_g=v_ln_emb_g, ln_emb_b=v_ln_emb_b, w_in=v_w_in, b_in=v_b_in,
             conv_w=v_conv_w, conv_b=v_conv_b, w_ra=v_w_ra, b_ra=v_b_ra, w_ri=v_w_ri, b_ri=v_b_ri,
             lru_lambda=v_lru_lambda, sinks=v_sinks, w_rnn_out=v_w_rnn_out, w_attn_out=v_w_attn_out, w_o=v_w_o,
             b_o=v_b_o, ln_g=v_ln_g, ln_b=v_ln_b)
    px, py, pc = _place()
    as_idx = lambda t: jnp.reshape(t, (1,)).astype(jnp.int32)
    c_idx, q_idx, me_idx = as_idx(pc), as_idx(2 * px + py), as_idx(_dev(px, py, pc))

    w3_s, wrg_s, small_s = _cast_small(me_idx, w_rnn_out, w_attn_out, w_o, w_ra, w_ri, meta_tokens, conv_w)
    vec = lambda name: w[name].reshape(1, -1)
    p = {k: vec(k) for k in ("ln_emb_g", "ln_emb_b", "b_in", "conv_b", "b_ra", "b_ri", "lru_lambda", "sinks",
                             "b_o", "ln_g", "ln_b")}
    w_in_t = lambda a: jnp.swapaxes(a, 1, 2).reshape(SHARD_IN, D)
    wg, wrg, smallw = _all_gather([_cast_w_in(w_in_t(w_in), me_idx), wrg_s, small_s], [6, 1, 1])
    late = _split_start(_copies_own(_late_rows), 4, [wg], [], smallw, "gather_late_start")
    h32, hb = _ln_emb(x, smallw, p["ln_emb_g"], p["ln_emb_b"] + late[4][0:1, 0:1])
    z = _mm_z(hb, late[2][0].reshape(D_IN, D), p["b_in"], None, "mm_z_early")
    (wg,), _ = _split_wait(_copies_own(_late_rows), late[0], late[1], late[2], [], z, "gather_late_wait")
    passed = _split_start(_copies_pass(_late_rows), 3, [wg], [], smallw, "gather_pass_start")
    w3_own = _split_start(_copies_own(_whole_block), 4, [w3_s], [], passed[4], "gather_w3_start")
    zero = w3_own[4][0:1, 0:1]
    z = _mm_z(hb, passed[2][0].reshape(D_IN, D), p["b_in"] + zero, c_idx, "mm_z_late_own", z)
    (wg,), _ = _split_wait(_copies_pass(_late_rows), passed[0], passed[1], passed[2], [], z, "gather_pass_wait")
    w_full = wg.reshape(D_IN, D)

    z = _mm_z(hb, w_full, p["b_in"], 1 - c_idx, "mm_z_late_other", z)
    s = _step_rnn(h32, hb, z, wrg, smallw, p, zero)
    (w3,), _ = _split_wait(_copies_own(_whole_block), w3_own[0], w3_own[1], w3_own[2], [], s["ya"], "gather_w3_wait")
    w3_pass = _split_start(_copies_pass(_whole_block), 3, [w3], [], smallw, "gather_w3_pass_start")
    s = _step_attn(s, p, w3_pass[4][0:1, 0:1])
    (w3,), _ = _split_wait(_copies_pass(_whole_block), w3_pass[0], w3_pass[1], w3_pass[2], [], s["lse"],
                           "gather_w3_pass_wait")
    t = _step_merge(s, loss_target, w3, p)

    big = {}
    two_d = lambda name: (w[name].shape[-2], w[name].shape[-1])
    proj = ("w_o", "w_rnn_out", "w_attn_out")
    g_proj = [t[k].reshape(N_DEV, 256, D) for k in ("g_wo", "g_wrnn", "g_wattn")]
    g_pending = _split_start(_copies_direct(False), 7, g_proj, [lax.empty((N_DEV, 256, D), bf16) for _ in proj],
                             p["b_o"], "reduce_proj_start")
    u = _step_backward(s, t, wrg, smallw, p, p["conv_b"] + g_pending[4][0:1, 0:1])

    def siblings_start(gs, dep, tag):
        return _split_start(_copies_siblings, 4, gs, [lax.empty((4, *g.shape[1:]), bf16) for g in gs], dep,
                            "reduce_siblings_start_" + tag)

    def chips_start(gs, r1, dep, tag):
        parts = [_pair_sum(g, r, c_idx, "pair_sum_%s%d" % (tag, i)) for i, (g, r) in enumerate(zip(gs, r1))]
        return _split_start(_copies_chips, 3, parts, [lax.empty((3, *q.shape[1:]), bf16) for q in parts], dep,
                            "reduce_chips_start_" + tag)

    g_a, dz, db_in = _mm_dwin_parts(s["hb"], u["dz_parts"])
    shards = lambda g: g.reshape(N_DEV, SHARD_IN, W_IN_HALF)
    sib_a = siblings_start([shards(g_a), u["g_wrg"].reshape(N_DEV, 2 * RNN_BLOCK, RNN_BLOCK)], db_in, "a")
    g_proj, g_land = _split_wait(_copies_direct(False), *g_pending[:4], sib_a[4], "reduce_proj_wait")
    for i, name in enumerate(proj):
        res = _adamw_direct(g_proj[i], g_land[i], me_idx, w[name].reshape(two_d(name)), m[name].reshape(two_d(name)),
                            v[name].reshape(two_d(name)), "adamw_" + name)
        big[name] = tuple(r.reshape(w[name].shape) for r in res)
    chp_a = chips_start(*_split_wait(_copies_siblings, *sib_a[:4], big["w_attn_out"][3], "reduce_siblings_wait_a"),
                        db_in, "a")
    g_b = _mm_dwin(s["hb"], dz, chp_a[4])
    sib_b = siblings_start([shards(g_b)], db_in, "b")
    sm_e = _pack_early(u["vec_rnn"], t["st_out"], u["dsr"], db_in)
    early = _split_start(_copies_direct(True), 7, list(sm_e),
                         [lax.empty((N_DEV, *a.shape), f32) for a in sm_e], sib_b[4], "small_early_start")
    dh_lo = _mm_dh(dz, w_full, early[4], 0)
    chp_b = chips_start(*_split_wait(_copies_siblings, *sib_b[:4], dh_lo, "reduce_siblings_wait_b"), db_in, "b")
    dh_hi = _mm_dh(dz, w_full, chp_b[4], 1)
    parts_a, r2_a = _split_wait(_copies_chips, *chp_a[:4], dh_hi, "reduce_chips_wait_a")
    w_in_res = _adamw_big(parts_a[0], r2_a[0], q_idx, w_in_t(w["w_in"]), w_in_t(m["w_in"]), w_in_t(v["w_in"]),
                          "adamw_w_in_a", cols=(0, 2))
    u.update(_step_input_grad(dh_lo, dh_hi, t["du32"], x, smallw, p, w_in_res[3]))
    sm_l = _pack_late(u["st_emb"], u["dmeta"])
    late_x = _split_start(_copies_direct(True), 7, list(sm_l), [lax.empty((N_DEV, *a.shape), f32) for a in sm_l],
                          p["b_o"], "small_late_start")
    parts_b, r2_b = _split_wait(_copies_chips, *chp_b[:4], late_x[4], "reduce_chips_wait_b")
    res = _adamw_big(parts_b[0], r2_b[0], q_idx, w_in_t(w["w_in"]), w_in_t(m["w_in"]), w_in_t(v["w_in"]),
                     "adamw_w_in_b", cols=(1, 2), prev=w_in_res)
    big["w_in"] = tuple(jnp.swapaxes(r.reshape(1, SHARD_IN, D), 1, 2) for r in res)
    (sm_own, conv_own), (sm_land, conv_land) = _split_wait(_copies_direct(True), *early[:4], res[3], "small_early_wait")
    (l_own, meta_own), (l_land, meta_land) = _split_wait(_copies_direct(True), *late_x[:4], sm_land, "small_late_wait")
    me = _dev(px, py, pc)
    mine = lambda a, axis: lax.dynamic_index_in_dim(a, me, axis, keepdims=False)
    two = lambda name, t: t.reshape(_SMALL_2D.get(name, (1, D)))
    small, loss = _small_update(me_idx, (sm_own, sm_land, mine(conv_own, 0), mine(conv_land, 1)),
                                (l_own, l_land, mine(meta_own, 0), mine(meta_land, 1)),
                                {k: (two(k, w[k]), two(k, m[k]), two(k, v[k])) for k in _SMALL_NAMES})
    for i, name in enumerate(("w_ra", "w_ri")):
        sq = (RNN_BLOCK, RNN_BLOCK)
        res = _adamw_big(parts_a[1], r2_a[1], q_idx, w[name].reshape(sq), m[name].reshape(sq), v[name].reshape(sq),
                         "adamw_" + name, row_off=i)
        big[name] = tuple(r.reshape(w[name].shape) for r in res)
    res = dict(big)
    for k in _SMALL_NAMES:
        res[k] = tuple(t.reshape(w[k].shape) for t in small[k])

    outs = [loss, u["grad_x"]]
    for j in range(4):
        outs += [res[k][j] for k in _WEIGHTS]
    return tuple(outs)
```
